```python
import math
import jax, jax.numpy as jnp
from jax import lax
import numpy as np

D_MODEL = 1024
BATCH = 16
SEQ = 2048
DEPTH = 2

N_META = 16
N_EVEN = (DEPTH + 1) // 2
N_ODD = DEPTH // 2

LRU_WIDTH = D_MODEL // 2
LRU_HEADS = 4
LRU_HEAD_DIM = LRU_WIDTH // LRU_HEADS
CONV_WIDTH = 4
LRU_C = 8.0

MLA_HEADS = 8
MLA_NOPE = 64
MLA_ROPE = 32
MLA_V = 64
MLA_Q_RANK = D_MODEL // 4
MLA_KV_RANK = D_MODEL // 8
ATTN_BLOCK = 128

EVEN_IN = 2 * LRU_WIDTH + MLA_Q_RANK + MLA_KV_RANK + MLA_ROPE
EVEN_MIX = LRU_WIDTH + MLA_HEADS * MLA_V

RET_HEADS = 4
RET_QK_DIM = D_MODEL // RET_HEADS
RET_V_DIM = 2 * RET_QK_DIM
RET_CHUNK = 128
RET_IN = 2 * RET_HEADS * RET_QK_DIM + 2 * RET_HEADS * RET_V_DIM
RET_MIX = RET_HEADS * RET_V_DIM

D_FF = 4 * D_MODEL
ROPE_BASE = 10000.0
DN_ALPHA = (2 * DEPTH) ** 0.25
DN_BETA = (8 * DEPTH) ** -0.25
EPS = 1e-5
NEG_INF = -1e30

kernel_name = 'hybrid_rglru_mla_retention_deepnorm'


def _layernorm(x, g, b):
    xf = x.astype(jnp.float32)
    mu = jnp.mean(xf, axis=-1, keepdims=True)
    xc = xf - mu
    var = jnp.mean(jnp.square(xc), axis=-1, keepdims=True)
    return (xc * lax.rsqrt(var + EPS) * g + b).astype(x.dtype)


def _rmsnorm(x, g):
    xf = x.astype(jnp.float32)
    y = xf * lax.rsqrt(jnp.mean(jnp.square(xf), axis=-1, keepdims=True) + EPS)
    return (y * g).astype(x.dtype)


def _rope(x, pos):
    half = x.shape[-1] // 2
    inv = ROPE_BASE ** (-jnp.arange(half, dtype=jnp.float32) / half)
    ang = pos.astype(jnp.float32)[:, None] * inv[None, :]
    cos = jnp.cos(ang)[None, :, None, :].astype(x.dtype)
    sin = jnp.sin(ang)[None, :, None, :].astype(x.dtype)
    x1, x2 = x[..., :half], x[..., half:]
    return jnp.concatenate([x1 * cos - x2 * sin, x1 * sin + x2 * cos], axis=-1)


def _lru_combine(c1, c2):
    a1, b1 = c1
    a2, b2 = c2
    return a1 * a2, a2 * b1 + b2


def _rglru_group(p_gate, p_rec, conv_w, conv_b, w_rg_a, b_rg_a, w_rg_x, b_rg_x, lru_lambda):
    B, T, _ = p_rec.shape
    xc = lax.conv_general_dilated(
        p_rec, conv_w[:, None, :], window_strides=(1,), padding=[(CONV_WIDTH - 1, 0)],
        dimension_numbers=('NWC', 'WIO', 'NWC'), feature_group_count=LRU_WIDTH) + conv_b
    xh = xc.reshape(B, T, LRU_HEADS, LRU_HEAD_DIM)
    r = jax.nn.sigmoid(jnp.einsum('bthi,hij->bthj', xh, w_rg_a).reshape(B, T, LRU_WIDTH) + b_rg_a)
    i = jax.nn.sigmoid(jnp.einsum('bthi,hij->bthj', xh, w_rg_x).reshape(B, T, LRU_WIDTH) + b_rg_x)
    log_a = (-LRU_C * r * jax.nn.softplus(-lru_lambda)).astype(jnp.float32)
    a = jnp.exp(log_a)
    mult = jnp.sqrt(-jnp.expm1(2.0 * log_a))
    b = mult * (i * xc).astype(jnp.float32)
    _, h = lax.associative_scan(_lru_combine, (a, b), axis=1)
    return h.astype(p_rec.dtype) * jax.nn.gelu(p_gate)


def _attend(qb, qpos, k, v, kpos):
    scale = qb.shape[-1] ** -0.5
    s = jnp.einsum('bqhd,bkhd->bhqk', qb, k).astype(jnp.float32) * scale
    mask = kpos[None, :] <= qpos[:, None]
    s = jnp.where(mask[None, None], s, NEG_INF)
    p = jax.nn.softmax(s, axis=-1).astype(v.dtype)
    return jnp.einsum('bhqk,bkhd->bqhd', p, v)


def _causal_attention(q, k, v, pos):
    B, T, H, d = q.shape
    dv = v.shape[-1]
    out_meta = _attend(q[:, :N_META], pos[:N_META], k[:, :N_META], v[:, :N_META], pos[:N_META])
    nb = (T - N_META) // ATTN_BLOCK
    qr = q[:, N_META:].reshape(B, nb, ATTN_BLOCK, H, d).swapaxes(0, 1)
    pr = pos[N_META:].reshape(nb, ATTN_BLOCK)
    out_r = lax.map(lambda a: _attend(a[0], a[1], k, v, pos), (qr, pr))
    out_r = out_r.swapaxes(0, 1).reshape(B, T - N_META, H, dv)
    return jnp.concatenate([out_meta, out_r], axis=1)


def _mla_group(p_q, p_kv, p_kpe, pos, q_norm_g, w_uq, kv_norm_g, w_ukv):
    B, T, _ = p_q.shape
    q = (_rmsnorm(p_q, q_norm_g) @ w_uq).reshape(B, T, MLA_HEADS, MLA_NOPE + MLA_ROPE)
    q_nope, q_pe = q[..., :MLA_NOPE], _rope(q[..., MLA_NOPE:], pos)
    kv = (_rmsnorm(p_kv, kv_norm_g) @ w_ukv).reshape(B, T, MLA_HEADS, MLA_NOPE + MLA_V)
    k_nope, v = kv[..., :MLA_NOPE], kv[..., MLA_NOPE:]
    k_pe = _rope(p_kpe[:, :, None, :], pos)
    q = jnp.concatenate([q_nope, q_pe], axis=-1)
    k = jnp.concatenate([k_nope, jnp.broadcast_to(k_pe, (B, T, MLA_HEADS, MLA_ROPE))], axis=-1)
    o = _causal_attention(q, k, v, pos)
    return o.reshape(B, T, MLA_HEADS * MLA_V)


def _even_mixer(x, pos, w_in, conv_w, conv_b, w_rg_a, b_rg_a, w_rg_x, b_rg_x, lru_lambda,
                q_norm_g, w_uq, kv_norm_g, w_ukv, w_out):
    p = x @ w_in
    cuts = [LRU_WIDTH, 2 * LRU_WIDTH, 2 * LRU_WIDTH + MLA_Q_RANK,
            2 * LRU_WIDTH + MLA_Q_RANK + MLA_KV_RANK]
    p_gate, p_rec, p_q, p_kv, p_kpe = jnp.split(p, cuts, axis=-1)
    y_rec = _rglru_group(p_gate, p_rec, conv_w, conv_b, w_rg_a, b_rg_a, w_rg_x, b_rg_x, lru_lambda)
    y_att = _mla_group(p_q, p_kv, p_kpe, pos, q_norm_g, w_uq, kv_norm_g, w_ukv)
    return jnp.concatenate([y_rec, y_att], axis=-1) @ w_out


def _retention_chunk(q, k, v, s_prev, log_gamma):
    dt = q.dtype
    c = q.shape[2]
    idx = jnp.arange(c, dtype=jnp.float32)
    diff = idx[:, None] - idx[None, :]
    decay = jnp.where(diff >= 0, jnp.exp(log_gamma[:, None, None] * jnp.maximum(diff, 0.0)), 0.0).astype(dt)
    q_decay = jnp.exp(log_gamma[:, None] * (idx + 1.0))[None, :, :, None].astype(dt)
    k_decay = jnp.exp(log_gamma[:, None] * (c - 1.0 - idx))[None, :, :, None].astype(dt)
    chunk_decay = jnp.exp(log_gamma * c)[None, :, None, None].astype(dt)
    scores = jnp.einsum('bhid,bhjd->bhij', q, k) * decay
    o = jnp.einsum('bhij,bhjv->bhiv', scores, v) + q_decay * jnp.einsum('bhid,bhdv->bhiv', q, s_prev)
    s_new = chunk_decay * s_prev + jnp.einsum('bhjd,bhjv->bhdv', k * k_decay, v)
    return o, s_new


def _odd_mixer(x, pos, w_in, w_out):
    B, T, _ = x.shape
    qk = RET_HEADS * RET_QK_DIM
    p = x @ w_in
    q, k, v, g = jnp.split(p, [qk, 2 * qk, 2 * qk + RET_MIX], axis=-1)
    q = _rope(q.reshape(B, T, RET_HEADS, RET_QK_DIM), pos)
    k = _rope(k.reshape(B, T, RET_HEADS, RET_QK_DIM), pos) * (RET_QK_DIM ** -0.5)
    v = v.reshape(B, T, RET_HEADS, RET_V_DIM)
    q, k, v = (t.transpose(0, 2, 1, 3) for t in (q, k, v))
    log_gamma = jnp.log(1.0 - 2.0 ** (-5.0 - jnp.arange(RET_HEADS, dtype=jnp.float32)))
    s0 = jnp.zeros((B, RET_HEADS, RET_QK_DIM, RET_V_DIM), dtype=q.dtype)
    o_meta, s = _retention_chunk(q[:, :, :N_META], k[:, :, :N_META], v[:, :, :N_META], s0, log_gamma)
    nc = (T - N_META) // RET_CHUNK

    def to_chunks(t):
        return t[:, :, N_META:].reshape(B, RET_HEADS, nc, RET_CHUNK, t.shape[-1]).transpose(2, 0, 1, 3, 4)

    def body(state, qkv):
        qc, kc, vc = qkv
        o, state = _retention_chunk(qc, kc, vc, state, log_gamma)
        return state, o

    _, o_r = lax.scan(body, s, (to_chunks(q), to_chunks(k), to_chunks(v)))
    o_r = o_r.transpose(1, 2, 0, 3, 4).reshape(B, RET_HEADS, T - N_META, RET_V_DIM)
    o = jnp.concatenate([o_meta, o_r], axis=2)
    of = o.astype(jnp.float32)
    o = (of * lax.rsqrt(jnp.mean(jnp.square(of), axis=-1, keepdims=True) + EPS)).astype(x.dtype)
    y = o.transpose(0, 2, 1, 3).reshape(B, T, RET_MIX)
    return (jax.nn.silu(g) * y) @ w_out


def _fwd_setup_inputs(seed: int = 0) -> dict:
    key = jax.random.key(seed)
    ks = iter(jax.random.split(key, 40))
    f32 = jnp.float32

    def nrm(shape, scale):
        return jax.random.normal(next(ks), shape, f32) * scale

    u = jax.random.uniform(next(ks), (N_EVEN, LRU_WIDTH), f32, minval=0.9, maxval=0.999)
    a_base = u ** (1.0 / LRU_C)
    lru_lambda = jnp.log(a_base) - jnp.log1p(-a_base)
    return {
        'x': nrm((BATCH, SEQ, D_MODEL), 1.0),
        'meta_tokens': nrm((N_META, D_MODEL), 1.0),
        'ev_w_in': nrm((N_EVEN, D_MODEL, EVEN_IN), D_MODEL ** -0.5),
        'ev_conv_w': nrm((N_EVEN, CONV_WIDTH, LRU_WIDTH), CONV_WIDTH ** -0.5),
        'ev_conv_b': nrm((N_EVEN, LRU_WIDTH), 0.02),
        'ev_w_rg_a': nrm((N_EVEN, LRU_HEADS, LRU_HEAD_DIM, LRU_HEAD_DIM), LRU_HEAD_DIM ** -0.5),
        'ev_b_rg_a': nrm((N_EVEN, LRU_WIDTH), 0.02),
        'ev_w_rg_x': nrm((N_EVEN, LRU_HEADS, LRU_HEAD_DIM, LRU_HEAD_DIM), LRU_HEAD_DIM ** -0.5),
        'ev_b_rg_x': nrm((N_EVEN, LRU_WIDTH), 0.02),
        'ev_lru_lambda': lru_lambda,
        'ev_q_norm_g': 1.0 + nrm((N_EVEN, MLA_Q_RANK), 0.02),
        'ev_w_uq': nrm((N_EVEN, MLA_Q_RANK, MLA_HEADS * (MLA_NOPE + MLA_ROPE)), MLA_Q_RANK ** -0.5),
        'ev_kv_norm_g': 1.0 + nrm((N_EVEN, MLA_KV_RANK), 0.02),
        'ev_w_ukv': nrm((N_EVEN, MLA_KV_RANK, MLA_HEADS * (MLA_NOPE + MLA_V)), MLA_KV_RANK ** -0.5),
        'ev_w_out': nrm((N_EVEN, EVEN_MIX, D_MODEL), DN_BETA * EVEN_MIX ** -0.5),
        'od_w_in': nrm((N_ODD, D_MODEL, RET_IN), D_MODEL ** -0.5),
        'od_w_out': nrm((N_ODD, RET_MIX, D_MODEL), DN_BETA * RET_MIX ** -0.5),
        'ln_mix_g': 1.0 + nrm((DEPTH, D_MODEL), 0.02),
        'ln_mix_b': nrm((DEPTH, D_MODEL), 0.02),
        'mlp_w1': nrm((DEPTH, D_MODEL, D_FF), D_MODEL ** -0.5),
        'mlp_w2': nrm((DEPTH, D_FF, D_MODEL), DN_BETA * D_FF ** -0.5),
        'ln_mlp_g': 1.0 + nrm((DEPTH, D_MODEL), 0.02),
        'ln_mlp_b': nrm((DEPTH, D_MODEL), 0.02),
    }


def _fwd_reference(x, meta_tokens, ev_w_in, ev_conv_w, ev_conv_b, ev_w_rg_a, ev_b_rg_a, ev_w_rg_x,
              ev_b_rg_x, ev_lru_lambda, ev_q_norm_g, ev_w_uq, ev_kv_norm_g, ev_w_ukv, ev_w_out,
              od_w_in, od_w_out, ln_mix_g, ln_mix_b, mlp_w1, mlp_w2, ln_mlp_g, ln_mlp_b):
    B = x.shape[0]
    meta = jnp.broadcast_to(meta_tokens[None].astype(x.dtype), (B, N_META, D_MODEL))
    h = jnp.concatenate([meta, x], axis=1)
    pos = jnp.arange(h.shape[1], dtype=jnp.int32)
    for l in range(DEPTH):
        if l % 2 == 0:
            e = l // 2
            mix = _even_mixer(h, pos, ev_w_in[e], ev_conv_w[e], ev_conv_b[e], ev_w_rg_a[e], ev_b_rg_a[e],
                              ev_w_rg_x[e], ev_b_rg_x[e], ev_lru_lambda[e], ev_q_norm_g[e], ev_w_uq[e],
                              ev_kv_norm_g[e], ev_w_ukv[e], ev_w_out[e])
        else:
            o = l // 2
            mix = _odd_mixer(h, pos, od_w_in[o], od_w_out[o])
        h = _layernorm(DN_ALPHA * h + mix, ln_mix_g[l], ln_mix_b[l])
        f = jnp.square(jax.nn.relu(h @ mlp_w1[l])) @ mlp_w2[l]
        h = _layernorm(DN_ALPHA * h + f, ln_mlp_g[l], ln_mlp_b[l])
    return h[:, N_META:]


import jax as _jax
import jax.numpy as _jnp

TWIN_FORMAT = 'train_step'
FWD_PARAMS = ['x', 'meta_tokens', 'ev_w_in', 'ev_conv_w', 'ev_conv_b', 'ev_w_rg_a', 'ev_b_rg_a', 'ev_w_rg_x', 'ev_b_rg_x', 'ev_lru_lambda', 'ev_q_norm_g', 'ev_w_uq', 'ev_kv_norm_g', 'ev_w_ukv', 'ev_w_out', 'od_w_in', 'od_w_out', 'ln_mix_g', 'ln_mix_b', 'mlp_w1', 'mlp_w2', 'ln_mlp_g', 'ln_mlp_b']
TWIN_WEIGHTS = ['meta_tokens', 'ev_w_in', 'ev_conv_w', 'ev_conv_b', 'ev_w_rg_a', 'ev_b_rg_a', 'ev_w_rg_x', 'ev_b_rg_x', 'ev_lru_lambda', 'ev_q_norm_g', 'ev_w_uq', 'ev_kv_norm_g', 'ev_w_ukv', 'ev_w_out', 'od_w_in', 'od_w_out', 'ln_mix_g', 'ln_mix_b', 'mlp_w1', 'mlp_w2', 'ln_mlp_g', 'ln_mlp_b']
TWIN_DIFF_INPUT = 'x'
TWIN_INPUTS = ['x', 'meta_tokens', 'ev_w_in', 'ev_conv_w', 'ev_conv_b', 'ev_w_rg_a', 'ev_b_rg_a', 'ev_w_rg_x', 'ev_b_rg_x', 'ev_lru_lambda', 'ev_q_norm_g', 'ev_w_uq', 'ev_kv_norm_g', 'ev_w_ukv', 'ev_w_out', 'od_w_in', 'od_w_out', 'ln_mix_g', 'ln_mix_b', 'mlp_w1', 'mlp_w2', 'ln_mlp_g', 'ln_mlp_b', 'loss_target', 'm_meta_tokens', 'm_ev_w_in', 'm_ev_conv_w', 'm_ev_conv_b', 'm_ev_w_rg_a', 'm_ev_b_rg_a', 'm_ev_w_rg_x', 'm_ev_b_rg_x', 'm_ev_lru_lambda', 'm_ev_q_norm_g', 'm_ev_w_uq', 'm_ev_kv_norm_g', 'm_ev_w_ukv', 'm_ev_w_out', 'm_od_w_in', 'm_od_w_out', 'm_ln_mix_g', 'm_ln_mix_b', 'm_mlp_w1', 'm_mlp_w2', 'm_ln_mlp_g', 'm_ln_mlp_b', 'v_meta_tokens', 'v_ev_w_in', 'v_ev_conv_w', 'v_ev_conv_b', 'v_ev_w_rg_a', 'v_ev_b_rg_a', 'v_ev_w_rg_x', 'v_ev_b_rg_x', 'v_ev_lru_lambda', 'v_ev_q_norm_g', 'v_ev_w_uq', 'v_ev_kv_norm_g', 'v_ev_w_ukv', 'v_ev_w_out', 'v_od_w_in', 'v_od_w_out', 'v_ln_mix_g', 'v_ln_mix_b', 'v_mlp_w1', 'v_mlp_w2', 'v_ln_mlp_g', 'v_ln_mlp_b']
TWIN_OUTPUTS = ['loss', 'grad_x', 'grad_meta_tokens', 'grad_ev_w_in', 'grad_ev_conv_w', 'grad_ev_conv_b', 'grad_ev_w_rg_a', 'grad_ev_b_rg_a', 'grad_ev_w_rg_x', 'grad_ev_b_rg_x', 'grad_ev_lru_lambda', 'grad_ev_q_norm_g', 'grad_ev_w_uq', 'grad_ev_kv_norm_g', 'grad_ev_w_ukv', 'grad_ev_w_out', 'grad_od_w_in', 'grad_od_w_out', 'grad_ln_mix_g', 'grad_ln_mix_b', 'grad_mlp_w1', 'grad_mlp_w2', 'grad_ln_mlp_g', 'grad_ln_mlp_b', 'delta_meta_tokens', 'delta_ev_w_in', 'delta_ev_conv_w', 'delta_ev_conv_b', 'delta_ev_w_rg_a', 'delta_ev_b_rg_a', 'delta_ev_w_rg_x', 'delta_ev_b_rg_x', 'delta_ev_lru_lambda', 'delta_ev_q_norm_g', 'delta_ev_w_uq', 'delta_ev_kv_norm_g', 'delta_ev_w_ukv', 'delta_ev_w_out', 'delta_od_w_in', 'delta_od_w_out', 'delta_ln_mix_g', 'delta_ln_mix_b', 'delta_mlp_w1', 'delta_mlp_w2', 'delta_ln_mlp_g', 'delta_ln_mlp_b', 'new_m_meta_tokens', 'new_m_ev_w_in', 'new_m_ev_conv_w', 'new_m_ev_conv_b', 'new_m_ev_w_rg_a', 'new_m_ev_b_rg_a', 'new_m_ev_w_rg_x', 'new_m_ev_b_rg_x', 'new_m_ev_lru_lambda', 'new_m_ev_q_norm_g', 'new_m_ev_w_uq', 'new_m_ev_kv_norm_g', 'new_m_ev_w_ukv', 'new_m_ev_w_out', 'new_m_od_w_in', 'new_m_od_w_out', 'new_m_ln_mix_g', 'new_m_ln_mix_b', 'new_m_mlp_w1', 'new_m_mlp_w2', 'new_m_ln_mlp_g', 'new_m_ln_mlp_b', 'new_v_meta_tokens', 'new_v_ev_w_in', 'new_v_ev_conv_w', 'new_v_ev_conv_b', 'new_v_ev_w_rg_a', 'new_v_ev_b_rg_a', 'new_v_ev_w_rg_x', 'new_v_ev_b_rg_x', 'new_v_ev_lru_lambda', 'new_v_ev_q_norm_g', 'new_v_ev_w_uq', 'new_v_ev_kv_norm_g', 'new_v_ev_w_ukv', 'new_v_ev_w_out', 'new_v_od_w_in', 'new_v_od_w_out', 'new_v_ln_mix_g', 'new_v_ln_mix_b', 'new_v_mlp_w1', 'new_v_mlp_w2', 'new_v_ln_mlp_g', 'new_v_ln_mlp_b']
TWIN_LEAF_KINDS = {'loss': 'loss', 'grad_x': 'grad_x', 'grad_meta_tokens': 'grad_w', 'grad_ev_w_in': 'grad_w', 'grad_ev_conv_w': 'grad_w', 'grad_ev_conv_b': 'grad_w', 'grad_ev_w_rg_a': 'grad_w', 'grad_ev_b_rg_a': 'grad_w', 'grad_ev_w_rg_x': 'grad_w', 'grad_ev_b_rg_x': 'grad_w', 'grad_ev_lru_lambda': 'grad_w', 'grad_ev_q_norm_g': 'grad_w', 'grad_ev_w_uq': 'grad_w', 'grad_ev_kv_norm_g': 'grad_w', 'grad_ev_w_ukv': 'grad_w', 'grad_ev_w_out': 'grad_w', 'grad_od_w_in': 'grad_w', 'grad_od_w_out': 'grad_w', 'grad_ln_mix_g': 'grad_w', 'grad_ln_mix_b': 'grad_w', 'grad_mlp_w1': 'grad_w', 'grad_mlp_w2': 'grad_w', 'grad_ln_mlp_g': 'grad_w', 'grad_ln_mlp_b': 'grad_w', 'delta_meta_tokens': 'delta_w', 'delta_ev_w_in': 'delta_w', 'delta_ev_conv_w': 'delta_w', 'delta_ev_conv_b': 'delta_w', 'delta_ev_w_rg_a': 'delta_w', 'delta_ev_b_rg_a': 'delta_w', 'delta_ev_w_rg_x': 'delta_w', 'delta_ev_b_rg_x': 'delta_w', 'delta_ev_lru_lambda': 'delta_w', 'delta_ev_q_norm_g': 'delta_w', 'delta_ev_w_uq': 'delta_w', 'delta_ev_kv_norm_g': 'delta_w', 'delta_ev_w_ukv': 'delta_w', 'delta_ev_w_out': 'delta_w', 'delta_od_w_in': 'delta_w', 'delta_od_w_out': 'delta_w', 'delta_ln_mix_g': 'delta_w', 'delta_ln_mix_b': 'delta_w', 'delta_mlp_w1': 'delta_w', 'delta_mlp_w2': 'delta_w', 'delta_ln_mlp_g': 'delta_w', 'delta_ln_mlp_b': 'delta_w', 'new_m_meta_tokens': 'new_m', 'new_m_ev_w_in': 'new_m', 'new_m_ev_conv_w': 'new_m', 'new_m_ev_conv_b': 'new_m', 'new_m_ev_w_rg_a': 'new_m', 'new_m_ev_b_rg_a': 'new_m', 'new_m_ev_w_rg_x': 'new_m', 'new_m_ev_b_rg_x': 'new_m', 'new_m_ev_lru_lambda': 'new_m', 'new_m_ev_q_norm_g': 'new_m', 'new_m_ev_w_uq': 'new_m', 'new_m_ev_kv_norm_g': 'new_m', 'new_m_ev_w_ukv': 'new_m', 'new_m_ev_w_out': 'new_m', 'new_m_od_w_in': 'new_m', 'new_m_od_w_out': 'new_m', 'new_m_ln_mix_g': 'new_m', 'new_m_ln_mix_b': 'new_m', 'new_m_mlp_w1': 'new_m', 'new_m_mlp_w2': 'new_m', 'new_m_ln_mlp_g': 'new_m', 'new_m_ln_mlp_b': 'new_m', 'new_v_meta_tokens': 'new_v', 'new_v_ev_w_in': 'new_v', 'new_v_ev_conv_w': 'new_v', 'new_v_ev_conv_b': 'new_v', 'new_v_ev_w_rg_a': 'new_v', 'new_v_ev_b_rg_a': 'new_v', 'new_v_ev_w_rg_x': 'new_v', 'new_v_ev_b_rg_x': 'new_v', 'new_v_ev_lru_lambda': 'new_v', 'new_v_ev_q_norm_g': 'new_v', 'new_v_ev_w_uq': 'new_v', 'new_v_ev_kv_norm_g': 'new_v', 'new_v_ev_w_ukv': 'new_v', 'new_v_ev_w_out': 'new_v', 'new_v_od_w_in': 'new_v', 'new_v_od_w_out': 'new_v', 'new_v_ln_mix_g': 'new_v', 'new_v_ln_mix_b': 'new_v', 'new_v_mlp_w1': 'new_v', 'new_v_mlp_w2': 'new_v', 'new_v_ln_mlp_g': 'new_v', 'new_v_ln_mlp_b': 'new_v'}


def _forward(args):
    return _fwd_reference(*[args[k] for k in FWD_PARAMS])


def _output_shape():
    out = _jax.eval_shape(lambda: _forward(_fwd_setup_inputs(0)))
    return out.shape, out.dtype

N_MICROBATCH = 1
ADAM_LR = 0.001
ADAM_B1 = 0.9
ADAM_B2 = 0.999
ADAM_EPS = 1e-08
ADAM_WD = 0.01
ADAM_STEP = 10
PER_EXAMPLE_BATCH_AXIS = {'x': 0, 'loss_target': 0}
SHARED_INPUTS = []
_WEIGHT_DTYPES = {'meta_tokens': _jnp.float32, 'ev_w_in': _jnp.float32, 'ev_conv_w': _jnp.float32, 'ev_conv_b': _jnp.float32, 'ev_w_rg_a': _jnp.float32, 'ev_b_rg_a': _jnp.float32, 'ev_w_rg_x': _jnp.float32, 'ev_b_rg_x': _jnp.float32, 'ev_lru_lambda': _jnp.float32, 'ev_q_norm_g': _jnp.float32, 'ev_w_uq': _jnp.float32, 'ev_kv_norm_g': _jnp.float32, 'ev_w_ukv': _jnp.float32, 'ev_w_out': _jnp.float32, 'od_w_in': _jnp.float32, 'od_w_out': _jnp.float32, 'ln_mix_g': _jnp.float32, 'ln_mix_b': _jnp.float32, 'mlp_w1': _jnp.float32, 'mlp_w2': _jnp.float32, 'ln_mlp_g': _jnp.float32, 'ln_mlp_b': _jnp.float32}
MOMENT_SCALE = {'meta_tokens': 4.448480e-03, 'ev_w_in': 2.552745e-02, 'ev_conv_w': 3.303474e-02, 'ev_conv_b': 4.452818e-01, 'ev_w_rg_a': 1.193671e-02, 'ev_b_rg_a': 8.327818e-03, 'ev_w_rg_x': 2.163514e-02, 'ev_b_rg_x': 1.294437e-02, 'ev_lru_lambda': 1.645621e-02, 'ev_q_norm_g': 1.751859e-02, 'ev_w_uq': 9.722169e-03, 'ev_kv_norm_g': 3.670177e-02, 'ev_w_ukv': 1.207509e-02, 'ev_w_out': 4.550460e-02, 'od_w_in': 3.134049e-02, 'od_w_out': 7.489749e-02, 'ln_mix_g': 8.123301e-01, 'ln_mix_b': 4.842308e-01, 'mlp_w1': 4.288900e-02, 'mlp_w2': 1.955862e-01, 'ln_mlp_g': 2.270732e+01, 'ln_mlp_b': 4.999300e+00}


def _to_microbatches(a, axis):
    t = _jnp.moveaxis(a, axis, 0)
    t = t.reshape((N_MICROBATCH, t.shape[0] // N_MICROBATCH) + t.shape[1:])
    return _jnp.moveaxis(t, 1, axis + 1)


def setup_inputs(seed: int = 0) -> dict:
    inp = _fwd_setup_inputs(seed)
    key = _jax.random.fold_in(_jax.random.key(seed), 7919)
    shape, _ = _output_shape()
    out = dict(inp)
    out["loss_target"] = _jax.random.normal(_jax.random.fold_in(key, 0), shape, _jnp.float32)
    for i, name in enumerate(TWIN_WEIGHTS):
        w = inp[name].astype(_jnp.float32)
        if MOMENT_SCALE is None:
            s = _jnp.sqrt(_jnp.mean(_jnp.square(w)) + 1e-30)
        else:
            s = MOMENT_SCALE[name]
        km, kv = _jax.random.split(_jax.random.fold_in(key, i + 1))
        out[name] = w
        out["m_" + name] = s * _jax.random.normal(km, w.shape, _jnp.float32)
        out["v_" + name] = (s * s) * _jax.random.uniform(kv, w.shape, _jnp.float32, 0.5, 1.5)
    if N_MICROBATCH > 1:
        for name, axis in PER_EXAMPLE_BATCH_AXIS.items():
            out[name] = _to_microbatches(out[name], axis)
    return {'x': out['x'], 'meta_tokens': out['meta_tokens'], 'ev_w_in': out['ev_w_in'], 'ev_conv_w': out['ev_conv_w'], 'ev_conv_b': out['ev_conv_b'], 'ev_w_rg_a': out['ev_w_rg_a'], 'ev_b_rg_a': out['ev_b_rg_a'], 'ev_w_rg_x': out['ev_w_rg_x'], 'ev_b_rg_x': out['ev_b_rg_x'], 'ev_lru_lambda': out['ev_lru_lambda'], 'ev_q_norm_g': out['ev_q_norm_g'], 'ev_w_uq': out['ev_w_uq'], 'ev_kv_norm_g': out['ev_kv_norm_g'], 'ev_w_ukv': out['ev_w_ukv'], 'ev_w_out': out['ev_w_out'], 'od_w_in': out['od_w_in'], 'od_w_out': out['od_w_out'], 'ln_mix_g': out['ln_mix_g'], 'ln_mix_b': out['ln_mix_b'], 'mlp_w1': out['mlp_w1'], 'mlp_w2': out['mlp_w2'], 'ln_mlp_g': out['ln_mlp_g'], 'ln_mlp_b': out['ln_mlp_b'], 'loss_target': out['loss_target'], 'm_meta_tokens': out['m_meta_tokens'], 'm_ev_w_in': out['m_ev_w_in'], 'm_ev_conv_w': out['m_ev_conv_w'], 'm_ev_conv_b': out['m_ev_conv_b'], 'm_ev_w_rg_a': out['m_ev_w_rg_a'], 'm_ev_b_rg_a': out['m_ev_b_rg_a'], 'm_ev_w_rg_x': out['m_ev_w_rg_x'], 'm_ev_b_rg_x': out['m_ev_b_rg_x'], 'm_ev_lru_lambda': out['m_ev_lru_lambda'], 'm_ev_q_norm_g': out['m_ev_q_norm_g'], 'm_ev_w_uq': out['m_ev_w_uq'], 'm_ev_kv_norm_g': out['m_ev_kv_norm_g'], 'm_ev_w_ukv': out['m_ev_w_ukv'], 'm_ev_w_out': out['m_ev_w_out'], 'm_od_w_in': out['m_od_w_in'], 'm_od_w_out': out['m_od_w_out'], 'm_ln_mix_g': out['m_ln_mix_g'], 'm_ln_mix_b': out['m_ln_mix_b'], 'm_mlp_w1': out['m_mlp_w1'], 'm_mlp_w2': out['m_mlp_w2'], 'm_ln_mlp_g': out['m_ln_mlp_g'], 'm_ln_mlp_b': out['m_ln_mlp_b'], 'v_meta_tokens': out['v_meta_tokens'], 'v_ev_w_in': out['v_ev_w_in'], 'v_ev_conv_w': out['v_ev_conv_w'], 'v_ev_conv_b': out['v_ev_conv_b'], 'v_ev_w_rg_a': out['v_ev_w_rg_a'], 'v_ev_b_rg_a': out['v_ev_b_rg_a'], 'v_ev_w_rg_x': out['v_ev_w_rg_x'], 'v_ev_b_rg_x': out['v_ev_b_rg_x'], 'v_ev_lru_lambda': out['v_ev_lru_lambda'], 'v_ev_q_norm_g': out['v_ev_q_norm_g'], 'v_ev_w_uq': out['v_ev_w_uq'], 'v_ev_kv_norm_g': out['v_ev_kv_norm_g'], 'v_ev_w_ukv': out['v_ev_w_ukv'], 'v_ev_w_out': out['v_ev_w_out'], 'v_od_w_in': out['v_od_w_in'], 'v_od_w_out': out['v_od_w_out'], 'v_ln_mix_g': out['v_ln_mix_g'], 'v_ln_mix_b': out['v_ln_mix_b'], 'v_mlp_w1': out['v_mlp_w1'], 'v_mlp_w2': out['v_mlp_w2'], 'v_ln_mlp_g': out['v_ln_mlp_g'], 'v_ln_mlp_b': out['v_ln_mlp_b']}


def _loss(weights, diff, rest, loss_target):
    with _jax.named_scope("forward"):
        args = {**rest, TWIN_DIFF_INPUT: diff, **{k: w.astype(_WEIGHT_DTYPES[k]) for k, w in weights.items()}}
        y = _forward(args)
    with _jax.named_scope("loss_head"):
        err = _jnp.square(y.astype(_jnp.float32) - loss_target)
        return 0.5 * _jnp.sum(_jnp.mean(err, axis=-1)) if err.ndim else 0.5 * err


def _adamw(w, g, m, v):
    m = ADAM_B1 * m + (1.0 - ADAM_B1) * g
    v = ADAM_B2 * v + (1.0 - ADAM_B2) * _jnp.square(g)
    m_hat = m / (1.0 - ADAM_B1 ** ADAM_STEP)
    v_hat = v / (1.0 - ADAM_B2 ** ADAM_STEP)
    delta = -ADAM_LR * (m_hat / (_jnp.sqrt(v_hat) + ADAM_EPS) + ADAM_WD * w)
    return delta, m, v


def reference(x, meta_tokens, ev_w_in, ev_conv_w, ev_conv_b, ev_w_rg_a, ev_b_rg_a, ev_w_rg_x, ev_b_rg_x, ev_lru_lambda, ev_q_norm_g, ev_w_uq, ev_kv_norm_g, ev_w_ukv, ev_w_out, od_w_in, od_w_out, ln_mix_g, ln_mix_b, mlp_w1, mlp_w2, ln_mlp_g, ln_mlp_b, loss_target, m_meta_tokens, m_ev_w_in, m_ev_conv_w, m_ev_conv_b, m_ev_w_rg_a, m_ev_b_rg_a, m_ev_w_rg_x, m_ev_b_rg_x, m_ev_lru_lambda, m_ev_q_norm_g, m_ev_w_uq, m_ev_kv_norm_g, m_ev_w_ukv, m_ev_w_out, m_od_w_in, m_od_w_out, m_ln_mix_g, m_ln_mix_b, m_mlp_w1, m_mlp_w2, m_ln_mlp_g, m_ln_mlp_b, v_meta_tokens, v_ev_w_in, v_ev_conv_w, v_ev_conv_b, v_ev_w_rg_a, v_ev_b_rg_a, v_ev_w_rg_x, v_ev_b_rg_x, v_ev_lru_lambda, v_ev_q_norm_g, v_ev_w_uq, v_ev_kv_norm_g, v_ev_w_ukv, v_ev_w_out, v_od_w_in, v_od_w_out, v_ln_mix_g, v_ln_mix_b, v_mlp_w1, v_mlp_w2, v_ln_mlp_g, v_ln_mlp_b):
    given = dict(x=x, meta_tokens=meta_tokens, ev_w_in=ev_w_in, ev_conv_w=ev_conv_w, ev_conv_b=ev_conv_b, ev_w_rg_a=ev_w_rg_a, ev_b_rg_a=ev_b_rg_a, ev_w_rg_x=ev_w_rg_x, ev_b_rg_x=ev_b_rg_x, ev_lru_lambda=ev_lru_lambda, ev_q_norm_g=ev_q_norm_g, ev_w_uq=ev_w_uq, ev_kv_norm_g=ev_kv_norm_g, ev_w_ukv=ev_w_ukv, ev_w_out=ev_w_out, od_w_in=od_w_in, od_w_out=od_w_out, ln_mix_g=ln_mix_g, ln_mix_b=ln_mix_b, mlp_w1=mlp_w1, mlp_w2=mlp_w2, ln_mlp_g=ln_mlp_g, ln_mlp_b=ln_mlp_b, loss_target=loss_target, m_meta_tokens=m_meta_tokens, m_ev_w_in=m_ev_w_in, m_ev_conv_w=m_ev_conv_w, m_ev_conv_b=m_ev_conv_b, m_ev_w_rg_a=m_ev_w_rg_a, m_ev_b_rg_a=m_ev_b_rg_a, m_ev_w_rg_x=m_ev_w_rg_x, m_ev_b_rg_x=m_ev_b_rg_x, m_ev_lru_lambda=m_ev_lru_lambda, m_ev_q_norm_g=m_ev_q_norm_g, m_ev_w_uq=m_ev_w_uq, m_ev_kv_norm_g=m_ev_kv_norm_g, m_ev_w_ukv=m_ev_w_ukv, m_ev_w_out=m_ev_w_out, m_od_w_in=m_od_w_in, m_od_w_out=m_od_w_out, m_ln_mix_g=m_ln_mix_g, m_ln_mix_b=m_ln_mix_b, m_mlp_w1=m_mlp_w1, m_mlp_w2=m_mlp_w2, m_ln_mlp_g=m_ln_mlp_g, m_ln_mlp_b=m_ln_mlp_b, v_meta_tokens=v_meta_tokens, v_ev_w_in=v_ev_w_in, v_ev_conv_w=v_ev_conv_w, v_ev_conv_b=v_ev_conv_b, v_ev_w_rg_a=v_ev_w_rg_a, v_ev_b_rg_a=v_ev_b_rg_a, v_ev_w_rg_x=v_ev_w_rg_x, v_ev_b_rg_x=v_ev_b_rg_x, v_ev_lru_lambda=v_ev_lru_lambda, v_ev_q_norm_g=v_ev_q_norm_g, v_ev_w_uq=v_ev_w_uq, v_ev_kv_norm_g=v_ev_kv_norm_g, v_ev_w_ukv=v_ev_w_ukv, v_ev_w_out=v_ev_w_out, v_od_w_in=v_od_w_in, v_od_w_out=v_od_w_out, v_ln_mix_g=v_ln_mix_g, v_ln_mix_b=v_ln_mix_b, v_mlp_w1=v_mlp_w1, v_mlp_w2=v_mlp_w2, v_ln_mlp_g=v_ln_mlp_g, v_ln_mlp_b=v_ln_mlp_b)
    weights = {n: given[n] for n in TWIN_WEIGHTS}
    shared = {n: given[n] for n in SHARED_INPUTS}
    per_example = {n: given[n] for n in ['x']}
    grad_fn = _jax.value_and_grad(_loss, argnums=(0, 1))

    def one_microbatch(ex, loss_target):
        ex = dict(ex)
        diff = ex.pop(TWIN_DIFF_INPUT)
        return grad_fn(weights, diff, {**shared, **ex}, loss_target)

    if N_MICROBATCH == 1:
        loss, (grad_w, grad_x) = one_microbatch(per_example, given["loss_target"])
    else:
        def body(carry, xs):
            loss_sum, grad_sum = carry
            l_k, (gw_k, gx_k) = one_microbatch(xs[0], xs[1])
            with _jax.named_scope("update"):
                return (loss_sum + l_k, _jax.tree.map(_jnp.add, grad_sum, gw_k)), gx_k

        init = (_jnp.zeros((), _jnp.float32), _jax.tree.map(_jnp.zeros_like, weights))
        (loss, grad_w), grad_x = _jax.lax.scan(body, init, (per_example, given["loss_target"]))
    with _jax.named_scope("update"):
        delta_w, new_m, new_v = {}, {}, {}
        for n in TWIN_WEIGHTS:
            delta_w[n], new_m[n], new_v[n] = _adamw(weights[n], grad_w[n], given["m_" + n], given["v_" + n])
    return (loss, grad_x, *[grad_w[n] for n in TWIN_WEIGHTS], *[delta_w[n] for n in TWIN_WEIGHTS],
            *[new_m[n] for n in TWIN_WEIGHTS], *[new_v[n] for n in TWIN_WEIGHTS])
```

```python
import functools
import math

import jax
import jax.numpy as jnp
from jax import lax
from jax.experimental import pallas as pl
from jax.experimental.pallas import tpu as pltpu

F32 = jnp.float32
MXU_DTYPE = jnp.bfloat16

D_MODEL = 1024
N_META = 16
LRU_WIDTH = 512
LRU_HEADS = 4
LRU_HEAD_DIM = 128
CONV_WIDTH = 4
LRU_C = 8.0
MLA_HEADS = 8
MLA_NOPE = 64
MLA_ROPE = 32
MLA_V = 64
MLA_Q_RANK = 256
MLA_KV_RANK = 128
RET_HEADS = 4
RET_QK_DIM = 256
RET_V_DIM = 512
D_FF = 4096
ROPE_BASE = 10000.0
DN_ALPHA = 4.0 ** 0.25
EPS = 1e-5
NEG_INF = -1e30
SEQ_BLOCK = 128

ADAM_LR = 0.001
ADAM_B1 = 0.9
ADAM_B2 = 0.999
ADAM_EPS = 1e-08
ADAM_WD = 0.01
ADAM_STEP = 10

PACK_COLS = 1024
N_CHIPS = 4

MESH = pl.DeviceIdType.MESH


def _pick(n, target, align):
    best = None
    for t in range(align, min(n, target) + 1, align):
        if n % t == 0:
            best = t
    return n if best is None else best


def _round_up(n, m):
    return (n + m - 1) // m * m


def _relu2(a):
    r = jnp.maximum(a, 0.0)
    return r * r


def _mm_nn(a, w, act, name):
    M, K = a.shape
    _, N = w.shape
    tm, tn, tk = _pick(M, 1088, 8), _pick(N, 1024, 128), _pick(K, 512, 128)
    nk = K // tk

    def body(a_ref, w_ref, o_ref, acc_ref):
        k = pl.program_id(2)

        @pl.when(k == 0)
        def _():
            acc_ref[...] = jnp.zeros_like(acc_ref)

        av = a_ref[...]
        if act:
            av = _relu2(av)
        acc_ref[...] += jnp.dot(av.astype(MXU_DTYPE), w_ref[...].astype(MXU_DTYPE), preferred_element_type=F32)

        @pl.when(k == nk - 1)
        def _():
            o_ref[...] = acc_ref[...]

    return pl.pallas_call(
        body, name=name,
        grid=(M // tm, N // tn, nk),
        in_specs=[pl.BlockSpec((tm, tk), lambda i, j, k: (i, k)), pl.BlockSpec((tk, tn), lambda i, j, k: (k, j))],
        out_specs=pl.BlockSpec((tm, tn), lambda i, j, k: (i, j)),
        out_shape=jax.ShapeDtypeStruct((M, N), F32),
        scratch_shapes=[pltpu.VMEM((tm, tn), F32)],
        compiler_params=pltpu.CompilerParams(dimension_semantics=("parallel", "parallel", "arbitrary")),
    )(a, w)


def _mm_nt(g, w, a_src, name):
    M, N = g.shape
    K, _ = w.shape
    tm, tn, tk = _pick(M, 1088, 8), _pick(K, 1024, 128), _pick(N, 512, 128)
    nk = N // tk
    has_src = a_src is not None

    def body(*refs):
        if has_src:
            g_ref, w_ref, s_ref, o_ref, acc_ref = refs
        else:
            g_ref, w_ref, o_ref, acc_ref = refs
        k = pl.program_id(2)

        @pl.when(k == 0)
        def _():
            acc_ref[...] = jnp.zeros_like(acc_ref)

        acc_ref[...] += lax.dot_general(g_ref[...].astype(MXU_DTYPE), w_ref[...].astype(MXU_DTYPE),
                                        (((1,), (1,)), ((), ())), preferred_element_type=F32)

        @pl.when(k == nk - 1)
        def _():
            r = acc_ref[...]
            if has_src:
                r = r * (2.0 * jnp.maximum(s_ref[...], 0.0))
            o_ref[...] = r

    in_specs = [pl.BlockSpec((tm, tk), lambda i, j, k: (i, k)), pl.BlockSpec((tn, tk), lambda i, j, k: (j, k))]
    args = [g, w]
    if has_src:
        in_specs.append(pl.BlockSpec((tm, tn), lambda i, j, k: (i, j)))
        args.append(a_src)
    return pl.pallas_call(
        body, name=name,
        grid=(M // tm, K // tn, nk),
        in_specs=in_specs,
        out_specs=pl.BlockSpec((tm, tn), lambda i, j, k: (i, j)),
        out_shape=jax.ShapeDtypeStruct((M, K), F32),
        scratch_shapes=[pltpu.VMEM((tm, tn), F32)],
        compiler_params=pltpu.CompilerParams(dimension_semantics=("parallel", "parallel", "arbitrary")),
    )(*args)


def _mm_tn(a, g, act, name):
    M, K = a.shape
    _, N = g.shape
    tm, tn, tk = _pick(K, 1024, 128), _pick(N, 1024, 128), _pick(M, 544, 8)
    nk = M // tk

    def body(a_ref, g_ref, o_ref, acc_ref):
        k = pl.program_id(2)

        @pl.when(k == 0)
        def _():
            acc_ref[...] = jnp.zeros_like(acc_ref)

        av = a_ref[...]
        if act:
            av = _relu2(av)
        acc_ref[...] += lax.dot_general(av.astype(MXU_DTYPE), g_ref[...].astype(MXU_DTYPE),
                                        (((0,), (0,)), ((), ())), preferred_element_type=F32)

        @pl.when(k == nk - 1)
        def _():
            o_ref[...] = acc_ref[...]

    return pl.pallas_call(
        body, name=name,
        grid=(K // tm, N // tn, nk),
        in_specs=[pl.BlockSpec((tk, tm), lambda i, j, k: (k, i)), pl.BlockSpec((tk, tn), lambda i, j, k: (k, j))],
        out_specs=pl.BlockSpec((tm, tn), lambda i, j, k: (i, j)),
        out_shape=jax.ShapeDtypeStruct((K, N), F32),
        scratch_shapes=[pltpu.VMEM((tm, tn), F32)],
        compiler_params=pltpu.CompilerParams(dimension_semantics=("parallel", "parallel", "arbitrary")),
    )(a, g)


@functools.partial(jax.custom_vjp, nondiff_argnums=(3, 4))
def matmul(a, w, w_grad_slot, act, name):
    return _mm_nn(a, w, act, name + "_fwd")


def _matmul_fwd(a, w, w_grad_slot, act, name):
    return _mm_nn(a, w, act, name + "_fwd"), (a, w)


def _matmul_bwd(act, name, res, g):
    a, w = res
    da = _mm_nt(g, w, a if act else None, name + "_dx")
    dw = _mm_tn(a, g, act, name + "_dw")
    return da, None, dw


matmul.defvjp(_matmul_fwd, _matmul_bwd)


def _ln_stats(z):
    mu = jnp.mean(z, axis=-1, keepdims=True)
    zc = z - mu
    var = jnp.mean(zc * zc, axis=-1, keepdims=True)
    return zc, lax.rsqrt(var + EPS)


def _ln_fwd_call(resid, branch, g, b, name):
    M, D = resid.shape
    tm = _pick(M, 544, 8)

    def body(r_ref, br_ref, g_ref, b_ref, o_ref):
        zc, rstd = _ln_stats(DN_ALPHA * r_ref[...] + br_ref[...])
        o_ref[...] = zc * rstd * g_ref[...] + b_ref[...]

    row = pl.BlockSpec((tm, D), lambda i: (i, 0))
    vec = pl.BlockSpec((1, D), lambda i: (0, 0))
    return pl.pallas_call(
        body, name=name, grid=(M // tm,), in_specs=[row, row, vec, vec], out_specs=row,
        out_shape=jax.ShapeDtypeStruct((M, D), F32),
        compiler_params=pltpu.CompilerParams(dimension_semantics=("parallel",)),
    )(resid, branch, g.reshape(1, D), b.reshape(1, D))


def _ln_bwd_call(resid, branch, g, dy, name):
    M, D = resid.shape
    tm = _pick(M, 544, 8)

    def body(r_ref, br_ref, g_ref, dy_ref, dz_ref, dg_ref, db_ref):
        @pl.when(pl.program_id(0) == 0)
        def _():
            dg_ref[...] = jnp.zeros_like(dg_ref)
            db_ref[...] = jnp.zeros_like(db_ref)

        zc, rstd = _ln_stats(DN_ALPHA * r_ref[...] + br_ref[...])
        xhat = zc * rstd
        dy = dy_ref[...]
        dxh = dy * g_ref[...]
        m1 = jnp.mean(dxh, axis=-1, keepdims=True)
        m2 = jnp.mean(dxh * xhat, axis=-1, keepdims=True)
        dz_ref[...] = rstd * (dxh - m1 - xhat * m2)
        dg_ref[...] += jnp.sum(dy * xhat, axis=0, keepdims=True)
        db_ref[...] += jnp.sum(dy, axis=0, keepdims=True)

    row = pl.BlockSpec((tm, D), lambda i: (i, 0))
    vec = pl.BlockSpec((1, D), lambda i: (0, 0))
    return pl.pallas_call(
        body, name=name, grid=(M // tm,), in_specs=[row, row, vec, row], out_specs=[row, vec, vec],
        out_shape=[jax.ShapeDtypeStruct((M, D), F32), jax.ShapeDtypeStruct((1, D), F32), jax.ShapeDtypeStruct((1, D), F32)],
        compiler_params=pltpu.CompilerParams(dimension_semantics=("arbitrary",)),
    )(resid, branch, g.reshape(1, D), dy)


@functools.partial(jax.custom_vjp, nondiff_argnums=(4,))
def deepnorm(resid, branch, g, b, name):
    return _ln_fwd_call(resid, branch, g, b, name + "_fwd")


def _deepnorm_fwd(resid, branch, g, b, name):
    return _ln_fwd_call(resid, branch, g, b, name + "_fwd"), (resid, branch, g)


def _deepnorm_bwd(name, res, dy):
    resid, branch, g = res
    dz, dg, db = _ln_bwd_call(resid, branch, g, dy, name + "_bwd")
    return DN_ALPHA * dz, dz, dg.reshape(g.shape), db.reshape(g.shape)


deepnorm.defvjp(_deepnorm_fwd, _deepnorm_bwd)


def _rms_fwd_call(x, g, name):
    R, W = x.shape
    tr = _pick(R, 1088, 8)

    def body(x_ref, g_ref, o_ref):
        xv = x_ref[...]
        rstd = lax.rsqrt(jnp.mean(xv * xv, axis=-1, keepdims=True) + EPS)
        o_ref[...] = xv * rstd * g_ref[...]

    row = pl.BlockSpec((tr, W), lambda i: (i, 0))
    vec = pl.BlockSpec((1, W), lambda i: (0, 0))
    return pl.pallas_call(
        body, name=name, grid=(R // tr,), in_specs=[row, vec], out_specs=row,
        out_shape=jax.ShapeDtypeStruct((R, W), F32),
        compiler_params=pltpu.CompilerParams(dimension_semantics=("parallel",)),
    )(x, g.reshape(1, W))


def _rms_bwd_call(x, g, dy, name):
    R, W = x.shape
    tr = _pick(R, 1088, 8)

    def body(x_ref, g_ref, dy_ref, dx_ref, dg_ref):
        @pl.when(pl.program_id(0) == 0)
        def _():
            dg_ref[...] = jnp.zeros_like(dg_ref)

        xv = x_ref[...]
        rstd = lax.rsqrt(jnp.mean(xv * xv, axis=-1, keepdims=True) + EPS)
        xhat = xv * rstd
        dy = dy_ref[...]
        dxh = dy * g_ref[...]
        dx_ref[...] = rstd * (dxh - xhat * jnp.mean(dxh * xhat, axis=-1, keepdims=True))
        dg_ref[...] += jnp.sum(dy * xhat, axis=0, keepdims=True)

    row = pl.BlockSpec((tr, W), lambda i: (i, 0))
    vec = pl.BlockSpec((1, W), lambda i: (0, 0))
    return pl.pallas_call(
        body, name=name, grid=(R // tr,), in_specs=[row, vec, row], out_specs=[row, vec],
        out_shape=[jax.ShapeDtypeStruct((R, W), F32), jax.ShapeDtypeStruct((1, W), F32)],
        compiler_params=pltpu.CompilerParams(dimension_semantics=("arbitrary",)),
    )(x, g.reshape(1, W), dy)


@functools.partial(jax.custom_vjp, nondiff_argnums=(2,))
def rmsnorm(x, g, name):
    return _rms_fwd_call(x, g, name + "_fwd")


def _rmsnorm_fwd(x, g, name):
    return _rms_fwd_call(x, g, name + "_fwd"), (x, g)


def _rmsnorm_bwd(name, res, dy):
    x, g = res
    dx, dg = _rms_bwd_call(x, g, dy, name + "_bwd")
    return dx, dg.reshape(g.shape)


rmsnorm.defvjp(_rmsnorm_fwd, _rmsnorm_bwd)


def _loss_call(y, tgt, name):
    R, D = y.shape
    tr = _pick(R, 512, 8)

    def body(y_ref, t_ref, dy_ref, acc_ref):
        @pl.when(pl.program_id(0) == 0)
        def _():
            acc_ref[...] = jnp.zeros_like(acc_ref)

        e = y_ref[...] - t_ref[...]
        dy_ref[...] = e * (1.0 / D)
        acc_ref[...] += jnp.sum(jnp.sum(e * e, axis=-1, keepdims=True), axis=0, keepdims=True) * (0.5 / D)

    row = pl.BlockSpec((tr, D), lambda i: (i, 0))
    one = pl.BlockSpec((1, 1), lambda i: (0, 0))
    return pl.pallas_call(
        body, name=name, grid=(R // tr,), in_specs=[row, row], out_specs=[row, one],
        out_shape=[jax.ShapeDtypeStruct((R, D), F32), jax.ShapeDtypeStruct((1, 1), F32)],
        compiler_params=pltpu.CompilerParams(dimension_semantics=("arbitrary",)),
    )(y, tgt)


@jax.custom_vjp
def loss_head(y, tgt):
    return _loss_call(y, tgt, "loss_head")[1][0, 0]


def _loss_head_fwd(y, tgt):
    dy, acc = _loss_call(y, tgt, "loss_head")
    return acc[0, 0], dy


def _loss_head_bwd(dy, ct):
    return ct * dy, None


loss_head.defvjp(_loss_head_fwd, _loss_head_bwd)


_GELU_C = math.sqrt(2.0 / math.pi)


def _gelu_parts(x):
    x2 = x * x
    t = jnp.tanh(_GELU_C * (x + 0.044715 * x * x2))
    gelu = 0.5 * x * (1.0 + t)
    dgelu = 0.5 * (1.0 + t) + 0.5 * x * (1.0 - t * t) * (_GELU_C * (1.0 + 3.0 * 0.044715 * x2))
    return gelu, dgelu


def _sigmoid(x):
    return 1.0 / (1.0 + jnp.exp(-x))


def _scan8(a, b, carry, reverse):
    row = lax.broadcasted_iota(jnp.int32, a.shape, 0)
    for s in (1, 2, 4):
        shift = 8 - s if reverse else s
        keep = (row < 8 - s) if reverse else (row >= s)
        b = jnp.where(keep, a * pltpu.roll(b, shift, 0) + b, b)
        a = jnp.where(keep, a * pltpu.roll(a, shift, 0), a)
    return a * carry + b


def _lru_pre(prec_ref, prev_ref, first, cw_ref, cb_ref, wa_ref, ba_ref, wx_ref, bx_ref, sp_ref):
    tc = prec_ref.shape[0]
    prev = jnp.where(first, 0.0, prev_ref[...])
    ext = jnp.concatenate([prev, prec_ref[...]], axis=0)
    cw = cw_ref[...]
    taps = [ext[8:] if k == CONV_WIDTH - 1 else pltpu.roll(ext, CONV_WIDTH - 1 - k, 0)[8:] for k in range(CONV_WIDTH)]
    xc = cb_ref[...] + sum(cw[k:k + 1, :] * taps[k] for k in range(CONV_WIDTH))
    ga, gx = [], []
    for h in range(LRU_HEADS):
        xh = xc[:, h * LRU_HEAD_DIM:(h + 1) * LRU_HEAD_DIM].astype(MXU_DTYPE)
        ga.append(jnp.dot(xh, wa_ref[h].astype(MXU_DTYPE), preferred_element_type=F32))
        gx.append(jnp.dot(xh, wx_ref[h].astype(MXU_DTYPE), preferred_element_type=F32))
    r = _sigmoid(jnp.concatenate(ga, axis=1) + ba_ref[...])
    i = _sigmoid(jnp.concatenate(gx, axis=1) + bx_ref[...])
    log_a = -LRU_C * r * sp_ref[...]
    a = jnp.exp(log_a)
    a2 = a * a
    mult = jnp.sqrt(-jnp.tanh(log_a) * (a2 + 1.0))
    return taps, xc, r, i, a, a2, mult


def _lru_fwd_call(pg, prec, cw, cb, wa, ba, wx, bx, sp):
    B, Tp, W = prec.shape
    tc = SEQ_BLOCK
    nc = Tp // tc

    def body(pg_ref, prec_ref, prev_ref, cw_ref, cb_ref, wa_ref, ba_ref, wx_ref, bx_ref, sp_ref, y_ref, h_ref, carry_ref):
        first = pl.program_id(1) == 0

        @pl.when(first)
        def _():
            carry_ref[...] = jnp.zeros_like(carry_ref)

        _, xc, r, i, a, a2, mult = _lru_pre(prec_ref, prev_ref, first, cw_ref, cb_ref, wa_ref, ba_ref, wx_ref, bx_ref, sp_ref)
        b = mult * (i * xc)
        carry = carry_ref[0:1, :]
        for t in range(tc // 8):
            h = _scan8(a[8 * t:8 * t + 8], b[8 * t:8 * t + 8], carry, False)
            h_ref[8 * t:8 * t + 8, :] = h
            carry = h[7:8, :]
        carry_ref[...] = jnp.broadcast_to(carry, carry_ref.shape)
        y_ref[...] = h_ref[...] * _gelu_parts(pg_ref[...])[0]

    cur = pl.BlockSpec((None, tc, W), lambda b, j: (b, j, 0))
    prev = pl.BlockSpec((None, 8, W), lambda b, j: (b, jnp.maximum(j * (tc // 8) - 1, 0), 0))
    vec = pl.BlockSpec((1, W), lambda b, j: (0, 0))
    cws = pl.BlockSpec((CONV_WIDTH, W), lambda b, j: (0, 0))
    wsp = pl.BlockSpec((LRU_HEADS, LRU_HEAD_DIM, LRU_HEAD_DIM), lambda b, j: (0, 0, 0))
    return pl.pallas_call(
        body, name="lru_fwd", grid=(B, nc),
        in_specs=[cur, cur, prev, cws, vec, wsp, vec, wsp, vec, vec],
        out_specs=[cur, cur],
        out_shape=[jax.ShapeDtypeStruct((B, Tp, W), F32), jax.ShapeDtypeStruct((B, Tp, W), F32)],
        scratch_shapes=[pltpu.VMEM((8, W), F32)],
        compiler_params=pltpu.CompilerParams(dimension_semantics=("arbitrary", "arbitrary")),
    )(pg, prec, prec, cw, cb, wa, ba, wx, bx, sp)


def _lru_bwd_call(pg, prec, hseq, dy, cw, cb, wa, ba, wx, bx, sp):
    B, Tp, W = prec.shape
    tc = SEQ_BLOCK
    nc = Tp // tc
    HD = LRU_HEAD_DIM

    def body(pg_ref, prec_ref, prev_ref, h_ref, hprev_ref, dy_ref, cw_ref, cb_ref, wa_ref, ba_ref, wx_ref, bx_ref, sp_ref,
             dpg_ref, dprec_ref, dcw_ref, dcb_ref, dwa_ref, dba_ref, dwx_ref, dbx_ref, dsp_ref,
             gcar_ref, anext_ref, halo_ref, g_ref):
        j = pl.program_id(1)
        first = j == nc - 1
        last = j == 0

        @pl.when(jnp.logical_and(pl.program_id(0) == 0, last))
        def _():
            for ref in (dcw_ref, dcb_ref, dwa_ref, dba_ref, dwx_ref, dbx_ref, dsp_ref):
                ref[...] = jnp.zeros_like(ref)

        @pl.when(last)
        def _():
            gcar_ref[...] = jnp.zeros_like(gcar_ref)
            anext_ref[...] = jnp.zeros_like(anext_ref)
            halo_ref[...] = jnp.zeros_like(halo_ref)

        taps, xc, r, i, a, a2, mult = _lru_pre(prec_ref, prev_ref, first, cw_ref, cb_ref, wa_ref, ba_ref, wx_ref, bx_ref, sp_ref)
        row = lax.broadcasted_iota(jnp.int32, (tc, W), 0)
        gelu, dgelu = _gelu_parts(pg_ref[...])
        dy = dy_ref[...]
        hcur = h_ref[...]
        dpg_ref[...] = dy * hcur * dgelu
        dh = dy * gelu
        a_next = jnp.where(row == tc - 1, anext_ref[0:1, :], pltpu.roll(a, tc - 1, 0))
        carry = gcar_ref[0:1, :]
        for t in reversed(range(tc // 8)):
            g = _scan8(a_next[8 * t:8 * t + 8], dh[8 * t:8 * t + 8], carry, True)
            g_ref[8 * t:8 * t + 8, :] = g
            carry = g[0:1, :]
        gcar_ref[...] = jnp.broadcast_to(carry, gcar_ref.shape)
        anext_ref[...] = jnp.broadcast_to(a[0:1, :], anext_ref.shape)
        G = g_ref[...]
        h_before = jnp.where(first, 0.0, hprev_ref[7:8, :])
        hprev = jnp.where(row == 0, h_before, pltpu.roll(hcur, 1, 0))
        d_a = G * hprev
        gx_ = G * xc
        d_mult = gx_ * i
        d_i = gx_ * mult
        dxc = G * (mult * i)
        d_la = d_a * a - d_mult * (a2 / mult)
        sp = sp_ref[...]
        d_r = d_la * (-LRU_C * sp)
        dsp_ref[...] += jnp.sum(d_la * (-LRU_C * r), axis=0, keepdims=True)
        dga = d_r * r * (1.0 - r)
        dgx = d_i * i * (1.0 - i)
        dba_ref[...] += jnp.sum(dga, axis=0, keepdims=True)
        dbx_ref[...] += jnp.sum(dgx, axis=0, keepdims=True)
        back = []
        for h in range(LRU_HEADS):
            sl = slice(h * HD, (h + 1) * HD)
            xh = xc[:, sl].astype(MXU_DTYPE)
            ah = dga[:, sl].astype(MXU_DTYPE)
            bh = dgx[:, sl].astype(MXU_DTYPE)
            tn = (((0,), (0,)), ((), ()))
            nt = (((1,), (1,)), ((), ()))
            dwa_ref[h] += lax.dot_general(xh, ah, tn, preferred_element_type=F32)
            dwx_ref[h] += lax.dot_general(xh, bh, tn, preferred_element_type=F32)
            back.append(lax.dot_general(ah, wa_ref[h].astype(MXU_DTYPE), nt, preferred_element_type=F32)
                        + lax.dot_general(bh, wx_ref[h].astype(MXU_DTYPE), nt, preferred_element_type=F32))
        dxc = dxc + jnp.concatenate(back, axis=1)
        dcb_ref[...] += jnp.sum(dxc, axis=0, keepdims=True)
        for k in range(CONV_WIDTH):
            dcw_ref[k:k + 1, :] += jnp.sum(dxc * taps[k], axis=0, keepdims=True)
        ext = jnp.concatenate([dxc, halo_ref[...]], axis=0)
        cw = cw_ref[...]
        acc = cw[CONV_WIDTH - 1:CONV_WIDTH, :] * dxc
        for k in range(CONV_WIDTH - 1):
            s = CONV_WIDTH - 1 - k
            acc = acc + cw[k:k + 1, :] * pltpu.roll(ext, tc + 8 - s, 0)[:tc]
        dprec_ref[...] = acc
        halo_ref[...] = dxc[0:8, :]

    rev = lambda j: nc - 1 - j
    cur = pl.BlockSpec((None, tc, W), lambda b, j: (b, rev(j), 0))
    prev = pl.BlockSpec((None, 8, W), lambda b, j: (b, jnp.maximum(rev(j) * (tc // 8) - 1, 0), 0))
    vec = pl.BlockSpec((1, W), lambda b, j: (0, 0))
    cws = pl.BlockSpec((CONV_WIDTH, W), lambda b, j: (0, 0))
    wsp = pl.BlockSpec((LRU_HEADS, HD, HD), lambda b, j: (0, 0, 0))
    seq = jax.ShapeDtypeStruct((B, Tp, W), F32)
    vs = jax.ShapeDtypeStruct((1, W), F32)
    ws = jax.ShapeDtypeStruct((LRU_HEADS, HD, HD), F32)
    return pl.pallas_call(
        body, name="lru_bwd", grid=(B, nc),
        in_specs=[cur, cur, prev, cur, prev, cur, cws, vec, wsp, vec, wsp, vec, vec],
        out_specs=[cur, cur, cws, vec, wsp, vec, wsp, vec, vec],
        out_shape=[seq, seq, jax.ShapeDtypeStruct((CONV_WIDTH, W), F32), vs, ws, vs, ws, vs, vs],
        scratch_shapes=[pltpu.VMEM((8, W), F32), pltpu.VMEM((8, W), F32), pltpu.VMEM((8, W), F32), pltpu.VMEM((tc, W), F32)],
        compiler_params=pltpu.CompilerParams(dimension_semantics=("arbitrary", "arbitrary")),
    )(pg, prec, prec, hseq, hseq, dy, cw, cb, wa, ba, wx, bx, sp)


@jax.custom_vjp
def rglru(pg, prec, cw, cb, wa, ba, wx, bx, sp):
    return _lru_fwd_call(pg, prec, cw, cb, wa, ba, wx, bx, sp)[0]


def _rglru_fwd(pg, prec, cw, cb, wa, ba, wx, bx, sp):
    y, hseq = _lru_fwd_call(pg, prec, cw, cb, wa, ba, wx, bx, sp)
    return y, (pg, prec, hseq, cw, cb, wa, ba, wx, bx, sp)


def _rglru_bwd(res, dy):
    pg, prec, hseq, cw, cb, wa, ba, wx, bx, sp = res
    return tuple(_lru_bwd_call(pg, prec, hseq, dy, cw, cb, wa, ba, wx, bx, sp))


rglru.defvjp(_rglru_fwd, _rglru_bwd)


_NT = (((1,), (1,)), ((), ()))
_TN = (((0,), (0,)), ((), ()))


def _causal_keep(qi, L):
    row = lax.broadcasted_iota(jnp.int32, (SEQ_BLOCK, L), 0) + qi * SEQ_BLOCK
    col = lax.broadcasted_iota(jnp.int32, (SEQ_BLOCK, L), 1)
    return col <= row


def _attn_fwd_call(q, k, v):
    B, H, Tp, dk = q.shape
    dv = v.shape[-1]
    nq = Tp // SEQ_BLOCK
    scale = dk ** -0.5

    def body(q_ref, k_ref, v_ref, o_ref, lse_ref):
        for qi in range(nq):
            L = (qi + 1) * SEQ_BLOCK
            blk = slice(qi * SEQ_BLOCK, L)
            s = lax.dot_general(q_ref[blk, :], k_ref[0:L, :], _NT, preferred_element_type=F32) * scale
            s = jnp.where(_causal_keep(qi, L), s, NEG_INF)
            m = jnp.max(s, axis=-1, keepdims=True)
            p = jnp.exp(s - m)
            l = jnp.sum(p, axis=-1, keepdims=True)
            o = jnp.dot(p.astype(MXU_DTYPE), v_ref[0:L, :], preferred_element_type=F32)
            o_ref[blk, :] = o / l
            lse_ref[blk, :] = m + jnp.log(l)

    def spec(d):
        return pl.BlockSpec((None, None, Tp, d), lambda b, h: (b, h, 0, 0))

    return pl.pallas_call(
        body, name="mla_attn_fwd", grid=(B, H), in_specs=[spec(dk), spec(dk), spec(dv)], out_specs=[spec(dv), spec(1)],
        out_shape=[jax.ShapeDtypeStruct((B, H, Tp, dv), F32), jax.ShapeDtypeStruct((B, H, Tp, 1), F32)],
        compiler_params=pltpu.CompilerParams(dimension_semantics=("parallel", "parallel")),
    )(q, k, v)


def _attn_bwd_call(q, k, v, o, lse, do):
    B, H, Tp, dk = q.shape
    dv = v.shape[-1]
    nq = Tp // SEQ_BLOCK
    scale = dk ** -0.5

    def body(q_ref, k_ref, v_ref, o_ref, lse_ref, do_ref, dq_ref, dk_ref, dv_ref):
        dk_ref[...] = jnp.zeros_like(dk_ref)
        dv_ref[...] = jnp.zeros_like(dv_ref)
        for qi in range(nq):
            L = (qi + 1) * SEQ_BLOCK
            blk = slice(qi * SEQ_BLOCK, L)
            qb = q_ref[blk, :]
            do = do_ref[blk, :]
            delta = jnp.sum(do * o_ref[blk, :], axis=-1, keepdims=True)
            s = lax.dot_general(qb, k_ref[0:L, :], _NT, preferred_element_type=F32) * scale
            s = jnp.where(_causal_keep(qi, L), s, NEG_INF)
            p = jnp.exp(s - lse_ref[blk, :])
            dob = do.astype(MXU_DTYPE)
            dv_ref[0:L, :] += lax.dot_general(p.astype(MXU_DTYPE), dob, _TN, preferred_element_type=F32)
            dp = lax.dot_general(dob, v_ref[0:L, :], _NT, preferred_element_type=F32)
            ds = (p * (dp - delta) * scale).astype(MXU_DTYPE)
            dq_ref[blk, :] = jnp.dot(ds, k_ref[0:L, :], preferred_element_type=F32)
            dk_ref[0:L, :] += lax.dot_general(ds, qb, _TN, preferred_element_type=F32)

    def spec(d):
        return pl.BlockSpec((None, None, Tp, d), lambda b, h: (b, h, 0, 0))

    return pl.pallas_call(
        body, name="mla_attn_bwd", grid=(B, H),
        in_specs=[spec(dk), spec(dk), spec(dv), spec(dv), spec(1), spec(dv)], out_specs=[spec(dk), spec(dk), spec(dv)],
        out_shape=[jax.ShapeDtypeStruct((B, H, Tp, dk), F32), jax.ShapeDtypeStruct((B, H, Tp, dk), F32),
                   jax.ShapeDtypeStruct((B, H, Tp, dv), F32)],
        compiler_params=pltpu.CompilerParams(dimension_semantics=("parallel", "parallel")),
    )(q, k, v, o, lse, do)


@jax.custom_vjp
def attention(q, k, v):
    return _attn_fwd_call(q.astype(MXU_DTYPE), k.astype(MXU_DTYPE), v.astype(MXU_DTYPE))[0]


def _attention_fwd(q, k, v):
    qb, kb, vb = q.astype(MXU_DTYPE), k.astype(MXU_DTYPE), v.astype(MXU_DTYPE)
    o, lse = _attn_fwd_call(qb, kb, vb)
    return o, (qb, kb, vb, o, lse)


def _attention_bwd(res, do):
    return tuple(_attn_bwd_call(*res, do))


attention.defvjp(_attention_fwd, _attention_bwd)


def _decay(qi, L, lg):
    row = lax.broadcasted_iota(jnp.int32, (SEQ_BLOCK, L), 0) + qi * SEQ_BLOCK
    col = lax.broadcasted_iota(jnp.int32, (SEQ_BLOCK, L), 1)
    diff = row - col
    return jnp.where(diff >= 0, jnp.exp(lg * jnp.maximum(diff, 0).astype(F32)), 0.0)


def _ret_fwd_call(q, k, v, lg):
    B, H, Tp, dk = q.shape
    dv = v.shape[-1]
    nq = Tp // SEQ_BLOCK

    def body(q_ref, k_ref, v_ref, lg_ref, o_ref):
        lg_ = lg_ref[...]
        for qi in range(nq):
            L = (qi + 1) * SEQ_BLOCK
            blk = slice(qi * SEQ_BLOCK, L)
            s = lax.dot_general(q_ref[blk, :], k_ref[0:L, :], _NT, preferred_element_type=F32) * _decay(qi, L, lg_)
            o_ref[blk, :] = jnp.dot(s.astype(MXU_DTYPE), v_ref[0:L, :], preferred_element_type=F32)

    def spec(d):
        return pl.BlockSpec((None, None, Tp, d), lambda b, h: (b, h, 0, 0))

    return pl.pallas_call(
        body, name="retention_fwd", grid=(B, H),
        in_specs=[spec(dk), spec(dk), spec(dv), pl.BlockSpec((None, 1, 1), lambda b, h: (h, 0, 0))], out_specs=spec(dv),
        out_shape=jax.ShapeDtypeStruct((B, H, Tp, dv), F32),
        compiler_params=pltpu.CompilerParams(dimension_semantics=("parallel", "parallel")),
    )(q, k, v, lg)


def _ret_bwd_call(q, k, v, lg, do):
    B, H, Tp, dk = q.shape
    dv = v.shape[-1]
    nq = Tp // SEQ_BLOCK

    def body(q_ref, k_ref, v_ref, lg_ref, do_ref, dq_ref, dk_ref, dv_ref):
        lg_ = lg_ref[...]
        dk_ref[...] = jnp.zeros_like(dk_ref)
        dv_ref[...] = jnp.zeros_like(dv_ref)
        for qi in range(nq):
            L = (qi + 1) * SEQ_BLOCK
            blk = slice(qi * SEQ_BLOCK, L)
            qb = q_ref[blk, :]
            dob = do_ref[blk, :].astype(MXU_DTYPE)
            dec = _decay(qi, L, lg_)
            s = (lax.dot_general(qb, k_ref[0:L, :], _NT, preferred_element_type=F32) * dec).astype(MXU_DTYPE)
            dv_ref[0:L, :] += lax.dot_general(s, dob, _TN, preferred_element_type=F32)
            ds = (lax.dot_general(dob, v_ref[0:L, :], _NT, preferred_element_type=F32) * dec).astype(MXU_DTYPE)
            dq_ref[blk, :] = jnp.dot(ds, k_ref[0:L, :], preferred_element_type=F32)
            dk_ref[0:L, :] += lax.dot_general(ds, qb, _TN, preferred_element_type=F32)

    def spec(d):
        return pl.BlockSpec((None, None, Tp, d), lambda b, h: (b, h, 0, 0))

    return pl.pallas_call(
        body, name="retention_bwd", grid=(B, H),
        in_specs=[spec(dk), spec(dk), spec(dv), pl.BlockSpec((None, 1, 1), lambda b, h: (h, 0, 0)), spec(dv)],
        out_specs=[spec(dk), spec(dk), spec(dv)],
        out_shape=[jax.ShapeDtypeStruct((B, H, Tp, dk), F32), jax.ShapeDtypeStruct((B, H, Tp, dk), F32),
                   jax.ShapeDtypeStruct((B, H, Tp, dv), F32)],
        compiler_params=pltpu.CompilerParams(dimension_semantics=("parallel", "parallel")),
    )(q, k, v, lg, do)


def _log_gamma():
    return jnp.log(1.0 - 2.0 ** (-5.0 - jnp.arange(RET_HEADS, dtype=F32))).reshape(RET_HEADS, 1, 1)


@jax.custom_vjp
def retention(q, k, v):
    return _ret_fwd_call(q.astype(MXU_DTYPE), k.astype(MXU_DTYPE), v.astype(MXU_DTYPE), _log_gamma())


def _retention_fwd(q, k, v):
    qb, kb, vb = q.astype(MXU_DTYPE), k.astype(MXU_DTYPE), v.astype(MXU_DTYPE)
    return _ret_fwd_call(qb, kb, vb, _log_gamma()), (qb, kb, vb)


def _retention_bwd(res, do):
    qb, kb, vb = res
    return tuple(_ret_bwd_call(qb, kb, vb, _log_gamma(), do))


retention.defvjp(_retention_fwd, _retention_bwd)


def _rope(x, pos):
    half = x.shape[-1] // 2
    inv = ROPE_BASE ** (-jnp.arange(half, dtype=F32) / half)
    ang = pos.astype(F32)[:, None] * inv[None, :]
    cos = jnp.cos(ang)[None, :, None, :]
    sin = jnp.sin(ang)[None, :, None, :]
    x1, x2 = x[..., :half], x[..., half:]
    return jnp.concatenate([x1 * cos - x2 * sin, x1 * sin + x2 * cos], axis=-1)


BIG_MATRICES = ("ev_w_in", "ev_w_uq", "ev_w_ukv", "ev_w_out", "od_w_in", "od_w_out",
                "mlp_w1_0", "mlp_w1_1", "mlp_w2_0", "mlp_w2_1")


def _local_loss(diff, wfull, tgt):
    x = diff["x"]
    B, S, D = x.shape
    T = S + N_META
    Tp = _round_up(T, SEQ_BLOCK)
    M = B * Tp
    pos = jnp.arange(Tp, dtype=jnp.int32)

    def mm(a, name, act=False):
        return matmul(a, wfull[name], diff[name], act, name)

    meta = jnp.broadcast_to(diff["meta_tokens"][None], (B, N_META, D))
    h = jnp.concatenate([meta, x, jnp.zeros((B, Tp - T, D), F32)], axis=1).reshape(M, D)

    p = mm(h, "ev_w_in")
    c0, c1, c2, c3 = LRU_WIDTH, 2 * LRU_WIDTH, 2 * LRU_WIDTH + MLA_Q_RANK, 2 * LRU_WIDTH + MLA_Q_RANK + MLA_KV_RANK
    p_gate, p_rec, p_q, p_kv, p_kpe = p[:, :c0], p[:, c0:c1], p[:, c1:c2], p[:, c2:c3], p[:, c3:]
    sp = jax.nn.softplus(-diff["ev_lru_lambda"]).reshape(1, LRU_WIDTH)
    y_rec = rglru(p_gate.reshape(B, Tp, LRU_WIDTH), p_rec.reshape(B, Tp, LRU_WIDTH),
                  diff["ev_conv_w"].reshape(CONV_WIDTH, LRU_WIDTH), diff["ev_conv_b"].reshape(1, LRU_WIDTH),
                  diff["ev_w_rg_a"].reshape(LRU_HEADS, LRU_HEAD_DIM, LRU_HEAD_DIM), diff["ev_b_rg_a"].reshape(1, LRU_WIDTH),
                  diff["ev_w_rg_x"].reshape(LRU_HEADS, LRU_HEAD_DIM, LRU_HEAD_DIM), diff["ev_b_rg_x"].reshape(1, LRU_WIDTH),
                  sp).reshape(M, LRU_WIDTH)
    q = mm(rmsnorm(p_q, diff["ev_q_norm_g"].reshape(-1), "q_norm"), "ev_w_uq").reshape(B, Tp, MLA_HEADS, MLA_NOPE + MLA_ROPE)
    q = jnp.concatenate([q[..., :MLA_NOPE], _rope(q[..., MLA_NOPE:], pos)], axis=-1)
    kv = mm(rmsnorm(p_kv, diff["ev_kv_norm_g"].reshape(-1), "kv_norm"), "ev_w_ukv").reshape(B, Tp, MLA_HEADS, MLA_NOPE + MLA_V)
    k_pe = _rope(p_kpe.reshape(B, Tp, 1, MLA_ROPE), pos)
    k = jnp.concatenate([kv[..., :MLA_NOPE], jnp.broadcast_to(k_pe, (B, Tp, MLA_HEADS, MLA_ROPE))], axis=-1)
    v = kv[..., MLA_NOPE:]
    o = attention(q.transpose(0, 2, 1, 3), k.transpose(0, 2, 1, 3), v.transpose(0, 2, 1, 3))
    y_att = o.transpose(0, 2, 1, 3).reshape(M, MLA_HEADS * MLA_V)
    mix = mm(jnp.concatenate([y_rec, y_att], axis=-1), "ev_w_out")
    h = deepnorm(h, mix, diff["ln_mix_g"][0], diff["ln_mix_b"][0], "ln_mix0")
    f = mm(mm(h, "mlp_w1_0"), "mlp_w2_0", act=True)
    h = deepnorm(h, f, diff["ln_mlp_g"][0], diff["ln_mlp_b"][0], "ln_mlp0")

    p = mm(h, "od_w_in")
    nqk = RET_HEADS * RET_QK_DIM
    nv = RET_HEADS * RET_V_DIM
    q = _rope(p[:, :nqk].reshape(B, Tp, RET_HEADS, RET_QK_DIM), pos)
    k = _rope(p[:, nqk:2 * nqk].reshape(B, Tp, RET_HEADS, RET_QK_DIM), pos) * (RET_QK_DIM ** -0.5)
    v = p[:, 2 * nqk:2 * nqk + nv].reshape(B, Tp, RET_HEADS, RET_V_DIM)
    gate = p[:, 2 * nqk + nv:]
    o = retention(q.transpose(0, 2, 1, 3), k.transpose(0, 2, 1, 3), v.transpose(0, 2, 1, 3))
    o = rmsnorm(o.reshape(B * RET_HEADS * Tp, RET_V_DIM), jnp.ones((RET_V_DIM,), F32), "ret_norm")
    y = o.reshape(B, RET_HEADS, Tp, RET_V_DIM).transpose(0, 2, 1, 3).reshape(M, nv)
    mix = mm(jax.nn.silu(gate) * y, "od_w_out")
    h = deepnorm(h, mix, diff["ln_mix_g"][1], diff["ln_mix_b"][1], "ln_mix1")
    f = mm(mm(h, "mlp_w1_1"), "mlp_w2_1", act=True)
    h = deepnorm(h, f, diff["ln_mlp_g"][1], diff["ln_mlp_b"][1], "ln_mlp1")

    y = h.reshape(B, Tp, D)[:, N_META:T].reshape(B * S, D)
    return loss_head(y, tgt.reshape(B * S, D))


_HBM = pl.BlockSpec(memory_space=pltpu.HBM)


def _place():
    return lax.axis_index("x"), lax.axis_index("y"), lax.axis_index("c")


def _other_chips(x, y):
    return [(1 - x, y), (x, 1 - y), (1 - x, 1 - y)]


def _allgather_chips(buf, name):
    R, C = buf.shape
    Rh = R // 2

    def body(x_ref, out_ref, send_sems, recv_sems, local_sem):
        x, y, c = _place()
        sibling = (x, y, 1 - c)
        chips = _other_chips(x, y)

        def half(cx, cy, hc):
            return out_ref.at[2 * cx + cy, pl.ds(hc * Rh, Rh), :]

        def copy(k, src, dst, to):
            return pltpu.make_async_remote_copy(src_ref=src, dst_ref=dst, send_sem=send_sems.at[k], recv_sem=recv_sems.at[k],
                                                device_id=to, device_id_type=MESH)

        mine = pltpu.make_async_copy(x_ref, out_ref.at[2 * x + y], local_sem)
        mine.start()
        first = [copy(j, x_ref.at[pl.ds(c * Rh, Rh), :], half(x, y, c), (*chip, c)) for j, chip in enumerate(chips)]
        for cp in first:
            cp.start()
        passed = [copy(3 + j, half(*chip, c), half(*chip, c), sibling) for j, chip in enumerate(chips)]
        for j, chip in enumerate(chips):
            copy(j, half(*chip, c), half(*chip, c), sibling).wait_recv()
            passed[j].start()
        for j, chip in enumerate(chips):
            copy(3 + j, half(*chip, 1 - c), half(*chip, 1 - c), sibling).wait_recv()
        for cp in first + passed:
            cp.wait_send()
        mine.wait()

    return pl.pallas_call(
        body, name=name, in_specs=[_HBM], out_specs=_HBM,
        out_shape=jax.ShapeDtypeStruct((N_CHIPS, R, C), buf.dtype),
        scratch_shapes=[pltpu.SemaphoreType.DMA((6,)), pltpu.SemaphoreType.DMA((6,)), pltpu.SemaphoreType.DMA],
    )(buf)


def _sibling_exchange(p, name):
    def body(p_ref, mine_ref, recv_ref, send_sem, recv_sem, local_sem):
        x, y, c = _place()
        loc = pltpu.make_async_copy(p_ref.at[c], mine_ref, local_sem)
        rc = pltpu.make_async_remote_copy(src_ref=p_ref.at[1 - c], dst_ref=recv_ref, send_sem=send_sem, recv_sem=recv_sem,
                                          device_id=(x, y, 1 - c), device_id_type=MESH)
        loc.start()
        rc.start()
        rc.wait()
        loc.wait()

    shp = jax.ShapeDtypeStruct(p.shape[1:], p.dtype)
    return pl.pallas_call(
        body, name=name, in_specs=[_HBM], out_specs=[_HBM, _HBM], out_shape=[shp, shp],
        scratch_shapes=[pltpu.SemaphoreType.DMA, pltpu.SemaphoreType.DMA, pltpu.SemaphoreType.DMA],
    )(p)


def _chip_scatter(s, name):
    def body(s_ref, t_ref, send_sems, recv_sems, local_sem):
        x, y, c = _place()
        my = 2 * x + y
        loc = pltpu.make_async_copy(s_ref.at[my], t_ref.at[my], local_sem)
        loc.start()
        copies = [pltpu.make_async_remote_copy(src_ref=s_ref.at[2 * cx + cy], dst_ref=t_ref.at[my], send_sem=send_sems.at[j],
                                               recv_sem=recv_sems.at[j], device_id=(cx, cy, c), device_id_type=MESH)
                  for j, (cx, cy) in enumerate(_other_chips(x, y))]
        for cp in copies:
            cp.start()
        for cp in copies:
            cp.wait()
        loc.wait()

    return pl.pallas_call(
        body, name=name, in_specs=[_HBM], out_specs=_HBM, out_shape=jax.ShapeDtypeStruct(s.shape, s.dtype),
        scratch_shapes=[pltpu.SemaphoreType.DMA((3,)), pltpu.SemaphoreType.DMA((3,)), pltpu.SemaphoreType.DMA],
    )(s)


def _sibling_gather(f, name):
    def body(f_ref, out_ref, send_sem, recv_sem, local_sem):
        x, y, c = _place()
        loc = pltpu.make_async_copy(f_ref, out_ref.at[c], local_sem)
        rc = pltpu.make_async_remote_copy(src_ref=f_ref, dst_ref=out_ref.at[c], send_sem=send_sem, recv_sem=recv_sem,
                                          device_id=(x, y, 1 - c), device_id_type=MESH)
        loc.start()
        rc.start()
        rc.wait()
        loc.wait()

    return pl.pallas_call(
        body, name=name, in_specs=[_HBM], out_specs=_HBM, out_shape=jax.ShapeDtypeStruct((2,) + f.shape, f.dtype),
        scratch_shapes=[pltpu.SemaphoreType.DMA, pltpu.SemaphoreType.DMA, pltpu.SemaphoreType.DMA],
    )(f)


def _add2(a, b, name):
    R, C = a.shape
    tr = _pick(R, 512, 8)

    def body(a_ref, b_ref, o_ref):
        o_ref[...] = a_ref[...] + b_ref[...]

    row = pl.BlockSpec((tr, C), lambda i: (i, 0))
    return pl.pallas_call(body, name=name, grid=(R // tr,), in_specs=[row, row], out_specs=row,
                          out_shape=jax.ShapeDtypeStruct((R, C), a.dtype),
                          compiler_params=pltpu.CompilerParams(dimension_semantics=("parallel",)))(a, b)


def _sum_slots(t, name):
    n, R, C = t.shape
    tr = _pick(R, 512, 8)

    def body(t_ref, o_ref):
        acc = t_ref[0]
        for j in range(1, n):
            acc = acc + t_ref[j]
        o_ref[...] = acc

    return pl.pallas_call(body, name=name, grid=(R // tr,), in_specs=[pl.BlockSpec((n, tr, C), lambda i: (0, i, 0))],
                          out_specs=pl.BlockSpec((tr, C), lambda i: (i, 0)), out_shape=jax.ShapeDtypeStruct((R, C), t.dtype),
                          compiler_params=pltpu.CompilerParams(dimension_semantics=("parallel",)))(t)


def _reduce_to_chips(p):
    _, n, R, C = p.shape
    mine, got = _sibling_exchange(p, "grad_sibling_exchange")
    s = _add2(mine.reshape(n * R, C), got.reshape(n * R, C), "grad_sibling_add").reshape(n, R, C)
    t = _chip_scatter(s, "grad_chip_scatter")
    f = _sum_slots(t, "grad_chip_sum")
    return _sibling_gather(f, "grad_sibling_gather").reshape(2 * R, C)


def _adamw(w, g, m, v, name):
    R, C = w.shape
    tr = _pick(R, 256, 8)

    def body(w_ref, g_ref, m_ref, v_ref, d_ref, nm_ref, nv_ref):
        g_ = g_ref[...]
        m_ = ADAM_B1 * m_ref[...] + (1.0 - ADAM_B1) * g_
        v_ = ADAM_B2 * v_ref[...] + (1.0 - ADAM_B2) * (g_ * g_)
        m_hat = m_ / (1.0 - ADAM_B1 ** ADAM_STEP)
        v_hat = v_ / (1.0 - ADAM_B2 ** ADAM_STEP)
        d_ref[...] = -ADAM_LR * (m_hat / (jnp.sqrt(v_hat) + ADAM_EPS) + ADAM_WD * w_ref[...])
        nm_ref[...] = m_
        nv_ref[...] = v_

    row = pl.BlockSpec((tr, C), lambda i: (i, 0))
    shp = jax.ShapeDtypeStruct((R, C), F32)
    return pl.pallas_call(body, name=name, grid=(R // tr,), in_specs=[row] * 4, out_specs=[row] * 3, out_shape=[shp] * 3,
                          compiler_params=pltpu.CompilerParams(dimension_semantics=("parallel",)))(w, g, m, v)


BIG_SPECS = (("ev_w_in", 1024, 1440, 1), ("ev_w_uq", 256, 768, 1), ("ev_w_ukv", 128, 1024, 1), ("ev_w_out", 1024, 1024, 0),
             ("od_w_in", 1024, 6144, 1), ("od_w_out", 2048, 1024, 0), ("mlp_w1_0", 1024, 4096, 1), ("mlp_w1_1", 1024, 4096, 1),
             ("mlp_w2_0", 4096, 1024, 0), ("mlp_w2_1", 4096, 1024, 0))
BIG_PARAMS = (("ev_w_in", ("ev_w_in",)), ("ev_w_uq", ("ev_w_uq",)), ("ev_w_ukv", ("ev_w_ukv",)), ("ev_w_out", ("ev_w_out",)),
              ("od_w_in", ("od_w_in",)), ("od_w_out", ("od_w_out",)), ("mlp_w1", ("mlp_w1_0", "mlp_w1_1")),
              ("mlp_w2", ("mlp_w2_0", "mlp_w2_1")))
REPLICATED = ("ev_conv_b", "ev_w_rg_a", "ev_b_rg_a", "ev_w_rg_x", "ev_b_rg_x", "ev_lru_lambda", "ev_q_norm_g", "ev_kv_norm_g",
              "ln_mix_g", "ln_mix_b", "ln_mlp_g", "ln_mlp_b")
SMALL_SHARDED = ("meta_tokens", "ev_conv_w")
WEIGHT_NAMES = ("meta_tokens", "ev_w_in", "ev_conv_w", "ev_conv_b", "ev_w_rg_a", "ev_b_rg_a", "ev_w_rg_x", "ev_b_rg_x",
                "ev_lru_lambda", "ev_q_norm_g", "ev_w_uq", "ev_kv_norm_g", "ev_w_ukv", "ev_w_out", "od_w_in", "od_w_out",
                "ln_mix_g", "ln_mix_b", "mlp_w1", "mlp_w2", "ln_mlp_g", "ln_mlp_b")


def _to_rows(flat, row_align):
    n = flat.shape[-1]
    rows = _round_up(-(-n // PACK_COLS), row_align)
    pad = rows * PACK_COLS - n
    if pad:
        flat = jnp.pad(flat, [(0, 0)] * (flat.ndim - 1) + [(0, pad)])
    return flat.reshape(flat.shape[:-1] + (rows, PACK_COLS))


def _shard_shape(K, N, axis):
    return (K // N_CHIPS, N) if axis == 0 else (K, N // N_CHIPS)


def _gather_shards(stacked, K, N, axis):
    if axis == 0:
        return stacked.reshape(K, N)
    return stacked.transpose(1, 0, 2).reshape(K, N)


def _split_shards(full, K, N, axis):
    if axis == 0:
        return full.reshape(N_CHIPS, -1)
    return full.reshape(K, N_CHIPS, N // N_CHIPS).transpose(1, 0, 2).reshape(N_CHIPS, -1)


def kernel(x, meta_tokens, ev_w_in, ev_conv_w, ev_conv_b, ev_w_rg_a, ev_b_rg_a, ev_w_rg_x, ev_b_rg_x, ev_lru_lambda, ev_q_norm_g, ev_w_uq, ev_kv_norm_g, ev_w_ukv, ev_w_out, od_w_in, od_w_out, ln_mix_g, ln_mix_b, mlp_w1, mlp_w2, ln_mlp_g, ln_mlp_b, loss_target, m_meta_tokens, m_ev_w_in, m_ev_conv_w, m_ev_conv_b, m_ev_w_rg_a, m_ev_b_rg_a, m_ev_w_rg_x, m_ev_b_rg_x, m_ev_lru_lambda, m_ev_q_norm_g, m_ev_w_uq, m_ev_kv_norm_g, m_ev_w_ukv, m_ev_w_out, m_od_w_in, m_od_w_out, m_ln_mix_g, m_ln_mix_b, m_mlp_w1, m_mlp_w2, m_ln_mlp_g, m_ln_mlp_b, v_meta_tokens, v_ev_w_in, v_ev_conv_w, v_ev_conv_b, v_ev_w_rg_a, v_ev_b_rg_a, v_ev_w_rg_x, v_ev_b_rg_x, v_ev_lru_lambda, v_ev_q_norm_g, v_ev_w_uq, v_ev_kv_norm_g, v_ev_w_ukv, v_ev_w_out, v_od_w_in, v_od_w_out, v_ln_mix_g, v_ln_mix_b, v_mlp_w1, v_mlp_w2, v_ln_mlp_g, v_ln_mlp_b):
    given = dict(locals())
    local_big = {"ev_w_in": ev_w_in[0], "ev_w_uq": ev_w_uq[0], "ev_w_ukv": ev_w_ukv[0], "ev_w_out": ev_w_out[0],
                 "od_w_in": od_w_in[0], "od_w_out": od_w_out[0], "mlp_w1_0": mlp_w1[0], "mlp_w1_1": mlp_w1[1],
                 "mlp_w2_0": mlp_w2[0], "mlp_w2_1": mlp_w2[1]}

    sizes = [math.prod(_shard_shape(K, N, ax)) for _, K, N, ax in BIG_SPECS]
    packed = _to_rows(jnp.concatenate([local_big[n].astype(MXU_DTYPE).reshape(-1) for n, _, _, _ in BIG_SPECS]), 32)
    gathered = _allgather_chips(packed, "weight_allgather").reshape(N_CHIPS, -1)
    wfull, off = {}, 0
    for (n, K, N, ax), sz in zip(BIG_SPECS, sizes):
        wfull[n] = _gather_shards(gathered[:, off:off + sz].reshape((N_CHIPS,) + _shard_shape(K, N, ax)), K, N, ax)
        off += sz
    small = _to_rows(jnp.concatenate([meta_tokens.reshape(-1), ev_conv_w.reshape(-1)]), 16)
    small = _allgather_chips(small, "small_allgather").reshape(N_CHIPS, -1)
    n_meta, n_conv = meta_tokens.size, ev_conv_w.size
    meta_full = _gather_shards(small[:, :n_meta].reshape(N_CHIPS, N_META, D_MODEL // N_CHIPS), N_META, D_MODEL, 1)
    conv_full = _gather_shards(small[:, n_meta:n_meta + n_conv].reshape(N_CHIPS, CONV_WIDTH, LRU_WIDTH // N_CHIPS),
                               CONV_WIDTH, LRU_WIDTH, 1)

    diff = {n: jnp.zeros((K, N), F32) for n, K, N, _ in BIG_SPECS}
    diff.update({n: given[n] for n in REPLICATED})
    diff.update(x=x, meta_tokens=meta_full, ev_conv_w=conv_full)
    loss, g = jax.value_and_grad(_local_loss)(diff, wfull, loss_target)
    loss = lax.psum(loss, ("x", "y", "c"))

    repl = jnp.concatenate([g[n].reshape(-1) for n in REPLICATED]).reshape(N_CHIPS, -1)
    pieces = [_split_shards(g[n], K, N, ax) for n, K, N, ax in BIG_SPECS]
    pieces += [_split_shards(g["meta_tokens"], N_META, D_MODEL, 1), _split_shards(g["ev_conv_w"], CONV_WIDTH, LRU_WIDTH, 1), repl]
    p = _to_rows(jnp.concatenate(pieces, axis=1), 16)
    rows = p.shape[1]
    p = p.reshape(N_CHIPS, 2, rows // 2, PACK_COLS).transpose(1, 0, 2, 3)
    red = _reduce_to_chips(p).reshape(-1)
    grads, off = {}, 0
    for name, parts in BIG_PARAMS:
        sz = given[name].size
        grads[name] = red[off:off + sz].reshape(given[name].shape)
        off += sz
    for name in SMALL_SHARDED:
        sz = given[name].size
        grads[name] = red[off:off + sz].reshape(given[name].shape)
        off += sz
    n_repl = repl.shape[1]
    repl_all = _allgather_chips(_to_rows(red[off:off + n_repl], 16), "replicated_allgather").reshape(N_CHIPS, -1)[:, :n_repl].reshape(-1)
    off = 0
    for name in REPLICATED:
        sz = given[name].size
        grads[name] = repl_all[off:off + sz].reshape(given[name].shape)
        off += sz

    delta, new_m, new_v = {}, {}, {}
    for name, _ in BIG_PARAMS:
        shp = given[name].shape
        two_d = (-1, shp[-1])
        d, nm, nv = _adamw(given[name].reshape(two_d), grads[name].reshape(two_d), given["m_" + name].reshape(two_d),
                           given["v_" + name].reshape(two_d), "adamw_" + name)
        delta[name], new_m[name], new_v[name] = d.reshape(shp), nm.reshape(shp), nv.reshape(shp)
    smalls = SMALL_SHARDED + REPLICATED

    def pack_small(get):
        return _to_rows(jnp.concatenate([get(n).reshape(-1) for n in smalls]), 8)

    outs = _adamw(pack_small(lambda n: given[n]), pack_small(lambda n: grads[n]), pack_small(lambda n: given["m_" + n]),
                  pack_small(lambda n: given["v_" + n]), "adamw_small")
    for res, flat in zip((delta, new_m, new_v), outs):
        flat, off = flat.reshape(-1), 0
        for n in smalls:
            sz = given[n].size
            res[n] = flat[off:off + sz].reshape(given[n].shape)
            off += sz

    return (loss, g["x"], *[grads[n] for n in WEIGHT_NAMES], *[delta[n] for n in WEIGHT_NAMES],
            *[new_m[n] for n in WEIGHT_NAMES], *[new_v[n] for n in WEIGHT_NAMES])
```

```python
import functools
import math

import jax
import jax.numpy as jnp
from jax import lax
from jax.experimental import pallas as pl
from jax.experimental.pallas import tpu as pltpu

F32 = jnp.float32
MXU_DTYPE = jnp.bfloat16

D_MODEL = 1024
N_META = 16
LRU_WIDTH = 512
LRU_HEADS = 4
LRU_HEAD_DIM = 128
CONV_WIDTH = 4
LRU_C = 8.0
MLA_HEADS = 8
MLA_NOPE = 64
MLA_ROPE = 32
MLA_V = 64
MLA_Q_RANK = 256
MLA_KV_RANK = 128
RET_HEADS = 4
RET_QK_DIM = 256
RET_V_DIM = 512
D_FF = 4096
ROPE_BASE = 10000.0
DN_ALPHA = 4.0 ** 0.25
EPS = 1e-5
NEG_INF = -1e30
SEQ_BLOCK = 128

ADAM_LR = 0.001
ADAM_B1 = 0.9
ADAM_B2 = 0.999
ADAM_EPS = 1e-08
ADAM_WD = 0.01
ADAM_STEP = 10

PACK_COLS = 1024
N_CHIPS = 4

MESH = pl.DeviceIdType.MESH


def _pick(n, target, align):
    best = None
    for t in range(align, min(n, target) + 1, align):
        if n % t == 0:
            best = t
    return n if best is None else best


def _round_up(n, m):
    return (n + m - 1) // m * m


def _relu2(a):
    r = jnp.maximum(a, 0.0)
    return r * r


def _mm_nn(a, w, act, name, out_dtype=F32):
    M, K = a.shape
    _, N = w.shape
    tm = _pick(M, 1088 if K * a.dtype.itemsize <= 4096 else 544, 8)
    tn = _pick(N, 1024, 128)

    def body(a_ref, w_ref, o_ref):
        av = a_ref[...]
        if act:
            av = _relu2(av.astype(F32))
        o_ref[...] = jnp.dot(av.astype(MXU_DTYPE), w_ref[...].astype(MXU_DTYPE), preferred_element_type=F32).astype(out_dtype)

    return pl.pallas_call(
        body, name=name,
        grid=(M // tm, N // tn),
        in_specs=[pl.BlockSpec((tm, K), lambda i, j: (i, 0)), pl.BlockSpec((K, tn), lambda i, j: (0, j))],
        out_specs=pl.BlockSpec((tm, tn), lambda i, j: (i, j)),
        out_shape=jax.ShapeDtypeStruct((M, N), out_dtype),
        compiler_params=pltpu.CompilerParams(dimension_semantics=("parallel", "arbitrary")),
    )(a, w)


def _mm_nt(g, w, a_src, name, out_dtype=F32):
    M, N = g.shape
    K, _ = w.shape
    tk = N if N * g.dtype.itemsize <= 8192 else _pick(N, 2048, 128)
    nk = N // tk
    tm = _pick(M, 1088 if tk * g.dtype.itemsize <= 4096 else 544, 8)
    tn = _pick(K, 1024, 128)
    has_src = a_src is not None
    assert nk == 1 or out_dtype == F32

    def body(*refs):
        if has_src:
            g_ref, w_ref, s_ref, o_ref = refs
        else:
            g_ref, w_ref, o_ref = refs
        r = lax.dot_general(g_ref[...].astype(MXU_DTYPE), w_ref[...].astype(MXU_DTYPE),
                            (((1,), (1,)), ((), ())), preferred_element_type=F32)
        if has_src:
            r = r * (2.0 * jnp.maximum(s_ref[...].astype(F32), 0.0))
        if nk == 1:
            o_ref[...] = r.astype(out_dtype)
        else:
            k = pl.program_id(2)

            @pl.when(k == 0)
            def _():
                o_ref[...] = r

            @pl.when(k > 0)
            def _():
                o_ref[...] += r

    in_specs = [pl.BlockSpec((tm, tk), lambda i, j, k: (i, k)), pl.BlockSpec((tn, tk), lambda i, j, k: (j, k))]
    args = [g, w]
    if has_src:
        assert nk == 1
        in_specs.append(pl.BlockSpec((tm, tn), lambda i, j, k: (i, j)))
        args.append(a_src)
    return pl.pallas_call(
        body, name=name,
        grid=(M // tm, K // tn, nk),
        in_specs=in_specs,
        out_specs=pl.BlockSpec((tm, tn), lambda i, j, k: (i, j)),
        out_shape=jax.ShapeDtypeStruct((M, K), out_dtype),
        compiler_params=pltpu.CompilerParams(dimension_semantics=("parallel", "parallel", "arbitrary")),
    )(*args)


def _mm_tn(a, g, act, name):
    M, K = a.shape
    _, N = g.shape
    tm, tn, tk = _pick(K, 1024, 128), _pick(N, 1024, 128), _pick(M, 1088, 8)
    nk = M // tk

    def body(a_ref, g_ref, o_ref):
        k = pl.program_id(2)
        av = a_ref[...]
        if act:
            av = _relu2(av.astype(F32))
        r = lax.dot_general(av.astype(MXU_DTYPE), g_ref[...].astype(MXU_DTYPE),
                            (((0,), (0,)), ((), ())), preferred_element_type=F32)

        @pl.when(k == 0)
        def _():
            o_ref[...] = r

        @pl.when(k > 0)
        def _():
            o_ref[...] += r

    return pl.pallas_call(
        body, name=name,
        grid=(K // tm, N // tn, nk),
        in_specs=[pl.BlockSpec((tk, tm), lambda i, j, k: (k, i)), pl.BlockSpec((tk, tn), lambda i, j, k: (k, j))],
        out_specs=pl.BlockSpec((tm, tn), lambda i, j, k: (i, j)),
        out_shape=jax.ShapeDtypeStruct((K, N), F32),
        compiler_params=pltpu.CompilerParams(dimension_semantics=("parallel", "parallel", "arbitrary")),
    )(a, g)


@functools.partial(jax.custom_vjp, nondiff_argnums=(3, 4))
def matmul(a, w, w_grad_slot, act, name):
    return _mm_nn(a, w, act, name + "_fwd")


def _matmul_fwd(a, w, w_grad_slot, act, name):
    return _mm_nn(a, w, act, name + "_fwd"), (a, w)


def _matmul_bwd(act, name, res, g):
    a, w = res
    da = _mm_nt(g, w, a if act else None, name + "_dx")
    dw = _mm_tn(a, g, act, name + "_dw")
    return da, None, dw


matmul.defvjp(_matmul_fwd, _matmul_bwd)


@functools.partial(jax.custom_vjp, nondiff_argnums=(5,))
def mlp(h, w1, w2, w1_grad_slot, w2_grad_slot, name):
    u = _mm_nn(h, w1, False, name + "_w1_fwd", out_dtype=MXU_DTYPE)
    return _mm_nn(u, w2, True, name + "_w2_fwd")


def _mlp_fwd(h, w1, w2, w1_grad_slot, w2_grad_slot, name):
    u = _mm_nn(h, w1, False, name + "_w1_fwd", out_dtype=MXU_DTYPE)
    return _mm_nn(u, w2, True, name + "_w2_fwd"), (h, u, w1, w2)


def _mlp_bwd(name, res, df):
    h, u, w1, w2 = res
    du = _mm_nt(df, w2, u, name + "_w2_dx", out_dtype=MXU_DTYPE)
    dw2 = _mm_tn(u, df, True, name + "_w2_dw")
    dh = _mm_nt(du, w1, None, name + "_w1_dx")
    dw1 = _mm_tn(h, du, False, name + "_w1_dw")
    return dh, None, None, dw1, dw2


mlp.defvjp(_mlp_fwd, _mlp_bwd)


def _ln_stats(z):
    mu = jnp.mean(z, axis=-1, keepdims=True)
    zc = z - mu
    var = jnp.mean(zc * zc, axis=-1, keepdims=True)
    return zc, lax.rsqrt(var + EPS)


def _ln_fwd_call(resid, branch, g, b, name):
    M, D = resid.shape
    tm = _pick(M, 544, 8)

    def body(r_ref, br_ref, g_ref, b_ref, o_ref):
        zc, rstd = _ln_stats(DN_ALPHA * r_ref[...] + br_ref[...])
        o_ref[...] = zc * rstd * g_ref[...] + b_ref[...]

    row = pl.BlockSpec((tm, D), lambda i: (i, 0))
    vec = pl.BlockSpec((1, D), lambda i: (0, 0))
    return pl.pallas_call(
        body, name=name, grid=(M // tm,), in_specs=[row, row, vec, vec], out_specs=row,
        out_shape=jax.ShapeDtypeStruct((M, D), F32),
        compiler_params=pltpu.CompilerParams(dimension_semantics=("parallel",)),
    )(resid, branch, g.reshape(1, D), b.reshape(1, D))


def _ln_bwd_call(resid, branch, g, dy, name):
    M, D = resid.shape
    tm = _pick(M, 544, 8)

    def body(r_ref, br_ref, g_ref, dy_ref, dz_ref, dg_ref, db_ref):
        @pl.when(pl.program_id(0) == 0)
        def _():
            dg_ref[...] = jnp.zeros_like(dg_ref)
            db_ref[...] = jnp.zeros_like(db_ref)

        zc, rstd = _ln_stats(DN_ALPHA * r_ref[...] + br_ref[...])
        xhat = zc * rstd
        dy = dy_ref[...]
        dxh = dy * g_ref[...]
        m1 = jnp.mean(dxh, axis=-1, keepdims=True)
        m2 = jnp.mean(dxh * xhat, axis=-1, keepdims=True)
        dz_ref[...] = rstd * (dxh - m1 - xhat * m2)
        dg_ref[...] += jnp.sum(dy * xhat, axis=0, keepdims=True)
        db_ref[...] += jnp.sum(dy, axis=0, keepdims=True)

    row = pl.BlockSpec((tm, D), lambda i: (i, 0))
    vec = pl.BlockSpec((1, D), lambda i: (0, 0))
    return pl.pallas_call(
        body, name=name, grid=(M // tm,), in_specs=[row, row, vec, row], out_specs=[row, vec, vec],
        out_shape=[jax.ShapeDtypeStruct((M, D), F32), jax.ShapeDtypeStruct((1, D), F32), jax.ShapeDtypeStruct((1, D), F32)],
        compiler_params=pltpu.CompilerParams(dimension_semantics=("arbitrary",)),
    )(resid, branch, g.reshape(1, D), dy)


@functools.partial(jax.custom_vjp, nondiff_argnums=(4,))
def deepnorm(resid, branch, g, b, name):
    return _ln_fwd_call(resid, branch, g, b, name + "_fwd")


def _deepnorm_fwd(resid, branch, g, b, name):
    return _ln_fwd_call(resid, branch, g, b, name + "_fwd"), (resid, branch, g)


def _deepnorm_bwd(name, res, dy):
    resid, branch, g = res
    dz, dg, db = _ln_bwd_call(resid, branch, g, dy, name + "_bwd")
    return DN_ALPHA * dz, dz, dg.reshape(g.shape), db.reshape(g.shape)


deepnorm.defvjp(_deepnorm_fwd, _deepnorm_bwd)


def _rms_fwd_call(x, g, name):
    R, W = x.shape
    tr = _pick(R, 1088, 8)

    def body(x_ref, g_ref, o_ref):
        xv = x_ref[...]
        rstd = lax.rsqrt(jnp.mean(xv * xv, axis=-1, keepdims=True) + EPS)
        o_ref[...] = xv * rstd * g_ref[...]

    row = pl.BlockSpec((tr, W), lambda i: (i, 0))
    vec = pl.BlockSpec((1, W), lambda i: (0, 0))
    return pl.pallas_call(
        body, name=name, grid=(R // tr,), in_specs=[row, vec], out_specs=row,
        out_shape=jax.ShapeDtypeStruct((R, W), F32),
        compiler_params=pltpu.CompilerParams(dimension_semantics=("parallel",)),
    )(x, g.reshape(1, W))


def _rms_bwd_call(x, g, dy, name):
    R, W = x.shape
    tr = _pick(R, 1088, 8)

    def body(x_ref, g_ref, dy_ref, dx_ref, dg_ref):
        @pl.when(pl.program_id(0) == 0)
        def _():
            dg_ref[...] = jnp.zeros_like(dg_ref)

        xv = x_ref[...]
        rstd = lax.rsqrt(jnp.mean(xv * xv, axis=-1, keepdims=True) + EPS)
        xhat = xv * rstd
        dy = dy_ref[...]
        dxh = dy * g_ref[...]
        dx_ref[...] = rstd * (dxh - xhat * jnp.mean(dxh * xhat, axis=-1, keepdims=True))
        dg_ref[...] += jnp.sum(dy * xhat, axis=0, keepdims=True)

    row = pl.BlockSpec((tr, W), lambda i: (i, 0))
    vec = pl.BlockSpec((1, W), lambda i: (0, 0))
    return pl.pallas_call(
        body, name=name, grid=(R // tr,), in_specs=[row, vec, row], out_specs=[row, vec],
        out_shape=[jax.ShapeDtypeStruct((R, W), F32), jax.ShapeDtypeStruct((1, W), F32)],
        compiler_params=pltpu.CompilerParams(dimension_semantics=("arbitrary",)),
    )(x, g.reshape(1, W), dy)


@functools.partial(jax.custom_vjp, nondiff_argnums=(2,))
def rmsnorm(x, g, name):
    return _rms_fwd_call(x, g, name + "_fwd")


def _rmsnorm_fwd(x, g, name):
    return _rms_fwd_call(x, g, name + "_fwd"), (x, g)


def _rmsnorm_bwd(name, res, dy):
    x, g = res
    dx, dg = _rms_bwd_call(x, g, dy, name + "_bwd")
    return dx, dg.reshape(g.shape)


rmsnorm.defvjp(_rmsnorm_fwd, _rmsnorm_bwd)


def _loss_call(y, tgt, name):
    R, D = y.shape
    tr = _pick(R, 512, 8)

    def body(y_ref, t_ref, dy_ref, acc_ref):
        @pl.when(pl.program_id(0) == 0)
        def _():
            acc_ref[...] = jnp.zeros_like(acc_ref)

        e = y_ref[...] - t_ref[...]
        dy_ref[...] = e * (1.0 / D)
        acc_ref[...] += jnp.sum(jnp.sum(e * e, axis=-1, keepdims=True), axis=0, keepdims=True) * (0.5 / D)

    row = pl.BlockSpec((tr, D), lambda i: (i, 0))
    one = pl.BlockSpec((1, 1), lambda i: (0, 0))
    return pl.pallas_call(
        body, name=name, grid=(R // tr,), in_specs=[row, row], out_specs=[row, one],
        out_shape=[jax.ShapeDtypeStruct((R, D), F32), jax.ShapeDtypeStruct((1, 1), F32)],
        compiler_params=pltpu.CompilerParams(dimension_semantics=("arbitrary",)),
    )(y, tgt)


@jax.custom_vjp
def loss_head(y, tgt):
    return _loss_call(y, tgt, "loss_head")[1][0, 0]


def _loss_head_fwd(y, tgt):
    dy, acc = _loss_call(y, tgt, "loss_head")
    return acc[0, 0], dy


def _loss_head_bwd(dy, ct):
    return ct * dy, None


loss_head.defvjp(_loss_head_fwd, _loss_head_bwd)


_GELU_C = math.sqrt(2.0 / math.pi)


def _gelu_parts(x):
    x2 = x * x
    t = jnp.tanh(_GELU_C * (x + 0.044715 * x * x2))
    gelu = 0.5 * x * (1.0 + t)
    dgelu = 0.5 * (1.0 + t) + 0.5 * x * (1.0 - t * t) * (_GELU_C * (1.0 + 3.0 * 0.044715 * x2))
    return gelu, dgelu


def _sigmoid(x):
    return 1.0 / (1.0 + jnp.exp(-x))


def _scan8(a, b, carry, reverse):
    row = lax.broadcasted_iota(jnp.int32, a.shape, 0)
    for s in (1, 2, 4):
        shift = 8 - s if reverse else s
        keep = (row < 8 - s) if reverse else (row >= s)
        b = jnp.where(keep, a * pltpu.roll(b, shift, 0) + b, b)
        a = jnp.where(keep, a * pltpu.roll(a, shift, 0), a)
    return a * carry + b


def _lru_pre(prec_ref, prev_ref, first, cw_ref, cb_ref, wa_ref, ba_ref, wx_ref, bx_ref, sp_ref):
    tc = prec_ref.shape[0]
    prev = jnp.where(first, 0.0, prev_ref[...])
    ext = jnp.concatenate([prev, prec_ref[...]], axis=0)
    cw = cw_ref[...]
    taps = [ext[8:] if k == CONV_WIDTH - 1 else pltpu.roll(ext, CONV_WIDTH - 1 - k, 0)[8:] for k in range(CONV_WIDTH)]
    xc = cb_ref[...] + sum(cw[k:k + 1, :] * taps[k] for k in range(CONV_WIDTH))
    ga, gx = [], []
    for h in range(LRU_HEADS):
        xh = xc[:, h * LRU_HEAD_DIM:(h + 1) * LRU_HEAD_DIM].astype(MXU_DTYPE)
        ga.append(jnp.dot(xh, wa_ref[h].astype(MXU_DTYPE), preferred_element_type=F32))
        gx.append(jnp.dot(xh, wx_ref[h].astype(MXU_DTYPE), preferred_element_type=F32))
    r = _sigmoid(jnp.concatenate(ga, axis=1) + ba_ref[...])
    i = _sigmoid(jnp.concatenate(gx, axis=1) + bx_ref[...])
    log_a = -LRU_C * r * sp_ref[...]
    a = jnp.exp(log_a)
    a2 = a * a
    mult = jnp.sqrt(-jnp.tanh(log_a) * (a2 + 1.0))
    return taps, xc, r, i, a, a2, mult


def _lru_fwd_call(pg, prec, cw, cb, wa, ba, wx, bx, sp):
    B, Tp, W = prec.shape
    tc = SEQ_BLOCK
    nc = Tp // tc

    def body(pg_ref, prec_ref, prev_ref, cw_ref, cb_ref, wa_ref, ba_ref, wx_ref, bx_ref, sp_ref, y_ref, h_ref, carry_ref):
        first = pl.program_id(1) == 0

        @pl.when(first)
        def _():
            carry_ref[...] = jnp.zeros_like(carry_ref)

        _, xc, r, i, a, a2, mult = _lru_pre(prec_ref, prev_ref, first, cw_ref, cb_ref, wa_ref, ba_ref, wx_ref, bx_ref, sp_ref)
        b = mult * (i * xc)
        carry = carry_ref[0:1, :]
        for t in range(tc // 8):
            h = _scan8(a[8 * t:8 * t + 8], b[8 * t:8 * t + 8], carry, False)
            h_ref[8 * t:8 * t + 8, :] = h
            carry = h[7:8, :]
        carry_ref[...] = jnp.broadcast_to(carry, carry_ref.shape)
        y_ref[...] = h_ref[...] * _gelu_parts(pg_ref[...])[0]

    cur = pl.BlockSpec((None, tc, W), lambda b, j: (b, j, 0))
    prev = pl.BlockSpec((None, 8, W), lambda b, j: (b, jnp.maximum(j * (tc // 8) - 1, 0), 0))
    vec = pl.BlockSpec((1, W), lambda b, j: (0, 0))
    cws = pl.BlockSpec((CONV_WIDTH, W), lambda b, j: (0, 0))
    wsp = pl.BlockSpec((LRU_HEADS, LRU_HEAD_DIM, LRU_HEAD_DIM), lambda b, j: (0, 0, 0))
    return pl.pallas_call(
        body, name="lru_fwd", grid=(B, nc),
        in_specs=[cur, cur, prev, cws, vec, wsp, vec, wsp, vec, vec],
        out_specs=[cur, cur],
        out_shape=[jax.ShapeDtypeStruct((B, Tp, W), F32), jax.ShapeDtypeStruct((B, Tp, W), F32)],
        scratch_shapes=[pltpu.VMEM((8, W), F32)],
        compiler_params=pltpu.CompilerParams(dimension_semantics=("arbitrary", "arbitrary")),
    )(pg, prec, prec, cw, cb, wa, ba, wx, bx, sp)


def _lru_bwd_call(pg, prec, hseq, dy, cw, cb, wa, ba, wx, bx, sp):
    B, Tp, W = prec.shape
    tc = SEQ_BLOCK
    nc = Tp // tc
    HD = LRU_HEAD_DIM

    def body(pg_ref, prec_ref, prev_ref, h_ref, hprev_ref, dy_ref, cw_ref, cb_ref, wa_ref, ba_ref, wx_ref, bx_ref, sp_ref,
             dpg_ref, dprec_ref, dcw_ref, dcb_ref, dwa_ref, dba_ref, dwx_ref, dbx_ref, dsp_ref,
             gcar_ref, anext_ref, halo_ref, g_ref):
        j = pl.program_id(1)
        first = j == nc - 1
        last = j == 0

        @pl.when(jnp.logical_and(pl.program_id(0) == 0, last))
        def _():
            for ref in (dcw_ref, dcb_ref, dwa_ref, dba_ref, dwx_ref, dbx_ref, dsp_ref):
                ref[...] = jnp.zeros_like(ref)

        @pl.when(last)
        def _():
            gcar_ref[...] = jnp.zeros_like(gcar_ref)
            anext_ref[...] = jnp.zeros_like(anext_ref)
            halo_ref[...] = jnp.zeros_like(halo_ref)

        taps, xc, r, i, a, a2, mult = _lru_pre(prec_ref, prev_ref, first, cw_ref, cb_ref, wa_ref, ba_ref, wx_ref, bx_ref, sp_ref)
        row = lax.broadcasted_iota(jnp.int32, (tc, W), 0)
        gelu, dgelu = _gelu_parts(pg_ref[...])
        dy = dy_ref[...]
        hcur = h_ref[...]
        dpg_ref[...] = dy * hcur * dgelu
        dh = dy * gelu
        a_next = jnp.where(row == tc - 1, anext_ref[0:1, :], pltpu.roll(a, tc - 1, 0))
        carry = gcar_ref[0:1, :]
        for t in reversed(range(tc // 8)):
            g = _scan8(a_next[8 * t:8 * t + 8], dh[8 * t:8 * t + 8], carry, True)
            g_ref[8 * t:8 * t + 8, :] = g
            carry = g[0:1, :]
        gcar_ref[...] = jnp.broadcast_to(carry, gcar_ref.shape)
        anext_ref[...] = jnp.broadcast_to(a[0:1, :], anext_ref.shape)
        G = g_ref[...]
        h_before = jnp.where(first, 0.0, hprev_ref[7:8, :])
        hprev = jnp.where(row == 0, h_before, pltpu.roll(hcur, 1, 0))
        d_a = G * hprev
        gx_ = G * xc
        d_mult = gx_ * i
        d_i = gx_ * mult
        dxc = G * (mult * i)
        d_la = d_a * a - d_mult * (a2 / mult)
        sp = sp_ref[...]
        d_r = d_la * (-LRU_C * sp)
        dsp_ref[...] += jnp.sum(d_la * (-LRU_C * r), axis=0, keepdims=True)
        dga = d_r * r * (1.0 - r)
        dgx = d_i * i * (1.0 - i)
        dba_ref[...] += jnp.sum(dga, axis=0, keepdims=True)
        dbx_ref[...] += jnp.sum(dgx, axis=0, keepdims=True)
        back = []
        for h in range(LRU_HEADS):
            sl = slice(h * HD, (h + 1) * HD)
            xh = xc[:, sl].astype(MXU_DTYPE)
            ah = dga[:, sl].astype(MXU_DTYPE)
            bh = dgx[:, sl].astype(MXU_DTYPE)
            tn = (((0,), (0,)), ((), ()))
            nt = (((1,), (1,)), ((), ()))
            dwa_ref[h] += lax.dot_general(xh, ah, tn, preferred_element_type=F32)
            dwx_ref[h] += lax.dot_general(xh, bh, tn, preferred_element_type=F32)
            back.append(lax.dot_general(ah, wa_ref[h].astype(MXU_DTYPE), nt, preferred_element_type=F32)
                        + lax.dot_general(bh, wx_ref[h].astype(MXU_DTYPE), nt, preferred_element_type=F32))
        dxc = dxc + jnp.concatenate(back, axis=1)
        dcb_ref[...] += jnp.sum(dxc, axis=0, keepdims=True)
        for k in range(CONV_WIDTH):
            dcw_ref[k:k + 1, :] += jnp.sum(dxc * taps[k], axis=0, keepdims=True)
        ext = jnp.concatenate([dxc, halo_ref[...]], axis=0)
        cw = cw_ref[...]
        acc = cw[CONV_WIDTH - 1:CONV_WIDTH, :] * dxc
        for k in range(CONV_WIDTH - 1):
            s = CONV_WIDTH - 1 - k
            acc = acc + cw[k:k + 1, :] * pltpu.roll(ext, tc + 8 - s, 0)[:tc]
        dprec_ref[...] = acc
        halo_ref[...] = dxc[0:8, :]

    rev = lambda j: nc - 1 - j
    cur = pl.BlockSpec((None, tc, W), lambda b, j: (b, rev(j), 0))
    prev = pl.BlockSpec((None, 8, W), lambda b, j: (b, jnp.maximum(rev(j) * (tc // 8) - 1, 0), 0))
    vec = pl.BlockSpec((1, W), lambda b, j: (0, 0))
    cws = pl.BlockSpec((CONV_WIDTH, W), lambda b, j: (0, 0))
    wsp = pl.BlockSpec((LRU_HEADS, HD, HD), lambda b, j: (0, 0, 0))
    seq = jax.ShapeDtypeStruct((B, Tp, W), F32)
    vs = jax.ShapeDtypeStruct((1, W), F32)
    ws = jax.ShapeDtypeStruct((LRU_HEADS, HD, HD), F32)
    return pl.pallas_call(
        body, name="lru_bwd", grid=(B, nc),
        in_specs=[cur, cur, prev, cur, prev, cur, cws, vec, wsp, vec, wsp, vec, vec],
        out_specs=[cur, cur, cws, vec, wsp, vec, wsp, vec, vec],
        out_shape=[seq, seq, jax.ShapeDtypeStruct((CONV_WIDTH, W), F32), vs, ws, vs, ws, vs, vs],
        scratch_shapes=[pltpu.VMEM((8, W), F32), pltpu.VMEM((8, W), F32), pltpu.VMEM((8, W), F32), pltpu.VMEM((tc, W), F32)],
        compiler_params=pltpu.CompilerParams(dimension_semantics=("arbitrary", "arbitrary")),
    )(pg, prec, prec, hseq, hseq, dy, cw, cb, wa, ba, wx, bx, sp)


@jax.custom_vjp
def rglru(pg, prec, cw, cb, wa, ba, wx, bx, sp):
    return _lru_fwd_call(pg, prec, cw, cb, wa, ba, wx, bx, sp)[0]


def _rglru_fwd(pg, prec, cw, cb, wa, ba, wx, bx, sp):
    y, hseq = _lru_fwd_call(pg, prec, cw, cb, wa, ba, wx, bx, sp)
    return y, (pg, prec, hseq, cw, cb, wa, ba, wx, bx, sp)


def _rglru_bwd(res, dy):
    pg, prec, hseq, cw, cb, wa, ba, wx, bx, sp = res
    return tuple(_lru_bwd_call(pg, prec, hseq, dy, cw, cb, wa, ba, wx, bx, sp))


rglru.defvjp(_rglru_fwd, _rglru_bwd)


_NT = (((1,), (1,)), ((), ()))
_TN = (((0,), (0,)), ((), ()))


def _causal_keep(qi, L):
    row = lax.broadcasted_iota(jnp.int32, (SEQ_BLOCK, L), 0) + qi * SEQ_BLOCK
    col = lax.broadcasted_iota(jnp.int32, (SEQ_BLOCK, L), 1)
    return col <= row


def _attn_fwd_call(q, k, v):
    B, H, Tp, dk = q.shape
    dv = v.shape[-1]
    nq = Tp // SEQ_BLOCK
    scale = dk ** -0.5

    def body(q_ref, k_ref, v_ref, o_ref, lse_ref):
        for qi in range(nq):
            L = (qi + 1) * SEQ_BLOCK
            blk = slice(qi * SEQ_BLOCK, L)
            s = lax.dot_general(q_ref[blk, :], k_ref[0:L, :], _NT, preferred_element_type=F32) * scale
            s = jnp.where(_causal_keep(qi, L), s, NEG_INF)
            m = jnp.max(s, axis=-1, keepdims=True)
            p = jnp.exp(s - m)
            l = jnp.sum(p, axis=-1, keepdims=True)
            o = jnp.dot(p.astype(MXU_DTYPE), v_ref[0:L, :], preferred_element_type=F32)
            o_ref[blk, :] = o / l
            lse_ref[blk, :] = m + jnp.log(l)

    def spec(d):
        return pl.BlockSpec((None, None, Tp, d), lambda b, h: (b, h, 0, 0))

    return pl.pallas_call(
        body, name="mla_attn_fwd", grid=(B, H), in_specs=[spec(dk), spec(dk), spec(dv)], out_specs=[spec(dv), spec(1)],
        out_shape=[jax.ShapeDtypeStruct((B, H, Tp, dv), F32), jax.ShapeDtypeStruct((B, H, Tp, 1), F32)],
        compiler_params=pltpu.CompilerParams(dimension_semantics=("parallel", "parallel")),
    )(q, k, v)


def _attn_bwd_call(q, k, v, o, lse, do):
    B, H, Tp, dk = q.shape
    dv = v.shape[-1]
    nq = Tp // SEQ_BLOCK
    scale = dk ** -0.5

    def body(q_ref, k_ref, v_ref, o_ref, lse_ref, do_ref, dq_ref, dk_ref, dv_ref):
        dk_ref[...] = jnp.zeros_like(dk_ref)
        dv_ref[...] = jnp.zeros_like(dv_ref)
        for qi in range(nq):
            L = (qi + 1) * SEQ_BLOCK
            blk = slice(qi * SEQ_BLOCK, L)
            qb = q_ref[blk, :]
            do = do_ref[blk, :]
            delta = jnp.sum(do * o_ref[blk, :], axis=-1, keepdims=True)
            s = lax.dot_general(qb, k_ref[0:L, :], _NT, preferred_element_type=F32) * scale
            s = jnp.where(_causal_keep(qi, L), s, NEG_INF)
            p = jnp.exp(s - lse_ref[blk, :])
            dob = do.astype(MXU_DTYPE)
            dv_ref[0:L, :] += lax.dot_general(p.astype(MXU_DTYPE), dob, _TN, preferred_element_type=F32)
            dp = lax.dot_general(dob, v_ref[0:L, :], _NT, preferred_element_type=F32)
            ds = (p * (dp - delta) * scale).astype(MXU_DTYPE)
            dq_ref[blk, :] = jnp.dot(ds, k_ref[0:L, :], preferred_element_type=F32)
            dk_ref[0:L, :] += lax.dot_general(ds, qb, _TN, preferred_element_type=F32)

    def spec(d):
        return pl.BlockSpec((None, None, Tp, d), lambda b, h: (b, h, 0, 0))

    return pl.pallas_call(
        body, name="mla_attn_bwd", grid=(B, H),
        in_specs=[spec(dk), spec(dk), spec(dv), spec(dv), spec(1), spec(dv)], out_specs=[spec(dk), spec(dk), spec(dv)],
        out_shape=[jax.ShapeDtypeStruct((B, H, Tp, dk), F32), jax.ShapeDtypeStruct((B, H, Tp, dk), F32),
                   jax.ShapeDtypeStruct((B, H, Tp, dv), F32)],
        compiler_params=pltpu.CompilerParams(dimension_semantics=("parallel", "parallel")),
    )(q, k, v, o, lse, do)


@jax.custom_vjp
def attention(q, k, v):
    return _attn_fwd_call(q.astype(MXU_DTYPE), k.astype(MXU_DTYPE), v.astype(MXU_DTYPE))[0]


def _attention_fwd(q, k, v):
    qb, kb, vb = q.astype(MXU_DTYPE), k.astype(MXU_DTYPE), v.astype(MXU_DTYPE)
    o, lse = _attn_fwd_call(qb, kb, vb)
    return o, (qb, kb, vb, o, lse)


def _attention_bwd(res, do):
    return tuple(_attn_bwd_call(*res, do))


attention.defvjp(_attention_fwd, _attention_bwd)


def _decay(qi, L, lg):
    row = lax.broadcasted_iota(jnp.int32, (SEQ_BLOCK, L), 0) + qi * SEQ_BLOCK
    col = lax.broadcasted_iota(jnp.int32, (SEQ_BLOCK, L), 1)
    diff = row - col
    return jnp.where(diff >= 0, jnp.exp(lg * jnp.maximum(diff, 0).astype(F32)), 0.0)


def _ret_fwd_call(q, k, v, lg):
    B, H, Tp, dk = q.shape
    dv = v.shape[-1]
    nq = Tp // SEQ_BLOCK

    def body(q_ref, k_ref, v_ref, lg_ref, o_ref):
        lg_ = lg_ref[...]
        for qi in range(nq):
            L = (qi + 1) * SEQ_BLOCK
            blk = slice(qi * SEQ_BLOCK, L)
            s = lax.dot_general(q_ref[blk, :], k_ref[0:L, :], _NT, preferred_element_type=F32) * _decay(qi, L, lg_)
            o_ref[blk, :] = jnp.dot(s.astype(MXU_DTYPE), v_ref[0:L, :], preferred_element_type=F32)

    def spec(d):
        return pl.BlockSpec((None, None, Tp, d), lambda b, h: (b, h, 0, 0))

    return pl.pallas_call(
        body, name="retention_fwd", grid=(B, H),
        in_specs=[spec(dk), spec(dk), spec(dv), pl.BlockSpec((None, 1, 1), lambda b, h: (h, 0, 0))], out_specs=spec(dv),
        out_shape=jax.ShapeDtypeStruct((B, H, Tp, dv), F32),
        compiler_params=pltpu.CompilerParams(dimension_semantics=("parallel", "parallel")),
    )(q, k, v, lg)


def _ret_bwd_call(q, k, v, lg, do):
    B, H, Tp, dk = q.shape
    dv = v.shape[-1]
    nq = Tp // SEQ_BLOCK

    def body(q_ref, k_ref, v_ref, lg_ref, do_ref, dq_ref, dk_ref, dv_ref):
        lg_ = lg_ref[...]
        dk_ref[...] = jnp.zeros_like(dk_ref)
        dv_ref[...] = jnp.zeros_like(dv_ref)
        for qi in range(nq):
            L = (qi + 1) * SEQ_BLOCK
            blk = slice(qi * SEQ_BLOCK, L)
            qb = q_ref[blk, :]
            dob = do_ref[blk, :].astype(MXU_DTYPE)
            dec = _decay(qi, L, lg_)
            s = (lax.dot_general(qb, k_ref[0:L, :], _NT, preferred_element_type=F32) * dec).astype(MXU_DTYPE)
            dv_ref[0:L, :] += lax.dot_general(s, dob, _TN, preferred_element_type=F32)
            ds = (lax.dot_general(dob, v_ref[0:L, :], _NT, preferred_element_type=F32) * dec).astype(MXU_DTYPE)
            dq_ref[blk, :] = jnp.dot(ds, k_ref[0:L, :], preferred_element_type=F32)
            dk_ref[0:L, :] += lax.dot_general(ds, qb, _TN, preferred_element_type=F32)

    def spec(d):
        return pl.BlockSpec((None, None, Tp, d), lambda b, h: (b, h, 0, 0))

    return pl.pallas_call(
        body, name="retention_bwd", grid=(B, H),
        in_specs=[spec(dk), spec(dk), spec(dv), pl.BlockSpec((None, 1, 1), lambda b, h: (h, 0, 0)), spec(dv)],
        out_specs=[spec(dk), spec(dk), spec(dv)],
        out_shape=[jax.ShapeDtypeStruct((B, H, Tp, dk), F32), jax.ShapeDtypeStruct((B, H, Tp, dk), F32),
                   jax.ShapeDtypeStruct((B, H, Tp, dv), F32)],
        compiler_params=pltpu.CompilerParams(dimension_semantics=("parallel", "parallel")),
    )(q, k, v, lg, do)


def _log_gamma():
    return jnp.log(1.0 - 2.0 ** (-5.0 - jnp.arange(RET_HEADS, dtype=F32))).reshape(RET_HEADS, 1, 1)


@jax.custom_vjp
def retention(q, k, v):
    return _ret_fwd_call(q.astype(MXU_DTYPE), k.astype(MXU_DTYPE), v.astype(MXU_DTYPE), _log_gamma())


def _retention_fwd(q, k, v):
    qb, kb, vb = q.astype(MXU_DTYPE), k.astype(MXU_DTYPE), v.astype(MXU_DTYPE)
    return _ret_fwd_call(qb, kb, vb, _log_gamma()), (qb, kb, vb)


def _retention_bwd(res, do):
    qb, kb, vb = res
    return tuple(_ret_bwd_call(qb, kb, vb, _log_gamma(), do))


retention.defvjp(_retention_fwd, _retention_bwd)


def _rope(x, pos):
    half = x.shape[-1] // 2
    inv = ROPE_BASE ** (-jnp.arange(half, dtype=F32) / half)
    ang = pos.astype(F32)[:, None] * inv[None, :]
    cos = jnp.cos(ang)[None, :, None, :]
    sin = jnp.sin(ang)[None, :, None, :]
    x1, x2 = x[..., :half], x[..., half:]
    return jnp.concatenate([x1 * cos - x2 * sin, x1 * sin + x2 * cos], axis=-1)


BIG_MATRICES = ("ev_w_in", "ev_w_uq", "ev_w_ukv", "ev_w_out", "od_w_in", "od_w_out",
                "mlp_w1_0", "mlp_w1_1", "mlp_w2_0", "mlp_w2_1")


def _local_loss(diff, wfull, tgt):
    x = diff["x"]
    B, S, D = x.shape
    T = S + N_META
    Tp = _round_up(T, SEQ_BLOCK)
    M = B * Tp
    pos = jnp.arange(Tp, dtype=jnp.int32)

    def mm(a, name, act=False):
        return matmul(a, wfull[name], diff[name], act, name)

    meta = jnp.broadcast_to(diff["meta_tokens"][None], (B, N_META, D))
    h = jnp.concatenate([meta, x, jnp.zeros((B, Tp - T, D), F32)], axis=1).reshape(M, D)

    p = mm(h, "ev_w_in")
    p_gate, p_rec, p_q, p_kv, p_kpe = lax.split(p, (LRU_WIDTH, LRU_WIDTH, MLA_Q_RANK, MLA_KV_RANK, MLA_ROPE), axis=1)
    sp = jax.nn.softplus(-diff["ev_lru_lambda"]).reshape(1, LRU_WIDTH)
    y_rec = rglru(p_gate.reshape(B, Tp, LRU_WIDTH), p_rec.reshape(B, Tp, LRU_WIDTH),
                  diff["ev_conv_w"].reshape(CONV_WIDTH, LRU_WIDTH), diff["ev_conv_b"].reshape(1, LRU_WIDTH),
                  diff["ev_w_rg_a"].reshape(LRU_HEADS, LRU_HEAD_DIM, LRU_HEAD_DIM), diff["ev_b_rg_a"].reshape(1, LRU_WIDTH),
                  diff["ev_w_rg_x"].reshape(LRU_HEADS, LRU_HEAD_DIM, LRU_HEAD_DIM), diff["ev_b_rg_x"].reshape(1, LRU_WIDTH),
                  sp).reshape(M, LRU_WIDTH)
    q = mm(rmsnorm(p_q, diff["ev_q_norm_g"].reshape(-1), "q_norm"), "ev_w_uq").reshape(B, Tp, MLA_HEADS, MLA_NOPE + MLA_ROPE)
    q = jnp.concatenate([q[..., :MLA_NOPE], _rope(q[..., MLA_NOPE:], pos)], axis=-1)
    kv = mm(rmsnorm(p_kv, diff["ev_kv_norm_g"].reshape(-1), "kv_norm"), "ev_w_ukv").reshape(B, Tp, MLA_HEADS, MLA_NOPE + MLA_V)
    k_pe = _rope(p_kpe.reshape(B, Tp, 1, MLA_ROPE), pos)
    k = jnp.concatenate([kv[..., :MLA_NOPE], jnp.broadcast_to(k_pe, (B, Tp, MLA_HEADS, MLA_ROPE))], axis=-1)
    v = kv[..., MLA_NOPE:]
    o = attention(q.transpose(0, 2, 1, 3), k.transpose(0, 2, 1, 3), v.transpose(0, 2, 1, 3))
    y_att = o.transpose(0, 2, 1, 3).reshape(M, MLA_HEADS * MLA_V)
    mix = mm(jnp.concatenate([y_rec, y_att], axis=-1), "ev_w_out")
    h = deepnorm(h, mix, diff["ln_mix_g"][0], diff["ln_mix_b"][0], "ln_mix0")
    f = mlp(h, wfull["mlp_w1_0"], wfull["mlp_w2_0"], diff["mlp_w1_0"], diff["mlp_w2_0"], "mlp0")
    h = deepnorm(h, f, diff["ln_mlp_g"][0], diff["ln_mlp_b"][0], "ln_mlp0")

    p = mm(h, "od_w_in")
    nqk = RET_HEADS * RET_QK_DIM
    nv = RET_HEADS * RET_V_DIM
    q, k, v, gate = lax.split(p, (nqk, nqk, nv, nv), axis=1)
    q = _rope(q.reshape(B, Tp, RET_HEADS, RET_QK_DIM), pos)
    k = _rope(k.reshape(B, Tp, RET_HEADS, RET_QK_DIM), pos) * (RET_QK_DIM ** -0.5)
    v = v.reshape(B, Tp, RET_HEADS, RET_V_DIM)
    o = retention(q.transpose(0, 2, 1, 3), k.transpose(0, 2, 1, 3), v.transpose(0, 2, 1, 3))
    o = rmsnorm(o.reshape(B * RET_HEADS * Tp, RET_V_DIM), jnp.ones((RET_V_DIM,), F32), "ret_norm")
    y = o.reshape(B, RET_HEADS, Tp, RET_V_DIM).transpose(0, 2, 1, 3).reshape(M, nv)
    mix = mm(jax.nn.silu(gate) * y, "od_w_out")
    h = deepnorm(h, mix, diff["ln_mix_g"][1], diff["ln_mix_b"][1], "ln_mix1")
    f = mlp(h, wfull["mlp_w1_1"], wfull["mlp_w2_1"], diff["mlp_w1_1"], diff["mlp_w2_1"], "mlp1")
    h = deepnorm(h, f, diff["ln_mlp_g"][1], diff["ln_mlp_b"][1], "ln_mlp1")

    y = h.reshape(B, Tp, D)[:, N_META:T].reshape(B * S, D)
    return loss_head(y, tgt.reshape(B * S, D))


_HBM = pl.BlockSpec(memory_space=pltpu.HBM)


def _place():
    return lax.axis_index("x"), lax.axis_index("y"), lax.axis_index("c")


def _other_chips(x, y):
    return [(1 - x, y), (x, 1 - y), (1 - x, 1 - y)]


def _chunks(rows, sublanes, most):
    for q in range(most, 0, -1):
        if rows % (q * sublanes) == 0:
            return q
    return 1


def _sublanes(dtype):
    return 8 * 4 // jnp.dtype(dtype).itemsize


def _allgather_chips(buf, name):
    R, C = buf.shape
    Rh = R // 2
    Q = _chunks(Rh, _sublanes(buf.dtype), 4)
    ch = Rh // Q

    def body(x_ref, out_ref, send_sems, recv_sems, local_sem):
        x, y, c = _place()
        sibling = (x, y, 1 - c)
        chips = _other_chips(x, y)

        def piece(cx, cy, hc, q):
            return out_ref.at[2 * cx + cy, pl.ds(hc * Rh + q * ch, ch), :]

        def copy(k, src, dst, to):
            return pltpu.make_async_remote_copy(src_ref=src, dst_ref=dst, send_sem=send_sems.at[k], recv_sem=recv_sems.at[k],
                                                device_id=to, device_id_type=MESH)

        mine = pltpu.make_async_copy(x_ref, out_ref.at[2 * x + y], local_sem)
        mine.start()
        first = [copy(j * Q + q, x_ref.at[pl.ds(c * Rh + q * ch, ch), :], piece(x, y, c, q), (*chip, c))
                 for q in range(Q) for j, chip in enumerate(chips)]
        for cp in first:
            cp.start()
        passed = []
        for q in range(Q):
            for j, chip in enumerate(chips):
                landed = piece(*chip, c, q)
                copy(j * Q + q, landed, landed, sibling).wait_recv()
                fwd = copy(3 * Q + j * Q + q, landed, landed, sibling)
                fwd.start()
                passed.append(fwd)
        for q in range(Q):
            for j, chip in enumerate(chips):
                theirs = piece(*chip, 1 - c, q)
                copy(3 * Q + j * Q + q, theirs, theirs, sibling).wait_recv()
        for cp in first + passed:
            cp.wait_send()
        mine.wait()

    return pl.pallas_call(
        body, name=name, in_specs=[_HBM], out_specs=_HBM,
        out_shape=jax.ShapeDtypeStruct((N_CHIPS, R, C), buf.dtype),
        scratch_shapes=[pltpu.SemaphoreType.DMA((6 * Q,)), pltpu.SemaphoreType.DMA((6 * Q,)), pltpu.SemaphoreType.DMA],
    )(buf)


def _sibling_exchange(p, name):
    _, R, C = p.shape
    Q = _chunks(R, _sublanes(p.dtype), 20)
    ch = R // Q

    def body(p_ref, recv_ref, send_sems, recv_sems):
        x, y, c = _place()
        copies = [pltpu.make_async_remote_copy(src_ref=p_ref.at[1 - c, pl.ds(q * ch, ch), :], dst_ref=recv_ref.at[pl.ds(q * ch, ch), :],
                                               send_sem=send_sems.at[q], recv_sem=recv_sems.at[q],
                                               device_id=(x, y, 1 - c), device_id_type=MESH) for q in range(Q)]
        for cp in copies:
            cp.start()
        for cp in copies:
            cp.wait()

    return pl.pallas_call(
        body, name=name, in_specs=[_HBM], out_specs=_HBM, out_shape=jax.ShapeDtypeStruct((R, C), p.dtype),
        scratch_shapes=[pltpu.SemaphoreType.DMA((Q,)), pltpu.SemaphoreType.DMA((Q,))],
    )(p)


def _chip_scatter(s, name):
    n, R, C = s.shape
    Q = _chunks(R, _sublanes(s.dtype), 5)
    ch = R // Q

    def body(s_ref, t_ref, send_sems, recv_sems, local_sem):
        x, y, c = _place()
        my = 2 * x + y
        loc = pltpu.make_async_copy(s_ref.at[my], t_ref.at[my], local_sem)
        loc.start()
        copies = [pltpu.make_async_remote_copy(src_ref=s_ref.at[2 * cx + cy, pl.ds(q * ch, ch), :],
                                               dst_ref=t_ref.at[my, pl.ds(q * ch, ch), :], send_sem=send_sems.at[j * Q + q],
                                               recv_sem=recv_sems.at[j * Q + q], device_id=(cx, cy, c), device_id_type=MESH)
                  for q in range(Q) for j, (cx, cy) in enumerate(_other_chips(x, y))]
        for cp in copies:
            cp.start()
        for cp in copies:
            cp.wait()
        loc.wait()

    return pl.pallas_call(
        body, name=name, in_specs=[_HBM], out_specs=_HBM, out_shape=jax.ShapeDtypeStruct(s.shape, s.dtype),
        scratch_shapes=[pltpu.SemaphoreType.DMA((3 * Q,)), pltpu.SemaphoreType.DMA((3 * Q,)), pltpu.SemaphoreType.DMA],
    )(s)


def _sibling_gather(f, name):
    R, C = f.shape
    Q = _chunks(R, _sublanes(f.dtype), 10)
    ch = R // Q

    def body(f_ref, out_ref, send_sems, recv_sems, local_sem):
        x, y, c = _place()
        loc = pltpu.make_async_copy(f_ref, out_ref.at[c], local_sem)
        loc.start()
        copies = [pltpu.make_async_remote_copy(src_ref=f_ref.at[pl.ds(q * ch, ch), :], dst_ref=out_ref.at[c, pl.ds(q * ch, ch), :],
                                               send_sem=send_sems.at[q], recv_sem=recv_sems.at[q],
                                               device_id=(x, y, 1 - c), device_id_type=MESH) for q in range(Q)]
        for cp in copies:
            cp.start()
        for cp in copies:
            cp.wait()
        loc.wait()

    return pl.pallas_call(
        body, name=name, in_specs=[_HBM], out_specs=_HBM, out_shape=jax.ShapeDtypeStruct((2, R, C), f.dtype),
        scratch_shapes=[pltpu.SemaphoreType.DMA((Q,)), pltpu.SemaphoreType.DMA((Q,)), pltpu.SemaphoreType.DMA],
    )(f)


def _add_own_half(p, got, name):
    _, R, C = p.shape
    tr = _pick(R, 512, 8)

    def body(c_ref, p_ref, g_ref, o_ref):
        o_ref[...] = p_ref[...] + g_ref[...]

    grid_spec = pltpu.PrefetchScalarGridSpec(
        num_scalar_prefetch=1, grid=(R // tr,),
        in_specs=[pl.BlockSpec((None, tr, C), lambda i, c_ref: (c_ref[0], i, 0)), pl.BlockSpec((tr, C), lambda i, c_ref: (i, 0))],
        out_specs=pl.BlockSpec((tr, C), lambda i, c_ref: (i, 0)))
    return pl.pallas_call(body, name=name, grid_spec=grid_spec, out_shape=jax.ShapeDtypeStruct((R, C), p.dtype),
                          compiler_params=pltpu.CompilerParams(dimension_semantics=("parallel",)))(
        lax.axis_index("c").astype(jnp.int32).reshape(1), p, got)


def _sum_slots(t, name):
    n, R, C = t.shape
    tr = _pick(R, 512, 8)

    def body(t_ref, o_ref):
        acc = t_ref[0]
        for j in range(1, n):
            acc = acc + t_ref[j]
        o_ref[...] = acc

    return pl.pallas_call(body, name=name, grid=(R // tr,), in_specs=[pl.BlockSpec((n, tr, C), lambda i: (0, i, 0))],
                          out_specs=pl.BlockSpec((tr, C), lambda i: (i, 0)), out_shape=jax.ShapeDtypeStruct((R, C), t.dtype),
                          compiler_params=pltpu.CompilerParams(dimension_semantics=("parallel",)))(t)


def _reduce_to_chips(p):
    _, n, R, C = p.shape
    p = p.reshape(2, n * R, C)
    got = _sibling_exchange(p, "grad_sibling_exchange")
    s = _add_own_half(p, got, "grad_sibling_add").reshape(n, R, C)
    t = _chip_scatter(s, "grad_chip_scatter")
    f = _sum_slots(t, "grad_chip_sum")
    return _sibling_gather(f, "grad_sibling_gather")


def _adamw(w, g, m, v, name):
    R, C = w.shape
    tr = _pick(R, 256, 8)

    def body(w_ref, g_ref, m_ref, v_ref, d_ref, nm_ref, nv_ref):
        g_ = g_ref[...]
        m_ = ADAM_B1 * m_ref[...] + (1.0 - ADAM_B1) * g_
        v_ = ADAM_B2 * v_ref[...] + (1.0 - ADAM_B2) * (g_ * g_)
        m_hat = m_ / (1.0 - ADAM_B1 ** ADAM_STEP)
        v_hat = v_ / (1.0 - ADAM_B2 ** ADAM_STEP)
        d_ref[...] = -ADAM_LR * (m_hat / (jnp.sqrt(v_hat) + ADAM_EPS) + ADAM_WD * w_ref[...])
        nm_ref[...] = m_
        nv_ref[...] = v_

    row = pl.BlockSpec((tr, C), lambda i: (i, 0))
    shp = jax.ShapeDtypeStruct((R, C), F32)
    return pl.pallas_call(body, name=name, grid=(R // tr,), in_specs=[row] * 4, out_specs=[row] * 3, out_shape=[shp] * 3,
                          compiler_params=pltpu.CompilerParams(dimension_semantics=("parallel",)))(w, g, m, v)


BIG_SPECS = (("ev_w_in", 1024, 1440, 1), ("ev_w_uq", 256, 768, 1), ("ev_w_ukv", 128, 1024, 1), ("ev_w_out", 1024, 1024, 0),
             ("od_w_in", 1024, 6144, 1), ("od_w_out", 2048, 1024, 0), ("mlp_w1_0", 1024, 4096, 1), ("mlp_w1_1", 1024, 4096, 1),
             ("mlp_w2_0", 4096, 1024, 0), ("mlp_w2_1", 4096, 1024, 0))
BIG_PARAMS = (("ev_w_in", ("ev_w_in",)), ("ev_w_uq", ("ev_w_uq",)), ("ev_w_ukv", ("ev_w_ukv",)), ("ev_w_out", ("ev_w_out",)),
              ("od_w_in", ("od_w_in",)), ("od_w_out", ("od_w_out",)), ("mlp_w1", ("mlp_w1_0", "mlp_w1_1")),
              ("mlp_w2", ("mlp_w2_0", "mlp_w2_1")))
REPLICATED = ("ev_conv_b", "ev_w_rg_a", "ev_b_rg_a", "ev_w_rg_x", "ev_b_rg_x", "ev_lru_lambda", "ev_q_norm_g", "ev_kv_norm_g",
              "ln_mix_g", "ln_mix_b", "ln_mlp_g", "ln_mlp_b")
SMALL_SHARDED = ("meta_tokens", "ev_conv_w")
WEIGHT_NAMES = ("meta_tokens", "ev_w_in", "ev_conv_w", "ev_conv_b", "ev_w_rg_a", "ev_b_rg_a", "ev_w_rg_x", "ev_b_rg_x",
                "ev_lru_lambda", "ev_q_norm_g", "ev_w_uq", "ev_kv_norm_g", "ev_w_ukv", "ev_w_out", "od_w_in", "od_w_out",
                "ln_mix_g", "ln_mix_b", "mlp_w1", "mlp_w2", "ln_mlp_g", "ln_mlp_b")


def _to_rows(flat, row_align):
    n = flat.shape[-1]
    rows = _round_up(-(-n // PACK_COLS), row_align)
    pad = rows * PACK_COLS - n
    if pad:
        flat = jnp.pad(flat, [(0, 0)] * (flat.ndim - 1) + [(0, pad)])
    return flat.reshape(flat.shape[:-1] + (rows, PACK_COLS))


def _shard_shape(K, N, axis):
    return (K // N_CHIPS, N) if axis == 0 else (K, N // N_CHIPS)


def _gather_shards(stacked, K, N, axis):
    if axis == 0:
        return stacked.reshape(K, N)
    return stacked.transpose(1, 0, 2).reshape(K, N)


def _split_shards(full, K, N, axis):
    if axis == 0:
        return full.reshape(N_CHIPS, -1)
    return full.reshape(K, N_CHIPS, N // N_CHIPS).transpose(1, 0, 2).reshape(N_CHIPS, -1)


def kernel(x, meta_tokens, ev_w_in, ev_conv_w, ev_conv_b, ev_w_rg_a, ev_b_rg_a, ev_w_rg_x, ev_b_rg_x, ev_lru_lambda, ev_q_norm_g, ev_w_uq, ev_kv_norm_g, ev_w_ukv, ev_w_out, od_w_in, od_w_out, ln_mix_g, ln_mix_b, mlp_w1, mlp_w2, ln_mlp_g, ln_mlp_b, loss_target, m_meta_tokens, m_ev_w_in, m_ev_conv_w, m_ev_conv_b, m_ev_w_rg_a, m_ev_b_rg_a, m_ev_w_rg_x, m_ev_b_rg_x, m_ev_lru_lambda, m_ev_q_norm_g, m_ev_w_uq, m_ev_kv_norm_g, m_ev_w_ukv, m_ev_w_out, m_od_w_in, m_od_w_out, m_ln_mix_g, m_ln_mix_b, m_mlp_w1, m_mlp_w2, m_ln_mlp_g, m_ln_mlp_b, v_meta_tokens, v_ev_w_in, v_ev_conv_w, v_ev_conv_b, v_ev_w_rg_a, v_ev_b_rg_a, v_ev_w_rg_x, v_ev_b_rg_x, v_ev_lru_lambda, v_ev_q_norm_g, v_ev_w_uq, v_ev_kv_norm_g, v_ev_w_ukv, v_ev_w_out, v_od_w_in, v_od_w_out, v_ln_mix_g, v_ln_mix_b, v_mlp_w1, v_mlp_w2, v_ln_mlp_g, v_ln_mlp_b):
    given = dict(locals())
    local_big = {"ev_w_in": ev_w_in[0], "ev_w_uq": ev_w_uq[0], "ev_w_ukv": ev_w_ukv[0], "ev_w_out": ev_w_out[0],
                 "od_w_in": od_w_in[0], "od_w_out": od_w_out[0], "mlp_w1_0": mlp_w1[0], "mlp_w1_1": mlp_w1[1],
                 "mlp_w2_0": mlp_w2[0], "mlp_w2_1": mlp_w2[1]}

    sizes = [math.prod(_shard_shape(K, N, ax)) for _, K, N, ax in BIG_SPECS]
    packed = _to_rows(jnp.concatenate([local_big[n].astype(MXU_DTYPE).reshape(-1) for n, _, _, _ in BIG_SPECS]), 256)
    gathered = _allgather_chips(packed, "weight_allgather").reshape(N_CHIPS, -1)
    wfull, off = {}, 0
    for (n, K, N, ax), sz in zip(BIG_SPECS, sizes):
        wfull[n] = _gather_shards(gathered[:, off:off + sz].reshape((N_CHIPS,) + _shard_shape(K, N, ax)), K, N, ax)
        off += sz
    small = _to_rows(jnp.concatenate([meta_tokens.reshape(-1), ev_conv_w.reshape(-1)]), 16)
    small = _allgather_chips(small, "small_allgather").reshape(N_CHIPS, -1)
    n_meta, n_conv = meta_tokens.size, ev_conv_w.size
    meta_full = _gather_shards(small[:, :n_meta].reshape(N_CHIPS, N_META, D_MODEL // N_CHIPS), N_META, D_MODEL, 1)
    conv_full = _gather_shards(small[:, n_meta:n_meta + n_conv].reshape(N_CHIPS, CONV_WIDTH, LRU_WIDTH // N_CHIPS),
                               CONV_WIDTH, LRU_WIDTH, 1)

    diff = {n: jnp.zeros((K, N), F32) for n, K, N, _ in BIG_SPECS}
    diff.update({n: given[n] for n in REPLICATED})
    diff.update(x=x, meta_tokens=meta_full, ev_conv_w=conv_full)
    loss, g = jax.value_and_grad(_local_loss)(diff, wfull, loss_target)
    loss = lax.psum(loss, ("x", "y", "c"))

    repl = jnp.concatenate([g[n].reshape(-1) for n in REPLICATED]).reshape(N_CHIPS, -1)
    pieces = [_split_shards(g[n], K, N, ax) for n, K, N, ax in BIG_SPECS]
    pieces += [_split_shards(g["meta_tokens"], N_META, D_MODEL, 1), _split_shards(g["ev_conv_w"], CONV_WIDTH, LRU_WIDTH, 1), repl]
    p = jnp.stack([_to_rows(jnp.concatenate([pc[:, hc * (pc.shape[1] // 2):(hc + 1) * (pc.shape[1] // 2)] for pc in pieces], axis=1), 8)
                   for hc in range(2)])
    red = _reduce_to_chips(p).reshape(2, -1)

    def take(off, sz):
        return jnp.concatenate([red[0, off // 2:(off + sz) // 2], red[1, off // 2:(off + sz) // 2]])

    grads, off = {}, 0
    for name, parts in BIG_PARAMS:
        shp = given[name].shape
        per = given[name].size // len(parts)
        grads[name] = jnp.concatenate([take(off + i * per, per) for i in range(len(parts))]).reshape(shp)
        off += given[name].size
    for name in SMALL_SHARDED:
        sz = given[name].size
        grads[name] = take(off, sz).reshape(given[name].shape)
        off += sz
    n_repl = repl.shape[1]
    repl_all = _allgather_chips(_to_rows(take(off, n_repl), 16), "replicated_allgather").reshape(N_CHIPS, -1)[:, :n_repl].reshape(-1)
    off = 0
    for name in REPLICATED:
        sz = given[name].size
        grads[name] = repl_all[off:off + sz].reshape(given[name].shape)
        off += sz

    delta, new_m, new_v = {}, {}, {}
    for name, _ in BIG_PARAMS:
        shp = given[name].shape
        two_d = (-1, shp[-1])
        d, nm, nv = _adamw(given[name].reshape(two_d), grads[name].reshape(two_d), given["m_" + name].reshape(two_d),
                           given["v_" + name].reshape(two_d), "adamw_" + name)
        delta[name], new_m[name], new_v[name] = d.reshape(shp), nm.reshape(shp), nv.reshape(shp)
    smalls = SMALL_SHARDED + REPLICATED

    def pack_small(get):
        return _to_rows(jnp.concatenate([get(n).reshape(-1) for n in smalls]), 8)

    outs = _adamw(pack_small(lambda n: given[n]), pack_small(lambda n: grads[n]), pack_small(lambda n: given["m_" + n]),
                  pack_small(lambda n: given["v_" + n]), "adamw_small")
    for res, flat in zip((delta, new_m, new_v), outs):
        flat, off = flat.reshape(-1), 0
        for n in smalls:
            sz = given[n].size
            res[n] = flat[off:off + sz].reshape(given[n].shape)
            off += sz

    return (loss, g["x"], *[grads[n] for n in WEIGHT_NAMES], *[delta[n] for n in WEIGHT_NAMES],
            *[new_m[n] for n in WEIGHT_NAMES], *[new_v[n] for n in WEIGHT_NAMES])
```

```python
import functools
import math

import jax
import jax.numpy as jnp
from jax import lax
from jax.experimental import pallas as pl
from jax.experimental.pallas import tpu as pltpu

F32 = jnp.float32
MXU_DTYPE = jnp.bfloat16

D_MODEL = 1024
N_META = 16
LRU_WIDTH = 512
LRU_HEADS = 4
LRU_HEAD_DIM = 128
CONV_WIDTH = 4
LRU_C = 8.0
MLA_HEADS = 8
MLA_NOPE = 64
MLA_ROPE = 32
MLA_V = 64
MLA_Q_RANK = 256
MLA_KV_RANK = 128
RET_HEADS = 4
RET_QK_DIM = 256
RET_V_DIM = 512
D_FF = 4096
ROPE_BASE = 10000.0
DN_ALPHA = 4.0 ** 0.25
EPS = 1e-5
NEG_INF = -1e30
SEQ_BLOCK = 128

ADAM_LR = 0.001
ADAM_B1 = 0.9
ADAM_B2 = 0.999
ADAM_EPS = 1e-08
ADAM_WD = 0.01
ADAM_STEP = 10

PACK_COLS = 1024
N_CHIPS = 4
LOCAL_PIECES = 10

MESH = pl.DeviceIdType.MESH


def _pick(n, target, align):
    best = None
    for t in range(align, min(n, target) + 1, align):
        if n % t == 0:
            best = t
    return n if best is None else best


def _round_up(n, m):
    return (n + m - 1) // m * m


def _relu2(a):
    r = jnp.maximum(a, 0.0)
    return r * r


def _mm_nn(a, w, act, name, out_dtype=F32):
    M, K = a.shape
    _, N = w.shape
    tm = _pick(M, 1088 if K * a.dtype.itemsize <= 4096 else 544, 8)
    tn = _pick(N, 1024, 128)

    def body(a_ref, w_ref, o_ref):
        av = a_ref[...]
        if act:
            av = _relu2(av.astype(F32))
        o_ref[...] = jnp.dot(av.astype(MXU_DTYPE), w_ref[...].astype(MXU_DTYPE), preferred_element_type=F32).astype(out_dtype)

    return pl.pallas_call(
        body, name=name,
        grid=(M // tm, N // tn),
        in_specs=[pl.BlockSpec((tm, K), lambda i, j: (i, 0)), pl.BlockSpec((K, tn), lambda i, j: (0, j))],
        out_specs=pl.BlockSpec((tm, tn), lambda i, j: (i, j)),
        out_shape=jax.ShapeDtypeStruct((M, N), out_dtype),
        compiler_params=pltpu.CompilerParams(dimension_semantics=("parallel", "arbitrary")),
    )(a, w)


def _mm_nt(g, w, a_src, name, out_dtype=F32):
    M, N = g.shape
    K, _ = w.shape
    tk = N if N * g.dtype.itemsize <= 8192 else _pick(N, 2048, 128)
    nk = N // tk
    tm = _pick(M, 1088 if tk * g.dtype.itemsize <= 4096 else 544, 8)
    tn = _pick(K, 1024, 128)
    has_src = a_src is not None
    assert nk == 1 or out_dtype == F32

    def body(*refs):
        if has_src:
            g_ref, w_ref, s_ref, o_ref = refs
        else:
            g_ref, w_ref, o_ref = refs
        r = lax.dot_general(g_ref[...].astype(MXU_DTYPE), w_ref[...].astype(MXU_DTYPE),
                            (((1,), (1,)), ((), ())), preferred_element_type=F32)
        if has_src:
            r = r * (2.0 * jnp.maximum(s_ref[...].astype(F32), 0.0))
        if nk == 1:
            o_ref[...] = r.astype(out_dtype)
        else:
            k = pl.program_id(2)

            @pl.when(k == 0)
            def _():
                o_ref[...] = r

            @pl.when(k > 0)
            def _():
                o_ref[...] += r

    in_specs = [pl.BlockSpec((tm, tk), lambda i, j, k: (i, k)), pl.BlockSpec((tn, tk), lambda i, j, k: (j, k))]
    args = [g, w]
    if has_src:
        assert nk == 1
        in_specs.append(pl.BlockSpec((tm, tn), lambda i, j, k: (i, j)))
        args.append(a_src)
    return pl.pallas_call(
        body, name=name,
        grid=(M // tm, K // tn, nk),
        in_specs=in_specs,
        out_specs=pl.BlockSpec((tm, tn), lambda i, j, k: (i, j)),
        out_shape=jax.ShapeDtypeStruct((M, K), out_dtype),
        compiler_params=pltpu.CompilerParams(dimension_semantics=("parallel", "parallel", "arbitrary")),
    )(*args)


def _mm_tn(a, g, act, name):
    M, K = a.shape
    _, N = g.shape
    tm, tn, tk = _pick(K, 1024, 128), _pick(N, 1024, 128), _pick(M, 1088, 8)
    nk = M // tk

    def body(a_ref, g_ref, o_ref):
        k = pl.program_id(2)
        av = a_ref[...]
        if act:
            av = _relu2(av.astype(F32))
        r = lax.dot_general(av.astype(MXU_DTYPE), g_ref[...].astype(MXU_DTYPE),
                            (((0,), (0,)), ((), ())), preferred_element_type=F32)

        @pl.when(k == 0)
        def _():
            o_ref[...] = r

        @pl.when(k > 0)
        def _():
            o_ref[...] += r

    return pl.pallas_call(
        body, name=name,
        grid=(K // tm, N // tn, nk),
        in_specs=[pl.BlockSpec((tk, tm), lambda i, j, k: (k, i)), pl.BlockSpec((tk, tn), lambda i, j, k: (k, j))],
        out_specs=pl.BlockSpec((tm, tn), lambda i, j, k: (i, j)),
        out_shape=jax.ShapeDtypeStruct((K, N), F32),
        compiler_params=pltpu.CompilerParams(dimension_semantics=("parallel", "parallel", "arbitrary")),
    )(a, g)


@functools.partial(jax.custom_vjp, nondiff_argnums=(3, 4, 5))
def matmul(a, w, w_grad_slot, act, name, out_dtype):
    return _mm_nn(a, w, act, name + "_fwd", out_dtype)


def _matmul_fwd(a, w, w_grad_slot, act, name, out_dtype):
    return _mm_nn(a, w, act, name + "_fwd", out_dtype), (a, w)


def _matmul_bwd(act, name, out_dtype, res, g):
    a, w = res
    da = _mm_nt(g, w, a if act else None, name + "_dx")
    dw = _mm_tn(a, g, act, name + "_dw")
    return da, None, dw


matmul.defvjp(_matmul_fwd, _matmul_bwd)


@functools.partial(jax.custom_vjp, nondiff_argnums=(5,))
def mlp(h, w1, w2, w1_grad_slot, w2_grad_slot, name):
    u = _mm_nn(h, w1, False, name + "_w1_fwd", out_dtype=MXU_DTYPE)
    return _mm_nn(u, w2, True, name + "_w2_fwd")


def _mlp_fwd(h, w1, w2, w1_grad_slot, w2_grad_slot, name):
    u = _mm_nn(h, w1, False, name + "_w1_fwd", out_dtype=MXU_DTYPE)
    return _mm_nn(u, w2, True, name + "_w2_fwd"), (h, u, w1, w2)


def _mlp_bwd(name, res, df):
    h, u, w1, w2 = res
    du = _mm_nt(df, w2, u, name + "_w2_dx", out_dtype=MXU_DTYPE)
    dw2 = _mm_tn(u, df, True, name + "_w2_dw")
    dh = _mm_nt(du, w1, None, name + "_w1_dx")
    dw1 = _mm_tn(h, du, False, name + "_w1_dw")
    return dh, None, None, dw1, dw2


mlp.defvjp(_mlp_fwd, _mlp_bwd)


def _ln_stats(z):
    mu = jnp.mean(z, axis=-1, keepdims=True)
    zc = z - mu
    var = jnp.mean(zc * zc, axis=-1, keepdims=True)
    return zc, lax.rsqrt(var + EPS)


def _ln_fwd_call(resid, branch, g, b, name):
    M, D = resid.shape
    tm = _pick(M, 544, 8)

    def body(r_ref, br_ref, g_ref, b_ref, o_ref):
        zc, rstd = _ln_stats(DN_ALPHA * r_ref[...] + br_ref[...])
        o_ref[...] = zc * rstd * g_ref[...] + b_ref[...]

    row = pl.BlockSpec((tm, D), lambda i: (i, 0))
    vec = pl.BlockSpec((1, D), lambda i: (0, 0))
    return pl.pallas_call(
        body, name=name, grid=(M // tm,), in_specs=[row, row, vec, vec], out_specs=row,
        out_shape=jax.ShapeDtypeStruct((M, D), F32),
        compiler_params=pltpu.CompilerParams(dimension_semantics=("parallel",)),
    )(resid, branch, g.reshape(1, D), b.reshape(1, D))


def _ln_bwd_call(resid, branch, g, dy, name):
    M, D = resid.shape
    tm = _pick(M, 544, 8)

    def body(r_ref, br_ref, g_ref, dy_ref, dz_ref, dg_ref, db_ref):
        @pl.when(pl.program_id(0) == 0)
        def _():
            dg_ref[...] = jnp.zeros_like(dg_ref)
            db_ref[...] = jnp.zeros_like(db_ref)

        zc, rstd = _ln_stats(DN_ALPHA * r_ref[...] + br_ref[...])
        xhat = zc * rstd
        dy = dy_ref[...]
        dxh = dy * g_ref[...]
        m1 = jnp.mean(dxh, axis=-1, keepdims=True)
        m2 = jnp.mean(dxh * xhat, axis=-1, keepdims=True)
        dz_ref[...] = rstd * (dxh - m1 - xhat * m2)
        dg_ref[...] += jnp.sum(dy * xhat, axis=0, keepdims=True)
        db_ref[...] += jnp.sum(dy, axis=0, keepdims=True)

    row = pl.BlockSpec((tm, D), lambda i: (i, 0))
    vec = pl.BlockSpec((1, D), lambda i: (0, 0))
    return pl.pallas_call(
        body, name=name, grid=(M // tm,), in_specs=[row, row, vec, row], out_specs=[row, vec, vec],
        out_shape=[jax.ShapeDtypeStruct((M, D), F32), jax.ShapeDtypeStruct((1, D), F32), jax.ShapeDtypeStruct((1, D), F32)],
        compiler_params=pltpu.CompilerParams(dimension_semantics=("arbitrary",)),
    )(resid, branch, g.reshape(1, D), dy)


@functools.partial(jax.custom_vjp, nondiff_argnums=(4,))
def deepnorm(resid, branch, g, b, name):
    return _ln_fwd_call(resid, branch, g, b, name + "_fwd")


def _deepnorm_fwd(resid, branch, g, b, name):
    return _ln_fwd_call(resid, branch, g, b, name + "_fwd"), (resid, branch, g)


def _deepnorm_bwd(name, res, dy):
    resid, branch, g = res
    dz, dg, db = _ln_bwd_call(resid, branch, g, dy, name + "_bwd")
    return DN_ALPHA * dz, dz, dg.reshape(g.shape), db.reshape(g.shape)


deepnorm.defvjp(_deepnorm_fwd, _deepnorm_bwd)


def _rms_fwd_call(x, g, name, col_block=0):
    R = x.shape[0]
    W = g.shape[-1]
    tr = _pick(R, 1088, 8)

    def body(x_ref, g_ref, o_ref):
        xv = x_ref[...]
        rstd = lax.rsqrt(jnp.mean(xv * xv, axis=-1, keepdims=True) + EPS)
        o_ref[...] = xv * rstd * g_ref[...]

    vec = pl.BlockSpec((1, W), lambda i: (0, 0))
    return pl.pallas_call(
        body, name=name, grid=(R // tr,), in_specs=[pl.BlockSpec((tr, W), lambda i: (i, col_block)), vec],
        out_specs=pl.BlockSpec((tr, W), lambda i: (i, 0)), out_shape=jax.ShapeDtypeStruct((R, W), F32),
        compiler_params=pltpu.CompilerParams(dimension_semantics=("parallel",)),
    )(x, g.reshape(1, W))


def _rms_bwd_call(x, g, dy, name, col_block=0):
    R = x.shape[0]
    W = g.shape[-1]
    tr = _pick(R, 1088, 8)

    def body(x_ref, g_ref, dy_ref, dx_ref, dg_ref):
        @pl.when(pl.program_id(0) == 0)
        def _():
            dg_ref[...] = jnp.zeros_like(dg_ref)

        xv = x_ref[...]
        rstd = lax.rsqrt(jnp.mean(xv * xv, axis=-1, keepdims=True) + EPS)
        xhat = xv * rstd
        dy = dy_ref[...]
        dxh = dy * g_ref[...]
        dx_ref[...] = rstd * (dxh - xhat * jnp.mean(dxh * xhat, axis=-1, keepdims=True))
        dg_ref[...] += jnp.sum(dy * xhat, axis=0, keepdims=True)

    row = pl.BlockSpec((tr, W), lambda i: (i, 0))
    vec = pl.BlockSpec((1, W), lambda i: (0, 0))
    return pl.pallas_call(
        body, name=name, grid=(R // tr,), in_specs=[pl.BlockSpec((tr, W), lambda i: (i, col_block)), vec, row], out_specs=[row, vec],
        out_shape=[jax.ShapeDtypeStruct((R, W), F32), jax.ShapeDtypeStruct((1, W), F32)],
        compiler_params=pltpu.CompilerParams(dimension_semantics=("arbitrary",)),
    )(x, g.reshape(1, W), dy)


def _loss_call(y, tgt, name):
    R, D = y.shape
    tr = _pick(R, 512, 8)

    def body(y_ref, t_ref, dy_ref, acc_ref):
        @pl.when(pl.program_id(0) == 0)
        def _():
            acc_ref[...] = jnp.zeros_like(acc_ref)

        e = y_ref[...] - t_ref[...]
        dy_ref[...] = e * (1.0 / D)
        acc_ref[...] += jnp.sum(jnp.sum(e * e, axis=-1, keepdims=True), axis=0, keepdims=True) * (0.5 / D)

    row = pl.BlockSpec((tr, D), lambda i: (i, 0))
    one = pl.BlockSpec((1, 1), lambda i: (0, 0))
    return pl.pallas_call(
        body, name=name, grid=(R // tr,), in_specs=[row, row], out_specs=[row, one],
        out_shape=[jax.ShapeDtypeStruct((R, D), F32), jax.ShapeDtypeStruct((1, 1), F32)],
        compiler_params=pltpu.CompilerParams(dimension_semantics=("arbitrary",)),
    )(y, tgt)


@jax.custom_vjp
def loss_head(y, tgt):
    return _loss_call(y, tgt, "loss_head")[1][0, 0]


def _loss_head_fwd(y, tgt):
    dy, acc = _loss_call(y, tgt, "loss_head")
    return acc[0, 0], dy


def _loss_head_bwd(dy, ct):
    return ct * dy, None


loss_head.defvjp(_loss_head_fwd, _loss_head_bwd)


_GELU_C = math.sqrt(2.0 / math.pi)


def _gelu_parts(x):
    x2 = x * x
    t = jnp.tanh(_GELU_C * (x + 0.044715 * x * x2))
    gelu = 0.5 * x * (1.0 + t)
    dgelu = 0.5 * (1.0 + t) + 0.5 * x * (1.0 - t * t) * (_GELU_C * (1.0 + 3.0 * 0.044715 * x2))
    return gelu, dgelu


def _sigmoid(x):
    return 1.0 / (1.0 + jnp.exp(-x))


def _scan8(a, b, carry, reverse):
    row = lax.broadcasted_iota(jnp.int32, a.shape, 0)
    for s in (1, 2, 4):
        shift = 8 - s if reverse else s
        keep = (row < 8 - s) if reverse else (row >= s)
        b = jnp.where(keep, a * pltpu.roll(b, shift, 0) + b, b)
        a = jnp.where(keep, a * pltpu.roll(a, shift, 0), a)
    return a * carry + b


def _lru_pre(prec_ref, prev_ref, first, cw_ref, cb_ref, wa_ref, ba_ref, wx_ref, bx_ref, sp_ref):
    tc = prec_ref.shape[0]
    prev = jnp.where(first, 0.0, prev_ref[...])
    ext = jnp.concatenate([prev, prec_ref[...]], axis=0)
    cw = cw_ref[...]
    taps = [ext[8:] if k == CONV_WIDTH - 1 else pltpu.roll(ext, CONV_WIDTH - 1 - k, 0)[8:] for k in range(CONV_WIDTH)]
    xc = cb_ref[...] + sum(cw[k:k + 1, :] * taps[k] for k in range(CONV_WIDTH))
    ga, gx = [], []
    for h in range(LRU_HEADS):
        xh = xc[:, h * LRU_HEAD_DIM:(h + 1) * LRU_HEAD_DIM].astype(MXU_DTYPE)
        ga.append(jnp.dot(xh, wa_ref[h].astype(MXU_DTYPE), preferred_element_type=F32))
        gx.append(jnp.dot(xh, wx_ref[h].astype(MXU_DTYPE), preferred_element_type=F32))
    r = _sigmoid(jnp.concatenate(ga, axis=1) + ba_ref[...])
    i = _sigmoid(jnp.concatenate(gx, axis=1) + bx_ref[...])
    log_a = -LRU_C * r * sp_ref[...]
    a = jnp.exp(log_a)
    a2 = a * a
    mult = jnp.sqrt(-jnp.tanh(log_a) * (a2 + 1.0))
    return taps, xc, r, i, a, a2, mult


def _lru_fwd_call(p, cw, cb, wa, ba, wx, bx, sp):
    B, Tp, _ = p.shape
    W = LRU_WIDTH
    tc = SEQ_BLOCK
    nc = Tp // tc

    def body(pg_ref, prec_ref, prev_ref, cw_ref, cb_ref, wa_ref, ba_ref, wx_ref, bx_ref, sp_ref, y_ref, h_ref, carry_ref):
        first = pl.program_id(1) == 0

        @pl.when(first)
        def _():
            carry_ref[...] = jnp.zeros_like(carry_ref)

        _, xc, r, i, a, a2, mult = _lru_pre(prec_ref, prev_ref, first, cw_ref, cb_ref, wa_ref, ba_ref, wx_ref, bx_ref, sp_ref)
        b = mult * (i * xc)
        carry = carry_ref[0:1, :]
        for t in range(tc // 8):
            h = _scan8(a[8 * t:8 * t + 8], b[8 * t:8 * t + 8], carry, False)
            h_ref[8 * t:8 * t + 8, :] = h
            carry = h[7:8, :]
        carry_ref[...] = jnp.broadcast_to(carry, carry_ref.shape)
        y_ref[...] = h_ref[...] * _gelu_parts(pg_ref[...])[0]

    cur = pl.BlockSpec((None, tc, W), lambda b, j: (b, j, 0))
    rec = pl.BlockSpec((None, tc, W), lambda b, j: (b, j, 1))
    prev = pl.BlockSpec((None, 8, W), lambda b, j: (b, jnp.maximum(j * (tc // 8) - 1, 0), 1))
    vec = pl.BlockSpec((1, W), lambda b, j: (0, 0))
    cws = pl.BlockSpec((CONV_WIDTH, W), lambda b, j: (0, 0))
    wsp = pl.BlockSpec((LRU_HEADS, LRU_HEAD_DIM, LRU_HEAD_DIM), lambda b, j: (0, 0, 0))
    return pl.pallas_call(
        body, name="lru_fwd", grid=(B, nc),
        in_specs=[cur, rec, prev, cws, vec, wsp, vec, wsp, vec, vec],
        out_specs=[cur, cur],
        out_shape=[jax.ShapeDtypeStruct((B, Tp, W), F32), jax.ShapeDtypeStruct((B, Tp, W), F32)],
        scratch_shapes=[pltpu.VMEM((8, W), F32)],
        compiler_params=pltpu.CompilerParams(dimension_semantics=("arbitrary", "arbitrary")),
    )(p, p, p, cw, cb, wa, ba, wx, bx, sp)


def _lru_bwd_call(p, hseq, dy, cw, cb, wa, ba, wx, bx, sp):
    B, Tp, _ = p.shape
    W = LRU_WIDTH
    tc = SEQ_BLOCK
    nc = Tp // tc
    HD = LRU_HEAD_DIM

    def body(pg_ref, prec_ref, prev_ref, h_ref, hprev_ref, dy_ref, cw_ref, cb_ref, wa_ref, ba_ref, wx_ref, bx_ref, sp_ref,
             dpg_ref, dprec_ref, dcw_ref, dcb_ref, dwa_ref, dba_ref, dwx_ref, dbx_ref, dsp_ref,
             gcar_ref, anext_ref, halo_ref, g_ref):
        j = pl.program_id(1)
        first = j == nc - 1
        last = j == 0

        @pl.when(jnp.logical_and(pl.program_id(0) == 0, last))
        def _():
            for ref in (dcw_ref, dcb_ref, dwa_ref, dba_ref, dwx_ref, dbx_ref, dsp_ref):
                ref[...] = jnp.zeros_like(ref)

        @pl.when(last)
        def _():
            gcar_ref[...] = jnp.zeros_like(gcar_ref)
            anext_ref[...] = jnp.zeros_like(anext_ref)
            halo_ref[...] = jnp.zeros_like(halo_ref)

        taps, xc, r, i, a, a2, mult = _lru_pre(prec_ref, prev_ref, first, cw_ref, cb_ref, wa_ref, ba_ref, wx_ref, bx_ref, sp_ref)
        row = lax.broadcasted_iota(jnp.int32, (tc, W), 0)
        gelu, dgelu = _gelu_parts(pg_ref[...])
        dy = dy_ref[...]
        hcur = h_ref[...]
        dpg_ref[...] = dy * hcur * dgelu
        dh = dy * gelu
        a_next = jnp.where(row == tc - 1, anext_ref[0:1, :], pltpu.roll(a, tc - 1, 0))
        carry = gcar_ref[0:1, :]
        for t in reversed(range(tc // 8)):
            g = _scan8(a_next[8 * t:8 * t + 8], dh[8 * t:8 * t + 8], carry, True)
            g_ref[8 * t:8 * t + 8, :] = g
            carry = g[0:1, :]
        gcar_ref[...] = jnp.broadcast_to(carry, gcar_ref.shape)
        anext_ref[...] = jnp.broadcast_to(a[0:1, :], anext_ref.shape)
        G = g_ref[...]
        h_before = jnp.where(first, 0.0, hprev_ref[7:8, :])
        hprev = jnp.where(row == 0, h_before, pltpu.roll(hcur, 1, 0))
        d_a = G * hprev
        gx_ = G * xc
        d_mult = gx_ * i
        d_i = gx_ * mult
        dxc = G * (mult * i)
        d_la = d_a * a - d_mult * (a2 / mult)
        sp = sp_ref[...]
        d_r = d_la * (-LRU_C * sp)
        dsp_ref[...] += jnp.sum(d_la * (-LRU_C * r), axis=0, keepdims=True)
        dga = d_r * r * (1.0 - r)
        dgx = d_i * i * (1.0 - i)
        dba_ref[...] += jnp.sum(dga, axis=0, keepdims=True)
        dbx_ref[...] += jnp.sum(dgx, axis=0, keepdims=True)
        back = []
        for h in range(LRU_HEADS):
            sl = slice(h * HD, (h + 1) * HD)
            xh = xc[:, sl].astype(MXU_DTYPE)
            ah = dga[:, sl].astype(MXU_DTYPE)
            bh = dgx[:, sl].astype(MXU_DTYPE)
            tn = (((0,), (0,)), ((), ()))
            nt = (((1,), (1,)), ((), ()))
            dwa_ref[h] += lax.dot_general(xh, ah, tn, preferred_element_type=F32)
            dwx_ref[h] += lax.dot_general(xh, bh, tn, preferred_element_type=F32)
            back.append(lax.dot_general(ah, wa_ref[h].astype(MXU_DTYPE), nt, preferred_element_type=F32)
                        + lax.dot_general(bh, wx_ref[h].astype(MXU_DTYPE), nt, preferred_element_type=F32))
        dxc = dxc + jnp.concatenate(back, axis=1)
        dcb_ref[...] += jnp.sum(dxc, axis=0, keepdims=True)
        for k in range(CONV_WIDTH):
            dcw_ref[k:k + 1, :] += jnp.sum(dxc * taps[k], axis=0, keepdims=True)
        ext = jnp.concatenate([dxc, halo_ref[...]], axis=0)
        cw = cw_ref[...]
        acc = cw[CONV_WIDTH - 1:CONV_WIDTH, :] * dxc
        for k in range(CONV_WIDTH - 1):
            s = CONV_WIDTH - 1 - k
            acc = acc + cw[k:k + 1, :] * pltpu.roll(ext, tc + 8 - s, 0)[:tc]
        dprec_ref[...] = acc
        halo_ref[...] = dxc[0:8, :]

    rev = lambda j: nc - 1 - j
    cur = pl.BlockSpec((None, tc, W), lambda b, j: (b, rev(j), 0))
    rec = pl.BlockSpec((None, tc, W), lambda b, j: (b, rev(j), 1))
    prev = pl.BlockSpec((None, 8, W), lambda b, j: (b, jnp.maximum(rev(j) * (tc // 8) - 1, 0), 0))
    prev_rec = pl.BlockSpec((None, 8, W), lambda b, j: (b, jnp.maximum(rev(j) * (tc // 8) - 1, 0), 1))
    vec = pl.BlockSpec((1, W), lambda b, j: (0, 0))
    cws = pl.BlockSpec((CONV_WIDTH, W), lambda b, j: (0, 0))
    wsp = pl.BlockSpec((LRU_HEADS, HD, HD), lambda b, j: (0, 0, 0))
    seq = jax.ShapeDtypeStruct((B, Tp, W), F32)
    vs = jax.ShapeDtypeStruct((1, W), F32)
    ws = jax.ShapeDtypeStruct((LRU_HEADS, HD, HD), F32)
    return pl.pallas_call(
        body, name="lru_bwd", grid=(B, nc),
        in_specs=[cur, rec, prev_rec, cur, prev, cur, cws, vec, wsp, vec, wsp, vec, vec],
        out_specs=[cur, cur, cws, vec, wsp, vec, wsp, vec, vec],
        out_shape=[seq, seq, jax.ShapeDtypeStruct((CONV_WIDTH, W), F32), vs, ws, vs, ws, vs, vs],
        scratch_shapes=[pltpu.VMEM((8, W), F32), pltpu.VMEM((8, W), F32), pltpu.VMEM((8, W), F32), pltpu.VMEM((tc, W), F32)],
        compiler_params=pltpu.CompilerParams(dimension_semantics=("arbitrary", "arbitrary")),
    )(p, p, p, hseq, hseq, dy, cw, cb, wa, ba, wx, bx, sp)


_Q_BLOCK = 2 * LRU_WIDTH // MLA_Q_RANK
_KV_BLOCK = (2 * LRU_WIDTH + MLA_Q_RANK) // MLA_KV_RANK
_KPE_START = 2 * LRU_WIDTH + MLA_Q_RANK + MLA_KV_RANK


@jax.custom_vjp
def even_front(p, cw, cb, wa, ba, wx, bx, sp, gq, gkv):
    return _even_front_fwd(p, cw, cb, wa, ba, wx, bx, sp, gq, gkv)[0]


def _even_front_fwd(p, cw, cb, wa, ba, wx, bx, sp, gq, gkv):
    B, Tp, W = p.shape
    p2d = p.reshape(B * Tp, W)
    y, hseq = _lru_fwd_call(p, cw, cb, wa, ba, wx, bx, sp)
    qn = _rms_fwd_call(p2d, gq, "q_norm_fwd", _Q_BLOCK)
    kvn = _rms_fwd_call(p2d, gkv, "kv_norm_fwd", _KV_BLOCK)
    return (y, qn, kvn, p2d[:, _KPE_START:]), (p, hseq, cw, cb, wa, ba, wx, bx, sp, gq, gkv)


def _even_front_bwd(res, cts):
    p, hseq, cw, cb, wa, ba, wx, bx, sp, gq, gkv = res
    dy, dqn, dkvn, dkpe = cts
    B, Tp, W = p.shape
    p2d = p.reshape(B * Tp, W)
    dpg, dprec, dcw, dcb, dwa, dba, dwx, dbx, dsp = _lru_bwd_call(p, hseq, dy, cw, cb, wa, ba, wx, bx, sp)
    dpq, dgq = _rms_bwd_call(p2d, gq, dqn, "q_norm_bwd", _Q_BLOCK)
    dpkv, dgkv = _rms_bwd_call(p2d, gkv, dkvn, "kv_norm_bwd", _KV_BLOCK)
    dp = jnp.concatenate([dpg.reshape(B * Tp, -1), dprec.reshape(B * Tp, -1), dpq, dpkv, dkpe], axis=1).reshape(B, Tp, W)
    return dp, dcw, dcb, dwa, dba, dwx, dbx, dsp, dgq.reshape(gq.shape), dgkv.reshape(gkv.shape)


even_front.defvjp(_even_front_fwd, _even_front_bwd)


def _rope_tables(pos, half):
    inv = ROPE_BASE ** (-jnp.arange(half, dtype=F32) / half)
    ang = pos.astype(F32)[:, None] * inv[None, :]
    return jnp.cos(ang), jnp.sin(ang)


_NT = (((1,), (1,)), ((), ()))
_TN = (((0,), (0,)), ((), ()))
HEAD_LANES = 128
_MLA_SCALE = (MLA_NOPE + MLA_ROPE) ** -0.5


def _causal_keep(qi, L):
    row = lax.broadcasted_iota(jnp.int32, (SEQ_BLOCK, L), 0) + qi * SEQ_BLOCK
    col = lax.broadcasted_iota(jnp.int32, (SEQ_BLOCK, L), 1)
    return col <= row


def _mla_rope_tables(pos):
    half = MLA_ROPE // 2
    cos, sin = _rope_tables(pos, half)
    T = pos.shape[0]
    ones, zeros = jnp.ones((T, MLA_NOPE), F32), jnp.zeros((T, MLA_NOPE), F32)
    tail1, tail0 = jnp.ones((T, HEAD_LANES - MLA_NOPE - MLA_ROPE), F32), jnp.zeros((T, HEAD_LANES - MLA_NOPE - MLA_ROPE), F32)
    zh = jnp.zeros((T, half), F32)
    c = jnp.concatenate([ones, cos, cos, tail1], axis=1)
    s_up = jnp.concatenate([zeros, -sin, zh, tail0], axis=1)
    s_down = jnp.concatenate([zeros, zh, sin, tail0], axis=1)
    return c, s_up, s_down


def _rope_lanes(x, c, s_up, s_down):
    half = MLA_ROPE // 2
    return x * c + pltpu.roll(x, HEAD_LANES - half, 1) * s_up + pltpu.roll(x, half, 1) * s_down


def _unrope_lanes(d, c, s_up, s_down):
    half = MLA_ROPE // 2
    return d * c + pltpu.roll(d * s_up, half, 1) + pltpu.roll(d * s_down, HEAD_LANES - half, 1)


def _mla_operands(q_ref, kv_ref, kpe_ref, c, s_up, s_down):
    lane = lax.broadcasted_iota(jnp.int32, kv_ref.shape, 1)
    qr = _rope_lanes(q_ref[...].astype(F32), c, s_up, s_down).astype(MXU_DTYPE)
    kr = jnp.where(lane < MLA_NOPE, kv_ref[...].astype(F32), _rope_lanes(kpe_ref[...], c, s_up, s_down)).astype(MXU_DTYPE)
    return qr, kr, lane


def _mla_specs(Tp):
    head = pl.BlockSpec((None, Tp, HEAD_LANES), lambda b, h: (b, 0, h))
    shared = pl.BlockSpec((None, Tp, HEAD_LANES), lambda b, h: (b, 0, 0))
    tab = pl.BlockSpec((Tp, HEAD_LANES), lambda b, h: (0, 0))
    lse = pl.BlockSpec((None, None, Tp, 1), lambda b, h: (b, h, 0, 0))
    return head, shared, tab, lse


def _attn_fwd_call(q, kv, kpe, tabs):
    B, Tp, _ = q.shape
    nq = Tp // SEQ_BLOCK

    def body(q_ref, kv_ref, kpe_ref, c_ref, su_ref, sd_ref, o_ref, lse_ref, qr_ref, kr_ref):
        qr, kr, lane = _mla_operands(q_ref, kv_ref, kpe_ref, c_ref[...], su_ref[...], sd_ref[...])
        qr_ref[...] = qr
        kr_ref[...] = kr
        for qi in range(nq):
            L = (qi + 1) * SEQ_BLOCK
            blk = slice(qi * SEQ_BLOCK, L)
            s = lax.dot_general(qr_ref[blk, :], kr_ref[0:L, :], _NT, preferred_element_type=F32) * _MLA_SCALE
            s = jnp.where(_causal_keep(qi, L), s, NEG_INF)
            m = jnp.max(s, axis=-1, keepdims=True)
            p = jnp.exp(s - m)
            l = jnp.sum(p, axis=-1, keepdims=True)
            o = jnp.dot(p.astype(MXU_DTYPE), kv_ref[0:L, :].astype(MXU_DTYPE), preferred_element_type=F32)
            o_ref[blk, :] = jnp.where(lane[blk, :] >= MLA_NOPE, o / l, 0.0)
            lse_ref[blk, :] = m + jnp.log(l)

    head, shared, tab, lse = _mla_specs(Tp)
    return pl.pallas_call(
        body, name="mla_attn_fwd", grid=(B, MLA_HEADS), in_specs=[head, head, shared, tab, tab, tab], out_specs=[head, lse],
        out_shape=[jax.ShapeDtypeStruct((B, Tp, MLA_HEADS * HEAD_LANES), F32), jax.ShapeDtypeStruct((B, MLA_HEADS, Tp, 1), F32)],
        scratch_shapes=[pltpu.VMEM((Tp, HEAD_LANES), MXU_DTYPE), pltpu.VMEM((Tp, HEAD_LANES), MXU_DTYPE)],
        compiler_params=pltpu.CompilerParams(dimension_semantics=("parallel", "parallel")),
    )(q, kv, kpe, *tabs)


def _attn_bwd_call(q, kv, kpe, tabs, o, lse, do):
    B, Tp, _ = q.shape
    nq = Tp // SEQ_BLOCK

    def body(q_ref, kv_ref, kpe_ref, c_ref, su_ref, sd_ref, o_ref, lse_ref, do_ref, dq_ref, dkv_ref, dkpe_ref,
             qr_ref, kr_ref, dqa_ref, dka_ref, dva_ref):
        c, s_up, s_down = c_ref[...], su_ref[...], sd_ref[...]
        qr, kr, lane = _mla_operands(q_ref, kv_ref, kpe_ref, c, s_up, s_down)
        qr_ref[...] = qr
        kr_ref[...] = kr
        dka_ref[...] = jnp.zeros_like(dka_ref)
        dva_ref[...] = jnp.zeros_like(dva_ref)
        for qi in range(nq):
            L = (qi + 1) * SEQ_BLOCK
            blk = slice(qi * SEQ_BLOCK, L)
            qb = qr_ref[blk, :]
            do = jnp.where(lane[blk, :] >= MLA_NOPE, do_ref[blk, :], 0.0)
            delta = jnp.sum(do * o_ref[blk, :], axis=-1, keepdims=True)
            s = lax.dot_general(qb, kr_ref[0:L, :], _NT, preferred_element_type=F32) * _MLA_SCALE
            s = jnp.where(_causal_keep(qi, L), s, NEG_INF)
            p = jnp.exp(s - lse_ref[blk, :])
            dob = do.astype(MXU_DTYPE)
            dva_ref[0:L, :] += lax.dot_general(p.astype(MXU_DTYPE), dob, _TN, preferred_element_type=F32)
            dp = lax.dot_general(dob, kv_ref[0:L, :].astype(MXU_DTYPE), _NT, preferred_element_type=F32)
            ds = (p * (dp - delta) * _MLA_SCALE).astype(MXU_DTYPE)
            dqa_ref[blk, :] = jnp.dot(ds, kr_ref[0:L, :], preferred_element_type=F32)
            dka_ref[0:L, :] += lax.dot_general(ds, qb, _TN, preferred_element_type=F32)
        dq_ref[...] = _unrope_lanes(dqa_ref[...], c, s_up, s_down).astype(dq_ref.dtype)
        dk = dka_ref[...]
        dkv_ref[...] = jnp.where(lane < MLA_NOPE, dk, dva_ref[...]).astype(dkv_ref.dtype)
        dkpe = jnp.where(lane >= MLA_NOPE, _unrope_lanes(dk, c, s_up, s_down), 0.0)

        @pl.when(pl.program_id(1) == 0)
        def _():
            dkpe_ref[...] = dkpe

        @pl.when(pl.program_id(1) > 0)
        def _():
            dkpe_ref[...] += dkpe

    head, shared, tab, lse_spec = _mla_specs(Tp)
    wide = jax.ShapeDtypeStruct((B, Tp, MLA_HEADS * HEAD_LANES), q.dtype)
    acc = pltpu.VMEM((Tp, HEAD_LANES), F32)
    return pl.pallas_call(
        body, name="mla_attn_bwd", grid=(B, MLA_HEADS),
        in_specs=[head, head, shared, tab, tab, tab, head, lse_spec, head], out_specs=[head, head, shared],
        out_shape=[wide, wide, jax.ShapeDtypeStruct((B, Tp, HEAD_LANES), F32)],
        scratch_shapes=[pltpu.VMEM((Tp, HEAD_LANES), MXU_DTYPE), pltpu.VMEM((Tp, HEAD_LANES), MXU_DTYPE), acc, acc, acc],
        compiler_params=pltpu.CompilerParams(dimension_semantics=("parallel", "arbitrary")),
    )(q, kv, kpe, *tabs, o, lse, do)


@jax.custom_vjp
def mla_attention(q, kv, kpe, tabs):
    return _attn_fwd_call(q, kv, kpe, tabs)[0]


def _mla_attention_fwd(q, kv, kpe, tabs):
    o, lse = _attn_fwd_call(q, kv, kpe, tabs)
    return o, (q, kv, kpe, tabs, o, lse)


def _mla_attention_bwd(res, do):
    q, kv, kpe, tabs, o, lse = res
    dq, dkv, dkpe = _attn_bwd_call(q, kv, kpe, tabs, o, lse, do)
    return dq, dkv, dkpe, None


mla_attention.defvjp(_mla_attention_fwd, _mla_attention_bwd)


def _decay(qi, L, lg):
    row = lax.broadcasted_iota(jnp.int32, (SEQ_BLOCK, L), 0) + qi * SEQ_BLOCK
    col = lax.broadcasted_iota(jnp.int32, (SEQ_BLOCK, L), 1)
    diff = row - col
    return jnp.where(diff >= 0, jnp.exp(lg * jnp.maximum(diff, 0).astype(F32)), 0.0)


def _rope_halves(x, cos, sin, scale):
    half = x.shape[1] // 2
    x1, x2 = x[:, :half], x[:, half:]
    return (jnp.concatenate([x1 * cos - x2 * sin, x1 * sin + x2 * cos], axis=1) * scale).astype(MXU_DTYPE)


def _unrope_halves(d, cos, sin, scale):
    half = d.shape[1] // 2
    d1, d2 = d[:, :half], d[:, half:]
    return jnp.concatenate([d1 * cos + d2 * sin, d2 * cos - d1 * sin], axis=1) * scale


_RET_K_SCALE = RET_QK_DIM ** -0.5
_RET_Q_BLOCKS = RET_HEADS
_RET_V_BLOCK0 = 2 * RET_HEADS * RET_QK_DIM // RET_V_DIM
_RET_G_BLOCK0 = _RET_V_BLOCK0 + RET_HEADS


def _ret_specs(Tp):
    q = pl.BlockSpec((None, Tp, RET_QK_DIM), lambda b, h: (b, 0, h))
    k = pl.BlockSpec((None, Tp, RET_QK_DIM), lambda b, h: (b, 0, _RET_Q_BLOCKS + h))
    v = pl.BlockSpec((None, Tp, RET_V_DIM), lambda b, h: (b, 0, _RET_V_BLOCK0 + h))
    tab = pl.BlockSpec((Tp, RET_QK_DIM // 2), lambda b, h: (0, 0))
    lg = pl.BlockSpec((None, 1, 1), lambda b, h: (h, 0, 0))
    return q, k, v, tab, lg


def _ret_core_fwd_call(p, cos, sin, lg):
    B, Tp, _ = p.shape
    nq = Tp // SEQ_BLOCK

    def body(q_ref, k_ref, v_ref, cos_ref, sin_ref, lg_ref, o_ref, qr_ref, kr_ref):
        lg_ = lg_ref[...]
        cos_, sin_ = cos_ref[...], sin_ref[...]
        qr_ref[...] = _rope_halves(q_ref[...].astype(F32), cos_, sin_, 1.0)
        kr_ref[...] = _rope_halves(k_ref[...].astype(F32), cos_, sin_, _RET_K_SCALE)
        for qi in range(nq):
            L = (qi + 1) * SEQ_BLOCK
            blk = slice(qi * SEQ_BLOCK, L)
            s = lax.dot_general(qr_ref[blk, :], kr_ref[0:L, :], _NT, preferred_element_type=F32) * _decay(qi, L, lg_)
            o_ref[blk, :] = jnp.dot(s.astype(MXU_DTYPE), v_ref[0:L, :].astype(MXU_DTYPE), preferred_element_type=F32)

    q, k, v, tab, lgs = _ret_specs(Tp)
    return pl.pallas_call(
        body, name="retention_fwd", grid=(B, RET_HEADS), in_specs=[q, k, v, tab, tab, lgs],
        out_specs=pl.BlockSpec((None, Tp, RET_V_DIM), lambda b, h: (b, 0, h)),
        out_shape=jax.ShapeDtypeStruct((B, Tp, RET_HEADS * RET_V_DIM), F32),
        scratch_shapes=[pltpu.VMEM((Tp, RET_QK_DIM), MXU_DTYPE), pltpu.VMEM((Tp, RET_QK_DIM), MXU_DTYPE)],
        compiler_params=pltpu.CompilerParams(dimension_semantics=("parallel", "parallel")),
    )(p, p, p, cos, sin, lg)


def _ret_core_bwd_call(p, do, cos, sin, lg):
    B, Tp, _ = p.shape
    nq = Tp // SEQ_BLOCK

    def body(q_ref, k_ref, v_ref, do_ref, cos_ref, sin_ref, lg_ref, dq_ref, dk_ref, dv_ref, qr_ref, kr_ref, dqa_ref, dka_ref, dva_ref):
        lg_ = lg_ref[...]
        cos_, sin_ = cos_ref[...], sin_ref[...]
        qr_ref[...] = _rope_halves(q_ref[...].astype(F32), cos_, sin_, 1.0)
        kr_ref[...] = _rope_halves(k_ref[...].astype(F32), cos_, sin_, _RET_K_SCALE)
        dka_ref[...] = jnp.zeros_like(dka_ref)
        dva_ref[...] = jnp.zeros_like(dva_ref)
        for qi in range(nq):
            L = (qi + 1) * SEQ_BLOCK
            blk = slice(qi * SEQ_BLOCK, L)
            qb = qr_ref[blk, :]
            dob = do_ref[blk, :].astype(MXU_DTYPE)
            dec = _decay(qi, L, lg_)
            s = (lax.dot_general(qb, kr_ref[0:L, :], _NT, preferred_element_type=F32) * dec).astype(MXU_DTYPE)
            dva_ref[0:L, :] += lax.dot_general(s, dob, _TN, preferred_element_type=F32)
            ds = (lax.dot_general(dob, v_ref[0:L, :].astype(MXU_DTYPE), _NT, preferred_element_type=F32) * dec).astype(MXU_DTYPE)
            dqa_ref[blk, :] = jnp.dot(ds, kr_ref[0:L, :], preferred_element_type=F32)
            dka_ref[0:L, :] += lax.dot_general(ds, qb, _TN, preferred_element_type=F32)
        dq_ref[...] = _unrope_halves(dqa_ref[...], cos_, sin_, 1.0).astype(dq_ref.dtype)
        dk_ref[...] = _unrope_halves(dka_ref[...], cos_, sin_, _RET_K_SCALE).astype(dk_ref.dtype)
        dv_ref[...] = dva_ref[...].astype(dv_ref.dtype)

    q, k, v, tab, lgs = _ret_specs(Tp)
    qk_out = pl.BlockSpec((None, Tp, RET_QK_DIM), lambda b, h: (b, 0, h))
    v_out = pl.BlockSpec((None, Tp, RET_V_DIM), lambda b, h: (b, 0, h))
    return pl.pallas_call(
        body, name="retention_bwd", grid=(B, RET_HEADS), in_specs=[q, k, v, v_out, tab, tab, lgs],
        out_specs=[qk_out, qk_out, v_out],
        out_shape=[jax.ShapeDtypeStruct((B, Tp, RET_HEADS * RET_QK_DIM), p.dtype), jax.ShapeDtypeStruct((B, Tp, RET_HEADS * RET_QK_DIM), p.dtype),
                   jax.ShapeDtypeStruct((B, Tp, RET_HEADS * RET_V_DIM), p.dtype)],
        scratch_shapes=[pltpu.VMEM((Tp, RET_QK_DIM), MXU_DTYPE), pltpu.VMEM((Tp, RET_QK_DIM), MXU_DTYPE),
                        pltpu.VMEM((Tp, RET_QK_DIM), F32), pltpu.VMEM((Tp, RET_QK_DIM), F32), pltpu.VMEM((Tp, RET_V_DIM), F32)],
        compiler_params=pltpu.CompilerParams(dimension_semantics=("parallel", "parallel")),
    )(p, p, p, do, cos, sin, lg)


def _ret_gate_specs(M):
    tm = _pick(M, 1088, 8)
    head = pl.BlockSpec((tm, RET_V_DIM), lambda i, h: (i, h))
    gate = pl.BlockSpec((tm, RET_V_DIM), lambda i, h: (i, _RET_G_BLOCK0 + h))
    return tm, head, gate


def _ret_gate_fwd_call(o, p2d):
    M = o.shape[0]
    tm, head, gate = _ret_gate_specs(M)

    def body(o_ref, g_ref, y_ref):
        ov = o_ref[...]
        gv = g_ref[...].astype(F32)
        rstd = lax.rsqrt(jnp.mean(ov * ov, axis=-1, keepdims=True) + EPS)
        y_ref[...] = (gv * _sigmoid(gv)) * (ov * rstd)

    return pl.pallas_call(
        body, name="retention_gate_fwd", grid=(M // tm, RET_HEADS), in_specs=[head, gate], out_specs=head,
        out_shape=jax.ShapeDtypeStruct(o.shape, F32),
        compiler_params=pltpu.CompilerParams(dimension_semantics=("parallel", "parallel")),
    )(o, p2d)


def _ret_gate_bwd_call(o, p2d, dy):
    M = o.shape[0]
    tm, head, gate = _ret_gate_specs(M)

    def body(o_ref, g_ref, dy_ref, do_ref, dg_ref):
        ov = o_ref[...]
        gv = g_ref[...].astype(F32)
        dy = dy_ref[...]
        rstd = lax.rsqrt(jnp.mean(ov * ov, axis=-1, keepdims=True) + EPS)
        on = ov * rstd
        sg = _sigmoid(gv)
        dg_ref[...] = (dy * on * (sg * (1.0 + gv * (1.0 - sg)))).astype(dg_ref.dtype)
        don = dy * (gv * sg)
        do_ref[...] = (rstd * (don - on * jnp.mean(don * on, axis=-1, keepdims=True))).astype(do_ref.dtype)

    shp = jax.ShapeDtypeStruct(o.shape, p2d.dtype)
    return pl.pallas_call(
        body, name="retention_gate_bwd", grid=(M // tm, RET_HEADS), in_specs=[head, gate, head], out_specs=[head, head],
        out_shape=[shp, shp],
        compiler_params=pltpu.CompilerParams(dimension_semantics=("parallel", "parallel")),
    )(o, p2d, dy)


def _log_gamma():
    return jnp.log(1.0 - 2.0 ** (-5.0 - jnp.arange(RET_HEADS, dtype=F32))).reshape(RET_HEADS, 1, 1)


@jax.custom_vjp
def retention_mixer(p, cos, sin):
    B, Tp, W = p.shape
    o = _ret_core_fwd_call(p, cos, sin, _log_gamma())
    return _ret_gate_fwd_call(o.reshape(B * Tp, -1), p.reshape(B * Tp, W))


def _retention_mixer_fwd(p, cos, sin):
    B, Tp, W = p.shape
    o = _ret_core_fwd_call(p, cos, sin, _log_gamma())
    return _ret_gate_fwd_call(o.reshape(B * Tp, -1), p.reshape(B * Tp, W)), (p, o, cos, sin)


def _retention_mixer_bwd(res, dy):
    p, o, cos, sin = res
    B, Tp, W = p.shape
    do, dg = _ret_gate_bwd_call(o.reshape(B * Tp, -1), p.reshape(B * Tp, W), dy)
    dq, dk, dv = _ret_core_bwd_call(p, do.reshape(B, Tp, -1), cos, sin, _log_gamma())
    return jnp.concatenate([dq, dk, dv, dg.reshape(B, Tp, -1)], axis=-1), None, None


retention_mixer.defvjp(_retention_mixer_fwd, _retention_mixer_bwd)


def _heads_to_lanes(w):
    K = w.shape[0]
    w = w.reshape(K, MLA_HEADS, MLA_NOPE + MLA_ROPE)
    return jnp.pad(w, ((0, 0), (0, 0), (0, HEAD_LANES - MLA_NOPE - MLA_ROPE))).reshape(K, MLA_HEADS * HEAD_LANES)


def _out_rows_to_lanes(w):
    N = w.shape[1]
    att = w[LRU_WIDTH:].reshape(MLA_HEADS, MLA_V, N)
    att = jnp.pad(att, ((0, 0), (HEAD_LANES - MLA_V, 0), (0, 0))).reshape(MLA_HEADS * HEAD_LANES, N)
    return jnp.concatenate([w[:LRU_WIDTH], att], axis=0)


def _local_loss(diff, wfull, tgt):
    x = diff["x"]
    B, S, D = x.shape
    T = S + N_META
    Tp = _round_up(T, SEQ_BLOCK)
    M = B * Tp
    pos = jnp.arange(Tp, dtype=jnp.int32)

    def mm(a, name, act=False, out_dtype=F32, layout=lambda w: w):
        return matmul(a, layout(wfull[name]), layout(diff[name]), act, name, out_dtype)

    meta = jnp.broadcast_to(diff["meta_tokens"][None], (B, N_META, D))
    h = jnp.concatenate([meta, x, jnp.zeros((B, Tp - T, D), F32)], axis=1).reshape(M, D)

    p = mm(h, "ev_w_in")
    sp = jax.nn.softplus(-diff["ev_lru_lambda"]).reshape(1, LRU_WIDTH)
    y_rec, qn, kvn, p_kpe = even_front(
        p.reshape(B, Tp, -1), diff["ev_conv_w"].reshape(CONV_WIDTH, LRU_WIDTH), diff["ev_conv_b"].reshape(1, LRU_WIDTH),
        diff["ev_w_rg_a"].reshape(LRU_HEADS, LRU_HEAD_DIM, LRU_HEAD_DIM), diff["ev_b_rg_a"].reshape(1, LRU_WIDTH),
        diff["ev_w_rg_x"].reshape(LRU_HEADS, LRU_HEAD_DIM, LRU_HEAD_DIM), diff["ev_b_rg_x"].reshape(1, LRU_WIDTH),
        sp, diff["ev_q_norm_g"].reshape(-1), diff["ev_kv_norm_g"].reshape(-1))
    y_rec = y_rec.reshape(M, LRU_WIDTH)
    q = mm(qn, "ev_w_uq", out_dtype=MXU_DTYPE, layout=_heads_to_lanes).reshape(B, Tp, -1)
    kv = mm(kvn, "ev_w_ukv", out_dtype=MXU_DTYPE).reshape(B, Tp, -1)
    kpe = jnp.pad(p_kpe.reshape(B, Tp, MLA_ROPE), ((0, 0), (0, 0), (MLA_NOPE, HEAD_LANES - MLA_NOPE - MLA_ROPE)))
    y_att = mla_attention(q, kv, kpe, _mla_rope_tables(pos)).reshape(M, -1)
    mix = mm(jnp.concatenate([y_rec, y_att], axis=-1), "ev_w_out", layout=_out_rows_to_lanes)
    h = deepnorm(h, mix, diff["ln_mix_g"][0], diff["ln_mix_b"][0], "ln_mix0")
    f = mlp(h, wfull["mlp_w1_0"], wfull["mlp_w2_0"], diff["mlp_w1_0"], diff["mlp_w2_0"], "mlp0")
    h = deepnorm(h, f, diff["ln_mlp_g"][0], diff["ln_mlp_b"][0], "ln_mlp0")

    p = mm(h, "od_w_in", out_dtype=MXU_DTYPE)
    cos, sin = _rope_tables(pos, RET_QK_DIM // 2)
    mix = mm(retention_mixer(p.reshape(B, Tp, -1), cos, sin), "od_w_out")
    h = deepnorm(h, mix, diff["ln_mix_g"][1], diff["ln_mix_b"][1], "ln_mix1")
    f = mlp(h, wfull["mlp_w1_1"], wfull["mlp_w2_1"], diff["mlp_w1_1"], diff["mlp_w2_1"], "mlp1")
    h = deepnorm(h, f, diff["ln_mlp_g"][1], diff["ln_mlp_b"][1], "ln_mlp1")

    y = h.reshape(B, Tp, D)[:, N_META:T].reshape(B * S, D)
    return loss_head(y, tgt.reshape(B * S, D))


_HBM = pl.BlockSpec(memory_space=pltpu.HBM)


def _place():
    return lax.axis_index("x"), lax.axis_index("y"), lax.axis_index("c")


def _other_chips(x, y):
    return [(1 - x, y), (x, 1 - y), (1 - x, 1 - y)]


def _chunks(rows, sublanes, most):
    for q in range(most, 0, -1):
        if rows % (q * sublanes) == 0:
            return q
    return 1


def _sublanes(dtype):
    return 8 * 4 // jnp.dtype(dtype).itemsize


def _allgather_chips(buf, name):
    R, C = buf.shape
    Rh = R // 2
    Q = _chunks(Rh, _sublanes(buf.dtype), 4)
    ch = Rh // Q
    QL = _chunks(R, _sublanes(buf.dtype), LOCAL_PIECES)
    lch = R // QL

    def body(x_ref, out_ref, send_sems, recv_sems, local_sems):
        x, y, c = _place()
        sibling = (x, y, 1 - c)
        chips = _other_chips(x, y)

        def piece(cx, cy, hc, q):
            return out_ref.at[2 * cx + cy, pl.ds(hc * Rh + q * ch, ch), :]

        def copy(k, src, dst, to):
            return pltpu.make_async_remote_copy(src_ref=src, dst_ref=dst, send_sem=send_sems.at[k], recv_sem=recv_sems.at[k],
                                                device_id=to, device_id_type=MESH)

        mine = [pltpu.make_async_copy(x_ref.at[pl.ds(q * lch, lch), :], out_ref.at[2 * x + y, pl.ds(q * lch, lch), :], local_sems.at[q])
                for q in range(QL)]
        for cp in mine:
            cp.start()
        first = [copy(j * Q + q, x_ref.at[pl.ds(c * Rh + q * ch, ch), :], piece(x, y, c, q), (*chip, c))
                 for q in range(Q) for j, chip in enumerate(chips)]
        for cp in first:
            cp.start()
        passed = []
        for q in range(Q):
            for j, chip in enumerate(chips):
                landed = piece(*chip, c, q)
                copy(j * Q + q, landed, landed, sibling).wait_recv()
                fwd = copy(3 * Q + j * Q + q, landed, landed, sibling)
                fwd.start()
                passed.append(fwd)
        for q in range(Q):
            for j, chip in enumerate(chips):
                theirs = piece(*chip, 1 - c, q)
                copy(3 * Q + j * Q + q, theirs, theirs, sibling).wait_recv()
        for cp in first + passed:
            cp.wait_send()
        for cp in mine:
            cp.wait()

    return pl.pallas_call(
        body, name=name, in_specs=[_HBM], out_specs=_HBM,
        out_shape=jax.ShapeDtypeStruct((N_CHIPS, R, C), buf.dtype),
        scratch_shapes=[pltpu.SemaphoreType.DMA((6 * Q,)), pltpu.SemaphoreType.DMA((6 * Q,)), pltpu.SemaphoreType.DMA((QL,))],
    )(buf)


def _sibling_exchange(p, name):
    _, R, C = p.shape
    Q = _chunks(R, _sublanes(p.dtype), 20)
    ch = R // Q

    def body(p_ref, recv_ref, send_sems, recv_sems):
        x, y, c = _place()
        copies = [pltpu.make_async_remote_copy(src_ref=p_ref.at[1 - c, pl.ds(q * ch, ch), :], dst_ref=recv_ref.at[pl.ds(q * ch, ch), :],
                                               send_sem=send_sems.at[q], recv_sem=recv_sems.at[q],
                                               device_id=(x, y, 1 - c), device_id_type=MESH) for q in range(Q)]
        for cp in copies:
            cp.start()
        for cp in copies:
            cp.wait()

    return pl.pallas_call(
        body, name=name, in_specs=[_HBM], out_specs=_HBM, out_shape=jax.ShapeDtypeStruct((R, C), p.dtype),
        scratch_shapes=[pltpu.SemaphoreType.DMA((Q,)), pltpu.SemaphoreType.DMA((Q,))],
    )(p)


def _chip_scatter(s, name):
    n, R, C = s.shape
    Q = _chunks(R, _sublanes(s.dtype), 5)
    ch = R // Q
    QL = _chunks(R, _sublanes(s.dtype), LOCAL_PIECES)
    lch = R // QL

    def body(s_ref, t_ref, send_sems, recv_sems, local_sems):
        x, y, c = _place()
        my = 2 * x + y
        loc = [pltpu.make_async_copy(s_ref.at[my, pl.ds(q * lch, lch), :], t_ref.at[my, pl.ds(q * lch, lch), :], local_sems.at[q])
               for q in range(QL)]
        for cp in loc:
            cp.start()
        copies = [pltpu.make_async_remote_copy(src_ref=s_ref.at[2 * cx + cy, pl.ds(q * ch, ch), :],
                                               dst_ref=t_ref.at[my, pl.ds(q * ch, ch), :], send_sem=send_sems.at[j * Q + q],
                                               recv_sem=recv_sems.at[j * Q + q], device_id=(cx, cy, c), device_id_type=MESH)
                  for q in range(Q) for j, (cx, cy) in enumerate(_other_chips(x, y))]
        for cp in copies:
            cp.start()
        for cp in copies + loc:
            cp.wait()

    return pl.pallas_call(
        body, name=name, in_specs=[_HBM], out_specs=_HBM, out_shape=jax.ShapeDtypeStruct(s.shape, s.dtype),
        scratch_shapes=[pltpu.SemaphoreType.DMA((3 * Q,)), pltpu.SemaphoreType.DMA((3 * Q,)), pltpu.SemaphoreType.DMA((QL,))],
    )(s)


def _sibling_gather(f, name):
    R, C = f.shape
    Q = _chunks(R, _sublanes(f.dtype), 10)
    ch = R // Q

    def body(f_ref, out_ref, send_sems, recv_sems, local_sems):
        x, y, c = _place()
        loc = [pltpu.make_async_copy(f_ref.at[pl.ds(q * ch, ch), :], out_ref.at[c, pl.ds(q * ch, ch), :], local_sems.at[q])
               for q in range(Q)]
        for cp in loc:
            cp.start()
        copies = [pltpu.make_async_remote_copy(src_ref=f_ref.at[pl.ds(q * ch, ch), :], dst_ref=out_ref.at[c, pl.ds(q * ch, ch), :],
                                               send_sem=send_sems.at[q], recv_sem=recv_sems.at[q],
                                               device_id=(x, y, 1 - c), device_id_type=MESH) for q in range(Q)]
        for cp in copies:
            cp.start()
        for cp in copies + loc:
            cp.wait()

    return pl.pallas_call(
        body, name=name, in_specs=[_HBM], out_specs=_HBM, out_shape=jax.ShapeDtypeStruct((2, R, C), f.dtype),
        scratch_shapes=[pltpu.SemaphoreType.DMA((Q,)), pltpu.SemaphoreType.DMA((Q,)), pltpu.SemaphoreType.DMA((Q,))],
    )(f)


def _add_own_half(p, got, name):
    _, R, C = p.shape
    tr = _pick(R, 512, 8)

    def body(c_ref, p_ref, g_ref, o_ref):
        o_ref[...] = p_ref[...] + g_ref[...]

    grid_spec = pltpu.PrefetchScalarGridSpec(
        num_scalar_prefetch=1, grid=(R // tr,),
        in_specs=[pl.BlockSpec((None, tr, C), lambda i, c_ref: (c_ref[0], i, 0)), pl.BlockSpec((tr, C), lambda i, c_ref: (i, 0))],
        out_specs=pl.BlockSpec((tr, C), lambda i, c_ref: (i, 0)))
    return pl.pallas_call(body, name=name, grid_spec=grid_spec, out_shape=jax.ShapeDtypeStruct((R, C), p.dtype),
                          compiler_params=pltpu.CompilerParams(dimension_semantics=("parallel",)))(
        lax.axis_index("c").astype(jnp.int32).reshape(1), p, got)


def _sum_slots(t, name):
    n, R, C = t.shape
    tr = _pick(R, 512, 8)

    def body(t_ref, o_ref):
        acc = t_ref[0]
        for j in range(1, n):
            acc = acc + t_ref[j]
        o_ref[...] = acc

    return pl.pallas_call(body, name=name, grid=(R // tr,), in_specs=[pl.BlockSpec((n, tr, C), lambda i: (0, i, 0))],
                          out_specs=pl.BlockSpec((tr, C), lambda i: (i, 0)), out_shape=jax.ShapeDtypeStruct((R, C), t.dtype),
                          compiler_params=pltpu.CompilerParams(dimension_semantics=("parallel",)))(t)


def _reduce_to_chips(p):
    _, n, R, C = p.shape
    p = p.reshape(2, n * R, C)
    got = _sibling_exchange(p, "grad_sibling_exchange")
    s = _add_own_half(p, got, "grad_sibling_add").reshape(n, R, C)
    t = _chip_scatter(s, "grad_chip_scatter")
    f = _sum_slots(t, "grad_chip_sum")
    return _sibling_gather(f, "grad_sibling_gather")


def _adamw(w, g, m, v, name):
    R, C = w.shape
    tr = _pick(R, 256, 8)

    def body(w_ref, g_ref, m_ref, v_ref, d_ref, nm_ref, nv_ref):
        g_ = g_ref[...]
        m_ = ADAM_B1 * m_ref[...] + (1.0 - ADAM_B1) * g_
        v_ = ADAM_B2 * v_ref[...] + (1.0 - ADAM_B2) * (g_ * g_)
        m_hat = m_ / (1.0 - ADAM_B1 ** ADAM_STEP)
        v_hat = v_ / (1.0 - ADAM_B2 ** ADAM_STEP)
        d_ref[...] = -ADAM_LR * (m_hat / (jnp.sqrt(v_hat) + ADAM_EPS) + ADAM_WD * w_ref[...])
        nm_ref[...] = m_
        nv_ref[...] = v_

    row = pl.BlockSpec((tr, C), lambda i: (i, 0))
    shp = jax.ShapeDtypeStruct((R, C), F32)
    return pl.pallas_call(body, name=name, grid=(R // tr,), in_specs=[row] * 4, out_specs=[row] * 3, out_shape=[shp] * 3,
                          compiler_params=pltpu.CompilerParams(dimension_semantics=("parallel",)))(w, g, m, v)


BIG_SPECS = (("ev_w_in", 1024, 1440, 1), ("ev_w_uq", 256, 768, 1), ("ev_w_ukv", 128, 1024, 1), ("ev_w_out", 1024, 1024, 0),
             ("od_w_in", 1024, 6144, 1), ("od_w_out", 2048, 1024, 0), ("mlp_w1_0", 1024, 4096, 1), ("mlp_w1_1", 1024, 4096, 1),
             ("mlp_w2_0", 4096, 1024, 0), ("mlp_w2_1", 4096, 1024, 0))
BIG_PARAMS = (("ev_w_in", ("ev_w_in",)), ("ev_w_uq", ("ev_w_uq",)), ("ev_w_ukv", ("ev_w_ukv",)), ("ev_w_out", ("ev_w_out",)),
              ("od_w_in", ("od_w_in",)), ("od_w_out", ("od_w_out",)), ("mlp_w1", ("mlp_w1_0", "mlp_w1_1")),
              ("mlp_w2", ("mlp_w2_0", "mlp_w2_1")))
REPLICATED = ("ev_conv_b", "ev_w_rg_a", "ev_b_rg_a", "ev_w_rg_x", "ev_b_rg_x", "ev_lru_lambda", "ev_q_norm_g", "ev_kv_norm_g",
              "ln_mix_g", "ln_mix_b", "ln_mlp_g", "ln_mlp_b")
SMALL_SHARDED = ("meta_tokens", "ev_conv_w")
WEIGHT_NAMES = ("meta_tokens", "ev_w_in", "ev_conv_w", "ev_conv_b", "ev_w_rg_a", "ev_b_rg_a", "ev_w_rg_x", "ev_b_rg_x",
                "ev_lru_lambda", "ev_q_norm_g", "ev_w_uq", "ev_kv_norm_g", "ev_w_ukv", "ev_w_out", "od_w_in", "od_w_out",
                "ln_mix_g", "ln_mix_b", "mlp_w1", "mlp_w2", "ln_mlp_g", "ln_mlp_b")


def _to_rows(flat, row_align):
    n = flat.shape[-1]
    rows = _round_up(-(-n // PACK_COLS), row_align)
    pad = rows * PACK_COLS - n
    if pad:
        flat = jnp.pad(flat, [(0, 0)] * (flat.ndim - 1) + [(0, pad)])
    return flat.reshape(flat.shape[:-1] + (rows, PACK_COLS))


def _shard_shape(K, N, axis):
    return (K // N_CHIPS, N) if axis == 0 else (K, N // N_CHIPS)


def _gather_shards(stacked, K, N, axis):
    if axis == 0:
        return stacked.reshape(K, N)
    return stacked.transpose(1, 0, 2).reshape(K, N)


def _split_shards(full, K, N, axis):
    if axis == 0:
        return full.reshape(N_CHIPS, -1)
    return full.reshape(K, N_CHIPS, N // N_CHIPS).transpose(1, 0, 2).reshape(N_CHIPS, -1)


def kernel(x, meta_tokens, ev_w_in, ev_conv_w, ev_conv_b, ev_w_rg_a, ev_b_rg_a, ev_w_rg_x, ev_b_rg_x, ev_lru_lambda, ev_q_norm_g, ev_w_uq, ev_kv_norm_g, ev_w_ukv, ev_w_out, od_w_in, od_w_out, ln_mix_g, ln_mix_b, mlp_w1, mlp_w2, ln_mlp_g, ln_mlp_b, loss_target, m_meta_tokens, m_ev_w_in, m_ev_conv_w, m_ev_conv_b, m_ev_w_rg_a, m_ev_b_rg_a, m_ev_w_rg_x, m_ev_b_rg_x, m_ev_lru_lambda, m_ev_q_norm_g, m_ev_w_uq, m_ev_kv_norm_g, m_ev_w_ukv, m_ev_w_out, m_od_w_in, m_od_w_out, m_ln_mix_g, m_ln_mix_b, m_mlp_w1, m_mlp_w2, m_ln_mlp_g, m_ln_mlp_b, v_meta_tokens, v_ev_w_in, v_ev_conv_w, v_ev_conv_b, v_ev_w_rg_a, v_ev_b_rg_a, v_ev_w_rg_x, v_ev_b_rg_x, v_ev_lru_lambda, v_ev_q_norm_g, v_ev_w_uq, v_ev_kv_norm_g, v_ev_w_ukv, v_ev_w_out, v_od_w_in, v_od_w_out, v_ln_mix_g, v_ln_mix_b, v_mlp_w1, v_mlp_w2, v_ln_mlp_g, v_ln_mlp_b):
    given = dict(locals())
    local_big = {"ev_w_in": ev_w_in[0], "ev_w_uq": ev_w_uq[0], "ev_w_ukv": ev_w_ukv[0], "ev_w_out": ev_w_out[0],
                 "od_w_in": od_w_in[0], "od_w_out": od_w_out[0], "mlp_w1_0": mlp_w1[0], "mlp_w1_1": mlp_w1[1],
                 "mlp_w2_0": mlp_w2[0], "mlp_w2_1": mlp_w2[1]}

    sizes = [math.prod(_shard_shape(K, N, ax)) for _, K, N, ax in BIG_SPECS]
    packed = _to_rows(jnp.concatenate([local_big[n].astype(MXU_DTYPE).reshape(-1) for n, _, _, _ in BIG_SPECS]), 256)
    gathered = _allgather_chips(packed, "weight_allgather").reshape(N_CHIPS, -1)
    wfull, off = {}, 0
    for (n, K, N, ax), sz in zip(BIG_SPECS, sizes):
        wfull[n] = _gather_shards(gathered[:, off:off + sz].reshape((N_CHIPS,) + _shard_shape(K, N, ax)), K, N, ax)
        off += sz
    small = _to_rows(jnp.concatenate([meta_tokens.reshape(-1), ev_conv_w.reshape(-1)]), 16)
    small = _allgather_chips(small, "small_allgather").reshape(N_CHIPS, -1)
    n_meta, n_conv = meta_tokens.size, ev_conv_w.size
    meta_full = _gather_shards(small[:, :n_meta].reshape(N_CHIPS, N_META, D_MODEL // N_CHIPS), N_META, D_MODEL, 1)
    conv_full = _gather_shards(small[:, n_meta:n_meta + n_conv].reshape(N_CHIPS, CONV_WIDTH, LRU_WIDTH // N_CHIPS),
                               CONV_WIDTH, LRU_WIDTH, 1)

    diff = {n: jnp.zeros((K, N), F32) for n, K, N, _ in BIG_SPECS}
    diff.update({n: given[n] for n in REPLICATED})
    diff.update(x=x, meta_tokens=meta_full, ev_conv_w=conv_full)
    loss, g = jax.value_and_grad(_local_loss)(diff, wfull, loss_target)
    loss = lax.psum(loss, ("x", "y", "c"))

    repl = jnp.concatenate([g[n].reshape(-1) for n in REPLICATED]).reshape(N_CHIPS, -1)
    pieces = [_split_shards(g[n], K, N, ax) for n, K, N, ax in BIG_SPECS]
    pieces += [_split_shards(g["meta_tokens"], N_META, D_MODEL, 1), _split_shards(g["ev_conv_w"], CONV_WIDTH, LRU_WIDTH, 1), repl]
    half_len = sum(pc.shape[1] for pc in pieces) // 2
    rows_half = _round_up(-(-half_len // PACK_COLS), 8)
    cut = [pc.reshape(N_CHIPS, 2, -1).transpose(1, 0, 2) for pc in pieces]
    cut.append(jnp.zeros((2, N_CHIPS, rows_half * PACK_COLS - half_len), F32))
    p = jnp.concatenate(cut, axis=2).reshape(2, N_CHIPS, rows_half, PACK_COLS)
    red = _reduce_to_chips(p).reshape(2, -1)

    def take(off, sz):
        return jnp.concatenate([red[0, off // 2:(off + sz) // 2], red[1, off // 2:(off + sz) // 2]])

    grads, off = {}, 0
    for name, parts in BIG_PARAMS:
        shp = given[name].shape
        per = given[name].size // len(parts)
        grads[name] = jnp.concatenate([take(off + i * per, per) for i in range(len(parts))]).reshape(shp)
        off += given[name].size
    for name in SMALL_SHARDED:
        sz = given[name].size
        grads[name] = take(off, sz).reshape(given[name].shape)
        off += sz
    n_repl = repl.shape[1]
    repl_all = _allgather_chips(_to_rows(take(off, n_repl), 16), "replicated_allgather").reshape(N_CHIPS, -1)[:, :n_repl].reshape(-1)
    off = 0
    for name in REPLICATED:
        sz = given[name].size
        grads[name] = repl_all[off:off + sz].reshape(given[name].shape)
        off += sz

    delta, new_m, new_v = {}, {}, {}
    for name, _ in BIG_PARAMS:
        shp = given[name].shape
        two_d = (-1, shp[-1])
        d, nm, nv = _adamw(given[name].reshape(two_d), grads[name].reshape(two_d), given["m_" + name].reshape(two_d),
                           given["v_" + name].reshape(two_d), "adamw_" + name)
        delta[name], new_m[name], new_v[name] = d.reshape(shp), nm.reshape(shp), nv.reshape(shp)
    smalls = SMALL_SHARDED + REPLICATED

    def pack_small(get):
        return _to_rows(jnp.concatenate([get(n).reshape(-1) for n in smalls]), 8)

    outs = _adamw(pack_small(lambda n: given[n]), pack_small(lambda n: grads[n]), pack_small(lambda n: given["m_" + n]),
                  pack_small(lambda n: given["v_" + n]), "adamw_small")
    for res, flat in zip((delta, new_m, new_v), outs):
        flat, off = flat.reshape(-1), 0
        for n in smalls:
            sz = given[n].size
            res[n] = flat[off:off + sz].reshape(given[n].shape)
            off += sz

    return (loss, g["x"], *[grads[n] for n in WEIGHT_NAMES], *[delta[n] for n in WEIGHT_NAMES],
            *[new_m[n] for n in WEIGHT_NAMES], *[new_v[n] for n in WEIGHT_NAMES])
```

```python
import functools
import math

import jax
import jax.numpy as jnp
from jax import lax
from jax.experimental import pallas as pl
from jax.experimental.pallas import tpu as pltpu

F32 = jnp.float32
MXU_DTYPE = jnp.bfloat16

D_MODEL = 1024
N_META = 16
LRU_WIDTH = 512
LRU_HEADS = 4
LRU_HEAD_DIM = 128
CONV_WIDTH = 4
LRU_C = 8.0
MLA_HEADS = 8
MLA_NOPE = 64
MLA_ROPE = 32
MLA_V = 64
MLA_Q_RANK = 256
MLA_KV_RANK = 128
RET_HEADS = 4
RET_QK_DIM = 256
RET_V_DIM = 512
D_FF = 4096
ROPE_BASE = 10000.0
DN_ALPHA = 4.0 ** 0.25
EPS = 1e-5
NEG_INF = -1e30
SEQ_BLOCK = 128

ADAM_LR = 0.001
ADAM_B1 = 0.9
ADAM_B2 = 0.999
ADAM_EPS = 1e-08
ADAM_WD = 0.01
ADAM_STEP = 10

PACK_COLS = 1024
N_CHIPS = 4

MESH = pl.DeviceIdType.MESH


def _pick(n, target, align):
    best = None
    for t in range(align, min(n, target) + 1, align):
        if n % t == 0:
            best = t
    return n if best is None else best


def _round_up(n, m):
    return (n + m - 1) // m * m


def _relu2(a):
    r = jnp.maximum(a, 0.0)
    return r * r


def _mm_nn(a, w, act, name, out_dtype=F32):
    M, K = a.shape
    _, N = w.shape
    tm = _pick(M, 1088 if K * a.dtype.itemsize <= 4096 else 544, 8)
    tn = _pick(N, 1024, 128)

    def body(a_ref, w_ref, o_ref):
        av = a_ref[...]
        if act:
            av = _relu2(av.astype(F32))
        o_ref[...] = jnp.dot(av.astype(MXU_DTYPE), w_ref[...].astype(MXU_DTYPE), preferred_element_type=F32).astype(out_dtype)

    return pl.pallas_call(
        body, name=name,
        grid=(M // tm, N // tn),
        in_specs=[pl.BlockSpec((tm, K), lambda i, j: (i, 0)), pl.BlockSpec((K, tn), lambda i, j: (0, j))],
        out_specs=pl.BlockSpec((tm, tn), lambda i, j: (i, j)),
        out_shape=jax.ShapeDtypeStruct((M, N), out_dtype),
        compiler_params=pltpu.CompilerParams(dimension_semantics=("parallel", "arbitrary")),
    )(a, w)


def _mm_nt(g, w, a_src, name, out_dtype=F32):
    M, N = g.shape
    K, _ = w.shape
    tk = N if N * g.dtype.itemsize <= 8192 else _pick(N, 2048, 128)
    nk = N // tk
    tm = _pick(M, 1088 if tk * g.dtype.itemsize <= 4096 else 544, 8)
    tn = _pick(K, 1024, 128)
    has_src = a_src is not None
    assert nk == 1 or out_dtype == F32

    def body(*refs):
        if has_src:
            g_ref, w_ref, s_ref, o_ref = refs
        else:
            g_ref, w_ref, o_ref = refs
        r = lax.dot_general(g_ref[...].astype(MXU_DTYPE), w_ref[...].astype(MXU_DTYPE),
                            (((1,), (1,)), ((), ())), preferred_element_type=F32)
        if has_src:
            r = r * (2.0 * jnp.maximum(s_ref[...].astype(F32), 0.0))
        if nk == 1:
            o_ref[...] = r.astype(out_dtype)
        else:
            k = pl.program_id(2)

            @pl.when(k == 0)
            def _():
                o_ref[...] = r

            @pl.when(k > 0)
            def _():
                o_ref[...] += r

    in_specs = [pl.BlockSpec((tm, tk), lambda i, j, k: (i, k)), pl.BlockSpec((tn, tk), lambda i, j, k: (j, k))]
    args = [g, w]
    if has_src:
        assert nk == 1
        in_specs.append(pl.BlockSpec((tm, tn), lambda i, j, k: (i, j)))
        args.append(a_src)
    return pl.pallas_call(
        body, name=name,
        grid=(M // tm, K // tn, nk),
        in_specs=in_specs,
        out_specs=pl.BlockSpec((tm, tn), lambda i, j, k: (i, j)),
        out_shape=jax.ShapeDtypeStruct((M, K), out_dtype),
        compiler_params=pltpu.CompilerParams(dimension_semantics=("parallel", "parallel", "arbitrary")),
    )(*args)


def _mm_tn(a, g, act, name, col_shards=1):
    M, K = a.shape
    _, N = g.shape
    n = N // col_shards
    tm, tn, tk = _pick(K, 1024, 128), _pick(n, 1024, 128), _pick(M, 1088, 8)
    nk = M // tk
    per = n // tn

    def body(a_ref, g_ref, o_ref):
        k = pl.program_id(2)
        av = a_ref[...]
        if act:
            av = _relu2(av.astype(F32))
        r = lax.dot_general(av.astype(MXU_DTYPE), g_ref[...].astype(MXU_DTYPE),
                            (((0,), (0,)), ((), ())), preferred_element_type=F32)

        @pl.when(k == 0)
        def _():
            o_ref[...] = r

        @pl.when(k > 0)
        def _():
            o_ref[...] += r

    if col_shards == 1:
        out_spec, out_shape = pl.BlockSpec((tm, tn), lambda i, j, k: (i, j)), (K, N)
    else:
        out_spec, out_shape = pl.BlockSpec((None, tm, tn), lambda i, j, k: (j // per, i, j % per)), (col_shards, K, n)
    return pl.pallas_call(
        body, name=name,
        grid=(K // tm, N // tn, nk),
        in_specs=[pl.BlockSpec((tk, tm), lambda i, j, k: (k, i)), pl.BlockSpec((tk, tn), lambda i, j, k: (k, j))],
        out_specs=out_spec,
        out_shape=jax.ShapeDtypeStruct(out_shape, F32),
        compiler_params=pltpu.CompilerParams(dimension_semantics=("parallel", "parallel", "arbitrary")),
    )(a, g)


@functools.partial(jax.custom_vjp, nondiff_argnums=(3, 4, 5, 6))
def matmul(a, w, w_grad_slot, act, name, out_dtype, col_shards):
    return _mm_nn(a, w, act, name + "_fwd", out_dtype)


def _matmul_fwd(a, w, w_grad_slot, act, name, out_dtype, col_shards):
    return _mm_nn(a, w, act, name + "_fwd", out_dtype), (a, w)


def _matmul_bwd(act, name, out_dtype, col_shards, res, g):
    a, w = res
    da = _mm_nt(g, w, a if act else None, name + "_dx")
    dw = _mm_tn(a, g, act, name + "_dw", col_shards)
    return da, None, dw


matmul.defvjp(_matmul_fwd, _matmul_bwd)


@functools.partial(jax.custom_vjp, nondiff_argnums=(5,))
def mlp(h, w1, w2, w1_grad_slot, w2_grad_slot, name):
    u = _mm_nn(h, w1, False, name + "_w1_fwd", out_dtype=MXU_DTYPE)
    return _mm_nn(u, w2, True, name + "_w2_fwd")


def _mlp_fwd(h, w1, w2, w1_grad_slot, w2_grad_slot, name):
    u = _mm_nn(h, w1, False, name + "_w1_fwd", out_dtype=MXU_DTYPE)
    return _mm_nn(u, w2, True, name + "_w2_fwd"), (h, u, w1, w2)


def _mlp_bwd(name, res, df):
    h, u, w1, w2 = res
    du = _mm_nt(df, w2, u, name + "_w2_dx", out_dtype=MXU_DTYPE)
    dw2 = _mm_tn(u, df, True, name + "_w2_dw")
    dh = _mm_nt(du, w1, None, name + "_w1_dx")
    dw1 = _mm_tn(h, du, False, name + "_w1_dw", N_CHIPS)
    return dh, None, None, dw1, dw2


mlp.defvjp(_mlp_fwd, _mlp_bwd)


def _ln_stats(z):
    mu = jnp.mean(z, axis=-1, keepdims=True)
    zc = z - mu
    var = jnp.mean(zc * zc, axis=-1, keepdims=True)
    return zc, lax.rsqrt(var + EPS)


def _ln_fwd_call(resid, branch, g, b, name):
    M, D = resid.shape
    tm = _pick(M, 544, 8)

    def body(r_ref, br_ref, g_ref, b_ref, o_ref):
        zc, rstd = _ln_stats(DN_ALPHA * r_ref[...] + br_ref[...])
        o_ref[...] = zc * rstd * g_ref[...] + b_ref[...]

    row = pl.BlockSpec((tm, D), lambda i: (i, 0))
    vec = pl.BlockSpec((1, D), lambda i: (0, 0))
    return pl.pallas_call(
        body, name=name, grid=(M // tm,), in_specs=[row, row, vec, vec], out_specs=row,
        out_shape=jax.ShapeDtypeStruct((M, D), F32),
        compiler_params=pltpu.CompilerParams(dimension_semantics=("parallel",)),
    )(resid, branch, g.reshape(1, D), b.reshape(1, D))


def _ln_bwd_call(resid, branch, g, dy, name):
    M, D = resid.shape
    tm = _pick(M, 544, 8)

    def body(r_ref, br_ref, g_ref, dy_ref, dz_ref, dg_ref, db_ref):
        @pl.when(pl.program_id(0) == 0)
        def _():
            dg_ref[...] = jnp.zeros_like(dg_ref)
            db_ref[...] = jnp.zeros_like(db_ref)

        zc, rstd = _ln_stats(DN_ALPHA * r_ref[...] + br_ref[...])
        xhat = zc * rstd
        dy = dy_ref[...]
        dxh = dy * g_ref[...]
        m1 = jnp.mean(dxh, axis=-1, keepdims=True)
        m2 = jnp.mean(dxh * xhat, axis=-1, keepdims=True)
        dz_ref[...] = rstd * (dxh - m1 - xhat * m2)
        dg_ref[...] += jnp.sum(dy * xhat, axis=0, keepdims=True)
        db_ref[...] += jnp.sum(dy, axis=0, keepdims=True)

    row = pl.BlockSpec((tm, D), lambda i: (i, 0))
    vec = pl.BlockSpec((1, D), lambda i: (0, 0))
    return pl.pallas_call(
        body, name=name, grid=(M // tm,), in_specs=[row, row, vec, row], out_specs=[row, vec, vec],
        out_shape=[jax.ShapeDtypeStruct((M, D), F32), jax.ShapeDtypeStruct((1, D), F32), jax.ShapeDtypeStruct((1, D), F32)],
        compiler_params=pltpu.CompilerParams(dimension_semantics=("arbitrary",)),
    )(resid, branch, g.reshape(1, D), dy)


@functools.partial(jax.custom_vjp, nondiff_argnums=(4,))
def deepnorm(resid, branch, g, b, name):
    return _ln_fwd_call(resid, branch, g, b, name + "_fwd")


def _deepnorm_fwd(resid, branch, g, b, name):
    return _ln_fwd_call(resid, branch, g, b, name + "_fwd"), (resid, branch, g)


def _deepnorm_bwd(name, res, dy):
    resid, branch, g = res
    dz, dg, db = _ln_bwd_call(resid, branch, g, dy, name + "_bwd")
    return DN_ALPHA * dz, dz, dg.reshape(g.shape), db.reshape(g.shape)


deepnorm.defvjp(_deepnorm_fwd, _deepnorm_bwd)


def _rms_fwd_call(x, g, name, col_block=0):
    R = x.shape[0]
    W = g.shape[-1]
    tr = _pick(R, 1088, 8)

    def body(x_ref, g_ref, o_ref):
        xv = x_ref[...]
        rstd = lax.rsqrt(jnp.mean(xv * xv, axis=-1, keepdims=True) + EPS)
        o_ref[...] = xv * rstd * g_ref[...]

    vec = pl.BlockSpec((1, W), lambda i: (0, 0))
    return pl.pallas_call(
        body, name=name, grid=(R // tr,), in_specs=[pl.BlockSpec((tr, W), lambda i: (i, col_block)), vec],
        out_specs=pl.BlockSpec((tr, W), lambda i: (i, 0)), out_shape=jax.ShapeDtypeStruct((R, W), F32),
        compiler_params=pltpu.CompilerParams(dimension_semantics=("parallel",)),
    )(x, g.reshape(1, W))


def _rms_bwd_call(x, g, dy, name, col_block=0):
    R = x.shape[0]
    W = g.shape[-1]
    tr = _pick(R, 1088, 8)

    def body(x_ref, g_ref, dy_ref, dx_ref, dg_ref):
        @pl.when(pl.program_id(0) == 0)
        def _():
            dg_ref[...] = jnp.zeros_like(dg_ref)

        xv = x_ref[...]
        rstd = lax.rsqrt(jnp.mean(xv * xv, axis=-1, keepdims=True) + EPS)
        xhat = xv * rstd
        dy = dy_ref[...]
        dxh = dy * g_ref[...]
        dx_ref[...] = rstd * (dxh - xhat * jnp.mean(dxh * xhat, axis=-1, keepdims=True))
        dg_ref[...] += jnp.sum(dy * xhat, axis=0, keepdims=True)

    row = pl.BlockSpec((tr, W), lambda i: (i, 0))
    vec = pl.BlockSpec((1, W), lambda i: (0, 0))
    return pl.pallas_call(
        body, name=name, grid=(R // tr,), in_specs=[pl.BlockSpec((tr, W), lambda i: (i, col_block)), vec, row], out_specs=[row, vec],
        out_shape=[jax.ShapeDtypeStruct((R, W), F32), jax.ShapeDtypeStruct((1, W), F32)],
        compiler_params=pltpu.CompilerParams(dimension_semantics=("arbitrary",)),
    )(x, g.reshape(1, W), dy)


def _loss_call(y, tgt, name):
    R, D = y.shape
    tr = _pick(R, 512, 8)

    def body(y_ref, t_ref, dy_ref, acc_ref):
        @pl.when(pl.program_id(0) == 0)
        def _():
            acc_ref[...] = jnp.zeros_like(acc_ref)

        e = y_ref[...] - t_ref[...]
        dy_ref[...] = e * (1.0 / D)
        acc_ref[...] += jnp.sum(jnp.sum(e * e, axis=-1, keepdims=True), axis=0, keepdims=True) * (0.5 / D)

    row = pl.BlockSpec((tr, D), lambda i: (i, 0))
    one = pl.BlockSpec((1, 1), lambda i: (0, 0))
    return pl.pallas_call(
        body, name=name, grid=(R // tr,), in_specs=[row, row], out_specs=[row, one],
        out_shape=[jax.ShapeDtypeStruct((R, D), F32), jax.ShapeDtypeStruct((1, 1), F32)],
        compiler_params=pltpu.CompilerParams(dimension_semantics=("arbitrary",)),
    )(y, tgt)


@jax.custom_vjp
def loss_head(y, tgt):
    return _loss_call(y, tgt, "loss_head")[1][0, 0]


def _loss_head_fwd(y, tgt):
    dy, acc = _loss_call(y, tgt, "loss_head")
    return acc[0, 0], dy


def _loss_head_bwd(dy, ct):
    return ct * dy, None


loss_head.defvjp(_loss_head_fwd, _loss_head_bwd)


_GELU_C = math.sqrt(2.0 / math.pi)


def _gelu_parts(x):
    x2 = x * x
    t = jnp.tanh(_GELU_C * (x + 0.044715 * x * x2))
    gelu = 0.5 * x * (1.0 + t)
    dgelu = 0.5 * (1.0 + t) + 0.5 * x * (1.0 - t * t) * (_GELU_C * (1.0 + 3.0 * 0.044715 * x2))
    return gelu, dgelu


def _sigmoid(x):
    return 1.0 / (1.0 + jnp.exp(-x))


def _scan8(a, b, carry, reverse):
    row = lax.broadcasted_iota(jnp.int32, a.shape, 0)
    for s in (1, 2, 4):
        shift = 8 - s if reverse else s
        keep = (row < 8 - s) if reverse else (row >= s)
        b = jnp.where(keep, a * pltpu.roll(b, shift, 0) + b, b)
        a = jnp.where(keep, a * pltpu.roll(a, shift, 0), a)
    return a * carry + b


def _lru_pre(prec_ref, prev_ref, first, cw_ref, cb_ref, wa_ref, ba_ref, wx_ref, bx_ref, sp_ref):
    tc = prec_ref.shape[0]
    prev = jnp.where(first, 0.0, prev_ref[...])
    ext = jnp.concatenate([prev, prec_ref[...]], axis=0)
    cw = cw_ref[...]
    taps = [ext[8:] if k == CONV_WIDTH - 1 else pltpu.roll(ext, CONV_WIDTH - 1 - k, 0)[8:] for k in range(CONV_WIDTH)]
    xc = cb_ref[...] + sum(cw[k:k + 1, :] * taps[k] for k in range(CONV_WIDTH))
    ga, gx = [], []
    for h in range(LRU_HEADS):
        xh = xc[:, h * LRU_HEAD_DIM:(h + 1) * LRU_HEAD_DIM].astype(MXU_DTYPE)
        ga.append(jnp.dot(xh, wa_ref[h].astype(MXU_DTYPE), preferred_element_type=F32))
        gx.append(jnp.dot(xh, wx_ref[h].astype(MXU_DTYPE), preferred_element_type=F32))
    r = _sigmoid(jnp.concatenate(ga, axis=1) + ba_ref[...])
    i = _sigmoid(jnp.concatenate(gx, axis=1) + bx_ref[...])
    log_a = -LRU_C * r * sp_ref[...]
    a = jnp.exp(log_a)
    a2 = a * a
    mult = jnp.sqrt(-jnp.tanh(log_a) * (a2 + 1.0))
    return taps, xc, r, i, a, a2, mult


def _lru_fwd_call(p, cw, cb, wa, ba, wx, bx, sp):
    B, Tp, _ = p.shape
    W = LRU_WIDTH
    tc = SEQ_BLOCK
    nc = Tp // tc

    def body(pg_ref, prec_ref, prev_ref, cw_ref, cb_ref, wa_ref, ba_ref, wx_ref, bx_ref, sp_ref, y_ref, h_ref, carry_ref):
        first = pl.program_id(1) == 0

        @pl.when(first)
        def _():
            carry_ref[...] = jnp.zeros_like(carry_ref)

        _, xc, r, i, a, a2, mult = _lru_pre(prec_ref, prev_ref, first, cw_ref, cb_ref, wa_ref, ba_ref, wx_ref, bx_ref, sp_ref)
        b = mult * (i * xc)
        carry = carry_ref[0:1, :]
        for t in range(tc // 8):
            h = _scan8(a[8 * t:8 * t + 8], b[8 * t:8 * t + 8], carry, False)
            h_ref[8 * t:8 * t + 8, :] = h
            carry = h[7:8, :]
        carry_ref[...] = jnp.broadcast_to(carry, carry_ref.shape)
        y_ref[...] = h_ref[...] * _gelu_parts(pg_ref[...])[0]

    cur = pl.BlockSpec((None, tc, W), lambda b, j: (b, j, 0))
    rec = pl.BlockSpec((None, tc, W), lambda b, j: (b, j, 1))
    prev = pl.BlockSpec((None, 8, W), lambda b, j: (b, jnp.maximum(j * (tc // 8) - 1, 0), 1))
    vec = pl.BlockSpec((1, W), lambda b, j: (0, 0))
    cws = pl.BlockSpec((CONV_WIDTH, W), lambda b, j: (0, 0))
    wsp = pl.BlockSpec((LRU_HEADS, LRU_HEAD_DIM, LRU_HEAD_DIM), lambda b, j: (0, 0, 0))
    return pl.pallas_call(
        body, name="lru_fwd", grid=(B, nc),
        in_specs=[cur, rec, prev, cws, vec, wsp, vec, wsp, vec, vec],
        out_specs=[cur, cur],
        out_shape=[jax.ShapeDtypeStruct((B, Tp, W), F32), jax.ShapeDtypeStruct((B, Tp, W), F32)],
        scratch_shapes=[pltpu.VMEM((8, W), F32)],
        compiler_params=pltpu.CompilerParams(dimension_semantics=("arbitrary", "arbitrary")),
    )(p, p, p, cw, cb, wa, ba, wx, bx, sp)


def _lru_bwd_call(p, hseq, dy, cw, cb, wa, ba, wx, bx, sp):
    B, Tp, _ = p.shape
    W = LRU_WIDTH
    tc = SEQ_BLOCK
    nc = Tp // tc
    HD = LRU_HEAD_DIM

    def body(pg_ref, prec_ref, prev_ref, h_ref, hprev_ref, dy_ref, cw_ref, cb_ref, wa_ref, ba_ref, wx_ref, bx_ref, sp_ref,
             dpg_ref, dprec_ref, dcw_ref, dcb_ref, dwa_ref, dba_ref, dwx_ref, dbx_ref, dsp_ref,
             gcar_ref, anext_ref, halo_ref, g_ref):
        j = pl.program_id(1)
        first = j == nc - 1
        last = j == 0

        @pl.when(jnp.logical_and(pl.program_id(0) == 0, last))
        def _():
            for ref in (dcw_ref, dcb_ref, dwa_ref, dba_ref, dwx_ref, dbx_ref, dsp_ref):
                ref[...] = jnp.zeros_like(ref)

        @pl.when(last)
        def _():
            gcar_ref[...] = jnp.zeros_like(gcar_ref)
            anext_ref[...] = jnp.zeros_like(anext_ref)
            halo_ref[...] = jnp.zeros_like(halo_ref)

        taps, xc, r, i, a, a2, mult = _lru_pre(prec_ref, prev_ref, first, cw_ref, cb_ref, wa_ref, ba_ref, wx_ref, bx_ref, sp_ref)
        row = lax.broadcasted_iota(jnp.int32, (tc, W), 0)
        gelu, dgelu = _gelu_parts(pg_ref[...])
        dy = dy_ref[...]
        hcur = h_ref[...]
        dpg_ref[...] = dy * hcur * dgelu
        dh = dy * gelu
        a_next = jnp.where(row == tc - 1, anext_ref[0:1, :], pltpu.roll(a, tc - 1, 0))
        carry = gcar_ref[0:1, :]
        for t in reversed(range(tc // 8)):
            g = _scan8(a_next[8 * t:8 * t + 8], dh[8 * t:8 * t + 8], carry, True)
            g_ref[8 * t:8 * t + 8, :] = g
            carry = g[0:1, :]
        gcar_ref[...] = jnp.broadcast_to(carry, gcar_ref.shape)
        anext_ref[...] = jnp.broadcast_to(a[0:1, :], anext_ref.shape)
        G = g_ref[...]
        h_before = jnp.where(first, 0.0, hprev_ref[7:8, :])
        hprev = jnp.where(row == 0, h_before, pltpu.roll(hcur, 1, 0))
        d_a = G * hprev
        gx_ = G * xc
        d_mult = gx_ * i
        d_i = gx_ * mult
        dxc = G * (mult * i)
        d_la = d_a * a - d_mult * (a2 / mult)
        sp = sp_ref[...]
        d_r = d_la * (-LRU_C * sp)
        dsp_ref[...] += jnp.sum(d_la * (-LRU_C * r), axis=0, keepdims=True)
        dga = d_r * r * (1.0 - r)
        dgx = d_i * i * (1.0 - i)
        dba_ref[...] += jnp.sum(dga, axis=0, keepdims=True)
        dbx_ref[...] += jnp.sum(dgx, axis=0, keepdims=True)
        back = []
        for h in range(LRU_HEADS):
            sl = slice(h * HD, (h + 1) * HD)
            xh = xc[:, sl].astype(MXU_DTYPE)
            ah = dga[:, sl].astype(MXU_DTYPE)
            bh = dgx[:, sl].astype(MXU_DTYPE)
            tn = (((0,), (0,)), ((), ()))
            nt = (((1,), (1,)), ((), ()))
            dwa_ref[h] += lax.dot_general(xh, ah, tn, preferred_element_type=F32)
            dwx_ref[h] += lax.dot_general(xh, bh, tn, preferred_element_type=F32)
            back.append(lax.dot_general(ah, wa_ref[h].astype(MXU_DTYPE), nt, preferred_element_type=F32)
                        + lax.dot_general(bh, wx_ref[h].astype(MXU_DTYPE), nt, preferred_element_type=F32))
        dxc = dxc + jnp.concatenate(back, axis=1)
        dcb_ref[...] += jnp.sum(dxc, axis=0, keepdims=True)
        for k in range(CONV_WIDTH):
            dcw_ref[k:k + 1, :] += jnp.sum(dxc * taps[k], axis=0, keepdims=True)
        ext = jnp.concatenate([dxc, halo_ref[...]], axis=0)
        cw = cw_ref[...]
        acc = cw[CONV_WIDTH - 1:CONV_WIDTH, :] * dxc
        for k in range(CONV_WIDTH - 1):
            s = CONV_WIDTH - 1 - k
            acc = acc + cw[k:k + 1, :] * pltpu.roll(ext, tc + 8 - s, 0)[:tc]
        dprec_ref[...] = acc
        halo_ref[...] = dxc[0:8, :]

    rev = lambda j: nc - 1 - j
    cur = pl.BlockSpec((None, tc, W), lambda b, j: (b, rev(j), 0))
    rec = pl.BlockSpec((None, tc, W), lambda b, j: (b, rev(j), 1))
    prev = pl.BlockSpec((None, 8, W), lambda b, j: (b, jnp.maximum(rev(j) * (tc // 8) - 1, 0), 0))
    prev_rec = pl.BlockSpec((None, 8, W), lambda b, j: (b, jnp.maximum(rev(j) * (tc // 8) - 1, 0), 1))
    vec = pl.BlockSpec((1, W), lambda b, j: (0, 0))
    cws = pl.BlockSpec((CONV_WIDTH, W), lambda b, j: (0, 0))
    wsp = pl.BlockSpec((LRU_HEADS, HD, HD), lambda b, j: (0, 0, 0))
    seq = jax.ShapeDtypeStruct((B, Tp, W), F32)
    vs = jax.ShapeDtypeStruct((1, W), F32)
    ws = jax.ShapeDtypeStruct((LRU_HEADS, HD, HD), F32)
    return pl.pallas_call(
        body, name="lru_bwd", grid=(B, nc),
        in_specs=[cur, rec, prev_rec, cur, prev, cur, cws, vec, wsp, vec, wsp, vec, vec],
        out_specs=[cur, cur, cws, vec, wsp, vec, wsp, vec, vec],
        out_shape=[seq, seq, jax.ShapeDtypeStruct((CONV_WIDTH, W), F32), vs, ws, vs, ws, vs, vs],
        scratch_shapes=[pltpu.VMEM((8, W), F32), pltpu.VMEM((8, W), F32), pltpu.VMEM((8, W), F32), pltpu.VMEM((tc, W), F32)],
        compiler_params=pltpu.CompilerParams(dimension_semantics=("arbitrary", "arbitrary")),
    )(p, p, p, hseq, hseq, dy, cw, cb, wa, ba, wx, bx, sp)


_Q_BLOCK = 2 * LRU_WIDTH // MLA_Q_RANK
_KV_BLOCK = (2 * LRU_WIDTH + MLA_Q_RANK) // MLA_KV_RANK
_KPE_START = 2 * LRU_WIDTH + MLA_Q_RANK + MLA_KV_RANK


@jax.custom_vjp
def even_front(p, cw, cb, wa, ba, wx, bx, sp, gq, gkv):
    return _even_front_fwd(p, cw, cb, wa, ba, wx, bx, sp, gq, gkv)[0]


def _even_front_fwd(p, cw, cb, wa, ba, wx, bx, sp, gq, gkv):
    B, Tp, W = p.shape
    p2d = p.reshape(B * Tp, W)
    y, hseq = _lru_fwd_call(p, cw, cb, wa, ba, wx, bx, sp)
    qn = _rms_fwd_call(p2d, gq, "q_norm_fwd", _Q_BLOCK)
    kvn = _rms_fwd_call(p2d, gkv, "kv_norm_fwd", _KV_BLOCK)
    return (y, qn, kvn, p2d[:, _KPE_START:]), (p, hseq, cw, cb, wa, ba, wx, bx, sp, gq, gkv)


def _even_front_bwd(res, cts):
    p, hseq, cw, cb, wa, ba, wx, bx, sp, gq, gkv = res
    dy, dqn, dkvn, dkpe = cts
    B, Tp, W = p.shape
    p2d = p.reshape(B * Tp, W)
    dpg, dprec, dcw, dcb, dwa, dba, dwx, dbx, dsp = _lru_bwd_call(p, hseq, dy, cw, cb, wa, ba, wx, bx, sp)
    dpq, dgq = _rms_bwd_call(p2d, gq, dqn, "q_norm_bwd", _Q_BLOCK)
    dpkv, dgkv = _rms_bwd_call(p2d, gkv, dkvn, "kv_norm_bwd", _KV_BLOCK)
    dp = jnp.concatenate([dpg.reshape(B * Tp, -1), dprec.reshape(B * Tp, -1), dpq, dpkv, dkpe], axis=1).reshape(B, Tp, W)
    return dp, dcw, dcb, dwa, dba, dwx, dbx, dsp, dgq.reshape(gq.shape), dgkv.reshape(gkv.shape)


even_front.defvjp(_even_front_fwd, _even_front_bwd)


def _rope_tables(pos, half):
    inv = ROPE_BASE ** (-jnp.arange(half, dtype=F32) / half)
    ang = pos.astype(F32)[:, None] * inv[None, :]
    return jnp.cos(ang), jnp.sin(ang)


_NT = (((1,), (1,)), ((), ()))
_TN = (((0,), (0,)), ((), ()))
HEAD_LANES = 128
_MLA_SCALE = (MLA_NOPE + MLA_ROPE) ** -0.5


def _causal_keep(qi, L):
    row = lax.broadcasted_iota(jnp.int32, (SEQ_BLOCK, L), 0) + qi * SEQ_BLOCK
    col = lax.broadcasted_iota(jnp.int32, (SEQ_BLOCK, L), 1)
    return col <= row


def _mla_rope_tables(pos):
    half = MLA_ROPE // 2
    cos, sin = _rope_tables(pos, half)
    T = pos.shape[0]
    ones, zeros = jnp.ones((T, MLA_NOPE), F32), jnp.zeros((T, MLA_NOPE), F32)
    tail1, tail0 = jnp.ones((T, HEAD_LANES - MLA_NOPE - MLA_ROPE), F32), jnp.zeros((T, HEAD_LANES - MLA_NOPE - MLA_ROPE), F32)
    zh = jnp.zeros((T, half), F32)
    c = jnp.concatenate([ones, cos, cos, tail1], axis=1)
    s_up = jnp.concatenate([zeros, -sin, zh, tail0], axis=1)
    s_down = jnp.concatenate([zeros, zh, sin, tail0], axis=1)
    return c, s_up, s_down


def _rope_lanes(x, c, s_up, s_down):
    half = MLA_ROPE // 2
    return x * c + pltpu.roll(x, HEAD_LANES - half, 1) * s_up + pltpu.roll(x, half, 1) * s_down


def _unrope_lanes(d, c, s_up, s_down):
    half = MLA_ROPE // 2
    return d * c + pltpu.roll(d * s_up, half, 1) + pltpu.roll(d * s_down, HEAD_LANES - half, 1)


def _mla_operands(q_ref, kv_ref, kpe_ref, c, s_up, s_down):
    lane = lax.broadcasted_iota(jnp.int32, kv_ref.shape, 1)
    qr = _rope_lanes(q_ref[...].astype(F32), c, s_up, s_down).astype(MXU_DTYPE)
    kr = jnp.where(lane < MLA_NOPE, kv_ref[...].astype(F32), _rope_lanes(kpe_ref[...], c, s_up, s_down)).astype(MXU_DTYPE)
    return qr, kr, lane


def _mla_specs(Tp):
    head = pl.BlockSpec((None, Tp, HEAD_LANES), lambda b, h: (b, 0, h))
    shared = pl.BlockSpec((None, Tp, HEAD_LANES), lambda b, h: (b, 0, 0))
    tab = pl.BlockSpec((Tp, HEAD_LANES), lambda b, h: (0, 0))
    lse = pl.BlockSpec((None, None, Tp, 1), lambda b, h: (b, h, 0, 0))
    return head, shared, tab, lse


def _attn_fwd_call(q, kv, kpe, tabs):
    B, Tp, _ = q.shape
    nq = Tp // SEQ_BLOCK

    def body(q_ref, kv_ref, kpe_ref, c_ref, su_ref, sd_ref, o_ref, lse_ref, qr_ref, kr_ref):
        qr, kr, lane = _mla_operands(q_ref, kv_ref, kpe_ref, c_ref[...], su_ref[...], sd_ref[...])
        qr_ref[...] = qr
        kr_ref[...] = kr
        for qi in range(nq):
            L = (qi + 1) * SEQ_BLOCK
            blk = slice(qi * SEQ_BLOCK, L)
            s = lax.dot_general(qr_ref[blk, :], kr_ref[0:L, :], _NT, preferred_element_type=F32) * _MLA_SCALE
            s = jnp.where(_causal_keep(qi, L), s, NEG_INF)
            m = jnp.max(s, axis=-1, keepdims=True)
            p = jnp.exp(s - m)
            l = jnp.sum(p, axis=-1, keepdims=True)
            o = jnp.dot(p.astype(MXU_DTYPE), kv_ref[0:L, :].astype(MXU_DTYPE), preferred_element_type=F32)
            o_ref[blk, :] = jnp.where(lane[blk, :] >= MLA_NOPE, o / l, 0.0)
            lse_ref[blk, :] = m + jnp.log(l)

    head, shared, tab, lse = _mla_specs(Tp)
    return pl.pallas_call(
        body, name="mla_attn_fwd", grid=(B, MLA_HEADS), in_specs=[head, head, shared, tab, tab, tab], out_specs=[head, lse],
        out_shape=[jax.ShapeDtypeStruct((B, Tp, MLA_HEADS * HEAD_LANES), F32), jax.ShapeDtypeStruct((B, MLA_HEADS, Tp, 1), F32)],
        scratch_shapes=[pltpu.VMEM((Tp, HEAD_LANES), MXU_DTYPE), pltpu.VMEM((Tp, HEAD_LANES), MXU_DTYPE)],
        compiler_params=pltpu.CompilerParams(dimension_semantics=("parallel", "parallel")),
    )(q, kv, kpe, *tabs)


def _attn_bwd_call(q, kv, kpe, tabs, o, lse, do):
    B, Tp, _ = q.shape
    nq = Tp // SEQ_BLOCK

    def body(q_ref, kv_ref, kpe_ref, c_ref, su_ref, sd_ref, o_ref, lse_ref, do_ref, dq_ref, dkv_ref, dkpe_ref,
             qr_ref, kr_ref, dqa_ref, dka_ref, dva_ref):
        c, s_up, s_down = c_ref[...], su_ref[...], sd_ref[...]
        qr, kr, lane = _mla_operands(q_ref, kv_ref, kpe_ref, c, s_up, s_down)
        qr_ref[...] = qr
        kr_ref[...] = kr
        dka_ref[...] = jnp.zeros_like(dka_ref)
        dva_ref[...] = jnp.zeros_like(dva_ref)
        for qi in range(nq):
            L = (qi + 1) * SEQ_BLOCK
            blk = slice(qi * SEQ_BLOCK, L)
            qb = qr_ref[blk, :]
            do = jnp.where(lane[blk, :] >= MLA_NOPE, do_ref[blk, :], 0.0)
            delta = jnp.sum(do * o_ref[blk, :], axis=-1, keepdims=True)
            s = lax.dot_general(qb, kr_ref[0:L, :], _NT, preferred_element_type=F32) * _MLA_SCALE
            s = jnp.where(_causal_keep(qi, L), s, NEG_INF)
            p = jnp.exp(s - lse_ref[blk, :])
            dob = do.astype(MXU_DTYPE)
            dva_ref[0:L, :] += lax.dot_general(p.astype(MXU_DTYPE), dob, _TN, preferred_element_type=F32)
            dp = lax.dot_general(dob, kv_ref[0:L, :].astype(MXU_DTYPE), _NT, preferred_element_type=F32)
            ds = (p * (dp - delta) * _MLA_SCALE).astype(MXU_DTYPE)
            dqa_ref[blk, :] = jnp.dot(ds, kr_ref[0:L, :], preferred_element_type=F32)
            dka_ref[0:L, :] += lax.dot_general(ds, qb, _TN, preferred_element_type=F32)
        dq_ref[...] = _unrope_lanes(dqa_ref[...], c, s_up, s_down).astype(dq_ref.dtype)
        dk = dka_ref[...]
        dkv_ref[...] = jnp.where(lane < MLA_NOPE, dk, dva_ref[...]).astype(dkv_ref.dtype)
        dkpe = jnp.where(lane >= MLA_NOPE, _unrope_lanes(dk, c, s_up, s_down), 0.0)

        @pl.when(pl.program_id(1) == 0)
        def _():
            dkpe_ref[...] = dkpe

        @pl.when(pl.program_id(1) > 0)
        def _():
            dkpe_ref[...] += dkpe

    head, shared, tab, lse_spec = _mla_specs(Tp)
    wide = jax.ShapeDtypeStruct((B, Tp, MLA_HEADS * HEAD_LANES), q.dtype)
    acc = pltpu.VMEM((Tp, HEAD_LANES), F32)
    return pl.pallas_call(
        body, name="mla_attn_bwd", grid=(B, MLA_HEADS),
        in_specs=[head, head, shared, tab, tab, tab, head, lse_spec, head], out_specs=[head, head, shared],
        out_shape=[wide, wide, jax.ShapeDtypeStruct((B, Tp, HEAD_LANES), F32)],
        scratch_shapes=[pltpu.VMEM((Tp, HEAD_LANES), MXU_DTYPE), pltpu.VMEM((Tp, HEAD_LANES), MXU_DTYPE), acc, acc, acc],
        compiler_params=pltpu.CompilerParams(dimension_semantics=("parallel", "arbitrary")),
    )(q, kv, kpe, *tabs, o, lse, do)


@jax.custom_vjp
def mla_attention(q, kv, kpe, tabs):
    return _attn_fwd_call(q, kv, kpe, tabs)[0]


def _mla_attention_fwd(q, kv, kpe, tabs):
    o, lse = _attn_fwd_call(q, kv, kpe, tabs)
    return o, (q, kv, kpe, tabs, o, lse)


def _mla_attention_bwd(res, do):
    q, kv, kpe, tabs, o, lse = res
    dq, dkv, dkpe = _attn_bwd_call(q, kv, kpe, tabs, o, lse, do)
    return dq, dkv, dkpe, None


mla_attention.defvjp(_mla_attention_fwd, _mla_attention_bwd)


def _decay(qi, L, lg):
    row = lax.broadcasted_iota(jnp.int32, (SEQ_BLOCK, L), 0) + qi * SEQ_BLOCK
    col = lax.broadcasted_iota(jnp.int32, (SEQ_BLOCK, L), 1)
    diff = row - col
    return jnp.where(diff >= 0, jnp.exp(lg * jnp.maximum(diff, 0).astype(F32)), 0.0)


def _rope_halves(x, cos, sin, scale):
    half = x.shape[1] // 2
    x1, x2 = x[:, :half], x[:, half:]
    return (jnp.concatenate([x1 * cos - x2 * sin, x1 * sin + x2 * cos], axis=1) * scale).astype(MXU_DTYPE)


def _unrope_halves(d, cos, sin, scale):
    half = d.shape[1] // 2
    d1, d2 = d[:, :half], d[:, half:]
    return jnp.concatenate([d1 * cos + d2 * sin, d2 * cos - d1 * sin], axis=1) * scale


_RET_K_SCALE = RET_QK_DIM ** -0.5
_RET_Q_BLOCKS = RET_HEADS
_RET_V_BLOCK0 = 2 * RET_HEADS * RET_QK_DIM // RET_V_DIM
_RET_G_BLOCK0 = _RET_V_BLOCK0 + RET_HEADS


def _ret_specs(Tp):
    q = pl.BlockSpec((None, Tp, RET_QK_DIM), lambda b, h: (b, 0, h))
    k = pl.BlockSpec((None, Tp, RET_QK_DIM), lambda b, h: (b, 0, _RET_Q_BLOCKS + h))
    v = pl.BlockSpec((None, Tp, RET_V_DIM), lambda b, h: (b, 0, _RET_V_BLOCK0 + h))
    tab = pl.BlockSpec((Tp, RET_QK_DIM // 2), lambda b, h: (0, 0))
    lg = pl.BlockSpec((None, 1, 1), lambda b, h: (h, 0, 0))
    return q, k, v, tab, lg


def _ret_core_fwd_call(p, cos, sin, lg):
    B, Tp, _ = p.shape
    nq = Tp // SEQ_BLOCK

    def body(q_ref, k_ref, v_ref, cos_ref, sin_ref, lg_ref, o_ref, qr_ref, kr_ref):
        lg_ = lg_ref[...]
        cos_, sin_ = cos_ref[...], sin_ref[...]
        qr_ref[...] = _rope_halves(q_ref[...].astype(F32), cos_, sin_, 1.0)
        kr_ref[...] = _rope_halves(k_ref[...].astype(F32), cos_, sin_, _RET_K_SCALE)
        for qi in range(nq):
            L = (qi + 1) * SEQ_BLOCK
            blk = slice(qi * SEQ_BLOCK, L)
            s = lax.dot_general(qr_ref[blk, :], kr_ref[0:L, :], _NT, preferred_element_type=F32) * _decay(qi, L, lg_)
            o_ref[blk, :] = jnp.dot(s.astype(MXU_DTYPE), v_ref[0:L, :].astype(MXU_DTYPE), preferred_element_type=F32)

    q, k, v, tab, lgs = _ret_specs(Tp)
    return pl.pallas_call(
        body, name="retention_fwd", grid=(B, RET_HEADS), in_specs=[q, k, v, tab, tab, lgs],
        out_specs=pl.BlockSpec((None, Tp, RET_V_DIM), lambda b, h: (b, 0, h)),
        out_shape=jax.ShapeDtypeStruct((B, Tp, RET_HEADS * RET_V_DIM), F32),
        scratch_shapes=[pltpu.VMEM((Tp, RET_QK_DIM), MXU_DTYPE), pltpu.VMEM((Tp, RET_QK_DIM), MXU_DTYPE)],
        compiler_params=pltpu.CompilerParams(dimension_semantics=("parallel", "parallel")),
    )(p, p, p, cos, sin, lg)


def _ret_core_bwd_call(p, do, cos, sin, lg):
    B, Tp, _ = p.shape
    nq = Tp // SEQ_BLOCK

    def body(q_ref, k_ref, v_ref, do_ref, cos_ref, sin_ref, lg_ref, dq_ref, dk_ref, dv_ref, qr_ref, kr_ref, dqa_ref, dka_ref, dva_ref):
        lg_ = lg_ref[...]
        cos_, sin_ = cos_ref[...], sin_ref[...]
        qr_ref[...] = _rope_halves(q_ref[...].astype(F32), cos_, sin_, 1.0)
        kr_ref[...] = _rope_halves(k_ref[...].astype(F32), cos_, sin_, _RET_K_SCALE)
        dka_ref[...] = jnp.zeros_like(dka_ref)
        dva_ref[...] = jnp.zeros_like(dva_ref)
        for qi in range(nq):
            L = (qi + 1) * SEQ_BLOCK
            blk = slice(qi * SEQ_BLOCK, L)
            qb = qr_ref[blk, :]
            dob = do_ref[blk, :].astype(MXU_DTYPE)
            dec = _decay(qi, L, lg_)
            s = (lax.dot_general(qb, kr_ref[0:L, :], _NT, preferred_element_type=F32) * dec).astype(MXU_DTYPE)
            dva_ref[0:L, :] += lax.dot_general(s, dob, _TN, preferred_element_type=F32)
            ds = (lax.dot_general(dob, v_ref[0:L, :].astype(MXU_DTYPE), _NT, preferred_element_type=F32) * dec).astype(MXU_DTYPE)
            dqa_ref[blk, :] = jnp.dot(ds, kr_ref[0:L, :], preferred_element_type=F32)
            dka_ref[0:L, :] += lax.dot_general(ds, qb, _TN, preferred_element_type=F32)
        dq_ref[...] = _unrope_halves(dqa_ref[...], cos_, sin_, 1.0).astype(dq_ref.dtype)
        dk_ref[...] = _unrope_halves(dka_ref[...], cos_, sin_, _RET_K_SCALE).astype(dk_ref.dtype)
        dv_ref[...] = dva_ref[...].astype(dv_ref.dtype)

    q, k, v, tab, lgs = _ret_specs(Tp)
    qk_out = pl.BlockSpec((None, Tp, RET_QK_DIM), lambda b, h: (b, 0, h))
    v_out = pl.BlockSpec((None, Tp, RET_V_DIM), lambda b, h: (b, 0, h))
    return pl.pallas_call(
        body, name="retention_bwd", grid=(B, RET_HEADS), in_specs=[q, k, v, v_out, tab, tab, lgs],
        out_specs=[qk_out, qk_out, v_out],
        out_shape=[jax.ShapeDtypeStruct((B, Tp, RET_HEADS * RET_QK_DIM), p.dtype), jax.ShapeDtypeStruct((B, Tp, RET_HEADS * RET_QK_DIM), p.dtype),
                   jax.ShapeDtypeStruct((B, Tp, RET_HEADS * RET_V_DIM), p.dtype)],
        scratch_shapes=[pltpu.VMEM((Tp, RET_QK_DIM), MXU_DTYPE), pltpu.VMEM((Tp, RET_QK_DIM), MXU_DTYPE),
                        pltpu.VMEM((Tp, RET_QK_DIM), F32), pltpu.VMEM((Tp, RET_QK_DIM), F32), pltpu.VMEM((Tp, RET_V_DIM), F32)],
        compiler_params=pltpu.CompilerParams(dimension_semantics=("parallel", "parallel")),
    )(p, p, p, do, cos, sin, lg)


def _ret_gate_specs(M):
    tm = _pick(M, 1088, 8)
    head = pl.BlockSpec((tm, RET_V_DIM), lambda i, h: (i, h))
    gate = pl.BlockSpec((tm, RET_V_DIM), lambda i, h: (i, _RET_G_BLOCK0 + h))
    return tm, head, gate


def _ret_gate_fwd_call(o, p2d):
    M = o.shape[0]
    tm, head, gate = _ret_gate_specs(M)

    def body(o_ref, g_ref, y_ref):
        ov = o_ref[...]
        gv = g_ref[...].astype(F32)
        rstd = lax.rsqrt(jnp.mean(ov * ov, axis=-1, keepdims=True) + EPS)
        y_ref[...] = (gv * _sigmoid(gv)) * (ov * rstd)

    return pl.pallas_call(
        body, name="retention_gate_fwd", grid=(M // tm, RET_HEADS), in_specs=[head, gate], out_specs=head,
        out_shape=jax.ShapeDtypeStruct(o.shape, F32),
        compiler_params=pltpu.CompilerParams(dimension_semantics=("parallel", "parallel")),
    )(o, p2d)


def _ret_gate_bwd_call(o, p2d, dy):
    M = o.shape[0]
    tm, head, gate = _ret_gate_specs(M)

    def body(o_ref, g_ref, dy_ref, do_ref, dg_ref):
        ov = o_ref[...]
        gv = g_ref[...].astype(F32)
        dy = dy_ref[...]
        rstd = lax.rsqrt(jnp.mean(ov * ov, axis=-1, keepdims=True) + EPS)
        on = ov * rstd
        sg = _sigmoid(gv)
        dg_ref[...] = (dy * on * (sg * (1.0 + gv * (1.0 - sg)))).astype(dg_ref.dtype)
        don = dy * (gv * sg)
        do_ref[...] = (rstd * (don - on * jnp.mean(don * on, axis=-1, keepdims=True))).astype(do_ref.dtype)

    shp = jax.ShapeDtypeStruct(o.shape, p2d.dtype)
    return pl.pallas_call(
        body, name="retention_gate_bwd", grid=(M // tm, RET_HEADS), in_specs=[head, gate, head], out_specs=[head, head],
        out_shape=[shp, shp],
        compiler_params=pltpu.CompilerParams(dimension_semantics=("parallel", "parallel")),
    )(o, p2d, dy)


def _log_gamma():
    return jnp.log(1.0 - 2.0 ** (-5.0 - jnp.arange(RET_HEADS, dtype=F32))).reshape(RET_HEADS, 1, 1)


@jax.custom_vjp
def retention_mixer(p, cos, sin):
    B, Tp, W = p.shape
    o = _ret_core_fwd_call(p, cos, sin, _log_gamma())
    return _ret_gate_fwd_call(o.reshape(B * Tp, -1), p.reshape(B * Tp, W))


def _retention_mixer_fwd(p, cos, sin):
    B, Tp, W = p.shape
    o = _ret_core_fwd_call(p, cos, sin, _log_gamma())
    return _ret_gate_fwd_call(o.reshape(B * Tp, -1), p.reshape(B * Tp, W)), (p, o, cos, sin)


def _retention_mixer_bwd(res, dy):
    p, o, cos, sin = res
    B, Tp, W = p.shape
    do, dg = _ret_gate_bwd_call(o.reshape(B * Tp, -1), p.reshape(B * Tp, W), dy)
    dq, dk, dv = _ret_core_bwd_call(p, do.reshape(B, Tp, -1), cos, sin, _log_gamma())
    return jnp.concatenate([dq, dk, dv, dg.reshape(B, Tp, -1)], axis=-1), None, None


retention_mixer.defvjp(_retention_mixer_fwd, _retention_mixer_bwd)


def _heads_to_lanes(w):
    K = w.shape[0]
    w = w.reshape(K, MLA_HEADS, MLA_NOPE + MLA_ROPE)
    return jnp.pad(w, ((0, 0), (0, 0), (0, HEAD_LANES - MLA_NOPE - MLA_ROPE))).reshape(K, MLA_HEADS * HEAD_LANES)


def _out_rows_to_lanes(w):
    N = w.shape[1]
    att = w[LRU_WIDTH:].reshape(MLA_HEADS, MLA_V, N)
    att = jnp.pad(att, ((0, 0), (HEAD_LANES - MLA_V, 0), (0, 0))).reshape(MLA_HEADS * HEAD_LANES, N)
    return jnp.concatenate([w[:LRU_WIDTH], att], axis=0)


def _local_loss(diff, wfull, tgt):
    x = diff["x"]
    B, S, D = x.shape
    T = S + N_META
    Tp = _round_up(T, SEQ_BLOCK)
    M = B * Tp
    pos = jnp.arange(Tp, dtype=jnp.int32)

    def mm(a, name, act=False, out_dtype=F32, layout=lambda w: w, col_shards=1):
        return matmul(a, layout(wfull[name]), layout(diff[name]), act, name, out_dtype, col_shards)

    meta = jnp.broadcast_to(diff["meta_tokens"][None], (B, N_META, D))
    h = jnp.concatenate([meta, x, jnp.zeros((B, Tp - T, D), F32)], axis=1).reshape(M, D)

    p = mm(h, "ev_w_in")
    sp = jax.nn.softplus(-diff["ev_lru_lambda"]).reshape(1, LRU_WIDTH)
    y_rec, qn, kvn, p_kpe = even_front(
        p.reshape(B, Tp, -1), diff["ev_conv_w"].reshape(CONV_WIDTH, LRU_WIDTH), diff["ev_conv_b"].reshape(1, LRU_WIDTH),
        diff["ev_w_rg_a"].reshape(LRU_HEADS, LRU_HEAD_DIM, LRU_HEAD_DIM), diff["ev_b_rg_a"].reshape(1, LRU_WIDTH),
        diff["ev_w_rg_x"].reshape(LRU_HEADS, LRU_HEAD_DIM, LRU_HEAD_DIM), diff["ev_b_rg_x"].reshape(1, LRU_WIDTH),
        sp, diff["ev_q_norm_g"].reshape(-1), diff["ev_kv_norm_g"].reshape(-1))
    y_rec = y_rec.reshape(M, LRU_WIDTH)
    q = mm(qn, "ev_w_uq", out_dtype=MXU_DTYPE, layout=_heads_to_lanes).reshape(B, Tp, -1)
    kv = mm(kvn, "ev_w_ukv", out_dtype=MXU_DTYPE).reshape(B, Tp, -1)
    kpe = jnp.pad(p_kpe.reshape(B, Tp, MLA_ROPE), ((0, 0), (0, 0), (MLA_NOPE, HEAD_LANES - MLA_NOPE - MLA_ROPE)))
    y_att = mla_attention(q, kv, kpe, _mla_rope_tables(pos)).reshape(M, -1)
    mix = mm(jnp.concatenate([y_rec, y_att], axis=-1), "ev_w_out", layout=_out_rows_to_lanes)
    h = deepnorm(h, mix, diff["ln_mix_g"][0], diff["ln_mix_b"][0], "ln_mix0")
    f = mlp(h, wfull["mlp_w1_0"], wfull["mlp_w2_0"], diff["mlp_w1_0"], diff["mlp_w2_0"], "mlp0")
    h = deepnorm(h, f, diff["ln_mlp_g"][0], diff["ln_mlp_b"][0], "ln_mlp0")

    p = mm(h, "od_w_in", out_dtype=MXU_DTYPE, col_shards=N_CHIPS)
    cos, sin = _rope_tables(pos, RET_QK_DIM // 2)
    mix = mm(retention_mixer(p.reshape(B, Tp, -1), cos, sin), "od_w_out")
    h = deepnorm(h, mix, diff["ln_mix_g"][1], diff["ln_mix_b"][1], "ln_mix1")
    f = mlp(h, wfull["mlp_w1_1"], wfull["mlp_w2_1"], diff["mlp_w1_1"], diff["mlp_w2_1"], "mlp1")
    h = deepnorm(h, f, diff["ln_mlp_g"][1], diff["ln_mlp_b"][1], "ln_mlp1")

    y = h.reshape(B, Tp, D)[:, N_META:T].reshape(B * S, D)
    return loss_head(y, tgt.reshape(B * S, D))


_HBM = pl.BlockSpec(memory_space=pltpu.HBM)


def _place():
    return lax.axis_index("x"), lax.axis_index("y"), lax.axis_index("c")


def _other_chips(x, y):
    return [(1 - x, y), (x, 1 - y), (1 - x, 1 - y)]


def _chunks(rows, sublanes, most):
    for q in range(most, 0, -1):
        if rows % (q * sublanes) == 0:
            return q
    return 1


def _sublanes(dtype):
    return 8 * 4 // jnp.dtype(dtype).itemsize


def _allgather_chips(buf, name):
    R, C = buf.shape
    Rh = R // 2
    Q = _chunks(Rh, _sublanes(buf.dtype), 4)
    ch = Rh // Q

    def body(x_ref, out_ref, send_sems, recv_sems):
        x, y, c = _place()
        sibling = (x, y, 1 - c)
        chips = _other_chips(x, y)

        def piece(cx, cy, hc, q):
            return out_ref.at[2 * cx + cy, pl.ds(hc * Rh + q * ch, ch), :]

        def copy(k, src, dst, to):
            return pltpu.make_async_remote_copy(src_ref=src, dst_ref=dst, send_sem=send_sems.at[k], recv_sem=recv_sems.at[k],
                                                device_id=to, device_id_type=MESH)

        first = [copy(j * Q + q, x_ref.at[pl.ds(c * Rh + q * ch, ch), :], piece(x, y, c, q), (*chip, c))
                 for q in range(Q) for j, chip in enumerate(chips)]
        for cp in first:
            cp.start()
        passed = []
        for q in range(Q):
            for j, chip in enumerate(chips):
                landed = piece(*chip, c, q)
                copy(j * Q + q, landed, landed, sibling).wait_recv()
                fwd = copy(3 * Q + j * Q + q, landed, landed, sibling)
                fwd.start()
                passed.append(fwd)
        for q in range(Q):
            for j, chip in enumerate(chips):
                theirs = piece(*chip, 1 - c, q)
                copy(3 * Q + j * Q + q, theirs, theirs, sibling).wait_recv()
        for cp in first + passed:
            cp.wait_send()

    return pl.pallas_call(
        body, name=name, in_specs=[_HBM], out_specs=_HBM,
        out_shape=jax.ShapeDtypeStruct((N_CHIPS, R, C), buf.dtype),
        scratch_shapes=[pltpu.SemaphoreType.DMA((6 * Q,)), pltpu.SemaphoreType.DMA((6 * Q,))],
    )(buf)


def _with_own(gathered, own):
    my = 2 * lax.axis_index("x") + lax.axis_index("y")
    return lax.dynamic_update_slice(gathered, own[None], (my, 0, 0))


def _sibling_exchange(ps, name):
    n = len(ps)

    def body(*refs):
        p_refs, out_refs, (send_sems, recv_sems) = refs[:n], refs[n:2 * n], refs[2 * n:]
        x, y, c = _place()
        copies = [pltpu.make_async_remote_copy(src_ref=p_ref.at[j, 1 - c], dst_ref=out_ref.at[j], send_sem=send_sems.at[N_CHIPS * i + j],
                                               recv_sem=recv_sems.at[N_CHIPS * i + j], device_id=(x, y, 1 - c), device_id_type=MESH)
                  for i, (p_ref, out_ref) in enumerate(zip(p_refs, out_refs)) for j in range(N_CHIPS)]
        for cp in copies:
            cp.start()
        for cp in copies:
            cp.wait()

    return pl.pallas_call(
        body, name=name, in_specs=[_HBM] * n, out_specs=[_HBM] * n,
        out_shape=[jax.ShapeDtypeStruct((N_CHIPS,) + p.shape[2:], p.dtype) for p in ps],
        scratch_shapes=[pltpu.SemaphoreType.DMA((N_CHIPS * n,)), pltpu.SemaphoreType.DMA((N_CHIPS * n,))],
    )(*ps)


def _chip_scatter(ss, name):
    n = len(ss)

    def body(*refs):
        s_refs, t_refs, (send_sems, recv_sems) = refs[:n], refs[n:2 * n], refs[2 * n:]
        x, y, c = _place()
        copies = [pltpu.make_async_remote_copy(src_ref=s_ref.at[j + 1], dst_ref=t_ref.at[j], send_sem=send_sems.at[3 * i + j],
                                               recv_sem=recv_sems.at[3 * i + j], device_id=(cx, cy, c), device_id_type=MESH)
                  for i, (s_ref, t_ref) in enumerate(zip(s_refs, t_refs)) for j, (cx, cy) in enumerate(_other_chips(x, y))]
        for cp in copies:
            cp.start()
        for cp in copies:
            cp.wait()

    return pl.pallas_call(
        body, name=name, in_specs=[_HBM] * n, out_specs=[_HBM] * n,
        out_shape=[jax.ShapeDtypeStruct((3,) + s.shape[1:], s.dtype) for s in ss],
        scratch_shapes=[pltpu.SemaphoreType.DMA((3 * n,)), pltpu.SemaphoreType.DMA((3 * n,))],
    )(*ss)


def _sibling_gather(fs, name):
    n = len(fs)

    def body(*refs):
        out_refs, (send_sems, recv_sems) = refs[n:2 * n], refs[2 * n:]
        x, y, c = _place()
        copies = [pltpu.make_async_remote_copy(src_ref=out_ref.at[c], dst_ref=out_ref.at[c], send_sem=send_sems.at[i], recv_sem=recv_sems.at[i],
                                               device_id=(x, y, 1 - c), device_id_type=MESH) for i, out_ref in enumerate(out_refs)]
        for cp in copies:
            cp.start()
        for cp in copies:
            cp.wait()

    return pl.pallas_call(
        body, name=name, in_specs=[_HBM] * n, out_specs=[_HBM] * n,
        out_shape=[jax.ShapeDtypeStruct(f.shape, f.dtype) for f in fs], input_output_aliases={i: i for i in range(n)},
        scratch_shapes=[pltpu.SemaphoreType.DMA((n,)), pltpu.SemaphoreType.DMA((n,))],
    )(*fs)


def _axis_scalar(name):
    return lax.axis_index(name).astype(jnp.int32).reshape(1)


def _add_own_half(p, got, out_dtype, name):
    n, _, R, C = p.shape
    tr = _pick(R, 512, 16)

    def body(x_ref, y_ref, c_ref, p_ref, g_ref, o_ref):
        o_ref[...] = (p_ref[...] + g_ref[...]).astype(out_dtype)

    def chip(r, x_ref, y_ref):
        return 2 * (x_ref[0] ^ (r & 1)) + (y_ref[0] ^ (r >> 1))

    grid_spec = pltpu.PrefetchScalarGridSpec(
        num_scalar_prefetch=3, grid=(n, R // tr),
        in_specs=[pl.BlockSpec((None, None, tr, C), lambda r, i, x_ref, y_ref, c_ref: (chip(r, x_ref, y_ref), c_ref[0], i, 0)),
                  pl.BlockSpec((None, tr, C), lambda r, i, x_ref, y_ref, c_ref: (chip(r, x_ref, y_ref), i, 0))],
        out_specs=pl.BlockSpec((None, tr, C), lambda r, i, x_ref, y_ref, c_ref: (r, i, 0)))
    return pl.pallas_call(body, name=name, grid_spec=grid_spec, out_shape=jax.ShapeDtypeStruct((n, R, C), out_dtype),
                          compiler_params=pltpu.CompilerParams(dimension_semantics=("parallel", "parallel")))(
        _axis_scalar("x"), _axis_scalar("y"), _axis_scalar("c"), p, got)


def _sum_partials(s, t, name):
    _, R, C = s.shape
    tr = _pick(R, 512, 16)

    def body(c_ref, s_ref, t_ref, o_ref):
        acc = s_ref[...].astype(F32)
        for j in range(3):
            acc = acc + t_ref[j].astype(F32)
        o_ref[...] = acc

    grid_spec = pltpu.PrefetchScalarGridSpec(
        num_scalar_prefetch=1, grid=(R // tr,),
        in_specs=[pl.BlockSpec((None, tr, C), lambda i, c_ref: (0, i, 0)), pl.BlockSpec((3, tr, C), lambda i, c_ref: (0, i, 0))],
        out_specs=pl.BlockSpec((None, tr, C), lambda i, c_ref: (c_ref[0], i, 0)))
    return pl.pallas_call(body, name=name, grid_spec=grid_spec, out_shape=jax.ShapeDtypeStruct((2, R, C), F32),
                          compiler_params=pltpu.CompilerParams(dimension_semantics=("parallel",)))(_axis_scalar("c"), s, t)


def _reduce_to_chips(ps, wire_dtypes):
    got = _sibling_exchange(ps, "grad_sibling_exchange")
    ss = [_add_own_half(p, g, dt, "grad_sibling_add%d" % i) for i, (p, g, dt) in enumerate(zip(ps, got, wire_dtypes))]
    ts = _chip_scatter(ss, "grad_chip_scatter")
    fs = [_sum_partials(s, t, "grad_chip_sum%d" % i) for i, (s, t) in enumerate(zip(ss, ts))]
    return _sibling_gather(fs, "grad_sibling_gather")


def _adamw(w, g, m, v, name):
    R, C = w.shape
    tr = _pick(R, 256, 8)

    def body(w_ref, g_ref, m_ref, v_ref, d_ref, nm_ref, nv_ref):
        g_ = g_ref[...]
        m_ = ADAM_B1 * m_ref[...] + (1.0 - ADAM_B1) * g_
        v_ = ADAM_B2 * v_ref[...] + (1.0 - ADAM_B2) * (g_ * g_)
        m_hat = m_ / (1.0 - ADAM_B1 ** ADAM_STEP)
        v_hat = v_ / (1.0 - ADAM_B2 ** ADAM_STEP)
        d_ref[...] = -ADAM_LR * (m_hat / (jnp.sqrt(v_hat) + ADAM_EPS) + ADAM_WD * w_ref[...])
        nm_ref[...] = m_
        nv_ref[...] = v_

    row = pl.BlockSpec((tr, C), lambda i: (i, 0))
    shp = jax.ShapeDtypeStruct((R, C), F32)
    return pl.pallas_call(body, name=name, grid=(R // tr,), in_specs=[row] * 4, out_specs=[row] * 3, out_shape=[shp] * 3,
                          compiler_params=pltpu.CompilerParams(dimension_semantics=("parallel",)))(w, g, m, v)


BIG_SPECS = (("ev_w_in", 1024, 1440, 1), ("ev_w_uq", 256, 768, 1), ("ev_w_ukv", 128, 1024, 1), ("ev_w_out", 1024, 1024, 0),
             ("od_w_in", 1024, 6144, 1), ("od_w_out", 2048, 1024, 0), ("mlp_w1_0", 1024, 4096, 1), ("mlp_w1_1", 1024, 4096, 1),
             ("mlp_w2_0", 4096, 1024, 0), ("mlp_w2_1", 4096, 1024, 0))
BIG_PARAMS = (("ev_w_in", ("ev_w_in",)), ("ev_w_uq", ("ev_w_uq",)), ("ev_w_ukv", ("ev_w_ukv",)), ("ev_w_out", ("ev_w_out",)),
              ("od_w_in", ("od_w_in",)), ("od_w_out", ("od_w_out",)), ("mlp_w1", ("mlp_w1_0", "mlp_w1_1")),
              ("mlp_w2", ("mlp_w2_0", "mlp_w2_1")))
REPLICATED = ("ev_conv_b", "ev_w_rg_a", "ev_b_rg_a", "ev_w_rg_x", "ev_b_rg_x", "ev_lru_lambda", "ev_q_norm_g", "ev_kv_norm_g",
              "ln_mix_g", "ln_mix_b", "ln_mlp_g", "ln_mlp_b")
SMALL_SHARDED = ("meta_tokens", "ev_conv_w")
COL_SHARD_GRADS = ("od_w_in", "mlp_w1_0", "mlp_w1_1")
WEIGHT_NAMES = ("meta_tokens", "ev_w_in", "ev_conv_w", "ev_conv_b", "ev_w_rg_a", "ev_b_rg_a", "ev_w_rg_x", "ev_b_rg_x",
                "ev_lru_lambda", "ev_q_norm_g", "ev_w_uq", "ev_kv_norm_g", "ev_w_ukv", "ev_w_out", "od_w_in", "od_w_out",
                "ln_mix_g", "ln_mix_b", "mlp_w1", "mlp_w2", "ln_mlp_g", "ln_mlp_b")


def _to_rows(flat, row_align):
    n = flat.shape[-1]
    rows = _round_up(-(-n // PACK_COLS), row_align)
    pad = rows * PACK_COLS - n
    if pad:
        flat = jnp.pad(flat, [(0, 0)] * (flat.ndim - 1) + [(0, pad)])
    return flat.reshape(flat.shape[:-1] + (rows, PACK_COLS))


def _shard_shape(K, N, axis):
    return (K // N_CHIPS, N) if axis == 0 else (K, N // N_CHIPS)


def _gather_shards(stacked, K, N, axis):
    if axis == 0:
        return stacked.reshape(K, N)
    return stacked.transpose(1, 0, 2).reshape(K, N)


def _split_shards(full, K, N, axis):
    if axis == 0:
        return full.reshape(N_CHIPS, -1)
    return full.reshape(K, N_CHIPS, N // N_CHIPS).transpose(1, 0, 2).reshape(N_CHIPS, -1)


def kernel(x, meta_tokens, ev_w_in, ev_conv_w, ev_conv_b, ev_w_rg_a, ev_b_rg_a, ev_w_rg_x, ev_b_rg_x, ev_lru_lambda, ev_q_norm_g, ev_w_uq, ev_kv_norm_g, ev_w_ukv, ev_w_out, od_w_in, od_w_out, ln_mix_g, ln_mix_b, mlp_w1, mlp_w2, ln_mlp_g, ln_mlp_b, loss_target, m_meta_tokens, m_ev_w_in, m_ev_conv_w, m_ev_conv_b, m_ev_w_rg_a, m_ev_b_rg_a, m_ev_w_rg_x, m_ev_b_rg_x, m_ev_lru_lambda, m_ev_q_norm_g, m_ev_w_uq, m_ev_kv_norm_g, m_ev_w_ukv, m_ev_w_out, m_od_w_in, m_od_w_out, m_ln_mix_g, m_ln_mix_b, m_mlp_w1, m_mlp_w2, m_ln_mlp_g, m_ln_mlp_b, v_meta_tokens, v_ev_w_in, v_ev_conv_w, v_ev_conv_b, v_ev_w_rg_a, v_ev_b_rg_a, v_ev_w_rg_x, v_ev_b_rg_x, v_ev_lru_lambda, v_ev_q_norm_g, v_ev_w_uq, v_ev_kv_norm_g, v_ev_w_ukv, v_ev_w_out, v_od_w_in, v_od_w_out, v_ln_mix_g, v_ln_mix_b, v_mlp_w1, v_mlp_w2, v_ln_mlp_g, v_ln_mlp_b):
    given = dict(locals())
    local_big = {"ev_w_in": ev_w_in[0], "ev_w_uq": ev_w_uq[0], "ev_w_ukv": ev_w_ukv[0], "ev_w_out": ev_w_out[0],
                 "od_w_in": od_w_in[0], "od_w_out": od_w_out[0], "mlp_w1_0": mlp_w1[0], "mlp_w1_1": mlp_w1[1],
                 "mlp_w2_0": mlp_w2[0], "mlp_w2_1": mlp_w2[1]}

    sizes = [math.prod(_shard_shape(K, N, ax)) for _, K, N, ax in BIG_SPECS]
    packed = _to_rows(jnp.concatenate([local_big[n].astype(MXU_DTYPE).reshape(-1) for n, _, _, _ in BIG_SPECS]), 256)
    gathered = _with_own(_allgather_chips(packed, "weight_allgather"), packed).reshape(N_CHIPS, -1)
    wfull, off = {}, 0
    for (n, K, N, ax), sz in zip(BIG_SPECS, sizes):
        wfull[n] = _gather_shards(gathered[:, off:off + sz].reshape((N_CHIPS,) + _shard_shape(K, N, ax)), K, N, ax)
        off += sz
    small = _to_rows(jnp.concatenate([meta_tokens.reshape(-1), ev_conv_w.reshape(-1)]), 16)
    small = _with_own(_allgather_chips(small, "small_allgather"), small).reshape(N_CHIPS, -1)
    n_meta, n_conv = meta_tokens.size, ev_conv_w.size
    meta_full = _gather_shards(small[:, :n_meta].reshape(N_CHIPS, N_META, D_MODEL // N_CHIPS), N_META, D_MODEL, 1)
    conv_full = _gather_shards(small[:, n_meta:n_meta + n_conv].reshape(N_CHIPS, CONV_WIDTH, LRU_WIDTH // N_CHIPS),
                               CONV_WIDTH, LRU_WIDTH, 1)

    diff = {n: jnp.zeros((N_CHIPS, K, N // N_CHIPS) if n in COL_SHARD_GRADS else (K, N), F32) for n, K, N, _ in BIG_SPECS}
    diff.update({n: given[n] for n in REPLICATED})
    diff.update(x=x, meta_tokens=meta_full, ev_conv_w=conv_full)
    loss, g = jax.value_and_grad(_local_loss)(diff, wfull, loss_target)
    loss = lax.psum(loss, ("x", "y", "c"))

    def blocks_of(n, K, N, ax):
        if n in COL_SHARD_GRADS:
            blocks = g[n]
        elif ax == 0:
            blocks = g[n].reshape(N_CHIPS, K // N_CHIPS, N)
        else:
            blocks = g[n].reshape(K, N_CHIPS, N // N_CHIPS).transpose(1, 0, 2)
        return blocks.reshape(N_CHIPS, 2, blocks.shape[1] // 2, blocks.shape[2])

    repl = jnp.concatenate([g[n].reshape(-1) for n in REPLICATED]).reshape(N_CHIPS, -1)
    small = [_split_shards(g["meta_tokens"], N_META, D_MODEL, 1), _split_shards(g["ev_conv_w"], CONV_WIDTH, LRU_WIDTH, 1), repl]
    small = [pc.reshape(N_CHIPS, 2, -1) for pc in small]
    n_small = sum(pc.shape[2] for pc in small)
    small.append(jnp.zeros((N_CHIPS, 2, _round_up(n_small, 32 * PACK_COLS) - n_small), F32))
    p_small = jnp.concatenate(small, axis=2).reshape(N_CHIPS, 2, -1, PACK_COLS)
    reduced = _reduce_to_chips([blocks_of(*spec) for spec in BIG_SPECS] + [p_small], [MXU_DTYPE] * len(BIG_SPECS) + [F32])
    red_big = dict(zip([spec[0] for spec in BIG_SPECS], reduced))
    red_small = reduced[-1].reshape(2, -1)

    grads = {}
    for name, parts in BIG_PARAMS:
        grads[name] = jnp.stack([red_big[part].reshape(given[name].shape[1:]) for part in parts])

    def take(off, sz):
        return jnp.concatenate([red_small[0, off // 2:(off + sz) // 2], red_small[1, off // 2:(off + sz) // 2]])

    off = 0
    for name in SMALL_SHARDED:
        sz = given[name].size
        grads[name] = take(off, sz).reshape(given[name].shape)
        off += sz
    n_repl = repl.shape[1]
    own_repl = _to_rows(take(off, n_repl), 16)
    repl_all = _with_own(_allgather_chips(own_repl, "replicated_allgather"), own_repl).reshape(N_CHIPS, -1)[:, :n_repl].reshape(-1)
    off = 0
    for name in REPLICATED:
        sz = given[name].size
        grads[name] = repl_all[off:off + sz].reshape(given[name].shape)
        off += sz

    delta, new_m, new_v = {}, {}, {}
    for name, _ in BIG_PARAMS:
        shp = given[name].shape
        two_d = (-1, shp[-1])
        d, nm, nv = _adamw(given[name].reshape(two_d), grads[name].reshape(two_d), given["m_" + name].reshape(two_d),
                           given["v_" + name].reshape(two_d), "adamw_" + name)
        delta[name], new_m[name], new_v[name] = d.reshape(shp), nm.reshape(shp), nv.reshape(shp)
    smalls = SMALL_SHARDED + REPLICATED

    def pack_small(get):
        return _to_rows(jnp.concatenate([get(n).reshape(-1) for n in smalls]), 8)

    outs = _adamw(pack_small(lambda n: given[n]), pack_small(lambda n: grads[n]), pack_small(lambda n: given["m_" + n]),
                  pack_small(lambda n: given["v_" + n]), "adamw_small")
    for res, flat in zip((delta, new_m, new_v), outs):
        flat, off = flat.reshape(-1), 0
        for n in smalls:
            sz = given[n].size
            res[n] = flat[off:off + sz].reshape(given[n].shape)
            off += sz

    return (loss, g["x"], *[grads[n] for n in WEIGHT_NAMES], *[delta[n] for n in WEIGHT_NAMES],
            *[new_m[n] for n in WEIGHT_NAMES], *[new_v[n] for n in WEIGHT_NAMES])
```

```python
import functools
import math

import jax
import jax.numpy as jnp
from jax import lax
from jax.experimental import pallas as pl
from jax.experimental.pallas import tpu as pltpu

F32 = jnp.float32
MXU_DTYPE = jnp.bfloat16

D_MODEL = 1024
N_META = 16
LRU_WIDTH = 512
LRU_HEADS = 4
LRU_HEAD_DIM = 128
CONV_WIDTH = 4
LRU_C = 8.0
MLA_HEADS = 8
MLA_NOPE = 64
MLA_ROPE = 32
MLA_V = 64
MLA_Q_RANK = 256
MLA_KV_RANK = 128
RET_HEADS = 4
RET_QK_DIM = 256
RET_V_DIM = 512
D_FF = 4096
ROPE_BASE = 10000.0
DN_ALPHA = 4.0 ** 0.25
EPS = 1e-5
NEG_INF = -1e30
SEQ_BLOCK = 128

ADAM_LR = 0.001
ADAM_B1 = 0.9
ADAM_B2 = 0.999
ADAM_EPS = 1e-08
ADAM_WD = 0.01
ADAM_STEP = 10

PACK_COLS = 1024
N_CHIPS = 4

MESH = pl.DeviceIdType.MESH


def _pick(n, target, align):
    best = None
    for t in range(align, min(n, target) + 1, align):
        if n % t == 0:
            best = t
    return n if best is None else best


def _round_up(n, m):
    return (n + m - 1) // m * m


def _relu2(a):
    r = jnp.maximum(a, 0.0)
    return r * r


def _mm_nn(a, w, act, name, out_dtype=F32):
    M, K = a.shape
    _, N = w.shape
    tm = _pick(M, 1088 if K * a.dtype.itemsize <= 4096 else 544, 8)
    tn = _pick(N, 1024, 128)

    def body(a_ref, w_ref, o_ref):
        av = a_ref[...]
        if act:
            av = _relu2(av.astype(F32))
        o_ref[...] = jnp.dot(av.astype(MXU_DTYPE), w_ref[...].astype(MXU_DTYPE), preferred_element_type=F32).astype(out_dtype)

    return pl.pallas_call(
        body, name=name,
        grid=(M // tm, N // tn),
        in_specs=[pl.BlockSpec((tm, K), lambda i, j: (i, 0)), pl.BlockSpec((K, tn), lambda i, j: (0, j))],
        out_specs=pl.BlockSpec((tm, tn), lambda i, j: (i, j)),
        out_shape=jax.ShapeDtypeStruct((M, N), out_dtype),
        compiler_params=pltpu.CompilerParams(dimension_semantics=("parallel", "arbitrary")),
    )(a, w)


def _mm_nt(g, w, a_src, name, out_dtype=F32):
    M, N = g.shape
    K, _ = w.shape
    tk = N if N * g.dtype.itemsize <= 8192 else _pick(N, 2048, 128)
    nk = N // tk
    tm = _pick(M, 1088 if tk * g.dtype.itemsize <= 4096 else 544, 8)
    tn = _pick(K, 1024, 128)
    has_src = a_src is not None
    assert nk == 1 or out_dtype == F32

    def body(*refs):
        if has_src:
            g_ref, w_ref, s_ref, o_ref = refs
        else:
            g_ref, w_ref, o_ref = refs
        r = lax.dot_general(g_ref[...].astype(MXU_DTYPE), w_ref[...].astype(MXU_DTYPE),
                            (((1,), (1,)), ((), ())), preferred_element_type=F32)
        if has_src:
            r = r * (2.0 * jnp.maximum(s_ref[...].astype(F32), 0.0))
        if nk == 1:
            o_ref[...] = r.astype(out_dtype)
        else:
            k = pl.program_id(2)

            @pl.when(k == 0)
            def _():
                o_ref[...] = r

            @pl.when(k > 0)
            def _():
                o_ref[...] += r

    in_specs = [pl.BlockSpec((tm, tk), lambda i, j, k: (i, k)), pl.BlockSpec((tn, tk), lambda i, j, k: (j, k))]
    args = [g, w]
    if has_src:
        assert nk == 1
        in_specs.append(pl.BlockSpec((tm, tn), lambda i, j, k: (i, j)))
        args.append(a_src)
    return pl.pallas_call(
        body, name=name,
        grid=(M // tm, K // tn, nk),
        in_specs=in_specs,
        out_specs=pl.BlockSpec((tm, tn), lambda i, j, k: (i, j)),
        out_shape=jax.ShapeDtypeStruct((M, K), out_dtype),
        compiler_params=pltpu.CompilerParams(dimension_semantics=("parallel", "parallel", "arbitrary")),
    )(*args)


def _mm_tn(a, g, act, name, col_shards=1):
    M, K = a.shape
    _, N = g.shape
    n = N // col_shards
    tm, tn, tk = _pick(K, 1024, 128), _pick(n, 1024, 128), _pick(M, 1088, 8)
    nk = M // tk
    per = n // tn

    def body(a_ref, g_ref, o_ref):
        k = pl.program_id(2)
        av = a_ref[...]
        if act:
            av = _relu2(av.astype(F32))
        r = lax.dot_general(av.astype(MXU_DTYPE), g_ref[...].astype(MXU_DTYPE),
                            (((0,), (0,)), ((), ())), preferred_element_type=F32)

        @pl.when(k == 0)
        def _():
            o_ref[...] = r

        @pl.when(k > 0)
        def _():
            o_ref[...] += r

    if col_shards == 1:
        out_spec, out_shape = pl.BlockSpec((tm, tn), lambda i, j, k: (i, j)), (K, N)
    else:
        out_spec, out_shape = pl.BlockSpec((None, tm, tn), lambda i, j, k: (j // per, i, j % per)), (col_shards, K, n)
    return pl.pallas_call(
        body, name=name,
        grid=(K // tm, N // tn, nk),
        in_specs=[pl.BlockSpec((tk, tm), lambda i, j, k: (k, i)), pl.BlockSpec((tk, tn), lambda i, j, k: (k, j))],
        out_specs=out_spec,
        out_shape=jax.ShapeDtypeStruct(out_shape, F32),
        compiler_params=pltpu.CompilerParams(dimension_semantics=("parallel", "parallel", "arbitrary")),
    )(a, g)


@functools.partial(jax.custom_vjp, nondiff_argnums=(3, 4, 5, 6))
def matmul(a, w, w_grad_slot, act, name, out_dtype, col_shards):
    return _mm_nn(a, w, act, name + "_fwd", out_dtype)


def _matmul_fwd(a, w, w_grad_slot, act, name, out_dtype, col_shards):
    return _mm_nn(a, w, act, name + "_fwd", out_dtype), (a, w)


def _matmul_bwd(act, name, out_dtype, col_shards, res, g):
    a, w = res
    da = _mm_nt(g, w, a if act else None, name + "_dx")
    dw = _mm_tn(a, g, act, name + "_dw", col_shards)
    return da, None, dw


matmul.defvjp(_matmul_fwd, _matmul_bwd)


@functools.partial(jax.custom_vjp, nondiff_argnums=(5,))
def mlp(h, w1, w2, w1_grad_slot, w2_grad_slot, name):
    u = _mm_nn(h, w1, False, name + "_w1_fwd", out_dtype=MXU_DTYPE)
    return _mm_nn(u, w2, True, name + "_w2_fwd")


def _mlp_fwd(h, w1, w2, w1_grad_slot, w2_grad_slot, name):
    u = _mm_nn(h, w1, False, name + "_w1_fwd", out_dtype=MXU_DTYPE)
    return _mm_nn(u, w2, True, name + "_w2_fwd"), (h, u, w1, w2)


def _mlp_bwd(name, res, df):
    h, u, w1, w2 = res
    du = _mm_nt(df, w2, u, name + "_w2_dx", out_dtype=MXU_DTYPE)
    dw2 = _mm_tn(u, df, True, name + "_w2_dw")
    dh = _mm_nt(du, w1, None, name + "_w1_dx")
    dw1 = _mm_tn(h, du, False, name + "_w1_dw", N_CHIPS)
    return dh, None, None, dw1, dw2


mlp.defvjp(_mlp_fwd, _mlp_bwd)


def _ln_stats(z):
    mu = jnp.mean(z, axis=-1, keepdims=True)
    zc = z - mu
    var = jnp.mean(zc * zc, axis=-1, keepdims=True)
    return zc, lax.rsqrt(var + EPS)


def _ln_fwd_call(resid, branch, g, b, name):
    M, D = resid.shape
    tm = _pick(M, 544, 8)

    def body(r_ref, br_ref, g_ref, b_ref, o_ref):
        zc, rstd = _ln_stats(DN_ALPHA * r_ref[...] + br_ref[...])
        o_ref[...] = zc * rstd * g_ref[...] + b_ref[...]

    row = pl.BlockSpec((tm, D), lambda i: (i, 0))
    vec = pl.BlockSpec((1, D), lambda i: (0, 0))
    return pl.pallas_call(
        body, name=name, grid=(M // tm,), in_specs=[row, row, vec, vec], out_specs=row,
        out_shape=jax.ShapeDtypeStruct((M, D), F32),
        compiler_params=pltpu.CompilerParams(dimension_semantics=("parallel",)),
    )(resid, branch, g.reshape(1, D), b.reshape(1, D))


def _ln_bwd_call(resid, branch, g, dy, name):
    M, D = resid.shape
    tm = _pick(M, 544, 8)

    def body(r_ref, br_ref, g_ref, dy_ref, dz_ref, dg_ref, db_ref):
        @pl.when(pl.program_id(0) == 0)
        def _():
            dg_ref[...] = jnp.zeros_like(dg_ref)
            db_ref[...] = jnp.zeros_like(db_ref)

        zc, rstd = _ln_stats(DN_ALPHA * r_ref[...] + br_ref[...])
        xhat = zc * rstd
        dy = dy_ref[...]
        dxh = dy * g_ref[...]
        m1 = jnp.mean(dxh, axis=-1, keepdims=True)
        m2 = jnp.mean(dxh * xhat, axis=-1, keepdims=True)
        dz_ref[...] = rstd * (dxh - m1 - xhat * m2)
        dg_ref[...] += jnp.sum(dy * xhat, axis=0, keepdims=True)
        db_ref[...] += jnp.sum(dy, axis=0, keepdims=True)

    row = pl.BlockSpec((tm, D), lambda i: (i, 0))
    vec = pl.BlockSpec((1, D), lambda i: (0, 0))
    return pl.pallas_call(
        body, name=name, grid=(M // tm,), in_specs=[row, row, vec, row], out_specs=[row, vec, vec],
        out_shape=[jax.ShapeDtypeStruct((M, D), F32), jax.ShapeDtypeStruct((1, D), F32), jax.ShapeDtypeStruct((1, D), F32)],
        compiler_params=pltpu.CompilerParams(dimension_semantics=("arbitrary",)),
    )(resid, branch, g.reshape(1, D), dy)


@functools.partial(jax.custom_vjp, nondiff_argnums=(4,))
def deepnorm(resid, branch, g, b, name):
    return _ln_fwd_call(resid, branch, g, b, name + "_fwd")


def _deepnorm_fwd(resid, branch, g, b, name):
    return _ln_fwd_call(resid, branch, g, b, name + "_fwd"), (resid, branch, g)


def _deepnorm_bwd(name, res, dy):
    resid, branch, g = res
    dz, dg, db = _ln_bwd_call(resid, branch, g, dy, name + "_bwd")
    return DN_ALPHA * dz, dz, dg.reshape(g.shape), db.reshape(g.shape)


deepnorm.defvjp(_deepnorm_fwd, _deepnorm_bwd)


def _rms_fwd_call(x, g, name, col_block=0):
    R = x.shape[0]
    W = g.shape[-1]
    tr = _pick(R, 1088, 8)

    def body(x_ref, g_ref, o_ref):
        xv = x_ref[...]
        rstd = lax.rsqrt(jnp.mean(xv * xv, axis=-1, keepdims=True) + EPS)
        o_ref[...] = xv * rstd * g_ref[...]

    vec = pl.BlockSpec((1, W), lambda i: (0, 0))
    return pl.pallas_call(
        body, name=name, grid=(R // tr,), in_specs=[pl.BlockSpec((tr, W), lambda i: (i, col_block)), vec],
        out_specs=pl.BlockSpec((tr, W), lambda i: (i, 0)), out_shape=jax.ShapeDtypeStruct((R, W), F32),
        compiler_params=pltpu.CompilerParams(dimension_semantics=("parallel",)),
    )(x, g.reshape(1, W))


def _rms_bwd_call(x, g, dy, name, col_block=0):
    R = x.shape[0]
    W = g.shape[-1]
    tr = _pick(R, 1088, 8)

    def body(x_ref, g_ref, dy_ref, dx_ref, dg_ref):
        @pl.when(pl.program_id(0) == 0)
        def _():
            dg_ref[...] = jnp.zeros_like(dg_ref)

        xv = x_ref[...]
        rstd = lax.rsqrt(jnp.mean(xv * xv, axis=-1, keepdims=True) + EPS)
        xhat = xv * rstd
        dy = dy_ref[...]
        dxh = dy * g_ref[...]
        dx_ref[...] = rstd * (dxh - xhat * jnp.mean(dxh * xhat, axis=-1, keepdims=True))
        dg_ref[...] += jnp.sum(dy * xhat, axis=0, keepdims=True)

    row = pl.BlockSpec((tr, W), lambda i: (i, 0))
    vec = pl.BlockSpec((1, W), lambda i: (0, 0))
    return pl.pallas_call(
        body, name=name, grid=(R // tr,), in_specs=[pl.BlockSpec((tr, W), lambda i: (i, col_block)), vec, row], out_specs=[row, vec],
        out_shape=[jax.ShapeDtypeStruct((R, W), F32), jax.ShapeDtypeStruct((1, W), F32)],
        compiler_params=pltpu.CompilerParams(dimension_semantics=("arbitrary",)),
    )(x, g.reshape(1, W), dy)


def _loss_call(y, tgt, name):
    R, D = y.shape
    tr = _pick(R, 512, 8)

    def body(y_ref, t_ref, dy_ref, acc_ref):
        @pl.when(pl.program_id(0) == 0)
        def _():
            acc_ref[...] = jnp.zeros_like(acc_ref)

        e = y_ref[...] - t_ref[...]
        dy_ref[...] = e * (1.0 / D)
        acc_ref[...] += jnp.sum(jnp.sum(e * e, axis=-1, keepdims=True), axis=0, keepdims=True) * (0.5 / D)

    row = pl.BlockSpec((tr, D), lambda i: (i, 0))
    one = pl.BlockSpec((1, 1), lambda i: (0, 0))
    return pl.pallas_call(
        body, name=name, grid=(R // tr,), in_specs=[row, row], out_specs=[row, one],
        out_shape=[jax.ShapeDtypeStruct((R, D), F32), jax.ShapeDtypeStruct((1, 1), F32)],
        compiler_params=pltpu.CompilerParams(dimension_semantics=("arbitrary",)),
    )(y, tgt)


@jax.custom_vjp
def loss_head(y, tgt):
    return _loss_call(y, tgt, "loss_head")[1][0, 0]


def _loss_head_fwd(y, tgt):
    dy, acc = _loss_call(y, tgt, "loss_head")
    return acc[0, 0], dy


def _loss_head_bwd(dy, ct):
    return ct * dy, None


loss_head.defvjp(_loss_head_fwd, _loss_head_bwd)


_GELU_C = math.sqrt(2.0 / math.pi)


def _gelu_parts(x):
    x2 = x * x
    t = jnp.tanh(_GELU_C * (x + 0.044715 * x * x2))
    gelu = 0.5 * x * (1.0 + t)
    dgelu = 0.5 * (1.0 + t) + 0.5 * x * (1.0 - t * t) * (_GELU_C * (1.0 + 3.0 * 0.044715 * x2))
    return gelu, dgelu


def _sigmoid(x):
    return 1.0 / (1.0 + jnp.exp(-x))


def _scan8(a, b, carry, reverse):
    row = lax.broadcasted_iota(jnp.int32, a.shape, 0)
    for s in (1, 2, 4):
        shift = 8 - s if reverse else s
        keep = (row < 8 - s) if reverse else (row >= s)
        b = jnp.where(keep, a * pltpu.roll(b, shift, 0) + b, b)
        a = jnp.where(keep, a * pltpu.roll(a, shift, 0), a)
    return a * carry + b


def _lru_pre(prec_ref, prev_ref, first, cw_ref, cb_ref, wa_ref, ba_ref, wx_ref, bx_ref, sp_ref):
    tc = prec_ref.shape[0]
    prev = jnp.where(first, 0.0, prev_ref[...])
    ext = jnp.concatenate([prev, prec_ref[...]], axis=0)
    cw = cw_ref[...]
    taps = [ext[8:] if k == CONV_WIDTH - 1 else pltpu.roll(ext, CONV_WIDTH - 1 - k, 0)[8:] for k in range(CONV_WIDTH)]
    xc = cb_ref[...] + sum(cw[k:k + 1, :] * taps[k] for k in range(CONV_WIDTH))
    ga, gx = [], []
    for h in range(LRU_HEADS):
        xh = xc[:, h * LRU_HEAD_DIM:(h + 1) * LRU_HEAD_DIM].astype(MXU_DTYPE)
        ga.append(jnp.dot(xh, wa_ref[h].astype(MXU_DTYPE), preferred_element_type=F32))
        gx.append(jnp.dot(xh, wx_ref[h].astype(MXU_DTYPE), preferred_element_type=F32))
    r = _sigmoid(jnp.concatenate(ga, axis=1) + ba_ref[...])
    i = _sigmoid(jnp.concatenate(gx, axis=1) + bx_ref[...])
    log_a = -LRU_C * r * sp_ref[...]
    a = jnp.exp(log_a)
    a2 = a * a
    mult = jnp.sqrt(-jnp.tanh(log_a) * (a2 + 1.0))
    return taps, xc, r, i, a, a2, mult


def _lru_fwd_call(p, cw, cb, wa, ba, wx, bx, sp):
    B, Tp, _ = p.shape
    W = LRU_WIDTH
    tc = SEQ_BLOCK
    nc = Tp // tc

    def body(pg_ref, prec_ref, prev_ref, cw_ref, cb_ref, wa_ref, ba_ref, wx_ref, bx_ref, sp_ref, y_ref, h_ref, carry_ref):
        first = pl.program_id(1) == 0

        @pl.when(first)
        def _():
            carry_ref[...] = jnp.zeros_like(carry_ref)

        _, xc, r, i, a, a2, mult = _lru_pre(prec_ref, prev_ref, first, cw_ref, cb_ref, wa_ref, ba_ref, wx_ref, bx_ref, sp_ref)
        b = mult * (i * xc)
        carry = carry_ref[0:1, :]
        for t in range(tc // 8):
            h = _scan8(a[8 * t:8 * t + 8], b[8 * t:8 * t + 8], carry, False)
            h_ref[8 * t:8 * t + 8, :] = h
            carry = h[7:8, :]
        carry_ref[...] = jnp.broadcast_to(carry, carry_ref.shape)
        y_ref[...] = h_ref[...] * _gelu_parts(pg_ref[...])[0]

    cur = pl.BlockSpec((None, tc, W), lambda b, j: (b, j, 0))
    rec = pl.BlockSpec((None, tc, W), lambda b, j: (b, j, 1))
    prev = pl.BlockSpec((None, 8, W), lambda b, j: (b, jnp.maximum(j * (tc // 8) - 1, 0), 1))
    vec = pl.BlockSpec((1, W), lambda b, j: (0, 0))
    cws = pl.BlockSpec((CONV_WIDTH, W), lambda b, j: (0, 0))
    wsp = pl.BlockSpec((LRU_HEADS, LRU_HEAD_DIM, LRU_HEAD_DIM), lambda b, j: (0, 0, 0))
    return pl.pallas_call(
        body, name="lru_fwd", grid=(B, nc),
        in_specs=[cur, rec, prev, cws, vec, wsp, vec, wsp, vec, vec],
        out_specs=[cur, cur],
        out_shape=[jax.ShapeDtypeStruct((B, Tp, W), F32), jax.ShapeDtypeStruct((B, Tp, W), F32)],
        scratch_shapes=[pltpu.VMEM((8, W), F32)],
        compiler_params=pltpu.CompilerParams(dimension_semantics=("arbitrary", "arbitrary")),
    )(p, p, p, cw, cb, wa, ba, wx, bx, sp)


def _lru_bwd_call(p, hseq, dy, cw, cb, wa, ba, wx, bx, sp):
    B, Tp, _ = p.shape
    W = LRU_WIDTH
    tc = SEQ_BLOCK
    nc = Tp // tc
    HD = LRU_HEAD_DIM

    def body(pg_ref, prec_ref, prev_ref, h_ref, hprev_ref, dy_ref, cw_ref, cb_ref, wa_ref, ba_ref, wx_ref, bx_ref, sp_ref,
             dpg_ref, dprec_ref, dcw_ref, dcb_ref, dwa_ref, dba_ref, dwx_ref, dbx_ref, dsp_ref,
             gcar_ref, anext_ref, halo_ref, g_ref):
        j = pl.program_id(1)
        first = j == nc - 1
        last = j == 0

        @pl.when(jnp.logical_and(pl.program_id(0) == 0, last))
        def _():
            for ref in (dcw_ref, dcb_ref, dwa_ref, dba_ref, dwx_ref, dbx_ref, dsp_ref):
                ref[...] = jnp.zeros_like(ref)

        @pl.when(last)
        def _():
            gcar_ref[...] = jnp.zeros_like(gcar_ref)
            anext_ref[...] = jnp.zeros_like(anext_ref)
            halo_ref[...] = jnp.zeros_like(halo_ref)

        taps, xc, r, i, a, a2, mult = _lru_pre(prec_ref, prev_ref, first, cw_ref, cb_ref, wa_ref, ba_ref, wx_ref, bx_ref, sp_ref)
        row = lax.broadcasted_iota(jnp.int32, (tc, W), 0)
        gelu, dgelu = _gelu_parts(pg_ref[...])
        dy = dy_ref[...]
        hcur = h_ref[...]
        dpg_ref[...] = dy * hcur * dgelu
        dh = dy * gelu
        a_next = jnp.where(row == tc - 1, anext_ref[0:1, :], pltpu.roll(a, tc - 1, 0))
        carry = gcar_ref[0:1, :]
        for t in reversed(range(tc // 8)):
            g = _scan8(a_next[8 * t:8 * t + 8], dh[8 * t:8 * t + 8], carry, True)
            g_ref[8 * t:8 * t + 8, :] = g
            carry = g[0:1, :]
        gcar_ref[...] = jnp.broadcast_to(carry, gcar_ref.shape)
        anext_ref[...] = jnp.broadcast_to(a[0:1, :], anext_ref.shape)
        G = g_ref[...]
        h_before = jnp.where(first, 0.0, hprev_ref[7:8, :])
        hprev = jnp.where(row == 0, h_before, pltpu.roll(hcur, 1, 0))
        d_a = G * hprev
        gx_ = G * xc
        d_mult = gx_ * i
        d_i = gx_ * mult
        dxc = G * (mult * i)
        d_la = d_a * a - d_mult * (a2 / mult)
        sp = sp_ref[...]
        d_r = d_la * (-LRU_C * sp)
        dsp_ref[...] += jnp.sum(d_la * (-LRU_C * r), axis=0, keepdims=True)
        dga = d_r * r * (1.0 - r)
        dgx = d_i * i * (1.0 - i)
        dba_ref[...] += jnp.sum(dga, axis=0, keepdims=True)
        dbx_ref[...] += jnp.sum(dgx, axis=0, keepdims=True)
        back = []
        for h in range(LRU_HEADS):
            sl = slice(h * HD, (h + 1) * HD)
            xh = xc[:, sl].astype(MXU_DTYPE)
            ah = dga[:, sl].astype(MXU_DTYPE)
            bh = dgx[:, sl].astype(MXU_DTYPE)
            tn = (((0,), (0,)), ((), ()))
            nt = (((1,), (1,)), ((), ()))
            dwa_ref[h] += lax.dot_general(xh, ah, tn, preferred_element_type=F32)
            dwx_ref[h] += lax.dot_general(xh, bh, tn, preferred_element_type=F32)
            back.append(lax.dot_general(ah, wa_ref[h].astype(MXU_DTYPE), nt, preferred_element_type=F32)
                        + lax.dot_general(bh, wx_ref[h].astype(MXU_DTYPE), nt, preferred_element_type=F32))
        dxc = dxc + jnp.concatenate(back, axis=1)
        dcb_ref[...] += jnp.sum(dxc, axis=0, keepdims=True)
        for k in range(CONV_WIDTH):
            dcw_ref[k:k + 1, :] += jnp.sum(dxc * taps[k], axis=0, keepdims=True)
        ext = jnp.concatenate([dxc, halo_ref[...]], axis=0)
        cw = cw_ref[...]
        acc = cw[CONV_WIDTH - 1:CONV_WIDTH, :] * dxc
        for k in range(CONV_WIDTH - 1):
            s = CONV_WIDTH - 1 - k
            acc = acc + cw[k:k + 1, :] * pltpu.roll(ext, tc + 8 - s, 0)[:tc]
        dprec_ref[...] = acc
        halo_ref[...] = dxc[0:8, :]

    rev = lambda j: nc - 1 - j
    cur = pl.BlockSpec((None, tc, W), lambda b, j: (b, rev(j), 0))
    rec = pl.BlockSpec((None, tc, W), lambda b, j: (b, rev(j), 1))
    prev = pl.BlockSpec((None, 8, W), lambda b, j: (b, jnp.maximum(rev(j) * (tc // 8) - 1, 0), 0))
    prev_rec = pl.BlockSpec((None, 8, W), lambda b, j: (b, jnp.maximum(rev(j) * (tc // 8) - 1, 0), 1))
    vec = pl.BlockSpec((1, W), lambda b, j: (0, 0))
    cws = pl.BlockSpec((CONV_WIDTH, W), lambda b, j: (0, 0))
    wsp = pl.BlockSpec((LRU_HEADS, HD, HD), lambda b, j: (0, 0, 0))
    seq = jax.ShapeDtypeStruct((B, Tp, W), F32)
    vs = jax.ShapeDtypeStruct((1, W), F32)
    ws = jax.ShapeDtypeStruct((LRU_HEADS, HD, HD), F32)
    return pl.pallas_call(
        body, name="lru_bwd", grid=(B, nc),
        in_specs=[cur, rec, prev_rec, cur, prev, cur, cws, vec, wsp, vec, wsp, vec, vec],
        out_specs=[cur, cur, cws, vec, wsp, vec, wsp, vec, vec],
        out_shape=[seq, seq, jax.ShapeDtypeStruct((CONV_WIDTH, W), F32), vs, ws, vs, ws, vs, vs],
        scratch_shapes=[pltpu.VMEM((8, W), F32), pltpu.VMEM((8, W), F32), pltpu.VMEM((8, W), F32), pltpu.VMEM((tc, W), F32)],
        compiler_params=pltpu.CompilerParams(dimension_semantics=("arbitrary", "arbitrary")),
    )(p, p, p, hseq, hseq, dy, cw, cb, wa, ba, wx, bx, sp)


_Q_BLOCK = 2 * LRU_WIDTH // MLA_Q_RANK
_KV_BLOCK = (2 * LRU_WIDTH + MLA_Q_RANK) // MLA_KV_RANK
_KPE_START = 2 * LRU_WIDTH + MLA_Q_RANK + MLA_KV_RANK


@jax.custom_vjp
def even_front(p, cw, cb, wa, ba, wx, bx, sp, gq, gkv):
    return _even_front_fwd(p, cw, cb, wa, ba, wx, bx, sp, gq, gkv)[0]


def _even_front_fwd(p, cw, cb, wa, ba, wx, bx, sp, gq, gkv):
    B, Tp, W = p.shape
    p2d = p.reshape(B * Tp, W)
    y, hseq = _lru_fwd_call(p, cw, cb, wa, ba, wx, bx, sp)
    qn = _rms_fwd_call(p2d, gq, "q_norm_fwd", _Q_BLOCK)
    kvn = _rms_fwd_call(p2d, gkv, "kv_norm_fwd", _KV_BLOCK)
    return (y, qn, kvn, p2d[:, _KPE_START:]), (p, hseq, cw, cb, wa, ba, wx, bx, sp, gq, gkv)


def _even_front_bwd(res, cts):
    p, hseq, cw, cb, wa, ba, wx, bx, sp, gq, gkv = res
    dy, dqn, dkvn, dkpe = cts
    B, Tp, W = p.shape
    p2d = p.reshape(B * Tp, W)
    dpg, dprec, dcw, dcb, dwa, dba, dwx, dbx, dsp = _lru_bwd_call(p, hseq, dy, cw, cb, wa, ba, wx, bx, sp)
    dpq, dgq = _rms_bwd_call(p2d, gq, dqn, "q_norm_bwd", _Q_BLOCK)
    dpkv, dgkv = _rms_bwd_call(p2d, gkv, dkvn, "kv_norm_bwd", _KV_BLOCK)
    dp = jnp.concatenate([dpg.reshape(B * Tp, -1), dprec.reshape(B * Tp, -1), dpq, dpkv, dkpe], axis=1).reshape(B, Tp, W)
    return dp, dcw, dcb, dwa, dba, dwx, dbx, dsp, dgq.reshape(gq.shape), dgkv.reshape(gkv.shape)


even_front.defvjp(_even_front_fwd, _even_front_bwd)


def _rope_tables(pos, half):
    inv = ROPE_BASE ** (-jnp.arange(half, dtype=F32) / half)
    ang = pos.astype(F32)[:, None] * inv[None, :]
    return jnp.cos(ang), jnp.sin(ang)


_NT = (((1,), (1,)), ((), ()))
_TN = (((0,), (0,)), ((), ()))
HEAD_LANES = 128
_MLA_SCALE = (MLA_NOPE + MLA_ROPE) ** -0.5


def _mask_diagonal(s, fill):
    L = s.shape[1]
    row = lax.broadcasted_iota(jnp.int32, (SEQ_BLOCK, SEQ_BLOCK), 0)
    col = lax.broadcasted_iota(jnp.int32, (SEQ_BLOCK, SEQ_BLOCK), 1)
    last = jnp.where(col <= row, s[:, L - SEQ_BLOCK:], fill)
    return last if L == SEQ_BLOCK else jnp.concatenate([s[:, :L - SEQ_BLOCK], last], axis=1)


def _mla_rope_tables(pos):
    half = MLA_ROPE // 2
    cos, sin = _rope_tables(pos, half)
    T = pos.shape[0]
    ones, zeros = jnp.ones((T, MLA_NOPE), F32), jnp.zeros((T, MLA_NOPE), F32)
    tail1, tail0 = jnp.ones((T, HEAD_LANES - MLA_NOPE - MLA_ROPE), F32), jnp.zeros((T, HEAD_LANES - MLA_NOPE - MLA_ROPE), F32)
    zh = jnp.zeros((T, half), F32)
    c = jnp.concatenate([ones, cos, cos, tail1], axis=1)
    s_up = jnp.concatenate([zeros, -sin, zh, tail0], axis=1)
    s_down = jnp.concatenate([zeros, zh, sin, tail0], axis=1)
    return c, s_up, s_down


def _rope_lanes(x, c, s_up, s_down):
    half = MLA_ROPE // 2
    return x * c + pltpu.roll(x, HEAD_LANES - half, 1) * s_up + pltpu.roll(x, half, 1) * s_down


def _unrope_lanes(d, c, s_up, s_down):
    half = MLA_ROPE // 2
    return d * c + pltpu.roll(d * s_up, half, 1) + pltpu.roll(d * s_down, HEAD_LANES - half, 1)


def _mla_operands(q_ref, kv_ref, kpe_ref, c, s_up, s_down):
    lane = lax.broadcasted_iota(jnp.int32, kv_ref.shape, 1)
    qr = _rope_lanes(q_ref[...].astype(F32), c, s_up, s_down).astype(MXU_DTYPE)
    kr = jnp.where(lane < MLA_NOPE, kv_ref[...].astype(F32), _rope_lanes(kpe_ref[...], c, s_up, s_down)).astype(MXU_DTYPE)
    return qr, kr, lane


def _mla_specs(Tp):
    head = pl.BlockSpec((None, Tp, HEAD_LANES), lambda b, h: (b, 0, h))
    shared = pl.BlockSpec((None, Tp, HEAD_LANES), lambda b, h: (b, 0, 0))
    tab = pl.BlockSpec((Tp, HEAD_LANES), lambda b, h: (0, 0))
    lse = pl.BlockSpec((None, None, Tp, 1), lambda b, h: (b, h, 0, 0))
    return head, shared, tab, lse


def _attn_fwd_call(q, kv, kpe, tabs):
    B, Tp, _ = q.shape
    nq = Tp // SEQ_BLOCK

    def body(q_ref, kv_ref, kpe_ref, c_ref, su_ref, sd_ref, o_ref, lse_ref, qr_ref, kr_ref):
        qr, kr, lane = _mla_operands(q_ref, kv_ref, kpe_ref, c_ref[...], su_ref[...], sd_ref[...])
        qr_ref[...] = qr
        kr_ref[...] = kr
        for qi in range(nq):
            L = (qi + 1) * SEQ_BLOCK
            blk = slice(qi * SEQ_BLOCK, L)
            s = lax.dot_general(qr_ref[blk, :], kr_ref[0:L, :], _NT, preferred_element_type=F32) * _MLA_SCALE
            s = _mask_diagonal(s, NEG_INF)
            m = jnp.max(s, axis=-1, keepdims=True)
            p = jnp.exp(s - m)
            l = jnp.sum(p, axis=-1, keepdims=True)
            o = jnp.dot(p.astype(MXU_DTYPE), kv_ref[0:L, :].astype(MXU_DTYPE), preferred_element_type=F32)
            o_ref[blk, :] = jnp.where(lane[blk, :] >= MLA_NOPE, o / l, 0.0)
            lse_ref[blk, :] = m + jnp.log(l)

    head, shared, tab, lse = _mla_specs(Tp)
    return pl.pallas_call(
        body, name="mla_attn_fwd", grid=(B, MLA_HEADS), in_specs=[head, head, shared, tab, tab, tab], out_specs=[head, lse],
        out_shape=[jax.ShapeDtypeStruct((B, Tp, MLA_HEADS * HEAD_LANES), F32), jax.ShapeDtypeStruct((B, MLA_HEADS, Tp, 1), F32)],
        scratch_shapes=[pltpu.VMEM((Tp, HEAD_LANES), MXU_DTYPE), pltpu.VMEM((Tp, HEAD_LANES), MXU_DTYPE)],
        compiler_params=pltpu.CompilerParams(dimension_semantics=("parallel", "parallel")),
    )(q, kv, kpe, *tabs)


def _attn_bwd_call(q, kv, kpe, tabs, o, lse, do):
    B, Tp, _ = q.shape
    nq = Tp // SEQ_BLOCK

    def body(q_ref, kv_ref, kpe_ref, c_ref, su_ref, sd_ref, o_ref, lse_ref, do_ref, dq_ref, dkv_ref, dkpe_ref,
             qr_ref, kr_ref, dqa_ref, dka_ref, dva_ref):
        c, s_up, s_down = c_ref[...], su_ref[...], sd_ref[...]
        qr, kr, lane = _mla_operands(q_ref, kv_ref, kpe_ref, c, s_up, s_down)
        qr_ref[...] = qr
        kr_ref[...] = kr
        dka_ref[...] = jnp.zeros_like(dka_ref)
        dva_ref[...] = jnp.zeros_like(dva_ref)
        for qi in range(nq):
            L = (qi + 1) * SEQ_BLOCK
            blk = slice(qi * SEQ_BLOCK, L)
            qb = qr_ref[blk, :]
            do = jnp.where(lane[blk, :] >= MLA_NOPE, do_ref[blk, :], 0.0)
            delta = jnp.sum(do * o_ref[blk, :], axis=-1, keepdims=True)
            s = lax.dot_general(qb, kr_ref[0:L, :], _NT, preferred_element_type=F32) * _MLA_SCALE
            s = _mask_diagonal(s, NEG_INF)
            p = jnp.exp(s - lse_ref[blk, :])
            dob = do.astype(MXU_DTYPE)
            dva_ref[0:L, :] += lax.dot_general(p.astype(MXU_DTYPE), dob, _TN, preferred_element_type=F32)
            dp = lax.dot_general(dob, kv_ref[0:L, :].astype(MXU_DTYPE), _NT, preferred_element_type=F32)
            ds = (p * (dp - delta) * _MLA_SCALE).astype(MXU_DTYPE)
            dqa_ref[blk, :] = jnp.dot(ds, kr_ref[0:L, :], preferred_element_type=F32)
            dka_ref[0:L, :] += lax.dot_general(ds, qb, _TN, preferred_element_type=F32)
        dq_ref[...] = _unrope_lanes(dqa_ref[...], c, s_up, s_down).astype(dq_ref.dtype)
        dk = dka_ref[...]
        dkv_ref[...] = jnp.where(lane < MLA_NOPE, dk, dva_ref[...]).astype(dkv_ref.dtype)
        dkpe = jnp.where(lane >= MLA_NOPE, _unrope_lanes(dk, c, s_up, s_down), 0.0)

        @pl.when(pl.program_id(1) == 0)
        def _():
            dkpe_ref[...] = dkpe

        @pl.when(pl.program_id(1) > 0)
        def _():
            dkpe_ref[...] += dkpe

    head, shared, tab, lse_spec = _mla_specs(Tp)
    wide = jax.ShapeDtypeStruct((B, Tp, MLA_HEADS * HEAD_LANES), q.dtype)
    acc = pltpu.VMEM((Tp, HEAD_LANES), F32)
    return pl.pallas_call(
        body, name="mla_attn_bwd", grid=(B, MLA_HEADS),
        in_specs=[head, head, shared, tab, tab, tab, head, lse_spec, head], out_specs=[head, head, shared],
        out_shape=[wide, wide, jax.ShapeDtypeStruct((B, Tp, HEAD_LANES), F32)],
        scratch_shapes=[pltpu.VMEM((Tp, HEAD_LANES), MXU_DTYPE), pltpu.VMEM((Tp, HEAD_LANES), MXU_DTYPE), acc, acc, acc],
        compiler_params=pltpu.CompilerParams(dimension_semantics=("parallel", "arbitrary")),
    )(q, kv, kpe, *tabs, o, lse, do)


@jax.custom_vjp
def mla_attention(q, kv, kpe, tabs):
    return _attn_fwd_call(q, kv, kpe, tabs)[0]


def _mla_attention_fwd(q, kv, kpe, tabs):
    o, lse = _attn_fwd_call(q, kv, kpe, tabs)
    return o, (q, kv, kpe, tabs, o, lse)


def _mla_attention_bwd(res, do):
    q, kv, kpe, tabs, o, lse = res
    dq, dkv, dkpe = _attn_bwd_call(q, kv, kpe, tabs, o, lse, do)
    return dq, dkv, dkpe, None


mla_attention.defvjp(_mla_attention_fwd, _mla_attention_bwd)


def _rope_halves(x, cos, sin):
    half = x.shape[1] // 2
    x1, x2 = x[:, :half], x[:, half:]
    return jnp.concatenate([x1 * cos - x2 * sin, x1 * sin + x2 * cos], axis=1)


def _unrope_halves(d, cos, sin):
    half = d.shape[1] // 2
    d1, d2 = d[:, :half], d[:, half:]
    return jnp.concatenate([d1 * cos + d2 * sin, d2 * cos - d1 * sin], axis=1)


_RET_K_SCALE = RET_QK_DIM ** -0.5
_RET_Q_BLOCKS = RET_HEADS
_RET_V_BLOCK0 = 2 * RET_HEADS * RET_QK_DIM // RET_V_DIM
_RET_G_BLOCK0 = _RET_V_BLOCK0 + RET_HEADS


def _ret_specs(Tp):
    q = pl.BlockSpec((None, Tp, RET_QK_DIM), lambda b, h: (b, 0, h))
    k = pl.BlockSpec((None, Tp, RET_QK_DIM), lambda b, h: (b, 0, _RET_Q_BLOCKS + h))
    v = pl.BlockSpec((None, Tp, RET_V_DIM), lambda b, h: (b, 0, _RET_V_BLOCK0 + h))
    tab = pl.BlockSpec((Tp, RET_QK_DIM // 2), lambda b, h: (0, 0))
    lg = pl.BlockSpec((None, 1, 1), lambda b, h: (h, 0, 0))
    return q, k, v, tab, lg


def _ret_operands(q_ref, k_ref, cos, sin, lg):
    t = lax.broadcasted_iota(jnp.int32, (q_ref.shape[0], 1), 0).astype(F32)
    grow, shrink = jnp.exp(-lg * t), jnp.exp(lg * t)
    qs = (_rope_halves(q_ref[...].astype(F32), cos, sin) * shrink).astype(MXU_DTYPE)
    ks = (_rope_halves(k_ref[...].astype(F32), cos, sin) * (grow * _RET_K_SCALE)).astype(MXU_DTYPE)
    return qs, ks, shrink, grow * _RET_K_SCALE


def _ret_core_fwd_call(p, cos, sin, lg):
    B, Tp, _ = p.shape
    nq = Tp // SEQ_BLOCK

    def body(q_ref, k_ref, v_ref, cos_ref, sin_ref, lg_ref, o_ref, qs_ref, ks_ref):
        qs_ref[...], ks_ref[...], _, _ = _ret_operands(q_ref, k_ref, cos_ref[...], sin_ref[...], lg_ref[...])
        for qi in range(nq):
            L = (qi + 1) * SEQ_BLOCK
            blk = slice(qi * SEQ_BLOCK, L)
            s = _mask_diagonal(lax.dot_general(qs_ref[blk, :], ks_ref[0:L, :], _NT, preferred_element_type=F32), 0.0)
            o_ref[blk, :] = jnp.dot(s.astype(MXU_DTYPE), v_ref[0:L, :].astype(MXU_DTYPE), preferred_element_type=F32)

    q, k, v, tab, lgs = _ret_specs(Tp)
    return pl.pallas_call(
        body, name="retention_fwd", grid=(B, RET_HEADS), in_specs=[q, k, v, tab, tab, lgs],
        out_specs=pl.BlockSpec((None, Tp, RET_V_DIM), lambda b, h: (b, 0, h)),
        out_shape=jax.ShapeDtypeStruct((B, Tp, RET_HEADS * RET_V_DIM), F32),
        scratch_shapes=[pltpu.VMEM((Tp, RET_QK_DIM), MXU_DTYPE), pltpu.VMEM((Tp, RET_QK_DIM), MXU_DTYPE)],
        compiler_params=pltpu.CompilerParams(dimension_semantics=("parallel", "parallel")),
    )(p, p, p, cos, sin, lg)


def _ret_core_bwd_call(p, do, cos, sin, lg):
    B, Tp, _ = p.shape
    nq = Tp // SEQ_BLOCK

    def body(q_ref, k_ref, v_ref, do_ref, cos_ref, sin_ref, lg_ref, dq_ref, dk_ref, dv_ref, qs_ref, ks_ref, dqa_ref, dka_ref, dva_ref):
        cos_, sin_ = cos_ref[...], sin_ref[...]
        qs_ref[...], ks_ref[...], q_scale, k_scale = _ret_operands(q_ref, k_ref, cos_, sin_, lg_ref[...])
        dka_ref[...] = jnp.zeros_like(dka_ref)
        dva_ref[...] = jnp.zeros_like(dva_ref)
        for qi in range(nq):
            L = (qi + 1) * SEQ_BLOCK
            blk = slice(qi * SEQ_BLOCK, L)
            qb = qs_ref[blk, :]
            dob = do_ref[blk, :].astype(MXU_DTYPE)
            s = _mask_diagonal(lax.dot_general(qb, ks_ref[0:L, :], _NT, preferred_element_type=F32), 0.0).astype(MXU_DTYPE)
            dva_ref[0:L, :] += lax.dot_general(s, dob, _TN, preferred_element_type=F32)
            ds = _mask_diagonal(lax.dot_general(dob, v_ref[0:L, :].astype(MXU_DTYPE), _NT, preferred_element_type=F32), 0.0).astype(MXU_DTYPE)
            dqa_ref[blk, :] = jnp.dot(ds, ks_ref[0:L, :], preferred_element_type=F32)
            dka_ref[0:L, :] += lax.dot_general(ds, qb, _TN, preferred_element_type=F32)
        dq_ref[...] = _unrope_halves(dqa_ref[...] * q_scale, cos_, sin_).astype(dq_ref.dtype)
        dk_ref[...] = _unrope_halves(dka_ref[...] * k_scale, cos_, sin_).astype(dk_ref.dtype)
        dv_ref[...] = dva_ref[...].astype(dv_ref.dtype)

    q, k, v, tab, lgs = _ret_specs(Tp)
    qk_out = pl.BlockSpec((None, Tp, RET_QK_DIM), lambda b, h: (b, 0, h))
    v_out = pl.BlockSpec((None, Tp, RET_V_DIM), lambda b, h: (b, 0, h))
    return pl.pallas_call(
        body, name="retention_bwd", grid=(B, RET_HEADS), in_specs=[q, k, v, v_out, tab, tab, lgs],
        out_specs=[qk_out, qk_out, v_out],
        out_shape=[jax.ShapeDtypeStruct((B, Tp, RET_HEADS * RET_QK_DIM), p.dtype), jax.ShapeDtypeStruct((B, Tp, RET_HEADS * RET_QK_DIM), p.dtype),
                   jax.ShapeDtypeStruct((B, Tp, RET_HEADS * RET_V_DIM), p.dtype)],
        scratch_shapes=[pltpu.VMEM((Tp, RET_QK_DIM), MXU_DTYPE), pltpu.VMEM((Tp, RET_QK_DIM), MXU_DTYPE),
                        pltpu.VMEM((Tp, RET_QK_DIM), F32), pltpu.VMEM((Tp, RET_QK_DIM), F32), pltpu.VMEM((Tp, RET_V_DIM), F32)],
        compiler_params=pltpu.CompilerParams(dimension_semantics=("parallel", "parallel")),
    )(p, p, p, do, cos, sin, lg)


def _ret_gate_specs(M):
    tm = _pick(M, 1088, 8)
    head = pl.BlockSpec((tm, RET_V_DIM), lambda i, h: (i, h))
    gate = pl.BlockSpec((tm, RET_V_DIM), lambda i, h: (i, _RET_G_BLOCK0 + h))
    return tm, head, gate


def _ret_gate_fwd_call(o, p2d):
    M = o.shape[0]
    tm, head, gate = _ret_gate_specs(M)

    def body(o_ref, g_ref, y_ref):
        ov = o_ref[...]
        gv = g_ref[...].astype(F32)
        rstd = lax.rsqrt(jnp.mean(ov * ov, axis=-1, keepdims=True) + EPS)
        y_ref[...] = (gv * _sigmoid(gv)) * (ov * rstd)

    return pl.pallas_call(
        body, name="retention_gate_fwd", grid=(M // tm, RET_HEADS), in_specs=[head, gate], out_specs=head,
        out_shape=jax.ShapeDtypeStruct(o.shape, F32),
        compiler_params=pltpu.CompilerParams(dimension_semantics=("parallel", "parallel")),
    )(o, p2d)


def _ret_gate_bwd_call(o, p2d, dy):
    M = o.shape[0]
    tm, head, gate = _ret_gate_specs(M)

    def body(o_ref, g_ref, dy_ref, do_ref, dg_ref):
        ov = o_ref[...]
        gv = g_ref[...].astype(F32)
        dy = dy_ref[...]
        rstd = lax.rsqrt(jnp.mean(ov * ov, axis=-1, keepdims=True) + EPS)
        on = ov * rstd
        sg = _sigmoid(gv)
        dg_ref[...] = (dy * on * (sg * (1.0 + gv * (1.0 - sg)))).astype(dg_ref.dtype)
        don = dy * (gv * sg)
        do_ref[...] = (rstd * (don - on * jnp.mean(don * on, axis=-1, keepdims=True))).astype(do_ref.dtype)

    shp = jax.ShapeDtypeStruct(o.shape, p2d.dtype)
    return pl.pallas_call(
        body, name="retention_gate_bwd", grid=(M // tm, RET_HEADS), in_specs=[head, gate, head], out_specs=[head, head],
        out_shape=[shp, shp],
        compiler_params=pltpu.CompilerParams(dimension_semantics=("parallel", "parallel")),
    )(o, p2d, dy)


def _log_gamma():
    return jnp.log(1.0 - 2.0 ** (-5.0 - jnp.arange(RET_HEADS, dtype=F32))).reshape(RET_HEADS, 1, 1)


@jax.custom_vjp
def retention_mixer(p, cos, sin):
    B, Tp, W = p.shape
    o = _ret_core_fwd_call(p, cos, sin, _log_gamma())
    return _ret_gate_fwd_call(o.reshape(B * Tp, -1), p.reshape(B * Tp, W))


def _retention_mixer_fwd(p, cos, sin):
    B, Tp, W = p.shape
    o = _ret_core_fwd_call(p, cos, sin, _log_gamma())
    return _ret_gate_fwd_call(o.reshape(B * Tp, -1), p.reshape(B * Tp, W)), (p, o, cos, sin)


def _retention_mixer_bwd(res, dy):
    p, o, cos, sin = res
    B, Tp, W = p.shape
    do, dg = _ret_gate_bwd_call(o.reshape(B * Tp, -1), p.reshape(B * Tp, W), dy)
    dq, dk, dv = _ret_core_bwd_call(p, do.reshape(B, Tp, -1), cos, sin, _log_gamma())
    return jnp.concatenate([dq, dk, dv, dg.reshape(B, Tp, -1)], axis=-1), None, None


retention_mixer.defvjp(_retention_mixer_fwd, _retention_mixer_bwd)


def _heads_to_lanes(w):
    K = w.shape[0]
    w = w.reshape(K, MLA_HEADS, MLA_NOPE + MLA_ROPE)
    return jnp.pad(w, ((0, 0), (0, 0), (0, HEAD_LANES - MLA_NOPE - MLA_ROPE))).reshape(K, MLA_HEADS * HEAD_LANES)


def _out_rows_to_lanes(w):
    N = w.shape[1]
    att = w[LRU_WIDTH:].reshape(MLA_HEADS, MLA_V, N)
    att = jnp.pad(att, ((0, 0), (HEAD_LANES - MLA_V, 0), (0, 0))).reshape(MLA_HEADS * HEAD_LANES, N)
    return jnp.concatenate([w[:LRU_WIDTH], att], axis=0)


def _local_loss(diff, wfull, tgt):
    x = diff["x"]
    B, S, D = x.shape
    T = S + N_META
    Tp = _round_up(T, SEQ_BLOCK)
    M = B * Tp
    pos = jnp.arange(Tp, dtype=jnp.int32)

    def mm(a, name, act=False, out_dtype=F32, layout=lambda w: w, col_shards=1):
        return matmul(a, layout(wfull[name]), layout(diff[name]), act, name, out_dtype, col_shards)

    meta = jnp.broadcast_to(diff["meta_tokens"][None], (B, N_META, D))
    h = jnp.concatenate([meta, x, jnp.zeros((B, Tp - T, D), F32)], axis=1).reshape(M, D)

    p = mm(h, "ev_w_in")
    sp = jax.nn.softplus(-diff["ev_lru_lambda"]).reshape(1, LRU_WIDTH)
    y_rec, qn, kvn, p_kpe = even_front(
        p.reshape(B, Tp, -1), diff["ev_conv_w"].reshape(CONV_WIDTH, LRU_WIDTH), diff["ev_conv_b"].reshape(1, LRU_WIDTH),
        diff["ev_w_rg_a"].reshape(LRU_HEADS, LRU_HEAD_DIM, LRU_HEAD_DIM), diff["ev_b_rg_a"].reshape(1, LRU_WIDTH),
        diff["ev_w_rg_x"].reshape(LRU_HEADS, LRU_HEAD_DIM, LRU_HEAD_DIM), diff["ev_b_rg_x"].reshape(1, LRU_WIDTH),
        sp, diff["ev_q_norm_g"].reshape(-1), diff["ev_kv_norm_g"].reshape(-1))
    y_rec = y_rec.reshape(M, LRU_WIDTH)
    q = mm(qn, "ev_w_uq", out_dtype=MXU_DTYPE, layout=_heads_to_lanes).reshape(B, Tp, -1)
    kv = mm(kvn, "ev_w_ukv", out_dtype=MXU_DTYPE).reshape(B, Tp, -1)
    kpe = jnp.pad(p_kpe.reshape(B, Tp, MLA_ROPE), ((0, 0), (0, 0), (MLA_NOPE, HEAD_LANES - MLA_NOPE - MLA_ROPE)))
    y_att = mla_attention(q, kv, kpe, _mla_rope_tables(pos)).reshape(M, -1)
    mix = mm(jnp.concatenate([y_rec, y_att], axis=-1), "ev_w_out", layout=_out_rows_to_lanes)
    h = deepnorm(h, mix, diff["ln_mix_g"][0], diff["ln_mix_b"][0], "ln_mix0")
    f = mlp(h, wfull["mlp_w1_0"], wfull["mlp_w2_0"], diff["mlp_w1_0"], diff["mlp_w2_0"], "mlp0")
    h = deepnorm(h, f, diff["ln_mlp_g"][0], diff["ln_mlp_b"][0], "ln_mlp0")

    p = mm(h, "od_w_in", out_dtype=MXU_DTYPE, col_shards=N_CHIPS)
    cos, sin = _rope_tables(pos, RET_QK_DIM // 2)
    mix = mm(retention_mixer(p.reshape(B, Tp, -1), cos, sin), "od_w_out")
    h = deepnorm(h, mix, diff["ln_mix_g"][1], diff["ln_mix_b"][1], "ln_mix1")
    f = mlp(h, wfull["mlp_w1_1"], wfull["mlp_w2_1"], diff["mlp_w1_1"], diff["mlp_w2_1"], "mlp1")
    h = deepnorm(h, f, diff["ln_mlp_g"][1], diff["ln_mlp_b"][1], "ln_mlp1")

    y = h.reshape(B, Tp, D)[:, N_META:T].reshape(B * S, D)
    return loss_head(y, tgt.reshape(B * S, D))


_HBM = pl.BlockSpec(memory_space=pltpu.HBM)


def _place():
    return lax.axis_index("x"), lax.axis_index("y"), lax.axis_index("c")


def _other_chips(x, y):
    return [(1 - x, y), (x, 1 - y), (1 - x, 1 - y)]


def _chunks(rows, sublanes, most):
    for q in range(most, 0, -1):
        if rows % (q * sublanes) == 0:
            return q
    return 1


def _sublanes(dtype):
    return 8 * 4 // jnp.dtype(dtype).itemsize


def _allgather_chips(buf, name):
    R, C = buf.shape
    Rh = R // 2
    Q = _chunks(Rh, _sublanes(buf.dtype), 4)
    ch = Rh // Q

    def body(x_ref, out_ref, send_sems, recv_sems):
        x, y, c = _place()
        sibling = (x, y, 1 - c)
        chips = _other_chips(x, y)

        def piece(cx, cy, hc, q):
            return out_ref.at[2 * cx + cy, pl.ds(hc * Rh + q * ch, ch), :]

        def copy(k, src, dst, to):
            return pltpu.make_async_remote_copy(src_ref=src, dst_ref=dst, send_sem=send_sems.at[k], recv_sem=recv_sems.at[k],
                                                device_id=to, device_id_type=MESH)

        first = [copy(j * Q + q, x_ref.at[pl.ds(c * Rh + q * ch, ch), :], piece(x, y, c, q), (*chip, c))
                 for q in range(Q) for j, chip in enumerate(chips)]
        for cp in first:
            cp.start()
        passed = []
        for q in range(Q):
            for j, chip in enumerate(chips):
                landed = piece(*chip, c, q)
                copy(j * Q + q, landed, landed, sibling).wait_recv()
                fwd = copy(3 * Q + j * Q + q, landed, landed, sibling)
                fwd.start()
                passed.append(fwd)
        for q in range(Q):
            for j, chip in enumerate(chips):
                theirs = piece(*chip, 1 - c, q)
                copy(3 * Q + j * Q + q, theirs, theirs, sibling).wait_recv()
        for cp in first + passed:
            cp.wait_send()

    return pl.pallas_call(
        body, name=name, in_specs=[_HBM], out_specs=_HBM,
        out_shape=jax.ShapeDtypeStruct((N_CHIPS, R, C), buf.dtype),
        scratch_shapes=[pltpu.SemaphoreType.DMA((6 * Q,)), pltpu.SemaphoreType.DMA((6 * Q,))],
    )(buf)


def _with_own(gathered, own):
    my = 2 * lax.axis_index("x") + lax.axis_index("y")
    return lax.dynamic_update_slice(gathered, own[None], (my, 0, 0))


def _sibling_exchange(ps, name):
    n = len(ps)

    def body(*refs):
        p_refs, out_refs, (send_sems, recv_sems) = refs[:n], refs[n:2 * n], refs[2 * n:]
        x, y, c = _place()
        copies = [pltpu.make_async_remote_copy(src_ref=p_ref.at[j, 1 - c], dst_ref=out_ref.at[j], send_sem=send_sems.at[N_CHIPS * i + j],
                                               recv_sem=recv_sems.at[N_CHIPS * i + j], device_id=(x, y, 1 - c), device_id_type=MESH)
                  for i, (p_ref, out_ref) in enumerate(zip(p_refs, out_refs)) for j in range(N_CHIPS)]
        for cp in copies:
            cp.start()
        for cp in copies:
            cp.wait()

    return pl.pallas_call(
        body, name=name, in_specs=[_HBM] * n, out_specs=[_HBM] * n,
        out_shape=[jax.ShapeDtypeStruct((N_CHIPS,) + p.shape[2:], p.dtype) for p in ps],
        scratch_shapes=[pltpu.SemaphoreType.DMA((N_CHIPS * n,)), pltpu.SemaphoreType.DMA((N_CHIPS * n,))],
    )(*ps)


def _chip_scatter(ss, name):
    n = len(ss)

    def body(*refs):
        s_refs, t_refs, (send_sems, recv_sems) = refs[:n], refs[n:2 * n], refs[2 * n:]
        x, y, c = _place()
        copies = [pltpu.make_async_remote_copy(src_ref=s_ref.at[j + 1], dst_ref=t_ref.at[j], send_sem=send_sems.at[3 * i + j],
                                               recv_sem=recv_sems.at[3 * i + j], device_id=(cx, cy, c), device_id_type=MESH)
                  for i, (s_ref, t_ref) in enumerate(zip(s_refs, t_refs)) for j, (cx, cy) in enumerate(_other_chips(x, y))]
        for cp in copies:
            cp.start()
        for cp in copies:
            cp.wait()

    return pl.pallas_call(
        body, name=name, in_specs=[_HBM] * n, out_specs=[_HBM] * n,
        out_shape=[jax.ShapeDtypeStruct((3,) + s.shape[1:], s.dtype) for s in ss],
        scratch_shapes=[pltpu.SemaphoreType.DMA((3 * n,)), pltpu.SemaphoreType.DMA((3 * n,))],
    )(*ss)


def _sibling_gather(fs, name):
    n = len(fs)

    def body(*refs):
        out_refs, (send_sems, recv_sems) = refs[n:2 * n], refs[2 * n:]
        x, y, c = _place()
        copies = [pltpu.make_async_remote_copy(src_ref=out_ref.at[c], dst_ref=out_ref.at[c], send_sem=send_sems.at[i], recv_sem=recv_sems.at[i],
                                               device_id=(x, y, 1 - c), device_id_type=MESH) for i, out_ref in enumerate(out_refs)]
        for cp in copies:
            cp.start()
        for cp in copies:
            cp.wait()

    return pl.pallas_call(
        body, name=name, in_specs=[_HBM] * n, out_specs=[_HBM] * n,
        out_shape=[jax.ShapeDtypeStruct(f.shape, f.dtype) for f in fs], input_output_aliases={i: i for i in range(n)},
        scratch_shapes=[pltpu.SemaphoreType.DMA((n,)), pltpu.SemaphoreType.DMA((n,))],
    )(*fs)


def _axis_scalar(name):
    return lax.axis_index(name).astype(jnp.int32).reshape(1)


def _add_own_half(p, got, out_dtype, name):
    n, _, R, C = p.shape
    tr = _pick(R, 512, 16)

    def body(x_ref, y_ref, c_ref, p_ref, g_ref, o_ref):
        o_ref[...] = (p_ref[...] + g_ref[...]).astype(out_dtype)

    def chip(r, x_ref, y_ref):
        return 2 * (x_ref[0] ^ (r & 1)) + (y_ref[0] ^ (r >> 1))

    grid_spec = pltpu.PrefetchScalarGridSpec(
        num_scalar_prefetch=3, grid=(n, R // tr),
        in_specs=[pl.BlockSpec((None, None, tr, C), lambda r, i, x_ref, y_ref, c_ref: (chip(r, x_ref, y_ref), c_ref[0], i, 0)),
                  pl.BlockSpec((None, tr, C), lambda r, i, x_ref, y_ref, c_ref: (chip(r, x_ref, y_ref), i, 0))],
        out_specs=pl.BlockSpec((None, tr, C), lambda r, i, x_ref, y_ref, c_ref: (r, i, 0)))
    return pl.pallas_call(body, name=name, grid_spec=grid_spec, out_shape=jax.ShapeDtypeStruct((n, R, C), out_dtype),
                          compiler_params=pltpu.CompilerParams(dimension_semantics=("parallel", "parallel")))(
        _axis_scalar("x"), _axis_scalar("y"), _axis_scalar("c"), p, got)


def _sum_partials(s, t, name):
    _, R, C = s.shape
    tr = _pick(R, 512, 16)

    def body(c_ref, s_ref, t_ref, o_ref):
        acc = s_ref[...].astype(F32)
        for j in range(3):
            acc = acc + t_ref[j].astype(F32)
        o_ref[...] = acc

    grid_spec = pltpu.PrefetchScalarGridSpec(
        num_scalar_prefetch=1, grid=(R // tr,),
        in_specs=[pl.BlockSpec((None, tr, C), lambda i, c_ref: (0, i, 0)), pl.BlockSpec((3, tr, C), lambda i, c_ref: (0, i, 0))],
        out_specs=pl.BlockSpec((None, tr, C), lambda i, c_ref: (c_ref[0], i, 0)))
    return pl.pallas_call(body, name=name, grid_spec=grid_spec, out_shape=jax.ShapeDtypeStruct((2, R, C), F32),
                          compiler_params=pltpu.CompilerParams(dimension_semantics=("parallel",)))(_axis_scalar("c"), s, t)


def _reduce_to_chips(ps, wire_dtypes):
    got = _sibling_exchange(ps, "grad_sibling_exchange")
    ss = [_add_own_half(p, g, dt, "grad_sibling_add%d" % i) for i, (p, g, dt) in enumerate(zip(ps, got, wire_dtypes))]
    ts = _chip_scatter(ss, "grad_chip_scatter")
    fs = [_sum_partials(s, t, "grad_chip_sum%d" % i) for i, (s, t) in enumerate(zip(ss, ts))]
    return _sibling_gather(fs, "grad_sibling_gather")


def _adamw(w, g, m, v, name):
    R, C = w.shape
    tr = _pick(R, 256, 8)

    def body(w_ref, g_ref, m_ref, v_ref, d_ref, nm_ref, nv_ref):
        g_ = g_ref[...]
        m_ = ADAM_B1 * m_ref[...] + (1.0 - ADAM_B1) * g_
        v_ = ADAM_B2 * v_ref[...] + (1.0 - ADAM_B2) * (g_ * g_)
        m_hat = m_ / (1.0 - ADAM_B1 ** ADAM_STEP)
        v_hat = v_ / (1.0 - ADAM_B2 ** ADAM_STEP)
        d_ref[...] = -ADAM_LR * (m_hat / (jnp.sqrt(v_hat) + ADAM_EPS) + ADAM_WD * w_ref[...])
        nm_ref[...] = m_
        nv_ref[...] = v_

    row = pl.BlockSpec((tr, C), lambda i: (i, 0))
    shp = jax.ShapeDtypeStruct((R, C), F32)
    return pl.pallas_call(body, name=name, grid=(R // tr,), in_specs=[row] * 4, out_specs=[row] * 3, out_shape=[shp] * 3,
                          compiler_params=pltpu.CompilerParams(dimension_semantics=("parallel",)))(w, g, m, v)


BIG_SPECS = (("ev_w_in", 1024, 1440, 1), ("ev_w_uq", 256, 768, 1), ("ev_w_ukv", 128, 1024, 1), ("ev_w_out", 1024, 1024, 0),
             ("od_w_in", 1024, 6144, 1), ("od_w_out", 2048, 1024, 0), ("mlp_w1_0", 1024, 4096, 1), ("mlp_w1_1", 1024, 4096, 1),
             ("mlp_w2_0", 4096, 1024, 0), ("mlp_w2_1", 4096, 1024, 0))
BIG_PARAMS = (("ev_w_in", ("ev_w_in",)), ("ev_w_uq", ("ev_w_uq",)), ("ev_w_ukv", ("ev_w_ukv",)), ("ev_w_out", ("ev_w_out",)),
              ("od_w_in", ("od_w_in",)), ("od_w_out", ("od_w_out",)), ("mlp_w1", ("mlp_w1_0", "mlp_w1_1")),
              ("mlp_w2", ("mlp_w2_0", "mlp_w2_1")))
REPLICATED = ("ev_conv_b", "ev_w_rg_a", "ev_b_rg_a", "ev_w_rg_x", "ev_b_rg_x", "ev_lru_lambda", "ev_q_norm_g", "ev_kv_norm_g",
              "ln_mix_g", "ln_mix_b", "ln_mlp_g", "ln_mlp_b")
SMALL_SHARDED = ("meta_tokens", "ev_conv_w")
COL_SHARD_GRADS = ("od_w_in", "mlp_w1_0", "mlp_w1_1")
WEIGHT_NAMES = ("meta_tokens", "ev_w_in", "ev_conv_w", "ev_conv_b", "ev_w_rg_a", "ev_b_rg_a", "ev_w_rg_x", "ev_b_rg_x",
                "ev_lru_lambda", "ev_q_norm_g", "ev_w_uq", "ev_kv_norm_g", "ev_w_ukv", "ev_w_out", "od_w_in", "od_w_out",
                "ln_mix_g", "ln_mix_b", "mlp_w1", "mlp_w2", "ln_mlp_g", "ln_mlp_b")


def _to_rows(flat, row_align):
    n = flat.shape[-1]
    rows = _round_up(-(-n // PACK_COLS), row_align)
    pad = rows * PACK_COLS - n
    if pad:
        flat = jnp.pad(flat, [(0, 0)] * (flat.ndim - 1) + [(0, pad)])
    return flat.reshape(flat.shape[:-1] + (rows, PACK_COLS))


def _shard_shape(K, N, axis):
    return (K // N_CHIPS, N) if axis == 0 else (K, N // N_CHIPS)


def _gather_shards(stacked, K, N, axis):
    if axis == 0:
        return stacked.reshape(K, N)
    return stacked.transpose(1, 0, 2).reshape(K, N)


def _split_shards(full, K, N, axis):
    if axis == 0:
        return full.reshape(N_CHIPS, -1)
    return full.reshape(K, N_CHIPS, N // N_CHIPS).transpose(1, 0, 2).reshape(N_CHIPS, -1)


def kernel(x, meta_tokens, ev_w_in, ev_conv_w, ev_conv_b, ev_w_rg_a, ev_b_rg_a, ev_w_rg_x, ev_b_rg_x, ev_lru_lambda, ev_q_norm_g, ev_w_uq, ev_kv_norm_g, ev_w_ukv, ev_w_out, od_w_in, od_w_out, ln_mix_g, ln_mix_b, mlp_w1, mlp_w2, ln_mlp_g, ln_mlp_b, loss_target, m_meta_tokens, m_ev_w_in, m_ev_conv_w, m_ev_conv_b, m_ev_w_rg_a, m_ev_b_rg_a, m_ev_w_rg_x, m_ev_b_rg_x, m_ev_lru_lambda, m_ev_q_norm_g, m_ev_w_uq, m_ev_kv_norm_g, m_ev_w_ukv, m_ev_w_out, m_od_w_in, m_od_w_out, m_ln_mix_g, m_ln_mix_b, m_mlp_w1, m_mlp_w2, m_ln_mlp_g, m_ln_mlp_b, v_meta_tokens, v_ev_w_in, v_ev_conv_w, v_ev_conv_b, v_ev_w_rg_a, v_ev_b_rg_a, v_ev_w_rg_x, v_ev_b_rg_x, v_ev_lru_lambda, v_ev_q_norm_g, v_ev_w_uq, v_ev_kv_norm_g, v_ev_w_ukv, v_ev_w_out, v_od_w_in, v_od_w_out, v_ln_mix_g, v_ln_mix_b, v_mlp_w1, v_mlp_w2, v_ln_mlp_g, v_ln_mlp_b):
    given = dict(locals())
    local_big = {"ev_w_in": ev_w_in[0], "ev_w_uq": ev_w_uq[0], "ev_w_ukv": ev_w_ukv[0], "ev_w_out": ev_w_out[0],
                 "od_w_in": od_w_in[0], "od_w_out": od_w_out[0], "mlp_w1_0": mlp_w1[0], "mlp_w1_1": mlp_w1[1],
                 "mlp_w2_0": mlp_w2[0], "mlp_w2_1": mlp_w2[1]}

    sizes = [math.prod(_shard_shape(K, N, ax)) for _, K, N, ax in BIG_SPECS]
    packed = _to_rows(jnp.concatenate([local_big[n].astype(MXU_DTYPE).reshape(-1) for n, _, _, _ in BIG_SPECS]), 256)
    gathered = _with_own(_allgather_chips(packed, "weight_allgather"), packed).reshape(N_CHIPS, -1)
    wfull, off = {}, 0
    for (n, K, N, ax), sz in zip(BIG_SPECS, sizes):
        wfull[n] = _gather_shards(gathered[:, off:off + sz].reshape((N_CHIPS,) + _shard_shape(K, N, ax)), K, N, ax)
        off += sz
    small = _to_rows(jnp.concatenate([meta_tokens.reshape(-1), ev_conv_w.reshape(-1)]), 16)
    small = _with_own(_allgather_chips(small, "small_allgather"), small).reshape(N_CHIPS, -1)
    n_meta, n_conv = meta_tokens.size, ev_conv_w.size
    meta_full = _gather_shards(small[:, :n_meta].reshape(N_CHIPS, N_META, D_MODEL // N_CHIPS), N_META, D_MODEL, 1)
    conv_full = _gather_shards(small[:, n_meta:n_meta + n_conv].reshape(N_CHIPS, CONV_WIDTH, LRU_WIDTH // N_CHIPS),
                               CONV_WIDTH, LRU_WIDTH, 1)

    diff = {n: jnp.zeros((N_CHIPS, K, N // N_CHIPS) if n in COL_SHARD_GRADS else (K, N), F32) for n, K, N, _ in BIG_SPECS}
    diff.update({n: given[n] for n in REPLICATED})
    diff.update(x=x, meta_tokens=meta_full, ev_conv_w=conv_full)
    loss, g = jax.value_and_grad(_local_loss)(diff, wfull, loss_target)
    loss = lax.psum(loss, ("x", "y", "c"))

    def blocks_of(n, K, N, ax):
        if n in COL_SHARD_GRADS:
            blocks = g[n]
        elif ax == 0:
            blocks = g[n].reshape(N_CHIPS, K // N_CHIPS, N)
        else:
            blocks = g[n].reshape(K, N_CHIPS, N // N_CHIPS).transpose(1, 0, 2)
        return blocks.reshape(N_CHIPS, 2, blocks.shape[1] // 2, blocks.shape[2])

    repl = jnp.concatenate([g[n].reshape(-1) for n in REPLICATED]).reshape(N_CHIPS, -1)
    small = [_split_shards(g["meta_tokens"], N_META, D_MODEL, 1), _split_shards(g["ev_conv_w"], CONV_WIDTH, LRU_WIDTH, 1), repl]
    small = [pc.reshape(N_CHIPS, 2, -1) for pc in small]
    n_small = sum(pc.shape[2] for pc in small)
    small.append(jnp.zeros((N_CHIPS, 2, _round_up(n_small, 32 * PACK_COLS) - n_small), F32))
    p_small = jnp.concatenate(small, axis=2).reshape(N_CHIPS, 2, -1, PACK_COLS)
    reduced = _reduce_to_chips([blocks_of(*spec) for spec in BIG_SPECS] + [p_small], [MXU_DTYPE] * len(BIG_SPECS) + [F32])
    red_big = dict(zip([spec[0] for spec in BIG_SPECS], reduced))
    red_small = reduced[-1].reshape(2, -1)

    grads = {}
    for name, parts in BIG_PARAMS:
        grads[name] = jnp.stack([red_big[part].reshape(given[name].shape[1:]) for part in parts])

    def take(off, sz):
        return jnp.concatenate([red_small[0, off // 2:(off + sz) // 2], red_small[1, off // 2:(off + sz) // 2]])

    off = 0
    for name in SMALL_SHARDED:
        sz = given[name].size
        grads[name] = take(off, sz).reshape(given[name].shape)
        off += sz
    n_repl = repl.shape[1]
    own_repl = _to_rows(take(off, n_repl), 16)
    repl_all = _with_own(_allgather_chips(own_repl, "replicated_allgather"), own_repl).reshape(N_CHIPS, -1)[:, :n_repl].reshape(-1)
    off = 0
    for name in REPLICATED:
        sz = given[name].size
        grads[name] = repl_all[off:off + sz].reshape(given[name].shape)
        off += sz

    delta, new_m, new_v = {}, {}, {}
    for name, _ in BIG_PARAMS:
        shp = given[name].shape
        two_d = (-1, shp[-1])
        d, nm, nv = _adamw(given[name].reshape(two_d), grads[name].reshape(two_d), given["m_" + name].reshape(two_d),
                           given["v_" + name].reshape(two_d), "adamw_" + name)
        delta[name], new_m[name], new_v[name] = d.reshape(shp), nm.reshape(shp), nv.reshape(shp)
    smalls = SMALL_SHARDED + REPLICATED

    def pack_small(get):
        return _to_rows(jnp.concatenate([get(n).reshape(-1) for n in smalls]), 8)

    outs = _adamw(pack_small(lambda n: given[n]), pack_small(lambda n: grads[n]), pack_small(lambda n: given["m_" + n]),
                  pack_small(lambda n: given["v_" + n]), "adamw_small")
    for res, flat in zip((delta, new_m, new_v), outs):
        flat, off = flat.reshape(-1), 0
        for n in smalls:
            sz = given[n].size
            res[n] = flat[off:off + sz].reshape(given[n].shape)
            off += sz

    return (loss, g["x"], *[grads[n] for n in WEIGHT_NAMES], *[delta[n] for n in WEIGHT_NAMES],
            *[new_m[n] for n in WEIGHT_NAMES], *[new_v[n] for n in WEIGHT_NAMES])
```

```python
import functools
import math

import jax
import jax.numpy as jnp
from jax import lax
from jax.experimental import pallas as pl
from jax.experimental.pallas import tpu as pltpu

F32 = jnp.float32
MXU_DTYPE = jnp.bfloat16

D_MODEL = 1024
N_META = 16
LRU_WIDTH = 512
LRU_HEADS = 4
LRU_HEAD_DIM = 128
CONV_WIDTH = 4
LRU_C = 8.0
MLA_HEADS = 8
MLA_NOPE = 64
MLA_ROPE = 32
MLA_V = 64
MLA_Q_RANK = 256
MLA_KV_RANK = 128
RET_HEADS = 4
RET_QK_DIM = 256
RET_V_DIM = 512
D_FF = 4096
ROPE_BASE = 10000.0
DN_ALPHA = 4.0 ** 0.25
EPS = 1e-5
NEG_INF = -1e30
SEQ_BLOCK = 128

ADAM_LR = 0.001
ADAM_B1 = 0.9
ADAM_B2 = 0.999
ADAM_EPS = 1e-08
ADAM_WD = 0.01
ADAM_STEP = 10

PACK_COLS = 1024
N_CHIPS = 4

MESH = pl.DeviceIdType.MESH


def _pick(n, target, align):
    best = None
    for t in range(align, min(n, target) + 1, align):
        if n % t == 0:
            best = t
    return n if best is None else best


def _round_up(n, m):
    return (n + m - 1) // m * m


def _relu2(a):
    r = jnp.maximum(a, 0.0)
    return r * r


def _mm_nn(a, w, act, name, out_dtype=F32):
    M, K = a.shape
    _, N = w.shape
    tm = _pick(M, 1088 if K * a.dtype.itemsize <= 4096 else 544, 8)
    tn = _pick(N, 1024, 128)

    def body(a_ref, w_ref, o_ref):
        av = a_ref[...]
        if act:
            av = _relu2(av.astype(F32))
        o_ref[...] = jnp.dot(av.astype(MXU_DTYPE), w_ref[...].astype(MXU_DTYPE), preferred_element_type=F32).astype(out_dtype)

    return pl.pallas_call(
        body, name=name,
        grid=(M // tm, N // tn),
        in_specs=[pl.BlockSpec((tm, K), lambda i, j: (i, 0)), pl.BlockSpec((K, tn), lambda i, j: (0, j))],
        out_specs=pl.BlockSpec((tm, tn), lambda i, j: (i, j)),
        out_shape=jax.ShapeDtypeStruct((M, N), out_dtype),
        compiler_params=pltpu.CompilerParams(dimension_semantics=("parallel", "arbitrary")),
    )(a, w)


def _mm_nt(g, w, a_src, name, out_dtype=F32):
    M, N = g.shape
    K, _ = w.shape
    tk = N if N * g.dtype.itemsize <= 8192 else _pick(N, 2048, 128)
    nk = N // tk
    tm = _pick(M, 1088 if tk * g.dtype.itemsize <= 4096 else 544, 8)
    tn = _pick(K, 1024, 128)
    has_src = a_src is not None
    assert nk == 1 or out_dtype == F32

    def body(*refs):
        if has_src:
            g_ref, w_ref, s_ref, o_ref = refs
        else:
            g_ref, w_ref, o_ref = refs
        r = lax.dot_general(g_ref[...].astype(MXU_DTYPE), w_ref[...].astype(MXU_DTYPE),
                            (((1,), (1,)), ((), ())), preferred_element_type=F32)
        if has_src:
            r = r * (2.0 * jnp.maximum(s_ref[...].astype(F32), 0.0))
        if nk == 1:
            o_ref[...] = r.astype(out_dtype)
        else:
            k = pl.program_id(2)

            @pl.when(k == 0)
            def _():
                o_ref[...] = r

            @pl.when(k > 0)
            def _():
                o_ref[...] += r

    in_specs = [pl.BlockSpec((tm, tk), lambda i, j, k: (i, k)), pl.BlockSpec((tn, tk), lambda i, j, k: (j, k))]
    args = [g, w]
    if has_src:
        assert nk == 1
        in_specs.append(pl.BlockSpec((tm, tn), lambda i, j, k: (i, j)))
        args.append(a_src)
    return pl.pallas_call(
        body, name=name,
        grid=(M // tm, K // tn, nk),
        in_specs=in_specs,
        out_specs=pl.BlockSpec((tm, tn), lambda i, j, k: (i, j)),
        out_shape=jax.ShapeDtypeStruct((M, K), out_dtype),
        compiler_params=pltpu.CompilerParams(dimension_semantics=("parallel", "parallel", "arbitrary")),
    )(*args)


def _mm_tn(a, g, act, name, col_shards=1):
    M, K = a.shape
    _, N = g.shape
    n = N // col_shards
    tm, tn, tk = _pick(K, 1024, 128), _pick(n, 1024, 128), _pick(M, 1088, 8)
    nk = M // tk
    per = n // tn

    def body(a_ref, g_ref, o_ref):
        k = pl.program_id(2)
        av = a_ref[...]
        if act:
            av = _relu2(av.astype(F32))
        r = lax.dot_general(av.astype(MXU_DTYPE), g_ref[...].astype(MXU_DTYPE),
                            (((0,), (0,)), ((), ())), preferred_element_type=F32)

        @pl.when(k == 0)
        def _():
            o_ref[...] = r

        @pl.when(k > 0)
        def _():
            o_ref[...] += r

    if col_shards == 1:
        out_spec, out_shape = pl.BlockSpec((tm, tn), lambda i, j, k: (i, j)), (K, N)
    else:
        out_spec, out_shape = pl.BlockSpec((None, tm, tn), lambda i, j, k: (j // per, i, j % per)), (col_shards, K, n)
    return pl.pallas_call(
        body, name=name,
        grid=(K // tm, N // tn, nk),
        in_specs=[pl.BlockSpec((tk, tm), lambda i, j, k: (k, i)), pl.BlockSpec((tk, tn), lambda i, j, k: (k, j))],
        out_specs=out_spec,
        out_shape=jax.ShapeDtypeStruct(out_shape, F32),
        compiler_params=pltpu.CompilerParams(dimension_semantics=("parallel", "parallel", "arbitrary")),
    )(a, g)


@functools.partial(jax.custom_vjp, nondiff_argnums=(3, 4, 5, 6))
def matmul(a, w, w_grad_slot, act, name, out_dtype, col_shards):
    return _mm_nn(a, w, act, name + "_fwd", out_dtype)


def _matmul_fwd(a, w, w_grad_slot, act, name, out_dtype, col_shards):
    return _mm_nn(a, w, act, name + "_fwd", out_dtype), (a, w)


def _matmul_bwd(act, name, out_dtype, col_shards, res, g):
    a, w = res
    da = _mm_nt(g, w, a if act else None, name + "_dx")
    dw = _mm_tn(a, g, act, name + "_dw", col_shards)
    return da, None, dw


matmul.defvjp(_matmul_fwd, _matmul_bwd)


@functools.partial(jax.custom_vjp, nondiff_argnums=(5,))
def mlp(h, w1, w2, w1_grad_slot, w2_grad_slot, name):
    u = _mm_nn(h, w1, False, name + "_w1_fwd", out_dtype=MXU_DTYPE)
    return _mm_nn(u, w2, True, name + "_w2_fwd")


def _mlp_fwd(h, w1, w2, w1_grad_slot, w2_grad_slot, name):
    u = _mm_nn(h, w1, False, name + "_w1_fwd", out_dtype=MXU_DTYPE)
    return _mm_nn(u, w2, True, name + "_w2_fwd"), (h, u, w1, w2)


def _mlp_bwd(name, res, df):
    h, u, w1, w2 = res
    du = _mm_nt(df, w2, u, name + "_w2_dx", out_dtype=MXU_DTYPE)
    dw2 = _mm_tn(u, df, True, name + "_w2_dw")
    dh = _mm_nt(du, w1, None, name + "_w1_dx")
    dw1 = _mm_tn(h, du, False, name + "_w1_dw", N_CHIPS)
    return dh, None, None, dw1, dw2


mlp.defvjp(_mlp_fwd, _mlp_bwd)


def _ln_stats(z):
    mu = jnp.mean(z, axis=-1, keepdims=True)
    zc = z - mu
    var = jnp.mean(zc * zc, axis=-1, keepdims=True)
    return zc, lax.rsqrt(var + EPS)


def _ln_fwd_call(resid, branch, g, b, name):
    M, D = resid.shape
    tm = _pick(M, 544, 8)

    def body(r_ref, br_ref, g_ref, b_ref, o_ref):
        zc, rstd = _ln_stats(DN_ALPHA * r_ref[...] + br_ref[...])
        o_ref[...] = zc * rstd * g_ref[...] + b_ref[...]

    row = pl.BlockSpec((tm, D), lambda i: (i, 0))
    vec = pl.BlockSpec((1, D), lambda i: (0, 0))
    return pl.pallas_call(
        body, name=name, grid=(M // tm,), in_specs=[row, row, vec, vec], out_specs=row,
        out_shape=jax.ShapeDtypeStruct((M, D), F32),
        compiler_params=pltpu.CompilerParams(dimension_semantics=("parallel",)),
    )(resid, branch, g.reshape(1, D), b.reshape(1, D))


def _ln_bwd_call(resid, branch, g, dy, name):
    M, D = resid.shape
    tm = _pick(M, 544, 8)

    def body(r_ref, br_ref, g_ref, dy_ref, dz_ref, dg_ref, db_ref):
        @pl.when(pl.program_id(0) == 0)
        def _():
            dg_ref[...] = jnp.zeros_like(dg_ref)
            db_ref[...] = jnp.zeros_like(db_ref)

        zc, rstd = _ln_stats(DN_ALPHA * r_ref[...] + br_ref[...])
        xhat = zc * rstd
        dy = dy_ref[...]
        dxh = dy * g_ref[...]
        m1 = jnp.mean(dxh, axis=-1, keepdims=True)
        m2 = jnp.mean(dxh * xhat, axis=-1, keepdims=True)
        dz_ref[...] = rstd * (dxh - m1 - xhat * m2)
        dg_ref[...] += jnp.sum(dy * xhat, axis=0, keepdims=True)
        db_ref[...] += jnp.sum(dy, axis=0, keepdims=True)

    row = pl.BlockSpec((tm, D), lambda i: (i, 0))
    vec = pl.BlockSpec((1, D), lambda i: (0, 0))
    return pl.pallas_call(
        body, name=name, grid=(M // tm,), in_specs=[row, row, vec, row], out_specs=[row, vec, vec],
        out_shape=[jax.ShapeDtypeStruct((M, D), F32), jax.ShapeDtypeStruct((1, D), F32), jax.ShapeDtypeStruct((1, D), F32)],
        compiler_params=pltpu.CompilerParams(dimension_semantics=("arbitrary",)),
    )(resid, branch, g.reshape(1, D), dy)


@functools.partial(jax.custom_vjp, nondiff_argnums=(4,))
def deepnorm(resid, branch, g, b, name):
    return _ln_fwd_call(resid, branch, g, b, name + "_fwd")


def _deepnorm_fwd(resid, branch, g, b, name):
    return _ln_fwd_call(resid, branch, g, b, name + "_fwd"), (resid, branch, g)


def _deepnorm_bwd(name, res, dy):
    resid, branch, g = res
    dz, dg, db = _ln_bwd_call(resid, branch, g, dy, name + "_bwd")
    return DN_ALPHA * dz, dz, dg.reshape(g.shape), db.reshape(g.shape)


deepnorm.defvjp(_deepnorm_fwd, _deepnorm_bwd)


def _rms_fwd_call(x, g, name, col_block=0):
    R = x.shape[0]
    W = g.shape[-1]
    tr = _pick(R, 1088, 8)

    def body(x_ref, g_ref, o_ref):
        xv = x_ref[...]
        rstd = lax.rsqrt(jnp.mean(xv * xv, axis=-1, keepdims=True) + EPS)
        o_ref[...] = xv * rstd * g_ref[...]

    vec = pl.BlockSpec((1, W), lambda i: (0, 0))
    return pl.pallas_call(
        body, name=name, grid=(R // tr,), in_specs=[pl.BlockSpec((tr, W), lambda i: (i, col_block)), vec],
        out_specs=pl.BlockSpec((tr, W), lambda i: (i, 0)), out_shape=jax.ShapeDtypeStruct((R, W), F32),
        compiler_params=pltpu.CompilerParams(dimension_semantics=("parallel",)),
    )(x, g.reshape(1, W))


def _rms_bwd_call(x, g, dy, name, col_block=0):
    R = x.shape[0]
    W = g.shape[-1]
    tr = _pick(R, 1088, 8)

    def body(x_ref, g_ref, dy_ref, dx_ref, dg_ref):
        @pl.when(pl.program_id(0) == 0)
        def _():
            dg_ref[...] = jnp.zeros_like(dg_ref)

        xv = x_ref[...]
        rstd = lax.rsqrt(jnp.mean(xv * xv, axis=-1, keepdims=True) + EPS)
        xhat = xv * rstd
        dy = dy_ref[...]
        dxh = dy * g_ref[...]
        dx_ref[...] = rstd * (dxh - xhat * jnp.mean(dxh * xhat, axis=-1, keepdims=True))
        dg_ref[...] += jnp.sum(dy * xhat, axis=0, keepdims=True)

    row = pl.BlockSpec((tr, W), lambda i: (i, 0))
    vec = pl.BlockSpec((1, W), lambda i: (0, 0))
    return pl.pallas_call(
        body, name=name, grid=(R // tr,), in_specs=[pl.BlockSpec((tr, W), lambda i: (i, col_block)), vec, row], out_specs=[row, vec],
        out_shape=[jax.ShapeDtypeStruct((R, W), F32), jax.ShapeDtypeStruct((1, W), F32)],
        compiler_params=pltpu.CompilerParams(dimension_semantics=("arbitrary",)),
    )(x, g.reshape(1, W), dy)


def _loss_call(y, tgt, name):
    R, D = y.shape
    tr = _pick(R, 512, 8)

    def body(y_ref, t_ref, dy_ref, acc_ref):
        @pl.when(pl.program_id(0) == 0)
        def _():
            acc_ref[...] = jnp.zeros_like(acc_ref)

        e = y_ref[...] - t_ref[...]
        dy_ref[...] = e * (1.0 / D)
        acc_ref[...] += jnp.sum(jnp.sum(e * e, axis=-1, keepdims=True), axis=0, keepdims=True) * (0.5 / D)

    row = pl.BlockSpec((tr, D), lambda i: (i, 0))
    one = pl.BlockSpec((1, 1), lambda i: (0, 0))
    return pl.pallas_call(
        body, name=name, grid=(R // tr,), in_specs=[row, row], out_specs=[row, one],
        out_shape=[jax.ShapeDtypeStruct((R, D), F32), jax.ShapeDtypeStruct((1, 1), F32)],
        compiler_params=pltpu.CompilerParams(dimension_semantics=("arbitrary",)),
    )(y, tgt)


@jax.custom_vjp
def loss_head(y, tgt):
    return _loss_call(y, tgt, "loss_head")[1][0, 0]


def _loss_head_fwd(y, tgt):
    dy, acc = _loss_call(y, tgt, "loss_head")
    return acc[0, 0], dy


def _loss_head_bwd(dy, ct):
    return ct * dy, None


loss_head.defvjp(_loss_head_fwd, _loss_head_bwd)


_GELU_C = math.sqrt(2.0 / math.pi)


def _gelu_parts(x):
    x2 = x * x
    t = jnp.tanh(_GELU_C * (x + 0.044715 * x * x2))
    gelu = 0.5 * x * (1.0 + t)
    dgelu = 0.5 * (1.0 + t) + 0.5 * x * (1.0 - t * t) * (_GELU_C * (1.0 + 3.0 * 0.044715 * x2))
    return gelu, dgelu


def _sigmoid(x):
    return 1.0 / (1.0 + jnp.exp(-x))


def _scan8(a, b, carry, reverse):
    row = lax.broadcasted_iota(jnp.int32, a.shape, 0)
    for s in (1, 2, 4):
        shift = 8 - s if reverse else s
        keep = (row < 8 - s) if reverse else (row >= s)
        b = jnp.where(keep, a * pltpu.roll(b, shift, 0) + b, b)
        a = jnp.where(keep, a * pltpu.roll(a, shift, 0), a)
    return a * carry + b


def _lru_pre(prec_ref, prev_ref, first, cw_ref, cb_ref, wa_ref, ba_ref, wx_ref, bx_ref, sp_ref):
    tc = prec_ref.shape[0]
    prev = jnp.where(first, 0.0, prev_ref[...])
    ext = jnp.concatenate([prev, prec_ref[...]], axis=0)
    cw = cw_ref[...]
    taps = [ext[8:] if k == CONV_WIDTH - 1 else pltpu.roll(ext, CONV_WIDTH - 1 - k, 0)[8:] for k in range(CONV_WIDTH)]
    xc = cb_ref[...] + sum(cw[k:k + 1, :] * taps[k] for k in range(CONV_WIDTH))
    ga, gx = [], []
    for h in range(LRU_HEADS):
        xh = xc[:, h * LRU_HEAD_DIM:(h + 1) * LRU_HEAD_DIM].astype(MXU_DTYPE)
        ga.append(jnp.dot(xh, wa_ref[h].astype(MXU_DTYPE), preferred_element_type=F32))
        gx.append(jnp.dot(xh, wx_ref[h].astype(MXU_DTYPE), preferred_element_type=F32))
    r = _sigmoid(jnp.concatenate(ga, axis=1) + ba_ref[...])
    i = _sigmoid(jnp.concatenate(gx, axis=1) + bx_ref[...])
    log_a = -LRU_C * r * sp_ref[...]
    a = jnp.exp(log_a)
    a2 = a * a
    mult = jnp.sqrt(-jnp.tanh(log_a) * (a2 + 1.0))
    return taps, xc, r, i, a, a2, mult


def _lru_fwd_call(p, cw, cb, wa, ba, wx, bx, sp):
    B, Tp, _ = p.shape
    W = LRU_WIDTH
    tc = SEQ_BLOCK
    nc = Tp // tc

    def body(pg_ref, prec_ref, prev_ref, cw_ref, cb_ref, wa_ref, ba_ref, wx_ref, bx_ref, sp_ref, y_ref, h_ref, carry_ref):
        first = pl.program_id(1) == 0

        @pl.when(first)
        def _():
            carry_ref[...] = jnp.zeros_like(carry_ref)

        _, xc, r, i, a, a2, mult = _lru_pre(prec_ref, prev_ref, first, cw_ref, cb_ref, wa_ref, ba_ref, wx_ref, bx_ref, sp_ref)
        b = mult * (i * xc)
        carry = carry_ref[0:1, :]
        for t in range(tc // 8):
            h = _scan8(a[8 * t:8 * t + 8], b[8 * t:8 * t + 8], carry, False)
            h_ref[8 * t:8 * t + 8, :] = h
            carry = h[7:8, :]
        carry_ref[...] = jnp.broadcast_to(carry, carry_ref.shape)
        y_ref[...] = h_ref[...] * _gelu_parts(pg_ref[...])[0]

    cur = pl.BlockSpec((None, tc, W), lambda b, j: (b, j, 0))
    rec = pl.BlockSpec((None, tc, W), lambda b, j: (b, j, 1))
    prev = pl.BlockSpec((None, 8, W), lambda b, j: (b, jnp.maximum(j * (tc // 8) - 1, 0), 1))
    vec = pl.BlockSpec((1, W), lambda b, j: (0, 0))
    cws = pl.BlockSpec((CONV_WIDTH, W), lambda b, j: (0, 0))
    wsp = pl.BlockSpec((LRU_HEADS, LRU_HEAD_DIM, LRU_HEAD_DIM), lambda b, j: (0, 0, 0))
    return pl.pallas_call(
        body, name="lru_fwd", grid=(B, nc),
        in_specs=[cur, rec, prev, cws, vec, wsp, vec, wsp, vec, vec],
        out_specs=[cur, cur],
        out_shape=[jax.ShapeDtypeStruct((B, Tp, W), F32), jax.ShapeDtypeStruct((B, Tp, W), F32)],
        scratch_shapes=[pltpu.VMEM((8, W), F32)],
        compiler_params=pltpu.CompilerParams(dimension_semantics=("arbitrary", "arbitrary")),
    )(p, p, p, cw, cb, wa, ba, wx, bx, sp)


def _lru_bwd_call(p, hseq, dy, cw, cb, wa, ba, wx, bx, sp):
    B, Tp, _ = p.shape
    W = LRU_WIDTH
    tc = SEQ_BLOCK
    nc = Tp // tc
    HD = LRU_HEAD_DIM

    def body(pg_ref, prec_ref, prev_ref, h_ref, hprev_ref, dy_ref, cw_ref, cb_ref, wa_ref, ba_ref, wx_ref, bx_ref, sp_ref,
             dpg_ref, dprec_ref, dcw_ref, dcb_ref, dwa_ref, dba_ref, dwx_ref, dbx_ref, dsp_ref,
             gcar_ref, anext_ref, halo_ref, g_ref):
        j = pl.program_id(1)
        first = j == nc - 1
        last = j == 0

        @pl.when(jnp.logical_and(pl.program_id(0) == 0, last))
        def _():
            for ref in (dcw_ref, dcb_ref, dwa_ref, dba_ref, dwx_ref, dbx_ref, dsp_ref):
                ref[...] = jnp.zeros_like(ref)

        @pl.when(last)
        def _():
            gcar_ref[...] = jnp.zeros_like(gcar_ref)
            anext_ref[...] = jnp.zeros_like(anext_ref)
            halo_ref[...] = jnp.zeros_like(halo_ref)

        taps, xc, r, i, a, a2, mult = _lru_pre(prec_ref, prev_ref, first, cw_ref, cb_ref, wa_ref, ba_ref, wx_ref, bx_ref, sp_ref)
        row = lax.broadcasted_iota(jnp.int32, (tc, W), 0)
        gelu, dgelu = _gelu_parts(pg_ref[...])
        dy = dy_ref[...]
        hcur = h_ref[...]
        dpg_ref[...] = dy * hcur * dgelu
        dh = dy * gelu
        a_next = jnp.where(row == tc - 1, anext_ref[0:1, :], pltpu.roll(a, tc - 1, 0))
        carry = gcar_ref[0:1, :]
        for t in reversed(range(tc // 8)):
            g = _scan8(a_next[8 * t:8 * t + 8], dh[8 * t:8 * t + 8], carry, True)
            g_ref[8 * t:8 * t + 8, :] = g
            carry = g[0:1, :]
        gcar_ref[...] = jnp.broadcast_to(carry, gcar_ref.shape)
        anext_ref[...] = jnp.broadcast_to(a[0:1, :], anext_ref.shape)
        G = g_ref[...]
        h_before = jnp.where(first, 0.0, hprev_ref[7:8, :])
        hprev = jnp.where(row == 0, h_before, pltpu.roll(hcur, 1, 0))
        d_a = G * hprev
        gx_ = G * xc
        d_mult = gx_ * i
        d_i = gx_ * mult
        dxc = G * (mult * i)
        d_la = d_a * a - d_mult * (a2 / mult)
        sp = sp_ref[...]
        d_r = d_la * (-LRU_C * sp)
        dsp_ref[...] += jnp.sum(d_la * (-LRU_C * r), axis=0, keepdims=True)
        dga = d_r * r * (1.0 - r)
        dgx = d_i * i * (1.0 - i)
        dba_ref[...] += jnp.sum(dga, axis=0, keepdims=True)
        dbx_ref[...] += jnp.sum(dgx, axis=0, keepdims=True)
        back = []
        for h in range(LRU_HEADS):
            sl = slice(h * HD, (h + 1) * HD)
            xh = xc[:, sl].astype(MXU_DTYPE)
            ah = dga[:, sl].astype(MXU_DTYPE)
            bh = dgx[:, sl].astype(MXU_DTYPE)
            tn = (((0,), (0,)), ((), ()))
            nt = (((1,), (1,)), ((), ()))
            dwa_ref[h] += lax.dot_general(xh, ah, tn, preferred_element_type=F32)
            dwx_ref[h] += lax.dot_general(xh, bh, tn, preferred_element_type=F32)
            back.append(lax.dot_general(ah, wa_ref[h].astype(MXU_DTYPE), nt, preferred_element_type=F32)
                        + lax.dot_general(bh, wx_ref[h].astype(MXU_DTYPE), nt, preferred_element_type=F32))
        dxc = dxc + jnp.concatenate(back, axis=1)
        dcb_ref[...] += jnp.sum(dxc, axis=0, keepdims=True)
        for k in range(CONV_WIDTH):
            dcw_ref[k:k + 1, :] += jnp.sum(dxc * taps[k], axis=0, keepdims=True)
        ext = jnp.concatenate([dxc, halo_ref[...]], axis=0)
        cw = cw_ref[...]
        acc = cw[CONV_WIDTH - 1:CONV_WIDTH, :] * dxc
        for k in range(CONV_WIDTH - 1):
            s = CONV_WIDTH - 1 - k
            acc = acc + cw[k:k + 1, :] * pltpu.roll(ext, tc + 8 - s, 0)[:tc]
        dprec_ref[...] = acc
        halo_ref[...] = dxc[0:8, :]

    rev = lambda j: nc - 1 - j
    cur = pl.BlockSpec((None, tc, W), lambda b, j: (b, rev(j), 0))
    rec = pl.BlockSpec((None, tc, W), lambda b, j: (b, rev(j), 1))
    prev = pl.BlockSpec((None, 8, W), lambda b, j: (b, jnp.maximum(rev(j) * (tc // 8) - 1, 0), 0))
    prev_rec = pl.BlockSpec((None, 8, W), lambda b, j: (b, jnp.maximum(rev(j) * (tc // 8) - 1, 0), 1))
    vec = pl.BlockSpec((1, W), lambda b, j: (0, 0))
    cws = pl.BlockSpec((CONV_WIDTH, W), lambda b, j: (0, 0))
    wsp = pl.BlockSpec((LRU_HEADS, HD, HD), lambda b, j: (0, 0, 0))
    seq = jax.ShapeDtypeStruct((B, Tp, W), F32)
    vs = jax.ShapeDtypeStruct((1, W), F32)
    ws = jax.ShapeDtypeStruct((LRU_HEADS, HD, HD), F32)
    return pl.pallas_call(
        body, name="lru_bwd", grid=(B, nc),
        in_specs=[cur, rec, prev_rec, cur, prev, cur, cws, vec, wsp, vec, wsp, vec, vec],
        out_specs=[cur, cur, cws, vec, wsp, vec, wsp, vec, vec],
        out_shape=[seq, seq, jax.ShapeDtypeStruct((CONV_WIDTH, W), F32), vs, ws, vs, ws, vs, vs],
        scratch_shapes=[pltpu.VMEM((8, W), F32), pltpu.VMEM((8, W), F32), pltpu.VMEM((8, W), F32), pltpu.VMEM((tc, W), F32)],
        compiler_params=pltpu.CompilerParams(dimension_semantics=("arbitrary", "arbitrary")),
    )(p, p, p, hseq, hseq, dy, cw, cb, wa, ba, wx, bx, sp)


_Q_BLOCK = 2 * LRU_WIDTH // MLA_Q_RANK
_KV_BLOCK = (2 * LRU_WIDTH + MLA_Q_RANK) // MLA_KV_RANK
_KPE_START = 2 * LRU_WIDTH + MLA_Q_RANK + MLA_KV_RANK


@jax.custom_vjp
def even_front(p, cw, cb, wa, ba, wx, bx, sp, gq, gkv):
    return _even_front_fwd(p, cw, cb, wa, ba, wx, bx, sp, gq, gkv)[0]


def _even_front_fwd(p, cw, cb, wa, ba, wx, bx, sp, gq, gkv):
    B, Tp, W = p.shape
    p2d = p.reshape(B * Tp, W)
    y, hseq = _lru_fwd_call(p, cw, cb, wa, ba, wx, bx, sp)
    qn = _rms_fwd_call(p2d, gq, "q_norm_fwd", _Q_BLOCK)
    kvn = _rms_fwd_call(p2d, gkv, "kv_norm_fwd", _KV_BLOCK)
    return (y, qn, kvn, p2d[:, _KPE_START:]), (p, hseq, cw, cb, wa, ba, wx, bx, sp, gq, gkv)


def _even_front_bwd(res, cts):
    p, hseq, cw, cb, wa, ba, wx, bx, sp, gq, gkv = res
    dy, dqn, dkvn, dkpe = cts
    B, Tp, W = p.shape
    p2d = p.reshape(B * Tp, W)
    dpg, dprec, dcw, dcb, dwa, dba, dwx, dbx, dsp = _lru_bwd_call(p, hseq, dy, cw, cb, wa, ba, wx, bx, sp)
    dpq, dgq = _rms_bwd_call(p2d, gq, dqn, "q_norm_bwd", _Q_BLOCK)
    dpkv, dgkv = _rms_bwd_call(p2d, gkv, dkvn, "kv_norm_bwd", _KV_BLOCK)
    dp = jnp.concatenate([dpg.reshape(B * Tp, -1), dprec.reshape(B * Tp, -1), dpq, dpkv, dkpe], axis=1).reshape(B, Tp, W)
    return dp, dcw, dcb, dwa, dba, dwx, dbx, dsp, dgq.reshape(gq.shape), dgkv.reshape(gkv.shape)


even_front.defvjp(_even_front_fwd, _even_front_bwd)


def _rope_tables(pos, half):
    inv = ROPE_BASE ** (-jnp.arange(half, dtype=F32) / half)
    ang = pos.astype(F32)[:, None] * inv[None, :]
    return jnp.cos(ang), jnp.sin(ang)


_NT = (((1,), (1,)), ((), ()))
_TN = (((0,), (0,)), ((), ()))
HEAD_LANES = 128
_MLA_SCALE = (MLA_NOPE + MLA_ROPE) ** -0.5


def _mask_diagonal(s, fill):
    L = s.shape[1]
    row = lax.broadcasted_iota(jnp.int32, (SEQ_BLOCK, SEQ_BLOCK), 0)
    col = lax.broadcasted_iota(jnp.int32, (SEQ_BLOCK, SEQ_BLOCK), 1)
    last = jnp.where(col <= row, s[:, L - SEQ_BLOCK:], fill)
    return last if L == SEQ_BLOCK else jnp.concatenate([s[:, :L - SEQ_BLOCK], last], axis=1)


def _mla_rope_tables(pos):
    half = MLA_ROPE // 2
    cos, sin = _rope_tables(pos, half)
    T = pos.shape[0]
    ones, zeros = jnp.ones((T, MLA_NOPE), F32), jnp.zeros((T, MLA_NOPE), F32)
    tail1, tail0 = jnp.ones((T, HEAD_LANES - MLA_NOPE - MLA_ROPE), F32), jnp.zeros((T, HEAD_LANES - MLA_NOPE - MLA_ROPE), F32)
    zh = jnp.zeros((T, half), F32)
    c = jnp.concatenate([ones, cos, cos, tail1], axis=1)
    s_up = jnp.concatenate([zeros, -sin, zh, tail0], axis=1)
    s_down = jnp.concatenate([zeros, zh, sin, tail0], axis=1)
    return c, s_up, s_down


def _rope_lanes(x, c, s_up, s_down):
    half = MLA_ROPE // 2
    return x * c + pltpu.roll(x, HEAD_LANES - half, 1) * s_up + pltpu.roll(x, half, 1) * s_down


def _unrope_lanes(d, c, s_up, s_down):
    half = MLA_ROPE // 2
    return d * c + pltpu.roll(d * s_up, half, 1) + pltpu.roll(d * s_down, HEAD_LANES - half, 1)


def _mla_operands(q_ref, kv_ref, kpe_ref, c, s_up, s_down):
    lane = lax.broadcasted_iota(jnp.int32, kv_ref.shape, 1)
    qr = _rope_lanes(q_ref[...].astype(F32), c, s_up, s_down).astype(MXU_DTYPE)
    kr = jnp.where(lane < MLA_NOPE, kv_ref[...].astype(F32), _rope_lanes(kpe_ref[...], c, s_up, s_down)).astype(MXU_DTYPE)
    return qr, kr, lane


def _mla_specs(Tp):
    head = pl.BlockSpec((None, Tp, HEAD_LANES), lambda b, h: (b, 0, h))
    shared = pl.BlockSpec((None, Tp, HEAD_LANES), lambda b, h: (b, 0, 0))
    tab = pl.BlockSpec((Tp, HEAD_LANES), lambda b, h: (0, 0))
    lse = pl.BlockSpec((None, None, Tp, 1), lambda b, h: (b, h, 0, 0))
    return head, shared, tab, lse


def _attn_fwd_call(q, kv, kpe, tabs):
    B, Tp, _ = q.shape
    nq = Tp // SEQ_BLOCK

    def body(q_ref, kv_ref, kpe_ref, c_ref, su_ref, sd_ref, o_ref, lse_ref, qr_ref, kr_ref):
        qr, kr, lane = _mla_operands(q_ref, kv_ref, kpe_ref, c_ref[...], su_ref[...], sd_ref[...])
        qr_ref[...] = qr
        kr_ref[...] = kr
        for qi in range(nq):
            L = (qi + 1) * SEQ_BLOCK
            blk = slice(qi * SEQ_BLOCK, L)
            s = lax.dot_general(qr_ref[blk, :], kr_ref[0:L, :], _NT, preferred_element_type=F32) * _MLA_SCALE
            s = _mask_diagonal(s, NEG_INF)
            m = jnp.max(s, axis=-1, keepdims=True)
            p = jnp.exp(s - m)
            l = jnp.sum(p, axis=-1, keepdims=True)
            o = jnp.dot(p.astype(MXU_DTYPE), kv_ref[0:L, :].astype(MXU_DTYPE), preferred_element_type=F32)
            o_ref[blk, :] = jnp.where(lane[blk, :] >= MLA_NOPE, o / l, 0.0)
            lse_ref[blk, :] = m + jnp.log(l)

    head, shared, tab, lse = _mla_specs(Tp)
    return pl.pallas_call(
        body, name="mla_attn_fwd", grid=(B, MLA_HEADS), in_specs=[head, head, shared, tab, tab, tab], out_specs=[head, lse],
        out_shape=[jax.ShapeDtypeStruct((B, Tp, MLA_HEADS * HEAD_LANES), F32), jax.ShapeDtypeStruct((B, MLA_HEADS, Tp, 1), F32)],
        scratch_shapes=[pltpu.VMEM((Tp, HEAD_LANES), MXU_DTYPE), pltpu.VMEM((Tp, HEAD_LANES), MXU_DTYPE)],
        compiler_params=pltpu.CompilerParams(dimension_semantics=("parallel", "parallel")),
    )(q, kv, kpe, *tabs)


def _attn_bwd_call(q, kv, kpe, tabs, o, lse, do):
    B, Tp, _ = q.shape
    nq = Tp // SEQ_BLOCK

    def body(q_ref, kv_ref, kpe_ref, c_ref, su_ref, sd_ref, o_ref, lse_ref, do_ref, dq_ref, dkv_ref, dkpe_ref,
             qr_ref, kr_ref, dqa_ref, dka_ref, dva_ref):
        c, s_up, s_down = c_ref[...], su_ref[...], sd_ref[...]
        qr, kr, lane = _mla_operands(q_ref, kv_ref, kpe_ref, c, s_up, s_down)
        qr_ref[...] = qr
        kr_ref[...] = kr
        dka_ref[...] = jnp.zeros_like(dka_ref)
        dva_ref[...] = jnp.zeros_like(dva_ref)
        for qi in range(nq):
            L = (qi + 1) * SEQ_BLOCK
            blk = slice(qi * SEQ_BLOCK, L)
            qb = qr_ref[blk, :]
            do = jnp.where(lane[blk, :] >= MLA_NOPE, do_ref[blk, :], 0.0)
            delta = jnp.sum(do * o_ref[blk, :], axis=-1, keepdims=True)
            s = lax.dot_general(qb, kr_ref[0:L, :], _NT, preferred_element_type=F32) * _MLA_SCALE
            s = _mask_diagonal(s, NEG_INF)
            p = jnp.exp(s - lse_ref[blk, :])
            dob = do.astype(MXU_DTYPE)
            dva_ref[0:L, :] += lax.dot_general(p.astype(MXU_DTYPE), dob, _TN, preferred_element_type=F32)
            dp = lax.dot_general(dob, kv_ref[0:L, :].astype(MXU_DTYPE), _NT, preferred_element_type=F32)
            ds = (p * (dp - delta) * _MLA_SCALE).astype(MXU_DTYPE)
            dqa_ref[blk, :] = jnp.dot(ds, kr_ref[0:L, :], preferred_element_type=F32)
            dka_ref[0:L, :] += lax.dot_general(ds, qb, _TN, preferred_element_type=F32)
        dq_ref[...] = _unrope_lanes(dqa_ref[...], c, s_up, s_down).astype(dq_ref.dtype)
        dk = dka_ref[...]
        dkv_ref[...] = jnp.where(lane < MLA_NOPE, dk, dva_ref[...]).astype(dkv_ref.dtype)
        dkpe = jnp.where(lane >= MLA_NOPE, _unrope_lanes(dk, c, s_up, s_down), 0.0)

        @pl.when(pl.program_id(1) == 0)
        def _():
            dkpe_ref[...] = dkpe

        @pl.when(pl.program_id(1) > 0)
        def _():
            dkpe_ref[...] += dkpe

    head, shared, tab, lse_spec = _mla_specs(Tp)
    wide = jax.ShapeDtypeStruct((B, Tp, MLA_HEADS * HEAD_LANES), q.dtype)
    acc = pltpu.VMEM((Tp, HEAD_LANES), F32)
    return pl.pallas_call(
        body, name="mla_attn_bwd", grid=(B, MLA_HEADS),
        in_specs=[head, head, shared, tab, tab, tab, head, lse_spec, head], out_specs=[head, head, shared],
        out_shape=[wide, wide, jax.ShapeDtypeStruct((B, Tp, HEAD_LANES), F32)],
        scratch_shapes=[pltpu.VMEM((Tp, HEAD_LANES), MXU_DTYPE), pltpu.VMEM((Tp, HEAD_LANES), MXU_DTYPE), acc, acc, acc],
        compiler_params=pltpu.CompilerParams(dimension_semantics=("parallel", "arbitrary")),
    )(q, kv, kpe, *tabs, o, lse, do)


@jax.custom_vjp
def mla_attention(q, kv, kpe, tabs):
    return _attn_fwd_call(q, kv, kpe, tabs)[0]


def _mla_attention_fwd(q, kv, kpe, tabs):
    o, lse = _attn_fwd_call(q, kv, kpe, tabs)
    return o, (q, kv, kpe, tabs, o, lse)


def _mla_attention_bwd(res, do):
    q, kv, kpe, tabs, o, lse = res
    dq, dkv, dkpe = _attn_bwd_call(q, kv, kpe, tabs, o, lse, do)
    return dq, dkv, dkpe, None


mla_attention.defvjp(_mla_attention_fwd, _mla_attention_bwd)


def _rope_halves(x, cos, sin):
    half = x.shape[1] // 2
    x1, x2 = x[:, :half], x[:, half:]
    return jnp.concatenate([x1 * cos - x2 * sin, x1 * sin + x2 * cos], axis=1)


def _unrope_halves(d, cos, sin):
    half = d.shape[1] // 2
    d1, d2 = d[:, :half], d[:, half:]
    return jnp.concatenate([d1 * cos + d2 * sin, d2 * cos - d1 * sin], axis=1)


_RET_K_SCALE = RET_QK_DIM ** -0.5
_RET_Q_BLOCKS = RET_HEADS
_RET_V_BLOCK0 = 2 * RET_HEADS * RET_QK_DIM // RET_V_DIM
_RET_G_BLOCK0 = _RET_V_BLOCK0 + RET_HEADS


def _ret_specs(Tp):
    q = pl.BlockSpec((None, Tp, RET_QK_DIM), lambda b, h: (b, 0, h))
    k = pl.BlockSpec((None, Tp, RET_QK_DIM), lambda b, h: (b, 0, _RET_Q_BLOCKS + h))
    v = pl.BlockSpec((None, Tp, RET_V_DIM), lambda b, h: (b, 0, _RET_V_BLOCK0 + h))
    tab = pl.BlockSpec((Tp, RET_QK_DIM // 2), lambda b, h: (0, 0))
    lg = pl.BlockSpec((None, 1, 1), lambda b, h: (h, 0, 0))
    return q, k, v, tab, lg


def _ret_operands(q_ref, k_ref, cos, sin, lg):
    t = lax.broadcasted_iota(jnp.int32, (q_ref.shape[0], 1), 0).astype(F32)
    grow, shrink = jnp.exp(-lg * t), jnp.exp(lg * t)
    qs = (_rope_halves(q_ref[...].astype(F32), cos, sin) * shrink).astype(MXU_DTYPE)
    ks = (_rope_halves(k_ref[...].astype(F32), cos, sin) * (grow * _RET_K_SCALE)).astype(MXU_DTYPE)
    return qs, ks, shrink, grow * _RET_K_SCALE


def _ret_core_fwd_call(p, cos, sin, lg):
    B, Tp, _ = p.shape
    nq = Tp // SEQ_BLOCK

    def body(q_ref, k_ref, v_ref, cos_ref, sin_ref, lg_ref, o_ref, qs_ref, ks_ref):
        qs_ref[...], ks_ref[...], _, _ = _ret_operands(q_ref, k_ref, cos_ref[...], sin_ref[...], lg_ref[...])
        for qi in range(nq):
            L = (qi + 1) * SEQ_BLOCK
            blk = slice(qi * SEQ_BLOCK, L)
            s = _mask_diagonal(lax.dot_general(qs_ref[blk, :], ks_ref[0:L, :], _NT, preferred_element_type=F32), 0.0)
            o_ref[blk, :] = jnp.dot(s.astype(MXU_DTYPE), v_ref[0:L, :].astype(MXU_DTYPE), preferred_element_type=F32)

    q, k, v, tab, lgs = _ret_specs(Tp)
    return pl.pallas_call(
        body, name="retention_fwd", grid=(B, RET_HEADS), in_specs=[q, k, v, tab, tab, lgs],
        out_specs=pl.BlockSpec((None, Tp, RET_V_DIM), lambda b, h: (b, 0, h)),
        out_shape=jax.ShapeDtypeStruct((B, Tp, RET_HEADS * RET_V_DIM), F32),
        scratch_shapes=[pltpu.VMEM((Tp, RET_QK_DIM), MXU_DTYPE), pltpu.VMEM((Tp, RET_QK_DIM), MXU_DTYPE)],
        compiler_params=pltpu.CompilerParams(dimension_semantics=("parallel", "parallel")),
    )(p, p, p, cos, sin, lg)


def _ret_core_bwd_call(p, do, cos, sin, lg):
    B, Tp, _ = p.shape
    nq = Tp // SEQ_BLOCK

    def body(q_ref, k_ref, v_ref, do_ref, cos_ref, sin_ref, lg_ref, dq_ref, dk_ref, dv_ref, qs_ref, ks_ref, dqa_ref, dka_ref, dva_ref):
        cos_, sin_ = cos_ref[...], sin_ref[...]
        qs_ref[...], ks_ref[...], q_scale, k_scale = _ret_operands(q_ref, k_ref, cos_, sin_, lg_ref[...])
        dka_ref[...] = jnp.zeros_like(dka_ref)
        dva_ref[...] = jnp.zeros_like(dva_ref)
        for qi in range(nq):
            L = (qi + 1) * SEQ_BLOCK
            blk = slice(qi * SEQ_BLOCK, L)
            qb = qs_ref[blk, :]
            dob = do_ref[blk, :].astype(MXU_DTYPE)
            s = _mask_diagonal(lax.dot_general(qb, ks_ref[0:L, :], _NT, preferred_element_type=F32), 0.0).astype(MXU_DTYPE)
            dva_ref[0:L, :] += lax.dot_general(s, dob, _TN, preferred_element_type=F32)
            ds = _mask_diagonal(lax.dot_general(dob, v_ref[0:L, :].astype(MXU_DTYPE), _NT, preferred_element_type=F32), 0.0).astype(MXU_DTYPE)
            dqa_ref[blk, :] = jnp.dot(ds, ks_ref[0:L, :], preferred_element_type=F32)
            dka_ref[0:L, :] += lax.dot_general(ds, qb, _TN, preferred_element_type=F32)
        dq_ref[...] = _unrope_halves(dqa_ref[...] * q_scale, cos_, sin_).astype(dq_ref.dtype)
        dk_ref[...] = _unrope_halves(dka_ref[...] * k_scale, cos_, sin_).astype(dk_ref.dtype)
        dv_ref[...] = dva_ref[...].astype(dv_ref.dtype)

    q, k, v, tab, lgs = _ret_specs(Tp)
    qk_out = pl.BlockSpec((None, Tp, RET_QK_DIM), lambda b, h: (b, 0, h))
    v_out = pl.BlockSpec((None, Tp, RET_V_DIM), lambda b, h: (b, 0, h))
    return pl.pallas_call(
        body, name="retention_bwd", grid=(B, RET_HEADS), in_specs=[q, k, v, v_out, tab, tab, lgs],
        out_specs=[qk_out, qk_out, v_out],
        out_shape=[jax.ShapeDtypeStruct((B, Tp, RET_HEADS * RET_QK_DIM), p.dtype), jax.ShapeDtypeStruct((B, Tp, RET_HEADS * RET_QK_DIM), p.dtype),
                   jax.ShapeDtypeStruct((B, Tp, RET_HEADS * RET_V_DIM), p.dtype)],
        scratch_shapes=[pltpu.VMEM((Tp, RET_QK_DIM), MXU_DTYPE), pltpu.VMEM((Tp, RET_QK_DIM), MXU_DTYPE),
                        pltpu.VMEM((Tp, RET_QK_DIM), F32), pltpu.VMEM((Tp, RET_QK_DIM), F32), pltpu.VMEM((Tp, RET_V_DIM), F32)],
        compiler_params=pltpu.CompilerParams(dimension_semantics=("parallel", "parallel")),
    )(p, p, p, do, cos, sin, lg)


def _ret_gate_specs(M):
    tm = _pick(M, 1088, 8)
    head = pl.BlockSpec((tm, RET_V_DIM), lambda i, h: (i, h))
    gate = pl.BlockSpec((tm, RET_V_DIM), lambda i, h: (i, _RET_G_BLOCK0 + h))
    return tm, head, gate


def _ret_gate_fwd_call(o, p2d):
    M = o.shape[0]
    tm, head, gate = _ret_gate_specs(M)

    def body(o_ref, g_ref, y_ref):
        ov = o_ref[...]
        gv = g_ref[...].astype(F32)
        rstd = lax.rsqrt(jnp.mean(ov * ov, axis=-1, keepdims=True) + EPS)
        y_ref[...] = (gv * _sigmoid(gv)) * (ov * rstd)

    return pl.pallas_call(
        body, name="retention_gate_fwd", grid=(M // tm, RET_HEADS), in_specs=[head, gate], out_specs=head,
        out_shape=jax.ShapeDtypeStruct(o.shape, F32),
        compiler_params=pltpu.CompilerParams(dimension_semantics=("parallel", "parallel")),
    )(o, p2d)


def _ret_gate_bwd_call(o, p2d, dy):
    M = o.shape[0]
    tm, head, gate = _ret_gate_specs(M)

    def body(o_ref, g_ref, dy_ref, do_ref, dg_ref):
        ov = o_ref[...]
        gv = g_ref[...].astype(F32)
        dy = dy_ref[...]
        rstd = lax.rsqrt(jnp.mean(ov * ov, axis=-1, keepdims=True) + EPS)
        on = ov * rstd
        sg = _sigmoid(gv)
        dg_ref[...] = (dy * on * (sg * (1.0 + gv * (1.0 - sg)))).astype(dg_ref.dtype)
        don = dy * (gv * sg)
        do_ref[...] = (rstd * (don - on * jnp.mean(don * on, axis=-1, keepdims=True))).astype(do_ref.dtype)

    shp = jax.ShapeDtypeStruct(o.shape, p2d.dtype)
    return pl.pallas_call(
        body, name="retention_gate_bwd", grid=(M // tm, RET_HEADS), in_specs=[head, gate, head], out_specs=[head, head],
        out_shape=[shp, shp],
        compiler_params=pltpu.CompilerParams(dimension_semantics=("parallel", "parallel")),
    )(o, p2d, dy)


def _log_gamma():
    return jnp.log(1.0 - 2.0 ** (-5.0 - jnp.arange(RET_HEADS, dtype=F32))).reshape(RET_HEADS, 1, 1)


@jax.custom_vjp
def retention_mixer(p, cos, sin):
    B, Tp, W = p.shape
    o = _ret_core_fwd_call(p, cos, sin, _log_gamma())
    return _ret_gate_fwd_call(o.reshape(B * Tp, -1), p.reshape(B * Tp, W))


def _retention_mixer_fwd(p, cos, sin):
    B, Tp, W = p.shape
    o = _ret_core_fwd_call(p, cos, sin, _log_gamma())
    return _ret_gate_fwd_call(o.reshape(B * Tp, -1), p.reshape(B * Tp, W)), (p, o, cos, sin)


def _retention_mixer_bwd(res, dy):
    p, o, cos, sin = res
    B, Tp, W = p.shape
    do, dg = _ret_gate_bwd_call(o.reshape(B * Tp, -1), p.reshape(B * Tp, W), dy)
    dq, dk, dv = _ret_core_bwd_call(p, do.reshape(B, Tp, -1), cos, sin, _log_gamma())
    return jnp.concatenate([dq, dk, dv, dg.reshape(B, Tp, -1)], axis=-1), None, None


retention_mixer.defvjp(_retention_mixer_fwd, _retention_mixer_bwd)


def _heads_to_lanes(w):
    K = w.shape[0]
    w = w.reshape(K, MLA_HEADS, MLA_NOPE + MLA_ROPE)
    return jnp.pad(w, ((0, 0), (0, 0), (0, HEAD_LANES - MLA_NOPE - MLA_ROPE))).reshape(K, MLA_HEADS * HEAD_LANES)


def _out_rows_to_lanes(w):
    N = w.shape[1]
    att = w[LRU_WIDTH:].reshape(MLA_HEADS, MLA_V, N)
    att = jnp.pad(att, ((0, 0), (HEAD_LANES - MLA_V, 0), (0, 0))).reshape(MLA_HEADS * HEAD_LANES, N)
    return jnp.concatenate([w[:LRU_WIDTH], att], axis=0)


def _seq_dims(x):
    B, S, D = x.shape
    T = S + N_META
    Tp = _round_up(T, SEQ_BLOCK)
    return B, S, T, Tp


def _layer0(diff, w, token):
    x = diff["x"]
    B, S, T, Tp = _seq_dims(x)
    D = x.shape[-1]
    M = B * Tp
    pos = jnp.arange(Tp, dtype=jnp.int32)

    def mm(a, name, act=False, out_dtype=F32, layout=lambda m: m, col_shards=1):
        return matmul(a, layout(w[name]), layout(diff[name]), act, name, out_dtype, col_shards)

    meta = jnp.broadcast_to(diff["meta_tokens"][None], (B, N_META, D))
    h = jnp.concatenate([meta, x + token, jnp.zeros((B, Tp - T, D), F32)], axis=1).reshape(M, D)
    p = mm(h, "ev_w_in")
    sp = jax.nn.softplus(-diff["ev_lru_lambda"]).reshape(1, LRU_WIDTH)
    y_rec, qn, kvn, p_kpe = even_front(
        p.reshape(B, Tp, -1), diff["ev_conv_w"].reshape(CONV_WIDTH, LRU_WIDTH), diff["ev_conv_b"].reshape(1, LRU_WIDTH),
        diff["ev_w_rg_a"].reshape(LRU_HEADS, LRU_HEAD_DIM, LRU_HEAD_DIM), diff["ev_b_rg_a"].reshape(1, LRU_WIDTH),
        diff["ev_w_rg_x"].reshape(LRU_HEADS, LRU_HEAD_DIM, LRU_HEAD_DIM), diff["ev_b_rg_x"].reshape(1, LRU_WIDTH),
        sp, diff["ev_q_norm_g"].reshape(-1), diff["ev_kv_norm_g"].reshape(-1))
    y_rec = y_rec.reshape(M, LRU_WIDTH)
    q = mm(qn, "ev_w_uq", out_dtype=MXU_DTYPE, layout=_heads_to_lanes).reshape(B, Tp, -1)
    kv = mm(kvn, "ev_w_ukv", out_dtype=MXU_DTYPE).reshape(B, Tp, -1)
    kpe = jnp.pad(p_kpe.reshape(B, Tp, MLA_ROPE), ((0, 0), (0, 0), (MLA_NOPE, HEAD_LANES - MLA_NOPE - MLA_ROPE)))
    y_att = mla_attention(q, kv, kpe, _mla_rope_tables(pos)).reshape(M, -1)
    mix = mm(jnp.concatenate([y_rec, y_att], axis=-1), "ev_w_out", layout=_out_rows_to_lanes)
    h = deepnorm(h, mix, diff["ln_mix_g"], diff["ln_mix_b"], "ln_mix0")
    f = mlp(h, w["mlp_w1_0"], w["mlp_w2_0"], diff["mlp_w1_0"], diff["mlp_w2_0"], "mlp0")
    return deepnorm(h, f, diff["ln_mlp_g"], diff["ln_mlp_b"], "ln_mlp0")


def _layer1_loss(diff, h, w, tgt):
    B, S, T, Tp = _seq_dims(tgt)
    D = tgt.shape[-1]
    pos = jnp.arange(Tp, dtype=jnp.int32)

    def mm(a, name, out_dtype=F32, col_shards=1):
        return matmul(a, w[name], diff[name], False, name, out_dtype, col_shards)

    p = mm(h, "od_w_in", out_dtype=MXU_DTYPE, col_shards=N_CHIPS)
    cos, sin = _rope_tables(pos, RET_QK_DIM // 2)
    mix = mm(retention_mixer(p.reshape(B, Tp, -1), cos, sin), "od_w_out")
    h = deepnorm(h, mix, diff["ln_mix_g"], diff["ln_mix_b"], "ln_mix1")
    f = mlp(h, w["mlp_w1_1"], w["mlp_w2_1"], diff["mlp_w1_1"], diff["mlp_w2_1"], "mlp1")
    h = deepnorm(h, f, diff["ln_mlp_g"], diff["ln_mlp_b"], "ln_mlp1")
    y = h.reshape(B, Tp, D)[:, N_META:T].reshape(B * S, D)
    return loss_head(y, tgt.reshape(B * S, D))


_HBM = pl.BlockSpec(memory_space=pltpu.HBM)


def _place():
    return lax.axis_index("x"), lax.axis_index("y"), lax.axis_index("c")


def _other_chips(x, y):
    return [(1 - x, y), (x, 1 - y), (1 - x, 1 - y)]


def _chunks(rows, sublanes, most):
    for q in range(most, 0, -1):
        if rows % (q * sublanes) == 0:
            return q
    return 1


def _sublanes(dtype):
    return 8 * 4 // jnp.dtype(dtype).itemsize


def _allgather_chips(buf, name):
    R, C = buf.shape
    Rh = R // 2
    Q = _chunks(Rh, _sublanes(buf.dtype), 4)
    ch = Rh // Q

    def body(x_ref, out_ref, send_sems, recv_sems):
        x, y, c = _place()
        sibling = (x, y, 1 - c)
        chips = _other_chips(x, y)

        def piece(cx, cy, hc, q):
            return out_ref.at[2 * cx + cy, pl.ds(hc * Rh + q * ch, ch), :]

        def copy(k, src, dst, to):
            return pltpu.make_async_remote_copy(src_ref=src, dst_ref=dst, send_sem=send_sems.at[k], recv_sem=recv_sems.at[k],
                                                device_id=to, device_id_type=MESH)

        first = [copy(j * Q + q, x_ref.at[pl.ds(c * Rh + q * ch, ch), :], piece(x, y, c, q), (*chip, c))
                 for q in range(Q) for j, chip in enumerate(chips)]
        for cp in first:
            cp.start()
        passed = []
        for q in range(Q):
            for j, chip in enumerate(chips):
                landed = piece(*chip, c, q)
                copy(j * Q + q, landed, landed, sibling).wait_recv()
                fwd = copy(3 * Q + j * Q + q, landed, landed, sibling)
                fwd.start()
                passed.append(fwd)
        for q in range(Q):
            for j, chip in enumerate(chips):
                theirs = piece(*chip, 1 - c, q)
                copy(3 * Q + j * Q + q, theirs, theirs, sibling).wait_recv()
        for cp in first + passed:
            cp.wait_send()

    return pl.pallas_call(
        body, name=name, in_specs=[_HBM], out_specs=_HBM,
        out_shape=jax.ShapeDtypeStruct((N_CHIPS, R, C), buf.dtype),
        scratch_shapes=[pltpu.SemaphoreType.DMA((6 * Q,)), pltpu.SemaphoreType.DMA((6 * Q,))],
    )(buf)


def _with_own(gathered, own):
    my = 2 * lax.axis_index("x") + lax.axis_index("y")
    return lax.dynamic_update_slice(gathered, own[None], (my, 0, 0))


def _sibling_exchange(ps, name):
    n = len(ps)

    def body(*refs):
        p_refs, out_refs, (send_sems, recv_sems) = refs[:n], refs[n:2 * n], refs[2 * n:]
        x, y, c = _place()
        copies = [pltpu.make_async_remote_copy(src_ref=p_ref.at[j, 1 - c], dst_ref=out_ref.at[j], send_sem=send_sems.at[N_CHIPS * i + j],
                                               recv_sem=recv_sems.at[N_CHIPS * i + j], device_id=(x, y, 1 - c), device_id_type=MESH)
                  for i, (p_ref, out_ref) in enumerate(zip(p_refs, out_refs)) for j in range(N_CHIPS)]
        for cp in copies:
            cp.start()
        for cp in copies:
            cp.wait()

    return pl.pallas_call(
        body, name=name, in_specs=[_HBM] * n, out_specs=[_HBM] * n,
        out_shape=[jax.ShapeDtypeStruct((N_CHIPS,) + p.shape[2:], p.dtype) for p in ps],
        scratch_shapes=[pltpu.SemaphoreType.DMA((N_CHIPS * n,)), pltpu.SemaphoreType.DMA((N_CHIPS * n,))],
    )(*ps)


def _chip_scatter(ss, name):
    n = len(ss)

    def body(*refs):
        s_refs, t_refs, (send_sems, recv_sems) = refs[:n], refs[n:2 * n], refs[2 * n:]
        x, y, c = _place()
        copies = [pltpu.make_async_remote_copy(src_ref=s_ref.at[j + 1], dst_ref=t_ref.at[j], send_sem=send_sems.at[3 * i + j],
                                               recv_sem=recv_sems.at[3 * i + j], device_id=(cx, cy, c), device_id_type=MESH)
                  for i, (s_ref, t_ref) in enumerate(zip(s_refs, t_refs)) for j, (cx, cy) in enumerate(_other_chips(x, y))]
        for cp in copies:
            cp.start()
        for cp in copies:
            cp.wait()

    return pl.pallas_call(
        body, name=name, in_specs=[_HBM] * n, out_specs=[_HBM] * n,
        out_shape=[jax.ShapeDtypeStruct((3,) + s.shape[1:], s.dtype) for s in ss],
        scratch_shapes=[pltpu.SemaphoreType.DMA((3 * n,)), pltpu.SemaphoreType.DMA((3 * n,))],
    )(*ss)


def _sibling_gather(fs, name):
    n = len(fs)

    def body(*refs):
        out_refs, (send_sems, recv_sems) = refs[n:2 * n], refs[2 * n:]
        x, y, c = _place()
        copies = [pltpu.make_async_remote_copy(src_ref=out_ref.at[c], dst_ref=out_ref.at[c], send_sem=send_sems.at[i], recv_sem=recv_sems.at[i],
                                               device_id=(x, y, 1 - c), device_id_type=MESH) for i, out_ref in enumerate(out_refs)]
        for cp in copies:
            cp.start()
        for cp in copies:
            cp.wait()

    return pl.pallas_call(
        body, name=name, in_specs=[_HBM] * n, out_specs=[_HBM] * n,
        out_shape=[jax.ShapeDtypeStruct(f.shape, f.dtype) for f in fs], input_output_aliases={i: i for i in range(n)},
        scratch_shapes=[pltpu.SemaphoreType.DMA((n,)), pltpu.SemaphoreType.DMA((n,))],
    )(*fs)


def _axis_scalar(name):
    return lax.axis_index(name).astype(jnp.int32).reshape(1)


def _add_own_half(p, got, out_dtype, name):
    n, _, R, C = p.shape
    tr = _pick(R, 512, 16)

    def body(x_ref, y_ref, c_ref, p_ref, g_ref, o_ref):
        o_ref[...] = (p_ref[...] + g_ref[...]).astype(out_dtype)

    def chip(r, x_ref, y_ref):
        return 2 * (x_ref[0] ^ (r & 1)) + (y_ref[0] ^ (r >> 1))

    grid_spec = pltpu.PrefetchScalarGridSpec(
        num_scalar_prefetch=3, grid=(n, R // tr),
        in_specs=[pl.BlockSpec((None, None, tr, C), lambda r, i, x_ref, y_ref, c_ref: (chip(r, x_ref, y_ref), c_ref[0], i, 0)),
                  pl.BlockSpec((None, tr, C), lambda r, i, x_ref, y_ref, c_ref: (chip(r, x_ref, y_ref), i, 0))],
        out_specs=pl.BlockSpec((None, tr, C), lambda r, i, x_ref, y_ref, c_ref: (r, i, 0)))
    return pl.pallas_call(body, name=name, grid_spec=grid_spec, out_shape=jax.ShapeDtypeStruct((n, R, C), out_dtype),
                          compiler_params=pltpu.CompilerParams(dimension_semantics=("parallel", "parallel")))(
        _axis_scalar("x"), _axis_scalar("y"), _axis_scalar("c"), p, got)


def _sum_partials(s, t, name):
    _, R, C = s.shape
    tr = _pick(R, 512, 16)

    def body(c_ref, s_ref, t_ref, o_ref):
        acc = s_ref[...].astype(F32)
        for j in range(3):
            acc = acc + t_ref[j].astype(F32)
        o_ref[...] = acc

    grid_spec = pltpu.PrefetchScalarGridSpec(
        num_scalar_prefetch=1, grid=(R // tr,),
        in_specs=[pl.BlockSpec((None, tr, C), lambda i, c_ref: (0, i, 0)), pl.BlockSpec((3, tr, C), lambda i, c_ref: (0, i, 0))],
        out_specs=pl.BlockSpec((None, tr, C), lambda i, c_ref: (c_ref[0], i, 0)))
    return pl.pallas_call(body, name=name, grid_spec=grid_spec, out_shape=jax.ShapeDtypeStruct((2, R, C), F32),
                          compiler_params=pltpu.CompilerParams(dimension_semantics=("parallel",)))(_axis_scalar("c"), s, t)


def _sibling_reduce(ps, wire_dtypes, tag):
    got = _sibling_exchange(ps, "grad_sibling_exchange_" + tag)
    return [_add_own_half(p, g, dt, "grad_sibling_add_%s%d" % (tag, i)) for i, (p, g, dt) in enumerate(zip(ps, got, wire_dtypes))]


def _sum_and_share(ss, ts, tag):
    fs = [_sum_partials(s, t, "grad_chip_sum_%s%d" % (tag, i)) for i, (s, t) in enumerate(zip(ss, ts))]
    return _sibling_gather(fs, "grad_sibling_gather_" + tag)


_SEM = pl.BlockSpec(memory_space=pltpu.SEMAPHORE)
_ANY = pl.BlockSpec(memory_space=pl.ANY)
_EFFECT = pltpu.SideEffectType.DATAFLOW_SIDE_EFFECTING


def _in_hbm(a):
    return pltpu.with_memory_space_constraint(a, pltpu.HBM)


def _half_copies(x_ref, land_ref, send_sems, recv_sems, Rh, arriving):
    x, y, c = _place()
    rows = pl.ds(c * Rh, Rh)
    return [pltpu.make_async_remote_copy(src_ref=x_ref.at[rows, :], dst_ref=land_ref.at[2 * cx + cy if arriving else 2 * x + y, rows, :],
                                         send_sem=send_sems.at[j], recv_sem=recv_sems.at[j], device_id=(cx, cy, c), device_id_type=MESH)
            for j, (cx, cy) in enumerate(_other_chips(x, y))]


def _allgather_start(buf, name):
    R, C = buf.shape

    def body(x_ref, land_ref, send_sems, recv_sems, x_thru, land_thru, token):
        for cp in _half_copies(x_ref, land_ref, send_sems, recv_sems, R // 2, False):
            cp.start()
        token[...] = jnp.zeros_like(token)

    send_sems, recv_sems, x_thru, land_thru, token = pl.pallas_call(
        body, name=name,
        out_shape=(pltpu.SemaphoreType.DMA((3,)), pltpu.SemaphoreType.DMA((3,)), pltpu.HBM(buf.shape, buf.dtype),
                   pltpu.HBM((N_CHIPS, R, C), buf.dtype), jax.ShapeDtypeStruct((8, 128), F32)),
        in_specs=(_HBM, _HBM), out_specs=(_SEM, _SEM, _HBM, _HBM, pl.BlockSpec(memory_space=pltpu.VMEM)),
        input_output_aliases={0: 2, 1: 3}, compiler_params=pltpu.CompilerParams(has_side_effects=_EFFECT),
    )(_in_hbm(buf), _in_hbm(lax.empty((N_CHIPS, R, C), buf.dtype)))
    return (send_sems, recv_sems, x_thru, land_thru), token[0, 0]


def _allgather_wait(pending, after, name):
    send_sems, recv_sems, x_thru, land_thru = pending
    R = x_thru.shape[0]

    def body(x_ref, land_ref, send_sems, recv_sems, after_ref, x_dead, got_ref):
        for cp in _half_copies(x_ref, land_ref, send_sems, recv_sems, R // 2, False):
            cp.wait_send()
        for cp in _half_copies(x_ref, land_ref, send_sems, recv_sems, R // 2, True):
            cp.wait_recv()

    return pl.pallas_call(
        body, name=name, out_shape=(pltpu.HBM(x_thru.shape, x_thru.dtype), pltpu.HBM(land_thru.shape, land_thru.dtype)),
        in_specs=(_HBM, _HBM, _SEM, _SEM, _ANY), out_specs=(_HBM, _HBM), input_output_aliases={0: 0, 1: 1},
        compiler_params=pltpu.CompilerParams(has_side_effects=_EFFECT),
    )(x_thru, land_thru, send_sems, recv_sems, after)[1]


def _sibling_forward(land, name):
    _, R, C = land.shape
    Rh = R // 2
    Q = _chunks(Rh, _sublanes(land.dtype), 4)
    ch = Rh // Q

    def body(in_ref, out_ref, send_sems, recv_sems):
        x, y, c = _place()
        copies = []
        for j, (cx, cy) in enumerate(_other_chips(x, y)):
            for q in range(Q):
                rows = out_ref.at[2 * cx + cy, pl.ds(c * Rh + q * ch, ch), :]
                copies.append(pltpu.make_async_remote_copy(src_ref=rows, dst_ref=rows, send_sem=send_sems.at[j * Q + q],
                                                           recv_sem=recv_sems.at[j * Q + q], device_id=(x, y, 1 - c), device_id_type=MESH))
        for cp in copies:
            cp.start()
        for cp in copies:
            cp.wait_send()
        for j, (cx, cy) in enumerate(_other_chips(x, y)):
            for q in range(Q):
                rows = out_ref.at[2 * cx + cy, pl.ds((1 - c) * Rh + q * ch, ch), :]
                pltpu.make_async_remote_copy(src_ref=rows, dst_ref=rows, send_sem=send_sems.at[j * Q + q], recv_sem=recv_sems.at[j * Q + q],
                                             device_id=(x, y, 1 - c), device_id_type=MESH).wait_recv()

    return pl.pallas_call(
        body, name=name, in_specs=[_HBM], out_specs=_HBM, out_shape=jax.ShapeDtypeStruct(land.shape, land.dtype),
        input_output_aliases={0: 0},
        scratch_shapes=[pltpu.SemaphoreType.DMA((3 * Q,)), pltpu.SemaphoreType.DMA((3 * Q,))],
    )(land)


def _scatter_copies(s_refs, t_refs, send_sems, recv_sems):
    x, y, c = _place()
    return [pltpu.make_async_remote_copy(src_ref=s_ref.at[j + 1], dst_ref=t_ref.at[j], send_sem=send_sems.at[3 * i + j],
                                         recv_sem=recv_sems.at[3 * i + j], device_id=(cx, cy, c), device_id_type=MESH)
            for i, (s_ref, t_ref) in enumerate(zip(s_refs, t_refs)) for j, (cx, cy) in enumerate(_other_chips(x, y))]


def _chip_scatter_start(ss, name):
    n = len(ss)

    def body(*refs):
        s_refs, t_refs, (send_sems, recv_sems), token = refs[:n], refs[n:2 * n], refs[2 * n:2 * n + 2], refs[-1]
        for cp in _scatter_copies(s_refs, t_refs, send_sems, recv_sems):
            cp.start()
        token[...] = jnp.zeros_like(token)

    lands = [lax.empty((3,) + s.shape[1:], s.dtype) for s in ss]
    out = pl.pallas_call(
        body, name=name,
        out_shape=(pltpu.SemaphoreType.DMA((3 * n,)), pltpu.SemaphoreType.DMA((3 * n,)), *[pltpu.HBM(a.shape, a.dtype) for a in ss + lands],
                   jax.ShapeDtypeStruct((8, 128), F32)),
        in_specs=[_HBM] * (2 * n), out_specs=(_SEM, _SEM, *[_HBM] * (2 * n), pl.BlockSpec(memory_space=pltpu.VMEM)),
        input_output_aliases={i: 2 + i for i in range(2 * n)}, compiler_params=pltpu.CompilerParams(has_side_effects=_EFFECT),
    )(*[_in_hbm(a) for a in ss + lands])
    return (out[0], out[1], list(out[2:2 + n]), list(out[2 + n:2 + 2 * n])), out[-1][0, 0]


def _chip_scatter_wait(pending, after, name):
    send_sems, recv_sems, ss, lands = pending
    n = len(ss)

    def body(*refs):
        s_refs, t_refs, send_sems, recv_sems = refs[:n], refs[n:2 * n], refs[2 * n], refs[2 * n + 1]
        for cp in _scatter_copies(s_refs, t_refs, send_sems, recv_sems):
            cp.wait_send()
            cp.wait_recv()

    out = pl.pallas_call(
        body, name=name, out_shape=tuple(pltpu.HBM(a.shape, a.dtype) for a in ss + lands),
        in_specs=[_HBM] * (2 * n) + [_SEM, _SEM, _ANY], out_specs=tuple([_HBM] * (2 * n)),
        input_output_aliases={i: i for i in range(2 * n)}, compiler_params=pltpu.CompilerParams(has_side_effects=_EFFECT),
    )(*ss, *lands, send_sems, recv_sems, after)
    return list(out[:n]), list(out[n:])


def _adamw(w, g, m, v, name):
    R, C = w.shape
    tr = _pick(R, 256, 8)

    def body(w_ref, g_ref, m_ref, v_ref, d_ref, nm_ref, nv_ref):
        g_ = g_ref[...]
        m_ = ADAM_B1 * m_ref[...] + (1.0 - ADAM_B1) * g_
        v_ = ADAM_B2 * v_ref[...] + (1.0 - ADAM_B2) * (g_ * g_)
        m_hat = m_ / (1.0 - ADAM_B1 ** ADAM_STEP)
        v_hat = v_ / (1.0 - ADAM_B2 ** ADAM_STEP)
        d_ref[...] = -ADAM_LR * (m_hat / (jnp.sqrt(v_hat) + ADAM_EPS) + ADAM_WD * w_ref[...])
        nm_ref[...] = m_
        nv_ref[...] = v_

    row = pl.BlockSpec((tr, C), lambda i: (i, 0))
    shp = jax.ShapeDtypeStruct((R, C), F32)
    return pl.pallas_call(body, name=name, grid=(R // tr,), in_specs=[row] * 4, out_specs=[row] * 3, out_shape=[shp] * 3,
                          compiler_params=pltpu.CompilerParams(dimension_semantics=("parallel",)))(w, g, m, v)


BIG_SPECS = (("ev_w_in", 1024, 1440, 1), ("ev_w_uq", 256, 768, 1), ("ev_w_ukv", 128, 1024, 1), ("ev_w_out", 1024, 1024, 0),
             ("od_w_in", 1024, 6144, 1), ("od_w_out", 2048, 1024, 0), ("mlp_w1_0", 1024, 4096, 1), ("mlp_w1_1", 1024, 4096, 1),
             ("mlp_w2_0", 4096, 1024, 0), ("mlp_w2_1", 4096, 1024, 0))
BIG_PARAMS = (("ev_w_in", ("ev_w_in",)), ("ev_w_uq", ("ev_w_uq",)), ("ev_w_ukv", ("ev_w_ukv",)), ("ev_w_out", ("ev_w_out",)),
              ("od_w_in", ("od_w_in",)), ("od_w_out", ("od_w_out",)), ("mlp_w1", ("mlp_w1_0", "mlp_w1_1")),
              ("mlp_w2", ("mlp_w2_0", "mlp_w2_1")))
REPLICATED = ("ev_conv_b", "ev_w_rg_a", "ev_b_rg_a", "ev_w_rg_x", "ev_b_rg_x", "ev_lru_lambda", "ev_q_norm_g", "ev_kv_norm_g",
              "ln_mix_g", "ln_mix_b", "ln_mlp_g", "ln_mlp_b")
SMALL_SHARDED = ("meta_tokens", "ev_conv_w")
COL_SHARD_GRADS = ("od_w_in", "mlp_w1_0", "mlp_w1_1")
LAYER0_MATRICES = ("ev_w_in", "ev_w_uq", "ev_w_ukv", "ev_w_out", "mlp_w1_0", "mlp_w2_0")
LAYER1_MATRICES = ("od_w_in", "od_w_out", "mlp_w1_1", "mlp_w2_1")
LAYER_NORMS = ("ln_mix_g", "ln_mix_b", "ln_mlp_g", "ln_mlp_b")
WEIGHT_NAMES = ("meta_tokens", "ev_w_in", "ev_conv_w", "ev_conv_b", "ev_w_rg_a", "ev_b_rg_a", "ev_w_rg_x", "ev_b_rg_x",
                "ev_lru_lambda", "ev_q_norm_g", "ev_w_uq", "ev_kv_norm_g", "ev_w_ukv", "ev_w_out", "od_w_in", "od_w_out",
                "ln_mix_g", "ln_mix_b", "mlp_w1", "mlp_w2", "ln_mlp_g", "ln_mlp_b")


def _to_rows(flat, row_align):
    n = flat.shape[-1]
    rows = _round_up(-(-n // PACK_COLS), row_align)
    pad = rows * PACK_COLS - n
    if pad:
        flat = jnp.pad(flat, [(0, 0)] * (flat.ndim - 1) + [(0, pad)])
    return flat.reshape(flat.shape[:-1] + (rows, PACK_COLS))


def _shard_shape(K, N, axis):
    return (K // N_CHIPS, N) if axis == 0 else (K, N // N_CHIPS)


def _gather_shards(stacked, K, N, axis):
    if axis == 0:
        return stacked.reshape(K, N)
    return stacked.transpose(1, 0, 2).reshape(K, N)


def _split_shards(full, K, N, axis):
    if axis == 0:
        return full.reshape(N_CHIPS, -1)
    return full.reshape(K, N_CHIPS, N // N_CHIPS).transpose(1, 0, 2).reshape(N_CHIPS, -1)


def kernel(x, meta_tokens, ev_w_in, ev_conv_w, ev_conv_b, ev_w_rg_a, ev_b_rg_a, ev_w_rg_x, ev_b_rg_x, ev_lru_lambda, ev_q_norm_g, ev_w_uq, ev_kv_norm_g, ev_w_ukv, ev_w_out, od_w_in, od_w_out, ln_mix_g, ln_mix_b, mlp_w1, mlp_w2, ln_mlp_g, ln_mlp_b, loss_target, m_meta_tokens, m_ev_w_in, m_ev_conv_w, m_ev_conv_b, m_ev_w_rg_a, m_ev_b_rg_a, m_ev_w_rg_x, m_ev_b_rg_x, m_ev_lru_lambda, m_ev_q_norm_g, m_ev_w_uq, m_ev_kv_norm_g, m_ev_w_ukv, m_ev_w_out, m_od_w_in, m_od_w_out, m_ln_mix_g, m_ln_mix_b, m_mlp_w1, m_mlp_w2, m_ln_mlp_g, m_ln_mlp_b, v_meta_tokens, v_ev_w_in, v_ev_conv_w, v_ev_conv_b, v_ev_w_rg_a, v_ev_b_rg_a, v_ev_w_rg_x, v_ev_b_rg_x, v_ev_lru_lambda, v_ev_q_norm_g, v_ev_w_uq, v_ev_kv_norm_g, v_ev_w_ukv, v_ev_w_out, v_od_w_in, v_od_w_out, v_ln_mix_g, v_ln_mix_b, v_mlp_w1, v_mlp_w2, v_ln_mlp_g, v_ln_mlp_b):
    given = dict(locals())
    local_big = {"ev_w_in": ev_w_in[0], "ev_w_uq": ev_w_uq[0], "ev_w_ukv": ev_w_ukv[0], "ev_w_out": ev_w_out[0],
                 "od_w_in": od_w_in[0], "od_w_out": od_w_out[0], "mlp_w1_0": mlp_w1[0], "mlp_w1_1": mlp_w1[1],
                 "mlp_w2_0": mlp_w2[0], "mlp_w2_1": mlp_w2[1]}

    specs = {spec[0]: spec for spec in BIG_SPECS}

    def pack(names):
        return _to_rows(jnp.concatenate([local_big[n].astype(MXU_DTYPE).reshape(-1) for n in names]), 256)

    def unpack(gathered, names):
        gathered, out, off = gathered.reshape(N_CHIPS, -1), {}, 0
        for n in names:
            _, K, N, ax = specs[n]
            shard = _shard_shape(K, N, ax)
            out[n] = _gather_shards(gathered[:, off:off + math.prod(shard)].reshape((N_CHIPS,) + shard), K, N, ax)
            off += math.prod(shard)
        return out

    packed0, packed1 = pack(LAYER0_MATRICES), pack(LAYER1_MATRICES)
    gathered0 = _with_own(_allgather_chips(packed0, "weight_allgather_layer0"), packed0)
    pending1, token = _allgather_start(packed1, "weight_allgather_layer1_start")
    small = _to_rows(jnp.concatenate([meta_tokens.reshape(-1), ev_conv_w.reshape(-1)]), 16)
    small = _with_own(_allgather_chips(small, "small_allgather"), small).reshape(N_CHIPS, -1)
    n_meta, n_conv = meta_tokens.size, ev_conv_w.size
    meta_full = _gather_shards(small[:, :n_meta].reshape(N_CHIPS, N_META, D_MODEL // N_CHIPS), N_META, D_MODEL, 1)
    conv_full = _gather_shards(small[:, n_meta:n_meta + n_conv].reshape(N_CHIPS, CONV_WIDTH, LRU_WIDTH // N_CHIPS),
                               CONV_WIDTH, LRU_WIDTH, 1)

    def slots(names):
        return {n: jnp.zeros((N_CHIPS, specs[n][1], specs[n][2] // N_CHIPS) if n in COL_SHARD_GRADS else specs[n][1:3], F32) for n in names}

    def norms(layer):
        return {n: given[n][layer] for n in LAYER_NORMS}

    diff0 = {**slots(LAYER0_MATRICES), **norms(0), **{n: given[n] for n in REPLICATED if n not in LAYER_NORMS},
             "x": x, "meta_tokens": meta_full, "ev_conv_w": conv_full}
    diff1 = {**slots(LAYER1_MATRICES), **norms(1)}
    w0 = unpack(gathered0, LAYER0_MATRICES)
    h, back0 = jax.vjp(lambda d: _layer0(d, w0, token), diff0)
    landed1 = _allgather_wait(pending1, lax.stop_gradient(h), "weight_allgather_layer1_wait")
    w1 = unpack(_with_own(_sibling_forward(landed1, "weight_allgather_layer1_forward"), packed1), LAYER1_MATRICES)
    loss, back1 = jax.vjp(lambda d, hh: _layer1_loss(d, hh, w1, loss_target), diff1, h)
    g1, dh = back1(jnp.ones((), F32))
    loss = lax.psum(loss, ("x", "y", "c"))

    def blocks_of(grad, n):
        _, K, N, ax = specs[n]
        if n in COL_SHARD_GRADS:
            blocks = grad
        elif ax == 0:
            blocks = grad.reshape(N_CHIPS, K // N_CHIPS, N)
        else:
            blocks = grad.reshape(K, N_CHIPS, N // N_CHIPS).transpose(1, 0, 2)
        return blocks.reshape(N_CHIPS, 2, blocks.shape[1] // 2, blocks.shape[2])

    ss1 = _sibling_reduce([blocks_of(g1[n], n) for n in LAYER1_MATRICES], [MXU_DTYPE] * len(LAYER1_MATRICES), "layer1_")
    pending, token = _chip_scatter_start(ss1, "grad_chip_scatter_layer1_start")
    (g0,) = back0(dh + token)
    ss1, ts1 = _chip_scatter_wait(pending, g0["x"], "grad_chip_scatter_layer1_wait")

    g = {**g0, **g1}
    for n in LAYER_NORMS:
        g[n] = jnp.stack([g0[n], g1[n]])
    repl = jnp.concatenate([g[n].reshape(-1) for n in REPLICATED]).reshape(N_CHIPS, -1)
    small = [_split_shards(g["meta_tokens"], N_META, D_MODEL, 1), _split_shards(g["ev_conv_w"], CONV_WIDTH, LRU_WIDTH, 1), repl]
    small = [pc.reshape(N_CHIPS, 2, -1) for pc in small]
    n_small = sum(pc.shape[2] for pc in small)
    small.append(jnp.zeros((N_CHIPS, 2, _round_up(n_small, 32 * PACK_COLS) - n_small), F32))
    p_small = jnp.concatenate(small, axis=2).reshape(N_CHIPS, 2, -1, PACK_COLS)
    ss0 = _sibling_reduce([blocks_of(g0[n], n) for n in LAYER0_MATRICES] + [p_small], [MXU_DTYPE] * len(LAYER0_MATRICES) + [F32], "layer0_")
    ts0 = list(_chip_scatter(ss0, "grad_chip_scatter_layer0"))
    reduced = _sum_and_share(ss0 + ss1, ts0 + ts1, "")
    red_big = dict(zip(LAYER0_MATRICES, reduced[:len(LAYER0_MATRICES)]))
    red_big.update(zip(LAYER1_MATRICES, reduced[len(LAYER0_MATRICES) + 1:]))
    red_small = reduced[len(LAYER0_MATRICES)].reshape(2, -1)

    grads = {}
    for name, parts in BIG_PARAMS:
        grads[name] = jnp.stack([red_big[part].reshape(given[name].shape[1:]) for part in parts])

    def take(off, sz):
        return jnp.concatenate([red_small[0, off // 2:(off + sz) // 2], red_small[1, off // 2:(off + sz) // 2]])

    off = 0
    for name in SMALL_SHARDED:
        sz = given[name].size
        grads[name] = take(off, sz).reshape(given[name].shape)
        off += sz
    n_repl = repl.shape[1]
    own_repl = _to_rows(take(off, n_repl), 16)
    repl_all = _with_own(_allgather_chips(own_repl, "replicated_allgather"), own_repl).reshape(N_CHIPS, -1)[:, :n_repl].reshape(-1)
    off = 0
    for name in REPLICATED:
        sz = given[name].size
        grads[name] = repl_all[off:off + sz].reshape(given[name].shape)
        off += sz

    delta, new_m, new_v = {}, {}, {}
    for name, _ in BIG_PARAMS:
        shp = given[name].shape
        two_d = (-1, shp[-1])
        d, nm, nv = _adamw(given[name].reshape(two_d), grads[name].reshape(two_d), given["m_" + name].reshape(two_d),
                           given["v_" + name].reshape(two_d), "adamw_" + name)
        delta[name], new_m[name], new_v[name] = d.reshape(shp), nm.reshape(shp), nv.reshape(shp)
    smalls = SMALL_SHARDED + REPLICATED

    def pack_small(get):
        return _to_rows(jnp.concatenate([get(n).reshape(-1) for n in smalls]), 8)

    outs = _adamw(pack_small(lambda n: given[n]), pack_small(lambda n: grads[n]), pack_small(lambda n: given["m_" + n]),
                  pack_small(lambda n: given["v_" + n]), "adamw_small")
    for res, flat in zip((delta, new_m, new_v), outs):
        flat, off = flat.reshape(-1), 0
        for n in smalls:
            sz = given[n].size
            res[n] = flat[off:off + sz].reshape(given[n].shape)
            off += sz

    return (loss, g0["x"], *[grads[n] for n in WEIGHT_NAMES], *[delta[n] for n in WEIGHT_NAMES],
            *[new_m[n] for n in WEIGHT_NAMES], *[new_v[n] for n in WEIGHT_NAMES])
```

```python
import functools
import math

import jax
import jax.numpy as jnp
from jax import lax
from jax.experimental import pallas as pl
from jax.experimental.pallas import tpu as pltpu

F32 = jnp.float32
MXU_DTYPE = jnp.bfloat16

D_MODEL = 1024
N_META = 16
LRU_WIDTH = 512
LRU_HEADS = 4
LRU_HEAD_DIM = 128
CONV_WIDTH = 4
LRU_C = 8.0
MLA_HEADS = 8
MLA_NOPE = 64
MLA_ROPE = 32
MLA_V = 64
MLA_Q_RANK = 256
MLA_KV_RANK = 128
RET_HEADS = 4
RET_QK_DIM = 256
RET_V_DIM = 512
D_FF = 4096
ROPE_BASE = 10000.0
DN_ALPHA = 4.0 ** 0.25
EPS = 1e-5
NEG_INF = -1e30
SEQ_BLOCK = 128

ADAM_LR = 0.001
ADAM_B1 = 0.9
ADAM_B2 = 0.999
ADAM_EPS = 1e-08
ADAM_WD = 0.01
ADAM_STEP = 10

PACK_COLS = 1024
N_CHIPS = 4

MESH = pl.DeviceIdType.MESH


def _pick(n, target, align):
    best = None
    for t in range(align, min(n, target) + 1, align):
        if n % t == 0:
            best = t
    return n if best is None else best


def _round_up(n, m):
    return (n + m - 1) // m * m


def _relu2(a):
    r = jnp.maximum(a, 0.0)
    return r * r


def _mm_nn(a, w, act, name, out_dtype=F32):
    M, K = a.shape
    _, N = w.shape
    tm = _pick(M, 1088 if K * a.dtype.itemsize <= 4096 else 544, 8)
    tn = _pick(N, 1024, 128)

    def body(a_ref, w_ref, o_ref):
        av = a_ref[...]
        if act:
            av = _relu2(av.astype(F32))
        o_ref[...] = jnp.dot(av.astype(MXU_DTYPE), w_ref[...].astype(MXU_DTYPE), preferred_element_type=F32).astype(out_dtype)

    return pl.pallas_call(
        body, name=name,
        grid=(M // tm, N // tn),
        in_specs=[pl.BlockSpec((tm, K), lambda i, j: (i, 0)), pl.BlockSpec((K, tn), lambda i, j: (0, j))],
        out_specs=pl.BlockSpec((tm, tn), lambda i, j: (i, j)),
        out_shape=jax.ShapeDtypeStruct((M, N), out_dtype),
        compiler_params=pltpu.CompilerParams(dimension_semantics=("parallel", "arbitrary")),
    )(a, w)


def _mm_nt(g, w, a_src, name, out_dtype=F32):
    M, N = g.shape
    K, _ = w.shape
    tk = N if N * g.dtype.itemsize <= 8192 else _pick(N, 2048, 128)
    nk = N // tk
    tm = _pick(M, 1088 if tk * g.dtype.itemsize <= 4096 else 544, 8)
    tn = _pick(K, 1024, 128)
    has_src = a_src is not None
    assert nk == 1 or out_dtype == F32

    def body(*refs):
        if has_src:
            g_ref, w_ref, s_ref, o_ref = refs
        else:
            g_ref, w_ref, o_ref = refs
        r = lax.dot_general(g_ref[...].astype(MXU_DTYPE), w_ref[...].astype(MXU_DTYPE),
                            (((1,), (1,)), ((), ())), preferred_element_type=F32)
        if has_src:
            r = r * (2.0 * jnp.maximum(s_ref[...].astype(F32), 0.0))
        if nk == 1:
            o_ref[...] = r.astype(out_dtype)
        else:
            k = pl.program_id(2)

            @pl.when(k == 0)
            def _():
                o_ref[...] = r

            @pl.when(k > 0)
            def _():
                o_ref[...] += r

    in_specs = [pl.BlockSpec((tm, tk), lambda i, j, k: (i, k)), pl.BlockSpec((tn, tk), lambda i, j, k: (j, k))]
    args = [g, w]
    if has_src:
        assert nk == 1
        in_specs.append(pl.BlockSpec((tm, tn), lambda i, j, k: (i, j)))
        args.append(a_src)
    return pl.pallas_call(
        body, name=name,
        grid=(M // tm, K // tn, nk),
        in_specs=in_specs,
        out_specs=pl.BlockSpec((tm, tn), lambda i, j, k: (i, j)),
        out_shape=jax.ShapeDtypeStruct((M, K), out_dtype),
        compiler_params=pltpu.CompilerParams(dimension_semantics=("parallel", "parallel", "arbitrary")),
    )(*args)


def _mm_tn(a, g, act, name, col_shards=1):
    M, K = a.shape
    _, N = g.shape
    n = N // col_shards
    tm, tn, tk = _pick(K, 1024, 128), _pick(n, 1024, 128), _pick(M, 1088, 8)
    nk = M // tk
    per = n // tn

    def body(a_ref, g_ref, o_ref):
        k = pl.program_id(2)
        av = a_ref[...]
        if act:
            av = _relu2(av.astype(F32))
        r = lax.dot_general(av.astype(MXU_DTYPE), g_ref[...].astype(MXU_DTYPE),
                            (((0,), (0,)), ((), ())), preferred_element_type=F32)

        @pl.when(k == 0)
        def _():
            o_ref[...] = r

        @pl.when(k > 0)
        def _():
            o_ref[...] += r

    if col_shards == 1:
        out_spec, out_shape = pl.BlockSpec((tm, tn), lambda i, j, k: (i, j)), (K, N)
    else:
        out_spec, out_shape = pl.BlockSpec((None, tm, tn), lambda i, j, k: (j // per, i, j % per)), (col_shards, K, n)
    return pl.pallas_call(
        body, name=name,
        grid=(K // tm, N // tn, nk),
        in_specs=[pl.BlockSpec((tk, tm), lambda i, j, k: (k, i)), pl.BlockSpec((tk, tn), lambda i, j, k: (k, j))],
        out_specs=out_spec,
        out_shape=jax.ShapeDtypeStruct(out_shape, F32),
        compiler_params=pltpu.CompilerParams(dimension_semantics=("parallel", "parallel", "arbitrary")),
    )(a, g)


@functools.partial(jax.custom_vjp, nondiff_argnums=(3, 4, 5, 6))
def matmul(a, w, w_grad_slot, act, name, out_dtype, col_shards):
    return _mm_nn(a, w, act, name + "_fwd", out_dtype)


def _matmul_fwd(a, w, w_grad_slot, act, name, out_dtype, col_shards):
    return _mm_nn(a, w, act, name + "_fwd", out_dtype), (a, w)


def _matmul_bwd(act, name, out_dtype, col_shards, res, g):
    a, w = res
    da = _mm_nt(g, w, a if act else None, name + "_dx")
    dw = _mm_tn(a, g, act, name + "_dw", col_shards)
    return da, None, dw


matmul.defvjp(_matmul_fwd, _matmul_bwd)


@functools.partial(jax.custom_vjp, nondiff_argnums=(5,))
def mlp(h, w1, w2, w1_grad_slot, w2_grad_slot, name):
    u = _mm_nn(h, w1, False, name + "_w1_fwd", out_dtype=MXU_DTYPE)
    return _mm_nn(u, w2, True, name + "_w2_fwd")


def _mlp_fwd(h, w1, w2, w1_grad_slot, w2_grad_slot, name):
    u = _mm_nn(h, w1, False, name + "_w1_fwd", out_dtype=MXU_DTYPE)
    return _mm_nn(u, w2, True, name + "_w2_fwd"), (h, u, w1, w2)


def _mlp_bwd(name, res, df):
    h, u, w1, w2 = res
    du = _mm_nt(df, w2, u, name + "_w2_dx", out_dtype=MXU_DTYPE)
    dw2 = _mm_tn(u, df, True, name + "_w2_dw")
    dh = _mm_nt(du, w1, None, name + "_w1_dx")
    dw1 = _mm_tn(h, du, False, name + "_w1_dw", N_CHIPS)
    return dh, None, None, dw1, dw2


mlp.defvjp(_mlp_fwd, _mlp_bwd)


def _ln_stats(z):
    mu = jnp.mean(z, axis=-1, keepdims=True)
    zc = z - mu
    var = jnp.mean(zc * zc, axis=-1, keepdims=True)
    return zc, lax.rsqrt(var + EPS)


def _ln_fwd_call(resid, branch, g, b, name):
    M, D = resid.shape
    tm = _pick(M, 544, 8)

    def body(r_ref, br_ref, g_ref, b_ref, o_ref):
        zc, rstd = _ln_stats(DN_ALPHA * r_ref[...] + br_ref[...])
        o_ref[...] = zc * rstd * g_ref[...] + b_ref[...]

    row = pl.BlockSpec((tm, D), lambda i: (i, 0))
    vec = pl.BlockSpec((1, D), lambda i: (0, 0))
    return pl.pallas_call(
        body, name=name, grid=(M // tm,), in_specs=[row, row, vec, vec], out_specs=row,
        out_shape=jax.ShapeDtypeStruct((M, D), F32),
        compiler_params=pltpu.CompilerParams(dimension_semantics=("parallel",)),
    )(resid, branch, g.reshape(1, D), b.reshape(1, D))


def _ln_bwd_call(resid, branch, g, dy, name):
    M, D = resid.shape
    tm = _pick(M, 544, 8)

    def body(r_ref, br_ref, g_ref, dy_ref, dz_ref, dg_ref, db_ref):
        @pl.when(pl.program_id(0) == 0)
        def _():
            dg_ref[...] = jnp.zeros_like(dg_ref)
            db_ref[...] = jnp.zeros_like(db_ref)

        zc, rstd = _ln_stats(DN_ALPHA * r_ref[...] + br_ref[...])
        xhat = zc * rstd
        dy = dy_ref[...]
        dxh = dy * g_ref[...]
        m1 = jnp.mean(dxh, axis=-1, keepdims=True)
        m2 = jnp.mean(dxh * xhat, axis=-1, keepdims=True)
        dz_ref[...] = rstd * (dxh - m1 - xhat * m2)
        dg_ref[...] += jnp.sum(dy * xhat, axis=0, keepdims=True)
        db_ref[...] += jnp.sum(dy, axis=0, keepdims=True)

    row = pl.BlockSpec((tm, D), lambda i: (i, 0))
    vec = pl.BlockSpec((1, D), lambda i: (0, 0))
    return pl.pallas_call(
        body, name=name, grid=(M // tm,), in_specs=[row, row, vec, row], out_specs=[row, vec, vec],
        out_shape=[jax.ShapeDtypeStruct((M, D), F32), jax.ShapeDtypeStruct((1, D), F32), jax.ShapeDtypeStruct((1, D), F32)],
        compiler_params=pltpu.CompilerParams(dimension_semantics=("arbitrary",)),
    )(resid, branch, g.reshape(1, D), dy)


@functools.partial(jax.custom_vjp, nondiff_argnums=(4,))
def deepnorm(resid, branch, g, b, name):
    return _ln_fwd_call(resid, branch, g, b, name + "_fwd")


def _deepnorm_fwd(resid, branch, g, b, name):
    return _ln_fwd_call(resid, branch, g, b, name + "_fwd"), (resid, branch, g)


def _deepnorm_bwd(name, res, dy):
    resid, branch, g = res
    dz, dg, db = _ln_bwd_call(resid, branch, g, dy, name + "_bwd")
    return DN_ALPHA * dz, dz, dg.reshape(g.shape), db.reshape(g.shape)


deepnorm.defvjp(_deepnorm_fwd, _deepnorm_bwd)


def _rms_fwd_call(x, g, name, col_block=0):
    R = x.shape[0]
    W = g.shape[-1]
    tr = _pick(R, 1088, 8)

    def body(x_ref, g_ref, o_ref):
        xv = x_ref[...]
        rstd = lax.rsqrt(jnp.mean(xv * xv, axis=-1, keepdims=True) + EPS)
        o_ref[...] = xv * rstd * g_ref[...]

    vec = pl.BlockSpec((1, W), lambda i: (0, 0))
    return pl.pallas_call(
        body, name=name, grid=(R // tr,), in_specs=[pl.BlockSpec((tr, W), lambda i: (i, col_block)), vec],
        out_specs=pl.BlockSpec((tr, W), lambda i: (i, 0)), out_shape=jax.ShapeDtypeStruct((R, W), F32),
        compiler_params=pltpu.CompilerParams(dimension_semantics=("parallel",)),
    )(x, g.reshape(1, W))


def _rms_bwd_call(x, g, dy, name, col_block=0):
    R = x.shape[0]
    W = g.shape[-1]
    tr = _pick(R, 1088, 8)

    def body(x_ref, g_ref, dy_ref, dx_ref, dg_ref):
        @pl.when(pl.program_id(0) == 0)
        def _():
            dg_ref[...] = jnp.zeros_like(dg_ref)

        xv = x_ref[...]
        rstd = lax.rsqrt(jnp.mean(xv * xv, axis=-1, keepdims=True) + EPS)
        xhat = xv * rstd
        dy = dy_ref[...]
        dxh = dy * g_ref[...]
        dx_ref[...] = rstd * (dxh - xhat * jnp.mean(dxh * xhat, axis=-1, keepdims=True))
        dg_ref[...] += jnp.sum(dy * xhat, axis=0, keepdims=True)

    row = pl.BlockSpec((tr, W), lambda i: (i, 0))
    vec = pl.BlockSpec((1, W), lambda i: (0, 0))
    return pl.pallas_call(
        body, name=name, grid=(R // tr,), in_specs=[pl.BlockSpec((tr, W), lambda i: (i, col_block)), vec, row], out_specs=[row, vec],
        out_shape=[jax.ShapeDtypeStruct((R, W), F32), jax.ShapeDtypeStruct((1, W), F32)],
        compiler_params=pltpu.CompilerParams(dimension_semantics=("arbitrary",)),
    )(x, g.reshape(1, W), dy)


def _loss_call(y, tgt, name):
    R, D = y.shape
    tr = _pick(R, 512, 8)

    def body(y_ref, t_ref, dy_ref, acc_ref):
        @pl.when(pl.program_id(0) == 0)
        def _():
            acc_ref[...] = jnp.zeros_like(acc_ref)

        e = y_ref[...] - t_ref[...]
        dy_ref[...] = e * (1.0 / D)
        acc_ref[...] += jnp.sum(jnp.sum(e * e, axis=-1, keepdims=True), axis=0, keepdims=True) * (0.5 / D)

    row = pl.BlockSpec((tr, D), lambda i: (i, 0))
    one = pl.BlockSpec((1, 1), lambda i: (0, 0))
    return pl.pallas_call(
        body, name=name, grid=(R // tr,), in_specs=[row, row], out_specs=[row, one],
        out_shape=[jax.ShapeDtypeStruct((R, D), F32), jax.ShapeDtypeStruct((1, 1), F32)],
        compiler_params=pltpu.CompilerParams(dimension_semantics=("arbitrary",)),
    )(y, tgt)


@jax.custom_vjp
def loss_head(y, tgt):
    return _loss_call(y, tgt, "loss_head")[1][0, 0]


def _loss_head_fwd(y, tgt):
    dy, acc = _loss_call(y, tgt, "loss_head")
    return acc[0, 0], dy


def _loss_head_bwd(dy, ct):
    return ct * dy, None


loss_head.defvjp(_loss_head_fwd, _loss_head_bwd)


_GELU_C = math.sqrt(2.0 / math.pi)


def _gelu_parts(x):
    x2 = x * x
    t = jnp.tanh(_GELU_C * (x + 0.044715 * x * x2))
    gelu = 0.5 * x * (1.0 + t)
    dgelu = 0.5 * (1.0 + t) + 0.5 * x * (1.0 - t * t) * (_GELU_C * (1.0 + 3.0 * 0.044715 * x2))
    return gelu, dgelu


def _sigmoid(x):
    return 1.0 / (1.0 + jnp.exp(-x))


def _scan8(a, b, carry, reverse):
    row = lax.broadcasted_iota(jnp.int32, a.shape, 0)
    for s in (1, 2, 4):
        shift = 8 - s if reverse else s
        keep = (row < 8 - s) if reverse else (row >= s)
        b = jnp.where(keep, a * pltpu.roll(b, shift, 0) + b, b)
        a = jnp.where(keep, a * pltpu.roll(a, shift, 0), a)
    return a * carry + b


def _lru_pre(prec_ref, prev_ref, first, cw_ref, cb_ref, wa_ref, ba_ref, wx_ref, bx_ref, sp_ref):
    tc = prec_ref.shape[0]
    prev = jnp.where(first, 0.0, prev_ref[...])
    ext = jnp.concatenate([prev, prec_ref[...]], axis=0)
    cw = cw_ref[...]
    taps = [ext[8:] if k == CONV_WIDTH - 1 else pltpu.roll(ext, CONV_WIDTH - 1 - k, 0)[8:] for k in range(CONV_WIDTH)]
    xc = cb_ref[...] + sum(cw[k:k + 1, :] * taps[k] for k in range(CONV_WIDTH))
    ga, gx = [], []
    for h in range(LRU_HEADS):
        xh = xc[:, h * LRU_HEAD_DIM:(h + 1) * LRU_HEAD_DIM].astype(MXU_DTYPE)
        ga.append(jnp.dot(xh, wa_ref[h].astype(MXU_DTYPE), preferred_element_type=F32))
        gx.append(jnp.dot(xh, wx_ref[h].astype(MXU_DTYPE), preferred_element_type=F32))
    r = _sigmoid(jnp.concatenate(ga, axis=1) + ba_ref[...])
    i = _sigmoid(jnp.concatenate(gx, axis=1) + bx_ref[...])
    log_a = -LRU_C * r * sp_ref[...]
    a = jnp.exp(log_a)
    a2 = a * a
    mult = jnp.sqrt(-jnp.tanh(log_a) * (a2 + 1.0))
    return taps, xc, r, i, a, a2, mult


def _lru_fwd_call(p, cw, cb, wa, ba, wx, bx, sp):
    B, Tp, _ = p.shape
    W = LRU_WIDTH
    tc = SEQ_BLOCK
    nc = Tp // tc

    def body(pg_ref, prec_ref, prev_ref, cw_ref, cb_ref, wa_ref, ba_ref, wx_ref, bx_ref, sp_ref, y_ref, h_ref, carry_ref):
        first = pl.program_id(1) == 0

        @pl.when(first)
        def _():
            carry_ref[...] = jnp.zeros_like(carry_ref)

        _, xc, r, i, a, a2, mult = _lru_pre(prec_ref, prev_ref, first, cw_ref, cb_ref, wa_ref, ba_ref, wx_ref, bx_ref, sp_ref)
        b = mult * (i * xc)
        carry = carry_ref[0:1, :]
        for t in range(tc // 8):
            h = _scan8(a[8 * t:8 * t + 8], b[8 * t:8 * t + 8], carry, False)
            h_ref[8 * t:8 * t + 8, :] = h
            carry = h[7:8, :]
        carry_ref[...] = jnp.broadcast_to(carry, carry_ref.shape)
        y_ref[...] = h_ref[...] * _gelu_parts(pg_ref[...])[0]

    cur = pl.BlockSpec((None, tc, W), lambda b, j: (b, j, 0))
    rec = pl.BlockSpec((None, tc, W), lambda b, j: (b, j, 1))
    prev = pl.BlockSpec((None, 8, W), lambda b, j: (b, jnp.maximum(j * (tc // 8) - 1, 0), 1))
    vec = pl.BlockSpec((1, W), lambda b, j: (0, 0))
    cws = pl.BlockSpec((CONV_WIDTH, W), lambda b, j: (0, 0))
    wsp = pl.BlockSpec((LRU_HEADS, LRU_HEAD_DIM, LRU_HEAD_DIM), lambda b, j: (0, 0, 0))
    return pl.pallas_call(
        body, name="lru_fwd", grid=(B, nc),
        in_specs=[cur, rec, prev, cws, vec, wsp, vec, wsp, vec, vec],
        out_specs=[cur, cur],
        out_shape=[jax.ShapeDtypeStruct((B, Tp, W), F32), jax.ShapeDtypeStruct((B, Tp, W), F32)],
        scratch_shapes=[pltpu.VMEM((8, W), F32)],
        compiler_params=pltpu.CompilerParams(dimension_semantics=("arbitrary", "arbitrary")),
    )(p, p, p, cw, cb, wa, ba, wx, bx, sp)


def _lru_bwd_call(p, hseq, dy, cw, cb, wa, ba, wx, bx, sp):
    B, Tp, _ = p.shape
    W = LRU_WIDTH
    tc = SEQ_BLOCK
    nc = Tp // tc
    HD = LRU_HEAD_DIM

    def body(pg_ref, prec_ref, prev_ref, h_ref, hprev_ref, dy_ref, cw_ref, cb_ref, wa_ref, ba_ref, wx_ref, bx_ref, sp_ref,
             dpg_ref, dprec_ref, dcw_ref, dcb_ref, dwa_ref, dba_ref, dwx_ref, dbx_ref, dsp_ref,
             gcar_ref, anext_ref, halo_ref, g_ref):
        j = pl.program_id(1)
        first = j == nc - 1
        last = j == 0

        @pl.when(jnp.logical_and(pl.program_id(0) == 0, last))
        def _():
            for ref in (dcw_ref, dcb_ref, dwa_ref, dba_ref, dwx_ref, dbx_ref, dsp_ref):
                ref[...] = jnp.zeros_like(ref)

        @pl.when(last)
        def _():
            gcar_ref[...] = jnp.zeros_like(gcar_ref)
            anext_ref[...] = jnp.zeros_like(anext_ref)
            halo_ref[...] = jnp.zeros_like(halo_ref)

        taps, xc, r, i, a, a2, mult = _lru_pre(prec_ref, prev_ref, first, cw_ref, cb_ref, wa_ref, ba_ref, wx_ref, bx_ref, sp_ref)
        row = lax.broadcasted_iota(jnp.int32, (tc, W), 0)
        gelu, dgelu = _gelu_parts(pg_ref[...])
        dy = dy_ref[...]
        hcur = h_ref[...]
        dpg_ref[...] = dy * hcur * dgelu
        dh = dy * gelu
        a_next = jnp.where(row == tc - 1, anext_ref[0:1, :], pltpu.roll(a, tc - 1, 0))
        carry = gcar_ref[0:1, :]
        for t in reversed(range(tc // 8)):
            g = _scan8(a_next[8 * t:8 * t + 8], dh[8 * t:8 * t + 8], carry, True)
            g_ref[8 * t:8 * t + 8, :] = g
            carry = g[0:1, :]
        gcar_ref[...] = jnp.broadcast_to(carry, gcar_ref.shape)
        anext_ref[...] = jnp.broadcast_to(a[0:1, :], anext_ref.shape)
        G = g_ref[...]
        h_before = jnp.where(first, 0.0, hprev_ref[7:8, :])
        hprev = jnp.where(row == 0, h_before, pltpu.roll(hcur, 1, 0))
        d_a = G * hprev
        gx_ = G * xc
        d_mult = gx_ * i
        d_i = gx_ * mult
        dxc = G * (mult * i)
        d_la = d_a * a - d_mult * (a2 / mult)
        sp = sp_ref[...]
        d_r = d_la * (-LRU_C * sp)
        dsp_ref[...] += jnp.sum(d_la * (-LRU_C * r), axis=0, keepdims=True)
        dga = d_r * r * (1.0 - r)
        dgx = d_i * i * (1.0 - i)
        dba_ref[...] += jnp.sum(dga, axis=0, keepdims=True)
        dbx_ref[...] += jnp.sum(dgx, axis=0, keepdims=True)
        back = []
        for h in range(LRU_HEADS):
            sl = slice(h * HD, (h + 1) * HD)
            xh = xc[:, sl].astype(MXU_DTYPE)
            ah = dga[:, sl].astype(MXU_DTYPE)
            bh = dgx[:, sl].astype(MXU_DTYPE)
            tn = (((0,), (0,)), ((), ()))
            nt = (((1,), (1,)), ((), ()))
            dwa_ref[h] += lax.dot_general(xh, ah, tn, preferred_element_type=F32)
            dwx_ref[h] += lax.dot_general(xh, bh, tn, preferred_element_type=F32)
            back.append(lax.dot_general(ah, wa_ref[h].astype(MXU_DTYPE), nt, preferred_element_type=F32)
                        + lax.dot_general(bh, wx_ref[h].astype(MXU_DTYPE), nt, preferred_element_type=F32))
        dxc = dxc + jnp.concatenate(back, axis=1)
        dcb_ref[...] += jnp.sum(dxc, axis=0, keepdims=True)
        for k in range(CONV_WIDTH):
            dcw_ref[k:k + 1, :] += jnp.sum(dxc * taps[k], axis=0, keepdims=True)
        ext = jnp.concatenate([dxc, halo_ref[...]], axis=0)
        cw = cw_ref[...]
        acc = cw[CONV_WIDTH - 1:CONV_WIDTH, :] * dxc
        for k in range(CONV_WIDTH - 1):
            s = CONV_WIDTH - 1 - k
            acc = acc + cw[k:k + 1, :] * pltpu.roll(ext, tc + 8 - s, 0)[:tc]
        dprec_ref[...] = acc
        halo_ref[...] = dxc[0:8, :]

    rev = lambda j: nc - 1 - j
    cur = pl.BlockSpec((None, tc, W), lambda b, j: (b, rev(j), 0))
    rec = pl.BlockSpec((None, tc, W), lambda b, j: (b, rev(j), 1))
    prev = pl.BlockSpec((None, 8, W), lambda b, j: (b, jnp.maximum(rev(j) * (tc // 8) - 1, 0), 0))
    prev_rec = pl.BlockSpec((None, 8, W), lambda b, j: (b, jnp.maximum(rev(j) * (tc // 8) - 1, 0), 1))
    vec = pl.BlockSpec((1, W), lambda b, j: (0, 0))
    cws = pl.BlockSpec((CONV_WIDTH, W), lambda b, j: (0, 0))
    wsp = pl.BlockSpec((LRU_HEADS, HD, HD), lambda b, j: (0, 0, 0))
    seq = jax.ShapeDtypeStruct((B, Tp, W), F32)
    vs = jax.ShapeDtypeStruct((1, W), F32)
    ws = jax.ShapeDtypeStruct((LRU_HEADS, HD, HD), F32)
    return pl.pallas_call(
        body, name="lru_bwd", grid=(B, nc),
        in_specs=[cur, rec, prev_rec, cur, prev, cur, cws, vec, wsp, vec, wsp, vec, vec],
        out_specs=[cur, cur, cws, vec, wsp, vec, wsp, vec, vec],
        out_shape=[seq, seq, jax.ShapeDtypeStruct((CONV_WIDTH, W), F32), vs, ws, vs, ws, vs, vs],
        scratch_shapes=[pltpu.VMEM((8, W), F32), pltpu.VMEM((8, W), F32), pltpu.VMEM((8, W), F32), pltpu.VMEM((tc, W), F32)],
        compiler_params=pltpu.CompilerParams(dimension_semantics=("arbitrary", "arbitrary")),
    )(p, p, p, hseq, hseq, dy, cw, cb, wa, ba, wx, bx, sp)


_Q_BLOCK = 2 * LRU_WIDTH // MLA_Q_RANK
_KV_BLOCK = (2 * LRU_WIDTH + MLA_Q_RANK) // MLA_KV_RANK
_KPE_START = 2 * LRU_WIDTH + MLA_Q_RANK + MLA_KV_RANK


@jax.custom_vjp
def even_front(p, cw, cb, wa, ba, wx, bx, sp, gq, gkv):
    return _even_front_fwd(p, cw, cb, wa, ba, wx, bx, sp, gq, gkv)[0]


def _even_front_fwd(p, cw, cb, wa, ba, wx, bx, sp, gq, gkv):
    B, Tp, W = p.shape
    p2d = p.reshape(B * Tp, W)
    y, hseq = _lru_fwd_call(p, cw, cb, wa, ba, wx, bx, sp)
    qn = _rms_fwd_call(p2d, gq, "q_norm_fwd", _Q_BLOCK)
    kvn = _rms_fwd_call(p2d, gkv, "kv_norm_fwd", _KV_BLOCK)
    return (y, qn, kvn, p2d[:, _KPE_START:]), (p, hseq, cw, cb, wa, ba, wx, bx, sp, gq, gkv)


def _even_front_bwd(res, cts):
    p, hseq, cw, cb, wa, ba, wx, bx, sp, gq, gkv = res
    dy, dqn, dkvn, dkpe = cts
    B, Tp, W = p.shape
    p2d = p.reshape(B * Tp, W)
    dpg, dprec, dcw, dcb, dwa, dba, dwx, dbx, dsp = _lru_bwd_call(p, hseq, dy, cw, cb, wa, ba, wx, bx, sp)
    dpq, dgq = _rms_bwd_call(p2d, gq, dqn, "q_norm_bwd", _Q_BLOCK)
    dpkv, dgkv = _rms_bwd_call(p2d, gkv, dkvn, "kv_norm_bwd", _KV_BLOCK)
    dp = jnp.concatenate([dpg.reshape(B * Tp, -1), dprec.reshape(B * Tp, -1), dpq, dpkv, dkpe], axis=1).reshape(B, Tp, W)
    return dp, dcw, dcb, dwa, dba, dwx, dbx, dsp, dgq.reshape(gq.shape), dgkv.reshape(gkv.shape)


even_front.defvjp(_even_front_fwd, _even_front_bwd)


def _rope_tables(pos, half):
    inv = ROPE_BASE ** (-jnp.arange(half, dtype=F32) / half)
    ang = pos.astype(F32)[:, None] * inv[None, :]
    return jnp.cos(ang), jnp.sin(ang)


_NT = (((1,), (1,)), ((), ()))
_TN = (((0,), (0,)), ((), ()))
HEAD_LANES = 128
_MLA_SCALE = (MLA_NOPE + MLA_ROPE) ** -0.5
_LOG2E = math.log2(math.e)


def _mask_diagonal(s, fill):
    L = s.shape[1]
    row = lax.broadcasted_iota(jnp.int32, (SEQ_BLOCK, SEQ_BLOCK), 0)
    col = lax.broadcasted_iota(jnp.int32, (SEQ_BLOCK, SEQ_BLOCK), 1)
    last = jnp.where(col <= row, s[:, L - SEQ_BLOCK:], fill)
    return last if L == SEQ_BLOCK else jnp.concatenate([s[:, :L - SEQ_BLOCK], last], axis=1)


def _mla_rope_tables(pos):
    half = MLA_ROPE // 2
    cos, sin = _rope_tables(pos, half)
    T = pos.shape[0]
    ones, zeros = jnp.ones((T, MLA_NOPE), F32), jnp.zeros((T, MLA_NOPE), F32)
    tail1, tail0 = jnp.ones((T, HEAD_LANES - MLA_NOPE - MLA_ROPE), F32), jnp.zeros((T, HEAD_LANES - MLA_NOPE - MLA_ROPE), F32)
    zh = jnp.zeros((T, half), F32)
    c = jnp.concatenate([ones, cos, cos, tail1], axis=1)
    s_up = jnp.concatenate([zeros, -sin, zh, tail0], axis=1)
    s_down = jnp.concatenate([zeros, zh, sin, tail0], axis=1)
    return c, s_up, s_down


def _rope_lanes(x, c, s_up, s_down):
    half = MLA_ROPE // 2
    return x * c + pltpu.roll(x, HEAD_LANES - half, 1) * s_up + pltpu.roll(x, half, 1) * s_down


def _unrope_lanes(d, c, s_up, s_down):
    half = MLA_ROPE // 2
    return d * c + pltpu.roll(d * s_up, half, 1) + pltpu.roll(d * s_down, HEAD_LANES - half, 1)


def _mla_operands(q_ref, kv_ref, kpe_ref, c, s_up, s_down):
    lane = lax.broadcasted_iota(jnp.int32, kv_ref.shape, 1)
    qr = (_rope_lanes(q_ref[...].astype(F32), c, s_up, s_down) * (_MLA_SCALE * _LOG2E)).astype(MXU_DTYPE)
    kr = jnp.where(lane < MLA_NOPE, kv_ref[...].astype(F32), _rope_lanes(kpe_ref[...], c, s_up, s_down)).astype(MXU_DTYPE)
    return qr, kr, lane


def _mla_specs(Tp):
    head = pl.BlockSpec((None, Tp, HEAD_LANES), lambda b, h: (b, 0, h))
    shared = pl.BlockSpec((None, Tp, HEAD_LANES), lambda b, h: (b, 0, 0))
    tab = pl.BlockSpec((Tp, HEAD_LANES), lambda b, h: (0, 0))
    lse = pl.BlockSpec((None, None, Tp, 1), lambda b, h: (b, h, 0, 0))
    return head, shared, tab, lse


def _attn_fwd_call(q, kv, kpe, tabs):
    B, Tp, _ = q.shape
    nq = Tp // SEQ_BLOCK

    def body(q_ref, kv_ref, kpe_ref, c_ref, su_ref, sd_ref, o_ref, lse_ref, qr_ref, kr_ref):
        qr, kr, lane = _mla_operands(q_ref, kv_ref, kpe_ref, c_ref[...], su_ref[...], sd_ref[...])
        qr_ref[...] = qr
        kr_ref[...] = kr
        for qi in range(nq):
            L = (qi + 1) * SEQ_BLOCK
            blk = slice(qi * SEQ_BLOCK, L)
            s = _mask_diagonal(lax.dot_general(qr_ref[blk, :], kr_ref[0:L, :], _NT, preferred_element_type=F32), NEG_INF)
            m = jnp.max(s, axis=-1, keepdims=True)
            p = jnp.exp2(s - m)
            l = jnp.sum(p, axis=-1, keepdims=True)
            o = jnp.dot(p.astype(MXU_DTYPE), kv_ref[0:L, :].astype(MXU_DTYPE), preferred_element_type=F32)
            o_ref[blk, :] = jnp.where(lane[blk, :] >= MLA_NOPE, o / l, 0.0)
            lse_ref[blk, :] = m + jnp.log2(l)

    head, shared, tab, lse = _mla_specs(Tp)
    return pl.pallas_call(
        body, name="mla_attn_fwd", grid=(B, MLA_HEADS), in_specs=[head, head, shared, tab, tab, tab], out_specs=[head, lse],
        out_shape=[jax.ShapeDtypeStruct((B, Tp, MLA_HEADS * HEAD_LANES), F32), jax.ShapeDtypeStruct((B, MLA_HEADS, Tp, 1), F32)],
        scratch_shapes=[pltpu.VMEM((Tp, HEAD_LANES), MXU_DTYPE), pltpu.VMEM((Tp, HEAD_LANES), MXU_DTYPE)],
        compiler_params=pltpu.CompilerParams(dimension_semantics=("parallel", "parallel")),
    )(q, kv, kpe, *tabs)


def _attn_bwd_call(q, kv, kpe, tabs, o, lse, do):
    B, Tp, _ = q.shape
    nq = Tp // SEQ_BLOCK

    def body(q_ref, kv_ref, kpe_ref, c_ref, su_ref, sd_ref, o_ref, lse_ref, do_ref, dq_ref, dkv_ref, dkpe_ref,
             qr_ref, kr_ref, dqa_ref, dka_ref, dva_ref):
        c, s_up, s_down = c_ref[...], su_ref[...], sd_ref[...]
        qr, kr, lane = _mla_operands(q_ref, kv_ref, kpe_ref, c, s_up, s_down)
        qr_ref[...] = qr
        kr_ref[...] = kr
        dka_ref[...] = jnp.zeros_like(dka_ref)
        dva_ref[...] = jnp.zeros_like(dva_ref)
        for qi in range(nq):
            L = (qi + 1) * SEQ_BLOCK
            blk = slice(qi * SEQ_BLOCK, L)
            qb = qr_ref[blk, :]
            do = jnp.where(lane[blk, :] >= MLA_NOPE, do_ref[blk, :], 0.0)
            delta = jnp.sum(do * o_ref[blk, :], axis=-1, keepdims=True)
            s = _mask_diagonal(lax.dot_general(qb, kr_ref[0:L, :], _NT, preferred_element_type=F32), NEG_INF)
            p = jnp.exp2(s - lse_ref[blk, :])
            dob = do.astype(MXU_DTYPE)
            dva_ref[0:L, :] += lax.dot_general(p.astype(MXU_DTYPE), dob, _TN, preferred_element_type=F32)
            dp = lax.dot_general(dob, kv_ref[0:L, :].astype(MXU_DTYPE), _NT, preferred_element_type=F32)
            ds = (p * (dp - delta)).astype(MXU_DTYPE)
            dqa_ref[blk, :] = jnp.dot(ds, kr_ref[0:L, :], preferred_element_type=F32)
            dka_ref[0:L, :] += lax.dot_general(ds, qb, _TN, preferred_element_type=F32)
        dq_ref[...] = _unrope_lanes(dqa_ref[...] * _MLA_SCALE, c, s_up, s_down).astype(dq_ref.dtype)
        dk = dka_ref[...] * (1.0 / _LOG2E)
        dkv_ref[...] = jnp.where(lane < MLA_NOPE, dk, dva_ref[...]).astype(dkv_ref.dtype)
        dkpe = jnp.where(lane >= MLA_NOPE, _unrope_lanes(dk, c, s_up, s_down), 0.0)

        @pl.when(pl.program_id(1) == 0)
        def _():
            dkpe_ref[...] = dkpe

        @pl.when(pl.program_id(1) > 0)
        def _():
            dkpe_ref[...] += dkpe

    head, shared, tab, lse_spec = _mla_specs(Tp)
    wide = jax.ShapeDtypeStruct((B, Tp, MLA_HEADS * HEAD_LANES), q.dtype)
    acc = pltpu.VMEM((Tp, HEAD_LANES), F32)
    return pl.pallas_call(
        body, name="mla_attn_bwd", grid=(B, MLA_HEADS),
        in_specs=[head, head, shared, tab, tab, tab, head, lse_spec, head], out_specs=[head, head, shared],
        out_shape=[wide, wide, jax.ShapeDtypeStruct((B, Tp, HEAD_LANES), F32)],
        scratch_shapes=[pltpu.VMEM((Tp, HEAD_LANES), MXU_DTYPE), pltpu.VMEM((Tp, HEAD_LANES), MXU_DTYPE), acc, acc, acc],
        compiler_params=pltpu.CompilerParams(dimension_semantics=("parallel", "arbitrary")),
    )(q, kv, kpe, *tabs, o, lse, do)


@jax.custom_vjp
def mla_attention(q, kv, kpe, tabs):
    return _attn_fwd_call(q, kv, kpe, tabs)[0]


def _mla_attention_fwd(q, kv, kpe, tabs):
    o, lse = _attn_fwd_call(q, kv, kpe, tabs)
    return o, (q, kv, kpe, tabs, o, lse)


def _mla_attention_bwd(res, do):
    q, kv, kpe, tabs, o, lse = res
    dq, dkv, dkpe = _attn_bwd_call(q, kv, kpe, tabs, o, lse, do)
    return dq, dkv, dkpe, None


mla_attention.defvjp(_mla_attention_fwd, _mla_attention_bwd)


def _rope_halves(x, cos, sin):
    half = x.shape[1] // 2
    x1, x2 = x[:, :half], x[:, half:]
    return jnp.concatenate([x1 * cos - x2 * sin, x1 * sin + x2 * cos], axis=1)


def _unrope_halves(d, cos, sin):
    half = d.shape[1] // 2
    d1, d2 = d[:, :half], d[:, half:]
    return jnp.concatenate([d1 * cos + d2 * sin, d2 * cos - d1 * sin], axis=1)


_RET_K_SCALE = RET_QK_DIM ** -0.5
_RET_Q_BLOCKS = RET_HEADS
_RET_V_BLOCK0 = 2 * RET_HEADS * RET_QK_DIM // RET_V_DIM
_RET_G_BLOCK0 = _RET_V_BLOCK0 + RET_HEADS


def _ret_specs(Tp):
    q = pl.BlockSpec((None, Tp, RET_QK_DIM), lambda b, h: (b, 0, h))
    k = pl.BlockSpec((None, Tp, RET_QK_DIM), lambda b, h: (b, 0, _RET_Q_BLOCKS + h))
    v = pl.BlockSpec((None, Tp, RET_V_DIM), lambda b, h: (b, 0, _RET_V_BLOCK0 + h))
    tab = pl.BlockSpec((Tp, RET_QK_DIM // 2), lambda b, h: (0, 0))
    lg = pl.BlockSpec((None, 1, 1), lambda b, h: (h, 0, 0))
    return q, k, v, tab, lg


def _ret_operands(q_ref, k_ref, cos, sin, lg):
    t = lax.broadcasted_iota(jnp.int32, (q_ref.shape[0], 1), 0).astype(F32)
    grow, shrink = jnp.exp(-lg * t), jnp.exp(lg * t)
    qs = (_rope_halves(q_ref[...].astype(F32), cos, sin) * shrink).astype(MXU_DTYPE)
    ks = (_rope_halves(k_ref[...].astype(F32), cos, sin) * (grow * _RET_K_SCALE)).astype(MXU_DTYPE)
    return qs, ks, shrink, grow * _RET_K_SCALE


def _ret_core_fwd_call(p, cos, sin, lg):
    B, Tp, _ = p.shape
    nq = Tp // SEQ_BLOCK

    def body(q_ref, k_ref, v_ref, cos_ref, sin_ref, lg_ref, o_ref, qs_ref, ks_ref):
        qs_ref[...], ks_ref[...], _, _ = _ret_operands(q_ref, k_ref, cos_ref[...], sin_ref[...], lg_ref[...])
        for qi in range(nq):
            L = (qi + 1) * SEQ_BLOCK
            blk = slice(qi * SEQ_BLOCK, L)
            s = _mask_diagonal(lax.dot_general(qs_ref[blk, :], ks_ref[0:L, :], _NT, preferred_element_type=F32), 0.0)
            o_ref[blk, :] = jnp.dot(s.astype(MXU_DTYPE), v_ref[0:L, :].astype(MXU_DTYPE), preferred_element_type=F32)

    q, k, v, tab, lgs = _ret_specs(Tp)
    return pl.pallas_call(
        body, name="retention_fwd", grid=(B, RET_HEADS), in_specs=[q, k, v, tab, tab, lgs],
        out_specs=pl.BlockSpec((None, Tp, RET_V_DIM), lambda b, h: (b, 0, h)),
        out_shape=jax.ShapeDtypeStruct((B, Tp, RET_HEADS * RET_V_DIM), F32),
        scratch_shapes=[pltpu.VMEM((Tp, RET_QK_DIM), MXU_DTYPE), pltpu.VMEM((Tp, RET_QK_DIM), MXU_DTYPE)],
        compiler_params=pltpu.CompilerParams(dimension_semantics=("parallel", "parallel")),
    )(p, p, p, cos, sin, lg)


def _ret_core_bwd_call(p, do, cos, sin, lg):
    B, Tp, _ = p.shape
    nq = Tp // SEQ_BLOCK

    def body(q_ref, k_ref, v_ref, do_ref, cos_ref, sin_ref, lg_ref, dq_ref, dk_ref, dv_ref, qs_ref, ks_ref, dqa_ref, dka_ref, dva_ref):
        cos_, sin_ = cos_ref[...], sin_ref[...]
        qs_ref[...], ks_ref[...], q_scale, k_scale = _ret_operands(q_ref, k_ref, cos_, sin_, lg_ref[...])
        dka_ref[...] = jnp.zeros_like(dka_ref)
        dva_ref[...] = jnp.zeros_like(dva_ref)
        for qi in range(nq):
            L = (qi + 1) * SEQ_BLOCK
            blk = slice(qi * SEQ_BLOCK, L)
            qb = qs_ref[blk, :]
            dob = do_ref[blk, :].astype(MXU_DTYPE)
            s = _mask_diagonal(lax.dot_general(qb, ks_ref[0:L, :], _NT, preferred_element_type=F32), 0.0).astype(MXU_DTYPE)
            dva_ref[0:L, :] += lax.dot_general(s, dob, _TN, preferred_element_type=F32)
            ds = _mask_diagonal(lax.dot_general(dob, v_ref[0:L, :].astype(MXU_DTYPE), _NT, preferred_element_type=F32), 0.0).astype(MXU_DTYPE)
            dqa_ref[blk, :] = jnp.dot(ds, ks_ref[0:L, :], preferred_element_type=F32)
            dka_ref[0:L, :] += lax.dot_general(ds, qb, _TN, preferred_element_type=F32)
        dq_ref[...] = _unrope_halves(dqa_ref[...] * q_scale, cos_, sin_).astype(dq_ref.dtype)
        dk_ref[...] = _unrope_halves(dka_ref[...] * k_scale, cos_, sin_).astype(dk_ref.dtype)
        dv_ref[...] = dva_ref[...].astype(dv_ref.dtype)

    q, k, v, tab, lgs = _ret_specs(Tp)
    qk_out = pl.BlockSpec((None, Tp, RET_QK_DIM), lambda b, h: (b, 0, h))
    v_out = pl.BlockSpec((None, Tp, RET_V_DIM), lambda b, h: (b, 0, h))
    return pl.pallas_call(
        body, name="retention_bwd", grid=(B, RET_HEADS), in_specs=[q, k, v, v_out, tab, tab, lgs],
        out_specs=[qk_out, qk_out, v_out],
        out_shape=[jax.ShapeDtypeStruct((B, Tp, RET_HEADS * RET_QK_DIM), p.dtype), jax.ShapeDtypeStruct((B, Tp, RET_HEADS * RET_QK_DIM), p.dtype),
                   jax.ShapeDtypeStruct((B, Tp, RET_HEADS * RET_V_DIM), p.dtype)],
        scratch_shapes=[pltpu.VMEM((Tp, RET_QK_DIM), MXU_DTYPE), pltpu.VMEM((Tp, RET_QK_DIM), MXU_DTYPE),
                        pltpu.VMEM((Tp, RET_QK_DIM), F32), pltpu.VMEM((Tp, RET_QK_DIM), F32), pltpu.VMEM((Tp, RET_V_DIM), F32)],
        compiler_params=pltpu.CompilerParams(dimension_semantics=("parallel", "parallel")),
    )(p, p, p, do, cos, sin, lg)


def _ret_gate_specs(M):
    tm = _pick(M, 1088, 8)
    head = pl.BlockSpec((tm, RET_V_DIM), lambda i, h: (i, h))
    gate = pl.BlockSpec((tm, RET_V_DIM), lambda i, h: (i, _RET_G_BLOCK0 + h))
    return tm, head, gate


def _ret_gate_fwd_call(o, p2d):
    M = o.shape[0]
    tm, head, gate = _ret_gate_specs(M)

    def body(o_ref, g_ref, y_ref):
        ov = o_ref[...]
        gv = g_ref[...].astype(F32)
        rstd = lax.rsqrt(jnp.mean(ov * ov, axis=-1, keepdims=True) + EPS)
        y_ref[...] = (gv * _sigmoid(gv)) * (ov * rstd)

    return pl.pallas_call(
        body, name="retention_gate_fwd", grid=(M // tm, RET_HEADS), in_specs=[head, gate], out_specs=head,
        out_shape=jax.ShapeDtypeStruct(o.shape, F32),
        compiler_params=pltpu.CompilerParams(dimension_semantics=("parallel", "parallel")),
    )(o, p2d)


def _ret_gate_bwd_call(o, p2d, dy):
    M = o.shape[0]
    tm, head, gate = _ret_gate_specs(M)

    def body(o_ref, g_ref, dy_ref, do_ref, dg_ref):
        ov = o_ref[...]
        gv = g_ref[...].astype(F32)
        dy = dy_ref[...]
        rstd = lax.rsqrt(jnp.mean(ov * ov, axis=-1, keepdims=True) + EPS)
        on = ov * rstd
        sg = _sigmoid(gv)
        dg_ref[...] = (dy * on * (sg * (1.0 + gv * (1.0 - sg)))).astype(dg_ref.dtype)
        don = dy * (gv * sg)
        do_ref[...] = (rstd * (don - on * jnp.mean(don * on, axis=-1, keepdims=True))).astype(do_ref.dtype)

    shp = jax.ShapeDtypeStruct(o.shape, p2d.dtype)
    return pl.pallas_call(
        body, name="retention_gate_bwd", grid=(M // tm, RET_HEADS), in_specs=[head, gate, head], out_specs=[head, head],
        out_shape=[shp, shp],
        compiler_params=pltpu.CompilerParams(dimension_semantics=("parallel", "parallel")),
    )(o, p2d, dy)


def _log_gamma():
    return jnp.log(1.0 - 2.0 ** (-5.0 - jnp.arange(RET_HEADS, dtype=F32))).reshape(RET_HEADS, 1, 1)


@jax.custom_vjp
def retention_mixer(p, cos, sin):
    B, Tp, W = p.shape
    o = _ret_core_fwd_call(p, cos, sin, _log_gamma())
    return _ret_gate_fwd_call(o.reshape(B * Tp, -1), p.reshape(B * Tp, W))


def _retention_mixer_fwd(p, cos, sin):
    B, Tp, W = p.shape
    o = _ret_core_fwd_call(p, cos, sin, _log_gamma())
    return _ret_gate_fwd_call(o.reshape(B * Tp, -1), p.reshape(B * Tp, W)), (p, o, cos, sin)


def _retention_mixer_bwd(res, dy):
    p, o, cos, sin = res
    B, Tp, W = p.shape
    do, dg = _ret_gate_bwd_call(o.reshape(B * Tp, -1), p.reshape(B * Tp, W), dy)
    dq, dk, dv = _ret_core_bwd_call(p, do.reshape(B, Tp, -1), cos, sin, _log_gamma())
    return jnp.concatenate([dq, dk, dv, dg.reshape(B, Tp, -1)], axis=-1), None, None


retention_mixer.defvjp(_retention_mixer_fwd, _retention_mixer_bwd)


def _heads_to_lanes(w):
    K = w.shape[0]
    w = w.reshape(K, MLA_HEADS, MLA_NOPE + MLA_ROPE)
    return jnp.pad(w, ((0, 0), (0, 0), (0, HEAD_LANES - MLA_NOPE - MLA_ROPE))).reshape(K, MLA_HEADS * HEAD_LANES)


def _out_rows_to_lanes(w):
    N = w.shape[1]
    att = w[LRU_WIDTH:].reshape(MLA_HEADS, MLA_V, N)
    att = jnp.pad(att, ((0, 0), (HEAD_LANES - MLA_V, 0), (0, 0))).reshape(MLA_HEADS * HEAD_LANES, N)
    return jnp.concatenate([w[:LRU_WIDTH], att], axis=0)


def _seq_dims(x):
    B, S, D = x.shape
    T = S + N_META
    Tp = _round_up(T, SEQ_BLOCK)
    return B, S, T, Tp


def _mixer0(diff, w, token):
    x = diff["x"]
    B, S, T, Tp = _seq_dims(x)
    D = x.shape[-1]
    M = B * Tp
    pos = jnp.arange(Tp, dtype=jnp.int32)

    def mm(a, name, act=False, out_dtype=F32, layout=lambda m: m, col_shards=1):
        return matmul(a, layout(w[name]), layout(diff[name]), act, name, out_dtype, col_shards)

    meta = jnp.broadcast_to(diff["meta_tokens"][None], (B, N_META, D))
    h = jnp.concatenate([meta, x + token, jnp.zeros((B, Tp - T, D), F32)], axis=1).reshape(M, D)
    p = mm(h, "ev_w_in")
    sp = jax.nn.softplus(-diff["ev_lru_lambda"]).reshape(1, LRU_WIDTH)
    y_rec, qn, kvn, p_kpe = even_front(
        p.reshape(B, Tp, -1), diff["ev_conv_w"].reshape(CONV_WIDTH, LRU_WIDTH), diff["ev_conv_b"].reshape(1, LRU_WIDTH),
        diff["ev_w_rg_a"].reshape(LRU_HEADS, LRU_HEAD_DIM, LRU_HEAD_DIM), diff["ev_b_rg_a"].reshape(1, LRU_WIDTH),
        diff["ev_w_rg_x"].reshape(LRU_HEADS, LRU_HEAD_DIM, LRU_HEAD_DIM), diff["ev_b_rg_x"].reshape(1, LRU_WIDTH),
        sp, diff["ev_q_norm_g"].reshape(-1), diff["ev_kv_norm_g"].reshape(-1))
    y_rec = y_rec.reshape(M, LRU_WIDTH)
    q = mm(qn, "ev_w_uq", out_dtype=MXU_DTYPE, layout=_heads_to_lanes).reshape(B, Tp, -1)
    kv = mm(kvn, "ev_w_ukv", out_dtype=MXU_DTYPE).reshape(B, Tp, -1)
    kpe = jnp.pad(p_kpe.reshape(B, Tp, MLA_ROPE), ((0, 0), (0, 0), (MLA_NOPE, HEAD_LANES - MLA_NOPE - MLA_ROPE)))
    y_att = mla_attention(q, kv, kpe, _mla_rope_tables(pos)).reshape(M, -1)
    mix = mm(jnp.concatenate([y_rec, y_att], axis=-1), "ev_w_out", layout=_out_rows_to_lanes)
    return deepnorm(h, mix, diff["ln_mix_g"], diff["ln_mix_b"], "ln_mix0")


def _mlp0(diff, h, w):
    f = mlp(h, w["mlp_w1_0"], w["mlp_w2_0"], diff["mlp_w1_0"], diff["mlp_w2_0"], "mlp0")
    return deepnorm(h, f, diff["ln_mlp_g"], diff["ln_mlp_b"], "ln_mlp0")


def _layer1_loss(diff, h, w, tgt):
    B, S, T, Tp = _seq_dims(tgt)
    D = tgt.shape[-1]
    pos = jnp.arange(Tp, dtype=jnp.int32)

    def mm(a, name, out_dtype=F32, col_shards=1):
        return matmul(a, w[name], diff[name], False, name, out_dtype, col_shards)

    p = mm(h, "od_w_in", out_dtype=MXU_DTYPE, col_shards=N_CHIPS)
    cos, sin = _rope_tables(pos, RET_QK_DIM // 2)
    mix = mm(retention_mixer(p.reshape(B, Tp, -1), cos, sin), "od_w_out")
    h = deepnorm(h, mix, diff["ln_mix_g"], diff["ln_mix_b"], "ln_mix1")
    f = mlp(h, w["mlp_w1_1"], w["mlp_w2_1"], diff["mlp_w1_1"], diff["mlp_w2_1"], "mlp1")
    h = deepnorm(h, f, diff["ln_mlp_g"], diff["ln_mlp_b"], "ln_mlp1")
    y = h.reshape(B, Tp, D)[:, N_META:T].reshape(B * S, D)
    return loss_head(y, tgt.reshape(B * S, D))


_HBM = pl.BlockSpec(memory_space=pltpu.HBM)


def _place():
    return lax.axis_index("x"), lax.axis_index("y"), lax.axis_index("c")


def _other_chips(x, y):
    return [(1 - x, y), (x, 1 - y), (1 - x, 1 - y)]


def _chunks(rows, sublanes, most):
    for q in range(most, 0, -1):
        if rows % (q * sublanes) == 0:
            return q
    return 1


def _sublanes(dtype):
    return 8 * 4 // jnp.dtype(dtype).itemsize


def _allgather_chips(buf, name):
    R, C = buf.shape
    Rh = R // 2
    Q = _chunks(Rh, _sublanes(buf.dtype), 4)
    ch = Rh // Q

    def body(x_ref, out_ref, send_sems, recv_sems):
        x, y, c = _place()
        sibling = (x, y, 1 - c)
        chips = _other_chips(x, y)

        def piece(cx, cy, hc, q):
            return out_ref.at[2 * cx + cy, pl.ds(hc * Rh + q * ch, ch), :]

        def copy(k, src, dst, to):
            return pltpu.make_async_remote_copy(src_ref=src, dst_ref=dst, send_sem=send_sems.at[k], recv_sem=recv_sems.at[k],
                                                device_id=to, device_id_type=MESH)

        first = [copy(j * Q + q, x_ref.at[pl.ds(c * Rh + q * ch, ch), :], piece(x, y, c, q), (*chip, c))
                 for q in range(Q) for j, chip in enumerate(chips)]
        for cp in first:
            cp.start()
        passed = []
        for q in range(Q):
            for j, chip in enumerate(chips):
                landed = piece(*chip, c, q)
                copy(j * Q + q, landed, landed, sibling).wait_recv()
                fwd = copy(3 * Q + j * Q + q, landed, landed, sibling)
                fwd.start()
                passed.append(fwd)
        for q in range(Q):
            for j, chip in enumerate(chips):
                theirs = piece(*chip, 1 - c, q)
                copy(3 * Q + j * Q + q, theirs, theirs, sibling).wait_recv()
        for cp in first + passed:
            cp.wait_send()

    return pl.pallas_call(
        body, name=name, in_specs=[_HBM], out_specs=_HBM,
        out_shape=jax.ShapeDtypeStruct((N_CHIPS, R, C), buf.dtype),
        scratch_shapes=[pltpu.SemaphoreType.DMA((6 * Q,)), pltpu.SemaphoreType.DMA((6 * Q,))],
    )(buf)


def _with_own(gathered, own):
    my = 2 * lax.axis_index("x") + lax.axis_index("y")
    return lax.dynamic_update_slice(gathered, own[None], (my, 0, 0))


def _sibling_exchange(ps, name):
    n = len(ps)

    def body(*refs):
        p_refs, out_refs, (send_sems, recv_sems) = refs[:n], refs[n:2 * n], refs[2 * n:]
        x, y, c = _place()
        copies = [pltpu.make_async_remote_copy(src_ref=p_ref.at[j, 1 - c], dst_ref=out_ref.at[j], send_sem=send_sems.at[N_CHIPS * i + j],
                                               recv_sem=recv_sems.at[N_CHIPS * i + j], device_id=(x, y, 1 - c), device_id_type=MESH)
                  for i, (p_ref, out_ref) in enumerate(zip(p_refs, out_refs)) for j in range(N_CHIPS)]
        for cp in copies:
            cp.start()
        for cp in copies:
            cp.wait()

    return pl.pallas_call(
        body, name=name, in_specs=[_HBM] * n, out_specs=[_HBM] * n,
        out_shape=[jax.ShapeDtypeStruct((N_CHIPS,) + p.shape[2:], p.dtype) for p in ps],
        scratch_shapes=[pltpu.SemaphoreType.DMA((N_CHIPS * n,)), pltpu.SemaphoreType.DMA((N_CHIPS * n,))],
    )(*ps)


def _chip_scatter(ss, name):
    n = len(ss)

    def body(*refs):
        s_refs, t_refs, (send_sems, recv_sems) = refs[:n], refs[n:2 * n], refs[2 * n:]
        x, y, c = _place()
        copies = [pltpu.make_async_remote_copy(src_ref=s_ref.at[j + 1], dst_ref=t_ref.at[j], send_sem=send_sems.at[3 * i + j],
                                               recv_sem=recv_sems.at[3 * i + j], device_id=(cx, cy, c), device_id_type=MESH)
                  for i, (s_ref, t_ref) in enumerate(zip(s_refs, t_refs)) for j, (cx, cy) in enumerate(_other_chips(x, y))]
        for cp in copies:
            cp.start()
        for cp in copies:
            cp.wait()

    return pl.pallas_call(
        body, name=name, in_specs=[_HBM] * n, out_specs=[_HBM] * n,
        out_shape=[jax.ShapeDtypeStruct((3,) + s.shape[1:], s.dtype) for s in ss],
        scratch_shapes=[pltpu.SemaphoreType.DMA((3 * n,)), pltpu.SemaphoreType.DMA((3 * n,))],
    )(*ss)


def _sibling_gather(fs, name):
    n = len(fs)

    def body(*refs):
        out_refs, (send_sems, recv_sems) = refs[n:2 * n], refs[2 * n:]
        x, y, c = _place()
        copies = [pltpu.make_async_remote_copy(src_ref=out_ref.at[c], dst_ref=out_ref.at[c], send_sem=send_sems.at[i], recv_sem=recv_sems.at[i],
                                               device_id=(x, y, 1 - c), device_id_type=MESH) for i, out_ref in enumerate(out_refs)]
        for cp in copies:
            cp.start()
        for cp in copies:
            cp.wait()

    return pl.pallas_call(
        body, name=name, in_specs=[_HBM] * n, out_specs=[_HBM] * n,
        out_shape=[jax.ShapeDtypeStruct(f.shape, f.dtype) for f in fs], input_output_aliases={i: i for i in range(n)},
        scratch_shapes=[pltpu.SemaphoreType.DMA((n,)), pltpu.SemaphoreType.DMA((n,))],
    )(*fs)


def _axis_scalar(name):
    return lax.axis_index(name).astype(jnp.int32).reshape(1)


def _add_own_half(p, got, out_dtype, name):
    n, _, R, C = p.shape
    tr = _pick(R, 512, 16)

    def body(x_ref, y_ref, c_ref, p_ref, g_ref, o_ref):
        o_ref[...] = (p_ref[...] + g_ref[...]).astype(out_dtype)

    def chip(r, x_ref, y_ref):
        return 2 * (x_ref[0] ^ (r & 1)) + (y_ref[0] ^ (r >> 1))

    grid_spec = pltpu.PrefetchScalarGridSpec(
        num_scalar_prefetch=3, grid=(n, R // tr),
        in_specs=[pl.BlockSpec((None, None, tr, C), lambda r, i, x_ref, y_ref, c_ref: (chip(r, x_ref, y_ref), c_ref[0], i, 0)),
                  pl.BlockSpec((None, tr, C), lambda r, i, x_ref, y_ref, c_ref: (chip(r, x_ref, y_ref), i, 0))],
        out_specs=pl.BlockSpec((None, tr, C), lambda r, i, x_ref, y_ref, c_ref: (r, i, 0)))
    return pl.pallas_call(body, name=name, grid_spec=grid_spec, out_shape=jax.ShapeDtypeStruct((n, R, C), out_dtype),
                          compiler_params=pltpu.CompilerParams(dimension_semantics=("parallel", "parallel")))(
        _axis_scalar("x"), _axis_scalar("y"), _axis_scalar("c"), p, got)


def _sum_partials(s, t, name):
    _, R, C = s.shape
    tr = _pick(R, 512, 16)

    def body(c_ref, s_ref, t_ref, o_ref):
        acc = s_ref[...].astype(F32)
        for j in range(3):
            acc = acc + t_ref[j].astype(F32)
        o_ref[...] = acc

    grid_spec = pltpu.PrefetchScalarGridSpec(
        num_scalar_prefetch=1, grid=(R // tr,),
        in_specs=[pl.BlockSpec((None, tr, C), lambda i, c_ref: (0, i, 0)), pl.BlockSpec((3, tr, C), lambda i, c_ref: (0, i, 0))],
        out_specs=pl.BlockSpec((None, tr, C), lambda i, c_ref: (c_ref[0], i, 0)))
    return pl.pallas_call(body, name=name, grid_spec=grid_spec, out_shape=jax.ShapeDtypeStruct((2, R, C), F32),
                          compiler_params=pltpu.CompilerParams(dimension_semantics=("parallel",)))(_axis_scalar("c"), s, t)


def _sibling_reduce(ps, wire_dtypes, tag):
    got = _sibling_exchange(ps, "grad_sibling_exchange_" + tag)
    return [_add_own_half(p, g, dt, "grad_sibling_add_%s%d" % (tag, i)) for i, (p, g, dt) in enumerate(zip(ps, got, wire_dtypes))]


def _sum_and_share(ss, ts, tag):
    fs = [_sum_partials(s, t, "grad_chip_sum_%s%d" % (tag, i)) for i, (s, t) in enumerate(zip(ss, ts))]
    return _sibling_gather(fs, "grad_sibling_gather_" + tag)


_SEM = pl.BlockSpec(memory_space=pltpu.SEMAPHORE)
_ANY = pl.BlockSpec(memory_space=pl.ANY)
_EFFECT = pltpu.SideEffectType.DATAFLOW_SIDE_EFFECTING


def _in_hbm(a):
    return pltpu.with_memory_space_constraint(a, pltpu.HBM)


def _half_copies(x_ref, land_ref, send_sems, recv_sems, Rh, arriving):
    x, y, c = _place()
    rows = pl.ds(c * Rh, Rh)
    return [pltpu.make_async_remote_copy(src_ref=x_ref.at[rows, :], dst_ref=land_ref.at[2 * cx + cy if arriving else 2 * x + y, rows, :],
                                         send_sem=send_sems.at[j], recv_sem=recv_sems.at[j], device_id=(cx, cy, c), device_id_type=MESH)
            for j, (cx, cy) in enumerate(_other_chips(x, y))]


def _allgather_start(buf, name):
    R, C = buf.shape

    def body(x_ref, land_ref, send_sems, recv_sems, x_thru, land_thru, token):
        for cp in _half_copies(x_ref, land_ref, send_sems, recv_sems, R // 2, False):
            cp.start()
        token[...] = jnp.zeros_like(token)

    send_sems, recv_sems, x_thru, land_thru, token = pl.pallas_call(
        body, name=name,
        out_shape=(pltpu.SemaphoreType.DMA((3,)), pltpu.SemaphoreType.DMA((3,)), pltpu.HBM(buf.shape, buf.dtype),
                   pltpu.HBM((N_CHIPS, R, C), buf.dtype), jax.ShapeDtypeStruct((8, 128), F32)),
        in_specs=(_HBM, _HBM), out_specs=(_SEM, _SEM, _HBM, _HBM, pl.BlockSpec(memory_space=pltpu.VMEM)),
        input_output_aliases={0: 2, 1: 3}, compiler_params=pltpu.CompilerParams(has_side_effects=_EFFECT),
    )(_in_hbm(buf), _in_hbm(lax.empty((N_CHIPS, R, C), buf.dtype)))
    return (send_sems, recv_sems, x_thru, land_thru), token[0, 0]


def _allgather_wait(pending, after, name):
    send_sems, recv_sems, x_thru, land_thru = pending
    R = x_thru.shape[0]

    def body(x_ref, land_ref, send_sems, recv_sems, after_ref, x_dead, got_ref):
        for cp in _half_copies(x_ref, land_ref, send_sems, recv_sems, R // 2, False):
            cp.wait_send()
        for cp in _half_copies(x_ref, land_ref, send_sems, recv_sems, R // 2, True):
            cp.wait_recv()

    return pl.pallas_call(
        body, name=name, out_shape=(pltpu.HBM(x_thru.shape, x_thru.dtype), pltpu.HBM(land_thru.shape, land_thru.dtype)),
        in_specs=(_HBM, _HBM, _SEM, _SEM, _ANY), out_specs=(_HBM, _HBM), input_output_aliases={0: 0, 1: 1},
        compiler_params=pltpu.CompilerParams(has_side_effects=_EFFECT),
    )(x_thru, land_thru, send_sems, recv_sems, after)[1]


def _sibling_forward(land, name):
    _, R, C = land.shape
    Rh = R // 2
    Q = _chunks(Rh, _sublanes(land.dtype), 4)
    ch = Rh // Q

    def body(in_ref, out_ref, send_sems, recv_sems):
        x, y, c = _place()
        copies = []
        for j, (cx, cy) in enumerate(_other_chips(x, y)):
            for q in range(Q):
                rows = out_ref.at[2 * cx + cy, pl.ds(c * Rh + q * ch, ch), :]
                copies.append(pltpu.make_async_remote_copy(src_ref=rows, dst_ref=rows, send_sem=send_sems.at[j * Q + q],
                                                           recv_sem=recv_sems.at[j * Q + q], device_id=(x, y, 1 - c), device_id_type=MESH))
        for cp in copies:
            cp.start()
        for cp in copies:
            cp.wait_send()
        for j, (cx, cy) in enumerate(_other_chips(x, y)):
            for q in range(Q):
                rows = out_ref.at[2 * cx + cy, pl.ds((1 - c) * Rh + q * ch, ch), :]
                pltpu.make_async_remote_copy(src_ref=rows, dst_ref=rows, send_sem=send_sems.at[j * Q + q], recv_sem=recv_sems.at[j * Q + q],
                                             device_id=(x, y, 1 - c), device_id_type=MESH).wait_recv()

    return pl.pallas_call(
        body, name=name, in_specs=[_HBM], out_specs=_HBM, out_shape=jax.ShapeDtypeStruct(land.shape, land.dtype),
        input_output_aliases={0: 0},
        scratch_shapes=[pltpu.SemaphoreType.DMA((3 * Q,)), pltpu.SemaphoreType.DMA((3 * Q,))],
    )(land)


def _scatter_copies(s_refs, t_refs, send_sems, recv_sems):
    x, y, c = _place()
    return [pltpu.make_async_remote_copy(src_ref=s_ref.at[j + 1], dst_ref=t_ref.at[j], send_sem=send_sems.at[3 * i + j],
                                         recv_sem=recv_sems.at[3 * i + j], device_id=(cx, cy, c), device_id_type=MESH)
            for i, (s_ref, t_ref) in enumerate(zip(s_refs, t_refs)) for j, (cx, cy) in enumerate(_other_chips(x, y))]


def _chip_scatter_start(ss, name):
    n = len(ss)

    def body(*refs):
        s_refs, t_refs, (send_sems, recv_sems), token = refs[:n], refs[n:2 * n], refs[2 * n:2 * n + 2], refs[-1]
        for cp in _scatter_copies(s_refs, t_refs, send_sems, recv_sems):
            cp.start()
        token[...] = jnp.zeros_like(token)

    lands = [lax.empty((3,) + s.shape[1:], s.dtype) for s in ss]
    out = pl.pallas_call(
        body, name=name,
        out_shape=(pltpu.SemaphoreType.DMA((3 * n,)), pltpu.SemaphoreType.DMA((3 * n,)), *[pltpu.HBM(a.shape, a.dtype) for a in ss + lands],
                   jax.ShapeDtypeStruct((8, 128), F32)),
        in_specs=[_HBM] * (2 * n), out_specs=(_SEM, _SEM, *[_HBM] * (2 * n), pl.BlockSpec(memory_space=pltpu.VMEM)),
        input_output_aliases={i: 2 + i for i in range(2 * n)}, compiler_params=pltpu.CompilerParams(has_side_effects=_EFFECT),
    )(*[_in_hbm(a) for a in ss + lands])
    return (out[0], out[1], list(out[2:2 + n]), list(out[2 + n:2 + 2 * n])), out[-1][0, 0]


def _chip_scatter_wait(pending, after, name):
    send_sems, recv_sems, ss, lands = pending
    n = len(ss)

    def body(*refs):
        s_refs, t_refs, send_sems, recv_sems = refs[:n], refs[n:2 * n], refs[2 * n], refs[2 * n + 1]
        for cp in _scatter_copies(s_refs, t_refs, send_sems, recv_sems):
            cp.wait_send()
            cp.wait_recv()

    out = pl.pallas_call(
        body, name=name, out_shape=tuple(pltpu.HBM(a.shape, a.dtype) for a in ss + lands),
        in_specs=[_HBM] * (2 * n) + [_SEM, _SEM, _ANY], out_specs=tuple([_HBM] * (2 * n)),
        input_output_aliases={i: i for i in range(2 * n)}, compiler_params=pltpu.CompilerParams(has_side_effects=_EFFECT),
    )(*ss, *lands, send_sems, recv_sems, after)
    return list(out[:n]), list(out[n:])


def _adamw(w, g, m, v, name):
    R, C = w.shape
    tr = _pick(R, 256, 8)

    def body(w_ref, g_ref, m_ref, v_ref, d_ref, nm_ref, nv_ref):
        g_ = g_ref[...]
        m_ = ADAM_B1 * m_ref[...] + (1.0 - ADAM_B1) * g_
        v_ = ADAM_B2 * v_ref[...] + (1.0 - ADAM_B2) * (g_ * g_)
        m_hat = m_ / (1.0 - ADAM_B1 ** ADAM_STEP)
        v_hat = v_ / (1.0 - ADAM_B2 ** ADAM_STEP)
        d_ref[...] = -ADAM_LR * (m_hat / (jnp.sqrt(v_hat) + ADAM_EPS) + ADAM_WD * w_ref[...])
        nm_ref[...] = m_
        nv_ref[...] = v_

    row = pl.BlockSpec((tr, C), lambda i: (i, 0))
    shp = jax.ShapeDtypeStruct((R, C), F32)
    return pl.pallas_call(body, name=name, grid=(R // tr,), in_specs=[row] * 4, out_specs=[row] * 3, out_shape=[shp] * 3,
                          compiler_params=pltpu.CompilerParams(dimension_semantics=("parallel",)))(w, g, m, v)


BIG_SPECS = (("ev_w_in", 1024, 1440, 1), ("ev_w_uq", 256, 768, 1), ("ev_w_ukv", 128, 1024, 1), ("ev_w_out", 1024, 1024, 0),
             ("od_w_in", 1024, 6144, 1), ("od_w_out", 2048, 1024, 0), ("mlp_w1_0", 1024, 4096, 1), ("mlp_w1_1", 1024, 4096, 1),
             ("mlp_w2_0", 4096, 1024, 0), ("mlp_w2_1", 4096, 1024, 0))
BIG_PARAMS = (("ev_w_in", ("ev_w_in",)), ("ev_w_uq", ("ev_w_uq",)), ("ev_w_ukv", ("ev_w_ukv",)), ("ev_w_out", ("ev_w_out",)),
              ("od_w_in", ("od_w_in",)), ("od_w_out", ("od_w_out",)), ("mlp_w1", ("mlp_w1_0", "mlp_w1_1")),
              ("mlp_w2", ("mlp_w2_0", "mlp_w2_1")))
REPLICATED = ("ev_conv_b", "ev_w_rg_a", "ev_b_rg_a", "ev_w_rg_x", "ev_b_rg_x", "ev_lru_lambda", "ev_q_norm_g", "ev_kv_norm_g",
              "ln_mix_g", "ln_mix_b", "ln_mlp_g", "ln_mlp_b")
SMALL_SHARDED = ("meta_tokens", "ev_conv_w")
COL_SHARD_GRADS = ("od_w_in", "mlp_w1_0", "mlp_w1_1")
MATRIX_GROUPS = (("ev_w_in", "ev_w_uq", "ev_w_ukv", "ev_w_out"), ("mlp_w1_0", "mlp_w2_0"), ("od_w_in", "od_w_out", "mlp_w1_1", "mlp_w2_1"))
LAYER_NORMS = ("ln_mix_g", "ln_mix_b", "ln_mlp_g", "ln_mlp_b")
WEIGHT_NAMES = ("meta_tokens", "ev_w_in", "ev_conv_w", "ev_conv_b", "ev_w_rg_a", "ev_b_rg_a", "ev_w_rg_x", "ev_b_rg_x",
                "ev_lru_lambda", "ev_q_norm_g", "ev_w_uq", "ev_kv_norm_g", "ev_w_ukv", "ev_w_out", "od_w_in", "od_w_out",
                "ln_mix_g", "ln_mix_b", "mlp_w1", "mlp_w2", "ln_mlp_g", "ln_mlp_b")


def _to_rows(flat, row_align):
    n = flat.shape[-1]
    rows = _round_up(-(-n // PACK_COLS), row_align)
    pad = rows * PACK_COLS - n
    if pad:
        flat = jnp.pad(flat, [(0, 0)] * (flat.ndim - 1) + [(0, pad)])
    return flat.reshape(flat.shape[:-1] + (rows, PACK_COLS))


def _shard_shape(K, N, axis):
    return (K // N_CHIPS, N) if axis == 0 else (K, N // N_CHIPS)


def _gather_shards(stacked, K, N, axis):
    if axis == 0:
        return stacked.reshape(K, N)
    return stacked.transpose(1, 0, 2).reshape(K, N)


def _split_shards(full, K, N, axis):
    if axis == 0:
        return full.reshape(N_CHIPS, -1)
    return full.reshape(K, N_CHIPS, N // N_CHIPS).transpose(1, 0, 2).reshape(N_CHIPS, -1)


def kernel(x, meta_tokens, ev_w_in, ev_conv_w, ev_conv_b, ev_w_rg_a, ev_b_rg_a, ev_w_rg_x, ev_b_rg_x, ev_lru_lambda, ev_q_norm_g, ev_w_uq, ev_kv_norm_g, ev_w_ukv, ev_w_out, od_w_in, od_w_out, ln_mix_g, ln_mix_b, mlp_w1, mlp_w2, ln_mlp_g, ln_mlp_b, loss_target, m_meta_tokens, m_ev_w_in, m_ev_conv_w, m_ev_conv_b, m_ev_w_rg_a, m_ev_b_rg_a, m_ev_w_rg_x, m_ev_b_rg_x, m_ev_lru_lambda, m_ev_q_norm_g, m_ev_w_uq, m_ev_kv_norm_g, m_ev_w_ukv, m_ev_w_out, m_od_w_in, m_od_w_out, m_ln_mix_g, m_ln_mix_b, m_mlp_w1, m_mlp_w2, m_ln_mlp_g, m_ln_mlp_b, v_meta_tokens, v_ev_w_in, v_ev_conv_w, v_ev_conv_b, v_ev_w_rg_a, v_ev_b_rg_a, v_ev_w_rg_x, v_ev_b_rg_x, v_ev_lru_lambda, v_ev_q_norm_g, v_ev_w_uq, v_ev_kv_norm_g, v_ev_w_ukv, v_ev_w_out, v_od_w_in, v_od_w_out, v_ln_mix_g, v_ln_mix_b, v_mlp_w1, v_mlp_w2, v_ln_mlp_g, v_ln_mlp_b):
    given = dict(locals())
    local_big = {"ev_w_in": ev_w_in[0], "ev_w_uq": ev_w_uq[0], "ev_w_ukv": ev_w_ukv[0], "ev_w_out": ev_w_out[0],
                 "od_w_in": od_w_in[0], "od_w_out": od_w_out[0], "mlp_w1_0": mlp_w1[0], "mlp_w1_1": mlp_w1[1],
                 "mlp_w2_0": mlp_w2[0], "mlp_w2_1": mlp_w2[1]}

    specs = {spec[0]: spec for spec in BIG_SPECS}

    def pack(names):
        return _to_rows(jnp.concatenate([local_big[n].astype(MXU_DTYPE).reshape(-1) for n in names]), 256)

    def unpack(gathered, names):
        gathered, out, off = gathered.reshape(N_CHIPS, -1), {}, 0
        for n in names:
            _, K, N, ax = specs[n]
            shard = _shard_shape(K, N, ax)
            out[n] = _gather_shards(gathered[:, off:off + math.prod(shard)].reshape((N_CHIPS,) + shard), K, N, ax)
            off += math.prod(shard)
        return out

    packed = [pack(names) for names in MATRIX_GROUPS]
    gathered0 = _with_own(_allgather_chips(packed[0], "weight_allgather_mixer0"), packed[0])
    pending1, token1 = _allgather_start(packed[1], "weight_allgather_mlp0_start")
    pending2, token2 = _allgather_start(packed[2], "weight_allgather_layer1_start")
    small = _to_rows(jnp.concatenate([meta_tokens.reshape(-1), ev_conv_w.reshape(-1)]), 16)
    small = _with_own(_allgather_chips(small, "small_allgather"), small).reshape(N_CHIPS, -1)
    n_meta, n_conv = meta_tokens.size, ev_conv_w.size
    meta_full = _gather_shards(small[:, :n_meta].reshape(N_CHIPS, N_META, D_MODEL // N_CHIPS), N_META, D_MODEL, 1)
    conv_full = _gather_shards(small[:, n_meta:n_meta + n_conv].reshape(N_CHIPS, CONV_WIDTH, LRU_WIDTH // N_CHIPS),
                               CONV_WIDTH, LRU_WIDTH, 1)

    def slots(names):
        return {n: jnp.zeros((N_CHIPS, specs[n][1], specs[n][2] // N_CHIPS) if n in COL_SHARD_GRADS else specs[n][1:3], F32) for n in names}

    def norms(names, layer):
        return {n: given[n][layer] for n in names}

    def finish_gather(pending, own, after, names, tag):
        landed = _allgather_wait(pending, lax.stop_gradient(after), "weight_allgather_%s_wait" % tag)
        return unpack(_with_own(_sibling_forward(landed, "weight_allgather_%s_forward" % tag), own), names)

    mixer0_m, mlp0_m, layer1_m = MATRIX_GROUPS
    diff_a = {**slots(mixer0_m), **norms(("ln_mix_g", "ln_mix_b"), 0), **{n: given[n] for n in REPLICATED if n not in LAYER_NORMS},
              "x": x, "meta_tokens": meta_full, "ev_conv_w": conv_full}
    diff_b = {**slots(mlp0_m), **norms(("ln_mlp_g", "ln_mlp_b"), 0)}
    diff_c = {**slots(layer1_m), **norms(LAYER_NORMS, 1)}
    w_a = unpack(gathered0, mixer0_m)
    h_a, back_a = jax.vjp(lambda d: _mixer0(d, w_a, token1 + token2), diff_a)
    w_b = finish_gather(pending1, packed[1], h_a, mlp0_m, "mlp0")
    h_b, back_b = jax.vjp(lambda d, hh: _mlp0(d, hh, w_b), diff_b, h_a)
    w_c = finish_gather(pending2, packed[2], h_b, layer1_m, "layer1")
    loss, back_c = jax.vjp(lambda d, hh: _layer1_loss(d, hh, w_c, loss_target), diff_c, h_b)
    loss = lax.psum(loss, ("x", "y", "c"))

    def blocks_of(grad, n):
        _, K, N, ax = specs[n]
        if n in COL_SHARD_GRADS:
            blocks = grad
        elif ax == 0:
            blocks = grad.reshape(N_CHIPS, K // N_CHIPS, N)
        else:
            blocks = grad.reshape(K, N_CHIPS, N // N_CHIPS).transpose(1, 0, 2)
        return blocks.reshape(N_CHIPS, 2, blocks.shape[1] // 2, blocks.shape[2])

    def start_reduce(grads_of, names, tag):
        ss = _sibling_reduce([blocks_of(grads_of[n], n) for n in names], [MXU_DTYPE] * len(names), tag + "_")
        return _chip_scatter_start(ss, "grad_chip_scatter_%s_start" % tag)

    g_c, dh = back_c(jnp.ones((), F32))
    pending_c, token = start_reduce(g_c, layer1_m, "layer1")
    g_b, dh = back_b(dh + token)
    pending_b, token = start_reduce(g_b, mlp0_m, "mlp0")
    (g_a,) = back_a(dh + token)
    ss_c, ts_c = _chip_scatter_wait(pending_c, g_a["x"], "grad_chip_scatter_layer1_wait")
    ss_b, ts_b = _chip_scatter_wait(pending_b, g_a["x"], "grad_chip_scatter_mlp0_wait")

    g = {**g_a, **g_b, **g_c}
    g.update({n: jnp.stack([(g_b if n in g_b else g_a)[n], g_c[n]]) for n in LAYER_NORMS})
    repl = jnp.concatenate([g[n].reshape(-1) for n in REPLICATED]).reshape(N_CHIPS, -1)
    small = [_split_shards(g["meta_tokens"], N_META, D_MODEL, 1), _split_shards(g["ev_conv_w"], CONV_WIDTH, LRU_WIDTH, 1), repl]
    small = [pc.reshape(N_CHIPS, 2, -1) for pc in small]
    n_small = sum(pc.shape[2] for pc in small)
    small.append(jnp.zeros((N_CHIPS, 2, _round_up(n_small, 32 * PACK_COLS) - n_small), F32))
    p_small = jnp.concatenate(small, axis=2).reshape(N_CHIPS, 2, -1, PACK_COLS)
    ss_a = _sibling_reduce([blocks_of(g_a[n], n) for n in mixer0_m] + [p_small], [MXU_DTYPE] * len(mixer0_m) + [F32], "mixer0_")
    ts_a = list(_chip_scatter(ss_a, "grad_chip_scatter_mixer0"))
    reduced = _sum_and_share(ss_a + ss_b + ss_c, ts_a + ts_b + ts_c, "")
    red_big = dict(zip(mixer0_m + ("small",) + mlp0_m + layer1_m, reduced))
    red_small = red_big.pop("small").reshape(2, -1)

    grads = {}
    for name, parts in BIG_PARAMS:
        grads[name] = jnp.stack([red_big[part].reshape(given[name].shape[1:]) for part in parts])

    def take(off, sz):
        return jnp.concatenate([red_small[0, off // 2:(off + sz) // 2], red_small[1, off // 2:(off + sz) // 2]])

    off = 0
    for name in SMALL_SHARDED:
        sz = given[name].size
        grads[name] = take(off, sz).reshape(given[name].shape)
        off += sz
    n_repl = repl.shape[1]
    own_repl = _to_rows(take(off, n_repl), 16)
    repl_all = _with_own(_allgather_chips(own_repl, "replicated_allgather"), own_repl).reshape(N_CHIPS, -1)[:, :n_repl].reshape(-1)
    off = 0
    for name in REPLICATED:
        sz = given[name].size
        grads[name] = repl_all[off:off + sz].reshape(given[name].shape)
        off += sz

    delta, new_m, new_v = {}, {}, {}
    for name, _ in BIG_PARAMS:
        shp = given[name].shape
        two_d = (-1, shp[-1])
        d, nm, nv = _adamw(given[name].reshape(two_d), grads[name].reshape(two_d), given["m_" + name].reshape(two_d),
                           given["v_" + name].reshape(two_d), "adamw_" + name)
        delta[name], new_m[name], new_v[name] = d.reshape(shp), nm.reshape(shp), nv.reshape(shp)
    smalls = SMALL_SHARDED + REPLICATED

    def pack_small(get):
        return _to_rows(jnp.concatenate([get(n).reshape(-1) for n in smalls]), 8)

    outs = _adamw(pack_small(lambda n: given[n]), pack_small(lambda n: grads[n]), pack_small(lambda n: given["m_" + n]),
                  pack_small(lambda n: given["v_" + n]), "adamw_small")
    for res, flat in zip((delta, new_m, new_v), outs):
        flat, off = flat.reshape(-1), 0
        for n in smalls:
            sz = given[n].size
            res[n] = flat[off:off + sz].reshape(given[n].shape)
            off += sz

    return (loss, g_a["x"], *[grads[n] for n in WEIGHT_NAMES], *[delta[n] for n in WEIGHT_NAMES],
            *[new_m[n] for n in WEIGHT_NAMES], *[new_v[n] for n in WEIGHT_NAMES])
```

```python
import functools
import math

import jax
import jax.numpy as jnp
from jax import lax
from jax.experimental import pallas as pl
from jax.experimental.pallas import tpu as pltpu

F32 = jnp.float32
MXU_DTYPE = jnp.bfloat16

D_MODEL = 1024
N_META = 16
LRU_WIDTH = 512
LRU_HEADS = 4
LRU_HEAD_DIM = 128
CONV_WIDTH = 4
LRU_C = 8.0
MLA_HEADS = 8
MLA_NOPE = 64
MLA_ROPE = 32
MLA_V = 64
MLA_Q_RANK = 256
MLA_KV_RANK = 128
RET_HEADS = 4
RET_QK_DIM = 256
RET_V_DIM = 512
D_FF = 4096
ROPE_BASE = 10000.0
DN_ALPHA = 4.0 ** 0.25
EPS = 1e-5
NEG_INF = -1e30
SEQ_BLOCK = 128

ADAM_LR = 0.001
ADAM_B1 = 0.9
ADAM_B2 = 0.999
ADAM_EPS = 1e-08
ADAM_WD = 0.01
ADAM_STEP = 10

PACK_COLS = 1024
N_CHIPS = 4

MESH = pl.DeviceIdType.MESH


def _pick(n, target, align):
    best = None
    for t in range(align, min(n, target) + 1, align):
        if n % t == 0:
            best = t
    return n if best is None else best


def _round_up(n, m):
    return (n + m - 1) // m * m


def _relu2(a):
    r = jnp.maximum(a, 0.0)
    return r * r


def _mm_nn(a, w, act, name, out_dtype=F32):
    M, K = a.shape
    _, N = w.shape
    tm = _pick(M, 1088 if K * a.dtype.itemsize <= 4096 else 544, 8)
    tn = _pick(N, 1024, 128)

    def body(a_ref, w_ref, o_ref):
        av = a_ref[...]
        if act:
            av = _relu2(av.astype(F32))
        o_ref[...] = jnp.dot(av.astype(MXU_DTYPE), w_ref[...].astype(MXU_DTYPE), preferred_element_type=F32).astype(out_dtype)

    return pl.pallas_call(
        body, name=name,
        grid=(M // tm, N // tn),
        in_specs=[pl.BlockSpec((tm, K), lambda i, j: (i, 0)), pl.BlockSpec((K, tn), lambda i, j: (0, j))],
        out_specs=pl.BlockSpec((tm, tn), lambda i, j: (i, j)),
        out_shape=jax.ShapeDtypeStruct((M, N), out_dtype),
        compiler_params=pltpu.CompilerParams(dimension_semantics=("parallel", "arbitrary")),
    )(a, w)


def _mm_nt(g, w, a_src, name, out_dtype=F32):
    M, N = g.shape
    K, _ = w.shape
    tk = N if N * g.dtype.itemsize <= 8192 else _pick(N, 2048, 128)
    nk = N // tk
    tm = _pick(M, 1088 if tk * g.dtype.itemsize <= 4096 else 544, 8)
    tn = _pick(K, 1024, 128)
    has_src = a_src is not None
    assert nk == 1 or out_dtype == F32

    def body(*refs):
        if has_src:
            g_ref, w_ref, s_ref, o_ref = refs
        else:
            g_ref, w_ref, o_ref = refs
        r = lax.dot_general(g_ref[...].astype(MXU_DTYPE), w_ref[...].astype(MXU_DTYPE),
                            (((1,), (1,)), ((), ())), preferred_element_type=F32)
        if has_src:
            r = r * (2.0 * jnp.maximum(s_ref[...].astype(F32), 0.0))
        if nk == 1:
            o_ref[...] = r.astype(out_dtype)
        else:
            k = pl.program_id(2)

            @pl.when(k == 0)
            def _():
                o_ref[...] = r

            @pl.when(k > 0)
            def _():
                o_ref[...] += r

    in_specs = [pl.BlockSpec((tm, tk), lambda i, j, k: (i, k)), pl.BlockSpec((tn, tk), lambda i, j, k: (j, k))]
    args = [g, w]
    if has_src:
        assert nk == 1
        in_specs.append(pl.BlockSpec((tm, tn), lambda i, j, k: (i, j)))
        args.append(a_src)
    return pl.pallas_call(
        body, name=name,
        grid=(M // tm, K // tn, nk),
        in_specs=in_specs,
        out_specs=pl.BlockSpec((tm, tn), lambda i, j, k: (i, j)),
        out_shape=jax.ShapeDtypeStruct((M, K), out_dtype),
        compiler_params=pltpu.CompilerParams(dimension_semantics=("parallel", "parallel", "arbitrary")),
    )(*args)


def _mm_tn(a, g, act, name, col_shards=1, out_dtype=F32):
    M, K = a.shape
    _, N = g.shape
    n = N // col_shards
    tm, tn, tk = _pick(K, 1024, 128), _pick(n, 1024, 128), _pick(M, 2176, 8)
    nk = M // tk
    per = n // tn
    direct = out_dtype == F32

    def body(a_ref, g_ref, o_ref, *scratch):
        acc_ref = o_ref if direct else scratch[0]
        k = pl.program_id(2)
        av = a_ref[...]
        if act:
            av = _relu2(av.astype(F32))
        r = lax.dot_general(av.astype(MXU_DTYPE), g_ref[...].astype(MXU_DTYPE),
                            (((0,), (0,)), ((), ())), preferred_element_type=F32)

        @pl.when(k == 0)
        def _():
            acc_ref[...] = r

        @pl.when(k > 0)
        def _():
            acc_ref[...] += r

        if not direct:
            @pl.when(k == nk - 1)
            def _():
                o_ref[...] = acc_ref[...].astype(out_dtype)

    if col_shards == 1:
        out_spec, out_shape = pl.BlockSpec((tm, tn), lambda i, j, k: (i, j)), (K, N)
    else:
        out_spec, out_shape = pl.BlockSpec((None, tm, tn), lambda i, j, k: (j // per, i, j % per)), (col_shards, K, n)
    return pl.pallas_call(
        body, name=name,
        grid=(K // tm, N // tn, nk),
        in_specs=[pl.BlockSpec((tk, tm), lambda i, j, k: (k, i)), pl.BlockSpec((tk, tn), lambda i, j, k: (k, j))],
        out_specs=out_spec,
        out_shape=jax.ShapeDtypeStruct(out_shape, out_dtype),
        scratch_shapes=[] if direct else [pltpu.VMEM((tm, tn), F32)],
        compiler_params=pltpu.CompilerParams(dimension_semantics=("parallel", "parallel", "arbitrary")),
    )(a, g)


@functools.partial(jax.custom_vjp, nondiff_argnums=(3, 4, 5, 6))
def matmul(a, w, w_grad_slot, act, name, out_dtype, col_shards):
    return _mm_nn(a, w, act, name + "_fwd", out_dtype)


def _matmul_fwd(a, w, w_grad_slot, act, name, out_dtype, col_shards):
    return _mm_nn(a, w, act, name + "_fwd", out_dtype), (a, w, jnp.zeros((), w_grad_slot.dtype))


def _matmul_bwd(act, name, out_dtype, col_shards, res, g):
    a, w, slot_like = res
    w_grad_dtype = slot_like.dtype
    da = _mm_nt(g, w, a if act else None, name + "_dx")
    dw = _mm_tn(a, g, act, name + "_dw", col_shards, w_grad_dtype)
    return da, None, dw


matmul.defvjp(_matmul_fwd, _matmul_bwd)


@functools.partial(jax.custom_vjp, nondiff_argnums=(5,))
def mlp(h, w1, w2, w1_grad_slot, w2_grad_slot, name):
    u = _mm_nn(h, w1, False, name + "_w1_fwd", out_dtype=MXU_DTYPE)
    return _mm_nn(u, w2, True, name + "_w2_fwd")


def _mlp_fwd(h, w1, w2, w1_grad_slot, w2_grad_slot, name):
    u = _mm_nn(h, w1, False, name + "_w1_fwd", out_dtype=MXU_DTYPE)
    return _mm_nn(u, w2, True, name + "_w2_fwd"), (h, u, w1, w2, jnp.zeros((), w1_grad_slot.dtype))


def _mlp_bwd(name, res, df):
    h, u, w1, w2, slot_like = res
    du = _mm_nt(df, w2, u, name + "_w2_dx", out_dtype=MXU_DTYPE)
    dw2 = _mm_tn(u, df, True, name + "_w2_dw", 1, slot_like.dtype)
    dh = _mm_nt(du, w1, None, name + "_w1_dx")
    dw1 = _mm_tn(h, du, False, name + "_w1_dw", N_CHIPS, slot_like.dtype)
    return dh, None, None, dw1, dw2


mlp.defvjp(_mlp_fwd, _mlp_bwd)


def _ln_stats(z):
    mu = jnp.mean(z, axis=-1, keepdims=True)
    zc = z - mu
    var = jnp.mean(zc * zc, axis=-1, keepdims=True)
    return zc, lax.rsqrt(var + EPS)


def _ln_fwd_call(resid, branch, g, b, name):
    M, D = resid.shape
    tm = _pick(M, 544, 8)

    def body(r_ref, br_ref, g_ref, b_ref, o_ref):
        zc, rstd = _ln_stats(DN_ALPHA * r_ref[...] + br_ref[...])
        o_ref[...] = zc * rstd * g_ref[...] + b_ref[...]

    row = pl.BlockSpec((tm, D), lambda i: (i, 0))
    vec = pl.BlockSpec((1, D), lambda i: (0, 0))
    return pl.pallas_call(
        body, name=name, grid=(M // tm,), in_specs=[row, row, vec, vec], out_specs=row,
        out_shape=jax.ShapeDtypeStruct((M, D), F32),
        compiler_params=pltpu.CompilerParams(dimension_semantics=("parallel",)),
    )(resid, branch, g.reshape(1, D), b.reshape(1, D))


def _ln_bwd_call(resid, branch, g, dy, name):
    M, D = resid.shape
    tm = _pick(M, 544, 8)

    def body(r_ref, br_ref, g_ref, dy_ref, dz_ref, dg_ref, db_ref):
        @pl.when(pl.program_id(0) == 0)
        def _():
            dg_ref[...] = jnp.zeros_like(dg_ref)
            db_ref[...] = jnp.zeros_like(db_ref)

        zc, rstd = _ln_stats(DN_ALPHA * r_ref[...] + br_ref[...])
        xhat = zc * rstd
        dy = dy_ref[...]
        dxh = dy * g_ref[...]
        m1 = jnp.mean(dxh, axis=-1, keepdims=True)
        m2 = jnp.mean(dxh * xhat, axis=-1, keepdims=True)
        dz_ref[...] = rstd * (dxh - m1 - xhat * m2)
        dg_ref[...] += jnp.sum(dy * xhat, axis=0, keepdims=True)
        db_ref[...] += jnp.sum(dy, axis=0, keepdims=True)

    row = pl.BlockSpec((tm, D), lambda i: (i, 0))
    vec = pl.BlockSpec((1, D), lambda i: (0, 0))
    return pl.pallas_call(
        body, name=name, grid=(M // tm,), in_specs=[row, row, vec, row], out_specs=[row, vec, vec],
        out_shape=[jax.ShapeDtypeStruct((M, D), F32), jax.ShapeDtypeStruct((1, D), F32), jax.ShapeDtypeStruct((1, D), F32)],
        compiler_params=pltpu.CompilerParams(dimension_semantics=("arbitrary",)),
    )(resid, branch, g.reshape(1, D), dy)


@functools.partial(jax.custom_vjp, nondiff_argnums=(4,))
def deepnorm(resid, branch, g, b, name):
    return _ln_fwd_call(resid, branch, g, b, name + "_fwd")


def _deepnorm_fwd(resid, branch, g, b, name):
    return _ln_fwd_call(resid, branch, g, b, name + "_fwd"), (resid, branch, g)


def _deepnorm_bwd(name, res, dy):
    resid, branch, g = res
    dz, dg, db = _ln_bwd_call(resid, branch, g, dy, name + "_bwd")
    return DN_ALPHA * dz, dz, dg.reshape(g.shape), db.reshape(g.shape)


deepnorm.defvjp(_deepnorm_fwd, _deepnorm_bwd)


def _rms_fwd_call(x, g, name, col_block=0):
    R = x.shape[0]
    W = g.shape[-1]
    tr = _pick(R, 1088, 8)

    def body(x_ref, g_ref, o_ref):
        xv = x_ref[...]
        rstd = lax.rsqrt(jnp.mean(xv * xv, axis=-1, keepdims=True) + EPS)
        o_ref[...] = xv * rstd * g_ref[...]

    vec = pl.BlockSpec((1, W), lambda i: (0, 0))
    return pl.pallas_call(
        body, name=name, grid=(R // tr,), in_specs=[pl.BlockSpec((tr, W), lambda i: (i, col_block)), vec],
        out_specs=pl.BlockSpec((tr, W), lambda i: (i, 0)), out_shape=jax.ShapeDtypeStruct((R, W), F32),
        compiler_params=pltpu.CompilerParams(dimension_semantics=("parallel",)),
    )(x, g.reshape(1, W))


def _rms_bwd_call(x, g, dy, name, col_block=0):
    R = x.shape[0]
    W = g.shape[-1]
    tr = _pick(R, 1088, 8)

    def body(x_ref, g_ref, dy_ref, dx_ref, dg_ref):
        @pl.when(pl.program_id(0) == 0)
        def _():
            dg_ref[...] = jnp.zeros_like(dg_ref)

        xv = x_ref[...]
        rstd = lax.rsqrt(jnp.mean(xv * xv, axis=-1, keepdims=True) + EPS)
        xhat = xv * rstd
        dy = dy_ref[...]
        dxh = dy * g_ref[...]
        dx_ref[...] = rstd * (dxh - xhat * jnp.mean(dxh * xhat, axis=-1, keepdims=True))
        dg_ref[...] += jnp.sum(dy * xhat, axis=0, keepdims=True)

    row = pl.BlockSpec((tr, W), lambda i: (i, 0))
    vec = pl.BlockSpec((1, W), lambda i: (0, 0))
    return pl.pallas_call(
        body, name=name, grid=(R // tr,), in_specs=[pl.BlockSpec((tr, W), lambda i: (i, col_block)), vec, row], out_specs=[row, vec],
        out_shape=[jax.ShapeDtypeStruct((R, W), F32), jax.ShapeDtypeStruct((1, W), F32)],
        compiler_params=pltpu.CompilerParams(dimension_semantics=("arbitrary",)),
    )(x, g.reshape(1, W), dy)


def _loss_call(y, tgt, name):
    R, D = y.shape
    tr = _pick(R, 512, 8)

    def body(y_ref, t_ref, dy_ref, acc_ref):
        @pl.when(pl.program_id(0) == 0)
        def _():
            acc_ref[...] = jnp.zeros_like(acc_ref)

        e = y_ref[...] - t_ref[...]
        dy_ref[...] = e * (1.0 / D)
        acc_ref[...] += jnp.sum(jnp.sum(e * e, axis=-1, keepdims=True), axis=0, keepdims=True) * (0.5 / D)

    row = pl.BlockSpec((tr, D), lambda i: (i, 0))
    one = pl.BlockSpec((1, 1), lambda i: (0, 0))
    return pl.pallas_call(
        body, name=name, grid=(R // tr,), in_specs=[row, row], out_specs=[row, one],
        out_shape=[jax.ShapeDtypeStruct((R, D), F32), jax.ShapeDtypeStruct((1, 1), F32)],
        compiler_params=pltpu.CompilerParams(dimension_semantics=("arbitrary",)),
    )(y, tgt)


@jax.custom_vjp
def loss_head(y, tgt):
    return _loss_call(y, tgt, "loss_head")[1][0, 0]


def _loss_head_fwd(y, tgt):
    dy, acc = _loss_call(y, tgt, "loss_head")
    return acc[0, 0], dy


def _loss_head_bwd(dy, ct):
    return ct * dy, None


loss_head.defvjp(_loss_head_fwd, _loss_head_bwd)


_GELU_C = math.sqrt(2.0 / math.pi)


def _gelu_parts(x):
    x2 = x * x
    t = jnp.tanh(_GELU_C * (x + 0.044715 * x * x2))
    gelu = 0.5 * x * (1.0 + t)
    dgelu = 0.5 * (1.0 + t) + 0.5 * x * (1.0 - t * t) * (_GELU_C * (1.0 + 3.0 * 0.044715 * x2))
    return gelu, dgelu


def _sigmoid(x):
    return 1.0 / (1.0 + jnp.exp(-x))


def _scan8(a, b, carry, reverse):
    row = lax.broadcasted_iota(jnp.int32, a.shape, 0)
    for s in (1, 2, 4):
        shift = 8 - s if reverse else s
        keep = (row < 8 - s) if reverse else (row >= s)
        b = jnp.where(keep, a * pltpu.roll(b, shift, 0) + b, b)
        a = jnp.where(keep, a * pltpu.roll(a, shift, 0), a)
    return a * carry + b


def _lru_pre(prec_ref, prev_ref, first, cw_ref, cb_ref, wa_ref, ba_ref, wx_ref, bx_ref, sp_ref):
    tc = prec_ref.shape[0]
    prev = jnp.where(first, 0.0, prev_ref[...])
    ext = jnp.concatenate([prev, prec_ref[...]], axis=0)
    cw = cw_ref[...]
    taps = [ext[8:] if k == CONV_WIDTH - 1 else pltpu.roll(ext, CONV_WIDTH - 1 - k, 0)[8:] for k in range(CONV_WIDTH)]
    xc = cb_ref[...] + sum(cw[k:k + 1, :] * taps[k] for k in range(CONV_WIDTH))
    ga, gx = [], []
    for h in range(LRU_HEADS):
        xh = xc[:, h * LRU_HEAD_DIM:(h + 1) * LRU_HEAD_DIM].astype(MXU_DTYPE)
        ga.append(jnp.dot(xh, wa_ref[h].astype(MXU_DTYPE), preferred_element_type=F32))
        gx.append(jnp.dot(xh, wx_ref[h].astype(MXU_DTYPE), preferred_element_type=F32))
    r = _sigmoid(jnp.concatenate(ga, axis=1) + ba_ref[...])
    i = _sigmoid(jnp.concatenate(gx, axis=1) + bx_ref[...])
    log_a = -LRU_C * r * sp_ref[...]
    a = jnp.exp(log_a)
    a2 = a * a
    mult = jnp.sqrt(-jnp.tanh(log_a) * (a2 + 1.0))
    return taps, xc, r, i, a, a2, mult


def _lru_fwd_call(p, cw, cb, wa, ba, wx, bx, sp):
    B, Tp, _ = p.shape
    W = LRU_WIDTH
    tc = SEQ_BLOCK
    nc = Tp // tc

    def body(pg_ref, prec_ref, prev_ref, cw_ref, cb_ref, wa_ref, ba_ref, wx_ref, bx_ref, sp_ref, y_ref, h_ref, carry_ref):
        first = pl.program_id(1) == 0

        @pl.when(first)
        def _():
            carry_ref[...] = jnp.zeros_like(carry_ref)

        _, xc, r, i, a, a2, mult = _lru_pre(prec_ref, prev_ref, first, cw_ref, cb_ref, wa_ref, ba_ref, wx_ref, bx_ref, sp_ref)
        b = mult * (i * xc)
        carry = carry_ref[0:1, :]
        for t in range(tc // 8):
            h = _scan8(a[8 * t:8 * t + 8], b[8 * t:8 * t + 8], carry, False)
            h_ref[8 * t:8 * t + 8, :] = h
            carry = h[7:8, :]
        carry_ref[...] = jnp.broadcast_to(carry, carry_ref.shape)
        y_ref[...] = h_ref[...] * _gelu_parts(pg_ref[...])[0]

    cur = pl.BlockSpec((None, tc, W), lambda b, j: (b, j, 0))
    rec = pl.BlockSpec((None, tc, W), lambda b, j: (b, j, 1))
    prev = pl.BlockSpec((None, 8, W), lambda b, j: (b, jnp.maximum(j * (tc // 8) - 1, 0), 1))
    vec = pl.BlockSpec((1, W), lambda b, j: (0, 0))
    cws = pl.BlockSpec((CONV_WIDTH, W), lambda b, j: (0, 0))
    wsp = pl.BlockSpec((LRU_HEADS, LRU_HEAD_DIM, LRU_HEAD_DIM), lambda b, j: (0, 0, 0))
    return pl.pallas_call(
        body, name="lru_fwd", grid=(B, nc),
        in_specs=[cur, rec, prev, cws, vec, wsp, vec, wsp, vec, vec],
        out_specs=[cur, cur],
        out_shape=[jax.ShapeDtypeStruct((B, Tp, W), F32), jax.ShapeDtypeStruct((B, Tp, W), F32)],
        scratch_shapes=[pltpu.VMEM((8, W), F32)],
        compiler_params=pltpu.CompilerParams(dimension_semantics=("arbitrary", "arbitrary")),
    )(p, p, p, cw, cb, wa, ba, wx, bx, sp)


def _lru_bwd_call(p, hseq, dy, cw, cb, wa, ba, wx, bx, sp):
    B, Tp, _ = p.shape
    W = LRU_WIDTH
    tc = SEQ_BLOCK
    nc = Tp // tc
    HD = LRU_HEAD_DIM

    def body(pg_ref, prec_ref, prev_ref, h_ref, hprev_ref, dy_ref, cw_ref, cb_ref, wa_ref, ba_ref, wx_ref, bx_ref, sp_ref,
             dpg_ref, dprec_ref, dcw_ref, dcb_ref, dwa_ref, dba_ref, dwx_ref, dbx_ref, dsp_ref,
             gcar_ref, anext_ref, halo_ref, g_ref):
        j = pl.program_id(1)
        first = j == nc - 1
        last = j == 0

        @pl.when(jnp.logical_and(pl.program_id(0) == 0, last))
        def _():
            for ref in (dcw_ref, dcb_ref, dwa_ref, dba_ref, dwx_ref, dbx_ref, dsp_ref):
                ref[...] = jnp.zeros_like(ref)

        @pl.when(last)
        def _():
            gcar_ref[...] = jnp.zeros_like(gcar_ref)
            anext_ref[...] = jnp.zeros_like(anext_ref)
            halo_ref[...] = jnp.zeros_like(halo_ref)

        taps, xc, r, i, a, a2, mult = _lru_pre(prec_ref, prev_ref, first, cw_ref, cb_ref, wa_ref, ba_ref, wx_ref, bx_ref, sp_ref)
        row = lax.broadcasted_iota(jnp.int32, (tc, W), 0)
        gelu, dgelu = _gelu_parts(pg_ref[...])
        dy = dy_ref[...]
        hcur = h_ref[...]
        dpg_ref[...] = dy * hcur * dgelu
        dh = dy * gelu
        a_next = jnp.where(row == tc - 1, anext_ref[0:1, :], pltpu.roll(a, tc - 1, 0))
        carry = gcar_ref[0:1, :]
        for t in reversed(range(tc // 8)):
            g = _scan8(a_next[8 * t:8 * t + 8], dh[8 * t:8 * t + 8], carry, True)
            g_ref[8 * t:8 * t + 8, :] = g
            carry = g[0:1, :]
        gcar_ref[...] = jnp.broadcast_to(carry, gcar_ref.shape)
        anext_ref[...] = jnp.broadcast_to(a[0:1, :], anext_ref.shape)
        G = g_ref[...]
        h_before = jnp.where(first, 0.0, hprev_ref[7:8, :])
        hprev = jnp.where(row == 0, h_before, pltpu.roll(hcur, 1, 0))
        d_a = G * hprev
        gx_ = G * xc
        d_mult = gx_ * i
        d_i = gx_ * mult
        dxc = G * (mult * i)
        d_la = d_a * a - d_mult * (a2 / mult)
        sp = sp_ref[...]
        d_r = d_la * (-LRU_C * sp)
        dsp_ref[...] += jnp.sum(d_la * (-LRU_C * r), axis=0, keepdims=True)
        dga = d_r * r * (1.0 - r)
        dgx = d_i * i * (1.0 - i)
        dba_ref[...] += jnp.sum(dga, axis=0, keepdims=True)
        dbx_ref[...] += jnp.sum(dgx, axis=0, keepdims=True)
        back = []
        for h in range(LRU_HEADS):
            sl = slice(h * HD, (h + 1) * HD)
            xh = xc[:, sl].astype(MXU_DTYPE)
            ah = dga[:, sl].astype(MXU_DTYPE)
            bh = dgx[:, sl].astype(MXU_DTYPE)
            tn = (((0,), (0,)), ((), ()))
            nt = (((1,), (1,)), ((), ()))
            dwa_ref[h] += lax.dot_general(xh, ah, tn, preferred_element_type=F32)
            dwx_ref[h] += lax.dot_general(xh, bh, tn, preferred_element_type=F32)
            back.append(lax.dot_general(ah, wa_ref[h].astype(MXU_DTYPE), nt, preferred_element_type=F32)
                        + lax.dot_general(bh, wx_ref[h].astype(MXU_DTYPE), nt, preferred_element_type=F32))
        dxc = dxc + jnp.concatenate(back, axis=1)
        dcb_ref[...] += jnp.sum(dxc, axis=0, keepdims=True)
        for k in range(CONV_WIDTH):
            dcw_ref[k:k + 1, :] += jnp.sum(dxc * taps[k], axis=0, keepdims=True)
        ext = jnp.concatenate([dxc, halo_ref[...]], axis=0)
        cw = cw_ref[...]
        acc = cw[CONV_WIDTH - 1:CONV_WIDTH, :] * dxc
        for k in range(CONV_WIDTH - 1):
            s = CONV_WIDTH - 1 - k
            acc = acc + cw[k:k + 1, :] * pltpu.roll(ext, tc + 8 - s, 0)[:tc]
        dprec_ref[...] = acc
        halo_ref[...] = dxc[0:8, :]

    rev = lambda j: nc - 1 - j
    cur = pl.BlockSpec((None, tc, W), lambda b, j: (b, rev(j), 0))
    rec = pl.BlockSpec((None, tc, W), lambda b, j: (b, rev(j), 1))
    prev = pl.BlockSpec((None, 8, W), lambda b, j: (b, jnp.maximum(rev(j) * (tc // 8) - 1, 0), 0))
    prev_rec = pl.BlockSpec((None, 8, W), lambda b, j: (b, jnp.maximum(rev(j) * (tc // 8) - 1, 0), 1))
    vec = pl.BlockSpec((1, W), lambda b, j: (0, 0))
    cws = pl.BlockSpec((CONV_WIDTH, W), lambda b, j: (0, 0))
    wsp = pl.BlockSpec((LRU_HEADS, HD, HD), lambda b, j: (0, 0, 0))
    seq = jax.ShapeDtypeStruct((B, Tp, W), F32)
    vs = jax.ShapeDtypeStruct((1, W), F32)
    ws = jax.ShapeDtypeStruct((LRU_HEADS, HD, HD), F32)
    return pl.pallas_call(
        body, name="lru_bwd", grid=(B, nc),
        in_specs=[cur, rec, prev_rec, cur, prev, cur, cws, vec, wsp, vec, wsp, vec, vec],
        out_specs=[cur, cur, cws, vec, wsp, vec, wsp, vec, vec],
        out_shape=[seq, seq, jax.ShapeDtypeStruct((CONV_WIDTH, W), F32), vs, ws, vs, ws, vs, vs],
        scratch_shapes=[pltpu.VMEM((8, W), F32), pltpu.VMEM((8, W), F32), pltpu.VMEM((8, W), F32), pltpu.VMEM((tc, W), F32)],
        compiler_params=pltpu.CompilerParams(dimension_semantics=("arbitrary", "arbitrary")),
    )(p, p, p, hseq, hseq, dy, cw, cb, wa, ba, wx, bx, sp)


_Q_BLOCK = 2 * LRU_WIDTH // MLA_Q_RANK
_KV_BLOCK = (2 * LRU_WIDTH + MLA_Q_RANK) // MLA_KV_RANK
_KPE_START = 2 * LRU_WIDTH + MLA_Q_RANK + MLA_KV_RANK


@jax.custom_vjp
def even_front(p, cw, cb, wa, ba, wx, bx, sp, gq, gkv):
    return _even_front_fwd(p, cw, cb, wa, ba, wx, bx, sp, gq, gkv)[0]


def _even_front_fwd(p, cw, cb, wa, ba, wx, bx, sp, gq, gkv):
    B, Tp, W = p.shape
    p2d = p.reshape(B * Tp, W)
    y, hseq = _lru_fwd_call(p, cw, cb, wa, ba, wx, bx, sp)
    qn = _rms_fwd_call(p2d, gq, "q_norm_fwd", _Q_BLOCK)
    kvn = _rms_fwd_call(p2d, gkv, "kv_norm_fwd", _KV_BLOCK)
    return (y, qn, kvn, p2d[:, _KPE_START:]), (p, hseq, cw, cb, wa, ba, wx, bx, sp, gq, gkv)


def _even_front_bwd(res, cts):
    p, hseq, cw, cb, wa, ba, wx, bx, sp, gq, gkv = res
    dy, dqn, dkvn, dkpe = cts
    B, Tp, W = p.shape
    p2d = p.reshape(B * Tp, W)
    dpg, dprec, dcw, dcb, dwa, dba, dwx, dbx, dsp = _lru_bwd_call(p, hseq, dy, cw, cb, wa, ba, wx, bx, sp)
    dpq, dgq = _rms_bwd_call(p2d, gq, dqn, "q_norm_bwd", _Q_BLOCK)
    dpkv, dgkv = _rms_bwd_call(p2d, gkv, dkvn, "kv_norm_bwd", _KV_BLOCK)
    dp = jnp.concatenate([dpg.reshape(B * Tp, -1), dprec.reshape(B * Tp, -1), dpq, dpkv, dkpe], axis=1).reshape(B, Tp, W)
    return dp, dcw, dcb, dwa, dba, dwx, dbx, dsp, dgq.reshape(gq.shape), dgkv.reshape(gkv.shape)


even_front.defvjp(_even_front_fwd, _even_front_bwd)


def _rope_tables(pos, half):
    inv = ROPE_BASE ** (-jnp.arange(half, dtype=F32) / half)
    ang = pos.astype(F32)[:, None] * inv[None, :]
    return jnp.cos(ang), jnp.sin(ang)


_NT = (((1,), (1,)), ((), ()))
_TN = (((0,), (0,)), ((), ()))
HEAD_LANES = 128
_MLA_SCALE = (MLA_NOPE + MLA_ROPE) ** -0.5
_LOG2E = math.log2(math.e)


def _mask_diagonal(s, fill):
    L = s.shape[1]
    row = lax.broadcasted_iota(jnp.int32, (SEQ_BLOCK, SEQ_BLOCK), 0)
    col = lax.broadcasted_iota(jnp.int32, (SEQ_BLOCK, SEQ_BLOCK), 1)
    last = jnp.where(col <= row, s[:, L - SEQ_BLOCK:], fill)
    return last if L == SEQ_BLOCK else jnp.concatenate([s[:, :L - SEQ_BLOCK], last], axis=1)


def _mla_rope_tables(pos):
    half = MLA_ROPE // 2
    cos, sin = _rope_tables(pos, half)
    T = pos.shape[0]
    ones, zeros = jnp.ones((T, MLA_NOPE), F32), jnp.zeros((T, MLA_NOPE), F32)
    tail1, tail0 = jnp.ones((T, HEAD_LANES - MLA_NOPE - MLA_ROPE), F32), jnp.zeros((T, HEAD_LANES - MLA_NOPE - MLA_ROPE), F32)
    zh = jnp.zeros((T, half), F32)
    c = jnp.concatenate([ones, cos, cos, tail1], axis=1)
    s_up = jnp.concatenate([zeros, -sin, zh, tail0], axis=1)
    s_down = jnp.concatenate([zeros, zh, sin, tail0], axis=1)
    return c, s_up, s_down


def _rope_lanes(x, c, s_up, s_down):
    half = MLA_ROPE // 2
    return x * c + pltpu.roll(x, HEAD_LANES - half, 1) * s_up + pltpu.roll(x, half, 1) * s_down


def _unrope_lanes(d, c, s_up, s_down):
    half = MLA_ROPE // 2
    return d * c + pltpu.roll(d * s_up, half, 1) + pltpu.roll(d * s_down, HEAD_LANES - half, 1)


def _mla_operands(q_ref, kv_ref, kpe_ref, c, s_up, s_down):
    lane = lax.broadcasted_iota(jnp.int32, kv_ref.shape, 1)
    qr = (_rope_lanes(q_ref[...].astype(F32), c, s_up, s_down) * (_MLA_SCALE * _LOG2E)).astype(MXU_DTYPE)
    kr = jnp.where(lane < MLA_NOPE, kv_ref[...].astype(F32), _rope_lanes(kpe_ref[...], c, s_up, s_down)).astype(MXU_DTYPE)
    return qr, kr, lane


def _mla_specs(Tp):
    head = pl.BlockSpec((None, Tp, HEAD_LANES), lambda b, h: (b, 0, h))
    shared = pl.BlockSpec((None, Tp, HEAD_LANES), lambda b, h: (b, 0, 0))
    tab = pl.BlockSpec((Tp, HEAD_LANES), lambda b, h: (0, 0))
    lse = pl.BlockSpec((None, None, Tp, 1), lambda b, h: (b, h, 0, 0))
    return head, shared, tab, lse


def _attn_fwd_call(q, kv, kpe, tabs):
    B, Tp, _ = q.shape
    nq = Tp // SEQ_BLOCK

    def body(q_ref, kv_ref, kpe_ref, c_ref, su_ref, sd_ref, o_ref, lse_ref, qr_ref, kr_ref):
        qr, kr, lane = _mla_operands(q_ref, kv_ref, kpe_ref, c_ref[...], su_ref[...], sd_ref[...])
        qr_ref[...] = qr
        kr_ref[...] = kr
        for qi in range(nq):
            L = (qi + 1) * SEQ_BLOCK
            blk = slice(qi * SEQ_BLOCK, L)
            s = _mask_diagonal(lax.dot_general(qr_ref[blk, :], kr_ref[0:L, :], _NT, preferred_element_type=F32), NEG_INF)
            m = jnp.max(s, axis=-1, keepdims=True)
            p = jnp.exp2(s - m)
            l = jnp.sum(p, axis=-1, keepdims=True)
            o = jnp.dot(p.astype(MXU_DTYPE), kv_ref[0:L, :].astype(MXU_DTYPE), preferred_element_type=F32)
            o_ref[blk, :] = jnp.where(lane[blk, :] >= MLA_NOPE, o / l, 0.0)
            lse_ref[blk, :] = m + jnp.log2(l)

    head, shared, tab, lse = _mla_specs(Tp)
    return pl.pallas_call(
        body, name="mla_attn_fwd", grid=(B, MLA_HEADS), in_specs=[head, head, shared, tab, tab, tab], out_specs=[head, lse],
        out_shape=[jax.ShapeDtypeStruct((B, Tp, MLA_HEADS * HEAD_LANES), F32), jax.ShapeDtypeStruct((B, MLA_HEADS, Tp, 1), F32)],
        scratch_shapes=[pltpu.VMEM((Tp, HEAD_LANES), MXU_DTYPE), pltpu.VMEM((Tp, HEAD_LANES), MXU_DTYPE)],
        compiler_params=pltpu.CompilerParams(dimension_semantics=("parallel", "parallel")),
    )(q, kv, kpe, *tabs)


def _attn_bwd_call(q, kv, kpe, tabs, o, lse, do):
    B, Tp, _ = q.shape
    nq = Tp // SEQ_BLOCK

    def body(q_ref, kv_ref, kpe_ref, c_ref, su_ref, sd_ref, o_ref, lse_ref, do_ref, dq_ref, dkv_ref, dkpe_ref,
             qr_ref, kr_ref, dqa_ref, dka_ref, dva_ref):
        c, s_up, s_down = c_ref[...], su_ref[...], sd_ref[...]
        qr, kr, lane = _mla_operands(q_ref, kv_ref, kpe_ref, c, s_up, s_down)
        qr_ref[...] = qr
        kr_ref[...] = kr
        dka_ref[...] = jnp.zeros_like(dka_ref)
        dva_ref[...] = jnp.zeros_like(dva_ref)
        for qi in range(nq):
            L = (qi + 1) * SEQ_BLOCK
            blk = slice(qi * SEQ_BLOCK, L)
            qb = qr_ref[blk, :]
            do = jnp.where(lane[blk, :] >= MLA_NOPE, do_ref[blk, :], 0.0)
            delta = jnp.sum(do * o_ref[blk, :], axis=-1, keepdims=True)
            s = _mask_diagonal(lax.dot_general(qb, kr_ref[0:L, :], _NT, preferred_element_type=F32), NEG_INF)
            p = jnp.exp2(s - lse_ref[blk, :])
            dob = do.astype(MXU_DTYPE)
            dva_ref[0:L, :] += lax.dot_general(p.astype(MXU_DTYPE), dob, _TN, preferred_element_type=F32)
            dp = lax.dot_general(dob, kv_ref[0:L, :].astype(MXU_DTYPE), _NT, preferred_element_type=F32)
            ds = (p * (dp - delta)).astype(MXU_DTYPE)
            dqa_ref[blk, :] = jnp.dot(ds, kr_ref[0:L, :], preferred_element_type=F32)
            dka_ref[0:L, :] += lax.dot_general(ds, qb, _TN, preferred_element_type=F32)
        dq_ref[...] = _unrope_lanes(dqa_ref[...] * _MLA_SCALE, c, s_up, s_down).astype(dq_ref.dtype)
        dk = dka_ref[...] * (1.0 / _LOG2E)
        dkv_ref[...] = jnp.where(lane < MLA_NOPE, dk, dva_ref[...]).astype(dkv_ref.dtype)
        dkpe = jnp.where(lane >= MLA_NOPE, _unrope_lanes(dk, c, s_up, s_down), 0.0)

        @pl.when(pl.program_id(1) == 0)
        def _():
            dkpe_ref[...] = dkpe

        @pl.when(pl.program_id(1) > 0)
        def _():
            dkpe_ref[...] += dkpe

    head, shared, tab, lse_spec = _mla_specs(Tp)
    wide = jax.ShapeDtypeStruct((B, Tp, MLA_HEADS * HEAD_LANES), q.dtype)
    acc = pltpu.VMEM((Tp, HEAD_LANES), F32)
    return pl.pallas_call(
        body, name="mla_attn_bwd", grid=(B, MLA_HEADS),
        in_specs=[head, head, shared, tab, tab, tab, head, lse_spec, head], out_specs=[head, head, shared],
        out_shape=[wide, wide, jax.ShapeDtypeStruct((B, Tp, HEAD_LANES), F32)],
        scratch_shapes=[pltpu.VMEM((Tp, HEAD_LANES), MXU_DTYPE), pltpu.VMEM((Tp, HEAD_LANES), MXU_DTYPE), acc, acc, acc],
        compiler_params=pltpu.CompilerParams(dimension_semantics=("parallel", "arbitrary")),
    )(q, kv, kpe, *tabs, o, lse, do)


@jax.custom_vjp
def mla_attention(q, kv, kpe, tabs):
    return _attn_fwd_call(q, kv, kpe, tabs)[0]


def _mla_attention_fwd(q, kv, kpe, tabs):
    o, lse = _attn_fwd_call(q, kv, kpe, tabs)
    return o, (q, kv, kpe, tabs, o, lse)


def _mla_attention_bwd(res, do):
    q, kv, kpe, tabs, o, lse = res
    dq, dkv, dkpe = _attn_bwd_call(q, kv, kpe, tabs, o, lse, do)
    return dq, dkv, dkpe, None


mla_attention.defvjp(_mla_attention_fwd, _mla_attention_bwd)


def _rope_halves(x, cos, sin):
    half = x.shape[1] // 2
    x1, x2 = x[:, :half], x[:, half:]
    return jnp.concatenate([x1 * cos - x2 * sin, x1 * sin + x2 * cos], axis=1)


def _unrope_halves(d, cos, sin):
    half = d.shape[1] // 2
    d1, d2 = d[:, :half], d[:, half:]
    return jnp.concatenate([d1 * cos + d2 * sin, d2 * cos - d1 * sin], axis=1)


_RET_K_SCALE = RET_QK_DIM ** -0.5
_RET_Q_BLOCKS = RET_HEADS
_RET_V_BLOCK0 = 2 * RET_HEADS * RET_QK_DIM // RET_V_DIM
_RET_G_BLOCK0 = _RET_V_BLOCK0 + RET_HEADS


def _ret_specs(Tp):
    q = pl.BlockSpec((None, Tp, RET_QK_DIM), lambda b, h: (b, 0, h))
    k = pl.BlockSpec((None, Tp, RET_QK_DIM), lambda b, h: (b, 0, _RET_Q_BLOCKS + h))
    v = pl.BlockSpec((None, Tp, RET_V_DIM), lambda b, h: (b, 0, _RET_V_BLOCK0 + h))
    tab = pl.BlockSpec((Tp, RET_QK_DIM // 2), lambda b, h: (0, 0))
    lg = pl.BlockSpec((None, 1, 1), lambda b, h: (h, 0, 0))
    return q, k, v, tab, lg


def _ret_operands(q_ref, k_ref, cos, sin, lg):
    t = lax.broadcasted_iota(jnp.int32, (q_ref.shape[0], 1), 0).astype(F32)
    grow, shrink = jnp.exp(-lg * t), jnp.exp(lg * t)
    qs = (_rope_halves(q_ref[...].astype(F32), cos, sin) * shrink).astype(MXU_DTYPE)
    ks = (_rope_halves(k_ref[...].astype(F32), cos, sin) * (grow * _RET_K_SCALE)).astype(MXU_DTYPE)
    return qs, ks, shrink, grow * _RET_K_SCALE


def _ret_core_fwd_call(p, cos, sin, lg):
    B, Tp, _ = p.shape
    nq = Tp // SEQ_BLOCK

    def body(q_ref, k_ref, v_ref, cos_ref, sin_ref, lg_ref, o_ref, qs_ref, ks_ref):
        qs_ref[...], ks_ref[...], _, _ = _ret_operands(q_ref, k_ref, cos_ref[...], sin_ref[...], lg_ref[...])
        for qi in range(nq):
            L = (qi + 1) * SEQ_BLOCK
            blk = slice(qi * SEQ_BLOCK, L)
            s = _mask_diagonal(lax.dot_general(qs_ref[blk, :], ks_ref[0:L, :], _NT, preferred_element_type=F32), 0.0)
            o_ref[blk, :] = jnp.dot(s.astype(MXU_DTYPE), v_ref[0:L, :].astype(MXU_DTYPE), preferred_element_type=F32)

    q, k, v, tab, lgs = _ret_specs(Tp)
    return pl.pallas_call(
        body, name="retention_fwd", grid=(B, RET_HEADS), in_specs=[q, k, v, tab, tab, lgs],
        out_specs=pl.BlockSpec((None, Tp, RET_V_DIM), lambda b, h: (b, 0, h)),
        out_shape=jax.ShapeDtypeStruct((B, Tp, RET_HEADS * RET_V_DIM), F32),
        scratch_shapes=[pltpu.VMEM((Tp, RET_QK_DIM), MXU_DTYPE), pltpu.VMEM((Tp, RET_QK_DIM), MXU_DTYPE)],
        compiler_params=pltpu.CompilerParams(dimension_semantics=("parallel", "parallel")),
    )(p, p, p, cos, sin, lg)


def _ret_core_bwd_call(p, do, cos, sin, lg):
    B, Tp, _ = p.shape
    nq = Tp // SEQ_BLOCK

    def body(q_ref, k_ref, v_ref, do_ref, cos_ref, sin_ref, lg_ref, dq_ref, dk_ref, dv_ref, qs_ref, ks_ref, dqa_ref, dka_ref, dva_ref):
        cos_, sin_ = cos_ref[...], sin_ref[...]
        qs_ref[...], ks_ref[...], q_scale, k_scale = _ret_operands(q_ref, k_ref, cos_, sin_, lg_ref[...])
        dka_ref[...] = jnp.zeros_like(dka_ref)
        dva_ref[...] = jnp.zeros_like(dva_ref)
        for qi in range(nq):
            L = (qi + 1) * SEQ_BLOCK
            blk = slice(qi * SEQ_BLOCK, L)
            qb = qs_ref[blk, :]
            dob = do_ref[blk, :].astype(MXU_DTYPE)
            s = _mask_diagonal(lax.dot_general(qb, ks_ref[0:L, :], _NT, preferred_element_type=F32), 0.0).astype(MXU_DTYPE)
            dva_ref[0:L, :] += lax.dot_general(s, dob, _TN, preferred_element_type=F32)
            ds = _mask_diagonal(lax.dot_general(dob, v_ref[0:L, :].astype(MXU_DTYPE), _NT, preferred_element_type=F32), 0.0).astype(MXU_DTYPE)
            dqa_ref[blk, :] = jnp.dot(ds, ks_ref[0:L, :], preferred_element_type=F32)
            dka_ref[0:L, :] += lax.dot_general(ds, qb, _TN, preferred_element_type=F32)
        dq_ref[...] = _unrope_halves(dqa_ref[...] * q_scale, cos_, sin_).astype(dq_ref.dtype)
        dk_ref[...] = _unrope_halves(dka_ref[...] * k_scale, cos_, sin_).astype(dk_ref.dtype)
        dv_ref[...] = dva_ref[...].astype(dv_ref.dtype)

    q, k, v, tab, lgs = _ret_specs(Tp)
    qk_out = pl.BlockSpec((None, Tp, RET_QK_DIM), lambda b, h: (b, 0, h))
    v_out = pl.BlockSpec((None, Tp, RET_V_DIM), lambda b, h: (b, 0, h))
    return pl.pallas_call(
        body, name="retention_bwd", grid=(B, RET_HEADS), in_specs=[q, k, v, v_out, tab, tab, lgs],
        out_specs=[qk_out, qk_out, v_out],
        out_shape=[jax.ShapeDtypeStruct((B, Tp, RET_HEADS * RET_QK_DIM), p.dtype), jax.ShapeDtypeStruct((B, Tp, RET_HEADS * RET_QK_DIM), p.dtype),
                   jax.ShapeDtypeStruct((B, Tp, RET_HEADS * RET_V_DIM), p.dtype)],
        scratch_shapes=[pltpu.VMEM((Tp, RET_QK_DIM), MXU_DTYPE), pltpu.VMEM((Tp, RET_QK_DIM), MXU_DTYPE),
                        pltpu.VMEM((Tp, RET_QK_DIM), F32), pltpu.VMEM((Tp, RET_QK_DIM), F32), pltpu.VMEM((Tp, RET_V_DIM), F32)],
        compiler_params=pltpu.CompilerParams(dimension_semantics=("parallel", "parallel")),
    )(p, p, p, do, cos, sin, lg)


def _ret_gate_specs(M):
    tm = _pick(M, 1088, 8)
    head = pl.BlockSpec((tm, RET_V_DIM), lambda i, h: (i, h))
    gate = pl.BlockSpec((tm, RET_V_DIM), lambda i, h: (i, _RET_G_BLOCK0 + h))
    return tm, head, gate


def _ret_gate_fwd_call(o, p2d):
    M = o.shape[0]
    tm, head, gate = _ret_gate_specs(M)

    def body(o_ref, g_ref, y_ref):
        ov = o_ref[...]
        gv = g_ref[...].astype(F32)
        rstd = lax.rsqrt(jnp.mean(ov * ov, axis=-1, keepdims=True) + EPS)
        y_ref[...] = (gv * _sigmoid(gv)) * (ov * rstd)

    return pl.pallas_call(
        body, name="retention_gate_fwd", grid=(M // tm, RET_HEADS), in_specs=[head, gate], out_specs=head,
        out_shape=jax.ShapeDtypeStruct(o.shape, F32),
        compiler_params=pltpu.CompilerParams(dimension_semantics=("parallel", "parallel")),
    )(o, p2d)


def _ret_gate_bwd_call(o, p2d, dy):
    M = o.shape[0]
    tm, head, gate = _ret_gate_specs(M)

    def body(o_ref, g_ref, dy_ref, do_ref, dg_ref):
        ov = o_ref[...]
        gv = g_ref[...].astype(F32)
        dy = dy_ref[...]
        rstd = lax.rsqrt(jnp.mean(ov * ov, axis=-1, keepdims=True) + EPS)
        on = ov * rstd
        sg = _sigmoid(gv)
        dg_ref[...] = (dy * on * (sg * (1.0 + gv * (1.0 - sg)))).astype(dg_ref.dtype)
        don = dy * (gv * sg)
        do_ref[...] = (rstd * (don - on * jnp.mean(don * on, axis=-1, keepdims=True))).astype(do_ref.dtype)

    shp = jax.ShapeDtypeStruct(o.shape, p2d.dtype)
    return pl.pallas_call(
        body, name="retention_gate_bwd", grid=(M // tm, RET_HEADS), in_specs=[head, gate, head], out_specs=[head, head],
        out_shape=[shp, shp],
        compiler_params=pltpu.CompilerParams(dimension_semantics=("parallel", "parallel")),
    )(o, p2d, dy)


def _log_gamma():
    return jnp.log(1.0 - 2.0 ** (-5.0 - jnp.arange(RET_HEADS, dtype=F32))).reshape(RET_HEADS, 1, 1)


@jax.custom_vjp
def retention_mixer(p, cos, sin):
    B, Tp, W = p.shape
    o = _ret_core_fwd_call(p, cos, sin, _log_gamma())
    return _ret_gate_fwd_call(o.reshape(B * Tp, -1), p.reshape(B * Tp, W))


def _retention_mixer_fwd(p, cos, sin):
    B, Tp, W = p.shape
    o = _ret_core_fwd_call(p, cos, sin, _log_gamma())
    return _ret_gate_fwd_call(o.reshape(B * Tp, -1), p.reshape(B * Tp, W)), (p, o, cos, sin)


def _retention_mixer_bwd(res, dy):
    p, o, cos, sin = res
    B, Tp, W = p.shape
    do, dg = _ret_gate_bwd_call(o.reshape(B * Tp, -1), p.reshape(B * Tp, W), dy)
    dq, dk, dv = _ret_core_bwd_call(p, do.reshape(B, Tp, -1), cos, sin, _log_gamma())
    return jnp.concatenate([dq, dk, dv, dg.reshape(B, Tp, -1)], axis=-1), None, None


retention_mixer.defvjp(_retention_mixer_fwd, _retention_mixer_bwd)


def _heads_to_lanes(w):
    K = w.shape[0]
    w = w.reshape(K, MLA_HEADS, MLA_NOPE + MLA_ROPE)
    return jnp.pad(w, ((0, 0), (0, 0), (0, HEAD_LANES - MLA_NOPE - MLA_ROPE))).reshape(K, MLA_HEADS * HEAD_LANES)


def _out_rows_to_lanes(w):
    N = w.shape[1]
    att = w[LRU_WIDTH:].reshape(MLA_HEADS, MLA_V, N)
    att = jnp.pad(att, ((0, 0), (HEAD_LANES - MLA_V, 0), (0, 0))).reshape(MLA_HEADS * HEAD_LANES, N)
    return jnp.concatenate([w[:LRU_WIDTH], att], axis=0)


def _seq_dims(x):
    B, S, D = x.shape
    T = S + N_META
    Tp = _round_up(T, SEQ_BLOCK)
    return B, S, T, Tp


def _mixer0(diff, w, token):
    x = diff["x"]
    B, S, T, Tp = _seq_dims(x)
    D = x.shape[-1]
    M = B * Tp
    pos = jnp.arange(Tp, dtype=jnp.int32)

    def mm(a, name, act=False, out_dtype=F32, layout=lambda m: m, col_shards=1):
        return matmul(a, layout(w[name]), layout(diff[name]), act, name, out_dtype, col_shards)

    meta = jnp.broadcast_to(diff["meta_tokens"][None], (B, N_META, D))
    h = jnp.concatenate([meta, x + token, jnp.zeros((B, Tp - T, D), F32)], axis=1).reshape(M, D)
    p = mm(h, "ev_w_in")
    sp = jax.nn.softplus(-diff["ev_lru_lambda"]).reshape(1, LRU_WIDTH)
    y_rec, qn, kvn, p_kpe = even_front(
        p.reshape(B, Tp, -1), diff["ev_conv_w"].reshape(CONV_WIDTH, LRU_WIDTH), diff["ev_conv_b"].reshape(1, LRU_WIDTH),
        diff["ev_w_rg_a"].reshape(LRU_HEADS, LRU_HEAD_DIM, LRU_HEAD_DIM), diff["ev_b_rg_a"].reshape(1, LRU_WIDTH),
        diff["ev_w_rg_x"].reshape(LRU_HEADS, LRU_HEAD_DIM, LRU_HEAD_DIM), diff["ev_b_rg_x"].reshape(1, LRU_WIDTH),
        sp, diff["ev_q_norm_g"].reshape(-1), diff["ev_kv_norm_g"].reshape(-1))
    y_rec = y_rec.reshape(M, LRU_WIDTH)
    q = mm(qn, "ev_w_uq", out_dtype=MXU_DTYPE, layout=_heads_to_lanes).reshape(B, Tp, -1)
    kv = mm(kvn, "ev_w_ukv", out_dtype=MXU_DTYPE).reshape(B, Tp, -1)
    kpe = jnp.pad(p_kpe.reshape(B, Tp, MLA_ROPE), ((0, 0), (0, 0), (MLA_NOPE, HEAD_LANES - MLA_NOPE - MLA_ROPE)))
    y_att = mla_attention(q, kv, kpe, _mla_rope_tables(pos)).reshape(M, -1)
    mix = mm(jnp.concatenate([y_rec, y_att], axis=-1), "ev_w_out", layout=_out_rows_to_lanes)
    return deepnorm(h, mix, diff["ln_mix_g"], diff["ln_mix_b"], "ln_mix0")


def _mlp0(diff, h, w):
    f = mlp(h, w["mlp_w1_0"], w["mlp_w2_0"], diff["mlp_w1_0"], diff["mlp_w2_0"], "mlp0")
    return deepnorm(h, f, diff["ln_mlp_g"], diff["ln_mlp_b"], "ln_mlp0")


def _layer1_loss(diff, h, w, tgt):
    B, S, T, Tp = _seq_dims(tgt)
    D = tgt.shape[-1]
    pos = jnp.arange(Tp, dtype=jnp.int32)

    def mm(a, name, out_dtype=F32, col_shards=1):
        return matmul(a, w[name], diff[name], False, name, out_dtype, col_shards)

    p = mm(h, "od_w_in", out_dtype=MXU_DTYPE, col_shards=N_CHIPS)
    cos, sin = _rope_tables(pos, RET_QK_DIM // 2)
    mix = mm(retention_mixer(p.reshape(B, Tp, -1), cos, sin), "od_w_out")
    h = deepnorm(h, mix, diff["ln_mix_g"], diff["ln_mix_b"], "ln_mix1")
    f = mlp(h, w["mlp_w1_1"], w["mlp_w2_1"], diff["mlp_w1_1"], diff["mlp_w2_1"], "mlp1")
    h = deepnorm(h, f, diff["ln_mlp_g"], diff["ln_mlp_b"], "ln_mlp1")
    y = h.reshape(B, Tp, D)[:, N_META:T].reshape(B * S, D)
    return loss_head(y, tgt.reshape(B * S, D))


_HBM = pl.BlockSpec(memory_space=pltpu.HBM)


def _place():
    return lax.axis_index("x"), lax.axis_index("y"), lax.axis_index("c")


def _other_chips(x, y):
    return [(1 - x, y), (x, 1 - y), (1 - x, 1 - y)]


def _chunks(rows, sublanes, most):
    for q in range(most, 0, -1):
        if rows % (q * sublanes) == 0:
            return q
    return 1


def _sublanes(dtype):
    return 8 * 4 // jnp.dtype(dtype).itemsize


def _allgather_chips(buf, name):
    R, C = buf.shape
    Rh = R // 2
    Q = _chunks(Rh, _sublanes(buf.dtype), 4)
    ch = Rh // Q

    def body(x_ref, out_ref, send_sems, recv_sems):
        x, y, c = _place()
        sibling = (x, y, 1 - c)
        chips = _other_chips(x, y)

        def piece(cx, cy, hc, q):
            return out_ref.at[2 * cx + cy, pl.ds(hc * Rh + q * ch, ch), :]

        def copy(k, src, dst, to):
            return pltpu.make_async_remote_copy(src_ref=src, dst_ref=dst, send_sem=send_sems.at[k], recv_sem=recv_sems.at[k],
                                                device_id=to, device_id_type=MESH)

        first = [copy(j * Q + q, x_ref.at[pl.ds(c * Rh + q * ch, ch), :], piece(x, y, c, q), (*chip, c))
                 for q in range(Q) for j, chip in enumerate(chips)]
        for cp in first:
            cp.start()
        passed = []
        for q in range(Q):
            for j, chip in enumerate(chips):
                landed = piece(*chip, c, q)
                copy(j * Q + q, landed, landed, sibling).wait_recv()
                fwd = copy(3 * Q + j * Q + q, landed, landed, sibling)
                fwd.start()
                passed.append(fwd)
        for q in range(Q):
            for j, chip in enumerate(chips):
                theirs = piece(*chip, 1 - c, q)
                copy(3 * Q + j * Q + q, theirs, theirs, sibling).wait_recv()
        for cp in first + passed:
            cp.wait_send()

    return pl.pallas_call(
        body, name=name, in_specs=[_HBM], out_specs=_HBM,
        out_shape=jax.ShapeDtypeStruct((N_CHIPS, R, C), buf.dtype),
        scratch_shapes=[pltpu.SemaphoreType.DMA((6 * Q,)), pltpu.SemaphoreType.DMA((6 * Q,))],
    )(buf)


def _with_own(gathered, own):
    my = 2 * lax.axis_index("x") + lax.axis_index("y")
    return lax.dynamic_update_slice(gathered, own[None], (my, 0, 0))


def _sibling_exchange(ps, name):
    n = len(ps)

    def body(*refs):
        p_refs, out_refs, (send_sems, recv_sems) = refs[:n], refs[n:2 * n], refs[2 * n:]
        x, y, c = _place()
        copies = [pltpu.make_async_remote_copy(src_ref=p_ref.at[j, 1 - c], dst_ref=out_ref.at[j], send_sem=send_sems.at[N_CHIPS * i + j],
                                               recv_sem=recv_sems.at[N_CHIPS * i + j], device_id=(x, y, 1 - c), device_id_type=MESH)
                  for i, (p_ref, out_ref) in enumerate(zip(p_refs, out_refs)) for j in range(N_CHIPS)]
        for cp in copies:
            cp.start()
        for cp in copies:
            cp.wait()

    return pl.pallas_call(
        body, name=name, in_specs=[_HBM] * n, out_specs=[_HBM] * n,
        out_shape=[jax.ShapeDtypeStruct((N_CHIPS,) + p.shape[2:], p.dtype) for p in ps],
        scratch_shapes=[pltpu.SemaphoreType.DMA((N_CHIPS * n,)), pltpu.SemaphoreType.DMA((N_CHIPS * n,))],
    )(*ps)


def _chip_scatter(ss, name):
    n = len(ss)

    def body(*refs):
        s_refs, t_refs, (send_sems, recv_sems) = refs[:n], refs[n:2 * n], refs[2 * n:]
        x, y, c = _place()
        copies = [pltpu.make_async_remote_copy(src_ref=s_ref.at[j + 1], dst_ref=t_ref.at[j], send_sem=send_sems.at[3 * i + j],
                                               recv_sem=recv_sems.at[3 * i + j], device_id=(cx, cy, c), device_id_type=MESH)
                  for i, (s_ref, t_ref) in enumerate(zip(s_refs, t_refs)) for j, (cx, cy) in enumerate(_other_chips(x, y))]
        for cp in copies:
            cp.start()
        for cp in copies:
            cp.wait()

    return pl.pallas_call(
        body, name=name, in_specs=[_HBM] * n, out_specs=[_HBM] * n,
        out_shape=[jax.ShapeDtypeStruct((3,) + s.shape[1:], s.dtype) for s in ss],
        scratch_shapes=[pltpu.SemaphoreType.DMA((3 * n,)), pltpu.SemaphoreType.DMA((3 * n,))],
    )(*ss)


def _sibling_gather(fs, name):
    n = len(fs)

    def body(*refs):
        out_refs, (send_sems, recv_sems) = refs[n:2 * n], refs[2 * n:]
        x, y, c = _place()
        copies = [pltpu.make_async_remote_copy(src_ref=out_ref.at[c], dst_ref=out_ref.at[c], send_sem=send_sems.at[i], recv_sem=recv_sems.at[i],
                                               device_id=(x, y, 1 - c), device_id_type=MESH) for i, out_ref in enumerate(out_refs)]
        for cp in copies:
            cp.start()
        for cp in copies:
            cp.wait()

    return pl.pallas_call(
        body, name=name, in_specs=[_HBM] * n, out_specs=[_HBM] * n,
        out_shape=[jax.ShapeDtypeStruct(f.shape, f.dtype) for f in fs], input_output_aliases={i: i for i in range(n)},
        scratch_shapes=[pltpu.SemaphoreType.DMA((n,)), pltpu.SemaphoreType.DMA((n,))],
    )(*fs)


def _axis_scalar(name):
    return lax.axis_index(name).astype(jnp.int32).reshape(1)


def _add_own_half(p, got, out_dtype, name):
    n, _, R, C = p.shape
    tr = _pick(R, 512, 16)

    def body(x_ref, y_ref, c_ref, p_ref, g_ref, o_ref):
        o_ref[...] = (p_ref[...] + g_ref[...]).astype(out_dtype)

    def chip(r, x_ref, y_ref):
        return 2 * (x_ref[0] ^ (r & 1)) + (y_ref[0] ^ (r >> 1))

    grid_spec = pltpu.PrefetchScalarGridSpec(
        num_scalar_prefetch=3, grid=(n, R // tr),
        in_specs=[pl.BlockSpec((None, None, tr, C), lambda r, i, x_ref, y_ref, c_ref: (chip(r, x_ref, y_ref), c_ref[0], i, 0)),
                  pl.BlockSpec((None, tr, C), lambda r, i, x_ref, y_ref, c_ref: (chip(r, x_ref, y_ref), i, 0))],
        out_specs=pl.BlockSpec((None, tr, C), lambda r, i, x_ref, y_ref, c_ref: (r, i, 0)))
    return pl.pallas_call(body, name=name, grid_spec=grid_spec, out_shape=jax.ShapeDtypeStruct((n, R, C), out_dtype),
                          compiler_params=pltpu.CompilerParams(dimension_semantics=("parallel", "parallel")))(
        _axis_scalar("x"), _axis_scalar("y"), _axis_scalar("c"), p, got)


def _sum_partials(s, t, name):
    _, R, C = s.shape
    tr = _pick(R, 512, 16)

    def body(c_ref, s_ref, t_ref, o_ref):
        acc = s_ref[...].astype(F32)
        for j in range(3):
            acc = acc + t_ref[j].astype(F32)
        o_ref[...] = acc

    grid_spec = pltpu.PrefetchScalarGridSpec(
        num_scalar_prefetch=1, grid=(R // tr,),
        in_specs=[pl.BlockSpec((None, tr, C), lambda i, c_ref: (0, i, 0)), pl.BlockSpec((3, tr, C), lambda i, c_ref: (0, i, 0))],
        out_specs=pl.BlockSpec((None, tr, C), lambda i, c_ref: (c_ref[0], i, 0)))
    return pl.pallas_call(body, name=name, grid_spec=grid_spec, out_shape=jax.ShapeDtypeStruct((2, R, C), F32),
                          compiler_params=pltpu.CompilerParams(dimension_semantics=("parallel",)))(_axis_scalar("c"), s, t)


def _sibling_reduce(ps, wire_dtypes, tag):
    got = _sibling_exchange(ps, "grad_sibling_exchange_" + tag)
    return [_add_own_half(p, g, dt, "grad_sibling_add_%s%d" % (tag, i)) for i, (p, g, dt) in enumerate(zip(ps, got, wire_dtypes))]


_SEM = pl.BlockSpec(memory_space=pltpu.SEMAPHORE)
_ANY = pl.BlockSpec(memory_space=pl.ANY)
_EFFECT = pltpu.SideEffectType.DATAFLOW_SIDE_EFFECTING


def _in_hbm(a):
    return pltpu.with_memory_space_constraint(a, pltpu.HBM)


def _half_copies(x_ref, land_ref, send_sems, recv_sems, Rh, arriving):
    x, y, c = _place()
    rows = pl.ds(c * Rh, Rh)
    return [pltpu.make_async_remote_copy(src_ref=x_ref.at[rows, :], dst_ref=land_ref.at[2 * cx + cy if arriving else 2 * x + y, rows, :],
                                         send_sem=send_sems.at[j], recv_sem=recv_sems.at[j], device_id=(cx, cy, c), device_id_type=MESH)
            for j, (cx, cy) in enumerate(_other_chips(x, y))]


def _allgather_start(buf, name):
    R, C = buf.shape

    def body(x_ref, land_ref, send_sems, recv_sems, x_thru, land_thru, token):
        for cp in _half_copies(x_ref, land_ref, send_sems, recv_sems, R // 2, False):
            cp.start()
        token[...] = jnp.zeros_like(token)

    send_sems, recv_sems, x_thru, land_thru, token = pl.pallas_call(
        body, name=name,
        out_shape=(pltpu.SemaphoreType.DMA((3,)), pltpu.SemaphoreType.DMA((3,)), pltpu.HBM(buf.shape, buf.dtype),
                   pltpu.HBM((N_CHIPS, R, C), buf.dtype), jax.ShapeDtypeStruct((8, 128), F32)),
        in_specs=(_HBM, _HBM), out_specs=(_SEM, _SEM, _HBM, _HBM, pl.BlockSpec(memory_space=pltpu.VMEM)),
        input_output_aliases={0: 2, 1: 3}, compiler_params=pltpu.CompilerParams(has_side_effects=_EFFECT),
    )(_in_hbm(buf), _in_hbm(lax.empty((N_CHIPS, R, C), buf.dtype)))
    return (send_sems, recv_sems, x_thru, land_thru), token[0, 0]


def _allgather_wait(pending, after, name):
    send_sems, recv_sems, x_thru, land_thru = pending
    R = x_thru.shape[0]

    def body(x_ref, land_ref, send_sems, recv_sems, after_ref, x_dead, got_ref):
        for cp in _half_copies(x_ref, land_ref, send_sems, recv_sems, R // 2, False):
            cp.wait_send()
        for cp in _half_copies(x_ref, land_ref, send_sems, recv_sems, R // 2, True):
            cp.wait_recv()

    return pl.pallas_call(
        body, name=name, out_shape=(pltpu.HBM(x_thru.shape, x_thru.dtype), pltpu.HBM(land_thru.shape, land_thru.dtype)),
        in_specs=(_HBM, _HBM, _SEM, _SEM, _ANY), out_specs=(_HBM, _HBM), input_output_aliases={0: 0, 1: 1},
        compiler_params=pltpu.CompilerParams(has_side_effects=_EFFECT),
    )(x_thru, land_thru, send_sems, recv_sems, after)[1]


def _sibling_forward(land, name):
    _, R, C = land.shape
    Rh = R // 2
    Q = _chunks(Rh, _sublanes(land.dtype), 4)
    ch = Rh // Q

    def body(in_ref, out_ref, send_sems, recv_sems):
        x, y, c = _place()
        copies = []
        for j, (cx, cy) in enumerate(_other_chips(x, y)):
            for q in range(Q):
                rows = out_ref.at[2 * cx + cy, pl.ds(c * Rh + q * ch, ch), :]
                copies.append(pltpu.make_async_remote_copy(src_ref=rows, dst_ref=rows, send_sem=send_sems.at[j * Q + q],
                                                           recv_sem=recv_sems.at[j * Q + q], device_id=(x, y, 1 - c), device_id_type=MESH))
        for cp in copies:
            cp.start()
        for cp in copies:
            cp.wait_send()
        for j, (cx, cy) in enumerate(_other_chips(x, y)):
            for q in range(Q):
                rows = out_ref.at[2 * cx + cy, pl.ds((1 - c) * Rh + q * ch, ch), :]
                pltpu.make_async_remote_copy(src_ref=rows, dst_ref=rows, send_sem=send_sems.at[j * Q + q], recv_sem=recv_sems.at[j * Q + q],
                                             device_id=(x, y, 1 - c), device_id_type=MESH).wait_recv()

    return pl.pallas_call(
        body, name=name, in_specs=[_HBM], out_specs=_HBM, out_shape=jax.ShapeDtypeStruct(land.shape, land.dtype),
        input_output_aliases={0: 0},
        scratch_shapes=[pltpu.SemaphoreType.DMA((3 * Q,)), pltpu.SemaphoreType.DMA((3 * Q,))],
    )(land)


N_PEERS = 7


def _direct_copies(p_refs, t_refs, send_sems, recv_sems):
    x, y, c = _place()
    copies = []
    for i, (p_ref, t_ref) in enumerate(zip(p_refs, t_refs)):
        for f in range(1, N_PEERS + 1):
            px, py, pc = x ^ (f >> 2), y ^ ((f >> 1) & 1), c ^ (f & 1)
            copies.append(pltpu.make_async_remote_copy(
                src_ref=p_ref.at[2 * px + py, pc], dst_ref=t_ref.at[f - 1], send_sem=send_sems.at[N_PEERS * i + f - 1],
                recv_sem=recv_sems.at[N_PEERS * i + f - 1], device_id=(px, py, pc), device_id_type=MESH))
    return copies


def _direct_scatter_start(ps, name):
    n = len(ps)

    def body(*refs):
        p_refs, t_refs, (send_sems, recv_sems), token = refs[:n], refs[n:2 * n], refs[2 * n:2 * n + 2], refs[-1]
        for cp in _direct_copies(p_refs, t_refs, send_sems, recv_sems):
            cp.start()
        token[...] = jnp.zeros_like(token)

    lands = [lax.empty((N_PEERS,) + p.shape[2:], p.dtype) for p in ps]
    out = pl.pallas_call(
        body, name=name,
        out_shape=(pltpu.SemaphoreType.DMA((N_PEERS * n,)), pltpu.SemaphoreType.DMA((N_PEERS * n,)),
                   *[pltpu.HBM(a.shape, a.dtype) for a in ps + lands], jax.ShapeDtypeStruct((8, 128), F32)),
        in_specs=[_HBM] * (2 * n), out_specs=(_SEM, _SEM, *[_HBM] * (2 * n), pl.BlockSpec(memory_space=pltpu.VMEM)),
        input_output_aliases={i: 2 + i for i in range(2 * n)}, compiler_params=pltpu.CompilerParams(has_side_effects=_EFFECT),
    )(*[_in_hbm(a) for a in ps + lands])
    return (out[0], out[1], list(out[2:2 + n]), list(out[2 + n:2 + 2 * n])), out[-1][0, 0]


def _direct_scatter_wait(pending, after, name):
    send_sems, recv_sems, ps, lands = pending
    n = len(ps)

    def body(*refs):
        p_refs, t_refs, send_sems, recv_sems = refs[:n], refs[n:2 * n], refs[2 * n], refs[2 * n + 1]
        for cp in _direct_copies(p_refs, t_refs, send_sems, recv_sems):
            cp.wait_send()
            cp.wait_recv()

    out = pl.pallas_call(
        body, name=name, out_shape=tuple(pltpu.HBM(a.shape, a.dtype) for a in ps + lands),
        in_specs=[_HBM] * (2 * n) + [_SEM, _SEM, _ANY], out_specs=tuple([_HBM] * (2 * n)),
        input_output_aliases={i: i for i in range(2 * n)}, compiler_params=pltpu.CompilerParams(has_side_effects=_EFFECT),
    )(*ps, *lands, send_sems, recv_sems, after)
    return list(out[:n]), list(out[n:])


def _sum_direct(p, t, name):
    _, _, R, C = p.shape
    tr = _pick(R, 512, 16)

    def body(x_ref, y_ref, c_ref, p_ref, t_ref, o_ref):
        acc = p_ref[...].astype(F32)
        for f in range(N_PEERS):
            acc = acc + t_ref[f].astype(F32)
        o_ref[...] = acc

    grid_spec = pltpu.PrefetchScalarGridSpec(
        num_scalar_prefetch=3, grid=(R // tr,),
        in_specs=[pl.BlockSpec((None, None, tr, C), lambda i, x_ref, y_ref, c_ref: (2 * x_ref[0] + y_ref[0], c_ref[0], i, 0)),
                  pl.BlockSpec((N_PEERS, tr, C), lambda i, x_ref, y_ref, c_ref: (0, i, 0))],
        out_specs=pl.BlockSpec((None, tr, C), lambda i, x_ref, y_ref, c_ref: (c_ref[0], i, 0)))
    return pl.pallas_call(body, name=name, grid_spec=grid_spec, out_shape=jax.ShapeDtypeStruct((2, R, C), F32),
                          compiler_params=pltpu.CompilerParams(dimension_semantics=("parallel",)))(
        _axis_scalar("x"), _axis_scalar("y"), _axis_scalar("c"), p, t)


def _adamw(w, g, m, v, name):
    R, C = w.shape
    tr = _pick(R, 256, 8)

    def body(w_ref, g_ref, m_ref, v_ref, d_ref, nm_ref, nv_ref):
        g_ = g_ref[...]
        m_ = ADAM_B1 * m_ref[...] + (1.0 - ADAM_B1) * g_
        v_ = ADAM_B2 * v_ref[...] + (1.0 - ADAM_B2) * (g_ * g_)
        m_hat = m_ / (1.0 - ADAM_B1 ** ADAM_STEP)
        v_hat = v_ / (1.0 - ADAM_B2 ** ADAM_STEP)
        d_ref[...] = -ADAM_LR * (m_hat / (jnp.sqrt(v_hat) + ADAM_EPS) + ADAM_WD * w_ref[...])
        nm_ref[...] = m_
        nv_ref[...] = v_

    row = pl.BlockSpec((tr, C), lambda i: (i, 0))
    shp = jax.ShapeDtypeStruct((R, C), F32)
    return pl.pallas_call(body, name=name, grid=(R // tr,), in_specs=[row] * 4, out_specs=[row] * 3, out_shape=[shp] * 3,
                          compiler_params=pltpu.CompilerParams(dimension_semantics=("parallel",)))(w, g, m, v)


BIG_SPECS = (("ev_w_in", 1024, 1440, 1), ("ev_w_uq", 256, 768, 1), ("ev_w_ukv", 128, 1024, 1), ("ev_w_out", 1024, 1024, 0),
             ("od_w_in", 1024, 6144, 1), ("od_w_out", 2048, 1024, 0), ("mlp_w1_0", 1024, 4096, 1), ("mlp_w1_1", 1024, 4096, 1),
             ("mlp_w2_0", 4096, 1024, 0), ("mlp_w2_1", 4096, 1024, 0))
BIG_PARAMS = (("ev_w_in", ("ev_w_in",)), ("ev_w_uq", ("ev_w_uq",)), ("ev_w_ukv", ("ev_w_ukv",)), ("ev_w_out", ("ev_w_out",)),
              ("od_w_in", ("od_w_in",)), ("od_w_out", ("od_w_out",)), ("mlp_w1", ("mlp_w1_0", "mlp_w1_1")),
              ("mlp_w2", ("mlp_w2_0", "mlp_w2_1")))
REPLICATED = ("ev_conv_b", "ev_w_rg_a", "ev_b_rg_a", "ev_w_rg_x", "ev_b_rg_x", "ev_lru_lambda", "ev_q_norm_g", "ev_kv_norm_g",
              "ln_mix_g", "ln_mix_b", "ln_mlp_g", "ln_mlp_b")
SMALL_SHARDED = ("meta_tokens", "ev_conv_w")
COL_SHARD_GRADS = ("od_w_in", "mlp_w1_0", "mlp_w1_1")
MATRIX_GROUPS = (("ev_w_in", "ev_w_uq", "ev_w_ukv", "ev_w_out"), ("mlp_w1_0", "mlp_w2_0"), ("od_w_in", "od_w_out", "mlp_w1_1", "mlp_w2_1"))
LAYER_NORMS = ("ln_mix_g", "ln_mix_b", "ln_mlp_g", "ln_mlp_b")
WEIGHT_NAMES = ("meta_tokens", "ev_w_in", "ev_conv_w", "ev_conv_b", "ev_w_rg_a", "ev_b_rg_a", "ev_w_rg_x", "ev_b_rg_x",
                "ev_lru_lambda", "ev_q_norm_g", "ev_w_uq", "ev_kv_norm_g", "ev_w_ukv", "ev_w_out", "od_w_in", "od_w_out",
                "ln_mix_g", "ln_mix_b", "mlp_w1", "mlp_w2", "ln_mlp_g", "ln_mlp_b")


def _to_rows(flat, row_align):
    n = flat.shape[-1]
    rows = _round_up(-(-n // PACK_COLS), row_align)
    pad = rows * PACK_COLS - n
    if pad:
        flat = jnp.pad(flat, [(0, 0)] * (flat.ndim - 1) + [(0, pad)])
    return flat.reshape(flat.shape[:-1] + (rows, PACK_COLS))


def _shard_shape(K, N, axis):
    return (K // N_CHIPS, N) if axis == 0 else (K, N // N_CHIPS)


def _gather_shards(stacked, K, N, axis):
    if axis == 0:
        return stacked.reshape(K, N)
    return stacked.transpose(1, 0, 2).reshape(K, N)


def _split_shards(full, K, N, axis):
    if axis == 0:
        return full.reshape(N_CHIPS, -1)
    return full.reshape(K, N_CHIPS, N // N_CHIPS).transpose(1, 0, 2).reshape(N_CHIPS, -1)


def kernel(x, meta_tokens, ev_w_in, ev_conv_w, ev_conv_b, ev_w_rg_a, ev_b_rg_a, ev_w_rg_x, ev_b_rg_x, ev_lru_lambda, ev_q_norm_g, ev_w_uq, ev_kv_norm_g, ev_w_ukv, ev_w_out, od_w_in, od_w_out, ln_mix_g, ln_mix_b, mlp_w1, mlp_w2, ln_mlp_g, ln_mlp_b, loss_target, m_meta_tokens, m_ev_w_in, m_ev_conv_w, m_ev_conv_b, m_ev_w_rg_a, m_ev_b_rg_a, m_ev_w_rg_x, m_ev_b_rg_x, m_ev_lru_lambda, m_ev_q_norm_g, m_ev_w_uq, m_ev_kv_norm_g, m_ev_w_ukv, m_ev_w_out, m_od_w_in, m_od_w_out, m_ln_mix_g, m_ln_mix_b, m_mlp_w1, m_mlp_w2, m_ln_mlp_g, m_ln_mlp_b, v_meta_tokens, v_ev_w_in, v_ev_conv_w, v_ev_conv_b, v_ev_w_rg_a, v_ev_b_rg_a, v_ev_w_rg_x, v_ev_b_rg_x, v_ev_lru_lambda, v_ev_q_norm_g, v_ev_w_uq, v_ev_kv_norm_g, v_ev_w_ukv, v_ev_w_out, v_od_w_in, v_od_w_out, v_ln_mix_g, v_ln_mix_b, v_mlp_w1, v_mlp_w2, v_ln_mlp_g, v_ln_mlp_b):
    given = dict(locals())
    local_big = {"ev_w_in": ev_w_in[0], "ev_w_uq": ev_w_uq[0], "ev_w_ukv": ev_w_ukv[0], "ev_w_out": ev_w_out[0],
                 "od_w_in": od_w_in[0], "od_w_out": od_w_out[0], "mlp_w1_0": mlp_w1[0], "mlp_w1_1": mlp_w1[1],
                 "mlp_w2_0": mlp_w2[0], "mlp_w2_1": mlp_w2[1]}

    specs = {spec[0]: spec for spec in BIG_SPECS}

    def pack(names):
        return _to_rows(jnp.concatenate([local_big[n].astype(MXU_DTYPE).reshape(-1) for n in names]), 256)

    def unpack(gathered, names):
        gathered, out, off = gathered.reshape(N_CHIPS, -1), {}, 0
        for n in names:
            _, K, N, ax = specs[n]
            shard = _shard_shape(K, N, ax)
            out[n] = _gather_shards(gathered[:, off:off + math.prod(shard)].reshape((N_CHIPS,) + shard), K, N, ax)
            off += math.prod(shard)
        return out

    packed = [pack(names) for names in MATRIX_GROUPS]
    gathered0 = _with_own(_allgather_chips(packed[0], "weight_allgather_mixer0"), packed[0])
    pending1, token1 = _allgather_start(packed[1], "weight_allgather_mlp0_start")
    pending2, token2 = _allgather_start(packed[2], "weight_allgather_layer1_start")
    small = _to_rows(jnp.concatenate([meta_tokens.reshape(-1), ev_conv_w.reshape(-1)]), 16)
    small = _with_own(_allgather_chips(small, "small_allgather"), small).reshape(N_CHIPS, -1)
    n_meta, n_conv = meta_tokens.size, ev_conv_w.size
    meta_full = _gather_shards(small[:, :n_meta].reshape(N_CHIPS, N_META, D_MODEL // N_CHIPS), N_META, D_MODEL, 1)
    conv_full = _gather_shards(small[:, n_meta:n_meta + n_conv].reshape(N_CHIPS, CONV_WIDTH, LRU_WIDTH // N_CHIPS),
                               CONV_WIDTH, LRU_WIDTH, 1)

    def slots(names, dtype):
        return {n: jnp.zeros((N_CHIPS, specs[n][1], specs[n][2] // N_CHIPS) if n in COL_SHARD_GRADS else specs[n][1:3], dtype) for n in names}

    def norms(names, layer):
        return {n: given[n][layer] for n in names}

    def finish_gather(pending, own, after, names, tag):
        landed = _allgather_wait(pending, lax.stop_gradient(after), "weight_allgather_%s_wait" % tag)
        return unpack(_with_own(_sibling_forward(landed, "weight_allgather_%s_forward" % tag), own), names)

    mixer0_m, mlp0_m, layer1_m = MATRIX_GROUPS
    diff_a = {**slots(mixer0_m, F32), **norms(("ln_mix_g", "ln_mix_b"), 0), **{n: given[n] for n in REPLICATED if n not in LAYER_NORMS},
              "x": x, "meta_tokens": meta_full, "ev_conv_w": conv_full}
    diff_b = {**slots(mlp0_m, MXU_DTYPE), **norms(("ln_mlp_g", "ln_mlp_b"), 0)}
    diff_c = {**slots(layer1_m, MXU_DTYPE), **norms(LAYER_NORMS, 1)}
    w_a = unpack(gathered0, mixer0_m)
    h_a, back_a = jax.vjp(lambda d: _mixer0(d, w_a, token1 + token2), diff_a)
    w_b = finish_gather(pending1, packed[1], h_a, mlp0_m, "mlp0")
    h_b, back_b = jax.vjp(lambda d, hh: _mlp0(d, hh, w_b), diff_b, h_a)
    w_c = finish_gather(pending2, packed[2], h_b, layer1_m, "layer1")
    loss, back_c = jax.vjp(lambda d, hh: _layer1_loss(d, hh, w_c, loss_target), diff_c, h_b)
    loss = lax.psum(loss, ("x", "y", "c"))

    def blocks_of(grad, n):
        _, K, N, ax = specs[n]
        if n in COL_SHARD_GRADS:
            blocks = grad
        elif ax == 0:
            blocks = grad.reshape(N_CHIPS, K // N_CHIPS, N)
        else:
            blocks = grad.reshape(K, N_CHIPS, N // N_CHIPS).transpose(1, 0, 2)
        return blocks.reshape(N_CHIPS, 2, blocks.shape[1] // 2, blocks.shape[2])

    def start_reduce(grads_of, names, tag):
        return _direct_scatter_start([blocks_of(grads_of[n], n) for n in names], "grad_scatter_%s_start" % tag)

    g_c, dh = back_c(jnp.ones((), F32))
    pending_c, token = start_reduce(g_c, layer1_m, "layer1")
    g_b, dh = back_b(dh + token)
    pending_b, token = start_reduce(g_b, mlp0_m, "mlp0")
    (g_a,) = back_a(dh + token)
    ps_c, ts_c = _direct_scatter_wait(pending_c, g_a["x"], "grad_scatter_layer1_wait")
    ps_b, ts_b = _direct_scatter_wait(pending_b, g_a["x"], "grad_scatter_mlp0_wait")

    g = {**g_a, **g_b, **g_c}
    g.update({n: jnp.stack([(g_b if n in g_b else g_a)[n], g_c[n]]) for n in LAYER_NORMS})
    repl = jnp.concatenate([g[n].reshape(-1) for n in REPLICATED]).reshape(N_CHIPS, -1)
    small = [_split_shards(g["meta_tokens"], N_META, D_MODEL, 1), _split_shards(g["ev_conv_w"], CONV_WIDTH, LRU_WIDTH, 1), repl]
    small = [pc.reshape(N_CHIPS, 2, -1) for pc in small]
    n_small = sum(pc.shape[2] for pc in small)
    small.append(jnp.zeros((N_CHIPS, 2, _round_up(n_small, 32 * PACK_COLS) - n_small), F32))
    p_small = jnp.concatenate(small, axis=2).reshape(N_CHIPS, 2, -1, PACK_COLS)
    ss_a = _sibling_reduce([blocks_of(g_a[n], n) for n in mixer0_m] + [p_small], [MXU_DTYPE] * len(mixer0_m) + [F32], "mixer0_")
    ts_a = list(_chip_scatter(ss_a, "grad_chip_scatter_mixer0"))
    fs = [_sum_partials(s, t, "grad_chip_sum_mixer0_%d" % i) for i, (s, t) in enumerate(zip(ss_a, ts_a))]
    fs += [_sum_direct(p, t, "grad_sum_%d" % i) for i, (p, t) in enumerate(zip(ps_b + ps_c, ts_b + ts_c))]
    reduced = _sibling_gather(fs, "grad_sibling_gather")
    red_big = dict(zip(mixer0_m + ("small",) + mlp0_m + layer1_m, reduced))
    red_small = red_big.pop("small").reshape(2, -1)

    grads = {}
    for name, parts in BIG_PARAMS:
        grads[name] = jnp.stack([red_big[part].reshape(given[name].shape[1:]) for part in parts])

    def take(off, sz):
        return jnp.concatenate([red_small[0, off // 2:(off + sz) // 2], red_small[1, off // 2:(off + sz) // 2]])

    off = 0
    for name in SMALL_SHARDED:
        sz = given[name].size
        grads[name] = take(off, sz).reshape(given[name].shape)
        off += sz
    n_repl = repl.shape[1]
    own_repl = _to_rows(take(off, n_repl), 16)
    repl_all = _with_own(_allgather_chips(own_repl, "replicated_allgather"), own_repl).reshape(N_CHIPS, -1)[:, :n_repl].reshape(-1)
    off = 0
    for name in REPLICATED:
        sz = given[name].size
        grads[name] = repl_all[off:off + sz].reshape(given[name].shape)
        off += sz

    delta, new_m, new_v = {}, {}, {}
    for name, _ in BIG_PARAMS:
        shp = given[name].shape
        two_d = (-1, shp[-1])
        d, nm, nv = _adamw(given[name].reshape(two_d), grads[name].reshape(two_d), given["m_" + name].reshape(two_d),
                           given["v_" + name].reshape(two_d), "adamw_" + name)
        delta[name], new_m[name], new_v[name] = d.reshape(shp), nm.reshape(shp), nv.reshape(shp)
    smalls = SMALL_SHARDED + REPLICATED

    def pack_small(get):
        return _to_rows(jnp.concatenate([get(n).reshape(-1) for n in smalls]), 8)

    outs = _adamw(pack_small(lambda n: given[n]), pack_small(lambda n: grads[n]), pack_small(lambda n: given["m_" + n]),
                  pack_small(lambda n: given["v_" + n]), "adamw_small")
    for res, flat in zip((delta, new_m, new_v), outs):
        flat, off = flat.reshape(-1), 0
        for n in smalls:
            sz = given[n].size
            res[n] = flat[off:off + sz].reshape(given[n].shape)
            off += sz

    return (loss, g_a["x"], *[grads[n] for n in WEIGHT_NAMES], *[delta[n] for n in WEIGHT_NAMES],
            *[new_m[n] for n in WEIGHT_NAMES], *[new_v[n] for n in WEIGHT_NAMES])
```

```python
import functools
import math

import jax
import jax.numpy as jnp
from jax import lax
from jax.experimental import pallas as pl
from jax.experimental.pallas import tpu as pltpu

F32 = jnp.float32
MXU_DTYPE = jnp.bfloat16

D_MODEL = 1024
N_META = 16
LRU_WIDTH = 512
LRU_HEADS = 4
LRU_HEAD_DIM = 128
CONV_WIDTH = 4
LRU_C = 8.0
MLA_HEADS = 8
MLA_NOPE = 64
MLA_ROPE = 32
MLA_V = 64
MLA_Q_RANK = 256
MLA_KV_RANK = 128
RET_HEADS = 4
RET_QK_DIM = 256
RET_V_DIM = 512
D_FF = 4096
ROPE_BASE = 10000.0
DN_ALPHA = 4.0 ** 0.25
EPS = 1e-5
NEG_INF = -1e30
SEQ_BLOCK = 128

ADAM_LR = 0.001
ADAM_B1 = 0.9
ADAM_B2 = 0.999
ADAM_EPS = 1e-08
ADAM_WD = 0.01
ADAM_STEP = 10

PACK_COLS = 1024
N_CHIPS = 4

MESH = pl.DeviceIdType.MESH


def _pick(n, target, align):
    best = None
    for t in range(align, min(n, target) + 1, align):
        if n % t == 0:
            best = t
    return n if best is None else best


def _round_up(n, m):
    return (n + m - 1) // m * m


def _relu2(a):
    r = jnp.maximum(a, 0.0)
    return r * r


def _mm_nn(a, w, act, name, out_dtype=F32):
    M, K = a.shape
    sharded = w.ndim == 3
    n = w.shape[-1]
    N = n * (w.shape[0] if sharded else 1)
    tm = _pick(M, 1088 if K * a.dtype.itemsize <= 4096 else 544, 8)
    tn = _pick(n, 1024, 128)
    per = n // tn

    def body(a_ref, w_ref, o_ref):
        av = a_ref[...]
        if act:
            av = _relu2(av.astype(F32))
        o_ref[...] = jnp.dot(av.astype(MXU_DTYPE), w_ref[...].astype(MXU_DTYPE), preferred_element_type=F32).astype(out_dtype)

    w_spec = pl.BlockSpec((None, K, tn), lambda i, j: (j // per, 0, j % per)) if sharded else pl.BlockSpec((K, tn), lambda i, j: (0, j))
    return pl.pallas_call(
        body, name=name,
        grid=(M // tm, N // tn),
        in_specs=[pl.BlockSpec((tm, K), lambda i, j: (i, 0)), w_spec],
        out_specs=pl.BlockSpec((tm, tn), lambda i, j: (i, j)),
        out_shape=jax.ShapeDtypeStruct((M, N), out_dtype),
        compiler_params=pltpu.CompilerParams(dimension_semantics=("parallel", "arbitrary")),
    )(a, w)


def _mm_nt(g, w, a_src, name, out_dtype=F32):
    M, N = g.shape
    sharded = w.ndim == 3
    K, n = w.shape[-2], w.shape[-1]
    tk = n if n * g.dtype.itemsize <= 8192 else _pick(n, 2048, 128)
    nk = N // tk
    per = n // tk
    tm = _pick(M, 1088 if tk * g.dtype.itemsize <= 4096 else 544, 8)
    tn = _pick(K, 1024, 128)
    has_src = a_src is not None
    assert nk == 1 or out_dtype == F32

    def body(*refs):
        if has_src:
            g_ref, w_ref, s_ref, o_ref = refs
        else:
            g_ref, w_ref, o_ref = refs
        r = lax.dot_general(g_ref[...].astype(MXU_DTYPE), w_ref[...].astype(MXU_DTYPE),
                            (((1,), (1,)), ((), ())), preferred_element_type=F32)
        if has_src:
            r = r * (2.0 * jnp.maximum(s_ref[...].astype(F32), 0.0))
        if nk == 1:
            o_ref[...] = r.astype(out_dtype)
        else:
            k = pl.program_id(2)

            @pl.when(k == 0)
            def _():
                o_ref[...] = r

            @pl.when(k > 0)
            def _():
                o_ref[...] += r

    w_spec = (pl.BlockSpec((None, tn, tk), lambda i, j, k: (k // per, j, k % per)) if sharded
              else pl.BlockSpec((tn, tk), lambda i, j, k: (j, k)))
    in_specs = [pl.BlockSpec((tm, tk), lambda i, j, k: (i, k)), w_spec]
    args = [g, w]
    if has_src:
        assert nk == 1
        in_specs.append(pl.BlockSpec((tm, tn), lambda i, j, k: (i, j)))
        args.append(a_src)
    return pl.pallas_call(
        body, name=name,
        grid=(M // tm, K // tn, nk),
        in_specs=in_specs,
        out_specs=pl.BlockSpec((tm, tn), lambda i, j, k: (i, j)),
        out_shape=jax.ShapeDtypeStruct((M, K), out_dtype),
        compiler_params=pltpu.CompilerParams(dimension_semantics=("parallel", "parallel", "arbitrary")),
    )(*args)


def _mm_tn(a, g, act, name, col_shards=1, out_dtype=F32):
    M, K = a.shape
    _, N = g.shape
    n = N // col_shards
    tm, tn, tk = _pick(K, 1024, 128), _pick(n, 1024, 128), _pick(M, 2176, 8)
    nk = M // tk
    per = n // tn
    direct = out_dtype == F32

    def body(a_ref, g_ref, o_ref, *scratch):
        acc_ref = o_ref if direct else scratch[0]
        k = pl.program_id(2)
        av = a_ref[...]
        if act:
            av = _relu2(av.astype(F32))
        r = lax.dot_general(av.astype(MXU_DTYPE), g_ref[...].astype(MXU_DTYPE),
                            (((0,), (0,)), ((), ())), preferred_element_type=F32)

        @pl.when(k == 0)
        def _():
            acc_ref[...] = r

        @pl.when(k > 0)
        def _():
            acc_ref[...] += r

        if not direct:
            @pl.when(k == nk - 1)
            def _():
                o_ref[...] = acc_ref[...].astype(out_dtype)

    if col_shards == 1:
        out_spec, out_shape = pl.BlockSpec((tm, tn), lambda i, j, k: (i, j)), (K, N)
    else:
        out_spec, out_shape = pl.BlockSpec((None, tm, tn), lambda i, j, k: (j // per, i, j % per)), (col_shards, K, n)
    return pl.pallas_call(
        body, name=name,
        grid=(K // tm, N // tn, nk),
        in_specs=[pl.BlockSpec((tk, tm), lambda i, j, k: (k, i)), pl.BlockSpec((tk, tn), lambda i, j, k: (k, j))],
        out_specs=out_spec,
        out_shape=jax.ShapeDtypeStruct(out_shape, out_dtype),
        scratch_shapes=[] if direct else [pltpu.VMEM((tm, tn), F32)],
        compiler_params=pltpu.CompilerParams(dimension_semantics=("parallel", "parallel", "arbitrary")),
    )(a, g)


@functools.partial(jax.custom_vjp, nondiff_argnums=(3, 4, 5, 6))
def matmul(a, w, w_grad_slot, act, name, out_dtype, col_shards):
    return _mm_nn(a, w, act, name + "_fwd", out_dtype)


def _matmul_fwd(a, w, w_grad_slot, act, name, out_dtype, col_shards):
    return _mm_nn(a, w, act, name + "_fwd", out_dtype), (a, w, jnp.zeros((), w_grad_slot.dtype))


def _matmul_bwd(act, name, out_dtype, col_shards, res, g):
    a, w, slot_like = res
    w_grad_dtype = slot_like.dtype
    da = _mm_nt(g, w, a if act else None, name + "_dx")
    dw = _mm_tn(a, g, act, name + "_dw", col_shards, w_grad_dtype)
    return da, None, dw


matmul.defvjp(_matmul_fwd, _matmul_bwd)


@functools.partial(jax.custom_vjp, nondiff_argnums=(5,))
def mlp(h, w1, w2, w1_grad_slot, w2_grad_slot, name):
    u = _mm_nn(h, w1, False, name + "_w1_fwd", out_dtype=MXU_DTYPE)
    return _mm_nn(u, w2, True, name + "_w2_fwd")


def _mlp_fwd(h, w1, w2, w1_grad_slot, w2_grad_slot, name):
    u = _mm_nn(h, w1, False, name + "_w1_fwd", out_dtype=MXU_DTYPE)
    return _mm_nn(u, w2, True, name + "_w2_fwd"), (h, u, w1, w2, jnp.zeros((), w1_grad_slot.dtype))


def _mlp_bwd(name, res, df):
    h, u, w1, w2, slot_like = res
    du = _mm_nt(df, w2, u, name + "_w2_dx", out_dtype=MXU_DTYPE)
    dw2 = _mm_tn(u, df, True, name + "_w2_dw", 1, slot_like.dtype)
    dh = _mm_nt(du, w1, None, name + "_w1_dx")
    dw1 = _mm_tn(h, du, False, name + "_w1_dw", N_CHIPS, slot_like.dtype)
    return dh, None, None, dw1, dw2


mlp.defvjp(_mlp_fwd, _mlp_bwd)


def _ln_stats(z):
    mu = jnp.mean(z, axis=-1, keepdims=True)
    zc = z - mu
    var = jnp.mean(zc * zc, axis=-1, keepdims=True)
    return zc, lax.rsqrt(var + EPS)


def _ln_fwd_call(resid, branch, g, b, name):
    M, D = resid.shape
    tm = _pick(M, 544, 8)

    def body(r_ref, br_ref, g_ref, b_ref, o_ref):
        zc, rstd = _ln_stats(DN_ALPHA * r_ref[...] + br_ref[...])
        o_ref[...] = zc * rstd * g_ref[...] + b_ref[...]

    row = pl.BlockSpec((tm, D), lambda i: (i, 0))
    vec = pl.BlockSpec((1, D), lambda i: (0, 0))
    return pl.pallas_call(
        body, name=name, grid=(M // tm,), in_specs=[row, row, vec, vec], out_specs=row,
        out_shape=jax.ShapeDtypeStruct((M, D), F32),
        compiler_params=pltpu.CompilerParams(dimension_semantics=("parallel",)),
    )(resid, branch, g.reshape(1, D), b.reshape(1, D))


def _ln_bwd_call(resid, branch, g, dy, name):
    M, D = resid.shape
    tm = _pick(M, 544, 8)

    def body(r_ref, br_ref, g_ref, dy_ref, dz_ref, dg_ref, db_ref):
        @pl.when(pl.program_id(0) == 0)
        def _():
            dg_ref[...] = jnp.zeros_like(dg_ref)
            db_ref[...] = jnp.zeros_like(db_ref)

        zc, rstd = _ln_stats(DN_ALPHA * r_ref[...] + br_ref[...])
        xhat = zc * rstd
        dy = dy_ref[...]
        dxh = dy * g_ref[...]
        m1 = jnp.mean(dxh, axis=-1, keepdims=True)
        m2 = jnp.mean(dxh * xhat, axis=-1, keepdims=True)
        dz_ref[...] = rstd * (dxh - m1 - xhat * m2)
        dg_ref[...] += jnp.sum(dy * xhat, axis=0, keepdims=True)
        db_ref[...] += jnp.sum(dy, axis=0, keepdims=True)

    row = pl.BlockSpec((tm, D), lambda i: (i, 0))
    vec = pl.BlockSpec((1, D), lambda i: (0, 0))
    return pl.pallas_call(
        body, name=name, grid=(M // tm,), in_specs=[row, row, vec, row], out_specs=[row, vec, vec],
        out_shape=[jax.ShapeDtypeStruct((M, D), F32), jax.ShapeDtypeStruct((1, D), F32), jax.ShapeDtypeStruct((1, D), F32)],
        compiler_params=pltpu.CompilerParams(dimension_semantics=("arbitrary",)),
    )(resid, branch, g.reshape(1, D), dy)


@functools.partial(jax.custom_vjp, nondiff_argnums=(4,))
def deepnorm(resid, branch, g, b, name):
    return _ln_fwd_call(resid, branch, g, b, name + "_fwd")


def _deepnorm_fwd(resid, branch, g, b, name):
    return _ln_fwd_call(resid, branch, g, b, name + "_fwd"), (resid, branch, g)


def _deepnorm_bwd(name, res, dy):
    resid, branch, g = res
    dz, dg, db = _ln_bwd_call(resid, branch, g, dy, name + "_bwd")
    return DN_ALPHA * dz, dz, dg.reshape(g.shape), db.reshape(g.shape)


deepnorm.defvjp(_deepnorm_fwd, _deepnorm_bwd)


def _rms_fwd_call(x, g, name, col_block=0):
    R = x.shape[0]
    W = g.shape[-1]
    tr = _pick(R, 1088, 8)

    def body(x_ref, g_ref, o_ref):
        xv = x_ref[...]
        rstd = lax.rsqrt(jnp.mean(xv * xv, axis=-1, keepdims=True) + EPS)
        o_ref[...] = xv * rstd * g_ref[...]

    vec = pl.BlockSpec((1, W), lambda i: (0, 0))
    return pl.pallas_call(
        body, name=name, grid=(R // tr,), in_specs=[pl.BlockSpec((tr, W), lambda i: (i, col_block)), vec],
        out_specs=pl.BlockSpec((tr, W), lambda i: (i, 0)), out_shape=jax.ShapeDtypeStruct((R, W), F32),
        compiler_params=pltpu.CompilerParams(dimension_semantics=("parallel",)),
    )(x, g.reshape(1, W))


def _rms_bwd_call(x, g, dy, name, col_block=0):
    R = x.shape[0]
    W = g.shape[-1]
    tr = _pick(R, 1088, 8)

    def body(x_ref, g_ref, dy_ref, dx_ref, dg_ref):
        @pl.when(pl.program_id(0) == 0)
        def _():
            dg_ref[...] = jnp.zeros_like(dg_ref)

        xv = x_ref[...]
        rstd = lax.rsqrt(jnp.mean(xv * xv, axis=-1, keepdims=True) + EPS)
        xhat = xv * rstd
        dy = dy_ref[...]
        dxh = dy * g_ref[...]
        dx_ref[...] = rstd * (dxh - xhat * jnp.mean(dxh * xhat, axis=-1, keepdims=True))
        dg_ref[...] += jnp.sum(dy * xhat, axis=0, keepdims=True)

    row = pl.BlockSpec((tr, W), lambda i: (i, 0))
    vec = pl.BlockSpec((1, W), lambda i: (0, 0))
    return pl.pallas_call(
        body, name=name, grid=(R // tr,), in_specs=[pl.BlockSpec((tr, W), lambda i: (i, col_block)), vec, row], out_specs=[row, vec],
        out_shape=[jax.ShapeDtypeStruct((R, W), F32), jax.ShapeDtypeStruct((1, W), F32)],
        compiler_params=pltpu.CompilerParams(dimension_semantics=("arbitrary",)),
    )(x, g.reshape(1, W), dy)


def _loss_call(y, tgt, name):
    R, D = y.shape
    tr = _pick(R, 512, 8)

    def body(y_ref, t_ref, dy_ref, acc_ref):
        @pl.when(pl.program_id(0) == 0)
        def _():
            acc_ref[...] = jnp.zeros_like(acc_ref)

        e = y_ref[...] - t_ref[...]
        dy_ref[...] = e * (1.0 / D)
        acc_ref[...] += jnp.sum(jnp.sum(e * e, axis=-1, keepdims=True), axis=0, keepdims=True) * (0.5 / D)

    row = pl.BlockSpec((tr, D), lambda i: (i, 0))
    one = pl.BlockSpec((1, 1), lambda i: (0, 0))
    return pl.pallas_call(
        body, name=name, grid=(R // tr,), in_specs=[row, row], out_specs=[row, one],
        out_shape=[jax.ShapeDtypeStruct((R, D), F32), jax.ShapeDtypeStruct((1, 1), F32)],
        compiler_params=pltpu.CompilerParams(dimension_semantics=("arbitrary",)),
    )(y, tgt)


@jax.custom_vjp
def loss_head(y, tgt):
    return _loss_call(y, tgt, "loss_head")[1][0, 0]


def _loss_head_fwd(y, tgt):
    dy, acc = _loss_call(y, tgt, "loss_head")
    return acc[0, 0], dy


def _loss_head_bwd(dy, ct):
    return ct * dy, None


loss_head.defvjp(_loss_head_fwd, _loss_head_bwd)


_GELU_C = math.sqrt(2.0 / math.pi)


def _gelu_parts(x):
    x2 = x * x
    t = jnp.tanh(_GELU_C * (x + 0.044715 * x * x2))
    gelu = 0.5 * x * (1.0 + t)
    dgelu = 0.5 * (1.0 + t) + 0.5 * x * (1.0 - t * t) * (_GELU_C * (1.0 + 3.0 * 0.044715 * x2))
    return gelu, dgelu


def _sigmoid(x):
    return 1.0 / (1.0 + jnp.exp(-x))


def _scan8(a, b, carry, reverse):
    row = lax.broadcasted_iota(jnp.int32, a.shape, 0)
    for s in (1, 2, 4):
        shift = 8 - s if reverse else s
        keep = (row < 8 - s) if reverse else (row >= s)
        b = jnp.where(keep, a * pltpu.roll(b, shift, 0) + b, b)
        a = jnp.where(keep, a * pltpu.roll(a, shift, 0), a)
    return a * carry + b


def _lru_pre(prec_ref, prev_ref, first, cw_ref, cb_ref, wa_ref, ba_ref, wx_ref, bx_ref, sp_ref):
    tc = prec_ref.shape[0]
    prev = jnp.where(first, 0.0, prev_ref[...])
    ext = jnp.concatenate([prev, prec_ref[...]], axis=0)
    cw = cw_ref[...]
    taps = [ext[8:] if k == CONV_WIDTH - 1 else pltpu.roll(ext, CONV_WIDTH - 1 - k, 0)[8:] for k in range(CONV_WIDTH)]
    xc = cb_ref[...] + sum(cw[k:k + 1, :] * taps[k] for k in range(CONV_WIDTH))
    ga, gx = [], []
    for h in range(LRU_HEADS):
        xh = xc[:, h * LRU_HEAD_DIM:(h + 1) * LRU_HEAD_DIM].astype(MXU_DTYPE)
        ga.append(jnp.dot(xh, wa_ref[h].astype(MXU_DTYPE), preferred_element_type=F32))
        gx.append(jnp.dot(xh, wx_ref[h].astype(MXU_DTYPE), preferred_element_type=F32))
    r = _sigmoid(jnp.concatenate(ga, axis=1) + ba_ref[...])
    i = _sigmoid(jnp.concatenate(gx, axis=1) + bx_ref[...])
    log_a = -LRU_C * r * sp_ref[...]
    a = jnp.exp(log_a)
    a2 = a * a
    mult = jnp.sqrt(-jnp.tanh(log_a) * (a2 + 1.0))
    return taps, xc, r, i, a, a2, mult


def _lru_fwd_call(p, cw, cb, wa, ba, wx, bx, sp):
    B, Tp, _ = p.shape
    W = LRU_WIDTH
    tc = SEQ_BLOCK
    nc = Tp // tc

    def body(pg_ref, prec_ref, prev_ref, cw_ref, cb_ref, wa_ref, ba_ref, wx_ref, bx_ref, sp_ref, y_ref, h_ref, carry_ref):
        first = pl.program_id(1) == 0

        @pl.when(first)
        def _():
            carry_ref[...] = jnp.zeros_like(carry_ref)

        _, xc, r, i, a, a2, mult = _lru_pre(prec_ref, prev_ref, first, cw_ref, cb_ref, wa_ref, ba_ref, wx_ref, bx_ref, sp_ref)
        b = mult * (i * xc)
        carry = carry_ref[0:1, :]
        for t in range(tc // 8):
            h = _scan8(a[8 * t:8 * t + 8], b[8 * t:8 * t + 8], carry, False)
            h_ref[8 * t:8 * t + 8, :] = h
            carry = h[7:8, :]
        carry_ref[...] = jnp.broadcast_to(carry, carry_ref.shape)
        y_ref[...] = h_ref[...] * _gelu_parts(pg_ref[...])[0]

    cur = pl.BlockSpec((None, tc, W), lambda b, j: (b, j, 0))
    rec = pl.BlockSpec((None, tc, W), lambda b, j: (b, j, 1))
    prev = pl.BlockSpec((None, 8, W), lambda b, j: (b, jnp.maximum(j * (tc // 8) - 1, 0), 1))
    vec = pl.BlockSpec((1, W), lambda b, j: (0, 0))
    cws = pl.BlockSpec((CONV_WIDTH, W), lambda b, j: (0, 0))
    wsp = pl.BlockSpec((LRU_HEADS, LRU_HEAD_DIM, LRU_HEAD_DIM), lambda b, j: (0, 0, 0))
    return pl.pallas_call(
        body, name="lru_fwd", grid=(B, nc),
        in_specs=[cur, rec, prev, cws, vec, wsp, vec, wsp, vec, vec],
        out_specs=[cur, cur],
        out_shape=[jax.ShapeDtypeStruct((B, Tp, W), F32), jax.ShapeDtypeStruct((B, Tp, W), F32)],
        scratch_shapes=[pltpu.VMEM((8, W), F32)],
        compiler_params=pltpu.CompilerParams(dimension_semantics=("arbitrary", "arbitrary")),
    )(p, p, p, cw, cb, wa, ba, wx, bx, sp)


def _lru_bwd_call(p, hseq, dy, cw, cb, wa, ba, wx, bx, sp):
    B, Tp, _ = p.shape
    W = LRU_WIDTH
    tc = SEQ_BLOCK
    nc = Tp // tc
    HD = LRU_HEAD_DIM

    def body(pg_ref, prec_ref, prev_ref, h_ref, hprev_ref, dy_ref, cw_ref, cb_ref, wa_ref, ba_ref, wx_ref, bx_ref, sp_ref,
             dpg_ref, dprec_ref, dcw_ref, dcb_ref, dwa_ref, dba_ref, dwx_ref, dbx_ref, dsp_ref,
             gcar_ref, anext_ref, halo_ref, g_ref):
        j = pl.program_id(1)
        first = j == nc - 1
        last = j == 0

        @pl.when(jnp.logical_and(pl.program_id(0) == 0, last))
        def _():
            for ref in (dcw_ref, dcb_ref, dwa_ref, dba_ref, dwx_ref, dbx_ref, dsp_ref):
                ref[...] = jnp.zeros_like(ref)

        @pl.when(last)
        def _():
            gcar_ref[...] = jnp.zeros_like(gcar_ref)
            anext_ref[...] = jnp.zeros_like(anext_ref)
            halo_ref[...] = jnp.zeros_like(halo_ref)

        taps, xc, r, i, a, a2, mult = _lru_pre(prec_ref, prev_ref, first, cw_ref, cb_ref, wa_ref, ba_ref, wx_ref, bx_ref, sp_ref)
        row = lax.broadcasted_iota(jnp.int32, (tc, W), 0)
        gelu, dgelu = _gelu_parts(pg_ref[...])
        dy = dy_ref[...]
        hcur = h_ref[...]
        dpg_ref[...] = dy * hcur * dgelu
        dh = dy * gelu
        a_next = jnp.where(row == tc - 1, anext_ref[0:1, :], pltpu.roll(a, tc - 1, 0))
        carry = gcar_ref[0:1, :]
        for t in reversed(range(tc // 8)):
            g = _scan8(a_next[8 * t:8 * t + 8], dh[8 * t:8 * t + 8], carry, True)
            g_ref[8 * t:8 * t + 8, :] = g
            carry = g[0:1, :]
        gcar_ref[...] = jnp.broadcast_to(carry, gcar_ref.shape)
        anext_ref[...] = jnp.broadcast_to(a[0:1, :], anext_ref.shape)
        G = g_ref[...]
        h_before = jnp.where(first, 0.0, hprev_ref[7:8, :])
        hprev = jnp.where(row == 0, h_before, pltpu.roll(hcur, 1, 0))
        d_a = G * hprev
        gx_ = G * xc
        d_mult = gx_ * i
        d_i = gx_ * mult
        dxc = G * (mult * i)
        d_la = d_a * a - d_mult * (a2 / mult)
        sp = sp_ref[...]
        d_r = d_la * (-LRU_C * sp)
        dsp_ref[...] += jnp.sum(d_la * (-LRU_C * r), axis=0, keepdims=True)
        dga = d_r * r * (1.0 - r)
        dgx = d_i * i * (1.0 - i)
        dba_ref[...] += jnp.sum(dga, axis=0, keepdims=True)
        dbx_ref[...] += jnp.sum(dgx, axis=0, keepdims=True)
        back = []
        for h in range(LRU_HEADS):
            sl = slice(h * HD, (h + 1) * HD)
            xh = xc[:, sl].astype(MXU_DTYPE)
            ah = dga[:, sl].astype(MXU_DTYPE)
            bh = dgx[:, sl].astype(MXU_DTYPE)
            tn = (((0,), (0,)), ((), ()))
            nt = (((1,), (1,)), ((), ()))
            dwa_ref[h] += lax.dot_general(xh, ah, tn, preferred_element_type=F32)
            dwx_ref[h] += lax.dot_general(xh, bh, tn, preferred_element_type=F32)
            back.append(lax.dot_general(ah, wa_ref[h].astype(MXU_DTYPE), nt, preferred_element_type=F32)
                        + lax.dot_general(bh, wx_ref[h].astype(MXU_DTYPE), nt, preferred_element_type=F32))
        dxc = dxc + jnp.concatenate(back, axis=1)
        dcb_ref[...] += jnp.sum(dxc, axis=0, keepdims=True)
        for k in range(CONV_WIDTH):
            dcw_ref[k:k + 1, :] += jnp.sum(dxc * taps[k], axis=0, keepdims=True)
        ext = jnp.concatenate([dxc, halo_ref[...]], axis=0)
        cw = cw_ref[...]
        acc = cw[CONV_WIDTH - 1:CONV_WIDTH, :] * dxc
        for k in range(CONV_WIDTH - 1):
            s = CONV_WIDTH - 1 - k
            acc = acc + cw[k:k + 1, :] * pltpu.roll(ext, tc + 8 - s, 0)[:tc]
        dprec_ref[...] = acc
        halo_ref[...] = dxc[0:8, :]

    rev = lambda j: nc - 1 - j
    cur = pl.BlockSpec((None, tc, W), lambda b, j: (b, rev(j), 0))
    rec = pl.BlockSpec((None, tc, W), lambda b, j: (b, rev(j), 1))
    prev = pl.BlockSpec((None, 8, W), lambda b, j: (b, jnp.maximum(rev(j) * (tc // 8) - 1, 0), 0))
    prev_rec = pl.BlockSpec((None, 8, W), lambda b, j: (b, jnp.maximum(rev(j) * (tc // 8) - 1, 0), 1))
    vec = pl.BlockSpec((1, W), lambda b, j: (0, 0))
    cws = pl.BlockSpec((CONV_WIDTH, W), lambda b, j: (0, 0))
    wsp = pl.BlockSpec((LRU_HEADS, HD, HD), lambda b, j: (0, 0, 0))
    seq = jax.ShapeDtypeStruct((B, Tp, W), F32)
    vs = jax.ShapeDtypeStruct((1, W), F32)
    ws = jax.ShapeDtypeStruct((LRU_HEADS, HD, HD), F32)
    return pl.pallas_call(
        body, name="lru_bwd", grid=(B, nc),
        in_specs=[cur, rec, prev_rec, cur, prev, cur, cws, vec, wsp, vec, wsp, vec, vec],
        out_specs=[cur, cur, cws, vec, wsp, vec, wsp, vec, vec],
        out_shape=[seq, seq, jax.ShapeDtypeStruct((CONV_WIDTH, W), F32), vs, ws, vs, ws, vs, vs],
        scratch_shapes=[pltpu.VMEM((8, W), F32), pltpu.VMEM((8, W), F32), pltpu.VMEM((8, W), F32), pltpu.VMEM((tc, W), F32)],
        compiler_params=pltpu.CompilerParams(dimension_semantics=("arbitrary", "arbitrary")),
    )(p, p, p, hseq, hseq, dy, cw, cb, wa, ba, wx, bx, sp)


_Q_BLOCK = 2 * LRU_WIDTH // MLA_Q_RANK
_KV_BLOCK = (2 * LRU_WIDTH + MLA_Q_RANK) // MLA_KV_RANK
_KPE_START = 2 * LRU_WIDTH + MLA_Q_RANK + MLA_KV_RANK


@jax.custom_vjp
def even_front(p, cw, cb, wa, ba, wx, bx, sp, gq, gkv):
    return _even_front_fwd(p, cw, cb, wa, ba, wx, bx, sp, gq, gkv)[0]


def _even_front_fwd(p, cw, cb, wa, ba, wx, bx, sp, gq, gkv):
    B, Tp, W = p.shape
    p2d = p.reshape(B * Tp, W)
    y, hseq = _lru_fwd_call(p, cw, cb, wa, ba, wx, bx, sp)
    qn = _rms_fwd_call(p2d, gq, "q_norm_fwd", _Q_BLOCK)
    kvn = _rms_fwd_call(p2d, gkv, "kv_norm_fwd", _KV_BLOCK)
    return (y, qn, kvn, p2d[:, _KPE_START:]), (p, hseq, cw, cb, wa, ba, wx, bx, sp, gq, gkv)


def _even_front_bwd(res, cts):
    p, hseq, cw, cb, wa, ba, wx, bx, sp, gq, gkv = res
    dy, dqn, dkvn, dkpe = cts
    B, Tp, W = p.shape
    p2d = p.reshape(B * Tp, W)
    dpg, dprec, dcw, dcb, dwa, dba, dwx, dbx, dsp = _lru_bwd_call(p, hseq, dy, cw, cb, wa, ba, wx, bx, sp)
    dpq, dgq = _rms_bwd_call(p2d, gq, dqn, "q_norm_bwd", _Q_BLOCK)
    dpkv, dgkv = _rms_bwd_call(p2d, gkv, dkvn, "kv_norm_bwd", _KV_BLOCK)
    dp = jnp.concatenate([dpg.reshape(B * Tp, -1), dprec.reshape(B * Tp, -1), dpq, dpkv, dkpe], axis=1).reshape(B, Tp, W)
    return dp, dcw, dcb, dwa, dba, dwx, dbx, dsp, dgq.reshape(gq.shape), dgkv.reshape(gkv.shape)


even_front.defvjp(_even_front_fwd, _even_front_bwd)


def _rope_tables(pos, half):
    inv = ROPE_BASE ** (-jnp.arange(half, dtype=F32) / half)
    ang = pos.astype(F32)[:, None] * inv[None, :]
    return jnp.cos(ang), jnp.sin(ang)


_NT = (((1,), (1,)), ((), ()))
_TN = (((0,), (0,)), ((), ()))
HEAD_LANES = 128
_MLA_SCALE = (MLA_NOPE + MLA_ROPE) ** -0.5
_LOG2E = math.log2(math.e)


def _mask_diagonal(s, fill):
    L = s.shape[1]
    row = lax.broadcasted_iota(jnp.int32, (SEQ_BLOCK, SEQ_BLOCK), 0)
    col = lax.broadcasted_iota(jnp.int32, (SEQ_BLOCK, SEQ_BLOCK), 1)
    last = jnp.where(col <= row, s[:, L - SEQ_BLOCK:], fill)
    return last if L == SEQ_BLOCK else jnp.concatenate([s[:, :L - SEQ_BLOCK], last], axis=1)


def _mla_rope_tables(pos):
    half = MLA_ROPE // 2
    cos, sin = _rope_tables(pos, half)
    T = pos.shape[0]
    ones, zeros = jnp.ones((T, MLA_NOPE), F32), jnp.zeros((T, MLA_NOPE), F32)
    tail1, tail0 = jnp.ones((T, HEAD_LANES - MLA_NOPE - MLA_ROPE), F32), jnp.zeros((T, HEAD_LANES - MLA_NOPE - MLA_ROPE), F32)
    zh = jnp.zeros((T, half), F32)
    c = jnp.concatenate([ones, cos, cos, tail1], axis=1)
    s_up = jnp.concatenate([zeros, -sin, zh, tail0], axis=1)
    s_down = jnp.concatenate([zeros, zh, sin, tail0], axis=1)
    return c, s_up, s_down


def _rope_lanes(x, c, s_up, s_down):
    half = MLA_ROPE // 2
    return x * c + pltpu.roll(x, HEAD_LANES - half, 1) * s_up + pltpu.roll(x, half, 1) * s_down


def _unrope_lanes(d, c, s_up, s_down):
    half = MLA_ROPE // 2
    return d * c + pltpu.roll(d * s_up, half, 1) + pltpu.roll(d * s_down, HEAD_LANES - half, 1)


def _mla_operands(q_ref, kv_ref, kpe_ref, c, s_up, s_down):
    lane = lax.broadcasted_iota(jnp.int32, kv_ref.shape, 1)
    qr = (_rope_lanes(q_ref[...].astype(F32), c, s_up, s_down) * (_MLA_SCALE * _LOG2E)).astype(MXU_DTYPE)
    kr = jnp.where(lane < MLA_NOPE, kv_ref[...].astype(F32), _rope_lanes(kpe_ref[...], c, s_up, s_down)).astype(MXU_DTYPE)
    return qr, kr, lane


def _mla_specs(Tp):
    head = pl.BlockSpec((None, Tp, HEAD_LANES), lambda b, h: (b, 0, h))
    shared = pl.BlockSpec((None, Tp, HEAD_LANES), lambda b, h: (b, 0, 0))
    tab = pl.BlockSpec((Tp, HEAD_LANES), lambda b, h: (0, 0))
    lse = pl.BlockSpec((None, None, Tp, 1), lambda b, h: (b, h, 0, 0))
    return head, shared, tab, lse


def _attn_fwd_call(q, kv, kpe, tabs):
    B, Tp, _ = q.shape
    nq = Tp // SEQ_BLOCK

    def body(q_ref, kv_ref, kpe_ref, c_ref, su_ref, sd_ref, o_ref, lse_ref, qr_ref, kr_ref):
        qr, kr, lane = _mla_operands(q_ref, kv_ref, kpe_ref, c_ref[...], su_ref[...], sd_ref[...])
        qr_ref[...] = qr
        kr_ref[...] = kr
        for qi in range(nq):
            L = (qi + 1) * SEQ_BLOCK
            blk = slice(qi * SEQ_BLOCK, L)
            s = _mask_diagonal(lax.dot_general(qr_ref[blk, :], kr_ref[0:L, :], _NT, preferred_element_type=F32), NEG_INF)
            m = jnp.max(s, axis=-1, keepdims=True)
            p = jnp.exp2(s - m)
            l = jnp.sum(p, axis=-1, keepdims=True)
            o = jnp.dot(p.astype(MXU_DTYPE), kv_ref[0:L, :].astype(MXU_DTYPE), preferred_element_type=F32)
            o_ref[blk, :] = jnp.where(lane[blk, :] >= MLA_NOPE, o / l, 0.0)
            lse_ref[blk, :] = m + jnp.log2(l)

    head, shared, tab, lse = _mla_specs(Tp)
    return pl.pallas_call(
        body, name="mla_attn_fwd", grid=(B, MLA_HEADS), in_specs=[head, head, shared, tab, tab, tab], out_specs=[head, lse],
        out_shape=[jax.ShapeDtypeStruct((B, Tp, MLA_HEADS * HEAD_LANES), F32), jax.ShapeDtypeStruct((B, MLA_HEADS, Tp, 1), F32)],
        scratch_shapes=[pltpu.VMEM((Tp, HEAD_LANES), MXU_DTYPE), pltpu.VMEM((Tp, HEAD_LANES), MXU_DTYPE)],
        compiler_params=pltpu.CompilerParams(dimension_semantics=("parallel", "parallel")),
    )(q, kv, kpe, *tabs)


def _attn_bwd_call(q, kv, kpe, tabs, o, lse, do):
    B, Tp, _ = q.shape
    nq = Tp // SEQ_BLOCK

    def body(q_ref, kv_ref, kpe_ref, c_ref, su_ref, sd_ref, o_ref, lse_ref, do_ref, dq_ref, dkv_ref, dkpe_ref,
             qr_ref, kr_ref, dqa_ref, dka_ref, dva_ref):
        c, s_up, s_down = c_ref[...], su_ref[...], sd_ref[...]
        qr, kr, lane = _mla_operands(q_ref, kv_ref, kpe_ref, c, s_up, s_down)
        qr_ref[...] = qr
        kr_ref[...] = kr
        dka_ref[...] = jnp.zeros_like(dka_ref)
        dva_ref[...] = jnp.zeros_like(dva_ref)
        for qi in range(nq):
            L = (qi + 1) * SEQ_BLOCK
            blk = slice(qi * SEQ_BLOCK, L)
            qb = qr_ref[blk, :]
            do = jnp.where(lane[blk, :] >= MLA_NOPE, do_ref[blk, :], 0.0)
            delta = jnp.sum(do * o_ref[blk, :], axis=-1, keepdims=True)
            s = _mask_diagonal(lax.dot_general(qb, kr_ref[0:L, :], _NT, preferred_element_type=F32), NEG_INF)
            p = jnp.exp2(s - lse_ref[blk, :])
            dob = do.astype(MXU_DTYPE)
            dva_ref[0:L, :] += lax.dot_general(p.astype(MXU_DTYPE), dob, _TN, preferred_element_type=F32)
            dp = lax.dot_general(dob, kv_ref[0:L, :].astype(MXU_DTYPE), _NT, preferred_element_type=F32)
            ds = (p * (dp - delta)).astype(MXU_DTYPE)
            dqa_ref[blk, :] = jnp.dot(ds, kr_ref[0:L, :], preferred_element_type=F32)
            dka_ref[0:L, :] += lax.dot_general(ds, qb, _TN, preferred_element_type=F32)
        dq_ref[...] = _unrope_lanes(dqa_ref[...] * _MLA_SCALE, c, s_up, s_down).astype(dq_ref.dtype)
        dk = dka_ref[...] * (1.0 / _LOG2E)
        dkv_ref[...] = jnp.where(lane < MLA_NOPE, dk, dva_ref[...]).astype(dkv_ref.dtype)
        dkpe = jnp.where(lane >= MLA_NOPE, _unrope_lanes(dk, c, s_up, s_down), 0.0)

        @pl.when(pl.program_id(1) == 0)
        def _():
            dkpe_ref[...] = dkpe

        @pl.when(pl.program_id(1) > 0)
        def _():
            dkpe_ref[...] += dkpe

    head, shared, tab, lse_spec = _mla_specs(Tp)
    wide = jax.ShapeDtypeStruct((B, Tp, MLA_HEADS * HEAD_LANES), q.dtype)
    acc = pltpu.VMEM((Tp, HEAD_LANES), F32)
    return pl.pallas_call(
        body, name="mla_attn_bwd", grid=(B, MLA_HEADS),
        in_specs=[head, head, shared, tab, tab, tab, head, lse_spec, head], out_specs=[head, head, shared],
        out_shape=[wide, wide, jax.ShapeDtypeStruct((B, Tp, HEAD_LANES), F32)],
        scratch_shapes=[pltpu.VMEM((Tp, HEAD_LANES), MXU_DTYPE), pltpu.VMEM((Tp, HEAD_LANES), MXU_DTYPE), acc, acc, acc],
        compiler_params=pltpu.CompilerParams(dimension_semantics=("parallel", "arbitrary")),
    )(q, kv, kpe, *tabs, o, lse, do)


@jax.custom_vjp
def mla_attention(q, kv, kpe, tabs):
    return _attn_fwd_call(q, kv, kpe, tabs)[0]


def _mla_attention_fwd(q, kv, kpe, tabs):
    o, lse = _attn_fwd_call(q, kv, kpe, tabs)
    return o, (q, kv, kpe, tabs, o, lse)


def _mla_attention_bwd(res, do):
    q, kv, kpe, tabs, o, lse = res
    dq, dkv, dkpe = _attn_bwd_call(q, kv, kpe, tabs, o, lse, do)
    return dq, dkv, dkpe, None


mla_attention.defvjp(_mla_attention_fwd, _mla_attention_bwd)


def _rope_halves(x, cos, sin):
    half = x.shape[1] // 2
    x1, x2 = x[:, :half], x[:, half:]
    return jnp.concatenate([x1 * cos - x2 * sin, x1 * sin + x2 * cos], axis=1)


def _unrope_halves(d, cos, sin):
    half = d.shape[1] // 2
    d1, d2 = d[:, :half], d[:, half:]
    return jnp.concatenate([d1 * cos + d2 * sin, d2 * cos - d1 * sin], axis=1)


_RET_K_SCALE = RET_QK_DIM ** -0.5
_RET_Q_BLOCKS = RET_HEADS
_RET_V_BLOCK0 = 2 * RET_HEADS * RET_QK_DIM // RET_V_DIM
_RET_G_BLOCK0 = _RET_V_BLOCK0 + RET_HEADS


def _ret_specs(Tp):
    q = pl.BlockSpec((None, Tp, RET_QK_DIM), lambda b, h: (b, 0, h))
    k = pl.BlockSpec((None, Tp, RET_QK_DIM), lambda b, h: (b, 0, _RET_Q_BLOCKS + h))
    v = pl.BlockSpec((None, Tp, RET_V_DIM), lambda b, h: (b, 0, _RET_V_BLOCK0 + h))
    tab = pl.BlockSpec((Tp, RET_QK_DIM // 2), lambda b, h: (0, 0))
    lg = pl.BlockSpec((None, 1, 1), lambda b, h: (h, 0, 0))
    return q, k, v, tab, lg


def _ret_operands(q_ref, k_ref, cos, sin, lg):
    t = lax.broadcasted_iota(jnp.int32, (q_ref.shape[0], 1), 0).astype(F32)
    grow, shrink = jnp.exp(-lg * t), jnp.exp(lg * t)
    qs = (_rope_halves(q_ref[...].astype(F32), cos, sin) * shrink).astype(MXU_DTYPE)
    ks = (_rope_halves(k_ref[...].astype(F32), cos, sin) * (grow * _RET_K_SCALE)).astype(MXU_DTYPE)
    return qs, ks, shrink, grow * _RET_K_SCALE


def _ret_core_fwd_call(p, cos, sin, lg):
    B, Tp, _ = p.shape
    nq = Tp // SEQ_BLOCK

    def body(q_ref, k_ref, v_ref, cos_ref, sin_ref, lg_ref, o_ref, qs_ref, ks_ref):
        qs_ref[...], ks_ref[...], _, _ = _ret_operands(q_ref, k_ref, cos_ref[...], sin_ref[...], lg_ref[...])
        for qi in range(nq):
            L = (qi + 1) * SEQ_BLOCK
            blk = slice(qi * SEQ_BLOCK, L)
            s = _mask_diagonal(lax.dot_general(qs_ref[blk, :], ks_ref[0:L, :], _NT, preferred_element_type=F32), 0.0)
            o_ref[blk, :] = jnp.dot(s.astype(MXU_DTYPE), v_ref[0:L, :].astype(MXU_DTYPE), preferred_element_type=F32)

    q, k, v, tab, lgs = _ret_specs(Tp)
    return pl.pallas_call(
        body, name="retention_fwd", grid=(B, RET_HEADS), in_specs=[q, k, v, tab, tab, lgs],
        out_specs=pl.BlockSpec((None, Tp, RET_V_DIM), lambda b, h: (b, 0, h)),
        out_shape=jax.ShapeDtypeStruct((B, Tp, RET_HEADS * RET_V_DIM), F32),
        scratch_shapes=[pltpu.VMEM((Tp, RET_QK_DIM), MXU_DTYPE), pltpu.VMEM((Tp, RET_QK_DIM), MXU_DTYPE)],
        compiler_params=pltpu.CompilerParams(dimension_semantics=("parallel", "parallel")),
    )(p, p, p, cos, sin, lg)


def _ret_core_bwd_call(p, do, cos, sin, lg):
    B, Tp, _ = p.shape
    nq = Tp // SEQ_BLOCK

    def body(q_ref, k_ref, v_ref, do_ref, cos_ref, sin_ref, lg_ref, dq_ref, dk_ref, dv_ref, qs_ref, ks_ref, dqa_ref, dka_ref, dva_ref):
        cos_, sin_ = cos_ref[...], sin_ref[...]
        qs_ref[...], ks_ref[...], q_scale, k_scale = _ret_operands(q_ref, k_ref, cos_, sin_, lg_ref[...])
        dka_ref[...] = jnp.zeros_like(dka_ref)
        dva_ref[...] = jnp.zeros_like(dva_ref)
        for qi in range(nq):
            L = (qi + 1) * SEQ_BLOCK
            blk = slice(qi * SEQ_BLOCK, L)
            qb = qs_ref[blk, :]
            dob = do_ref[blk, :].astype(MXU_DTYPE)
            s = _mask_diagonal(lax.dot_general(qb, ks_ref[0:L, :], _NT, preferred_element_type=F32), 0.0).astype(MXU_DTYPE)
            dva_ref[0:L, :] += lax.dot_general(s, dob, _TN, preferred_element_type=F32)
            ds = _mask_diagonal(lax.dot_general(dob, v_ref[0:L, :].astype(MXU_DTYPE), _NT, preferred_element_type=F32), 0.0).astype(MXU_DTYPE)
            dqa_ref[blk, :] = jnp.dot(ds, ks_ref[0:L, :], preferred_element_type=F32)
            dka_ref[0:L, :] += lax.dot_general(ds, qb, _TN, preferred_element_type=F32)
        dq_ref[...] = _unrope_halves(dqa_ref[...] * q_scale, cos_, sin_).astype(dq_ref.dtype)
        dk_ref[...] = _unrope_halves(dka_ref[...] * k_scale, cos_, sin_).astype(dk_ref.dtype)
        dv_ref[...] = dva_ref[...].astype(dv_ref.dtype)

    q, k, v, tab, lgs = _ret_specs(Tp)
    qk_out = pl.BlockSpec((None, Tp, RET_QK_DIM), lambda b, h: (b, 0, h))
    v_out = pl.BlockSpec((None, Tp, RET_V_DIM), lambda b, h: (b, 0, h))
    return pl.pallas_call(
        body, name="retention_bwd", grid=(B, RET_HEADS), in_specs=[q, k, v, v_out, tab, tab, lgs],
        out_specs=[qk_out, qk_out, v_out],
        out_shape=[jax.ShapeDtypeStruct((B, Tp, RET_HEADS * RET_QK_DIM), p.dtype), jax.ShapeDtypeStruct((B, Tp, RET_HEADS * RET_QK_DIM), p.dtype),
                   jax.ShapeDtypeStruct((B, Tp, RET_HEADS * RET_V_DIM), p.dtype)],
        scratch_shapes=[pltpu.VMEM((Tp, RET_QK_DIM), MXU_DTYPE), pltpu.VMEM((Tp, RET_QK_DIM), MXU_DTYPE),
                        pltpu.VMEM((Tp, RET_QK_DIM), F32), pltpu.VMEM((Tp, RET_QK_DIM), F32), pltpu.VMEM((Tp, RET_V_DIM), F32)],
        compiler_params=pltpu.CompilerParams(dimension_semantics=("parallel", "parallel")),
    )(p, p, p, do, cos, sin, lg)


def _ret_gate_specs(M):
    tm = _pick(M, 1088, 8)
    head = pl.BlockSpec((tm, RET_V_DIM), lambda i, h: (i, h))
    gate = pl.BlockSpec((tm, RET_V_DIM), lambda i, h: (i, _RET_G_BLOCK0 + h))
    return tm, head, gate


def _ret_gate_fwd_call(o, p2d):
    M = o.shape[0]
    tm, head, gate = _ret_gate_specs(M)

    def body(o_ref, g_ref, y_ref):
        ov = o_ref[...]
        gv = g_ref[...].astype(F32)
        rstd = lax.rsqrt(jnp.mean(ov * ov, axis=-1, keepdims=True) + EPS)
        y_ref[...] = (gv * _sigmoid(gv)) * (ov * rstd)

    return pl.pallas_call(
        body, name="retention_gate_fwd", grid=(M // tm, RET_HEADS), in_specs=[head, gate], out_specs=head,
        out_shape=jax.ShapeDtypeStruct(o.shape, F32),
        compiler_params=pltpu.CompilerParams(dimension_semantics=("parallel", "parallel")),
    )(o, p2d)


def _ret_gate_bwd_call(o, p2d, dy):
    M = o.shape[0]
    tm, head, gate = _ret_gate_specs(M)

    def body(o_ref, g_ref, dy_ref, do_ref, dg_ref):
        ov = o_ref[...]
        gv = g_ref[...].astype(F32)
        dy = dy_ref[...]
        rstd = lax.rsqrt(jnp.mean(ov * ov, axis=-1, keepdims=True) + EPS)
        on = ov * rstd
        sg = _sigmoid(gv)
        dg_ref[...] = (dy * on * (sg * (1.0 + gv * (1.0 - sg)))).astype(dg_ref.dtype)
        don = dy * (gv * sg)
        do_ref[...] = (rstd * (don - on * jnp.mean(don * on, axis=-1, keepdims=True))).astype(do_ref.dtype)

    shp = jax.ShapeDtypeStruct(o.shape, p2d.dtype)
    return pl.pallas_call(
        body, name="retention_gate_bwd", grid=(M // tm, RET_HEADS), in_specs=[head, gate, head], out_specs=[head, head],
        out_shape=[shp, shp],
        compiler_params=pltpu.CompilerParams(dimension_semantics=("parallel", "parallel")),
    )(o, p2d, dy)


def _log_gamma():
    return jnp.log(1.0 - 2.0 ** (-5.0 - jnp.arange(RET_HEADS, dtype=F32))).reshape(RET_HEADS, 1, 1)


@jax.custom_vjp
def retention_mixer(p, cos, sin):
    B, Tp, W = p.shape
    o = _ret_core_fwd_call(p, cos, sin, _log_gamma())
    return _ret_gate_fwd_call(o.reshape(B * Tp, -1), p.reshape(B * Tp, W))


def _retention_mixer_fwd(p, cos, sin):
    B, Tp, W = p.shape
    o = _ret_core_fwd_call(p, cos, sin, _log_gamma())
    return _ret_gate_fwd_call(o.reshape(B * Tp, -1), p.reshape(B * Tp, W)), (p, o, cos, sin)


def _retention_mixer_bwd(res, dy):
    p, o, cos, sin = res
    B, Tp, W = p.shape
    do, dg = _ret_gate_bwd_call(o.reshape(B * Tp, -1), p.reshape(B * Tp, W), dy)
    dq, dk, dv = _ret_core_bwd_call(p, do.reshape(B, Tp, -1), cos, sin, _log_gamma())
    return jnp.concatenate([dq, dk, dv, dg.reshape(B, Tp, -1)], axis=-1), None, None


retention_mixer.defvjp(_retention_mixer_fwd, _retention_mixer_bwd)


def _heads_to_lanes(w):
    K = w.shape[0]
    w = w.reshape(K, MLA_HEADS, MLA_NOPE + MLA_ROPE)
    return jnp.pad(w, ((0, 0), (0, 0), (0, HEAD_LANES - MLA_NOPE - MLA_ROPE))).reshape(K, MLA_HEADS * HEAD_LANES)


def _out_rows_to_lanes(w):
    N = w.shape[1]
    att = w[LRU_WIDTH:].reshape(MLA_HEADS, MLA_V, N)
    att = jnp.pad(att, ((0, 0), (HEAD_LANES - MLA_V, 0), (0, 0))).reshape(MLA_HEADS * HEAD_LANES, N)
    return jnp.concatenate([w[:LRU_WIDTH], att], axis=0)


def _seq_dims(x):
    B, S, D = x.shape
    T = S + N_META
    Tp = _round_up(T, SEQ_BLOCK)
    return B, S, T, Tp


def _mixer0(diff, w, token):
    x = diff["x"]
    B, S, T, Tp = _seq_dims(x)
    D = x.shape[-1]
    M = B * Tp
    pos = jnp.arange(Tp, dtype=jnp.int32)

    def mm(a, name, act=False, out_dtype=F32, layout=lambda m: m, col_shards=1):
        return matmul(a, layout(w[name]), layout(diff[name]), act, name, out_dtype, col_shards)

    meta = jnp.broadcast_to(diff["meta_tokens"][None], (B, N_META, D))
    h = jnp.concatenate([meta, x + token, jnp.zeros((B, Tp - T, D), F32)], axis=1).reshape(M, D)
    p = mm(h, "ev_w_in")
    sp = jax.nn.softplus(-diff["ev_lru_lambda"]).reshape(1, LRU_WIDTH)
    y_rec, qn, kvn, p_kpe = even_front(
        p.reshape(B, Tp, -1), diff["ev_conv_w"].reshape(CONV_WIDTH, LRU_WIDTH), diff["ev_conv_b"].reshape(1, LRU_WIDTH),
        diff["ev_w_rg_a"].reshape(LRU_HEADS, LRU_HEAD_DIM, LRU_HEAD_DIM), diff["ev_b_rg_a"].reshape(1, LRU_WIDTH),
        diff["ev_w_rg_x"].reshape(LRU_HEADS, LRU_HEAD_DIM, LRU_HEAD_DIM), diff["ev_b_rg_x"].reshape(1, LRU_WIDTH),
        sp, diff["ev_q_norm_g"].reshape(-1), diff["ev_kv_norm_g"].reshape(-1))
    y_rec = y_rec.reshape(M, LRU_WIDTH)
    q = mm(qn, "ev_w_uq", out_dtype=MXU_DTYPE, layout=_heads_to_lanes).reshape(B, Tp, -1)
    kv = mm(kvn, "ev_w_ukv", out_dtype=MXU_DTYPE).reshape(B, Tp, -1)
    kpe = jnp.pad(p_kpe.reshape(B, Tp, MLA_ROPE), ((0, 0), (0, 0), (MLA_NOPE, HEAD_LANES - MLA_NOPE - MLA_ROPE)))
    y_att = mla_attention(q, kv, kpe, _mla_rope_tables(pos)).reshape(M, -1)
    mix = mm(jnp.concatenate([y_rec, y_att], axis=-1), "ev_w_out", layout=_out_rows_to_lanes)
    return deepnorm(h, mix, diff["ln_mix_g"], diff["ln_mix_b"], "ln_mix0")


def _mlp0(diff, h, w):
    f = mlp(h, w["mlp_w1_0"], w["mlp_w2_0"], diff["mlp_w1_0"], diff["mlp_w2_0"], "mlp0")
    return deepnorm(h, f, diff["ln_mlp_g"], diff["ln_mlp_b"], "ln_mlp0")


def _layer1_loss(diff, h, w, tgt):
    B, S, T, Tp = _seq_dims(tgt)
    D = tgt.shape[-1]
    pos = jnp.arange(Tp, dtype=jnp.int32)

    def mm(a, name, out_dtype=F32, col_shards=1):
        return matmul(a, w[name], diff[name], False, name, out_dtype, col_shards)

    p = mm(h, "od_w_in", out_dtype=MXU_DTYPE, col_shards=N_CHIPS)
    cos, sin = _rope_tables(pos, RET_QK_DIM // 2)
    mix = mm(retention_mixer(p.reshape(B, Tp, -1), cos, sin), "od_w_out")
    h = deepnorm(h, mix, diff["ln_mix_g"], diff["ln_mix_b"], "ln_mix1")
    f = mlp(h, w["mlp_w1_1"], w["mlp_w2_1"], diff["mlp_w1_1"], diff["mlp_w2_1"], "mlp1")
    h = deepnorm(h, f, diff["ln_mlp_g"], diff["ln_mlp_b"], "ln_mlp1")
    y = h.reshape(B, Tp, D)[:, N_META:T].reshape(B * S, D)
    return loss_head(y, tgt.reshape(B * S, D))


_HBM = pl.BlockSpec(memory_space=pltpu.HBM)


def _place():
    return lax.axis_index("x"), lax.axis_index("y"), lax.axis_index("c")


def _other_chips(x, y):
    return [(1 - x, y), (x, 1 - y), (1 - x, 1 - y)]


def _chunks(rows, sublanes, most):
    for q in range(most, 0, -1):
        if rows % (q * sublanes) == 0:
            return q
    return 1


def _sublanes(dtype):
    return 8 * 4 // jnp.dtype(dtype).itemsize


def _gather_pieces(bufs):
    plan, first = [], []
    for b in bufs:
        Rh = b.shape[0] // 2
        Q = _chunks(Rh, _sublanes(b.dtype), 4) if Rh * b.shape[1] * b.dtype.itemsize > (1 << 20) else 1
        first.append(3 * sum(q for _, q, _ in plan))
        plan.append((Rh, Q, Rh // Q))
    return plan, first, 3 * sum(q for _, q, _ in plan)


def _allgather_chips(bufs, name):
    n = len(bufs)
    plan, first, n_sems = _gather_pieces(bufs)

    def body(*refs):
        x_refs, out_refs, (send_sems, recv_sems) = refs[:n], refs[n:2 * n], refs[2 * n:]
        x, y, c = _place()
        sibling = (x, y, 1 - c)
        chips = _other_chips(x, y)

        def copy(k, src, dst, to):
            return pltpu.make_async_remote_copy(src_ref=src, dst_ref=dst, send_sem=send_sems.at[k], recv_sem=recv_sems.at[k],
                                                device_id=to, device_id_type=MESH)

        def piece(i, cx, cy, hc, q):
            Rh, _, ch = plan[i]
            return out_refs[i].at[2 * cx + cy, pl.ds(hc * Rh + q * ch, ch), :]

        slots = [(i, q, j) for i in range(n) for q in range(plan[i][1]) for j in range(3)]
        sem = {(i, q, j): first[i] + 3 * q + j for i, q, j in slots}
        sent = [copy(sem[i, q, j], x_refs[i].at[pl.ds(c * plan[i][0] + q * plan[i][2], plan[i][2]), :], piece(i, x, y, c, q), (*chips[j], c))
                for i, q, j in slots]
        for cp in sent:
            cp.start()
        passed = []
        for i, q, j in slots:
            landed = piece(i, *chips[j], c, q)
            copy(sem[i, q, j], landed, landed, sibling).wait_recv()
            fwd = copy(n_sems + sem[i, q, j], landed, landed, sibling)
            fwd.start()
            passed.append(fwd)
        for i, q, j in slots:
            theirs = piece(i, *chips[j], 1 - c, q)
            copy(n_sems + sem[i, q, j], theirs, theirs, sibling).wait_recv()
        for cp in sent + passed:
            cp.wait_send()

    return pl.pallas_call(
        body, name=name, in_specs=[_HBM] * n, out_specs=[_HBM] * n,
        out_shape=[jax.ShapeDtypeStruct((N_CHIPS,) + b.shape, b.dtype) for b in bufs],
        scratch_shapes=[pltpu.SemaphoreType.DMA((2 * n_sems,)), pltpu.SemaphoreType.DMA((2 * n_sems,))],
    )(*bufs)


def _with_own(gathered, own):
    my = 2 * lax.axis_index("x") + lax.axis_index("y")
    return lax.dynamic_update_slice(gathered, own[None], (my, 0, 0))


def _sibling_exchange(ps, name):
    n = len(ps)

    def body(*refs):
        p_refs, out_refs, (send_sems, recv_sems) = refs[:n], refs[n:2 * n], refs[2 * n:]
        x, y, c = _place()
        copies = [pltpu.make_async_remote_copy(src_ref=p_ref.at[j, 1 - c], dst_ref=out_ref.at[j], send_sem=send_sems.at[N_CHIPS * i + j],
                                               recv_sem=recv_sems.at[N_CHIPS * i + j], device_id=(x, y, 1 - c), device_id_type=MESH)
                  for i, (p_ref, out_ref) in enumerate(zip(p_refs, out_refs)) for j in range(N_CHIPS)]
        for cp in copies:
            cp.start()
        for cp in copies:
            cp.wait()

    return pl.pallas_call(
        body, name=name, in_specs=[_HBM] * n, out_specs=[_HBM] * n,
        out_shape=[jax.ShapeDtypeStruct((N_CHIPS,) + p.shape[2:], p.dtype) for p in ps],
        scratch_shapes=[pltpu.SemaphoreType.DMA((N_CHIPS * n,)), pltpu.SemaphoreType.DMA((N_CHIPS * n,))],
    )(*ps)


def _chip_scatter(ss, name):
    n = len(ss)

    def body(*refs):
        s_refs, t_refs, (send_sems, recv_sems) = refs[:n], refs[n:2 * n], refs[2 * n:]
        x, y, c = _place()
        copies = [pltpu.make_async_remote_copy(src_ref=s_ref.at[j + 1], dst_ref=t_ref.at[j], send_sem=send_sems.at[3 * i + j],
                                               recv_sem=recv_sems.at[3 * i + j], device_id=(cx, cy, c), device_id_type=MESH)
                  for i, (s_ref, t_ref) in enumerate(zip(s_refs, t_refs)) for j, (cx, cy) in enumerate(_other_chips(x, y))]
        for cp in copies:
            cp.start()
        for cp in copies:
            cp.wait()

    return pl.pallas_call(
        body, name=name, in_specs=[_HBM] * n, out_specs=[_HBM] * n,
        out_shape=[jax.ShapeDtypeStruct((3,) + s.shape[1:], s.dtype) for s in ss],
        scratch_shapes=[pltpu.SemaphoreType.DMA((3 * n,)), pltpu.SemaphoreType.DMA((3 * n,))],
    )(*ss)


def _sibling_gather(fs, name):
    n = len(fs)

    def body(*refs):
        out_refs, (send_sems, recv_sems) = refs[n:2 * n], refs[2 * n:]
        x, y, c = _place()
        copies = [pltpu.make_async_remote_copy(src_ref=out_ref.at[c], dst_ref=out_ref.at[c], send_sem=send_sems.at[i], recv_sem=recv_sems.at[i],
                                               device_id=(x, y, 1 - c), device_id_type=MESH) for i, out_ref in enumerate(out_refs)]
        for cp in copies:
            cp.start()
        for cp in copies:
            cp.wait()

    return pl.pallas_call(
        body, name=name, in_specs=[_HBM] * n, out_specs=[_HBM] * n,
        out_shape=[jax.ShapeDtypeStruct(f.shape, f.dtype) for f in fs], input_output_aliases={i: i for i in range(n)},
        scratch_shapes=[pltpu.SemaphoreType.DMA((n,)), pltpu.SemaphoreType.DMA((n,))],
    )(*fs)


def _axis_scalar(name):
    return lax.axis_index(name).astype(jnp.int32).reshape(1)


def _add_own_half(p, got, out_dtype, name):
    n, _, R, C = p.shape
    tr = _pick(R, 512, 16)

    def body(x_ref, y_ref, c_ref, p_ref, g_ref, o_ref):
        o_ref[...] = (p_ref[...] + g_ref[...]).astype(out_dtype)

    def chip(r, x_ref, y_ref):
        return 2 * (x_ref[0] ^ (r & 1)) + (y_ref[0] ^ (r >> 1))

    grid_spec = pltpu.PrefetchScalarGridSpec(
        num_scalar_prefetch=3, grid=(n, R // tr),
        in_specs=[pl.BlockSpec((None, None, tr, C), lambda r, i, x_ref, y_ref, c_ref: (chip(r, x_ref, y_ref), c_ref[0], i, 0)),
                  pl.BlockSpec((None, tr, C), lambda r, i, x_ref, y_ref, c_ref: (chip(r, x_ref, y_ref), i, 0))],
        out_specs=pl.BlockSpec((None, tr, C), lambda r, i, x_ref, y_ref, c_ref: (r, i, 0)))
    return pl.pallas_call(body, name=name, grid_spec=grid_spec, out_shape=jax.ShapeDtypeStruct((n, R, C), out_dtype),
                          compiler_params=pltpu.CompilerParams(dimension_semantics=("parallel", "parallel")))(
        _axis_scalar("x"), _axis_scalar("y"), _axis_scalar("c"), p, got)


def _sum_partials(s, t, name):
    _, R, C = s.shape
    tr = _pick(R, 512, 16)

    def body(c_ref, s_ref, t_ref, o_ref):
        acc = s_ref[...].astype(F32)
        for j in range(3):
            acc = acc + t_ref[j].astype(F32)
        o_ref[...] = acc

    grid_spec = pltpu.PrefetchScalarGridSpec(
        num_scalar_prefetch=1, grid=(R // tr,),
        in_specs=[pl.BlockSpec((None, tr, C), lambda i, c_ref: (0, i, 0)), pl.BlockSpec((3, tr, C), lambda i, c_ref: (0, i, 0))],
        out_specs=pl.BlockSpec((None, tr, C), lambda i, c_ref: (c_ref[0], i, 0)))
    return pl.pallas_call(body, name=name, grid_spec=grid_spec, out_shape=jax.ShapeDtypeStruct((2, R, C), F32),
                          compiler_params=pltpu.CompilerParams(dimension_semantics=("parallel",)))(_axis_scalar("c"), s, t)


def _sibling_reduce(ps, wire_dtypes, tag):
    got = _sibling_exchange(ps, "grad_sibling_exchange_" + tag)
    return [_add_own_half(p, g, dt, "grad_sibling_add_%s%d" % (tag, i)) for i, (p, g, dt) in enumerate(zip(ps, got, wire_dtypes))]


_SEM = pl.BlockSpec(memory_space=pltpu.SEMAPHORE)
_ANY = pl.BlockSpec(memory_space=pl.ANY)
_EFFECT = pltpu.SideEffectType.DATAFLOW_SIDE_EFFECTING


def _in_hbm(a):
    return pltpu.with_memory_space_constraint(a, pltpu.HBM)


def _half_copies(x_refs, land_refs, send_sems, recv_sems, arriving):
    x, y, c = _place()
    copies = []
    for i, (x_ref, land_ref) in enumerate(zip(x_refs, land_refs)):
        Rh = x_ref.shape[0] // 2
        rows = pl.ds(c * Rh, Rh)
        for j, (cx, cy) in enumerate(_other_chips(x, y)):
            copies.append(pltpu.make_async_remote_copy(
                src_ref=x_ref.at[rows, :], dst_ref=land_ref.at[2 * cx + cy if arriving else 2 * x + y, rows, :],
                send_sem=send_sems.at[3 * i + j], recv_sem=recv_sems.at[3 * i + j], device_id=(cx, cy, c), device_id_type=MESH))
    return copies


def _allgather_start(bufs, name):
    n = len(bufs)

    def body(*refs):
        x_refs, land_refs, (send_sems, recv_sems), token = refs[:n], refs[n:2 * n], refs[2 * n:2 * n + 2], refs[-1]
        for cp in _half_copies(x_refs, land_refs, send_sems, recv_sems, False):
            cp.start()
        token[...] = jnp.zeros_like(token)

    lands = [lax.empty((N_CHIPS,) + b.shape, b.dtype) for b in bufs]
    out = pl.pallas_call(
        body, name=name,
        out_shape=(pltpu.SemaphoreType.DMA((3 * n,)), pltpu.SemaphoreType.DMA((3 * n,)), *[pltpu.HBM(a.shape, a.dtype) for a in bufs + lands],
                   jax.ShapeDtypeStruct((8, 128), F32)),
        in_specs=[_HBM] * (2 * n), out_specs=(_SEM, _SEM, *[_HBM] * (2 * n), pl.BlockSpec(memory_space=pltpu.VMEM)),
        input_output_aliases={i: 2 + i for i in range(2 * n)}, compiler_params=pltpu.CompilerParams(has_side_effects=_EFFECT),
    )(*[_in_hbm(a) for a in bufs + lands])
    return (out[0], out[1], list(out[2:2 + n]), list(out[2 + n:2 + 2 * n])), out[-1][0, 0]


def _allgather_wait(pending, after, name):
    send_sems, recv_sems, bufs, lands = pending
    n = len(bufs)

    def body(*refs):
        x_refs, land_refs, send_sems, recv_sems = refs[:n], refs[n:2 * n], refs[2 * n], refs[2 * n + 1]
        for cp in _half_copies(x_refs, land_refs, send_sems, recv_sems, False):
            cp.wait_send()
        for cp in _half_copies(x_refs, land_refs, send_sems, recv_sems, True):
            cp.wait_recv()

    out = pl.pallas_call(
        body, name=name, out_shape=tuple(pltpu.HBM(a.shape, a.dtype) for a in bufs + lands),
        in_specs=[_HBM] * (2 * n) + [_SEM, _SEM, _ANY], out_specs=tuple([_HBM] * (2 * n)), input_output_aliases={i: i for i in range(2 * n)},
        compiler_params=pltpu.CompilerParams(has_side_effects=_EFFECT),
    )(*bufs, *lands, send_sems, recv_sems, after)
    return list(out[n:])


def _sibling_forward(lands, name):
    n = len(lands)
    plan, first, n_sems = _gather_pieces([jax.ShapeDtypeStruct(l.shape[1:], l.dtype) for l in lands])

    def body(*refs):
        out_refs, (send_sems, recv_sems) = refs[n:2 * n], refs[2 * n:]
        x, y, c = _place()

        def copies(hc):
            return [pltpu.make_async_remote_copy(
                        src_ref=out_refs[i].at[2 * cx + cy, pl.ds(hc * plan[i][0] + q * plan[i][2], plan[i][2]), :],
                        dst_ref=out_refs[i].at[2 * cx + cy, pl.ds(hc * plan[i][0] + q * plan[i][2], plan[i][2]), :],
                        send_sem=send_sems.at[first[i] + 3 * q + j], recv_sem=recv_sems.at[first[i] + 3 * q + j],
                        device_id=(x, y, 1 - c), device_id_type=MESH)
                    for i in range(n) for q in range(plan[i][1]) for j, (cx, cy) in enumerate(_other_chips(x, y))]

        mine = copies(c)
        for cp in mine:
            cp.start()
        for cp in mine:
            cp.wait_send()
        for cp in copies(1 - c):
            cp.wait_recv()

    return pl.pallas_call(
        body, name=name, in_specs=[_HBM] * n, out_specs=[_HBM] * n, out_shape=[jax.ShapeDtypeStruct(l.shape, l.dtype) for l in lands],
        input_output_aliases={i: i for i in range(n)},
        scratch_shapes=[pltpu.SemaphoreType.DMA((n_sems,)), pltpu.SemaphoreType.DMA((n_sems,))],
    )(*lands)


N_PEERS = 7


def _direct_copies(p_refs, t_refs, send_sems, recv_sems):
    x, y, c = _place()
    copies = []
    for i, (p_ref, t_ref) in enumerate(zip(p_refs, t_refs)):
        for f in range(1, N_PEERS + 1):
            px, py, pc = x ^ (f >> 2), y ^ ((f >> 1) & 1), c ^ (f & 1)
            copies.append(pltpu.make_async_remote_copy(
                src_ref=p_ref.at[2 * px + py, pc], dst_ref=t_ref.at[f - 1], send_sem=send_sems.at[N_PEERS * i + f - 1],
                recv_sem=recv_sems.at[N_PEERS * i + f - 1], device_id=(px, py, pc), device_id_type=MESH))
    return copies


def _direct_scatter_start(ps, name):
    n = len(ps)

    def body(*refs):
        p_refs, t_refs, (send_sems, recv_sems), token = refs[:n], refs[n:2 * n], refs[2 * n:2 * n + 2], refs[-1]
        for cp in _direct_copies(p_refs, t_refs, send_sems, recv_sems):
            cp.start()
        token[...] = jnp.zeros_like(token)

    lands = [lax.empty((N_PEERS,) + p.shape[2:], p.dtype) for p in ps]
    out = pl.pallas_call(
        body, name=name,
        out_shape=(pltpu.SemaphoreType.DMA((N_PEERS * n,)), pltpu.SemaphoreType.DMA((N_PEERS * n,)),
                   *[pltpu.HBM(a.shape, a.dtype) for a in ps + lands], jax.ShapeDtypeStruct((8, 128), F32)),
        in_specs=[_HBM] * (2 * n), out_specs=(_SEM, _SEM, *[_HBM] * (2 * n), pl.BlockSpec(memory_space=pltpu.VMEM)),
        input_output_aliases={i: 2 + i for i in range(2 * n)}, compiler_params=pltpu.CompilerParams(has_side_effects=_EFFECT),
    )(*[_in_hbm(a) for a in ps + lands])
    return (out[0], out[1], list(out[2:2 + n]), list(out[2 + n:2 + 2 * n])), out[-1][0, 0]


def _direct_scatter_wait(pending, after, name):
    send_sems, recv_sems, ps, lands = pending
    n = len(ps)

    def body(*refs):
        p_refs, t_refs, send_sems, recv_sems = refs[:n], refs[n:2 * n], refs[2 * n], refs[2 * n + 1]
        for cp in _direct_copies(p_refs, t_refs, send_sems, recv_sems):
            cp.wait_send()
            cp.wait_recv()

    out = pl.pallas_call(
        body, name=name, out_shape=tuple(pltpu.HBM(a.shape, a.dtype) for a in ps + lands),
        in_specs=[_HBM] * (2 * n) + [_SEM, _SEM, _ANY], out_specs=tuple([_HBM] * (2 * n)),
        input_output_aliases={i: i for i in range(2 * n)}, compiler_params=pltpu.CompilerParams(has_side_effects=_EFFECT),
    )(*ps, *lands, send_sems, recv_sems, after)
    return list(out[:n]), list(out[n:])


def _sum_direct(p, t, name):
    _, _, R, C = p.shape
    tr = _pick(R, 512, 16)

    def body(x_ref, y_ref, c_ref, p_ref, t_ref, o_ref):
        acc = p_ref[...].astype(F32)
        for f in range(N_PEERS):
            acc = acc + t_ref[f].astype(F32)
        o_ref[...] = acc

    grid_spec = pltpu.PrefetchScalarGridSpec(
        num_scalar_prefetch=3, grid=(R // tr,),
        in_specs=[pl.BlockSpec((None, None, tr, C), lambda i, x_ref, y_ref, c_ref: (2 * x_ref[0] + y_ref[0], c_ref[0], i, 0)),
                  pl.BlockSpec((N_PEERS, tr, C), lambda i, x_ref, y_ref, c_ref: (0, i, 0))],
        out_specs=pl.BlockSpec((None, tr, C), lambda i, x_ref, y_ref, c_ref: (c_ref[0], i, 0)))
    return pl.pallas_call(body, name=name, grid_spec=grid_spec, out_shape=jax.ShapeDtypeStruct((2, R, C), F32),
                          compiler_params=pltpu.CompilerParams(dimension_semantics=("parallel",)))(
        _axis_scalar("x"), _axis_scalar("y"), _axis_scalar("c"), p, t)


def _adamw(w, g, m, v, name):
    R, C = w.shape
    tr = _pick(R, 256, 8)

    def body(w_ref, g_ref, m_ref, v_ref, d_ref, nm_ref, nv_ref):
        g_ = g_ref[...]
        m_ = ADAM_B1 * m_ref[...] + (1.0 - ADAM_B1) * g_
        v_ = ADAM_B2 * v_ref[...] + (1.0 - ADAM_B2) * (g_ * g_)
        m_hat = m_ / (1.0 - ADAM_B1 ** ADAM_STEP)
        v_hat = v_ / (1.0 - ADAM_B2 ** ADAM_STEP)
        d_ref[...] = -ADAM_LR * (m_hat / (jnp.sqrt(v_hat) + ADAM_EPS) + ADAM_WD * w_ref[...])
        nm_ref[...] = m_
        nv_ref[...] = v_

    row = pl.BlockSpec((tr, C), lambda i: (i, 0))
    shp = jax.ShapeDtypeStruct((R, C), F32)
    return pl.pallas_call(body, name=name, grid=(R // tr,), in_specs=[row] * 4, out_specs=[row] * 3, out_shape=[shp] * 3,
                          compiler_params=pltpu.CompilerParams(dimension_semantics=("parallel",)))(w, g, m, v)


BIG_SPECS = (("ev_w_in", 1024, 1440, 1), ("ev_w_uq", 256, 768, 1), ("ev_w_ukv", 128, 1024, 1), ("ev_w_out", 1024, 1024, 0),
             ("od_w_in", 1024, 6144, 1), ("od_w_out", 2048, 1024, 0), ("mlp_w1_0", 1024, 4096, 1), ("mlp_w1_1", 1024, 4096, 1),
             ("mlp_w2_0", 4096, 1024, 0), ("mlp_w2_1", 4096, 1024, 0))
BIG_PARAMS = (("ev_w_in", ("ev_w_in",)), ("ev_w_uq", ("ev_w_uq",)), ("ev_w_ukv", ("ev_w_ukv",)), ("ev_w_out", ("ev_w_out",)),
              ("od_w_in", ("od_w_in",)), ("od_w_out", ("od_w_out",)), ("mlp_w1", ("mlp_w1_0", "mlp_w1_1")),
              ("mlp_w2", ("mlp_w2_0", "mlp_w2_1")))
REPLICATED = ("ev_conv_b", "ev_w_rg_a", "ev_b_rg_a", "ev_w_rg_x", "ev_b_rg_x", "ev_lru_lambda", "ev_q_norm_g", "ev_kv_norm_g",
              "ln_mix_g", "ln_mix_b", "ln_mlp_g", "ln_mlp_b")
SMALL_SHARDED = ("meta_tokens", "ev_conv_w")
COL_SHARD_GRADS = ("od_w_in", "mlp_w1_0", "mlp_w1_1")
MATRIX_GROUPS = (("ev_w_in", "ev_w_uq", "ev_w_ukv", "ev_w_out"), ("mlp_w1_0", "mlp_w2_0"), ("od_w_in", "od_w_out", "mlp_w1_1", "mlp_w2_1"))
LAYER_NORMS = ("ln_mix_g", "ln_mix_b", "ln_mlp_g", "ln_mlp_b")
WEIGHT_NAMES = ("meta_tokens", "ev_w_in", "ev_conv_w", "ev_conv_b", "ev_w_rg_a", "ev_b_rg_a", "ev_w_rg_x", "ev_b_rg_x",
                "ev_lru_lambda", "ev_q_norm_g", "ev_w_uq", "ev_kv_norm_g", "ev_w_ukv", "ev_w_out", "od_w_in", "od_w_out",
                "ln_mix_g", "ln_mix_b", "mlp_w1", "mlp_w2", "ln_mlp_g", "ln_mlp_b")


def _to_rows(flat, row_align):
    n = flat.shape[-1]
    rows = _round_up(-(-n // PACK_COLS), row_align)
    pad = rows * PACK_COLS - n
    if pad:
        flat = jnp.pad(flat, [(0, 0)] * (flat.ndim - 1) + [(0, pad)])
    return flat.reshape(flat.shape[:-1] + (rows, PACK_COLS))


def _shard_shape(K, N, axis):
    return (K // N_CHIPS, N) if axis == 0 else (K, N // N_CHIPS)


def _gather_shards(stacked, K, N, axis):
    if axis == 0:
        return stacked.reshape(K, N)
    return stacked.transpose(1, 0, 2).reshape(K, N)


def _split_shards(full, K, N, axis):
    if axis == 0:
        return full.reshape(N_CHIPS, -1)
    return full.reshape(K, N_CHIPS, N // N_CHIPS).transpose(1, 0, 2).reshape(N_CHIPS, -1)


def kernel(x, meta_tokens, ev_w_in, ev_conv_w, ev_conv_b, ev_w_rg_a, ev_b_rg_a, ev_w_rg_x, ev_b_rg_x, ev_lru_lambda, ev_q_norm_g, ev_w_uq, ev_kv_norm_g, ev_w_ukv, ev_w_out, od_w_in, od_w_out, ln_mix_g, ln_mix_b, mlp_w1, mlp_w2, ln_mlp_g, ln_mlp_b, loss_target, m_meta_tokens, m_ev_w_in, m_ev_conv_w, m_ev_conv_b, m_ev_w_rg_a, m_ev_b_rg_a, m_ev_w_rg_x, m_ev_b_rg_x, m_ev_lru_lambda, m_ev_q_norm_g, m_ev_w_uq, m_ev_kv_norm_g, m_ev_w_ukv, m_ev_w_out, m_od_w_in, m_od_w_out, m_ln_mix_g, m_ln_mix_b, m_mlp_w1, m_mlp_w2, m_ln_mlp_g, m_ln_mlp_b, v_meta_tokens, v_ev_w_in, v_ev_conv_w, v_ev_conv_b, v_ev_w_rg_a, v_ev_b_rg_a, v_ev_w_rg_x, v_ev_b_rg_x, v_ev_lru_lambda, v_ev_q_norm_g, v_ev_w_uq, v_ev_kv_norm_g, v_ev_w_ukv, v_ev_w_out, v_od_w_in, v_od_w_out, v_ln_mix_g, v_ln_mix_b, v_mlp_w1, v_mlp_w2, v_ln_mlp_g, v_ln_mlp_b):
    given = dict(locals())
    local_big = {"ev_w_in": ev_w_in[0], "ev_w_uq": ev_w_uq[0], "ev_w_ukv": ev_w_ukv[0], "ev_w_out": ev_w_out[0],
                 "od_w_in": od_w_in[0], "od_w_out": od_w_out[0], "mlp_w1_0": mlp_w1[0], "mlp_w1_1": mlp_w1[1],
                 "mlp_w2_0": mlp_w2[0], "mlp_w2_1": mlp_w2[1]}

    specs = {spec[0]: spec for spec in BIG_SPECS}
    mixer0_m, mlp0_m, layer1_m = MATRIX_GROUPS

    def shards(names):
        return [local_big[n].astype(MXU_DTYPE) for n in names]

    def whole(stacked, n):
        _, K, N, ax = specs[n]
        return stacked if n in COL_SHARD_GRADS else _gather_shards(stacked, K, N, ax)

    def filled(gathered, own, names):
        return {n: whole(_with_own(g_, o_), n) for n, g_, o_ in zip(names, gathered, own)}

    own_a, own_b, own_c = shards(mixer0_m), shards(mlp0_m), shards(layer1_m)
    small = [meta_tokens, jnp.pad(ev_conv_w[0], ((0, 16 - CONV_WIDTH), (0, 0)))]
    gathered_a = _allgather_chips(own_a + small, "weight_allgather_mixer0")
    pending_b, token1 = _allgather_start(own_b, "weight_allgather_mlp0_start")
    pending_c, token2 = _allgather_start(own_c, "weight_allgather_layer1_start")
    meta_full = _gather_shards(_with_own(gathered_a[-2], small[0]), N_META, D_MODEL, 1)
    conv_full = _gather_shards(_with_own(gathered_a[-1], small[1])[:, :CONV_WIDTH], CONV_WIDTH, LRU_WIDTH, 1)

    def slots(names, dtype):
        return {n: jnp.zeros((N_CHIPS, specs[n][1], specs[n][2] // N_CHIPS) if n in COL_SHARD_GRADS else specs[n][1:3], dtype) for n in names}

    def norms(names, layer):
        return {n: given[n][layer] for n in names}

    def finish_gather(pending, own, after, names, tag):
        landed = _allgather_wait(pending, lax.stop_gradient(after), "weight_allgather_%s_wait" % tag)
        return filled(_sibling_forward(landed, "weight_allgather_%s_forward" % tag), own, names)

    diff_a = {**slots(mixer0_m, F32), **norms(("ln_mix_g", "ln_mix_b"), 0), **{n: given[n] for n in REPLICATED if n not in LAYER_NORMS},
              "x": x, "meta_tokens": meta_full, "ev_conv_w": conv_full}
    diff_b = {**slots(mlp0_m, MXU_DTYPE), **norms(("ln_mlp_g", "ln_mlp_b"), 0)}
    diff_c = {**slots(layer1_m, MXU_DTYPE), **norms(LAYER_NORMS, 1)}
    w_a = filled(gathered_a[:len(mixer0_m)], own_a, mixer0_m)
    h_a, back_a = jax.vjp(lambda d: _mixer0(d, w_a, token1 + token2), diff_a)
    w_b = finish_gather(pending_b, own_b, h_a, mlp0_m, "mlp0")
    h_b, back_b = jax.vjp(lambda d, hh: _mlp0(d, hh, w_b), diff_b, h_a)
    w_c = finish_gather(pending_c, own_c, h_b, layer1_m, "layer1")
    loss, back_c = jax.vjp(lambda d, hh: _layer1_loss(d, hh, w_c, loss_target), diff_c, h_b)
    loss = lax.psum(loss, ("x", "y", "c"))

    def blocks_of(grad, n):
        _, K, N, ax = specs[n]
        if n in COL_SHARD_GRADS:
            blocks = grad
        elif ax == 0:
            blocks = grad.reshape(N_CHIPS, K // N_CHIPS, N)
        else:
            blocks = grad.reshape(K, N_CHIPS, N // N_CHIPS).transpose(1, 0, 2)
        return blocks.reshape(N_CHIPS, 2, blocks.shape[1] // 2, blocks.shape[2])

    def start_reduce(grads_of, names, tag):
        return _direct_scatter_start([blocks_of(grads_of[n], n) for n in names], "grad_scatter_%s_start" % tag)

    g_c, dh = back_c(jnp.ones((), F32))
    flying_c, token = start_reduce(g_c, layer1_m, "layer1")
    g_b, dh = back_b(dh + token)
    flying_b, token = start_reduce(g_b, mlp0_m, "mlp0")
    (g_a,) = back_a(dh + token)
    ps_c, ts_c = _direct_scatter_wait(flying_c, g_a["x"], "grad_scatter_layer1_wait")
    ps_b, ts_b = _direct_scatter_wait(flying_b, g_a["x"], "grad_scatter_mlp0_wait")

    g = {**g_a, **g_b, **g_c}
    g.update({n: jnp.stack([(g_b if n in g_b else g_a)[n], g_c[n]]) for n in LAYER_NORMS})
    repl = jnp.concatenate([g[n].reshape(-1) for n in REPLICATED]).reshape(N_CHIPS, -1)
    small = [_split_shards(g["meta_tokens"], N_META, D_MODEL, 1), _split_shards(g["ev_conv_w"], CONV_WIDTH, LRU_WIDTH, 1), repl]
    small = [pc.reshape(N_CHIPS, 2, -1) for pc in small]
    n_small = sum(pc.shape[2] for pc in small)
    small.append(jnp.zeros((N_CHIPS, 2, _round_up(n_small, 32 * PACK_COLS) - n_small), F32))
    p_small = jnp.concatenate(small, axis=2).reshape(N_CHIPS, 2, -1, PACK_COLS)
    ss_a = _sibling_reduce([blocks_of(g_a[n], n) for n in mixer0_m] + [p_small], [MXU_DTYPE] * len(mixer0_m) + [F32], "mixer0_")
    ts_a = list(_chip_scatter(ss_a, "grad_chip_scatter_mixer0"))
    fs = [_sum_partials(s, t, "grad_chip_sum_mixer0_%d" % i) for i, (s, t) in enumerate(zip(ss_a, ts_a))]
    fs += [_sum_direct(p, t, "grad_sum_%d" % i) for i, (p, t) in enumerate(zip(ps_b + ps_c, ts_b + ts_c))]
    reduced = _sibling_gather(fs, "grad_sibling_gather")
    red_big = dict(zip(mixer0_m + ("small",) + mlp0_m + layer1_m, reduced))
    red_small = red_big.pop("small").reshape(2, -1)

    grads = {}
    for name, parts in BIG_PARAMS:
        grads[name] = jnp.stack([red_big[part].reshape(given[name].shape[1:]) for part in parts])

    def take(off, sz):
        return jnp.concatenate([red_small[0, off // 2:(off + sz) // 2], red_small[1, off // 2:(off + sz) // 2]])

    off = 0
    for name in SMALL_SHARDED:
        sz = given[name].size
        grads[name] = take(off, sz).reshape(given[name].shape)
        off += sz
    n_repl = repl.shape[1]
    own_repl = _to_rows(take(off, n_repl), 16)
    repl_all = _with_own(_allgather_chips([own_repl], "replicated_allgather")[0], own_repl).reshape(N_CHIPS, -1)[:, :n_repl].reshape(-1)
    off = 0
    for name in REPLICATED:
        sz = given[name].size
        grads[name] = repl_all[off:off + sz].reshape(given[name].shape)
        off += sz

    delta, new_m, new_v = {}, {}, {}
    for name, _ in BIG_PARAMS:
        shp = given[name].shape
        two_d = (-1, shp[-1])
        d, nm, nv = _adamw(given[name].reshape(two_d), grads[name].reshape(two_d), given["m_" + name].reshape(two_d),
                           given["v_" + name].reshape(two_d), "adamw_" + name)
        delta[name], new_m[name], new_v[name] = d.reshape(shp), nm.reshape(shp), nv.reshape(shp)
    smalls = SMALL_SHARDED + REPLICATED

    def pack_small(get):
        return _to_rows(jnp.concatenate([get(n).reshape(-1) for n in smalls]), 8)

    outs = _adamw(pack_small(lambda n: given[n]), pack_small(lambda n: grads[n]), pack_small(lambda n: given["m_" + n]),
                  pack_small(lambda n: given["v_" + n]), "adamw_small")
    for res, flat in zip((delta, new_m, new_v), outs):
        flat, off = flat.reshape(-1), 0
        for n in smalls:
            sz = given[n].size
            res[n] = flat[off:off + sz].reshape(given[n].shape)
            off += sz

    return (loss, g_a["x"], *[grads[n] for n in WEIGHT_NAMES], *[delta[n] for n in WEIGHT_NAMES],
            *[new_m[n] for n in WEIGHT_NAMES], *[new_v[n] for n in WEIGHT_NAMES])
```

```python
import functools
import math

import jax
import jax.numpy as jnp
from jax import lax
from jax.experimental import pallas as pl
from jax.experimental.pallas import tpu as pltpu

F32 = jnp.float32
MXU_DTYPE = jnp.bfloat16

D_MODEL = 1024
N_META = 16
LRU_WIDTH = 512
LRU_HEADS = 4
LRU_HEAD_DIM = 128
CONV_WIDTH = 4
LRU_C = 8.0
MLA_HEADS = 8
MLA_NOPE = 64
MLA_ROPE = 32
MLA_V = 64
MLA_Q_RANK = 256
MLA_KV_RANK = 128
RET_HEADS = 4
RET_QK_DIM = 256
RET_V_DIM = 512
D_FF = 4096
ROPE_BASE = 10000.0
DN_ALPHA = 4.0 ** 0.25
EPS = 1e-5
NEG_INF = -1e30
SEQ_BLOCK = 128

ADAM_LR = 0.001
ADAM_B1 = 0.9
ADAM_B2 = 0.999
ADAM_EPS = 1e-08
ADAM_WD = 0.01
ADAM_STEP = 10

PACK_COLS = 1024
N_CHIPS = 4

MESH = pl.DeviceIdType.MESH


def _pick(n, target, align):
    best = None
    for t in range(align, min(n, target) + 1, align):
        if n % t == 0:
            best = t
    return n if best is None else best


def _round_up(n, m):
    return (n + m - 1) // m * m


def _relu2(a):
    r = jnp.maximum(a, 0.0)
    return r * r


def _mm_nn(a, w, act, name, out_dtype=F32):
    M, K = a.shape
    sharded = w.ndim == 3
    n = w.shape[-1]
    N = n * (w.shape[0] if sharded else 1)
    tm = _pick(M, 1088 if K * a.dtype.itemsize <= 4096 else 544, 8)
    tn = _pick(n, 1024, 128)
    per = n // tn

    def body(a_ref, w_ref, o_ref):
        av = a_ref[...]
        if act:
            av = _relu2(av.astype(F32))
        o_ref[...] = jnp.dot(av.astype(MXU_DTYPE), w_ref[...].astype(MXU_DTYPE), preferred_element_type=F32).astype(out_dtype)

    w_spec = pl.BlockSpec((None, K, tn), lambda i, j: (j // per, 0, j % per)) if sharded else pl.BlockSpec((K, tn), lambda i, j: (0, j))
    return pl.pallas_call(
        body, name=name,
        grid=(M // tm, N // tn),
        in_specs=[pl.BlockSpec((tm, K), lambda i, j: (i, 0)), w_spec],
        out_specs=pl.BlockSpec((tm, tn), lambda i, j: (i, j)),
        out_shape=jax.ShapeDtypeStruct((M, N), out_dtype),
        compiler_params=pltpu.CompilerParams(dimension_semantics=("parallel", "arbitrary")),
    )(a, w)


def _mm_nt(g, w, a_src, name, out_dtype=F32):
    M, N = g.shape
    sharded = w.ndim == 3
    K, n = w.shape[-2], w.shape[-1]
    if sharded:
        tk, nk = N, 1
    else:
        tk = N if N * g.dtype.itemsize <= 8192 else _pick(N, 2048, 128)
        nk = N // tk
    tm = _pick(M, 1088 if tk * g.dtype.itemsize <= 4096 else 544, 8)
    tn = _pick(K, 1024, 128)
    has_src = a_src is not None
    assert nk == 1 or out_dtype == F32

    def body(*refs):
        if has_src:
            g_ref, w_ref, s_ref, o_ref = refs
        else:
            g_ref, w_ref, o_ref = refs
        nt = (((1,), (1,)), ((), ()))
        if sharded:
            r = sum(lax.dot_general(g_ref[:, s * n:(s + 1) * n].astype(MXU_DTYPE), w_ref[s].astype(MXU_DTYPE), nt, preferred_element_type=F32)
                    for s in range(w_ref.shape[0]))
        else:
            r = lax.dot_general(g_ref[...].astype(MXU_DTYPE), w_ref[...].astype(MXU_DTYPE), nt, preferred_element_type=F32)
        if has_src:
            r = r * (2.0 * jnp.maximum(s_ref[...].astype(F32), 0.0))
        if nk == 1:
            o_ref[...] = r.astype(out_dtype)
        else:
            k = pl.program_id(2)

            @pl.when(k == 0)
            def _():
                o_ref[...] = r

            @pl.when(k > 0)
            def _():
                o_ref[...] += r

    w_spec = (pl.BlockSpec((w.shape[0], tn, n), lambda i, j, k: (0, j, 0)) if sharded
              else pl.BlockSpec((tn, tk), lambda i, j, k: (j, k)))
    in_specs = [pl.BlockSpec((tm, tk), lambda i, j, k: (i, k)), w_spec]
    args = [g, w]
    if has_src:
        assert nk == 1
        in_specs.append(pl.BlockSpec((tm, tn), lambda i, j, k: (i, j)))
        args.append(a_src)
    return pl.pallas_call(
        body, name=name,
        grid=(M // tm, K // tn, nk),
        in_specs=in_specs,
        out_specs=pl.BlockSpec((tm, tn), lambda i, j, k: (i, j)),
        out_shape=jax.ShapeDtypeStruct((M, K), out_dtype),
        compiler_params=pltpu.CompilerParams(dimension_semantics=("parallel", "parallel", "arbitrary")),
    )(*args)


def _mm_tn(a, g, act, name, col_shards=1, out_dtype=F32):
    M, K = a.shape
    _, N = g.shape
    n = N // col_shards
    tm, tn, tk = _pick(K, 1024, 128), _pick(n, 1024, 128), _pick(M, 2176, 8)
    nk = M // tk
    per = n // tn
    direct = out_dtype == F32

    def body(a_ref, g_ref, o_ref, *scratch):
        acc_ref = o_ref if direct else scratch[0]
        k = pl.program_id(2)
        av = a_ref[...]
        if act:
            av = _relu2(av.astype(F32))
        r = lax.dot_general(av.astype(MXU_DTYPE), g_ref[...].astype(MXU_DTYPE),
                            (((0,), (0,)), ((), ())), preferred_element_type=F32)

        @pl.when(k == 0)
        def _():
            acc_ref[...] = r

        @pl.when(k > 0)
        def _():
            acc_ref[...] += r

        if not direct:
            @pl.when(k == nk - 1)
            def _():
                o_ref[...] = acc_ref[...].astype(out_dtype)

    if col_shards == 1:
        out_spec, out_shape = pl.BlockSpec((tm, tn), lambda i, j, k: (i, j)), (K, N)
    else:
        out_spec, out_shape = pl.BlockSpec((None, tm, tn), lambda i, j, k: (j // per, i, j % per)), (col_shards, K, n)
    return pl.pallas_call(
        body, name=name,
        grid=(K // tm, N // tn, nk),
        in_specs=[pl.BlockSpec((tk, tm), lambda i, j, k: (k, i)), pl.BlockSpec((tk, tn), lambda i, j, k: (k, j))],
        out_specs=out_spec,
        out_shape=jax.ShapeDtypeStruct(out_shape, out_dtype),
        scratch_shapes=[] if direct else [pltpu.VMEM((tm, tn), F32)],
        compiler_params=pltpu.CompilerParams(dimension_semantics=("parallel", "parallel", "arbitrary")),
    )(a, g)


@functools.partial(jax.custom_vjp, nondiff_argnums=(3, 4, 5, 6))
def matmul(a, w, w_grad_slot, act, name, out_dtype, col_shards):
    return _mm_nn(a, w, act, name + "_fwd", out_dtype)


def _matmul_fwd(a, w, w_grad_slot, act, name, out_dtype, col_shards):
    return _mm_nn(a, w, act, name + "_fwd", out_dtype), (a, w, jnp.zeros((), w_grad_slot.dtype))


def _matmul_bwd(act, name, out_dtype, col_shards, res, g):
    a, w, slot_like = res
    w_grad_dtype = slot_like.dtype
    da = _mm_nt(g, w, a if act else None, name + "_dx")
    dw = _mm_tn(a, g, act, name + "_dw", col_shards, w_grad_dtype)
    return da, None, dw


matmul.defvjp(_matmul_fwd, _matmul_bwd)


@functools.partial(jax.custom_vjp, nondiff_argnums=(5,))
def mlp(h, w1, w2, w1_grad_slot, w2_grad_slot, name):
    u = _mm_nn(h, w1, False, name + "_w1_fwd", out_dtype=MXU_DTYPE)
    return _mm_nn(u, w2, True, name + "_w2_fwd")


def _mlp_fwd(h, w1, w2, w1_grad_slot, w2_grad_slot, name):
    u = _mm_nn(h, w1, False, name + "_w1_fwd", out_dtype=MXU_DTYPE)
    return _mm_nn(u, w2, True, name + "_w2_fwd"), (h, u, w1, w2, jnp.zeros((), w1_grad_slot.dtype))


def _mlp_bwd(name, res, df):
    h, u, w1, w2, slot_like = res
    du = _mm_nt(df, w2, u, name + "_w2_dx", out_dtype=MXU_DTYPE)
    dw2 = _mm_tn(u, df, True, name + "_w2_dw", 1, slot_like.dtype)
    dh = _mm_nt(du, w1, None, name + "_w1_dx")
    dw1 = _mm_tn(h, du, False, name + "_w1_dw", N_CHIPS, slot_like.dtype)
    return dh, None, None, dw1, dw2


mlp.defvjp(_mlp_fwd, _mlp_bwd)


def _ln_stats(z):
    mu = jnp.mean(z, axis=-1, keepdims=True)
    zc = z - mu
    var = jnp.mean(zc * zc, axis=-1, keepdims=True)
    return zc, lax.rsqrt(var + EPS)


def _ln_fwd_call(resid, branch, g, b, name):
    M, D = resid.shape
    tm = _pick(M, 544, 8)

    def body(r_ref, br_ref, g_ref, b_ref, o_ref):
        zc, rstd = _ln_stats(DN_ALPHA * r_ref[...] + br_ref[...])
        o_ref[...] = zc * rstd * g_ref[...] + b_ref[...]

    row = pl.BlockSpec((tm, D), lambda i: (i, 0))
    vec = pl.BlockSpec((1, D), lambda i: (0, 0))
    return pl.pallas_call(
        body, name=name, grid=(M // tm,), in_specs=[row, row, vec, vec], out_specs=row,
        out_shape=jax.ShapeDtypeStruct((M, D), F32),
        compiler_params=pltpu.CompilerParams(dimension_semantics=("parallel",)),
    )(resid, branch, g.reshape(1, D), b.reshape(1, D))


def _ln_bwd_call(resid, branch, g, dy, name):
    M, D = resid.shape
    tm = _pick(M, 544, 8)

    def body(r_ref, br_ref, g_ref, dy_ref, dz_ref, dg_ref, db_ref):
        @pl.when(pl.program_id(0) == 0)
        def _():
            dg_ref[...] = jnp.zeros_like(dg_ref)
            db_ref[...] = jnp.zeros_like(db_ref)

        zc, rstd = _ln_stats(DN_ALPHA * r_ref[...] + br_ref[...])
        xhat = zc * rstd
        dy = dy_ref[...]
        dxh = dy * g_ref[...]
        m1 = jnp.mean(dxh, axis=-1, keepdims=True)
        m2 = jnp.mean(dxh * xhat, axis=-1, keepdims=True)
        dz_ref[...] = rstd * (dxh - m1 - xhat * m2)
        dg_ref[...] += jnp.sum(dy * xhat, axis=0, keepdims=True)
        db_ref[...] += jnp.sum(dy, axis=0, keepdims=True)

    row = pl.BlockSpec((tm, D), lambda i: (i, 0))
    vec = pl.BlockSpec((1, D), lambda i: (0, 0))
    return pl.pallas_call(
        body, name=name, grid=(M // tm,), in_specs=[row, row, vec, row], out_specs=[row, vec, vec],
        out_shape=[jax.ShapeDtypeStruct((M, D), F32), jax.ShapeDtypeStruct((1, D), F32), jax.ShapeDtypeStruct((1, D), F32)],
        compiler_params=pltpu.CompilerParams(dimension_semantics=("arbitrary",)),
    )(resid, branch, g.reshape(1, D), dy)


@functools.partial(jax.custom_vjp, nondiff_argnums=(4,))
def deepnorm(resid, branch, g, b, name):
    return _ln_fwd_call(resid, branch, g, b, name + "_fwd")


def _deepnorm_fwd(resid, branch, g, b, name):
    return _ln_fwd_call(resid, branch, g, b, name + "_fwd"), (resid, branch, g)


def _deepnorm_bwd(name, res, dy):
    resid, branch, g = res
    dz, dg, db = _ln_bwd_call(resid, branch, g, dy, name + "_bwd")
    return DN_ALPHA * dz, dz, dg.reshape(g.shape), db.reshape(g.shape)


deepnorm.defvjp(_deepnorm_fwd, _deepnorm_bwd)


def _rms_fwd_call(x, g, name, col_block=0):
    R = x.shape[0]
    W = g.shape[-1]
    tr = _pick(R, 1088, 8)

    def body(x_ref, g_ref, o_ref):
        xv = x_ref[...]
        rstd = lax.rsqrt(jnp.mean(xv * xv, axis=-1, keepdims=True) + EPS)
        o_ref[...] = xv * rstd * g_ref[...]

    vec = pl.BlockSpec((1, W), lambda i: (0, 0))
    return pl.pallas_call(
        body, name=name, grid=(R // tr,), in_specs=[pl.BlockSpec((tr, W), lambda i: (i, col_block)), vec],
        out_specs=pl.BlockSpec((tr, W), lambda i: (i, 0)), out_shape=jax.ShapeDtypeStruct((R, W), F32),
        compiler_params=pltpu.CompilerParams(dimension_semantics=("parallel",)),
    )(x, g.reshape(1, W))


def _rms_bwd_call(x, g, dy, name, col_block=0):
    R = x.shape[0]
    W = g.shape[-1]
    tr = _pick(R, 1088, 8)

    def body(x_ref, g_ref, dy_ref, dx_ref, dg_ref):
        @pl.when(pl.program_id(0) == 0)
        def _():
            dg_ref[...] = jnp.zeros_like(dg_ref)

        xv = x_ref[...]
        rstd = lax.rsqrt(jnp.mean(xv * xv, axis=-1, keepdims=True) + EPS)
        xhat = xv * rstd
        dy = dy_ref[...]
        dxh = dy * g_ref[...]
        dx_ref[...] = rstd * (dxh - xhat * jnp.mean(dxh * xhat, axis=-1, keepdims=True))
        dg_ref[...] += jnp.sum(dy * xhat, axis=0, keepdims=True)

    row = pl.BlockSpec((tr, W), lambda i: (i, 0))
    vec = pl.BlockSpec((1, W), lambda i: (0, 0))
    return pl.pallas_call(
        body, name=name, grid=(R // tr,), in_specs=[pl.BlockSpec((tr, W), lambda i: (i, col_block)), vec, row], out_specs=[row, vec],
        out_shape=[jax.ShapeDtypeStruct((R, W), F32), jax.ShapeDtypeStruct((1, W), F32)],
        compiler_params=pltpu.CompilerParams(dimension_semantics=("arbitrary",)),
    )(x, g.reshape(1, W), dy)


def _loss_call(h, tgt, n_tokens, name):
    B, Tp, D = h.shape
    tr = _pick(Tp, 544, 8)

    def body(y_ref, t_ref, dy_ref, acc_ref):
        @pl.when(jnp.logical_and(pl.program_id(0) == 0, pl.program_id(1) == 0))
        def _():
            acc_ref[...] = jnp.zeros_like(acc_ref)

        t = lax.broadcasted_iota(jnp.int32, (tr, 1), 0) + pl.program_id(1) * tr
        counts = jnp.logical_and(t >= N_META, t < N_META + n_tokens)
        e = jnp.where(counts, y_ref[...] - t_ref[...], 0.0)
        dy_ref[...] = e * (1.0 / D)
        acc_ref[...] += jnp.sum(jnp.sum(e * e, axis=-1, keepdims=True), axis=0, keepdims=True) * (0.5 / D)

    row = pl.BlockSpec((None, tr, D), lambda b, i: (b, i, 0))
    one = pl.BlockSpec((1, 1), lambda b, i: (0, 0))
    return pl.pallas_call(
        body, name=name, grid=(B, Tp // tr), in_specs=[row, row], out_specs=[row, one],
        out_shape=[jax.ShapeDtypeStruct((B, Tp, D), F32), jax.ShapeDtypeStruct((1, 1), F32)],
        compiler_params=pltpu.CompilerParams(dimension_semantics=("arbitrary", "arbitrary")),
    )(h, tgt)


@functools.partial(jax.custom_vjp, nondiff_argnums=(2,))
def loss_head(h, tgt, n_tokens):
    return _loss_call(h, tgt, n_tokens, "loss_head")[1][0, 0]


def _loss_head_fwd(h, tgt, n_tokens):
    dy, acc = _loss_call(h, tgt, n_tokens, "loss_head")
    return acc[0, 0], dy


def _loss_head_bwd(n_tokens, dy, ct):
    return ct * dy, None


loss_head.defvjp(_loss_head_fwd, _loss_head_bwd)


_GELU_C = math.sqrt(2.0 / math.pi)


def _gelu_parts(x):
    x2 = x * x
    t = jnp.tanh(_GELU_C * (x + 0.044715 * x * x2))
    gelu = 0.5 * x * (1.0 + t)
    dgelu = 0.5 * (1.0 + t) + 0.5 * x * (1.0 - t * t) * (_GELU_C * (1.0 + 3.0 * 0.044715 * x2))
    return gelu, dgelu


def _sigmoid(x):
    return 1.0 / (1.0 + jnp.exp(-x))


def _scan8(a, b, carry, reverse):
    row = lax.broadcasted_iota(jnp.int32, a.shape, 0)
    for s in (1, 2, 4):
        shift = 8 - s if reverse else s
        keep = (row < 8 - s) if reverse else (row >= s)
        b = jnp.where(keep, a * pltpu.roll(b, shift, 0) + b, b)
        a = jnp.where(keep, a * pltpu.roll(a, shift, 0), a)
    return a * carry + b


def _lru_pre(prec_ref, prev_ref, first, cw_ref, cb_ref, wa_ref, ba_ref, wx_ref, bx_ref, sp_ref):
    tc = prec_ref.shape[0]
    prev = jnp.where(first, 0.0, prev_ref[...])
    ext = jnp.concatenate([prev, prec_ref[...]], axis=0)
    cw = cw_ref[...]
    taps = [ext[8:] if k == CONV_WIDTH - 1 else pltpu.roll(ext, CONV_WIDTH - 1 - k, 0)[8:] for k in range(CONV_WIDTH)]
    xc = cb_ref[...] + sum(cw[k:k + 1, :] * taps[k] for k in range(CONV_WIDTH))
    ga, gx = [], []
    for h in range(LRU_HEADS):
        xh = xc[:, h * LRU_HEAD_DIM:(h + 1) * LRU_HEAD_DIM].astype(MXU_DTYPE)
        ga.append(jnp.dot(xh, wa_ref[h].astype(MXU_DTYPE), preferred_element_type=F32))
        gx.append(jnp.dot(xh, wx_ref[h].astype(MXU_DTYPE), preferred_element_type=F32))
    r = _sigmoid(jnp.concatenate(ga, axis=1) + ba_ref[...])
    i = _sigmoid(jnp.concatenate(gx, axis=1) + bx_ref[...])
    log_a = -LRU_C * r * sp_ref[...]
    a = jnp.exp(log_a)
    a2 = a * a
    mult = jnp.sqrt(-jnp.tanh(log_a) * (a2 + 1.0))
    return taps, xc, r, i, a, a2, mult


def _lru_fwd_call(p, cw, cb, wa, ba, wx, bx, sp):
    B, Tp, _ = p.shape
    W = LRU_WIDTH
    tc = SEQ_BLOCK
    nc = Tp // tc

    def body(pg_ref, prec_ref, prev_ref, cw_ref, cb_ref, wa_ref, ba_ref, wx_ref, bx_ref, sp_ref, y_ref, h_ref, carry_ref):
        first = pl.program_id(1) == 0

        @pl.when(first)
        def _():
            carry_ref[...] = jnp.zeros_like(carry_ref)

        _, xc, r, i, a, a2, mult = _lru_pre(prec_ref, prev_ref, first, cw_ref, cb_ref, wa_ref, ba_ref, wx_ref, bx_ref, sp_ref)
        b = mult * (i * xc)
        carry = carry_ref[0:1, :]
        for t in range(tc // 8):
            h = _scan8(a[8 * t:8 * t + 8], b[8 * t:8 * t + 8], carry, False)
            h_ref[8 * t:8 * t + 8, :] = h
            carry = h[7:8, :]
        carry_ref[...] = jnp.broadcast_to(carry, carry_ref.shape)
        y_ref[...] = h_ref[...] * _gelu_parts(pg_ref[...])[0]

    cur = pl.BlockSpec((None, tc, W), lambda b, j: (b, j, 0))
    rec = pl.BlockSpec((None, tc, W), lambda b, j: (b, j, 1))
    prev = pl.BlockSpec((None, 8, W), lambda b, j: (b, jnp.maximum(j * (tc // 8) - 1, 0), 1))
    vec = pl.BlockSpec((1, W), lambda b, j: (0, 0))
    cws = pl.BlockSpec((CONV_WIDTH, W), lambda b, j: (0, 0))
    wsp = pl.BlockSpec((LRU_HEADS, LRU_HEAD_DIM, LRU_HEAD_DIM), lambda b, j: (0, 0, 0))
    return pl.pallas_call(
        body, name="lru_fwd", grid=(B, nc),
        in_specs=[cur, rec, prev, cws, vec, wsp, vec, wsp, vec, vec],
        out_specs=[cur, cur],
        out_shape=[jax.ShapeDtypeStruct((B, Tp, W), F32), jax.ShapeDtypeStruct((B, Tp, W), F32)],
        scratch_shapes=[pltpu.VMEM((8, W), F32)],
        compiler_params=pltpu.CompilerParams(dimension_semantics=("arbitrary", "arbitrary")),
    )(p, p, p, cw, cb, wa, ba, wx, bx, sp)


def _lru_bwd_call(p, hseq, dy, cw, cb, wa, ba, wx, bx, sp, dpq, dpkv, dkpe):
    B, Tp, P = p.shape
    W = LRU_WIDTH
    tc = SEQ_BLOCK
    nc = Tp // tc
    HD = LRU_HEAD_DIM

    def body(pg_ref, prec_ref, prev_ref, h_ref, hprev_ref, dy_ref, cw_ref, cb_ref, wa_ref, ba_ref, wx_ref, bx_ref, sp_ref,
             dpq_ref, dpkv_ref, dkpe_ref, dp_ref, dcw_ref, dcb_ref, dwa_ref, dba_ref, dwx_ref, dbx_ref, dsp_ref,
             gcar_ref, anext_ref, halo_ref, g_ref):
        j = pl.program_id(1)
        first = j == nc - 1
        last = j == 0

        @pl.when(jnp.logical_and(pl.program_id(0) == 0, last))
        def _():
            for ref in (dcw_ref, dcb_ref, dwa_ref, dba_ref, dwx_ref, dbx_ref, dsp_ref):
                ref[...] = jnp.zeros_like(ref)

        @pl.when(last)
        def _():
            gcar_ref[...] = jnp.zeros_like(gcar_ref)
            anext_ref[...] = jnp.zeros_like(anext_ref)
            halo_ref[...] = jnp.zeros_like(halo_ref)

        taps, xc, r, i, a, a2, mult = _lru_pre(prec_ref, prev_ref, first, cw_ref, cb_ref, wa_ref, ba_ref, wx_ref, bx_ref, sp_ref)
        row = lax.broadcasted_iota(jnp.int32, (tc, W), 0)
        gelu, dgelu = _gelu_parts(pg_ref[...])
        dy = dy_ref[...]
        hcur = h_ref[...]
        dp_ref[:, 0:W] = dy * hcur * dgelu
        dp_ref[:, 2 * W:2 * W + MLA_Q_RANK] = dpq_ref[...]
        dp_ref[:, _KPE_START - MLA_KV_RANK:_KPE_START] = dpkv_ref[...]
        dp_ref[:, _KPE_START:P] = pltpu.roll(dkpe_ref[...], HEAD_LANES - MLA_NOPE, 1)[:, 0:P - _KPE_START]
        dh = dy * gelu
        a_next = jnp.where(row == tc - 1, anext_ref[0:1, :], pltpu.roll(a, tc - 1, 0))
        carry = gcar_ref[0:1, :]
        for t in reversed(range(tc // 8)):
            g = _scan8(a_next[8 * t:8 * t + 8], dh[8 * t:8 * t + 8], carry, True)
            g_ref[8 * t:8 * t + 8, :] = g
            carry = g[0:1, :]
        gcar_ref[...] = jnp.broadcast_to(carry, gcar_ref.shape)
        anext_ref[...] = jnp.broadcast_to(a[0:1, :], anext_ref.shape)
        G = g_ref[...]
        h_before = jnp.where(first, 0.0, hprev_ref[7:8, :])
        hprev = jnp.where(row == 0, h_before, pltpu.roll(hcur, 1, 0))
        d_a = G * hprev
        gx_ = G * xc
        d_mult = gx_ * i
        d_i = gx_ * mult
        dxc = G * (mult * i)
        d_la = d_a * a - d_mult * (a2 / mult)
        sp = sp_ref[...]
        d_r = d_la * (-LRU_C * sp)
        dsp_ref[...] += jnp.sum(d_la * (-LRU_C * r), axis=0, keepdims=True)
        dga = d_r * r * (1.0 - r)
        dgx = d_i * i * (1.0 - i)
        dba_ref[...] += jnp.sum(dga, axis=0, keepdims=True)
        dbx_ref[...] += jnp.sum(dgx, axis=0, keepdims=True)
        back = []
        for h in range(LRU_HEADS):
            sl = slice(h * HD, (h + 1) * HD)
            xh = xc[:, sl].astype(MXU_DTYPE)
            ah = dga[:, sl].astype(MXU_DTYPE)
            bh = dgx[:, sl].astype(MXU_DTYPE)
            tn = (((0,), (0,)), ((), ()))
            nt = (((1,), (1,)), ((), ()))
            dwa_ref[h] += lax.dot_general(xh, ah, tn, preferred_element_type=F32)
            dwx_ref[h] += lax.dot_general(xh, bh, tn, preferred_element_type=F32)
            back.append(lax.dot_general(ah, wa_ref[h].astype(MXU_DTYPE), nt, preferred_element_type=F32)
                        + lax.dot_general(bh, wx_ref[h].astype(MXU_DTYPE), nt, preferred_element_type=F32))
        dxc = dxc + jnp.concatenate(back, axis=1)
        dcb_ref[...] += jnp.sum(dxc, axis=0, keepdims=True)
        for k in range(CONV_WIDTH):
            dcw_ref[k:k + 1, :] += jnp.sum(dxc * taps[k], axis=0, keepdims=True)
        ext = jnp.concatenate([dxc, halo_ref[...]], axis=0)
        cw = cw_ref[...]
        acc = cw[CONV_WIDTH - 1:CONV_WIDTH, :] * dxc
        for k in range(CONV_WIDTH - 1):
            s = CONV_WIDTH - 1 - k
            acc = acc + cw[k:k + 1, :] * pltpu.roll(ext, tc + 8 - s, 0)[:tc]
        dp_ref[:, W:2 * W] = acc
        halo_ref[...] = dxc[0:8, :]

    rev = lambda j: nc - 1 - j
    cur = pl.BlockSpec((None, tc, W), lambda b, j: (b, rev(j), 0))
    rec = pl.BlockSpec((None, tc, W), lambda b, j: (b, rev(j), 1))
    prev = pl.BlockSpec((None, 8, W), lambda b, j: (b, jnp.maximum(rev(j) * (tc // 8) - 1, 0), 0))
    prev_rec = pl.BlockSpec((None, 8, W), lambda b, j: (b, jnp.maximum(rev(j) * (tc // 8) - 1, 0), 1))
    vec = pl.BlockSpec((1, W), lambda b, j: (0, 0))
    cws = pl.BlockSpec((CONV_WIDTH, W), lambda b, j: (0, 0))
    wsp = pl.BlockSpec((LRU_HEADS, HD, HD), lambda b, j: (0, 0, 0))
    vs = jax.ShapeDtypeStruct((1, W), F32)
    ws = jax.ShapeDtypeStruct((LRU_HEADS, HD, HD), F32)

    def rows(width):
        return pl.BlockSpec((None, tc, width), lambda b, j: (b, rev(j), 0))

    return pl.pallas_call(
        body, name="lru_bwd", grid=(B, nc),
        in_specs=[cur, rec, prev_rec, cur, prev, cur, cws, vec, wsp, vec, wsp, vec, vec, rows(MLA_Q_RANK), rows(MLA_KV_RANK), rows(HEAD_LANES)],
        out_specs=[rows(P), cws, vec, wsp, vec, wsp, vec, vec],
        out_shape=[jax.ShapeDtypeStruct((B, Tp, P), F32), jax.ShapeDtypeStruct((CONV_WIDTH, W), F32), vs, ws, vs, ws, vs, vs],
        scratch_shapes=[pltpu.VMEM((8, W), F32), pltpu.VMEM((8, W), F32), pltpu.VMEM((8, W), F32), pltpu.VMEM((tc, W), F32)],
        compiler_params=pltpu.CompilerParams(dimension_semantics=("arbitrary", "arbitrary")),
    )(p, p, p, hseq, hseq, dy, cw, cb, wa, ba, wx, bx, sp, dpq, dpkv, dkpe)


_Q_BLOCK = 2 * LRU_WIDTH // MLA_Q_RANK
_KV_BLOCK = (2 * LRU_WIDTH + MLA_Q_RANK) // MLA_KV_RANK
_KPE_START = 2 * LRU_WIDTH + MLA_Q_RANK + MLA_KV_RANK


@jax.custom_vjp
def even_front(p, cw, cb, wa, ba, wx, bx, sp, gq, gkv):
    return _even_front_fwd(p, cw, cb, wa, ba, wx, bx, sp, gq, gkv)[0]


def _even_front_fwd(p, cw, cb, wa, ba, wx, bx, sp, gq, gkv):
    B, Tp, W = p.shape
    p2d = p.reshape(B * Tp, W)
    y, hseq = _lru_fwd_call(p, cw, cb, wa, ba, wx, bx, sp)
    qn = _rms_fwd_call(p2d, gq, "q_norm_fwd", _Q_BLOCK)
    kvn = _rms_fwd_call(p2d, gkv, "kv_norm_fwd", _KV_BLOCK)
    kpe = jnp.pad(p[:, :, _KPE_START:], ((0, 0), (0, 0), (MLA_NOPE, HEAD_LANES - MLA_NOPE - MLA_ROPE)))
    return (y, qn, kvn, kpe), (p, hseq, cw, cb, wa, ba, wx, bx, sp, gq, gkv)


def _even_front_bwd(res, cts):
    p, hseq, cw, cb, wa, ba, wx, bx, sp, gq, gkv = res
    dy, dqn, dkvn, dkpe = cts
    B, Tp, W = p.shape
    p2d = p.reshape(B * Tp, W)
    dpq, dgq = _rms_bwd_call(p2d, gq, dqn, "q_norm_bwd", _Q_BLOCK)
    dpkv, dgkv = _rms_bwd_call(p2d, gkv, dkvn, "kv_norm_bwd", _KV_BLOCK)
    dp, dcw, dcb, dwa, dba, dwx, dbx, dsp = _lru_bwd_call(p, hseq, dy, cw, cb, wa, ba, wx, bx, sp, dpq.reshape(B, Tp, -1),
                                                          dpkv.reshape(B, Tp, -1), dkpe)
    return dp, dcw, dcb, dwa, dba, dwx, dbx, dsp, dgq.reshape(gq.shape), dgkv.reshape(gkv.shape)


even_front.defvjp(_even_front_fwd, _even_front_bwd)


def _rope_tables(pos, half):
    inv = ROPE_BASE ** (-jnp.arange(half, dtype=F32) / half)
    ang = pos.astype(F32)[:, None] * inv[None, :]
    return jnp.cos(ang), jnp.sin(ang)


_NT = (((1,), (1,)), ((), ()))
_TN = (((0,), (0,)), ((), ()))
HEAD_LANES = 128
_MLA_SCALE = (MLA_NOPE + MLA_ROPE) ** -0.5
_LOG2E = math.log2(math.e)


def _mask_diagonal(s, fill):
    L = s.shape[1]
    row = lax.broadcasted_iota(jnp.int32, (SEQ_BLOCK, SEQ_BLOCK), 0)
    col = lax.broadcasted_iota(jnp.int32, (SEQ_BLOCK, SEQ_BLOCK), 1)
    last = jnp.where(col <= row, s[:, L - SEQ_BLOCK:], fill)
    return last if L == SEQ_BLOCK else jnp.concatenate([s[:, :L - SEQ_BLOCK], last], axis=1)


def _mla_rope_tables(pos):
    half = MLA_ROPE // 2
    cos, sin = _rope_tables(pos, half)
    T = pos.shape[0]
    ones, zeros = jnp.ones((T, MLA_NOPE), F32), jnp.zeros((T, MLA_NOPE), F32)
    tail1, tail0 = jnp.ones((T, HEAD_LANES - MLA_NOPE - MLA_ROPE), F32), jnp.zeros((T, HEAD_LANES - MLA_NOPE - MLA_ROPE), F32)
    zh = jnp.zeros((T, half), F32)
    c = jnp.concatenate([ones, cos, cos, tail1], axis=1)
    s_up = jnp.concatenate([zeros, -sin, zh, tail0], axis=1)
    s_down = jnp.concatenate([zeros, zh, sin, tail0], axis=1)
    return c, s_up, s_down


def _rope_lanes(x, c, s_up, s_down):
    half = MLA_ROPE // 2
    return x * c + pltpu.roll(x, HEAD_LANES - half, 1) * s_up + pltpu.roll(x, half, 1) * s_down


def _unrope_lanes(d, c, s_up, s_down):
    half = MLA_ROPE // 2
    return d * c + pltpu.roll(d * s_up, half, 1) + pltpu.roll(d * s_down, HEAD_LANES - half, 1)


def _mla_operands(q_ref, kv_ref, kpe_ref, c, s_up, s_down):
    lane = lax.broadcasted_iota(jnp.int32, kv_ref.shape, 1)
    qr = (_rope_lanes(q_ref[...].astype(F32), c, s_up, s_down) * (_MLA_SCALE * _LOG2E)).astype(MXU_DTYPE)
    kr = jnp.where(lane < MLA_NOPE, kv_ref[...].astype(F32), _rope_lanes(kpe_ref[...], c, s_up, s_down)).astype(MXU_DTYPE)
    return qr, kr, lane


def _mla_specs(Tp):
    head = pl.BlockSpec((None, Tp, HEAD_LANES), lambda b, h: (b, 0, h))
    shared = pl.BlockSpec((None, Tp, HEAD_LANES), lambda b, h: (b, 0, 0))
    tab = pl.BlockSpec((Tp, HEAD_LANES), lambda b, h: (0, 0))
    lse = pl.BlockSpec((None, None, Tp, 1), lambda b, h: (b, h, 0, 0))
    return head, shared, tab, lse


def _attn_fwd_call(q, kv, kpe, tabs):
    B, Tp, _ = q.shape
    nq = Tp // SEQ_BLOCK

    def body(q_ref, kv_ref, kpe_ref, c_ref, su_ref, sd_ref, o_ref, lse_ref, qr_ref, kr_ref):
        qr, kr, lane = _mla_operands(q_ref, kv_ref, kpe_ref, c_ref[...], su_ref[...], sd_ref[...])
        qr_ref[...] = qr
        kr_ref[...] = kr
        for qi in range(nq):
            L = (qi + 1) * SEQ_BLOCK
            blk = slice(qi * SEQ_BLOCK, L)
            s = _mask_diagonal(lax.dot_general(qr_ref[blk, :], kr_ref[0:L, :], _NT, preferred_element_type=F32), NEG_INF)
            m = jnp.max(s, axis=-1, keepdims=True)
            p = jnp.exp2(s - m)
            l = jnp.sum(p, axis=-1, keepdims=True)
            o = jnp.dot(p.astype(MXU_DTYPE), kv_ref[0:L, :].astype(MXU_DTYPE), preferred_element_type=F32)
            o_ref[blk, :] = jnp.where(lane[blk, :] >= MLA_NOPE, o / l, 0.0)
            lse_ref[blk, :] = m + jnp.log2(l)

    head, shared, tab, lse = _mla_specs(Tp)
    return pl.pallas_call(
        body, name="mla_attn_fwd", grid=(B, MLA_HEADS), in_specs=[head, head, shared, tab, tab, tab], out_specs=[head, lse],
        out_shape=[jax.ShapeDtypeStruct((B, Tp, MLA_HEADS * HEAD_LANES), F32), jax.ShapeDtypeStruct((B, MLA_HEADS, Tp, 1), F32)],
        scratch_shapes=[pltpu.VMEM((Tp, HEAD_LANES), MXU_DTYPE), pltpu.VMEM((Tp, HEAD_LANES), MXU_DTYPE)],
        compiler_params=pltpu.CompilerParams(dimension_semantics=("parallel", "parallel")),
    )(q, kv, kpe, *tabs)


def _attn_bwd_call(q, kv, kpe, tabs, o, lse, do):
    B, Tp, _ = q.shape
    nq = Tp // SEQ_BLOCK

    def body(q_ref, kv_ref, kpe_ref, c_ref, su_ref, sd_ref, o_ref, lse_ref, do_ref, dq_ref, dkv_ref, dkpe_ref,
             qr_ref, kr_ref, dqa_ref, dka_ref, dva_ref):
        c, s_up, s_down = c_ref[...], su_ref[...], sd_ref[...]
        qr, kr, lane = _mla_operands(q_ref, kv_ref, kpe_ref, c, s_up, s_down)
        qr_ref[...] = qr
        kr_ref[...] = kr
        dka_ref[...] = jnp.zeros_like(dka_ref)
        dva_ref[...] = jnp.zeros_like(dva_ref)
        for qi in range(nq):
            L = (qi + 1) * SEQ_BLOCK
            blk = slice(qi * SEQ_BLOCK, L)
            qb = qr_ref[blk, :]
            do = jnp.where(lane[blk, :] >= MLA_NOPE, do_ref[blk, :], 0.0)
            delta = jnp.sum(do * o_ref[blk, :], axis=-1, keepdims=True)
            s = _mask_diagonal(lax.dot_general(qb, kr_ref[0:L, :], _NT, preferred_element_type=F32), NEG_INF)
            p = jnp.exp2(s - lse_ref[blk, :])
            dob = do.astype(MXU_DTYPE)
            dva_ref[0:L, :] += lax.dot_general(p.astype(MXU_DTYPE), dob, _TN, preferred_element_type=F32)
            dp = lax.dot_general(dob, kv_ref[0:L, :].astype(MXU_DTYPE), _NT, preferred_element_type=F32)
            ds = (p * (dp - delta)).astype(MXU_DTYPE)
            dqa_ref[blk, :] = jnp.dot(ds, kr_ref[0:L, :], preferred_element_type=F32)
            dka_ref[0:L, :] += lax.dot_general(ds, qb, _TN, preferred_element_type=F32)
        dq_ref[...] = _unrope_lanes(dqa_ref[...] * _MLA_SCALE, c, s_up, s_down).astype(dq_ref.dtype)
        dk = dka_ref[...] * (1.0 / _LOG2E)
        dkv_ref[...] = jnp.where(lane < MLA_NOPE, dk, dva_ref[...]).astype(dkv_ref.dtype)
        dkpe = jnp.where(lane >= MLA_NOPE, _unrope_lanes(dk, c, s_up, s_down), 0.0)

        @pl.when(pl.program_id(1) == 0)
        def _():
            dkpe_ref[...] = dkpe

        @pl.when(pl.program_id(1) > 0)
        def _():
            dkpe_ref[...] += dkpe

    head, shared, tab, lse_spec = _mla_specs(Tp)
    wide = jax.ShapeDtypeStruct((B, Tp, MLA_HEADS * HEAD_LANES), q.dtype)
    acc = pltpu.VMEM((Tp, HEAD_LANES), F32)
    return pl.pallas_call(
        body, name="mla_attn_bwd", grid=(B, MLA_HEADS),
        in_specs=[head, head, shared, tab, tab, tab, head, lse_spec, head], out_specs=[head, head, shared],
        out_shape=[wide, wide, jax.ShapeDtypeStruct((B, Tp, HEAD_LANES), F32)],
        scratch_shapes=[pltpu.VMEM((Tp, HEAD_LANES), MXU_DTYPE), pltpu.VMEM((Tp, HEAD_LANES), MXU_DTYPE), acc, acc, acc],
        compiler_params=pltpu.CompilerParams(dimension_semantics=("parallel", "arbitrary")),
    )(q, kv, kpe, *tabs, o, lse, do)


@jax.custom_vjp
def mla_attention(q, kv, kpe, tabs):
    return _attn_fwd_call(q, kv, kpe, tabs)[0]


def _mla_attention_fwd(q, kv, kpe, tabs):
    o, lse = _attn_fwd_call(q, kv, kpe, tabs)
    return o, (q, kv, kpe, tabs, o, lse)


def _mla_attention_bwd(res, do):
    q, kv, kpe, tabs, o, lse = res
    dq, dkv, dkpe = _attn_bwd_call(q, kv, kpe, tabs, o, lse, do)
    return dq, dkv, dkpe, None


mla_attention.defvjp(_mla_attention_fwd, _mla_attention_bwd)


def _rope_halves(x, cos, sin):
    half = x.shape[1] // 2
    x1, x2 = x[:, :half], x[:, half:]
    return jnp.concatenate([x1 * cos - x2 * sin, x1 * sin + x2 * cos], axis=1)


def _unrope_halves(d, cos, sin):
    half = d.shape[1] // 2
    d1, d2 = d[:, :half], d[:, half:]
    return jnp.concatenate([d1 * cos + d2 * sin, d2 * cos - d1 * sin], axis=1)


_RET_K_SCALE = RET_QK_DIM ** -0.5
_RET_Q_BLOCKS = RET_HEADS
_RET_V_BLOCK0 = 2 * RET_HEADS * RET_QK_DIM // RET_V_DIM
_RET_G_BLOCK0 = _RET_V_BLOCK0 + RET_HEADS


def _ret_specs(Tp):
    q = pl.BlockSpec((None, Tp, RET_QK_DIM), lambda b, h: (b, 0, h))
    k = pl.BlockSpec((None, Tp, RET_QK_DIM), lambda b, h: (b, 0, _RET_Q_BLOCKS + h))
    v = pl.BlockSpec((None, Tp, RET_V_DIM), lambda b, h: (b, 0, _RET_V_BLOCK0 + h))
    tab = pl.BlockSpec((Tp, RET_QK_DIM // 2), lambda b, h: (0, 0))
    lg = pl.BlockSpec((None, 1, 1), lambda b, h: (h, 0, 0))
    return q, k, v, tab, lg


def _ret_operands(q_ref, k_ref, cos, sin, lg):
    t = lax.broadcasted_iota(jnp.int32, (q_ref.shape[0], 1), 0).astype(F32)
    grow, shrink = jnp.exp(-lg * t), jnp.exp(lg * t)
    qs = (_rope_halves(q_ref[...].astype(F32), cos, sin) * shrink).astype(MXU_DTYPE)
    ks = (_rope_halves(k_ref[...].astype(F32), cos, sin) * (grow * _RET_K_SCALE)).astype(MXU_DTYPE)
    return qs, ks, shrink, grow * _RET_K_SCALE


def _ret_core_fwd_call(p, cos, sin, lg):
    B, Tp, _ = p.shape
    nq = Tp // SEQ_BLOCK

    def body(q_ref, k_ref, v_ref, cos_ref, sin_ref, lg_ref, o_ref, qs_ref, ks_ref):
        qs_ref[...], ks_ref[...], _, _ = _ret_operands(q_ref, k_ref, cos_ref[...], sin_ref[...], lg_ref[...])
        for qi in range(nq):
            L = (qi + 1) * SEQ_BLOCK
            blk = slice(qi * SEQ_BLOCK, L)
            s = _mask_diagonal(lax.dot_general(qs_ref[blk, :], ks_ref[0:L, :], _NT, preferred_element_type=F32), 0.0)
            o_ref[blk, :] = jnp.dot(s.astype(MXU_DTYPE), v_ref[0:L, :].astype(MXU_DTYPE), preferred_element_type=F32)

    q, k, v, tab, lgs = _ret_specs(Tp)
    return pl.pallas_call(
        body, name="retention_fwd", grid=(B, RET_HEADS), in_specs=[q, k, v, tab, tab, lgs],
        out_specs=pl.BlockSpec((None, Tp, RET_V_DIM), lambda b, h: (b, 0, h)),
        out_shape=jax.ShapeDtypeStruct((B, Tp, RET_HEADS * RET_V_DIM), F32),
        scratch_shapes=[pltpu.VMEM((Tp, RET_QK_DIM), MXU_DTYPE), pltpu.VMEM((Tp, RET_QK_DIM), MXU_DTYPE)],
        compiler_params=pltpu.CompilerParams(dimension_semantics=("parallel", "parallel")),
    )(p, p, p, cos, sin, lg)


def _ret_core_bwd_call(p, do, cos, sin, lg):
    B, Tp, _ = p.shape
    nq = Tp // SEQ_BLOCK

    def body(q_ref, k_ref, v_ref, do_ref, cos_ref, sin_ref, lg_ref, dq_ref, dk_ref, dv_ref, qs_ref, ks_ref, dqa_ref, dka_ref, dva_ref):
        cos_, sin_ = cos_ref[...], sin_ref[...]
        qs_ref[...], ks_ref[...], q_scale, k_scale = _ret_operands(q_ref, k_ref, cos_, sin_, lg_ref[...])
        dka_ref[...] = jnp.zeros_like(dka_ref)
        dva_ref[...] = jnp.zeros_like(dva_ref)
        for qi in range(nq):
            L = (qi + 1) * SEQ_BLOCK
            blk = slice(qi * SEQ_BLOCK, L)
            qb = qs_ref[blk, :]
            dob = do_ref[blk, :].astype(MXU_DTYPE)
            s = _mask_diagonal(lax.dot_general(qb, ks_ref[0:L, :], _NT, preferred_element_type=F32), 0.0).astype(MXU_DTYPE)
            dva_ref[0:L, :] += lax.dot_general(s, dob, _TN, preferred_element_type=F32)
            ds = _mask_diagonal(lax.dot_general(dob, v_ref[0:L, :].astype(MXU_DTYPE), _NT, preferred_element_type=F32), 0.0).astype(MXU_DTYPE)
            dqa_ref[blk, :] = jnp.dot(ds, ks_ref[0:L, :], preferred_element_type=F32)
            dka_ref[0:L, :] += lax.dot_general(ds, qb, _TN, preferred_element_type=F32)
        dq_ref[...] = _unrope_halves(dqa_ref[...] * q_scale, cos_, sin_).astype(dq_ref.dtype)
        dk_ref[...] = _unrope_halves(dka_ref[...] * k_scale, cos_, sin_).astype(dk_ref.dtype)
        dv_ref[...] = dva_ref[...].astype(dv_ref.dtype)

    q, k, v, tab, lgs = _ret_specs(Tp)
    qk_out = pl.BlockSpec((None, Tp, RET_QK_DIM), lambda b, h: (b, 0, h))
    v_out = pl.BlockSpec((None, Tp, RET_V_DIM), lambda b, h: (b, 0, h))
    return pl.pallas_call(
        body, name="retention_bwd", grid=(B, RET_HEADS), in_specs=[q, k, v, v_out, tab, tab, lgs],
        out_specs=[qk_out, qk_out, v_out],
        out_shape=[jax.ShapeDtypeStruct((B, Tp, RET_HEADS * RET_QK_DIM), p.dtype), jax.ShapeDtypeStruct((B, Tp, RET_HEADS * RET_QK_DIM), p.dtype),
                   jax.ShapeDtypeStruct((B, Tp, RET_HEADS * RET_V_DIM), p.dtype)],
        scratch_shapes=[pltpu.VMEM((Tp, RET_QK_DIM), MXU_DTYPE), pltpu.VMEM((Tp, RET_QK_DIM), MXU_DTYPE),
                        pltpu.VMEM((Tp, RET_QK_DIM), F32), pltpu.VMEM((Tp, RET_QK_DIM), F32), pltpu.VMEM((Tp, RET_V_DIM), F32)],
        compiler_params=pltpu.CompilerParams(dimension_semantics=("parallel", "parallel")),
    )(p, p, p, do, cos, sin, lg)


def _ret_gate_specs(M):
    tm = _pick(M, 1088, 8)
    head = pl.BlockSpec((tm, RET_V_DIM), lambda i, h: (i, h))
    gate = pl.BlockSpec((tm, RET_V_DIM), lambda i, h: (i, _RET_G_BLOCK0 + h))
    return tm, head, gate


def _ret_gate_fwd_call(o, p2d):
    M = o.shape[0]
    tm, head, gate = _ret_gate_specs(M)

    def body(o_ref, g_ref, y_ref):
        ov = o_ref[...]
        gv = g_ref[...].astype(F32)
        rstd = lax.rsqrt(jnp.mean(ov * ov, axis=-1, keepdims=True) + EPS)
        y_ref[...] = (gv * _sigmoid(gv)) * (ov * rstd)

    return pl.pallas_call(
        body, name="retention_gate_fwd", grid=(M // tm, RET_HEADS), in_specs=[head, gate], out_specs=head,
        out_shape=jax.ShapeDtypeStruct(o.shape, F32),
        compiler_params=pltpu.CompilerParams(dimension_semantics=("parallel", "parallel")),
    )(o, p2d)


def _ret_gate_bwd_call(o, p2d, dy):
    M = o.shape[0]
    tm, head, gate = _ret_gate_specs(M)

    def body(o_ref, g_ref, dy_ref, do_ref, dg_ref):
        ov = o_ref[...]
        gv = g_ref[...].astype(F32)
        dy = dy_ref[...]
        rstd = lax.rsqrt(jnp.mean(ov * ov, axis=-1, keepdims=True) + EPS)
        on = ov * rstd
        sg = _sigmoid(gv)
        dg_ref[...] = (dy * on * (sg * (1.0 + gv * (1.0 - sg)))).astype(dg_ref.dtype)
        don = dy * (gv * sg)
        do_ref[...] = (rstd * (don - on * jnp.mean(don * on, axis=-1, keepdims=True))).astype(do_ref.dtype)

    shp = jax.ShapeDtypeStruct(o.shape, p2d.dtype)
    return pl.pallas_call(
        body, name="retention_gate_bwd", grid=(M // tm, RET_HEADS), in_specs=[head, gate, head], out_specs=[head, head],
        out_shape=[shp, shp],
        compiler_params=pltpu.CompilerParams(dimension_semantics=("parallel", "parallel")),
    )(o, p2d, dy)


def _log_gamma():
    return jnp.log(1.0 - 2.0 ** (-5.0 - jnp.arange(RET_HEADS, dtype=F32))).reshape(RET_HEADS, 1, 1)


@jax.custom_vjp
def retention_mixer(p, cos, sin):
    B, Tp, W = p.shape
    o = _ret_core_fwd_call(p, cos, sin, _log_gamma())
    return _ret_gate_fwd_call(o.reshape(B * Tp, -1), p.reshape(B * Tp, W))


def _retention_mixer_fwd(p, cos, sin):
    B, Tp, W = p.shape
    o = _ret_core_fwd_call(p, cos, sin, _log_gamma())
    return _ret_gate_fwd_call(o.reshape(B * Tp, -1), p.reshape(B * Tp, W)), (p, o, cos, sin)


def _retention_mixer_bwd(res, dy):
    p, o, cos, sin = res
    B, Tp, W = p.shape
    do, dg = _ret_gate_bwd_call(o.reshape(B * Tp, -1), p.reshape(B * Tp, W), dy)
    dq, dk, dv = _ret_core_bwd_call(p, do.reshape(B, Tp, -1), cos, sin, _log_gamma())
    return jnp.concatenate([dq, dk, dv, dg.reshape(B, Tp, -1)], axis=-1), None, None


retention_mixer.defvjp(_retention_mixer_fwd, _retention_mixer_bwd)


def _heads_to_lanes(w):
    K = w.shape[0]
    w = w.reshape(K, MLA_HEADS, MLA_NOPE + MLA_ROPE)
    return jnp.pad(w, ((0, 0), (0, 0), (0, HEAD_LANES - MLA_NOPE - MLA_ROPE))).reshape(K, MLA_HEADS * HEAD_LANES)


def _out_rows_to_lanes(w):
    N = w.shape[1]
    att = w[LRU_WIDTH:].reshape(MLA_HEADS, MLA_V, N)
    att = jnp.pad(att, ((0, 0), (HEAD_LANES - MLA_V, 0), (0, 0))).reshape(MLA_HEADS * HEAD_LANES, N)
    return jnp.concatenate([w[:LRU_WIDTH], att], axis=0)


def _seq_dims(x):
    B, S, D = x.shape
    T = S + N_META
    Tp = _round_up(T, SEQ_BLOCK)
    return B, S, T, Tp


def _mixer0(diff, w, token):
    x = diff["x"]
    B, S, T, Tp = _seq_dims(x)
    D = x.shape[-1]
    M = B * Tp
    pos = jnp.arange(Tp, dtype=jnp.int32)

    def mm(a, name, act=False, out_dtype=F32, layout=lambda m: m, col_shards=1):
        return matmul(a, layout(w[name]), layout(diff[name]), act, name, out_dtype, col_shards)

    meta = jnp.broadcast_to(diff["meta_tokens"][None], (B, N_META, D))
    h = jnp.concatenate([meta, x + token, jnp.zeros((B, Tp - T, D), F32)], axis=1).reshape(M, D)
    p = mm(h, "ev_w_in")
    sp = jax.nn.softplus(-diff["ev_lru_lambda"]).reshape(1, LRU_WIDTH)
    y_rec, qn, kvn, kpe = even_front(
        p.reshape(B, Tp, -1), diff["ev_conv_w"].reshape(CONV_WIDTH, LRU_WIDTH), diff["ev_conv_b"].reshape(1, LRU_WIDTH),
        diff["ev_w_rg_a"].reshape(LRU_HEADS, LRU_HEAD_DIM, LRU_HEAD_DIM), diff["ev_b_rg_a"].reshape(1, LRU_WIDTH),
        diff["ev_w_rg_x"].reshape(LRU_HEADS, LRU_HEAD_DIM, LRU_HEAD_DIM), diff["ev_b_rg_x"].reshape(1, LRU_WIDTH),
        sp, diff["ev_q_norm_g"].reshape(-1), diff["ev_kv_norm_g"].reshape(-1))
    y_rec = y_rec.reshape(M, LRU_WIDTH)
    q = mm(qn, "ev_w_uq", out_dtype=MXU_DTYPE, layout=_heads_to_lanes).reshape(B, Tp, -1)
    kv = mm(kvn, "ev_w_ukv", out_dtype=MXU_DTYPE).reshape(B, Tp, -1)
    y_att = mla_attention(q, kv, kpe, _mla_rope_tables(pos)).reshape(M, -1)
    mix = mm(jnp.concatenate([y_rec, y_att], axis=-1), "ev_w_out", layout=_out_rows_to_lanes)
    return deepnorm(h, mix, diff["ln_mix_g"], diff["ln_mix_b"], "ln_mix0")


def _mlp0(diff, h, w):
    f = mlp(h, w["mlp_w1_0"], w["mlp_w2_0"], diff["mlp_w1_0"], diff["mlp_w2_0"], "mlp0")
    return deepnorm(h, f, diff["ln_mlp_g"], diff["ln_mlp_b"], "ln_mlp0")


def _layer1_loss(diff, h, w, tgt):
    B, S, T, Tp = _seq_dims(tgt)
    D = tgt.shape[-1]
    pos = jnp.arange(Tp, dtype=jnp.int32)

    def mm(a, name, out_dtype=F32, col_shards=1):
        return matmul(a, w[name], diff[name], False, name, out_dtype, col_shards)

    p = mm(h, "od_w_in", out_dtype=MXU_DTYPE, col_shards=N_CHIPS)
    cos, sin = _rope_tables(pos, RET_QK_DIM // 2)
    mix = mm(retention_mixer(p.reshape(B, Tp, -1), cos, sin), "od_w_out")
    h = deepnorm(h, mix, diff["ln_mix_g"], diff["ln_mix_b"], "ln_mix1")
    f = mlp(h, w["mlp_w1_1"], w["mlp_w2_1"], diff["mlp_w1_1"], diff["mlp_w2_1"], "mlp1")
    h = deepnorm(h, f, diff["ln_mlp_g"], diff["ln_mlp_b"], "ln_mlp1")
    return loss_head(h.reshape(B, Tp, D), jnp.pad(tgt, ((0, 0), (N_META, Tp - T), (0, 0))), S)


_HBM = pl.BlockSpec(memory_space=pltpu.HBM)


def _place():
    return lax.axis_index("x"), lax.axis_index("y"), lax.axis_index("c")


def _other_chips(x, y):
    return [(1 - x, y), (x, 1 - y), (1 - x, 1 - y)]


def _chunks(rows, sublanes, most):
    for q in range(most, 0, -1):
        if rows % (q * sublanes) == 0:
            return q
    return 1


def _sublanes(dtype):
    return 8 * 4 // jnp.dtype(dtype).itemsize


def _gather_pieces(bufs):
    plan, first = [], []
    for b in bufs:
        Rh = b.shape[0] // 2
        Q = _chunks(Rh, _sublanes(b.dtype), 4) if Rh * b.shape[1] * b.dtype.itemsize > (1 << 20) else 1
        first.append(3 * sum(q for _, q, _ in plan))
        plan.append((Rh, Q, Rh // Q))
    return plan, first, 3 * sum(q for _, q, _ in plan)


def _allgather_chips(bufs, name):
    n = len(bufs)
    plan, first, n_sems = _gather_pieces(bufs)

    def body(*refs):
        x_refs, out_refs, (send_sems, recv_sems) = refs[:n], refs[n:2 * n], refs[2 * n:]
        x, y, c = _place()
        sibling = (x, y, 1 - c)
        chips = _other_chips(x, y)

        def copy(k, src, dst, to):
            return pltpu.make_async_remote_copy(src_ref=src, dst_ref=dst, send_sem=send_sems.at[k], recv_sem=recv_sems.at[k],
                                                device_id=to, device_id_type=MESH)

        def piece(i, cx, cy, hc, q):
            Rh, _, ch = plan[i]
            return out_refs[i].at[2 * cx + cy, pl.ds(hc * Rh + q * ch, ch), :]

        slots = [(i, q, j) for i in range(n) for q in range(plan[i][1]) for j in range(3)]
        sem = {(i, q, j): first[i] + 3 * q + j for i, q, j in slots}
        sent = [copy(sem[i, q, j], x_refs[i].at[pl.ds(c * plan[i][0] + q * plan[i][2], plan[i][2]), :], piece(i, x, y, c, q), (*chips[j], c))
                for i, q, j in slots]
        for cp in sent:
            cp.start()
        passed = []
        for i, q, j in slots:
            landed = piece(i, *chips[j], c, q)
            copy(sem[i, q, j], landed, landed, sibling).wait_recv()
            fwd = copy(n_sems + sem[i, q, j], landed, landed, sibling)
            fwd.start()
            passed.append(fwd)
        for i, q, j in slots:
            theirs = piece(i, *chips[j], 1 - c, q)
            copy(n_sems + sem[i, q, j], theirs, theirs, sibling).wait_recv()
        for cp in sent + passed:
            cp.wait_send()

    return pl.pallas_call(
        body, name=name, in_specs=[_HBM] * n, out_specs=[_HBM] * n,
        out_shape=[jax.ShapeDtypeStruct((N_CHIPS,) + b.shape, b.dtype) for b in bufs],
        scratch_shapes=[pltpu.SemaphoreType.DMA((2 * n_sems,)), pltpu.SemaphoreType.DMA((2 * n_sems,))],
    )(*bufs)


def _with_own(gathered, own):
    my = 2 * lax.axis_index("x") + lax.axis_index("y")
    return lax.dynamic_update_slice(gathered, own[None], (my, 0, 0))


def _sibling_exchange(ps, name):
    n = len(ps)

    def body(*refs):
        p_refs, out_refs, (send_sems, recv_sems) = refs[:n], refs[n:2 * n], refs[2 * n:]
        x, y, c = _place()
        copies = [pltpu.make_async_remote_copy(src_ref=p_ref.at[j, 1 - c], dst_ref=out_ref.at[j], send_sem=send_sems.at[N_CHIPS * i + j],
                                               recv_sem=recv_sems.at[N_CHIPS * i + j], device_id=(x, y, 1 - c), device_id_type=MESH)
                  for i, (p_ref, out_ref) in enumerate(zip(p_refs, out_refs)) for j in range(N_CHIPS)]
        for cp in copies:
            cp.start()
        for cp in copies:
            cp.wait()

    return pl.pallas_call(
        body, name=name, in_specs=[_HBM] * n, out_specs=[_HBM] * n,
        out_shape=[jax.ShapeDtypeStruct((N_CHIPS,) + p.shape[2:], p.dtype) for p in ps],
        scratch_shapes=[pltpu.SemaphoreType.DMA((N_CHIPS * n,)), pltpu.SemaphoreType.DMA((N_CHIPS * n,))],
    )(*ps)


def _chip_scatter(ss, name):
    n = len(ss)

    def body(*refs):
        s_refs, t_refs, (send_sems, recv_sems) = refs[:n], refs[n:2 * n], refs[2 * n:]
        x, y, c = _place()
        copies = [pltpu.make_async_remote_copy(src_ref=s_ref.at[j + 1], dst_ref=t_ref.at[j], send_sem=send_sems.at[3 * i + j],
                                               recv_sem=recv_sems.at[3 * i + j], device_id=(cx, cy, c), device_id_type=MESH)
                  for i, (s_ref, t_ref) in enumerate(zip(s_refs, t_refs)) for j, (cx, cy) in enumerate(_other_chips(x, y))]
        for cp in copies:
            cp.start()
        for cp in copies:
            cp.wait()

    return pl.pallas_call(
        body, name=name, in_specs=[_HBM] * n, out_specs=[_HBM] * n,
        out_shape=[jax.ShapeDtypeStruct((3,) + s.shape[1:], s.dtype) for s in ss],
        scratch_shapes=[pltpu.SemaphoreType.DMA((3 * n,)), pltpu.SemaphoreType.DMA((3 * n,))],
    )(*ss)


def _sibling_gather(fs, name):
    n = len(fs)

    def body(*refs):
        out_refs, (send_sems, recv_sems) = refs[n:2 * n], refs[2 * n:]
        x, y, c = _place()
        copies = [pltpu.make_async_remote_copy(src_ref=out_ref.at[c], dst_ref=out_ref.at[c], send_sem=send_sems.at[i], recv_sem=recv_sems.at[i],
                                               device_id=(x, y, 1 - c), device_id_type=MESH) for i, out_ref in enumerate(out_refs)]
        for cp in copies:
            cp.start()
        for cp in copies:
            cp.wait()

    return pl.pallas_call(
        body, name=name, in_specs=[_HBM] * n, out_specs=[_HBM] * n,
        out_shape=[jax.ShapeDtypeStruct(f.shape, f.dtype) for f in fs], input_output_aliases={i: i for i in range(n)},
        scratch_shapes=[pltpu.SemaphoreType.DMA((n,)), pltpu.SemaphoreType.DMA((n,))],
    )(*fs)


def _axis_scalar(name):
    return lax.axis_index(name).astype(jnp.int32).reshape(1)


def _add_own_half(p, got, out_dtype, name):
    n, _, R, C = p.shape
    tr = _pick(R, 512, 16)

    def body(x_ref, y_ref, c_ref, p_ref, g_ref, o_ref):
        o_ref[...] = (p_ref[...] + g_ref[...]).astype(out_dtype)

    def chip(r, x_ref, y_ref):
        return 2 * (x_ref[0] ^ (r & 1)) + (y_ref[0] ^ (r >> 1))

    grid_spec = pltpu.PrefetchScalarGridSpec(
        num_scalar_prefetch=3, grid=(n, R // tr),
        in_specs=[pl.BlockSpec((None, None, tr, C), lambda r, i, x_ref, y_ref, c_ref: (chip(r, x_ref, y_ref), c_ref[0], i, 0)),
                  pl.BlockSpec((None, tr, C), lambda r, i, x_ref, y_ref, c_ref: (chip(r, x_ref, y_ref), i, 0))],
        out_specs=pl.BlockSpec((None, tr, C), lambda r, i, x_ref, y_ref, c_ref: (r, i, 0)))
    return pl.pallas_call(body, name=name, grid_spec=grid_spec, out_shape=jax.ShapeDtypeStruct((n, R, C), out_dtype),
                          compiler_params=pltpu.CompilerParams(dimension_semantics=("parallel", "parallel")))(
        _axis_scalar("x"), _axis_scalar("y"), _axis_scalar("c"), p, got)


def _sum_partials(s, t, name):
    _, R, C = s.shape
    tr = _pick(R, 512, 16)

    def body(c_ref, s_ref, t_ref, o_ref):
        acc = s_ref[...].astype(F32)
        for j in range(3):
            acc = acc + t_ref[j].astype(F32)
        o_ref[...] = acc

    grid_spec = pltpu.PrefetchScalarGridSpec(
        num_scalar_prefetch=1, grid=(R // tr,),
        in_specs=[pl.BlockSpec((None, tr, C), lambda i, c_ref: (0, i, 0)), pl.BlockSpec((3, tr, C), lambda i, c_ref: (0, i, 0))],
        out_specs=pl.BlockSpec((None, tr, C), lambda i, c_ref: (c_ref[0], i, 0)))
    return pl.pallas_call(body, name=name, grid_spec=grid_spec, out_shape=jax.ShapeDtypeStruct((2, R, C), F32),
                          compiler_params=pltpu.CompilerParams(dimension_semantics=("parallel",)))(_axis_scalar("c"), s, t)


def _sibling_reduce(ps, wire_dtypes, tag):
    got = _sibling_exchange(ps, "grad_sibling_exchange_" + tag)
    return [_add_own_half(p, g, dt, "grad_sibling_add_%s%d" % (tag, i)) for i, (p, g, dt) in enumerate(zip(ps, got, wire_dtypes))]


_SEM = pl.BlockSpec(memory_space=pltpu.SEMAPHORE)
_ANY = pl.BlockSpec(memory_space=pl.ANY)
_EFFECT = pltpu.SideEffectType.DATAFLOW_SIDE_EFFECTING


def _in_hbm(a):
    return pltpu.with_memory_space_constraint(a, pltpu.HBM)


def _half_copies(x_refs, land_refs, send_sems, recv_sems, arriving):
    x, y, c = _place()
    copies = []
    for i, (x_ref, land_ref) in enumerate(zip(x_refs, land_refs)):
        Rh = x_ref.shape[0] // 2
        rows = pl.ds(c * Rh, Rh)
        for j, (cx, cy) in enumerate(_other_chips(x, y)):
            copies.append(pltpu.make_async_remote_copy(
                src_ref=x_ref.at[rows, :], dst_ref=land_ref.at[2 * cx + cy if arriving else 2 * x + y, rows, :],
                send_sem=send_sems.at[3 * i + j], recv_sem=recv_sems.at[3 * i + j], device_id=(cx, cy, c), device_id_type=MESH))
    return copies


def _allgather_start(bufs, name):
    n = len(bufs)

    def body(*refs):
        x_refs, land_refs, (send_sems, recv_sems), token = refs[:n], refs[n:2 * n], refs[2 * n:2 * n + 2], refs[-1]
        for cp in _half_copies(x_refs, land_refs, send_sems, recv_sems, False):
            cp.start()
        token[...] = jnp.zeros_like(token)

    lands = [lax.empty((N_CHIPS,) + b.shape, b.dtype) for b in bufs]
    out = pl.pallas_call(
        body, name=name,
        out_shape=(pltpu.SemaphoreType.DMA((3 * n,)), pltpu.SemaphoreType.DMA((3 * n,)), *[pltpu.HBM(a.shape, a.dtype) for a in bufs + lands],
                   jax.ShapeDtypeStruct((8, 128), F32)),
        in_specs=[_HBM] * (2 * n), out_specs=(_SEM, _SEM, *[_HBM] * (2 * n), pl.BlockSpec(memory_space=pltpu.VMEM)),
        input_output_aliases={i: 2 + i for i in range(2 * n)}, compiler_params=pltpu.CompilerParams(has_side_effects=_EFFECT),
    )(*[_in_hbm(a) for a in bufs + lands])
    return (out[0], out[1], list(out[2:2 + n]), list(out[2 + n:2 + 2 * n])), out[-1][0, 0]


def _allgather_wait(pending, after, name):
    send_sems, recv_sems, bufs, lands = pending
    n = len(bufs)

    def body(*refs):
        x_refs, land_refs, send_sems, recv_sems = refs[:n], refs[n:2 * n], refs[2 * n], refs[2 * n + 1]
        for cp in _half_copies(x_refs, land_refs, send_sems, recv_sems, False):
            cp.wait_send()
        for cp in _half_copies(x_refs, land_refs, send_sems, recv_sems, True):
            cp.wait_recv()

    out = pl.pallas_call(
        body, name=name, out_shape=tuple(pltpu.HBM(a.shape, a.dtype) for a in bufs + lands),
        in_specs=[_HBM] * (2 * n) + [_SEM, _SEM, _ANY], out_specs=tuple([_HBM] * (2 * n)), input_output_aliases={i: i for i in range(2 * n)},
        compiler_params=pltpu.CompilerParams(has_side_effects=_EFFECT),
    )(*bufs, *lands, send_sems, recv_sems, after)
    return list(out[n:])


def _sibling_forward(lands, name):
    n = len(lands)
    plan, first, n_sems = _gather_pieces([jax.ShapeDtypeStruct(l.shape[1:], l.dtype) for l in lands])

    def body(*refs):
        out_refs, (send_sems, recv_sems) = refs[n:2 * n], refs[2 * n:]
        x, y, c = _place()

        def copies(hc):
            return [pltpu.make_async_remote_copy(
                        src_ref=out_refs[i].at[2 * cx + cy, pl.ds(hc * plan[i][0] + q * plan[i][2], plan[i][2]), :],
                        dst_ref=out_refs[i].at[2 * cx + cy, pl.ds(hc * plan[i][0] + q * plan[i][2], plan[i][2]), :],
                        send_sem=send_sems.at[first[i] + 3 * q + j], recv_sem=recv_sems.at[first[i] + 3 * q + j],
                        device_id=(x, y, 1 - c), device_id_type=MESH)
                    for i in range(n) for q in range(plan[i][1]) for j, (cx, cy) in enumerate(_other_chips(x, y))]

        mine = copies(c)
        for cp in mine:
            cp.start()
        for cp in mine:
            cp.wait_send()
        for cp in copies(1 - c):
            cp.wait_recv()

    return pl.pallas_call(
        body, name=name, in_specs=[_HBM] * n, out_specs=[_HBM] * n, out_shape=[jax.ShapeDtypeStruct(l.shape, l.dtype) for l in lands],
        input_output_aliases={i: i for i in range(n)},
        scratch_shapes=[pltpu.SemaphoreType.DMA((n_sems,)), pltpu.SemaphoreType.DMA((n_sems,))],
    )(*lands)


N_PEERS = 7


def _direct_copies(p_refs, t_refs, send_sems, recv_sems):
    x, y, c = _place()
    copies = []
    for i, (p_ref, t_ref) in enumerate(zip(p_refs, t_refs)):
        for f in range(1, N_PEERS + 1):
            px, py, pc = x ^ (f >> 2), y ^ ((f >> 1) & 1), c ^ (f & 1)
            copies.append(pltpu.make_async_remote_copy(
                src_ref=p_ref.at[2 * px + py, pc], dst_ref=t_ref.at[f - 1], send_sem=send_sems.at[N_PEERS * i + f - 1],
                recv_sem=recv_sems.at[N_PEERS * i + f - 1], device_id=(px, py, pc), device_id_type=MESH))
    return copies


def _direct_scatter_start(ps, name):
    n = len(ps)

    def body(*refs):
        p_refs, t_refs, (send_sems, recv_sems), token = refs[:n], refs[n:2 * n], refs[2 * n:2 * n + 2], refs[-1]
        for cp in _direct_copies(p_refs, t_refs, send_sems, recv_sems):
            cp.start()
        token[...] = jnp.zeros_like(token)

    lands = [lax.empty((N_PEERS,) + p.shape[2:], p.dtype) for p in ps]
    out = pl.pallas_call(
        body, name=name,
        out_shape=(pltpu.SemaphoreType.DMA((N_PEERS * n,)), pltpu.SemaphoreType.DMA((N_PEERS * n,)),
                   *[pltpu.HBM(a.shape, a.dtype) for a in ps + lands], jax.ShapeDtypeStruct((8, 128), F32)),
        in_specs=[_HBM] * (2 * n), out_specs=(_SEM, _SEM, *[_HBM] * (2 * n), pl.BlockSpec(memory_space=pltpu.VMEM)),
        input_output_aliases={i: 2 + i for i in range(2 * n)}, compiler_params=pltpu.CompilerParams(has_side_effects=_EFFECT),
    )(*[_in_hbm(a) for a in ps + lands])
    return (out[0], out[1], list(out[2:2 + n]), list(out[2 + n:2 + 2 * n])), out[-1][0, 0]


def _direct_scatter_wait(pending, after, name):
    send_sems, recv_sems, ps, lands = pending
    n = len(ps)

    def body(*refs):
        p_refs, t_refs, send_sems, recv_sems = refs[:n], refs[n:2 * n], refs[2 * n], refs[2 * n + 1]
        for cp in _direct_copies(p_refs, t_refs, send_sems, recv_sems):
            cp.wait_send()
            cp.wait_recv()

    out = pl.pallas_call(
        body, name=name, out_shape=tuple(pltpu.HBM(a.shape, a.dtype) for a in ps + lands),
        in_specs=[_HBM] * (2 * n) + [_SEM, _SEM, _ANY], out_specs=tuple([_HBM] * (2 * n)),
        input_output_aliases={i: i for i in range(2 * n)}, compiler_params=pltpu.CompilerParams(has_side_effects=_EFFECT),
    )(*ps, *lands, send_sems, recv_sems, after)
    return list(out[:n]), list(out[n:])


def _sum_direct(p, t, name):
    _, _, R, C = p.shape
    tr = _pick(R, 512, 16)

    def body(x_ref, y_ref, c_ref, p_ref, t_ref, o_ref):
        acc = p_ref[...].astype(F32)
        for f in range(N_PEERS):
            acc = acc + t_ref[f].astype(F32)
        o_ref[...] = acc

    grid_spec = pltpu.PrefetchScalarGridSpec(
        num_scalar_prefetch=3, grid=(R // tr,),
        in_specs=[pl.BlockSpec((None, None, tr, C), lambda i, x_ref, y_ref, c_ref: (2 * x_ref[0] + y_ref[0], c_ref[0], i, 0)),
                  pl.BlockSpec((N_PEERS, tr, C), lambda i, x_ref, y_ref, c_ref: (0, i, 0))],
        out_specs=pl.BlockSpec((None, tr, C), lambda i, x_ref, y_ref, c_ref: (c_ref[0], i, 0)))
    return pl.pallas_call(body, name=name, grid_spec=grid_spec, out_shape=jax.ShapeDtypeStruct((2, R, C), F32),
                          compiler_params=pltpu.CompilerParams(dimension_semantics=("parallel",)))(
        _axis_scalar("x"), _axis_scalar("y"), _axis_scalar("c"), p, t)


def _adamw(w, g, m, v, name):
    R, C = w.shape
    tr = _pick(R, 256, 8)

    def body(w_ref, g_ref, m_ref, v_ref, d_ref, nm_ref, nv_ref):
        g_ = g_ref[...]
        m_ = ADAM_B1 * m_ref[...] + (1.0 - ADAM_B1) * g_
        v_ = ADAM_B2 * v_ref[...] + (1.0 - ADAM_B2) * (g_ * g_)
        m_hat = m_ / (1.0 - ADAM_B1 ** ADAM_STEP)
        v_hat = v_ / (1.0 - ADAM_B2 ** ADAM_STEP)
        d_ref[...] = -ADAM_LR * (m_hat / (jnp.sqrt(v_hat) + ADAM_EPS) + ADAM_WD * w_ref[...])
        nm_ref[...] = m_
        nv_ref[...] = v_

    row = pl.BlockSpec((tr, C), lambda i: (i, 0))
    shp = jax.ShapeDtypeStruct((R, C), F32)
    return pl.pallas_call(body, name=name, grid=(R // tr,), in_specs=[row] * 4, out_specs=[row] * 3, out_shape=[shp] * 3,
                          compiler_params=pltpu.CompilerParams(dimension_semantics=("parallel",)))(w, g, m, v)


BIG_SPECS = (("ev_w_in", 1024, 1440, 1), ("ev_w_uq", 256, 768, 1), ("ev_w_ukv", 128, 1024, 1), ("ev_w_out", 1024, 1024, 0),
             ("od_w_in", 1024, 6144, 1), ("od_w_out", 2048, 1024, 0), ("mlp_w1_0", 1024, 4096, 1), ("mlp_w1_1", 1024, 4096, 1),
             ("mlp_w2_0", 4096, 1024, 0), ("mlp_w2_1", 4096, 1024, 0))
BIG_PARAMS = (("ev_w_in", ("ev_w_in",)), ("ev_w_uq", ("ev_w_uq",)), ("ev_w_ukv", ("ev_w_ukv",)), ("ev_w_out", ("ev_w_out",)),
              ("od_w_in", ("od_w_in",)), ("od_w_out", ("od_w_out",)), ("mlp_w1", ("mlp_w1_0", "mlp_w1_1")),
              ("mlp_w2", ("mlp_w2_0", "mlp_w2_1")))
REPLICATED = ("ev_conv_b", "ev_w_rg_a", "ev_b_rg_a", "ev_w_rg_x", "ev_b_rg_x", "ev_lru_lambda", "ev_q_norm_g", "ev_kv_norm_g",
              "ln_mix_g", "ln_mix_b", "ln_mlp_g", "ln_mlp_b")
SMALL_SHARDED = ("meta_tokens", "ev_conv_w")
COL_SHARD_GRADS = ("od_w_in", "mlp_w1_0", "mlp_w1_1")
MATRIX_GROUPS = (("ev_w_in", "ev_w_uq", "ev_w_ukv", "ev_w_out"), ("mlp_w1_0", "mlp_w2_0"), ("od_w_in", "od_w_out", "mlp_w1_1", "mlp_w2_1"))
LAYER_NORMS = ("ln_mix_g", "ln_mix_b", "ln_mlp_g", "ln_mlp_b")
WEIGHT_NAMES = ("meta_tokens", "ev_w_in", "ev_conv_w", "ev_conv_b", "ev_w_rg_a", "ev_b_rg_a", "ev_w_rg_x", "ev_b_rg_x",
                "ev_lru_lambda", "ev_q_norm_g", "ev_w_uq", "ev_kv_norm_g", "ev_w_ukv", "ev_w_out", "od_w_in", "od_w_out",
                "ln_mix_g", "ln_mix_b", "mlp_w1", "mlp_w2", "ln_mlp_g", "ln_mlp_b")


def _to_rows(flat, row_align):
    n = flat.shape[-1]
    rows = _round_up(-(-n // PACK_COLS), row_align)
    pad = rows * PACK_COLS - n
    if pad:
        flat = jnp.pad(flat, [(0, 0)] * (flat.ndim - 1) + [(0, pad)])
    return flat.reshape(flat.shape[:-1] + (rows, PACK_COLS))


def _shard_shape(K, N, axis):
    return (K // N_CHIPS, N) if axis == 0 else (K, N // N_CHIPS)


def _gather_shards(stacked, K, N, axis):
    if axis == 0:
        return stacked.reshape(K, N)
    return stacked.transpose(1, 0, 2).reshape(K, N)


def _split_shards(full, K, N, axis):
    if axis == 0:
        return full.reshape(N_CHIPS, -1)
    return full.reshape(K, N_CHIPS, N // N_CHIPS).transpose(1, 0, 2).reshape(N_CHIPS, -1)


def kernel(x, meta_tokens, ev_w_in, ev_conv_w, ev_conv_b, ev_w_rg_a, ev_b_rg_a, ev_w_rg_x, ev_b_rg_x, ev_lru_lambda, ev_q_norm_g, ev_w_uq, ev_kv_norm_g, ev_w_ukv, ev_w_out, od_w_in, od_w_out, ln_mix_g, ln_mix_b, mlp_w1, mlp_w2, ln_mlp_g, ln_mlp_b, loss_target, m_meta_tokens, m_ev_w_in, m_ev_conv_w, m_ev_conv_b, m_ev_w_rg_a, m_ev_b_rg_a, m_ev_w_rg_x, m_ev_b_rg_x, m_ev_lru_lambda, m_ev_q_norm_g, m_ev_w_uq, m_ev_kv_norm_g, m_ev_w_ukv, m_ev_w_out, m_od_w_in, m_od_w_out, m_ln_mix_g, m_ln_mix_b, m_mlp_w1, m_mlp_w2, m_ln_mlp_g, m_ln_mlp_b, v_meta_tokens, v_ev_w_in, v_ev_conv_w, v_ev_conv_b, v_ev_w_rg_a, v_ev_b_rg_a, v_ev_w_rg_x, v_ev_b_rg_x, v_ev_lru_lambda, v_ev_q_norm_g, v_ev_w_uq, v_ev_kv_norm_g, v_ev_w_ukv, v_ev_w_out, v_od_w_in, v_od_w_out, v_ln_mix_g, v_ln_mix_b, v_mlp_w1, v_mlp_w2, v_ln_mlp_g, v_ln_mlp_b):
    given = dict(locals())
    local_big = {"ev_w_in": ev_w_in[0], "ev_w_uq": ev_w_uq[0], "ev_w_ukv": ev_w_ukv[0], "ev_w_out": ev_w_out[0],
                 "od_w_in": od_w_in[0], "od_w_out": od_w_out[0], "mlp_w1_0": mlp_w1[0], "mlp_w1_1": mlp_w1[1],
                 "mlp_w2_0": mlp_w2[0], "mlp_w2_1": mlp_w2[1]}

    specs = {spec[0]: spec for spec in BIG_SPECS}
    mixer0_m, mlp0_m, layer1_m = MATRIX_GROUPS

    def shards(names):
        return [local_big[n].astype(MXU_DTYPE) for n in names]

    def whole(stacked, n):
        _, K, N, ax = specs[n]
        return stacked if n in COL_SHARD_GRADS else _gather_shards(stacked, K, N, ax)

    def filled(gathered, own, names):
        return {n: whole(_with_own(g_, o_), n) for n, g_, o_ in zip(names, gathered, own)}

    own_a, own_b, own_c = shards(mixer0_m), shards(mlp0_m), shards(layer1_m)
    small = [meta_tokens, jnp.pad(ev_conv_w[0], ((0, 16 - CONV_WIDTH), (0, 0)))]
    gathered_a = _allgather_chips(own_a + small, "weight_allgather_mixer0")
    pending_b, token1 = _allgather_start(own_b, "weight_allgather_mlp0_start")
    pending_c, token2 = _allgather_start(own_c, "weight_allgather_layer1_start")
    meta_full = _gather_shards(_with_own(gathered_a[-2], small[0]), N_META, D_MODEL, 1)
    conv_full = _gather_shards(_with_own(gathered_a[-1], small[1])[:, :CONV_WIDTH], CONV_WIDTH, LRU_WIDTH, 1)

    def slots(names, dtype):
        return {n: jnp.zeros((N_CHIPS, specs[n][1], specs[n][2] // N_CHIPS) if n in COL_SHARD_GRADS else specs[n][1:3], dtype) for n in names}

    def norms(names, layer):
        return {n: given[n][layer] for n in names}

    def finish_gather(pending, own, after, names, tag):
        landed = _allgather_wait(pending, lax.stop_gradient(after), "weight_allgather_%s_wait" % tag)
        return filled(_sibling_forward(landed, "weight_allgather_%s_forward" % tag), own, names)

    diff_a = {**slots(mixer0_m, F32), **norms(("ln_mix_g", "ln_mix_b"), 0), **{n: given[n] for n in REPLICATED if n not in LAYER_NORMS},
              "x": x, "meta_tokens": meta_full, "ev_conv_w": conv_full}
    diff_b = {**slots(mlp0_m, MXU_DTYPE), **norms(("ln_mlp_g", "ln_mlp_b"), 0)}
    diff_c = {**slots(layer1_m, MXU_DTYPE), **norms(LAYER_NORMS, 1)}
    w_a = filled(gathered_a[:len(mixer0_m)], own_a, mixer0_m)
    h_a, back_a = jax.vjp(lambda d: _mixer0(d, w_a, token1 + token2), diff_a)
    w_b = finish_gather(pending_b, own_b, h_a, mlp0_m, "mlp0")
    h_b, back_b = jax.vjp(lambda d, hh: _mlp0(d, hh, w_b), diff_b, h_a)
    w_c = finish_gather(pending_c, own_c, h_b, layer1_m, "layer1")
    loss, back_c = jax.vjp(lambda d, hh: _layer1_loss(d, hh, w_c, loss_target), diff_c, h_b)
    loss = lax.psum(loss, ("x", "y", "c"))

    def blocks_of(grad, n):
        _, K, N, ax = specs[n]
        if n in COL_SHARD_GRADS:
            blocks = grad
        elif ax == 0:
            blocks = grad.reshape(N_CHIPS, K // N_CHIPS, N)
        else:
            blocks = grad.reshape(K, N_CHIPS, N // N_CHIPS).transpose(1, 0, 2)
        return blocks.reshape(N_CHIPS, 2, blocks.shape[1] // 2, blocks.shape[2])

    def start_reduce(grads_of, names, tag):
        return _direct_scatter_start([blocks_of(grads_of[n], n) for n in names], "grad_scatter_%s_start" % tag)

    g_c, dh = back_c(jnp.ones((), F32))
    flying_c, token = start_reduce(g_c, layer1_m, "layer1")
    g_b, dh = back_b(dh + token)
    flying_b, token = start_reduce(g_b, mlp0_m, "mlp0")
    (g_a,) = back_a(dh + token)
    ps_c, ts_c = _direct_scatter_wait(flying_c, g_a["x"], "grad_scatter_layer1_wait")
    ps_b, ts_b = _direct_scatter_wait(flying_b, g_a["x"], "grad_scatter_mlp0_wait")

    g = {**g_a, **g_b, **g_c}
    g.update({n: jnp.stack([(g_b if n in g_b else g_a)[n], g_c[n]]) for n in LAYER_NORMS})
    repl = jnp.concatenate([g[n].reshape(-1) for n in REPLICATED]).reshape(N_CHIPS, -1)
    small = [_split_shards(g["meta_tokens"], N_META, D_MODEL, 1), _split_shards(g["ev_conv_w"], CONV_WIDTH, LRU_WIDTH, 1), repl]
    small = [pc.reshape(N_CHIPS, 2, -1) for pc in small]
    n_small = sum(pc.shape[2] for pc in small)
    small.append(jnp.zeros((N_CHIPS, 2, _round_up(n_small, 32 * PACK_COLS) - n_small), F32))
    p_small = jnp.concatenate(small, axis=2).reshape(N_CHIPS, 2, -1, PACK_COLS)
    ss_a = _sibling_reduce([blocks_of(g_a[n], n) for n in mixer0_m] + [p_small], [MXU_DTYPE] * len(mixer0_m) + [F32], "mixer0_")
    ts_a = list(_chip_scatter(ss_a, "grad_chip_scatter_mixer0"))
    fs = [_sum_partials(s, t, "grad_chip_sum_mixer0_%d" % i) for i, (s, t) in enumerate(zip(ss_a, ts_a))]
    fs += [_sum_direct(p, t, "grad_sum_%d" % i) for i, (p, t) in enumerate(zip(ps_b + ps_c, ts_b + ts_c))]
    reduced = _sibling_gather(fs, "grad_sibling_gather")
    red_big = dict(zip(mixer0_m + ("small",) + mlp0_m + layer1_m, reduced))
    red_small = red_big.pop("small").reshape(2, -1)

    grads = {}
    for name, parts in BIG_PARAMS:
        grads[name] = jnp.stack([red_big[part].reshape(given[name].shape[1:]) for part in parts])

    def take(off, sz):
        return jnp.concatenate([red_small[0, off // 2:(off + sz) // 2], red_small[1, off // 2:(off + sz) // 2]])

    off = 0
    for name in SMALL_SHARDED:
        sz = given[name].size
        grads[name] = take(off, sz).reshape(given[name].shape)
        off += sz
    n_repl = repl.shape[1]
    own_repl = _to_rows(take(off, n_repl), 16)
    repl_all = _with_own(_allgather_chips([own_repl], "replicated_allgather")[0], own_repl).reshape(N_CHIPS, -1)[:, :n_repl].reshape(-1)
    off = 0
    for name in REPLICATED:
        sz = given[name].size
        grads[name] = repl_all[off:off + sz].reshape(given[name].shape)
        off += sz

    delta, new_m, new_v = {}, {}, {}
    for name, _ in BIG_PARAMS:
        shp = given[name].shape
        two_d = (-1, shp[-1])
        d, nm, nv = _adamw(given[name].reshape(two_d), grads[name].reshape(two_d), given["m_" + name].reshape(two_d),
                           given["v_" + name].reshape(two_d), "adamw_" + name)
        delta[name], new_m[name], new_v[name] = d.reshape(shp), nm.reshape(shp), nv.reshape(shp)
    smalls = SMALL_SHARDED + REPLICATED

    def pack_small(get):
        return _to_rows(jnp.concatenate([get(n).reshape(-1) for n in smalls]), 8)

    outs = _adamw(pack_small(lambda n: given[n]), pack_small(lambda n: grads[n]), pack_small(lambda n: given["m_" + n]),
                  pack_small(lambda n: given["v_" + n]), "adamw_small")
    for res, flat in zip((delta, new_m, new_v), outs):
        flat, off = flat.reshape(-1), 0
        for n in smalls:
            sz = given[n].size
            res[n] = flat[off:off + sz].reshape(given[n].shape)
            off += sz

    return (loss, g_a["x"], *[grads[n] for n in WEIGHT_NAMES], *[delta[n] for n in WEIGHT_NAMES],
            *[new_m[n] for n in WEIGHT_NAMES], *[new_v[n] for n in WEIGHT_NAMES])
```

```python
import functools
import math

import jax
import jax.numpy as jnp
from jax import lax
from jax.experimental import pallas as pl
from jax.experimental.pallas import tpu as pltpu

F32 = jnp.float32
MXU_DTYPE = jnp.bfloat16

D_MODEL = 1024
N_META = 16
LRU_WIDTH = 512
LRU_HEADS = 4
LRU_HEAD_DIM = 128
CONV_WIDTH = 4
LRU_C = 8.0
MLA_HEADS = 8
MLA_NOPE = 64
MLA_ROPE = 32
MLA_V = 64
MLA_Q_RANK = 256
MLA_KV_RANK = 128
RET_HEADS = 4
RET_QK_DIM = 256
RET_V_DIM = 512
D_FF = 4096
ROPE_BASE = 10000.0
DN_ALPHA = 4.0 ** 0.25
EPS = 1e-5
NEG_INF = -1e30
SEQ_BLOCK = 128

ADAM_LR = 0.001
ADAM_B1 = 0.9
ADAM_B2 = 0.999
ADAM_EPS = 1e-08
ADAM_WD = 0.01
ADAM_STEP = 10

PACK_COLS = 1024
N_CHIPS = 4

MESH = pl.DeviceIdType.MESH


def _pick(n, target, align):
    best = None
    for t in range(align, min(n, target) + 1, align):
        if n % t == 0:
            best = t
    return n if best is None else best


def _round_up(n, m):
    return (n + m - 1) // m * m


def _relu2(a):
    r = jnp.maximum(a, 0.0)
    return r * r


def _mm_nn(a, w, act, name, out_dtype=F32):
    M, K = a.shape
    sharded = w.ndim == 3
    n = w.shape[-1]
    N = n * (w.shape[0] if sharded else 1)
    tm = _pick(M, 1088 if K * a.dtype.itemsize <= 4096 else 544, 8)
    tn = _pick(n, 1024, 128)
    per = n // tn

    def body(a_ref, w_ref, o_ref):
        av = a_ref[...]
        if act:
            av = _relu2(av.astype(F32))
        o_ref[...] = jnp.dot(av.astype(MXU_DTYPE), w_ref[...].astype(MXU_DTYPE), preferred_element_type=F32).astype(out_dtype)

    w_spec = pl.BlockSpec((None, K, tn), lambda i, j: (j // per, 0, j % per)) if sharded else pl.BlockSpec((K, tn), lambda i, j: (0, j))
    return pl.pallas_call(
        body, name=name,
        grid=(M // tm, N // tn),
        in_specs=[pl.BlockSpec((tm, K), lambda i, j: (i, 0)), w_spec],
        out_specs=pl.BlockSpec((tm, tn), lambda i, j: (i, j)),
        out_shape=jax.ShapeDtypeStruct((M, N), out_dtype),
        compiler_params=pltpu.CompilerParams(dimension_semantics=("parallel", "arbitrary")),
    )(a, w)


def _mm_nt(g, w, a_src, name, out_dtype=F32):
    M, N = g.shape
    sharded = w.ndim == 3
    K, n = w.shape[-2], w.shape[-1]
    if sharded:
        tk, nk = N, 1
    else:
        tk = N if N * g.dtype.itemsize <= 8192 else _pick(N, 2048, 128)
        nk = N // tk
    tm = _pick(M, 1088 if tk * g.dtype.itemsize <= 4096 else 544, 8)
    tn = _pick(K, 1024, 128)
    has_src = a_src is not None
    assert nk == 1 or out_dtype == F32

    def body(*refs):
        if has_src:
            g_ref, w_ref, s_ref, o_ref = refs
        else:
            g_ref, w_ref, o_ref = refs
        nt = (((1,), (1,)), ((), ()))
        if sharded:
            r = sum(lax.dot_general(g_ref[:, s * n:(s + 1) * n].astype(MXU_DTYPE), w_ref[s].astype(MXU_DTYPE), nt, preferred_element_type=F32)
                    for s in range(w_ref.shape[0]))
        else:
            r = lax.dot_general(g_ref[...].astype(MXU_DTYPE), w_ref[...].astype(MXU_DTYPE), nt, preferred_element_type=F32)
        if has_src:
            r = r * (2.0 * jnp.maximum(s_ref[...].astype(F32), 0.0))
        if nk == 1:
            o_ref[...] = r.astype(out_dtype)
        else:
            k = pl.program_id(2)

            @pl.when(k == 0)
            def _():
                o_ref[...] = r

            @pl.when(k > 0)
            def _():
                o_ref[...] += r

    w_spec = (pl.BlockSpec((w.shape[0], tn, n), lambda i, j, k: (0, j, 0)) if sharded
              else pl.BlockSpec((tn, tk), lambda i, j, k: (j, k)))
    in_specs = [pl.BlockSpec((tm, tk), lambda i, j, k: (i, k)), w_spec]
    args = [g, w]
    if has_src:
        assert nk == 1
        in_specs.append(pl.BlockSpec((tm, tn), lambda i, j, k: (i, j)))
        args.append(a_src)
    return pl.pallas_call(
        body, name=name,
        grid=(M // tm, K // tn, nk),
        in_specs=in_specs,
        out_specs=pl.BlockSpec((tm, tn), lambda i, j, k: (i, j)),
        out_shape=jax.ShapeDtypeStruct((M, K), out_dtype),
        compiler_params=pltpu.CompilerParams(dimension_semantics=("parallel", "parallel", "arbitrary")),
    )(*args)


def _mm_tn(a, g, act, name, col_shards=1, out_dtype=F32):
    M, K = a.shape
    _, N = g.shape
    n = N // col_shards
    tm, tn, tk = _pick(K, 1024, 128), _pick(n, 1024, 128), _pick(M, 2176, 8)
    nk = M // tk
    per = n // tn
    direct = out_dtype == F32

    def body(a_ref, g_ref, o_ref, *scratch):
        acc_ref = o_ref if direct else scratch[0]
        k = pl.program_id(2)
        av = a_ref[...]
        if act:
            av = _relu2(av.astype(F32))
        r = lax.dot_general(av.astype(MXU_DTYPE), g_ref[...].astype(MXU_DTYPE),
                            (((0,), (0,)), ((), ())), preferred_element_type=F32)

        @pl.when(k == 0)
        def _():
            acc_ref[...] = r

        @pl.when(k > 0)
        def _():
            acc_ref[...] += r

        if not direct:
            @pl.when(k == nk - 1)
            def _():
                o_ref[...] = acc_ref[...].astype(out_dtype)

    if col_shards == 1:
        out_spec, out_shape = pl.BlockSpec((tm, tn), lambda i, j, k: (i, j)), (K, N)
    else:
        out_spec, out_shape = pl.BlockSpec((None, tm, tn), lambda i, j, k: (j // per, i, j % per)), (col_shards, K, n)
    return pl.pallas_call(
        body, name=name,
        grid=(K // tm, N // tn, nk),
        in_specs=[pl.BlockSpec((tk, tm), lambda i, j, k: (k, i)), pl.BlockSpec((tk, tn), lambda i, j, k: (k, j))],
        out_specs=out_spec,
        out_shape=jax.ShapeDtypeStruct(out_shape, out_dtype),
        scratch_shapes=[] if direct else [pltpu.VMEM((tm, tn), F32)],
        compiler_params=pltpu.CompilerParams(dimension_semantics=("parallel", "parallel", "arbitrary")),
    )(a, g)


@functools.partial(jax.custom_vjp, nondiff_argnums=(3, 4, 5, 6))
def matmul(a, w, w_grad_slot, act, name, out_dtype, col_shards):
    return _mm_nn(a, w, act, name + "_fwd", out_dtype)


def _matmul_fwd(a, w, w_grad_slot, act, name, out_dtype, col_shards):
    return _mm_nn(a, w, act, name + "_fwd", out_dtype), (a, w, jnp.zeros((), w_grad_slot.dtype))


def _matmul_bwd(act, name, out_dtype, col_shards, res, g):
    a, w, slot_like = res
    w_grad_dtype = slot_like.dtype
    da = _mm_nt(g, w, a if act else None, name + "_dx")
    dw = _mm_tn(a, g, act, name + "_dw", col_shards, w_grad_dtype)
    return da, None, dw


matmul.defvjp(_matmul_fwd, _matmul_bwd)


@functools.partial(jax.custom_vjp, nondiff_argnums=(5,))
def mlp(h, w1, w2, w1_grad_slot, w2_grad_slot, name):
    u = _mm_nn(h, w1, False, name + "_w1_fwd", out_dtype=MXU_DTYPE)
    return _mm_nn(u, w2, True, name + "_w2_fwd")


def _mlp_fwd(h, w1, w2, w1_grad_slot, w2_grad_slot, name):
    u = _mm_nn(h, w1, False, name + "_w1_fwd", out_dtype=MXU_DTYPE)
    return _mm_nn(u, w2, True, name + "_w2_fwd"), (h, u, w1, w2, jnp.zeros((), w1_grad_slot.dtype))


def _mlp_bwd(name, res, df):
    h, u, w1, w2, slot_like = res
    du = _mm_nt(df, w2, u, name + "_w2_dx", out_dtype=MXU_DTYPE)
    dw2 = _mm_tn(u, df, True, name + "_w2_dw", 1, slot_like.dtype)
    dh = _mm_nt(du, w1, None, name + "_w1_dx")
    dw1 = _mm_tn(h, du, False, name + "_w1_dw", N_CHIPS, slot_like.dtype)
    return dh, None, None, dw1, dw2


mlp.defvjp(_mlp_fwd, _mlp_bwd)


def _ln_stats(z):
    mu = jnp.mean(z, axis=-1, keepdims=True)
    zc = z - mu
    var = jnp.mean(zc * zc, axis=-1, keepdims=True)
    return zc, lax.rsqrt(var + EPS)


def _ln_fwd_call(resid, branch, g, b, name):
    M, D = resid.shape
    tm = _pick(M, 544, 8)

    def body(r_ref, br_ref, g_ref, b_ref, o_ref):
        zc, rstd = _ln_stats(DN_ALPHA * r_ref[...] + br_ref[...])
        o_ref[...] = zc * rstd * g_ref[...] + b_ref[...]

    row = pl.BlockSpec((tm, D), lambda i: (i, 0))
    vec = pl.BlockSpec((1, D), lambda i: (0, 0))
    return pl.pallas_call(
        body, name=name, grid=(M // tm,), in_specs=[row, row, vec, vec], out_specs=row,
        out_shape=jax.ShapeDtypeStruct((M, D), F32),
        compiler_params=pltpu.CompilerParams(dimension_semantics=("parallel",)),
    )(resid, branch, g.reshape(1, D), b.reshape(1, D))


def _ln_bwd_call(resid, branch, g, dy, name):
    M, D = resid.shape
    tm = _pick(M, 544, 8)

    def body(r_ref, br_ref, g_ref, dy_ref, dz_ref, dg_ref, db_ref):
        @pl.when(pl.program_id(0) == 0)
        def _():
            dg_ref[...] = jnp.zeros_like(dg_ref)
            db_ref[...] = jnp.zeros_like(db_ref)

        zc, rstd = _ln_stats(DN_ALPHA * r_ref[...] + br_ref[...])
        xhat = zc * rstd
        dy = dy_ref[...]
        dxh = dy * g_ref[...]
        m1 = jnp.mean(dxh, axis=-1, keepdims=True)
        m2 = jnp.mean(dxh * xhat, axis=-1, keepdims=True)
        dz_ref[...] = rstd * (dxh - m1 - xhat * m2)
        dg_ref[...] += jnp.sum(dy * xhat, axis=0, keepdims=True)
        db_ref[...] += jnp.sum(dy, axis=0, keepdims=True)

    row = pl.BlockSpec((tm, D), lambda i: (i, 0))
    vec = pl.BlockSpec((1, D), lambda i: (0, 0))
    return pl.pallas_call(
        body, name=name, grid=(M // tm,), in_specs=[row, row, vec, row], out_specs=[row, vec, vec],
        out_shape=[jax.ShapeDtypeStruct((M, D), F32), jax.ShapeDtypeStruct((1, D), F32), jax.ShapeDtypeStruct((1, D), F32)],
        compiler_params=pltpu.CompilerParams(dimension_semantics=("arbitrary",)),
    )(resid, branch, g.reshape(1, D), dy)


@functools.partial(jax.custom_vjp, nondiff_argnums=(4,))
def deepnorm(resid, branch, g, b, name):
    return _ln_fwd_call(resid, branch, g, b, name + "_fwd")


def _deepnorm_fwd(resid, branch, g, b, name):
    return _ln_fwd_call(resid, branch, g, b, name + "_fwd"), (resid, branch, g)


def _deepnorm_bwd(name, res, dy):
    resid, branch, g = res
    dz, dg, db = _ln_bwd_call(resid, branch, g, dy, name + "_bwd")
    return DN_ALPHA * dz, dz, dg.reshape(g.shape), db.reshape(g.shape)


deepnorm.defvjp(_deepnorm_fwd, _deepnorm_bwd)


def _rms_fwd_call(x, g, name, col_block=0):
    R = x.shape[0]
    W = g.shape[-1]
    tr = _pick(R, 1088, 8)

    def body(x_ref, g_ref, o_ref):
        xv = x_ref[...]
        rstd = lax.rsqrt(jnp.mean(xv * xv, axis=-1, keepdims=True) + EPS)
        o_ref[...] = xv * rstd * g_ref[...]

    vec = pl.BlockSpec((1, W), lambda i: (0, 0))
    return pl.pallas_call(
        body, name=name, grid=(R // tr,), in_specs=[pl.BlockSpec((tr, W), lambda i: (i, col_block)), vec],
        out_specs=pl.BlockSpec((tr, W), lambda i: (i, 0)), out_shape=jax.ShapeDtypeStruct((R, W), F32),
        compiler_params=pltpu.CompilerParams(dimension_semantics=("parallel",)),
    )(x, g.reshape(1, W))


def _rms_bwd_call(x, g, dy, name, col_block=0):
    R = x.shape[0]
    W = g.shape[-1]
    tr = _pick(R, 1088, 8)

    def body(x_ref, g_ref, dy_ref, dx_ref, dg_ref):
        @pl.when(pl.program_id(0) == 0)
        def _():
            dg_ref[...] = jnp.zeros_like(dg_ref)

        xv = x_ref[...]
        rstd = lax.rsqrt(jnp.mean(xv * xv, axis=-1, keepdims=True) + EPS)
        xhat = xv * rstd
        dy = dy_ref[...]
        dxh = dy * g_ref[...]
        dx_ref[...] = rstd * (dxh - xhat * jnp.mean(dxh * xhat, axis=-1, keepdims=True))
        dg_ref[...] += jnp.sum(dy * xhat, axis=0, keepdims=True)

    row = pl.BlockSpec((tr, W), lambda i: (i, 0))
    vec = pl.BlockSpec((1, W), lambda i: (0, 0))
    return pl.pallas_call(
        body, name=name, grid=(R // tr,), in_specs=[pl.BlockSpec((tr, W), lambda i: (i, col_block)), vec, row], out_specs=[row, vec],
        out_shape=[jax.ShapeDtypeStruct((R, W), F32), jax.ShapeDtypeStruct((1, W), F32)],
        compiler_params=pltpu.CompilerParams(dimension_semantics=("arbitrary",)),
    )(x, g.reshape(1, W), dy)


def _loss_call(h, tgt, n_tokens, name):
    B, Tp, D = h.shape
    tr = _pick(Tp, 544, 8)

    def body(y_ref, t_ref, dy_ref, acc_ref):
        @pl.when(jnp.logical_and(pl.program_id(0) == 0, pl.program_id(1) == 0))
        def _():
            acc_ref[...] = jnp.zeros_like(acc_ref)

        t = lax.broadcasted_iota(jnp.int32, (tr, 1), 0) + pl.program_id(1) * tr
        counts = jnp.logical_and(t >= N_META, t < N_META + n_tokens)
        e = jnp.where(counts, y_ref[...] - t_ref[...], 0.0)
        dy_ref[...] = e * (1.0 / D)
        acc_ref[...] += jnp.sum(jnp.sum(e * e, axis=-1, keepdims=True), axis=0, keepdims=True) * (0.5 / D)

    row = pl.BlockSpec((None, tr, D), lambda b, i: (b, i, 0))
    one = pl.BlockSpec((1, 1), lambda b, i: (0, 0))
    return pl.pallas_call(
        body, name=name, grid=(B, Tp // tr), in_specs=[row, row], out_specs=[row, one],
        out_shape=[jax.ShapeDtypeStruct((B, Tp, D), F32), jax.ShapeDtypeStruct((1, 1), F32)],
        compiler_params=pltpu.CompilerParams(dimension_semantics=("arbitrary", "arbitrary")),
    )(h, tgt)


@functools.partial(jax.custom_vjp, nondiff_argnums=(2,))
def loss_head(h, tgt, n_tokens):
    return _loss_call(h, tgt, n_tokens, "loss_head")[1][0, 0]


def _loss_head_fwd(h, tgt, n_tokens):
    dy, acc = _loss_call(h, tgt, n_tokens, "loss_head")
    return acc[0, 0], dy


def _loss_head_bwd(n_tokens, dy, ct):
    return ct * dy, None


loss_head.defvjp(_loss_head_fwd, _loss_head_bwd)


_GELU_C = math.sqrt(2.0 / math.pi)


def _gelu_parts(x):
    x2 = x * x
    t = jnp.tanh(_GELU_C * (x + 0.044715 * x * x2))
    gelu = 0.5 * x * (1.0 + t)
    dgelu = 0.5 * (1.0 + t) + 0.5 * x * (1.0 - t * t) * (_GELU_C * (1.0 + 3.0 * 0.044715 * x2))
    return gelu, dgelu


def _sigmoid(x):
    return 1.0 / (1.0 + jnp.exp(-x))


def _scan8(a, b, carry, reverse):
    row = lax.broadcasted_iota(jnp.int32, a.shape, 0)
    for s in (1, 2, 4):
        shift = 8 - s if reverse else s
        keep = (row < 8 - s) if reverse else (row >= s)
        b = jnp.where(keep, a * pltpu.roll(b, shift, 0) + b, b)
        a = jnp.where(keep, a * pltpu.roll(a, shift, 0), a)
    return a * carry + b


def _lru_pre(prec_ref, prev_ref, first, cw_ref, cb_ref, wa_ref, ba_ref, wx_ref, bx_ref, sp_ref):
    tc = prec_ref.shape[0]
    prev = jnp.where(first, 0.0, prev_ref[...])
    ext = jnp.concatenate([prev, prec_ref[...]], axis=0)
    cw = cw_ref[...]
    taps = [ext[8:] if k == CONV_WIDTH - 1 else pltpu.roll(ext, CONV_WIDTH - 1 - k, 0)[8:] for k in range(CONV_WIDTH)]
    xc = cb_ref[...] + sum(cw[k:k + 1, :] * taps[k] for k in range(CONV_WIDTH))
    ga, gx = [], []
    for h in range(LRU_HEADS):
        xh = xc[:, h * LRU_HEAD_DIM:(h + 1) * LRU_HEAD_DIM].astype(MXU_DTYPE)
        ga.append(jnp.dot(xh, wa_ref[h].astype(MXU_DTYPE), preferred_element_type=F32))
        gx.append(jnp.dot(xh, wx_ref[h].astype(MXU_DTYPE), preferred_element_type=F32))
    r = _sigmoid(jnp.concatenate(ga, axis=1) + ba_ref[...])
    i = _sigmoid(jnp.concatenate(gx, axis=1) + bx_ref[...])
    log_a = -LRU_C * r * sp_ref[...]
    a = jnp.exp(log_a)
    a2 = a * a
    mult = jnp.sqrt(-jnp.tanh(log_a) * (a2 + 1.0))
    return taps, xc, r, i, a, a2, mult


def _lru_fwd_call(p, cw, cb, wa, ba, wx, bx, sp):
    B, Tp, _ = p.shape
    W = LRU_WIDTH
    tc = SEQ_BLOCK
    nc = Tp // tc

    def body(pg_ref, prec_ref, prev_ref, cw_ref, cb_ref, wa_ref, ba_ref, wx_ref, bx_ref, sp_ref, y_ref, h_ref, carry_ref):
        first = pl.program_id(1) == 0

        @pl.when(first)
        def _():
            carry_ref[...] = jnp.zeros_like(carry_ref)

        _, xc, r, i, a, a2, mult = _lru_pre(prec_ref, prev_ref, first, cw_ref, cb_ref, wa_ref, ba_ref, wx_ref, bx_ref, sp_ref)
        b = mult * (i * xc)
        carry = carry_ref[0:1, :]
        for t in range(tc // 8):
            h = _scan8(a[8 * t:8 * t + 8], b[8 * t:8 * t + 8], carry, False)
            h_ref[8 * t:8 * t + 8, :] = h
            carry = h[7:8, :]
        carry_ref[...] = jnp.broadcast_to(carry, carry_ref.shape)
        y_ref[...] = h_ref[...] * _gelu_parts(pg_ref[...])[0]

    cur = pl.BlockSpec((None, tc, W), lambda b, j: (b, j, 0))
    rec = pl.BlockSpec((None, tc, W), lambda b, j: (b, j, 1))
    prev = pl.BlockSpec((None, 8, W), lambda b, j: (b, jnp.maximum(j * (tc // 8) - 1, 0), 1))
    vec = pl.BlockSpec((1, W), lambda b, j: (0, 0))
    cws = pl.BlockSpec((CONV_WIDTH, W), lambda b, j: (0, 0))
    wsp = pl.BlockSpec((LRU_HEADS, LRU_HEAD_DIM, LRU_HEAD_DIM), lambda b, j: (0, 0, 0))
    return pl.pallas_call(
        body, name="lru_fwd", grid=(B, nc),
        in_specs=[cur, rec, prev, cws, vec, wsp, vec, wsp, vec, vec],
        out_specs=[cur, cur],
        out_shape=[jax.ShapeDtypeStruct((B, Tp, W), F32), jax.ShapeDtypeStruct((B, Tp, W), F32)],
        scratch_shapes=[pltpu.VMEM((8, W), F32)],
        compiler_params=pltpu.CompilerParams(dimension_semantics=("arbitrary", "arbitrary")),
    )(p, p, p, cw, cb, wa, ba, wx, bx, sp)


def _lru_bwd_call(p, hseq, dy, cw, cb, wa, ba, wx, bx, sp, dpq, dpkv, dkpe):
    B, Tp, P = p.shape
    W = LRU_WIDTH
    tc = SEQ_BLOCK
    nc = Tp // tc
    HD = LRU_HEAD_DIM

    def body(pg_ref, prec_ref, prev_ref, h_ref, hprev_ref, dy_ref, cw_ref, cb_ref, wa_ref, ba_ref, wx_ref, bx_ref, sp_ref,
             dpq_ref, dpkv_ref, dkpe_ref, dp_ref, dcw_ref, dcb_ref, dwa_ref, dba_ref, dwx_ref, dbx_ref, dsp_ref,
             gcar_ref, anext_ref, halo_ref, g_ref):
        j = pl.program_id(1)
        first = j == nc - 1
        last = j == 0

        @pl.when(jnp.logical_and(pl.program_id(0) == 0, last))
        def _():
            for ref in (dcw_ref, dcb_ref, dwa_ref, dba_ref, dwx_ref, dbx_ref, dsp_ref):
                ref[...] = jnp.zeros_like(ref)

        @pl.when(last)
        def _():
            gcar_ref[...] = jnp.zeros_like(gcar_ref)
            anext_ref[...] = jnp.zeros_like(anext_ref)
            halo_ref[...] = jnp.zeros_like(halo_ref)

        taps, xc, r, i, a, a2, mult = _lru_pre(prec_ref, prev_ref, first, cw_ref, cb_ref, wa_ref, ba_ref, wx_ref, bx_ref, sp_ref)
        row = lax.broadcasted_iota(jnp.int32, (tc, W), 0)
        gelu, dgelu = _gelu_parts(pg_ref[...])
        dy = dy_ref[...]
        hcur = h_ref[...]
        dp_ref[:, 0:W] = dy * hcur * dgelu
        dp_ref[:, 2 * W:2 * W + MLA_Q_RANK] = dpq_ref[...]
        dp_ref[:, _KPE_START - MLA_KV_RANK:_KPE_START] = dpkv_ref[...]
        dp_ref[:, _KPE_START:P] = pltpu.roll(dkpe_ref[...], HEAD_LANES - MLA_NOPE, 1)[:, 0:P - _KPE_START]
        dh = dy * gelu
        a_next = jnp.where(row == tc - 1, anext_ref[0:1, :], pltpu.roll(a, tc - 1, 0))
        carry = gcar_ref[0:1, :]
        for t in reversed(range(tc // 8)):
            g = _scan8(a_next[8 * t:8 * t + 8], dh[8 * t:8 * t + 8], carry, True)
            g_ref[8 * t:8 * t + 8, :] = g
            carry = g[0:1, :]
        gcar_ref[...] = jnp.broadcast_to(carry, gcar_ref.shape)
        anext_ref[...] = jnp.broadcast_to(a[0:1, :], anext_ref.shape)
        G = g_ref[...]
        h_before = jnp.where(first, 0.0, hprev_ref[7:8, :])
        hprev = jnp.where(row == 0, h_before, pltpu.roll(hcur, 1, 0))
        d_a = G * hprev
        gx_ = G * xc
        d_mult = gx_ * i
        d_i = gx_ * mult
        dxc = G * (mult * i)
        d_la = d_a * a - d_mult * (a2 / mult)
        sp = sp_ref[...]
        d_r = d_la * (-LRU_C * sp)
        dsp_ref[...] += jnp.sum(d_la * (-LRU_C * r), axis=0, keepdims=True)
        dga = d_r * r * (1.0 - r)
        dgx = d_i * i * (1.0 - i)
        dba_ref[...] += jnp.sum(dga, axis=0, keepdims=True)
        dbx_ref[...] += jnp.sum(dgx, axis=0, keepdims=True)
        back = []
        for h in range(LRU_HEADS):
            sl = slice(h * HD, (h + 1) * HD)
            xh = xc[:, sl].astype(MXU_DTYPE)
            ah = dga[:, sl].astype(MXU_DTYPE)
            bh = dgx[:, sl].astype(MXU_DTYPE)
            tn = (((0,), (0,)), ((), ()))
            nt = (((1,), (1,)), ((), ()))
            dwa_ref[h] += lax.dot_general(xh, ah, tn, preferred_element_type=F32)
            dwx_ref[h] += lax.dot_general(xh, bh, tn, preferred_element_type=F32)
            back.append(lax.dot_general(ah, wa_ref[h].astype(MXU_DTYPE), nt, preferred_element_type=F32)
                        + lax.dot_general(bh, wx_ref[h].astype(MXU_DTYPE), nt, preferred_element_type=F32))
        dxc = dxc + jnp.concatenate(back, axis=1)
        dcb_ref[...] += jnp.sum(dxc, axis=0, keepdims=True)
        for k in range(CONV_WIDTH):
            dcw_ref[k:k + 1, :] += jnp.sum(dxc * taps[k], axis=0, keepdims=True)
        ext = jnp.concatenate([dxc, halo_ref[...]], axis=0)
        cw = cw_ref[...]
        acc = cw[CONV_WIDTH - 1:CONV_WIDTH, :] * dxc
        for k in range(CONV_WIDTH - 1):
            s = CONV_WIDTH - 1 - k
            acc = acc + cw[k:k + 1, :] * pltpu.roll(ext, tc + 8 - s, 0)[:tc]
        dp_ref[:, W:2 * W] = acc
        halo_ref[...] = dxc[0:8, :]

    rev = lambda j: nc - 1 - j
    cur = pl.BlockSpec((None, tc, W), lambda b, j: (b, rev(j), 0))
    rec = pl.BlockSpec((None, tc, W), lambda b, j: (b, rev(j), 1))
    prev = pl.BlockSpec((None, 8, W), lambda b, j: (b, jnp.maximum(rev(j) * (tc // 8) - 1, 0), 0))
    prev_rec = pl.BlockSpec((None, 8, W), lambda b, j: (b, jnp.maximum(rev(j) * (tc // 8) - 1, 0), 1))
    vec = pl.BlockSpec((1, W), lambda b, j: (0, 0))
    cws = pl.BlockSpec((CONV_WIDTH, W), lambda b, j: (0, 0))
    wsp = pl.BlockSpec((LRU_HEADS, HD, HD), lambda b, j: (0, 0, 0))
    vs = jax.ShapeDtypeStruct((1, W), F32)
    ws = jax.ShapeDtypeStruct((LRU_HEADS, HD, HD), F32)

    def rows(width):
        return pl.BlockSpec((None, tc, width), lambda b, j: (b, rev(j), 0))

    return pl.pallas_call(
        body, name="lru_bwd", grid=(B, nc),
        in_specs=[cur, rec, prev_rec, cur, prev, cur, cws, vec, wsp, vec, wsp, vec, vec, rows(MLA_Q_RANK), rows(MLA_KV_RANK), rows(HEAD_LANES)],
        out_specs=[rows(P), cws, vec, wsp, vec, wsp, vec, vec],
        out_shape=[jax.ShapeDtypeStruct((B, Tp, P), F32), jax.ShapeDtypeStruct((CONV_WIDTH, W), F32), vs, ws, vs, ws, vs, vs],
        scratch_shapes=[pltpu.VMEM((8, W), F32), pltpu.VMEM((8, W), F32), pltpu.VMEM((8, W), F32), pltpu.VMEM((tc, W), F32)],
        compiler_params=pltpu.CompilerParams(dimension_semantics=("arbitrary", "arbitrary")),
    )(p, p, p, hseq, hseq, dy, cw, cb, wa, ba, wx, bx, sp, dpq, dpkv, dkpe)


_Q_BLOCK = 2 * LRU_WIDTH // MLA_Q_RANK
_KV_BLOCK = (2 * LRU_WIDTH + MLA_Q_RANK) // MLA_KV_RANK
_KPE_START = 2 * LRU_WIDTH + MLA_Q_RANK + MLA_KV_RANK


@jax.custom_vjp
def even_front(p, cw, cb, wa, ba, wx, bx, sp, gq, gkv):
    return _even_front_fwd(p, cw, cb, wa, ba, wx, bx, sp, gq, gkv)[0]


def _even_front_fwd(p, cw, cb, wa, ba, wx, bx, sp, gq, gkv):
    B, Tp, W = p.shape
    p2d = p.reshape(B * Tp, W)
    y, hseq = _lru_fwd_call(p, cw, cb, wa, ba, wx, bx, sp)
    qn = _rms_fwd_call(p2d, gq, "q_norm_fwd", _Q_BLOCK)
    kvn = _rms_fwd_call(p2d, gkv, "kv_norm_fwd", _KV_BLOCK)
    kpe = jnp.pad(p[:, :, _KPE_START:], ((0, 0), (0, 0), (MLA_NOPE, HEAD_LANES - MLA_NOPE - MLA_ROPE)))
    return (y, qn, kvn, kpe), (p, hseq, cw, cb, wa, ba, wx, bx, sp, gq, gkv)


def _even_front_bwd(res, cts):
    p, hseq, cw, cb, wa, ba, wx, bx, sp, gq, gkv = res
    dy, dqn, dkvn, dkpe = cts
    B, Tp, W = p.shape
    p2d = p.reshape(B * Tp, W)
    dpq, dgq = _rms_bwd_call(p2d, gq, dqn, "q_norm_bwd", _Q_BLOCK)
    dpkv, dgkv = _rms_bwd_call(p2d, gkv, dkvn, "kv_norm_bwd", _KV_BLOCK)
    dp, dcw, dcb, dwa, dba, dwx, dbx, dsp = _lru_bwd_call(p, hseq, dy, cw, cb, wa, ba, wx, bx, sp, dpq.reshape(B, Tp, -1),
                                                          dpkv.reshape(B, Tp, -1), dkpe)
    return dp, dcw, dcb, dwa, dba, dwx, dbx, dsp, dgq.reshape(gq.shape), dgkv.reshape(gkv.shape)


even_front.defvjp(_even_front_fwd, _even_front_bwd)


def _rope_tables(pos, half):
    inv = ROPE_BASE ** (-jnp.arange(half, dtype=F32) / half)
    ang = pos.astype(F32)[:, None] * inv[None, :]
    return jnp.cos(ang), jnp.sin(ang)


_NT = (((1,), (1,)), ((), ()))
_TN = (((0,), (0,)), ((), ()))
HEAD_LANES = 128
_MLA_SCALE = (MLA_NOPE + MLA_ROPE) ** -0.5
_LOG2E = math.log2(math.e)


Q_BLOCK = 512


def _query_blocks(Tp):
    first = Tp % Q_BLOCK or Q_BLOCK
    return [(0, first)] + [(r, r + Q_BLOCK) for r in range(first, Tp, Q_BLOCK)]


def _mask_diagonal(s, fill):
    R, L = s.shape
    row = lax.broadcasted_iota(jnp.int32, (R, R), 0)
    col = lax.broadcasted_iota(jnp.int32, (R, R), 1)
    last = jnp.where(col <= row, s[:, L - R:], fill)
    return last if L == R else jnp.concatenate([s[:, :L - R], last], axis=1)


def _mla_rope_tables(pos):
    half = MLA_ROPE // 2
    cos, sin = _rope_tables(pos, half)
    T = pos.shape[0]
    ones, zeros = jnp.ones((T, MLA_NOPE), F32), jnp.zeros((T, MLA_NOPE), F32)
    tail1, tail0 = jnp.ones((T, HEAD_LANES - MLA_NOPE - MLA_ROPE), F32), jnp.zeros((T, HEAD_LANES - MLA_NOPE - MLA_ROPE), F32)
    zh = jnp.zeros((T, half), F32)
    c = jnp.concatenate([ones, cos, cos, tail1], axis=1)
    s_up = jnp.concatenate([zeros, -sin, zh, tail0], axis=1)
    s_down = jnp.concatenate([zeros, zh, sin, tail0], axis=1)
    return c, s_up, s_down


def _rope_lanes(x, c, s_up, s_down):
    half = MLA_ROPE // 2
    return x * c + pltpu.roll(x, HEAD_LANES - half, 1) * s_up + pltpu.roll(x, half, 1) * s_down


def _unrope_lanes(d, c, s_up, s_down):
    half = MLA_ROPE // 2
    return d * c + pltpu.roll(d * s_up, half, 1) + pltpu.roll(d * s_down, HEAD_LANES - half, 1)


def _mla_operands(q_ref, kv_ref, kpe_ref, c, s_up, s_down):
    lane = lax.broadcasted_iota(jnp.int32, kv_ref.shape, 1)
    qr = (_rope_lanes(q_ref[...].astype(F32), c, s_up, s_down) * (_MLA_SCALE * _LOG2E)).astype(MXU_DTYPE)
    kr = jnp.where(lane < MLA_NOPE, kv_ref[...].astype(F32), _rope_lanes(kpe_ref[...], c, s_up, s_down)).astype(MXU_DTYPE)
    return qr, kr, lane


def _mla_specs(Tp):
    head = pl.BlockSpec((None, Tp, HEAD_LANES), lambda b, h: (b, 0, h))
    shared = pl.BlockSpec((None, Tp, HEAD_LANES), lambda b, h: (b, 0, 0))
    tab = pl.BlockSpec((Tp, HEAD_LANES), lambda b, h: (0, 0))
    lse = pl.BlockSpec((None, None, Tp, 1), lambda b, h: (b, h, 0, 0))
    return head, shared, tab, lse


def _attn_fwd_call(q, kv, kpe, tabs):
    B, Tp, _ = q.shape

    def body(q_ref, kv_ref, kpe_ref, c_ref, su_ref, sd_ref, o_ref, lse_ref, qr_ref, kr_ref):
        qr, kr, lane = _mla_operands(q_ref, kv_ref, kpe_ref, c_ref[...], su_ref[...], sd_ref[...])
        qr_ref[...] = qr
        kr_ref[...] = kr
        for r0, L in _query_blocks(Tp):
            blk = slice(r0, L)
            s = _mask_diagonal(lax.dot_general(qr_ref[blk, :], kr_ref[0:L, :], _NT, preferred_element_type=F32), NEG_INF)
            m = jnp.max(s, axis=-1, keepdims=True)
            p = jnp.exp2(s - m)
            l = jnp.sum(p, axis=-1, keepdims=True)
            o = jnp.dot(p.astype(MXU_DTYPE), kv_ref[0:L, :].astype(MXU_DTYPE), preferred_element_type=F32)
            o_ref[blk, :] = jnp.where(lane[blk, :] >= MLA_NOPE, o / l, 0.0)
            lse_ref[blk, :] = m + jnp.log2(l)

    head, shared, tab, lse = _mla_specs(Tp)
    return pl.pallas_call(
        body, name="mla_attn_fwd", grid=(B, MLA_HEADS), in_specs=[head, head, shared, tab, tab, tab], out_specs=[head, lse],
        out_shape=[jax.ShapeDtypeStruct((B, Tp, MLA_HEADS * HEAD_LANES), F32), jax.ShapeDtypeStruct((B, MLA_HEADS, Tp, 1), F32)],
        scratch_shapes=[pltpu.VMEM((Tp, HEAD_LANES), MXU_DTYPE), pltpu.VMEM((Tp, HEAD_LANES), MXU_DTYPE)],
        compiler_params=pltpu.CompilerParams(dimension_semantics=("parallel", "parallel")),
    )(q, kv, kpe, *tabs)


def _attn_bwd_call(q, kv, kpe, tabs, o, lse, do):
    B, Tp, _ = q.shape

    def body(q_ref, kv_ref, kpe_ref, c_ref, su_ref, sd_ref, o_ref, lse_ref, do_ref, dq_ref, dkv_ref, dkpe_ref,
             qr_ref, kr_ref, dqa_ref, dka_ref, dva_ref):
        c, s_up, s_down = c_ref[...], su_ref[...], sd_ref[...]
        qr, kr, lane = _mla_operands(q_ref, kv_ref, kpe_ref, c, s_up, s_down)
        qr_ref[...] = qr
        kr_ref[...] = kr
        dka_ref[...] = jnp.zeros_like(dka_ref)
        dva_ref[...] = jnp.zeros_like(dva_ref)
        for r0, L in _query_blocks(Tp):
            blk = slice(r0, L)
            qb = qr_ref[blk, :]
            do = jnp.where(lane[blk, :] >= MLA_NOPE, do_ref[blk, :], 0.0)
            delta = jnp.sum(do * o_ref[blk, :], axis=-1, keepdims=True)
            s = _mask_diagonal(lax.dot_general(qb, kr_ref[0:L, :], _NT, preferred_element_type=F32), NEG_INF)
            p = jnp.exp2(s - lse_ref[blk, :])
            dob = do.astype(MXU_DTYPE)
            dva_ref[0:L, :] += lax.dot_general(p.astype(MXU_DTYPE), dob, _TN, preferred_element_type=F32)
            dp = lax.dot_general(dob, kv_ref[0:L, :].astype(MXU_DTYPE), _NT, preferred_element_type=F32)
            ds = (p * (dp - delta)).astype(MXU_DTYPE)
            dqa_ref[blk, :] = jnp.dot(ds, kr_ref[0:L, :], preferred_element_type=F32)
            dka_ref[0:L, :] += lax.dot_general(ds, qb, _TN, preferred_element_type=F32)
        dq_ref[...] = _unrope_lanes(dqa_ref[...] * _MLA_SCALE, c, s_up, s_down).astype(dq_ref.dtype)
        dk = dka_ref[...] * (1.0 / _LOG2E)
        dkv_ref[...] = jnp.where(lane < MLA_NOPE, dk, dva_ref[...]).astype(dkv_ref.dtype)
        dkpe = jnp.where(lane >= MLA_NOPE, _unrope_lanes(dk, c, s_up, s_down), 0.0)

        @pl.when(pl.program_id(1) == 0)
        def _():
            dkpe_ref[...] = dkpe

        @pl.when(pl.program_id(1) > 0)
        def _():
            dkpe_ref[...] += dkpe

    head, shared, tab, lse_spec = _mla_specs(Tp)
    wide = jax.ShapeDtypeStruct((B, Tp, MLA_HEADS * HEAD_LANES), q.dtype)
    acc = pltpu.VMEM((Tp, HEAD_LANES), F32)
    return pl.pallas_call(
        body, name="mla_attn_bwd", grid=(B, MLA_HEADS),
        in_specs=[head, head, shared, tab, tab, tab, head, lse_spec, head], out_specs=[head, head, shared],
        out_shape=[wide, wide, jax.ShapeDtypeStruct((B, Tp, HEAD_LANES), F32)],
        scratch_shapes=[pltpu.VMEM((Tp, HEAD_LANES), MXU_DTYPE), pltpu.VMEM((Tp, HEAD_LANES), MXU_DTYPE), acc, acc, acc],
        compiler_params=pltpu.CompilerParams(dimension_semantics=("parallel", "arbitrary")),
    )(q, kv, kpe, *tabs, o, lse, do)


@jax.custom_vjp
def mla_attention(q, kv, kpe, tabs):
    return _attn_fwd_call(q, kv, kpe, tabs)[0]


def _mla_attention_fwd(q, kv, kpe, tabs):
    o, lse = _attn_fwd_call(q, kv, kpe, tabs)
    return o, (q, kv, kpe, tabs, o, lse)


def _mla_attention_bwd(res, do):
    q, kv, kpe, tabs, o, lse = res
    dq, dkv, dkpe = _attn_bwd_call(q, kv, kpe, tabs, o, lse, do)
    return dq, dkv, dkpe, None


mla_attention.defvjp(_mla_attention_fwd, _mla_attention_bwd)


def _rope_halves(x, cos, sin):
    half = x.shape[1] // 2
    x1, x2 = x[:, :half], x[:, half:]
    return jnp.concatenate([x1 * cos - x2 * sin, x1 * sin + x2 * cos], axis=1)


def _unrope_halves(d, cos, sin):
    half = d.shape[1] // 2
    d1, d2 = d[:, :half], d[:, half:]
    return jnp.concatenate([d1 * cos + d2 * sin, d2 * cos - d1 * sin], axis=1)


_RET_K_SCALE = RET_QK_DIM ** -0.5
_RET_Q_BLOCKS = RET_HEADS
_RET_V_BLOCK0 = 2 * RET_HEADS * RET_QK_DIM // RET_V_DIM
_RET_G_BLOCK0 = _RET_V_BLOCK0 + RET_HEADS


def _ret_specs(Tp):
    q = pl.BlockSpec((None, Tp, RET_QK_DIM), lambda b, h: (b, 0, h))
    k = pl.BlockSpec((None, Tp, RET_QK_DIM), lambda b, h: (b, 0, _RET_Q_BLOCKS + h))
    v = pl.BlockSpec((None, Tp, RET_V_DIM), lambda b, h: (b, 0, _RET_V_BLOCK0 + h))
    tab = pl.BlockSpec((Tp, RET_QK_DIM // 2), lambda b, h: (0, 0))
    lg = pl.BlockSpec((None, 1, 1), lambda b, h: (h, 0, 0))
    return q, k, v, tab, lg


def _ret_operands(q_ref, k_ref, cos, sin, lg):
    t = lax.broadcasted_iota(jnp.int32, (q_ref.shape[0], 1), 0).astype(F32)
    grow, shrink = jnp.exp(-lg * t), jnp.exp(lg * t)
    qs = (_rope_halves(q_ref[...].astype(F32), cos, sin) * shrink).astype(MXU_DTYPE)
    ks = (_rope_halves(k_ref[...].astype(F32), cos, sin) * (grow * _RET_K_SCALE)).astype(MXU_DTYPE)
    return qs, ks, shrink, grow * _RET_K_SCALE


def _ret_core_fwd_call(p, cos, sin, lg):
    B, Tp, _ = p.shape

    def body(q_ref, k_ref, v_ref, cos_ref, sin_ref, lg_ref, o_ref, qs_ref, ks_ref):
        qs_ref[...], ks_ref[...], _, _ = _ret_operands(q_ref, k_ref, cos_ref[...], sin_ref[...], lg_ref[...])
        for r0, L in _query_blocks(Tp):
            blk = slice(r0, L)
            s = _mask_diagonal(lax.dot_general(qs_ref[blk, :], ks_ref[0:L, :], _NT, preferred_element_type=F32), 0.0)
            o_ref[blk, :] = jnp.dot(s.astype(MXU_DTYPE), v_ref[0:L, :].astype(MXU_DTYPE), preferred_element_type=F32)

    q, k, v, tab, lgs = _ret_specs(Tp)
    return pl.pallas_call(
        body, name="retention_fwd", grid=(B, RET_HEADS), in_specs=[q, k, v, tab, tab, lgs],
        out_specs=pl.BlockSpec((None, Tp, RET_V_DIM), lambda b, h: (b, 0, h)),
        out_shape=jax.ShapeDtypeStruct((B, Tp, RET_HEADS * RET_V_DIM), F32),
        scratch_shapes=[pltpu.VMEM((Tp, RET_QK_DIM), MXU_DTYPE), pltpu.VMEM((Tp, RET_QK_DIM), MXU_DTYPE)],
        compiler_params=pltpu.CompilerParams(dimension_semantics=("parallel", "parallel")),
    )(p, p, p, cos, sin, lg)


def _ret_core_bwd_call(p, do, cos, sin, lg):
    B, Tp, _ = p.shape

    def body(q_ref, k_ref, v_ref, do_ref, cos_ref, sin_ref, lg_ref, dq_ref, dk_ref, dv_ref, qs_ref, ks_ref, dqa_ref, dka_ref, dva_ref):
        cos_, sin_ = cos_ref[...], sin_ref[...]
        qs_ref[...], ks_ref[...], q_scale, k_scale = _ret_operands(q_ref, k_ref, cos_, sin_, lg_ref[...])
        dka_ref[...] = jnp.zeros_like(dka_ref)
        dva_ref[...] = jnp.zeros_like(dva_ref)
        for r0, L in _query_blocks(Tp):
            blk = slice(r0, L)
            qb = qs_ref[blk, :]
            dob = do_ref[blk, :].astype(MXU_DTYPE)
            s = _mask_diagonal(lax.dot_general(qb, ks_ref[0:L, :], _NT, preferred_element_type=F32), 0.0).astype(MXU_DTYPE)
            dva_ref[0:L, :] += lax.dot_general(s, dob, _TN, preferred_element_type=F32)
            ds = _mask_diagonal(lax.dot_general(dob, v_ref[0:L, :].astype(MXU_DTYPE), _NT, preferred_element_type=F32), 0.0).astype(MXU_DTYPE)
            dqa_ref[blk, :] = jnp.dot(ds, ks_ref[0:L, :], preferred_element_type=F32)
            dka_ref[0:L, :] += lax.dot_general(ds, qb, _TN, preferred_element_type=F32)
        dq_ref[...] = _unrope_halves(dqa_ref[...] * q_scale, cos_, sin_).astype(dq_ref.dtype)
        dk_ref[...] = _unrope_halves(dka_ref[...] * k_scale, cos_, sin_).astype(dk_ref.dtype)
        dv_ref[...] = dva_ref[...].astype(dv_ref.dtype)

    q, k, v, tab, lgs = _ret_specs(Tp)
    qk_out = pl.BlockSpec((None, Tp, RET_QK_DIM), lambda b, h: (b, 0, h))
    v_out = pl.BlockSpec((None, Tp, RET_V_DIM), lambda b, h: (b, 0, h))
    return pl.pallas_call(
        body, name="retention_bwd", grid=(B, RET_HEADS), in_specs=[q, k, v, v_out, tab, tab, lgs],
        out_specs=[qk_out, qk_out, v_out],
        out_shape=[jax.ShapeDtypeStruct((B, Tp, RET_HEADS * RET_QK_DIM), p.dtype), jax.ShapeDtypeStruct((B, Tp, RET_HEADS * RET_QK_DIM), p.dtype),
                   jax.ShapeDtypeStruct((B, Tp, RET_HEADS * RET_V_DIM), p.dtype)],
        scratch_shapes=[pltpu.VMEM((Tp, RET_QK_DIM), MXU_DTYPE), pltpu.VMEM((Tp, RET_QK_DIM), MXU_DTYPE),
                        pltpu.VMEM((Tp, RET_QK_DIM), F32), pltpu.VMEM((Tp, RET_QK_DIM), F32), pltpu.VMEM((Tp, RET_V_DIM), F32)],
        compiler_params=pltpu.CompilerParams(dimension_semantics=("parallel", "parallel")),
    )(p, p, p, do, cos, sin, lg)


def _ret_gate_specs(M):
    tm = _pick(M, 1088, 8)
    head = pl.BlockSpec((tm, RET_V_DIM), lambda i, h: (i, h))
    gate = pl.BlockSpec((tm, RET_V_DIM), lambda i, h: (i, _RET_G_BLOCK0 + h))
    return tm, head, gate


def _ret_gate_fwd_call(o, p2d):
    M = o.shape[0]
    tm, head, gate = _ret_gate_specs(M)

    def body(o_ref, g_ref, y_ref):
        ov = o_ref[...]
        gv = g_ref[...].astype(F32)
        rstd = lax.rsqrt(jnp.mean(ov * ov, axis=-1, keepdims=True) + EPS)
        y_ref[...] = (gv * _sigmoid(gv)) * (ov * rstd)

    return pl.pallas_call(
        body, name="retention_gate_fwd", grid=(M // tm, RET_HEADS), in_specs=[head, gate], out_specs=head,
        out_shape=jax.ShapeDtypeStruct(o.shape, F32),
        compiler_params=pltpu.CompilerParams(dimension_semantics=("parallel", "parallel")),
    )(o, p2d)


def _ret_gate_bwd_call(o, p2d, dy):
    M = o.shape[0]
    tm, head, gate = _ret_gate_specs(M)

    def body(o_ref, g_ref, dy_ref, do_ref, dg_ref):
        ov = o_ref[...]
        gv = g_ref[...].astype(F32)
        dy = dy_ref[...]
        rstd = lax.rsqrt(jnp.mean(ov * ov, axis=-1, keepdims=True) + EPS)
        on = ov * rstd
        sg = _sigmoid(gv)
        dg_ref[...] = (dy * on * (sg * (1.0 + gv * (1.0 - sg)))).astype(dg_ref.dtype)
        don = dy * (gv * sg)
        do_ref[...] = (rstd * (don - on * jnp.mean(don * on, axis=-1, keepdims=True))).astype(do_ref.dtype)

    shp = jax.ShapeDtypeStruct(o.shape, p2d.dtype)
    return pl.pallas_call(
        body, name="retention_gate_bwd", grid=(M // tm, RET_HEADS), in_specs=[head, gate, head], out_specs=[head, head],
        out_shape=[shp, shp],
        compiler_params=pltpu.CompilerParams(dimension_semantics=("parallel", "parallel")),
    )(o, p2d, dy)


def _log_gamma():
    return jnp.log(1.0 - 2.0 ** (-5.0 - jnp.arange(RET_HEADS, dtype=F32))).reshape(RET_HEADS, 1, 1)


@jax.custom_vjp
def retention_mixer(p, cos, sin):
    B, Tp, W = p.shape
    o = _ret_core_fwd_call(p, cos, sin, _log_gamma())
    return _ret_gate_fwd_call(o.reshape(B * Tp, -1), p.reshape(B * Tp, W))


def _retention_mixer_fwd(p, cos, sin):
    B, Tp, W = p.shape
    o = _ret_core_fwd_call(p, cos, sin, _log_gamma())
    return _ret_gate_fwd_call(o.reshape(B * Tp, -1), p.reshape(B * Tp, W)), (p, o, cos, sin)


def _retention_mixer_bwd(res, dy):
    p, o, cos, sin = res
    B, Tp, W = p.shape
    do, dg = _ret_gate_bwd_call(o.reshape(B * Tp, -1), p.reshape(B * Tp, W), dy)
    dq, dk, dv = _ret_core_bwd_call(p, do.reshape(B, Tp, -1), cos, sin, _log_gamma())
    return jnp.concatenate([dq, dk, dv, dg.reshape(B, Tp, -1)], axis=-1), None, None


retention_mixer.defvjp(_retention_mixer_fwd, _retention_mixer_bwd)


def _heads_to_lanes(w):
    K = w.shape[0]
    w = w.reshape(K, MLA_HEADS, MLA_NOPE + MLA_ROPE)
    return jnp.pad(w, ((0, 0), (0, 0), (0, HEAD_LANES - MLA_NOPE - MLA_ROPE))).reshape(K, MLA_HEADS * HEAD_LANES)


def _out_rows_to_lanes(w):
    N = w.shape[1]
    att = w[LRU_WIDTH:].reshape(MLA_HEADS, MLA_V, N)
    att = jnp.pad(att, ((0, 0), (HEAD_LANES - MLA_V, 0), (0, 0))).reshape(MLA_HEADS * HEAD_LANES, N)
    return jnp.concatenate([w[:LRU_WIDTH], att], axis=0)


def _seq_dims(x):
    B, S, D = x.shape
    T = S + N_META
    Tp = _round_up(T, SEQ_BLOCK)
    return B, S, T, Tp


def _mixer0(diff, w, token):
    x = diff["x"]
    B, S, T, Tp = _seq_dims(x)
    D = x.shape[-1]
    M = B * Tp
    pos = jnp.arange(Tp, dtype=jnp.int32)

    def mm(a, name, act=False, out_dtype=F32, layout=lambda m: m, col_shards=1):
        return matmul(a, layout(w[name]), layout(diff[name]), act, name, out_dtype, col_shards)

    meta = jnp.broadcast_to(diff["meta_tokens"][None], (B, N_META, D))
    h = jnp.concatenate([meta, x + token, jnp.zeros((B, Tp - T, D), F32)], axis=1).reshape(M, D)
    p = mm(h, "ev_w_in")
    sp = jax.nn.softplus(-diff["ev_lru_lambda"]).reshape(1, LRU_WIDTH)
    y_rec, qn, kvn, kpe = even_front(
        p.reshape(B, Tp, -1), diff["ev_conv_w"].reshape(CONV_WIDTH, LRU_WIDTH), diff["ev_conv_b"].reshape(1, LRU_WIDTH),
        diff["ev_w_rg_a"].reshape(LRU_HEADS, LRU_HEAD_DIM, LRU_HEAD_DIM), diff["ev_b_rg_a"].reshape(1, LRU_WIDTH),
        diff["ev_w_rg_x"].reshape(LRU_HEADS, LRU_HEAD_DIM, LRU_HEAD_DIM), diff["ev_b_rg_x"].reshape(1, LRU_WIDTH),
        sp, diff["ev_q_norm_g"].reshape(-1), diff["ev_kv_norm_g"].reshape(-1))
    y_rec = y_rec.reshape(M, LRU_WIDTH)
    q = mm(qn, "ev_w_uq", out_dtype=MXU_DTYPE, layout=_heads_to_lanes).reshape(B, Tp, -1)
    kv = mm(kvn, "ev_w_ukv", out_dtype=MXU_DTYPE).reshape(B, Tp, -1)
    y_att = mla_attention(q, kv, kpe, _mla_rope_tables(pos)).reshape(M, -1)
    mix = mm(jnp.concatenate([y_rec, y_att], axis=-1), "ev_w_out", layout=_out_rows_to_lanes)
    return deepnorm(h, mix, diff["ln_mix_g"], diff["ln_mix_b"], "ln_mix0")


def _mlp0(diff, h, w):
    f = mlp(h, w["mlp_w1_0"], w["mlp_w2_0"], diff["mlp_w1_0"], diff["mlp_w2_0"], "mlp0")
    return deepnorm(h, f, diff["ln_mlp_g"], diff["ln_mlp_b"], "ln_mlp0")


def _layer1_loss(diff, h, w, tgt):
    B, S, T, Tp = _seq_dims(tgt)
    D = tgt.shape[-1]
    pos = jnp.arange(Tp, dtype=jnp.int32)

    def mm(a, name, out_dtype=F32, col_shards=1):
        return matmul(a, w[name], diff[name], False, name, out_dtype, col_shards)

    p = mm(h, "od_w_in", out_dtype=MXU_DTYPE, col_shards=N_CHIPS)
    cos, sin = _rope_tables(pos, RET_QK_DIM // 2)
    mix = mm(retention_mixer(p.reshape(B, Tp, -1), cos, sin), "od_w_out")
    h = deepnorm(h, mix, diff["ln_mix_g"], diff["ln_mix_b"], "ln_mix1")
    f = mlp(h, w["mlp_w1_1"], w["mlp_w2_1"], diff["mlp_w1_1"], diff["mlp_w2_1"], "mlp1")
    h = deepnorm(h, f, diff["ln_mlp_g"], diff["ln_mlp_b"], "ln_mlp1")
    return loss_head(h.reshape(B, Tp, D), jnp.pad(tgt, ((0, 0), (N_META, Tp - T), (0, 0))), S)


_HBM = pl.BlockSpec(memory_space=pltpu.HBM)


def _place():
    return lax.axis_index("x"), lax.axis_index("y"), lax.axis_index("c")


def _other_chips(x, y):
    return [(1 - x, y), (x, 1 - y), (1 - x, 1 - y)]


def _chunks(rows, sublanes, most):
    for q in range(most, 0, -1):
        if rows % (q * sublanes) == 0:
            return q
    return 1


def _sublanes(dtype):
    return 8 * 4 // jnp.dtype(dtype).itemsize


def _gather_pieces(bufs):
    plan, first = [], []
    for b in bufs:
        Rh = b.shape[0] // 2
        Q = _chunks(Rh, _sublanes(b.dtype), 4) if Rh * b.shape[1] * b.dtype.itemsize > (1 << 20) else 1
        first.append(3 * sum(q for _, q, _ in plan))
        plan.append((Rh, Q, Rh // Q))
    return plan, first, 3 * sum(q for _, q, _ in plan)


def _allgather_chips(bufs, name):
    n = len(bufs)
    plan, first, n_sems = _gather_pieces(bufs)

    def body(*refs):
        x_refs, out_refs, (send_sems, recv_sems) = refs[:n], refs[n:2 * n], refs[2 * n:]
        x, y, c = _place()
        sibling = (x, y, 1 - c)
        chips = _other_chips(x, y)

        def copy(k, src, dst, to):
            return pltpu.make_async_remote_copy(src_ref=src, dst_ref=dst, send_sem=send_sems.at[k], recv_sem=recv_sems.at[k],
                                                device_id=to, device_id_type=MESH)

        def piece(i, cx, cy, hc, q):
            Rh, _, ch = plan[i]
            return out_refs[i].at[2 * cx + cy, pl.ds(hc * Rh + q * ch, ch), :]

        slots = [(i, q, j) for i in range(n) for q in range(plan[i][1]) for j in range(3)]
        sem = {(i, q, j): first[i] + 3 * q + j for i, q, j in slots}
        sent = [copy(sem[i, q, j], x_refs[i].at[pl.ds(c * plan[i][0] + q * plan[i][2], plan[i][2]), :], piece(i, x, y, c, q), (*chips[j], c))
                for i, q, j in slots]
        for cp in sent:
            cp.start()
        passed = []
        for i, q, j in slots:
            landed = piece(i, *chips[j], c, q)
            copy(sem[i, q, j], landed, landed, sibling).wait_recv()
            fwd = copy(n_sems + sem[i, q, j], landed, landed, sibling)
            fwd.start()
            passed.append(fwd)
        for i, q, j in slots:
            theirs = piece(i, *chips[j], 1 - c, q)
            copy(n_sems + sem[i, q, j], theirs, theirs, sibling).wait_recv()
        for cp in sent + passed:
            cp.wait_send()

    return pl.pallas_call(
        body, name=name, in_specs=[_HBM] * n, out_specs=[_HBM] * n,
        out_shape=[jax.ShapeDtypeStruct((N_CHIPS,) + b.shape, b.dtype) for b in bufs],
        scratch_shapes=[pltpu.SemaphoreType.DMA((2 * n_sems,)), pltpu.SemaphoreType.DMA((2 * n_sems,))],
    )(*bufs)


def _with_own(gathered, own):
    my = 2 * lax.axis_index("x") + lax.axis_index("y")
    return lax.dynamic_update_slice(gathered, own[None], (my, 0, 0))


def _sibling_exchange(ps, name):
    n = len(ps)

    def body(*refs):
        p_refs, out_refs, (send_sems, recv_sems) = refs[:n], refs[n:2 * n], refs[2 * n:]
        x, y, c = _place()
        copies = [pltpu.make_async_remote_copy(src_ref=p_ref.at[j, 1 - c], dst_ref=out_ref.at[j], send_sem=send_sems.at[N_CHIPS * i + j],
                                               recv_sem=recv_sems.at[N_CHIPS * i + j], device_id=(x, y, 1 - c), device_id_type=MESH)
                  for i, (p_ref, out_ref) in enumerate(zip(p_refs, out_refs)) for j in range(N_CHIPS)]
        for cp in copies:
            cp.start()
        for cp in copies:
            cp.wait()

    return pl.pallas_call(
        body, name=name, in_specs=[_HBM] * n, out_specs=[_HBM] * n,
        out_shape=[jax.ShapeDtypeStruct((N_CHIPS,) + p.shape[2:], p.dtype) for p in ps],
        scratch_shapes=[pltpu.SemaphoreType.DMA((N_CHIPS * n,)), pltpu.SemaphoreType.DMA((N_CHIPS * n,))],
    )(*ps)


def _chip_scatter(ss, name):
    n = len(ss)

    def body(*refs):
        s_refs, t_refs, (send_sems, recv_sems) = refs[:n], refs[n:2 * n], refs[2 * n:]
        x, y, c = _place()
        copies = [pltpu.make_async_remote_copy(src_ref=s_ref.at[j + 1], dst_ref=t_ref.at[j], send_sem=send_sems.at[3 * i + j],
                                               recv_sem=recv_sems.at[3 * i + j], device_id=(cx, cy, c), device_id_type=MESH)
                  for i, (s_ref, t_ref) in enumerate(zip(s_refs, t_refs)) for j, (cx, cy) in enumerate(_other_chips(x, y))]
        for cp in copies:
            cp.start()
        for cp in copies:
            cp.wait()

    return pl.pallas_call(
        body, name=name, in_specs=[_HBM] * n, out_specs=[_HBM] * n,
        out_shape=[jax.ShapeDtypeStruct((3,) + s.shape[1:], s.dtype) for s in ss],
        scratch_shapes=[pltpu.SemaphoreType.DMA((3 * n,)), pltpu.SemaphoreType.DMA((3 * n,))],
    )(*ss)


def _sibling_gather(fs, name):
    n = len(fs)

    def body(*refs):
        out_refs, (send_sems, recv_sems) = refs[n:2 * n], refs[2 * n:]
        x, y, c = _place()
        copies = [pltpu.make_async_remote_copy(src_ref=out_ref.at[c], dst_ref=out_ref.at[c], send_sem=send_sems.at[i], recv_sem=recv_sems.at[i],
                                               device_id=(x, y, 1 - c), device_id_type=MESH) for i, out_ref in enumerate(out_refs)]
        for cp in copies:
            cp.start()
        for cp in copies:
            cp.wait()

    return pl.pallas_call(
        body, name=name, in_specs=[_HBM] * n, out_specs=[_HBM] * n,
        out_shape=[jax.ShapeDtypeStruct(f.shape, f.dtype) for f in fs], input_output_aliases={i: i for i in range(n)},
        scratch_shapes=[pltpu.SemaphoreType.DMA((n,)), pltpu.SemaphoreType.DMA((n,))],
    )(*fs)


def _axis_scalar(name):
    return lax.axis_index(name).astype(jnp.int32).reshape(1)


def _add_own_half(p, got, out_dtype, name):
    n, _, R, C = p.shape
    tr = _pick(R, 512, 16)

    def body(x_ref, y_ref, c_ref, p_ref, g_ref, o_ref):
        o_ref[...] = (p_ref[...] + g_ref[...]).astype(out_dtype)

    def chip(r, x_ref, y_ref):
        return 2 * (x_ref[0] ^ (r & 1)) + (y_ref[0] ^ (r >> 1))

    grid_spec = pltpu.PrefetchScalarGridSpec(
        num_scalar_prefetch=3, grid=(n, R // tr),
        in_specs=[pl.BlockSpec((None, None, tr, C), lambda r, i, x_ref, y_ref, c_ref: (chip(r, x_ref, y_ref), c_ref[0], i, 0)),
                  pl.BlockSpec((None, tr, C), lambda r, i, x_ref, y_ref, c_ref: (chip(r, x_ref, y_ref), i, 0))],
        out_specs=pl.BlockSpec((None, tr, C), lambda r, i, x_ref, y_ref, c_ref: (r, i, 0)))
    return pl.pallas_call(body, name=name, grid_spec=grid_spec, out_shape=jax.ShapeDtypeStruct((n, R, C), out_dtype),
                          compiler_params=pltpu.CompilerParams(dimension_semantics=("parallel", "parallel")))(
        _axis_scalar("x"), _axis_scalar("y"), _axis_scalar("c"), p, got)


def _sum_partials(s, t, name):
    _, R, C = s.shape
    tr = _pick(R, 512, 16)

    def body(c_ref, s_ref, t_ref, o_ref):
        acc = s_ref[...].astype(F32)
        for j in range(3):
            acc = acc + t_ref[j].astype(F32)
        o_ref[...] = acc

    grid_spec = pltpu.PrefetchScalarGridSpec(
        num_scalar_prefetch=1, grid=(R // tr,),
        in_specs=[pl.BlockSpec((None, tr, C), lambda i, c_ref: (0, i, 0)), pl.BlockSpec((3, tr, C), lambda i, c_ref: (0, i, 0))],
        out_specs=pl.BlockSpec((None, tr, C), lambda i, c_ref: (c_ref[0], i, 0)))
    return pl.pallas_call(body, name=name, grid_spec=grid_spec, out_shape=jax.ShapeDtypeStruct((2, R, C), F32),
                          compiler_params=pltpu.CompilerParams(dimension_semantics=("parallel",)))(_axis_scalar("c"), s, t)


def _sibling_reduce(ps, wire_dtypes, tag):
    got = _sibling_exchange(ps, "grad_sibling_exchange_" + tag)
    return [_add_own_half(p, g, dt, "grad_sibling_add_%s%d" % (tag, i)) for i, (p, g, dt) in enumerate(zip(ps, got, wire_dtypes))]


_SEM = pl.BlockSpec(memory_space=pltpu.SEMAPHORE)
_ANY = pl.BlockSpec(memory_space=pl.ANY)
_EFFECT = pltpu.SideEffectType.DATAFLOW_SIDE_EFFECTING


def _in_hbm(a):
    return pltpu.with_memory_space_constraint(a, pltpu.HBM)


def _half_copies(x_refs, land_refs, send_sems, recv_sems, arriving):
    x, y, c = _place()
    copies = []
    for i, (x_ref, land_ref) in enumerate(zip(x_refs, land_refs)):
        Rh = x_ref.shape[0] // 2
        rows = pl.ds(c * Rh, Rh)
        for j, (cx, cy) in enumerate(_other_chips(x, y)):
            copies.append(pltpu.make_async_remote_copy(
                src_ref=x_ref.at[rows, :], dst_ref=land_ref.at[2 * cx + cy if arriving else 2 * x + y, rows, :],
                send_sem=send_sems.at[3 * i + j], recv_sem=recv_sems.at[3 * i + j], device_id=(cx, cy, c), device_id_type=MESH))
    return copies


def _allgather_start(bufs, name):
    n = len(bufs)

    def body(*refs):
        x_refs, land_refs, (send_sems, recv_sems), token = refs[:n], refs[n:2 * n], refs[2 * n:2 * n + 2], refs[-1]
        for cp in _half_copies(x_refs, land_refs, send_sems, recv_sems, False):
            cp.start()
        token[...] = jnp.zeros_like(token)

    lands = [lax.empty((N_CHIPS,) + b.shape, b.dtype) for b in bufs]
    out = pl.pallas_call(
        body, name=name,
        out_shape=(pltpu.SemaphoreType.DMA((3 * n,)), pltpu.SemaphoreType.DMA((3 * n,)), *[pltpu.HBM(a.shape, a.dtype) for a in bufs + lands],
                   jax.ShapeDtypeStruct((8, 128), F32)),
        in_specs=[_HBM] * (2 * n), out_specs=(_SEM, _SEM, *[_HBM] * (2 * n), pl.BlockSpec(memory_space=pltpu.VMEM)),
        input_output_aliases={i: 2 + i for i in range(2 * n)}, compiler_params=pltpu.CompilerParams(has_side_effects=_EFFECT),
    )(*[_in_hbm(a) for a in bufs + lands])
    return (out[0], out[1], list(out[2:2 + n]), list(out[2 + n:2 + 2 * n])), out[-1][0, 0]


def _allgather_wait(pending, after, name):
    send_sems, recv_sems, bufs, lands = pending
    n = len(bufs)

    def body(*refs):
        x_refs, land_refs, send_sems, recv_sems = refs[:n], refs[n:2 * n], refs[2 * n], refs[2 * n + 1]
        for cp in _half_copies(x_refs, land_refs, send_sems, recv_sems, False):
            cp.wait_send()
        for cp in _half_copies(x_refs, land_refs, send_sems, recv_sems, True):
            cp.wait_recv()

    out = pl.pallas_call(
        body, name=name, out_shape=tuple(pltpu.HBM(a.shape, a.dtype) for a in bufs + lands),
        in_specs=[_HBM] * (2 * n) + [_SEM, _SEM, _ANY], out_specs=tuple([_HBM] * (2 * n)), input_output_aliases={i: i for i in range(2 * n)},
        compiler_params=pltpu.CompilerParams(has_side_effects=_EFFECT),
    )(*bufs, *lands, send_sems, recv_sems, after)
    return list(out[n:])


def _sibling_forward(lands, name):
    n = len(lands)
    plan, first, n_sems = _gather_pieces([jax.ShapeDtypeStruct(l.shape[1:], l.dtype) for l in lands])

    def body(*refs):
        out_refs, (send_sems, recv_sems) = refs[n:2 * n], refs[2 * n:]
        x, y, c = _place()

        def copies(hc):
            return [pltpu.make_async_remote_copy(
                        src_ref=out_refs[i].at[2 * cx + cy, pl.ds(hc * plan[i][0] + q * plan[i][2], plan[i][2]), :],
                        dst_ref=out_refs[i].at[2 * cx + cy, pl.ds(hc * plan[i][0] + q * plan[i][2], plan[i][2]), :],
                        send_sem=send_sems.at[first[i] + 3 * q + j], recv_sem=recv_sems.at[first[i] + 3 * q + j],
                        device_id=(x, y, 1 - c), device_id_type=MESH)
                    for i in range(n) for q in range(plan[i][1]) for j, (cx, cy) in enumerate(_other_chips(x, y))]

        mine = copies(c)
        for cp in mine:
            cp.start()
        for cp in mine:
            cp.wait_send()
        for cp in copies(1 - c):
            cp.wait_recv()

    return pl.pallas_call(
        body, name=name, in_specs=[_HBM] * n, out_specs=[_HBM] * n, out_shape=[jax.ShapeDtypeStruct(l.shape, l.dtype) for l in lands],
        input_output_aliases={i: i for i in range(n)},
        scratch_shapes=[pltpu.SemaphoreType.DMA((n_sems,)), pltpu.SemaphoreType.DMA((n_sems,))],
    )(*lands)


N_PEERS = 7


def _direct_copies(p_refs, t_refs, send_sems, recv_sems):
    x, y, c = _place()
    copies = []
    for i, (p_ref, t_ref) in enumerate(zip(p_refs, t_refs)):
        for f in range(1, N_PEERS + 1):
            px, py, pc = x ^ (f >> 2), y ^ ((f >> 1) & 1), c ^ (f & 1)
            copies.append(pltpu.make_async_remote_copy(
                src_ref=p_ref.at[2 * px + py, pc], dst_ref=t_ref.at[f - 1], send_sem=send_sems.at[N_PEERS * i + f - 1],
                recv_sem=recv_sems.at[N_PEERS * i + f - 1], device_id=(px, py, pc), device_id_type=MESH))
    return copies


def _direct_scatter_start(ps, name):
    n = len(ps)

    def body(*refs):
        p_refs, t_refs, (send_sems, recv_sems), token = refs[:n], refs[n:2 * n], refs[2 * n:2 * n + 2], refs[-1]
        for cp in _direct_copies(p_refs, t_refs, send_sems, recv_sems):
            cp.start()
        token[...] = jnp.zeros_like(token)

    lands = [lax.empty((N_PEERS,) + p.shape[2:], p.dtype) for p in ps]
    out = pl.pallas_call(
        body, name=name,
        out_shape=(pltpu.SemaphoreType.DMA((N_PEERS * n,)), pltpu.SemaphoreType.DMA((N_PEERS * n,)),
                   *[pltpu.HBM(a.shape, a.dtype) for a in ps + lands], jax.ShapeDtypeStruct((8, 128), F32)),
        in_specs=[_HBM] * (2 * n), out_specs=(_SEM, _SEM, *[_HBM] * (2 * n), pl.BlockSpec(memory_space=pltpu.VMEM)),
        input_output_aliases={i: 2 + i for i in range(2 * n)}, compiler_params=pltpu.CompilerParams(has_side_effects=_EFFECT),
    )(*[_in_hbm(a) for a in ps + lands])
    return (out[0], out[1], list(out[2:2 + n]), list(out[2 + n:2 + 2 * n])), out[-1][0, 0]


def _direct_scatter_wait(pending, after, name):
    send_sems, recv_sems, ps, lands = pending
    n = len(ps)

    def body(*refs):
        p_refs, t_refs, send_sems, recv_sems = refs[:n], refs[n:2 * n], refs[2 * n], refs[2 * n + 1]
        for cp in _direct_copies(p_refs, t_refs, send_sems, recv_sems):
            cp.wait_send()
            cp.wait_recv()

    out = pl.pallas_call(
        body, name=name, out_shape=tuple(pltpu.HBM(a.shape, a.dtype) for a in ps + lands),
        in_specs=[_HBM] * (2 * n) + [_SEM, _SEM, _ANY], out_specs=tuple([_HBM] * (2 * n)),
        input_output_aliases={i: i for i in range(2 * n)}, compiler_params=pltpu.CompilerParams(has_side_effects=_EFFECT),
    )(*ps, *lands, send_sems, recv_sems, after)
    return list(out[:n]), list(out[n:])


def _sum_direct(p, t, name):
    _, _, R, C = p.shape
    tr = _pick(R, 512, 16)

    def body(x_ref, y_ref, c_ref, p_ref, t_ref, o_ref):
        acc = p_ref[...].astype(F32)
        for f in range(N_PEERS):
            acc = acc + t_ref[f].astype(F32)
        o_ref[...] = acc

    grid_spec = pltpu.PrefetchScalarGridSpec(
        num_scalar_prefetch=3, grid=(R // tr,),
        in_specs=[pl.BlockSpec((None, None, tr, C), lambda i, x_ref, y_ref, c_ref: (2 * x_ref[0] + y_ref[0], c_ref[0], i, 0)),
                  pl.BlockSpec((N_PEERS, tr, C), lambda i, x_ref, y_ref, c_ref: (0, i, 0))],
        out_specs=pl.BlockSpec((None, tr, C), lambda i, x_ref, y_ref, c_ref: (c_ref[0], i, 0)))
    return pl.pallas_call(body, name=name, grid_spec=grid_spec, out_shape=jax.ShapeDtypeStruct((2, R, C), F32),
                          compiler_params=pltpu.CompilerParams(dimension_semantics=("parallel",)))(
        _axis_scalar("x"), _axis_scalar("y"), _axis_scalar("c"), p, t)


def _adamw(w, g, m, v, name):
    R, C = w.shape
    tr = _pick(R, 256, 8)

    def body(w_ref, g_ref, m_ref, v_ref, d_ref, nm_ref, nv_ref):
        g_ = g_ref[...]
        m_ = ADAM_B1 * m_ref[...] + (1.0 - ADAM_B1) * g_
        v_ = ADAM_B2 * v_ref[...] + (1.0 - ADAM_B2) * (g_ * g_)
        m_hat = m_ / (1.0 - ADAM_B1 ** ADAM_STEP)
        v_hat = v_ / (1.0 - ADAM_B2 ** ADAM_STEP)
        d_ref[...] = -ADAM_LR * (m_hat / (jnp.sqrt(v_hat) + ADAM_EPS) + ADAM_WD * w_ref[...])
        nm_ref[...] = m_
        nv_ref[...] = v_

    row = pl.BlockSpec((tr, C), lambda i: (i, 0))
    shp = jax.ShapeDtypeStruct((R, C), F32)
    return pl.pallas_call(body, name=name, grid=(R // tr,), in_specs=[row] * 4, out_specs=[row] * 3, out_shape=[shp] * 3,
                          compiler_params=pltpu.CompilerParams(dimension_semantics=("parallel",)))(w, g, m, v)


BIG_SPECS = (("ev_w_in", 1024, 1440, 1), ("ev_w_uq", 256, 768, 1), ("ev_w_ukv", 128, 1024, 1), ("ev_w_out", 1024, 1024, 0),
             ("od_w_in", 1024, 6144, 1), ("od_w_out", 2048, 1024, 0), ("mlp_w1_0", 1024, 4096, 1), ("mlp_w1_1", 1024, 4096, 1),
             ("mlp_w2_0", 4096, 1024, 0), ("mlp_w2_1", 4096, 1024, 0))
BIG_PARAMS = (("ev_w_in", ("ev_w_in",)), ("ev_w_uq", ("ev_w_uq",)), ("ev_w_ukv", ("ev_w_ukv",)), ("ev_w_out", ("ev_w_out",)),
              ("od_w_in", ("od_w_in",)), ("od_w_out", ("od_w_out",)), ("mlp_w1", ("mlp_w1_0", "mlp_w1_1")),
              ("mlp_w2", ("mlp_w2_0", "mlp_w2_1")))
REPLICATED = ("ev_conv_b", "ev_w_rg_a", "ev_b_rg_a", "ev_w_rg_x", "ev_b_rg_x", "ev_lru_lambda", "ev_q_norm_g", "ev_kv_norm_g",
              "ln_mix_g", "ln_mix_b", "ln_mlp_g", "ln_mlp_b")
SMALL_SHARDED = ("meta_tokens", "ev_conv_w")
COL_SHARD_GRADS = ("od_w_in", "mlp_w1_0", "mlp_w1_1")
MATRIX_GROUPS = (("ev_w_in", "ev_w_uq", "ev_w_ukv", "ev_w_out"), ("mlp_w1_0", "mlp_w2_0"), ("od_w_in", "od_w_out", "mlp_w1_1", "mlp_w2_1"))
LAYER_NORMS = ("ln_mix_g", "ln_mix_b", "ln_mlp_g", "ln_mlp_b")
WEIGHT_NAMES = ("meta_tokens", "ev_w_in", "ev_conv_w", "ev_conv_b", "ev_w_rg_a", "ev_b_rg_a", "ev_w_rg_x", "ev_b_rg_x",
                "ev_lru_lambda", "ev_q_norm_g", "ev_w_uq", "ev_kv_norm_g", "ev_w_ukv", "ev_w_out", "od_w_in", "od_w_out",
                "ln_mix_g", "ln_mix_b", "mlp_w1", "mlp_w2", "ln_mlp_g", "ln_mlp_b")


def _to_rows(flat, row_align):
    n = flat.shape[-1]
    rows = _round_up(-(-n // PACK_COLS), row_align)
    pad = rows * PACK_COLS - n
    if pad:
        flat = jnp.pad(flat, [(0, 0)] * (flat.ndim - 1) + [(0, pad)])
    return flat.reshape(flat.shape[:-1] + (rows, PACK_COLS))


def _shard_shape(K, N, axis):
    return (K // N_CHIPS, N) if axis == 0 else (K, N // N_CHIPS)


def _gather_shards(stacked, K, N, axis):
    if axis == 0:
        return stacked.reshape(K, N)
    return stacked.transpose(1, 0, 2).reshape(K, N)


def _split_shards(full, K, N, axis):
    if axis == 0:
        return full.reshape(N_CHIPS, -1)
    return full.reshape(K, N_CHIPS, N // N_CHIPS).transpose(1, 0, 2).reshape(N_CHIPS, -1)


def kernel(x, meta_tokens, ev_w_in, ev_conv_w, ev_conv_b, ev_w_rg_a, ev_b_rg_a, ev_w_rg_x, ev_b_rg_x, ev_lru_lambda, ev_q_norm_g, ev_w_uq, ev_kv_norm_g, ev_w_ukv, ev_w_out, od_w_in, od_w_out, ln_mix_g, ln_mix_b, mlp_w1, mlp_w2, ln_mlp_g, ln_mlp_b, loss_target, m_meta_tokens, m_ev_w_in, m_ev_conv_w, m_ev_conv_b, m_ev_w_rg_a, m_ev_b_rg_a, m_ev_w_rg_x, m_ev_b_rg_x, m_ev_lru_lambda, m_ev_q_norm_g, m_ev_w_uq, m_ev_kv_norm_g, m_ev_w_ukv, m_ev_w_out, m_od_w_in, m_od_w_out, m_ln_mix_g, m_ln_mix_b, m_mlp_w1, m_mlp_w2, m_ln_mlp_g, m_ln_mlp_b, v_meta_tokens, v_ev_w_in, v_ev_conv_w, v_ev_conv_b, v_ev_w_rg_a, v_ev_b_rg_a, v_ev_w_rg_x, v_ev_b_rg_x, v_ev_lru_lambda, v_ev_q_norm_g, v_ev_w_uq, v_ev_kv_norm_g, v_ev_w_ukv, v_ev_w_out, v_od_w_in, v_od_w_out, v_ln_mix_g, v_ln_mix_b, v_mlp_w1, v_mlp_w2, v_ln_mlp_g, v_ln_mlp_b):
    given = dict(locals())
    local_big = {"ev_w_in": ev_w_in[0], "ev_w_uq": ev_w_uq[0], "ev_w_ukv": ev_w_ukv[0], "ev_w_out": ev_w_out[0],
                 "od_w_in": od_w_in[0], "od_w_out": od_w_out[0], "mlp_w1_0": mlp_w1[0], "mlp_w1_1": mlp_w1[1],
                 "mlp_w2_0": mlp_w2[0], "mlp_w2_1": mlp_w2[1]}

    specs = {spec[0]: spec for spec in BIG_SPECS}
    mixer0_m, mlp0_m, layer1_m = MATRIX_GROUPS

    def shards(names):
        return [local_big[n].astype(MXU_DTYPE) for n in names]

    def whole(stacked, n):
        _, K, N, ax = specs[n]
        return stacked if n in COL_SHARD_GRADS else _gather_shards(stacked, K, N, ax)

    def filled(gathered, own, names):
        return {n: whole(_with_own(g_, o_), n) for n, g_, o_ in zip(names, gathered, own)}

    own_a, own_b, own_c = shards(mixer0_m), shards(mlp0_m), shards(layer1_m)
    small = [meta_tokens, jnp.pad(ev_conv_w[0], ((0, 16 - CONV_WIDTH), (0, 0)))]
    gathered_a = _allgather_chips(own_a + small, "weight_allgather_mixer0")
    pending_b, token1 = _allgather_start(own_b, "weight_allgather_mlp0_start")
    pending_c, token2 = _allgather_start(own_c, "weight_allgather_layer1_start")
    meta_full = _gather_shards(_with_own(gathered_a[-2], small[0]), N_META, D_MODEL, 1)
    conv_full = _gather_shards(_with_own(gathered_a[-1], small[1])[:, :CONV_WIDTH], CONV_WIDTH, LRU_WIDTH, 1)

    def slots(names, dtype):
        return {n: jnp.zeros((N_CHIPS, specs[n][1], specs[n][2] // N_CHIPS) if n in COL_SHARD_GRADS else specs[n][1:3], dtype) for n in names}

    def norms(names, layer):
        return {n: given[n][layer] for n in names}

    def finish_gather(pending, own, after, names, tag):
        landed = _allgather_wait(pending, lax.stop_gradient(after), "weight_allgather_%s_wait" % tag)
        return filled(_sibling_forward(landed, "weight_allgather_%s_forward" % tag), own, names)

    diff_a = {**slots(mixer0_m, F32), **norms(("ln_mix_g", "ln_mix_b"), 0), **{n: given[n] for n in REPLICATED if n not in LAYER_NORMS},
              "x": x, "meta_tokens": meta_full, "ev_conv_w": conv_full}
    diff_b = {**slots(mlp0_m, MXU_DTYPE), **norms(("ln_mlp_g", "ln_mlp_b"), 0)}
    diff_c = {**slots(layer1_m, MXU_DTYPE), **norms(LAYER_NORMS, 1)}
    w_a = filled(gathered_a[:len(mixer0_m)], own_a, mixer0_m)
    h_a, back_a = jax.vjp(lambda d: _mixer0(d, w_a, token1 + token2), diff_a)
    w_b = finish_gather(pending_b, own_b, h_a, mlp0_m, "mlp0")
    h_b, back_b = jax.vjp(lambda d, hh: _mlp0(d, hh, w_b), diff_b, h_a)
    w_c = finish_gather(pending_c, own_c, h_b, layer1_m, "layer1")
    loss, back_c = jax.vjp(lambda d, hh: _layer1_loss(d, hh, w_c, loss_target), diff_c, h_b)
    loss = lax.psum(loss, ("x", "y", "c"))

    def blocks_of(grad, n):
        _, K, N, ax = specs[n]
        if n in COL_SHARD_GRADS:
            blocks = grad
        elif ax == 0:
            blocks = grad.reshape(N_CHIPS, K // N_CHIPS, N)
        else:
            blocks = grad.reshape(K, N_CHIPS, N // N_CHIPS).transpose(1, 0, 2)
        return blocks.reshape(N_CHIPS, 2, blocks.shape[1] // 2, blocks.shape[2])

    def start_reduce(grads_of, names, tag):
        return _direct_scatter_start([blocks_of(grads_of[n], n) for n in names], "grad_scatter_%s_start" % tag)

    g_c, dh = back_c(jnp.ones((), F32))
    flying_c, token = start_reduce(g_c, layer1_m, "layer1")
    g_b, dh = back_b(dh + token)
    flying_b, token = start_reduce(g_b, mlp0_m, "mlp0")
    (g_a,) = back_a(dh + token)
    ps_c, ts_c = _direct_scatter_wait(flying_c, g_a["x"], "grad_scatter_layer1_wait")
    ps_b, ts_b = _direct_scatter_wait(flying_b, g_a["x"], "grad_scatter_mlp0_wait")

    g = {**g_a, **g_b, **g_c}
    g.update({n: jnp.stack([(g_b if n in g_b else g_a)[n], g_c[n]]) for n in LAYER_NORMS})
    repl = jnp.concatenate([g[n].reshape(-1) for n in REPLICATED]).reshape(N_CHIPS, -1)
    small = [_split_shards(g["meta_tokens"], N_META, D_MODEL, 1), _split_shards(g["ev_conv_w"], CONV_WIDTH, LRU_WIDTH, 1), repl]
    small = [pc.reshape(N_CHIPS, 2, -1) for pc in small]
    n_small = sum(pc.shape[2] for pc in small)
    small.append(jnp.zeros((N_CHIPS, 2, _round_up(n_small, 32 * PACK_COLS) - n_small), F32))
    p_small = jnp.concatenate(small, axis=2).reshape(N_CHIPS, 2, -1, PACK_COLS)
    ss_a = _sibling_reduce([blocks_of(g_a[n], n) for n in mixer0_m] + [p_small], [MXU_DTYPE] * len(mixer0_m) + [F32], "mixer0_")
    ts_a = list(_chip_scatter(ss_a, "grad_chip_scatter_mixer0"))
    fs = [_sum_partials(s, t, "grad_chip_sum_mixer0_%d" % i) for i, (s, t) in enumerate(zip(ss_a, ts_a))]
    fs += [_sum_direct(p, t, "grad_sum_%d" % i) for i, (p, t) in enumerate(zip(ps_b + ps_c, ts_b + ts_c))]
    reduced = _sibling_gather(fs, "grad_sibling_gather")
    red_big = dict(zip(mixer0_m + ("small",) + mlp0_m + layer1_m, reduced))
    red_small = red_big.pop("small").reshape(2, -1)

    grads = {}
    for name, parts in BIG_PARAMS:
        grads[name] = jnp.stack([red_big[part].reshape(given[name].shape[1:]) for part in parts])

    def take(off, sz):
        return jnp.concatenate([red_small[0, off // 2:(off + sz) // 2], red_small[1, off // 2:(off + sz) // 2]])

    off = 0
    for name in SMALL_SHARDED:
        sz = given[name].size
        grads[name] = take(off, sz).reshape(given[name].shape)
        off += sz
    n_repl = repl.shape[1]
    own_repl = _to_rows(take(off, n_repl), 16)
    repl_all = _with_own(_allgather_chips([own_repl], "replicated_allgather")[0], own_repl).reshape(N_CHIPS, -1)[:, :n_repl].reshape(-1)
    off = 0
    for name in REPLICATED:
        sz = given[name].size
        grads[name] = repl_all[off:off + sz].reshape(given[name].shape)
        off += sz

    delta, new_m, new_v = {}, {}, {}
    for name, _ in BIG_PARAMS:
        shp = given[name].shape
        two_d = (-1, shp[-1])
        d, nm, nv = _adamw(given[name].reshape(two_d), grads[name].reshape(two_d), given["m_" + name].reshape(two_d),
                           given["v_" + name].reshape(two_d), "adamw_" + name)
        delta[name], new_m[name], new_v[name] = d.reshape(shp), nm.reshape(shp), nv.reshape(shp)
    smalls = SMALL_SHARDED + REPLICATED

    def pack_small(get):
        return _to_rows(jnp.concatenate([get(n).reshape(-1) for n in smalls]), 8)

    outs = _adamw(pack_small(lambda n: given[n]), pack_small(lambda n: grads[n]), pack_small(lambda n: given["m_" + n]),
                  pack_small(lambda n: given["v_" + n]), "adamw_small")
    for res, flat in zip((delta, new_m, new_v), outs):
        flat, off = flat.reshape(-1), 0
        for n in smalls:
            sz = given[n].size
            res[n] = flat[off:off + sz].reshape(given[n].shape)
            off += sz

    return (loss, g_a["x"], *[grads[n] for n in WEIGHT_NAMES], *[delta[n] for n in WEIGHT_NAMES],
            *[new_m[n] for n in WEIGHT_NAMES], *[new_v[n] for n in WEIGHT_NAMES])
```

```python
import functools
import math

import jax
import jax.numpy as jnp
from jax import lax
from jax.experimental import pallas as pl
from jax.experimental.pallas import tpu as pltpu

F32 = jnp.float32
MXU_DTYPE = jnp.bfloat16

D_MODEL = 1024
N_META = 16
LRU_WIDTH = 512
LRU_HEADS = 4
LRU_HEAD_DIM = 128
CONV_WIDTH = 4
LRU_C = 8.0
MLA_HEADS = 8
MLA_NOPE = 64
MLA_ROPE = 32
MLA_V = 64
MLA_Q_RANK = 256
MLA_KV_RANK = 128
RET_HEADS = 4
RET_QK_DIM = 256
RET_V_DIM = 512
D_FF = 4096
ROPE_BASE = 10000.0
DN_ALPHA = 4.0 ** 0.25
EPS = 1e-5
NEG_INF = -1e30
SEQ_BLOCK = 128

ADAM_LR = 0.001
ADAM_B1 = 0.9
ADAM_B2 = 0.999
ADAM_EPS = 1e-08
ADAM_WD = 0.01
ADAM_STEP = 10

PACK_COLS = 1024
N_CHIPS = 4

MESH = pl.DeviceIdType.MESH


def _pick(n, target, align):
    best = None
    for t in range(align, min(n, target) + 1, align):
        if n % t == 0:
            best = t
    return n if best is None else best


def _round_up(n, m):
    return (n + m - 1) // m * m


def _relu2(a):
    r = jnp.maximum(a, 0.0)
    return r * r


def _ln_stats(z):
    mu = jnp.mean(z, axis=-1, keepdims=True)
    zc = z - mu
    var = jnp.mean(zc * zc, axis=-1, keepdims=True)
    return zc, lax.rsqrt(var + EPS)


def _mm_nn(a, w, act, name, out_dtype=F32, norm=None):
    M, K = a.shape
    sharded = w.ndim == 3
    n = w.shape[-1]
    N = n * (w.shape[0] if sharded else 1)
    tm = _pick(M, 1088 if K * a.dtype.itemsize <= 4096 and norm is None else 544, 8)
    tn = _pick(n, 1024, 128)
    per = n // tn
    assert norm is None or tn == N

    def body(a_ref, w_ref, *rest):
        av = a_ref[...]
        if act:
            av = _relu2(av.astype(F32))
        r = jnp.dot(av.astype(MXU_DTYPE), w_ref[...].astype(MXU_DTYPE), preferred_element_type=F32)
        if norm is None:
            rest[0][...] = r.astype(out_dtype)
        else:
            r_ref, g_ref, b_ref, o_ref, z_ref = rest
            z = DN_ALPHA * r_ref[...] + r
            zc, rstd = _ln_stats(z)
            z_ref[...] = z
            o_ref[...] = zc * rstd * g_ref[...] + b_ref[...]

    w_spec = pl.BlockSpec((None, K, tn), lambda i, j: (j // per, 0, j % per)) if sharded else pl.BlockSpec((K, tn), lambda i, j: (0, j))
    tile = pl.BlockSpec((tm, tn), lambda i, j: (i, j))
    in_specs, args = [pl.BlockSpec((tm, K), lambda i, j: (i, 0)), w_spec], [a, w]
    if norm is None:
        out_specs, out_shape = tile, jax.ShapeDtypeStruct((M, N), out_dtype)
    else:
        vec = pl.BlockSpec((1, N), lambda i, j: (0, 0))
        in_specs += [tile, vec, vec]
        args += [norm[0], norm[1].reshape(1, N), norm[2].reshape(1, N)]
        out_specs, out_shape = [tile, tile], [jax.ShapeDtypeStruct((M, N), F32)] * 2
    return pl.pallas_call(
        body, name=name, grid=(M // tm, N // tn), in_specs=in_specs, out_specs=out_specs, out_shape=out_shape,
        compiler_params=pltpu.CompilerParams(dimension_semantics=("parallel", "arbitrary")),
    )(*args)


def _mm_nt(g, w, a_src, name, out_dtype=F32, plus=None):
    M, N = g.shape
    sharded = w.ndim == 3
    K, n = w.shape[-2], w.shape[-1]
    if sharded:
        tk, nk = N, 1
    else:
        tk = N if N * g.dtype.itemsize <= 8192 else _pick(N, 2048, 128)
        nk = N // tk
    tm = _pick(M, 1088 if tk * g.dtype.itemsize <= 4096 else 544, 8)
    tn = _pick(K, 1024, 128)
    has_src = a_src is not None
    assert nk == 1 or out_dtype == F32
    assert plus is None or not has_src

    def body(*refs):
        if has_src:
            g_ref, w_ref, s_ref, o_ref = refs
        elif plus is not None:
            g_ref, w_ref, p_ref, o_ref = refs
        else:
            g_ref, w_ref, o_ref = refs
        nt = (((1,), (1,)), ((), ()))
        if sharded:
            r = sum(lax.dot_general(g_ref[:, s * n:(s + 1) * n].astype(MXU_DTYPE), w_ref[s].astype(MXU_DTYPE), nt, preferred_element_type=F32)
                    for s in range(w_ref.shape[0]))
        else:
            r = lax.dot_general(g_ref[...].astype(MXU_DTYPE), w_ref[...].astype(MXU_DTYPE), nt, preferred_element_type=F32)
        if has_src:
            r = r * (2.0 * jnp.maximum(s_ref[...].astype(F32), 0.0))
        first = r if plus is None else r + DN_ALPHA * p_ref[...]
        if nk == 1:
            o_ref[...] = first.astype(out_dtype)
        else:
            k = pl.program_id(2)

            @pl.when(k == 0)
            def _():
                o_ref[...] = first

            @pl.when(k > 0)
            def _():
                o_ref[...] += r

    w_spec = (pl.BlockSpec((w.shape[0], tn, n), lambda i, j, k: (0, j, 0)) if sharded
              else pl.BlockSpec((tn, tk), lambda i, j, k: (j, k)))
    in_specs = [pl.BlockSpec((tm, tk), lambda i, j, k: (i, k)), w_spec]
    args = [g, w]
    if has_src:
        assert nk == 1
        in_specs.append(pl.BlockSpec((tm, tn), lambda i, j, k: (i, j)))
        args.append(a_src)
    if plus is not None:
        in_specs.append(pl.BlockSpec((tm, tn), lambda i, j, k: (i, j)))
        args.append(plus)
    return pl.pallas_call(
        body, name=name,
        grid=(M // tm, K // tn, nk),
        in_specs=in_specs,
        out_specs=pl.BlockSpec((tm, tn), lambda i, j, k: (i, j)),
        out_shape=jax.ShapeDtypeStruct((M, K), out_dtype),
        compiler_params=pltpu.CompilerParams(dimension_semantics=("parallel", "parallel", "arbitrary")),
    )(*args)


def _mm_tn(a, g, act, name, col_shards=1, out_dtype=F32):
    M, K = a.shape
    _, N = g.shape
    n = N // col_shards
    tm, tn, tk = _pick(K, 1024, 128), _pick(n, 1024, 128), _pick(M, 2176, 8)
    nk = M // tk
    per = n // tn
    direct = out_dtype == F32

    def body(a_ref, g_ref, o_ref, *scratch):
        acc_ref = o_ref if direct else scratch[0]
        k = pl.program_id(2)
        av = a_ref[...]
        if act:
            av = _relu2(av.astype(F32))
        r = lax.dot_general(av.astype(MXU_DTYPE), g_ref[...].astype(MXU_DTYPE),
                            (((0,), (0,)), ((), ())), preferred_element_type=F32)

        @pl.when(k == 0)
        def _():
            acc_ref[...] = r

        @pl.when(k > 0)
        def _():
            acc_ref[...] += r

        if not direct:
            @pl.when(k == nk - 1)
            def _():
                o_ref[...] = acc_ref[...].astype(out_dtype)

    if col_shards == 1:
        out_spec, out_shape = pl.BlockSpec((tm, tn), lambda i, j, k: (i, j)), (K, N)
    else:
        out_spec, out_shape = pl.BlockSpec((None, tm, tn), lambda i, j, k: (j // per, i, j % per)), (col_shards, K, n)
    return pl.pallas_call(
        body, name=name,
        grid=(K // tm, N // tn, nk),
        in_specs=[pl.BlockSpec((tk, tm), lambda i, j, k: (k, i)), pl.BlockSpec((tk, tn), lambda i, j, k: (k, j))],
        out_specs=out_spec,
        out_shape=jax.ShapeDtypeStruct(out_shape, out_dtype),
        scratch_shapes=[] if direct else [pltpu.VMEM((tm, tn), F32)],
        compiler_params=pltpu.CompilerParams(dimension_semantics=("parallel", "parallel", "arbitrary")),
    )(a, g)


@functools.partial(jax.custom_vjp, nondiff_argnums=(3, 4, 5, 6))
def matmul(a, w, w_grad_slot, act, name, out_dtype, col_shards):
    return _mm_nn(a, w, act, name + "_fwd", out_dtype)


def _matmul_fwd(a, w, w_grad_slot, act, name, out_dtype, col_shards):
    return _mm_nn(a, w, act, name + "_fwd", out_dtype), (a, w, jnp.zeros((), w_grad_slot.dtype))


def _matmul_bwd(act, name, out_dtype, col_shards, res, g):
    a, w, slot_like = res
    w_grad_dtype = slot_like.dtype
    da = _mm_nt(g, w, a if act else None, name + "_dx")
    dw = _mm_tn(a, g, act, name + "_dw", col_shards, w_grad_dtype)
    return da, None, dw


matmul.defvjp(_matmul_fwd, _matmul_bwd)


def _ln_bwd_call(z, g, dy, name):
    M, D = z.shape
    tm = _pick(M, 544, 8)

    def body(z_ref, g_ref, dy_ref, dz_ref, dg_ref, db_ref):
        @pl.when(pl.program_id(0) == 0)
        def _():
            dg_ref[...] = jnp.zeros_like(dg_ref)
            db_ref[...] = jnp.zeros_like(db_ref)

        zc, rstd = _ln_stats(z_ref[...])
        xhat = zc * rstd
        dy = dy_ref[...]
        dxh = dy * g_ref[...]
        m1 = jnp.mean(dxh, axis=-1, keepdims=True)
        m2 = jnp.mean(dxh * xhat, axis=-1, keepdims=True)
        dz_ref[...] = rstd * (dxh - m1 - xhat * m2)
        dg_ref[...] += jnp.sum(dy * xhat, axis=0, keepdims=True)
        db_ref[...] += jnp.sum(dy, axis=0, keepdims=True)

    row = pl.BlockSpec((tm, D), lambda i: (i, 0))
    vec = pl.BlockSpec((1, D), lambda i: (0, 0))
    return pl.pallas_call(
        body, name=name, grid=(M // tm,), in_specs=[row, vec, row], out_specs=[row, vec, vec],
        out_shape=[jax.ShapeDtypeStruct((M, D), F32), jax.ShapeDtypeStruct((1, D), F32), jax.ShapeDtypeStruct((1, D), F32)],
        compiler_params=pltpu.CompilerParams(dimension_semantics=("arbitrary",)),
    )(z, g.reshape(1, D), dy)


@functools.partial(jax.custom_vjp, nondiff_argnums=(7,))
def mlp_block(h, w1, w2, w1_grad_slot, w2_grad_slot, g, b, name):
    return _mlp_block_fwd(h, w1, w2, w1_grad_slot, w2_grad_slot, g, b, name)[0]


def _mlp_block_fwd(h, w1, w2, w1_grad_slot, w2_grad_slot, g, b, name):
    u = _mm_nn(h, w1, False, name + "_w1_fwd", out_dtype=MXU_DTYPE)
    out, z = _mm_nn(u, w2, True, name + "_w2_norm_fwd", norm=(h, g, b))
    return out, (h, u, z, w1, w2, g, jnp.zeros((), w1_grad_slot.dtype))


def _mlp_block_bwd(name, res, dy):
    h, u, z, w1, w2, g, slot_like = res
    dz, dg, db = _ln_bwd_call(z, g, dy, name + "_norm_bwd")
    du = _mm_nt(dz, w2, u, name + "_w2_dx", out_dtype=MXU_DTYPE)
    dw2 = _mm_tn(u, dz, True, name + "_w2_dw", 1, slot_like.dtype)
    dh = _mm_nt(du, w1, None, name + "_w1_dx", plus=dz)
    dw1 = _mm_tn(h, du, False, name + "_w1_dw", N_CHIPS, slot_like.dtype)
    return dh, None, None, dw1, dw2, dg.reshape(g.shape), db.reshape(g.shape)


mlp_block.defvjp(_mlp_block_fwd, _mlp_block_bwd)


@functools.partial(jax.custom_vjp, nondiff_argnums=(6,))
def out_block(h, y, w, w_grad_slot, g, b, name):
    return _out_block_fwd(h, y, w, w_grad_slot, g, b, name)[0]


def _out_block_fwd(h, y, w, w_grad_slot, g, b, name):
    out, z = _mm_nn(y, w, False, name + "_norm_fwd", norm=(h, g, b))
    return out, (y, z, w, g, jnp.zeros((), w_grad_slot.dtype))


def _out_block_bwd(name, res, dy):
    y, z, w, g, slot_like = res
    dz, dg, db = _ln_bwd_call(z, g, dy, name + "_norm_bwd")
    d_y = _mm_nt(dz, w, None, name + "_dx")
    dw = _mm_tn(y, dz, False, name + "_dw", 1, slot_like.dtype)
    return DN_ALPHA * dz, d_y, None, dw, dg.reshape(g.shape), db.reshape(g.shape)


out_block.defvjp(_out_block_fwd, _out_block_bwd)


def _rms_fwd_call(x, g, name, col_block=0):
    R = x.shape[0]
    W = g.shape[-1]
    tr = _pick(R, 1088, 8)

    def body(x_ref, g_ref, o_ref):
        xv = x_ref[...]
        rstd = lax.rsqrt(jnp.mean(xv * xv, axis=-1, keepdims=True) + EPS)
        o_ref[...] = xv * rstd * g_ref[...]

    vec = pl.BlockSpec((1, W), lambda i: (0, 0))
    return pl.pallas_call(
        body, name=name, grid=(R // tr,), in_specs=[pl.BlockSpec((tr, W), lambda i: (i, col_block)), vec],
        out_specs=pl.BlockSpec((tr, W), lambda i: (i, 0)), out_shape=jax.ShapeDtypeStruct((R, W), F32),
        compiler_params=pltpu.CompilerParams(dimension_semantics=("parallel",)),
    )(x, g.reshape(1, W))


def _rms_bwd_call(x, g, dy, name, col_block=0):
    R = x.shape[0]
    W = g.shape[-1]
    tr = _pick(R, 1088, 8)

    def body(x_ref, g_ref, dy_ref, dx_ref, dg_ref):
        @pl.when(pl.program_id(0) == 0)
        def _():
            dg_ref[...] = jnp.zeros_like(dg_ref)

        xv = x_ref[...]
        rstd = lax.rsqrt(jnp.mean(xv * xv, axis=-1, keepdims=True) + EPS)
        xhat = xv * rstd
        dy = dy_ref[...]
        dxh = dy * g_ref[...]
        dx_ref[...] = rstd * (dxh - xhat * jnp.mean(dxh * xhat, axis=-1, keepdims=True))
        dg_ref[...] += jnp.sum(dy * xhat, axis=0, keepdims=True)

    row = pl.BlockSpec((tr, W), lambda i: (i, 0))
    vec = pl.BlockSpec((1, W), lambda i: (0, 0))
    return pl.pallas_call(
        body, name=name, grid=(R // tr,), in_specs=[pl.BlockSpec((tr, W), lambda i: (i, col_block)), vec, row], out_specs=[row, vec],
        out_shape=[jax.ShapeDtypeStruct((R, W), F32), jax.ShapeDtypeStruct((1, W), F32)],
        compiler_params=pltpu.CompilerParams(dimension_semantics=("arbitrary",)),
    )(x, g.reshape(1, W), dy)


def _loss_call(h, tgt, n_tokens, name):
    B, Tp, D = h.shape
    tr = _pick(Tp, 544, 8)

    def body(y_ref, t_ref, dy_ref, acc_ref):
        @pl.when(jnp.logical_and(pl.program_id(0) == 0, pl.program_id(1) == 0))
        def _():
            acc_ref[...] = jnp.zeros_like(acc_ref)

        t = lax.broadcasted_iota(jnp.int32, (tr, 1), 0) + pl.program_id(1) * tr
        counts = jnp.logical_and(t >= N_META, t < N_META + n_tokens)
        e = jnp.where(counts, y_ref[...] - t_ref[...], 0.0)
        dy_ref[...] = e * (1.0 / D)
        acc_ref[...] += jnp.sum(jnp.sum(e * e, axis=-1, keepdims=True), axis=0, keepdims=True) * (0.5 / D)

    row = pl.BlockSpec((None, tr, D), lambda b, i: (b, i, 0))
    one = pl.BlockSpec((1, 1), lambda b, i: (0, 0))
    return pl.pallas_call(
        body, name=name, grid=(B, Tp // tr), in_specs=[row, row], out_specs=[row, one],
        out_shape=[jax.ShapeDtypeStruct((B, Tp, D), F32), jax.ShapeDtypeStruct((1, 1), F32)],
        compiler_params=pltpu.CompilerParams(dimension_semantics=("arbitrary", "arbitrary")),
    )(h, tgt)


@functools.partial(jax.custom_vjp, nondiff_argnums=(2,))
def loss_head(h, tgt, n_tokens):
    return _loss_call(h, tgt, n_tokens, "loss_head")[1][0, 0]


def _loss_head_fwd(h, tgt, n_tokens):
    dy, acc = _loss_call(h, tgt, n_tokens, "loss_head")
    return acc[0, 0], dy


def _loss_head_bwd(n_tokens, dy, ct):
    return ct * dy, None


loss_head.defvjp(_loss_head_fwd, _loss_head_bwd)


_GELU_C = math.sqrt(2.0 / math.pi)


def _gelu_parts(x):
    x2 = x * x
    t = jnp.tanh(_GELU_C * (x + 0.044715 * x * x2))
    gelu = 0.5 * x * (1.0 + t)
    dgelu = 0.5 * (1.0 + t) + 0.5 * x * (1.0 - t * t) * (_GELU_C * (1.0 + 3.0 * 0.044715 * x2))
    return gelu, dgelu


def _sigmoid(x):
    return 1.0 / (1.0 + jnp.exp(-x))


def _scan8(a, b, carry, reverse):
    row = lax.broadcasted_iota(jnp.int32, a.shape, 0)
    for s in (1, 2, 4):
        shift = 8 - s if reverse else s
        keep = (row < 8 - s) if reverse else (row >= s)
        b = jnp.where(keep, a * pltpu.roll(b, shift, 0) + b, b)
        a = jnp.where(keep, a * pltpu.roll(a, shift, 0), a)
    return a * carry + b


def _lru_pre(prec_ref, prev_ref, first, cw_ref, cb_ref, wa_ref, ba_ref, wx_ref, bx_ref, sp_ref):
    tc = prec_ref.shape[0]
    prev = jnp.where(first, 0.0, prev_ref[...])
    ext = jnp.concatenate([prev, prec_ref[...]], axis=0)
    cw = cw_ref[...]
    taps = [ext[8:] if k == CONV_WIDTH - 1 else pltpu.roll(ext, CONV_WIDTH - 1 - k, 0)[8:] for k in range(CONV_WIDTH)]
    xc = cb_ref[...] + sum(cw[k:k + 1, :] * taps[k] for k in range(CONV_WIDTH))
    ga, gx = [], []
    for h in range(LRU_HEADS):
        xh = xc[:, h * LRU_HEAD_DIM:(h + 1) * LRU_HEAD_DIM].astype(MXU_DTYPE)
        ga.append(jnp.dot(xh, wa_ref[h].astype(MXU_DTYPE), preferred_element_type=F32))
        gx.append(jnp.dot(xh, wx_ref[h].astype(MXU_DTYPE), preferred_element_type=F32))
    r = _sigmoid(jnp.concatenate(ga, axis=1) + ba_ref[...])
    i = _sigmoid(jnp.concatenate(gx, axis=1) + bx_ref[...])
    log_a = -LRU_C * r * sp_ref[...]
    a = jnp.exp(log_a)
    a2 = a * a
    mult = jnp.sqrt(-jnp.tanh(log_a) * (a2 + 1.0))
    return taps, xc, r, i, a, a2, mult


def _lru_fwd_call(p, cw, cb, wa, ba, wx, bx, sp):
    B, Tp, _ = p.shape
    W = LRU_WIDTH
    tc = SEQ_BLOCK
    nc = Tp // tc

    def body(pg_ref, prec_ref, prev_ref, cw_ref, cb_ref, wa_ref, ba_ref, wx_ref, bx_ref, sp_ref, y_ref, h_ref, carry_ref):
        first = pl.program_id(1) == 0

        @pl.when(first)
        def _():
            carry_ref[...] = jnp.zeros_like(carry_ref)

        _, xc, r, i, a, a2, mult = _lru_pre(prec_ref, prev_ref, first, cw_ref, cb_ref, wa_ref, ba_ref, wx_ref, bx_ref, sp_ref)
        b = mult * (i * xc)
        carry = carry_ref[0:1, :]
        for t in range(tc // 8):
            h = _scan8(a[8 * t:8 * t + 8], b[8 * t:8 * t + 8], carry, False)
            h_ref[8 * t:8 * t + 8, :] = h
            carry = h[7:8, :]
        carry_ref[...] = jnp.broadcast_to(carry, carry_ref.shape)
        y_ref[...] = h_ref[...] * _gelu_parts(pg_ref[...])[0]

    cur = pl.BlockSpec((None, tc, W), lambda b, j: (b, j, 0))
    rec = pl.BlockSpec((None, tc, W), lambda b, j: (b, j, 1))
    prev = pl.BlockSpec((None, 8, W), lambda b, j: (b, jnp.maximum(j * (tc // 8) - 1, 0), 1))
    vec = pl.BlockSpec((1, W), lambda b, j: (0, 0))
    cws = pl.BlockSpec((CONV_WIDTH, W), lambda b, j: (0, 0))
    wsp = pl.BlockSpec((LRU_HEADS, LRU_HEAD_DIM, LRU_HEAD_DIM), lambda b, j: (0, 0, 0))
    return pl.pallas_call(
        body, name="lru_fwd", grid=(B, nc),
        in_specs=[cur, rec, prev, cws, vec, wsp, vec, wsp, vec, vec],
        out_specs=[cur, cur],
        out_shape=[jax.ShapeDtypeStruct((B, Tp, W), F32), jax.ShapeDtypeStruct((B, Tp, W), F32)],
        scratch_shapes=[pltpu.VMEM((8, W), F32)],
        compiler_params=pltpu.CompilerParams(dimension_semantics=("arbitrary", "arbitrary")),
    )(p, p, p, cw, cb, wa, ba, wx, bx, sp)


def _lru_bwd_call(p, hseq, dy, cw, cb, wa, ba, wx, bx, sp, dpq, dpkv, dkpe):
    B, Tp, P = p.shape
    W = LRU_WIDTH
    tc = SEQ_BLOCK
    nc = Tp // tc
    HD = LRU_HEAD_DIM

    def body(pg_ref, prec_ref, prev_ref, h_ref, hprev_ref, dy_ref, cw_ref, cb_ref, wa_ref, ba_ref, wx_ref, bx_ref, sp_ref,
             dpq_ref, dpkv_ref, dkpe_ref, dp_ref, dcw_ref, dcb_ref, dwa_ref, dba_ref, dwx_ref, dbx_ref, dsp_ref,
             gcar_ref, anext_ref, halo_ref, g_ref):
        j = pl.program_id(1)
        first = j == nc - 1
        last = j == 0

        @pl.when(jnp.logical_and(pl.program_id(0) == 0, last))
        def _():
            for ref in (dcw_ref, dcb_ref, dwa_ref, dba_ref, dwx_ref, dbx_ref, dsp_ref):
                ref[...] = jnp.zeros_like(ref)

        @pl.when(last)
        def _():
            gcar_ref[...] = jnp.zeros_like(gcar_ref)
            anext_ref[...] = jnp.zeros_like(anext_ref)
            halo_ref[...] = jnp.zeros_like(halo_ref)

        taps, xc, r, i, a, a2, mult = _lru_pre(prec_ref, prev_ref, first, cw_ref, cb_ref, wa_ref, ba_ref, wx_ref, bx_ref, sp_ref)
        row = lax.broadcasted_iota(jnp.int32, (tc, W), 0)
        gelu, dgelu = _gelu_parts(pg_ref[...])
        dy = dy_ref[...]
        hcur = h_ref[...]
        dp_ref[:, 0:W] = dy * hcur * dgelu
        dp_ref[:, 2 * W:2 * W + MLA_Q_RANK] = dpq_ref[...]
        dp_ref[:, _KPE_START - MLA_KV_RANK:_KPE_START] = dpkv_ref[...]
        dp_ref[:, _KPE_START:P] = pltpu.roll(dkpe_ref[...], HEAD_LANES - MLA_NOPE, 1)[:, 0:P - _KPE_START]
        dh = dy * gelu
        a_next = jnp.where(row == tc - 1, anext_ref[0:1, :], pltpu.roll(a, tc - 1, 0))
        carry = gcar_ref[0:1, :]
        for t in reversed(range(tc // 8)):
            g = _scan8(a_next[8 * t:8 * t + 8], dh[8 * t:8 * t + 8], carry, True)
            g_ref[8 * t:8 * t + 8, :] = g
            carry = g[0:1, :]
        gcar_ref[...] = jnp.broadcast_to(carry, gcar_ref.shape)
        anext_ref[...] = jnp.broadcast_to(a[0:1, :], anext_ref.shape)
        G = g_ref[...]
        h_before = jnp.where(first, 0.0, hprev_ref[7:8, :])
        hprev = jnp.where(row == 0, h_before, pltpu.roll(hcur, 1, 0))
        d_a = G * hprev
        gx_ = G * xc
        d_mult = gx_ * i
        d_i = gx_ * mult
        dxc = G * (mult * i)
        d_la = d_a * a - d_mult * (a2 / mult)
        sp = sp_ref[...]
        d_r = d_la * (-LRU_C * sp)
        dsp_ref[...] += jnp.sum(d_la * (-LRU_C * r), axis=0, keepdims=True)
        dga = d_r * r * (1.0 - r)
        dgx = d_i * i * (1.0 - i)
        dba_ref[...] += jnp.sum(dga, axis=0, keepdims=True)
        dbx_ref[...] += jnp.sum(dgx, axis=0, keepdims=True)
        back = []
        for h in range(LRU_HEADS):
            sl = slice(h * HD, (h + 1) * HD)
            xh = xc[:, sl].astype(MXU_DTYPE)
            ah = dga[:, sl].astype(MXU_DTYPE)
            bh = dgx[:, sl].astype(MXU_DTYPE)
            tn = (((0,), (0,)), ((), ()))
            nt = (((1,), (1,)), ((), ()))
            dwa_ref[h] += lax.dot_general(xh, ah, tn, preferred_element_type=F32)
            dwx_ref[h] += lax.dot_general(xh, bh, tn, preferred_element_type=F32)
            back.append(lax.dot_general(ah, wa_ref[h].astype(MXU_DTYPE), nt, preferred_element_type=F32)
                        + lax.dot_general(bh, wx_ref[h].astype(MXU_DTYPE), nt, preferred_element_type=F32))
        dxc = dxc + jnp.concatenate(back, axis=1)
        dcb_ref[...] += jnp.sum(dxc, axis=0, keepdims=True)
        for k in range(CONV_WIDTH):
            dcw_ref[k:k + 1, :] += jnp.sum(dxc * taps[k], axis=0, keepdims=True)
        ext = jnp.concatenate([dxc, halo_ref[...]], axis=0)
        cw = cw_ref[...]
        acc = cw[CONV_WIDTH - 1:CONV_WIDTH, :] * dxc
        for k in range(CONV_WIDTH - 1):
            s = CONV_WIDTH - 1 - k
            acc = acc + cw[k:k + 1, :] * pltpu.roll(ext, tc + 8 - s, 0)[:tc]
        dp_ref[:, W:2 * W] = acc
        halo_ref[...] = dxc[0:8, :]

    rev = lambda j: nc - 1 - j
    cur = pl.BlockSpec((None, tc, W), lambda b, j: (b, rev(j), 0))
    rec = pl.BlockSpec((None, tc, W), lambda b, j: (b, rev(j), 1))
    prev = pl.BlockSpec((None, 8, W), lambda b, j: (b, jnp.maximum(rev(j) * (tc // 8) - 1, 0), 0))
    prev_rec = pl.BlockSpec((None, 8, W), lambda b, j: (b, jnp.maximum(rev(j) * (tc // 8) - 1, 0), 1))
    vec = pl.BlockSpec((1, W), lambda b, j: (0, 0))
    cws = pl.BlockSpec((CONV_WIDTH, W), lambda b, j: (0, 0))
    wsp = pl.BlockSpec((LRU_HEADS, HD, HD), lambda b, j: (0, 0, 0))
    vs = jax.ShapeDtypeStruct((1, W), F32)
    ws = jax.ShapeDtypeStruct((LRU_HEADS, HD, HD), F32)

    def rows(width):
        return pl.BlockSpec((None, tc, width), lambda b, j: (b, rev(j), 0))

    return pl.pallas_call(
        body, name="lru_bwd", grid=(B, nc),
        in_specs=[cur, rec, prev_rec, cur, prev, cur, cws, vec, wsp, vec, wsp, vec, vec, rows(MLA_Q_RANK), rows(MLA_KV_RANK), rows(HEAD_LANES)],
        out_specs=[rows(P), cws, vec, wsp, vec, wsp, vec, vec],
        out_shape=[jax.ShapeDtypeStruct((B, Tp, P), F32), jax.ShapeDtypeStruct((CONV_WIDTH, W), F32), vs, ws, vs, ws, vs, vs],
        scratch_shapes=[pltpu.VMEM((8, W), F32), pltpu.VMEM((8, W), F32), pltpu.VMEM((8, W), F32), pltpu.VMEM((tc, W), F32)],
        compiler_params=pltpu.CompilerParams(dimension_semantics=("arbitrary", "arbitrary")),
    )(p, p, p, hseq, hseq, dy, cw, cb, wa, ba, wx, bx, sp, dpq, dpkv, dkpe)


_Q_BLOCK = 2 * LRU_WIDTH // MLA_Q_RANK
_KV_BLOCK = (2 * LRU_WIDTH + MLA_Q_RANK) // MLA_KV_RANK
_KPE_START = 2 * LRU_WIDTH + MLA_Q_RANK + MLA_KV_RANK


@jax.custom_vjp
def even_front(p, cw, cb, wa, ba, wx, bx, sp, gq, gkv):
    return _even_front_fwd(p, cw, cb, wa, ba, wx, bx, sp, gq, gkv)[0]


def _even_front_fwd(p, cw, cb, wa, ba, wx, bx, sp, gq, gkv):
    B, Tp, W = p.shape
    p2d = p.reshape(B * Tp, W)
    y, hseq = _lru_fwd_call(p, cw, cb, wa, ba, wx, bx, sp)
    qn = _rms_fwd_call(p2d, gq, "q_norm_fwd", _Q_BLOCK)
    kvn = _rms_fwd_call(p2d, gkv, "kv_norm_fwd", _KV_BLOCK)
    kpe = jnp.pad(p[:, :, _KPE_START:], ((0, 0), (0, 0), (MLA_NOPE, HEAD_LANES - MLA_NOPE - MLA_ROPE)))
    return (y, qn, kvn, kpe), (p, hseq, cw, cb, wa, ba, wx, bx, sp, gq, gkv)


def _even_front_bwd(res, cts):
    p, hseq, cw, cb, wa, ba, wx, bx, sp, gq, gkv = res
    dy, dqn, dkvn, dkpe = cts
    B, Tp, W = p.shape
    p2d = p.reshape(B * Tp, W)
    dpq, dgq = _rms_bwd_call(p2d, gq, dqn, "q_norm_bwd", _Q_BLOCK)
    dpkv, dgkv = _rms_bwd_call(p2d, gkv, dkvn, "kv_norm_bwd", _KV_BLOCK)
    dp, dcw, dcb, dwa, dba, dwx, dbx, dsp = _lru_bwd_call(p, hseq, dy, cw, cb, wa, ba, wx, bx, sp, dpq.reshape(B, Tp, -1),
                                                          dpkv.reshape(B, Tp, -1), dkpe)
    return dp, dcw, dcb, dwa, dba, dwx, dbx, dsp, dgq.reshape(gq.shape), dgkv.reshape(gkv.shape)


even_front.defvjp(_even_front_fwd, _even_front_bwd)


def _rope_tables(pos, half):
    inv = ROPE_BASE ** (-jnp.arange(half, dtype=F32) / half)
    ang = pos.astype(F32)[:, None] * inv[None, :]
    return jnp.cos(ang), jnp.sin(ang)


_NT = (((1,), (1,)), ((), ()))
_TN = (((0,), (0,)), ((), ()))
HEAD_LANES = 128
_MLA_SCALE = (MLA_NOPE + MLA_ROPE) ** -0.5
_LOG2E = math.log2(math.e)


Q_BLOCK = 512


def _query_blocks(Tp):
    first = Tp % Q_BLOCK or Q_BLOCK
    return [(0, first)] + [(r, r + Q_BLOCK) for r in range(first, Tp, Q_BLOCK)]


def _mask_diagonal(s, fill):
    R, L = s.shape
    row = lax.broadcasted_iota(jnp.int32, (R, R), 0)
    col = lax.broadcasted_iota(jnp.int32, (R, R), 1)
    last = jnp.where(col <= row, s[:, L - R:], fill)
    return last if L == R else jnp.concatenate([s[:, :L - R], last], axis=1)


def _mla_rope_tables(pos):
    half = MLA_ROPE // 2
    cos, sin = _rope_tables(pos, half)
    T = pos.shape[0]
    ones, zeros = jnp.ones((T, MLA_NOPE), F32), jnp.zeros((T, MLA_NOPE), F32)
    tail1, tail0 = jnp.ones((T, HEAD_LANES - MLA_NOPE - MLA_ROPE), F32), jnp.zeros((T, HEAD_LANES - MLA_NOPE - MLA_ROPE), F32)
    zh = jnp.zeros((T, half), F32)
    c = jnp.concatenate([ones, cos, cos, tail1], axis=1)
    s_up = jnp.concatenate([zeros, -sin, zh, tail0], axis=1)
    s_down = jnp.concatenate([zeros, zh, sin, tail0], axis=1)
    return c, s_up, s_down


def _rope_lanes(x, c, s_up, s_down):
    half = MLA_ROPE // 2
    return x * c + pltpu.roll(x, HEAD_LANES - half, 1) * s_up + pltpu.roll(x, half, 1) * s_down


def _unrope_lanes(d, c, s_up, s_down):
    half = MLA_ROPE // 2
    return d * c + pltpu.roll(d * s_up, half, 1) + pltpu.roll(d * s_down, HEAD_LANES - half, 1)


def _mla_operands(q_ref, kv_ref, kpe_ref, c, s_up, s_down):
    lane = lax.broadcasted_iota(jnp.int32, kv_ref.shape, 1)
    qr = (_rope_lanes(q_ref[...].astype(F32), c, s_up, s_down) * (_MLA_SCALE * _LOG2E)).astype(MXU_DTYPE)
    kr = jnp.where(lane < MLA_NOPE, kv_ref[...].astype(F32), _rope_lanes(kpe_ref[...], c, s_up, s_down)).astype(MXU_DTYPE)
    return qr, kr, lane


def _mla_specs(Tp):
    head = pl.BlockSpec((None, Tp, HEAD_LANES), lambda b, h: (b, 0, h))
    shared = pl.BlockSpec((None, Tp, HEAD_LANES), lambda b, h: (b, 0, 0))
    tab = pl.BlockSpec((Tp, HEAD_LANES), lambda b, h: (0, 0))
    lse = pl.BlockSpec((None, None, Tp, 1), lambda b, h: (b, h, 0, 0))
    return head, shared, tab, lse


def _attn_fwd_call(q, kv, kpe, tabs):
    B, Tp, _ = q.shape

    def body(q_ref, kv_ref, kpe_ref, c_ref, su_ref, sd_ref, o_ref, lse_ref, qr_ref, kr_ref):
        qr, kr, lane = _mla_operands(q_ref, kv_ref, kpe_ref, c_ref[...], su_ref[...], sd_ref[...])
        qr_ref[...] = qr
        kr_ref[...] = kr
        for r0, L in _query_blocks(Tp):
            blk = slice(r0, L)
            s = _mask_diagonal(lax.dot_general(qr_ref[blk, :], kr_ref[0:L, :], _NT, preferred_element_type=F32), NEG_INF)
            m = jnp.max(s, axis=-1, keepdims=True)
            p = jnp.exp2(s - m)
            l = jnp.sum(p, axis=-1, keepdims=True)
            o = jnp.dot(p.astype(MXU_DTYPE), kv_ref[0:L, :].astype(MXU_DTYPE), preferred_element_type=F32)
            o_ref[blk, :] = jnp.where(lane[blk, :] >= MLA_NOPE, o / l, 0.0)
            lse_ref[blk, :] = m + jnp.log2(l)

    head, shared, tab, lse = _mla_specs(Tp)
    return pl.pallas_call(
        body, name="mla_attn_fwd", grid=(B, MLA_HEADS), in_specs=[head, head, shared, tab, tab, tab], out_specs=[head, lse],
        out_shape=[jax.ShapeDtypeStruct((B, Tp, MLA_HEADS * HEAD_LANES), F32), jax.ShapeDtypeStruct((B, MLA_HEADS, Tp, 1), F32)],
        scratch_shapes=[pltpu.VMEM((Tp, HEAD_LANES), MXU_DTYPE), pltpu.VMEM((Tp, HEAD_LANES), MXU_DTYPE)],
        compiler_params=pltpu.CompilerParams(dimension_semantics=("parallel", "parallel")),
    )(q, kv, kpe, *tabs)


def _attn_bwd_call(q, kv, kpe, tabs, o, lse, do):
    B, Tp, _ = q.shape

    def body(q_ref, kv_ref, kpe_ref, c_ref, su_ref, sd_ref, o_ref, lse_ref, do_ref, dq_ref, dkv_ref, dkpe_ref,
             qr_ref, kr_ref, dqa_ref, dka_ref, dva_ref):
        c, s_up, s_down = c_ref[...], su_ref[...], sd_ref[...]
        qr, kr, lane = _mla_operands(q_ref, kv_ref, kpe_ref, c, s_up, s_down)
        qr_ref[...] = qr
        kr_ref[...] = kr
        dka_ref[...] = jnp.zeros_like(dka_ref)
        dva_ref[...] = jnp.zeros_like(dva_ref)
        for r0, L in _query_blocks(Tp):
            blk = slice(r0, L)
            qb = qr_ref[blk, :]
            do = jnp.where(lane[blk, :] >= MLA_NOPE, do_ref[blk, :], 0.0)
            delta = jnp.sum(do * o_ref[blk, :], axis=-1, keepdims=True)
            s = _mask_diagonal(lax.dot_general(qb, kr_ref[0:L, :], _NT, preferred_element_type=F32), NEG_INF)
            p = jnp.exp2(s - lse_ref[blk, :])
            dob = do.astype(MXU_DTYPE)
            dva_ref[0:L, :] += lax.dot_general(p.astype(MXU_DTYPE), dob, _TN, preferred_element_type=F32)
            dp = lax.dot_general(dob, kv_ref[0:L, :].astype(MXU_DTYPE), _NT, preferred_element_type=F32)
            ds = (p * (dp - delta)).astype(MXU_DTYPE)
            dqa_ref[blk, :] = jnp.dot(ds, kr_ref[0:L, :], preferred_element_type=F32)
            dka_ref[0:L, :] += lax.dot_general(ds, qb, _TN, preferred_element_type=F32)
        dq_ref[...] = _unrope_lanes(dqa_ref[...] * _MLA_SCALE, c, s_up, s_down).astype(dq_ref.dtype)
        dk = dka_ref[...] * (1.0 / _LOG2E)
        dkv_ref[...] = jnp.where(lane < MLA_NOPE, dk, dva_ref[...]).astype(dkv_ref.dtype)
        dkpe = jnp.where(lane >= MLA_NOPE, _unrope_lanes(dk, c, s_up, s_down), 0.0)

        @pl.when(pl.program_id(1) == 0)
        def _():
            dkpe_ref[...] = dkpe

        @pl.when(pl.program_id(1) > 0)
        def _():
            dkpe_ref[...] += dkpe

    head, shared, tab, lse_spec = _mla_specs(Tp)
    wide = jax.ShapeDtypeStruct((B, Tp, MLA_HEADS * HEAD_LANES), q.dtype)
    acc = pltpu.VMEM((Tp, HEAD_LANES), F32)
    return pl.pallas_call(
        body, name="mla_attn_bwd", grid=(B, MLA_HEADS),
        in_specs=[head, head, shared, tab, tab, tab, head, lse_spec, head], out_specs=[head, head, shared],
        out_shape=[wide, wide, jax.ShapeDtypeStruct((B, Tp, HEAD_LANES), F32)],
        scratch_shapes=[pltpu.VMEM((Tp, HEAD_LANES), MXU_DTYPE), pltpu.VMEM((Tp, HEAD_LANES), MXU_DTYPE), acc, acc, acc],
        compiler_params=pltpu.CompilerParams(dimension_semantics=("parallel", "arbitrary")),
    )(q, kv, kpe, *tabs, o, lse, do)


@jax.custom_vjp
def mla_attention(q, kv, kpe, tabs):
    return _attn_fwd_call(q, kv, kpe, tabs)[0]


def _mla_attention_fwd(q, kv, kpe, tabs):
    o, lse = _attn_fwd_call(q, kv, kpe, tabs)
    return o, (q, kv, kpe, tabs, o, lse)


def _mla_attention_bwd(res, do):
    q, kv, kpe, tabs, o, lse = res
    dq, dkv, dkpe = _attn_bwd_call(q, kv, kpe, tabs, o, lse, do)
    return dq, dkv, dkpe, None


mla_attention.defvjp(_mla_attention_fwd, _mla_attention_bwd)


def _rope_halves(x, cos, sin):
    half = x.shape[1] // 2
    x1, x2 = x[:, :half], x[:, half:]
    return jnp.concatenate([x1 * cos - x2 * sin, x1 * sin + x2 * cos], axis=1)


def _unrope_halves(d, cos, sin):
    half = d.shape[1] // 2
    d1, d2 = d[:, :half], d[:, half:]
    return jnp.concatenate([d1 * cos + d2 * sin, d2 * cos - d1 * sin], axis=1)


_RET_K_SCALE = RET_QK_DIM ** -0.5
_RET_Q_BLOCKS = RET_HEADS
_RET_V_BLOCK0 = 2 * RET_HEADS * RET_QK_DIM // RET_V_DIM
_RET_G_BLOCK0 = _RET_V_BLOCK0 + RET_HEADS


def _ret_specs(Tp):
    q = pl.BlockSpec((None, Tp, RET_QK_DIM), lambda b, h: (b, 0, h))
    k = pl.BlockSpec((None, Tp, RET_QK_DIM), lambda b, h: (b, 0, _RET_Q_BLOCKS + h))
    v = pl.BlockSpec((None, Tp, RET_V_DIM), lambda b, h: (b, 0, _RET_V_BLOCK0 + h))
    tab = pl.BlockSpec((Tp, RET_QK_DIM // 2), lambda b, h: (0, 0))
    lg = pl.BlockSpec((None, 1, 1), lambda b, h: (h, 0, 0))
    return q, k, v, tab, lg


def _ret_operands(q_ref, k_ref, cos, sin, lg):
    t = lax.broadcasted_iota(jnp.int32, (q_ref.shape[0], 1), 0).astype(F32)
    grow, shrink = jnp.exp(-lg * t), jnp.exp(lg * t)
    qs = (_rope_halves(q_ref[...].astype(F32), cos, sin) * shrink).astype(MXU_DTYPE)
    ks = (_rope_halves(k_ref[...].astype(F32), cos, sin) * (grow * _RET_K_SCALE)).astype(MXU_DTYPE)
    return qs, ks, shrink, grow * _RET_K_SCALE


def _ret_core_fwd_call(p, cos, sin, lg):
    B, Tp, _ = p.shape

    def body(q_ref, k_ref, v_ref, cos_ref, sin_ref, lg_ref, o_ref, qs_ref, ks_ref):
        qs_ref[...], ks_ref[...], _, _ = _ret_operands(q_ref, k_ref, cos_ref[...], sin_ref[...], lg_ref[...])
        for r0, L in _query_blocks(Tp):
            blk = slice(r0, L)
            s = _mask_diagonal(lax.dot_general(qs_ref[blk, :], ks_ref[0:L, :], _NT, preferred_element_type=F32), 0.0)
            o_ref[blk, :] = jnp.dot(s.astype(MXU_DTYPE), v_ref[0:L, :].astype(MXU_DTYPE), preferred_element_type=F32)

    q, k, v, tab, lgs = _ret_specs(Tp)
    return pl.pallas_call(
        body, name="retention_fwd", grid=(B, RET_HEADS), in_specs=[q, k, v, tab, tab, lgs],
        out_specs=pl.BlockSpec((None, Tp, RET_V_DIM), lambda b, h: (b, 0, h)),
        out_shape=jax.ShapeDtypeStruct((B, Tp, RET_HEADS * RET_V_DIM), F32),
        scratch_shapes=[pltpu.VMEM((Tp, RET_QK_DIM), MXU_DTYPE), pltpu.VMEM((Tp, RET_QK_DIM), MXU_DTYPE)],
        compiler_params=pltpu.CompilerParams(dimension_semantics=("parallel", "parallel")),
    )(p, p, p, cos, sin, lg)


def _ret_core_bwd_call(p, do, cos, sin, lg):
    B, Tp, _ = p.shape

    def body(q_ref, k_ref, v_ref, do_ref, cos_ref, sin_ref, lg_ref, dq_ref, dk_ref, dv_ref, qs_ref, ks_ref, dqa_ref, dka_ref, dva_ref):
        cos_, sin_ = cos_ref[...], sin_ref[...]
        qs_ref[...], ks_ref[...], q_scale, k_scale = _ret_operands(q_ref, k_ref, cos_, sin_, lg_ref[...])
        dka_ref[...] = jnp.zeros_like(dka_ref)
        dva_ref[...] = jnp.zeros_like(dva_ref)
        for r0, L in _query_blocks(Tp):
            blk = slice(r0, L)
            qb = qs_ref[blk, :]
            dob = do_ref[blk, :].astype(MXU_DTYPE)
            s = _mask_diagonal(lax.dot_general(qb, ks_ref[0:L, :], _NT, preferred_element_type=F32), 0.0).astype(MXU_DTYPE)
            dva_ref[0:L, :] += lax.dot_general(s, dob, _TN, preferred_element_type=F32)
            ds = _mask_diagonal(lax.dot_general(dob, v_ref[0:L, :].astype(MXU_DTYPE), _NT, preferred_element_type=F32), 0.0).astype(MXU_DTYPE)
            dqa_ref[blk, :] = jnp.dot(ds, ks_ref[0:L, :], preferred_element_type=F32)
            dka_ref[0:L, :] += lax.dot_general(ds, qb, _TN, preferred_element_type=F32)
        dq_ref[...] = _unrope_halves(dqa_ref[...] * q_scale, cos_, sin_).astype(dq_ref.dtype)
        dk_ref[...] = _unrope_halves(dka_ref[...] * k_scale, cos_, sin_).astype(dk_ref.dtype)
        dv_ref[...] = dva_ref[...].astype(dv_ref.dtype)

    q, k, v, tab, lgs = _ret_specs(Tp)
    qk_out = pl.BlockSpec((None, Tp, RET_QK_DIM), lambda b, h: (b, 0, h))
    v_out = pl.BlockSpec((None, Tp, RET_V_DIM), lambda b, h: (b, 0, h))
    return pl.pallas_call(
        body, name="retention_bwd", grid=(B, RET_HEADS), in_specs=[q, k, v, v_out, tab, tab, lgs],
        out_specs=[qk_out, qk_out, v_out],
        out_shape=[jax.ShapeDtypeStruct((B, Tp, RET_HEADS * RET_QK_DIM), p.dtype), jax.ShapeDtypeStruct((B, Tp, RET_HEADS * RET_QK_DIM), p.dtype),
                   jax.ShapeDtypeStruct((B, Tp, RET_HEADS * RET_V_DIM), p.dtype)],
        scratch_shapes=[pltpu.VMEM((Tp, RET_QK_DIM), MXU_DTYPE), pltpu.VMEM((Tp, RET_QK_DIM), MXU_DTYPE),
                        pltpu.VMEM((Tp, RET_QK_DIM), F32), pltpu.VMEM((Tp, RET_QK_DIM), F32), pltpu.VMEM((Tp, RET_V_DIM), F32)],
        compiler_params=pltpu.CompilerParams(dimension_semantics=("parallel", "parallel")),
    )(p, p, p, do, cos, sin, lg)


def _ret_gate_specs(M):
    tm = _pick(M, 1088, 8)
    head = pl.BlockSpec((tm, RET_V_DIM), lambda i, h: (i, h))
    gate = pl.BlockSpec((tm, RET_V_DIM), lambda i, h: (i, _RET_G_BLOCK0 + h))
    return tm, head, gate


def _ret_gate_fwd_call(o, p2d):
    M = o.shape[0]
    tm, head, gate = _ret_gate_specs(M)

    def body(o_ref, g_ref, y_ref):
        ov = o_ref[...]
        gv = g_ref[...].astype(F32)
        rstd = lax.rsqrt(jnp.mean(ov * ov, axis=-1, keepdims=True) + EPS)
        y_ref[...] = (gv * _sigmoid(gv)) * (ov * rstd)

    return pl.pallas_call(
        body, name="retention_gate_fwd", grid=(M // tm, RET_HEADS), in_specs=[head, gate], out_specs=head,
        out_shape=jax.ShapeDtypeStruct(o.shape, F32),
        compiler_params=pltpu.CompilerParams(dimension_semantics=("parallel", "parallel")),
    )(o, p2d)


def _ret_gate_bwd_call(o, p2d, dy):
    M = o.shape[0]
    tm, head, gate = _ret_gate_specs(M)

    def body(o_ref, g_ref, dy_ref, do_ref, dg_ref):
        ov = o_ref[...]
        gv = g_ref[...].astype(F32)
        dy = dy_ref[...]
        rstd = lax.rsqrt(jnp.mean(ov * ov, axis=-1, keepdims=True) + EPS)
        on = ov * rstd
        sg = _sigmoid(gv)
        dg_ref[...] = (dy * on * (sg * (1.0 + gv * (1.0 - sg)))).astype(dg_ref.dtype)
        don = dy * (gv * sg)
        do_ref[...] = (rstd * (don - on * jnp.mean(don * on, axis=-1, keepdims=True))).astype(do_ref.dtype)

    shp = jax.ShapeDtypeStruct(o.shape, p2d.dtype)
    return pl.pallas_call(
        body, name="retention_gate_bwd", grid=(M // tm, RET_HEADS), in_specs=[head, gate, head], out_specs=[head, head],
        out_shape=[shp, shp],
        compiler_params=pltpu.CompilerParams(dimension_semantics=("parallel", "parallel")),
    )(o, p2d, dy)


def _log_gamma():
    return jnp.log(1.0 - 2.0 ** (-5.0 - jnp.arange(RET_HEADS, dtype=F32))).reshape(RET_HEADS, 1, 1)


@jax.custom_vjp
def retention_mixer(p, cos, sin):
    B, Tp, W = p.shape
    o = _ret_core_fwd_call(p, cos, sin, _log_gamma())
    return _ret_gate_fwd_call(o.reshape(B * Tp, -1), p.reshape(B * Tp, W))


def _retention_mixer_fwd(p, cos, sin):
    B, Tp, W = p.shape
    o = _ret_core_fwd_call(p, cos, sin, _log_gamma())
    return _ret_gate_fwd_call(o.reshape(B * Tp, -1), p.reshape(B * Tp, W)), (p, o, cos, sin)


def _retention_mixer_bwd(res, dy):
    p, o, cos, sin = res
    B, Tp, W = p.shape
    do, dg = _ret_gate_bwd_call(o.reshape(B * Tp, -1), p.reshape(B * Tp, W), dy)
    dq, dk, dv = _ret_core_bwd_call(p, do.reshape(B, Tp, -1), cos, sin, _log_gamma())
    return jnp.concatenate([dq, dk, dv, dg.reshape(B, Tp, -1)], axis=-1), None, None


retention_mixer.defvjp(_retention_mixer_fwd, _retention_mixer_bwd)


def _heads_to_lanes(w):
    K = w.shape[0]
    w = w.reshape(K, MLA_HEADS, MLA_NOPE + MLA_ROPE)
    return jnp.pad(w, ((0, 0), (0, 0), (0, HEAD_LANES - MLA_NOPE - MLA_ROPE))).reshape(K, MLA_HEADS * HEAD_LANES)


def _out_rows_to_lanes(w):
    N = w.shape[1]
    att = w[LRU_WIDTH:].reshape(MLA_HEADS, MLA_V, N)
    att = jnp.pad(att, ((0, 0), (HEAD_LANES - MLA_V, 0), (0, 0))).reshape(MLA_HEADS * HEAD_LANES, N)
    return jnp.concatenate([w[:LRU_WIDTH], att], axis=0)


def _seq_dims(x):
    B, S, D = x.shape
    T = S + N_META
    Tp = _round_up(T, SEQ_BLOCK)
    return B, S, T, Tp


def _mixer0(diff, w, token):
    x = diff["x"]
    B, S, T, Tp = _seq_dims(x)
    D = x.shape[-1]
    M = B * Tp
    pos = jnp.arange(Tp, dtype=jnp.int32)

    def mm(a, name, act=False, out_dtype=F32, layout=lambda m: m, col_shards=1):
        return matmul(a, layout(w[name]), layout(diff[name]), act, name, out_dtype, col_shards)

    meta = jnp.broadcast_to(diff["meta_tokens"][None], (B, N_META, D))
    h = jnp.concatenate([meta, x + token, jnp.zeros((B, Tp - T, D), F32)], axis=1).reshape(M, D)
    p = mm(h, "ev_w_in")
    sp = jax.nn.softplus(-diff["ev_lru_lambda"]).reshape(1, LRU_WIDTH)
    y_rec, qn, kvn, kpe = even_front(
        p.reshape(B, Tp, -1), diff["ev_conv_w"].reshape(CONV_WIDTH, LRU_WIDTH), diff["ev_conv_b"].reshape(1, LRU_WIDTH),
        diff["ev_w_rg_a"].reshape(LRU_HEADS, LRU_HEAD_DIM, LRU_HEAD_DIM), diff["ev_b_rg_a"].reshape(1, LRU_WIDTH),
        diff["ev_w_rg_x"].reshape(LRU_HEADS, LRU_HEAD_DIM, LRU_HEAD_DIM), diff["ev_b_rg_x"].reshape(1, LRU_WIDTH),
        sp, diff["ev_q_norm_g"].reshape(-1), diff["ev_kv_norm_g"].reshape(-1))
    y_rec = y_rec.reshape(M, LRU_WIDTH)
    q = mm(qn, "ev_w_uq", out_dtype=MXU_DTYPE, layout=_heads_to_lanes).reshape(B, Tp, -1)
    kv = mm(kvn, "ev_w_ukv", out_dtype=MXU_DTYPE).reshape(B, Tp, -1)
    y_att = mla_attention(q, kv, kpe, _mla_rope_tables(pos)).reshape(M, -1)
    return out_block(h, jnp.concatenate([y_rec, y_att], axis=-1), _out_rows_to_lanes(w["ev_w_out"]), _out_rows_to_lanes(diff["ev_w_out"]),
                     diff["ln_mix_g"], diff["ln_mix_b"], "ev_w_out")


def _mlp0(diff, h, w):
    return mlp_block(h, w["mlp_w1_0"], w["mlp_w2_0"], diff["mlp_w1_0"], diff["mlp_w2_0"], diff["ln_mlp_g"], diff["ln_mlp_b"], "mlp0")


def _layer1_loss(diff, h, w, tgt):
    B, S, T, Tp = _seq_dims(tgt)
    D = tgt.shape[-1]
    pos = jnp.arange(Tp, dtype=jnp.int32)

    def mm(a, name, out_dtype=F32, col_shards=1):
        return matmul(a, w[name], diff[name], False, name, out_dtype, col_shards)

    p = mm(h, "od_w_in", out_dtype=MXU_DTYPE, col_shards=N_CHIPS)
    cos, sin = _rope_tables(pos, RET_QK_DIM // 2)
    h = out_block(h, retention_mixer(p.reshape(B, Tp, -1), cos, sin), w["od_w_out"], diff["od_w_out"],
                  diff["ln_mix_g"], diff["ln_mix_b"], "od_w_out")
    h = mlp_block(h, w["mlp_w1_1"], w["mlp_w2_1"], diff["mlp_w1_1"], diff["mlp_w2_1"], diff["ln_mlp_g"], diff["ln_mlp_b"], "mlp1")
    return loss_head(h.reshape(B, Tp, D), jnp.pad(tgt, ((0, 0), (N_META, Tp - T), (0, 0))), S)


_HBM = pl.BlockSpec(memory_space=pltpu.HBM)


def _place():
    return lax.axis_index("x"), lax.axis_index("y"), lax.axis_index("c")


def _other_chips(x, y):
    return [(1 - x, y), (x, 1 - y), (1 - x, 1 - y)]


def _chunks(rows, sublanes, most):
    for q in range(most, 0, -1):
        if rows % (q * sublanes) == 0:
            return q
    return 1


def _sublanes(dtype):
    return 8 * 4 // jnp.dtype(dtype).itemsize


def _gather_pieces(bufs):
    plan, first = [], []
    for b in bufs:
        Rh = b.shape[0] // 2
        Q = _chunks(Rh, _sublanes(b.dtype), 4) if Rh * b.shape[1] * b.dtype.itemsize > (1 << 20) else 1
        first.append(3 * sum(q for _, q, _ in plan))
        plan.append((Rh, Q, Rh // Q))
    return plan, first, 3 * sum(q for _, q, _ in plan)


def _allgather_chips(bufs, name):
    n = len(bufs)
    plan, first, n_sems = _gather_pieces(bufs)

    def body(*refs):
        x_refs, out_refs, (send_sems, recv_sems) = refs[:n], refs[n:2 * n], refs[2 * n:]
        x, y, c = _place()
        sibling = (x, y, 1 - c)
        chips = _other_chips(x, y)

        def copy(k, src, dst, to):
            return pltpu.make_async_remote_copy(src_ref=src, dst_ref=dst, send_sem=send_sems.at[k], recv_sem=recv_sems.at[k],
                                                device_id=to, device_id_type=MESH)

        def piece(i, cx, cy, hc, q):
            Rh, _, ch = plan[i]
            return out_refs[i].at[2 * cx + cy, pl.ds(hc * Rh + q * ch, ch), :]

        slots = [(i, q, j) for i in range(n) for q in range(plan[i][1]) for j in range(3)]
        sem = {(i, q, j): first[i] + 3 * q + j for i, q, j in slots}
        sent = [copy(sem[i, q, j], x_refs[i].at[pl.ds(c * plan[i][0] + q * plan[i][2], plan[i][2]), :], piece(i, x, y, c, q), (*chips[j], c))
                for i, q, j in slots]
        for cp in sent:
            cp.start()
        passed = []
        for i, q, j in slots:
            landed = piece(i, *chips[j], c, q)
            copy(sem[i, q, j], landed, landed, sibling).wait_recv()
            fwd = copy(n_sems + sem[i, q, j], landed, landed, sibling)
            fwd.start()
            passed.append(fwd)
        for i, q, j in slots:
            theirs = piece(i, *chips[j], 1 - c, q)
            copy(n_sems + sem[i, q, j], theirs, theirs, sibling).wait_recv()
        for cp in sent + passed:
            cp.wait_send()

    return pl.pallas_call(
        body, name=name, in_specs=[_HBM] * n, out_specs=[_HBM] * n,
        out_shape=[jax.ShapeDtypeStruct((N_CHIPS,) + b.shape, b.dtype) for b in bufs],
        scratch_shapes=[pltpu.SemaphoreType.DMA((2 * n_sems,)), pltpu.SemaphoreType.DMA((2 * n_sems,))],
    )(*bufs)


def _with_own(gathered, own):
    my = 2 * lax.axis_index("x") + lax.axis_index("y")
    return lax.dynamic_update_slice(gathered, own[None], (my, 0, 0))


def _sibling_exchange(ps, name):
    n = len(ps)

    def body(*refs):
        p_refs, out_refs, (send_sems, recv_sems) = refs[:n], refs[n:2 * n], refs[2 * n:]
        x, y, c = _place()
        copies = [pltpu.make_async_remote_copy(src_ref=p_ref.at[j, 1 - c], dst_ref=out_ref.at[j], send_sem=send_sems.at[N_CHIPS * i + j],
                                               recv_sem=recv_sems.at[N_CHIPS * i + j], device_id=(x, y, 1 - c), device_id_type=MESH)
                  for i, (p_ref, out_ref) in enumerate(zip(p_refs, out_refs)) for j in range(N_CHIPS)]
        for cp in copies:
            cp.start()
        for cp in copies:
            cp.wait()

    return pl.pallas_call(
        body, name=name, in_specs=[_HBM] * n, out_specs=[_HBM] * n,
        out_shape=[jax.ShapeDtypeStruct((N_CHIPS,) + p.shape[2:], p.dtype) for p in ps],
        scratch_shapes=[pltpu.SemaphoreType.DMA((N_CHIPS * n,)), pltpu.SemaphoreType.DMA((N_CHIPS * n,))],
    )(*ps)


def _chip_scatter(ss, name):
    n = len(ss)

    def body(*refs):
        s_refs, t_refs, (send_sems, recv_sems) = refs[:n], refs[n:2 * n], refs[2 * n:]
        x, y, c = _place()
        copies = [pltpu.make_async_remote_copy(src_ref=s_ref.at[j + 1], dst_ref=t_ref.at[j], send_sem=send_sems.at[3 * i + j],
                                               recv_sem=recv_sems.at[3 * i + j], device_id=(cx, cy, c), device_id_type=MESH)
                  for i, (s_ref, t_ref) in enumerate(zip(s_refs, t_refs)) for j, (cx, cy) in enumerate(_other_chips(x, y))]
        for cp in copies:
            cp.start()
        for cp in copies:
            cp.wait()

    return pl.pallas_call(
        body, name=name, in_specs=[_HBM] * n, out_specs=[_HBM] * n,
        out_shape=[jax.ShapeDtypeStruct((3,) + s.shape[1:], s.dtype) for s in ss],
        scratch_shapes=[pltpu.SemaphoreType.DMA((3 * n,)), pltpu.SemaphoreType.DMA((3 * n,))],
    )(*ss)


def _sibling_gather(fs, name):
    n = len(fs)

    def body(*refs):
        out_refs, (send_sems, recv_sems) = refs[n:2 * n], refs[2 * n:]
        x, y, c = _place()
        copies = [pltpu.make_async_remote_copy(src_ref=out_ref.at[c], dst_ref=out_ref.at[c], send_sem=send_sems.at[i], recv_sem=recv_sems.at[i],
                                               device_id=(x, y, 1 - c), device_id_type=MESH) for i, out_ref in enumerate(out_refs)]
        for cp in copies:
            cp.start()
        for cp in copies:
            cp.wait()

    return pl.pallas_call(
        body, name=name, in_specs=[_HBM] * n, out_specs=[_HBM] * n,
        out_shape=[jax.ShapeDtypeStruct(f.shape, f.dtype) for f in fs], input_output_aliases={i: i for i in range(n)},
        scratch_shapes=[pltpu.SemaphoreType.DMA((n,)), pltpu.SemaphoreType.DMA((n,))],
    )(*fs)


def _axis_scalar(name):
    return lax.axis_index(name).astype(jnp.int32).reshape(1)


def _add_own_half(p, got, out_dtype, name):
    n, _, R, C = p.shape
    tr = _pick(R, 512, 16)

    def body(x_ref, y_ref, c_ref, p_ref, g_ref, o_ref):
        o_ref[...] = (p_ref[...] + g_ref[...]).astype(out_dtype)

    def chip(r, x_ref, y_ref):
        return 2 * (x_ref[0] ^ (r & 1)) + (y_ref[0] ^ (r >> 1))

    grid_spec = pltpu.PrefetchScalarGridSpec(
        num_scalar_prefetch=3, grid=(n, R // tr),
        in_specs=[pl.BlockSpec((None, None, tr, C), lambda r, i, x_ref, y_ref, c_ref: (chip(r, x_ref, y_ref), c_ref[0], i, 0)),
                  pl.BlockSpec((None, tr, C), lambda r, i, x_ref, y_ref, c_ref: (chip(r, x_ref, y_ref), i, 0))],
        out_specs=pl.BlockSpec((None, tr, C), lambda r, i, x_ref, y_ref, c_ref: (r, i, 0)))
    return pl.pallas_call(body, name=name, grid_spec=grid_spec, out_shape=jax.ShapeDtypeStruct((n, R, C), out_dtype),
                          compiler_params=pltpu.CompilerParams(dimension_semantics=("parallel", "parallel")))(
        _axis_scalar("x"), _axis_scalar("y"), _axis_scalar("c"), p, got)


def _sum_partials(s, t, name):
    _, R, C = s.shape
    tr = _pick(R, 512, 16)

    def body(c_ref, s_ref, t_ref, o_ref):
        acc = s_ref[...].astype(F32)
        for j in range(3):
            acc = acc + t_ref[j].astype(F32)
        o_ref[...] = acc

    grid_spec = pltpu.PrefetchScalarGridSpec(
        num_scalar_prefetch=1, grid=(R // tr,),
        in_specs=[pl.BlockSpec((None, tr, C), lambda i, c_ref: (0, i, 0)), pl.BlockSpec((3, tr, C), lambda i, c_ref: (0, i, 0))],
        out_specs=pl.BlockSpec((None, tr, C), lambda i, c_ref: (c_ref[0], i, 0)))
    return pl.pallas_call(body, name=name, grid_spec=grid_spec, out_shape=jax.ShapeDtypeStruct((2, R, C), F32),
                          compiler_params=pltpu.CompilerParams(dimension_semantics=("parallel",)))(_axis_scalar("c"), s, t)


def _sibling_reduce(ps, wire_dtypes, tag):
    got = _sibling_exchange(ps, "grad_sibling_exchange_" + tag)
    return [_add_own_half(p, g, dt, "grad_sibling_add_%s%d" % (tag, i)) for i, (p, g, dt) in enumerate(zip(ps, got, wire_dtypes))]


_SEM = pl.BlockSpec(memory_space=pltpu.SEMAPHORE)
_ANY = pl.BlockSpec(memory_space=pl.ANY)
_EFFECT = pltpu.SideEffectType.DATAFLOW_SIDE_EFFECTING


def _in_hbm(a):
    return pltpu.with_memory_space_constraint(a, pltpu.HBM)


def _half_copies(x_refs, land_refs, send_sems, recv_sems, arriving):
    x, y, c = _place()
    copies = []
    for i, (x_ref, land_ref) in enumerate(zip(x_refs, land_refs)):
        Rh = x_ref.shape[0] // 2
        rows = pl.ds(c * Rh, Rh)
        for j, (cx, cy) in enumerate(_other_chips(x, y)):
            copies.append(pltpu.make_async_remote_copy(
                src_ref=x_ref.at[rows, :], dst_ref=land_ref.at[2 * cx + cy if arriving else 2 * x + y, rows, :],
                send_sem=send_sems.at[3 * i + j], recv_sem=recv_sems.at[3 * i + j], device_id=(cx, cy, c), device_id_type=MESH))
    return copies


def _allgather_start(bufs, name):
    n = len(bufs)

    def body(*refs):
        x_refs, land_refs, (send_sems, recv_sems), token = refs[:n], refs[n:2 * n], refs[2 * n:2 * n + 2], refs[-1]
        for cp in _half_copies(x_refs, land_refs, send_sems, recv_sems, False):
            cp.start()
        token[...] = jnp.zeros_like(token)

    lands = [lax.empty((N_CHIPS,) + b.shape, b.dtype) for b in bufs]
    out = pl.pallas_call(
        body, name=name,
        out_shape=(pltpu.SemaphoreType.DMA((3 * n,)), pltpu.SemaphoreType.DMA((3 * n,)), *[pltpu.HBM(a.shape, a.dtype) for a in bufs + lands],
                   jax.ShapeDtypeStruct((8, 128), F32)),
        in_specs=[_HBM] * (2 * n), out_specs=(_SEM, _SEM, *[_HBM] * (2 * n), pl.BlockSpec(memory_space=pltpu.VMEM)),
        input_output_aliases={i: 2 + i for i in range(2 * n)}, compiler_params=pltpu.CompilerParams(has_side_effects=_EFFECT),
    )(*[_in_hbm(a) for a in bufs + lands])
    return (out[0], out[1], list(out[2:2 + n]), list(out[2 + n:2 + 2 * n])), out[-1][0, 0]


def _allgather_wait(pending, after, name):
    send_sems, recv_sems, bufs, lands = pending
    n = len(bufs)

    def body(*refs):
        x_refs, land_refs, send_sems, recv_sems = refs[:n], refs[n:2 * n], refs[2 * n], refs[2 * n + 1]
        for cp in _half_copies(x_refs, land_refs, send_sems, recv_sems, False):
            cp.wait_send()
        for cp in _half_copies(x_refs, land_refs, send_sems, recv_sems, True):
            cp.wait_recv()

    out = pl.pallas_call(
        body, name=name, out_shape=tuple(pltpu.HBM(a.shape, a.dtype) for a in bufs + lands),
        in_specs=[_HBM] * (2 * n) + [_SEM, _SEM, _ANY], out_specs=tuple([_HBM] * (2 * n)), input_output_aliases={i: i for i in range(2 * n)},
        compiler_params=pltpu.CompilerParams(has_side_effects=_EFFECT),
    )(*bufs, *lands, send_sems, recv_sems, after)
    return list(out[n:])


def _sibling_forward(lands, name):
    n = len(lands)
    plan, first, n_sems = _gather_pieces([jax.ShapeDtypeStruct(l.shape[1:], l.dtype) for l in lands])

    def body(*refs):
        out_refs, (send_sems, recv_sems) = refs[n:2 * n], refs[2 * n:]
        x, y, c = _place()

        def copies(hc):
            return [pltpu.make_async_remote_copy(
                        src_ref=out_refs[i].at[2 * cx + cy, pl.ds(hc * plan[i][0] + q * plan[i][2], plan[i][2]), :],
                        dst_ref=out_refs[i].at[2 * cx + cy, pl.ds(hc * plan[i][0] + q * plan[i][2], plan[i][2]), :],
                        send_sem=send_sems.at[first[i] + 3 * q + j], recv_sem=recv_sems.at[first[i] + 3 * q + j],
                        device_id=(x, y, 1 - c), device_id_type=MESH)
                    for i in range(n) for q in range(plan[i][1]) for j, (cx, cy) in enumerate(_other_chips(x, y))]

        mine = copies(c)
        for cp in mine:
            cp.start()
        for cp in mine:
            cp.wait_send()
        for cp in copies(1 - c):
            cp.wait_recv()

    return pl.pallas_call(
        body, name=name, in_specs=[_HBM] * n, out_specs=[_HBM] * n, out_shape=[jax.ShapeDtypeStruct(l.shape, l.dtype) for l in lands],
        input_output_aliases={i: i for i in range(n)},
        scratch_shapes=[pltpu.SemaphoreType.DMA((n_sems,)), pltpu.SemaphoreType.DMA((n_sems,))],
    )(*lands)


N_PEERS = 7


def _direct_copies(p_refs, t_refs, send_sems, recv_sems):
    x, y, c = _place()
    copies = []
    for i, (p_ref, t_ref) in enumerate(zip(p_refs, t_refs)):
        for f in range(1, N_PEERS + 1):
            px, py, pc = x ^ (f >> 2), y ^ ((f >> 1) & 1), c ^ (f & 1)
            copies.append(pltpu.make_async_remote_copy(
                src_ref=p_ref.at[2 * px + py, pc], dst_ref=t_ref.at[f - 1], send_sem=send_sems.at[N_PEERS * i + f - 1],
                recv_sem=recv_sems.at[N_PEERS * i + f - 1], device_id=(px, py, pc), device_id_type=MESH))
    return copies


def _direct_scatter_start(ps, name):
    n = len(ps)

    def body(*refs):
        p_refs, t_refs, (send_sems, recv_sems), token = refs[:n], refs[n:2 * n], refs[2 * n:2 * n + 2], refs[-1]
        for cp in _direct_copies(p_refs, t_refs, send_sems, recv_sems):
            cp.start()
        token[...] = jnp.zeros_like(token)

    lands = [lax.empty((N_PEERS,) + p.shape[2:], p.dtype) for p in ps]
    out = pl.pallas_call(
        body, name=name,
        out_shape=(pltpu.SemaphoreType.DMA((N_PEERS * n,)), pltpu.SemaphoreType.DMA((N_PEERS * n,)),
                   *[pltpu.HBM(a.shape, a.dtype) for a in ps + lands], jax.ShapeDtypeStruct((8, 128), F32)),
        in_specs=[_HBM] * (2 * n), out_specs=(_SEM, _SEM, *[_HBM] * (2 * n), pl.BlockSpec(memory_space=pltpu.VMEM)),
        input_output_aliases={i: 2 + i for i in range(2 * n)}, compiler_params=pltpu.CompilerParams(has_side_effects=_EFFECT),
    )(*[_in_hbm(a) for a in ps + lands])
    return (out[0], out[1], list(out[2:2 + n]), list(out[2 + n:2 + 2 * n])), out[-1][0, 0]


def _direct_scatter_wait(pending, after, name):
    send_sems, recv_sems, ps, lands = pending
    n = len(ps)

    def body(*refs):
        p_refs, t_refs, send_sems, recv_sems = refs[:n], refs[n:2 * n], refs[2 * n], refs[2 * n + 1]
        for cp in _direct_copies(p_refs, t_refs, send_sems, recv_sems):
            cp.wait_send()
            cp.wait_recv()

    out = pl.pallas_call(
        body, name=name, out_shape=tuple(pltpu.HBM(a.shape, a.dtype) for a in ps + lands),
        in_specs=[_HBM] * (2 * n) + [_SEM, _SEM, _ANY], out_specs=tuple([_HBM] * (2 * n)),
        input_output_aliases={i: i for i in range(2 * n)}, compiler_params=pltpu.CompilerParams(has_side_effects=_EFFECT),
    )(*ps, *lands, send_sems, recv_sems, after)
    return list(out[:n]), list(out[n:])


def _sum_direct(p, t, name):
    _, _, R, C = p.shape
    tr = _pick(R, 512, 16)

    def body(x_ref, y_ref, c_ref, p_ref, t_ref, o_ref):
        acc = p_ref[...].astype(F32)
        for f in range(N_PEERS):
            acc = acc + t_ref[f].astype(F32)
        o_ref[...] = acc

    grid_spec = pltpu.PrefetchScalarGridSpec(
        num_scalar_prefetch=3, grid=(R // tr,),
        in_specs=[pl.BlockSpec((None, None, tr, C), lambda i, x_ref, y_ref, c_ref: (2 * x_ref[0] + y_ref[0], c_ref[0], i, 0)),
                  pl.BlockSpec((N_PEERS, tr, C), lambda i, x_ref, y_ref, c_ref: (0, i, 0))],
        out_specs=pl.BlockSpec((None, tr, C), lambda i, x_ref, y_ref, c_ref: (c_ref[0], i, 0)))
    return pl.pallas_call(body, name=name, grid_spec=grid_spec, out_shape=jax.ShapeDtypeStruct((2, R, C), F32),
                          compiler_params=pltpu.CompilerParams(dimension_semantics=("parallel",)))(
        _axis_scalar("x"), _axis_scalar("y"), _axis_scalar("c"), p, t)


def _adamw(w, g, m, v, name):
    R, C = w.shape
    tr = _pick(R, 256, 8)

    def body(w_ref, g_ref, m_ref, v_ref, d_ref, nm_ref, nv_ref):
        g_ = g_ref[...]
        m_ = ADAM_B1 * m_ref[...] + (1.0 - ADAM_B1) * g_
        v_ = ADAM_B2 * v_ref[...] + (1.0 - ADAM_B2) * (g_ * g_)
        m_hat = m_ / (1.0 - ADAM_B1 ** ADAM_STEP)
        v_hat = v_ / (1.0 - ADAM_B2 ** ADAM_STEP)
        d_ref[...] = -ADAM_LR * (m_hat / (jnp.sqrt(v_hat) + ADAM_EPS) + ADAM_WD * w_ref[...])
        nm_ref[...] = m_
        nv_ref[...] = v_

    row = pl.BlockSpec((tr, C), lambda i: (i, 0))
    shp = jax.ShapeDtypeStruct((R, C), F32)
    return pl.pallas_call(body, name=name, grid=(R // tr,), in_specs=[row] * 4, out_specs=[row] * 3, out_shape=[shp] * 3,
                          compiler_params=pltpu.CompilerParams(dimension_semantics=("parallel",)))(w, g, m, v)


BIG_SPECS = (("ev_w_in", 1024, 1440, 1), ("ev_w_uq", 256, 768, 1), ("ev_w_ukv", 128, 1024, 1), ("ev_w_out", 1024, 1024, 0),
             ("od_w_in", 1024, 6144, 1), ("od_w_out", 2048, 1024, 0), ("mlp_w1_0", 1024, 4096, 1), ("mlp_w1_1", 1024, 4096, 1),
             ("mlp_w2_0", 4096, 1024, 0), ("mlp_w2_1", 4096, 1024, 0))
BIG_PARAMS = (("ev_w_in", ("ev_w_in",)), ("ev_w_uq", ("ev_w_uq",)), ("ev_w_ukv", ("ev_w_ukv",)), ("ev_w_out", ("ev_w_out",)),
              ("od_w_in", ("od_w_in",)), ("od_w_out", ("od_w_out",)), ("mlp_w1", ("mlp_w1_0", "mlp_w1_1")),
              ("mlp_w2", ("mlp_w2_0", "mlp_w2_1")))
REPLICATED = ("ev_conv_b", "ev_w_rg_a", "ev_b_rg_a", "ev_w_rg_x", "ev_b_rg_x", "ev_lru_lambda", "ev_q_norm_g", "ev_kv_norm_g",
              "ln_mix_g", "ln_mix_b", "ln_mlp_g", "ln_mlp_b")
SMALL_SHARDED = ("meta_tokens", "ev_conv_w")
COL_SHARD_GRADS = ("od_w_in", "mlp_w1_0", "mlp_w1_1")
MATRIX_GROUPS = (("ev_w_in", "ev_w_uq", "ev_w_ukv", "ev_w_out"), ("mlp_w1_0", "mlp_w2_0"), ("od_w_in", "od_w_out", "mlp_w1_1", "mlp_w2_1"))
LAYER_NORMS = ("ln_mix_g", "ln_mix_b", "ln_mlp_g", "ln_mlp_b")
WEIGHT_NAMES = ("meta_tokens", "ev_w_in", "ev_conv_w", "ev_conv_b", "ev_w_rg_a", "ev_b_rg_a", "ev_w_rg_x", "ev_b_rg_x",
                "ev_lru_lambda", "ev_q_norm_g", "ev_w_uq", "ev_kv_norm_g", "ev_w_ukv", "ev_w_out", "od_w_in", "od_w_out",
                "ln_mix_g", "ln_mix_b", "mlp_w1", "mlp_w2", "ln_mlp_g", "ln_mlp_b")


def _to_rows(flat, row_align):
    n = flat.shape[-1]
    rows = _round_up(-(-n // PACK_COLS), row_align)
    pad = rows * PACK_COLS - n
    if pad:
        flat = jnp.pad(flat, [(0, 0)] * (flat.ndim - 1) + [(0, pad)])
    return flat.reshape(flat.shape[:-1] + (rows, PACK_COLS))


def _shard_shape(K, N, axis):
    return (K // N_CHIPS, N) if axis == 0 else (K, N // N_CHIPS)


def _gather_shards(stacked, K, N, axis):
    if axis == 0:
        return stacked.reshape(K, N)
    return stacked.transpose(1, 0, 2).reshape(K, N)


def _split_shards(full, K, N, axis):
    if axis == 0:
        return full.reshape(N_CHIPS, -1)
    return full.reshape(K, N_CHIPS, N // N_CHIPS).transpose(1, 0, 2).reshape(N_CHIPS, -1)


def kernel(x, meta_tokens, ev_w_in, ev_conv_w, ev_conv_b, ev_w_rg_a, ev_b_rg_a, ev_w_rg_x, ev_b_rg_x, ev_lru_lambda, ev_q_norm_g, ev_w_uq, ev_kv_norm_g, ev_w_ukv, ev_w_out, od_w_in, od_w_out, ln_mix_g, ln_mix_b, mlp_w1, mlp_w2, ln_mlp_g, ln_mlp_b, loss_target, m_meta_tokens, m_ev_w_in, m_ev_conv_w, m_ev_conv_b, m_ev_w_rg_a, m_ev_b_rg_a, m_ev_w_rg_x, m_ev_b_rg_x, m_ev_lru_lambda, m_ev_q_norm_g, m_ev_w_uq, m_ev_kv_norm_g, m_ev_w_ukv, m_ev_w_out, m_od_w_in, m_od_w_out, m_ln_mix_g, m_ln_mix_b, m_mlp_w1, m_mlp_w2, m_ln_mlp_g, m_ln_mlp_b, v_meta_tokens, v_ev_w_in, v_ev_conv_w, v_ev_conv_b, v_ev_w_rg_a, v_ev_b_rg_a, v_ev_w_rg_x, v_ev_b_rg_x, v_ev_lru_lambda, v_ev_q_norm_g, v_ev_w_uq, v_ev_kv_norm_g, v_ev_w_ukv, v_ev_w_out, v_od_w_in, v_od_w_out, v_ln_mix_g, v_ln_mix_b, v_mlp_w1, v_mlp_w2, v_ln_mlp_g, v_ln_mlp_b):
    given = dict(locals())
    local_big = {"ev_w_in": ev_w_in[0], "ev_w_uq": ev_w_uq[0], "ev_w_ukv": ev_w_ukv[0], "ev_w_out": ev_w_out[0],
                 "od_w_in": od_w_in[0], "od_w_out": od_w_out[0], "mlp_w1_0": mlp_w1[0], "mlp_w1_1": mlp_w1[1],
                 "mlp_w2_0": mlp_w2[0], "mlp_w2_1": mlp_w2[1]}

    specs = {spec[0]: spec for spec in BIG_SPECS}
    mixer0_m, mlp0_m, layer1_m = MATRIX_GROUPS

    def shards(names):
        return [local_big[n].astype(MXU_DTYPE) for n in names]

    def whole(stacked, n):
        _, K, N, ax = specs[n]
        return stacked if n in COL_SHARD_GRADS else _gather_shards(stacked, K, N, ax)

    def filled(gathered, own, names):
        return {n: whole(_with_own(g_, o_), n) for n, g_, o_ in zip(names, gathered, own)}

    own_a, own_b, own_c = shards(mixer0_m), shards(mlp0_m), shards(layer1_m)
    small = [meta_tokens, jnp.pad(ev_conv_w[0], ((0, 16 - CONV_WIDTH), (0, 0)))]
    gathered_a = _allgather_chips(own_a + small, "weight_allgather_mixer0")
    pending_b, token1 = _allgather_start(own_b, "weight_allgather_mlp0_start")
    pending_c, token2 = _allgather_start(own_c, "weight_allgather_layer1_start")
    meta_full = _gather_shards(_with_own(gathered_a[-2], small[0]), N_META, D_MODEL, 1)
    conv_full = _gather_shards(_with_own(gathered_a[-1], small[1])[:, :CONV_WIDTH], CONV_WIDTH, LRU_WIDTH, 1)

    def slots(names, dtype):
        return {n: jnp.zeros((N_CHIPS, specs[n][1], specs[n][2] // N_CHIPS) if n in COL_SHARD_GRADS else specs[n][1:3], dtype) for n in names}

    def norms(names, layer):
        return {n: given[n][layer] for n in names}

    def finish_gather(pending, own, after, names, tag):
        landed = _allgather_wait(pending, lax.stop_gradient(after), "weight_allgather_%s_wait" % tag)
        return filled(_sibling_forward(landed, "weight_allgather_%s_forward" % tag), own, names)

    diff_a = {**slots(mixer0_m, F32), **norms(("ln_mix_g", "ln_mix_b"), 0), **{n: given[n] for n in REPLICATED if n not in LAYER_NORMS},
              "x": x, "meta_tokens": meta_full, "ev_conv_w": conv_full}
    diff_b = {**slots(mlp0_m, MXU_DTYPE), **norms(("ln_mlp_g", "ln_mlp_b"), 0)}
    diff_c = {**slots(layer1_m, MXU_DTYPE), **norms(LAYER_NORMS, 1)}
    w_a = filled(gathered_a[:len(mixer0_m)], own_a, mixer0_m)
    h_a, back_a = jax.vjp(lambda d: _mixer0(d, w_a, token1 + token2), diff_a)
    w_b = finish_gather(pending_b, own_b, h_a, mlp0_m, "mlp0")
    h_b, back_b = jax.vjp(lambda d, hh: _mlp0(d, hh, w_b), diff_b, h_a)
    w_c = finish_gather(pending_c, own_c, h_b, layer1_m, "layer1")
    loss, back_c = jax.vjp(lambda d, hh: _layer1_loss(d, hh, w_c, loss_target), diff_c, h_b)
    loss = lax.psum(loss, ("x", "y", "c"))

    def blocks_of(grad, n):
        _, K, N, ax = specs[n]
        if n in COL_SHARD_GRADS:
            blocks = grad
        elif ax == 0:
            blocks = grad.reshape(N_CHIPS, K // N_CHIPS, N)
        else:
            blocks = grad.reshape(K, N_CHIPS, N // N_CHIPS).transpose(1, 0, 2)
        return blocks.reshape(N_CHIPS, 2, blocks.shape[1] // 2, blocks.shape[2])

    def start_reduce(grads_of, names, tag):
        return _direct_scatter_start([blocks_of(grads_of[n], n) for n in names], "grad_scatter_%s_start" % tag)

    g_c, dh = back_c(jnp.ones((), F32))
    flying_c, token = start_reduce(g_c, layer1_m, "layer1")
    g_b, dh = back_b(dh + token)
    flying_b, token = start_reduce(g_b, mlp0_m, "mlp0")
    (g_a,) = back_a(dh + token)
    ps_c, ts_c = _direct_scatter_wait(flying_c, g_a["x"], "grad_scatter_layer1_wait")
    ps_b, ts_b = _direct_scatter_wait(flying_b, g_a["x"], "grad_scatter_mlp0_wait")

    g = {**g_a, **g_b, **g_c}
    g.update({n: jnp.stack([(g_b if n in g_b else g_a)[n], g_c[n]]) for n in LAYER_NORMS})
    repl = jnp.concatenate([g[n].reshape(-1) for n in REPLICATED]).reshape(N_CHIPS, -1)
    small = [_split_shards(g["meta_tokens"], N_META, D_MODEL, 1), _split_shards(g["ev_conv_w"], CONV_WIDTH, LRU_WIDTH, 1), repl]
    small = [pc.reshape(N_CHIPS, 2, -1) for pc in small]
    n_small = sum(pc.shape[2] for pc in small)
    small.append(jnp.zeros((N_CHIPS, 2, _round_up(n_small, 32 * PACK_COLS) - n_small), F32))
    p_small = jnp.concatenate(small, axis=2).reshape(N_CHIPS, 2, -1, PACK_COLS)
    ss_a = _sibling_reduce([blocks_of(g_a[n], n) for n in mixer0_m] + [p_small], [MXU_DTYPE] * len(mixer0_m) + [F32], "mixer0_")
    ts_a = list(_chip_scatter(ss_a, "grad_chip_scatter_mixer0"))
    fs = [_sum_partials(s, t, "grad_chip_sum_mixer0_%d" % i) for i, (s, t) in enumerate(zip(ss_a, ts_a))]
    fs += [_sum_direct(p, t, "grad_sum_%d" % i) for i, (p, t) in enumerate(zip(ps_b + ps_c, ts_b + ts_c))]
    reduced = _sibling_gather(fs, "grad_sibling_gather")
    red_big = dict(zip(mixer0_m + ("small",) + mlp0_m + layer1_m, reduced))
    red_small = red_big.pop("small").reshape(2, -1)

    grads = {}
    for name, parts in BIG_PARAMS:
        grads[name] = jnp.stack([red_big[part].reshape(given[name].shape[1:]) for part in parts])

    def take(off, sz):
        return jnp.concatenate([red_small[0, off // 2:(off + sz) // 2], red_small[1, off // 2:(off + sz) // 2]])

    off = 0
    for name in SMALL_SHARDED:
        sz = given[name].size
        grads[name] = take(off, sz).reshape(given[name].shape)
        off += sz
    n_repl = repl.shape[1]
    own_repl = _to_rows(take(off, n_repl), 16)
    repl_all = _with_own(_allgather_chips([own_repl], "replicated_allgather")[0], own_repl).reshape(N_CHIPS, -1)[:, :n_repl].reshape(-1)
    off = 0
    for name in REPLICATED:
        sz = given[name].size
        grads[name] = repl_all[off:off + sz].reshape(given[name].shape)
        off += sz

    delta, new_m, new_v = {}, {}, {}
    for name, _ in BIG_PARAMS:
        shp = given[name].shape
        two_d = (-1, shp[-1])
        d, nm, nv = _adamw(given[name].reshape(two_d), grads[name].reshape(two_d), given["m_" + name].reshape(two_d),
                           given["v_" + name].reshape(two_d), "adamw_" + name)
        delta[name], new_m[name], new_v[name] = d.reshape(shp), nm.reshape(shp), nv.reshape(shp)
    smalls = SMALL_SHARDED + REPLICATED

    def pack_small(get):
        return _to_rows(jnp.concatenate([get(n).reshape(-1) for n in smalls]), 8)

    outs = _adamw(pack_small(lambda n: given[n]), pack_small(lambda n: grads[n]), pack_small(lambda n: given["m_" + n]),
                  pack_small(lambda n: given["v_" + n]), "adamw_small")
    for res, flat in zip((delta, new_m, new_v), outs):
        flat, off = flat.reshape(-1), 0
        for n in smalls:
            sz = given[n].size
            res[n] = flat[off:off + sz].reshape(given[n].shape)
            off += sz

    return (loss, g_a["x"], *[grads[n] for n in WEIGHT_NAMES], *[delta[n] for n in WEIGHT_NAMES],
            *[new_m[n] for n in WEIGHT_NAMES], *[new_v[n] for n in WEIGHT_NAMES])
```

```python
import functools
import math

import jax
import jax.numpy as jnp
from jax import lax
from jax.experimental import pallas as pl
from jax.experimental.pallas import tpu as pltpu

F32 = jnp.float32
MXU_DTYPE = jnp.bfloat16

D_MODEL = 1024
N_META = 16
LRU_WIDTH = 512
LRU_HEADS = 4
LRU_HEAD_DIM = 128
CONV_WIDTH = 4
LRU_C = 8.0
MLA_HEADS = 8
MLA_NOPE = 64
MLA_ROPE = 32
MLA_V = 64
MLA_Q_RANK = 256
MLA_KV_RANK = 128
RET_HEADS = 4
RET_QK_DIM = 256
RET_V_DIM = 512
D_FF = 4096
ROPE_BASE = 10000.0
DN_ALPHA = 4.0 ** 0.25
EPS = 1e-5
NEG_INF = -1e30
SEQ_BLOCK = 128

ADAM_LR = 0.001
ADAM_B1 = 0.9
ADAM_B2 = 0.999
ADAM_EPS = 1e-08
ADAM_WD = 0.01
ADAM_STEP = 10

PACK_COLS = 1024
N_CHIPS = 4

MESH = pl.DeviceIdType.MESH


def _pick(n, target, align):
    best = None
    for t in range(align, min(n, target) + 1, align):
        if n % t == 0:
            best = t
    return n if best is None else best


def _round_up(n, m):
    return (n + m - 1) // m * m


def _relu2(a):
    r = jnp.maximum(a, 0.0)
    return r * r


def _ln_stats(z):
    mu = jnp.mean(z, axis=-1, keepdims=True)
    zc = z - mu
    var = jnp.mean(zc * zc, axis=-1, keepdims=True)
    return zc, lax.rsqrt(var + EPS)


def _mm_nn(a, w, act, name, out_dtype=F32, norm=None):
    pair = isinstance(a, (tuple, list))
    parts = list(a) if pair else [a]
    a = parts[0]
    M = a.shape[0]
    K = sum(p.shape[1] for p in parts)
    sharded = w.ndim == 3
    n = w.shape[-1]
    N = n * (w.shape[0] if sharded else 1)
    tm = _pick(M, 1088 if K * a.dtype.itemsize <= 4096 and norm is None else 544, 8)
    tn = _pick(n, 1024, 128)
    per = n // tn
    assert norm is None or tn == N

    def body(*refs):
        a_refs, w_ref, rest = refs[:len(parts)], refs[len(parts)], refs[len(parts) + 1:]
        r, k0 = None, 0
        for a_ref in a_refs:
            av = a_ref[...]
            if act:
                av = _relu2(av.astype(F32))
            k1 = k0 + a_ref.shape[1]
            d = jnp.dot(av.astype(MXU_DTYPE), w_ref[k0:k1, :].astype(MXU_DTYPE), preferred_element_type=F32)
            r, k0 = d if r is None else r + d, k1
        if norm is None:
            rest[0][...] = r.astype(out_dtype)
        else:
            r_ref, g_ref, b_ref, o_ref, z_ref = rest
            z = DN_ALPHA * r_ref[...] + r
            zc, rstd = _ln_stats(z)
            z_ref[...] = z
            o_ref[...] = zc * rstd * g_ref[...] + b_ref[...]

    w_spec = pl.BlockSpec((None, K, tn), lambda i, j: (j // per, 0, j % per)) if sharded else pl.BlockSpec((K, tn), lambda i, j: (0, j))
    tile = pl.BlockSpec((tm, tn), lambda i, j: (i, j))
    in_specs = [pl.BlockSpec((tm, p.shape[1]), lambda i, j: (i, 0)) for p in parts] + [w_spec]
    args = parts + [w]
    if norm is None:
        out_specs, out_shape = tile, jax.ShapeDtypeStruct((M, N), out_dtype)
    else:
        vec = pl.BlockSpec((1, N), lambda i, j: (0, 0))
        in_specs += [tile, vec, vec]
        args += [norm[0], norm[1].reshape(1, N), norm[2].reshape(1, N)]
        out_specs, out_shape = [tile, tile], [jax.ShapeDtypeStruct((M, N), F32)] * 2
    return pl.pallas_call(
        body, name=name, grid=(M // tm, N // tn), in_specs=in_specs, out_specs=out_specs, out_shape=out_shape,
        compiler_params=pltpu.CompilerParams(dimension_semantics=("parallel", "arbitrary")),
    )(*args)


def _mm_nt(g, w, a_src, name, out_dtype=F32, plus=None):
    M, N = g.shape
    sharded = w.ndim == 3
    K, n = w.shape[-2], w.shape[-1]
    if sharded:
        tk, nk = N, 1
    else:
        tk = N if N * g.dtype.itemsize <= 8192 else _pick(N, 2048, 128)
        nk = N // tk
    tm = _pick(M, 1088 if tk * g.dtype.itemsize <= 4096 else 544, 8)
    tn = _pick(K, 1024, 128)
    has_src = a_src is not None
    assert nk == 1 or out_dtype == F32
    assert plus is None or not has_src

    def body(*refs):
        if has_src:
            g_ref, w_ref, s_ref, o_ref = refs
        elif plus is not None:
            g_ref, w_ref, p_ref, o_ref = refs
        else:
            g_ref, w_ref, o_ref = refs
        nt = (((1,), (1,)), ((), ()))
        if sharded:
            r = sum(lax.dot_general(g_ref[:, s * n:(s + 1) * n].astype(MXU_DTYPE), w_ref[s].astype(MXU_DTYPE), nt, preferred_element_type=F32)
                    for s in range(w_ref.shape[0]))
        else:
            r = lax.dot_general(g_ref[...].astype(MXU_DTYPE), w_ref[...].astype(MXU_DTYPE), nt, preferred_element_type=F32)
        if has_src:
            r = r * (2.0 * jnp.maximum(s_ref[...].astype(F32), 0.0))
        first = r if plus is None else r + DN_ALPHA * p_ref[...]
        if nk == 1:
            o_ref[...] = first.astype(out_dtype)
        else:
            k = pl.program_id(2)

            @pl.when(k == 0)
            def _():
                o_ref[...] = first

            @pl.when(k > 0)
            def _():
                o_ref[...] += r

    w_spec = (pl.BlockSpec((w.shape[0], tn, n), lambda i, j, k: (0, j, 0)) if sharded
              else pl.BlockSpec((tn, tk), lambda i, j, k: (j, k)))
    in_specs = [pl.BlockSpec((tm, tk), lambda i, j, k: (i, k)), w_spec]
    args = [g, w]
    if has_src:
        assert nk == 1
        in_specs.append(pl.BlockSpec((tm, tn), lambda i, j, k: (i, j)))
        args.append(a_src)
    if plus is not None:
        in_specs.append(pl.BlockSpec((tm, tn), lambda i, j, k: (i, j)))
        args.append(plus)
    return pl.pallas_call(
        body, name=name,
        grid=(M // tm, K // tn, nk),
        in_specs=in_specs,
        out_specs=pl.BlockSpec((tm, tn), lambda i, j, k: (i, j)),
        out_shape=jax.ShapeDtypeStruct((M, K), out_dtype),
        compiler_params=pltpu.CompilerParams(dimension_semantics=("parallel", "parallel", "arbitrary")),
    )(*args)


def _mm_tn(a, g, act, name, col_shards=1, out_dtype=F32):
    M, K = a.shape
    _, N = g.shape
    n = N // col_shards
    tm, tn, tk = _pick(K, 1024, 128), _pick(n, 1024, 128), _pick(M, 2176, 8)
    nk = M // tk
    per = n // tn
    direct = out_dtype == F32

    def body(a_ref, g_ref, o_ref, *scratch):
        acc_ref = o_ref if direct else scratch[0]
        k = pl.program_id(2)
        av = a_ref[...]
        if act:
            av = _relu2(av.astype(F32))
        r = lax.dot_general(av.astype(MXU_DTYPE), g_ref[...].astype(MXU_DTYPE),
                            (((0,), (0,)), ((), ())), preferred_element_type=F32)

        @pl.when(k == 0)
        def _():
            acc_ref[...] = r

        @pl.when(k > 0)
        def _():
            acc_ref[...] += r

        if not direct:
            @pl.when(k == nk - 1)
            def _():
                o_ref[...] = acc_ref[...].astype(out_dtype)

    if col_shards == 1:
        out_spec, out_shape = pl.BlockSpec((tm, tn), lambda i, j, k: (i, j)), (K, N)
    else:
        out_spec, out_shape = pl.BlockSpec((None, tm, tn), lambda i, j, k: (j // per, i, j % per)), (col_shards, K, n)
    return pl.pallas_call(
        body, name=name,
        grid=(K // tm, N // tn, nk),
        in_specs=[pl.BlockSpec((tk, tm), lambda i, j, k: (k, i)), pl.BlockSpec((tk, tn), lambda i, j, k: (k, j))],
        out_specs=out_spec,
        out_shape=jax.ShapeDtypeStruct(out_shape, out_dtype),
        scratch_shapes=[] if direct else [pltpu.VMEM((tm, tn), F32)],
        compiler_params=pltpu.CompilerParams(dimension_semantics=("parallel", "parallel", "arbitrary")),
    )(a, g)


@functools.partial(jax.custom_vjp, nondiff_argnums=(3, 4, 5, 6))
def matmul(a, w, w_grad_slot, act, name, out_dtype, col_shards):
    return _mm_nn(a, w, act, name + "_fwd", out_dtype)


def _matmul_fwd(a, w, w_grad_slot, act, name, out_dtype, col_shards):
    return _mm_nn(a, w, act, name + "_fwd", out_dtype), (a, w, jnp.zeros((), w_grad_slot.dtype))


def _matmul_bwd(act, name, out_dtype, col_shards, res, g):
    a, w, slot_like = res
    w_grad_dtype = slot_like.dtype
    da = _mm_nt(g, w, a if act else None, name + "_dx")
    dw = _mm_tn(a, g, act, name + "_dw", col_shards, w_grad_dtype)
    return da, None, dw


matmul.defvjp(_matmul_fwd, _matmul_bwd)


def _ln_bwd_call(z, g, dy, name):
    M, D = z.shape
    tm = _pick(M, 544, 8)

    def body(z_ref, g_ref, dy_ref, dz_ref, dg_ref, db_ref):
        @pl.when(pl.program_id(0) == 0)
        def _():
            dg_ref[...] = jnp.zeros_like(dg_ref)
            db_ref[...] = jnp.zeros_like(db_ref)

        zc, rstd = _ln_stats(z_ref[...])
        xhat = zc * rstd
        dy = dy_ref[...]
        dxh = dy * g_ref[...]
        m1 = jnp.mean(dxh, axis=-1, keepdims=True)
        m2 = jnp.mean(dxh * xhat, axis=-1, keepdims=True)
        dz_ref[...] = rstd * (dxh - m1 - xhat * m2)
        dg_ref[...] += jnp.sum(dy * xhat, axis=0, keepdims=True)
        db_ref[...] += jnp.sum(dy, axis=0, keepdims=True)

    row = pl.BlockSpec((tm, D), lambda i: (i, 0))
    vec = pl.BlockSpec((1, D), lambda i: (0, 0))
    return pl.pallas_call(
        body, name=name, grid=(M // tm,), in_specs=[row, vec, row], out_specs=[row, vec, vec],
        out_shape=[jax.ShapeDtypeStruct((M, D), F32), jax.ShapeDtypeStruct((1, D), F32), jax.ShapeDtypeStruct((1, D), F32)],
        compiler_params=pltpu.CompilerParams(dimension_semantics=("arbitrary",)),
    )(z, g.reshape(1, D), dy)


@functools.partial(jax.custom_vjp, nondiff_argnums=(7,))
def mlp_block(h, w1, w2, w1_grad_slot, w2_grad_slot, g, b, name):
    return _mlp_block_fwd(h, w1, w2, w1_grad_slot, w2_grad_slot, g, b, name)[0]


def _mlp_block_fwd(h, w1, w2, w1_grad_slot, w2_grad_slot, g, b, name):
    u = _mm_nn(h, w1, False, name + "_w1_fwd", out_dtype=MXU_DTYPE)
    out, z = _mm_nn(u, w2, True, name + "_w2_norm_fwd", norm=(h, g, b))
    return out, (h, u, z, w1, w2, g, jnp.zeros((), w1_grad_slot.dtype))


def _mlp_block_bwd(name, res, dy):
    h, u, z, w1, w2, g, slot_like = res
    dz, dg, db = _ln_bwd_call(z, g, dy, name + "_norm_bwd")
    du = _mm_nt(dz, w2, u, name + "_w2_dx", out_dtype=MXU_DTYPE)
    dw2 = _mm_tn(u, dz, True, name + "_w2_dw", 1, slot_like.dtype)
    dh = _mm_nt(du, w1, None, name + "_w1_dx", plus=dz)
    dw1 = _mm_tn(h, du, False, name + "_w1_dw", N_CHIPS, slot_like.dtype)
    return dh, None, None, dw1, dw2, dg.reshape(g.shape), db.reshape(g.shape)


mlp_block.defvjp(_mlp_block_fwd, _mlp_block_bwd)


@functools.partial(jax.custom_vjp, nondiff_argnums=(6,))
def out_block(h, y, w, w_grad_slot, g, b, name):
    return _out_block_fwd(h, y, w, w_grad_slot, g, b, name)[0]


def _out_block_fwd(h, y, w, w_grad_slot, g, b, name):
    out, z = _mm_nn(y, w, False, name + "_norm_fwd", norm=(h, g, b))
    return out, (y, z, w, g, jnp.zeros((), w_grad_slot.dtype))


def _out_block_bwd(name, res, dy):
    y, z, w, g, slot_like = res
    dz, dg, db = _ln_bwd_call(z, g, dy, name + "_norm_bwd")
    if isinstance(y, (tuple, list)):
        d_y, dw, k0 = [], [], 0
        for i, part in enumerate(y):
            k1 = k0 + part.shape[1]
            d_y.append(_mm_nt(dz, w[k0:k1], None, "%s_dx%d" % (name, i)))
            dw.append(_mm_tn(part, dz, False, "%s_dw%d" % (name, i), 1, slot_like.dtype))
            k0 = k1
        d_y, dw = tuple(d_y), jnp.concatenate(dw, axis=0)
    else:
        d_y = _mm_nt(dz, w, None, name + "_dx")
        dw = _mm_tn(y, dz, False, name + "_dw", 1, slot_like.dtype)
    return DN_ALPHA * dz, d_y, None, dw, dg.reshape(g.shape), db.reshape(g.shape)


out_block.defvjp(_out_block_fwd, _out_block_bwd)


def _rms_fwd_call(x, g, name, col_block=0):
    R = x.shape[0]
    W = g.shape[-1]
    tr = _pick(R, 1088, 8)

    def body(x_ref, g_ref, o_ref):
        xv = x_ref[...]
        rstd = lax.rsqrt(jnp.mean(xv * xv, axis=-1, keepdims=True) + EPS)
        o_ref[...] = xv * rstd * g_ref[...]

    vec = pl.BlockSpec((1, W), lambda i: (0, 0))
    return pl.pallas_call(
        body, name=name, grid=(R // tr,), in_specs=[pl.BlockSpec((tr, W), lambda i: (i, col_block)), vec],
        out_specs=pl.BlockSpec((tr, W), lambda i: (i, 0)), out_shape=jax.ShapeDtypeStruct((R, W), F32),
        compiler_params=pltpu.CompilerParams(dimension_semantics=("parallel",)),
    )(x, g.reshape(1, W))


def _rms_bwd_call(x, g, dy, name, col_block=0):
    R = x.shape[0]
    W = g.shape[-1]
    tr = _pick(R, 1088, 8)

    def body(x_ref, g_ref, dy_ref, dx_ref, dg_ref):
        @pl.when(pl.program_id(0) == 0)
        def _():
            dg_ref[...] = jnp.zeros_like(dg_ref)

        xv = x_ref[...]
        rstd = lax.rsqrt(jnp.mean(xv * xv, axis=-1, keepdims=True) + EPS)
        xhat = xv * rstd
        dy = dy_ref[...]
        dxh = dy * g_ref[...]
        dx_ref[...] = rstd * (dxh - xhat * jnp.mean(dxh * xhat, axis=-1, keepdims=True))
        dg_ref[...] += jnp.sum(dy * xhat, axis=0, keepdims=True)

    row = pl.BlockSpec((tr, W), lambda i: (i, 0))
    vec = pl.BlockSpec((1, W), lambda i: (0, 0))
    return pl.pallas_call(
        body, name=name, grid=(R // tr,), in_specs=[pl.BlockSpec((tr, W), lambda i: (i, col_block)), vec, row], out_specs=[row, vec],
        out_shape=[jax.ShapeDtypeStruct((R, W), F32), jax.ShapeDtypeStruct((1, W), F32)],
        compiler_params=pltpu.CompilerParams(dimension_semantics=("arbitrary",)),
    )(x, g.reshape(1, W), dy)


def _loss_call(h, tgt, n_tokens, name):
    B, Tp, D = h.shape
    tr = _pick(Tp, 544, 8)

    def body(y_ref, t_ref, dy_ref, acc_ref):
        @pl.when(jnp.logical_and(pl.program_id(0) == 0, pl.program_id(1) == 0))
        def _():
            acc_ref[...] = jnp.zeros_like(acc_ref)

        t = lax.broadcasted_iota(jnp.int32, (tr, 1), 0) + pl.program_id(1) * tr
        counts = jnp.logical_and(t >= N_META, t < N_META + n_tokens)
        e = jnp.where(counts, y_ref[...] - t_ref[...], 0.0)
        dy_ref[...] = e * (1.0 / D)
        acc_ref[...] += jnp.sum(jnp.sum(e * e, axis=-1, keepdims=True), axis=0, keepdims=True) * (0.5 / D)

    row = pl.BlockSpec((None, tr, D), lambda b, i: (b, i, 0))
    one = pl.BlockSpec((1, 1), lambda b, i: (0, 0))
    return pl.pallas_call(
        body, name=name, grid=(B, Tp // tr), in_specs=[row, row], out_specs=[row, one],
        out_shape=[jax.ShapeDtypeStruct((B, Tp, D), F32), jax.ShapeDtypeStruct((1, 1), F32)],
        compiler_params=pltpu.CompilerParams(dimension_semantics=("arbitrary", "arbitrary")),
    )(h, tgt)


@functools.partial(jax.custom_vjp, nondiff_argnums=(2,))
def loss_head(h, tgt, n_tokens):
    return _loss_call(h, tgt, n_tokens, "loss_head")[1][0, 0]


def _loss_head_fwd(h, tgt, n_tokens):
    dy, acc = _loss_call(h, tgt, n_tokens, "loss_head")
    return acc[0, 0], dy


def _loss_head_bwd(n_tokens, dy, ct):
    return ct * dy, None


loss_head.defvjp(_loss_head_fwd, _loss_head_bwd)


_GELU_C = math.sqrt(2.0 / math.pi)


def _gelu_parts(x):
    x2 = x * x
    t = jnp.tanh(_GELU_C * (x + 0.044715 * x * x2))
    gelu = 0.5 * x * (1.0 + t)
    dgelu = 0.5 * (1.0 + t) + 0.5 * x * (1.0 - t * t) * (_GELU_C * (1.0 + 3.0 * 0.044715 * x2))
    return gelu, dgelu


def _sigmoid(x):
    return 1.0 / (1.0 + jnp.exp(-x))


def _scan8(a, b, carry, reverse):
    row = lax.broadcasted_iota(jnp.int32, a.shape, 0)
    for s in (1, 2, 4):
        shift = 8 - s if reverse else s
        keep = (row < 8 - s) if reverse else (row >= s)
        b = jnp.where(keep, a * pltpu.roll(b, shift, 0) + b, b)
        a = jnp.where(keep, a * pltpu.roll(a, shift, 0), a)
    return a * carry + b


def _lru_pre(prec_ref, prev_ref, first, cw_ref, cb_ref, wa_ref, ba_ref, wx_ref, bx_ref, sp_ref):
    tc = prec_ref.shape[0]
    prev = jnp.where(first, 0.0, prev_ref[...])
    ext = jnp.concatenate([prev, prec_ref[...]], axis=0)
    cw = cw_ref[...]
    taps = [ext[8:] if k == CONV_WIDTH - 1 else pltpu.roll(ext, CONV_WIDTH - 1 - k, 0)[8:] for k in range(CONV_WIDTH)]
    xc = cb_ref[...] + sum(cw[k:k + 1, :] * taps[k] for k in range(CONV_WIDTH))
    ga, gx = [], []
    for h in range(LRU_HEADS):
        xh = xc[:, h * LRU_HEAD_DIM:(h + 1) * LRU_HEAD_DIM].astype(MXU_DTYPE)
        ga.append(jnp.dot(xh, wa_ref[h].astype(MXU_DTYPE), preferred_element_type=F32))
        gx.append(jnp.dot(xh, wx_ref[h].astype(MXU_DTYPE), preferred_element_type=F32))
    r = _sigmoid(jnp.concatenate(ga, axis=1) + ba_ref[...])
    i = _sigmoid(jnp.concatenate(gx, axis=1) + bx_ref[...])
    log_a = -LRU_C * r * sp_ref[...]
    a = jnp.exp(log_a)
    a2 = a * a
    mult = jnp.sqrt(-jnp.tanh(log_a) * (a2 + 1.0))
    return taps, xc, r, i, a, a2, mult


def _lru_fwd_call(p, cw, cb, wa, ba, wx, bx, sp):
    B, Tp, _ = p.shape
    W = LRU_WIDTH
    tc = SEQ_BLOCK
    nc = Tp // tc

    def body(pg_ref, prec_ref, prev_ref, cw_ref, cb_ref, wa_ref, ba_ref, wx_ref, bx_ref, sp_ref, y_ref, h_ref, carry_ref):
        first = pl.program_id(1) == 0

        @pl.when(first)
        def _():
            carry_ref[...] = jnp.zeros_like(carry_ref)

        _, xc, r, i, a, a2, mult = _lru_pre(prec_ref, prev_ref, first, cw_ref, cb_ref, wa_ref, ba_ref, wx_ref, bx_ref, sp_ref)
        b = mult * (i * xc)
        carry = carry_ref[0:1, :]
        for t in range(tc // 8):
            h = _scan8(a[8 * t:8 * t + 8], b[8 * t:8 * t + 8], carry, False)
            h_ref[8 * t:8 * t + 8, :] = h
            carry = h[7:8, :]
        carry_ref[...] = jnp.broadcast_to(carry, carry_ref.shape)
        y_ref[...] = h_ref[...] * _gelu_parts(pg_ref[...])[0]

    cur = pl.BlockSpec((None, tc, W), lambda b, j: (b, j, 0))
    rec = pl.BlockSpec((None, tc, W), lambda b, j: (b, j, 1))
    prev = pl.BlockSpec((None, 8, W), lambda b, j: (b, jnp.maximum(j * (tc // 8) - 1, 0), 1))
    vec = pl.BlockSpec((1, W), lambda b, j: (0, 0))
    cws = pl.BlockSpec((CONV_WIDTH, W), lambda b, j: (0, 0))
    wsp = pl.BlockSpec((LRU_HEADS, LRU_HEAD_DIM, LRU_HEAD_DIM), lambda b, j: (0, 0, 0))
    return pl.pallas_call(
        body, name="lru_fwd", grid=(B, nc),
        in_specs=[cur, rec, prev, cws, vec, wsp, vec, wsp, vec, vec],
        out_specs=[cur, cur],
        out_shape=[jax.ShapeDtypeStruct((B, Tp, W), F32), jax.ShapeDtypeStruct((B, Tp, W), F32)],
        scratch_shapes=[pltpu.VMEM((8, W), F32)],
        compiler_params=pltpu.CompilerParams(dimension_semantics=("arbitrary", "arbitrary")),
    )(p, p, p, cw, cb, wa, ba, wx, bx, sp)


def _lru_bwd_call(p, hseq, dy, cw, cb, wa, ba, wx, bx, sp, dpq, dpkv, dkpe):
    B, Tp, P = p.shape
    W = LRU_WIDTH
    tc = SEQ_BLOCK
    nc = Tp // tc
    HD = LRU_HEAD_DIM

    def body(pg_ref, prec_ref, prev_ref, h_ref, hprev_ref, dy_ref, cw_ref, cb_ref, wa_ref, ba_ref, wx_ref, bx_ref, sp_ref,
             dpq_ref, dpkv_ref, dkpe_ref, dp_ref, dcw_ref, dcb_ref, dwa_ref, dba_ref, dwx_ref, dbx_ref, dsp_ref,
             gcar_ref, anext_ref, halo_ref, g_ref):
        j = pl.program_id(1)
        first = j == nc - 1
        last = j == 0

        @pl.when(jnp.logical_and(pl.program_id(0) == 0, last))
        def _():
            for ref in (dcw_ref, dcb_ref, dwa_ref, dba_ref, dwx_ref, dbx_ref, dsp_ref):
                ref[...] = jnp.zeros_like(ref)

        @pl.when(last)
        def _():
            gcar_ref[...] = jnp.zeros_like(gcar_ref)
            anext_ref[...] = jnp.zeros_like(anext_ref)
            halo_ref[...] = jnp.zeros_like(halo_ref)

        taps, xc, r, i, a, a2, mult = _lru_pre(prec_ref, prev_ref, first, cw_ref, cb_ref, wa_ref, ba_ref, wx_ref, bx_ref, sp_ref)
        row = lax.broadcasted_iota(jnp.int32, (tc, W), 0)
        gelu, dgelu = _gelu_parts(pg_ref[...])
        dy = dy_ref[...]
        hcur = h_ref[...]
        dp_ref[:, 0:W] = dy * hcur * dgelu
        dp_ref[:, 2 * W:2 * W + MLA_Q_RANK] = dpq_ref[...]
        dp_ref[:, _KPE_START - MLA_KV_RANK:_KPE_START] = dpkv_ref[...]
        dp_ref[:, _KPE_START:P] = pltpu.roll(dkpe_ref[...], HEAD_LANES - MLA_NOPE, 1)[:, 0:P - _KPE_START]
        dh = dy * gelu
        a_next = jnp.where(row == tc - 1, anext_ref[0:1, :], pltpu.roll(a, tc - 1, 0))
        carry = gcar_ref[0:1, :]
        for t in reversed(range(tc // 8)):
            g = _scan8(a_next[8 * t:8 * t + 8], dh[8 * t:8 * t + 8], carry, True)
            g_ref[8 * t:8 * t + 8, :] = g
            carry = g[0:1, :]
        gcar_ref[...] = jnp.broadcast_to(carry, gcar_ref.shape)
        anext_ref[...] = jnp.broadcast_to(a[0:1, :], anext_ref.shape)
        G = g_ref[...]
        h_before = jnp.where(first, 0.0, hprev_ref[7:8, :])
        hprev = jnp.where(row == 0, h_before, pltpu.roll(hcur, 1, 0))
        d_a = G * hprev
        gx_ = G * xc
        d_mult = gx_ * i
        d_i = gx_ * mult
        dxc = G * (mult * i)
        d_la = d_a * a - d_mult * (a2 / mult)
        sp = sp_ref[...]
        d_r = d_la * (-LRU_C * sp)
        dsp_ref[...] += jnp.sum(d_la * (-LRU_C * r), axis=0, keepdims=True)
        dga = d_r * r * (1.0 - r)
        dgx = d_i * i * (1.0 - i)
        dba_ref[...] += jnp.sum(dga, axis=0, keepdims=True)
        dbx_ref[...] += jnp.sum(dgx, axis=0, keepdims=True)
        back = []
        for h in range(LRU_HEADS):
            sl = slice(h * HD, (h + 1) * HD)
            xh = xc[:, sl].astype(MXU_DTYPE)
            ah = dga[:, sl].astype(MXU_DTYPE)
            bh = dgx[:, sl].astype(MXU_DTYPE)
            tn = (((0,), (0,)), ((), ()))
            nt = (((1,), (1,)), ((), ()))
            dwa_ref[h] += lax.dot_general(xh, ah, tn, preferred_element_type=F32)
            dwx_ref[h] += lax.dot_general(xh, bh, tn, preferred_element_type=F32)
            back.append(lax.dot_general(ah, wa_ref[h].astype(MXU_DTYPE), nt, preferred_element_type=F32)
                        + lax.dot_general(bh, wx_ref[h].astype(MXU_DTYPE), nt, preferred_element_type=F32))
        dxc = dxc + jnp.concatenate(back, axis=1)
        dcb_ref[...] += jnp.sum(dxc, axis=0, keepdims=True)
        for k in range(CONV_WIDTH):
            dcw_ref[k:k + 1, :] += jnp.sum(dxc * taps[k], axis=0, keepdims=True)
        ext = jnp.concatenate([dxc, halo_ref[...]], axis=0)
        cw = cw_ref[...]
        acc = cw[CONV_WIDTH - 1:CONV_WIDTH, :] * dxc
        for k in range(CONV_WIDTH - 1):
            s = CONV_WIDTH - 1 - k
            acc = acc + cw[k:k + 1, :] * pltpu.roll(ext, tc + 8 - s, 0)[:tc]
        dp_ref[:, W:2 * W] = acc
        halo_ref[...] = dxc[0:8, :]

    rev = lambda j: nc - 1 - j
    cur = pl.BlockSpec((None, tc, W), lambda b, j: (b, rev(j), 0))
    rec = pl.BlockSpec((None, tc, W), lambda b, j: (b, rev(j), 1))
    prev = pl.BlockSpec((None, 8, W), lambda b, j: (b, jnp.maximum(rev(j) * (tc // 8) - 1, 0), 0))
    prev_rec = pl.BlockSpec((None, 8, W), lambda b, j: (b, jnp.maximum(rev(j) * (tc // 8) - 1, 0), 1))
    vec = pl.BlockSpec((1, W), lambda b, j: (0, 0))
    cws = pl.BlockSpec((CONV_WIDTH, W), lambda b, j: (0, 0))
    wsp = pl.BlockSpec((LRU_HEADS, HD, HD), lambda b, j: (0, 0, 0))
    vs = jax.ShapeDtypeStruct((1, W), F32)
    ws = jax.ShapeDtypeStruct((LRU_HEADS, HD, HD), F32)

    def rows(width):
        return pl.BlockSpec((None, tc, width), lambda b, j: (b, rev(j), 0))

    return pl.pallas_call(
        body, name="lru_bwd", grid=(B, nc),
        in_specs=[cur, rec, prev_rec, cur, prev, cur, cws, vec, wsp, vec, wsp, vec, vec, rows(MLA_Q_RANK), rows(MLA_KV_RANK), rows(HEAD_LANES)],
        out_specs=[rows(P), cws, vec, wsp, vec, wsp, vec, vec],
        out_shape=[jax.ShapeDtypeStruct((B, Tp, P), F32), jax.ShapeDtypeStruct((CONV_WIDTH, W), F32), vs, ws, vs, ws, vs, vs],
        scratch_shapes=[pltpu.VMEM((8, W), F32), pltpu.VMEM((8, W), F32), pltpu.VMEM((8, W), F32), pltpu.VMEM((tc, W), F32)],
        compiler_params=pltpu.CompilerParams(dimension_semantics=("arbitrary", "arbitrary")),
    )(p, p, p, hseq, hseq, dy, cw, cb, wa, ba, wx, bx, sp, dpq, dpkv, dkpe)


_Q_BLOCK = 2 * LRU_WIDTH // MLA_Q_RANK
_KV_BLOCK = (2 * LRU_WIDTH + MLA_Q_RANK) // MLA_KV_RANK
_KPE_START = 2 * LRU_WIDTH + MLA_Q_RANK + MLA_KV_RANK


@jax.custom_vjp
def even_front(p, cw, cb, wa, ba, wx, bx, sp, gq, gkv):
    return _even_front_fwd(p, cw, cb, wa, ba, wx, bx, sp, gq, gkv)[0]


def _even_front_fwd(p, cw, cb, wa, ba, wx, bx, sp, gq, gkv):
    B, Tp, W = p.shape
    p2d = p.reshape(B * Tp, W)
    y, hseq = _lru_fwd_call(p, cw, cb, wa, ba, wx, bx, sp)
    qn = _rms_fwd_call(p2d, gq, "q_norm_fwd", _Q_BLOCK)
    kvn = _rms_fwd_call(p2d, gkv, "kv_norm_fwd", _KV_BLOCK)
    kpe = jnp.pad(p[:, :, _KPE_START:], ((0, 0), (0, 0), (MLA_NOPE, HEAD_LANES - MLA_NOPE - MLA_ROPE)))
    return (y, qn, kvn, kpe), (p, hseq, cw, cb, wa, ba, wx, bx, sp, gq, gkv)


def _even_front_bwd(res, cts):
    p, hseq, cw, cb, wa, ba, wx, bx, sp, gq, gkv = res
    dy, dqn, dkvn, dkpe = cts
    B, Tp, W = p.shape
    p2d = p.reshape(B * Tp, W)
    dpq, dgq = _rms_bwd_call(p2d, gq, dqn, "q_norm_bwd", _Q_BLOCK)
    dpkv, dgkv = _rms_bwd_call(p2d, gkv, dkvn, "kv_norm_bwd", _KV_BLOCK)
    dp, dcw, dcb, dwa, dba, dwx, dbx, dsp = _lru_bwd_call(p, hseq, dy, cw, cb, wa, ba, wx, bx, sp, dpq.reshape(B, Tp, -1),
                                                          dpkv.reshape(B, Tp, -1), dkpe)
    return dp, dcw, dcb, dwa, dba, dwx, dbx, dsp, dgq.reshape(gq.shape), dgkv.reshape(gkv.shape)


even_front.defvjp(_even_front_fwd, _even_front_bwd)


def _rope_tables(pos, half):
    inv = ROPE_BASE ** (-jnp.arange(half, dtype=F32) / half)
    ang = pos.astype(F32)[:, None] * inv[None, :]
    return jnp.cos(ang), jnp.sin(ang)


_NT = (((1,), (1,)), ((), ()))
_TN = (((0,), (0,)), ((), ()))
HEAD_LANES = 128
_MLA_SCALE = (MLA_NOPE + MLA_ROPE) ** -0.5
_LOG2E = math.log2(math.e)


Q_BLOCK = 512


def _query_blocks(Tp):
    first = Tp % Q_BLOCK or Q_BLOCK
    return [(0, first)] + [(r, r + Q_BLOCK) for r in range(first, Tp, Q_BLOCK)]


def _mask_diagonal(s, fill):
    R, L = s.shape
    row = lax.broadcasted_iota(jnp.int32, (R, R), 0)
    col = lax.broadcasted_iota(jnp.int32, (R, R), 1)
    last = jnp.where(col <= row, s[:, L - R:], fill)
    return last if L == R else jnp.concatenate([s[:, :L - R], last], axis=1)


def _mla_rope_tables(pos):
    half = MLA_ROPE // 2
    cos, sin = _rope_tables(pos, half)
    T = pos.shape[0]
    ones, zeros = jnp.ones((T, MLA_NOPE), F32), jnp.zeros((T, MLA_NOPE), F32)
    tail1, tail0 = jnp.ones((T, HEAD_LANES - MLA_NOPE - MLA_ROPE), F32), jnp.zeros((T, HEAD_LANES - MLA_NOPE - MLA_ROPE), F32)
    zh = jnp.zeros((T, half), F32)
    c = jnp.concatenate([ones, cos, cos, tail1], axis=1)
    s_up = jnp.concatenate([zeros, -sin, zh, tail0], axis=1)
    s_down = jnp.concatenate([zeros, zh, sin, tail0], axis=1)
    return c, s_up, s_down


def _rope_lanes(x, c, s_up, s_down):
    half = MLA_ROPE // 2
    return x * c + pltpu.roll(x, HEAD_LANES - half, 1) * s_up + pltpu.roll(x, half, 1) * s_down


def _unrope_lanes(d, c, s_up, s_down):
    half = MLA_ROPE // 2
    return d * c + pltpu.roll(d * s_up, half, 1) + pltpu.roll(d * s_down, HEAD_LANES - half, 1)


def _mla_operands(q_ref, kv_ref, kpe_ref, c, s_up, s_down):
    lane = lax.broadcasted_iota(jnp.int32, kv_ref.shape, 1)
    qr = (_rope_lanes(q_ref[...].astype(F32), c, s_up, s_down) * (_MLA_SCALE * _LOG2E)).astype(MXU_DTYPE)
    kr = jnp.where(lane < MLA_NOPE, kv_ref[...].astype(F32), _rope_lanes(kpe_ref[...], c, s_up, s_down)).astype(MXU_DTYPE)
    return qr, kr, lane


def _mla_specs(Tp):
    head = pl.BlockSpec((None, Tp, HEAD_LANES), lambda b, h: (b, 0, h))
    shared = pl.BlockSpec((None, Tp, HEAD_LANES), lambda b, h: (b, 0, 0))
    tab = pl.BlockSpec((Tp, HEAD_LANES), lambda b, h: (0, 0))
    lse = pl.BlockSpec((None, None, Tp, 1), lambda b, h: (b, h, 0, 0))
    return head, shared, tab, lse


def _attn_fwd_call(q, kv, kpe, tabs):
    B, Tp, _ = q.shape

    def body(q_ref, kv_ref, kpe_ref, c_ref, su_ref, sd_ref, o_ref, lse_ref, qr_ref, kr_ref):
        qr, kr, lane = _mla_operands(q_ref, kv_ref, kpe_ref, c_ref[...], su_ref[...], sd_ref[...])
        qr_ref[...] = qr
        kr_ref[...] = kr
        for r0, L in _query_blocks(Tp):
            blk = slice(r0, L)
            s = _mask_diagonal(lax.dot_general(qr_ref[blk, :], kr_ref[0:L, :], _NT, preferred_element_type=F32), NEG_INF)
            m = jnp.max(s, axis=-1, keepdims=True)
            p = jnp.exp2(s - m)
            l = jnp.sum(p, axis=-1, keepdims=True)
            o = jnp.dot(p.astype(MXU_DTYPE), kv_ref[0:L, :].astype(MXU_DTYPE), preferred_element_type=F32)
            o_ref[blk, :] = jnp.where(lane[blk, :] >= MLA_NOPE, o / l, 0.0)
            lse_ref[blk, :] = m + jnp.log2(l)

    head, shared, tab, lse = _mla_specs(Tp)
    return pl.pallas_call(
        body, name="mla_attn_fwd", grid=(B, MLA_HEADS), in_specs=[head, head, shared, tab, tab, tab], out_specs=[head, lse],
        out_shape=[jax.ShapeDtypeStruct((B, Tp, MLA_HEADS * HEAD_LANES), F32), jax.ShapeDtypeStruct((B, MLA_HEADS, Tp, 1), F32)],
        scratch_shapes=[pltpu.VMEM((Tp, HEAD_LANES), MXU_DTYPE), pltpu.VMEM((Tp, HEAD_LANES), MXU_DTYPE)],
        compiler_params=pltpu.CompilerParams(dimension_semantics=("parallel", "parallel")),
    )(q, kv, kpe, *tabs)


def _attn_bwd_call(q, kv, kpe, tabs, o, lse, do):
    B, Tp, _ = q.shape

    def body(q_ref, kv_ref, kpe_ref, c_ref, su_ref, sd_ref, o_ref, lse_ref, do_ref, dq_ref, dkv_ref, dkpe_ref,
             qr_ref, kr_ref, dqa_ref, dka_ref, dva_ref):
        c, s_up, s_down = c_ref[...], su_ref[...], sd_ref[...]
        qr, kr, lane = _mla_operands(q_ref, kv_ref, kpe_ref, c, s_up, s_down)
        qr_ref[...] = qr
        kr_ref[...] = kr
        dka_ref[...] = jnp.zeros_like(dka_ref)
        dva_ref[...] = jnp.zeros_like(dva_ref)
        for r0, L in _query_blocks(Tp):
            blk = slice(r0, L)
            qb = qr_ref[blk, :]
            do = jnp.where(lane[blk, :] >= MLA_NOPE, do_ref[blk, :], 0.0)
            delta = jnp.sum(do * o_ref[blk, :], axis=-1, keepdims=True)
            s = _mask_diagonal(lax.dot_general(qb, kr_ref[0:L, :], _NT, preferred_element_type=F32), NEG_INF)
            p = jnp.exp2(s - lse_ref[blk, :])
            dob = do.astype(MXU_DTYPE)
            dva_ref[0:L, :] += lax.dot_general(p.astype(MXU_DTYPE), dob, _TN, preferred_element_type=F32)
            dp = lax.dot_general(dob, kv_ref[0:L, :].astype(MXU_DTYPE), _NT, preferred_element_type=F32)
            ds = (p * (dp - delta)).astype(MXU_DTYPE)
            dqa_ref[blk, :] = jnp.dot(ds, kr_ref[0:L, :], preferred_element_type=F32)
            dka_ref[0:L, :] += lax.dot_general(ds, qb, _TN, preferred_element_type=F32)
        dq_ref[...] = _unrope_lanes(dqa_ref[...] * _MLA_SCALE, c, s_up, s_down).astype(dq_ref.dtype)
        dk = dka_ref[...] * (1.0 / _LOG2E)
        dkv_ref[...] = jnp.where(lane < MLA_NOPE, dk, dva_ref[...]).astype(dkv_ref.dtype)
        dkpe = jnp.where(lane >= MLA_NOPE, _unrope_lanes(dk, c, s_up, s_down), 0.0)

        @pl.when(pl.program_id(1) == 0)
        def _():
            dkpe_ref[...] = dkpe

        @pl.when(pl.program_id(1) > 0)
        def _():
            dkpe_ref[...] += dkpe

    head, shared, tab, lse_spec = _mla_specs(Tp)
    wide = jax.ShapeDtypeStruct((B, Tp, MLA_HEADS * HEAD_LANES), q.dtype)
    acc = pltpu.VMEM((Tp, HEAD_LANES), F32)
    return pl.pallas_call(
        body, name="mla_attn_bwd", grid=(B, MLA_HEADS),
        in_specs=[head, head, shared, tab, tab, tab, head, lse_spec, head], out_specs=[head, head, shared],
        out_shape=[wide, wide, jax.ShapeDtypeStruct((B, Tp, HEAD_LANES), F32)],
        scratch_shapes=[pltpu.VMEM((Tp, HEAD_LANES), MXU_DTYPE), pltpu.VMEM((Tp, HEAD_LANES), MXU_DTYPE), acc, acc, acc],
        compiler_params=pltpu.CompilerParams(dimension_semantics=("parallel", "arbitrary")),
    )(q, kv, kpe, *tabs, o, lse, do)


@jax.custom_vjp
def mla_attention(q, kv, kpe, tabs):
    return _attn_fwd_call(q, kv, kpe, tabs)[0]


def _mla_attention_fwd(q, kv, kpe, tabs):
    o, lse = _attn_fwd_call(q, kv, kpe, tabs)
    return o, (q, kv, kpe, tabs, o, lse)


def _mla_attention_bwd(res, do):
    q, kv, kpe, tabs, o, lse = res
    dq, dkv, dkpe = _attn_bwd_call(q, kv, kpe, tabs, o, lse, do)
    return dq, dkv, dkpe, None


mla_attention.defvjp(_mla_attention_fwd, _mla_attention_bwd)


def _rope_halves(x, cos, sin):
    half = x.shape[1] // 2
    x1, x2 = x[:, :half], x[:, half:]
    return jnp.concatenate([x1 * cos - x2 * sin, x1 * sin + x2 * cos], axis=1)


def _unrope_halves(d, cos, sin):
    half = d.shape[1] // 2
    d1, d2 = d[:, :half], d[:, half:]
    return jnp.concatenate([d1 * cos + d2 * sin, d2 * cos - d1 * sin], axis=1)


_RET_K_SCALE = RET_QK_DIM ** -0.5
_RET_Q_BLOCKS = RET_HEADS
_RET_V_BLOCK0 = 2 * RET_HEADS * RET_QK_DIM // RET_V_DIM
_RET_G_BLOCK0 = _RET_V_BLOCK0 + RET_HEADS


def _ret_specs(Tp):
    q = pl.BlockSpec((None, Tp, RET_QK_DIM), lambda b, h: (b, 0, h))
    k = pl.BlockSpec((None, Tp, RET_QK_DIM), lambda b, h: (b, 0, _RET_Q_BLOCKS + h))
    v = pl.BlockSpec((None, Tp, RET_V_DIM), lambda b, h: (b, 0, _RET_V_BLOCK0 + h))
    tab = pl.BlockSpec((Tp, RET_QK_DIM // 2), lambda b, h: (0, 0))
    lg = pl.BlockSpec((None, 1, 1), lambda b, h: (h, 0, 0))
    return q, k, v, tab, lg


def _ret_operands(q_ref, k_ref, cos, sin, lg):
    t = lax.broadcasted_iota(jnp.int32, (q_ref.shape[0], 1), 0).astype(F32)
    grow, shrink = jnp.exp(-lg * t), jnp.exp(lg * t)
    qs = (_rope_halves(q_ref[...].astype(F32), cos, sin) * shrink).astype(MXU_DTYPE)
    ks = (_rope_halves(k_ref[...].astype(F32), cos, sin) * (grow * _RET_K_SCALE)).astype(MXU_DTYPE)
    return qs, ks, shrink, grow * _RET_K_SCALE


def _ret_core_fwd_call(p, cos, sin, lg):
    B, Tp, _ = p.shape

    def body(q_ref, k_ref, v_ref, cos_ref, sin_ref, lg_ref, o_ref, qs_ref, ks_ref):
        qs_ref[...], ks_ref[...], _, _ = _ret_operands(q_ref, k_ref, cos_ref[...], sin_ref[...], lg_ref[...])
        for r0, L in _query_blocks(Tp):
            blk = slice(r0, L)
            s = _mask_diagonal(lax.dot_general(qs_ref[blk, :], ks_ref[0:L, :], _NT, preferred_element_type=F32), 0.0)
            o_ref[blk, :] = jnp.dot(s.astype(MXU_DTYPE), v_ref[0:L, :].astype(MXU_DTYPE), preferred_element_type=F32)

    q, k, v, tab, lgs = _ret_specs(Tp)
    return pl.pallas_call(
        body, name="retention_fwd", grid=(B, RET_HEADS), in_specs=[q, k, v, tab, tab, lgs],
        out_specs=pl.BlockSpec((None, Tp, RET_V_DIM), lambda b, h: (b, 0, h)),
        out_shape=jax.ShapeDtypeStruct((B, Tp, RET_HEADS * RET_V_DIM), F32),
        scratch_shapes=[pltpu.VMEM((Tp, RET_QK_DIM), MXU_DTYPE), pltpu.VMEM((Tp, RET_QK_DIM), MXU_DTYPE)],
        compiler_params=pltpu.CompilerParams(dimension_semantics=("parallel", "parallel")),
    )(p, p, p, cos, sin, lg)


def _ret_core_bwd_call(p, do, cos, sin, lg):
    B, Tp, _ = p.shape

    def body(q_ref, k_ref, v_ref, do_ref, cos_ref, sin_ref, lg_ref, dq_ref, dk_ref, dv_ref, qs_ref, ks_ref, dqa_ref, dka_ref, dva_ref):
        cos_, sin_ = cos_ref[...], sin_ref[...]
        qs_ref[...], ks_ref[...], q_scale, k_scale = _ret_operands(q_ref, k_ref, cos_, sin_, lg_ref[...])
        dka_ref[...] = jnp.zeros_like(dka_ref)
        dva_ref[...] = jnp.zeros_like(dva_ref)
        for r0, L in _query_blocks(Tp):
            blk = slice(r0, L)
            qb = qs_ref[blk, :]
            dob = do_ref[blk, :].astype(MXU_DTYPE)
            s = _mask_diagonal(lax.dot_general(qb, ks_ref[0:L, :], _NT, preferred_element_type=F32), 0.0).astype(MXU_DTYPE)
            dva_ref[0:L, :] += lax.dot_general(s, dob, _TN, preferred_element_type=F32)
            ds = _mask_diagonal(lax.dot_general(dob, v_ref[0:L, :].astype(MXU_DTYPE), _NT, preferred_element_type=F32), 0.0).astype(MXU_DTYPE)
            dqa_ref[blk, :] = jnp.dot(ds, ks_ref[0:L, :], preferred_element_type=F32)
            dka_ref[0:L, :] += lax.dot_general(ds, qb, _TN, preferred_element_type=F32)
        dq_ref[...] = _unrope_halves(dqa_ref[...] * q_scale, cos_, sin_).astype(dq_ref.dtype)
        dk_ref[...] = _unrope_halves(dka_ref[...] * k_scale, cos_, sin_).astype(dk_ref.dtype)
        dv_ref[...] = dva_ref[...].astype(dv_ref.dtype)

    q, k, v, tab, lgs = _ret_specs(Tp)
    qk_out = pl.BlockSpec((None, Tp, RET_QK_DIM), lambda b, h: (b, 0, h))
    v_out = pl.BlockSpec((None, Tp, RET_V_DIM), lambda b, h: (b, 0, h))
    return pl.pallas_call(
        body, name="retention_bwd", grid=(B, RET_HEADS), in_specs=[q, k, v, v_out, tab, tab, lgs],
        out_specs=[qk_out, qk_out, v_out],
        out_shape=[jax.ShapeDtypeStruct((B, Tp, RET_HEADS * RET_QK_DIM), p.dtype), jax.ShapeDtypeStruct((B, Tp, RET_HEADS * RET_QK_DIM), p.dtype),
                   jax.ShapeDtypeStruct((B, Tp, RET_HEADS * RET_V_DIM), p.dtype)],
        scratch_shapes=[pltpu.VMEM((Tp, RET_QK_DIM), MXU_DTYPE), pltpu.VMEM((Tp, RET_QK_DIM), MXU_DTYPE),
                        pltpu.VMEM((Tp, RET_QK_DIM), F32), pltpu.VMEM((Tp, RET_QK_DIM), F32), pltpu.VMEM((Tp, RET_V_DIM), F32)],
        compiler_params=pltpu.CompilerParams(dimension_semantics=("parallel", "parallel")),
    )(p, p, p, do, cos, sin, lg)


def _ret_gate_specs(M):
    tm = _pick(M, 1088, 8)
    head = pl.BlockSpec((tm, RET_V_DIM), lambda i, h: (i, h))
    gate = pl.BlockSpec((tm, RET_V_DIM), lambda i, h: (i, _RET_G_BLOCK0 + h))
    return tm, head, gate


def _ret_gate_fwd_call(o, p2d):
    M = o.shape[0]
    tm, head, gate = _ret_gate_specs(M)

    def body(o_ref, g_ref, y_ref):
        ov = o_ref[...]
        gv = g_ref[...].astype(F32)
        rstd = lax.rsqrt(jnp.mean(ov * ov, axis=-1, keepdims=True) + EPS)
        y_ref[...] = (gv * _sigmoid(gv)) * (ov * rstd)

    return pl.pallas_call(
        body, name="retention_gate_fwd", grid=(M // tm, RET_HEADS), in_specs=[head, gate], out_specs=head,
        out_shape=jax.ShapeDtypeStruct(o.shape, F32),
        compiler_params=pltpu.CompilerParams(dimension_semantics=("parallel", "parallel")),
    )(o, p2d)


def _ret_gate_bwd_call(o, p2d, dy):
    M = o.shape[0]
    tm, head, gate = _ret_gate_specs(M)

    def body(o_ref, g_ref, dy_ref, do_ref, dg_ref):
        ov = o_ref[...]
        gv = g_ref[...].astype(F32)
        dy = dy_ref[...]
        rstd = lax.rsqrt(jnp.mean(ov * ov, axis=-1, keepdims=True) + EPS)
        on = ov * rstd
        sg = _sigmoid(gv)
        dg_ref[...] = (dy * on * (sg * (1.0 + gv * (1.0 - sg)))).astype(dg_ref.dtype)
        don = dy * (gv * sg)
        do_ref[...] = (rstd * (don - on * jnp.mean(don * on, axis=-1, keepdims=True))).astype(do_ref.dtype)

    shp = jax.ShapeDtypeStruct(o.shape, p2d.dtype)
    return pl.pallas_call(
        body, name="retention_gate_bwd", grid=(M // tm, RET_HEADS), in_specs=[head, gate, head], out_specs=[head, head],
        out_shape=[shp, shp],
        compiler_params=pltpu.CompilerParams(dimension_semantics=("parallel", "parallel")),
    )(o, p2d, dy)


def _log_gamma():
    return jnp.log(1.0 - 2.0 ** (-5.0 - jnp.arange(RET_HEADS, dtype=F32))).reshape(RET_HEADS, 1, 1)


@jax.custom_vjp
def retention_mixer(p, cos, sin):
    B, Tp, W = p.shape
    o = _ret_core_fwd_call(p, cos, sin, _log_gamma())
    return _ret_gate_fwd_call(o.reshape(B * Tp, -1), p.reshape(B * Tp, W))


def _retention_mixer_fwd(p, cos, sin):
    B, Tp, W = p.shape
    o = _ret_core_fwd_call(p, cos, sin, _log_gamma())
    return _ret_gate_fwd_call(o.reshape(B * Tp, -1), p.reshape(B * Tp, W)), (p, o, cos, sin)


def _retention_mixer_bwd(res, dy):
    p, o, cos, sin = res
    B, Tp, W = p.shape
    do, dg = _ret_gate_bwd_call(o.reshape(B * Tp, -1), p.reshape(B * Tp, W), dy)
    dq, dk, dv = _ret_core_bwd_call(p, do.reshape(B, Tp, -1), cos, sin, _log_gamma())
    return jnp.concatenate([dq, dk, dv, dg.reshape(B, Tp, -1)], axis=-1), None, None


retention_mixer.defvjp(_retention_mixer_fwd, _retention_mixer_bwd)


def _heads_to_lanes(w):
    K = w.shape[0]
    w = w.reshape(K, MLA_HEADS, MLA_NOPE + MLA_ROPE)
    return jnp.pad(w, ((0, 0), (0, 0), (0, HEAD_LANES - MLA_NOPE - MLA_ROPE))).reshape(K, MLA_HEADS * HEAD_LANES)


def _out_rows_to_lanes(w):
    N = w.shape[1]
    att = w[LRU_WIDTH:].reshape(MLA_HEADS, MLA_V, N)
    att = jnp.pad(att, ((0, 0), (HEAD_LANES - MLA_V, 0), (0, 0))).reshape(MLA_HEADS * HEAD_LANES, N)
    return jnp.concatenate([w[:LRU_WIDTH], att], axis=0)


def _seq_dims(x):
    B, S, D = x.shape
    T = S + N_META
    Tp = _round_up(T, SEQ_BLOCK)
    return B, S, T, Tp


def _mixer0(diff, w, token):
    x = diff["x"]
    B, S, T, Tp = _seq_dims(x)
    D = x.shape[-1]
    M = B * Tp
    pos = jnp.arange(Tp, dtype=jnp.int32)

    def mm(a, name, act=False, out_dtype=F32, layout=lambda m: m, col_shards=1):
        return matmul(a, layout(w[name]), layout(diff[name]), act, name, out_dtype, col_shards)

    meta = jnp.broadcast_to(diff["meta_tokens"][None], (B, N_META, D))
    h = jnp.concatenate([meta, x + token, jnp.zeros((B, Tp - T, D), F32)], axis=1).reshape(M, D)
    p = mm(h, "ev_w_in")
    sp = jax.nn.softplus(-diff["ev_lru_lambda"]).reshape(1, LRU_WIDTH)
    y_rec, qn, kvn, kpe = even_front(
        p.reshape(B, Tp, -1), diff["ev_conv_w"].reshape(CONV_WIDTH, LRU_WIDTH), diff["ev_conv_b"].reshape(1, LRU_WIDTH),
        diff["ev_w_rg_a"].reshape(LRU_HEADS, LRU_HEAD_DIM, LRU_HEAD_DIM), diff["ev_b_rg_a"].reshape(1, LRU_WIDTH),
        diff["ev_w_rg_x"].reshape(LRU_HEADS, LRU_HEAD_DIM, LRU_HEAD_DIM), diff["ev_b_rg_x"].reshape(1, LRU_WIDTH),
        sp, diff["ev_q_norm_g"].reshape(-1), diff["ev_kv_norm_g"].reshape(-1))
    y_rec = y_rec.reshape(M, LRU_WIDTH)
    q = mm(qn, "ev_w_uq", out_dtype=MXU_DTYPE, layout=_heads_to_lanes).reshape(B, Tp, -1)
    kv = mm(kvn, "ev_w_ukv", out_dtype=MXU_DTYPE).reshape(B, Tp, -1)
    y_att = mla_attention(q, kv, kpe, _mla_rope_tables(pos)).reshape(M, -1)
    return out_block(h, (y_rec, y_att), _out_rows_to_lanes(w["ev_w_out"]), _out_rows_to_lanes(diff["ev_w_out"]),
                     diff["ln_mix_g"], diff["ln_mix_b"], "ev_w_out")


def _mlp0(diff, h, w):
    return mlp_block(h, w["mlp_w1_0"], w["mlp_w2_0"], diff["mlp_w1_0"], diff["mlp_w2_0"], diff["ln_mlp_g"], diff["ln_mlp_b"], "mlp0")


def _layer1_loss(diff, h, w, tgt):
    B, S, T, Tp = _seq_dims(tgt)
    D = tgt.shape[-1]
    pos = jnp.arange(Tp, dtype=jnp.int32)

    def mm(a, name, out_dtype=F32, col_shards=1):
        return matmul(a, w[name], diff[name], False, name, out_dtype, col_shards)

    p = mm(h, "od_w_in", out_dtype=MXU_DTYPE, col_shards=N_CHIPS)
    cos, sin = _rope_tables(pos, RET_QK_DIM // 2)
    h = out_block(h, retention_mixer(p.reshape(B, Tp, -1), cos, sin), w["od_w_out"], diff["od_w_out"],
                  diff["ln_mix_g"], diff["ln_mix_b"], "od_w_out")
    h = mlp_block(h, w["mlp_w1_1"], w["mlp_w2_1"], diff["mlp_w1_1"], diff["mlp_w2_1"], diff["ln_mlp_g"], diff["ln_mlp_b"], "mlp1")
    return loss_head(h.reshape(B, Tp, D), jnp.pad(tgt, ((0, 0), (N_META, Tp - T), (0, 0))), S)


_HBM = pl.BlockSpec(memory_space=pltpu.HBM)


def _place():
    return lax.axis_index("x"), lax.axis_index("y"), lax.axis_index("c")


def _other_chips(x, y):
    return [(1 - x, y), (x, 1 - y), (1 - x, 1 - y)]


def _chunks(rows, sublanes, most):
    for q in range(most, 0, -1):
        if rows % (q * sublanes) == 0:
            return q
    return 1


def _sublanes(dtype):
    return 8 * 4 // jnp.dtype(dtype).itemsize


def _gather_pieces(bufs):
    plan, first = [], []
    for b in bufs:
        Rh = b.shape[0] // 2
        Q = _chunks(Rh, _sublanes(b.dtype), 4) if Rh * b.shape[1] * b.dtype.itemsize > (1 << 20) else 1
        first.append(3 * sum(q for _, q, _ in plan))
        plan.append((Rh, Q, Rh // Q))
    return plan, first, 3 * sum(q for _, q, _ in plan)


def _allgather_chips(bufs, name):
    n = len(bufs)
    plan, first, n_sems = _gather_pieces(bufs)

    def body(*refs):
        x_refs, out_refs, (send_sems, recv_sems) = refs[:n], refs[n:2 * n], refs[2 * n:]
        x, y, c = _place()
        sibling = (x, y, 1 - c)
        chips = _other_chips(x, y)

        def copy(k, src, dst, to):
            return pltpu.make_async_remote_copy(src_ref=src, dst_ref=dst, send_sem=send_sems.at[k], recv_sem=recv_sems.at[k],
                                                device_id=to, device_id_type=MESH)

        def piece(i, cx, cy, hc, q):
            Rh, _, ch = plan[i]
            return out_refs[i].at[2 * cx + cy, pl.ds(hc * Rh + q * ch, ch), :]

        slots = [(i, q, j) for i in range(n) for q in range(plan[i][1]) for j in range(3)]
        sem = {(i, q, j): first[i] + 3 * q + j for i, q, j in slots}
        sent = [copy(sem[i, q, j], x_refs[i].at[pl.ds(c * plan[i][0] + q * plan[i][2], plan[i][2]), :], piece(i, x, y, c, q), (*chips[j], c))
                for i, q, j in slots]
        for cp in sent:
            cp.start()
        passed = []
        for i, q, j in slots:
            landed = piece(i, *chips[j], c, q)
            copy(sem[i, q, j], landed, landed, sibling).wait_recv()
            fwd = copy(n_sems + sem[i, q, j], landed, landed, sibling)
            fwd.start()
            passed.append(fwd)
        for i, q, j in slots:
            theirs = piece(i, *chips[j], 1 - c, q)
            copy(n_sems + sem[i, q, j], theirs, theirs, sibling).wait_recv()
        for cp in sent + passed:
            cp.wait_send()

    return pl.pallas_call(
        body, name=name, in_specs=[_HBM] * n, out_specs=[_HBM] * n,
        out_shape=[jax.ShapeDtypeStruct((N_CHIPS,) + b.shape, b.dtype) for b in bufs],
        scratch_shapes=[pltpu.SemaphoreType.DMA((2 * n_sems,)), pltpu.SemaphoreType.DMA((2 * n_sems,))],
    )(*bufs)


def _with_own(gathered, own):
    my = 2 * lax.axis_index("x") + lax.axis_index("y")
    return lax.dynamic_update_slice(gathered, own[None], (my, 0, 0))


def _sibling_exchange(ps, name):
    n = len(ps)

    def body(*refs):
        p_refs, out_refs, (send_sems, recv_sems) = refs[:n], refs[n:2 * n], refs[2 * n:]
        x, y, c = _place()
        copies = [pltpu.make_async_remote_copy(src_ref=p_ref.at[j, 1 - c], dst_ref=out_ref.at[j], send_sem=send_sems.at[N_CHIPS * i + j],
                                               recv_sem=recv_sems.at[N_CHIPS * i + j], device_id=(x, y, 1 - c), device_id_type=MESH)
                  for i, (p_ref, out_ref) in enumerate(zip(p_refs, out_refs)) for j in range(N_CHIPS)]
        for cp in copies:
            cp.start()
        for cp in copies:
            cp.wait()

    return pl.pallas_call(
        body, name=name, in_specs=[_HBM] * n, out_specs=[_HBM] * n,
        out_shape=[jax.ShapeDtypeStruct((N_CHIPS,) + p.shape[2:], p.dtype) for p in ps],
        scratch_shapes=[pltpu.SemaphoreType.DMA((N_CHIPS * n,)), pltpu.SemaphoreType.DMA((N_CHIPS * n,))],
    )(*ps)


def _chip_scatter(ss, name):
    n = len(ss)

    def body(*refs):
        s_refs, t_refs, (send_sems, recv_sems) = refs[:n], refs[n:2 * n], refs[2 * n:]
        x, y, c = _place()
        copies = [pltpu.make_async_remote_copy(src_ref=s_ref.at[j + 1], dst_ref=t_ref.at[j], send_sem=send_sems.at[3 * i + j],
                                               recv_sem=recv_sems.at[3 * i + j], device_id=(cx, cy, c), device_id_type=MESH)
                  for i, (s_ref, t_ref) in enumerate(zip(s_refs, t_refs)) for j, (cx, cy) in enumerate(_other_chips(x, y))]
        for cp in copies:
            cp.start()
        for cp in copies:
            cp.wait()

    return pl.pallas_call(
        body, name=name, in_specs=[_HBM] * n, out_specs=[_HBM] * n,
        out_shape=[jax.ShapeDtypeStruct((3,) + s.shape[1:], s.dtype) for s in ss],
        scratch_shapes=[pltpu.SemaphoreType.DMA((3 * n,)), pltpu.SemaphoreType.DMA((3 * n,))],
    )(*ss)


def _sibling_gather(fs, name):
    n = len(fs)

    def body(*refs):
        out_refs, (send_sems, recv_sems) = refs[n:2 * n], refs[2 * n:]
        x, y, c = _place()
        copies = [pltpu.make_async_remote_copy(src_ref=out_ref.at[c], dst_ref=out_ref.at[c], send_sem=send_sems.at[i], recv_sem=recv_sems.at[i],
                                               device_id=(x, y, 1 - c), device_id_type=MESH) for i, out_ref in enumerate(out_refs)]
        for cp in copies:
            cp.start()
        for cp in copies:
            cp.wait()

    return pl.pallas_call(
        body, name=name, in_specs=[_HBM] * n, out_specs=[_HBM] * n,
        out_shape=[jax.ShapeDtypeStruct(f.shape, f.dtype) for f in fs], input_output_aliases={i: i for i in range(n)},
        scratch_shapes=[pltpu.SemaphoreType.DMA((n,)), pltpu.SemaphoreType.DMA((n,))],
    )(*fs)


def _axis_scalar(name):
    return lax.axis_index(name).astype(jnp.int32).reshape(1)


def _add_own_half(p, got, out_dtype, name):
    n, _, R, C = p.shape
    tr = _pick(R, 512, 16)

    def body(x_ref, y_ref, c_ref, p_ref, g_ref, o_ref):
        o_ref[...] = (p_ref[...] + g_ref[...]).astype(out_dtype)

    def chip(r, x_ref, y_ref):
        return 2 * (x_ref[0] ^ (r & 1)) + (y_ref[0] ^ (r >> 1))

    grid_spec = pltpu.PrefetchScalarGridSpec(
        num_scalar_prefetch=3, grid=(n, R // tr),
        in_specs=[pl.BlockSpec((None, None, tr, C), lambda r, i, x_ref, y_ref, c_ref: (chip(r, x_ref, y_ref), c_ref[0], i, 0)),
                  pl.BlockSpec((None, tr, C), lambda r, i, x_ref, y_ref, c_ref: (chip(r, x_ref, y_ref), i, 0))],
        out_specs=pl.BlockSpec((None, tr, C), lambda r, i, x_ref, y_ref, c_ref: (r, i, 0)))
    return pl.pallas_call(body, name=name, grid_spec=grid_spec, out_shape=jax.ShapeDtypeStruct((n, R, C), out_dtype),
                          compiler_params=pltpu.CompilerParams(dimension_semantics=("parallel", "parallel")))(
        _axis_scalar("x"), _axis_scalar("y"), _axis_scalar("c"), p, got)


def _sum_partials(s, t, name):
    _, R, C = s.shape
    tr = _pick(R, 512, 16)

    def body(c_ref, s_ref, t_ref, o_ref):
        acc = s_ref[...].astype(F32)
        for j in range(3):
            acc = acc + t_ref[j].astype(F32)
        o_ref[...] = acc

    grid_spec = pltpu.PrefetchScalarGridSpec(
        num_scalar_prefetch=1, grid=(R // tr,),
        in_specs=[pl.BlockSpec((None, tr, C), lambda i, c_ref: (0, i, 0)), pl.BlockSpec((3, tr, C), lambda i, c_ref: (0, i, 0))],
        out_specs=pl.BlockSpec((None, tr, C), lambda i, c_ref: (c_ref[0], i, 0)))
    return pl.pallas_call(body, name=name, grid_spec=grid_spec, out_shape=jax.ShapeDtypeStruct((2, R, C), F32),
                          compiler_params=pltpu.CompilerParams(dimension_semantics=("parallel",)))(_axis_scalar("c"), s, t)


def _sibling_reduce(ps, wire_dtypes, tag):
    got = _sibling_exchange(ps, "grad_sibling_exchange_" + tag)
    return [_add_own_half(p, g, dt, "grad_sibling_add_%s%d" % (tag, i)) for i, (p, g, dt) in enumerate(zip(ps, got, wire_dtypes))]


_SEM = pl.BlockSpec(memory_space=pltpu.SEMAPHORE)
_ANY = pl.BlockSpec(memory_space=pl.ANY)
_EFFECT = pltpu.SideEffectType.DATAFLOW_SIDE_EFFECTING


def _in_hbm(a):
    return pltpu.with_memory_space_constraint(a, pltpu.HBM)


def _half_copies(x_refs, land_refs, send_sems, recv_sems, arriving):
    x, y, c = _place()
    copies = []
    for i, (x_ref, land_ref) in enumerate(zip(x_refs, land_refs)):
        Rh = x_ref.shape[0] // 2
        rows = pl.ds(c * Rh, Rh)
        for j, (cx, cy) in enumerate(_other_chips(x, y)):
            copies.append(pltpu.make_async_remote_copy(
                src_ref=x_ref.at[rows, :], dst_ref=land_ref.at[2 * cx + cy if arriving else 2 * x + y, rows, :],
                send_sem=send_sems.at[3 * i + j], recv_sem=recv_sems.at[3 * i + j], device_id=(cx, cy, c), device_id_type=MESH))
    return copies


def _allgather_start(bufs, name):
    n = len(bufs)

    def body(*refs):
        x_refs, land_refs, (send_sems, recv_sems), token = refs[:n], refs[n:2 * n], refs[2 * n:2 * n + 2], refs[-1]
        for cp in _half_copies(x_refs, land_refs, send_sems, recv_sems, False):
            cp.start()
        token[...] = jnp.zeros_like(token)

    lands = [lax.empty((N_CHIPS,) + b.shape, b.dtype) for b in bufs]
    out = pl.pallas_call(
        body, name=name,
        out_shape=(pltpu.SemaphoreType.DMA((3 * n,)), pltpu.SemaphoreType.DMA((3 * n,)), *[pltpu.HBM(a.shape, a.dtype) for a in bufs + lands],
                   jax.ShapeDtypeStruct((8, 128), F32)),
        in_specs=[_HBM] * (2 * n), out_specs=(_SEM, _SEM, *[_HBM] * (2 * n), pl.BlockSpec(memory_space=pltpu.VMEM)),
        input_output_aliases={i: 2 + i for i in range(2 * n)}, compiler_params=pltpu.CompilerParams(has_side_effects=_EFFECT),
    )(*[_in_hbm(a) for a in bufs + lands])
    return (out[0], out[1], list(out[2:2 + n]), list(out[2 + n:2 + 2 * n])), out[-1][0, 0]


def _allgather_wait(pending, after, name):
    send_sems, recv_sems, bufs, lands = pending
    n = len(bufs)

    def body(*refs):
        x_refs, land_refs, send_sems, recv_sems = refs[:n], refs[n:2 * n], refs[2 * n], refs[2 * n + 1]
        for cp in _half_copies(x_refs, land_refs, send_sems, recv_sems, False):
            cp.wait_send()
        for cp in _half_copies(x_refs, land_refs, send_sems, recv_sems, True):
            cp.wait_recv()

    out = pl.pallas_call(
        body, name=name, out_shape=tuple(pltpu.HBM(a.shape, a.dtype) for a in bufs + lands),
        in_specs=[_HBM] * (2 * n) + [_SEM, _SEM, _ANY], out_specs=tuple([_HBM] * (2 * n)), input_output_aliases={i: i for i in range(2 * n)},
        compiler_params=pltpu.CompilerParams(has_side_effects=_EFFECT),
    )(*bufs, *lands, send_sems, recv_sems, after)
    return list(out[n:])


def _sibling_forward(lands, name):
    n = len(lands)
    plan, first, n_sems = _gather_pieces([jax.ShapeDtypeStruct(l.shape[1:], l.dtype) for l in lands])

    def body(*refs):
        out_refs, (send_sems, recv_sems) = refs[n:2 * n], refs[2 * n:]
        x, y, c = _place()

        def copies(hc):
            return [pltpu.make_async_remote_copy(
                        src_ref=out_refs[i].at[2 * cx + cy, pl.ds(hc * plan[i][0] + q * plan[i][2], plan[i][2]), :],
                        dst_ref=out_refs[i].at[2 * cx + cy, pl.ds(hc * plan[i][0] + q * plan[i][2], plan[i][2]), :],
                        send_sem=send_sems.at[first[i] + 3 * q + j], recv_sem=recv_sems.at[first[i] + 3 * q + j],
                        device_id=(x, y, 1 - c), device_id_type=MESH)
                    for i in range(n) for q in range(plan[i][1]) for j, (cx, cy) in enumerate(_other_chips(x, y))]

        mine = copies(c)
        for cp in mine:
            cp.start()
        for cp in mine:
            cp.wait_send()
        for cp in copies(1 - c):
            cp.wait_recv()

    return pl.pallas_call(
        body, name=name, in_specs=[_HBM] * n, out_specs=[_HBM] * n, out_shape=[jax.ShapeDtypeStruct(l.shape, l.dtype) for l in lands],
        input_output_aliases={i: i for i in range(n)},
        scratch_shapes=[pltpu.SemaphoreType.DMA((n_sems,)), pltpu.SemaphoreType.DMA((n_sems,))],
    )(*lands)


N_PEERS = 7


def _direct_copies(p_refs, t_refs, send_sems, recv_sems):
    x, y, c = _place()
    copies = []
    for i, (p_ref, t_ref) in enumerate(zip(p_refs, t_refs)):
        for f in range(1, N_PEERS + 1):
            px, py, pc = x ^ (f >> 2), y ^ ((f >> 1) & 1), c ^ (f & 1)
            copies.append(pltpu.make_async_remote_copy(
                src_ref=p_ref.at[2 * px + py, pc], dst_ref=t_ref.at[f - 1], send_sem=send_sems.at[N_PEERS * i + f - 1],
                recv_sem=recv_sems.at[N_PEERS * i + f - 1], device_id=(px, py, pc), device_id_type=MESH))
    return copies


def _direct_scatter_start(ps, name):
    n = len(ps)

    def body(*refs):
        p_refs, t_refs, (send_sems, recv_sems), token = refs[:n], refs[n:2 * n], refs[2 * n:2 * n + 2], refs[-1]
        for cp in _direct_copies(p_refs, t_refs, send_sems, recv_sems):
            cp.start()
        token[...] = jnp.zeros_like(token)

    lands = [lax.empty((N_PEERS,) + p.shape[2:], p.dtype) for p in ps]
    out = pl.pallas_call(
        body, name=name,
        out_shape=(pltpu.SemaphoreType.DMA((N_PEERS * n,)), pltpu.SemaphoreType.DMA((N_PEERS * n,)),
                   *[pltpu.HBM(a.shape, a.dtype) for a in ps + lands], jax.ShapeDtypeStruct((8, 128), F32)),
        in_specs=[_HBM] * (2 * n), out_specs=(_SEM, _SEM, *[_HBM] * (2 * n), pl.BlockSpec(memory_space=pltpu.VMEM)),
        input_output_aliases={i: 2 + i for i in range(2 * n)}, compiler_params=pltpu.CompilerParams(has_side_effects=_EFFECT),
    )(*[_in_hbm(a) for a in ps + lands])
    return (out[0], out[1], list(out[2:2 + n]), list(out[2 + n:2 + 2 * n])), out[-1][0, 0]


def _direct_scatter_wait(pending, after, name):
    send_sems, recv_sems, ps, lands = pending
    n = len(ps)

    def body(*refs):
        p_refs, t_refs, send_sems, recv_sems = refs[:n], refs[n:2 * n], refs[2 * n], refs[2 * n + 1]
        for cp in _direct_copies(p_refs, t_refs, send_sems, recv_sems):
            cp.wait_send()
            cp.wait_recv()

    out = pl.pallas_call(
        body, name=name, out_shape=tuple(pltpu.HBM(a.shape, a.dtype) for a in ps + lands),
        in_specs=[_HBM] * (2 * n) + [_SEM, _SEM, _ANY], out_specs=tuple([_HBM] * (2 * n)),
        input_output_aliases={i: i for i in range(2 * n)}, compiler_params=pltpu.CompilerParams(has_side_effects=_EFFECT),
    )(*ps, *lands, send_sems, recv_sems, after)
    return list(out[:n]), list(out[n:])


def _sum_direct(p, t, name):
    _, _, R, C = p.shape
    tr = _pick(R, 512, 16)

    def body(x_ref, y_ref, c_ref, p_ref, t_ref, o_ref):
        acc = p_ref[...].astype(F32)
        for f in range(N_PEERS):
            acc = acc + t_ref[f].astype(F32)
        o_ref[...] = acc

    grid_spec = pltpu.PrefetchScalarGridSpec(
        num_scalar_prefetch=3, grid=(R // tr,),
        in_specs=[pl.BlockSpec((None, None, tr, C), lambda i, x_ref, y_ref, c_ref: (2 * x_ref[0] + y_ref[0], c_ref[0], i, 0)),
                  pl.BlockSpec((N_PEERS, tr, C), lambda i, x_ref, y_ref, c_ref: (0, i, 0))],
        out_specs=pl.BlockSpec((None, tr, C), lambda i, x_ref, y_ref, c_ref: (c_ref[0], i, 0)))
    return pl.pallas_call(body, name=name, grid_spec=grid_spec, out_shape=jax.ShapeDtypeStruct((2, R, C), F32),
                          compiler_params=pltpu.CompilerParams(dimension_semantics=("parallel",)))(
        _axis_scalar("x"), _axis_scalar("y"), _axis_scalar("c"), p, t)


def _adamw(w, g, m, v, name):
    R, C = w.shape
    tr = _pick(R, 256, 8)

    def body(w_ref, g_ref, m_ref, v_ref, d_ref, nm_ref, nv_ref):
        g_ = g_ref[...]
        m_ = ADAM_B1 * m_ref[...] + (1.0 - ADAM_B1) * g_
        v_ = ADAM_B2 * v_ref[...] + (1.0 - ADAM_B2) * (g_ * g_)
        m_hat = m_ / (1.0 - ADAM_B1 ** ADAM_STEP)
        v_hat = v_ / (1.0 - ADAM_B2 ** ADAM_STEP)
        d_ref[...] = -ADAM_LR * (m_hat / (jnp.sqrt(v_hat) + ADAM_EPS) + ADAM_WD * w_ref[...])
        nm_ref[...] = m_
        nv_ref[...] = v_

    row = pl.BlockSpec((tr, C), lambda i: (i, 0))
    shp = jax.ShapeDtypeStruct((R, C), F32)
    return pl.pallas_call(body, name=name, grid=(R // tr,), in_specs=[row] * 4, out_specs=[row] * 3, out_shape=[shp] * 3,
                          compiler_params=pltpu.CompilerParams(dimension_semantics=("parallel",)))(w, g, m, v)


BIG_SPECS = (("ev_w_in", 1024, 1440, 1), ("ev_w_uq", 256, 768, 1), ("ev_w_ukv", 128, 1024, 1), ("ev_w_out", 1024, 1024, 0),
             ("od_w_in", 1024, 6144, 1), ("od_w_out", 2048, 1024, 0), ("mlp_w1_0", 1024, 4096, 1), ("mlp_w1_1", 1024, 4096, 1),
             ("mlp_w2_0", 4096, 1024, 0), ("mlp_w2_1", 4096, 1024, 0))
BIG_PARAMS = (("ev_w_in", ("ev_w_in",)), ("ev_w_uq", ("ev_w_uq",)), ("ev_w_ukv", ("ev_w_ukv",)), ("ev_w_out", ("ev_w_out",)),
              ("od_w_in", ("od_w_in",)), ("od_w_out", ("od_w_out",)), ("mlp_w1", ("mlp_w1_0", "mlp_w1_1")),
              ("mlp_w2", ("mlp_w2_0", "mlp_w2_1")))
REPLICATED = ("ev_conv_b", "ev_w_rg_a", "ev_b_rg_a", "ev_w_rg_x", "ev_b_rg_x", "ev_lru_lambda", "ev_q_norm_g", "ev_kv_norm_g",
              "ln_mix_g", "ln_mix_b", "ln_mlp_g", "ln_mlp_b")
SMALL_SHARDED = ("meta_tokens", "ev_conv_w")
COL_SHARD_GRADS = ("od_w_in", "mlp_w1_0", "mlp_w1_1")
MATRIX_GROUPS = (("ev_w_in", "ev_w_uq", "ev_w_ukv", "ev_w_out"), ("mlp_w1_0", "mlp_w2_0"), ("od_w_in", "od_w_out", "mlp_w1_1", "mlp_w2_1"))
LAYER_NORMS = ("ln_mix_g", "ln_mix_b", "ln_mlp_g", "ln_mlp_b")
WEIGHT_NAMES = ("meta_tokens", "ev_w_in", "ev_conv_w", "ev_conv_b", "ev_w_rg_a", "ev_b_rg_a", "ev_w_rg_x", "ev_b_rg_x",
                "ev_lru_lambda", "ev_q_norm_g", "ev_w_uq", "ev_kv_norm_g", "ev_w_ukv", "ev_w_out", "od_w_in", "od_w_out",
                "ln_mix_g", "ln_mix_b", "mlp_w1", "mlp_w2", "ln_mlp_g", "ln_mlp_b")


def _to_rows(flat, row_align):
    n = flat.shape[-1]
    rows = _round_up(-(-n // PACK_COLS), row_align)
    pad = rows * PACK_COLS - n
    if pad:
        flat = jnp.pad(flat, [(0, 0)] * (flat.ndim - 1) + [(0, pad)])
    return flat.reshape(flat.shape[:-1] + (rows, PACK_COLS))


def _shard_shape(K, N, axis):
    return (K // N_CHIPS, N) if axis == 0 else (K, N // N_CHIPS)


def _gather_shards(stacked, K, N, axis):
    if axis == 0:
        return stacked.reshape(K, N)
    return stacked.transpose(1, 0, 2).reshape(K, N)


def _split_shards(full, K, N, axis):
    if axis == 0:
        return full.reshape(N_CHIPS, -1)
    return full.reshape(K, N_CHIPS, N // N_CHIPS).transpose(1, 0, 2).reshape(N_CHIPS, -1)


def kernel(x, meta_tokens, ev_w_in, ev_conv_w, ev_conv_b, ev_w_rg_a, ev_b_rg_a, ev_w_rg_x, ev_b_rg_x, ev_lru_lambda, ev_q_norm_g, ev_w_uq, ev_kv_norm_g, ev_w_ukv, ev_w_out, od_w_in, od_w_out, ln_mix_g, ln_mix_b, mlp_w1, mlp_w2, ln_mlp_g, ln_mlp_b, loss_target, m_meta_tokens, m_ev_w_in, m_ev_conv_w, m_ev_conv_b, m_ev_w_rg_a, m_ev_b_rg_a, m_ev_w_rg_x, m_ev_b_rg_x, m_ev_lru_lambda, m_ev_q_norm_g, m_ev_w_uq, m_ev_kv_norm_g, m_ev_w_ukv, m_ev_w_out, m_od_w_in, m_od_w_out, m_ln_mix_g, m_ln_mix_b, m_mlp_w1, m_mlp_w2, m_ln_mlp_g, m_ln_mlp_b, v_meta_tokens, v_ev_w_in, v_ev_conv_w, v_ev_conv_b, v_ev_w_rg_a, v_ev_b_rg_a, v_ev_w_rg_x, v_ev_b_rg_x, v_ev_lru_lambda, v_ev_q_norm_g, v_ev_w_uq, v_ev_kv_norm_g, v_ev_w_ukv, v_ev_w_out, v_od_w_in, v_od_w_out, v_ln_mix_g, v_ln_mix_b, v_mlp_w1, v_mlp_w2, v_ln_mlp_g, v_ln_mlp_b):
    given = dict(locals())
    local_big = {"ev_w_in": ev_w_in[0], "ev_w_uq": ev_w_uq[0], "ev_w_ukv": ev_w_ukv[0], "ev_w_out": ev_w_out[0],
                 "od_w_in": od_w_in[0], "od_w_out": od_w_out[0], "mlp_w1_0": mlp_w1[0], "mlp_w1_1": mlp_w1[1],
                 "mlp_w2_0": mlp_w2[0], "mlp_w2_1": mlp_w2[1]}

    specs = {spec[0]: spec for spec in BIG_SPECS}
    mixer0_m, mlp0_m, layer1_m = MATRIX_GROUPS

    def shards(names):
        return [local_big[n].astype(MXU_DTYPE) for n in names]

    def whole(stacked, n):
        _, K, N, ax = specs[n]
        return stacked if n in COL_SHARD_GRADS else _gather_shards(stacked, K, N, ax)

    def filled(gathered, own, names):
        return {n: whole(_with_own(g_, o_), n) for n, g_, o_ in zip(names, gathered, own)}

    own_a, own_b, own_c = shards(mixer0_m), shards(mlp0_m), shards(layer1_m)
    small = [meta_tokens, jnp.pad(ev_conv_w[0], ((0, 16 - CONV_WIDTH), (0, 0)))]
    gathered_a = _allgather_chips(own_a + small, "weight_allgather_mixer0")
    pending_b, token1 = _allgather_start(own_b, "weight_allgather_mlp0_start")
    pending_c, token2 = _allgather_start(own_c, "weight_allgather_layer1_start")
    meta_full = _gather_shards(_with_own(gathered_a[-2], small[0]), N_META, D_MODEL, 1)
    conv_full = _gather_shards(_with_own(gathered_a[-1], small[1])[:, :CONV_WIDTH], CONV_WIDTH, LRU_WIDTH, 1)

    def slots(names, dtype):
        return {n: jnp.zeros((N_CHIPS, specs[n][1], specs[n][2] // N_CHIPS) if n in COL_SHARD_GRADS else specs[n][1:3], dtype) for n in names}

    def norms(names, layer):
        return {n: given[n][layer] for n in names}

    def finish_gather(pending, own, after, names, tag):
        landed = _allgather_wait(pending, lax.stop_gradient(after), "weight_allgather_%s_wait" % tag)
        return filled(_sibling_forward(landed, "weight_allgather_%s_forward" % tag), own, names)

    diff_a = {**slots(mixer0_m, F32), **norms(("ln_mix_g", "ln_mix_b"), 0), **{n: given[n] for n in REPLICATED if n not in LAYER_NORMS},
              "x": x, "meta_tokens": meta_full, "ev_conv_w": conv_full}
    diff_b = {**slots(mlp0_m, MXU_DTYPE), **norms(("ln_mlp_g", "ln_mlp_b"), 0)}
    diff_c = {**slots(layer1_m, MXU_DTYPE), **norms(LAYER_NORMS, 1)}
    w_a = filled(gathered_a[:len(mixer0_m)], own_a, mixer0_m)
    h_a, back_a = jax.vjp(lambda d: _mixer0(d, w_a, token1 + token2), diff_a)
    w_b = finish_gather(pending_b, own_b, h_a, mlp0_m, "mlp0")
    h_b, back_b = jax.vjp(lambda d, hh: _mlp0(d, hh, w_b), diff_b, h_a)
    w_c = finish_gather(pending_c, own_c, h_b, layer1_m, "layer1")
    loss, back_c = jax.vjp(lambda d, hh: _layer1_loss(d, hh, w_c, loss_target), diff_c, h_b)
    loss = lax.psum(loss, ("x", "y", "c"))

    def blocks_of(grad, n):
        _, K, N, ax = specs[n]
        if n in COL_SHARD_GRADS:
            blocks = grad
        elif ax == 0:
            blocks = grad.reshape(N_CHIPS, K // N_CHIPS, N)
        else:
            blocks = grad.reshape(K, N_CHIPS, N // N_CHIPS).transpose(1, 0, 2)
        return blocks.reshape(N_CHIPS, 2, blocks.shape[1] // 2, blocks.shape[2])

    def start_reduce(grads_of, names, tag):
        return _direct_scatter_start([blocks_of(grads_of[n], n) for n in names], "grad_scatter_%s_start" % tag)

    g_c, dh = back_c(jnp.ones((), F32))
    flying_c, token = start_reduce(g_c, layer1_m, "layer1")
    g_b, dh = back_b(dh + token)
    flying_b, token = start_reduce(g_b, mlp0_m, "mlp0")
    (g_a,) = back_a(dh + token)
    ps_c, ts_c = _direct_scatter_wait(flying_c, g_a["x"], "grad_scatter_layer1_wait")
    ps_b, ts_b = _direct_scatter_wait(flying_b, g_a["x"], "grad_scatter_mlp0_wait")

    g = {**g_a, **g_b, **g_c}
    g.update({n: jnp.stack([(g_b if n in g_b else g_a)[n], g_c[n]]) for n in LAYER_NORMS})
    repl = jnp.concatenate([g[n].reshape(-1) for n in REPLICATED]).reshape(N_CHIPS, -1)
    small = [_split_shards(g["meta_tokens"], N_META, D_MODEL, 1), _split_shards(g["ev_conv_w"], CONV_WIDTH, LRU_WIDTH, 1), repl]
    small = [pc.reshape(N_CHIPS, 2, -1) for pc in small]
    n_small = sum(pc.shape[2] for pc in small)
    small.append(jnp.zeros((N_CHIPS, 2, _round_up(n_small, 32 * PACK_COLS) - n_small), F32))
    p_small = jnp.concatenate(small, axis=2).reshape(N_CHIPS, 2, -1, PACK_COLS)
    ss_a = _sibling_reduce([blocks_of(g_a[n], n) for n in mixer0_m] + [p_small], [MXU_DTYPE] * len(mixer0_m) + [F32], "mixer0_")
    ts_a = list(_chip_scatter(ss_a, "grad_chip_scatter_mixer0"))
    fs = [_sum_partials(s, t, "grad_chip_sum_mixer0_%d" % i) for i, (s, t) in enumerate(zip(ss_a, ts_a))]
    fs += [_sum_direct(p, t, "grad_sum_%d" % i) for i, (p, t) in enumerate(zip(ps_b + ps_c, ts_b + ts_c))]
    reduced = _sibling_gather(fs, "grad_sibling_gather")
    red_big = dict(zip(mixer0_m + ("small",) + mlp0_m + layer1_m, reduced))
    red_small = red_big.pop("small").reshape(2, -1)

    grads = {}
    for name, parts in BIG_PARAMS:
        grads[name] = jnp.stack([red_big[part].reshape(given[name].shape[1:]) for part in parts])

    def take(off, sz):
        return jnp.concatenate([red_small[0, off // 2:(off + sz) // 2], red_small[1, off // 2:(off + sz) // 2]])

    off = 0
    for name in SMALL_SHARDED:
        sz = given[name].size
        grads[name] = take(off, sz).reshape(given[name].shape)
        off += sz
    n_repl = repl.shape[1]
    own_repl = _to_rows(take(off, n_repl), 16)
    repl_all = _with_own(_allgather_chips([own_repl], "replicated_allgather")[0], own_repl).reshape(N_CHIPS, -1)[:, :n_repl].reshape(-1)
    off = 0
    for name in REPLICATED:
        sz = given[name].size
        grads[name] = repl_all[off:off + sz].reshape(given[name].shape)
        off += sz

    delta, new_m, new_v = {}, {}, {}
    for name, _ in BIG_PARAMS:
        shp = given[name].shape
        two_d = (-1, shp[-1])
        d, nm, nv = _adamw(given[name].reshape(two_d), grads[name].reshape(two_d), given["m_" + name].reshape(two_d),
                           given["v_" + name].reshape(two_d), "adamw_" + name)
        delta[name], new_m[name], new_v[name] = d.reshape(shp), nm.reshape(shp), nv.reshape(shp)
    smalls = SMALL_SHARDED + REPLICATED

    def pack_small(get):
        return _to_rows(jnp.concatenate([get(n).reshape(-1) for n in smalls]), 8)

    outs = _adamw(pack_small(lambda n: given[n]), pack_small(lambda n: grads[n]), pack_small(lambda n: given["m_" + n]),
                  pack_small(lambda n: given["v_" + n]), "adamw_small")
    for res, flat in zip((delta, new_m, new_v), outs):
        flat, off = flat.reshape(-1), 0
        for n in smalls:
            sz = given[n].size
            res[n] = flat[off:off + sz].reshape(given[n].shape)
            off += sz

    return (loss, g_a["x"], *[grads[n] for n in WEIGHT_NAMES], *[delta[n] for n in WEIGHT_NAMES],
            *[new_m[n] for n in WEIGHT_NAMES], *[new_v[n] for n in WEIGHT_NAMES])
```

```python
import functools
import math

import jax
import jax.numpy as jnp
from jax import lax
from jax.experimental import pallas as pl
from jax.experimental.pallas import tpu as pltpu

F32 = jnp.float32
MXU_DTYPE = jnp.bfloat16

D_MODEL = 1024
N_META = 16
LRU_WIDTH = 512
LRU_HEADS = 4
LRU_HEAD_DIM = 128
CONV_WIDTH = 4
LRU_C = 8.0
MLA_HEADS = 8
MLA_NOPE = 64
MLA_ROPE = 32
MLA_V = 64
MLA_Q_RANK = 256
MLA_KV_RANK = 128
RET_HEADS = 4
RET_QK_DIM = 256
RET_V_DIM = 512
D_FF = 4096
ROPE_BASE = 10000.0
DN_ALPHA = 4.0 ** 0.25
EPS = 1e-5
NEG_INF = -1e30
SEQ_BLOCK = 128

ADAM_LR = 0.001
ADAM_B1 = 0.9
ADAM_B2 = 0.999
ADAM_EPS = 1e-08
ADAM_WD = 0.01
ADAM_STEP = 10

PACK_COLS = 1024
N_CHIPS = 4

MESH = pl.DeviceIdType.MESH


def _pick(n, target, align):
    best = None
    for t in range(align, min(n, target) + 1, align):
        if n % t == 0:
            best = t
    return n if best is None else best


def _round_up(n, m):
    return (n + m - 1) // m * m


def _relu2(a):
    r = jnp.maximum(a, 0.0)
    return r * r


def _ln_stats(z):
    mu = jnp.mean(z, axis=-1, keepdims=True)
    zc = z - mu
    var = jnp.mean(zc * zc, axis=-1, keepdims=True)
    return zc, lax.rsqrt(var + EPS)


def _mm_nn(a, w, act, name, out_dtype=F32, norm=None):
    M, K = a.shape
    sharded = w.ndim == 3
    n = w.shape[-1]
    N = n * (w.shape[0] if sharded else 1)
    tm = _pick(M, 1088 if K * a.dtype.itemsize <= 4096 and norm is None else 544, 8)
    tn = _pick(n, 1024, 128)
    per = n // tn
    assert norm is None or tn == N

    def body(a_ref, w_ref, *rest):
        av = a_ref[...]
        if act:
            av = _relu2(av.astype(F32))
        r = jnp.dot(av.astype(MXU_DTYPE), w_ref[...].astype(MXU_DTYPE), preferred_element_type=F32)
        if norm is None:
            rest[0][...] = r.astype(out_dtype)
        else:
            r_ref, g_ref, b_ref, o_ref, z_ref = rest
            z = DN_ALPHA * r_ref[...] + r
            zc, rstd = _ln_stats(z)
            z_ref[...] = z
            o_ref[...] = zc * rstd * g_ref[...] + b_ref[...]

    w_spec = pl.BlockSpec((None, K, tn), lambda i, j: (j // per, 0, j % per)) if sharded else pl.BlockSpec((K, tn), lambda i, j: (0, j))
    tile = pl.BlockSpec((tm, tn), lambda i, j: (i, j))
    in_specs, args = [pl.BlockSpec((tm, K), lambda i, j: (i, 0)), w_spec], [a, w]
    if norm is None:
        out_specs, out_shape = tile, jax.ShapeDtypeStruct((M, N), out_dtype)
    else:
        vec = pl.BlockSpec((1, N), lambda i, j: (0, 0))
        in_specs += [tile, vec, vec]
        args += [norm[0], norm[1].reshape(1, N), norm[2].reshape(1, N)]
        out_specs, out_shape = [tile, tile], [jax.ShapeDtypeStruct((M, N), F32)] * 2
    return pl.pallas_call(
        body, name=name, grid=(M // tm, N // tn), in_specs=in_specs, out_specs=out_specs, out_shape=out_shape,
        compiler_params=pltpu.CompilerParams(dimension_semantics=("parallel", "arbitrary")),
    )(*args)


def _mm_nt(g, w, a_src, name, out_dtype=F32, plus=None):
    M, N = g.shape
    sharded = w.ndim == 3
    K, n = w.shape[-2], w.shape[-1]
    if sharded:
        tk, nk = N, 1
    else:
        tk = N if N * g.dtype.itemsize <= 8192 else _pick(N, 2048, 128)
        nk = N // tk
    tm = _pick(M, 1088 if tk * g.dtype.itemsize <= 4096 else 544, 8)
    tn = _pick(K, 1024, 128)
    has_src = a_src is not None
    assert nk == 1 or out_dtype == F32
    assert plus is None or not has_src

    def body(*refs):
        if has_src:
            g_ref, w_ref, s_ref, o_ref = refs
        elif plus is not None:
            g_ref, w_ref, p_ref, o_ref = refs
        else:
            g_ref, w_ref, o_ref = refs
        nt = (((1,), (1,)), ((), ()))
        if sharded:
            r = sum(lax.dot_general(g_ref[:, s * n:(s + 1) * n].astype(MXU_DTYPE), w_ref[s].astype(MXU_DTYPE), nt, preferred_element_type=F32)
                    for s in range(w_ref.shape[0]))
        else:
            r = lax.dot_general(g_ref[...].astype(MXU_DTYPE), w_ref[...].astype(MXU_DTYPE), nt, preferred_element_type=F32)
        if has_src:
            r = r * (2.0 * jnp.maximum(s_ref[...].astype(F32), 0.0))
        first = r if plus is None else r + DN_ALPHA * p_ref[...]
        if nk == 1:
            o_ref[...] = first.astype(out_dtype)
        else:
            k = pl.program_id(2)

            @pl.when(k == 0)
            def _():
                o_ref[...] = first

            @pl.when(k > 0)
            def _():
                o_ref[...] += r

    w_spec = (pl.BlockSpec((w.shape[0], tn, n), lambda i, j, k: (0, j, 0)) if sharded
              else pl.BlockSpec((tn, tk), lambda i, j, k: (j, k)))
    in_specs = [pl.BlockSpec((tm, tk), lambda i, j, k: (i, k)), w_spec]
    args = [g, w]
    if has_src:
        assert nk == 1
        in_specs.append(pl.BlockSpec((tm, tn), lambda i, j, k: (i, j)))
        args.append(a_src)
    if plus is not None:
        in_specs.append(pl.BlockSpec((tm, tn), lambda i, j, k: (i, j)))
        args.append(plus)
    return pl.pallas_call(
        body, name=name,
        grid=(M // tm, K // tn, nk),
        in_specs=in_specs,
        out_specs=pl.BlockSpec((tm, tn), lambda i, j, k: (i, j)),
        out_shape=jax.ShapeDtypeStruct((M, K), out_dtype),
        compiler_params=pltpu.CompilerParams(dimension_semantics=("parallel", "parallel", "arbitrary")),
    )(*args)


def _mm_tn(a, g, act, name, col_shards=1, out_dtype=F32):
    M, K = a.shape
    _, N = g.shape
    n = N // col_shards
    tm, tn, tk = _pick(K, 1024, 128), _pick(n, 1024, 128), _pick(M, 2176, 8)
    nk = M // tk
    per = n // tn
    direct = out_dtype == F32

    def body(a_ref, g_ref, o_ref, *scratch):
        acc_ref = o_ref if direct else scratch[0]
        k = pl.program_id(2)
        av = a_ref[...]
        if act:
            av = _relu2(av.astype(F32))
        r = lax.dot_general(av.astype(MXU_DTYPE), g_ref[...].astype(MXU_DTYPE),
                            (((0,), (0,)), ((), ())), preferred_element_type=F32)

        @pl.when(k == 0)
        def _():
            acc_ref[...] = r

        @pl.when(k > 0)
        def _():
            acc_ref[...] += r

        if not direct:
            @pl.when(k == nk - 1)
            def _():
                o_ref[...] = acc_ref[...].astype(out_dtype)

    if col_shards == 1:
        out_spec, out_shape = pl.BlockSpec((tm, tn), lambda i, j, k: (i, j)), (K, N)
    else:
        out_spec, out_shape = pl.BlockSpec((None, tm, tn), lambda i, j, k: (j // per, i, j % per)), (col_shards, K, n)
    return pl.pallas_call(
        body, name=name,
        grid=(K // tm, N // tn, nk),
        in_specs=[pl.BlockSpec((tk, tm), lambda i, j, k: (k, i)), pl.BlockSpec((tk, tn), lambda i, j, k: (k, j))],
        out_specs=out_spec,
        out_shape=jax.ShapeDtypeStruct(out_shape, out_dtype),
        scratch_shapes=[] if direct else [pltpu.VMEM((tm, tn), F32)],
        compiler_params=pltpu.CompilerParams(dimension_semantics=("parallel", "parallel", "arbitrary")),
    )(a, g)


@functools.partial(jax.custom_vjp, nondiff_argnums=(3, 4, 5, 6))
def matmul(a, w, w_grad_slot, act, name, out_dtype, col_shards):
    return _mm_nn(a, w, act, name + "_fwd", out_dtype)


def _matmul_fwd(a, w, w_grad_slot, act, name, out_dtype, col_shards):
    return _mm_nn(a, w, act, name + "_fwd", out_dtype), (a, w, jnp.zeros((), w_grad_slot.dtype))


def _matmul_bwd(act, name, out_dtype, col_shards, res, g):
    a, w, slot_like = res
    w_grad_dtype = slot_like.dtype
    da = _mm_nt(g, w, a if act else None, name + "_dx")
    dw = _mm_tn(a, g, act, name + "_dw", col_shards, w_grad_dtype)
    return da, None, dw


matmul.defvjp(_matmul_fwd, _matmul_bwd)


def _ln_bwd_call(z, g, dy, name):
    M, D = z.shape
    tm = _pick(M, 544, 8)

    def body(z_ref, g_ref, dy_ref, dz_ref, dg_ref, db_ref):
        @pl.when(pl.program_id(0) == 0)
        def _():
            dg_ref[...] = jnp.zeros_like(dg_ref)
            db_ref[...] = jnp.zeros_like(db_ref)

        zc, rstd = _ln_stats(z_ref[...])
        xhat = zc * rstd
        dy = dy_ref[...]
        dxh = dy * g_ref[...]
        m1 = jnp.mean(dxh, axis=-1, keepdims=True)
        m2 = jnp.mean(dxh * xhat, axis=-1, keepdims=True)
        dz_ref[...] = rstd * (dxh - m1 - xhat * m2)
        dg_ref[...] += jnp.sum(dy * xhat, axis=0, keepdims=True)
        db_ref[...] += jnp.sum(dy, axis=0, keepdims=True)

    row = pl.BlockSpec((tm, D), lambda i: (i, 0))
    vec = pl.BlockSpec((1, D), lambda i: (0, 0))
    return pl.pallas_call(
        body, name=name, grid=(M // tm,), in_specs=[row, vec, row], out_specs=[row, vec, vec],
        out_shape=[jax.ShapeDtypeStruct((M, D), F32), jax.ShapeDtypeStruct((1, D), F32), jax.ShapeDtypeStruct((1, D), F32)],
        compiler_params=pltpu.CompilerParams(dimension_semantics=("arbitrary",)),
    )(z, g.reshape(1, D), dy)


@functools.partial(jax.custom_vjp, nondiff_argnums=(7,))
def mlp_block(h, w1, w2, w1_grad_slot, w2_grad_slot, g, b, name):
    return _mlp_block_fwd(h, w1, w2, w1_grad_slot, w2_grad_slot, g, b, name)[0]


def _mlp_block_fwd(h, w1, w2, w1_grad_slot, w2_grad_slot, g, b, name):
    u = _mm_nn(h, w1, False, name + "_w1_fwd", out_dtype=MXU_DTYPE)
    out, z = _mm_nn(u, w2, True, name + "_w2_norm_fwd", norm=(h, g, b))
    return out, (h, u, z, w1, w2, g, jnp.zeros((), w1_grad_slot.dtype))


def _mlp_block_bwd(name, res, dy):
    h, u, z, w1, w2, g, slot_like = res
    dz, dg, db = _ln_bwd_call(z, g, dy, name + "_norm_bwd")
    du = _mm_nt(dz, w2, u, name + "_w2_dx", out_dtype=MXU_DTYPE)
    dw2 = _mm_tn(u, dz, True, name + "_w2_dw", 1, slot_like.dtype)
    dh = _mm_nt(du, w1, None, name + "_w1_dx", plus=dz)
    dw1 = _mm_tn(h, du, False, name + "_w1_dw", N_CHIPS, slot_like.dtype)
    return dh, None, None, dw1, dw2, dg.reshape(g.shape), db.reshape(g.shape)


mlp_block.defvjp(_mlp_block_fwd, _mlp_block_bwd)


@functools.partial(jax.custom_vjp, nondiff_argnums=(6,))
def out_block(h, y, w, w_grad_slot, g, b, name):
    return _out_block_fwd(h, y, w, w_grad_slot, g, b, name)[0]


def _out_block_fwd(h, y, w, w_grad_slot, g, b, name):
    out, z = _mm_nn(y, w, False, name + "_norm_fwd", norm=(h, g, b))
    return out, (y, z, w, g, jnp.zeros((), w_grad_slot.dtype))


def _out_block_bwd(name, res, dy):
    y, z, w, g, slot_like = res
    dz, dg, db = _ln_bwd_call(z, g, dy, name + "_norm_bwd")
    d_y = _mm_nt(dz, w, None, name + "_dx")
    dw = _mm_tn(y, dz, False, name + "_dw", 1, slot_like.dtype)
    return DN_ALPHA * dz, d_y, None, dw, dg.reshape(g.shape), db.reshape(g.shape)


out_block.defvjp(_out_block_fwd, _out_block_bwd)


def _rms_fwd_call(x, g, name, col_block=0):
    R = x.shape[0]
    W = g.shape[-1]
    tr = _pick(R, 1088, 8)

    def body(x_ref, g_ref, o_ref):
        xv = x_ref[...]
        rstd = lax.rsqrt(jnp.mean(xv * xv, axis=-1, keepdims=True) + EPS)
        o_ref[...] = xv * rstd * g_ref[...]

    vec = pl.BlockSpec((1, W), lambda i: (0, 0))
    return pl.pallas_call(
        body, name=name, grid=(R // tr,), in_specs=[pl.BlockSpec((tr, W), lambda i: (i, col_block)), vec],
        out_specs=pl.BlockSpec((tr, W), lambda i: (i, 0)), out_shape=jax.ShapeDtypeStruct((R, W), F32),
        compiler_params=pltpu.CompilerParams(dimension_semantics=("parallel",)),
    )(x, g.reshape(1, W))


def _rms_bwd_call(x, g, dy, name, col_block=0):
    R = x.shape[0]
    W = g.shape[-1]
    tr = _pick(R, 1088, 8)

    def body(x_ref, g_ref, dy_ref, dx_ref, dg_ref):
        @pl.when(pl.program_id(0) == 0)
        def _():
            dg_ref[...] = jnp.zeros_like(dg_ref)

        xv = x_ref[...]
        rstd = lax.rsqrt(jnp.mean(xv * xv, axis=-1, keepdims=True) + EPS)
        xhat = xv * rstd
        dy = dy_ref[...]
        dxh = dy * g_ref[...]
        dx_ref[...] = rstd * (dxh - xhat * jnp.mean(dxh * xhat, axis=-1, keepdims=True))
        dg_ref[...] += jnp.sum(dy * xhat, axis=0, keepdims=True)

    row = pl.BlockSpec((tr, W), lambda i: (i, 0))
    vec = pl.BlockSpec((1, W), lambda i: (0, 0))
    return pl.pallas_call(
        body, name=name, grid=(R // tr,), in_specs=[pl.BlockSpec((tr, W), lambda i: (i, col_block)), vec, row], out_specs=[row, vec],
        out_shape=[jax.ShapeDtypeStruct((R, W), F32), jax.ShapeDtypeStruct((1, W), F32)],
        compiler_params=pltpu.CompilerParams(dimension_semantics=("arbitrary",)),
    )(x, g.reshape(1, W), dy)


def _loss_call(h, tgt, n_tokens, name):
    B, Tp, D = h.shape
    tr = _pick(Tp, 544, 8)

    def body(y_ref, t_ref, dy_ref, acc_ref):
        @pl.when(jnp.logical_and(pl.program_id(0) == 0, pl.program_id(1) == 0))
        def _():
            acc_ref[...] = jnp.zeros_like(acc_ref)

        t = lax.broadcasted_iota(jnp.int32, (tr, 1), 0) + pl.program_id(1) * tr
        counts = jnp.logical_and(t >= N_META, t < N_META + n_tokens)
        e = jnp.where(counts, y_ref[...] - t_ref[...], 0.0)
        dy_ref[...] = e * (1.0 / D)
        acc_ref[...] += jnp.sum(jnp.sum(e * e, axis=-1, keepdims=True), axis=0, keepdims=True) * (0.5 / D)

    row = pl.BlockSpec((None, tr, D), lambda b, i: (b, i, 0))
    one = pl.BlockSpec((1, 1), lambda b, i: (0, 0))
    return pl.pallas_call(
        body, name=name, grid=(B, Tp // tr), in_specs=[row, row], out_specs=[row, one],
        out_shape=[jax.ShapeDtypeStruct((B, Tp, D), F32), jax.ShapeDtypeStruct((1, 1), F32)],
        compiler_params=pltpu.CompilerParams(dimension_semantics=("arbitrary", "arbitrary")),
    )(h, tgt)


@functools.partial(jax.custom_vjp, nondiff_argnums=(2,))
def loss_head(h, tgt, n_tokens):
    return _loss_call(h, tgt, n_tokens, "loss_head")[1][0, 0]


def _loss_head_fwd(h, tgt, n_tokens):
    dy, acc = _loss_call(h, tgt, n_tokens, "loss_head")
    return acc[0, 0], dy


def _loss_head_bwd(n_tokens, dy, ct):
    return ct * dy, None


loss_head.defvjp(_loss_head_fwd, _loss_head_bwd)


_GELU_C = math.sqrt(2.0 / math.pi)


def _gelu_parts(x):
    x2 = x * x
    t = jnp.tanh(_GELU_C * (x + 0.044715 * x * x2))
    gelu = 0.5 * x * (1.0 + t)
    dgelu = 0.5 * (1.0 + t) + 0.5 * x * (1.0 - t * t) * (_GELU_C * (1.0 + 3.0 * 0.044715 * x2))
    return gelu, dgelu


def _sigmoid(x):
    return 1.0 / (1.0 + jnp.exp(-x))


def _scan8(a, b, carry, reverse):
    row = lax.broadcasted_iota(jnp.int32, a.shape, 0)
    for s in (1, 2, 4):
        shift = 8 - s if reverse else s
        keep = (row < 8 - s) if reverse else (row >= s)
        b = jnp.where(keep, a * pltpu.roll(b, shift, 0) + b, b)
        a = jnp.where(keep, a * pltpu.roll(a, shift, 0), a)
    return a * carry + b


def _lru_pre(prec_ref, prev_ref, first, cw_ref, cb_ref, wa_ref, ba_ref, wx_ref, bx_ref, sp_ref):
    tc = prec_ref.shape[0]
    prev = jnp.where(first, 0.0, prev_ref[...])
    ext = jnp.concatenate([prev, prec_ref[...]], axis=0)
    cw = cw_ref[...]
    taps = [ext[8:] if k == CONV_WIDTH - 1 else pltpu.roll(ext, CONV_WIDTH - 1 - k, 0)[8:] for k in range(CONV_WIDTH)]
    xc = cb_ref[...] + sum(cw[k:k + 1, :] * taps[k] for k in range(CONV_WIDTH))
    ga, gx = [], []
    for h in range(LRU_HEADS):
        xh = xc[:, h * LRU_HEAD_DIM:(h + 1) * LRU_HEAD_DIM].astype(MXU_DTYPE)
        ga.append(jnp.dot(xh, wa_ref[h].astype(MXU_DTYPE), preferred_element_type=F32))
        gx.append(jnp.dot(xh, wx_ref[h].astype(MXU_DTYPE), preferred_element_type=F32))
    r = _sigmoid(jnp.concatenate(ga, axis=1) + ba_ref[...])
    i = _sigmoid(jnp.concatenate(gx, axis=1) + bx_ref[...])
    log_a = -LRU_C * r * sp_ref[...]
    a = jnp.exp(log_a)
    a2 = a * a
    mult = jnp.sqrt(-jnp.tanh(log_a) * (a2 + 1.0))
    return taps, xc, r, i, a, a2, mult


def _lru_fwd_call(p, cw, cb, wa, ba, wx, bx, sp):
    B, Tp, _ = p.shape
    W = LRU_WIDTH
    tc = SEQ_BLOCK
    nc = Tp // tc

    def body(pg_ref, prec_ref, prev_ref, cw_ref, cb_ref, wa_ref, ba_ref, wx_ref, bx_ref, sp_ref, y_ref, h_ref, carry_ref):
        first = pl.program_id(1) == 0

        @pl.when(first)
        def _():
            carry_ref[...] = jnp.zeros_like(carry_ref)

        _, xc, r, i, a, a2, mult = _lru_pre(prec_ref, prev_ref, first, cw_ref, cb_ref, wa_ref, ba_ref, wx_ref, bx_ref, sp_ref)
        b = mult * (i * xc)
        carry = carry_ref[0:1, :]
        for t in range(tc // 8):
            h = _scan8(a[8 * t:8 * t + 8], b[8 * t:8 * t + 8], carry, False)
            h_ref[8 * t:8 * t + 8, :] = h
            carry = h[7:8, :]
        carry_ref[...] = jnp.broadcast_to(carry, carry_ref.shape)
        y_ref[...] = h_ref[...] * _gelu_parts(pg_ref[...])[0]

    cur = pl.BlockSpec((None, tc, W), lambda b, j: (b, j, 0))
    rec = pl.BlockSpec((None, tc, W), lambda b, j: (b, j, 1))
    prev = pl.BlockSpec((None, 8, W), lambda b, j: (b, jnp.maximum(j * (tc // 8) - 1, 0), 1))
    vec = pl.BlockSpec((1, W), lambda b, j: (0, 0))
    cws = pl.BlockSpec((CONV_WIDTH, W), lambda b, j: (0, 0))
    wsp = pl.BlockSpec((LRU_HEADS, LRU_HEAD_DIM, LRU_HEAD_DIM), lambda b, j: (0, 0, 0))
    return pl.pallas_call(
        body, name="lru_fwd", grid=(B, nc),
        in_specs=[cur, rec, prev, cws, vec, wsp, vec, wsp, vec, vec],
        out_specs=[cur, cur],
        out_shape=[jax.ShapeDtypeStruct((B, Tp, W), F32), jax.ShapeDtypeStruct((B, Tp, W), F32)],
        scratch_shapes=[pltpu.VMEM((8, W), F32)],
        compiler_params=pltpu.CompilerParams(dimension_semantics=("arbitrary", "arbitrary")),
    )(p, p, p, cw, cb, wa, ba, wx, bx, sp)


def _lru_bwd_call(p, hseq, dy, cw, cb, wa, ba, wx, bx, sp, dpq, dpkv, dkpe):
    B, Tp, P = p.shape
    W = LRU_WIDTH
    tc = SEQ_BLOCK
    nc = Tp // tc
    HD = LRU_HEAD_DIM

    def body(pg_ref, prec_ref, prev_ref, h_ref, hprev_ref, dy_ref, cw_ref, cb_ref, wa_ref, ba_ref, wx_ref, bx_ref, sp_ref,
             dpq_ref, dpkv_ref, dkpe_ref, dp_ref, dcw_ref, dcb_ref, dwa_ref, dba_ref, dwx_ref, dbx_ref, dsp_ref,
             gcar_ref, anext_ref, halo_ref, g_ref):
        j = pl.program_id(1)
        first = j == nc - 1
        last = j == 0

        @pl.when(jnp.logical_and(pl.program_id(0) == 0, last))
        def _():
            for ref in (dcw_ref, dcb_ref, dwa_ref, dba_ref, dwx_ref, dbx_ref, dsp_ref):
                ref[...] = jnp.zeros_like(ref)

        @pl.when(last)
        def _():
            gcar_ref[...] = jnp.zeros_like(gcar_ref)
            anext_ref[...] = jnp.zeros_like(anext_ref)
            halo_ref[...] = jnp.zeros_like(halo_ref)

        taps, xc, r, i, a, a2, mult = _lru_pre(prec_ref, prev_ref, first, cw_ref, cb_ref, wa_ref, ba_ref, wx_ref, bx_ref, sp_ref)
        row = lax.broadcasted_iota(jnp.int32, (tc, W), 0)
        gelu, dgelu = _gelu_parts(pg_ref[...])
        dy = dy_ref[...]
        hcur = h_ref[...]
        dp_ref[:, 0:W] = dy * hcur * dgelu
        dp_ref[:, 2 * W:2 * W + MLA_Q_RANK] = dpq_ref[...]
        dp_ref[:, _KPE_START - MLA_KV_RANK:_KPE_START] = dpkv_ref[...]
        dp_ref[:, _KPE_START:P] = pltpu.roll(dkpe_ref[...], HEAD_LANES - MLA_NOPE, 1)[:, 0:P - _KPE_START]
        dh = dy * gelu
        a_next = jnp.where(row == tc - 1, anext_ref[0:1, :], pltpu.roll(a, tc - 1, 0))
        carry = gcar_ref[0:1, :]
        for t in reversed(range(tc // 8)):
            g = _scan8(a_next[8 * t:8 * t + 8], dh[8 * t:8 * t + 8], carry, True)
            g_ref[8 * t:8 * t + 8, :] = g
            carry = g[0:1, :]
        gcar_ref[...] = jnp.broadcast_to(carry, gcar_ref.shape)
        anext_ref[...] = jnp.broadcast_to(a[0:1, :], anext_ref.shape)
        G = g_ref[...]
        h_before = jnp.where(first, 0.0, hprev_ref[7:8, :])
        hprev = jnp.where(row == 0, h_before, pltpu.roll(hcur, 1, 0))
        d_a = G * hprev
        gx_ = G * xc
        d_mult = gx_ * i
        d_i = gx_ * mult
        dxc = G * (mult * i)
        d_la = d_a * a - d_mult * (a2 / mult)
        sp = sp_ref[...]
        d_r = d_la * (-LRU_C * sp)
        dsp_ref[...] += jnp.sum(d_la * (-LRU_C * r), axis=0, keepdims=True)
        dga = d_r * r * (1.0 - r)
        dgx = d_i * i * (1.0 - i)
        dba_ref[...] += jnp.sum(dga, axis=0, keepdims=True)
        dbx_ref[...] += jnp.sum(dgx, axis=0, keepdims=True)
        back = []
        for h in range(LRU_HEADS):
            sl = slice(h * HD, (h + 1) * HD)
            xh = xc[:, sl].astype(MXU_DTYPE)
            ah = dga[:, sl].astype(MXU_DTYPE)
            bh = dgx[:, sl].astype(MXU_DTYPE)
            tn = (((0,), (0,)), ((), ()))
            nt = (((1,), (1,)), ((), ()))
            dwa_ref[h] += lax.dot_general(xh, ah, tn, preferred_element_type=F32)
            dwx_ref[h] += lax.dot_general(xh, bh, tn, preferred_element_type=F32)
            back.append(lax.dot_general(ah, wa_ref[h].astype(MXU_DTYPE), nt, preferred_element_type=F32)
                        + lax.dot_general(bh, wx_ref[h].astype(MXU_DTYPE), nt, preferred_element_type=F32))
        dxc = dxc + jnp.concatenate(back, axis=1)
        dcb_ref[...] += jnp.sum(dxc, axis=0, keepdims=True)
        for k in range(CONV_WIDTH):
            dcw_ref[k:k + 1, :] += jnp.sum(dxc * taps[k], axis=0, keepdims=True)
        ext = jnp.concatenate([dxc, halo_ref[...]], axis=0)
        cw = cw_ref[...]
        acc = cw[CONV_WIDTH - 1:CONV_WIDTH, :] * dxc
        for k in range(CONV_WIDTH - 1):
            s = CONV_WIDTH - 1 - k
            acc = acc + cw[k:k + 1, :] * pltpu.roll(ext, tc + 8 - s, 0)[:tc]
        dp_ref[:, W:2 * W] = acc
        halo_ref[...] = dxc[0:8, :]

    rev = lambda j: nc - 1 - j
    cur = pl.BlockSpec((None, tc, W), lambda b, j: (b, rev(j), 0))
    rec = pl.BlockSpec((None, tc, W), lambda b, j: (b, rev(j), 1))
    prev = pl.BlockSpec((None, 8, W), lambda b, j: (b, jnp.maximum(rev(j) * (tc // 8) - 1, 0), 0))
    prev_rec = pl.BlockSpec((None, 8, W), lambda b, j: (b, jnp.maximum(rev(j) * (tc // 8) - 1, 0), 1))
    vec = pl.BlockSpec((1, W), lambda b, j: (0, 0))
    cws = pl.BlockSpec((CONV_WIDTH, W), lambda b, j: (0, 0))
    wsp = pl.BlockSpec((LRU_HEADS, HD, HD), lambda b, j: (0, 0, 0))
    vs = jax.ShapeDtypeStruct((1, W), F32)
    ws = jax.ShapeDtypeStruct((LRU_HEADS, HD, HD), F32)

    def rows(width):
        return pl.BlockSpec((None, tc, width), lambda b, j: (b, rev(j), 0))

    return pl.pallas_call(
        body, name="lru_bwd", grid=(B, nc),
        in_specs=[cur, rec, prev_rec, cur, prev, cur, cws, vec, wsp, vec, wsp, vec, vec, rows(MLA_Q_RANK), rows(MLA_KV_RANK), rows(HEAD_LANES)],
        out_specs=[rows(P), cws, vec, wsp, vec, wsp, vec, vec],
        out_shape=[jax.ShapeDtypeStruct((B, Tp, P), F32), jax.ShapeDtypeStruct((CONV_WIDTH, W), F32), vs, ws, vs, ws, vs, vs],
        scratch_shapes=[pltpu.VMEM((8, W), F32), pltpu.VMEM((8, W), F32), pltpu.VMEM((8, W), F32), pltpu.VMEM((tc, W), F32)],
        compiler_params=pltpu.CompilerParams(dimension_semantics=("arbitrary", "arbitrary")),
    )(p, p, p, hseq, hseq, dy, cw, cb, wa, ba, wx, bx, sp, dpq, dpkv, dkpe)


_Q_BLOCK = 2 * LRU_WIDTH // MLA_Q_RANK
_KV_BLOCK = (2 * LRU_WIDTH + MLA_Q_RANK) // MLA_KV_RANK
_KPE_START = 2 * LRU_WIDTH + MLA_Q_RANK + MLA_KV_RANK


@jax.custom_vjp
def even_front(p, cw, cb, wa, ba, wx, bx, sp, gq, gkv):
    return _even_front_fwd(p, cw, cb, wa, ba, wx, bx, sp, gq, gkv)[0]


def _even_front_fwd(p, cw, cb, wa, ba, wx, bx, sp, gq, gkv):
    B, Tp, W = p.shape
    p2d = p.reshape(B * Tp, W)
    y, hseq = _lru_fwd_call(p, cw, cb, wa, ba, wx, bx, sp)
    qn = _rms_fwd_call(p2d, gq, "q_norm_fwd", _Q_BLOCK)
    kvn = _rms_fwd_call(p2d, gkv, "kv_norm_fwd", _KV_BLOCK)
    kpe = jnp.pad(p[:, :, _KPE_START:], ((0, 0), (0, 0), (MLA_NOPE, HEAD_LANES - MLA_NOPE - MLA_ROPE)))
    return (y, qn, kvn, kpe), (p, hseq, cw, cb, wa, ba, wx, bx, sp, gq, gkv)


def _even_front_bwd(res, cts):
    p, hseq, cw, cb, wa, ba, wx, bx, sp, gq, gkv = res
    dy, dqn, dkvn, dkpe = cts
    B, Tp, W = p.shape
    p2d = p.reshape(B * Tp, W)
    dpq, dgq = _rms_bwd_call(p2d, gq, dqn, "q_norm_bwd", _Q_BLOCK)
    dpkv, dgkv = _rms_bwd_call(p2d, gkv, dkvn, "kv_norm_bwd", _KV_BLOCK)
    dp, dcw, dcb, dwa, dba, dwx, dbx, dsp = _lru_bwd_call(p, hseq, dy, cw, cb, wa, ba, wx, bx, sp, dpq.reshape(B, Tp, -1),
                                                          dpkv.reshape(B, Tp, -1), dkpe)
    return dp, dcw, dcb, dwa, dba, dwx, dbx, dsp, dgq.reshape(gq.shape), dgkv.reshape(gkv.shape)


even_front.defvjp(_even_front_fwd, _even_front_bwd)


def _rope_tables(pos, half):
    inv = ROPE_BASE ** (-jnp.arange(half, dtype=F32) / half)
    ang = pos.astype(F32)[:, None] * inv[None, :]
    return jnp.cos(ang), jnp.sin(ang)


_NT = (((1,), (1,)), ((), ()))
_TN = (((0,), (0,)), ((), ()))
HEAD_LANES = 128
_MLA_SCALE = (MLA_NOPE + MLA_ROPE) ** -0.5
_LOG2E = math.log2(math.e)


Q_BLOCK = 512


def _query_blocks(Tp):
    first = Tp % Q_BLOCK or Q_BLOCK
    return [(0, first)] + [(r, r + Q_BLOCK) for r in range(first, Tp, Q_BLOCK)]


def _mask_diagonal(s, fill):
    R, L = s.shape
    row = lax.broadcasted_iota(jnp.int32, (R, R), 0)
    col = lax.broadcasted_iota(jnp.int32, (R, R), 1)
    last = jnp.where(col <= row, s[:, L - R:], fill)
    return last if L == R else jnp.concatenate([s[:, :L - R], last], axis=1)


def _mla_rope_tables(pos):
    half = MLA_ROPE // 2
    cos, sin = _rope_tables(pos, half)
    T = pos.shape[0]
    ones, zeros = jnp.ones((T, MLA_NOPE), F32), jnp.zeros((T, MLA_NOPE), F32)
    tail1, tail0 = jnp.ones((T, HEAD_LANES - MLA_NOPE - MLA_ROPE), F32), jnp.zeros((T, HEAD_LANES - MLA_NOPE - MLA_ROPE), F32)
    zh = jnp.zeros((T, half), F32)
    c = jnp.concatenate([ones, cos, cos, tail1], axis=1)
    s_up = jnp.concatenate([zeros, -sin, zh, tail0], axis=1)
    s_down = jnp.concatenate([zeros, zh, sin, tail0], axis=1)
    return c, s_up, s_down


def _rope_lanes(x, c, s_up, s_down):
    half = MLA_ROPE // 2
    return x * c + pltpu.roll(x, HEAD_LANES - half, 1) * s_up + pltpu.roll(x, half, 1) * s_down


def _unrope_lanes(d, c, s_up, s_down):
    half = MLA_ROPE // 2
    return d * c + pltpu.roll(d * s_up, half, 1) + pltpu.roll(d * s_down, HEAD_LANES - half, 1)


def _mla_operands(q_ref, kv_ref, kpe_ref, c, s_up, s_down):
    lane = lax.broadcasted_iota(jnp.int32, kv_ref.shape, 1)
    qr = (_rope_lanes(q_ref[...].astype(F32), c, s_up, s_down) * (_MLA_SCALE * _LOG2E)).astype(MXU_DTYPE)
    kr = jnp.where(lane < MLA_NOPE, kv_ref[...].astype(F32), _rope_lanes(kpe_ref[...], c, s_up, s_down)).astype(MXU_DTYPE)
    return qr, kr, lane


def _mla_specs(Tp):
    head = pl.BlockSpec((None, Tp, HEAD_LANES), lambda b, h: (b, 0, h))
    shared = pl.BlockSpec((None, Tp, HEAD_LANES), lambda b, h: (b, 0, 0))
    tab = pl.BlockSpec((Tp, HEAD_LANES), lambda b, h: (0, 0))
    lse = pl.BlockSpec((None, None, Tp, 1), lambda b, h: (b, h, 0, 0))
    return head, shared, tab, lse


def _attn_fwd_call(q, kv, kpe, tabs):
    B, Tp, _ = q.shape

    def body(q_ref, kv_ref, kpe_ref, c_ref, su_ref, sd_ref, o_ref, lse_ref, qr_ref, kr_ref):
        qr, kr, lane = _mla_operands(q_ref, kv_ref, kpe_ref, c_ref[...], su_ref[...], sd_ref[...])
        qr_ref[...] = qr
        kr_ref[...] = kr
        for r0, L in _query_blocks(Tp):
            blk = slice(r0, L)
            s = _mask_diagonal(lax.dot_general(qr_ref[blk, :], kr_ref[0:L, :], _NT, preferred_element_type=F32), NEG_INF)
            m = jnp.max(s, axis=-1, keepdims=True)
            p = jnp.exp2(s - m)
            l = jnp.sum(p, axis=-1, keepdims=True)
            o = jnp.dot(p.astype(MXU_DTYPE), kv_ref[0:L, :].astype(MXU_DTYPE), preferred_element_type=F32)
            o_ref[blk, :] = jnp.where(lane[blk, :] >= MLA_NOPE, o / l, 0.0)
            lse_ref[blk, :] = m + jnp.log2(l)

    head, shared, tab, lse = _mla_specs(Tp)
    return pl.pallas_call(
        body, name="mla_attn_fwd", grid=(B, MLA_HEADS), in_specs=[head, head, shared, tab, tab, tab], out_specs=[head, lse],
        out_shape=[jax.ShapeDtypeStruct((B, Tp, MLA_HEADS * HEAD_LANES), F32), jax.ShapeDtypeStruct((B, MLA_HEADS, Tp, 1), F32)],
        scratch_shapes=[pltpu.VMEM((Tp, HEAD_LANES), MXU_DTYPE), pltpu.VMEM((Tp, HEAD_LANES), MXU_DTYPE)],
        compiler_params=pltpu.CompilerParams(dimension_semantics=("parallel", "parallel")),
    )(q, kv, kpe, *tabs)


def _attn_bwd_call(q, kv, kpe, tabs, o, lse, do):
    B, Tp, _ = q.shape

    def body(q_ref, kv_ref, kpe_ref, c_ref, su_ref, sd_ref, o_ref, lse_ref, do_ref, dq_ref, dkv_ref, dkpe_ref,
             qr_ref, kr_ref, dqa_ref, dka_ref, dva_ref):
        c, s_up, s_down = c_ref[...], su_ref[...], sd_ref[...]
        qr, kr, lane = _mla_operands(q_ref, kv_ref, kpe_ref, c, s_up, s_down)
        qr_ref[...] = qr
        kr_ref[...] = kr
        dka_ref[...] = jnp.zeros_like(dka_ref)
        dva_ref[...] = jnp.zeros_like(dva_ref)
        for r0, L in _query_blocks(Tp):
            blk = slice(r0, L)
            qb = qr_ref[blk, :]
            do = jnp.where(lane[blk, :] >= MLA_NOPE, do_ref[blk, :], 0.0)
            delta = jnp.sum(do * o_ref[blk, :], axis=-1, keepdims=True)
            s = _mask_diagonal(lax.dot_general(qb, kr_ref[0:L, :], _NT, preferred_element_type=F32), NEG_INF)
            p = jnp.exp2(s - lse_ref[blk, :])
            dob = do.astype(MXU_DTYPE)
            dva_ref[0:L, :] += lax.dot_general(p.astype(MXU_DTYPE), dob, _TN, preferred_element_type=F32)
            dp = lax.dot_general(dob, kv_ref[0:L, :].astype(MXU_DTYPE), _NT, preferred_element_type=F32)
            ds = (p * (dp - delta)).astype(MXU_DTYPE)
            dqa_ref[blk, :] = jnp.dot(ds, kr_ref[0:L, :], preferred_element_type=F32)
            dka_ref[0:L, :] += lax.dot_general(ds, qb, _TN, preferred_element_type=F32)
        dq_ref[...] = _unrope_lanes(dqa_ref[...] * _MLA_SCALE, c, s_up, s_down).astype(dq_ref.dtype)
        dk = dka_ref[...] * (1.0 / _LOG2E)
        dkv_ref[...] = jnp.where(lane < MLA_NOPE, dk, dva_ref[...]).astype(dkv_ref.dtype)
        dkpe = jnp.where(lane >= MLA_NOPE, _unrope_lanes(dk, c, s_up, s_down), 0.0)

        @pl.when(pl.program_id(1) == 0)
        def _():
            dkpe_ref[...] = dkpe

        @pl.when(pl.program_id(1) > 0)
        def _():
            dkpe_ref[...] += dkpe

    head, shared, tab, lse_spec = _mla_specs(Tp)
    wide = jax.ShapeDtypeStruct((B, Tp, MLA_HEADS * HEAD_LANES), q.dtype)
    acc = pltpu.VMEM((Tp, HEAD_LANES), F32)
    return pl.pallas_call(
        body, name="mla_attn_bwd", grid=(B, MLA_HEADS),
        in_specs=[head, head, shared, tab, tab, tab, head, lse_spec, head], out_specs=[head, head, shared],
        out_shape=[wide, wide, jax.ShapeDtypeStruct((B, Tp, HEAD_LANES), F32)],
        scratch_shapes=[pltpu.VMEM((Tp, HEAD_LANES), MXU_DTYPE), pltpu.VMEM((Tp, HEAD_LANES), MXU_DTYPE), acc, acc, acc],
        compiler_params=pltpu.CompilerParams(dimension_semantics=("parallel", "arbitrary")),
    )(q, kv, kpe, *tabs, o, lse, do)


@jax.custom_vjp
def mla_attention(q, kv, kpe, tabs):
    return _attn_fwd_call(q, kv, kpe, tabs)[0]


def _mla_attention_fwd(q, kv, kpe, tabs):
    o, lse = _attn_fwd_call(q, kv, kpe, tabs)
    return o, (q, kv, kpe, tabs, o, lse)


def _mla_attention_bwd(res, do):
    q, kv, kpe, tabs, o, lse = res
    dq, dkv, dkpe = _attn_bwd_call(q, kv, kpe, tabs, o, lse, do)
    return dq, dkv, dkpe, None


mla_attention.defvjp(_mla_attention_fwd, _mla_attention_bwd)


def _rope_halves(x, cos, sin):
    half = x.shape[1] // 2
    x1, x2 = x[:, :half], x[:, half:]
    return jnp.concatenate([x1 * cos - x2 * sin, x1 * sin + x2 * cos], axis=1)


def _unrope_halves(d, cos, sin):
    half = d.shape[1] // 2
    d1, d2 = d[:, :half], d[:, half:]
    return jnp.concatenate([d1 * cos + d2 * sin, d2 * cos - d1 * sin], axis=1)


_RET_K_SCALE = RET_QK_DIM ** -0.5
_RET_Q_BLOCKS = RET_HEADS
_RET_V_BLOCK0 = 2 * RET_HEADS * RET_QK_DIM // RET_V_DIM
_RET_G_BLOCK0 = _RET_V_BLOCK0 + RET_HEADS


def _ret_specs(Tp):
    q = pl.BlockSpec((None, Tp, RET_QK_DIM), lambda b, h: (b, 0, h))
    k = pl.BlockSpec((None, Tp, RET_QK_DIM), lambda b, h: (b, 0, _RET_Q_BLOCKS + h))
    v = pl.BlockSpec((None, Tp, RET_V_DIM), lambda b, h: (b, 0, _RET_V_BLOCK0 + h))
    tab = pl.BlockSpec((Tp, RET_QK_DIM // 2), lambda b, h: (0, 0))
    lg = pl.BlockSpec((None, 1, 1), lambda b, h: (h, 0, 0))
    return q, k, v, tab, lg


def _ret_operands(q_ref, k_ref, cos, sin, lg):
    t = lax.broadcasted_iota(jnp.int32, (q_ref.shape[0], 1), 0).astype(F32)
    grow, shrink = jnp.exp(-lg * t), jnp.exp(lg * t)
    qs = (_rope_halves(q_ref[...].astype(F32), cos, sin) * shrink).astype(MXU_DTYPE)
    ks = (_rope_halves(k_ref[...].astype(F32), cos, sin) * (grow * _RET_K_SCALE)).astype(MXU_DTYPE)
    return qs, ks, shrink, grow * _RET_K_SCALE


def _ret_core_fwd_call(p, cos, sin, lg):
    B, Tp, _ = p.shape

    def body(q_ref, k_ref, v_ref, cos_ref, sin_ref, lg_ref, o_ref, qs_ref, ks_ref):
        qs_ref[...], ks_ref[...], _, _ = _ret_operands(q_ref, k_ref, cos_ref[...], sin_ref[...], lg_ref[...])
        for r0, L in _query_blocks(Tp):
            blk = slice(r0, L)
            s = _mask_diagonal(lax.dot_general(qs_ref[blk, :], ks_ref[0:L, :], _NT, preferred_element_type=F32), 0.0)
            o_ref[blk, :] = jnp.dot(s.astype(MXU_DTYPE), v_ref[0:L, :].astype(MXU_DTYPE), preferred_element_type=F32)

    q, k, v, tab, lgs = _ret_specs(Tp)
    return pl.pallas_call(
        body, name="retention_fwd", grid=(B, RET_HEADS), in_specs=[q, k, v, tab, tab, lgs],
        out_specs=pl.BlockSpec((None, Tp, RET_V_DIM), lambda b, h: (b, 0, h)),
        out_shape=jax.ShapeDtypeStruct((B, Tp, RET_HEADS * RET_V_DIM), F32),
        scratch_shapes=[pltpu.VMEM((Tp, RET_QK_DIM), MXU_DTYPE), pltpu.VMEM((Tp, RET_QK_DIM), MXU_DTYPE)],
        compiler_params=pltpu.CompilerParams(dimension_semantics=("parallel", "parallel")),
    )(p, p, p, cos, sin, lg)


def _ret_core_bwd_call(p, do, cos, sin, lg):
    B, Tp, _ = p.shape

    def body(q_ref, k_ref, v_ref, do_ref, cos_ref, sin_ref, lg_ref, dq_ref, dk_ref, dv_ref, qs_ref, ks_ref, dqa_ref, dka_ref, dva_ref):
        cos_, sin_ = cos_ref[...], sin_ref[...]
        qs_ref[...], ks_ref[...], q_scale, k_scale = _ret_operands(q_ref, k_ref, cos_, sin_, lg_ref[...])
        dka_ref[...] = jnp.zeros_like(dka_ref)
        dva_ref[...] = jnp.zeros_like(dva_ref)
        for r0, L in _query_blocks(Tp):
            blk = slice(r0, L)
            qb = qs_ref[blk, :]
            dob = do_ref[blk, :].astype(MXU_DTYPE)
            s = _mask_diagonal(lax.dot_general(qb, ks_ref[0:L, :], _NT, preferred_element_type=F32), 0.0).astype(MXU_DTYPE)
            dva_ref[0:L, :] += lax.dot_general(s, dob, _TN, preferred_element_type=F32)
            ds = _mask_diagonal(lax.dot_general(dob, v_ref[0:L, :].astype(MXU_DTYPE), _NT, preferred_element_type=F32), 0.0).astype(MXU_DTYPE)
            dqa_ref[blk, :] = jnp.dot(ds, ks_ref[0:L, :], preferred_element_type=F32)
            dka_ref[0:L, :] += lax.dot_general(ds, qb, _TN, preferred_element_type=F32)
        dq_ref[...] = _unrope_halves(dqa_ref[...] * q_scale, cos_, sin_).astype(dq_ref.dtype)
        dk_ref[...] = _unrope_halves(dka_ref[...] * k_scale, cos_, sin_).astype(dk_ref.dtype)
        dv_ref[...] = dva_ref[...].astype(dv_ref.dtype)

    q, k, v, tab, lgs = _ret_specs(Tp)
    qk_out = pl.BlockSpec((None, Tp, RET_QK_DIM), lambda b, h: (b, 0, h))
    v_out = pl.BlockSpec((None, Tp, RET_V_DIM), lambda b, h: (b, 0, h))
    return pl.pallas_call(
        body, name="retention_bwd", grid=(B, RET_HEADS), in_specs=[q, k, v, v_out, tab, tab, lgs],
        out_specs=[qk_out, qk_out, v_out],
        out_shape=[jax.ShapeDtypeStruct((B, Tp, RET_HEADS * RET_QK_DIM), p.dtype), jax.ShapeDtypeStruct((B, Tp, RET_HEADS * RET_QK_DIM), p.dtype),
                   jax.ShapeDtypeStruct((B, Tp, RET_HEADS * RET_V_DIM), p.dtype)],
        scratch_shapes=[pltpu.VMEM((Tp, RET_QK_DIM), MXU_DTYPE), pltpu.VMEM((Tp, RET_QK_DIM), MXU_DTYPE),
                        pltpu.VMEM((Tp, RET_QK_DIM), F32), pltpu.VMEM((Tp, RET_QK_DIM), F32), pltpu.VMEM((Tp, RET_V_DIM), F32)],
        compiler_params=pltpu.CompilerParams(dimension_semantics=("parallel", "parallel")),
    )(p, p, p, do, cos, sin, lg)


def _ret_gate_specs(M):
    tm = _pick(M, 1088, 8)
    head = pl.BlockSpec((tm, RET_V_DIM), lambda i, h: (i, h))
    gate = pl.BlockSpec((tm, RET_V_DIM), lambda i, h: (i, _RET_G_BLOCK0 + h))
    return tm, head, gate


def _ret_gate_fwd_call(o, p2d):
    M = o.shape[0]
    tm, head, gate = _ret_gate_specs(M)

    def body(o_ref, g_ref, y_ref):
        ov = o_ref[...]
        gv = g_ref[...].astype(F32)
        rstd = lax.rsqrt(jnp.mean(ov * ov, axis=-1, keepdims=True) + EPS)
        y_ref[...] = (gv * _sigmoid(gv)) * (ov * rstd)

    return pl.pallas_call(
        body, name="retention_gate_fwd", grid=(M // tm, RET_HEADS), in_specs=[head, gate], out_specs=head,
        out_shape=jax.ShapeDtypeStruct(o.shape, F32),
        compiler_params=pltpu.CompilerParams(dimension_semantics=("parallel", "parallel")),
    )(o, p2d)


def _ret_gate_bwd_call(o, p2d, dy):
    M = o.shape[0]
    tm, head, gate = _ret_gate_specs(M)

    def body(o_ref, g_ref, dy_ref, do_ref, dg_ref):
        ov = o_ref[...]
        gv = g_ref[...].astype(F32)
        dy = dy_ref[...]
        rstd = lax.rsqrt(jnp.mean(ov * ov, axis=-1, keepdims=True) + EPS)
        on = ov * rstd
        sg = _sigmoid(gv)
        dg_ref[...] = (dy * on * (sg * (1.0 + gv * (1.0 - sg)))).astype(dg_ref.dtype)
        don = dy * (gv * sg)
        do_ref[...] = (rstd * (don - on * jnp.mean(don * on, axis=-1, keepdims=True))).astype(do_ref.dtype)

    shp = jax.ShapeDtypeStruct(o.shape, p2d.dtype)
    return pl.pallas_call(
        body, name="retention_gate_bwd", grid=(M // tm, RET_HEADS), in_specs=[head, gate, head], out_specs=[head, head],
        out_shape=[shp, shp],
        compiler_params=pltpu.CompilerParams(dimension_semantics=("parallel", "parallel")),
    )(o, p2d, dy)


def _log_gamma():
    return jnp.log(1.0 - 2.0 ** (-5.0 - jnp.arange(RET_HEADS, dtype=F32))).reshape(RET_HEADS, 1, 1)


def _pieces_dx(pieces, w, plus, name):
    M = pieces[0].shape[0]
    S, K, n = w.shape
    tm, tn = _pick(M, 544, 8), _pick(K, 512, 128)
    step = math.gcd(n, *[p.shape[1] for p in pieces])
    spans, col = [], 0
    for i, p in enumerate(pieces):
        for c in range(0, p.shape[1], step):
            g0 = col + c
            spans.append((i, c, g0 // n, g0 % n))
        col += p.shape[1]

    def body(*refs):
        p_refs, w_ref, plus_ref, o_ref = refs[:len(pieces)], refs[len(pieces)], refs[len(pieces) + 1], refs[len(pieces) + 2]
        nt = (((1,), (1,)), ((), ()))
        r = DN_ALPHA * plus_ref[...]
        for i, c, s, wc in spans:
            r = r + lax.dot_general(p_refs[i][:, c:c + step].astype(MXU_DTYPE), w_ref[s, :, wc:wc + step].astype(MXU_DTYPE), nt,
                                    preferred_element_type=F32)
        o_ref[...] = r

    tile = pl.BlockSpec((tm, tn), lambda i, j: (i, j))
    return pl.pallas_call(
        body, name=name, grid=(M // tm, K // tn),
        in_specs=[pl.BlockSpec((tm, p.shape[1]), lambda i, j: (i, 0)) for p in pieces] + [pl.BlockSpec((S, tn, n), lambda i, j: (0, j, 0)), tile],
        out_specs=tile, out_shape=jax.ShapeDtypeStruct((M, K), F32),
        compiler_params=pltpu.CompilerParams(dimension_semantics=("parallel", "arbitrary")),
    )(*pieces, w, plus)


def _pieces_dw(a, pieces, col_shards, out_dtype, name):
    M, K = a.shape
    N = sum(p.shape[1] for p in pieces)
    n = N // col_shards
    tn = math.gcd(n, 512, *[p.shape[1] for p in pieces])
    tm, tk = _pick(K, 1024, 128), _pick(M, 2176, 8)
    nk, per = M // tk, n // tn
    first = [sum(p.shape[1] for p in pieces[:i]) // tn for i in range(len(pieces) + 1)]

    def body(*refs):
        a_ref, g_refs, o_ref, acc_ref = refs[0], refs[1:1 + len(pieces)], refs[1 + len(pieces)], refs[2 + len(pieces)]
        j, k = pl.program_id(1), pl.program_id(2)
        av = a_ref[...].astype(MXU_DTYPE)
        for i, g_ref in enumerate(g_refs):
            @pl.when(jnp.logical_and(j >= first[i], j < first[i + 1]))
            def _():
                r = lax.dot_general(av, g_ref[...].astype(MXU_DTYPE), (((0,), (0,)), ((), ())), preferred_element_type=F32)

                @pl.when(k == 0)
                def _():
                    acc_ref[...] = r

                @pl.when(k > 0)
                def _():
                    acc_ref[...] += r

        @pl.when(k == nk - 1)
        def _():
            o_ref[...] = acc_ref[...].astype(out_dtype)

    def piece_spec(i):
        tiles = pieces[i].shape[1] // tn
        return pl.BlockSpec((tk, tn), lambda ii, j, k: (k, jnp.clip(j - first[i], 0, tiles - 1)))

    return pl.pallas_call(
        body, name=name, grid=(K // tm, N // tn, nk),
        in_specs=[pl.BlockSpec((tk, tm), lambda i, j, k: (k, i))] + [piece_spec(i) for i in range(len(pieces))],
        out_specs=pl.BlockSpec((None, tm, tn), lambda i, j, k: (j // per, i, j % per)),
        out_shape=jax.ShapeDtypeStruct((col_shards, K, n), out_dtype),
        scratch_shapes=[pltpu.VMEM((tm, tn), F32)],
        compiler_params=pltpu.CompilerParams(dimension_semantics=("parallel", "parallel", "arbitrary")),
    )(a, *pieces)


@functools.partial(jax.custom_vjp, nondiff_argnums=(9,))
def retention_block(h, w_in, w_out, w_in_grad_slot, w_out_grad_slot, g, b, cos, sin, dims):
    return _retention_block_fwd(h, w_in, w_out, w_in_grad_slot, w_out_grad_slot, g, b, cos, sin, dims)[0]


def _retention_block_fwd(h, w_in, w_out, w_in_grad_slot, w_out_grad_slot, g, b, cos, sin, dims):
    B, Tp = dims
    p = _mm_nn(h, w_in, False, "od_w_in_fwd", out_dtype=MXU_DTYPE)
    o = _ret_core_fwd_call(p.reshape(B, Tp, -1), cos, sin, _log_gamma())
    y = _ret_gate_fwd_call(o.reshape(B * Tp, -1), p)
    out, z = _mm_nn(y, w_out, False, "od_w_out_norm_fwd", norm=(h, g, b))
    return out, (h, p, o, y, z, w_in, w_out, g, cos, sin, jnp.zeros((), w_in_grad_slot.dtype))


def _retention_block_bwd(dims, res, dout):
    B, Tp = dims
    h, p, o, y, z, w_in, w_out, g, cos, sin, slot_like = res
    dz, dg, db = _ln_bwd_call(z, g, dout, "od_w_out_norm_bwd")
    dy = _mm_nt(dz, w_out, None, "od_w_out_dx")
    dw_out = _mm_tn(y, dz, False, "od_w_out_dw", 1, slot_like.dtype)
    do, dgate = _ret_gate_bwd_call(o.reshape(B * Tp, -1), p, dy)
    dq, dk, dv = _ret_core_bwd_call(p.reshape(B, Tp, -1), do.reshape(B, Tp, -1), cos, sin, _log_gamma())
    pieces = [dq.reshape(B * Tp, -1), dk.reshape(B * Tp, -1), dv.reshape(B * Tp, -1), dgate]
    dh = _pieces_dx(pieces, w_in, dz, "od_w_in_dx")
    dw_in = _pieces_dw(h, pieces, N_CHIPS, slot_like.dtype, "od_w_in_dw")
    return dh, None, None, dw_in, dw_out, dg.reshape(g.shape), db.reshape(g.shape), None, None


retention_block.defvjp(_retention_block_fwd, _retention_block_bwd)


def _heads_to_lanes(w):
    K = w.shape[0]
    w = w.reshape(K, MLA_HEADS, MLA_NOPE + MLA_ROPE)
    return jnp.pad(w, ((0, 0), (0, 0), (0, HEAD_LANES - MLA_NOPE - MLA_ROPE))).reshape(K, MLA_HEADS * HEAD_LANES)


def _out_rows_to_lanes(w):
    N = w.shape[1]
    att = w[LRU_WIDTH:].reshape(MLA_HEADS, MLA_V, N)
    att = jnp.pad(att, ((0, 0), (HEAD_LANES - MLA_V, 0), (0, 0))).reshape(MLA_HEADS * HEAD_LANES, N)
    return jnp.concatenate([w[:LRU_WIDTH], att], axis=0)


def _seq_dims(x):
    B, S, D = x.shape
    T = S + N_META
    Tp = _round_up(T, SEQ_BLOCK)
    return B, S, T, Tp


def _mixer0(diff, w, token):
    x = diff["x"]
    B, S, T, Tp = _seq_dims(x)
    D = x.shape[-1]
    M = B * Tp
    pos = jnp.arange(Tp, dtype=jnp.int32)

    def mm(a, name, act=False, out_dtype=F32, layout=lambda m: m, col_shards=1):
        return matmul(a, layout(w[name]), layout(diff[name]), act, name, out_dtype, col_shards)

    meta = jnp.broadcast_to(diff["meta_tokens"][None], (B, N_META, D))
    h = jnp.concatenate([meta, x + token, jnp.zeros((B, Tp - T, D), F32)], axis=1).reshape(M, D)
    p = mm(h, "ev_w_in")
    sp = jax.nn.softplus(-diff["ev_lru_lambda"]).reshape(1, LRU_WIDTH)
    y_rec, qn, kvn, kpe = even_front(
        p.reshape(B, Tp, -1), diff["ev_conv_w"].reshape(CONV_WIDTH, LRU_WIDTH), diff["ev_conv_b"].reshape(1, LRU_WIDTH),
        diff["ev_w_rg_a"].reshape(LRU_HEADS, LRU_HEAD_DIM, LRU_HEAD_DIM), diff["ev_b_rg_a"].reshape(1, LRU_WIDTH),
        diff["ev_w_rg_x"].reshape(LRU_HEADS, LRU_HEAD_DIM, LRU_HEAD_DIM), diff["ev_b_rg_x"].reshape(1, LRU_WIDTH),
        sp, diff["ev_q_norm_g"].reshape(-1), diff["ev_kv_norm_g"].reshape(-1))
    y_rec = y_rec.reshape(M, LRU_WIDTH)
    q = mm(qn, "ev_w_uq", out_dtype=MXU_DTYPE, layout=_heads_to_lanes).reshape(B, Tp, -1)
    kv = mm(kvn, "ev_w_ukv", out_dtype=MXU_DTYPE).reshape(B, Tp, -1)
    y_att = mla_attention(q, kv, kpe, _mla_rope_tables(pos)).reshape(M, -1)
    return out_block(h, jnp.concatenate([y_rec, y_att], axis=-1), _out_rows_to_lanes(w["ev_w_out"]), _out_rows_to_lanes(diff["ev_w_out"]),
                     diff["ln_mix_g"], diff["ln_mix_b"], "ev_w_out")


def _mlp0(diff, h, w):
    return mlp_block(h, w["mlp_w1_0"], w["mlp_w2_0"], diff["mlp_w1_0"], diff["mlp_w2_0"], diff["ln_mlp_g"], diff["ln_mlp_b"], "mlp0")


def _layer1_loss(diff, h, w, tgt):
    B, S, T, Tp = _seq_dims(tgt)
    D = tgt.shape[-1]
    pos = jnp.arange(Tp, dtype=jnp.int32)

    cos, sin = _rope_tables(pos, RET_QK_DIM // 2)
    h = retention_block(h, w["od_w_in"], w["od_w_out"], diff["od_w_in"], diff["od_w_out"], diff["ln_mix_g"], diff["ln_mix_b"], cos, sin, (B, Tp))
    h = mlp_block(h, w["mlp_w1_1"], w["mlp_w2_1"], diff["mlp_w1_1"], diff["mlp_w2_1"], diff["ln_mlp_g"], diff["ln_mlp_b"], "mlp1")
    return loss_head(h.reshape(B, Tp, D), jnp.pad(tgt, ((0, 0), (N_META, Tp - T), (0, 0))), S)


_HBM = pl.BlockSpec(memory_space=pltpu.HBM)


def _place():
    return lax.axis_index("x"), lax.axis_index("y"), lax.axis_index("c")


def _other_chips(x, y):
    return [(1 - x, y), (x, 1 - y), (1 - x, 1 - y)]


def _chunks(rows, sublanes, most):
    for q in range(most, 0, -1):
        if rows % (q * sublanes) == 0:
            return q
    return 1


def _sublanes(dtype):
    return 8 * 4 // jnp.dtype(dtype).itemsize


def _gather_pieces(bufs):
    plan, first = [], []
    for b in bufs:
        Rh = b.shape[0] // 2
        Q = _chunks(Rh, _sublanes(b.dtype), 4) if Rh * b.shape[1] * b.dtype.itemsize > (1 << 20) else 1
        first.append(3 * sum(q for _, q, _ in plan))
        plan.append((Rh, Q, Rh // Q))
    return plan, first, 3 * sum(q for _, q, _ in plan)


def _allgather_chips(bufs, name):
    n = len(bufs)
    plan, first, n_sems = _gather_pieces(bufs)

    def body(*refs):
        x_refs, out_refs, (send_sems, recv_sems) = refs[:n], refs[n:2 * n], refs[2 * n:]
        x, y, c = _place()
        sibling = (x, y, 1 - c)
        chips = _other_chips(x, y)

        def copy(k, src, dst, to):
            return pltpu.make_async_remote_copy(src_ref=src, dst_ref=dst, send_sem=send_sems.at[k], recv_sem=recv_sems.at[k],
                                                device_id=to, device_id_type=MESH)

        def piece(i, cx, cy, hc, q):
            Rh, _, ch = plan[i]
            return out_refs[i].at[2 * cx + cy, pl.ds(hc * Rh + q * ch, ch), :]

        slots = [(i, q, j) for i in range(n) for q in range(plan[i][1]) for j in range(3)]
        sem = {(i, q, j): first[i] + 3 * q + j for i, q, j in slots}
        sent = [copy(sem[i, q, j], x_refs[i].at[pl.ds(c * plan[i][0] + q * plan[i][2], plan[i][2]), :], piece(i, x, y, c, q), (*chips[j], c))
                for i, q, j in slots]
        for cp in sent:
            cp.start()
        passed = []
        for i, q, j in slots:
            landed = piece(i, *chips[j], c, q)
            copy(sem[i, q, j], landed, landed, sibling).wait_recv()
            fwd = copy(n_sems + sem[i, q, j], landed, landed, sibling)
            fwd.start()
            passed.append(fwd)
        for i, q, j in slots:
            theirs = piece(i, *chips[j], 1 - c, q)
            copy(n_sems + sem[i, q, j], theirs, theirs, sibling).wait_recv()
        for cp in sent + passed:
            cp.wait_send()

    return pl.pallas_call(
        body, name=name, in_specs=[_HBM] * n, out_specs=[_HBM] * n,
        out_shape=[jax.ShapeDtypeStruct((N_CHIPS,) + b.shape, b.dtype) for b in bufs],
        scratch_shapes=[pltpu.SemaphoreType.DMA((2 * n_sems,)), pltpu.SemaphoreType.DMA((2 * n_sems,))],
    )(*bufs)


def _with_own(gathered, own):
    my = 2 * lax.axis_index("x") + lax.axis_index("y")
    return lax.dynamic_update_slice(gathered, own[None], (my, 0, 0))


def _sibling_exchange(ps, name):
    n = len(ps)

    def body(*refs):
        p_refs, out_refs, (send_sems, recv_sems) = refs[:n], refs[n:2 * n], refs[2 * n:]
        x, y, c = _place()
        copies = [pltpu.make_async_remote_copy(src_ref=p_ref.at[j, 1 - c], dst_ref=out_ref.at[j], send_sem=send_sems.at[N_CHIPS * i + j],
                                               recv_sem=recv_sems.at[N_CHIPS * i + j], device_id=(x, y, 1 - c), device_id_type=MESH)
                  for i, (p_ref, out_ref) in enumerate(zip(p_refs, out_refs)) for j in range(N_CHIPS)]
        for cp in copies:
            cp.start()
        for cp in copies:
            cp.wait()

    return pl.pallas_call(
        body, name=name, in_specs=[_HBM] * n, out_specs=[_HBM] * n,
        out_shape=[jax.ShapeDtypeStruct((N_CHIPS,) + p.shape[2:], p.dtype) for p in ps],
        scratch_shapes=[pltpu.SemaphoreType.DMA((N_CHIPS * n,)), pltpu.SemaphoreType.DMA((N_CHIPS * n,))],
    )(*ps)


def _chip_scatter(ss, name):
    n = len(ss)

    def body(*refs):
        s_refs, t_refs, (send_sems, recv_sems) = refs[:n], refs[n:2 * n], refs[2 * n:]
        x, y, c = _place()
        copies = [pltpu.make_async_remote_copy(src_ref=s_ref.at[j + 1], dst_ref=t_ref.at[j], send_sem=send_sems.at[3 * i + j],
                                               recv_sem=recv_sems.at[3 * i + j], device_id=(cx, cy, c), device_id_type=MESH)
                  for i, (s_ref, t_ref) in enumerate(zip(s_refs, t_refs)) for j, (cx, cy) in enumerate(_other_chips(x, y))]
        for cp in copies:
            cp.start()
        for cp in copies:
            cp.wait()

    return pl.pallas_call(
        body, name=name, in_specs=[_HBM] * n, out_specs=[_HBM] * n,
        out_shape=[jax.ShapeDtypeStruct((3,) + s.shape[1:], s.dtype) for s in ss],
        scratch_shapes=[pltpu.SemaphoreType.DMA((3 * n,)), pltpu.SemaphoreType.DMA((3 * n,))],
    )(*ss)


def _sibling_gather(fs, name):
    n = len(fs)

    def body(*refs):
        out_refs, (send_sems, recv_sems) = refs[n:2 * n], refs[2 * n:]
        x, y, c = _place()
        copies = [pltpu.make_async_remote_copy(src_ref=out_ref.at[c], dst_ref=out_ref.at[c], send_sem=send_sems.at[i], recv_sem=recv_sems.at[i],
                                               device_id=(x, y, 1 - c), device_id_type=MESH) for i, out_ref in enumerate(out_refs)]
        for cp in copies:
            cp.start()
        for cp in copies:
            cp.wait()

    return pl.pallas_call(
        body, name=name, in_specs=[_HBM] * n, out_specs=[_HBM] * n,
        out_shape=[jax.ShapeDtypeStruct(f.shape, f.dtype) for f in fs], input_output_aliases={i: i for i in range(n)},
        scratch_shapes=[pltpu.SemaphoreType.DMA((n,)), pltpu.SemaphoreType.DMA((n,))],
    )(*fs)


def _axis_scalar(name):
    return lax.axis_index(name).astype(jnp.int32).reshape(1)


def _add_own_half(p, got, out_dtype, name):
    n, _, R, C = p.shape
    tr = _pick(R, 512, 16)

    def body(x_ref, y_ref, c_ref, p_ref, g_ref, o_ref):
        o_ref[...] = (p_ref[...] + g_ref[...]).astype(out_dtype)

    def chip(r, x_ref, y_ref):
        return 2 * (x_ref[0] ^ (r & 1)) + (y_ref[0] ^ (r >> 1))

    grid_spec = pltpu.PrefetchScalarGridSpec(
        num_scalar_prefetch=3, grid=(n, R // tr),
        in_specs=[pl.BlockSpec((None, None, tr, C), lambda r, i, x_ref, y_ref, c_ref: (chip(r, x_ref, y_ref), c_ref[0], i, 0)),
                  pl.BlockSpec((None, tr, C), lambda r, i, x_ref, y_ref, c_ref: (chip(r, x_ref, y_ref), i, 0))],
        out_specs=pl.BlockSpec((None, tr, C), lambda r, i, x_ref, y_ref, c_ref: (r, i, 0)))
    return pl.pallas_call(body, name=name, grid_spec=grid_spec, out_shape=jax.ShapeDtypeStruct((n, R, C), out_dtype),
                          compiler_params=pltpu.CompilerParams(dimension_semantics=("parallel", "parallel")))(
        _axis_scalar("x"), _axis_scalar("y"), _axis_scalar("c"), p, got)


def _sum_partials(s, t, name):
    _, R, C = s.shape
    tr = _pick(R, 512, 16)

    def body(c_ref, s_ref, t_ref, o_ref):
        acc = s_ref[...].astype(F32)
        for j in range(3):
            acc = acc + t_ref[j].astype(F32)
        o_ref[...] = acc

    grid_spec = pltpu.PrefetchScalarGridSpec(
        num_scalar_prefetch=1, grid=(R // tr,),
        in_specs=[pl.BlockSpec((None, tr, C), lambda i, c_ref: (0, i, 0)), pl.BlockSpec((3, tr, C), lambda i, c_ref: (0, i, 0))],
        out_specs=pl.BlockSpec((None, tr, C), lambda i, c_ref: (c_ref[0], i, 0)))
    return pl.pallas_call(body, name=name, grid_spec=grid_spec, out_shape=jax.ShapeDtypeStruct((2, R, C), F32),
                          compiler_params=pltpu.CompilerParams(dimension_semantics=("parallel",)))(_axis_scalar("c"), s, t)


def _sibling_reduce(ps, wire_dtypes, tag):
    got = _sibling_exchange(ps, "grad_sibling_exchange_" + tag)
    return [_add_own_half(p, g, dt, "grad_sibling_add_%s%d" % (tag, i)) for i, (p, g, dt) in enumerate(zip(ps, got, wire_dtypes))]


_SEM = pl.BlockSpec(memory_space=pltpu.SEMAPHORE)
_ANY = pl.BlockSpec(memory_space=pl.ANY)
_EFFECT = pltpu.SideEffectType.DATAFLOW_SIDE_EFFECTING


def _in_hbm(a):
    return pltpu.with_memory_space_constraint(a, pltpu.HBM)


def _half_copies(x_refs, land_refs, send_sems, recv_sems, arriving):
    x, y, c = _place()
    copies = []
    for i, (x_ref, land_ref) in enumerate(zip(x_refs, land_refs)):
        Rh = x_ref.shape[0] // 2
        rows = pl.ds(c * Rh, Rh)
        for j, (cx, cy) in enumerate(_other_chips(x, y)):
            copies.append(pltpu.make_async_remote_copy(
                src_ref=x_ref.at[rows, :], dst_ref=land_ref.at[2 * cx + cy if arriving else 2 * x + y, rows, :],
                send_sem=send_sems.at[3 * i + j], recv_sem=recv_sems.at[3 * i + j], device_id=(cx, cy, c), device_id_type=MESH))
    return copies


def _allgather_start(bufs, name):
    n = len(bufs)

    def body(*refs):
        x_refs, land_refs, (send_sems, recv_sems), token = refs[:n], refs[n:2 * n], refs[2 * n:2 * n + 2], refs[-1]
        for cp in _half_copies(x_refs, land_refs, send_sems, recv_sems, False):
            cp.start()
        token[...] = jnp.zeros_like(token)

    lands = [lax.empty((N_CHIPS,) + b.shape, b.dtype) for b in bufs]
    out = pl.pallas_call(
        body, name=name,
        out_shape=(pltpu.SemaphoreType.DMA((3 * n,)), pltpu.SemaphoreType.DMA((3 * n,)), *[pltpu.HBM(a.shape, a.dtype) for a in bufs + lands],
                   jax.ShapeDtypeStruct((8, 128), F32)),
        in_specs=[_HBM] * (2 * n), out_specs=(_SEM, _SEM, *[_HBM] * (2 * n), pl.BlockSpec(memory_space=pltpu.VMEM)),
        input_output_aliases={i: 2 + i for i in range(2 * n)}, compiler_params=pltpu.CompilerParams(has_side_effects=_EFFECT),
    )(*[_in_hbm(a) for a in bufs + lands])
    return (out[0], out[1], list(out[2:2 + n]), list(out[2 + n:2 + 2 * n])), out[-1][0, 0]


def _allgather_wait(pending, after, name):
    send_sems, recv_sems, bufs, lands = pending
    n = len(bufs)

    def body(*refs):
        x_refs, land_refs, send_sems, recv_sems = refs[:n], refs[n:2 * n], refs[2 * n], refs[2 * n + 1]
        for cp in _half_copies(x_refs, land_refs, send_sems, recv_sems, False):
            cp.wait_send()
        for cp in _half_copies(x_refs, land_refs, send_sems, recv_sems, True):
            cp.wait_recv()

    out = pl.pallas_call(
        body, name=name, out_shape=tuple(pltpu.HBM(a.shape, a.dtype) for a in bufs + lands),
        in_specs=[_HBM] * (2 * n) + [_SEM, _SEM, _ANY], out_specs=tuple([_HBM] * (2 * n)), input_output_aliases={i: i for i in range(2 * n)},
        compiler_params=pltpu.CompilerParams(has_side_effects=_EFFECT),
    )(*bufs, *lands, send_sems, recv_sems, after)
    return list(out[n:])


def _sibling_forward(lands, name):
    n = len(lands)
    plan, first, n_sems = _gather_pieces([jax.ShapeDtypeStruct(l.shape[1:], l.dtype) for l in lands])

    def body(*refs):
        out_refs, (send_sems, recv_sems) = refs[n:2 * n], refs[2 * n:]
        x, y, c = _place()

        def copies(hc):
            return [pltpu.make_async_remote_copy(
                        src_ref=out_refs[i].at[2 * cx + cy, pl.ds(hc * plan[i][0] + q * plan[i][2], plan[i][2]), :],
                        dst_ref=out_refs[i].at[2 * cx + cy, pl.ds(hc * plan[i][0] + q * plan[i][2], plan[i][2]), :],
                        send_sem=send_sems.at[first[i] + 3 * q + j], recv_sem=recv_sems.at[first[i] + 3 * q + j],
                        device_id=(x, y, 1 - c), device_id_type=MESH)
                    for i in range(n) for q in range(plan[i][1]) for j, (cx, cy) in enumerate(_other_chips(x, y))]

        mine = copies(c)
        for cp in mine:
            cp.start()
        for cp in mine:
            cp.wait_send()
        for cp in copies(1 - c):
            cp.wait_recv()

    return pl.pallas_call(
        body, name=name, in_specs=[_HBM] * n, out_specs=[_HBM] * n, out_shape=[jax.ShapeDtypeStruct(l.shape, l.dtype) for l in lands],
        input_output_aliases={i: i for i in range(n)},
        scratch_shapes=[pltpu.SemaphoreType.DMA((n_sems,)), pltpu.SemaphoreType.DMA((n_sems,))],
    )(*lands)


N_PEERS = 7


def _direct_copies(p_refs, t_refs, send_sems, recv_sems):
    x, y, c = _place()
    copies = []
    for i, (p_ref, t_ref) in enumerate(zip(p_refs, t_refs)):
        for f in range(1, N_PEERS + 1):
            px, py, pc = x ^ (f >> 2), y ^ ((f >> 1) & 1), c ^ (f & 1)
            copies.append(pltpu.make_async_remote_copy(
                src_ref=p_ref.at[2 * px + py, pc], dst_ref=t_ref.at[f - 1], send_sem=send_sems.at[N_PEERS * i + f - 1],
                recv_sem=recv_sems.at[N_PEERS * i + f - 1], device_id=(px, py, pc), device_id_type=MESH))
    return copies


def _direct_scatter_start(ps, name):
    n = len(ps)

    def body(*refs):
        p_refs, t_refs, (send_sems, recv_sems), token = refs[:n], refs[n:2 * n], refs[2 * n:2 * n + 2], refs[-1]
        for cp in _direct_copies(p_refs, t_refs, send_sems, recv_sems):
            cp.start()
        token[...] = jnp.zeros_like(token)

    lands = [lax.empty((N_PEERS,) + p.shape[2:], p.dtype) for p in ps]
    out = pl.pallas_call(
        body, name=name,
        out_shape=(pltpu.SemaphoreType.DMA((N_PEERS * n,)), pltpu.SemaphoreType.DMA((N_PEERS * n,)),
                   *[pltpu.HBM(a.shape, a.dtype) for a in ps + lands], jax.ShapeDtypeStruct((8, 128), F32)),
        in_specs=[_HBM] * (2 * n), out_specs=(_SEM, _SEM, *[_HBM] * (2 * n), pl.BlockSpec(memory_space=pltpu.VMEM)),
        input_output_aliases={i: 2 + i for i in range(2 * n)}, compiler_params=pltpu.CompilerParams(has_side_effects=_EFFECT),
    )(*[_in_hbm(a) for a in ps + lands])
    return (out[0], out[1], list(out[2:2 + n]), list(out[2 + n:2 + 2 * n])), out[-1][0, 0]


def _direct_scatter_wait(pending, after, name):
    send_sems, recv_sems, ps, lands = pending
    n = len(ps)

    def body(*refs):
        p_refs, t_refs, send_sems, recv_sems = refs[:n], refs[n:2 * n], refs[2 * n], refs[2 * n + 1]
        for cp in _direct_copies(p_refs, t_refs, send_sems, recv_sems):
            cp.wait_send()
            cp.wait_recv()

    out = pl.pallas_call(
        body, name=name, out_shape=tuple(pltpu.HBM(a.shape, a.dtype) for a in ps + lands),
        in_specs=[_HBM] * (2 * n) + [_SEM, _SEM, _ANY], out_specs=tuple([_HBM] * (2 * n)),
        input_output_aliases={i: i for i in range(2 * n)}, compiler_params=pltpu.CompilerParams(has_side_effects=_EFFECT),
    )(*ps, *lands, send_sems, recv_sems, after)
    return list(out[:n]), list(out[n:])


def _sum_direct(p, t, name):
    _, _, R, C = p.shape
    tr = _pick(R, 512, 16)

    def body(x_ref, y_ref, c_ref, p_ref, t_ref, o_ref):
        acc = p_ref[...].astype(F32)
        for f in range(N_PEERS):
            acc = acc + t_ref[f].astype(F32)
        o_ref[...] = acc

    grid_spec = pltpu.PrefetchScalarGridSpec(
        num_scalar_prefetch=3, grid=(R // tr,),
        in_specs=[pl.BlockSpec((None, None, tr, C), lambda i, x_ref, y_ref, c_ref: (2 * x_ref[0] + y_ref[0], c_ref[0], i, 0)),
                  pl.BlockSpec((N_PEERS, tr, C), lambda i, x_ref, y_ref, c_ref: (0, i, 0))],
        out_specs=pl.BlockSpec((None, tr, C), lambda i, x_ref, y_ref, c_ref: (c_ref[0], i, 0)))
    return pl.pallas_call(body, name=name, grid_spec=grid_spec, out_shape=jax.ShapeDtypeStruct((2, R, C), F32),
                          compiler_params=pltpu.CompilerParams(dimension_semantics=("parallel",)))(
        _axis_scalar("x"), _axis_scalar("y"), _axis_scalar("c"), p, t)


def _adamw(w, g, m, v, name):
    R, C = w.shape
    tr = _pick(R, 256, 8)

    def body(w_ref, g_ref, m_ref, v_ref, d_ref, nm_ref, nv_ref):
        g_ = g_ref[...]
        m_ = ADAM_B1 * m_ref[...] + (1.0 - ADAM_B1) * g_
        v_ = ADAM_B2 * v_ref[...] + (1.0 - ADAM_B2) * (g_ * g_)
        m_hat = m_ / (1.0 - ADAM_B1 ** ADAM_STEP)
        v_hat = v_ / (1.0 - ADAM_B2 ** ADAM_STEP)
        d_ref[...] = -ADAM_LR * (m_hat / (jnp.sqrt(v_hat) + ADAM_EPS) + ADAM_WD * w_ref[...])
        nm_ref[...] = m_
        nv_ref[...] = v_

    row = pl.BlockSpec((tr, C), lambda i: (i, 0))
    shp = jax.ShapeDtypeStruct((R, C), F32)
    return pl.pallas_call(body, name=name, grid=(R // tr,), in_specs=[row] * 4, out_specs=[row] * 3, out_shape=[shp] * 3,
                          compiler_params=pltpu.CompilerParams(dimension_semantics=("parallel",)))(w, g, m, v)


BIG_SPECS = (("ev_w_in", 1024, 1440, 1), ("ev_w_uq", 256, 768, 1), ("ev_w_ukv", 128, 1024, 1), ("ev_w_out", 1024, 1024, 0),
             ("od_w_in", 1024, 6144, 1), ("od_w_out", 2048, 1024, 0), ("mlp_w1_0", 1024, 4096, 1), ("mlp_w1_1", 1024, 4096, 1),
             ("mlp_w2_0", 4096, 1024, 0), ("mlp_w2_1", 4096, 1024, 0))
BIG_PARAMS = (("ev_w_in", ("ev_w_in",)), ("ev_w_uq", ("ev_w_uq",)), ("ev_w_ukv", ("ev_w_ukv",)), ("ev_w_out", ("ev_w_out",)),
              ("od_w_in", ("od_w_in",)), ("od_w_out", ("od_w_out",)), ("mlp_w1", ("mlp_w1_0", "mlp_w1_1")),
              ("mlp_w2", ("mlp_w2_0", "mlp_w2_1")))
REPLICATED = ("ev_conv_b", "ev_w_rg_a", "ev_b_rg_a", "ev_w_rg_x", "ev_b_rg_x", "ev_lru_lambda", "ev_q_norm_g", "ev_kv_norm_g",
              "ln_mix_g", "ln_mix_b", "ln_mlp_g", "ln_mlp_b")
SMALL_SHARDED = ("meta_tokens", "ev_conv_w")
COL_SHARD_GRADS = ("od_w_in", "mlp_w1_0", "mlp_w1_1")
MATRIX_GROUPS = (("ev_w_in", "ev_w_uq", "ev_w_ukv", "ev_w_out"), ("mlp_w1_0", "mlp_w2_0"), ("od_w_in", "od_w_out", "mlp_w1_1", "mlp_w2_1"))
LAYER_NORMS = ("ln_mix_g", "ln_mix_b", "ln_mlp_g", "ln_mlp_b")
WEIGHT_NAMES = ("meta_tokens", "ev_w_in", "ev_conv_w", "ev_conv_b", "ev_w_rg_a", "ev_b_rg_a", "ev_w_rg_x", "ev_b_rg_x",
                "ev_lru_lambda", "ev_q_norm_g", "ev_w_uq", "ev_kv_norm_g", "ev_w_ukv", "ev_w_out", "od_w_in", "od_w_out",
                "ln_mix_g", "ln_mix_b", "mlp_w1", "mlp_w2", "ln_mlp_g", "ln_mlp_b")


def _to_rows(flat, row_align):
    n = flat.shape[-1]
    rows = _round_up(-(-n // PACK_COLS), row_align)
    pad = rows * PACK_COLS - n
    if pad:
        flat = jnp.pad(flat, [(0, 0)] * (flat.ndim - 1) + [(0, pad)])
    return flat.reshape(flat.shape[:-1] + (rows, PACK_COLS))


def _shard_shape(K, N, axis):
    return (K // N_CHIPS, N) if axis == 0 else (K, N // N_CHIPS)


def _gather_shards(stacked, K, N, axis):
    if axis == 0:
        return stacked.reshape(K, N)
    return stacked.transpose(1, 0, 2).reshape(K, N)


def _split_shards(full, K, N, axis):
    if axis == 0:
        return full.reshape(N_CHIPS, -1)
    return full.reshape(K, N_CHIPS, N // N_CHIPS).transpose(1, 0, 2).reshape(N_CHIPS, -1)


def kernel(x, meta_tokens, ev_w_in, ev_conv_w, ev_conv_b, ev_w_rg_a, ev_b_rg_a, ev_w_rg_x, ev_b_rg_x, ev_lru_lambda, ev_q_norm_g, ev_w_uq, ev_kv_norm_g, ev_w_ukv, ev_w_out, od_w_in, od_w_out, ln_mix_g, ln_mix_b, mlp_w1, mlp_w2, ln_mlp_g, ln_mlp_b, loss_target, m_meta_tokens, m_ev_w_in, m_ev_conv_w, m_ev_conv_b, m_ev_w_rg_a, m_ev_b_rg_a, m_ev_w_rg_x, m_ev_b_rg_x, m_ev_lru_lambda, m_ev_q_norm_g, m_ev_w_uq, m_ev_kv_norm_g, m_ev_w_ukv, m_ev_w_out, m_od_w_in, m_od_w_out, m_ln_mix_g, m_ln_mix_b, m_mlp_w1, m_mlp_w2, m_ln_mlp_g, m_ln_mlp_b, v_meta_tokens, v_ev_w_in, v_ev_conv_w, v_ev_conv_b, v_ev_w_rg_a, v_ev_b_rg_a, v_ev_w_rg_x, v_ev_b_rg_x, v_ev_lru_lambda, v_ev_q_norm_g, v_ev_w_uq, v_ev_kv_norm_g, v_ev_w_ukv, v_ev_w_out, v_od_w_in, v_od_w_out, v_ln_mix_g, v_ln_mix_b, v_mlp_w1, v_mlp_w2, v_ln_mlp_g, v_ln_mlp_b):
    given = dict(locals())
    local_big = {"ev_w_in": ev_w_in[0], "ev_w_uq": ev_w_uq[0], "ev_w_ukv": ev_w_ukv[0], "ev_w_out": ev_w_out[0],
                 "od_w_in": od_w_in[0], "od_w_out": od_w_out[0], "mlp_w1_0": mlp_w1[0], "mlp_w1_1": mlp_w1[1],
                 "mlp_w2_0": mlp_w2[0], "mlp_w2_1": mlp_w2[1]}

    specs = {spec[0]: spec for spec in BIG_SPECS}
    mixer0_m, mlp0_m, layer1_m = MATRIX_GROUPS

    def shards(names):
        return [local_big[n].astype(MXU_DTYPE) for n in names]

    def whole(stacked, n):
        _, K, N, ax = specs[n]
        return stacked if n in COL_SHARD_GRADS else _gather_shards(stacked, K, N, ax)

    def filled(gathered, own, names):
        return {n: whole(_with_own(g_, o_), n) for n, g_, o_ in zip(names, gathered, own)}

    own_a, own_b, own_c = shards(mixer0_m), shards(mlp0_m), shards(layer1_m)
    small = [meta_tokens, jnp.pad(ev_conv_w[0], ((0, 16 - CONV_WIDTH), (0, 0)))]
    gathered_a = _allgather_chips(own_a + small, "weight_allgather_mixer0")
    pending_b, token1 = _allgather_start(own_b, "weight_allgather_mlp0_start")
    pending_c, token2 = _allgather_start(own_c, "weight_allgather_layer1_start")
    meta_full = _gather_shards(_with_own(gathered_a[-2], small[0]), N_META, D_MODEL, 1)
    conv_full = _gather_shards(_with_own(gathered_a[-1], small[1])[:, :CONV_WIDTH], CONV_WIDTH, LRU_WIDTH, 1)

    def slots(names, dtype):
        return {n: jnp.zeros((N_CHIPS, specs[n][1], specs[n][2] // N_CHIPS) if n in COL_SHARD_GRADS else specs[n][1:3], dtype) for n in names}

    def norms(names, layer):
        return {n: given[n][layer] for n in names}

    def finish_gather(pending, own, after, names, tag):
        landed = _allgather_wait(pending, lax.stop_gradient(after), "weight_allgather_%s_wait" % tag)
        return filled(_sibling_forward(landed, "weight_allgather_%s_forward" % tag), own, names)

    diff_a = {**slots(mixer0_m, F32), **norms(("ln_mix_g", "ln_mix_b"), 0), **{n: given[n] for n in REPLICATED if n not in LAYER_NORMS},
              "x": x, "meta_tokens": meta_full, "ev_conv_w": conv_full}
    diff_b = {**slots(mlp0_m, MXU_DTYPE), **norms(("ln_mlp_g", "ln_mlp_b"), 0)}
    diff_c = {**slots(layer1_m, MXU_DTYPE), **norms(LAYER_NORMS, 1)}
    w_a = filled(gathered_a[:len(mixer0_m)], own_a, mixer0_m)
    h_a, back_a = jax.vjp(lambda d: _mixer0(d, w_a, token1 + token2), diff_a)
    w_b = finish_gather(pending_b, own_b, h_a, mlp0_m, "mlp0")
    h_b, back_b = jax.vjp(lambda d, hh: _mlp0(d, hh, w_b), diff_b, h_a)
    w_c = finish_gather(pending_c, own_c, h_b, layer1_m, "layer1")
    loss, back_c = jax.vjp(lambda d, hh: _layer1_loss(d, hh, w_c, loss_target), diff_c, h_b)
    loss = lax.psum(loss, ("x", "y", "c"))

    def blocks_of(grad, n):
        _, K, N, ax = specs[n]
        if n in COL_SHARD_GRADS:
            blocks = grad
        elif ax == 0:
            blocks = grad.reshape(N_CHIPS, K // N_CHIPS, N)
        else:
            blocks = grad.reshape(K, N_CHIPS, N // N_CHIPS).transpose(1, 0, 2)
        return blocks.reshape(N_CHIPS, 2, blocks.shape[1] // 2, blocks.shape[2])

    def start_reduce(grads_of, names, tag):
        return _direct_scatter_start([blocks_of(grads_of[n], n) for n in names], "grad_scatter_%s_start" % tag)

    g_c, dh = back_c(jnp.ones((), F32))
    flying_c, token = start_reduce(g_c, layer1_m, "layer1")
    g_b, dh = back_b(dh + token)
    flying_b, token = start_reduce(g_b, mlp0_m, "mlp0")
    (g_a,) = back_a(dh + token)
    ps_c, ts_c = _direct_scatter_wait(flying_c, g_a["x"], "grad_scatter_layer1_wait")
    ps_b, ts_b = _direct_scatter_wait(flying_b, g_a["x"], "grad_scatter_mlp0_wait")

    g = {**g_a, **g_b, **g_c}
    g.update({n: jnp.stack([(g_b if n in g_b else g_a)[n], g_c[n]]) for n in LAYER_NORMS})
    repl = jnp.concatenate([g[n].reshape(-1) for n in REPLICATED]).reshape(N_CHIPS, -1)
    small = [_split_shards(g["meta_tokens"], N_META, D_MODEL, 1), _split_shards(g["ev_conv_w"], CONV_WIDTH, LRU_WIDTH, 1), repl]
    small = [pc.reshape(N_CHIPS, 2, -1) for pc in small]
    n_small = sum(pc.shape[2] for pc in small)
    small.append(jnp.zeros((N_CHIPS, 2, _round_up(n_small, 32 * PACK_COLS) - n_small), F32))
    p_small = jnp.concatenate(small, axis=2).reshape(N_CHIPS, 2, -1, PACK_COLS)
    ss_a = _sibling_reduce([blocks_of(g_a[n], n) for n in mixer0_m] + [p_small], [MXU_DTYPE] * len(mixer0_m) + [F32], "mixer0_")
    ts_a = list(_chip_scatter(ss_a, "grad_chip_scatter_mixer0"))
    fs = [_sum_partials(s, t, "grad_chip_sum_mixer0_%d" % i) for i, (s, t) in enumerate(zip(ss_a, ts_a))]
    fs += [_sum_direct(p, t, "grad_sum_%d" % i) for i, (p, t) in enumerate(zip(ps_b + ps_c, ts_b + ts_c))]
    reduced = _sibling_gather(fs, "grad_sibling_gather")
    red_big = dict(zip(mixer0_m + ("small",) + mlp0_m + layer1_m, reduced))
    red_small = red_big.pop("small").reshape(2, -1)

    grads = {}
    for name, parts in BIG_PARAMS:
        grads[name] = jnp.stack([red_big[part].reshape(given[name].shape[1:]) for part in parts])

    def take(off, sz):
        return jnp.concatenate([red_small[0, off // 2:(off + sz) // 2], red_small[1, off // 2:(off + sz) // 2]])

    off = 0
    for name in SMALL_SHARDED:
        sz = given[name].size
        grads[name] = take(off, sz).reshape(given[name].shape)
        off += sz
    n_repl = repl.shape[1]
    own_repl = _to_rows(take(off, n_repl), 16)
    repl_all = _with_own(_allgather_chips([own_repl], "replicated_allgather")[0], own_repl).reshape(N_CHIPS, -1)[:, :n_repl].reshape(-1)
    off = 0
    for name in REPLICATED:
        sz = given[name].size
        grads[name] = repl_all[off:off + sz].reshape(given[name].shape)
        off += sz

    delta, new_m, new_v = {}, {}, {}
    for name, _ in BIG_PARAMS:
        shp = given[name].shape
        two_d = (-1, shp[-1])
        d, nm, nv = _adamw(given[name].reshape(two_d), grads[name].reshape(two_d), given["m_" + name].reshape(two_d),
                           given["v_" + name].reshape(two_d), "adamw_" + name)
        delta[name], new_m[name], new_v[name] = d.reshape(shp), nm.reshape(shp), nv.reshape(shp)
    smalls = SMALL_SHARDED + REPLICATED

    def pack_small(get):
        return _to_rows(jnp.concatenate([get(n).reshape(-1) for n in smalls]), 8)

    outs = _adamw(pack_small(lambda n: given[n]), pack_small(lambda n: grads[n]), pack_small(lambda n: given["m_" + n]),
                  pack_small(lambda n: given["v_" + n]), "adamw_small")
    for res, flat in zip((delta, new_m, new_v), outs):
        flat, off = flat.reshape(-1), 0
        for n in smalls:
            sz = given[n].size
            res[n] = flat[off:off + sz].reshape(given[n].shape)
            off += sz

    return (loss, g_a["x"], *[grads[n] for n in WEIGHT_NAMES], *[delta[n] for n in WEIGHT_NAMES],
            *[new_m[n] for n in WEIGHT_NAMES], *[new_v[n] for n in WEIGHT_NAMES])
```

```python
import functools
import math

import jax
import jax.numpy as jnp
from jax import lax
from jax.experimental import pallas as pl
from jax.experimental.pallas import tpu as pltpu

F32 = jnp.float32
MXU_DTYPE = jnp.bfloat16

D_MODEL = 1024
N_META = 16
LRU_WIDTH = 512
LRU_HEADS = 4
LRU_HEAD_DIM = 128
CONV_WIDTH = 4
LRU_C = 8.0
MLA_HEADS = 8
MLA_NOPE = 64
MLA_ROPE = 32
MLA_V = 64
MLA_Q_RANK = 256
MLA_KV_RANK = 128
RET_HEADS = 4
RET_QK_DIM = 256
RET_V_DIM = 512
D_FF = 4096
ROPE_BASE = 10000.0
DN_ALPHA = 4.0 ** 0.25
EPS = 1e-5
NEG_INF = -1e30
SEQ_BLOCK = 128

ADAM_LR = 0.001
ADAM_B1 = 0.9
ADAM_B2 = 0.999
ADAM_EPS = 1e-08
ADAM_WD = 0.01
ADAM_STEP = 10

PACK_COLS = 1024
N_CHIPS = 4

MESH = pl.DeviceIdType.MESH


def _pick(n, target, align):
    best = None
    for t in range(align, min(n, target) + 1, align):
        if n % t == 0:
            best = t
    return n if best is None else best


def _round_up(n, m):
    return (n + m - 1) // m * m


def _relu2(a):
    r = jnp.maximum(a, 0.0)
    return r * r


def _ln_stats(z):
    mu = jnp.mean(z, axis=-1, keepdims=True)
    zc = z - mu
    var = jnp.mean(zc * zc, axis=-1, keepdims=True)
    return zc, lax.rsqrt(var + EPS)


def _mm_nn(a, w, act, name, out_dtype=F32, norm=None):
    M, K = a.shape
    sharded = w.ndim == 3
    n = w.shape[-1]
    N = n * (w.shape[0] if sharded else 1)
    tm = _pick(M, 1088 if K * a.dtype.itemsize <= 4096 and norm is None else 544, 8)
    tn = _pick(n, 1024, 128)
    per = n // tn
    assert norm is None or tn == N

    def body(a_ref, w_ref, *rest):
        av = a_ref[...]
        if act:
            av = _relu2(av.astype(F32))
        r = jnp.dot(av.astype(MXU_DTYPE), w_ref[...].astype(MXU_DTYPE), preferred_element_type=F32)
        if norm is None:
            rest[0][...] = r.astype(out_dtype)
        else:
            r_ref, g_ref, b_ref, o_ref, z_ref = rest
            z = DN_ALPHA * r_ref[...] + r
            zc, rstd = _ln_stats(z)
            z_ref[...] = z
            o_ref[...] = zc * rstd * g_ref[...] + b_ref[...]

    w_spec = pl.BlockSpec((None, K, tn), lambda i, j: (j // per, 0, j % per)) if sharded else pl.BlockSpec((K, tn), lambda i, j: (0, j))
    tile = pl.BlockSpec((tm, tn), lambda i, j: (i, j))
    in_specs, args = [pl.BlockSpec((tm, K), lambda i, j: (i, 0)), w_spec], [a, w]
    if norm is None:
        out_specs, out_shape = tile, jax.ShapeDtypeStruct((M, N), out_dtype)
    else:
        vec = pl.BlockSpec((1, N), lambda i, j: (0, 0))
        in_specs += [tile, vec, vec]
        args += [norm[0], norm[1].reshape(1, N), norm[2].reshape(1, N)]
        out_specs, out_shape = [tile, tile], [jax.ShapeDtypeStruct((M, N), F32)] * 2
    return pl.pallas_call(
        body, name=name, grid=(M // tm, N // tn), in_specs=in_specs, out_specs=out_specs, out_shape=out_shape,
        compiler_params=pltpu.CompilerParams(dimension_semantics=("parallel", "arbitrary")),
    )(*args)


def _mm_nt(g, w, a_src, name, out_dtype=F32, plus=None):
    M, N = g.shape
    sharded = w.ndim == 3
    K, n = w.shape[-2], w.shape[-1]
    if sharded:
        tk, nk = N, 1
    else:
        tk = N if N * g.dtype.itemsize <= 8192 else _pick(N, 2048, 128)
        nk = N // tk
    tm = _pick(M, 1088 if tk * g.dtype.itemsize <= 4096 else 544, 8)
    tn = _pick(K, 1024, 128)
    has_src = a_src is not None
    assert nk == 1 or out_dtype == F32
    assert plus is None or not has_src

    def body(*refs):
        if has_src:
            g_ref, w_ref, s_ref, o_ref = refs
        elif plus is not None:
            g_ref, w_ref, p_ref, o_ref = refs
        else:
            g_ref, w_ref, o_ref = refs
        nt = (((1,), (1,)), ((), ()))
        if sharded:
            r = sum(lax.dot_general(g_ref[:, s * n:(s + 1) * n].astype(MXU_DTYPE), w_ref[s].astype(MXU_DTYPE), nt, preferred_element_type=F32)
                    for s in range(w_ref.shape[0]))
        else:
            r = lax.dot_general(g_ref[...].astype(MXU_DTYPE), w_ref[...].astype(MXU_DTYPE), nt, preferred_element_type=F32)
        if has_src:
            r = r * (2.0 * jnp.maximum(s_ref[...].astype(F32), 0.0))
        first = r if plus is None else r + DN_ALPHA * p_ref[...]
        if nk == 1:
            o_ref[...] = first.astype(out_dtype)
        else:
            k = pl.program_id(2)

            @pl.when(k == 0)
            def _():
                o_ref[...] = first

            @pl.when(k > 0)
            def _():
                o_ref[...] += r

    w_spec = (pl.BlockSpec((w.shape[0], tn, n), lambda i, j, k: (0, j, 0)) if sharded
              else pl.BlockSpec((tn, tk), lambda i, j, k: (j, k)))
    in_specs = [pl.BlockSpec((tm, tk), lambda i, j, k: (i, k)), w_spec]
    args = [g, w]
    if has_src:
        assert nk == 1
        in_specs.append(pl.BlockSpec((tm, tn), lambda i, j, k: (i, j)))
        args.append(a_src)
    if plus is not None:
        in_specs.append(pl.BlockSpec((tm, tn), lambda i, j, k: (i, j)))
        args.append(plus)
    return pl.pallas_call(
        body, name=name,
        grid=(M // tm, K // tn, nk),
        in_specs=in_specs,
        out_specs=pl.BlockSpec((tm, tn), lambda i, j, k: (i, j)),
        out_shape=jax.ShapeDtypeStruct((M, K), out_dtype),
        compiler_params=pltpu.CompilerParams(dimension_semantics=("parallel", "parallel", "arbitrary")),
    )(*args)


def _mm_tn(a, g, act, name, col_shards=1, out_dtype=F32):
    M, K = a.shape
    _, N = g.shape
    n = N // col_shards
    tm, tn, tk = _pick(K, 1024, 128), _pick(n, 1024, 128), _pick(M, 2176, 8)
    nk = M // tk
    per = n // tn
    direct = out_dtype == F32

    def body(a_ref, g_ref, o_ref, *scratch):
        acc_ref = o_ref if direct else scratch[0]
        k = pl.program_id(2)
        av = a_ref[...]
        if act:
            av = _relu2(av.astype(F32))
        r = lax.dot_general(av.astype(MXU_DTYPE), g_ref[...].astype(MXU_DTYPE),
                            (((0,), (0,)), ((), ())), preferred_element_type=F32)

        @pl.when(k == 0)
        def _():
            acc_ref[...] = r

        @pl.when(k > 0)
        def _():
            acc_ref[...] += r

        if not direct:
            @pl.when(k == nk - 1)
            def _():
                o_ref[...] = acc_ref[...].astype(out_dtype)

    if col_shards == 1:
        out_spec, out_shape = pl.BlockSpec((tm, tn), lambda i, j, k: (i, j)), (K, N)
    else:
        out_spec, out_shape = pl.BlockSpec((None, tm, tn), lambda i, j, k: (j // per, i, j % per)), (col_shards, K, n)
    return pl.pallas_call(
        body, name=name,
        grid=(K // tm, N // tn, nk),
        in_specs=[pl.BlockSpec((tk, tm), lambda i, j, k: (k, i)), pl.BlockSpec((tk, tn), lambda i, j, k: (k, j))],
        out_specs=out_spec,
        out_shape=jax.ShapeDtypeStruct(out_shape, out_dtype),
        scratch_shapes=[] if direct else [pltpu.VMEM((tm, tn), F32)],
        compiler_params=pltpu.CompilerParams(dimension_semantics=("parallel", "parallel", "arbitrary")),
    )(a, g)


@functools.partial(jax.custom_vjp, nondiff_argnums=(3, 4, 5, 6))
def matmul(a, w, w_grad_slot, act, name, out_dtype, col_shards):
    return _mm_nn(a, w, act, name + "_fwd", out_dtype)


def _matmul_fwd(a, w, w_grad_slot, act, name, out_dtype, col_shards):
    return _mm_nn(a, w, act, name + "_fwd", out_dtype), (a, w, jnp.zeros((), w_grad_slot.dtype))


def _matmul_bwd(act, name, out_dtype, col_shards, res, g):
    a, w, slot_like = res
    w_grad_dtype = slot_like.dtype
    da = _mm_nt(g, w, a if act else None, name + "_dx")
    dw = _mm_tn(a, g, act, name + "_dw", col_shards, w_grad_dtype)
    return da, None, dw


matmul.defvjp(_matmul_fwd, _matmul_bwd)


def _ln_bwd_call(z, g, dy, name):
    M, D = z.shape
    tm = _pick(M, 544, 8)

    def body(z_ref, g_ref, dy_ref, dz_ref, dg_ref, db_ref):
        @pl.when(pl.program_id(0) == 0)
        def _():
            dg_ref[...] = jnp.zeros_like(dg_ref)
            db_ref[...] = jnp.zeros_like(db_ref)

        zc, rstd = _ln_stats(z_ref[...])
        xhat = zc * rstd
        dy = dy_ref[...]
        dxh = dy * g_ref[...]
        m1 = jnp.mean(dxh, axis=-1, keepdims=True)
        m2 = jnp.mean(dxh * xhat, axis=-1, keepdims=True)
        dz_ref[...] = rstd * (dxh - m1 - xhat * m2)
        dg_ref[...] += jnp.sum(dy * xhat, axis=0, keepdims=True)
        db_ref[...] += jnp.sum(dy, axis=0, keepdims=True)

    row = pl.BlockSpec((tm, D), lambda i: (i, 0))
    vec = pl.BlockSpec((1, D), lambda i: (0, 0))
    return pl.pallas_call(
        body, name=name, grid=(M // tm,), in_specs=[row, vec, row], out_specs=[row, vec, vec],
        out_shape=[jax.ShapeDtypeStruct((M, D), F32), jax.ShapeDtypeStruct((1, D), F32), jax.ShapeDtypeStruct((1, D), F32)],
        compiler_params=pltpu.CompilerParams(dimension_semantics=("arbitrary",)),
    )(z, g.reshape(1, D), dy)


@functools.partial(jax.custom_vjp, nondiff_argnums=(7,))
def mlp_block(h, w1, w2, w1_grad_slot, w2_grad_slot, g, b, name):
    return _mlp_block_fwd(h, w1, w2, w1_grad_slot, w2_grad_slot, g, b, name)[0]


def _mlp_block_fwd(h, w1, w2, w1_grad_slot, w2_grad_slot, g, b, name):
    u = _mm_nn(h, w1, False, name + "_w1_fwd", out_dtype=MXU_DTYPE)
    out, z = _mm_nn(u, w2, True, name + "_w2_norm_fwd", norm=(h, g, b))
    return out, (h, u, z, w1, w2, g, jnp.zeros((), w1_grad_slot.dtype))


def _mlp_block_bwd(name, res, dy):
    h, u, z, w1, w2, g, slot_like = res
    dz, dg, db = _ln_bwd_call(z, g, dy, name + "_norm_bwd")
    du = _mm_nt(dz, w2, u, name + "_w2_dx", out_dtype=MXU_DTYPE)
    dw2 = _mm_tn(u, dz, True, name + "_w2_dw", 1, slot_like.dtype)
    dh = _mm_nt(du, w1, None, name + "_w1_dx", plus=dz)
    dw1 = _mm_tn(h, du, False, name + "_w1_dw", N_CHIPS, slot_like.dtype)
    return dh, None, None, dw1, dw2, dg.reshape(g.shape), db.reshape(g.shape)


mlp_block.defvjp(_mlp_block_fwd, _mlp_block_bwd)


@functools.partial(jax.custom_vjp, nondiff_argnums=(6,))
def out_block(h, y, w, w_grad_slot, g, b, name):
    return _out_block_fwd(h, y, w, w_grad_slot, g, b, name)[0]


def _out_block_fwd(h, y, w, w_grad_slot, g, b, name):
    out, z = _mm_nn(y, w, False, name + "_norm_fwd", norm=(h, g, b))
    return out, (y, z, w, g, jnp.zeros((), w_grad_slot.dtype))


def _out_block_bwd(name, res, dy):
    y, z, w, g, slot_like = res
    dz, dg, db = _ln_bwd_call(z, g, dy, name + "_norm_bwd")
    d_y = _mm_nt(dz, w, None, name + "_dx")
    dw = _mm_tn(y, dz, False, name + "_dw", 1, slot_like.dtype)
    return DN_ALPHA * dz, d_y, None, dw, dg.reshape(g.shape), db.reshape(g.shape)


out_block.defvjp(_out_block_fwd, _out_block_bwd)


def _rms_fwd_call(x, g, name, col_block=0):
    R = x.shape[0]
    W = g.shape[-1]
    tr = _pick(R, 1088, 8)

    def body(x_ref, g_ref, o_ref):
        xv = x_ref[...]
        rstd = lax.rsqrt(jnp.mean(xv * xv, axis=-1, keepdims=True) + EPS)
        o_ref[...] = xv * rstd * g_ref[...]

    vec = pl.BlockSpec((1, W), lambda i: (0, 0))
    return pl.pallas_call(
        body, name=name, grid=(R // tr,), in_specs=[pl.BlockSpec((tr, W), lambda i: (i, col_block)), vec],
        out_specs=pl.BlockSpec((tr, W), lambda i: (i, 0)), out_shape=jax.ShapeDtypeStruct((R, W), F32),
        compiler_params=pltpu.CompilerParams(dimension_semantics=("parallel",)),
    )(x, g.reshape(1, W))


def _rms_bwd_call(x, g, dy, name, col_block=0):
    R = x.shape[0]
    W = g.shape[-1]
    tr = _pick(R, 1088, 8)

    def body(x_ref, g_ref, dy_ref, dx_ref, dg_ref):
        @pl.when(pl.program_id(0) == 0)
        def _():
            dg_ref[...] = jnp.zeros_like(dg_ref)

        xv = x_ref[...]
        rstd = lax.rsqrt(jnp.mean(xv * xv, axis=-1, keepdims=True) + EPS)
        xhat = xv * rstd
        dy = dy_ref[...]
        dxh = dy * g_ref[...]
        dx_ref[...] = rstd * (dxh - xhat * jnp.mean(dxh * xhat, axis=-1, keepdims=True))
        dg_ref[...] += jnp.sum(dy * xhat, axis=0, keepdims=True)

    row = pl.BlockSpec((tr, W), lambda i: (i, 0))
    vec = pl.BlockSpec((1, W), lambda i: (0, 0))
    return pl.pallas_call(
        body, name=name, grid=(R // tr,), in_specs=[pl.BlockSpec((tr, W), lambda i: (i, col_block)), vec, row], out_specs=[row, vec],
        out_shape=[jax.ShapeDtypeStruct((R, W), F32), jax.ShapeDtypeStruct((1, W), F32)],
        compiler_params=pltpu.CompilerParams(dimension_semantics=("arbitrary",)),
    )(x, g.reshape(1, W), dy)


def _loss_call(h, tgt, n_tokens, name):
    B, Tp, D = h.shape
    tr = _pick(Tp, 544, 8)

    def body(y_ref, t_ref, dy_ref, acc_ref):
        @pl.when(jnp.logical_and(pl.program_id(0) == 0, pl.program_id(1) == 0))
        def _():
            acc_ref[...] = jnp.zeros_like(acc_ref)

        t = lax.broadcasted_iota(jnp.int32, (tr, 1), 0) + pl.program_id(1) * tr
        counts = jnp.logical_and(t >= N_META, t < N_META + n_tokens)
        e = jnp.where(counts, y_ref[...] - t_ref[...], 0.0)
        dy_ref[...] = e * (1.0 / D)
        acc_ref[...] += jnp.sum(jnp.sum(e * e, axis=-1, keepdims=True), axis=0, keepdims=True) * (0.5 / D)

    row = pl.BlockSpec((None, tr, D), lambda b, i: (b, i, 0))
    one = pl.BlockSpec((1, 1), lambda b, i: (0, 0))
    return pl.pallas_call(
        body, name=name, grid=(B, Tp // tr), in_specs=[row, row], out_specs=[row, one],
        out_shape=[jax.ShapeDtypeStruct((B, Tp, D), F32), jax.ShapeDtypeStruct((1, 1), F32)],
        compiler_params=pltpu.CompilerParams(dimension_semantics=("arbitrary", "arbitrary")),
    )(h, tgt)


@functools.partial(jax.custom_vjp, nondiff_argnums=(2,))
def loss_head(h, tgt, n_tokens):
    return _loss_call(h, tgt, n_tokens, "loss_head")[1][0, 0]


def _loss_head_fwd(h, tgt, n_tokens):
    dy, acc = _loss_call(h, tgt, n_tokens, "loss_head")
    return acc[0, 0], dy


def _loss_head_bwd(n_tokens, dy, ct):
    return ct * dy, None


loss_head.defvjp(_loss_head_fwd, _loss_head_bwd)


_GELU_C = math.sqrt(2.0 / math.pi)


def _gelu_parts(x):
    x2 = x * x
    t = jnp.tanh(_GELU_C * (x + 0.044715 * x * x2))
    gelu = 0.5 * x * (1.0 + t)
    dgelu = 0.5 * (1.0 + t) + 0.5 * x * (1.0 - t * t) * (_GELU_C * (1.0 + 3.0 * 0.044715 * x2))
    return gelu, dgelu


def _sigmoid(x):
    return 1.0 / (1.0 + jnp.exp(-x))


def _scan8(a, b, carry, reverse):
    row = lax.broadcasted_iota(jnp.int32, a.shape, 0)
    for s in (1, 2, 4):
        shift = 8 - s if reverse else s
        keep = (row < 8 - s) if reverse else (row >= s)
        b = jnp.where(keep, a * pltpu.roll(b, shift, 0) + b, b)
        a = jnp.where(keep, a * pltpu.roll(a, shift, 0), a)
    return a * carry + b


def _lru_pre(prec_ref, prev_ref, first, cw_ref, cb_ref, wa_ref, ba_ref, wx_ref, bx_ref, sp_ref):
    tc = prec_ref.shape[0]
    prev = jnp.where(first, 0.0, prev_ref[...])
    ext = jnp.concatenate([prev, prec_ref[...]], axis=0)
    cw = cw_ref[...]
    taps = [ext[8:] if k == CONV_WIDTH - 1 else pltpu.roll(ext, CONV_WIDTH - 1 - k, 0)[8:] for k in range(CONV_WIDTH)]
    xc = cb_ref[...] + sum(cw[k:k + 1, :] * taps[k] for k in range(CONV_WIDTH))
    ga, gx = [], []
    for h in range(LRU_HEADS):
        xh = xc[:, h * LRU_HEAD_DIM:(h + 1) * LRU_HEAD_DIM].astype(MXU_DTYPE)
        ga.append(jnp.dot(xh, wa_ref[h].astype(MXU_DTYPE), preferred_element_type=F32))
        gx.append(jnp.dot(xh, wx_ref[h].astype(MXU_DTYPE), preferred_element_type=F32))
    r = _sigmoid(jnp.concatenate(ga, axis=1) + ba_ref[...])
    i = _sigmoid(jnp.concatenate(gx, axis=1) + bx_ref[...])
    log_a = -LRU_C * r * sp_ref[...]
    a = jnp.exp(log_a)
    a2 = a * a
    mult = jnp.sqrt(-jnp.tanh(log_a) * (a2 + 1.0))
    return taps, xc, r, i, a, a2, mult


def _lru_fwd_call(p, cw, cb, wa, ba, wx, bx, sp):
    B, Tp, _ = p.shape
    W = LRU_WIDTH
    tc = SEQ_BLOCK
    nc = Tp // tc

    def body(pg_ref, prec_ref, prev_ref, cw_ref, cb_ref, wa_ref, ba_ref, wx_ref, bx_ref, sp_ref, y_ref, h_ref, carry_ref):
        first = pl.program_id(1) == 0

        @pl.when(first)
        def _():
            carry_ref[...] = jnp.zeros_like(carry_ref)

        _, xc, r, i, a, a2, mult = _lru_pre(prec_ref, prev_ref, first, cw_ref, cb_ref, wa_ref, ba_ref, wx_ref, bx_ref, sp_ref)
        b = mult * (i * xc)
        carry = carry_ref[0:1, :]
        for t in range(tc // 8):
            h = _scan8(a[8 * t:8 * t + 8], b[8 * t:8 * t + 8], carry, False)
            h_ref[8 * t:8 * t + 8, :] = h
            carry = h[7:8, :]
        carry_ref[...] = jnp.broadcast_to(carry, carry_ref.shape)
        y_ref[...] = h_ref[...] * _gelu_parts(pg_ref[...])[0]

    cur = pl.BlockSpec((None, tc, W), lambda b, j: (b, j, 0))
    rec = pl.BlockSpec((None, tc, W), lambda b, j: (b, j, 1))
    prev = pl.BlockSpec((None, 8, W), lambda b, j: (b, jnp.maximum(j * (tc // 8) - 1, 0), 1))
    vec = pl.BlockSpec((1, W), lambda b, j: (0, 0))
    cws = pl.BlockSpec((CONV_WIDTH, W), lambda b, j: (0, 0))
    wsp = pl.BlockSpec((LRU_HEADS, LRU_HEAD_DIM, LRU_HEAD_DIM), lambda b, j: (0, 0, 0))
    return pl.pallas_call(
        body, name="lru_fwd", grid=(B, nc),
        in_specs=[cur, rec, prev, cws, vec, wsp, vec, wsp, vec, vec],
        out_specs=[cur, cur],
        out_shape=[jax.ShapeDtypeStruct((B, Tp, W), F32), jax.ShapeDtypeStruct((B, Tp, W), F32)],
        scratch_shapes=[pltpu.VMEM((8, W), F32)],
        compiler_params=pltpu.CompilerParams(dimension_semantics=("arbitrary", "arbitrary")),
    )(p, p, p, cw, cb, wa, ba, wx, bx, sp)


def _lru_bwd_call(p, hseq, dy, cw, cb, wa, ba, wx, bx, sp, dpq, dpkv, dkpe):
    B, Tp, P = p.shape
    W = LRU_WIDTH
    tc = SEQ_BLOCK
    nc = Tp // tc
    HD = LRU_HEAD_DIM

    def body(pg_ref, prec_ref, prev_ref, h_ref, hprev_ref, dy_ref, cw_ref, cb_ref, wa_ref, ba_ref, wx_ref, bx_ref, sp_ref,
             dpq_ref, dpkv_ref, dkpe_ref, dp_ref, dcw_ref, dcb_ref, dwa_ref, dba_ref, dwx_ref, dbx_ref, dsp_ref,
             gcar_ref, anext_ref, halo_ref, g_ref):
        j = pl.program_id(1)
        first = j == nc - 1
        last = j == 0

        @pl.when(jnp.logical_and(pl.program_id(0) == 0, last))
        def _():
            for ref in (dcw_ref, dcb_ref, dwa_ref, dba_ref, dwx_ref, dbx_ref, dsp_ref):
                ref[...] = jnp.zeros_like(ref)

        @pl.when(last)
        def _():
            gcar_ref[...] = jnp.zeros_like(gcar_ref)
            anext_ref[...] = jnp.zeros_like(anext_ref)
            halo_ref[...] = jnp.zeros_like(halo_ref)

        taps, xc, r, i, a, a2, mult = _lru_pre(prec_ref, prev_ref, first, cw_ref, cb_ref, wa_ref, ba_ref, wx_ref, bx_ref, sp_ref)
        row = lax.broadcasted_iota(jnp.int32, (tc, W), 0)
        gelu, dgelu = _gelu_parts(pg_ref[...])
        dy = dy_ref[...]
        hcur = h_ref[...]
        dp_ref[:, 0:W] = dy * hcur * dgelu
        dp_ref[:, 2 * W:2 * W + MLA_Q_RANK] = dpq_ref[...]
        dp_ref[:, _KPE_START - MLA_KV_RANK:_KPE_START] = dpkv_ref[...]
        dp_ref[:, _KPE_START:P] = pltpu.roll(dkpe_ref[...], HEAD_LANES - MLA_NOPE, 1)[:, 0:P - _KPE_START]
        dh = dy * gelu
        a_next = jnp.where(row == tc - 1, anext_ref[0:1, :], pltpu.roll(a, tc - 1, 0))
        carry = gcar_ref[0:1, :]
        for t in reversed(range(tc // 8)):
            g = _scan8(a_next[8 * t:8 * t + 8], dh[8 * t:8 * t + 8], carry, True)
            g_ref[8 * t:8 * t + 8, :] = g
            carry = g[0:1, :]
        gcar_ref[...] = jnp.broadcast_to(carry, gcar_ref.shape)
        anext_ref[...] = jnp.broadcast_to(a[0:1, :], anext_ref.shape)
        G = g_ref[...]
        h_before = jnp.where(first, 0.0, hprev_ref[7:8, :])
        hprev = jnp.where(row == 0, h_before, pltpu.roll(hcur, 1, 0))
        d_a = G * hprev
        gx_ = G * xc
        d_mult = gx_ * i
        d_i = gx_ * mult
        dxc = G * (mult * i)
        d_la = d_a * a - d_mult * (a2 / mult)
        sp = sp_ref[...]
        d_r = d_la * (-LRU_C * sp)
        dsp_ref[...] += jnp.sum(d_la * (-LRU_C * r), axis=0, keepdims=True)
        dga = d_r * r * (1.0 - r)
        dgx = d_i * i * (1.0 - i)
        dba_ref[...] += jnp.sum(dga, axis=0, keepdims=True)
        dbx_ref[...] += jnp.sum(dgx, axis=0, keepdims=True)
        back = []
        for h in range(LRU_HEADS):
            sl = slice(h * HD, (h + 1) * HD)
            xh = xc[:, sl].astype(MXU_DTYPE)
            ah = dga[:, sl].astype(MXU_DTYPE)
            bh = dgx[:, sl].astype(MXU_DTYPE)
            tn = (((0,), (0,)), ((), ()))
            nt = (((1,), (1,)), ((), ()))
            dwa_ref[h] += lax.dot_general(xh, ah, tn, preferred_element_type=F32)
            dwx_ref[h] += lax.dot_general(xh, bh, tn, preferred_element_type=F32)
            back.append(lax.dot_general(ah, wa_ref[h].astype(MXU_DTYPE), nt, preferred_element_type=F32)
                        + lax.dot_general(bh, wx_ref[h].astype(MXU_DTYPE), nt, preferred_element_type=F32))
        dxc = dxc + jnp.concatenate(back, axis=1)
        dcb_ref[...] += jnp.sum(dxc, axis=0, keepdims=True)
        for k in range(CONV_WIDTH):
            dcw_ref[k:k + 1, :] += jnp.sum(dxc * taps[k], axis=0, keepdims=True)
        ext = jnp.concatenate([dxc, halo_ref[...]], axis=0)
        cw = cw_ref[...]
        acc = cw[CONV_WIDTH - 1:CONV_WIDTH, :] * dxc
        for k in range(CONV_WIDTH - 1):
            s = CONV_WIDTH - 1 - k
            acc = acc + cw[k:k + 1, :] * pltpu.roll(ext, tc + 8 - s, 0)[:tc]
        dp_ref[:, W:2 * W] = acc
        halo_ref[...] = dxc[0:8, :]

    rev = lambda j: nc - 1 - j
    cur = pl.BlockSpec((None, tc, W), lambda b, j: (b, rev(j), 0))
    rec = pl.BlockSpec((None, tc, W), lambda b, j: (b, rev(j), 1))
    prev = pl.BlockSpec((None, 8, W), lambda b, j: (b, jnp.maximum(rev(j) * (tc // 8) - 1, 0), 0))
    prev_rec = pl.BlockSpec((None, 8, W), lambda b, j: (b, jnp.maximum(rev(j) * (tc // 8) - 1, 0), 1))
    vec = pl.BlockSpec((1, W), lambda b, j: (0, 0))
    cws = pl.BlockSpec((CONV_WIDTH, W), lambda b, j: (0, 0))
    wsp = pl.BlockSpec((LRU_HEADS, HD, HD), lambda b, j: (0, 0, 0))
    vs = jax.ShapeDtypeStruct((1, W), F32)
    ws = jax.ShapeDtypeStruct((LRU_HEADS, HD, HD), F32)

    def rows(width):
        return pl.BlockSpec((None, tc, width), lambda b, j: (b, rev(j), 0))

    return pl.pallas_call(
        body, name="lru_bwd", grid=(B, nc),
        in_specs=[cur, rec, prev_rec, cur, prev, cur, cws, vec, wsp, vec, wsp, vec, vec, rows(MLA_Q_RANK), rows(MLA_KV_RANK), rows(HEAD_LANES)],
        out_specs=[rows(P), cws, vec, wsp, vec, wsp, vec, vec],
        out_shape=[jax.ShapeDtypeStruct((B, Tp, P), F32), jax.ShapeDtypeStruct((CONV_WIDTH, W), F32), vs, ws, vs, ws, vs, vs],
        scratch_shapes=[pltpu.VMEM((8, W), F32), pltpu.VMEM((8, W), F32), pltpu.VMEM((8, W), F32), pltpu.VMEM((tc, W), F32)],
        compiler_params=pltpu.CompilerParams(dimension_semantics=("arbitrary", "arbitrary")),
    )(p, p, p, hseq, hseq, dy, cw, cb, wa, ba, wx, bx, sp, dpq, dpkv, dkpe)


_Q_BLOCK = 2 * LRU_WIDTH // MLA_Q_RANK
_KV_BLOCK = (2 * LRU_WIDTH + MLA_Q_RANK) // MLA_KV_RANK
_KPE_START = 2 * LRU_WIDTH + MLA_Q_RANK + MLA_KV_RANK


@jax.custom_vjp
def even_front(p, cw, cb, wa, ba, wx, bx, sp, gq, gkv):
    return _even_front_fwd(p, cw, cb, wa, ba, wx, bx, sp, gq, gkv)[0]


def _even_front_fwd(p, cw, cb, wa, ba, wx, bx, sp, gq, gkv):
    B, Tp, W = p.shape
    p2d = p.reshape(B * Tp, W)
    y, hseq = _lru_fwd_call(p, cw, cb, wa, ba, wx, bx, sp)
    qn = _rms_fwd_call(p2d, gq, "q_norm_fwd", _Q_BLOCK)
    kvn = _rms_fwd_call(p2d, gkv, "kv_norm_fwd", _KV_BLOCK)
    kpe = jnp.pad(p[:, :, _KPE_START:], ((0, 0), (0, 0), (MLA_NOPE, HEAD_LANES - MLA_NOPE - MLA_ROPE)))
    return (y, qn, kvn, kpe), (p, hseq, cw, cb, wa, ba, wx, bx, sp, gq, gkv)


def _even_front_bwd(res, cts):
    p, hseq, cw, cb, wa, ba, wx, bx, sp, gq, gkv = res
    dy, dqn, dkvn, dkpe = cts
    B, Tp, W = p.shape
    p2d = p.reshape(B * Tp, W)
    dpq, dgq = _rms_bwd_call(p2d, gq, dqn, "q_norm_bwd", _Q_BLOCK)
    dpkv, dgkv = _rms_bwd_call(p2d, gkv, dkvn, "kv_norm_bwd", _KV_BLOCK)
    dp, dcw, dcb, dwa, dba, dwx, dbx, dsp = _lru_bwd_call(p, hseq, dy, cw, cb, wa, ba, wx, bx, sp, dpq.reshape(B, Tp, -1),
                                                          dpkv.reshape(B, Tp, -1), dkpe)
    return dp, dcw, dcb, dwa, dba, dwx, dbx, dsp, dgq.reshape(gq.shape), dgkv.reshape(gkv.shape)


even_front.defvjp(_even_front_fwd, _even_front_bwd)


def _rope_tables(pos, half):
    inv = ROPE_BASE ** (-jnp.arange(half, dtype=F32) / half)
    ang = pos.astype(F32)[:, None] * inv[None, :]
    return jnp.cos(ang), jnp.sin(ang)


_NT = (((1,), (1,)), ((), ()))
_TN = (((0,), (0,)), ((), ()))
HEAD_LANES = 128
_MLA_SCALE = (MLA_NOPE + MLA_ROPE) ** -0.5
_LOG2E = math.log2(math.e)


Q_BLOCK = 512


def _query_blocks(Tp):
    first = Tp % Q_BLOCK or Q_BLOCK
    return [(0, first)] + [(r, r + Q_BLOCK) for r in range(first, Tp, Q_BLOCK)]


def _mask_diagonal(s, fill):
    R, L = s.shape
    row = lax.broadcasted_iota(jnp.int32, (R, R), 0)
    col = lax.broadcasted_iota(jnp.int32, (R, R), 1)
    last = jnp.where(col <= row, s[:, L - R:], fill)
    return last if L == R else jnp.concatenate([s[:, :L - R], last], axis=1)


def _mla_rope_tables(pos):
    half = MLA_ROPE // 2
    cos, sin = _rope_tables(pos, half)
    T = pos.shape[0]
    ones, zeros = jnp.ones((T, MLA_NOPE), F32), jnp.zeros((T, MLA_NOPE), F32)
    tail1, tail0 = jnp.ones((T, HEAD_LANES - MLA_NOPE - MLA_ROPE), F32), jnp.zeros((T, HEAD_LANES - MLA_NOPE - MLA_ROPE), F32)
    zh = jnp.zeros((T, half), F32)
    c = jnp.concatenate([ones, cos, cos, tail1], axis=1)
    s_up = jnp.concatenate([zeros, -sin, zh, tail0], axis=1)
    s_down = jnp.concatenate([zeros, zh, sin, tail0], axis=1)
    return c, s_up, s_down


def _rope_lanes(x, c, s_up, s_down):
    half = MLA_ROPE // 2
    return x * c + pltpu.roll(x, HEAD_LANES - half, 1) * s_up + pltpu.roll(x, half, 1) * s_down


def _unrope_lanes(d, c, s_up, s_down):
    half = MLA_ROPE // 2
    return d * c + pltpu.roll(d * s_up, half, 1) + pltpu.roll(d * s_down, HEAD_LANES - half, 1)


def _mla_operands(q_ref, kv_ref, kpe_ref, c, s_up, s_down):
    lane = lax.broadcasted_iota(jnp.int32, kv_ref.shape, 1)
    qr = (_rope_lanes(q_ref[...].astype(F32), c, s_up, s_down) * (_MLA_SCALE * _LOG2E)).astype(MXU_DTYPE)
    kr = jnp.where(lane < MLA_NOPE, kv_ref[...].astype(F32), _rope_lanes(kpe_ref[...], c, s_up, s_down)).astype(MXU_DTYPE)
    return qr, kr, lane


def _mla_specs(Tp):
    head = pl.BlockSpec((None, Tp, HEAD_LANES), lambda b, h: (b, 0, h))
    shared = pl.BlockSpec((None, Tp, HEAD_LANES), lambda b, h: (b, 0, 0))
    tab = pl.BlockSpec((Tp, HEAD_LANES), lambda b, h: (0, 0))
    lse = pl.BlockSpec((None, None, Tp, 1), lambda b, h: (b, h, 0, 0))
    return head, shared, tab, lse


def _attn_fwd_call(q, kv, kpe, tabs):
    B, Tp, _ = q.shape

    def body(q_ref, kv_ref, kpe_ref, c_ref, su_ref, sd_ref, o_ref, lse_ref, qr_ref, kr_ref):
        qr, kr, lane = _mla_operands(q_ref, kv_ref, kpe_ref, c_ref[...], su_ref[...], sd_ref[...])
        qr_ref[...] = qr
        kr_ref[...] = kr
        for r0, L in _query_blocks(Tp):
            blk = slice(r0, L)
            s = _mask_diagonal(lax.dot_general(qr_ref[blk, :], kr_ref[0:L, :], _NT, preferred_element_type=F32), NEG_INF)
            m = jnp.max(s, axis=-1, keepdims=True)
            p = jnp.exp2(s - m)
            l = jnp.sum(p, axis=-1, keepdims=True)
            o = jnp.dot(p.astype(MXU_DTYPE), kv_ref[0:L, :].astype(MXU_DTYPE), preferred_element_type=F32)
            o_ref[blk, :] = jnp.where(lane[blk, :] >= MLA_NOPE, o / l, 0.0)
            lse_ref[blk, :] = m + jnp.log2(l)

    head, shared, tab, lse = _mla_specs(Tp)
    return pl.pallas_call(
        body, name="mla_attn_fwd", grid=(B, MLA_HEADS), in_specs=[head, head, shared, tab, tab, tab], out_specs=[head, lse],
        out_shape=[jax.ShapeDtypeStruct((B, Tp, MLA_HEADS * HEAD_LANES), F32), jax.ShapeDtypeStruct((B, MLA_HEADS, Tp, 1), F32)],
        scratch_shapes=[pltpu.VMEM((Tp, HEAD_LANES), MXU_DTYPE), pltpu.VMEM((Tp, HEAD_LANES), MXU_DTYPE)],
        compiler_params=pltpu.CompilerParams(dimension_semantics=("parallel", "parallel")),
    )(q, kv, kpe, *tabs)


def _attn_bwd_call(q, kv, kpe, tabs, o, lse, do):
    B, Tp, _ = q.shape

    def body(q_ref, kv_ref, kpe_ref, c_ref, su_ref, sd_ref, o_ref, lse_ref, do_ref, dq_ref, dkv_ref, dkpe_ref,
             qr_ref, kr_ref, dqa_ref, dka_ref, dva_ref):
        c, s_up, s_down = c_ref[...], su_ref[...], sd_ref[...]
        qr, kr, lane = _mla_operands(q_ref, kv_ref, kpe_ref, c, s_up, s_down)
        qr_ref[...] = qr
        kr_ref[...] = kr
        dka_ref[...] = jnp.zeros_like(dka_ref)
        dva_ref[...] = jnp.zeros_like(dva_ref)
        for r0, L in _query_blocks(Tp):
            blk = slice(r0, L)
            qb = qr_ref[blk, :]
            do = jnp.where(lane[blk, :] >= MLA_NOPE, do_ref[blk, :], 0.0)
            delta = jnp.sum(do * o_ref[blk, :], axis=-1, keepdims=True)
            s = _mask_diagonal(lax.dot_general(qb, kr_ref[0:L, :], _NT, preferred_element_type=F32), NEG_INF)
            p = jnp.exp2(s - lse_ref[blk, :])
            dob = do.astype(MXU_DTYPE)
            dva_ref[0:L, :] += lax.dot_general(p.astype(MXU_DTYPE), dob, _TN, preferred_element_type=F32)
            dp = lax.dot_general(dob, kv_ref[0:L, :].astype(MXU_DTYPE), _NT, preferred_element_type=F32)
            ds = (p * (dp - delta)).astype(MXU_DTYPE)
            dqa_ref[blk, :] = jnp.dot(ds, kr_ref[0:L, :], preferred_element_type=F32)
            dka_ref[0:L, :] += lax.dot_general(ds, qb, _TN, preferred_element_type=F32)
        dq_ref[...] = _unrope_lanes(dqa_ref[...] * _MLA_SCALE, c, s_up, s_down).astype(dq_ref.dtype)
        dk = dka_ref[...] * (1.0 / _LOG2E)
        dkv_ref[...] = jnp.where(lane < MLA_NOPE, dk, dva_ref[...]).astype(dkv_ref.dtype)
        dkpe = jnp.where(lane >= MLA_NOPE, _unrope_lanes(dk, c, s_up, s_down), 0.0)

        @pl.when(pl.program_id(1) == 0)
        def _():
            dkpe_ref[...] = dkpe

        @pl.when(pl.program_id(1) > 0)
        def _():
            dkpe_ref[...] += dkpe

    head, shared, tab, lse_spec = _mla_specs(Tp)
    wide = jax.ShapeDtypeStruct((B, Tp, MLA_HEADS * HEAD_LANES), q.dtype)
    acc = pltpu.VMEM((Tp, HEAD_LANES), F32)
    return pl.pallas_call(
        body, name="mla_attn_bwd", grid=(B, MLA_HEADS),
        in_specs=[head, head, shared, tab, tab, tab, head, lse_spec, head], out_specs=[head, head, shared],
        out_shape=[wide, wide, jax.ShapeDtypeStruct((B, Tp, HEAD_LANES), F32)],
        scratch_shapes=[pltpu.VMEM((Tp, HEAD_LANES), MXU_DTYPE), pltpu.VMEM((Tp, HEAD_LANES), MXU_DTYPE), acc, acc, acc],
        compiler_params=pltpu.CompilerParams(dimension_semantics=("parallel", "arbitrary")),
    )(q, kv, kpe, *tabs, o, lse, do)


@jax.custom_vjp
def mla_attention(q, kv, kpe, tabs):
    return _attn_fwd_call(q, kv, kpe, tabs)[0]


def _mla_attention_fwd(q, kv, kpe, tabs):
    o, lse = _attn_fwd_call(q, kv, kpe, tabs)
    return o, (q, kv, kpe, tabs, o, lse)


def _mla_attention_bwd(res, do):
    q, kv, kpe, tabs, o, lse = res
    dq, dkv, dkpe = _attn_bwd_call(q, kv, kpe, tabs, o, lse, do)
    return dq, dkv, dkpe, None


mla_attention.defvjp(_mla_attention_fwd, _mla_attention_bwd)


def _rope_halves(x, cos, sin):
    half = x.shape[1] // 2
    x1, x2 = x[:, :half], x[:, half:]
    return jnp.concatenate([x1 * cos - x2 * sin, x1 * sin + x2 * cos], axis=1)


def _unrope_halves(d, cos, sin):
    half = d.shape[1] // 2
    d1, d2 = d[:, :half], d[:, half:]
    return jnp.concatenate([d1 * cos + d2 * sin, d2 * cos - d1 * sin], axis=1)


_RET_K_SCALE = RET_QK_DIM ** -0.5
_RET_Q_BLOCKS = RET_HEADS
_RET_V_BLOCK0 = 2 * RET_HEADS * RET_QK_DIM // RET_V_DIM
_RET_G_BLOCK0 = _RET_V_BLOCK0 + RET_HEADS


def _ret_specs(Tp):
    q = pl.BlockSpec((None, Tp, RET_QK_DIM), lambda b, h: (b, 0, h))
    k = pl.BlockSpec((None, Tp, RET_QK_DIM), lambda b, h: (b, 0, _RET_Q_BLOCKS + h))
    v = pl.BlockSpec((None, Tp, RET_V_DIM), lambda b, h: (b, 0, _RET_V_BLOCK0 + h))
    tab = pl.BlockSpec((Tp, RET_QK_DIM // 2), lambda b, h: (0, 0))
    lg = pl.BlockSpec((None, 1, 1), lambda b, h: (h, 0, 0))
    return q, k, v, tab, lg


def _ret_operands(q_ref, k_ref, cos, sin, lg):
    t = lax.broadcasted_iota(jnp.int32, (q_ref.shape[0], 1), 0).astype(F32)
    grow, shrink = jnp.exp(-lg * t), jnp.exp(lg * t)
    qs = (_rope_halves(q_ref[...].astype(F32), cos, sin) * shrink).astype(MXU_DTYPE)
    ks = (_rope_halves(k_ref[...].astype(F32), cos, sin) * (grow * _RET_K_SCALE)).astype(MXU_DTYPE)
    return qs, ks, shrink, grow * _RET_K_SCALE


def _ret_core_fwd_call(p, cos, sin, lg):
    B, Tp, _ = p.shape

    def body(q_ref, k_ref, v_ref, cos_ref, sin_ref, lg_ref, o_ref, qs_ref, ks_ref):
        qs_ref[...], ks_ref[...], _, _ = _ret_operands(q_ref, k_ref, cos_ref[...], sin_ref[...], lg_ref[...])
        for r0, L in _query_blocks(Tp):
            blk = slice(r0, L)
            s = _mask_diagonal(lax.dot_general(qs_ref[blk, :], ks_ref[0:L, :], _NT, preferred_element_type=F32), 0.0)
            o_ref[blk, :] = jnp.dot(s.astype(MXU_DTYPE), v_ref[0:L, :].astype(MXU_DTYPE), preferred_element_type=F32)

    q, k, v, tab, lgs = _ret_specs(Tp)
    return pl.pallas_call(
        body, name="retention_fwd", grid=(B, RET_HEADS), in_specs=[q, k, v, tab, tab, lgs],
        out_specs=pl.BlockSpec((None, Tp, RET_V_DIM), lambda b, h: (b, 0, h)),
        out_shape=jax.ShapeDtypeStruct((B, Tp, RET_HEADS * RET_V_DIM), F32),
        scratch_shapes=[pltpu.VMEM((Tp, RET_QK_DIM), MXU_DTYPE), pltpu.VMEM((Tp, RET_QK_DIM), MXU_DTYPE)],
        compiler_params=pltpu.CompilerParams(dimension_semantics=("parallel", "parallel")),
    )(p, p, p, cos, sin, lg)


def _ret_core_bwd_call(p, do, cos, sin, lg):
    B, Tp, _ = p.shape

    def body(q_ref, k_ref, v_ref, do_ref, cos_ref, sin_ref, lg_ref, dq_ref, dk_ref, dv_ref, qs_ref, ks_ref, dqa_ref, dka_ref, dva_ref):
        cos_, sin_ = cos_ref[...], sin_ref[...]
        qs_ref[...], ks_ref[...], q_scale, k_scale = _ret_operands(q_ref, k_ref, cos_, sin_, lg_ref[...])
        dka_ref[...] = jnp.zeros_like(dka_ref)
        dva_ref[...] = jnp.zeros_like(dva_ref)
        for r0, L in _query_blocks(Tp):
            blk = slice(r0, L)
            qb = qs_ref[blk, :]
            dob = do_ref[blk, :].astype(MXU_DTYPE)
            s = _mask_diagonal(lax.dot_general(qb, ks_ref[0:L, :], _NT, preferred_element_type=F32), 0.0).astype(MXU_DTYPE)
            dva_ref[0:L, :] += lax.dot_general(s, dob, _TN, preferred_element_type=F32)
            ds = _mask_diagonal(lax.dot_general(dob, v_ref[0:L, :].astype(MXU_DTYPE), _NT, preferred_element_type=F32), 0.0).astype(MXU_DTYPE)
            dqa_ref[blk, :] = jnp.dot(ds, ks_ref[0:L, :], preferred_element_type=F32)
            dka_ref[0:L, :] += lax.dot_general(ds, qb, _TN, preferred_element_type=F32)
        dq_ref[...] = _unrope_halves(dqa_ref[...] * q_scale, cos_, sin_).astype(dq_ref.dtype)
        dk_ref[...] = _unrope_halves(dka_ref[...] * k_scale, cos_, sin_).astype(dk_ref.dtype)
        dv_ref[...] = dva_ref[...].astype(dv_ref.dtype)

    q, k, v, tab, lgs = _ret_specs(Tp)
    qk_out = pl.BlockSpec((None, Tp, RET_QK_DIM), lambda b, h: (b, 0, h))
    v_out = pl.BlockSpec((None, Tp, RET_V_DIM), lambda b, h: (b, 0, h))
    return pl.pallas_call(
        body, name="retention_bwd", grid=(B, RET_HEADS), in_specs=[q, k, v, v_out, tab, tab, lgs],
        out_specs=[qk_out, qk_out, v_out],
        out_shape=[jax.ShapeDtypeStruct((B, Tp, RET_HEADS * RET_QK_DIM), p.dtype), jax.ShapeDtypeStruct((B, Tp, RET_HEADS * RET_QK_DIM), p.dtype),
                   jax.ShapeDtypeStruct((B, Tp, RET_HEADS * RET_V_DIM), p.dtype)],
        scratch_shapes=[pltpu.VMEM((Tp, RET_QK_DIM), MXU_DTYPE), pltpu.VMEM((Tp, RET_QK_DIM), MXU_DTYPE),
                        pltpu.VMEM((Tp, RET_QK_DIM), F32), pltpu.VMEM((Tp, RET_QK_DIM), F32), pltpu.VMEM((Tp, RET_V_DIM), F32)],
        compiler_params=pltpu.CompilerParams(dimension_semantics=("parallel", "parallel")),
    )(p, p, p, do, cos, sin, lg)


def _ret_gate_specs(M):
    tm = _pick(M, 1088, 8)
    head = pl.BlockSpec((tm, RET_V_DIM), lambda i, h: (i, h))
    gate = pl.BlockSpec((tm, RET_V_DIM), lambda i, h: (i, _RET_G_BLOCK0 + h))
    return tm, head, gate


def _ret_gate_fwd_call(o, p2d):
    M = o.shape[0]
    tm, head, gate = _ret_gate_specs(M)

    def body(o_ref, g_ref, y_ref):
        ov = o_ref[...]
        gv = g_ref[...].astype(F32)
        rstd = lax.rsqrt(jnp.mean(ov * ov, axis=-1, keepdims=True) + EPS)
        y_ref[...] = (gv * _sigmoid(gv)) * (ov * rstd)

    return pl.pallas_call(
        body, name="retention_gate_fwd", grid=(M // tm, RET_HEADS), in_specs=[head, gate], out_specs=head,
        out_shape=jax.ShapeDtypeStruct(o.shape, F32),
        compiler_params=pltpu.CompilerParams(dimension_semantics=("parallel", "parallel")),
    )(o, p2d)


def _ret_gate_bwd_call(o, p2d, dy):
    M = o.shape[0]
    tm, head, gate = _ret_gate_specs(M)

    def body(o_ref, g_ref, dy_ref, do_ref, dg_ref):
        ov = o_ref[...]
        gv = g_ref[...].astype(F32)
        dy = dy_ref[...]
        rstd = lax.rsqrt(jnp.mean(ov * ov, axis=-1, keepdims=True) + EPS)
        on = ov * rstd
        sg = _sigmoid(gv)
        dg_ref[...] = (dy * on * (sg * (1.0 + gv * (1.0 - sg)))).astype(dg_ref.dtype)
        don = dy * (gv * sg)
        do_ref[...] = (rstd * (don - on * jnp.mean(don * on, axis=-1, keepdims=True))).astype(do_ref.dtype)

    shp = jax.ShapeDtypeStruct(o.shape, p2d.dtype)
    return pl.pallas_call(
        body, name="retention_gate_bwd", grid=(M // tm, RET_HEADS), in_specs=[head, gate, head], out_specs=[head, head],
        out_shape=[shp, shp],
        compiler_params=pltpu.CompilerParams(dimension_semantics=("parallel", "parallel")),
    )(o, p2d, dy)


def _log_gamma():
    return jnp.log(1.0 - 2.0 ** (-5.0 - jnp.arange(RET_HEADS, dtype=F32))).reshape(RET_HEADS, 1, 1)


@functools.partial(jax.custom_vjp, nondiff_argnums=(9,))
def retention_block(h, w_in, w_out, w_in_grad_slot, w_out_grad_slot, g, b, cos, sin, dims):
    return _retention_block_fwd(h, w_in, w_out, w_in_grad_slot, w_out_grad_slot, g, b, cos, sin, dims)[0]


def _retention_block_fwd(h, w_in, w_out, w_in_grad_slot, w_out_grad_slot, g, b, cos, sin, dims):
    B, Tp = dims
    p = _mm_nn(h, w_in, False, "od_w_in_fwd", out_dtype=MXU_DTYPE)
    o = _ret_core_fwd_call(p.reshape(B, Tp, -1), cos, sin, _log_gamma())
    y = _ret_gate_fwd_call(o.reshape(B * Tp, -1), p)
    out, z = _mm_nn(y, w_out, False, "od_w_out_norm_fwd", norm=(h, g, b))
    return out, (h, p, o, y, z, w_in, w_out, g, cos, sin, jnp.zeros((), w_in_grad_slot.dtype))


def _retention_block_bwd(dims, res, dout):
    B, Tp = dims
    h, p, o, y, z, w_in, w_out, g, cos, sin, slot_like = res
    dz, dg, db = _ln_bwd_call(z, g, dout, "od_w_out_norm_bwd")
    dy = _mm_nt(dz, w_out, None, "od_w_out_dx")
    dw_out = _mm_tn(y, dz, False, "od_w_out_dw", 1, slot_like.dtype)
    do, dgate = _ret_gate_bwd_call(o.reshape(B * Tp, -1), p, dy)
    dq, dk, dv = _ret_core_bwd_call(p.reshape(B, Tp, -1), do.reshape(B, Tp, -1), cos, sin, _log_gamma())
    dp = jnp.concatenate([dq.reshape(B * Tp, -1), dk.reshape(B * Tp, -1), dv.reshape(B * Tp, -1), dgate], axis=-1)
    dh = _mm_nt(dp, w_in, None, "od_w_in_dx", plus=dz)
    dw_in = _mm_tn(h, dp, False, "od_w_in_dw", N_CHIPS, slot_like.dtype)
    return dh, None, None, dw_in, dw_out, dg.reshape(g.shape), db.reshape(g.shape), None, None


retention_block.defvjp(_retention_block_fwd, _retention_block_bwd)


def _heads_to_lanes(w):
    K = w.shape[0]
    w = w.reshape(K, MLA_HEADS, MLA_NOPE + MLA_ROPE)
    return jnp.pad(w, ((0, 0), (0, 0), (0, HEAD_LANES - MLA_NOPE - MLA_ROPE))).reshape(K, MLA_HEADS * HEAD_LANES)


def _out_rows_to_lanes(w):
    N = w.shape[1]
    att = w[LRU_WIDTH:].reshape(MLA_HEADS, MLA_V, N)
    att = jnp.pad(att, ((0, 0), (HEAD_LANES - MLA_V, 0), (0, 0))).reshape(MLA_HEADS * HEAD_LANES, N)
    return jnp.concatenate([w[:LRU_WIDTH], att], axis=0)


def _seq_dims(x):
    B, S, D = x.shape
    T = S + N_META
    Tp = _round_up(T, SEQ_BLOCK)
    return B, S, T, Tp


def _mixer0(diff, w, token):
    x = diff["x"]
    B, S, T, Tp = _seq_dims(x)
    D = x.shape[-1]
    M = B * Tp
    pos = jnp.arange(Tp, dtype=jnp.int32)

    def mm(a, name, act=False, out_dtype=F32, layout=lambda m: m, col_shards=1):
        return matmul(a, layout(w[name]), layout(diff[name]), act, name, out_dtype, col_shards)

    meta = jnp.broadcast_to(diff["meta_tokens"][None], (B, N_META, D))
    h = jnp.concatenate([meta, x + token, jnp.zeros((B, Tp - T, D), F32)], axis=1).reshape(M, D)
    p = mm(h, "ev_w_in")
    sp = jax.nn.softplus(-diff["ev_lru_lambda"]).reshape(1, LRU_WIDTH)
    y_rec, qn, kvn, kpe = even_front(
        p.reshape(B, Tp, -1), diff["ev_conv_w"].reshape(CONV_WIDTH, LRU_WIDTH), diff["ev_conv_b"].reshape(1, LRU_WIDTH),
        diff["ev_w_rg_a"].reshape(LRU_HEADS, LRU_HEAD_DIM, LRU_HEAD_DIM), diff["ev_b_rg_a"].reshape(1, LRU_WIDTH),
        diff["ev_w_rg_x"].reshape(LRU_HEADS, LRU_HEAD_DIM, LRU_HEAD_DIM), diff["ev_b_rg_x"].reshape(1, LRU_WIDTH),
        sp, diff["ev_q_norm_g"].reshape(-1), diff["ev_kv_norm_g"].reshape(-1))
    y_rec = y_rec.reshape(M, LRU_WIDTH)
    q = mm(qn, "ev_w_uq", out_dtype=MXU_DTYPE, layout=_heads_to_lanes).reshape(B, Tp, -1)
    kv = mm(kvn, "ev_w_ukv", out_dtype=MXU_DTYPE).reshape(B, Tp, -1)
    y_att = mla_attention(q, kv, kpe, _mla_rope_tables(pos)).reshape(M, -1)
    return out_block(h, jnp.concatenate([y_rec, y_att], axis=-1), _out_rows_to_lanes(w["ev_w_out"]), _out_rows_to_lanes(diff["ev_w_out"]),
                     diff["ln_mix_g"], diff["ln_mix_b"], "ev_w_out")


def _mlp0(diff, h, w):
    return mlp_block(h, w["mlp_w1_0"], w["mlp_w2_0"], diff["mlp_w1_0"], diff["mlp_w2_0"], diff["ln_mlp_g"], diff["ln_mlp_b"], "mlp0")


def _layer1_loss(diff, h, w, tgt):
    B, S, T, Tp = _seq_dims(tgt)
    D = tgt.shape[-1]
    pos = jnp.arange(Tp, dtype=jnp.int32)

    cos, sin = _rope_tables(pos, RET_QK_DIM // 2)
    h = retention_block(h, w["od_w_in"], w["od_w_out"], diff["od_w_in"], diff["od_w_out"], diff["ln_mix_g"], diff["ln_mix_b"], cos, sin, (B, Tp))
    h = mlp_block(h, w["mlp_w1_1"], w["mlp_w2_1"], diff["mlp_w1_1"], diff["mlp_w2_1"], diff["ln_mlp_g"], diff["ln_mlp_b"], "mlp1")
    return loss_head(h.reshape(B, Tp, D), jnp.pad(tgt, ((0, 0), (N_META, Tp - T), (0, 0))), S)


_HBM = pl.BlockSpec(memory_space=pltpu.HBM)


def _place():
    return lax.axis_index("x"), lax.axis_index("y"), lax.axis_index("c")


def _other_chips(x, y):
    return [(1 - x, y), (x, 1 - y), (1 - x, 1 - y)]


def _chunks(rows, sublanes, most):
    for q in range(most, 0, -1):
        if rows % (q * sublanes) == 0:
            return q
    return 1


def _sublanes(dtype):
    return 8 * 4 // jnp.dtype(dtype).itemsize


def _gather_pieces(bufs):
    plan, first = [], []
    for b in bufs:
        Rh = b.shape[0] // 2
        Q = _chunks(Rh, _sublanes(b.dtype), 4) if Rh * b.shape[1] * b.dtype.itemsize > (1 << 20) else 1
        first.append(3 * sum(q for _, q, _ in plan))
        plan.append((Rh, Q, Rh // Q))
    return plan, first, 3 * sum(q for _, q, _ in plan)


def _allgather_chips(bufs, name):
    n = len(bufs)
    plan, first, n_sems = _gather_pieces(bufs)

    def body(*refs):
        x_refs, out_refs, (send_sems, recv_sems) = refs[:n], refs[n:2 * n], refs[2 * n:]
        x, y, c = _place()
        sibling = (x, y, 1 - c)
        chips = _other_chips(x, y)

        def copy(k, src, dst, to):
            return pltpu.make_async_remote_copy(src_ref=src, dst_ref=dst, send_sem=send_sems.at[k], recv_sem=recv_sems.at[k],
                                                device_id=to, device_id_type=MESH)

        def piece(i, cx, cy, hc, q):
            Rh, _, ch = plan[i]
            return out_refs[i].at[2 * cx + cy, pl.ds(hc * Rh + q * ch, ch), :]

        slots = [(i, q, j) for i in range(n) for q in range(plan[i][1]) for j in range(3)]
        sem = {(i, q, j): first[i] + 3 * q + j for i, q, j in slots}
        sent = [copy(sem[i, q, j], x_refs[i].at[pl.ds(c * plan[i][0] + q * plan[i][2], plan[i][2]), :], piece(i, x, y, c, q), (*chips[j], c))
                for i, q, j in slots]
        for cp in sent:
            cp.start()
        passed = []
        for i, q, j in slots:
            landed = piece(i, *chips[j], c, q)
            copy(sem[i, q, j], landed, landed, sibling).wait_recv()
            fwd = copy(n_sems + sem[i, q, j], landed, landed, sibling)
            fwd.start()
            passed.append(fwd)
        for i, q, j in slots:
            theirs = piece(i, *chips[j], 1 - c, q)
            copy(n_sems + sem[i, q, j], theirs, theirs, sibling).wait_recv()
        for cp in sent + passed:
            cp.wait_send()

    return pl.pallas_call(
        body, name=name, in_specs=[_HBM] * n, out_specs=[_HBM] * n,
        out_shape=[jax.ShapeDtypeStruct((N_CHIPS,) + b.shape, b.dtype) for b in bufs],
        scratch_shapes=[pltpu.SemaphoreType.DMA((2 * n_sems,)), pltpu.SemaphoreType.DMA((2 * n_sems,))],
    )(*bufs)


def _with_own(gathered, own):
    my = 2 * lax.axis_index("x") + lax.axis_index("y")
    return lax.dynamic_update_slice(gathered, own[None], (my, 0, 0))


def _sibling_exchange(ps, name):
    n = len(ps)

    def body(*refs):
        p_refs, out_refs, (send_sems, recv_sems) = refs[:n], refs[n:2 * n], refs[2 * n:]
        x, y, c = _place()
        copies = [pltpu.make_async_remote_copy(src_ref=p_ref.at[j, 1 - c], dst_ref=out_ref.at[j], send_sem=send_sems.at[N_CHIPS * i + j],
                                               recv_sem=recv_sems.at[N_CHIPS * i + j], device_id=(x, y, 1 - c), device_id_type=MESH)
                  for i, (p_ref, out_ref) in enumerate(zip(p_refs, out_refs)) for j in range(N_CHIPS)]
        for cp in copies:
            cp.start()
        for cp in copies:
            cp.wait()

    return pl.pallas_call(
        body, name=name, in_specs=[_HBM] * n, out_specs=[_HBM] * n,
        out_shape=[jax.ShapeDtypeStruct((N_CHIPS,) + p.shape[2:], p.dtype) for p in ps],
        scratch_shapes=[pltpu.SemaphoreType.DMA((N_CHIPS * n,)), pltpu.SemaphoreType.DMA((N_CHIPS * n,))],
    )(*ps)


def _chip_scatter(ss, name):
    n = len(ss)

    def body(*refs):
        s_refs, t_refs, (send_sems, recv_sems) = refs[:n], refs[n:2 * n], refs[2 * n:]
        x, y, c = _place()
        copies = [pltpu.make_async_remote_copy(src_ref=s_ref.at[j + 1], dst_ref=t_ref.at[j], send_sem=send_sems.at[3 * i + j],
                                               recv_sem=recv_sems.at[3 * i + j], device_id=(cx, cy, c), device_id_type=MESH)
                  for i, (s_ref, t_ref) in enumerate(zip(s_refs, t_refs)) for j, (cx, cy) in enumerate(_other_chips(x, y))]
        for cp in copies:
            cp.start()
        for cp in copies:
            cp.wait()

    return pl.pallas_call(
        body, name=name, in_specs=[_HBM] * n, out_specs=[_HBM] * n,
        out_shape=[jax.ShapeDtypeStruct((3,) + s.shape[1:], s.dtype) for s in ss],
        scratch_shapes=[pltpu.SemaphoreType.DMA((3 * n,)), pltpu.SemaphoreType.DMA((3 * n,))],
    )(*ss)


def _sibling_gather(fs, name):
    n = len(fs)

    def body(*refs):
        out_refs, (send_sems, recv_sems) = refs[n:2 * n], refs[2 * n:]
        x, y, c = _place()
        copies = [pltpu.make_async_remote_copy(src_ref=out_ref.at[c], dst_ref=out_ref.at[c], send_sem=send_sems.at[i], recv_sem=recv_sems.at[i],
                                               device_id=(x, y, 1 - c), device_id_type=MESH) for i, out_ref in enumerate(out_refs)]
        for cp in copies:
            cp.start()
        for cp in copies:
            cp.wait()

    return pl.pallas_call(
        body, name=name, in_specs=[_HBM] * n, out_specs=[_HBM] * n,
        out_shape=[jax.ShapeDtypeStruct(f.shape, f.dtype) for f in fs], input_output_aliases={i: i for i in range(n)},
        scratch_shapes=[pltpu.SemaphoreType.DMA((n,)), pltpu.SemaphoreType.DMA((n,))],
    )(*fs)


def _axis_scalar(name):
    return lax.axis_index(name).astype(jnp.int32).reshape(1)


def _add_own_half(p, got, out_dtype, name):
    n, _, R, C = p.shape
    tr = _pick(R, 512, 16)

    def body(x_ref, y_ref, c_ref, p_ref, g_ref, o_ref):
        o_ref[...] = (p_ref[...] + g_ref[...]).astype(out_dtype)

    def chip(r, x_ref, y_ref):
        return 2 * (x_ref[0] ^ (r & 1)) + (y_ref[0] ^ (r >> 1))

    grid_spec = pltpu.PrefetchScalarGridSpec(
        num_scalar_prefetch=3, grid=(n, R // tr),
        in_specs=[pl.BlockSpec((None, None, tr, C), lambda r, i, x_ref, y_ref, c_ref: (chip(r, x_ref, y_ref), c_ref[0], i, 0)),
                  pl.BlockSpec((None, tr, C), lambda r, i, x_ref, y_ref, c_ref: (chip(r, x_ref, y_ref), i, 0))],
        out_specs=pl.BlockSpec((None, tr, C), lambda r, i, x_ref, y_ref, c_ref: (r, i, 0)))
    return pl.pallas_call(body, name=name, grid_spec=grid_spec, out_shape=jax.ShapeDtypeStruct((n, R, C), out_dtype),
                          compiler_params=pltpu.CompilerParams(dimension_semantics=("parallel", "parallel")))(
        _axis_scalar("x"), _axis_scalar("y"), _axis_scalar("c"), p, got)


def _sum_partials(s, t, name):
    _, R, C = s.shape
    tr = _pick(R, 512, 16)

    def body(c_ref, s_ref, t_ref, o_ref):
        acc = s_ref[...].astype(F32)
        for j in range(3):
            acc = acc + t_ref[j].astype(F32)
        o_ref[...] = acc

    grid_spec = pltpu.PrefetchScalarGridSpec(
        num_scalar_prefetch=1, grid=(R // tr,),
        in_specs=[pl.BlockSpec((None, tr, C), lambda i, c_ref: (0, i, 0)), pl.BlockSpec((3, tr, C), lambda i, c_ref: (0, i, 0))],
        out_specs=pl.BlockSpec((None, tr, C), lambda i, c_ref: (c_ref[0], i, 0)))
    return pl.pallas_call(body, name=name, grid_spec=grid_spec, out_shape=jax.ShapeDtypeStruct((2, R, C), F32),
                          compiler_params=pltpu.CompilerParams(dimension_semantics=("parallel",)))(_axis_scalar("c"), s, t)


def _sibling_reduce(ps, wire_dtypes, tag):
    got = _sibling_exchange(ps, "grad_sibling_exchange_" + tag)
    return [_add_own_half(p, g, dt, "grad_sibling_add_%s%d" % (tag, i)) for i, (p, g, dt) in enumerate(zip(ps, got, wire_dtypes))]


_SEM = pl.BlockSpec(memory_space=pltpu.SEMAPHORE)
_ANY = pl.BlockSpec(memory_space=pl.ANY)
_EFFECT = pltpu.SideEffectType.DATAFLOW_SIDE_EFFECTING


def _in_hbm(a):
    return pltpu.with_memory_space_constraint(a, pltpu.HBM)


def _half_copies(x_refs, land_refs, send_sems, recv_sems, arriving):
    x, y, c = _place()
    copies = []
    for i, (x_ref, land_ref) in enumerate(zip(x_refs, land_refs)):
        Rh = x_ref.shape[0] // 2
        rows = pl.ds(c * Rh, Rh)
        for j, (cx, cy) in enumerate(_other_chips(x, y)):
            copies.append(pltpu.make_async_remote_copy(
                src_ref=x_ref.at[rows, :], dst_ref=land_ref.at[2 * cx + cy if arriving else 2 * x + y, rows, :],
                send_sem=send_sems.at[3 * i + j], recv_sem=recv_sems.at[3 * i + j], device_id=(cx, cy, c), device_id_type=MESH))
    return copies


def _allgather_start(bufs, name):
    n = len(bufs)

    def body(*refs):
        x_refs, land_refs, (send_sems, recv_sems), token = refs[:n], refs[n:2 * n], refs[2 * n:2 * n + 2], refs[-1]
        for cp in _half_copies(x_refs, land_refs, send_sems, recv_sems, False):
            cp.start()
        token[...] = jnp.zeros_like(token)

    lands = [lax.empty((N_CHIPS,) + b.shape, b.dtype) for b in bufs]
    out = pl.pallas_call(
        body, name=name,
        out_shape=(pltpu.SemaphoreType.DMA((3 * n,)), pltpu.SemaphoreType.DMA((3 * n,)), *[pltpu.HBM(a.shape, a.dtype) for a in bufs + lands],
                   jax.ShapeDtypeStruct((8, 128), F32)),
        in_specs=[_HBM] * (2 * n), out_specs=(_SEM, _SEM, *[_HBM] * (2 * n), pl.BlockSpec(memory_space=pltpu.VMEM)),
        input_output_aliases={i: 2 + i for i in range(2 * n)}, compiler_params=pltpu.CompilerParams(has_side_effects=_EFFECT),
    )(*[_in_hbm(a) for a in bufs + lands])
    return (out[0], out[1], list(out[2:2 + n]), list(out[2 + n:2 + 2 * n])), out[-1][0, 0]


def _allgather_wait(pending, after, name):
    send_sems, recv_sems, bufs, lands = pending
    n = len(bufs)

    def body(*refs):
        x_refs, land_refs, send_sems, recv_sems = refs[:n], refs[n:2 * n], refs[2 * n], refs[2 * n + 1]
        for cp in _half_copies(x_refs, land_refs, send_sems, recv_sems, False):
            cp.wait_send()
        for cp in _half_copies(x_refs, land_refs, send_sems, recv_sems, True):
            cp.wait_recv()

    out = pl.pallas_call(
        body, name=name, out_shape=tuple(pltpu.HBM(a.shape, a.dtype) for a in bufs + lands),
        in_specs=[_HBM] * (2 * n) + [_SEM, _SEM, _ANY], out_specs=tuple([_HBM] * (2 * n)), input_output_aliases={i: i for i in range(2 * n)},
        compiler_params=pltpu.CompilerParams(has_side_effects=_EFFECT),
    )(*bufs, *lands, send_sems, recv_sems, after)
    return list(out[n:])


def _sibling_forward(lands, name):
    n = len(lands)
    plan, first, n_sems = _gather_pieces([jax.ShapeDtypeStruct(l.shape[1:], l.dtype) for l in lands])

    def body(*refs):
        out_refs, (send_sems, recv_sems) = refs[n:2 * n], refs[2 * n:]
        x, y, c = _place()

        def copies(hc):
            return [pltpu.make_async_remote_copy(
                        src_ref=out_refs[i].at[2 * cx + cy, pl.ds(hc * plan[i][0] + q * plan[i][2], plan[i][2]), :],
                        dst_ref=out_refs[i].at[2 * cx + cy, pl.ds(hc * plan[i][0] + q * plan[i][2], plan[i][2]), :],
                        send_sem=send_sems.at[first[i] + 3 * q + j], recv_sem=recv_sems.at[first[i] + 3 * q + j],
                        device_id=(x, y, 1 - c), device_id_type=MESH)
                    for i in range(n) for q in range(plan[i][1]) for j, (cx, cy) in enumerate(_other_chips(x, y))]

        mine = copies(c)
        for cp in mine:
            cp.start()
        for cp in mine:
            cp.wait_send()
        for cp in copies(1 - c):
            cp.wait_recv()

    return pl.pallas_call(
        body, name=name, in_specs=[_HBM] * n, out_specs=[_HBM] * n, out_shape=[jax.ShapeDtypeStruct(l.shape, l.dtype) for l in lands],
        input_output_aliases={i: i for i in range(n)},
        scratch_shapes=[pltpu.SemaphoreType.DMA((n_sems,)), pltpu.SemaphoreType.DMA((n_sems,))],
    )(*lands)


N_PEERS = 7


def _direct_copies(p_refs, t_refs, send_sems, recv_sems):
    x, y, c = _place()
    copies = []
    for i, (p_ref, t_ref) in enumerate(zip(p_refs, t_refs)):
        for f in range(1, N_PEERS + 1):
            px, py, pc = x ^ (f >> 2), y ^ ((f >> 1) & 1), c ^ (f & 1)
            copies.append(pltpu.make_async_remote_copy(
                src_ref=p_ref.at[2 * px + py, pc], dst_ref=t_ref.at[f - 1], send_sem=send_sems.at[N_PEERS * i + f - 1],
                recv_sem=recv_sems.at[N_PEERS * i + f - 1], device_id=(px, py, pc), device_id_type=MESH))
    return copies


def _direct_scatter_start(ps, name):
    n = len(ps)

    def body(*refs):
        p_refs, t_refs, (send_sems, recv_sems), token = refs[:n], refs[n:2 * n], refs[2 * n:2 * n + 2], refs[-1]
        for cp in _direct_copies(p_refs, t_refs, send_sems, recv_sems):
            cp.start()
        token[...] = jnp.zeros_like(token)

    lands = [lax.empty((N_PEERS,) + p.shape[2:], p.dtype) for p in ps]
    out = pl.pallas_call(
        body, name=name,
        out_shape=(pltpu.SemaphoreType.DMA((N_PEERS * n,)), pltpu.SemaphoreType.DMA((N_PEERS * n,)),
                   *[pltpu.HBM(a.shape, a.dtype) for a in ps + lands], jax.ShapeDtypeStruct((8, 128), F32)),
        in_specs=[_HBM] * (2 * n), out_specs=(_SEM, _SEM, *[_HBM] * (2 * n), pl.BlockSpec(memory_space=pltpu.VMEM)),
        input_output_aliases={i: 2 + i for i in range(2 * n)}, compiler_params=pltpu.CompilerParams(has_side_effects=_EFFECT),
    )(*[_in_hbm(a) for a in ps + lands])
    return (out[0], out[1], list(out[2:2 + n]), list(out[2 + n:2 + 2 * n])), out[-1][0, 0]


def _direct_scatter_wait(pending, after, name):
    send_sems, recv_sems, ps, lands = pending
    n = len(ps)

    def body(*refs):
        p_refs, t_refs, send_sems, recv_sems = refs[:n], refs[n:2 * n], refs[2 * n], refs[2 * n + 1]
        for cp in _direct_copies(p_refs, t_refs, send_sems, recv_sems):
            cp.wait_send()
            cp.wait_recv()

    out = pl.pallas_call(
        body, name=name, out_shape=tuple(pltpu.HBM(a.shape, a.dtype) for a in ps + lands),
        in_specs=[_HBM] * (2 * n) + [_SEM, _SEM, _ANY], out_specs=tuple([_HBM] * (2 * n)),
        input_output_aliases={i: i for i in range(2 * n)}, compiler_params=pltpu.CompilerParams(has_side_effects=_EFFECT),
    )(*ps, *lands, send_sems, recv_sems, after)
    return list(out[:n]), list(out[n:])


def _sum_direct(p, t, name):
    _, _, R, C = p.shape
    tr = _pick(R, 512, 16)

    def body(x_ref, y_ref, c_ref, p_ref, t_ref, o_ref):
        acc = p_ref[...].astype(F32)
        for f in range(N_PEERS):
            acc = acc + t_ref[f].astype(F32)
        o_ref[...] = acc

    grid_spec = pltpu.PrefetchScalarGridSpec(
        num_scalar_prefetch=3, grid=(R // tr,),
        in_specs=[pl.BlockSpec((None, None, tr, C), lambda i, x_ref, y_ref, c_ref: (2 * x_ref[0] + y_ref[0], c_ref[0], i, 0)),
                  pl.BlockSpec((N_PEERS, tr, C), lambda i, x_ref, y_ref, c_ref: (0, i, 0))],
        out_specs=pl.BlockSpec((None, tr, C), lambda i, x_ref, y_ref, c_ref: (c_ref[0], i, 0)))
    return pl.pallas_call(body, name=name, grid_spec=grid_spec, out_shape=jax.ShapeDtypeStruct((2, R, C), F32),
                          compiler_params=pltpu.CompilerParams(dimension_semantics=("parallel",)))(
        _axis_scalar("x"), _axis_scalar("y"), _axis_scalar("c"), p, t)


def _adamw(w, g, m, v, name):
    R, C = w.shape
    tr = _pick(R, 256, 8)

    def body(w_ref, g_ref, m_ref, v_ref, d_ref, nm_ref, nv_ref):
        g_ = g_ref[...]
        m_ = ADAM_B1 * m_ref[...] + (1.0 - ADAM_B1) * g_
        v_ = ADAM_B2 * v_ref[...] + (1.0 - ADAM_B2) * (g_ * g_)
        m_hat = m_ / (1.0 - ADAM_B1 ** ADAM_STEP)
        v_hat = v_ / (1.0 - ADAM_B2 ** ADAM_STEP)
        d_ref[...] = -ADAM_LR * (m_hat / (jnp.sqrt(v_hat) + ADAM_EPS) + ADAM_WD * w_ref[...])
        nm_ref[...] = m_
        nv_ref[...] = v_

    row = pl.BlockSpec((tr, C), lambda i: (i, 0))
    shp = jax.ShapeDtypeStruct((R, C), F32)
    return pl.pallas_call(body, name=name, grid=(R // tr,), in_specs=[row] * 4, out_specs=[row] * 3, out_shape=[shp] * 3,
                          compiler_params=pltpu.CompilerParams(dimension_semantics=("parallel",)))(w, g, m, v)


BIG_SPECS = (("ev_w_in", 1024, 1440, 1), ("ev_w_uq", 256, 768, 1), ("ev_w_ukv", 128, 1024, 1), ("ev_w_out", 1024, 1024, 0),
             ("od_w_in", 1024, 6144, 1), ("od_w_out", 2048, 1024, 0), ("mlp_w1_0", 1024, 4096, 1), ("mlp_w1_1", 1024, 4096, 1),
             ("mlp_w2_0", 4096, 1024, 0), ("mlp_w2_1", 4096, 1024, 0))
BIG_PARAMS = (("ev_w_in", ("ev_w_in",)), ("ev_w_uq", ("ev_w_uq",)), ("ev_w_ukv", ("ev_w_ukv",)), ("ev_w_out", ("ev_w_out",)),
              ("od_w_in", ("od_w_in",)), ("od_w_out", ("od_w_out",)), ("mlp_w1", ("mlp_w1_0", "mlp_w1_1")),
              ("mlp_w2", ("mlp_w2_0", "mlp_w2_1")))
REPLICATED = ("ev_conv_b", "ev_w_rg_a", "ev_b_rg_a", "ev_w_rg_x", "ev_b_rg_x", "ev_lru_lambda", "ev_q_norm_g", "ev_kv_norm_g",
              "ln_mix_g", "ln_mix_b", "ln_mlp_g", "ln_mlp_b")
SMALL_SHARDED = ("meta_tokens", "ev_conv_w")
COL_SHARD_GRADS = ("od_w_in", "mlp_w1_0", "mlp_w1_1")
MATRIX_GROUPS = (("ev_w_in", "ev_w_uq", "ev_w_ukv", "ev_w_out"), ("mlp_w1_0", "mlp_w2_0"), ("od_w_in", "od_w_out", "mlp_w1_1", "mlp_w2_1"))
LAYER_NORMS = ("ln_mix_g", "ln_mix_b", "ln_mlp_g", "ln_mlp_b")
WEIGHT_NAMES = ("meta_tokens", "ev_w_in", "ev_conv_w", "ev_conv_b", "ev_w_rg_a", "ev_b_rg_a", "ev_w_rg_x", "ev_b_rg_x",
                "ev_lru_lambda", "ev_q_norm_g", "ev_w_uq", "ev_kv_norm_g", "ev_w_ukv", "ev_w_out", "od_w_in", "od_w_out",
                "ln_mix_g", "ln_mix_b", "mlp_w1", "mlp_w2", "ln_mlp_g", "ln_mlp_b")


def _to_rows(flat, row_align):
    n = flat.shape[-1]
    rows = _round_up(-(-n // PACK_COLS), row_align)
    pad = rows * PACK_COLS - n
    if pad:
        flat = jnp.pad(flat, [(0, 0)] * (flat.ndim - 1) + [(0, pad)])
    return flat.reshape(flat.shape[:-1] + (rows, PACK_COLS))


def _shard_shape(K, N, axis):
    return (K // N_CHIPS, N) if axis == 0 else (K, N // N_CHIPS)


def _gather_shards(stacked, K, N, axis):
    if axis == 0:
        return stacked.reshape(K, N)
    return stacked.transpose(1, 0, 2).reshape(K, N)


def _split_shards(full, K, N, axis):
    if axis == 0:
        return full.reshape(N_CHIPS, -1)
    return full.reshape(K, N_CHIPS, N // N_CHIPS).transpose(1, 0, 2).reshape(N_CHIPS, -1)


def kernel(x, meta_tokens, ev_w_in, ev_conv_w, ev_conv_b, ev_w_rg_a, ev_b_rg_a, ev_w_rg_x, ev_b_rg_x, ev_lru_lambda, ev_q_norm_g, ev_w_uq, ev_kv_norm_g, ev_w_ukv, ev_w_out, od_w_in, od_w_out, ln_mix_g, ln_mix_b, mlp_w1, mlp_w2, ln_mlp_g, ln_mlp_b, loss_target, m_meta_tokens, m_ev_w_in, m_ev_conv_w, m_ev_conv_b, m_ev_w_rg_a, m_ev_b_rg_a, m_ev_w_rg_x, m_ev_b_rg_x, m_ev_lru_lambda, m_ev_q_norm_g, m_ev_w_uq, m_ev_kv_norm_g, m_ev_w_ukv, m_ev_w_out, m_od_w_in, m_od_w_out, m_ln_mix_g, m_ln_mix_b, m_mlp_w1, m_mlp_w2, m_ln_mlp_g, m_ln_mlp_b, v_meta_tokens, v_ev_w_in, v_ev_conv_w, v_ev_conv_b, v_ev_w_rg_a, v_ev_b_rg_a, v_ev_w_rg_x, v_ev_b_rg_x, v_ev_lru_lambda, v_ev_q_norm_g, v_ev_w_uq, v_ev_kv_norm_g, v_ev_w_ukv, v_ev_w_out, v_od_w_in, v_od_w_out, v_ln_mix_g, v_ln_mix_b, v_mlp_w1, v_mlp_w2, v_ln_mlp_g, v_ln_mlp_b):
    given = dict(locals())
    local_big = {"ev_w_in": ev_w_in[0], "ev_w_uq": ev_w_uq[0], "ev_w_ukv": ev_w_ukv[0], "ev_w_out": ev_w_out[0],
                 "od_w_in": od_w_in[0], "od_w_out": od_w_out[0], "mlp_w1_0": mlp_w1[0], "mlp_w1_1": mlp_w1[1],
                 "mlp_w2_0": mlp_w2[0], "mlp_w2_1": mlp_w2[1]}

    specs = {spec[0]: spec for spec in BIG_SPECS}
    mixer0_m, mlp0_m, layer1_m = MATRIX_GROUPS

    def shards(names):
        return [local_big[n].astype(MXU_DTYPE) for n in names]

    def whole(stacked, n):
        _, K, N, ax = specs[n]
        return stacked if n in COL_SHARD_GRADS else _gather_shards(stacked, K, N, ax)

    def filled(gathered, own, names):
        return {n: whole(_with_own(g_, o_), n) for n, g_, o_ in zip(names, gathered, own)}

    own_a, own_b, own_c = shards(mixer0_m), shards(mlp0_m), shards(layer1_m)
    small = [meta_tokens, jnp.pad(ev_conv_w[0], ((0, 16 - CONV_WIDTH), (0, 0)))]
    gathered_a = _allgather_chips(own_a + small, "weight_allgather_mixer0")
    pending_b, token1 = _allgather_start(own_b, "weight_allgather_mlp0_start")
    pending_c, token2 = _allgather_start(own_c, "weight_allgather_layer1_start")
    meta_full = _gather_shards(_with_own(gathered_a[-2], small[0]), N_META, D_MODEL, 1)
    conv_full = _gather_shards(_with_own(gathered_a[-1], small[1])[:, :CONV_WIDTH], CONV_WIDTH, LRU_WIDTH, 1)

    def slots(names, dtype):
        return {n: jnp.zeros((N_CHIPS, specs[n][1], specs[n][2] // N_CHIPS) if n in COL_SHARD_GRADS else specs[n][1:3], dtype) for n in names}

    def norms(names, layer):
        return {n: given[n][layer] for n in names}

    def finish_gather(pending, own, after, names, tag):
        landed = _allgather_wait(pending, lax.stop_gradient(after), "weight_allgather_%s_wait" % tag)
        return filled(_sibling_forward(landed, "weight_allgather_%s_forward" % tag), own, names)

    diff_a = {**slots(mixer0_m, F32), **norms(("ln_mix_g", "ln_mix_b"), 0), **{n: given[n] for n in REPLICATED if n not in LAYER_NORMS},
              "x": x, "meta_tokens": meta_full, "ev_conv_w": conv_full}
    diff_b = {**slots(mlp0_m, MXU_DTYPE), **norms(("ln_mlp_g", "ln_mlp_b"), 0)}
    diff_c = {**slots(layer1_m, MXU_DTYPE), **norms(LAYER_NORMS, 1)}
    w_a = filled(gathered_a[:len(mixer0_m)], own_a, mixer0_m)
    h_a, back_a = jax.vjp(lambda d: _mixer0(d, w_a, token1 + token2), diff_a)
    w_b = finish_gather(pending_b, own_b, h_a, mlp0_m, "mlp0")
    h_b, back_b = jax.vjp(lambda d, hh: _mlp0(d, hh, w_b), diff_b, h_a)
    w_c = finish_gather(pending_c, own_c, h_b, layer1_m, "layer1")
    loss, back_c = jax.vjp(lambda d, hh: _layer1_loss(d, hh, w_c, loss_target), diff_c, h_b)
    loss = lax.psum(loss, ("x", "y", "c"))

    def blocks_of(grad, n):
        _, K, N, ax = specs[n]
        if n in COL_SHARD_GRADS:
            blocks = grad
        elif ax == 0:
            blocks = grad.reshape(N_CHIPS, K // N_CHIPS, N)
        else:
            blocks = grad.reshape(K, N_CHIPS, N // N_CHIPS).transpose(1, 0, 2)
        return blocks.reshape(N_CHIPS, 2, blocks.shape[1] // 2, blocks.shape[2])

    def start_reduce(grads_of, names, tag):
        return _direct_scatter_start([blocks_of(grads_of[n], n) for n in names], "grad_scatter_%s_start" % tag)

    g_c, dh = back_c(jnp.ones((), F32))
    flying_c, token = start_reduce(g_c, layer1_m, "layer1")
    g_b, dh = back_b(dh + token)
    flying_b, token = start_reduce(g_b, mlp0_m, "mlp0")
    (g_a,) = back_a(dh + token)
    ps_c, ts_c = _direct_scatter_wait(flying_c, g_a["x"], "grad_scatter_layer1_wait")
    ps_b, ts_b = _direct_scatter_wait(flying_b, g_a["x"], "grad_scatter_mlp0_wait")

    g = {**g_a, **g_b, **g_c}
    g.update({n: jnp.stack([(g_b if n in g_b else g_a)[n], g_c[n]]) for n in LAYER_NORMS})
    repl = jnp.concatenate([g[n].reshape(-1) for n in REPLICATED]).reshape(N_CHIPS, -1)
    small = [_split_shards(g["meta_tokens"], N_META, D_MODEL, 1), _split_shards(g["ev_conv_w"], CONV_WIDTH, LRU_WIDTH, 1), repl]
    small = [pc.reshape(N_CHIPS, 2, -1) for pc in small]
    n_small = sum(pc.shape[2] for pc in small)
    small.append(jnp.zeros((N_CHIPS, 2, _round_up(n_small, 32 * PACK_COLS) - n_small), F32))
    p_small = jnp.concatenate(small, axis=2).reshape(N_CHIPS, 2, -1, PACK_COLS)
    ss_a = _sibling_reduce([blocks_of(g_a[n], n) for n in mixer0_m] + [p_small], [MXU_DTYPE] * len(mixer0_m) + [F32], "mixer0_")
    ts_a = list(_chip_scatter(ss_a, "grad_chip_scatter_mixer0"))
    fs = [_sum_partials(s, t, "grad_chip_sum_mixer0_%d" % i) for i, (s, t) in enumerate(zip(ss_a, ts_a))]
    fs += [_sum_direct(p, t, "grad_sum_%d" % i) for i, (p, t) in enumerate(zip(ps_b + ps_c, ts_b + ts_c))]
    reduced = _sibling_gather(fs, "grad_sibling_gather")
    red_big = dict(zip(mixer0_m + ("small",) + mlp0_m + layer1_m, reduced))
    red_small = red_big.pop("small").reshape(2, -1)

    grads = {}
    for name, parts in BIG_PARAMS:
        grads[name] = jnp.stack([red_big[part].reshape(given[name].shape[1:]) for part in parts])

    def take(off, sz):
        return jnp.concatenate([red_small[0, off // 2:(off + sz) // 2], red_small[1, off // 2:(off + sz) // 2]])

    off = 0
    for name in SMALL_SHARDED:
        sz = given[name].size
        grads[name] = take(off, sz).reshape(given[name].shape)
        off += sz
    n_repl = repl.shape[1]
    own_repl = _to_rows(take(off, n_repl), 16)
    repl_all = _with_own(_allgather_chips([own_repl], "replicated_allgather")[0], own_repl).reshape(N_CHIPS, -1)[:, :n_repl].reshape(-1)
    off = 0
    for name in REPLICATED:
        sz = given[name].size
        grads[name] = repl_all[off:off + sz].reshape(given[name].shape)
        off += sz

    delta, new_m, new_v = {}, {}, {}
    for name, _ in BIG_PARAMS:
        shp = given[name].shape
        two_d = (-1, shp[-1])
        d, nm, nv = _adamw(given[name].reshape(two_d), grads[name].reshape(two_d), given["m_" + name].reshape(two_d),
                           given["v_" + name].reshape(two_d), "adamw_" + name)
        delta[name], new_m[name], new_v[name] = d.reshape(shp), nm.reshape(shp), nv.reshape(shp)
    smalls = SMALL_SHARDED + REPLICATED

    def pack_small(get):
        return _to_rows(jnp.concatenate([get(n).reshape(-1) for n in smalls]), 8)

    outs = _adamw(pack_small(lambda n: given[n]), pack_small(lambda n: grads[n]), pack_small(lambda n: given["m_" + n]),
                  pack_small(lambda n: given["v_" + n]), "adamw_small")
    for res, flat in zip((delta, new_m, new_v), outs):
        flat, off = flat.reshape(-1), 0
        for n in smalls:
            sz = given[n].size
            res[n] = flat[off:off + sz].reshape(given[n].shape)
            off += sz

    return (loss, g_a["x"], *[grads[n] for n in WEIGHT_NAMES], *[delta[n] for n in WEIGHT_NAMES],
            *[new_m[n] for n in WEIGHT_NAMES], *[new_v[n] for n in WEIGHT_NAMES])
```

```python
import functools
import math

import jax
import jax.numpy as jnp
from jax import lax
from jax.experimental import pallas as pl
from jax.experimental.pallas import tpu as pltpu

F32 = jnp.float32
MXU_DTYPE = jnp.bfloat16

D_MODEL = 1024
N_META = 16
LRU_WIDTH = 512
LRU_HEADS = 4
LRU_HEAD_DIM = 128
CONV_WIDTH = 4
LRU_C = 8.0
MLA_HEADS = 8
MLA_NOPE = 64
MLA_ROPE = 32
MLA_V = 64
MLA_Q_RANK = 256
MLA_KV_RANK = 128
RET_HEADS = 4
RET_QK_DIM = 256
RET_V_DIM = 512
D_FF = 4096
ROPE_BASE = 10000.0
DN_ALPHA = 4.0 ** 0.25
EPS = 1e-5
NEG_INF = -1e30
SEQ_BLOCK = 128

ADAM_LR = 0.001
ADAM_B1 = 0.9
ADAM_B2 = 0.999
ADAM_EPS = 1e-08
ADAM_WD = 0.01
ADAM_STEP = 10

PACK_COLS = 1024
TN_INPUT_VMEM_BYTES = 28 << 20
N_CHIPS = 4

MESH = pl.DeviceIdType.MESH


def _pick(n, target, align):
    best = None
    for t in range(align, min(n, target) + 1, align):
        if n % t == 0:
            best = t
    return n if best is None else best


def _round_up(n, m):
    return (n + m - 1) // m * m


def _relu2(a):
    r = jnp.maximum(a, 0.0)
    return r * r


def _ln_stats(z):
    mu = jnp.mean(z, axis=-1, keepdims=True)
    zc = z - mu
    var = jnp.mean(zc * zc, axis=-1, keepdims=True)
    return zc, lax.rsqrt(var + EPS)


def _mm_nn(a, w, act, name, out_dtype=F32, norm=None):
    M, K = a.shape
    sharded = w.ndim == 3
    n = w.shape[-1]
    N = n * (w.shape[0] if sharded else 1)
    tm = _pick(M, 1088 if K * a.dtype.itemsize <= 4096 and norm is None else 544, 8)
    tn = _pick(n, 1024, 128)
    per = n // tn
    assert norm is None or tn == N

    def body(a_ref, w_ref, *rest):
        av = a_ref[...]
        if act:
            av = _relu2(av.astype(F32))
        r = jnp.dot(av.astype(MXU_DTYPE), w_ref[...].astype(MXU_DTYPE), preferred_element_type=F32)
        if norm is None:
            rest[0][...] = r.astype(out_dtype)
        else:
            r_ref, g_ref, b_ref, o_ref, z_ref = rest
            z = DN_ALPHA * r_ref[...] + r
            zc, rstd = _ln_stats(z)
            z_ref[...] = z
            o_ref[...] = zc * rstd * g_ref[...] + b_ref[...]

    w_spec = pl.BlockSpec((None, K, tn), lambda i, j: (j // per, 0, j % per)) if sharded else pl.BlockSpec((K, tn), lambda i, j: (0, j))
    tile = pl.BlockSpec((tm, tn), lambda i, j: (i, j))
    in_specs, args = [pl.BlockSpec((tm, K), lambda i, j: (i, 0)), w_spec], [a, w]
    if norm is None:
        out_specs, out_shape = tile, jax.ShapeDtypeStruct((M, N), out_dtype)
    else:
        vec = pl.BlockSpec((1, N), lambda i, j: (0, 0))
        in_specs += [tile, vec, vec]
        args += [norm[0], norm[1].reshape(1, N), norm[2].reshape(1, N)]
        out_specs, out_shape = [tile, tile], [jax.ShapeDtypeStruct((M, N), F32)] * 2
    return pl.pallas_call(
        body, name=name, grid=(M // tm, N // tn), in_specs=in_specs, out_specs=out_specs, out_shape=out_shape,
        compiler_params=pltpu.CompilerParams(dimension_semantics=("parallel", "arbitrary")),
    )(*args)


def _mm_nt(g, w, a_src, name, out_dtype=F32, plus=None):
    M, N = g.shape
    sharded = w.ndim == 3
    K, n = w.shape[-2], w.shape[-1]
    if sharded:
        tk, nk = N, 1
    else:
        tk = N if N * g.dtype.itemsize <= 8192 else _pick(N, 2048, 128)
        nk = N // tk
    tm = _pick(M, 1088 if tk * g.dtype.itemsize <= 4096 else 544, 8)
    tn = _pick(K, 1024, 128)
    has_src = a_src is not None
    assert nk == 1 or out_dtype == F32
    assert plus is None or not has_src

    def body(*refs):
        if has_src:
            g_ref, w_ref, s_ref, o_ref = refs
        elif plus is not None:
            g_ref, w_ref, p_ref, o_ref = refs
        else:
            g_ref, w_ref, o_ref = refs
        nt = (((1,), (1,)), ((), ()))
        if sharded:
            r = sum(lax.dot_general(g_ref[:, s * n:(s + 1) * n].astype(MXU_DTYPE), w_ref[s].astype(MXU_DTYPE), nt, preferred_element_type=F32)
                    for s in range(w_ref.shape[0]))
        else:
            r = lax.dot_general(g_ref[...].astype(MXU_DTYPE), w_ref[...].astype(MXU_DTYPE), nt, preferred_element_type=F32)
        if has_src:
            r = r * (2.0 * jnp.maximum(s_ref[...].astype(F32), 0.0))
        first = r if plus is None else r + DN_ALPHA * p_ref[...]
        if nk == 1:
            o_ref[...] = first.astype(out_dtype)
        else:
            k = pl.program_id(2)

            @pl.when(k == 0)
            def _():
                o_ref[...] = first

            @pl.when(k > 0)
            def _():
                o_ref[...] += r

    w_spec = (pl.BlockSpec((w.shape[0], tn, n), lambda i, j, k: (0, j, 0)) if sharded
              else pl.BlockSpec((tn, tk), lambda i, j, k: (j, k)))
    in_specs = [pl.BlockSpec((tm, tk), lambda i, j, k: (i, k)), w_spec]
    args = [g, w]
    if has_src:
        assert nk == 1
        in_specs.append(pl.BlockSpec((tm, tn), lambda i, j, k: (i, j)))
        args.append(a_src)
    if plus is not None:
        in_specs.append(pl.BlockSpec((tm, tn), lambda i, j, k: (i, j)))
        args.append(plus)
    return pl.pallas_call(
        body, name=name,
        grid=(M // tm, K // tn, nk),
        in_specs=in_specs,
        out_specs=pl.BlockSpec((tm, tn), lambda i, j, k: (i, j)),
        out_shape=jax.ShapeDtypeStruct((M, K), out_dtype),
        compiler_params=pltpu.CompilerParams(dimension_semantics=("parallel", "parallel", "arbitrary")),
    )(*args)


def _mm_tn(a, g, act, name, col_shards=1, out_dtype=F32):
    M, K = a.shape
    _, N = g.shape
    n = N // col_shards
    tm, tn = _pick(K, 1024, 128), _pick(n, 1024, 128)
    row_bytes = tm * a.dtype.itemsize + tn * g.dtype.itemsize
    tk = _pick(M, min(2176, TN_INPUT_VMEM_BYTES // (2 * row_bytes)), 8)
    nk = M // tk
    per = n // tn
    direct = out_dtype == F32

    def body(a_ref, g_ref, o_ref, *scratch):
        acc_ref = o_ref if direct else scratch[0]
        k = pl.program_id(2)
        av = a_ref[...]
        if act:
            av = _relu2(av.astype(F32))
        r = lax.dot_general(av.astype(MXU_DTYPE), g_ref[...].astype(MXU_DTYPE),
                            (((0,), (0,)), ((), ())), preferred_element_type=F32)

        @pl.when(k == 0)
        def _():
            acc_ref[...] = r

        @pl.when(k > 0)
        def _():
            acc_ref[...] += r

        if not direct:
            @pl.when(k == nk - 1)
            def _():
                o_ref[...] = acc_ref[...].astype(out_dtype)

    if col_shards == 1:
        out_spec, out_shape = pl.BlockSpec((tm, tn), lambda i, j, k: (i, j)), (K, N)
    else:
        out_spec, out_shape = pl.BlockSpec((None, tm, tn), lambda i, j, k: (j // per, i, j % per)), (col_shards, K, n)
    return pl.pallas_call(
        body, name=name,
        grid=(K // tm, N // tn, nk),
        in_specs=[pl.BlockSpec((tk, tm), lambda i, j, k: (k, i)), pl.BlockSpec((tk, tn), lambda i, j, k: (k, j))],
        out_specs=out_spec,
        out_shape=jax.ShapeDtypeStruct(out_shape, out_dtype),
        scratch_shapes=[] if direct else [pltpu.VMEM((tm, tn), F32)],
        compiler_params=pltpu.CompilerParams(dimension_semantics=("parallel", "parallel", "arbitrary")),
    )(a, g)


@functools.partial(jax.custom_vjp, nondiff_argnums=(3, 4, 5, 6))
def matmul(a, w, w_grad_slot, act, name, out_dtype, col_shards):
    return _mm_nn(a, w, act, name + "_fwd", out_dtype)


def _matmul_fwd(a, w, w_grad_slot, act, name, out_dtype, col_shards):
    return _mm_nn(a, w, act, name + "_fwd", out_dtype), (a, w, jnp.zeros((), w_grad_slot.dtype))


def _matmul_bwd(act, name, out_dtype, col_shards, res, g):
    a, w, slot_like = res
    w_grad_dtype = slot_like.dtype
    da = _mm_nt(g, w, a if act else None, name + "_dx")
    dw = _mm_tn(a, g, act, name + "_dw", col_shards, w_grad_dtype)
    return da, None, dw


matmul.defvjp(_matmul_fwd, _matmul_bwd)


def _ln_bwd_call(z, g, dy, name):
    M, D = z.shape
    tm = _pick(M, 544, 8)

    def body(z_ref, g_ref, dy_ref, dz_ref, dg_ref, db_ref):
        @pl.when(pl.program_id(0) == 0)
        def _():
            dg_ref[...] = jnp.zeros_like(dg_ref)
            db_ref[...] = jnp.zeros_like(db_ref)

        zc, rstd = _ln_stats(z_ref[...])
        xhat = zc * rstd
        dy = dy_ref[...]
        dxh = dy * g_ref[...]
        m1 = jnp.mean(dxh, axis=-1, keepdims=True)
        m2 = jnp.mean(dxh * xhat, axis=-1, keepdims=True)
        dz_ref[...] = rstd * (dxh - m1 - xhat * m2)
        dg_ref[...] += jnp.sum(dy * xhat, axis=0, keepdims=True)
        db_ref[...] += jnp.sum(dy, axis=0, keepdims=True)

    row = pl.BlockSpec((tm, D), lambda i: (i, 0))
    vec = pl.BlockSpec((1, D), lambda i: (0, 0))
    return pl.pallas_call(
        body, name=name, grid=(M // tm,), in_specs=[row, vec, row], out_specs=[row, vec, vec],
        out_shape=[jax.ShapeDtypeStruct((M, D), F32), jax.ShapeDtypeStruct((1, D), F32), jax.ShapeDtypeStruct((1, D), F32)],
        compiler_params=pltpu.CompilerParams(dimension_semantics=("arbitrary",)),
    )(z, g.reshape(1, D), dy)


@functools.partial(jax.custom_vjp, nondiff_argnums=(7,))
def mlp_block(h, w1, w2, w1_grad_slot, w2_grad_slot, g, b, name):
    return _mlp_block_fwd(h, w1, w2, w1_grad_slot, w2_grad_slot, g, b, name)[0]


def _mlp_block_fwd(h, w1, w2, w1_grad_slot, w2_grad_slot, g, b, name):
    u = _mm_nn(h, w1, False, name + "_w1_fwd", out_dtype=MXU_DTYPE)
    out, z = _mm_nn(u, w2, True, name + "_w2_norm_fwd", norm=(h, g, b))
    return out, (h, u, z, w1, w2, g, jnp.zeros((), w1_grad_slot.dtype))


def _mlp_block_bwd(name, res, dy):
    h, u, z, w1, w2, g, slot_like = res
    dz, dg, db = _ln_bwd_call(z, g, dy, name + "_norm_bwd")
    du = _mm_nt(dz, w2, u, name + "_w2_dx", out_dtype=MXU_DTYPE)
    dw2 = _mm_tn(u, dz, True, name + "_w2_dw", 1, slot_like.dtype)
    dh = _mm_nt(du, w1, None, name + "_w1_dx", plus=dz)
    dw1 = _mm_tn(h, du, False, name + "_w1_dw", N_CHIPS, slot_like.dtype)
    return dh, None, None, dw1, dw2, dg.reshape(g.shape), db.reshape(g.shape)


mlp_block.defvjp(_mlp_block_fwd, _mlp_block_bwd)


@functools.partial(jax.custom_vjp, nondiff_argnums=(6,))
def out_block(h, y, w, w_grad_slot, g, b, name):
    return _out_block_fwd(h, y, w, w_grad_slot, g, b, name)[0]


def _out_block_fwd(h, y, w, w_grad_slot, g, b, name):
    out, z = _mm_nn(y, w, False, name + "_norm_fwd", norm=(h, g, b))
    return out, (y, z, w, g, jnp.zeros((), w_grad_slot.dtype))


def _out_block_bwd(name, res, dy):
    y, z, w, g, slot_like = res
    dz, dg, db = _ln_bwd_call(z, g, dy, name + "_norm_bwd")
    d_y = _mm_nt(dz, w, None, name + "_dx")
    dw = _mm_tn(y, dz, False, name + "_dw", 1, slot_like.dtype)
    return DN_ALPHA * dz, d_y, None, dw, dg.reshape(g.shape), db.reshape(g.shape)


out_block.defvjp(_out_block_fwd, _out_block_bwd)


def _rms_fwd_call(x, g, name, col_block=0):
    R = x.shape[0]
    W = g.shape[-1]
    tr = _pick(R, 1088, 8)

    def body(x_ref, g_ref, o_ref):
        xv = x_ref[...]
        rstd = lax.rsqrt(jnp.mean(xv * xv, axis=-1, keepdims=True) + EPS)
        o_ref[...] = xv * rstd * g_ref[...]

    vec = pl.BlockSpec((1, W), lambda i: (0, 0))
    return pl.pallas_call(
        body, name=name, grid=(R // tr,), in_specs=[pl.BlockSpec((tr, W), lambda i: (i, col_block)), vec],
        out_specs=pl.BlockSpec((tr, W), lambda i: (i, 0)), out_shape=jax.ShapeDtypeStruct((R, W), F32),
        compiler_params=pltpu.CompilerParams(dimension_semantics=("parallel",)),
    )(x, g.reshape(1, W))


def _rms_bwd_call(x, g, dy, name, col_block=0):
    R = x.shape[0]
    W = g.shape[-1]
    tr = _pick(R, 1088, 8)

    def body(x_ref, g_ref, dy_ref, dx_ref, dg_ref):
        @pl.when(pl.program_id(0) == 0)
        def _():
            dg_ref[...] = jnp.zeros_like(dg_ref)

        xv = x_ref[...]
        rstd = lax.rsqrt(jnp.mean(xv * xv, axis=-1, keepdims=True) + EPS)
        xhat = xv * rstd
        dy = dy_ref[...]
        dxh = dy * g_ref[...]
        dx_ref[...] = rstd * (dxh - xhat * jnp.mean(dxh * xhat, axis=-1, keepdims=True))
        dg_ref[...] += jnp.sum(dy * xhat, axis=0, keepdims=True)

    row = pl.BlockSpec((tr, W), lambda i: (i, 0))
    vec = pl.BlockSpec((1, W), lambda i: (0, 0))
    return pl.pallas_call(
        body, name=name, grid=(R // tr,), in_specs=[pl.BlockSpec((tr, W), lambda i: (i, col_block)), vec, row], out_specs=[row, vec],
        out_shape=[jax.ShapeDtypeStruct((R, W), F32), jax.ShapeDtypeStruct((1, W), F32)],
        compiler_params=pltpu.CompilerParams(dimension_semantics=("arbitrary",)),
    )(x, g.reshape(1, W), dy)


def _loss_call(h, tgt, n_tokens, name):
    B, Tp, D = h.shape
    tr = _pick(Tp, 544, 8)

    def body(y_ref, t_ref, dy_ref, acc_ref):
        @pl.when(jnp.logical_and(pl.program_id(0) == 0, pl.program_id(1) == 0))
        def _():
            acc_ref[...] = jnp.zeros_like(acc_ref)

        t = lax.broadcasted_iota(jnp.int32, (tr, 1), 0) + pl.program_id(1) * tr
        counts = jnp.logical_and(t >= N_META, t < N_META + n_tokens)
        e = jnp.where(counts, y_ref[...] - t_ref[...], 0.0)
        dy_ref[...] = e * (1.0 / D)
        acc_ref[...] += jnp.sum(jnp.sum(e * e, axis=-1, keepdims=True), axis=0, keepdims=True) * (0.5 / D)

    row = pl.BlockSpec((None, tr, D), lambda b, i: (b, i, 0))
    one = pl.BlockSpec((1, 1), lambda b, i: (0, 0))
    return pl.pallas_call(
        body, name=name, grid=(B, Tp // tr), in_specs=[row, row], out_specs=[row, one],
        out_shape=[jax.ShapeDtypeStruct((B, Tp, D), F32), jax.ShapeDtypeStruct((1, 1), F32)],
        compiler_params=pltpu.CompilerParams(dimension_semantics=("arbitrary", "arbitrary")),
    )(h, tgt)


@functools.partial(jax.custom_vjp, nondiff_argnums=(2,))
def loss_head(h, tgt, n_tokens):
    return _loss_call(h, tgt, n_tokens, "loss_head")[1][0, 0]


def _loss_head_fwd(h, tgt, n_tokens):
    dy, acc = _loss_call(h, tgt, n_tokens, "loss_head")
    return acc[0, 0], dy


def _loss_head_bwd(n_tokens, dy, ct):
    return ct * dy, None


loss_head.defvjp(_loss_head_fwd, _loss_head_bwd)


_GELU_C = math.sqrt(2.0 / math.pi)


def _gelu_parts(x):
    x2 = x * x
    t = jnp.tanh(_GELU_C * (x + 0.044715 * x * x2))
    gelu = 0.5 * x * (1.0 + t)
    dgelu = 0.5 * (1.0 + t) + 0.5 * x * (1.0 - t * t) * (_GELU_C * (1.0 + 3.0 * 0.044715 * x2))
    return gelu, dgelu


def _sigmoid(x):
    return 1.0 / (1.0 + jnp.exp(-x))


def _scan8(a, b, carry, reverse):
    row = lax.broadcasted_iota(jnp.int32, a.shape, 0)
    for s in (1, 2, 4):
        shift = 8 - s if reverse else s
        keep = (row < 8 - s) if reverse else (row >= s)
        b = jnp.where(keep, a * pltpu.roll(b, shift, 0) + b, b)
        a = jnp.where(keep, a * pltpu.roll(a, shift, 0), a)
    return a * carry + b


def _lru_pre(prec_ref, prev_ref, first, cw_ref, cb_ref, wa_ref, ba_ref, wx_ref, bx_ref, sp_ref):
    tc = prec_ref.shape[0]
    prev = jnp.where(first, 0.0, prev_ref[...])
    ext = jnp.concatenate([prev, prec_ref[...]], axis=0)
    cw = cw_ref[...]
    taps = [ext[8:] if k == CONV_WIDTH - 1 else pltpu.roll(ext, CONV_WIDTH - 1 - k, 0)[8:] for k in range(CONV_WIDTH)]
    xc = cb_ref[...] + sum(cw[k:k + 1, :] * taps[k] for k in range(CONV_WIDTH))
    ga, gx = [], []
    for h in range(LRU_HEADS):
        xh = xc[:, h * LRU_HEAD_DIM:(h + 1) * LRU_HEAD_DIM].astype(MXU_DTYPE)
        ga.append(jnp.dot(xh, wa_ref[h].astype(MXU_DTYPE), preferred_element_type=F32))
        gx.append(jnp.dot(xh, wx_ref[h].astype(MXU_DTYPE), preferred_element_type=F32))
    r = _sigmoid(jnp.concatenate(ga, axis=1) + ba_ref[...])
    i = _sigmoid(jnp.concatenate(gx, axis=1) + bx_ref[...])
    log_a = -LRU_C * r * sp_ref[...]
    a = jnp.exp(log_a)
    a2 = a * a
    mult = jnp.sqrt(-jnp.tanh(log_a) * (a2 + 1.0))
    return taps, xc, r, i, a, a2, mult


def _lru_fwd_call(p, cw, cb, wa, ba, wx, bx, sp):
    B, Tp, _ = p.shape
    W = LRU_WIDTH
    tc = SEQ_BLOCK
    nc = Tp // tc

    def body(pg_ref, prec_ref, prev_ref, cw_ref, cb_ref, wa_ref, ba_ref, wx_ref, bx_ref, sp_ref, y_ref, h_ref, carry_ref):
        first = pl.program_id(1) == 0

        @pl.when(first)
        def _():
            carry_ref[...] = jnp.zeros_like(carry_ref)

        _, xc, r, i, a, a2, mult = _lru_pre(prec_ref, prev_ref, first, cw_ref, cb_ref, wa_ref, ba_ref, wx_ref, bx_ref, sp_ref)
        b = mult * (i * xc)
        carry = carry_ref[0:1, :]
        for t in range(tc // 8):
            h = _scan8(a[8 * t:8 * t + 8], b[8 * t:8 * t + 8], carry, False)
            h_ref[8 * t:8 * t + 8, :] = h
            carry = h[7:8, :]
        carry_ref[...] = jnp.broadcast_to(carry, carry_ref.shape)
        y_ref[...] = h_ref[...] * _gelu_parts(pg_ref[...])[0]

    cur = pl.BlockSpec((None, tc, W), lambda b, j: (b, j, 0))
    rec = pl.BlockSpec((None, tc, W), lambda b, j: (b, j, 1))
    prev = pl.BlockSpec((None, 8, W), lambda b, j: (b, jnp.maximum(j * (tc // 8) - 1, 0), 1))
    vec = pl.BlockSpec((1, W), lambda b, j: (0, 0))
    cws = pl.BlockSpec((CONV_WIDTH, W), lambda b, j: (0, 0))
    wsp = pl.BlockSpec((LRU_HEADS, LRU_HEAD_DIM, LRU_HEAD_DIM), lambda b, j: (0, 0, 0))
    return pl.pallas_call(
        body, name="lru_fwd", grid=(B, nc),
        in_specs=[cur, rec, prev, cws, vec, wsp, vec, wsp, vec, vec],
        out_specs=[cur, cur],
        out_shape=[jax.ShapeDtypeStruct((B, Tp, W), F32), jax.ShapeDtypeStruct((B, Tp, W), F32)],
        scratch_shapes=[pltpu.VMEM((8, W), F32)],
        compiler_params=pltpu.CompilerParams(dimension_semantics=("arbitrary", "arbitrary")),
    )(p, p, p, cw, cb, wa, ba, wx, bx, sp)


def _lru_bwd_call(p, hseq, dy, cw, cb, wa, ba, wx, bx, sp, dpq, dpkv, dkpe):
    B, Tp, P = p.shape
    W = LRU_WIDTH
    tc = SEQ_BLOCK
    nc = Tp // tc
    HD = LRU_HEAD_DIM

    def body(pg_ref, prec_ref, prev_ref, h_ref, hprev_ref, dy_ref, cw_ref, cb_ref, wa_ref, ba_ref, wx_ref, bx_ref, sp_ref,
             dpq_ref, dpkv_ref, dkpe_ref, dp_ref, dcw_ref, dcb_ref, dwa_ref, dba_ref, dwx_ref, dbx_ref, dsp_ref,
             gcar_ref, anext_ref, halo_ref, g_ref):
        j = pl.program_id(1)
        first = j == nc - 1
        last = j == 0

        @pl.when(jnp.logical_and(pl.program_id(0) == 0, last))
        def _():
            for ref in (dcw_ref, dcb_ref, dwa_ref, dba_ref, dwx_ref, dbx_ref, dsp_ref):
                ref[...] = jnp.zeros_like(ref)

        @pl.when(last)
        def _():
            gcar_ref[...] = jnp.zeros_like(gcar_ref)
            anext_ref[...] = jnp.zeros_like(anext_ref)
            halo_ref[...] = jnp.zeros_like(halo_ref)

        taps, xc, r, i, a, a2, mult = _lru_pre(prec_ref, prev_ref, first, cw_ref, cb_ref, wa_ref, ba_ref, wx_ref, bx_ref, sp_ref)
        row = lax.broadcasted_iota(jnp.int32, (tc, W), 0)
        gelu, dgelu = _gelu_parts(pg_ref[...])
        dy = dy_ref[...]
        hcur = h_ref[...]
        dp_ref[:, 0:W] = dy * hcur * dgelu
        dp_ref[:, 2 * W:2 * W + MLA_Q_RANK] = dpq_ref[...]
        dp_ref[:, _KPE_START - MLA_KV_RANK:_KPE_START] = dpkv_ref[...]
        dp_ref[:, _KPE_START:P] = pltpu.roll(dkpe_ref[...], HEAD_LANES - MLA_NOPE, 1)[:, 0:P - _KPE_START]
        dh = dy * gelu
        a_next = jnp.where(row == tc - 1, anext_ref[0:1, :], pltpu.roll(a, tc - 1, 0))
        carry = gcar_ref[0:1, :]
        for t in reversed(range(tc // 8)):
            g = _scan8(a_next[8 * t:8 * t + 8], dh[8 * t:8 * t + 8], carry, True)
            g_ref[8 * t:8 * t + 8, :] = g
            carry = g[0:1, :]
        gcar_ref[...] = jnp.broadcast_to(carry, gcar_ref.shape)
        anext_ref[...] = jnp.broadcast_to(a[0:1, :], anext_ref.shape)
        G = g_ref[...]
        h_before = jnp.where(first, 0.0, hprev_ref[7:8, :])
        hprev = jnp.where(row == 0, h_before, pltpu.roll(hcur, 1, 0))
        d_a = G * hprev
        gx_ = G * xc
        d_mult = gx_ * i
        d_i = gx_ * mult
        dxc = G * (mult * i)
        d_la = d_a * a - d_mult * (a2 / mult)
        sp = sp_ref[...]
        d_r = d_la * (-LRU_C * sp)
        dsp_ref[...] += jnp.sum(d_la * (-LRU_C * r), axis=0, keepdims=True)
        dga = d_r * r * (1.0 - r)
        dgx = d_i * i * (1.0 - i)
        dba_ref[...] += jnp.sum(dga, axis=0, keepdims=True)
        dbx_ref[...] += jnp.sum(dgx, axis=0, keepdims=True)
        back = []
        for h in range(LRU_HEADS):
            sl = slice(h * HD, (h + 1) * HD)
            xh = xc[:, sl].astype(MXU_DTYPE)
            ah = dga[:, sl].astype(MXU_DTYPE)
            bh = dgx[:, sl].astype(MXU_DTYPE)
            tn = (((0,), (0,)), ((), ()))
            nt = (((1,), (1,)), ((), ()))
            dwa_ref[h] += lax.dot_general(xh, ah, tn, preferred_element_type=F32)
            dwx_ref[h] += lax.dot_general(xh, bh, tn, preferred_element_type=F32)
            back.append(lax.dot_general(ah, wa_ref[h].astype(MXU_DTYPE), nt, preferred_element_type=F32)
                        + lax.dot_general(bh, wx_ref[h].astype(MXU_DTYPE), nt, preferred_element_type=F32))
        dxc = dxc + jnp.concatenate(back, axis=1)
        dcb_ref[...] += jnp.sum(dxc, axis=0, keepdims=True)
        for k in range(CONV_WIDTH):
            dcw_ref[k:k + 1, :] += jnp.sum(dxc * taps[k], axis=0, keepdims=True)
        ext = jnp.concatenate([dxc, halo_ref[...]], axis=0)
        cw = cw_ref[...]
        acc = cw[CONV_WIDTH - 1:CONV_WIDTH, :] * dxc
        for k in range(CONV_WIDTH - 1):
            s = CONV_WIDTH - 1 - k
            acc = acc + cw[k:k + 1, :] * pltpu.roll(ext, tc + 8 - s, 0)[:tc]
        dp_ref[:, W:2 * W] = acc
        halo_ref[...] = dxc[0:8, :]

    rev = lambda j: nc - 1 - j
    cur = pl.BlockSpec((None, tc, W), lambda b, j: (b, rev(j), 0))
    rec = pl.BlockSpec((None, tc, W), lambda b, j: (b, rev(j), 1))
    prev = pl.BlockSpec((None, 8, W), lambda b, j: (b, jnp.maximum(rev(j) * (tc // 8) - 1, 0), 0))
    prev_rec = pl.BlockSpec((None, 8, W), lambda b, j: (b, jnp.maximum(rev(j) * (tc // 8) - 1, 0), 1))
    vec = pl.BlockSpec((1, W), lambda b, j: (0, 0))
    cws = pl.BlockSpec((CONV_WIDTH, W), lambda b, j: (0, 0))
    wsp = pl.BlockSpec((LRU_HEADS, HD, HD), lambda b, j: (0, 0, 0))
    vs = jax.ShapeDtypeStruct((1, W), F32)
    ws = jax.ShapeDtypeStruct((LRU_HEADS, HD, HD), F32)

    def rows(width):
        return pl.BlockSpec((None, tc, width), lambda b, j: (b, rev(j), 0))

    return pl.pallas_call(
        body, name="lru_bwd", grid=(B, nc),
        in_specs=[cur, rec, prev_rec, cur, prev, cur, cws, vec, wsp, vec, wsp, vec, vec, rows(MLA_Q_RANK), rows(MLA_KV_RANK), rows(HEAD_LANES)],
        out_specs=[rows(P), cws, vec, wsp, vec, wsp, vec, vec],
        out_shape=[jax.ShapeDtypeStruct((B, Tp, P), F32), jax.ShapeDtypeStruct((CONV_WIDTH, W), F32), vs, ws, vs, ws, vs, vs],
        scratch_shapes=[pltpu.VMEM((8, W), F32), pltpu.VMEM((8, W), F32), pltpu.VMEM((8, W), F32), pltpu.VMEM((tc, W), F32)],
        compiler_params=pltpu.CompilerParams(dimension_semantics=("arbitrary", "arbitrary")),
    )(p, p, p, hseq, hseq, dy, cw, cb, wa, ba, wx, bx, sp, dpq, dpkv, dkpe)


_Q_BLOCK = 2 * LRU_WIDTH // MLA_Q_RANK
_KV_BLOCK = (2 * LRU_WIDTH + MLA_Q_RANK) // MLA_KV_RANK
_KPE_START = 2 * LRU_WIDTH + MLA_Q_RANK + MLA_KV_RANK


@jax.custom_vjp
def even_front(p, cw, cb, wa, ba, wx, bx, sp, gq, gkv):
    return _even_front_fwd(p, cw, cb, wa, ba, wx, bx, sp, gq, gkv)[0]


def _even_front_fwd(p, cw, cb, wa, ba, wx, bx, sp, gq, gkv):
    B, Tp, W = p.shape
    p2d = p.reshape(B * Tp, W)
    y, hseq = _lru_fwd_call(p, cw, cb, wa, ba, wx, bx, sp)
    qn = _rms_fwd_call(p2d, gq, "q_norm_fwd", _Q_BLOCK)
    kvn = _rms_fwd_call(p2d, gkv, "kv_norm_fwd", _KV_BLOCK)
    kpe = jnp.pad(p[:, :, _KPE_START:], ((0, 0), (0, 0), (MLA_NOPE, HEAD_LANES - MLA_NOPE - MLA_ROPE)))
    return (y, qn, kvn, kpe), (p, hseq, cw, cb, wa, ba, wx, bx, sp, gq, gkv)


def _even_front_bwd(res, cts):
    p, hseq, cw, cb, wa, ba, wx, bx, sp, gq, gkv = res
    dy, dqn, dkvn, dkpe = cts
    B, Tp, W = p.shape
    p2d = p.reshape(B * Tp, W)
    dpq, dgq = _rms_bwd_call(p2d, gq, dqn, "q_norm_bwd", _Q_BLOCK)
    dpkv, dgkv = _rms_bwd_call(p2d, gkv, dkvn, "kv_norm_bwd", _KV_BLOCK)
    dp, dcw, dcb, dwa, dba, dwx, dbx, dsp = _lru_bwd_call(p, hseq, dy, cw, cb, wa, ba, wx, bx, sp, dpq.reshape(B, Tp, -1),
                                                          dpkv.reshape(B, Tp, -1), dkpe)
    return dp, dcw, dcb, dwa, dba, dwx, dbx, dsp, dgq.reshape(gq.shape), dgkv.reshape(gkv.shape)


even_front.defvjp(_even_front_fwd, _even_front_bwd)


def _rope_tables(pos, half):
    inv = ROPE_BASE ** (-jnp.arange(half, dtype=F32) / half)
    ang = pos.astype(F32)[:, None] * inv[None, :]
    return jnp.cos(ang), jnp.sin(ang)


_NT = (((1,), (1,)), ((), ()))
_TN = (((0,), (0,)), ((), ()))
HEAD_LANES = 128
_MLA_SCALE = (MLA_NOPE + MLA_ROPE) ** -0.5
_LOG2E = math.log2(math.e)


Q_BLOCK = 512


def _query_blocks(Tp):
    first = Tp % Q_BLOCK or Q_BLOCK
    return [(0, first)] + [(r, r + Q_BLOCK) for r in range(first, Tp, Q_BLOCK)]


def _mask_diagonal(s, fill):
    R, L = s.shape
    row = lax.broadcasted_iota(jnp.int32, (R, R), 0)
    col = lax.broadcasted_iota(jnp.int32, (R, R), 1)
    last = jnp.where(col <= row, s[:, L - R:], fill)
    return last if L == R else jnp.concatenate([s[:, :L - R], last], axis=1)


def _mla_rope_tables(pos):
    half = MLA_ROPE // 2
    cos, sin = _rope_tables(pos, half)
    T = pos.shape[0]
    ones, zeros = jnp.ones((T, MLA_NOPE), F32), jnp.zeros((T, MLA_NOPE), F32)
    tail1, tail0 = jnp.ones((T, HEAD_LANES - MLA_NOPE - MLA_ROPE), F32), jnp.zeros((T, HEAD_LANES - MLA_NOPE - MLA_ROPE), F32)
    zh = jnp.zeros((T, half), F32)
    c = jnp.concatenate([ones, cos, cos, tail1], axis=1)
    s_up = jnp.concatenate([zeros, -sin, zh, tail0], axis=1)
    s_down = jnp.concatenate([zeros, zh, sin, tail0], axis=1)
    return c, s_up, s_down


def _rope_lanes(x, c, s_up, s_down):
    half = MLA_ROPE // 2
    return x * c + pltpu.roll(x, HEAD_LANES - half, 1) * s_up + pltpu.roll(x, half, 1) * s_down


def _unrope_lanes(d, c, s_up, s_down):
    half = MLA_ROPE // 2
    return d * c + pltpu.roll(d * s_up, half, 1) + pltpu.roll(d * s_down, HEAD_LANES - half, 1)


def _mla_operands(q_ref, kv_ref, kpe_ref, c, s_up, s_down):
    lane = lax.broadcasted_iota(jnp.int32, kv_ref.shape, 1)
    qr = (_rope_lanes(q_ref[...].astype(F32), c, s_up, s_down) * (_MLA_SCALE * _LOG2E)).astype(MXU_DTYPE)
    kr = jnp.where(lane < MLA_NOPE, kv_ref[...].astype(F32), _rope_lanes(kpe_ref[...], c, s_up, s_down)).astype(MXU_DTYPE)
    return qr, kr, lane


def _mla_specs(Tp):
    head = pl.BlockSpec((None, Tp, HEAD_LANES), lambda b, h: (b, 0, h))
    shared = pl.BlockSpec((None, Tp, HEAD_LANES), lambda b, h: (b, 0, 0))
    tab = pl.BlockSpec((Tp, HEAD_LANES), lambda b, h: (0, 0))
    lse = pl.BlockSpec((None, None, Tp, 1), lambda b, h: (b, h, 0, 0))
    return head, shared, tab, lse


def _attn_fwd_call(q, kv, kpe, tabs):
    B, Tp, _ = q.shape

    def body(q_ref, kv_ref, kpe_ref, c_ref, su_ref, sd_ref, o_ref, lse_ref, qr_ref, kr_ref):
        qr, kr, lane = _mla_operands(q_ref, kv_ref, kpe_ref, c_ref[...], su_ref[...], sd_ref[...])
        qr_ref[...] = qr
        kr_ref[...] = kr
        for r0, L in _query_blocks(Tp):
            blk = slice(r0, L)
            s = _mask_diagonal(lax.dot_general(qr_ref[blk, :], kr_ref[0:L, :], _NT, preferred_element_type=F32), NEG_INF)
            m = jnp.max(s, axis=-1, keepdims=True)
            p = jnp.exp2(s - m)
            l = jnp.sum(p, axis=-1, keepdims=True)
            o = jnp.dot(p.astype(MXU_DTYPE), kv_ref[0:L, :].astype(MXU_DTYPE), preferred_element_type=F32)
            o_ref[blk, :] = jnp.where(lane[blk, :] >= MLA_NOPE, o / l, 0.0)
            lse_ref[blk, :] = m + jnp.log2(l)

    head, shared, tab, lse = _mla_specs(Tp)
    return pl.pallas_call(
        body, name="mla_attn_fwd", grid=(B, MLA_HEADS), in_specs=[head, head, shared, tab, tab, tab], out_specs=[head, lse],
        out_shape=[jax.ShapeDtypeStruct((B, Tp, MLA_HEADS * HEAD_LANES), F32), jax.ShapeDtypeStruct((B, MLA_HEADS, Tp, 1), F32)],
        scratch_shapes=[pltpu.VMEM((Tp, HEAD_LANES), MXU_DTYPE), pltpu.VMEM((Tp, HEAD_LANES), MXU_DTYPE)],
        compiler_params=pltpu.CompilerParams(dimension_semantics=("parallel", "parallel")),
    )(q, kv, kpe, *tabs)


def _attn_bwd_call(q, kv, kpe, tabs, o, lse, do):
    B, Tp, _ = q.shape

    def body(q_ref, kv_ref, kpe_ref, c_ref, su_ref, sd_ref, o_ref, lse_ref, do_ref, dq_ref, dkv_ref, dkpe_ref,
             qr_ref, kr_ref, dqa_ref, dka_ref, dva_ref):
        c, s_up, s_down = c_ref[...], su_ref[...], sd_ref[...]
        qr, kr, lane = _mla_operands(q_ref, kv_ref, kpe_ref, c, s_up, s_down)
        qr_ref[...] = qr
        kr_ref[...] = kr
        dka_ref[...] = jnp.zeros_like(dka_ref)
        dva_ref[...] = jnp.zeros_like(dva_ref)
        for r0, L in _query_blocks(Tp):
            blk = slice(r0, L)
            qb = qr_ref[blk, :]
            do = jnp.where(lane[blk, :] >= MLA_NOPE, do_ref[blk, :], 0.0)
            delta = jnp.sum(do * o_ref[blk, :], axis=-1, keepdims=True)
            s = _mask_diagonal(lax.dot_general(qb, kr_ref[0:L, :], _NT, preferred_element_type=F32), NEG_INF)
            p = jnp.exp2(s - lse_ref[blk, :])
            dob = do.astype(MXU_DTYPE)
            dva_ref[0:L, :] += lax.dot_general(p.astype(MXU_DTYPE), dob, _TN, preferred_element_type=F32)
            dp = lax.dot_general(dob, kv_ref[0:L, :].astype(MXU_DTYPE), _NT, preferred_element_type=F32)
            ds = (p * (dp - delta)).astype(MXU_DTYPE)
            dqa_ref[blk, :] = jnp.dot(ds, kr_ref[0:L, :], preferred_element_type=F32)
            dka_ref[0:L, :] += lax.dot_general(ds, qb, _TN, preferred_element_type=F32)
        dq_ref[...] = _unrope_lanes(dqa_ref[...] * _MLA_SCALE, c, s_up, s_down).astype(dq_ref.dtype)
        dk = dka_ref[...] * (1.0 / _LOG2E)
        dkv_ref[...] = jnp.where(lane < MLA_NOPE, dk, dva_ref[...]).astype(dkv_ref.dtype)
        dkpe = jnp.where(lane >= MLA_NOPE, _unrope_lanes(dk, c, s_up, s_down), 0.0)

        @pl.when(pl.program_id(1) == 0)
        def _():
            dkpe_ref[...] = dkpe

        @pl.when(pl.program_id(1) > 0)
        def _():
            dkpe_ref[...] += dkpe

    head, shared, tab, lse_spec = _mla_specs(Tp)
    wide = jax.ShapeDtypeStruct((B, Tp, MLA_HEADS * HEAD_LANES), q.dtype)
    acc = pltpu.VMEM((Tp, HEAD_LANES), F32)
    return pl.pallas_call(
        body, name="mla_attn_bwd", grid=(B, MLA_HEADS),
        in_specs=[head, head, shared, tab, tab, tab, head, lse_spec, head], out_specs=[head, head, shared],
        out_shape=[wide, wide, jax.ShapeDtypeStruct((B, Tp, HEAD_LANES), F32)],
        scratch_shapes=[pltpu.VMEM((Tp, HEAD_LANES), MXU_DTYPE), pltpu.VMEM((Tp, HEAD_LANES), MXU_DTYPE), acc, acc, acc],
        compiler_params=pltpu.CompilerParams(dimension_semantics=("parallel", "arbitrary")),
    )(q, kv, kpe, *tabs, o, lse, do)


@jax.custom_vjp
def mla_attention(q, kv, kpe, tabs):
    return _attn_fwd_call(q, kv, kpe, tabs)[0]


def _mla_attention_fwd(q, kv, kpe, tabs):
    o, lse = _attn_fwd_call(q, kv, kpe, tabs)
    return o, (q, kv, kpe, tabs, o, lse)


def _mla_attention_bwd(res, do):
    q, kv, kpe, tabs, o, lse = res
    dq, dkv, dkpe = _attn_bwd_call(q, kv, kpe, tabs, o, lse, do)
    return dq, dkv, dkpe, None


mla_attention.defvjp(_mla_attention_fwd, _mla_attention_bwd)


def _rope_halves(x, cos, sin):
    half = x.shape[1] // 2
    x1, x2 = x[:, :half], x[:, half:]
    return jnp.concatenate([x1 * cos - x2 * sin, x1 * sin + x2 * cos], axis=1)


def _unrope_halves(d, cos, sin):
    half = d.shape[1] // 2
    d1, d2 = d[:, :half], d[:, half:]
    return jnp.concatenate([d1 * cos + d2 * sin, d2 * cos - d1 * sin], axis=1)


_RET_K_SCALE = RET_QK_DIM ** -0.5
_RET_Q_BLOCKS = RET_HEADS
_RET_V_BLOCK0 = 2 * RET_HEADS * RET_QK_DIM // RET_V_DIM
_RET_G_BLOCK0 = _RET_V_BLOCK0 + RET_HEADS


def _ret_specs(Tp):
    q = pl.BlockSpec((None, Tp, RET_QK_DIM), lambda b, h: (b, 0, h))
    k = pl.BlockSpec((None, Tp, RET_QK_DIM), lambda b, h: (b, 0, _RET_Q_BLOCKS + h))
    v = pl.BlockSpec((None, Tp, RET_V_DIM), lambda b, h: (b, 0, _RET_V_BLOCK0 + h))
    tab = pl.BlockSpec((Tp, RET_QK_DIM // 2), lambda b, h: (0, 0))
    lg = pl.BlockSpec((None, 1, 1), lambda b, h: (h, 0, 0))
    return q, k, v, tab, lg


def _ret_operands(q_ref, k_ref, cos, sin, lg):
    t = lax.broadcasted_iota(jnp.int32, (q_ref.shape[0], 1), 0).astype(F32)
    grow, shrink = jnp.exp(-lg * t), jnp.exp(lg * t)
    qs = (_rope_halves(q_ref[...].astype(F32), cos, sin) * shrink).astype(MXU_DTYPE)
    ks = (_rope_halves(k_ref[...].astype(F32), cos, sin) * (grow * _RET_K_SCALE)).astype(MXU_DTYPE)
    return qs, ks, shrink, grow * _RET_K_SCALE


def _ret_core_fwd_call(p, cos, sin, lg):
    B, Tp, _ = p.shape

    def body(q_ref, k_ref, v_ref, cos_ref, sin_ref, lg_ref, o_ref, qs_ref, ks_ref):
        qs_ref[...], ks_ref[...], _, _ = _ret_operands(q_ref, k_ref, cos_ref[...], sin_ref[...], lg_ref[...])
        for r0, L in _query_blocks(Tp):
            blk = slice(r0, L)
            s = _mask_diagonal(lax.dot_general(qs_ref[blk, :], ks_ref[0:L, :], _NT, preferred_element_type=F32), 0.0)
            o_ref[blk, :] = jnp.dot(s.astype(MXU_DTYPE), v_ref[0:L, :].astype(MXU_DTYPE), preferred_element_type=F32)

    q, k, v, tab, lgs = _ret_specs(Tp)
    return pl.pallas_call(
        body, name="retention_fwd", grid=(B, RET_HEADS), in_specs=[q, k, v, tab, tab, lgs],
        out_specs=pl.BlockSpec((None, Tp, RET_V_DIM), lambda b, h: (b, 0, h)),
        out_shape=jax.ShapeDtypeStruct((B, Tp, RET_HEADS * RET_V_DIM), F32),
        scratch_shapes=[pltpu.VMEM((Tp, RET_QK_DIM), MXU_DTYPE), pltpu.VMEM((Tp, RET_QK_DIM), MXU_DTYPE)],
        compiler_params=pltpu.CompilerParams(dimension_semantics=("parallel", "parallel")),
    )(p, p, p, cos, sin, lg)


def _ret_core_bwd_call(p, do, cos, sin, lg):
    B, Tp, _ = p.shape

    def body(q_ref, k_ref, v_ref, do_ref, cos_ref, sin_ref, lg_ref, dq_ref, dk_ref, dv_ref, qs_ref, ks_ref, dqa_ref, dka_ref, dva_ref):
        cos_, sin_ = cos_ref[...], sin_ref[...]
        qs_ref[...], ks_ref[...], q_scale, k_scale = _ret_operands(q_ref, k_ref, cos_, sin_, lg_ref[...])
        dka_ref[...] = jnp.zeros_like(dka_ref)
        dva_ref[...] = jnp.zeros_like(dva_ref)
        for r0, L in _query_blocks(Tp):
            blk = slice(r0, L)
            qb = qs_ref[blk, :]
            dob = do_ref[blk, :].astype(MXU_DTYPE)
            s = _mask_diagonal(lax.dot_general(qb, ks_ref[0:L, :], _NT, preferred_element_type=F32), 0.0).astype(MXU_DTYPE)
            dva_ref[0:L, :] += lax.dot_general(s, dob, _TN, preferred_element_type=F32)
            ds = _mask_diagonal(lax.dot_general(dob, v_ref[0:L, :].astype(MXU_DTYPE), _NT, preferred_element_type=F32), 0.0).astype(MXU_DTYPE)
            dqa_ref[blk, :] = jnp.dot(ds, ks_ref[0:L, :], preferred_element_type=F32)
            dka_ref[0:L, :] += lax.dot_general(ds, qb, _TN, preferred_element_type=F32)
        dq_ref[...] = _unrope_halves(dqa_ref[...] * q_scale, cos_, sin_).astype(dq_ref.dtype)
        dk_ref[...] = _unrope_halves(dka_ref[...] * k_scale, cos_, sin_).astype(dk_ref.dtype)
        dv_ref[...] = dva_ref[...].astype(dv_ref.dtype)

    q, k, v, tab, lgs = _ret_specs(Tp)
    qk_out = pl.BlockSpec((None, Tp, RET_QK_DIM), lambda b, h: (b, 0, h))
    v_out = pl.BlockSpec((None, Tp, RET_V_DIM), lambda b, h: (b, 0, h))
    return pl.pallas_call(
        body, name="retention_bwd", grid=(B, RET_HEADS), in_specs=[q, k, v, v_out, tab, tab, lgs],
        out_specs=[qk_out, qk_out, v_out],
        out_shape=[jax.ShapeDtypeStruct((B, Tp, RET_HEADS * RET_QK_DIM), p.dtype), jax.ShapeDtypeStruct((B, Tp, RET_HEADS * RET_QK_DIM), p.dtype),
                   jax.ShapeDtypeStruct((B, Tp, RET_HEADS * RET_V_DIM), p.dtype)],
        scratch_shapes=[pltpu.VMEM((Tp, RET_QK_DIM), MXU_DTYPE), pltpu.VMEM((Tp, RET_QK_DIM), MXU_DTYPE),
                        pltpu.VMEM((Tp, RET_QK_DIM), F32), pltpu.VMEM((Tp, RET_QK_DIM), F32), pltpu.VMEM((Tp, RET_V_DIM), F32)],
        compiler_params=pltpu.CompilerParams(dimension_semantics=("parallel", "parallel")),
    )(p, p, p, do, cos, sin, lg)


def _ret_gate_specs(M):
    tm = _pick(M, 1088, 8)
    head = pl.BlockSpec((tm, RET_V_DIM), lambda i, h: (i, h))
    gate = pl.BlockSpec((tm, RET_V_DIM), lambda i, h: (i, _RET_G_BLOCK0 + h))
    return tm, head, gate


def _ret_gate_fwd_call(o, p2d):
    M = o.shape[0]
    tm, head, gate = _ret_gate_specs(M)

    def body(o_ref, g_ref, y_ref):
        ov = o_ref[...]
        gv = g_ref[...].astype(F32)
        rstd = lax.rsqrt(jnp.mean(ov * ov, axis=-1, keepdims=True) + EPS)
        y_ref[...] = (gv * _sigmoid(gv)) * (ov * rstd)

    return pl.pallas_call(
        body, name="retention_gate_fwd", grid=(M // tm, RET_HEADS), in_specs=[head, gate], out_specs=head,
        out_shape=jax.ShapeDtypeStruct(o.shape, F32),
        compiler_params=pltpu.CompilerParams(dimension_semantics=("parallel", "parallel")),
    )(o, p2d)


def _ret_gate_bwd_call(o, p2d, dy):
    M = o.shape[0]
    tm, head, gate = _ret_gate_specs(M)

    def body(o_ref, g_ref, dy_ref, do_ref, dg_ref):
        ov = o_ref[...]
        gv = g_ref[...].astype(F32)
        dy = dy_ref[...]
        rstd = lax.rsqrt(jnp.mean(ov * ov, axis=-1, keepdims=True) + EPS)
        on = ov * rstd
        sg = _sigmoid(gv)
        dg_ref[...] = (dy * on * (sg * (1.0 + gv * (1.0 - sg)))).astype(dg_ref.dtype)
        don = dy * (gv * sg)
        do_ref[...] = (rstd * (don - on * jnp.mean(don * on, axis=-1, keepdims=True))).astype(do_ref.dtype)

    shp = jax.ShapeDtypeStruct(o.shape, p2d.dtype)
    return pl.pallas_call(
        body, name="retention_gate_bwd", grid=(M // tm, RET_HEADS), in_specs=[head, gate, head], out_specs=[head, head],
        out_shape=[shp, shp],
        compiler_params=pltpu.CompilerParams(dimension_semantics=("parallel", "parallel")),
    )(o, p2d, dy)


def _log_gamma():
    return jnp.log(1.0 - 2.0 ** (-5.0 - jnp.arange(RET_HEADS, dtype=F32))).reshape(RET_HEADS, 1, 1)


@functools.partial(jax.custom_vjp, nondiff_argnums=(9,))
def retention_block(h, w_in, w_out, w_in_grad_slot, w_out_grad_slot, g, b, cos, sin, dims):
    return _retention_block_fwd(h, w_in, w_out, w_in_grad_slot, w_out_grad_slot, g, b, cos, sin, dims)[0]


def _retention_block_fwd(h, w_in, w_out, w_in_grad_slot, w_out_grad_slot, g, b, cos, sin, dims):
    B, Tp = dims
    p = _mm_nn(h, w_in, False, "od_w_in_fwd", out_dtype=MXU_DTYPE)
    o = _ret_core_fwd_call(p.reshape(B, Tp, -1), cos, sin, _log_gamma())
    y = _ret_gate_fwd_call(o.reshape(B * Tp, -1), p)
    out, z = _mm_nn(y, w_out, False, "od_w_out_norm_fwd", norm=(h, g, b))
    return out, (h, p, o, y, z, w_in, w_out, g, cos, sin, jnp.zeros((), w_in_grad_slot.dtype))


def _retention_block_bwd(dims, res, dout):
    B, Tp = dims
    h, p, o, y, z, w_in, w_out, g, cos, sin, slot_like = res
    dz, dg, db = _ln_bwd_call(z, g, dout, "od_w_out_norm_bwd")
    dy = _mm_nt(dz, w_out, None, "od_w_out_dx")
    dw_out = _mm_tn(y, dz, False, "od_w_out_dw", 1, slot_like.dtype)
    do, dgate = _ret_gate_bwd_call(o.reshape(B * Tp, -1), p, dy)
    dq, dk, dv = _ret_core_bwd_call(p.reshape(B, Tp, -1), do.reshape(B, Tp, -1), cos, sin, _log_gamma())
    dp = jnp.concatenate([dq.reshape(B * Tp, -1), dk.reshape(B * Tp, -1), dv.reshape(B * Tp, -1), dgate], axis=-1)
    dh = _mm_nt(dp, w_in, None, "od_w_in_dx", plus=dz)
    dw_in = _mm_tn(h, dp, False, "od_w_in_dw", N_CHIPS, slot_like.dtype)
    return dh, None, None, dw_in, dw_out, dg.reshape(g.shape), db.reshape(g.shape), None, None


retention_block.defvjp(_retention_block_fwd, _retention_block_bwd)


def _heads_to_lanes(w):
    K = w.shape[0]
    w = w.reshape(K, MLA_HEADS, MLA_NOPE + MLA_ROPE)
    return jnp.pad(w, ((0, 0), (0, 0), (0, HEAD_LANES - MLA_NOPE - MLA_ROPE))).reshape(K, MLA_HEADS * HEAD_LANES)


def _out_rows_to_lanes(w):
    N = w.shape[1]
    att = w[LRU_WIDTH:].reshape(MLA_HEADS, MLA_V, N)
    att = jnp.pad(att, ((0, 0), (HEAD_LANES - MLA_V, 0), (0, 0))).reshape(MLA_HEADS * HEAD_LANES, N)
    return jnp.concatenate([w[:LRU_WIDTH], att], axis=0)


def _seq_dims(x):
    B, S, D = x.shape
    T = S + N_META
    Tp = _round_up(T, SEQ_BLOCK)
    return B, S, T, Tp


def _mixer0(diff, w, token):
    x = diff["x"]
    B, S, T, Tp = _seq_dims(x)
    D = x.shape[-1]
    M = B * Tp
    pos = jnp.arange(Tp, dtype=jnp.int32)

    def mm(a, name, act=False, out_dtype=F32, layout=lambda m: m, col_shards=1):
        return matmul(a, layout(w[name]), layout(diff[name]), act, name, out_dtype, col_shards)

    meta = jnp.broadcast_to(diff["meta_tokens"][None], (B, N_META, D))
    h = jnp.concatenate([meta, x + token, jnp.zeros((B, Tp - T, D), F32)], axis=1).reshape(M, D)
    p = mm(h, "ev_w_in")
    sp = jax.nn.softplus(-diff["ev_lru_lambda"]).reshape(1, LRU_WIDTH)
    y_rec, qn, kvn, kpe = even_front(
        p.reshape(B, Tp, -1), diff["ev_conv_w"].reshape(CONV_WIDTH, LRU_WIDTH), diff["ev_conv_b"].reshape(1, LRU_WIDTH),
        diff["ev_w_rg_a"].reshape(LRU_HEADS, LRU_HEAD_DIM, LRU_HEAD_DIM), diff["ev_b_rg_a"].reshape(1, LRU_WIDTH),
        diff["ev_w_rg_x"].reshape(LRU_HEADS, LRU_HEAD_DIM, LRU_HEAD_DIM), diff["ev_b_rg_x"].reshape(1, LRU_WIDTH),
        sp, diff["ev_q_norm_g"].reshape(-1), diff["ev_kv_norm_g"].reshape(-1))
    y_rec = y_rec.reshape(M, LRU_WIDTH)
    q = mm(qn, "ev_w_uq", out_dtype=MXU_DTYPE, layout=_heads_to_lanes).reshape(B, Tp, -1)
    kv = mm(kvn, "ev_w_ukv", out_dtype=MXU_DTYPE).reshape(B, Tp, -1)
    y_att = mla_attention(q, kv, kpe, _mla_rope_tables(pos)).reshape(M, -1)
    return out_block(h, jnp.concatenate([y_rec, y_att], axis=-1), _out_rows_to_lanes(w["ev_w_out"]), _out_rows_to_lanes(diff["ev_w_out"]),
                     diff["ln_mix_g"], diff["ln_mix_b"], "ev_w_out")


def _mlp0(diff, h, w):
    return mlp_block(h, w["mlp_w1_0"], w["mlp_w2_0"], diff["mlp_w1_0"], diff["mlp_w2_0"], diff["ln_mlp_g"], diff["ln_mlp_b"], "mlp0")


def _layer1_loss(diff, h, w, tgt):
    B, S, T, Tp = _seq_dims(tgt)
    D = tgt.shape[-1]
    pos = jnp.arange(Tp, dtype=jnp.int32)

    cos, sin = _rope_tables(pos, RET_QK_DIM // 2)
    h = retention_block(h, w["od_w_in"], w["od_w_out"], diff["od_w_in"], diff["od_w_out"], diff["ln_mix_g"], diff["ln_mix_b"], cos, sin, (B, Tp))
    h = mlp_block(h, w["mlp_w1_1"], w["mlp_w2_1"], diff["mlp_w1_1"], diff["mlp_w2_1"], diff["ln_mlp_g"], diff["ln_mlp_b"], "mlp1")
    return loss_head(h.reshape(B, Tp, D), jnp.pad(tgt, ((0, 0), (N_META, Tp - T), (0, 0))), S)


_HBM = pl.BlockSpec(memory_space=pltpu.HBM)


def _place():
    return lax.axis_index("x"), lax.axis_index("y"), lax.axis_index("c")


def _other_chips(x, y):
    return [(1 - x, y), (x, 1 - y), (1 - x, 1 - y)]


def _chunks(rows, sublanes, most):
    for q in range(most, 0, -1):
        if rows % (q * sublanes) == 0:
            return q
    return 1


def _sublanes(dtype):
    return 8 * 4 // jnp.dtype(dtype).itemsize


def _gather_pieces(bufs):
    plan, first = [], []
    for b in bufs:
        Rh = b.shape[0] // 2
        Q = _chunks(Rh, _sublanes(b.dtype), 4) if Rh * b.shape[1] * b.dtype.itemsize > (1 << 20) else 1
        first.append(3 * sum(q for _, q, _ in plan))
        plan.append((Rh, Q, Rh // Q))
    return plan, first, 3 * sum(q for _, q, _ in plan)


def _allgather_chips(bufs, name):
    n = len(bufs)
    plan, first, n_sems = _gather_pieces(bufs)

    def body(*refs):
        x_refs, out_refs, (send_sems, recv_sems) = refs[:n], refs[n:2 * n], refs[2 * n:]
        x, y, c = _place()
        sibling = (x, y, 1 - c)
        chips = _other_chips(x, y)

        def copy(k, src, dst, to):
            return pltpu.make_async_remote_copy(src_ref=src, dst_ref=dst, send_sem=send_sems.at[k], recv_sem=recv_sems.at[k],
                                                device_id=to, device_id_type=MESH)

        def piece(i, cx, cy, hc, q):
            Rh, _, ch = plan[i]
            return out_refs[i].at[2 * cx + cy, pl.ds(hc * Rh + q * ch, ch), :]

        slots = [(i, q, j) for i in range(n) for q in range(plan[i][1]) for j in range(3)]
        sem = {(i, q, j): first[i] + 3 * q + j for i, q, j in slots}
        sent = [copy(sem[i, q, j], x_refs[i].at[pl.ds(c * plan[i][0] + q * plan[i][2], plan[i][2]), :], piece(i, x, y, c, q), (*chips[j], c))
                for i, q, j in slots]
        for cp in sent:
            cp.start()
        passed = []
        for i, q, j in slots:
            landed = piece(i, *chips[j], c, q)
            copy(sem[i, q, j], landed, landed, sibling).wait_recv()
            fwd = copy(n_sems + sem[i, q, j], landed, landed, sibling)
            fwd.start()
            passed.append(fwd)
        for i, q, j in slots:
            theirs = piece(i, *chips[j], 1 - c, q)
            copy(n_sems + sem[i, q, j], theirs, theirs, sibling).wait_recv()
        for cp in sent + passed:
            cp.wait_send()

    return pl.pallas_call(
        body, name=name, in_specs=[_HBM] * n, out_specs=[_HBM] * n,
        out_shape=[jax.ShapeDtypeStruct((N_CHIPS,) + b.shape, b.dtype) for b in bufs],
        scratch_shapes=[pltpu.SemaphoreType.DMA((2 * n_sems,)), pltpu.SemaphoreType.DMA((2 * n_sems,))],
    )(*bufs)


def _with_own(gathered, own):
    my = 2 * lax.axis_index("x") + lax.axis_index("y")
    return lax.dynamic_update_slice(gathered, own[None], (my, 0, 0))


def _sibling_gather(fs, name):
    n = len(fs)

    def body(*refs):
        out_refs, (send_sems, recv_sems) = refs[n:2 * n], refs[2 * n:]
        x, y, c = _place()
        copies = [pltpu.make_async_remote_copy(src_ref=out_ref.at[c], dst_ref=out_ref.at[c], send_sem=send_sems.at[i], recv_sem=recv_sems.at[i],
                                               device_id=(x, y, 1 - c), device_id_type=MESH) for i, out_ref in enumerate(out_refs)]
        for cp in copies:
            cp.start()
        for cp in copies:
            cp.wait()

    return pl.pallas_call(
        body, name=name, in_specs=[_HBM] * n, out_specs=[_HBM] * n,
        out_shape=[jax.ShapeDtypeStruct(f.shape, f.dtype) for f in fs], input_output_aliases={i: i for i in range(n)},
        scratch_shapes=[pltpu.SemaphoreType.DMA((n,)), pltpu.SemaphoreType.DMA((n,))],
    )(*fs)


def _axis_scalar(name):
    return lax.axis_index(name).astype(jnp.int32).reshape(1)


_SEM = pl.BlockSpec(memory_space=pltpu.SEMAPHORE)
_ANY = pl.BlockSpec(memory_space=pl.ANY)
_EFFECT = pltpu.SideEffectType.DATAFLOW_SIDE_EFFECTING


def _in_hbm(a):
    return pltpu.with_memory_space_constraint(a, pltpu.HBM)


def _half_copies(x_refs, land_refs, send_sems, recv_sems, arriving):
    x, y, c = _place()
    copies = []
    for i, (x_ref, land_ref) in enumerate(zip(x_refs, land_refs)):
        Rh = x_ref.shape[0] // 2
        rows = pl.ds(c * Rh, Rh)
        for j, (cx, cy) in enumerate(_other_chips(x, y)):
            copies.append(pltpu.make_async_remote_copy(
                src_ref=x_ref.at[rows, :], dst_ref=land_ref.at[2 * cx + cy if arriving else 2 * x + y, rows, :],
                send_sem=send_sems.at[3 * i + j], recv_sem=recv_sems.at[3 * i + j], device_id=(cx, cy, c), device_id_type=MESH))
    return copies


def _allgather_start(bufs, name):
    n = len(bufs)

    def body(*refs):
        x_refs, land_refs, (send_sems, recv_sems), token = refs[:n], refs[n:2 * n], refs[2 * n:2 * n + 2], refs[-1]
        for cp in _half_copies(x_refs, land_refs, send_sems, recv_sems, False):
            cp.start()
        token[...] = jnp.zeros_like(token)

    lands = [lax.empty((N_CHIPS,) + b.shape, b.dtype) for b in bufs]
    out = pl.pallas_call(
        body, name=name,
        out_shape=(pltpu.SemaphoreType.DMA((3 * n,)), pltpu.SemaphoreType.DMA((3 * n,)), *[pltpu.HBM(a.shape, a.dtype) for a in bufs + lands],
                   jax.ShapeDtypeStruct((8, 128), F32)),
        in_specs=[_HBM] * (2 * n), out_specs=(_SEM, _SEM, *[_HBM] * (2 * n), pl.BlockSpec(memory_space=pltpu.VMEM)),
        input_output_aliases={i: 2 + i for i in range(2 * n)}, compiler_params=pltpu.CompilerParams(has_side_effects=_EFFECT),
    )(*[_in_hbm(a) for a in bufs + lands])
    return (out[0], out[1], list(out[2:2 + n]), list(out[2 + n:2 + 2 * n])), out[-1][0, 0]


def _allgather_wait(pending, after, name):
    send_sems, recv_sems, bufs, lands = pending
    n = len(bufs)

    def body(*refs):
        x_refs, land_refs, send_sems, recv_sems = refs[:n], refs[n:2 * n], refs[2 * n], refs[2 * n + 1]
        for cp in _half_copies(x_refs, land_refs, send_sems, recv_sems, False):
            cp.wait_send()
        for cp in _half_copies(x_refs, land_refs, send_sems, recv_sems, True):
            cp.wait_recv()

    out = pl.pallas_call(
        body, name=name, out_shape=tuple(pltpu.HBM(a.shape, a.dtype) for a in bufs + lands),
        in_specs=[_HBM] * (2 * n) + [_SEM, _SEM, _ANY], out_specs=tuple([_HBM] * (2 * n)), input_output_aliases={i: i for i in range(2 * n)},
        compiler_params=pltpu.CompilerParams(has_side_effects=_EFFECT),
    )(*bufs, *lands, send_sems, recv_sems, after)
    return list(out[n:])


def _sibling_forward(lands, name):
    n = len(lands)
    plan, first, n_sems = _gather_pieces([jax.ShapeDtypeStruct(l.shape[1:], l.dtype) for l in lands])

    def body(*refs):
        out_refs, (send_sems, recv_sems) = refs[n:2 * n], refs[2 * n:]
        x, y, c = _place()

        def copies(hc):
            return [pltpu.make_async_remote_copy(
                        src_ref=out_refs[i].at[2 * cx + cy, pl.ds(hc * plan[i][0] + q * plan[i][2], plan[i][2]), :],
                        dst_ref=out_refs[i].at[2 * cx + cy, pl.ds(hc * plan[i][0] + q * plan[i][2], plan[i][2]), :],
                        send_sem=send_sems.at[first[i] + 3 * q + j], recv_sem=recv_sems.at[first[i] + 3 * q + j],
                        device_id=(x, y, 1 - c), device_id_type=MESH)
                    for i in range(n) for q in range(plan[i][1]) for j, (cx, cy) in enumerate(_other_chips(x, y))]

        mine = copies(c)
        for cp in mine:
            cp.start()
        for cp in mine:
            cp.wait_send()
        for cp in copies(1 - c):
            cp.wait_recv()

    return pl.pallas_call(
        body, name=name, in_specs=[_HBM] * n, out_specs=[_HBM] * n, out_shape=[jax.ShapeDtypeStruct(l.shape, l.dtype) for l in lands],
        input_output_aliases={i: i for i in range(n)},
        scratch_shapes=[pltpu.SemaphoreType.DMA((n_sems,)), pltpu.SemaphoreType.DMA((n_sems,))],
    )(*lands)


N_PEERS = 7


def _direct_copies(p_refs, t_refs, send_sems, recv_sems):
    x, y, c = _place()
    copies = []
    for i, (p_ref, t_ref) in enumerate(zip(p_refs, t_refs)):
        for f in range(1, N_PEERS + 1):
            px, py, pc = x ^ (f >> 2), y ^ ((f >> 1) & 1), c ^ (f & 1)
            copies.append(pltpu.make_async_remote_copy(
                src_ref=p_ref.at[2 * px + py, pc], dst_ref=t_ref.at[f - 1], send_sem=send_sems.at[N_PEERS * i + f - 1],
                recv_sem=recv_sems.at[N_PEERS * i + f - 1], device_id=(px, py, pc), device_id_type=MESH))
    return copies


def _direct_scatter_start(ps, name):
    n = len(ps)

    def body(*refs):
        p_refs, t_refs, (send_sems, recv_sems), token = refs[:n], refs[n:2 * n], refs[2 * n:2 * n + 2], refs[-1]
        for cp in _direct_copies(p_refs, t_refs, send_sems, recv_sems):
            cp.start()
        token[...] = jnp.zeros_like(token)

    lands = [lax.empty((N_PEERS,) + p.shape[2:], p.dtype) for p in ps]
    out = pl.pallas_call(
        body, name=name,
        out_shape=(pltpu.SemaphoreType.DMA((N_PEERS * n,)), pltpu.SemaphoreType.DMA((N_PEERS * n,)),
                   *[pltpu.HBM(a.shape, a.dtype) for a in ps + lands], jax.ShapeDtypeStruct((8, 128), F32)),
        in_specs=[_HBM] * (2 * n), out_specs=(_SEM, _SEM, *[_HBM] * (2 * n), pl.BlockSpec(memory_space=pltpu.VMEM)),
        input_output_aliases={i: 2 + i for i in range(2 * n)}, compiler_params=pltpu.CompilerParams(has_side_effects=_EFFECT),
    )(*[_in_hbm(a) for a in ps + lands])
    return (out[0], out[1], list(out[2:2 + n]), list(out[2 + n:2 + 2 * n])), out[-1][0, 0]


def _direct_scatter_wait(pending, after, name):
    send_sems, recv_sems, ps, lands = pending
    n = len(ps)

    def body(*refs):
        p_refs, t_refs, send_sems, recv_sems = refs[:n], refs[n:2 * n], refs[2 * n], refs[2 * n + 1]
        for cp in _direct_copies(p_refs, t_refs, send_sems, recv_sems):
            cp.wait_send()
            cp.wait_recv()

    out = pl.pallas_call(
        body, name=name, out_shape=tuple(pltpu.HBM(a.shape, a.dtype) for a in ps + lands),
        in_specs=[_HBM] * (2 * n) + [_SEM, _SEM, _ANY], out_specs=tuple([_HBM] * (2 * n)),
        input_output_aliases={i: i for i in range(2 * n)}, compiler_params=pltpu.CompilerParams(has_side_effects=_EFFECT),
    )(*ps, *lands, send_sems, recv_sems, after)
    return list(out[:n]), list(out[n:])


def _sum_direct(p, t, name):
    _, _, R, C = p.shape
    tr = _pick(R, 512, 16)

    def body(x_ref, y_ref, c_ref, p_ref, t_ref, o_ref):
        acc = p_ref[...].astype(F32)
        for f in range(N_PEERS):
            acc = acc + t_ref[f].astype(F32)
        o_ref[...] = acc

    grid_spec = pltpu.PrefetchScalarGridSpec(
        num_scalar_prefetch=3, grid=(R // tr,),
        in_specs=[pl.BlockSpec((None, None, tr, C), lambda i, x_ref, y_ref, c_ref: (2 * x_ref[0] + y_ref[0], c_ref[0], i, 0)),
                  pl.BlockSpec((N_PEERS, tr, C), lambda i, x_ref, y_ref, c_ref: (0, i, 0))],
        out_specs=pl.BlockSpec((None, tr, C), lambda i, x_ref, y_ref, c_ref: (c_ref[0], i, 0)))
    return pl.pallas_call(body, name=name, grid_spec=grid_spec, out_shape=jax.ShapeDtypeStruct((2, R, C), F32),
                          compiler_params=pltpu.CompilerParams(dimension_semantics=("parallel",)))(
        _axis_scalar("x"), _axis_scalar("y"), _axis_scalar("c"), p, t)


def _adamw(w, g, m, v, name):
    R, C = w.shape
    tr = _pick(R, 256, 8)

    def body(w_ref, g_ref, m_ref, v_ref, d_ref, nm_ref, nv_ref):
        g_ = g_ref[...]
        m_ = ADAM_B1 * m_ref[...] + (1.0 - ADAM_B1) * g_
        v_ = ADAM_B2 * v_ref[...] + (1.0 - ADAM_B2) * (g_ * g_)
        m_hat = m_ / (1.0 - ADAM_B1 ** ADAM_STEP)
        v_hat = v_ / (1.0 - ADAM_B2 ** ADAM_STEP)
        d_ref[...] = -ADAM_LR * (m_hat / (jnp.sqrt(v_hat) + ADAM_EPS) + ADAM_WD * w_ref[...])
        nm_ref[...] = m_
        nv_ref[...] = v_

    row = pl.BlockSpec((tr, C), lambda i: (i, 0))
    shp = jax.ShapeDtypeStruct((R, C), F32)
    return pl.pallas_call(body, name=name, grid=(R // tr,), in_specs=[row] * 4, out_specs=[row] * 3, out_shape=[shp] * 3,
                          compiler_params=pltpu.CompilerParams(dimension_semantics=("parallel",)))(w, g, m, v)


BIG_SPECS = (("ev_w_in", 1024, 1440, 1), ("ev_w_uq", 256, 768, 1), ("ev_w_ukv", 128, 1024, 1), ("ev_w_out", 1024, 1024, 0),
             ("od_w_in", 1024, 6144, 1), ("od_w_out", 2048, 1024, 0), ("mlp_w1_0", 1024, 4096, 1), ("mlp_w1_1", 1024, 4096, 1),
             ("mlp_w2_0", 4096, 1024, 0), ("mlp_w2_1", 4096, 1024, 0))
BIG_PARAMS = (("ev_w_in", ("ev_w_in",)), ("ev_w_uq", ("ev_w_uq",)), ("ev_w_ukv", ("ev_w_ukv",)), ("ev_w_out", ("ev_w_out",)),
              ("od_w_in", ("od_w_in",)), ("od_w_out", ("od_w_out",)), ("mlp_w1", ("mlp_w1_0", "mlp_w1_1")),
              ("mlp_w2", ("mlp_w2_0", "mlp_w2_1")))
REPLICATED = ("ev_conv_b", "ev_w_rg_a", "ev_b_rg_a", "ev_w_rg_x", "ev_b_rg_x", "ev_lru_lambda", "ev_q_norm_g", "ev_kv_norm_g",
              "ln_mix_g", "ln_mix_b", "ln_mlp_g", "ln_mlp_b")
SMALL_SHARDED = ("meta_tokens", "ev_conv_w")
COL_SHARD_GRADS = ("od_w_in", "mlp_w1_0", "mlp_w1_1")
MATRIX_GROUPS = (("ev_w_in", "ev_w_uq", "ev_w_ukv", "ev_w_out"), ("mlp_w1_0", "mlp_w2_0"), ("od_w_in", "od_w_out", "mlp_w1_1", "mlp_w2_1"))
LAYER_NORMS = ("ln_mix_g", "ln_mix_b", "ln_mlp_g", "ln_mlp_b")
WEIGHT_NAMES = ("meta_tokens", "ev_w_in", "ev_conv_w", "ev_conv_b", "ev_w_rg_a", "ev_b_rg_a", "ev_w_rg_x", "ev_b_rg_x",
                "ev_lru_lambda", "ev_q_norm_g", "ev_w_uq", "ev_kv_norm_g", "ev_w_ukv", "ev_w_out", "od_w_in", "od_w_out",
                "ln_mix_g", "ln_mix_b", "mlp_w1", "mlp_w2", "ln_mlp_g", "ln_mlp_b")


def _to_rows(flat, row_align):
    n = flat.shape[-1]
    rows = _round_up(-(-n // PACK_COLS), row_align)
    pad = rows * PACK_COLS - n
    if pad:
        flat = jnp.pad(flat, [(0, 0)] * (flat.ndim - 1) + [(0, pad)])
    return flat.reshape(flat.shape[:-1] + (rows, PACK_COLS))


def _shard_shape(K, N, axis):
    return (K // N_CHIPS, N) if axis == 0 else (K, N // N_CHIPS)


def _gather_shards(stacked, K, N, axis):
    if axis == 0:
        return stacked.reshape(K, N)
    return stacked.transpose(1, 0, 2).reshape(K, N)


def _split_shards(full, K, N, axis):
    if axis == 0:
        return full.reshape(N_CHIPS, -1)
    return full.reshape(K, N_CHIPS, N // N_CHIPS).transpose(1, 0, 2).reshape(N_CHIPS, -1)


def kernel(x, meta_tokens, ev_w_in, ev_conv_w, ev_conv_b, ev_w_rg_a, ev_b_rg_a, ev_w_rg_x, ev_b_rg_x, ev_lru_lambda, ev_q_norm_g, ev_w_uq, ev_kv_norm_g, ev_w_ukv, ev_w_out, od_w_in, od_w_out, ln_mix_g, ln_mix_b, mlp_w1, mlp_w2, ln_mlp_g, ln_mlp_b, loss_target, m_meta_tokens, m_ev_w_in, m_ev_conv_w, m_ev_conv_b, m_ev_w_rg_a, m_ev_b_rg_a, m_ev_w_rg_x, m_ev_b_rg_x, m_ev_lru_lambda, m_ev_q_norm_g, m_ev_w_uq, m_ev_kv_norm_g, m_ev_w_ukv, m_ev_w_out, m_od_w_in, m_od_w_out, m_ln_mix_g, m_ln_mix_b, m_mlp_w1, m_mlp_w2, m_ln_mlp_g, m_ln_mlp_b, v_meta_tokens, v_ev_w_in, v_ev_conv_w, v_ev_conv_b, v_ev_w_rg_a, v_ev_b_rg_a, v_ev_w_rg_x, v_ev_b_rg_x, v_ev_lru_lambda, v_ev_q_norm_g, v_ev_w_uq, v_ev_kv_norm_g, v_ev_w_ukv, v_ev_w_out, v_od_w_in, v_od_w_out, v_ln_mix_g, v_ln_mix_b, v_mlp_w1, v_mlp_w2, v_ln_mlp_g, v_ln_mlp_b):
    given = dict(locals())
    local_big = {"ev_w_in": ev_w_in[0], "ev_w_uq": ev_w_uq[0], "ev_w_ukv": ev_w_ukv[0], "ev_w_out": ev_w_out[0],
                 "od_w_in": od_w_in[0], "od_w_out": od_w_out[0], "mlp_w1_0": mlp_w1[0], "mlp_w1_1": mlp_w1[1],
                 "mlp_w2_0": mlp_w2[0], "mlp_w2_1": mlp_w2[1]}

    specs = {spec[0]: spec for spec in BIG_SPECS}
    mixer0_m, mlp0_m, layer1_m = MATRIX_GROUPS

    def shards(names):
        return [local_big[n].astype(MXU_DTYPE) for n in names]

    def whole(stacked, n):
        _, K, N, ax = specs[n]
        return stacked if n in COL_SHARD_GRADS else _gather_shards(stacked, K, N, ax)

    def filled(gathered, own, names):
        return {n: whole(_with_own(g_, o_), n) for n, g_, o_ in zip(names, gathered, own)}

    own_a, own_b, own_c = shards(mixer0_m), shards(mlp0_m), shards(layer1_m)
    small = [meta_tokens, jnp.pad(ev_conv_w[0], ((0, 16 - CONV_WIDTH), (0, 0)))]
    gathered_a = _allgather_chips(own_a + small, "weight_allgather_mixer0")
    pending_b, token1 = _allgather_start(own_b, "weight_allgather_mlp0_start")
    pending_c, token2 = _allgather_start(own_c, "weight_allgather_layer1_start")
    meta_full = _gather_shards(_with_own(gathered_a[-2], small[0]), N_META, D_MODEL, 1)
    conv_full = _gather_shards(_with_own(gathered_a[-1], small[1])[:, :CONV_WIDTH], CONV_WIDTH, LRU_WIDTH, 1)

    def slots(names, dtype):
        return {n: jnp.zeros((N_CHIPS, specs[n][1], specs[n][2] // N_CHIPS) if n in COL_SHARD_GRADS else specs[n][1:3], dtype) for n in names}

    def norms(names, layer):
        return {n: given[n][layer] for n in names}

    def finish_gather(pending, own, after, names, tag):
        landed = _allgather_wait(pending, lax.stop_gradient(after), "weight_allgather_%s_wait" % tag)
        return filled(_sibling_forward(landed, "weight_allgather_%s_forward" % tag), own, names)

    diff_a = {**slots(mixer0_m, MXU_DTYPE), **norms(("ln_mix_g", "ln_mix_b"), 0), **{n: given[n] for n in REPLICATED if n not in LAYER_NORMS},
              "x": x, "meta_tokens": meta_full, "ev_conv_w": conv_full}
    diff_b = {**slots(mlp0_m, MXU_DTYPE), **norms(("ln_mlp_g", "ln_mlp_b"), 0)}
    diff_c = {**slots(layer1_m, MXU_DTYPE), **norms(LAYER_NORMS, 1)}
    w_a = filled(gathered_a[:len(mixer0_m)], own_a, mixer0_m)
    h_a, back_a = jax.vjp(lambda d: _mixer0(d, w_a, token1 + token2), diff_a)
    w_b = finish_gather(pending_b, own_b, h_a, mlp0_m, "mlp0")
    h_b, back_b = jax.vjp(lambda d, hh: _mlp0(d, hh, w_b), diff_b, h_a)
    w_c = finish_gather(pending_c, own_c, h_b, layer1_m, "layer1")
    loss, back_c = jax.vjp(lambda d, hh: _layer1_loss(d, hh, w_c, loss_target), diff_c, h_b)
    loss = lax.psum(loss, ("x", "y", "c"))

    def blocks_of(grad, n):
        _, K, N, ax = specs[n]
        if n in COL_SHARD_GRADS:
            blocks = grad
        elif ax == 0:
            blocks = grad.reshape(N_CHIPS, K // N_CHIPS, N)
        else:
            blocks = grad.reshape(K, N_CHIPS, N // N_CHIPS).transpose(1, 0, 2)
        return blocks.reshape(N_CHIPS, 2, blocks.shape[1] // 2, blocks.shape[2])

    def start_reduce(grads_of, names, tag):
        return _direct_scatter_start([blocks_of(grads_of[n], n) for n in names], "grad_scatter_%s_start" % tag)

    g_c, dh = back_c(jnp.ones((), F32))
    flying_c, token = start_reduce(g_c, layer1_m, "layer1")
    g_b, dh = back_b(dh + token)
    flying_b, token = start_reduce(g_b, mlp0_m, "mlp0")
    (g_a,) = back_a(dh + token)

    g = {**g_a, **g_b, **g_c}
    g.update({n: jnp.stack([(g_b if n in g_b else g_a)[n], g_c[n]]) for n in LAYER_NORMS})
    repl = jnp.concatenate([g[n].reshape(-1) for n in REPLICATED]).reshape(N_CHIPS, -1)
    small = [_split_shards(g["meta_tokens"], N_META, D_MODEL, 1), _split_shards(g["ev_conv_w"], CONV_WIDTH, LRU_WIDTH, 1), repl]
    small = [pc.reshape(N_CHIPS, 2, -1) for pc in small]
    n_small = sum(pc.shape[2] for pc in small)
    small.append(jnp.zeros((N_CHIPS, 2, _round_up(n_small, 32 * PACK_COLS) - n_small), F32))
    p_small = jnp.concatenate(small, axis=2).reshape(N_CHIPS, 2, -1, PACK_COLS)
    flying_a, token = _direct_scatter_start([blocks_of(g_a[n], n) for n in mixer0_m] + [p_small], "grad_scatter_mixer0_start")
    ps_c, ts_c = _direct_scatter_wait(flying_c, g_a["x"], "grad_scatter_layer1_wait")
    ps_b, ts_b = _direct_scatter_wait(flying_b, g_a["x"], "grad_scatter_mlp0_wait")
    fs_bc = [_sum_direct(p, t, "grad_sum_%d" % i) for i, (p, t) in enumerate(zip(ps_b + ps_c, ts_b + ts_c))]
    ps_a, ts_a = _direct_scatter_wait(flying_a, fs_bc[-1], "grad_scatter_mixer0_wait")
    fs_a = [_sum_direct(p, t, "grad_sum_mixer0_%d" % i) for i, (p, t) in enumerate(zip(ps_a, ts_a))]
    reduced = _sibling_gather(fs_a + fs_bc, "grad_sibling_gather")
    red_big = dict(zip(mixer0_m + ("small",) + mlp0_m + layer1_m, reduced))
    red_small = red_big.pop("small").reshape(2, -1)

    grads = {}
    for name, parts in BIG_PARAMS:
        grads[name] = jnp.stack([red_big[part].reshape(given[name].shape[1:]) for part in parts])

    def take(off, sz):
        return jnp.concatenate([red_small[0, off // 2:(off + sz) // 2], red_small[1, off // 2:(off + sz) // 2]])

    off = 0
    for name in SMALL_SHARDED:
        sz = given[name].size
        grads[name] = take(off, sz).reshape(given[name].shape)
        off += sz
    n_repl = repl.shape[1]
    own_repl = _to_rows(take(off, n_repl), 16)
    repl_all = _with_own(_allgather_chips([own_repl], "replicated_allgather")[0], own_repl).reshape(N_CHIPS, -1)[:, :n_repl].reshape(-1)
    off = 0
    for name in REPLICATED:
        sz = given[name].size
        grads[name] = repl_all[off:off + sz].reshape(given[name].shape)
        off += sz

    delta, new_m, new_v = {}, {}, {}
    for name, _ in BIG_PARAMS:
        shp = given[name].shape
        two_d = (-1, shp[-1])
        d, nm, nv = _adamw(given[name].reshape(two_d), grads[name].reshape(two_d), given["m_" + name].reshape(two_d),
                           given["v_" + name].reshape(two_d), "adamw_" + name)
        delta[name], new_m[name], new_v[name] = d.reshape(shp), nm.reshape(shp), nv.reshape(shp)
    smalls = SMALL_SHARDED + REPLICATED

    def pack_small(get):
        return _to_rows(jnp.concatenate([get(n).reshape(-1) for n in smalls]), 8)

    outs = _adamw(pack_small(lambda n: given[n]), pack_small(lambda n: grads[n]), pack_small(lambda n: given["m_" + n]),
                  pack_small(lambda n: given["v_" + n]), "adamw_small")
    for res, flat in zip((delta, new_m, new_v), outs):
        flat, off = flat.reshape(-1), 0
        for n in smalls:
            sz = given[n].size
            res[n] = flat[off:off + sz].reshape(given[n].shape)
            off += sz

    return (loss, g_a["x"], *[grads[n] for n in WEIGHT_NAMES], *[delta[n] for n in WEIGHT_NAMES],
            *[new_m[n] for n in WEIGHT_NAMES], *[new_v[n] for n in WEIGHT_NAMES])
```

```python
import functools
import math

import jax
import jax.numpy as jnp
from jax import lax
from jax.experimental import pallas as pl
from jax.experimental.pallas import tpu as pltpu

F32 = jnp.float32
MXU_DTYPE = jnp.bfloat16

D_MODEL = 1024
N_META = 16
LRU_WIDTH = 512
LRU_HEADS = 4
LRU_HEAD_DIM = 128
CONV_WIDTH = 4
LRU_C = 8.0
MLA_HEADS = 8
MLA_NOPE = 64
MLA_ROPE = 32
MLA_V = 64
MLA_Q_RANK = 256
MLA_KV_RANK = 128
RET_HEADS = 4
RET_QK_DIM = 256
RET_V_DIM = 512
D_FF = 4096
ROPE_BASE = 10000.0
DN_ALPHA = 4.0 ** 0.25
EPS = 1e-5
NEG_INF = -1e30
SEQ_BLOCK = 128

ADAM_LR = 0.001
ADAM_B1 = 0.9
ADAM_B2 = 0.999
ADAM_EPS = 1e-08
ADAM_WD = 0.01
ADAM_STEP = 10

PACK_COLS = 1024
TN_INPUT_VMEM_BYTES = 28 << 20
N_CHIPS = 4

MESH = pl.DeviceIdType.MESH


def _pick(n, target, align):
    best = None
    for t in range(align, min(n, target) + 1, align):
        if n % t == 0:
            best = t
    return n if best is None else best


def _round_up(n, m):
    return (n + m - 1) // m * m


def _relu2(a):
    r = jnp.maximum(a, 0.0)
    return r * r


def _ln_stats(z):
    mu = jnp.mean(z, axis=-1, keepdims=True)
    zc = z - mu
    var = jnp.mean(zc * zc, axis=-1, keepdims=True)
    return zc, lax.rsqrt(var + EPS)


def _mm_nn(a, w, act, name, out_dtype=F32, norm=None):
    M, K = a.shape
    sharded = w.ndim == 3
    n = w.shape[-1]
    N = n * (w.shape[0] if sharded else 1)
    tm = _pick(M, 1088 if K * a.dtype.itemsize <= 4096 and norm is None else 544, 8)
    tn = _pick(n, 1024, 128)
    per = n // tn
    assert norm is None or tn == N

    def body(a_ref, w_ref, *rest):
        av = a_ref[...]
        if act:
            av = _relu2(av.astype(F32))
        r = jnp.dot(av.astype(MXU_DTYPE), w_ref[...].astype(MXU_DTYPE), preferred_element_type=F32)
        if norm is None:
            rest[0][...] = r.astype(out_dtype)
        else:
            r_ref, g_ref, b_ref, o_ref, z_ref = rest
            z = DN_ALPHA * r_ref[...] + r
            zc, rstd = _ln_stats(z)
            z_ref[...] = z
            o_ref[...] = zc * rstd * g_ref[...] + b_ref[...]

    w_spec = pl.BlockSpec((None, K, tn), lambda i, j: (j // per, 0, j % per)) if sharded else pl.BlockSpec((K, tn), lambda i, j: (0, j))
    tile = pl.BlockSpec((tm, tn), lambda i, j: (i, j))
    in_specs, args = [pl.BlockSpec((tm, K), lambda i, j: (i, 0)), w_spec], [a, w]
    if norm is None:
        out_specs, out_shape = tile, jax.ShapeDtypeStruct((M, N), out_dtype)
    else:
        vec = pl.BlockSpec((1, N), lambda i, j: (0, 0))
        in_specs += [tile, vec, vec]
        args += [norm[0], norm[1].reshape(1, N), norm[2].reshape(1, N)]
        out_specs, out_shape = [tile, tile], [jax.ShapeDtypeStruct((M, N), F32)] * 2
    return pl.pallas_call(
        body, name=name, grid=(M // tm, N // tn), in_specs=in_specs, out_specs=out_specs, out_shape=out_shape,
        compiler_params=pltpu.CompilerParams(dimension_semantics=("parallel", "arbitrary")),
    )(*args)


def _mm_nt(g, w, a_src, name, out_dtype=F32, plus=None):
    M, N = g.shape
    sharded = w.ndim == 3
    K, n = w.shape[-2], w.shape[-1]
    if sharded:
        tk, nk = N, 1
    else:
        tk = N if N * g.dtype.itemsize <= 8192 else _pick(N, 2048, 128)
        nk = N // tk
    tm = _pick(M, 1088 if tk * g.dtype.itemsize <= 4096 else 544, 8)
    tn = _pick(K, 1024, 128)
    has_src = a_src is not None
    assert nk == 1 or out_dtype == F32
    assert plus is None or not has_src

    def body(*refs):
        if has_src:
            g_ref, w_ref, s_ref, o_ref = refs
        elif plus is not None:
            g_ref, w_ref, p_ref, o_ref = refs
        else:
            g_ref, w_ref, o_ref = refs
        nt = (((1,), (1,)), ((), ()))
        if sharded:
            r = sum(lax.dot_general(g_ref[:, s * n:(s + 1) * n].astype(MXU_DTYPE), w_ref[s].astype(MXU_DTYPE), nt, preferred_element_type=F32)
                    for s in range(w_ref.shape[0]))
        else:
            r = lax.dot_general(g_ref[...].astype(MXU_DTYPE), w_ref[...].astype(MXU_DTYPE), nt, preferred_element_type=F32)
        if has_src:
            r = r * (2.0 * jnp.maximum(s_ref[...].astype(F32), 0.0))
        first = r if plus is None else r + DN_ALPHA * p_ref[...]
        if nk == 1:
            o_ref[...] = first.astype(out_dtype)
        else:
            k = pl.program_id(2)

            @pl.when(k == 0)
            def _():
                o_ref[...] = first

            @pl.when(k > 0)
            def _():
                o_ref[...] += r

    w_spec = (pl.BlockSpec((w.shape[0], tn, n), lambda i, j, k: (0, j, 0)) if sharded
              else pl.BlockSpec((tn, tk), lambda i, j, k: (j, k)))
    in_specs = [pl.BlockSpec((tm, tk), lambda i, j, k: (i, k)), w_spec]
    args = [g, w]
    if has_src:
        assert nk == 1
        in_specs.append(pl.BlockSpec((tm, tn), lambda i, j, k: (i, j)))
        args.append(a_src)
    if plus is not None:
        in_specs.append(pl.BlockSpec((tm, tn), lambda i, j, k: (i, j)))
        args.append(plus)
    return pl.pallas_call(
        body, name=name,
        grid=(M // tm, K // tn, nk),
        in_specs=in_specs,
        out_specs=pl.BlockSpec((tm, tn), lambda i, j, k: (i, j)),
        out_shape=jax.ShapeDtypeStruct((M, K), out_dtype),
        compiler_params=pltpu.CompilerParams(dimension_semantics=("parallel", "parallel", "arbitrary")),
    )(*args)


def _mm_tn(a, g, act, name, col_shards=1, out_dtype=F32):
    M, K = a.shape
    _, N = g.shape
    n = N // col_shards
    tm, tn = _pick(K, 1024, 128), _pick(n, 1024, 128)
    row_bytes = tm * a.dtype.itemsize + tn * g.dtype.itemsize
    tk = _pick(M, min(2176, TN_INPUT_VMEM_BYTES // (2 * row_bytes)), 8)
    nk = M // tk
    per = n // tn
    direct = out_dtype == F32

    def body(a_ref, g_ref, o_ref, *scratch):
        acc_ref = o_ref if direct else scratch[0]
        k = pl.program_id(2)
        av = a_ref[...]
        if act:
            av = _relu2(av.astype(F32))
        r = lax.dot_general(av.astype(MXU_DTYPE), g_ref[...].astype(MXU_DTYPE),
                            (((0,), (0,)), ((), ())), preferred_element_type=F32)

        @pl.when(k == 0)
        def _():
            acc_ref[...] = r

        @pl.when(k > 0)
        def _():
            acc_ref[...] += r

        if not direct:
            @pl.when(k == nk - 1)
            def _():
                o_ref[...] = acc_ref[...].astype(out_dtype)

    if col_shards == 1:
        out_spec, out_shape = pl.BlockSpec((tm, tn), lambda i, j, k: (i, j)), (K, N)
    else:
        out_spec, out_shape = pl.BlockSpec((None, tm, tn), lambda i, j, k: (j // per, i, j % per)), (col_shards, K, n)
    return pl.pallas_call(
        body, name=name,
        grid=(K // tm, N // tn, nk),
        in_specs=[pl.BlockSpec((tk, tm), lambda i, j, k: (k, i)), pl.BlockSpec((tk, tn), lambda i, j, k: (k, j))],
        out_specs=out_spec,
        out_shape=jax.ShapeDtypeStruct(out_shape, out_dtype),
        scratch_shapes=[] if direct else [pltpu.VMEM((tm, tn), F32)],
        compiler_params=pltpu.CompilerParams(dimension_semantics=("parallel", "parallel", "arbitrary")),
    )(a, g)


@functools.partial(jax.custom_vjp, nondiff_argnums=(3, 4, 5, 6))
def matmul(a, w, w_grad_slot, act, name, out_dtype, col_shards):
    return _mm_nn(a, w, act, name + "_fwd", out_dtype)


def _matmul_fwd(a, w, w_grad_slot, act, name, out_dtype, col_shards):
    return _mm_nn(a, w, act, name + "_fwd", out_dtype), (a, w, jnp.zeros((), w_grad_slot.dtype))


def _matmul_bwd(act, name, out_dtype, col_shards, res, g):
    a, w, slot_like = res
    w_grad_dtype = slot_like.dtype
    da = _mm_nt(g, w, a if act else None, name + "_dx")
    dw = _mm_tn(a, g, act, name + "_dw", col_shards, w_grad_dtype)
    return da, None, dw


matmul.defvjp(_matmul_fwd, _matmul_bwd)


def _ln_bwd_call(z, g, dy, name):
    M, D = z.shape
    tm = _pick(M, 544, 8)

    def body(z_ref, g_ref, dy_ref, dz_ref, dg_ref, db_ref):
        @pl.when(pl.program_id(0) == 0)
        def _():
            dg_ref[...] = jnp.zeros_like(dg_ref)
            db_ref[...] = jnp.zeros_like(db_ref)

        zc, rstd = _ln_stats(z_ref[...])
        xhat = zc * rstd
        dy = dy_ref[...]
        dxh = dy * g_ref[...]
        m1 = jnp.mean(dxh, axis=-1, keepdims=True)
        m2 = jnp.mean(dxh * xhat, axis=-1, keepdims=True)
        dz_ref[...] = rstd * (dxh - m1 - xhat * m2)
        dg_ref[...] += jnp.sum(dy * xhat, axis=0, keepdims=True)
        db_ref[...] += jnp.sum(dy, axis=0, keepdims=True)

    row = pl.BlockSpec((tm, D), lambda i: (i, 0))
    vec = pl.BlockSpec((1, D), lambda i: (0, 0))
    return pl.pallas_call(
        body, name=name, grid=(M // tm,), in_specs=[row, vec, row], out_specs=[row, vec, vec],
        out_shape=[jax.ShapeDtypeStruct((M, D), F32), jax.ShapeDtypeStruct((1, D), F32), jax.ShapeDtypeStruct((1, D), F32)],
        compiler_params=pltpu.CompilerParams(dimension_semantics=("arbitrary",)),
    )(z, g.reshape(1, D), dy)


@functools.partial(jax.custom_vjp, nondiff_argnums=(7,))
def mlp_block(h, w1, w2, w1_grad_slot, w2_grad_slot, g, b, name):
    return _mlp_block_fwd(h, w1, w2, w1_grad_slot, w2_grad_slot, g, b, name)[0]


def _mlp_block_fwd(h, w1, w2, w1_grad_slot, w2_grad_slot, g, b, name):
    u = _mm_nn(h, w1, False, name + "_w1_fwd", out_dtype=MXU_DTYPE)
    out, z = _mm_nn(u, w2, True, name + "_w2_norm_fwd", norm=(h, g, b))
    return out, (h, u, z, w1, w2, g, jnp.zeros((), w1_grad_slot.dtype))


def _mlp_block_bwd(name, res, dy):
    h, u, z, w1, w2, g, slot_like = res
    dz, dg, db = _ln_bwd_call(z, g, dy, name + "_norm_bwd")
    du = _mm_nt(dz, w2, u, name + "_w2_dx", out_dtype=MXU_DTYPE)
    dw2 = _mm_tn(u, dz, True, name + "_w2_dw", 1, slot_like.dtype)
    dh = _mm_nt(du, w1, None, name + "_w1_dx", plus=dz)
    dw1 = _mm_tn(h, du, False, name + "_w1_dw", N_CHIPS, slot_like.dtype)
    return dh, None, None, dw1, dw2, dg.reshape(g.shape), db.reshape(g.shape)


mlp_block.defvjp(_mlp_block_fwd, _mlp_block_bwd)


@functools.partial(jax.custom_vjp, nondiff_argnums=(6,))
def out_block(h, y, w, w_grad_slot, g, b, name):
    return _out_block_fwd(h, y, w, w_grad_slot, g, b, name)[0]


def _out_block_fwd(h, y, w, w_grad_slot, g, b, name):
    out, z = _mm_nn(y, w, False, name + "_norm_fwd", norm=(h, g, b))
    return out, (y, z, w, g, jnp.zeros((), w_grad_slot.dtype))


def _out_block_bwd(name, res, dy):
    y, z, w, g, slot_like = res
    dz, dg, db = _ln_bwd_call(z, g, dy, name + "_norm_bwd")
    d_y = _mm_nt(dz, w, None, name + "_dx")
    dw = _mm_tn(y, dz, False, name + "_dw", 1, slot_like.dtype)
    return DN_ALPHA * dz, d_y, None, dw, dg.reshape(g.shape), db.reshape(g.shape)


out_block.defvjp(_out_block_fwd, _out_block_bwd)


def _rms_fwd_call(x, g, name, col_block=0):
    R = x.shape[0]
    W = g.shape[-1]
    tr = _pick(R, 1088, 8)

    def body(x_ref, g_ref, o_ref):
        xv = x_ref[...]
        rstd = lax.rsqrt(jnp.mean(xv * xv, axis=-1, keepdims=True) + EPS)
        o_ref[...] = xv * rstd * g_ref[...]

    vec = pl.BlockSpec((1, W), lambda i: (0, 0))
    return pl.pallas_call(
        body, name=name, grid=(R // tr,), in_specs=[pl.BlockSpec((tr, W), lambda i: (i, col_block)), vec],
        out_specs=pl.BlockSpec((tr, W), lambda i: (i, 0)), out_shape=jax.ShapeDtypeStruct((R, W), F32),
        compiler_params=pltpu.CompilerParams(dimension_semantics=("parallel",)),
    )(x, g.reshape(1, W))


def _rms_bwd_call(x, g, dy, name, col_block=0):
    R = x.shape[0]
    W = g.shape[-1]
    tr = _pick(R, 1088, 8)

    def body(x_ref, g_ref, dy_ref, dx_ref, dg_ref):
        @pl.when(pl.program_id(0) == 0)
        def _():
            dg_ref[...] = jnp.zeros_like(dg_ref)

        xv = x_ref[...]
        rstd = lax.rsqrt(jnp.mean(xv * xv, axis=-1, keepdims=True) + EPS)
        xhat = xv * rstd
        dy = dy_ref[...]
        dxh = dy * g_ref[...]
        dx_ref[...] = rstd * (dxh - xhat * jnp.mean(dxh * xhat, axis=-1, keepdims=True))
        dg_ref[...] += jnp.sum(dy * xhat, axis=0, keepdims=True)

    row = pl.BlockSpec((tr, W), lambda i: (i, 0))
    vec = pl.BlockSpec((1, W), lambda i: (0, 0))
    return pl.pallas_call(
        body, name=name, grid=(R // tr,), in_specs=[pl.BlockSpec((tr, W), lambda i: (i, col_block)), vec, row], out_specs=[row, vec],
        out_shape=[jax.ShapeDtypeStruct((R, W), F32), jax.ShapeDtypeStruct((1, W), F32)],
        compiler_params=pltpu.CompilerParams(dimension_semantics=("arbitrary",)),
    )(x, g.reshape(1, W), dy)


def _loss_call(h, tgt, n_tokens, name):
    B, Tp, D = h.shape
    tr = _pick(Tp, 544, 8)

    def body(y_ref, t_ref, dy_ref, acc_ref):
        @pl.when(jnp.logical_and(pl.program_id(0) == 0, pl.program_id(1) == 0))
        def _():
            acc_ref[...] = jnp.zeros_like(acc_ref)

        t = lax.broadcasted_iota(jnp.int32, (tr, 1), 0) + pl.program_id(1) * tr
        counts = jnp.logical_and(t >= N_META, t < N_META + n_tokens)
        e = jnp.where(counts, y_ref[...] - t_ref[...], 0.0)
        dy_ref[...] = e * (1.0 / D)
        acc_ref[...] += jnp.sum(jnp.sum(e * e, axis=-1, keepdims=True), axis=0, keepdims=True) * (0.5 / D)

    row = pl.BlockSpec((None, tr, D), lambda b, i: (b, i, 0))
    one = pl.BlockSpec((1, 1), lambda b, i: (0, 0))
    return pl.pallas_call(
        body, name=name, grid=(B, Tp // tr), in_specs=[row, row], out_specs=[row, one],
        out_shape=[jax.ShapeDtypeStruct((B, Tp, D), F32), jax.ShapeDtypeStruct((1, 1), F32)],
        compiler_params=pltpu.CompilerParams(dimension_semantics=("arbitrary", "arbitrary")),
    )(h, tgt)


@functools.partial(jax.custom_vjp, nondiff_argnums=(2,))
def loss_head(h, tgt, n_tokens):
    return _loss_call(h, tgt, n_tokens, "loss_head")[1][0, 0]


def _loss_head_fwd(h, tgt, n_tokens):
    dy, acc = _loss_call(h, tgt, n_tokens, "loss_head")
    return acc[0, 0], dy


def _loss_head_bwd(n_tokens, dy, ct):
    return ct * dy, None


loss_head.defvjp(_loss_head_fwd, _loss_head_bwd)


_GELU_C = math.sqrt(2.0 / math.pi)


def _gelu_parts(x):
    x2 = x * x
    t = jnp.tanh(_GELU_C * (x + 0.044715 * x * x2))
    gelu = 0.5 * x * (1.0 + t)
    dgelu = 0.5 * (1.0 + t) + 0.5 * x * (1.0 - t * t) * (_GELU_C * (1.0 + 3.0 * 0.044715 * x2))
    return gelu, dgelu


def _sigmoid(x):
    return 1.0 / (1.0 + jnp.exp(-x))


def _scan8(a, b, carry, reverse):
    row = lax.broadcasted_iota(jnp.int32, a.shape, 0)
    for s in (1, 2, 4):
        shift = 8 - s if reverse else s
        keep = (row < 8 - s) if reverse else (row >= s)
        b = jnp.where(keep, a * pltpu.roll(b, shift, 0) + b, b)
        a = jnp.where(keep, a * pltpu.roll(a, shift, 0), a)
    return a * carry + b


def _lru_pre(prec_ref, prev_ref, first, cw_ref, cb_ref, wa_ref, ba_ref, wx_ref, bx_ref, sp_ref):
    tc = prec_ref.shape[0]
    prev = jnp.where(first, 0.0, prev_ref[...])
    ext = jnp.concatenate([prev, prec_ref[...]], axis=0)
    cw = cw_ref[...]
    taps = [ext[8:] if k == CONV_WIDTH - 1 else pltpu.roll(ext, CONV_WIDTH - 1 - k, 0)[8:] for k in range(CONV_WIDTH)]
    xc = cb_ref[...] + sum(cw[k:k + 1, :] * taps[k] for k in range(CONV_WIDTH))
    ga, gx = [], []
    for h in range(LRU_HEADS):
        xh = xc[:, h * LRU_HEAD_DIM:(h + 1) * LRU_HEAD_DIM].astype(MXU_DTYPE)
        ga.append(jnp.dot(xh, wa_ref[h].astype(MXU_DTYPE), preferred_element_type=F32))
        gx.append(jnp.dot(xh, wx_ref[h].astype(MXU_DTYPE), preferred_element_type=F32))
    r = _sigmoid(jnp.concatenate(ga, axis=1) + ba_ref[...])
    i = _sigmoid(jnp.concatenate(gx, axis=1) + bx_ref[...])
    log_a = -LRU_C * r * sp_ref[...]
    a = jnp.exp(log_a)
    a2 = a * a
    mult = jnp.sqrt(-jnp.tanh(log_a) * (a2 + 1.0))
    return taps, xc, r, i, a, a2, mult


def _lru_fwd_call(p, cw, cb, wa, ba, wx, bx, sp):
    B, Tp, _ = p.shape
    W = LRU_WIDTH
    tc = SEQ_BLOCK
    nc = Tp // tc

    def body(pg_ref, prec_ref, prev_ref, cw_ref, cb_ref, wa_ref, ba_ref, wx_ref, bx_ref, sp_ref, y_ref, h_ref, carry_ref):
        first = pl.program_id(1) == 0

        @pl.when(first)
        def _():
            carry_ref[...] = jnp.zeros_like(carry_ref)

        _, xc, r, i, a, a2, mult = _lru_pre(prec_ref, prev_ref, first, cw_ref, cb_ref, wa_ref, ba_ref, wx_ref, bx_ref, sp_ref)
        b = mult * (i * xc)
        carry = carry_ref[0:1, :]
        for t in range(tc // 8):
            h = _scan8(a[8 * t:8 * t + 8], b[8 * t:8 * t + 8], carry, False)
            h_ref[8 * t:8 * t + 8, :] = h
            carry = h[7:8, :]
        carry_ref[...] = jnp.broadcast_to(carry, carry_ref.shape)
        y_ref[...] = h_ref[...] * _gelu_parts(pg_ref[...])[0]

    cur = pl.BlockSpec((None, tc, W), lambda b, j: (b, j, 0))
    rec = pl.BlockSpec((None, tc, W), lambda b, j: (b, j, 1))
    prev = pl.BlockSpec((None, 8, W), lambda b, j: (b, jnp.maximum(j * (tc // 8) - 1, 0), 1))
    vec = pl.BlockSpec((1, W), lambda b, j: (0, 0))
    cws = pl.BlockSpec((CONV_WIDTH, W), lambda b, j: (0, 0))
    wsp = pl.BlockSpec((LRU_HEADS, LRU_HEAD_DIM, LRU_HEAD_DIM), lambda b, j: (0, 0, 0))
    return pl.pallas_call(
        body, name="lru_fwd", grid=(B, nc),
        in_specs=[cur, rec, prev, cws, vec, wsp, vec, wsp, vec, vec],
        out_specs=[cur, cur],
        out_shape=[jax.ShapeDtypeStruct((B, Tp, W), F32), jax.ShapeDtypeStruct((B, Tp, W), F32)],
        scratch_shapes=[pltpu.VMEM((8, W), F32)],
        compiler_params=pltpu.CompilerParams(dimension_semantics=("arbitrary", "arbitrary")),
    )(p, p, p, cw, cb, wa, ba, wx, bx, sp)


def _lru_bwd_call(p, hseq, dy, cw, cb, wa, ba, wx, bx, sp, dpq, dpkv, dkpe):
    B, Tp, P = p.shape
    W = LRU_WIDTH
    tc = SEQ_BLOCK
    nc = Tp // tc
    HD = LRU_HEAD_DIM

    def body(pg_ref, prec_ref, prev_ref, h_ref, hprev_ref, dy_ref, cw_ref, cb_ref, wa_ref, ba_ref, wx_ref, bx_ref, sp_ref,
             dpq_ref, dpkv_ref, dkpe_ref, dp_ref, dcw_ref, dcb_ref, dwa_ref, dba_ref, dwx_ref, dbx_ref, dsp_ref,
             gcar_ref, anext_ref, halo_ref, g_ref):
        j = pl.program_id(1)
        first = j == nc - 1
        last = j == 0

        @pl.when(jnp.logical_and(pl.program_id(0) == 0, last))
        def _():
            for ref in (dcw_ref, dcb_ref, dwa_ref, dba_ref, dwx_ref, dbx_ref, dsp_ref):
                ref[...] = jnp.zeros_like(ref)

        @pl.when(last)
        def _():
            gcar_ref[...] = jnp.zeros_like(gcar_ref)
            anext_ref[...] = jnp.zeros_like(anext_ref)
            halo_ref[...] = jnp.zeros_like(halo_ref)

        taps, xc, r, i, a, a2, mult = _lru_pre(prec_ref, prev_ref, first, cw_ref, cb_ref, wa_ref, ba_ref, wx_ref, bx_ref, sp_ref)
        row = lax.broadcasted_iota(jnp.int32, (tc, W), 0)
        gelu, dgelu = _gelu_parts(pg_ref[...])
        dy = dy_ref[...]
        hcur = h_ref[...]
        dp_ref[:, 0:W] = dy * hcur * dgelu
        dp_ref[:, 2 * W:2 * W + MLA_Q_RANK] = dpq_ref[...]
        dp_ref[:, _KPE_START - MLA_KV_RANK:_KPE_START] = dpkv_ref[...]
        dp_ref[:, _KPE_START:P] = pltpu.roll(dkpe_ref[...], HEAD_LANES - MLA_NOPE, 1)[:, 0:P - _KPE_START]
        dh = dy * gelu
        a_next = jnp.where(row == tc - 1, anext_ref[0:1, :], pltpu.roll(a, tc - 1, 0))
        carry = gcar_ref[0:1, :]
        for t in reversed(range(tc // 8)):
            g = _scan8(a_next[8 * t:8 * t + 8], dh[8 * t:8 * t + 8], carry, True)
            g_ref[8 * t:8 * t + 8, :] = g
            carry = g[0:1, :]
        gcar_ref[...] = jnp.broadcast_to(carry, gcar_ref.shape)
        anext_ref[...] = jnp.broadcast_to(a[0:1, :], anext_ref.shape)
        G = g_ref[...]
        h_before = jnp.where(first, 0.0, hprev_ref[7:8, :])
        hprev = jnp.where(row == 0, h_before, pltpu.roll(hcur, 1, 0))
        d_a = G * hprev
        gx_ = G * xc
        d_mult = gx_ * i
        d_i = gx_ * mult
        dxc = G * (mult * i)
        d_la = d_a * a - d_mult * (a2 / mult)
        sp = sp_ref[...]
        d_r = d_la * (-LRU_C * sp)
        dsp_ref[...] += jnp.sum(d_la * (-LRU_C * r), axis=0, keepdims=True)
        dga = d_r * r * (1.0 - r)
        dgx = d_i * i * (1.0 - i)
        dba_ref[...] += jnp.sum(dga, axis=0, keepdims=True)
        dbx_ref[...] += jnp.sum(dgx, axis=0, keepdims=True)
        back = []
        for h in range(LRU_HEADS):
            sl = slice(h * HD, (h + 1) * HD)
            xh = xc[:, sl].astype(MXU_DTYPE)
            ah = dga[:, sl].astype(MXU_DTYPE)
            bh = dgx[:, sl].astype(MXU_DTYPE)
            tn = (((0,), (0,)), ((), ()))
            nt = (((1,), (1,)), ((), ()))
            dwa_ref[h] += lax.dot_general(xh, ah, tn, preferred_element_type=F32)
            dwx_ref[h] += lax.dot_general(xh, bh, tn, preferred_element_type=F32)
            back.append(lax.dot_general(ah, wa_ref[h].astype(MXU_DTYPE), nt, preferred_element_type=F32)
                        + lax.dot_general(bh, wx_ref[h].astype(MXU_DTYPE), nt, preferred_element_type=F32))
        dxc = dxc + jnp.concatenate(back, axis=1)
        dcb_ref[...] += jnp.sum(dxc, axis=0, keepdims=True)
        for k in range(CONV_WIDTH):
            dcw_ref[k:k + 1, :] += jnp.sum(dxc * taps[k], axis=0, keepdims=True)
        ext = jnp.concatenate([dxc, halo_ref[...]], axis=0)
        cw = cw_ref[...]
        acc = cw[CONV_WIDTH - 1:CONV_WIDTH, :] * dxc
        for k in range(CONV_WIDTH - 1):
            s = CONV_WIDTH - 1 - k
            acc = acc + cw[k:k + 1, :] * pltpu.roll(ext, tc + 8 - s, 0)[:tc]
        dp_ref[:, W:2 * W] = acc
        halo_ref[...] = dxc[0:8, :]

    rev = lambda j: nc - 1 - j
    cur = pl.BlockSpec((None, tc, W), lambda b, j: (b, rev(j), 0))
    rec = pl.BlockSpec((None, tc, W), lambda b, j: (b, rev(j), 1))
    prev = pl.BlockSpec((None, 8, W), lambda b, j: (b, jnp.maximum(rev(j) * (tc // 8) - 1, 0), 0))
    prev_rec = pl.BlockSpec((None, 8, W), lambda b, j: (b, jnp.maximum(rev(j) * (tc // 8) - 1, 0), 1))
    vec = pl.BlockSpec((1, W), lambda b, j: (0, 0))
    cws = pl.BlockSpec((CONV_WIDTH, W), lambda b, j: (0, 0))
    wsp = pl.BlockSpec((LRU_HEADS, HD, HD), lambda b, j: (0, 0, 0))
    vs = jax.ShapeDtypeStruct((1, W), F32)
    ws = jax.ShapeDtypeStruct((LRU_HEADS, HD, HD), F32)

    def rows(width):
        return pl.BlockSpec((None, tc, width), lambda b, j: (b, rev(j), 0))

    return pl.pallas_call(
        body, name="lru_bwd", grid=(B, nc),
        in_specs=[cur, rec, prev_rec, cur, prev, cur, cws, vec, wsp, vec, wsp, vec, vec, rows(MLA_Q_RANK), rows(MLA_KV_RANK), rows(HEAD_LANES)],
        out_specs=[rows(P), cws, vec, wsp, vec, wsp, vec, vec],
        out_shape=[jax.ShapeDtypeStruct((B, Tp, P), F32), jax.ShapeDtypeStruct((CONV_WIDTH, W), F32), vs, ws, vs, ws, vs, vs],
        scratch_shapes=[pltpu.VMEM((8, W), F32), pltpu.VMEM((8, W), F32), pltpu.VMEM((8, W), F32), pltpu.VMEM((tc, W), F32)],
        compiler_params=pltpu.CompilerParams(dimension_semantics=("arbitrary", "arbitrary")),
    )(p, p, p, hseq, hseq, dy, cw, cb, wa, ba, wx, bx, sp, dpq, dpkv, dkpe)


_Q_BLOCK = 2 * LRU_WIDTH // MLA_Q_RANK
_KV_BLOCK = (2 * LRU_WIDTH + MLA_Q_RANK) // MLA_KV_RANK
_KPE_START = 2 * LRU_WIDTH + MLA_Q_RANK + MLA_KV_RANK


@jax.custom_vjp
def even_front(p, cw, cb, wa, ba, wx, bx, sp, gq, gkv):
    return _even_front_fwd(p, cw, cb, wa, ba, wx, bx, sp, gq, gkv)[0]


def _even_front_fwd(p, cw, cb, wa, ba, wx, bx, sp, gq, gkv):
    B, Tp, W = p.shape
    p2d = p.reshape(B * Tp, W)
    y, hseq = _lru_fwd_call(p, cw, cb, wa, ba, wx, bx, sp)
    qn = _rms_fwd_call(p2d, gq, "q_norm_fwd", _Q_BLOCK)
    kvn = _rms_fwd_call(p2d, gkv, "kv_norm_fwd", _KV_BLOCK)
    kpe = jnp.pad(p[:, :, _KPE_START:], ((0, 0), (0, 0), (MLA_NOPE, HEAD_LANES - MLA_NOPE - MLA_ROPE)))
    return (y, qn, kvn, kpe), (p, hseq, cw, cb, wa, ba, wx, bx, sp, gq, gkv)


def _even_front_bwd(res, cts):
    p, hseq, cw, cb, wa, ba, wx, bx, sp, gq, gkv = res
    dy, dqn, dkvn, dkpe = cts
    B, Tp, W = p.shape
    p2d = p.reshape(B * Tp, W)
    dpq, dgq = _rms_bwd_call(p2d, gq, dqn, "q_norm_bwd", _Q_BLOCK)
    dpkv, dgkv = _rms_bwd_call(p2d, gkv, dkvn, "kv_norm_bwd", _KV_BLOCK)
    dp, dcw, dcb, dwa, dba, dwx, dbx, dsp = _lru_bwd_call(p, hseq, dy, cw, cb, wa, ba, wx, bx, sp, dpq.reshape(B, Tp, -1),
                                                          dpkv.reshape(B, Tp, -1), dkpe)
    return dp, dcw, dcb, dwa, dba, dwx, dbx, dsp, dgq.reshape(gq.shape), dgkv.reshape(gkv.shape)


even_front.defvjp(_even_front_fwd, _even_front_bwd)


def _rope_tables(pos, half):
    inv = ROPE_BASE ** (-jnp.arange(half, dtype=F32) / half)
    ang = pos.astype(F32)[:, None] * inv[None, :]
    return jnp.cos(ang), jnp.sin(ang)


_NT = (((1,), (1,)), ((), ()))
_TN = (((0,), (0,)), ((), ()))
HEAD_LANES = 128
_MLA_SCALE = (MLA_NOPE + MLA_ROPE) ** -0.5
_LOG2E = math.log2(math.e)


Q_BLOCK = 512


def _query_blocks(Tp):
    first = Tp % Q_BLOCK or Q_BLOCK
    return [(0, first)] + [(r, r + Q_BLOCK) for r in range(first, Tp, Q_BLOCK)]


def _mask_diagonal(s, fill):
    R, L = s.shape
    row = lax.broadcasted_iota(jnp.int32, (R, R), 0)
    col = lax.broadcasted_iota(jnp.int32, (R, R), 1)
    last = jnp.where(col <= row, s[:, L - R:], fill)
    return last if L == R else jnp.concatenate([s[:, :L - R], last], axis=1)


def _mla_rope_tables(pos):
    half = MLA_ROPE // 2
    cos, sin = _rope_tables(pos, half)
    T = pos.shape[0]
    ones, zeros = jnp.ones((T, MLA_NOPE), F32), jnp.zeros((T, MLA_NOPE), F32)
    tail1, tail0 = jnp.ones((T, HEAD_LANES - MLA_NOPE - MLA_ROPE), F32), jnp.zeros((T, HEAD_LANES - MLA_NOPE - MLA_ROPE), F32)
    zh = jnp.zeros((T, half), F32)
    c = jnp.concatenate([ones, cos, cos, tail1], axis=1)
    s_up = jnp.concatenate([zeros, -sin, zh, tail0], axis=1)
    s_down = jnp.concatenate([zeros, zh, sin, tail0], axis=1)
    return c, s_up, s_down


def _rope_lanes(x, c, s_up, s_down):
    half = MLA_ROPE // 2
    return x * c + pltpu.roll(x, HEAD_LANES - half, 1) * s_up + pltpu.roll(x, half, 1) * s_down


def _unrope_lanes(d, c, s_up, s_down):
    half = MLA_ROPE // 2
    return d * c + pltpu.roll(d * s_up, half, 1) + pltpu.roll(d * s_down, HEAD_LANES - half, 1)


def _mla_operands(q_ref, kv_ref, kpe_ref, c, s_up, s_down):
    lane = lax.broadcasted_iota(jnp.int32, kv_ref.shape, 1)
    qr = (_rope_lanes(q_ref[...].astype(F32), c, s_up, s_down) * (_MLA_SCALE * _LOG2E)).astype(MXU_DTYPE)
    kr = jnp.where(lane < MLA_NOPE, kv_ref[...].astype(F32), _rope_lanes(kpe_ref[...], c, s_up, s_down)).astype(MXU_DTYPE)
    return qr, kr, lane


def _mla_specs(Tp):
    head = pl.BlockSpec((None, Tp, HEAD_LANES), lambda b, h: (b, 0, h))
    shared = pl.BlockSpec((None, Tp, HEAD_LANES), lambda b, h: (b, 0, 0))
    tab = pl.BlockSpec((Tp, HEAD_LANES), lambda b, h: (0, 0))
    lse = pl.BlockSpec((None, None, Tp, 1), lambda b, h: (b, h, 0, 0))
    return head, shared, tab, lse


def _attn_fwd_call(q, kv, kpe, tabs):
    B, Tp, _ = q.shape

    def body(q_ref, kv_ref, kpe_ref, c_ref, su_ref, sd_ref, o_ref, lse_ref, qr_ref, kr_ref):
        qr, kr, lane = _mla_operands(q_ref, kv_ref, kpe_ref, c_ref[...], su_ref[...], sd_ref[...])
        qr_ref[...] = qr
        kr_ref[...] = kr
        for r0, L in _query_blocks(Tp):
            blk = slice(r0, L)
            s = _mask_diagonal(lax.dot_general(qr_ref[blk, :], kr_ref[0:L, :], _NT, preferred_element_type=F32), NEG_INF)
            m = jnp.max(s, axis=-1, keepdims=True)
            p = jnp.exp2(s - m)
            l = jnp.sum(p, axis=-1, keepdims=True)
            o = jnp.dot(p.astype(MXU_DTYPE), kv_ref[0:L, :].astype(MXU_DTYPE), preferred_element_type=F32)
            o_ref[blk, :] = jnp.where(lane[blk, :] >= MLA_NOPE, o / l, 0.0)
            lse_ref[blk, :] = m + jnp.log2(l)

    head, shared, tab, lse = _mla_specs(Tp)
    return pl.pallas_call(
        body, name="mla_attn_fwd", grid=(B, MLA_HEADS), in_specs=[head, head, shared, tab, tab, tab], out_specs=[head, lse],
        out_shape=[jax.ShapeDtypeStruct((B, Tp, MLA_HEADS * HEAD_LANES), F32), jax.ShapeDtypeStruct((B, MLA_HEADS, Tp, 1), F32)],
        scratch_shapes=[pltpu.VMEM((Tp, HEAD_LANES), MXU_DTYPE), pltpu.VMEM((Tp, HEAD_LANES), MXU_DTYPE)],
        compiler_params=pltpu.CompilerParams(dimension_semantics=("parallel", "parallel")),
    )(q, kv, kpe, *tabs)


def _attn_bwd_call(q, kv, kpe, tabs, o, lse, do):
    B, Tp, _ = q.shape

    def body(q_ref, kv_ref, kpe_ref, c_ref, su_ref, sd_ref, o_ref, lse_ref, do_ref, dq_ref, dkv_ref, dkpe_ref,
             qr_ref, kr_ref, dqa_ref, dka_ref, dva_ref):
        c, s_up, s_down = c_ref[...], su_ref[...], sd_ref[...]
        qr, kr, lane = _mla_operands(q_ref, kv_ref, kpe_ref, c, s_up, s_down)
        qr_ref[...] = qr
        kr_ref[...] = kr
        dka_ref[...] = jnp.zeros_like(dka_ref)
        dva_ref[...] = jnp.zeros_like(dva_ref)
        for r0, L in _query_blocks(Tp):
            blk = slice(r0, L)
            qb = qr_ref[blk, :]
            do = jnp.where(lane[blk, :] >= MLA_NOPE, do_ref[blk, :], 0.0)
            delta = jnp.sum(do * o_ref[blk, :], axis=-1, keepdims=True)
            s = _mask_diagonal(lax.dot_general(qb, kr_ref[0:L, :], _NT, preferred_element_type=F32), NEG_INF)
            p = jnp.exp2(s - lse_ref[blk, :])
            dob = do.astype(MXU_DTYPE)
            dva_ref[0:L, :] += lax.dot_general(p.astype(MXU_DTYPE), dob, _TN, preferred_element_type=F32)
            dp = lax.dot_general(dob, kv_ref[0:L, :].astype(MXU_DTYPE), _NT, preferred_element_type=F32)
            ds = (p * (dp - delta)).astype(MXU_DTYPE)
            dqa_ref[blk, :] = jnp.dot(ds, kr_ref[0:L, :], preferred_element_type=F32)
            dka_ref[0:L, :] += lax.dot_general(ds, qb, _TN, preferred_element_type=F32)
        dq_ref[...] = _unrope_lanes(dqa_ref[...] * _MLA_SCALE, c, s_up, s_down).astype(dq_ref.dtype)
        dk = dka_ref[...] * (1.0 / _LOG2E)
        dkv_ref[...] = jnp.where(lane < MLA_NOPE, dk, dva_ref[...]).astype(dkv_ref.dtype)
        dkpe = jnp.where(lane >= MLA_NOPE, _unrope_lanes(dk, c, s_up, s_down), 0.0)

        @pl.when(pl.program_id(1) == 0)
        def _():
            dkpe_ref[...] = dkpe

        @pl.when(pl.program_id(1) > 0)
        def _():
            dkpe_ref[...] += dkpe

    head, shared, tab, lse_spec = _mla_specs(Tp)
    wide = jax.ShapeDtypeStruct((B, Tp, MLA_HEADS * HEAD_LANES), q.dtype)
    acc = pltpu.VMEM((Tp, HEAD_LANES), F32)
    return pl.pallas_call(
        body, name="mla_attn_bwd", grid=(B, MLA_HEADS),
        in_specs=[head, head, shared, tab, tab, tab, head, lse_spec, head], out_specs=[head, head, shared],
        out_shape=[wide, wide, jax.ShapeDtypeStruct((B, Tp, HEAD_LANES), F32)],
        scratch_shapes=[pltpu.VMEM((Tp, HEAD_LANES), MXU_DTYPE), pltpu.VMEM((Tp, HEAD_LANES), MXU_DTYPE), acc, acc, acc],
        compiler_params=pltpu.CompilerParams(dimension_semantics=("parallel", "arbitrary")),
    )(q, kv, kpe, *tabs, o, lse, do)


@jax.custom_vjp
def mla_attention(q, kv, kpe, tabs):
    return _attn_fwd_call(q, kv, kpe, tabs)[0]


def _mla_attention_fwd(q, kv, kpe, tabs):
    o, lse = _attn_fwd_call(q, kv, kpe, tabs)
    return o, (q, kv, kpe, tabs, o, lse)


def _mla_attention_bwd(res, do):
    q, kv, kpe, tabs, o, lse = res
    dq, dkv, dkpe = _attn_bwd_call(q, kv, kpe, tabs, o, lse, do)
    return dq, dkv, dkpe, None


mla_attention.defvjp(_mla_attention_fwd, _mla_attention_bwd)


def _rope_halves(x, cos, sin):
    half = x.shape[1] // 2
    x1, x2 = x[:, :half], x[:, half:]
    return jnp.concatenate([x1 * cos - x2 * sin, x1 * sin + x2 * cos], axis=1)


def _unrope_halves(d, cos, sin):
    half = d.shape[1] // 2
    d1, d2 = d[:, :half], d[:, half:]
    return jnp.concatenate([d1 * cos + d2 * sin, d2 * cos - d1 * sin], axis=1)


_RET_K_SCALE = RET_QK_DIM ** -0.5
_RET_Q_BLOCKS = RET_HEADS
_RET_V_BLOCK0 = 2 * RET_HEADS * RET_QK_DIM // RET_V_DIM
_RET_G_BLOCK0 = _RET_V_BLOCK0 + RET_HEADS


def _ret_specs(Tp):
    q = pl.BlockSpec((None, Tp, RET_QK_DIM), lambda b, h: (b, 0, h))
    k = pl.BlockSpec((None, Tp, RET_QK_DIM), lambda b, h: (b, 0, _RET_Q_BLOCKS + h))
    v = pl.BlockSpec((None, Tp, RET_V_DIM), lambda b, h: (b, 0, _RET_V_BLOCK0 + h))
    tab = pl.BlockSpec((Tp, RET_QK_DIM // 2), lambda b, h: (0, 0))
    lg = pl.BlockSpec((None, 1, 1), lambda b, h: (h, 0, 0))
    return q, k, v, tab, lg


def _ret_operands(q_ref, k_ref, cos, sin, lg):
    t = lax.broadcasted_iota(jnp.int32, (q_ref.shape[0], 1), 0).astype(F32)
    grow, shrink = jnp.exp(-lg * t), jnp.exp(lg * t)
    qs = (_rope_halves(q_ref[...].astype(F32), cos, sin) * shrink).astype(MXU_DTYPE)
    ks = (_rope_halves(k_ref[...].astype(F32), cos, sin) * (grow * _RET_K_SCALE)).astype(MXU_DTYPE)
    return qs, ks, shrink, grow * _RET_K_SCALE


def _ret_core_fwd_call(p, cos, sin, lg):
    B, Tp, _ = p.shape

    def body(q_ref, k_ref, v_ref, cos_ref, sin_ref, lg_ref, o_ref, qs_ref, ks_ref):
        qs_ref[...], ks_ref[...], _, _ = _ret_operands(q_ref, k_ref, cos_ref[...], sin_ref[...], lg_ref[...])
        for r0, L in _query_blocks(Tp):
            blk = slice(r0, L)
            s = _mask_diagonal(lax.dot_general(qs_ref[blk, :], ks_ref[0:L, :], _NT, preferred_element_type=F32), 0.0)
            o_ref[blk, :] = jnp.dot(s.astype(MXU_DTYPE), v_ref[0:L, :].astype(MXU_DTYPE), preferred_element_type=F32)

    q, k, v, tab, lgs = _ret_specs(Tp)
    return pl.pallas_call(
        body, name="retention_fwd", grid=(B, RET_HEADS), in_specs=[q, k, v, tab, tab, lgs],
        out_specs=pl.BlockSpec((None, Tp, RET_V_DIM), lambda b, h: (b, 0, h)),
        out_shape=jax.ShapeDtypeStruct((B, Tp, RET_HEADS * RET_V_DIM), F32),
        scratch_shapes=[pltpu.VMEM((Tp, RET_QK_DIM), MXU_DTYPE), pltpu.VMEM((Tp, RET_QK_DIM), MXU_DTYPE)],
        compiler_params=pltpu.CompilerParams(dimension_semantics=("parallel", "parallel")),
    )(p, p, p, cos, sin, lg)


def _ret_core_bwd_call(p, do, cos, sin, lg):
    B, Tp, _ = p.shape

    def body(q_ref, k_ref, v_ref, do_ref, cos_ref, sin_ref, lg_ref, dq_ref, dk_ref, dv_ref, qs_ref, ks_ref, dqa_ref, dka_ref, dva_ref):
        cos_, sin_ = cos_ref[...], sin_ref[...]
        qs_ref[...], ks_ref[...], q_scale, k_scale = _ret_operands(q_ref, k_ref, cos_, sin_, lg_ref[...])
        dka_ref[...] = jnp.zeros_like(dka_ref)
        dva_ref[...] = jnp.zeros_like(dva_ref)
        for r0, L in _query_blocks(Tp):
            blk = slice(r0, L)
            qb = qs_ref[blk, :]
            dob = do_ref[blk, :].astype(MXU_DTYPE)
            s = _mask_diagonal(lax.dot_general(qb, ks_ref[0:L, :], _NT, preferred_element_type=F32), 0.0).astype(MXU_DTYPE)
            dva_ref[0:L, :] += lax.dot_general(s, dob, _TN, preferred_element_type=F32)
            ds = _mask_diagonal(lax.dot_general(dob, v_ref[0:L, :].astype(MXU_DTYPE), _NT, preferred_element_type=F32), 0.0).astype(MXU_DTYPE)
            dqa_ref[blk, :] = jnp.dot(ds, ks_ref[0:L, :], preferred_element_type=F32)
            dka_ref[0:L, :] += lax.dot_general(ds, qb, _TN, preferred_element_type=F32)
        dq_ref[...] = _unrope_halves(dqa_ref[...] * q_scale, cos_, sin_).astype(dq_ref.dtype)
        dk_ref[...] = _unrope_halves(dka_ref[...] * k_scale, cos_, sin_).astype(dk_ref.dtype)
        dv_ref[...] = dva_ref[...].astype(dv_ref.dtype)

    q, k, v, tab, lgs = _ret_specs(Tp)
    qk_out = pl.BlockSpec((None, Tp, RET_QK_DIM), lambda b, h: (b, 0, h))
    v_out = pl.BlockSpec((None, Tp, RET_V_DIM), lambda b, h: (b, 0, h))
    return pl.pallas_call(
        body, name="retention_bwd", grid=(B, RET_HEADS), in_specs=[q, k, v, v_out, tab, tab, lgs],
        out_specs=[qk_out, qk_out, v_out],
        out_shape=[jax.ShapeDtypeStruct((B, Tp, RET_HEADS * RET_QK_DIM), p.dtype), jax.ShapeDtypeStruct((B, Tp, RET_HEADS * RET_QK_DIM), p.dtype),
                   jax.ShapeDtypeStruct((B, Tp, RET_HEADS * RET_V_DIM), p.dtype)],
        scratch_shapes=[pltpu.VMEM((Tp, RET_QK_DIM), MXU_DTYPE), pltpu.VMEM((Tp, RET_QK_DIM), MXU_DTYPE),
                        pltpu.VMEM((Tp, RET_QK_DIM), F32), pltpu.VMEM((Tp, RET_QK_DIM), F32), pltpu.VMEM((Tp, RET_V_DIM), F32)],
        compiler_params=pltpu.CompilerParams(dimension_semantics=("parallel", "parallel")),
    )(p, p, p, do, cos, sin, lg)


def _ret_gate_specs(M):
    tm = _pick(M, 1088, 8)
    head = pl.BlockSpec((tm, RET_V_DIM), lambda i, h: (i, h))
    gate = pl.BlockSpec((tm, RET_V_DIM), lambda i, h: (i, _RET_G_BLOCK0 + h))
    return tm, head, gate


def _ret_gate_fwd_call(o, p2d):
    M = o.shape[0]
    tm, head, gate = _ret_gate_specs(M)

    def body(o_ref, g_ref, y_ref):
        ov = o_ref[...]
        gv = g_ref[...].astype(F32)
        rstd = lax.rsqrt(jnp.mean(ov * ov, axis=-1, keepdims=True) + EPS)
        y_ref[...] = (gv * _sigmoid(gv)) * (ov * rstd)

    return pl.pallas_call(
        body, name="retention_gate_fwd", grid=(M // tm, RET_HEADS), in_specs=[head, gate], out_specs=head,
        out_shape=jax.ShapeDtypeStruct(o.shape, F32),
        compiler_params=pltpu.CompilerParams(dimension_semantics=("parallel", "parallel")),
    )(o, p2d)


def _ret_gate_bwd_call(o, p2d, dy):
    M = o.shape[0]
    tm, head, gate = _ret_gate_specs(M)

    def body(o_ref, g_ref, dy_ref, do_ref, dg_ref):
        ov = o_ref[...]
        gv = g_ref[...].astype(F32)
        dy = dy_ref[...]
        rstd = lax.rsqrt(jnp.mean(ov * ov, axis=-1, keepdims=True) + EPS)
        on = ov * rstd
        sg = _sigmoid(gv)
        dg_ref[...] = (dy * on * (sg * (1.0 + gv * (1.0 - sg)))).astype(dg_ref.dtype)
        don = dy * (gv * sg)
        do_ref[...] = (rstd * (don - on * jnp.mean(don * on, axis=-1, keepdims=True))).astype(do_ref.dtype)

    shp = jax.ShapeDtypeStruct(o.shape, p2d.dtype)
    return pl.pallas_call(
        body, name="retention_gate_bwd", grid=(M // tm, RET_HEADS), in_specs=[head, gate, head], out_specs=[head, head],
        out_shape=[shp, shp],
        compiler_params=pltpu.CompilerParams(dimension_semantics=("parallel", "parallel")),
    )(o, p2d, dy)


def _log_gamma():
    return jnp.log(1.0 - 2.0 ** (-5.0 - jnp.arange(RET_HEADS, dtype=F32))).reshape(RET_HEADS, 1, 1)


@functools.partial(jax.custom_vjp, nondiff_argnums=(9,))
def retention_block(h, w_in, w_out, w_in_grad_slot, w_out_grad_slot, g, b, cos, sin, dims):
    return _retention_block_fwd(h, w_in, w_out, w_in_grad_slot, w_out_grad_slot, g, b, cos, sin, dims)[0]


def _retention_block_fwd(h, w_in, w_out, w_in_grad_slot, w_out_grad_slot, g, b, cos, sin, dims):
    B, Tp = dims
    p = _mm_nn(h, w_in, False, "od_w_in_fwd", out_dtype=MXU_DTYPE)
    o = _ret_core_fwd_call(p.reshape(B, Tp, -1), cos, sin, _log_gamma())
    y = _ret_gate_fwd_call(o.reshape(B * Tp, -1), p)
    out, z = _mm_nn(y, w_out, False, "od_w_out_norm_fwd", norm=(h, g, b))
    return out, (h, p, o, y, z, w_in, w_out, g, cos, sin, jnp.zeros((), w_in_grad_slot.dtype))


def _retention_block_bwd(dims, res, dout):
    B, Tp = dims
    h, p, o, y, z, w_in, w_out, g, cos, sin, slot_like = res
    dz, dg, db = _ln_bwd_call(z, g, dout, "od_w_out_norm_bwd")
    dy = _mm_nt(dz, w_out, None, "od_w_out_dx")
    dw_out = _mm_tn(y, dz, False, "od_w_out_dw", 1, slot_like.dtype)
    do, dgate = _ret_gate_bwd_call(o.reshape(B * Tp, -1), p, dy)
    dq, dk, dv = _ret_core_bwd_call(p.reshape(B, Tp, -1), do.reshape(B, Tp, -1), cos, sin, _log_gamma())
    dp = jnp.concatenate([dq.reshape(B * Tp, -1), dk.reshape(B * Tp, -1), dv.reshape(B * Tp, -1), dgate], axis=-1)
    dh = _mm_nt(dp, w_in, None, "od_w_in_dx", plus=dz)
    dw_in = _mm_tn(h, dp, False, "od_w_in_dw", N_CHIPS, slot_like.dtype)
    return dh, None, None, dw_in, dw_out, dg.reshape(g.shape), db.reshape(g.shape), None, None


retention_block.defvjp(_retention_block_fwd, _retention_block_bwd)


def _heads_to_lanes(w):
    K = w.shape[0]
    w = w.reshape(K, MLA_HEADS, MLA_NOPE + MLA_ROPE)
    return jnp.pad(w, ((0, 0), (0, 0), (0, HEAD_LANES - MLA_NOPE - MLA_ROPE))).reshape(K, MLA_HEADS * HEAD_LANES)


def _out_rows_to_lanes(w):
    N = w.shape[1]
    att = w[LRU_WIDTH:].reshape(MLA_HEADS, MLA_V, N)
    att = jnp.pad(att, ((0, 0), (HEAD_LANES - MLA_V, 0), (0, 0))).reshape(MLA_HEADS * HEAD_LANES, N)
    return jnp.concatenate([w[:LRU_WIDTH], att], axis=0)


def _seq_dims(x):
    B, S, D = x.shape
    T = S + N_META
    Tp = _round_up(T, SEQ_BLOCK)
    return B, S, T, Tp


def _mixer0(diff, w, token):
    x = diff["x"]
    B, S, T, Tp = _seq_dims(x)
    D = x.shape[-1]
    M = B * Tp
    pos = jnp.arange(Tp, dtype=jnp.int32)

    def mm(a, name, act=False, out_dtype=F32, layout=lambda m: m, col_shards=1):
        return matmul(a, layout(w[name]), layout(diff[name]), act, name, out_dtype, col_shards)

    meta = jnp.broadcast_to(diff["meta_tokens"][None], (B, N_META, D))
    h = jnp.concatenate([meta, x + token, jnp.zeros((B, Tp - T, D), F32)], axis=1).reshape(M, D)
    p = mm(h, "ev_w_in")
    sp = jax.nn.softplus(-diff["ev_lru_lambda"]).reshape(1, LRU_WIDTH)
    y_rec, qn, kvn, kpe = even_front(
        p.reshape(B, Tp, -1), diff["ev_conv_w"].reshape(CONV_WIDTH, LRU_WIDTH), diff["ev_conv_b"].reshape(1, LRU_WIDTH),
        diff["ev_w_rg_a"].reshape(LRU_HEADS, LRU_HEAD_DIM, LRU_HEAD_DIM), diff["ev_b_rg_a"].reshape(1, LRU_WIDTH),
        diff["ev_w_rg_x"].reshape(LRU_HEADS, LRU_HEAD_DIM, LRU_HEAD_DIM), diff["ev_b_rg_x"].reshape(1, LRU_WIDTH),
        sp, diff["ev_q_norm_g"].reshape(-1), diff["ev_kv_norm_g"].reshape(-1))
    y_rec = y_rec.reshape(M, LRU_WIDTH)
    q = mm(qn, "ev_w_uq", out_dtype=MXU_DTYPE, layout=_heads_to_lanes).reshape(B, Tp, -1)
    kv = mm(kvn, "ev_w_ukv", out_dtype=MXU_DTYPE).reshape(B, Tp, -1)
    y_att = mla_attention(q, kv, kpe, _mla_rope_tables(pos)).reshape(M, -1)
    return out_block(h, jnp.concatenate([y_rec, y_att], axis=-1), _out_rows_to_lanes(w["ev_w_out"]), _out_rows_to_lanes(diff["ev_w_out"]),
                     diff["ln_mix_g"], diff["ln_mix_b"], "ev_w_out")


def _mlp0(diff, h, w):
    return mlp_block(h, w["mlp_w1_0"], w["mlp_w2_0"], diff["mlp_w1_0"], diff["mlp_w2_0"], diff["ln_mlp_g"], diff["ln_mlp_b"], "mlp0")


def _layer1_loss(diff, h, w, tgt):
    B, S, T, Tp = _seq_dims(tgt)
    D = tgt.shape[-1]
    pos = jnp.arange(Tp, dtype=jnp.int32)

    cos, sin = _rope_tables(pos, RET_QK_DIM // 2)
    h = retention_block(h, w["od_w_in"], w["od_w_out"], diff["od_w_in"], diff["od_w_out"], diff["ln_mix_g"], diff["ln_mix_b"], cos, sin, (B, Tp))
    h = mlp_block(h, w["mlp_w1_1"], w["mlp_w2_1"], diff["mlp_w1_1"], diff["mlp_w2_1"], diff["ln_mlp_g"], diff["ln_mlp_b"], "mlp1")
    return loss_head(h.reshape(B, Tp, D), jnp.pad(tgt, ((0, 0), (N_META, Tp - T), (0, 0))), S)


_HBM = pl.BlockSpec(memory_space=pltpu.HBM)


def _place():
    return lax.axis_index("x"), lax.axis_index("y"), lax.axis_index("c")


def _other_chips(x, y):
    return [(1 - x, y), (x, 1 - y), (1 - x, 1 - y)]


def _chunks(rows, sublanes, most):
    for q in range(most, 0, -1):
        if rows % (q * sublanes) == 0:
            return q
    return 1


def _sublanes(dtype):
    return 8 * 4 // jnp.dtype(dtype).itemsize


def _gather_pieces(bufs):
    plan, first = [], []
    for b in bufs:
        Rh = b.shape[0] // 2
        Q = _chunks(Rh, _sublanes(b.dtype), 4) if Rh * b.shape[1] * b.dtype.itemsize > (1 << 20) else 1
        first.append(3 * sum(q for _, q, _ in plan))
        plan.append((Rh, Q, Rh // Q))
    return plan, first, 3 * sum(q for _, q, _ in plan)


def _allgather_chips(bufs, name):
    n = len(bufs)
    plan, first, n_sems = _gather_pieces(bufs)

    def body(*refs):
        x_refs, out_refs, (send_sems, recv_sems) = refs[:n], refs[n:2 * n], refs[2 * n:]
        x, y, c = _place()
        sibling = (x, y, 1 - c)
        chips = _other_chips(x, y)

        def copy(k, src, dst, to):
            return pltpu.make_async_remote_copy(src_ref=src, dst_ref=dst, send_sem=send_sems.at[k], recv_sem=recv_sems.at[k],
                                                device_id=to, device_id_type=MESH)

        def piece(i, cx, cy, hc, q):
            Rh, _, ch = plan[i]
            return out_refs[i].at[2 * cx + cy, pl.ds(hc * Rh + q * ch, ch), :]

        slots = [(i, q, j) for i in range(n) for q in range(plan[i][1]) for j in range(3)]
        sem = {(i, q, j): first[i] + 3 * q + j for i, q, j in slots}
        sent = [copy(sem[i, q, j], x_refs[i].at[pl.ds(c * plan[i][0] + q * plan[i][2], plan[i][2]), :], piece(i, x, y, c, q), (*chips[j], c))
                for i, q, j in slots]
        for cp in sent:
            cp.start()
        passed = []
        for i, q, j in slots:
            landed = piece(i, *chips[j], c, q)
            copy(sem[i, q, j], landed, landed, sibling).wait_recv()
            fwd = copy(n_sems + sem[i, q, j], landed, landed, sibling)
            fwd.start()
            passed.append(fwd)
        for i, q, j in slots:
            theirs = piece(i, *chips[j], 1 - c, q)
            copy(n_sems + sem[i, q, j], theirs, theirs, sibling).wait_recv()
        for cp in sent + passed:
            cp.wait_send()

    return pl.pallas_call(
        body, name=name, in_specs=[_HBM] * n, out_specs=[_HBM] * n,
        out_shape=[jax.ShapeDtypeStruct((N_CHIPS,) + b.shape, b.dtype) for b in bufs],
        scratch_shapes=[pltpu.SemaphoreType.DMA((2 * n_sems,)), pltpu.SemaphoreType.DMA((2 * n_sems,))],
    )(*bufs)


def _with_own(gathered, own):
    my = 2 * lax.axis_index("x") + lax.axis_index("y")
    return lax.dynamic_update_slice(gathered, own[None], (my, 0, 0))


def _sibling_gather(fs, name):
    n = len(fs)

    def body(*refs):
        out_refs, (send_sems, recv_sems) = refs[n:2 * n], refs[2 * n:]
        x, y, c = _place()
        copies = [pltpu.make_async_remote_copy(src_ref=out_ref.at[c], dst_ref=out_ref.at[c], send_sem=send_sems.at[i], recv_sem=recv_sems.at[i],
                                               device_id=(x, y, 1 - c), device_id_type=MESH) for i, out_ref in enumerate(out_refs)]
        for cp in copies:
            cp.start()
        for cp in copies:
            cp.wait()

    return pl.pallas_call(
        body, name=name, in_specs=[_HBM] * n, out_specs=[_HBM] * n,
        out_shape=[jax.ShapeDtypeStruct(f.shape, f.dtype) for f in fs], input_output_aliases={i: i for i in range(n)},
        scratch_shapes=[pltpu.SemaphoreType.DMA((n,)), pltpu.SemaphoreType.DMA((n,))],
    )(*fs)


def _axis_scalar(name):
    return lax.axis_index(name).astype(jnp.int32).reshape(1)


_SEM = pl.BlockSpec(memory_space=pltpu.SEMAPHORE)
_ANY = pl.BlockSpec(memory_space=pl.ANY)
_EFFECT = pltpu.SideEffectType.DATAFLOW_SIDE_EFFECTING


def _in_hbm(a):
    return pltpu.with_memory_space_constraint(a, pltpu.HBM)


def _half_copies(x_refs, land_refs, send_sems, recv_sems, arriving):
    x, y, c = _place()
    copies = []
    for i, (x_ref, land_ref) in enumerate(zip(x_refs, land_refs)):
        Rh = x_ref.shape[0] // 2
        rows = pl.ds(c * Rh, Rh)
        for j, (cx, cy) in enumerate(_other_chips(x, y)):
            copies.append(pltpu.make_async_remote_copy(
                src_ref=x_ref.at[rows, :], dst_ref=land_ref.at[2 * cx + cy if arriving else 2 * x + y, rows, :],
                send_sem=send_sems.at[3 * i + j], recv_sem=recv_sems.at[3 * i + j], device_id=(cx, cy, c), device_id_type=MESH))
    return copies


def _allgather_start(bufs, name):
    n = len(bufs)

    def body(*refs):
        x_refs, land_refs, (send_sems, recv_sems), token = refs[:n], refs[n:2 * n], refs[2 * n:2 * n + 2], refs[-1]
        for cp in _half_copies(x_refs, land_refs, send_sems, recv_sems, False):
            cp.start()
        token[...] = jnp.zeros_like(token)

    lands = [lax.empty((N_CHIPS,) + b.shape, b.dtype) for b in bufs]
    out = pl.pallas_call(
        body, name=name,
        out_shape=(pltpu.SemaphoreType.DMA((3 * n,)), pltpu.SemaphoreType.DMA((3 * n,)), *[pltpu.HBM(a.shape, a.dtype) for a in bufs + lands],
                   jax.ShapeDtypeStruct((8, 128), F32)),
        in_specs=[_HBM] * (2 * n), out_specs=(_SEM, _SEM, *[_HBM] * (2 * n), pl.BlockSpec(memory_space=pltpu.VMEM)),
        input_output_aliases={i: 2 + i for i in range(2 * n)}, compiler_params=pltpu.CompilerParams(has_side_effects=_EFFECT),
    )(*[_in_hbm(a) for a in bufs + lands])
    return (out[0], out[1], list(out[2:2 + n]), list(out[2 + n:2 + 2 * n])), out[-1][0, 0]


def _allgather_wait(pending, after, name):
    send_sems, recv_sems, bufs, lands = pending
    n = len(bufs)

    def body(*refs):
        x_refs, land_refs, send_sems, recv_sems = refs[:n], refs[n:2 * n], refs[2 * n], refs[2 * n + 1]
        for cp in _half_copies(x_refs, land_refs, send_sems, recv_sems, False):
            cp.wait_send()
        for cp in _half_copies(x_refs, land_refs, send_sems, recv_sems, True):
            cp.wait_recv()

    out = pl.pallas_call(
        body, name=name, out_shape=tuple(pltpu.HBM(a.shape, a.dtype) for a in bufs + lands),
        in_specs=[_HBM] * (2 * n) + [_SEM, _SEM, _ANY], out_specs=tuple([_HBM] * (2 * n)), input_output_aliases={i: i for i in range(2 * n)},
        compiler_params=pltpu.CompilerParams(has_side_effects=_EFFECT),
    )(*bufs, *lands, send_sems, recv_sems, after)
    return list(out[n:])


def _sibling_forward(lands, name):
    n = len(lands)
    plan, first, n_sems = _gather_pieces([jax.ShapeDtypeStruct(l.shape[1:], l.dtype) for l in lands])

    def body(*refs):
        out_refs, (send_sems, recv_sems) = refs[n:2 * n], refs[2 * n:]
        x, y, c = _place()

        def copies(hc):
            return [pltpu.make_async_remote_copy(
                        src_ref=out_refs[i].at[2 * cx + cy, pl.ds(hc * plan[i][0] + q * plan[i][2], plan[i][2]), :],
                        dst_ref=out_refs[i].at[2 * cx + cy, pl.ds(hc * plan[i][0] + q * plan[i][2], plan[i][2]), :],
                        send_sem=send_sems.at[first[i] + 3 * q + j], recv_sem=recv_sems.at[first[i] + 3 * q + j],
                        device_id=(x, y, 1 - c), device_id_type=MESH)
                    for i in range(n) for q in range(plan[i][1]) for j, (cx, cy) in enumerate(_other_chips(x, y))]

        mine = copies(c)
        for cp in mine:
            cp.start()
        for cp in mine:
            cp.wait_send()
        for cp in copies(1 - c):
            cp.wait_recv()

    return pl.pallas_call(
        body, name=name, in_specs=[_HBM] * n, out_specs=[_HBM] * n, out_shape=[jax.ShapeDtypeStruct(l.shape, l.dtype) for l in lands],
        input_output_aliases={i: i for i in range(n)},
        scratch_shapes=[pltpu.SemaphoreType.DMA((n_sems,)), pltpu.SemaphoreType.DMA((n_sems,))],
    )(*lands)


N_PEERS = 7


def _direct_copies(p_refs, t_refs, send_sems, recv_sems):
    x, y, c = _place()
    copies = []
    for i, (p_ref, t_ref) in enumerate(zip(p_refs, t_refs)):
        for f in range(1, N_PEERS + 1):
            px, py, pc = x ^ (f >> 2), y ^ ((f >> 1) & 1), c ^ (f & 1)
            copies.append(pltpu.make_async_remote_copy(
                src_ref=p_ref.at[2 * px + py, pc], dst_ref=t_ref.at[f - 1], send_sem=send_sems.at[N_PEERS * i + f - 1],
                recv_sem=recv_sems.at[N_PEERS * i + f - 1], device_id=(px, py, pc), device_id_type=MESH))
    return copies


def _direct_scatter_start(ps, name):
    n = len(ps)

    def body(*refs):
        p_refs, t_refs, (send_sems, recv_sems), token = refs[:n], refs[n:2 * n], refs[2 * n:2 * n + 2], refs[-1]
        for cp in _direct_copies(p_refs, t_refs, send_sems, recv_sems):
            cp.start()
        token[...] = jnp.zeros_like(token)

    lands = [lax.empty((N_PEERS,) + p.shape[2:], p.dtype) for p in ps]
    out = pl.pallas_call(
        body, name=name,
        out_shape=(pltpu.SemaphoreType.DMA((N_PEERS * n,)), pltpu.SemaphoreType.DMA((N_PEERS * n,)),
                   *[pltpu.HBM(a.shape, a.dtype) for a in ps + lands], jax.ShapeDtypeStruct((8, 128), F32)),
        in_specs=[_HBM] * (2 * n), out_specs=(_SEM, _SEM, *[_HBM] * (2 * n), pl.BlockSpec(memory_space=pltpu.VMEM)),
        input_output_aliases={i: 2 + i for i in range(2 * n)}, compiler_params=pltpu.CompilerParams(has_side_effects=_EFFECT),
    )(*[_in_hbm(a) for a in ps + lands])
    return (out[0], out[1], list(out[2:2 + n]), list(out[2 + n:2 + 2 * n])), out[-1][0, 0]


def _direct_scatter_wait(pending, after, name):
    send_sems, recv_sems, ps, lands = pending
    n = len(ps)

    def body(*refs):
        p_refs, t_refs, send_sems, recv_sems = refs[:n], refs[n:2 * n], refs[2 * n], refs[2 * n + 1]
        for cp in _direct_copies(p_refs, t_refs, send_sems, recv_sems):
            cp.wait_send()
            cp.wait_recv()

    out = pl.pallas_call(
        body, name=name, out_shape=tuple(pltpu.HBM(a.shape, a.dtype) for a in ps + lands),
        in_specs=[_HBM] * (2 * n) + [_SEM, _SEM, _ANY], out_specs=tuple([_HBM] * (2 * n)),
        input_output_aliases={i: i for i in range(2 * n)}, compiler_params=pltpu.CompilerParams(has_side_effects=_EFFECT),
    )(*ps, *lands, send_sems, recv_sems, after)
    return list(out[:n]), list(out[n:])


def _sum_direct(p, t, name):
    _, _, R, C = p.shape
    tr = _pick(R, 512, 16)

    def body(x_ref, y_ref, c_ref, p_ref, t_ref, o_ref):
        acc = p_ref[...].astype(F32)
        for f in range(N_PEERS):
            acc = acc + t_ref[f].astype(F32)
        o_ref[...] = acc

    grid_spec = pltpu.PrefetchScalarGridSpec(
        num_scalar_prefetch=3, grid=(R // tr,),
        in_specs=[pl.BlockSpec((None, None, tr, C), lambda i, x_ref, y_ref, c_ref: (2 * x_ref[0] + y_ref[0], c_ref[0], i, 0)),
                  pl.BlockSpec((N_PEERS, tr, C), lambda i, x_ref, y_ref, c_ref: (0, i, 0))],
        out_specs=pl.BlockSpec((None, tr, C), lambda i, x_ref, y_ref, c_ref: (c_ref[0], i, 0)))
    return pl.pallas_call(body, name=name, grid_spec=grid_spec, out_shape=jax.ShapeDtypeStruct((2, R, C), F32),
                          compiler_params=pltpu.CompilerParams(dimension_semantics=("parallel",)))(
        _axis_scalar("x"), _axis_scalar("y"), _axis_scalar("c"), p, t)


def _adamw(w, g, m, v, name):
    R, C = w.shape
    tr = _pick(R, 256, 8)

    def body(w_ref, g_ref, m_ref, v_ref, d_ref, nm_ref, nv_ref):
        g_ = g_ref[...]
        m_ = ADAM_B1 * m_ref[...] + (1.0 - ADAM_B1) * g_
        v_ = ADAM_B2 * v_ref[...] + (1.0 - ADAM_B2) * (g_ * g_)
        m_hat = m_ / (1.0 - ADAM_B1 ** ADAM_STEP)
        v_hat = v_ / (1.0 - ADAM_B2 ** ADAM_STEP)
        d_ref[...] = -ADAM_LR * (m_hat / (jnp.sqrt(v_hat) + ADAM_EPS) + ADAM_WD * w_ref[...])
        nm_ref[...] = m_
        nv_ref[...] = v_

    row = pl.BlockSpec((tr, C), lambda i: (i, 0))
    shp = jax.ShapeDtypeStruct((R, C), F32)
    return pl.pallas_call(body, name=name, grid=(R // tr,), in_specs=[row] * 4, out_specs=[row] * 3, out_shape=[shp] * 3,
                          compiler_params=pltpu.CompilerParams(dimension_semantics=("parallel",)))(w, g, m, v)


BIG_SPECS = (("ev_w_in", 1024, 1440, 1), ("ev_w_uq", 256, 768, 1), ("ev_w_ukv", 128, 1024, 1), ("ev_w_out", 1024, 1024, 0),
             ("od_w_in", 1024, 6144, 1), ("od_w_out", 2048, 1024, 0), ("mlp_w1_0", 1024, 4096, 1), ("mlp_w1_1", 1024, 4096, 1),
             ("mlp_w2_0", 4096, 1024, 0), ("mlp_w2_1", 4096, 1024, 0))
BIG_PARAMS = (("ev_w_in", ("ev_w_in",)), ("ev_w_uq", ("ev_w_uq",)), ("ev_w_ukv", ("ev_w_ukv",)), ("ev_w_out", ("ev_w_out",)),
              ("od_w_in", ("od_w_in",)), ("od_w_out", ("od_w_out",)), ("mlp_w1", ("mlp_w1_0", "mlp_w1_1")),
              ("mlp_w2", ("mlp_w2_0", "mlp_w2_1")))
REPLICATED = ("ev_conv_b", "ev_w_rg_a", "ev_b_rg_a", "ev_w_rg_x", "ev_b_rg_x", "ev_lru_lambda", "ev_q_norm_g", "ev_kv_norm_g",
              "ln_mix_g", "ln_mix_b", "ln_mlp_g", "ln_mlp_b")
SMALL_SHARDED = ("meta_tokens", "ev_conv_w")
COL_SHARD_GRADS = ("od_w_in", "mlp_w1_0", "mlp_w1_1")
MATRIX_GROUPS = (("ev_w_in", "ev_w_uq", "ev_w_ukv", "ev_w_out"), ("mlp_w1_0", "mlp_w2_0"), ("od_w_in", "od_w_out", "mlp_w1_1", "mlp_w2_1"))
LAYER_NORMS = ("ln_mix_g", "ln_mix_b", "ln_mlp_g", "ln_mlp_b")
WEIGHT_NAMES = ("meta_tokens", "ev_w_in", "ev_conv_w", "ev_conv_b", "ev_w_rg_a", "ev_b_rg_a", "ev_w_rg_x", "ev_b_rg_x",
                "ev_lru_lambda", "ev_q_norm_g", "ev_w_uq", "ev_kv_norm_g", "ev_w_ukv", "ev_w_out", "od_w_in", "od_w_out",
                "ln_mix_g", "ln_mix_b", "mlp_w1", "mlp_w2", "ln_mlp_g", "ln_mlp_b")


def _to_rows(flat, row_align):
    n = flat.shape[-1]
    rows = _round_up(-(-n // PACK_COLS), row_align)
    pad = rows * PACK_COLS - n
    if pad:
        flat = jnp.pad(flat, [(0, 0)] * (flat.ndim - 1) + [(0, pad)])
    return flat.reshape(flat.shape[:-1] + (rows, PACK_COLS))


def _as_lanes(a):
    lead, n = a.shape[:-2], a.shape[-2] * a.shape[-1]
    rows = _round_up(-(-n // PACK_COLS), 32)
    flat = jnp.pad(a.reshape(lead + (n,)), [(0, 0)] * len(lead) + [(0, rows * PACK_COLS - n)])
    return flat.reshape(lead + (rows, PACK_COLS))


def _from_lanes(a, shape):
    lead = a.shape[:-2]
    return a.reshape(lead + (-1,))[..., :shape[0] * shape[1]].reshape(lead + tuple(shape))


def _narrow(shape):
    return shape[-1] % 128 != 0


def _shard_shape(K, N, axis):
    return (K // N_CHIPS, N) if axis == 0 else (K, N // N_CHIPS)


def _gather_shards(stacked, K, N, axis):
    if axis == 0:
        return stacked.reshape(K, N)
    return stacked.transpose(1, 0, 2).reshape(K, N)


def _split_shards(full, K, N, axis):
    if axis == 0:
        return full.reshape(N_CHIPS, -1)
    return full.reshape(K, N_CHIPS, N // N_CHIPS).transpose(1, 0, 2).reshape(N_CHIPS, -1)


def kernel(x, meta_tokens, ev_w_in, ev_conv_w, ev_conv_b, ev_w_rg_a, ev_b_rg_a, ev_w_rg_x, ev_b_rg_x, ev_lru_lambda, ev_q_norm_g, ev_w_uq, ev_kv_norm_g, ev_w_ukv, ev_w_out, od_w_in, od_w_out, ln_mix_g, ln_mix_b, mlp_w1, mlp_w2, ln_mlp_g, ln_mlp_b, loss_target, m_meta_tokens, m_ev_w_in, m_ev_conv_w, m_ev_conv_b, m_ev_w_rg_a, m_ev_b_rg_a, m_ev_w_rg_x, m_ev_b_rg_x, m_ev_lru_lambda, m_ev_q_norm_g, m_ev_w_uq, m_ev_kv_norm_g, m_ev_w_ukv, m_ev_w_out, m_od_w_in, m_od_w_out, m_ln_mix_g, m_ln_mix_b, m_mlp_w1, m_mlp_w2, m_ln_mlp_g, m_ln_mlp_b, v_meta_tokens, v_ev_w_in, v_ev_conv_w, v_ev_conv_b, v_ev_w_rg_a, v_ev_b_rg_a, v_ev_w_rg_x, v_ev_b_rg_x, v_ev_lru_lambda, v_ev_q_norm_g, v_ev_w_uq, v_ev_kv_norm_g, v_ev_w_ukv, v_ev_w_out, v_od_w_in, v_od_w_out, v_ln_mix_g, v_ln_mix_b, v_mlp_w1, v_mlp_w2, v_ln_mlp_g, v_ln_mlp_b):
    given = dict(locals())
    local_big = {"ev_w_in": ev_w_in[0], "ev_w_uq": ev_w_uq[0], "ev_w_ukv": ev_w_ukv[0], "ev_w_out": ev_w_out[0],
                 "od_w_in": od_w_in[0], "od_w_out": od_w_out[0], "mlp_w1_0": mlp_w1[0], "mlp_w1_1": mlp_w1[1],
                 "mlp_w2_0": mlp_w2[0], "mlp_w2_1": mlp_w2[1]}

    specs = {spec[0]: spec for spec in BIG_SPECS}
    mixer0_m, mlp0_m, layer1_m = MATRIX_GROUPS

    def whole(stacked, n):
        _, K, N, ax = specs[n]
        return stacked if n in COL_SHARD_GRADS else _gather_shards(stacked, K, N, ax)

    def filled(gathered, own, names):
        out = {}
        for n, g_, o_ in zip(names, gathered, own):
            stacked = _with_own(g_, o_)
            shard = _shard_shape(*specs[n][1:])
            out[n] = whole(_from_lanes(stacked, shard) if _narrow(shard) else stacked, n)
        return out

    def shards(names):
        own = [local_big[n].astype(MXU_DTYPE) for n in names]
        return [_as_lanes(o) if _narrow(o.shape) else o for o in own]

    own_a, own_b, own_c = shards(mixer0_m), shards(mlp0_m), shards(layer1_m)
    small = [meta_tokens, jnp.pad(ev_conv_w[0], ((0, 16 - CONV_WIDTH), (0, 0)))]
    gathered_a = _allgather_chips(own_a + small, "weight_allgather_mixer0")
    pending_b, token1 = _allgather_start(own_b, "weight_allgather_mlp0_start")
    pending_c, token2 = _allgather_start(own_c, "weight_allgather_layer1_start")
    meta_full = _gather_shards(_with_own(gathered_a[-2], small[0]), N_META, D_MODEL, 1)
    conv_full = _gather_shards(_with_own(gathered_a[-1], small[1])[:, :CONV_WIDTH], CONV_WIDTH, LRU_WIDTH, 1)

    def slots(names, dtype):
        return {n: jnp.zeros((N_CHIPS, specs[n][1], specs[n][2] // N_CHIPS) if n in COL_SHARD_GRADS else specs[n][1:3], dtype) for n in names}

    def norms(names, layer):
        return {n: given[n][layer] for n in names}

    def finish_gather(pending, own, after, names, tag):
        landed = _allgather_wait(pending, lax.stop_gradient(after), "weight_allgather_%s_wait" % tag)
        return filled(_sibling_forward(landed, "weight_allgather_%s_forward" % tag), own, names)

    diff_a = {**slots(mixer0_m, MXU_DTYPE), **norms(("ln_mix_g", "ln_mix_b"), 0), **{n: given[n] for n in REPLICATED if n not in LAYER_NORMS},
              "x": x, "meta_tokens": meta_full, "ev_conv_w": conv_full}
    diff_b = {**slots(mlp0_m, MXU_DTYPE), **norms(("ln_mlp_g", "ln_mlp_b"), 0)}
    diff_c = {**slots(layer1_m, MXU_DTYPE), **norms(LAYER_NORMS, 1)}
    w_a = filled(gathered_a[:len(mixer0_m)], own_a, mixer0_m)
    h_a, back_a = jax.vjp(lambda d: _mixer0(d, w_a, token1 + token2), diff_a)
    w_b = finish_gather(pending_b, own_b, h_a, mlp0_m, "mlp0")
    h_b, back_b = jax.vjp(lambda d, hh: _mlp0(d, hh, w_b), diff_b, h_a)
    w_c = finish_gather(pending_c, own_c, h_b, layer1_m, "layer1")
    loss, back_c = jax.vjp(lambda d, hh: _layer1_loss(d, hh, w_c, loss_target), diff_c, h_b)
    loss = lax.psum(loss, ("x", "y", "c"))

    def blocks_of(grad, n):
        _, K, N, ax = specs[n]
        if n in COL_SHARD_GRADS:
            blocks = grad
        elif ax == 0:
            blocks = grad.reshape(N_CHIPS, K // N_CHIPS, N)
        else:
            blocks = grad.reshape(K, N_CHIPS, N // N_CHIPS).transpose(1, 0, 2)
        blocks = blocks.reshape(N_CHIPS, 2, blocks.shape[1] // 2, blocks.shape[2])
        return _as_lanes(blocks) if _narrow(blocks.shape) else blocks

    def start_reduce(grads_of, names, tag):
        return _direct_scatter_start([blocks_of(grads_of[n], n) for n in names], "grad_scatter_%s_start" % tag)

    g_c, dh = back_c(jnp.ones((), F32))
    flying_c, token = start_reduce(g_c, layer1_m, "layer1")
    g_b, dh = back_b(dh + token)
    flying_b, token = start_reduce(g_b, mlp0_m, "mlp0")
    (g_a,) = back_a(dh + token)

    g = {**g_a, **g_b, **g_c}
    g.update({n: jnp.stack([(g_b if n in g_b else g_a)[n], g_c[n]]) for n in LAYER_NORMS})
    repl = jnp.concatenate([g[n].reshape(-1) for n in REPLICATED]).reshape(N_CHIPS, -1)
    small = [_split_shards(g["meta_tokens"], N_META, D_MODEL, 1), _split_shards(g["ev_conv_w"], CONV_WIDTH, LRU_WIDTH, 1), repl]
    small = [pc.reshape(N_CHIPS, 2, -1) for pc in small]
    n_small = sum(pc.shape[2] for pc in small)
    small.append(jnp.zeros((N_CHIPS, 2, _round_up(n_small, 32 * PACK_COLS) - n_small), F32))
    p_small = jnp.concatenate(small, axis=2).reshape(N_CHIPS, 2, -1, PACK_COLS)
    flying_a, token = _direct_scatter_start([blocks_of(g_a[n], n) for n in mixer0_m] + [p_small], "grad_scatter_mixer0_start")
    ps_c, ts_c = _direct_scatter_wait(flying_c, g_a["x"], "grad_scatter_layer1_wait")
    ps_b, ts_b = _direct_scatter_wait(flying_b, g_a["x"], "grad_scatter_mlp0_wait")
    fs_bc = [_sum_direct(p, t, "grad_sum_%d" % i) for i, (p, t) in enumerate(zip(ps_b + ps_c, ts_b + ts_c))]
    ps_a, ts_a = _direct_scatter_wait(flying_a, fs_bc[-1], "grad_scatter_mixer0_wait")
    fs_a = [_sum_direct(p, t, "grad_sum_mixer0_%d" % i) for i, (p, t) in enumerate(zip(ps_a, ts_a))]
    reduced = _sibling_gather(fs_a + fs_bc, "grad_sibling_gather")
    red_big = dict(zip(mixer0_m + ("small",) + mlp0_m + layer1_m, reduced))
    red_small = red_big.pop("small").reshape(2, -1)

    def shard_of(part):
        shard = _shard_shape(*specs[part][1:])
        red = red_big[part]
        return (_from_lanes(red, (shard[0] // 2, shard[1])) if _narrow(shard) else red).reshape(shard)

    grads = {}
    for name, parts in BIG_PARAMS:
        grads[name] = jnp.stack([shard_of(part) for part in parts])

    def take(off, sz):
        return jnp.concatenate([red_small[0, off // 2:(off + sz) // 2], red_small[1, off // 2:(off + sz) // 2]])

    off = 0
    for name in SMALL_SHARDED:
        sz = given[name].size
        grads[name] = take(off, sz).reshape(given[name].shape)
        off += sz
    n_repl = repl.shape[1]
    own_repl = _to_rows(take(off, n_repl), 16)
    repl_all = _with_own(_allgather_chips([own_repl], "replicated_allgather")[0], own_repl).reshape(N_CHIPS, -1)[:, :n_repl].reshape(-1)
    off = 0
    for name in REPLICATED:
        sz = given[name].size
        grads[name] = repl_all[off:off + sz].reshape(given[name].shape)
        off += sz

    delta, new_m, new_v = {}, {}, {}
    for name, _ in BIG_PARAMS:
        shp = given[name].shape
        two_d = (-1, shp[-1])
        d, nm, nv = _adamw(given[name].reshape(two_d), grads[name].reshape(two_d), given["m_" + name].reshape(two_d),
                           given["v_" + name].reshape(two_d), "adamw_" + name)
        delta[name], new_m[name], new_v[name] = d.reshape(shp), nm.reshape(shp), nv.reshape(shp)
    smalls = SMALL_SHARDED + REPLICATED

    def pack_small(get):
        return _to_rows(jnp.concatenate([get(n).reshape(-1) for n in smalls]), 8)

    outs = _adamw(pack_small(lambda n: given[n]), pack_small(lambda n: grads[n]), pack_small(lambda n: given["m_" + n]),
                  pack_small(lambda n: given["v_" + n]), "adamw_small")
    for res, flat in zip((delta, new_m, new_v), outs):
        flat, off = flat.reshape(-1), 0
        for n in smalls:
            sz = given[n].size
            res[n] = flat[off:off + sz].reshape(given[n].shape)
            off += sz

    return (loss, g_a["x"], *[grads[n] for n in WEIGHT_NAMES], *[delta[n] for n in WEIGHT_NAMES],
            *[new_m[n] for n in WEIGHT_NAMES], *[new_v[n] for n in WEIGHT_NAMES])
```

```python
import functools
import math

import jax
import jax.numpy as jnp
from jax import lax
from jax.experimental import pallas as pl
from jax.experimental.pallas import tpu as pltpu

F32 = jnp.float32
MXU_DTYPE = jnp.bfloat16

D_MODEL = 1024
N_META = 16
LRU_WIDTH = 512
LRU_HEADS = 4
LRU_HEAD_DIM = 128
CONV_WIDTH = 4
LRU_C = 8.0
MLA_HEADS = 8
MLA_NOPE = 64
MLA_ROPE = 32
MLA_V = 64
MLA_Q_RANK = 256
MLA_KV_RANK = 128
RET_HEADS = 4
RET_QK_DIM = 256
RET_V_DIM = 512
D_FF = 4096
ROPE_BASE = 10000.0
DN_ALPHA = 4.0 ** 0.25
EPS = 1e-5
NEG_INF = -1e30
SEQ_BLOCK = 128

ADAM_LR = 0.001
ADAM_B1 = 0.9
ADAM_B2 = 0.999
ADAM_EPS = 1e-08
ADAM_WD = 0.01
ADAM_STEP = 10

PACK_COLS = 1024
TN_INPUT_VMEM_BYTES = 28 << 20
N_CHIPS = 4

MESH = pl.DeviceIdType.MESH


def _pick(n, target, align):
    best = None
    for t in range(align, min(n, target) + 1, align):
        if n % t == 0:
            best = t
    return n if best is None else best


def _round_up(n, m):
    return (n + m - 1) // m * m


def _relu2(a):
    r = jnp.maximum(a, 0.0)
    return r * r


def _ln_stats(z):
    mu = jnp.mean(z, axis=-1, keepdims=True)
    zc = z - mu
    var = jnp.mean(zc * zc, axis=-1, keepdims=True)
    return zc, lax.rsqrt(var + EPS)


def _mm_nn(a, w, act, name, out_dtype=F32, norm=None):
    M, K = a.shape
    sharded = w.ndim == 3
    n = w.shape[-1]
    N = n * (w.shape[0] if sharded else 1)
    tm = _pick(M, 1088 if K * a.dtype.itemsize <= 4096 and norm is None else 544, 8)
    tn = _pick(n, 1024, 128)
    per = n // tn
    assert norm is None or tn == N

    def body(a_ref, w_ref, *rest):
        av = a_ref[...]
        if act:
            av = _relu2(av.astype(F32))
        r = jnp.dot(av.astype(MXU_DTYPE), w_ref[...].astype(MXU_DTYPE), preferred_element_type=F32)
        if norm is None:
            rest[0][...] = r.astype(out_dtype)
        else:
            r_ref, g_ref, b_ref, o_ref, z_ref = rest
            z = DN_ALPHA * r_ref[...] + r
            zc, rstd = _ln_stats(z)
            z_ref[...] = z
            o_ref[...] = zc * rstd * g_ref[...] + b_ref[...]

    w_spec = pl.BlockSpec((None, K, tn), lambda i, j: (j // per, 0, j % per)) if sharded else pl.BlockSpec((K, tn), lambda i, j: (0, j))
    tile = pl.BlockSpec((tm, tn), lambda i, j: (i, j))
    in_specs, args = [pl.BlockSpec((tm, K), lambda i, j: (i, 0)), w_spec], [a, w]
    if norm is None:
        out_specs, out_shape = tile, jax.ShapeDtypeStruct((M, N), out_dtype)
    else:
        vec = pl.BlockSpec((1, N), lambda i, j: (0, 0))
        in_specs += [tile, vec, vec]
        args += [norm[0], norm[1].reshape(1, N), norm[2].reshape(1, N)]
        out_specs, out_shape = [tile, tile], [jax.ShapeDtypeStruct((M, N), F32)] * 2
    return pl.pallas_call(
        body, name=name, grid=(M // tm, N // tn), in_specs=in_specs, out_specs=out_specs, out_shape=out_shape,
        compiler_params=pltpu.CompilerParams(dimension_semantics=("parallel", "arbitrary")),
    )(*args)


def _mm_nt(g, w, a_src, name, out_dtype=F32, plus=None):
    M, N = g.shape
    sharded = w.ndim == 3
    K, n = w.shape[-2], w.shape[-1]
    if sharded:
        tk, nk = N, 1
    else:
        tk = N if N * g.dtype.itemsize <= 8192 else _pick(N, 2048, 128)
        nk = N // tk
    tm = _pick(M, 1088 if tk * g.dtype.itemsize <= 4096 else 544, 8)
    tn = _pick(K, 1024, 128)
    has_src = a_src is not None
    assert nk == 1 or out_dtype == F32
    assert plus is None or not has_src

    def body(*refs):
        if has_src:
            g_ref, w_ref, s_ref, o_ref = refs
        elif plus is not None:
            g_ref, w_ref, p_ref, o_ref = refs
        else:
            g_ref, w_ref, o_ref = refs
        nt = (((1,), (1,)), ((), ()))
        if sharded:
            r = sum(lax.dot_general(g_ref[:, s * n:(s + 1) * n].astype(MXU_DTYPE), w_ref[s].astype(MXU_DTYPE), nt, preferred_element_type=F32)
                    for s in range(w_ref.shape[0]))
        else:
            r = lax.dot_general(g_ref[...].astype(MXU_DTYPE), w_ref[...].astype(MXU_DTYPE), nt, preferred_element_type=F32)
        if has_src:
            r = r * (2.0 * jnp.maximum(s_ref[...].astype(F32), 0.0))
        first = r if plus is None else r + DN_ALPHA * p_ref[...]
        if nk == 1:
            o_ref[...] = first.astype(out_dtype)
        else:
            k = pl.program_id(2)

            @pl.when(k == 0)
            def _():
                o_ref[...] = first

            @pl.when(k > 0)
            def _():
                o_ref[...] += r

    w_spec = (pl.BlockSpec((w.shape[0], tn, n), lambda i, j, k: (0, j, 0)) if sharded
              else pl.BlockSpec((tn, tk), lambda i, j, k: (j, k)))
    in_specs = [pl.BlockSpec((tm, tk), lambda i, j, k: (i, k)), w_spec]
    args = [g, w]
    if has_src:
        assert nk == 1
        in_specs.append(pl.BlockSpec((tm, tn), lambda i, j, k: (i, j)))
        args.append(a_src)
    if plus is not None:
        in_specs.append(pl.BlockSpec((tm, tn), lambda i, j, k: (i, j)))
        args.append(plus)
    return pl.pallas_call(
        body, name=name,
        grid=(M // tm, K // tn, nk),
        in_specs=in_specs,
        out_specs=pl.BlockSpec((tm, tn), lambda i, j, k: (i, j)),
        out_shape=jax.ShapeDtypeStruct((M, K), out_dtype),
        compiler_params=pltpu.CompilerParams(dimension_semantics=("parallel", "parallel", "arbitrary")),
    )(*args)


def _mm_tn(a, g, act, name, col_shards=1, out_dtype=F32):
    M, K = a.shape
    _, N = g.shape
    n = N // col_shards
    tm, tn = _pick(K, 1024, 128), _pick(n, 1024, 128)
    row_bytes = tm * a.dtype.itemsize + tn * g.dtype.itemsize
    tk = _pick(M, min(2176, TN_INPUT_VMEM_BYTES // (2 * row_bytes)), 8)
    nk = M // tk
    per = n // tn
    direct = out_dtype == F32

    def body(a_ref, g_ref, o_ref, *scratch):
        acc_ref = o_ref if direct else scratch[0]
        k = pl.program_id(2)
        av = a_ref[...]
        if act:
            av = _relu2(av.astype(F32))
        r = lax.dot_general(av.astype(MXU_DTYPE), g_ref[...].astype(MXU_DTYPE),
                            (((0,), (0,)), ((), ())), preferred_element_type=F32)

        @pl.when(k == 0)
        def _():
            acc_ref[...] = r

        @pl.when(k > 0)
        def _():
            acc_ref[...] += r

        if not direct:
            @pl.when(k == nk - 1)
            def _():
                o_ref[...] = acc_ref[...].astype(out_dtype)

    if col_shards == 1:
        out_spec, out_shape = pl.BlockSpec((tm, tn), lambda i, j, k: (i, j)), (K, N)
    else:
        out_spec, out_shape = pl.BlockSpec((None, tm, tn), lambda i, j, k: (j // per, i, j % per)), (col_shards, K, n)
    return pl.pallas_call(
        body, name=name,
        grid=(K // tm, N // tn, nk),
        in_specs=[pl.BlockSpec((tk, tm), lambda i, j, k: (k, i)), pl.BlockSpec((tk, tn), lambda i, j, k: (k, j))],
        out_specs=out_spec,
        out_shape=jax.ShapeDtypeStruct(out_shape, out_dtype),
        scratch_shapes=[] if direct else [pltpu.VMEM((tm, tn), F32)],
        compiler_params=pltpu.CompilerParams(dimension_semantics=("parallel", "parallel", "arbitrary")),
    )(a, g)


@functools.partial(jax.custom_vjp, nondiff_argnums=(3, 4, 5, 6))
def matmul(a, w, w_grad_slot, act, name, out_dtype, col_shards):
    return _mm_nn(a, w, act, name + "_fwd", out_dtype)


def _matmul_fwd(a, w, w_grad_slot, act, name, out_dtype, col_shards):
    return _mm_nn(a, w, act, name + "_fwd", out_dtype), (a, w, jnp.zeros((), w_grad_slot.dtype))


def _matmul_bwd(act, name, out_dtype, col_shards, res, g):
    a, w, slot_like = res
    w_grad_dtype = slot_like.dtype
    da = _mm_nt(g, w, a if act else None, name + "_dx")
    dw = _mm_tn(a, g, act, name + "_dw", col_shards, w_grad_dtype)
    return da, None, dw


matmul.defvjp(_matmul_fwd, _matmul_bwd)


def _ln_bwd_call(z, g, dy, name):
    M, D = z.shape
    tm = _pick(M, 544, 8)

    def body(z_ref, g_ref, dy_ref, dz_ref, dg_ref, db_ref):
        @pl.when(pl.program_id(0) == 0)
        def _():
            dg_ref[...] = jnp.zeros_like(dg_ref)
            db_ref[...] = jnp.zeros_like(db_ref)

        zc, rstd = _ln_stats(z_ref[...])
        xhat = zc * rstd
        dy = dy_ref[...]
        dxh = dy * g_ref[...]
        m1 = jnp.mean(dxh, axis=-1, keepdims=True)
        m2 = jnp.mean(dxh * xhat, axis=-1, keepdims=True)
        dz_ref[...] = rstd * (dxh - m1 - xhat * m2)
        dg_ref[...] += jnp.sum(dy * xhat, axis=0, keepdims=True)
        db_ref[...] += jnp.sum(dy, axis=0, keepdims=True)

    row = pl.BlockSpec((tm, D), lambda i: (i, 0))
    vec = pl.BlockSpec((1, D), lambda i: (0, 0))
    return pl.pallas_call(
        body, name=name, grid=(M // tm,), in_specs=[row, vec, row], out_specs=[row, vec, vec],
        out_shape=[jax.ShapeDtypeStruct((M, D), F32), jax.ShapeDtypeStruct((1, D), F32), jax.ShapeDtypeStruct((1, D), F32)],
        compiler_params=pltpu.CompilerParams(dimension_semantics=("arbitrary",)),
    )(z, g.reshape(1, D), dy)


@functools.partial(jax.custom_vjp, nondiff_argnums=(7,))
def mlp_block(h, w1, w2, w1_grad_slot, w2_grad_slot, g, b, name):
    return _mlp_block_fwd(h, w1, w2, w1_grad_slot, w2_grad_slot, g, b, name)[0]


def _mlp_block_fwd(h, w1, w2, w1_grad_slot, w2_grad_slot, g, b, name):
    u = _mm_nn(h, w1, False, name + "_w1_fwd", out_dtype=MXU_DTYPE)
    out, z = _mm_nn(u, w2, True, name + "_w2_norm_fwd", norm=(h, g, b))
    return out, (h, u, z, w1, w2, g, jnp.zeros((), w1_grad_slot.dtype))


def _mlp_block_bwd(name, res, dy):
    h, u, z, w1, w2, g, slot_like = res
    dz, dg, db = _ln_bwd_call(z, g, dy, name + "_norm_bwd")
    du = _mm_nt(dz, w2, u, name + "_w2_dx", out_dtype=MXU_DTYPE)
    dw2 = _mm_tn(u, dz, True, name + "_w2_dw", 1, slot_like.dtype)
    dh = _mm_nt(du, w1, None, name + "_w1_dx", plus=dz)
    dw1 = _mm_tn(h, du, False, name + "_w1_dw", N_CHIPS, slot_like.dtype)
    return dh, None, None, dw1, dw2, dg.reshape(g.shape), db.reshape(g.shape)


mlp_block.defvjp(_mlp_block_fwd, _mlp_block_bwd)


@functools.partial(jax.custom_vjp, nondiff_argnums=(6,))
def out_block(h, y, w, w_grad_slot, g, b, name):
    return _out_block_fwd(h, y, w, w_grad_slot, g, b, name)[0]


def _out_block_fwd(h, y, w, w_grad_slot, g, b, name):
    out, z = _mm_nn(y, w, False, name + "_norm_fwd", norm=(h, g, b))
    return out, (y, z, w, g, jnp.zeros((), w_grad_slot.dtype))


def _out_block_bwd(name, res, dy):
    y, z, w, g, slot_like = res
    dz, dg, db = _ln_bwd_call(z, g, dy, name + "_norm_bwd")
    d_y = _mm_nt(dz, w, None, name + "_dx")
    dw = _mm_tn(y, dz, False, name + "_dw", 1, slot_like.dtype)
    return DN_ALPHA * dz, d_y, None, dw, dg.reshape(g.shape), db.reshape(g.shape)


out_block.defvjp(_out_block_fwd, _out_block_bwd)


def _rms_fwd_call(x, g, name, col_block=0):
    R = x.shape[0]
    W = g.shape[-1]
    tr = _pick(R, 1088, 8)

    def body(x_ref, g_ref, o_ref):
        xv = x_ref[...]
        rstd = lax.rsqrt(jnp.mean(xv * xv, axis=-1, keepdims=True) + EPS)
        o_ref[...] = xv * rstd * g_ref[...]

    vec = pl.BlockSpec((1, W), lambda i: (0, 0))
    return pl.pallas_call(
        body, name=name, grid=(R // tr,), in_specs=[pl.BlockSpec((tr, W), lambda i: (i, col_block)), vec],
        out_specs=pl.BlockSpec((tr, W), lambda i: (i, 0)), out_shape=jax.ShapeDtypeStruct((R, W), F32),
        compiler_params=pltpu.CompilerParams(dimension_semantics=("parallel",)),
    )(x, g.reshape(1, W))


def _rms_bwd_call(x, g, dy, name, col_block=0):
    R = x.shape[0]
    W = g.shape[-1]
    tr = _pick(R, 1088, 8)

    def body(x_ref, g_ref, dy_ref, dx_ref, dg_ref):
        @pl.when(pl.program_id(0) == 0)
        def _():
            dg_ref[...] = jnp.zeros_like(dg_ref)

        xv = x_ref[...]
        rstd = lax.rsqrt(jnp.mean(xv * xv, axis=-1, keepdims=True) + EPS)
        xhat = xv * rstd
        dy = dy_ref[...]
        dxh = dy * g_ref[...]
        dx_ref[...] = rstd * (dxh - xhat * jnp.mean(dxh * xhat, axis=-1, keepdims=True))
        dg_ref[...] += jnp.sum(dy * xhat, axis=0, keepdims=True)

    row = pl.BlockSpec((tr, W), lambda i: (i, 0))
    vec = pl.BlockSpec((1, W), lambda i: (0, 0))
    return pl.pallas_call(
        body, name=name, grid=(R // tr,), in_specs=[pl.BlockSpec((tr, W), lambda i: (i, col_block)), vec, row], out_specs=[row, vec],
        out_shape=[jax.ShapeDtypeStruct((R, W), F32), jax.ShapeDtypeStruct((1, W), F32)],
        compiler_params=pltpu.CompilerParams(dimension_semantics=("arbitrary",)),
    )(x, g.reshape(1, W), dy)


def _loss_call(h, tgt, n_tokens, name):
    B, Tp, D = h.shape
    tr = _pick(Tp, 544, 8)

    def body(y_ref, t_ref, dy_ref, acc_ref):
        @pl.when(jnp.logical_and(pl.program_id(0) == 0, pl.program_id(1) == 0))
        def _():
            acc_ref[...] = jnp.zeros_like(acc_ref)

        t = lax.broadcasted_iota(jnp.int32, (tr, 1), 0) + pl.program_id(1) * tr
        counts = jnp.logical_and(t >= N_META, t < N_META + n_tokens)
        e = jnp.where(counts, y_ref[...] - t_ref[...], 0.0)
        dy_ref[...] = e * (1.0 / D)
        acc_ref[...] += jnp.sum(jnp.sum(e * e, axis=-1, keepdims=True), axis=0, keepdims=True) * (0.5 / D)

    row = pl.BlockSpec((None, tr, D), lambda b, i: (b, i, 0))
    one = pl.BlockSpec((1, 1), lambda b, i: (0, 0))
    return pl.pallas_call(
        body, name=name, grid=(B, Tp // tr), in_specs=[row, row], out_specs=[row, one],
        out_shape=[jax.ShapeDtypeStruct((B, Tp, D), F32), jax.ShapeDtypeStruct((1, 1), F32)],
        compiler_params=pltpu.CompilerParams(dimension_semantics=("arbitrary", "arbitrary")),
    )(h, tgt)


@functools.partial(jax.custom_vjp, nondiff_argnums=(2,))
def loss_head(h, tgt, n_tokens):
    return _loss_call(h, tgt, n_tokens, "loss_head")[1][0, 0]


def _loss_head_fwd(h, tgt, n_tokens):
    dy, acc = _loss_call(h, tgt, n_tokens, "loss_head")
    return acc[0, 0], dy


def _loss_head_bwd(n_tokens, dy, ct):
    return ct * dy, None


loss_head.defvjp(_loss_head_fwd, _loss_head_bwd)


_GELU_C = math.sqrt(2.0 / math.pi)


def _gelu_parts(x):
    x2 = x * x
    t = jnp.tanh(_GELU_C * (x + 0.044715 * x * x2))
    gelu = 0.5 * x * (1.0 + t)
    dgelu = 0.5 * (1.0 + t) + 0.5 * x * (1.0 - t * t) * (_GELU_C * (1.0 + 3.0 * 0.044715 * x2))
    return gelu, dgelu


def _sigmoid(x):
    return 1.0 / (1.0 + jnp.exp(-x))


def _scan8(a, b, carry, reverse):
    row = lax.broadcasted_iota(jnp.int32, a.shape, 0)
    for s in (1, 2, 4):
        shift = 8 - s if reverse else s
        keep = (row < 8 - s) if reverse else (row >= s)
        b = jnp.where(keep, a * pltpu.roll(b, shift, 0) + b, b)
        a = jnp.where(keep, a * pltpu.roll(a, shift, 0), a)
    return a * carry + b


def _lru_pre(prec_ref, prev_ref, first, cw_ref, cb_ref, wa_ref, ba_ref, wx_ref, bx_ref, sp_ref):
    tc = prec_ref.shape[0]
    prev = jnp.where(first, 0.0, prev_ref[...])
    ext = jnp.concatenate([prev, prec_ref[...]], axis=0)
    cw = cw_ref[...]
    taps = [ext[8:] if k == CONV_WIDTH - 1 else pltpu.roll(ext, CONV_WIDTH - 1 - k, 0)[8:] for k in range(CONV_WIDTH)]
    xc = cb_ref[...] + sum(cw[k:k + 1, :] * taps[k] for k in range(CONV_WIDTH))
    ga, gx = [], []
    for h in range(LRU_HEADS):
        xh = xc[:, h * LRU_HEAD_DIM:(h + 1) * LRU_HEAD_DIM].astype(MXU_DTYPE)
        ga.append(jnp.dot(xh, wa_ref[h].astype(MXU_DTYPE), preferred_element_type=F32))
        gx.append(jnp.dot(xh, wx_ref[h].astype(MXU_DTYPE), preferred_element_type=F32))
    r = _sigmoid(jnp.concatenate(ga, axis=1) + ba_ref[...])
    i = _sigmoid(jnp.concatenate(gx, axis=1) + bx_ref[...])
    log_a = -LRU_C * r * sp_ref[...]
    a = jnp.exp(log_a)
    a2 = a * a
    mult = jnp.sqrt(-jnp.tanh(log_a) * (a2 + 1.0))
    return taps, xc, r, i, a, a2, mult


def _lru_fwd_call(p, cw, cb, wa, ba, wx, bx, sp):
    B, Tp, _ = p.shape
    W = LRU_WIDTH
    tc = SEQ_BLOCK
    nc = Tp // tc

    def body(pg_ref, prec_ref, prev_ref, cw_ref, cb_ref, wa_ref, ba_ref, wx_ref, bx_ref, sp_ref, y_ref, h_ref, carry_ref):
        first = pl.program_id(1) == 0

        @pl.when(first)
        def _():
            carry_ref[...] = jnp.zeros_like(carry_ref)

        _, xc, r, i, a, a2, mult = _lru_pre(prec_ref, prev_ref, first, cw_ref, cb_ref, wa_ref, ba_ref, wx_ref, bx_ref, sp_ref)
        b = mult * (i * xc)
        carry = carry_ref[0:1, :]
        for t in range(tc // 8):
            h = _scan8(a[8 * t:8 * t + 8], b[8 * t:8 * t + 8], carry, False)
            h_ref[8 * t:8 * t + 8, :] = h
            carry = h[7:8, :]
        carry_ref[...] = jnp.broadcast_to(carry, carry_ref.shape)
        y_ref[...] = h_ref[...] * _gelu_parts(pg_ref[...])[0]

    cur = pl.BlockSpec((None, tc, W), lambda b, j: (b, j, 0))
    rec = pl.BlockSpec((None, tc, W), lambda b, j: (b, j, 1))
    prev = pl.BlockSpec((None, 8, W), lambda b, j: (b, jnp.maximum(j * (tc // 8) - 1, 0), 1))
    vec = pl.BlockSpec((1, W), lambda b, j: (0, 0))
    cws = pl.BlockSpec((CONV_WIDTH, W), lambda b, j: (0, 0))
    wsp = pl.BlockSpec((LRU_HEADS, LRU_HEAD_DIM, LRU_HEAD_DIM), lambda b, j: (0, 0, 0))
    return pl.pallas_call(
        body, name="lru_fwd", grid=(B, nc),
        in_specs=[cur, rec, prev, cws, vec, wsp, vec, wsp, vec, vec],
        out_specs=[cur, cur],
        out_shape=[jax.ShapeDtypeStruct((B, Tp, W), F32), jax.ShapeDtypeStruct((B, Tp, W), F32)],
        scratch_shapes=[pltpu.VMEM((8, W), F32)],
        compiler_params=pltpu.CompilerParams(dimension_semantics=("arbitrary", "arbitrary")),
    )(p, p, p, cw, cb, wa, ba, wx, bx, sp)


def _lru_bwd_call(p, hseq, dy, cw, cb, wa, ba, wx, bx, sp, dpq, dpkv, dkpe):
    B, Tp, P = p.shape
    W = LRU_WIDTH
    tc = SEQ_BLOCK
    nc = Tp // tc
    HD = LRU_HEAD_DIM

    def body(pg_ref, prec_ref, prev_ref, h_ref, hprev_ref, dy_ref, cw_ref, cb_ref, wa_ref, ba_ref, wx_ref, bx_ref, sp_ref,
             dpq_ref, dpkv_ref, dkpe_ref, dp_ref, dcw_ref, dcb_ref, dwa_ref, dba_ref, dwx_ref, dbx_ref, dsp_ref,
             gcar_ref, anext_ref, halo_ref, g_ref):
        j = pl.program_id(1)
        first = j == nc - 1
        last = j == 0

        @pl.when(jnp.logical_and(pl.program_id(0) == 0, last))
        def _():
            for ref in (dcw_ref, dcb_ref, dwa_ref, dba_ref, dwx_ref, dbx_ref, dsp_ref):
                ref[...] = jnp.zeros_like(ref)

        @pl.when(last)
        def _():
            gcar_ref[...] = jnp.zeros_like(gcar_ref)
            anext_ref[...] = jnp.zeros_like(anext_ref)
            halo_ref[...] = jnp.zeros_like(halo_ref)

        taps, xc, r, i, a, a2, mult = _lru_pre(prec_ref, prev_ref, first, cw_ref, cb_ref, wa_ref, ba_ref, wx_ref, bx_ref, sp_ref)
        row = lax.broadcasted_iota(jnp.int32, (tc, W), 0)
        gelu, dgelu = _gelu_parts(pg_ref[...])
        dy = dy_ref[...]
        hcur = h_ref[...]
        dp_ref[:, 0:W] = dy * hcur * dgelu
        dp_ref[:, 2 * W:2 * W + MLA_Q_RANK] = dpq_ref[...]
        dp_ref[:, _KPE_START - MLA_KV_RANK:_KPE_START] = dpkv_ref[...]
        dp_ref[:, _KPE_START:P] = pltpu.roll(dkpe_ref[...], HEAD_LANES - MLA_NOPE, 1)[:, 0:P - _KPE_START]
        dh = dy * gelu
        a_next = jnp.where(row == tc - 1, anext_ref[0:1, :], pltpu.roll(a, tc - 1, 0))
        carry = gcar_ref[0:1, :]
        for t in reversed(range(tc // 8)):
            g = _scan8(a_next[8 * t:8 * t + 8], dh[8 * t:8 * t + 8], carry, True)
            g_ref[8 * t:8 * t + 8, :] = g
            carry = g[0:1, :]
        gcar_ref[...] = jnp.broadcast_to(carry, gcar_ref.shape)
        anext_ref[...] = jnp.broadcast_to(a[0:1, :], anext_ref.shape)
        G = g_ref[...]
        h_before = jnp.where(first, 0.0, hprev_ref[7:8, :])
        hprev = jnp.where(row == 0, h_before, pltpu.roll(hcur, 1, 0))
        d_a = G * hprev
        gx_ = G * xc
        d_mult = gx_ * i
        d_i = gx_ * mult
        dxc = G * (mult * i)
        d_la = d_a * a - d_mult * (a2 / mult)
        sp = sp_ref[...]
        d_r = d_la * (-LRU_C * sp)
        dsp_ref[...] += jnp.sum(d_la * (-LRU_C * r), axis=0, keepdims=True)
        dga = d_r * r * (1.0 - r)
        dgx = d_i * i * (1.0 - i)
        dba_ref[...] += jnp.sum(dga, axis=0, keepdims=True)
        dbx_ref[...] += jnp.sum(dgx, axis=0, keepdims=True)
        back = []
        for h in range(LRU_HEADS):
            sl = slice(h * HD, (h + 1) * HD)
            xh = xc[:, sl].astype(MXU_DTYPE)
            ah = dga[:, sl].astype(MXU_DTYPE)
            bh = dgx[:, sl].astype(MXU_DTYPE)
            tn = (((0,), (0,)), ((), ()))
            nt = (((1,), (1,)), ((), ()))
            dwa_ref[h] += lax.dot_general(xh, ah, tn, preferred_element_type=F32)
            dwx_ref[h] += lax.dot_general(xh, bh, tn, preferred_element_type=F32)
            back.append(lax.dot_general(ah, wa_ref[h].astype(MXU_DTYPE), nt, preferred_element_type=F32)
                        + lax.dot_general(bh, wx_ref[h].astype(MXU_DTYPE), nt, preferred_element_type=F32))
        dxc = dxc + jnp.concatenate(back, axis=1)
        dcb_ref[...] += jnp.sum(dxc, axis=0, keepdims=True)
        for k in range(CONV_WIDTH):
            dcw_ref[k:k + 1, :] += jnp.sum(dxc * taps[k], axis=0, keepdims=True)
        ext = jnp.concatenate([dxc, halo_ref[...]], axis=0)
        cw = cw_ref[...]
        acc = cw[CONV_WIDTH - 1:CONV_WIDTH, :] * dxc
        for k in range(CONV_WIDTH - 1):
            s = CONV_WIDTH - 1 - k
            acc = acc + cw[k:k + 1, :] * pltpu.roll(ext, tc + 8 - s, 0)[:tc]
        dp_ref[:, W:2 * W] = acc
        halo_ref[...] = dxc[0:8, :]

    rev = lambda j: nc - 1 - j
    cur = pl.BlockSpec((None, tc, W), lambda b, j: (b, rev(j), 0))
    rec = pl.BlockSpec((None, tc, W), lambda b, j: (b, rev(j), 1))
    prev = pl.BlockSpec((None, 8, W), lambda b, j: (b, jnp.maximum(rev(j) * (tc // 8) - 1, 0), 0))
    prev_rec = pl.BlockSpec((None, 8, W), lambda b, j: (b, jnp.maximum(rev(j) * (tc // 8) - 1, 0), 1))
    vec = pl.BlockSpec((1, W), lambda b, j: (0, 0))
    cws = pl.BlockSpec((CONV_WIDTH, W), lambda b, j: (0, 0))
    wsp = pl.BlockSpec((LRU_HEADS, HD, HD), lambda b, j: (0, 0, 0))
    vs = jax.ShapeDtypeStruct((1, W), F32)
    ws = jax.ShapeDtypeStruct((LRU_HEADS, HD, HD), F32)

    def rows(width):
        return pl.BlockSpec((None, tc, width), lambda b, j: (b, rev(j), 0))

    return pl.pallas_call(
        body, name="lru_bwd", grid=(B, nc),
        in_specs=[cur, rec, prev_rec, cur, prev, cur, cws, vec, wsp, vec, wsp, vec, vec, rows(MLA_Q_RANK), rows(MLA_KV_RANK), rows(HEAD_LANES)],
        out_specs=[rows(P), cws, vec, wsp, vec, wsp, vec, vec],
        out_shape=[jax.ShapeDtypeStruct((B, Tp, P), F32), jax.ShapeDtypeStruct((CONV_WIDTH, W), F32), vs, ws, vs, ws, vs, vs],
        scratch_shapes=[pltpu.VMEM((8, W), F32), pltpu.VMEM((8, W), F32), pltpu.VMEM((8, W), F32), pltpu.VMEM((tc, W), F32)],
        compiler_params=pltpu.CompilerParams(dimension_semantics=("arbitrary", "arbitrary")),
    )(p, p, p, hseq, hseq, dy, cw, cb, wa, ba, wx, bx, sp, dpq, dpkv, dkpe)


_Q_BLOCK = 2 * LRU_WIDTH // MLA_Q_RANK
_KV_BLOCK = (2 * LRU_WIDTH + MLA_Q_RANK) // MLA_KV_RANK
_KPE_START = 2 * LRU_WIDTH + MLA_Q_RANK + MLA_KV_RANK


@jax.custom_vjp
def even_front(p, cw, cb, wa, ba, wx, bx, sp, gq, gkv):
    return _even_front_fwd(p, cw, cb, wa, ba, wx, bx, sp, gq, gkv)[0]


def _even_front_fwd(p, cw, cb, wa, ba, wx, bx, sp, gq, gkv):
    B, Tp, W = p.shape
    p2d = p.reshape(B * Tp, W)
    y, hseq = _lru_fwd_call(p, cw, cb, wa, ba, wx, bx, sp)
    qn = _rms_fwd_call(p2d, gq, "q_norm_fwd", _Q_BLOCK)
    kvn = _rms_fwd_call(p2d, gkv, "kv_norm_fwd", _KV_BLOCK)
    kpe = jnp.pad(p[:, :, _KPE_START:], ((0, 0), (0, 0), (MLA_NOPE, HEAD_LANES - MLA_NOPE - MLA_ROPE)))
    return (y, qn, kvn, kpe), (p, hseq, cw, cb, wa, ba, wx, bx, sp, gq, gkv)


def _even_front_bwd(res, cts):
    p, hseq, cw, cb, wa, ba, wx, bx, sp, gq, gkv = res
    dy, dqn, dkvn, dkpe = cts
    B, Tp, W = p.shape
    p2d = p.reshape(B * Tp, W)
    dpq, dgq = _rms_bwd_call(p2d, gq, dqn, "q_norm_bwd", _Q_BLOCK)
    dpkv, dgkv = _rms_bwd_call(p2d, gkv, dkvn, "kv_norm_bwd", _KV_BLOCK)
    dp, dcw, dcb, dwa, dba, dwx, dbx, dsp = _lru_bwd_call(p, hseq, dy, cw, cb, wa, ba, wx, bx, sp, dpq.reshape(B, Tp, -1),
                                                          dpkv.reshape(B, Tp, -1), dkpe)
    return dp, dcw, dcb, dwa, dba, dwx, dbx, dsp, dgq.reshape(gq.shape), dgkv.reshape(gkv.shape)


even_front.defvjp(_even_front_fwd, _even_front_bwd)


def _rope_tables(pos, half):
    inv = ROPE_BASE ** (-jnp.arange(half, dtype=F32) / half)
    ang = pos.astype(F32)[:, None] * inv[None, :]
    return jnp.cos(ang), jnp.sin(ang)


_NT = (((1,), (1,)), ((), ()))
_TN = (((0,), (0,)), ((), ()))
HEAD_LANES = 128
_MLA_SCALE = (MLA_NOPE + MLA_ROPE) ** -0.5
_LOG2E = math.log2(math.e)


Q_BLOCK = 512


def _query_blocks(Tp):
    first = Tp % Q_BLOCK or Q_BLOCK
    return [(0, first)] + [(r, r + Q_BLOCK) for r in range(first, Tp, Q_BLOCK)]


def _mask_diagonal(s, fill):
    R, L = s.shape
    row = lax.broadcasted_iota(jnp.int32, (R, R), 0)
    col = lax.broadcasted_iota(jnp.int32, (R, R), 1)
    last = jnp.where(col <= row, s[:, L - R:], fill)
    return last if L == R else jnp.concatenate([s[:, :L - R], last], axis=1)


def _mla_rope_tables(pos):
    half = MLA_ROPE // 2
    cos, sin = _rope_tables(pos, half)
    T = pos.shape[0]
    ones, zeros = jnp.ones((T, MLA_NOPE), F32), jnp.zeros((T, MLA_NOPE), F32)
    tail1, tail0 = jnp.ones((T, HEAD_LANES - MLA_NOPE - MLA_ROPE), F32), jnp.zeros((T, HEAD_LANES - MLA_NOPE - MLA_ROPE), F32)
    zh = jnp.zeros((T, half), F32)
    c = jnp.concatenate([ones, cos, cos, tail1], axis=1)
    s_up = jnp.concatenate([zeros, -sin, zh, tail0], axis=1)
    s_down = jnp.concatenate([zeros, zh, sin, tail0], axis=1)
    return c, s_up, s_down


def _rope_lanes(x, c, s_up, s_down):
    half = MLA_ROPE // 2
    return x * c + pltpu.roll(x, HEAD_LANES - half, 1) * s_up + pltpu.roll(x, half, 1) * s_down


def _unrope_lanes(d, c, s_up, s_down):
    half = MLA_ROPE // 2
    return d * c + pltpu.roll(d * s_up, half, 1) + pltpu.roll(d * s_down, HEAD_LANES - half, 1)


def _mla_operands(q_ref, kv_ref, kpe_ref, c, s_up, s_down):
    lane = lax.broadcasted_iota(jnp.int32, kv_ref.shape, 1)
    qr = (_rope_lanes(q_ref[...].astype(F32), c, s_up, s_down) * (_MLA_SCALE * _LOG2E)).astype(MXU_DTYPE)
    kr = jnp.where(lane < MLA_NOPE, kv_ref[...].astype(F32), _rope_lanes(kpe_ref[...], c, s_up, s_down)).astype(MXU_DTYPE)
    return qr, kr, lane


def _mla_specs(Tp):
    head = pl.BlockSpec((None, Tp, HEAD_LANES), lambda b, h: (b, 0, h))
    shared = pl.BlockSpec((None, Tp, HEAD_LANES), lambda b, h: (b, 0, 0))
    tab = pl.BlockSpec((Tp, HEAD_LANES), lambda b, h: (0, 0))
    lse = pl.BlockSpec((None, None, Tp, 1), lambda b, h: (b, h, 0, 0))
    return head, shared, tab, lse


def _attn_fwd_call(q, kv, kpe, tabs):
    B, Tp, _ = q.shape

    def body(q_ref, kv_ref, kpe_ref, c_ref, su_ref, sd_ref, o_ref, lse_ref, qr_ref, kr_ref):
        qr, kr, lane = _mla_operands(q_ref, kv_ref, kpe_ref, c_ref[...], su_ref[...], sd_ref[...])
        qr_ref[...] = qr
        kr_ref[...] = kr
        for r0, L in _query_blocks(Tp):
            blk = slice(r0, L)
            s = _mask_diagonal(lax.dot_general(qr_ref[blk, :], kr_ref[0:L, :], _NT, preferred_element_type=F32), NEG_INF)
            m = jnp.max(s, axis=-1, keepdims=True)
            p = jnp.exp2(s - m)
            l = jnp.sum(p, axis=-1, keepdims=True)
            o = jnp.dot(p.astype(MXU_DTYPE), kv_ref[0:L, :].astype(MXU_DTYPE), preferred_element_type=F32)
            o_ref[blk, :] = jnp.where(lane[blk, :] >= MLA_NOPE, o / l, 0.0)
            lse_ref[blk, :] = m + jnp.log2(l)

    head, shared, tab, lse = _mla_specs(Tp)
    return pl.pallas_call(
        body, name="mla_attn_fwd", grid=(B, MLA_HEADS), in_specs=[head, head, shared, tab, tab, tab], out_specs=[head, lse],
        out_shape=[jax.ShapeDtypeStruct((B, Tp, MLA_HEADS * HEAD_LANES), F32), jax.ShapeDtypeStruct((B, MLA_HEADS, Tp, 1), F32)],
        scratch_shapes=[pltpu.VMEM((Tp, HEAD_LANES), MXU_DTYPE), pltpu.VMEM((Tp, HEAD_LANES), MXU_DTYPE)],
        compiler_params=pltpu.CompilerParams(dimension_semantics=("parallel", "parallel")),
    )(q, kv, kpe, *tabs)


def _attn_bwd_call(q, kv, kpe, tabs, o, lse, do):
    B, Tp, _ = q.shape

    def body(q_ref, kv_ref, kpe_ref, c_ref, su_ref, sd_ref, o_ref, lse_ref, do_ref, dq_ref, dkv_ref, dkpe_ref,
             qr_ref, kr_ref, dqa_ref, dka_ref, dva_ref):
        c, s_up, s_down = c_ref[...], su_ref[...], sd_ref[...]
        qr, kr, lane = _mla_operands(q_ref, kv_ref, kpe_ref, c, s_up, s_down)
        qr_ref[...] = qr
        kr_ref[...] = kr
        dka_ref[...] = jnp.zeros_like(dka_ref)
        dva_ref[...] = jnp.zeros_like(dva_ref)
        for r0, L in _query_blocks(Tp):
            blk = slice(r0, L)
            qb = qr_ref[blk, :]
            do = jnp.where(lane[blk, :] >= MLA_NOPE, do_ref[blk, :], 0.0)
            delta = jnp.sum(do * o_ref[blk, :], axis=-1, keepdims=True)
            s = _mask_diagonal(lax.dot_general(qb, kr_ref[0:L, :], _NT, preferred_element_type=F32), NEG_INF)
            p = jnp.exp2(s - lse_ref[blk, :])
            dob = do.astype(MXU_DTYPE)
            dva_ref[0:L, :] += lax.dot_general(p.astype(MXU_DTYPE), dob, _TN, preferred_element_type=F32)
            dp = lax.dot_general(dob, kv_ref[0:L, :].astype(MXU_DTYPE), _NT, preferred_element_type=F32)
            ds = (p * (dp - delta)).astype(MXU_DTYPE)
            dqa_ref[blk, :] = jnp.dot(ds, kr_ref[0:L, :], preferred_element_type=F32)
            dka_ref[0:L, :] += lax.dot_general(ds, qb, _TN, preferred_element_type=F32)
        dq_ref[...] = _unrope_lanes(dqa_ref[...] * _MLA_SCALE, c, s_up, s_down).astype(dq_ref.dtype)
        dk = dka_ref[...] * (1.0 / _LOG2E)
        dkv_ref[...] = jnp.where(lane < MLA_NOPE, dk, dva_ref[...]).astype(dkv_ref.dtype)
        dkpe = jnp.where(lane >= MLA_NOPE, _unrope_lanes(dk, c, s_up, s_down), 0.0)

        @pl.when(pl.program_id(1) == 0)
        def _():
            dkpe_ref[...] = dkpe

        @pl.when(pl.program_id(1) > 0)
        def _():
            dkpe_ref[...] += dkpe

    head, shared, tab, lse_spec = _mla_specs(Tp)
    wide = jax.ShapeDtypeStruct((B, Tp, MLA_HEADS * HEAD_LANES), q.dtype)
    acc = pltpu.VMEM((Tp, HEAD_LANES), F32)
    return pl.pallas_call(
        body, name="mla_attn_bwd", grid=(B, MLA_HEADS),
        in_specs=[head, head, shared, tab, tab, tab, head, lse_spec, head], out_specs=[head, head, shared],
        out_shape=[wide, wide, jax.ShapeDtypeStruct((B, Tp, HEAD_LANES), F32)],
        scratch_shapes=[pltpu.VMEM((Tp, HEAD_LANES), MXU_DTYPE), pltpu.VMEM((Tp, HEAD_LANES), MXU_DTYPE), acc, acc, acc],
        compiler_params=pltpu.CompilerParams(dimension_semantics=("parallel", "arbitrary")),
    )(q, kv, kpe, *tabs, o, lse, do)


@jax.custom_vjp
def mla_attention(q, kv, kpe, tabs):
    return _attn_fwd_call(q, kv, kpe, tabs)[0]


def _mla_attention_fwd(q, kv, kpe, tabs):
    o, lse = _attn_fwd_call(q, kv, kpe, tabs)
    return o, (q, kv, kpe, tabs, o, lse)


def _mla_attention_bwd(res, do):
    q, kv, kpe, tabs, o, lse = res
    dq, dkv, dkpe = _attn_bwd_call(q, kv, kpe, tabs, o, lse, do)
    return dq, dkv, dkpe, None


mla_attention.defvjp(_mla_attention_fwd, _mla_attention_bwd)


def _rope_halves(x, cos, sin):
    half = x.shape[1] // 2
    x1, x2 = x[:, :half], x[:, half:]
    return jnp.concatenate([x1 * cos - x2 * sin, x1 * sin + x2 * cos], axis=1)


def _unrope_halves(d, cos, sin):
    half = d.shape[1] // 2
    d1, d2 = d[:, :half], d[:, half:]
    return jnp.concatenate([d1 * cos + d2 * sin, d2 * cos - d1 * sin], axis=1)


_RET_K_SCALE = RET_QK_DIM ** -0.5
_RET_Q_BLOCKS = RET_HEADS
_RET_V_BLOCK0 = 2 * RET_HEADS * RET_QK_DIM // RET_V_DIM
_RET_G_BLOCK0 = _RET_V_BLOCK0 + RET_HEADS


def _ret_specs(Tp):
    q = pl.BlockSpec((None, Tp, RET_QK_DIM), lambda b, h: (b, 0, h))
    k = pl.BlockSpec((None, Tp, RET_QK_DIM), lambda b, h: (b, 0, _RET_Q_BLOCKS + h))
    v = pl.BlockSpec((None, Tp, RET_V_DIM), lambda b, h: (b, 0, _RET_V_BLOCK0 + h))
    tab = pl.BlockSpec((Tp, RET_QK_DIM // 2), lambda b, h: (0, 0))
    lg = pl.BlockSpec((None, 1, 1), lambda b, h: (h, 0, 0))
    return q, k, v, tab, lg


def _ret_operands(q_ref, k_ref, cos, sin, lg):
    t = lax.broadcasted_iota(jnp.int32, (q_ref.shape[0], 1), 0).astype(F32)
    grow, shrink = jnp.exp(-lg * t), jnp.exp(lg * t)
    qs = (_rope_halves(q_ref[...].astype(F32), cos, sin) * shrink).astype(MXU_DTYPE)
    ks = (_rope_halves(k_ref[...].astype(F32), cos, sin) * (grow * _RET_K_SCALE)).astype(MXU_DTYPE)
    return qs, ks, shrink, grow * _RET_K_SCALE


def _ret_core_fwd_call(p, cos, sin, lg):
    B, Tp, _ = p.shape

    def body(q_ref, k_ref, v_ref, cos_ref, sin_ref, lg_ref, o_ref, qs_ref, ks_ref):
        qs_ref[...], ks_ref[...], _, _ = _ret_operands(q_ref, k_ref, cos_ref[...], sin_ref[...], lg_ref[...])
        for r0, L in _query_blocks(Tp):
            blk = slice(r0, L)
            s = _mask_diagonal(lax.dot_general(qs_ref[blk, :], ks_ref[0:L, :], _NT, preferred_element_type=F32), 0.0)
            o_ref[blk, :] = jnp.dot(s.astype(MXU_DTYPE), v_ref[0:L, :].astype(MXU_DTYPE), preferred_element_type=F32)

    q, k, v, tab, lgs = _ret_specs(Tp)
    return pl.pallas_call(
        body, name="retention_fwd", grid=(B, RET_HEADS), in_specs=[q, k, v, tab, tab, lgs],
        out_specs=pl.BlockSpec((None, Tp, RET_V_DIM), lambda b, h: (b, 0, h)),
        out_shape=jax.ShapeDtypeStruct((B, Tp, RET_HEADS * RET_V_DIM), F32),
        scratch_shapes=[pltpu.VMEM((Tp, RET_QK_DIM), MXU_DTYPE), pltpu.VMEM((Tp, RET_QK_DIM), MXU_DTYPE)],
        compiler_params=pltpu.CompilerParams(dimension_semantics=("parallel", "parallel")),
    )(p, p, p, cos, sin, lg)


def _ret_core_bwd_call(p, do, cos, sin, lg):
    B, Tp, _ = p.shape

    def body(q_ref, k_ref, v_ref, do_ref, cos_ref, sin_ref, lg_ref, dq_ref, dk_ref, dv_ref, qs_ref, ks_ref, dqa_ref, dka_ref, dva_ref):
        cos_, sin_ = cos_ref[...], sin_ref[...]
        qs_ref[...], ks_ref[...], q_scale, k_scale = _ret_operands(q_ref, k_ref, cos_, sin_, lg_ref[...])
        dka_ref[...] = jnp.zeros_like(dka_ref)
        dva_ref[...] = jnp.zeros_like(dva_ref)
        for r0, L in _query_blocks(Tp):
            blk = slice(r0, L)
            qb = qs_ref[blk, :]
            dob = do_ref[blk, :].astype(MXU_DTYPE)
            s = _mask_diagonal(lax.dot_general(qb, ks_ref[0:L, :], _NT, preferred_element_type=F32), 0.0).astype(MXU_DTYPE)
            dva_ref[0:L, :] += lax.dot_general(s, dob, _TN, preferred_element_type=F32)
            ds = _mask_diagonal(lax.dot_general(dob, v_ref[0:L, :].astype(MXU_DTYPE), _NT, preferred_element_type=F32), 0.0).astype(MXU_DTYPE)
            dqa_ref[blk, :] = jnp.dot(ds, ks_ref[0:L, :], preferred_element_type=F32)
            dka_ref[0:L, :] += lax.dot_general(ds, qb, _TN, preferred_element_type=F32)
        dq_ref[...] = _unrope_halves(dqa_ref[...] * q_scale, cos_, sin_).astype(dq_ref.dtype)
        dk_ref[...] = _unrope_halves(dka_ref[...] * k_scale, cos_, sin_).astype(dk_ref.dtype)
        dv_ref[...] = dva_ref[...].astype(dv_ref.dtype)

    q, k, v, tab, lgs = _ret_specs(Tp)
    qk_out = pl.BlockSpec((None, Tp, RET_QK_DIM), lambda b, h: (b, 0, h))
    v_out = pl.BlockSpec((None, Tp, RET_V_DIM), lambda b, h: (b, 0, h))
    return pl.pallas_call(
        body, name="retention_bwd", grid=(B, RET_HEADS), in_specs=[q, k, v, v_out, tab, tab, lgs],
        out_specs=[qk_out, qk_out, v_out],
        out_shape=[jax.ShapeDtypeStruct((B, Tp, RET_HEADS * RET_QK_DIM), p.dtype), jax.ShapeDtypeStruct((B, Tp, RET_HEADS * RET_QK_DIM), p.dtype),
                   jax.ShapeDtypeStruct((B, Tp, RET_HEADS * RET_V_DIM), p.dtype)],
        scratch_shapes=[pltpu.VMEM((Tp, RET_QK_DIM), MXU_DTYPE), pltpu.VMEM((Tp, RET_QK_DIM), MXU_DTYPE),
                        pltpu.VMEM((Tp, RET_QK_DIM), F32), pltpu.VMEM((Tp, RET_QK_DIM), F32), pltpu.VMEM((Tp, RET_V_DIM), F32)],
        compiler_params=pltpu.CompilerParams(dimension_semantics=("parallel", "parallel")),
    )(p, p, p, do, cos, sin, lg)


def _ret_gate_specs(M):
    tm = _pick(M, 1088, 8)
    head = pl.BlockSpec((tm, RET_V_DIM), lambda i, h: (i, h))
    gate = pl.BlockSpec((tm, RET_V_DIM), lambda i, h: (i, _RET_G_BLOCK0 + h))
    return tm, head, gate


def _ret_gate_fwd_call(o, p2d):
    M = o.shape[0]
    tm, head, gate = _ret_gate_specs(M)

    def body(o_ref, g_ref, y_ref):
        ov = o_ref[...]
        gv = g_ref[...].astype(F32)
        rstd = lax.rsqrt(jnp.mean(ov * ov, axis=-1, keepdims=True) + EPS)
        y_ref[...] = (gv * _sigmoid(gv)) * (ov * rstd)

    return pl.pallas_call(
        body, name="retention_gate_fwd", grid=(M // tm, RET_HEADS), in_specs=[head, gate], out_specs=head,
        out_shape=jax.ShapeDtypeStruct(o.shape, F32),
        compiler_params=pltpu.CompilerParams(dimension_semantics=("parallel", "parallel")),
    )(o, p2d)


def _ret_gate_bwd_call(o, p2d, dy):
    M = o.shape[0]
    tm, head, gate = _ret_gate_specs(M)

    def body(o_ref, g_ref, dy_ref, do_ref, dg_ref):
        ov = o_ref[...]
        gv = g_ref[...].astype(F32)
        dy = dy_ref[...]
        rstd = lax.rsqrt(jnp.mean(ov * ov, axis=-1, keepdims=True) + EPS)
        on = ov * rstd
        sg = _sigmoid(gv)
        dg_ref[...] = (dy * on * (sg * (1.0 + gv * (1.0 - sg)))).astype(dg_ref.dtype)
        don = dy * (gv * sg)
        do_ref[...] = (rstd * (don - on * jnp.mean(don * on, axis=-1, keepdims=True))).astype(do_ref.dtype)

    shp = jax.ShapeDtypeStruct(o.shape, p2d.dtype)
    return pl.pallas_call(
        body, name="retention_gate_bwd", grid=(M // tm, RET_HEADS), in_specs=[head, gate, head], out_specs=[head, head],
        out_shape=[shp, shp],
        compiler_params=pltpu.CompilerParams(dimension_semantics=("parallel", "parallel")),
    )(o, p2d, dy)


def _log_gamma():
    return jnp.log(1.0 - 2.0 ** (-5.0 - jnp.arange(RET_HEADS, dtype=F32))).reshape(RET_HEADS, 1, 1)


@functools.partial(jax.custom_vjp, nondiff_argnums=(9,))
def retention_block(h, w_in, w_out, w_in_grad_slot, w_out_grad_slot, g, b, cos, sin, dims):
    return _retention_block_fwd(h, w_in, w_out, w_in_grad_slot, w_out_grad_slot, g, b, cos, sin, dims)[0]


def _retention_block_fwd(h, w_in, w_out, w_in_grad_slot, w_out_grad_slot, g, b, cos, sin, dims):
    B, Tp = dims
    p = _mm_nn(h, w_in, False, "od_w_in_fwd", out_dtype=MXU_DTYPE)
    o = _ret_core_fwd_call(p.reshape(B, Tp, -1), cos, sin, _log_gamma())
    y = _ret_gate_fwd_call(o.reshape(B * Tp, -1), p)
    out, z = _mm_nn(y, w_out, False, "od_w_out_norm_fwd", norm=(h, g, b))
    return out, (h, p, o, y, z, w_in, w_out, g, cos, sin, jnp.zeros((), w_in_grad_slot.dtype))


def _retention_block_bwd(dims, res, dout):
    B, Tp = dims
    h, p, o, y, z, w_in, w_out, g, cos, sin, slot_like = res
    dz, dg, db = _ln_bwd_call(z, g, dout, "od_w_out_norm_bwd")
    dy = _mm_nt(dz, w_out, None, "od_w_out_dx")
    dw_out = _mm_tn(y, dz, False, "od_w_out_dw", 1, slot_like.dtype)
    do, dgate = _ret_gate_bwd_call(o.reshape(B * Tp, -1), p, dy)
    dq, dk, dv = _ret_core_bwd_call(p.reshape(B, Tp, -1), do.reshape(B, Tp, -1), cos, sin, _log_gamma())
    dp = jnp.concatenate([dq.reshape(B * Tp, -1), dk.reshape(B * Tp, -1), dv.reshape(B * Tp, -1), dgate], axis=-1)
    dh = _mm_nt(dp, w_in, None, "od_w_in_dx", plus=dz)
    dw_in = _mm_tn(h, dp, False, "od_w_in_dw", N_CHIPS, slot_like.dtype)
    return dh, None, None, dw_in, dw_out, dg.reshape(g.shape), db.reshape(g.shape), None, None


retention_block.defvjp(_retention_block_fwd, _retention_block_bwd)


def _heads_to_lanes(w):
    K = w.shape[0]
    w = w.reshape(K, MLA_HEADS, MLA_NOPE + MLA_ROPE)
    return jnp.pad(w, ((0, 0), (0, 0), (0, HEAD_LANES - MLA_NOPE - MLA_ROPE))).reshape(K, MLA_HEADS * HEAD_LANES)


def _out_rows_to_lanes(w):
    N = w.shape[1]
    att = w[LRU_WIDTH:].reshape(MLA_HEADS, MLA_V, N)
    att = jnp.pad(att, ((0, 0), (HEAD_LANES - MLA_V, 0), (0, 0))).reshape(MLA_HEADS * HEAD_LANES, N)
    return jnp.concatenate([w[:LRU_WIDTH], att], axis=0)


def _seq_dims(x):
    B, S, D = x.shape
    T = S + N_META
    Tp = _round_up(T, SEQ_BLOCK)
    return B, S, T, Tp


def _mixer0(diff, w, token):
    x = diff["x"]
    B, S, T, Tp = _seq_dims(x)
    D = x.shape[-1]
    M = B * Tp
    pos = jnp.arange(Tp, dtype=jnp.int32)

    def mm(a, name, act=False, out_dtype=F32, layout=lambda m: m, col_shards=1):
        return matmul(a, layout(w[name]), layout(diff[name]), act, name, out_dtype, col_shards)

    meta = jnp.broadcast_to(diff["meta_tokens"][None], (B, N_META, D))
    h = jnp.concatenate([meta, x + token, jnp.zeros((B, Tp - T, D), F32)], axis=1).reshape(M, D)
    p = mm(h, "ev_w_in")
    sp = jax.nn.softplus(-diff["ev_lru_lambda"]).reshape(1, LRU_WIDTH)
    y_rec, qn, kvn, kpe = even_front(
        p.reshape(B, Tp, -1), diff["ev_conv_w"].reshape(CONV_WIDTH, LRU_WIDTH), diff["ev_conv_b"].reshape(1, LRU_WIDTH),
        diff["ev_w_rg_a"].reshape(LRU_HEADS, LRU_HEAD_DIM, LRU_HEAD_DIM), diff["ev_b_rg_a"].reshape(1, LRU_WIDTH),
        diff["ev_w_rg_x"].reshape(LRU_HEADS, LRU_HEAD_DIM, LRU_HEAD_DIM), diff["ev_b_rg_x"].reshape(1, LRU_WIDTH),
        sp, diff["ev_q_norm_g"].reshape(-1), diff["ev_kv_norm_g"].reshape(-1))
    y_rec = y_rec.reshape(M, LRU_WIDTH)
    q = mm(qn, "ev_w_uq", out_dtype=MXU_DTYPE, layout=_heads_to_lanes).reshape(B, Tp, -1)
    kv = mm(kvn, "ev_w_ukv", out_dtype=MXU_DTYPE).reshape(B, Tp, -1)
    y_att = mla_attention(q, kv, kpe, _mla_rope_tables(pos)).reshape(M, -1)
    return out_block(h, jnp.concatenate([y_rec, y_att], axis=-1), _out_rows_to_lanes(w["ev_w_out"]), _out_rows_to_lanes(diff["ev_w_out"]),
                     diff["ln_mix_g"], diff["ln_mix_b"], "ev_w_out")


def _mlp0(diff, h, w):
    return mlp_block(h, w["mlp_w1_0"], w["mlp_w2_0"], diff["mlp_w1_0"], diff["mlp_w2_0"], diff["ln_mlp_g"], diff["ln_mlp_b"], "mlp0")


def _layer1_loss(diff, h, w, tgt):
    B, S, T, Tp = _seq_dims(tgt)
    D = tgt.shape[-1]
    pos = jnp.arange(Tp, dtype=jnp.int32)

    cos, sin = _rope_tables(pos, RET_QK_DIM // 2)
    h = retention_block(h, w["od_w_in"], w["od_w_out"], diff["od_w_in"], diff["od_w_out"], diff["ln_mix_g"], diff["ln_mix_b"], cos, sin, (B, Tp))
    h = mlp_block(h, w["mlp_w1_1"], w["mlp_w2_1"], diff["mlp_w1_1"], diff["mlp_w2_1"], diff["ln_mlp_g"], diff["ln_mlp_b"], "mlp1")
    return loss_head(h.reshape(B, Tp, D), jnp.pad(tgt, ((0, 0), (N_META, Tp - T), (0, 0))), S)


_HBM = pl.BlockSpec(memory_space=pltpu.HBM)


def _place():
    return lax.axis_index("x"), lax.axis_index("y"), lax.axis_index("c")


def _other_chips(x, y):
    return [(1 - x, y), (x, 1 - y), (1 - x, 1 - y)]


def _chunks(rows, sublanes, most):
    for q in range(most, 0, -1):
        if rows % (q * sublanes) == 0:
            return q
    return 1


def _sublanes(dtype):
    return 8 * 4 // jnp.dtype(dtype).itemsize


def _gather_pieces(bufs):
    plan, first = [], []
    for b in bufs:
        Rh = b.shape[0] // 2
        Q = _chunks(Rh, _sublanes(b.dtype), 4) if Rh * b.shape[1] * b.dtype.itemsize > (1 << 20) else 1
        first.append(3 * sum(q for _, q, _ in plan))
        plan.append((Rh, Q, Rh // Q))
    return plan, first, 3 * sum(q for _, q, _ in plan)


def _allgather_chips(bufs, name):
    n = len(bufs)
    plan, first, n_sems = _gather_pieces(bufs)

    def body(*refs):
        x_refs, out_refs, (send_sems, recv_sems) = refs[:n], refs[n:2 * n], refs[2 * n:]
        x, y, c = _place()
        sibling = (x, y, 1 - c)
        chips = _other_chips(x, y)

        def copy(k, src, dst, to):
            return pltpu.make_async_remote_copy(src_ref=src, dst_ref=dst, send_sem=send_sems.at[k], recv_sem=recv_sems.at[k],
                                                device_id=to, device_id_type=MESH)

        def piece(i, cx, cy, hc, q):
            Rh, _, ch = plan[i]
            return out_refs[i].at[2 * cx + cy, pl.ds(hc * Rh + q * ch, ch), :]

        slots = [(i, q, j) for i in range(n) for q in range(plan[i][1]) for j in range(3)]
        sem = {(i, q, j): first[i] + 3 * q + j for i, q, j in slots}
        sent = [copy(sem[i, q, j], x_refs[i].at[pl.ds(c * plan[i][0] + q * plan[i][2], plan[i][2]), :], piece(i, x, y, c, q), (*chips[j], c))
                for i, q, j in slots]
        for cp in sent:
            cp.start()
        passed = []
        for i, q, j in slots:
            landed = piece(i, *chips[j], c, q)
            copy(sem[i, q, j], landed, landed, sibling).wait_recv()
            fwd = copy(n_sems + sem[i, q, j], landed, landed, sibling)
            fwd.start()
            passed.append(fwd)
        for i, q, j in slots:
            theirs = piece(i, *chips[j], 1 - c, q)
            copy(n_sems + sem[i, q, j], theirs, theirs, sibling).wait_recv()
        for cp in sent + passed:
            cp.wait_send()

    return pl.pallas_call(
        body, name=name, in_specs=[_HBM] * n, out_specs=[_HBM] * n,
        out_shape=[jax.ShapeDtypeStruct((N_CHIPS,) + b.shape, b.dtype) for b in bufs],
        scratch_shapes=[pltpu.SemaphoreType.DMA((2 * n_sems,)), pltpu.SemaphoreType.DMA((2 * n_sems,))],
    )(*bufs)


def _with_own(gathered, own):
    my = 2 * lax.axis_index("x") + lax.axis_index("y")
    return lax.dynamic_update_slice(gathered, own[None], (my, 0, 0))


def _sibling_gather(fs, name):
    n = len(fs)

    def body(*refs):
        out_refs, (send_sems, recv_sems) = refs[n:2 * n], refs[2 * n:]
        x, y, c = _place()
        copies = [pltpu.make_async_remote_copy(src_ref=out_ref.at[c], dst_ref=out_ref.at[c], send_sem=send_sems.at[i], recv_sem=recv_sems.at[i],
                                               device_id=(x, y, 1 - c), device_id_type=MESH) for i, out_ref in enumerate(out_refs)]
        for cp in copies:
            cp.start()
        for cp in copies:
            cp.wait()

    return pl.pallas_call(
        body, name=name, in_specs=[_HBM] * n, out_specs=[_HBM] * n,
        out_shape=[jax.ShapeDtypeStruct(f.shape, f.dtype) for f in fs], input_output_aliases={i: i for i in range(n)},
        scratch_shapes=[pltpu.SemaphoreType.DMA((n,)), pltpu.SemaphoreType.DMA((n,))],
    )(*fs)


def _axis_scalar(name):
    return lax.axis_index(name).astype(jnp.int32).reshape(1)


_SEM = pl.BlockSpec(memory_space=pltpu.SEMAPHORE)
_ANY = pl.BlockSpec(memory_space=pl.ANY)
_EFFECT = pltpu.SideEffectType.DATAFLOW_SIDE_EFFECTING


def _in_hbm(a):
    return pltpu.with_memory_space_constraint(a, pltpu.HBM)


def _half_copies(x_refs, land_refs, send_sems, recv_sems, arriving):
    x, y, c = _place()
    copies = []
    for i, (x_ref, land_ref) in enumerate(zip(x_refs, land_refs)):
        Rh = x_ref.shape[0] // 2
        rows = pl.ds(c * Rh, Rh)
        for j, (cx, cy) in enumerate(_other_chips(x, y)):
            copies.append(pltpu.make_async_remote_copy(
                src_ref=x_ref.at[rows, :], dst_ref=land_ref.at[2 * cx + cy if arriving else 2 * x + y, rows, :],
                send_sem=send_sems.at[3 * i + j], recv_sem=recv_sems.at[3 * i + j], device_id=(cx, cy, c), device_id_type=MESH))
    return copies


def _allgather_start(bufs, name):
    n = len(bufs)

    def body(*refs):
        x_refs, land_refs, (send_sems, recv_sems), token = refs[:n], refs[n:2 * n], refs[2 * n:2 * n + 2], refs[-1]
        for cp in _half_copies(x_refs, land_refs, send_sems, recv_sems, False):
            cp.start()
        token[...] = jnp.zeros_like(token)

    lands = [lax.empty((N_CHIPS,) + b.shape, b.dtype) for b in bufs]
    out = pl.pallas_call(
        body, name=name,
        out_shape=(pltpu.SemaphoreType.DMA((3 * n,)), pltpu.SemaphoreType.DMA((3 * n,)), *[pltpu.HBM(a.shape, a.dtype) for a in bufs + lands],
                   jax.ShapeDtypeStruct((8, 128), F32)),
        in_specs=[_HBM] * (2 * n), out_specs=(_SEM, _SEM, *[_HBM] * (2 * n), pl.BlockSpec(memory_space=pltpu.VMEM)),
        input_output_aliases={i: 2 + i for i in range(2 * n)}, compiler_params=pltpu.CompilerParams(has_side_effects=_EFFECT),
    )(*[_in_hbm(a) for a in bufs + lands])
    return (out[0], out[1], list(out[2:2 + n]), list(out[2 + n:2 + 2 * n])), out[-1][0, 0]


def _allgather_wait(pending, after, name):
    send_sems, recv_sems, bufs, lands = pending
    n = len(bufs)

    def body(*refs):
        x_refs, land_refs, send_sems, recv_sems = refs[:n], refs[n:2 * n], refs[2 * n], refs[2 * n + 1]
        for cp in _half_copies(x_refs, land_refs, send_sems, recv_sems, False):
            cp.wait_send()
        for cp in _half_copies(x_refs, land_refs, send_sems, recv_sems, True):
            cp.wait_recv()

    out = pl.pallas_call(
        body, name=name, out_shape=tuple(pltpu.HBM(a.shape, a.dtype) for a in bufs + lands),
        in_specs=[_HBM] * (2 * n) + [_SEM, _SEM, _ANY], out_specs=tuple([_HBM] * (2 * n)), input_output_aliases={i: i for i in range(2 * n)},
        compiler_params=pltpu.CompilerParams(has_side_effects=_EFFECT),
    )(*bufs, *lands, send_sems, recv_sems, after)
    return list(out[n:])


def _sibling_forward(lands, name):
    n = len(lands)
    plan, first, n_sems = _gather_pieces([jax.ShapeDtypeStruct(l.shape[1:], l.dtype) for l in lands])

    def body(*refs):
        out_refs, (send_sems, recv_sems) = refs[n:2 * n], refs[2 * n:]
        x, y, c = _place()

        def copies(hc):
            return [pltpu.make_async_remote_copy(
                        src_ref=out_refs[i].at[2 * cx + cy, pl.ds(hc * plan[i][0] + q * plan[i][2], plan[i][2]), :],
                        dst_ref=out_refs[i].at[2 * cx + cy, pl.ds(hc * plan[i][0] + q * plan[i][2], plan[i][2]), :],
                        send_sem=send_sems.at[first[i] + 3 * q + j], recv_sem=recv_sems.at[first[i] + 3 * q + j],
                        device_id=(x, y, 1 - c), device_id_type=MESH)
                    for i in range(n) for q in range(plan[i][1]) for j, (cx, cy) in enumerate(_other_chips(x, y))]

        mine = copies(c)
        for cp in mine:
            cp.start()
        for cp in mine:
            cp.wait_send()
        for cp in copies(1 - c):
            cp.wait_recv()

    return pl.pallas_call(
        body, name=name, in_specs=[_HBM] * n, out_specs=[_HBM] * n, out_shape=[jax.ShapeDtypeStruct(l.shape, l.dtype) for l in lands],
        input_output_aliases={i: i for i in range(n)},
        scratch_shapes=[pltpu.SemaphoreType.DMA((n_sems,)), pltpu.SemaphoreType.DMA((n_sems,))],
    )(*lands)


N_PEERS = 7


def _direct_copies(p_refs, t_refs, send_sems, recv_sems):
    x, y, c = _place()
    copies = []
    for i, (p_ref, t_ref) in enumerate(zip(p_refs, t_refs)):
        for f in range(1, N_PEERS + 1):
            px, py, pc = x ^ (f >> 2), y ^ ((f >> 1) & 1), c ^ (f & 1)
            copies.append(pltpu.make_async_remote_copy(
                src_ref=p_ref.at[2 * px + py, pc], dst_ref=t_ref.at[f - 1], send_sem=send_sems.at[N_PEERS * i + f - 1],
                recv_sem=recv_sems.at[N_PEERS * i + f - 1], device_id=(px, py, pc), device_id_type=MESH))
    return copies


def _direct_scatter_start(ps, name):
    n = len(ps)

    def body(*refs):
        p_refs, t_refs, (send_sems, recv_sems), token = refs[:n], refs[n:2 * n], refs[2 * n:2 * n + 2], refs[-1]
        for cp in _direct_copies(p_refs, t_refs, send_sems, recv_sems):
            cp.start()
        token[...] = jnp.zeros_like(token)

    lands = [lax.empty((N_PEERS,) + p.shape[2:], p.dtype) for p in ps]
    out = pl.pallas_call(
        body, name=name,
        out_shape=(pltpu.SemaphoreType.DMA((N_PEERS * n,)), pltpu.SemaphoreType.DMA((N_PEERS * n,)),
                   *[pltpu.HBM(a.shape, a.dtype) for a in ps + lands], jax.ShapeDtypeStruct((8, 128), F32)),
        in_specs=[_HBM] * (2 * n), out_specs=(_SEM, _SEM, *[_HBM] * (2 * n), pl.BlockSpec(memory_space=pltpu.VMEM)),
        input_output_aliases={i: 2 + i for i in range(2 * n)}, compiler_params=pltpu.CompilerParams(has_side_effects=_EFFECT),
    )(*[_in_hbm(a) for a in ps + lands])
    return (out[0], out[1], list(out[2:2 + n]), list(out[2 + n:2 + 2 * n])), out[-1][0, 0]


def _direct_scatter_wait(pending, after, name):
    send_sems, recv_sems, ps, lands = pending
    n = len(ps)

    def body(*refs):
        p_refs, t_refs, send_sems, recv_sems = refs[:n], refs[n:2 * n], refs[2 * n], refs[2 * n + 1]
        for cp in _direct_copies(p_refs, t_refs, send_sems, recv_sems):
            cp.wait_send()
            cp.wait_recv()

    out = pl.pallas_call(
        body, name=name, out_shape=tuple(pltpu.HBM(a.shape, a.dtype) for a in ps + lands),
        in_specs=[_HBM] * (2 * n) + [_SEM, _SEM, _ANY], out_specs=tuple([_HBM] * (2 * n)),
        input_output_aliases={i: i for i in range(2 * n)}, compiler_params=pltpu.CompilerParams(has_side_effects=_EFFECT),
    )(*ps, *lands, send_sems, recv_sems, after)
    return list(out[:n]), list(out[n:])


def _sum_direct(p, t, name):
    _, _, R, C = p.shape
    tr = _pick(R, 512, 16)

    def body(x_ref, y_ref, c_ref, p_ref, t_ref, o_ref):
        acc = p_ref[...].astype(F32)
        for f in range(N_PEERS):
            acc = acc + t_ref[f].astype(F32)
        o_ref[...] = acc

    grid_spec = pltpu.PrefetchScalarGridSpec(
        num_scalar_prefetch=3, grid=(R // tr,),
        in_specs=[pl.BlockSpec((None, None, tr, C), lambda i, x_ref, y_ref, c_ref: (2 * x_ref[0] + y_ref[0], c_ref[0], i, 0)),
                  pl.BlockSpec((N_PEERS, tr, C), lambda i, x_ref, y_ref, c_ref: (0, i, 0))],
        out_specs=pl.BlockSpec((None, tr, C), lambda i, x_ref, y_ref, c_ref: (c_ref[0], i, 0)))
    return pl.pallas_call(body, name=name, grid_spec=grid_spec, out_shape=jax.ShapeDtypeStruct((2, R, C), F32),
                          compiler_params=pltpu.CompilerParams(dimension_semantics=("parallel",)))(
        _axis_scalar("x"), _axis_scalar("y"), _axis_scalar("c"), p, t)


def _adamw(w, g, m, v, name):
    R, C = w.shape
    tr = _pick(R, 256, 8)

    def body(w_ref, g_ref, m_ref, v_ref, d_ref, nm_ref, nv_ref):
        g_ = g_ref[...]
        m_ = ADAM_B1 * m_ref[...] + (1.0 - ADAM_B1) * g_
        v_ = ADAM_B2 * v_ref[...] + (1.0 - ADAM_B2) * (g_ * g_)
        m_hat = m_ / (1.0 - ADAM_B1 ** ADAM_STEP)
        v_hat = v_ / (1.0 - ADAM_B2 ** ADAM_STEP)
        d_ref[...] = -ADAM_LR * (m_hat / (jnp.sqrt(v_hat) + ADAM_EPS) + ADAM_WD * w_ref[...])
        nm_ref[...] = m_
        nv_ref[...] = v_

    row = pl.BlockSpec((tr, C), lambda i: (i, 0))
    shp = jax.ShapeDtypeStruct((R, C), F32)
    return pl.pallas_call(body, name=name, grid=(R // tr,), in_specs=[row] * 4, out_specs=[row] * 3, out_shape=[shp] * 3,
                          compiler_params=pltpu.CompilerParams(dimension_semantics=("parallel",)))(w, g, m, v)


BIG_SPECS = (("ev_w_in", 1024, 1440, 1), ("ev_w_uq", 256, 768, 1), ("ev_w_ukv", 128, 1024, 1), ("ev_w_out", 1024, 1024, 0),
             ("od_w_in", 1024, 6144, 1), ("od_w_out", 2048, 1024, 0), ("mlp_w1_0", 1024, 4096, 1), ("mlp_w1_1", 1024, 4096, 1),
             ("mlp_w2_0", 4096, 1024, 0), ("mlp_w2_1", 4096, 1024, 0))
BIG_PARAMS = (("ev_w_in", ("ev_w_in",)), ("ev_w_uq", ("ev_w_uq",)), ("ev_w_ukv", ("ev_w_ukv",)), ("ev_w_out", ("ev_w_out",)),
              ("od_w_in", ("od_w_in",)), ("od_w_out", ("od_w_out",)), ("mlp_w1", ("mlp_w1_0", "mlp_w1_1")),
              ("mlp_w2", ("mlp_w2_0", "mlp_w2_1")))
REPLICATED = ("ev_conv_b", "ev_w_rg_a", "ev_b_rg_a", "ev_w_rg_x", "ev_b_rg_x", "ev_lru_lambda", "ev_q_norm_g", "ev_kv_norm_g",
              "ln_mix_g", "ln_mix_b", "ln_mlp_g", "ln_mlp_b")
SMALL_SHARDED = ("meta_tokens", "ev_conv_w")
COL_SHARD_GRADS = ("od_w_in", "mlp_w1_0", "mlp_w1_1")
MATRIX_GROUPS = (("ev_w_in", "ev_w_uq", "ev_w_ukv", "ev_w_out"), ("mlp_w1_0", "mlp_w2_0"), ("od_w_in", "od_w_out", "mlp_w1_1", "mlp_w2_1"))
LAYER_NORMS = ("ln_mix_g", "ln_mix_b", "ln_mlp_g", "ln_mlp_b")
WEIGHT_NAMES = ("meta_tokens", "ev_w_in", "ev_conv_w", "ev_conv_b", "ev_w_rg_a", "ev_b_rg_a", "ev_w_rg_x", "ev_b_rg_x",
                "ev_lru_lambda", "ev_q_norm_g", "ev_w_uq", "ev_kv_norm_g", "ev_w_ukv", "ev_w_out", "od_w_in", "od_w_out",
                "ln_mix_g", "ln_mix_b", "mlp_w1", "mlp_w2", "ln_mlp_g", "ln_mlp_b")


def _to_rows(flat, row_align):
    n = flat.shape[-1]
    rows = _round_up(-(-n // PACK_COLS), row_align)
    pad = rows * PACK_COLS - n
    if pad:
        flat = jnp.pad(flat, [(0, 0)] * (flat.ndim - 1) + [(0, pad)])
    return flat.reshape(flat.shape[:-1] + (rows, PACK_COLS))


def _shard_shape(K, N, axis):
    return (K // N_CHIPS, N) if axis == 0 else (K, N // N_CHIPS)


def _gather_shards(stacked, K, N, axis):
    if axis == 0:
        return stacked.reshape(K, N)
    return stacked.transpose(1, 0, 2).reshape(K, N)


def _split_shards(full, K, N, axis):
    if axis == 0:
        return full.reshape(N_CHIPS, -1)
    return full.reshape(K, N_CHIPS, N // N_CHIPS).transpose(1, 0, 2).reshape(N_CHIPS, -1)


def kernel(x, meta_tokens, ev_w_in, ev_conv_w, ev_conv_b, ev_w_rg_a, ev_b_rg_a, ev_w_rg_x, ev_b_rg_x, ev_lru_lambda, ev_q_norm_g, ev_w_uq, ev_kv_norm_g, ev_w_ukv, ev_w_out, od_w_in, od_w_out, ln_mix_g, ln_mix_b, mlp_w1, mlp_w2, ln_mlp_g, ln_mlp_b, loss_target, m_meta_tokens, m_ev_w_in, m_ev_conv_w, m_ev_conv_b, m_ev_w_rg_a, m_ev_b_rg_a, m_ev_w_rg_x, m_ev_b_rg_x, m_ev_lru_lambda, m_ev_q_norm_g, m_ev_w_uq, m_ev_kv_norm_g, m_ev_w_ukv, m_ev_w_out, m_od_w_in, m_od_w_out, m_ln_mix_g, m_ln_mix_b, m_mlp_w1, m_mlp_w2, m_ln_mlp_g, m_ln_mlp_b, v_meta_tokens, v_ev_w_in, v_ev_conv_w, v_ev_conv_b, v_ev_w_rg_a, v_ev_b_rg_a, v_ev_w_rg_x, v_ev_b_rg_x, v_ev_lru_lambda, v_ev_q_norm_g, v_ev_w_uq, v_ev_kv_norm_g, v_ev_w_ukv, v_ev_w_out, v_od_w_in, v_od_w_out, v_ln_mix_g, v_ln_mix_b, v_mlp_w1, v_mlp_w2, v_ln_mlp_g, v_ln_mlp_b):
    given = dict(locals())
    local_big = {"ev_w_in": ev_w_in[0], "ev_w_uq": ev_w_uq[0], "ev_w_ukv": ev_w_ukv[0], "ev_w_out": ev_w_out[0],
                 "od_w_in": od_w_in[0], "od_w_out": od_w_out[0], "mlp_w1_0": mlp_w1[0], "mlp_w1_1": mlp_w1[1],
                 "mlp_w2_0": mlp_w2[0], "mlp_w2_1": mlp_w2[1]}

    specs = {spec[0]: spec for spec in BIG_SPECS}
    mixer0_m, mlp0_m, layer1_m = MATRIX_GROUPS

    def shards(names):
        return [local_big[n].astype(MXU_DTYPE) for n in names]

    def whole(stacked, n):
        _, K, N, ax = specs[n]
        return stacked if n in COL_SHARD_GRADS else _gather_shards(stacked, K, N, ax)

    def filled(gathered, own, names):
        return {n: whole(_with_own(g_, o_), n) for n, g_, o_ in zip(names, gathered, own)}

    own_a, own_b, own_c = shards(mixer0_m), shards(mlp0_m), shards(layer1_m)
    small = [meta_tokens, jnp.pad(ev_conv_w[0], ((0, 16 - CONV_WIDTH), (0, 0)))]
    gathered_a = _allgather_chips(own_a + small, "weight_allgather_mixer0")
    pending_b, token1 = _allgather_start(own_b, "weight_allgather_mlp0_start")
    pending_c, token2 = _allgather_start(own_c, "weight_allgather_layer1_start")
    meta_full = _gather_shards(_with_own(gathered_a[-2], small[0]), N_META, D_MODEL, 1)
    conv_full = _gather_shards(_with_own(gathered_a[-1], small[1])[:, :CONV_WIDTH], CONV_WIDTH, LRU_WIDTH, 1)

    def slots(names, dtype):
        return {n: jnp.zeros((N_CHIPS, specs[n][1], specs[n][2] // N_CHIPS) if n in COL_SHARD_GRADS else specs[n][1:3], dtype) for n in names}

    def norms(names, layer):
        return {n: given[n][layer] for n in names}

    def finish_gather(pending, own, after, names, tag):
        landed = _allgather_wait(pending, lax.stop_gradient(after), "weight_allgather_%s_wait" % tag)
        return filled(_sibling_forward(landed, "weight_allgather_%s_forward" % tag), own, names)

    diff_a = {**slots(mixer0_m, MXU_DTYPE), **norms(("ln_mix_g", "ln_mix_b"), 0), **{n: given[n] for n in REPLICATED if n not in LAYER_NORMS},
              "x": x, "meta_tokens": meta_full, "ev_conv_w": conv_full}
    diff_b = {**slots(mlp0_m, MXU_DTYPE), **norms(("ln_mlp_g", "ln_mlp_b"), 0)}
    diff_c = {**slots(layer1_m, MXU_DTYPE), **norms(LAYER_NORMS, 1)}
    w_a = filled(gathered_a[:len(mixer0_m)], own_a, mixer0_m)
    h_a, back_a = jax.vjp(lambda d: _mixer0(d, w_a, token1 + token2), diff_a)
    w_b = finish_gather(pending_b, own_b, h_a, mlp0_m, "mlp0")
    h_b, back_b = jax.vjp(lambda d, hh: _mlp0(d, hh, w_b), diff_b, h_a)
    w_c = finish_gather(pending_c, own_c, h_b, layer1_m, "layer1")
    loss, back_c = jax.vjp(lambda d, hh: _layer1_loss(d, hh, w_c, loss_target), diff_c, h_b)
    loss = lax.psum(loss, ("x", "y", "c"))

    def blocks_of(grad, n):
        _, K, N, ax = specs[n]
        if n in COL_SHARD_GRADS:
            blocks = grad
        elif ax == 0:
            blocks = grad.reshape(N_CHIPS, K // N_CHIPS, N)
        else:
            blocks = grad.reshape(K, N_CHIPS, N // N_CHIPS).transpose(1, 0, 2)
        return blocks.reshape(N_CHIPS, 2, blocks.shape[1] // 2, blocks.shape[2])

    def start_reduce(grads_of, names, tag):
        return _direct_scatter_start([blocks_of(grads_of[n], n) for n in names], "grad_scatter_%s_start" % tag)

    g_c, dh = back_c(jnp.ones((), F32))
    flying_c, token = start_reduce(g_c, layer1_m, "layer1")
    g_b, dh = back_b(dh + token)
    flying_b, token = start_reduce(g_b, mlp0_m, "mlp0")
    (g_a,) = back_a(dh + token)

    g = {**g_a, **g_b, **g_c}
    g.update({n: jnp.stack([(g_b if n in g_b else g_a)[n], g_c[n]]) for n in LAYER_NORMS})
    repl = jnp.concatenate([g[n].reshape(-1) for n in REPLICATED]).reshape(N_CHIPS, -1)
    small = [_split_shards(g["meta_tokens"], N_META, D_MODEL, 1), _split_shards(g["ev_conv_w"], CONV_WIDTH, LRU_WIDTH, 1), repl]
    small = [pc.reshape(N_CHIPS, 2, -1) for pc in small]
    n_small = sum(pc.shape[2] for pc in small)
    small.append(jnp.zeros((N_CHIPS, 2, _round_up(n_small, 32 * PACK_COLS) - n_small), F32))
    p_small = jnp.concatenate(small, axis=2).reshape(N_CHIPS, 2, -1, PACK_COLS)
    flying_a, token = _direct_scatter_start([blocks_of(g_a[n], n) for n in mixer0_m] + [p_small], "grad_scatter_mixer0_start")
    ps_c, ts_c = _direct_scatter_wait(flying_c, g_a["x"], "grad_scatter_layer1_wait")
    ps_b, ts_b = _direct_scatter_wait(flying_b, g_a["x"], "grad_scatter_mlp0_wait")
    fs_bc = [_sum_direct(p, t, "grad_sum_%d" % i) for i, (p, t) in enumerate(zip(ps_b + ps_c, ts_b + ts_c))]
    red_big = dict(zip(mlp0_m + layer1_m, _sibling_gather(fs_bc, "grad_sibling_gather")))

    grads, delta, new_m, new_v = {}, {}, {}, {}

    def update_big(names):
        for name, parts in BIG_PARAMS:
            if parts[0] in names:
                shp = given[name].shape
                two_d = (-1, shp[-1])
                grads[name] = jnp.stack([red_big[part].reshape(shp[1:]) for part in parts])
                d, nm, nv = _adamw(given[name].reshape(two_d), grads[name].reshape(two_d), given["m_" + name].reshape(two_d),
                                   given["v_" + name].reshape(two_d), "adamw_" + name)
                delta[name], new_m[name], new_v[name] = d.reshape(shp), nm.reshape(shp), nv.reshape(shp)
        return nv

    updated = update_big(mlp0_m + layer1_m)
    ps_a, ts_a = _direct_scatter_wait(flying_a, updated, "grad_scatter_mixer0_wait")
    fs_a = [_sum_direct(p, t, "grad_sum_mixer0_%d" % i) for i, (p, t) in enumerate(zip(ps_a, ts_a))]
    reduced_a = _sibling_gather(fs_a, "grad_sibling_gather_mixer0")
    red_big.update(zip(mixer0_m, reduced_a))
    red_small = reduced_a[-1].reshape(2, -1)
    update_big(mixer0_m)

    def take(off, sz):
        return jnp.concatenate([red_small[0, off // 2:(off + sz) // 2], red_small[1, off // 2:(off + sz) // 2]])

    off = 0
    for name in SMALL_SHARDED:
        sz = given[name].size
        grads[name] = take(off, sz).reshape(given[name].shape)
        off += sz
    n_repl = repl.shape[1]
    own_repl = _to_rows(take(off, n_repl), 16)
    repl_all = _with_own(_allgather_chips([own_repl], "replicated_allgather")[0], own_repl).reshape(N_CHIPS, -1)[:, :n_repl].reshape(-1)
    off = 0
    for name in REPLICATED:
        sz = given[name].size
        grads[name] = repl_all[off:off + sz].reshape(given[name].shape)
        off += sz

    smalls = SMALL_SHARDED + REPLICATED

    def pack_small(get):
        return _to_rows(jnp.concatenate([get(n).reshape(-1) for n in smalls]), 8)

    outs = _adamw(pack_small(lambda n: given[n]), pack_small(lambda n: grads[n]), pack_small(lambda n: given["m_" + n]),
                  pack_small(lambda n: given["v_" + n]), "adamw_small")
    for res, flat in zip((delta, new_m, new_v), outs):
        flat, off = flat.reshape(-1), 0
        for n in smalls:
            sz = given[n].size
            res[n] = flat[off:off + sz].reshape(given[n].shape)
            off += sz

    return (loss, g_a["x"], *[grads[n] for n in WEIGHT_NAMES], *[delta[n] for n in WEIGHT_NAMES],
            *[new_m[n] for n in WEIGHT_NAMES], *[new_v[n] for n in WEIGHT_NAMES])
```

```python
import functools
import math

import jax
import jax.numpy as jnp
from jax import lax
from jax.experimental import pallas as pl
from jax.experimental.pallas import tpu as pltpu

F32 = jnp.float32
MXU_DTYPE = jnp.bfloat16

D_MODEL = 1024
N_META = 16
LRU_WIDTH = 512
LRU_HEADS = 4
LRU_HEAD_DIM = 128
CONV_WIDTH = 4
LRU_C = 8.0
MLA_HEADS = 8
MLA_NOPE = 64
MLA_ROPE = 32
MLA_V = 64
MLA_Q_RANK = 256
MLA_KV_RANK = 128
RET_HEADS = 4
RET_QK_DIM = 256
RET_V_DIM = 512
D_FF = 4096
ROPE_BASE = 10000.0
DN_ALPHA = 4.0 ** 0.25
EPS = 1e-5
NEG_INF = -1e30
SEQ_BLOCK = 128

ADAM_LR = 0.001
ADAM_B1 = 0.9
ADAM_B2 = 0.999
ADAM_EPS = 1e-08
ADAM_WD = 0.01
ADAM_STEP = 10

PACK_COLS = 1024
TN_INPUT_VMEM_BYTES = 28 << 20
N_CHIPS = 4

MESH = pl.DeviceIdType.MESH


def _pick(n, target, align):
    best = None
    for t in range(align, min(n, target) + 1, align):
        if n % t == 0:
            best = t
    return n if best is None else best


def _round_up(n, m):
    return (n + m - 1) // m * m


def _relu2(a):
    r = jnp.maximum(a, 0.0)
    return r * r


def _ln_stats(z):
    mu = jnp.mean(z, axis=-1, keepdims=True)
    zc = z - mu
    var = jnp.mean(zc * zc, axis=-1, keepdims=True)
    return zc, lax.rsqrt(var + EPS)


def _mm_nn(a, w, act, name, out_dtype=F32, norm=None):
    M, K = a.shape
    sharded = w.ndim == 3
    n = w.shape[-1]
    N = n * (w.shape[0] if sharded else 1)
    tm = _pick(M, 1088 if K * a.dtype.itemsize <= 4096 and norm is None else 544, 8)
    tn = _pick(n, 1024, 128)
    per = n // tn
    assert norm is None or tn == N

    def body(a_ref, w_ref, *rest):
        av = a_ref[...]
        if act:
            av = _relu2(av.astype(F32))
        r = jnp.dot(av.astype(MXU_DTYPE), w_ref[...].astype(MXU_DTYPE), preferred_element_type=F32)
        if norm is None:
            rest[0][...] = r.astype(out_dtype)
        else:
            r_ref, g_ref, b_ref, o_ref, z_ref = rest
            z = DN_ALPHA * r_ref[...] + r
            zc, rstd = _ln_stats(z)
            z_ref[...] = z
            o_ref[...] = zc * rstd * g_ref[...] + b_ref[...]

    w_spec = pl.BlockSpec((None, K, tn), lambda i, j: (j // per, 0, j % per)) if sharded else pl.BlockSpec((K, tn), lambda i, j: (0, j))
    tile = pl.BlockSpec((tm, tn), lambda i, j: (i, j))
    in_specs, args = [pl.BlockSpec((tm, K), lambda i, j: (i, 0)), w_spec], [a, w]
    if norm is None:
        out_specs, out_shape = tile, jax.ShapeDtypeStruct((M, N), out_dtype)
    else:
        vec = pl.BlockSpec((1, N), lambda i, j: (0, 0))
        in_specs += [tile, vec, vec]
        args += [norm[0], norm[1].reshape(1, N), norm[2].reshape(1, N)]
        out_specs, out_shape = [tile, tile], [jax.ShapeDtypeStruct((M, N), F32)] * 2
    return pl.pallas_call(
        body, name=name, grid=(M // tm, N // tn), in_specs=in_specs, out_specs=out_specs, out_shape=out_shape,
        compiler_params=pltpu.CompilerParams(dimension_semantics=("parallel", "arbitrary")),
    )(*args)


def _mm_nt(g, w, a_src, name, out_dtype=F32, plus=None):
    M, N = g.shape
    sharded = w.ndim == 3
    K, n = w.shape[-2], w.shape[-1]
    if sharded:
        tk, nk = N, 1
    else:
        tk = N if N * g.dtype.itemsize <= 8192 else _pick(N, 2048, 128)
        nk = N // tk
    tm = _pick(M, 1088 if tk * g.dtype.itemsize <= 4096 else 544, 8)
    tn = _pick(K, 1024, 128)
    has_src = a_src is not None
    assert nk == 1 or out_dtype == F32
    assert plus is None or not has_src

    def body(*refs):
        if has_src:
            g_ref, w_ref, s_ref, o_ref = refs
        elif plus is not None:
            g_ref, w_ref, p_ref, o_ref = refs
        else:
            g_ref, w_ref, o_ref = refs
        nt = (((1,), (1,)), ((), ()))
        if sharded:
            r = sum(lax.dot_general(g_ref[:, s * n:(s + 1) * n].astype(MXU_DTYPE), w_ref[s].astype(MXU_DTYPE), nt, preferred_element_type=F32)
                    for s in range(w_ref.shape[0]))
        else:
            r = lax.dot_general(g_ref[...].astype(MXU_DTYPE), w_ref[...].astype(MXU_DTYPE), nt, preferred_element_type=F32)
        if has_src:
            r = r * (2.0 * jnp.maximum(s_ref[...].astype(F32), 0.0))
        first = r if plus is None else r + DN_ALPHA * p_ref[...]
        if nk == 1:
            o_ref[...] = first.astype(out_dtype)
        else:
            k = pl.program_id(2)

            @pl.when(k == 0)
            def _():
                o_ref[...] = first

            @pl.when(k > 0)
            def _():
                o_ref[...] += r

    w_spec = (pl.BlockSpec((w.shape[0], tn, n), lambda i, j, k: (0, j, 0)) if sharded
              else pl.BlockSpec((tn, tk), lambda i, j, k: (j, k)))
    in_specs = [pl.BlockSpec((tm, tk), lambda i, j, k: (i, k)), w_spec]
    args = [g, w]
    if has_src:
        assert nk == 1
        in_specs.append(pl.BlockSpec((tm, tn), lambda i, j, k: (i, j)))
        args.append(a_src)
    if plus is not None:
        in_specs.append(pl.BlockSpec((tm, tn), lambda i, j, k: (i, j)))
        args.append(plus)
    return pl.pallas_call(
        body, name=name,
        grid=(M // tm, K // tn, nk),
        in_specs=in_specs,
        out_specs=pl.BlockSpec((tm, tn), lambda i, j, k: (i, j)),
        out_shape=jax.ShapeDtypeStruct((M, K), out_dtype),
        compiler_params=pltpu.CompilerParams(dimension_semantics=("parallel", "parallel", "arbitrary")),
    )(*args)


def _mm_tn(a, g, act, name, col_shards=1, out_dtype=F32):
    M, K = a.shape
    _, N = g.shape
    n = N // col_shards
    tm, tn = _pick(K, 1024, 128), _pick(n, 1024, 128)
    row_bytes = tm * a.dtype.itemsize + tn * g.dtype.itemsize
    tk = _pick(M, min(2176, TN_INPUT_VMEM_BYTES // (2 * row_bytes)), 8)
    nk = M // tk
    per = n // tn
    direct = out_dtype == F32

    def body(a_ref, g_ref, o_ref, *scratch):
        acc_ref = o_ref if direct else scratch[0]
        k = pl.program_id(2)
        av = a_ref[...]
        if act:
            av = _relu2(av.astype(F32))
        r = lax.dot_general(av.astype(MXU_DTYPE), g_ref[...].astype(MXU_DTYPE),
                            (((0,), (0,)), ((), ())), preferred_element_type=F32)

        @pl.when(k == 0)
        def _():
            acc_ref[...] = r

        @pl.when(k > 0)
        def _():
            acc_ref[...] += r

        if not direct:
            @pl.when(k == nk - 1)
            def _():
                o_ref[...] = acc_ref[...].astype(out_dtype)

    if col_shards == 1:
        out_spec, out_shape = pl.BlockSpec((tm, tn), lambda i, j, k: (i, j)), (K, N)
    else:
        out_spec, out_shape = pl.BlockSpec((None, tm, tn), lambda i, j, k: (j // per, i, j % per)), (col_shards, K, n)
    return pl.pallas_call(
        body, name=name,
        grid=(K // tm, N // tn, nk),
        in_specs=[pl.BlockSpec((tk, tm), lambda i, j, k: (k, i)), pl.BlockSpec((tk, tn), lambda i, j, k: (k, j))],
        out_specs=out_spec,
        out_shape=jax.ShapeDtypeStruct(out_shape, out_dtype),
        scratch_shapes=[] if direct else [pltpu.VMEM((tm, tn), F32)],
        compiler_params=pltpu.CompilerParams(dimension_semantics=("parallel", "parallel", "arbitrary")),
    )(a, g)


@functools.partial(jax.custom_vjp, nondiff_argnums=(3, 4, 5, 6))
def matmul(a, w, w_grad_slot, act, name, out_dtype, col_shards):
    return _mm_nn(a, w, act, name + "_fwd", out_dtype)


def _matmul_fwd(a, w, w_grad_slot, act, name, out_dtype, col_shards):
    return _mm_nn(a, w, act, name + "_fwd", out_dtype), (a, w, jnp.zeros((), w_grad_slot.dtype))


def _matmul_bwd(act, name, out_dtype, col_shards, res, g):
    a, w, slot_like = res
    w_grad_dtype = slot_like.dtype
    da = _mm_nt(g, w, a if act else None, name + "_dx")
    dw = _mm_tn(a, g, act, name + "_dw", col_shards, w_grad_dtype)
    return da, None, dw


matmul.defvjp(_matmul_fwd, _matmul_bwd)


def _ln_bwd_call(z, g, dy, name):
    M, D = z.shape
    tm = _pick(M, 544, 8)

    def body(z_ref, g_ref, dy_ref, dz_ref, dg_ref, db_ref):
        @pl.when(pl.program_id(0) == 0)
        def _():
            dg_ref[...] = jnp.zeros_like(dg_ref)
            db_ref[...] = jnp.zeros_like(db_ref)

        zc, rstd = _ln_stats(z_ref[...])
        xhat = zc * rstd
        dy = dy_ref[...]
        dxh = dy * g_ref[...]
        m1 = jnp.mean(dxh, axis=-1, keepdims=True)
        m2 = jnp.mean(dxh * xhat, axis=-1, keepdims=True)
        dz_ref[...] = rstd * (dxh - m1 - xhat * m2)
        dg_ref[...] += jnp.sum(dy * xhat, axis=0, keepdims=True)
        db_ref[...] += jnp.sum(dy, axis=0, keepdims=True)

    row = pl.BlockSpec((tm, D), lambda i: (i, 0))
    vec = pl.BlockSpec((1, D), lambda i: (0, 0))
    return pl.pallas_call(
        body, name=name, grid=(M // tm,), in_specs=[row, vec, row], out_specs=[row, vec, vec],
        out_shape=[jax.ShapeDtypeStruct((M, D), F32), jax.ShapeDtypeStruct((1, D), F32), jax.ShapeDtypeStruct((1, D), F32)],
        compiler_params=pltpu.CompilerParams(dimension_semantics=("arbitrary",)),
    )(z, g.reshape(1, D), dy)


@functools.partial(jax.custom_vjp, nondiff_argnums=(7,))
def mlp_block(h, w1, w2, w1_grad_slot, w2_grad_slot, g, b, name):
    return _mlp_block_fwd(h, w1, w2, w1_grad_slot, w2_grad_slot, g, b, name)[0]


def _mlp_block_fwd(h, w1, w2, w1_grad_slot, w2_grad_slot, g, b, name):
    u = _mm_nn(h, w1, False, name + "_w1_fwd", out_dtype=MXU_DTYPE)
    out, z = _mm_nn(u, w2, True, name + "_w2_norm_fwd", norm=(h, g, b))
    return out, (h, u, z, w1, w2, g, jnp.zeros((), w1_grad_slot.dtype))


def _mlp_block_bwd(name, res, dy):
    h, u, z, w1, w2, g, slot_like = res
    dz, dg, db = _ln_bwd_call(z, g, dy, name + "_norm_bwd")
    du = _mm_nt(dz, w2, u, name + "_w2_dx", out_dtype=MXU_DTYPE)
    dw2 = _mm_tn(u, dz, True, name + "_w2_dw", 1, slot_like.dtype)
    dh = _mm_nt(du, w1, None, name + "_w1_dx", plus=dz)
    dw1 = _mm_tn(h, du, False, name + "_w1_dw", N_CHIPS, slot_like.dtype)
    return dh, None, None, dw1, dw2, dg.reshape(g.shape), db.reshape(g.shape)


mlp_block.defvjp(_mlp_block_fwd, _mlp_block_bwd)


@functools.partial(jax.custom_vjp, nondiff_argnums=(6,))
def out_block(h, y, w, w_grad_slot, g, b, name):
    return _out_block_fwd(h, y, w, w_grad_slot, g, b, name)[0]


def _out_block_fwd(h, y, w, w_grad_slot, g, b, name):
    out, z = _mm_nn(y, w, False, name + "_norm_fwd", norm=(h, g, b))
    return out, (y, z, w, g, jnp.zeros((), w_grad_slot.dtype))


def _out_block_bwd(name, res, dy):
    y, z, w, g, slot_like = res
    dz, dg, db = _ln_bwd_call(z, g, dy, name + "_norm_bwd")
    d_y = _mm_nt(dz, w, None, name + "_dx")
    dw = _mm_tn(y, dz, False, name + "_dw", 1, slot_like.dtype)
    return DN_ALPHA * dz, d_y, None, dw, dg.reshape(g.shape), db.reshape(g.shape)


out_block.defvjp(_out_block_fwd, _out_block_bwd)


def _rms_fwd_call(x, g, name, col_block=0):
    R = x.shape[0]
    W = g.shape[-1]
    tr = _pick(R, 1088, 8)

    def body(x_ref, g_ref, o_ref):
        xv = x_ref[...]
        rstd = lax.rsqrt(jnp.mean(xv * xv, axis=-1, keepdims=True) + EPS)
        o_ref[...] = xv * rstd * g_ref[...]

    vec = pl.BlockSpec((1, W), lambda i: (0, 0))
    return pl.pallas_call(
        body, name=name, grid=(R // tr,), in_specs=[pl.BlockSpec((tr, W), lambda i: (i, col_block)), vec],
        out_specs=pl.BlockSpec((tr, W), lambda i: (i, 0)), out_shape=jax.ShapeDtypeStruct((R, W), F32),
        compiler_params=pltpu.CompilerParams(dimension_semantics=("parallel",)),
    )(x, g.reshape(1, W))


def _rms_bwd_call(x, g, dy, name, col_block=0):
    R = x.shape[0]
    W = g.shape[-1]
    tr = _pick(R, 1088, 8)

    def body(x_ref, g_ref, dy_ref, dx_ref, dg_ref):
        @pl.when(pl.program_id(0) == 0)
        def _():
            dg_ref[...] = jnp.zeros_like(dg_ref)

        xv = x_ref[...]
        rstd = lax.rsqrt(jnp.mean(xv * xv, axis=-1, keepdims=True) + EPS)
        xhat = xv * rstd
        dy = dy_ref[...]
        dxh = dy * g_ref[...]
        dx_ref[...] = rstd * (dxh - xhat * jnp.mean(dxh * xhat, axis=-1, keepdims=True))
        dg_ref[...] += jnp.sum(dy * xhat, axis=0, keepdims=True)

    row = pl.BlockSpec((tr, W), lambda i: (i, 0))
    vec = pl.BlockSpec((1, W), lambda i: (0, 0))
    return pl.pallas_call(
        body, name=name, grid=(R // tr,), in_specs=[pl.BlockSpec((tr, W), lambda i: (i, col_block)), vec, row], out_specs=[row, vec],
        out_shape=[jax.ShapeDtypeStruct((R, W), F32), jax.ShapeDtypeStruct((1, W), F32)],
        compiler_params=pltpu.CompilerParams(dimension_semantics=("arbitrary",)),
    )(x, g.reshape(1, W), dy)


def _loss_call(h, tgt, n_tokens, name):
    B, Tp, D = h.shape
    tr = _pick(Tp, 544, 8)

    def body(y_ref, t_ref, dy_ref, acc_ref):
        @pl.when(jnp.logical_and(pl.program_id(0) == 0, pl.program_id(1) == 0))
        def _():
            acc_ref[...] = jnp.zeros_like(acc_ref)

        t = lax.broadcasted_iota(jnp.int32, (tr, 1), 0) + pl.program_id(1) * tr
        counts = jnp.logical_and(t >= N_META, t < N_META + n_tokens)
        e = jnp.where(counts, y_ref[...] - t_ref[...], 0.0)
        dy_ref[...] = e * (1.0 / D)
        acc_ref[...] += jnp.sum(jnp.sum(e * e, axis=-1, keepdims=True), axis=0, keepdims=True) * (0.5 / D)

    row = pl.BlockSpec((None, tr, D), lambda b, i: (b, i, 0))
    one = pl.BlockSpec((1, 1), lambda b, i: (0, 0))
    return pl.pallas_call(
        body, name=name, grid=(B, Tp // tr), in_specs=[row, row], out_specs=[row, one],
        out_shape=[jax.ShapeDtypeStruct((B, Tp, D), F32), jax.ShapeDtypeStruct((1, 1), F32)],
        compiler_params=pltpu.CompilerParams(dimension_semantics=("arbitrary", "arbitrary")),
    )(h, tgt)


@functools.partial(jax.custom_vjp, nondiff_argnums=(2,))
def loss_head(h, tgt, n_tokens):
    return _loss_call(h, tgt, n_tokens, "loss_head")[1][0, 0]


def _loss_head_fwd(h, tgt, n_tokens):
    dy, acc = _loss_call(h, tgt, n_tokens, "loss_head")
    return acc[0, 0], dy


def _loss_head_bwd(n_tokens, dy, ct):
    return ct * dy, None


loss_head.defvjp(_loss_head_fwd, _loss_head_bwd)


_GELU_C = math.sqrt(2.0 / math.pi)


def _gelu_parts(x):
    x2 = x * x
    t = jnp.tanh(_GELU_C * (x + 0.044715 * x * x2))
    gelu = 0.5 * x * (1.0 + t)
    dgelu = 0.5 * (1.0 + t) + 0.5 * x * (1.0 - t * t) * (_GELU_C * (1.0 + 3.0 * 0.044715 * x2))
    return gelu, dgelu


def _sigmoid(x):
    return 1.0 / (1.0 + jnp.exp(-x))


def _scan8(a, b, carry, reverse):
    row = lax.broadcasted_iota(jnp.int32, a.shape, 0)
    for s in (1, 2, 4):
        shift = 8 - s if reverse else s
        keep = (row < 8 - s) if reverse else (row >= s)
        b = jnp.where(keep, a * pltpu.roll(b, shift, 0) + b, b)
        a = jnp.where(keep, a * pltpu.roll(a, shift, 0), a)
    return a * carry + b


def _lru_pre(prec_ref, prev_ref, first, cw_ref, cb_ref, wa_ref, ba_ref, wx_ref, bx_ref, sp_ref):
    tc = prec_ref.shape[0]
    prev = jnp.where(first, 0.0, prev_ref[...])
    ext = jnp.concatenate([prev, prec_ref[...]], axis=0)
    cw = cw_ref[...]
    taps = [ext[8:] if k == CONV_WIDTH - 1 else pltpu.roll(ext, CONV_WIDTH - 1 - k, 0)[8:] for k in range(CONV_WIDTH)]
    xc = cb_ref[...] + sum(cw[k:k + 1, :] * taps[k] for k in range(CONV_WIDTH))
    ga, gx = [], []
    for h in range(LRU_HEADS):
        xh = xc[:, h * LRU_HEAD_DIM:(h + 1) * LRU_HEAD_DIM].astype(MXU_DTYPE)
        ga.append(jnp.dot(xh, wa_ref[h].astype(MXU_DTYPE), preferred_element_type=F32))
        gx.append(jnp.dot(xh, wx_ref[h].astype(MXU_DTYPE), preferred_element_type=F32))
    r = _sigmoid(jnp.concatenate(ga, axis=1) + ba_ref[...])
    i = _sigmoid(jnp.concatenate(gx, axis=1) + bx_ref[...])
    log_a = -LRU_C * r * sp_ref[...]
    a = jnp.exp(log_a)
    a2 = a * a
    mult = jnp.sqrt(-jnp.tanh(log_a) * (a2 + 1.0))
    return taps, xc, r, i, a, a2, mult


def _lru_fwd_call(p, cw, cb, wa, ba, wx, bx, sp):
    B, Tp, _ = p.shape
    W = LRU_WIDTH
    tc = SEQ_BLOCK
    nc = Tp // tc

    def body(pg_ref, prec_ref, prev_ref, cw_ref, cb_ref, wa_ref, ba_ref, wx_ref, bx_ref, sp_ref, y_ref, h_ref, carry_ref):
        first = pl.program_id(1) == 0

        @pl.when(first)
        def _():
            carry_ref[...] = jnp.zeros_like(carry_ref)

        _, xc, r, i, a, a2, mult = _lru_pre(prec_ref, prev_ref, first, cw_ref, cb_ref, wa_ref, ba_ref, wx_ref, bx_ref, sp_ref)
        b = mult * (i * xc)
        carry = carry_ref[0:1, :]
        for t in range(tc // 8):
            h = _scan8(a[8 * t:8 * t + 8], b[8 * t:8 * t + 8], carry, False)
            h_ref[8 * t:8 * t + 8, :] = h
            carry = h[7:8, :]
        carry_ref[...] = jnp.broadcast_to(carry, carry_ref.shape)
        y_ref[...] = h_ref[...] * _gelu_parts(pg_ref[...])[0]

    cur = pl.BlockSpec((None, tc, W), lambda b, j: (b, j, 0))
    rec = pl.BlockSpec((None, tc, W), lambda b, j: (b, j, 1))
    prev = pl.BlockSpec((None, 8, W), lambda b, j: (b, jnp.maximum(j * (tc // 8) - 1, 0), 1))
    vec = pl.BlockSpec((1, W), lambda b, j: (0, 0))
    cws = pl.BlockSpec((CONV_WIDTH, W), lambda b, j: (0, 0))
    wsp = pl.BlockSpec((LRU_HEADS, LRU_HEAD_DIM, LRU_HEAD_DIM), lambda b, j: (0, 0, 0))
    return pl.pallas_call(
        body, name="lru_fwd", grid=(B, nc),
        in_specs=[cur, rec, prev, cws, vec, wsp, vec, wsp, vec, vec],
        out_specs=[cur, cur],
        out_shape=[jax.ShapeDtypeStruct((B, Tp, W), F32), jax.ShapeDtypeStruct((B, Tp, W), F32)],
        scratch_shapes=[pltpu.VMEM((8, W), F32)],
        compiler_params=pltpu.CompilerParams(dimension_semantics=("arbitrary", "arbitrary")),
    )(p, p, p, cw, cb, wa, ba, wx, bx, sp)


def _lru_bwd_call(p, hseq, dy, cw, cb, wa, ba, wx, bx, sp, dpq, dpkv, dkpe):
    B, Tp, P = p.shape
    W = LRU_WIDTH
    tc = SEQ_BLOCK
    nc = Tp // tc
    HD = LRU_HEAD_DIM

    def body(pg_ref, prec_ref, prev_ref, h_ref, hprev_ref, dy_ref, cw_ref, cb_ref, wa_ref, ba_ref, wx_ref, bx_ref, sp_ref,
             dpq_ref, dpkv_ref, dkpe_ref, dp_ref, dcw_ref, dcb_ref, dwa_ref, dba_ref, dwx_ref, dbx_ref, dsp_ref,
             gcar_ref, anext_ref, halo_ref, g_ref):
        j = pl.program_id(1)
        first = j == nc - 1
        last = j == 0

        @pl.when(jnp.logical_and(pl.program_id(0) == 0, last))
        def _():
            for ref in (dcw_ref, dcb_ref, dwa_ref, dba_ref, dwx_ref, dbx_ref, dsp_ref):
                ref[...] = jnp.zeros_like(ref)

        @pl.when(last)
        def _():
            gcar_ref[...] = jnp.zeros_like(gcar_ref)
            anext_ref[...] = jnp.zeros_like(anext_ref)
            halo_ref[...] = jnp.zeros_like(halo_ref)

        taps, xc, r, i, a, a2, mult = _lru_pre(prec_ref, prev_ref, first, cw_ref, cb_ref, wa_ref, ba_ref, wx_ref, bx_ref, sp_ref)
        row = lax.broadcasted_iota(jnp.int32, (tc, W), 0)
        gelu, dgelu = _gelu_parts(pg_ref[...])
        dy = dy_ref[...]
        hcur = h_ref[...]
        dp_ref[:, 0:W] = dy * hcur * dgelu
        dp_ref[:, 2 * W:2 * W + MLA_Q_RANK] = dpq_ref[...]
        dp_ref[:, _KPE_START - MLA_KV_RANK:_KPE_START] = dpkv_ref[...]
        dp_ref[:, _KPE_START:P] = pltpu.roll(dkpe_ref[...], HEAD_LANES - MLA_NOPE, 1)[:, 0:P - _KPE_START]
        dh = dy * gelu
        a_next = jnp.where(row == tc - 1, anext_ref[0:1, :], pltpu.roll(a, tc - 1, 0))
        carry = gcar_ref[0:1, :]
        for t in reversed(range(tc // 8)):
            g = _scan8(a_next[8 * t:8 * t + 8], dh[8 * t:8 * t + 8], carry, True)
            g_ref[8 * t:8 * t + 8, :] = g
            carry = g[0:1, :]
        gcar_ref[...] = jnp.broadcast_to(carry, gcar_ref.shape)
        anext_ref[...] = jnp.broadcast_to(a[0:1, :], anext_ref.shape)
        G = g_ref[...]
        h_before = jnp.where(first, 0.0, hprev_ref[7:8, :])
        hprev = jnp.where(row == 0, h_before, pltpu.roll(hcur, 1, 0))
        d_a = G * hprev
        gx_ = G * xc
        d_mult = gx_ * i
        d_i = gx_ * mult
        dxc = G * (mult * i)
        d_la = d_a * a - d_mult * (a2 / mult)
        sp = sp_ref[...]
        d_r = d_la * (-LRU_C * sp)
        dsp_ref[...] += jnp.sum(d_la * (-LRU_C * r), axis=0, keepdims=True)
        dga = d_r * r * (1.0 - r)
        dgx = d_i * i * (1.0 - i)
        dba_ref[...] += jnp.sum(dga, axis=0, keepdims=True)
        dbx_ref[...] += jnp.sum(dgx, axis=0, keepdims=True)
        back = []
        for h in range(LRU_HEADS):
            sl = slice(h * HD, (h + 1) * HD)
            xh = xc[:, sl].astype(MXU_DTYPE)
            ah = dga[:, sl].astype(MXU_DTYPE)
            bh = dgx[:, sl].astype(MXU_DTYPE)
            tn = (((0,), (0,)), ((), ()))
            nt = (((1,), (1,)), ((), ()))
            dwa_ref[h] += lax.dot_general(xh, ah, tn, preferred_element_type=F32)
            dwx_ref[h] += lax.dot_general(xh, bh, tn, preferred_element_type=F32)
            back.append(lax.dot_general(ah, wa_ref[h].astype(MXU_DTYPE), nt, preferred_element_type=F32)
                        + lax.dot_general(bh, wx_ref[h].astype(MXU_DTYPE), nt, preferred_element_type=F32))
        dxc = dxc + jnp.concatenate(back, axis=1)
        dcb_ref[...] += jnp.sum(dxc, axis=0, keepdims=True)
        for k in range(CONV_WIDTH):
            dcw_ref[k:k + 1, :] += jnp.sum(dxc * taps[k], axis=0, keepdims=True)
        ext = jnp.concatenate([dxc, halo_ref[...]], axis=0)
        cw = cw_ref[...]
        acc = cw[CONV_WIDTH - 1:CONV_WIDTH, :] * dxc
        for k in range(CONV_WIDTH - 1):
            s = CONV_WIDTH - 1 - k
            acc = acc + cw[k:k + 1, :] * pltpu.roll(ext, tc + 8 - s, 0)[:tc]
        dp_ref[:, W:2 * W] = acc
        halo_ref[...] = dxc[0:8, :]

    rev = lambda j: nc - 1 - j
    cur = pl.BlockSpec((None, tc, W), lambda b, j: (b, rev(j), 0))
    rec = pl.BlockSpec((None, tc, W), lambda b, j: (b, rev(j), 1))
    prev = pl.BlockSpec((None, 8, W), lambda b, j: (b, jnp.maximum(rev(j) * (tc // 8) - 1, 0), 0))
    prev_rec = pl.BlockSpec((None, 8, W), lambda b, j: (b, jnp.maximum(rev(j) * (tc // 8) - 1, 0), 1))
    vec = pl.BlockSpec((1, W), lambda b, j: (0, 0))
    cws = pl.BlockSpec((CONV_WIDTH, W), lambda b, j: (0, 0))
    wsp = pl.BlockSpec((LRU_HEADS, HD, HD), lambda b, j: (0, 0, 0))
    vs = jax.ShapeDtypeStruct((1, W), F32)
    ws = jax.ShapeDtypeStruct((LRU_HEADS, HD, HD), F32)

    def rows(width):
        return pl.BlockSpec((None, tc, width), lambda b, j: (b, rev(j), 0))

    return pl.pallas_call(
        body, name="lru_bwd", grid=(B, nc),
        in_specs=[cur, rec, prev_rec, cur, prev, cur, cws, vec, wsp, vec, wsp, vec, vec, rows(MLA_Q_RANK), rows(MLA_KV_RANK), rows(HEAD_LANES)],
        out_specs=[rows(P), cws, vec, wsp, vec, wsp, vec, vec],
        out_shape=[jax.ShapeDtypeStruct((B, Tp, P), F32), jax.ShapeDtypeStruct((CONV_WIDTH, W), F32), vs, ws, vs, ws, vs, vs],
        scratch_shapes=[pltpu.VMEM((8, W), F32), pltpu.VMEM((8, W), F32), pltpu.VMEM((8, W), F32), pltpu.VMEM((tc, W), F32)],
        compiler_params=pltpu.CompilerParams(dimension_semantics=("arbitrary", "arbitrary")),
    )(p, p, p, hseq, hseq, dy, cw, cb, wa, ba, wx, bx, sp, dpq, dpkv, dkpe)


_Q_BLOCK = 2 * LRU_WIDTH // MLA_Q_RANK
_KV_BLOCK = (2 * LRU_WIDTH + MLA_Q_RANK) // MLA_KV_RANK
_KPE_START = 2 * LRU_WIDTH + MLA_Q_RANK + MLA_KV_RANK


@jax.custom_vjp
def even_front(p, cw, cb, wa, ba, wx, bx, sp, gq, gkv):
    return _even_front_fwd(p, cw, cb, wa, ba, wx, bx, sp, gq, gkv)[0]


def _even_front_fwd(p, cw, cb, wa, ba, wx, bx, sp, gq, gkv):
    B, Tp, W = p.shape
    p2d = p.reshape(B * Tp, W)
    y, hseq = _lru_fwd_call(p, cw, cb, wa, ba, wx, bx, sp)
    qn = _rms_fwd_call(p2d, gq, "q_norm_fwd", _Q_BLOCK)
    kvn = _rms_fwd_call(p2d, gkv, "kv_norm_fwd", _KV_BLOCK)
    kpe = jnp.pad(p[:, :, _KPE_START:], ((0, 0), (0, 0), (MLA_NOPE, HEAD_LANES - MLA_NOPE - MLA_ROPE)))
    return (y, qn, kvn, kpe), (p, hseq, cw, cb, wa, ba, wx, bx, sp, gq, gkv)


def _even_front_bwd(res, cts):
    p, hseq, cw, cb, wa, ba, wx, bx, sp, gq, gkv = res
    dy, dqn, dkvn, dkpe = cts
    B, Tp, W = p.shape
    p2d = p.reshape(B * Tp, W)
    dpq, dgq = _rms_bwd_call(p2d, gq, dqn, "q_norm_bwd", _Q_BLOCK)
    dpkv, dgkv = _rms_bwd_call(p2d, gkv, dkvn, "kv_norm_bwd", _KV_BLOCK)
    dp, dcw, dcb, dwa, dba, dwx, dbx, dsp = _lru_bwd_call(p, hseq, dy, cw, cb, wa, ba, wx, bx, sp, dpq.reshape(B, Tp, -1),
                                                          dpkv.reshape(B, Tp, -1), dkpe)
    return dp, dcw, dcb, dwa, dba, dwx, dbx, dsp, dgq.reshape(gq.shape), dgkv.reshape(gkv.shape)


even_front.defvjp(_even_front_fwd, _even_front_bwd)


def _rope_tables(pos, half):
    inv = ROPE_BASE ** (-jnp.arange(half, dtype=F32) / half)
    ang = pos.astype(F32)[:, None] * inv[None, :]
    return jnp.cos(ang), jnp.sin(ang)


_NT = (((1,), (1,)), ((), ()))
_TN = (((0,), (0,)), ((), ()))
HEAD_LANES = 128
_MLA_SCALE = (MLA_NOPE + MLA_ROPE) ** -0.5
_LOG2E = math.log2(math.e)


Q_BLOCK = 512


def _query_blocks(Tp):
    first = Tp % Q_BLOCK or Q_BLOCK
    return [(0, first)] + [(r, r + Q_BLOCK) for r in range(first, Tp, Q_BLOCK)]


def _mask_diagonal(s, fill):
    R, L = s.shape
    row = lax.broadcasted_iota(jnp.int32, (R, R), 0)
    col = lax.broadcasted_iota(jnp.int32, (R, R), 1)
    last = jnp.where(col <= row, s[:, L - R:], fill)
    return last if L == R else jnp.concatenate([s[:, :L - R], last], axis=1)


def _mla_rope_tables(pos):
    half = MLA_ROPE // 2
    cos, sin = _rope_tables(pos, half)
    T = pos.shape[0]
    ones, zeros = jnp.ones((T, MLA_NOPE), F32), jnp.zeros((T, MLA_NOPE), F32)
    tail1, tail0 = jnp.ones((T, HEAD_LANES - MLA_NOPE - MLA_ROPE), F32), jnp.zeros((T, HEAD_LANES - MLA_NOPE - MLA_ROPE), F32)
    zh = jnp.zeros((T, half), F32)
    c = jnp.concatenate([ones, cos, cos, tail1], axis=1)
    s_up = jnp.concatenate([zeros, -sin, zh, tail0], axis=1)
    s_down = jnp.concatenate([zeros, zh, sin, tail0], axis=1)
    return c, s_up, s_down


def _rope_lanes(x, c, s_up, s_down):
    half = MLA_ROPE // 2
    return x * c + pltpu.roll(x, HEAD_LANES - half, 1) * s_up + pltpu.roll(x, half, 1) * s_down


def _unrope_lanes(d, c, s_up, s_down):
    half = MLA_ROPE // 2
    return d * c + pltpu.roll(d * s_up, half, 1) + pltpu.roll(d * s_down, HEAD_LANES - half, 1)


def _mla_operands(q_ref, kv_ref, kpe_ref, c, s_up, s_down):
    lane = lax.broadcasted_iota(jnp.int32, kv_ref.shape, 1)
    qr = (_rope_lanes(q_ref[...].astype(F32), c, s_up, s_down) * (_MLA_SCALE * _LOG2E)).astype(MXU_DTYPE)
    kr = jnp.where(lane < MLA_NOPE, kv_ref[...].astype(F32), _rope_lanes(kpe_ref[...], c, s_up, s_down)).astype(MXU_DTYPE)
    return qr, kr, lane


def _mla_specs(Tp):
    head = pl.BlockSpec((None, Tp, HEAD_LANES), lambda b, h: (b, 0, h))
    shared = pl.BlockSpec((None, Tp, HEAD_LANES), lambda b, h: (b, 0, 0))
    tab = pl.BlockSpec((Tp, HEAD_LANES), lambda b, h: (0, 0))
    lse = pl.BlockSpec((None, None, Tp, 1), lambda b, h: (b, h, 0, 0))
    return head, shared, tab, lse


def _attn_fwd_call(q, kv, kpe, tabs):
    B, Tp, _ = q.shape

    def body(q_ref, kv_ref, kpe_ref, c_ref, su_ref, sd_ref, o_ref, lse_ref, qr_ref, kr_ref):
        qr, kr, lane = _mla_operands(q_ref, kv_ref, kpe_ref, c_ref[...], su_ref[...], sd_ref[...])
        qr_ref[...] = qr
        kr_ref[...] = kr
        for r0, L in _query_blocks(Tp):
            blk = slice(r0, L)
            s = _mask_diagonal(lax.dot_general(qr_ref[blk, :], kr_ref[0:L, :], _NT, preferred_element_type=F32), NEG_INF)
            m = jnp.max(s, axis=-1, keepdims=True)
            p = jnp.exp2(s - m)
            l = jnp.sum(p, axis=-1, keepdims=True)
            o = jnp.dot(p.astype(MXU_DTYPE), kv_ref[0:L, :].astype(MXU_DTYPE), preferred_element_type=F32)
            o_ref[blk, :] = jnp.where(lane[blk, :] >= MLA_NOPE, o / l, 0.0)
            lse_ref[blk, :] = m + jnp.log2(l)

    head, shared, tab, lse = _mla_specs(Tp)
    return pl.pallas_call(
        body, name="mla_attn_fwd", grid=(B, MLA_HEADS), in_specs=[head, head, shared, tab, tab, tab], out_specs=[head, lse],
        out_shape=[jax.ShapeDtypeStruct((B, Tp, MLA_HEADS * HEAD_LANES), F32), jax.ShapeDtypeStruct((B, MLA_HEADS, Tp, 1), F32)],
        scratch_shapes=[pltpu.VMEM((Tp, HEAD_LANES), MXU_DTYPE), pltpu.VMEM((Tp, HEAD_LANES), MXU_DTYPE)],
        compiler_params=pltpu.CompilerParams(dimension_semantics=("parallel", "parallel")),
    )(q, kv, kpe, *tabs)


def _attn_bwd_call(q, kv, kpe, tabs, o, lse, do):
    B, Tp, _ = q.shape

    def body(q_ref, kv_ref, kpe_ref, c_ref, su_ref, sd_ref, o_ref, lse_ref, do_ref, dq_ref, dkv_ref, dkpe_ref,
             qr_ref, kr_ref, dqa_ref, dka_ref, dva_ref):
        c, s_up, s_down = c_ref[...], su_ref[...], sd_ref[...]
        qr, kr, lane = _mla_operands(q_ref, kv_ref, kpe_ref, c, s_up, s_down)
        qr_ref[...] = qr
        kr_ref[...] = kr
        dka_ref[...] = jnp.zeros_like(dka_ref)
        dva_ref[...] = jnp.zeros_like(dva_ref)
        for r0, L in _query_blocks(Tp):
            blk = slice(r0, L)
            qb = qr_ref[blk, :]
            do = jnp.where(lane[blk, :] >= MLA_NOPE, do_ref[blk, :], 0.0)
            delta = jnp.sum(do * o_ref[blk, :], axis=-1, keepdims=True)
            s = _mask_diagonal(lax.dot_general(qb, kr_ref[0:L, :], _NT, preferred_element_type=F32), NEG_INF)
            p = jnp.exp2(s - lse_ref[blk, :])
            dob = do.astype(MXU_DTYPE)
            dva_ref[0:L, :] += lax.dot_general(p.astype(MXU_DTYPE), dob, _TN, preferred_element_type=F32)
            dp = lax.dot_general(dob, kv_ref[0:L, :].astype(MXU_DTYPE), _NT, preferred_element_type=F32)
            ds = (p * (dp - delta)).astype(MXU_DTYPE)
            dqa_ref[blk, :] = jnp.dot(ds, kr_ref[0:L, :], preferred_element_type=F32)
            dka_ref[0:L, :] += lax.dot_general(ds, qb, _TN, preferred_element_type=F32)
        dq_ref[...] = _unrope_lanes(dqa_ref[...] * _MLA_SCALE, c, s_up, s_down).astype(dq_ref.dtype)
        dk = dka_ref[...] * (1.0 / _LOG2E)
        dkv_ref[...] = jnp.where(lane < MLA_NOPE, dk, dva_ref[...]).astype(dkv_ref.dtype)
        dkpe = jnp.where(lane >= MLA_NOPE, _unrope_lanes(dk, c, s_up, s_down), 0.0)

        @pl.when(pl.program_id(1) == 0)
        def _():
            dkpe_ref[...] = dkpe

        @pl.when(pl.program_id(1) > 0)
        def _():
            dkpe_ref[...] += dkpe

    head, shared, tab, lse_spec = _mla_specs(Tp)
    wide = jax.ShapeDtypeStruct((B, Tp, MLA_HEADS * HEAD_LANES), q.dtype)
    acc = pltpu.VMEM((Tp, HEAD_LANES), F32)
    return pl.pallas_call(
        body, name="mla_attn_bwd", grid=(B, MLA_HEADS),
        in_specs=[head, head, shared, tab, tab, tab, head, lse_spec, head], out_specs=[head, head, shared],
        out_shape=[wide, wide, jax.ShapeDtypeStruct((B, Tp, HEAD_LANES), F32)],
        scratch_shapes=[pltpu.VMEM((Tp, HEAD_LANES), MXU_DTYPE), pltpu.VMEM((Tp, HEAD_LANES), MXU_DTYPE), acc, acc, acc],
        compiler_params=pltpu.CompilerParams(dimension_semantics=("parallel", "arbitrary")),
    )(q, kv, kpe, *tabs, o, lse, do)


@jax.custom_vjp
def mla_attention(q, kv, kpe, tabs):
    return _attn_fwd_call(q, kv, kpe, tabs)[0]


def _mla_attention_fwd(q, kv, kpe, tabs):
    o, lse = _attn_fwd_call(q, kv, kpe, tabs)
    return o, (q, kv, kpe, tabs, o, lse)


def _mla_attention_bwd(res, do):
    q, kv, kpe, tabs, o, lse = res
    dq, dkv, dkpe = _attn_bwd_call(q, kv, kpe, tabs, o, lse, do)
    return dq, dkv, dkpe, None


mla_attention.defvjp(_mla_attention_fwd, _mla_attention_bwd)


def _rope_halves(x, cos, sin):
    half = x.shape[1] // 2
    x1, x2 = x[:, :half], x[:, half:]
    return jnp.concatenate([x1 * cos - x2 * sin, x1 * sin + x2 * cos], axis=1)


def _unrope_halves(d, cos, sin):
    half = d.shape[1] // 2
    d1, d2 = d[:, :half], d[:, half:]
    return jnp.concatenate([d1 * cos + d2 * sin, d2 * cos - d1 * sin], axis=1)


_RET_K_SCALE = RET_QK_DIM ** -0.5
_RET_Q_BLOCKS = RET_HEADS
_RET_V_BLOCK0 = 2 * RET_HEADS * RET_QK_DIM // RET_V_DIM
_RET_G_BLOCK0 = _RET_V_BLOCK0 + RET_HEADS


def _ret_specs(Tp):
    q = pl.BlockSpec((None, Tp, RET_QK_DIM), lambda b, h: (b, 0, h))
    k = pl.BlockSpec((None, Tp, RET_QK_DIM), lambda b, h: (b, 0, _RET_Q_BLOCKS + h))
    v = pl.BlockSpec((None, Tp, RET_V_DIM), lambda b, h: (b, 0, _RET_V_BLOCK0 + h))
    tab = pl.BlockSpec((Tp, RET_QK_DIM // 2), lambda b, h: (0, 0))
    lg = pl.BlockSpec((None, 1, 1), lambda b, h: (h, 0, 0))
    return q, k, v, tab, lg


def _ret_operands(q_ref, k_ref, cos, sin, lg):
    t = lax.broadcasted_iota(jnp.int32, (q_ref.shape[0], 1), 0).astype(F32)
    grow, shrink = jnp.exp(-lg * t), jnp.exp(lg * t)
    qs = (_rope_halves(q_ref[...].astype(F32), cos, sin) * shrink).astype(MXU_DTYPE)
    ks = (_rope_halves(k_ref[...].astype(F32), cos, sin) * (grow * _RET_K_SCALE)).astype(MXU_DTYPE)
    return qs, ks, shrink, grow * _RET_K_SCALE


def _ret_core_fwd_call(p, cos, sin, lg):
    B, Tp, _ = p.shape

    def body(q_ref, k_ref, v_ref, cos_ref, sin_ref, lg_ref, o_ref, qs_ref, ks_ref):
        qs_ref[...], ks_ref[...], _, _ = _ret_operands(q_ref, k_ref, cos_ref[...], sin_ref[...], lg_ref[...])
        for r0, L in _query_blocks(Tp):
            blk = slice(r0, L)
            s = _mask_diagonal(lax.dot_general(qs_ref[blk, :], ks_ref[0:L, :], _NT, preferred_element_type=F32), 0.0)
            o_ref[blk, :] = jnp.dot(s.astype(MXU_DTYPE), v_ref[0:L, :].astype(MXU_DTYPE), preferred_element_type=F32)

    q, k, v, tab, lgs = _ret_specs(Tp)
    return pl.pallas_call(
        body, name="retention_fwd", grid=(B, RET_HEADS), in_specs=[q, k, v, tab, tab, lgs],
        out_specs=pl.BlockSpec((None, Tp, RET_V_DIM), lambda b, h: (b, 0, h)),
        out_shape=jax.ShapeDtypeStruct((B, Tp, RET_HEADS * RET_V_DIM), F32),
        scratch_shapes=[pltpu.VMEM((Tp, RET_QK_DIM), MXU_DTYPE), pltpu.VMEM((Tp, RET_QK_DIM), MXU_DTYPE)],
        compiler_params=pltpu.CompilerParams(dimension_semantics=("parallel", "parallel")),
    )(p, p, p, cos, sin, lg)


def _ret_core_bwd_call(p, do, cos, sin, lg):
    B, Tp, _ = p.shape

    def body(q_ref, k_ref, v_ref, do_ref, cos_ref, sin_ref, lg_ref, dq_ref, dk_ref, dv_ref, qs_ref, ks_ref, dqa_ref, dka_ref, dva_ref):
        cos_, sin_ = cos_ref[...], sin_ref[...]
        qs_ref[...], ks_ref[...], q_scale, k_scale = _ret_operands(q_ref, k_ref, cos_, sin_, lg_ref[...])
        dka_ref[...] = jnp.zeros_like(dka_ref)
        dva_ref[...] = jnp.zeros_like(dva_ref)
        for r0, L in _query_blocks(Tp):
            blk = slice(r0, L)
            qb = qs_ref[blk, :]
            dob = do_ref[blk, :].astype(MXU_DTYPE)
            s = _mask_diagonal(lax.dot_general(qb, ks_ref[0:L, :], _NT, preferred_element_type=F32), 0.0).astype(MXU_DTYPE)
            dva_ref[0:L, :] += lax.dot_general(s, dob, _TN, preferred_element_type=F32)
            ds = _mask_diagonal(lax.dot_general(dob, v_ref[0:L, :].astype(MXU_DTYPE), _NT, preferred_element_type=F32), 0.0).astype(MXU_DTYPE)
            dqa_ref[blk, :] = jnp.dot(ds, ks_ref[0:L, :], preferred_element_type=F32)
            dka_ref[0:L, :] += lax.dot_general(ds, qb, _TN, preferred_element_type=F32)
        dq_ref[...] = _unrope_halves(dqa_ref[...] * q_scale, cos_, sin_).astype(dq_ref.dtype)
        dk_ref[...] = _unrope_halves(dka_ref[...] * k_scale, cos_, sin_).astype(dk_ref.dtype)
        dv_ref[...] = dva_ref[...].astype(dv_ref.dtype)

    q, k, v, tab, lgs = _ret_specs(Tp)
    qk_out = pl.BlockSpec((None, Tp, RET_QK_DIM), lambda b, h: (b, 0, h))
    v_out = pl.BlockSpec((None, Tp, RET_V_DIM), lambda b, h: (b, 0, h))
    return pl.pallas_call(
        body, name="retention_bwd", grid=(B, RET_HEADS), in_specs=[q, k, v, v_out, tab, tab, lgs],
        out_specs=[qk_out, qk_out, v_out],
        out_shape=[jax.ShapeDtypeStruct((B, Tp, RET_HEADS * RET_QK_DIM), p.dtype), jax.ShapeDtypeStruct((B, Tp, RET_HEADS * RET_QK_DIM), p.dtype),
                   jax.ShapeDtypeStruct((B, Tp, RET_HEADS * RET_V_DIM), p.dtype)],
        scratch_shapes=[pltpu.VMEM((Tp, RET_QK_DIM), MXU_DTYPE), pltpu.VMEM((Tp, RET_QK_DIM), MXU_DTYPE),
                        pltpu.VMEM((Tp, RET_QK_DIM), F32), pltpu.VMEM((Tp, RET_QK_DIM), F32), pltpu.VMEM((Tp, RET_V_DIM), F32)],
        compiler_params=pltpu.CompilerParams(dimension_semantics=("parallel", "parallel")),
    )(p, p, p, do, cos, sin, lg)


def _ret_gate_specs(M):
    tm = _pick(M, 1088, 8)
    head = pl.BlockSpec((tm, RET_V_DIM), lambda i, h: (i, h))
    gate = pl.BlockSpec((tm, RET_V_DIM), lambda i, h: (i, _RET_G_BLOCK0 + h))
    return tm, head, gate


def _ret_gate_fwd_call(o, p2d):
    M = o.shape[0]
    tm, head, gate = _ret_gate_specs(M)

    def body(o_ref, g_ref, y_ref):
        ov = o_ref[...]
        gv = g_ref[...].astype(F32)
        rstd = lax.rsqrt(jnp.mean(ov * ov, axis=-1, keepdims=True) + EPS)
        y_ref[...] = (gv * _sigmoid(gv)) * (ov * rstd)

    return pl.pallas_call(
        body, name="retention_gate_fwd", grid=(M // tm, RET_HEADS), in_specs=[head, gate], out_specs=head,
        out_shape=jax.ShapeDtypeStruct(o.shape, F32),
        compiler_params=pltpu.CompilerParams(dimension_semantics=("parallel", "parallel")),
    )(o, p2d)


def _ret_gate_bwd_call(o, p2d, dy):
    M = o.shape[0]
    tm, head, gate = _ret_gate_specs(M)

    def body(o_ref, g_ref, dy_ref, do_ref, dg_ref):
        ov = o_ref[...]
        gv = g_ref[...].astype(F32)
        dy = dy_ref[...]
        rstd = lax.rsqrt(jnp.mean(ov * ov, axis=-1, keepdims=True) + EPS)
        on = ov * rstd
        sg = _sigmoid(gv)
        dg_ref[...] = (dy * on * (sg * (1.0 + gv * (1.0 - sg)))).astype(dg_ref.dtype)
        don = dy * (gv * sg)
        do_ref[...] = (rstd * (don - on * jnp.mean(don * on, axis=-1, keepdims=True))).astype(do_ref.dtype)

    shp = jax.ShapeDtypeStruct(o.shape, p2d.dtype)
    return pl.pallas_call(
        body, name="retention_gate_bwd", grid=(M // tm, RET_HEADS), in_specs=[head, gate, head], out_specs=[head, head],
        out_shape=[shp, shp],
        compiler_params=pltpu.CompilerParams(dimension_semantics=("parallel", "parallel")),
    )(o, p2d, dy)


def _log_gamma():
    return jnp.log(1.0 - 2.0 ** (-5.0 - jnp.arange(RET_HEADS, dtype=F32))).reshape(RET_HEADS, 1, 1)


@functools.partial(jax.custom_vjp, nondiff_argnums=(9,))
def retention_block(h, w_in, w_out, w_in_grad_slot, w_out_grad_slot, g, b, cos, sin, dims):
    return _retention_block_fwd(h, w_in, w_out, w_in_grad_slot, w_out_grad_slot, g, b, cos, sin, dims)[0]


def _retention_block_fwd(h, w_in, w_out, w_in_grad_slot, w_out_grad_slot, g, b, cos, sin, dims):
    B, Tp = dims
    p = _mm_nn(h, w_in, False, "od_w_in_fwd", out_dtype=MXU_DTYPE)
    o = _ret_core_fwd_call(p.reshape(B, Tp, -1), cos, sin, _log_gamma())
    y = _ret_gate_fwd_call(o.reshape(B * Tp, -1), p)
    out, z = _mm_nn(y, w_out, False, "od_w_out_norm_fwd", norm=(h, g, b))
    return out, (h, p, o, y, z, w_in, w_out, g, cos, sin, jnp.zeros((), w_in_grad_slot.dtype))


def _retention_block_bwd(dims, res, dout):
    B, Tp = dims
    h, p, o, y, z, w_in, w_out, g, cos, sin, slot_like = res
    dz, dg, db = _ln_bwd_call(z, g, dout, "od_w_out_norm_bwd")
    dy = _mm_nt(dz, w_out, None, "od_w_out_dx")
    dw_out = _mm_tn(y, dz, False, "od_w_out_dw", 1, slot_like.dtype)
    do, dgate = _ret_gate_bwd_call(o.reshape(B * Tp, -1), p, dy)
    dq, dk, dv = _ret_core_bwd_call(p.reshape(B, Tp, -1), do.reshape(B, Tp, -1), cos, sin, _log_gamma())
    dp = jnp.concatenate([dq.reshape(B * Tp, -1), dk.reshape(B * Tp, -1), dv.reshape(B * Tp, -1), dgate], axis=-1)
    dh = _mm_nt(dp, w_in, None, "od_w_in_dx", plus=dz)
    dw_in = _mm_tn(h, dp, False, "od_w_in_dw", N_CHIPS, slot_like.dtype)
    return dh, None, None, dw_in, dw_out, dg.reshape(g.shape), db.reshape(g.shape), None, None


retention_block.defvjp(_retention_block_fwd, _retention_block_bwd)


def _heads_to_lanes(w):
    K = w.shape[0]
    w = w.reshape(K, MLA_HEADS, MLA_NOPE + MLA_ROPE)
    return jnp.pad(w, ((0, 0), (0, 0), (0, HEAD_LANES - MLA_NOPE - MLA_ROPE))).reshape(K, MLA_HEADS * HEAD_LANES)


def _out_rows_to_lanes(w):
    N = w.shape[1]
    att = w[LRU_WIDTH:].reshape(MLA_HEADS, MLA_V, N)
    att = jnp.pad(att, ((0, 0), (HEAD_LANES - MLA_V, 0), (0, 0))).reshape(MLA_HEADS * HEAD_LANES, N)
    return jnp.concatenate([w[:LRU_WIDTH], att], axis=0)


def _seq_dims(x):
    B, S, D = x.shape
    T = S + N_META
    Tp = _round_up(T, SEQ_BLOCK)
    return B, S, T, Tp


def _mixer0(diff, w, token):
    x = diff["x"]
    B, S, T, Tp = _seq_dims(x)
    D = x.shape[-1]
    M = B * Tp
    pos = jnp.arange(Tp, dtype=jnp.int32)

    def mm(a, name, act=False, out_dtype=F32, layout=lambda m: m, col_shards=1):
        return matmul(a, layout(w[name]), layout(diff[name]), act, name, out_dtype, col_shards)

    meta = jnp.broadcast_to(diff["meta_tokens"][None], (B, N_META, D))
    h = jnp.concatenate([meta, x + token, jnp.zeros((B, Tp - T, D), F32)], axis=1).reshape(M, D)
    p = mm(h, "ev_w_in")
    sp = jax.nn.softplus(-diff["ev_lru_lambda"]).reshape(1, LRU_WIDTH)
    y_rec, qn, kvn, kpe = even_front(
        p.reshape(B, Tp, -1), diff["ev_conv_w"].reshape(CONV_WIDTH, LRU_WIDTH), diff["ev_conv_b"].reshape(1, LRU_WIDTH),
        diff["ev_w_rg_a"].reshape(LRU_HEADS, LRU_HEAD_DIM, LRU_HEAD_DIM), diff["ev_b_rg_a"].reshape(1, LRU_WIDTH),
        diff["ev_w_rg_x"].reshape(LRU_HEADS, LRU_HEAD_DIM, LRU_HEAD_DIM), diff["ev_b_rg_x"].reshape(1, LRU_WIDTH),
        sp, diff["ev_q_norm_g"].reshape(-1), diff["ev_kv_norm_g"].reshape(-1))
    y_rec = y_rec.reshape(M, LRU_WIDTH)
    q = mm(qn, "ev_w_uq", out_dtype=MXU_DTYPE, layout=_heads_to_lanes).reshape(B, Tp, -1)
    kv = mm(kvn, "ev_w_ukv", out_dtype=MXU_DTYPE).reshape(B, Tp, -1)
    y_att = mla_attention(q, kv, kpe, _mla_rope_tables(pos)).reshape(M, -1)
    return out_block(h, jnp.concatenate([y_rec, y_att], axis=-1), _out_rows_to_lanes(w["ev_w_out"]), _out_rows_to_lanes(diff["ev_w_out"]),
                     diff["ln_mix_g"], diff["ln_mix_b"], "ev_w_out")


def _mlp0(diff, h, w):
    return mlp_block(h, w["mlp_w1_0"], w["mlp_w2_0"], diff["mlp_w1_0"], diff["mlp_w2_0"], diff["ln_mlp_g"], diff["ln_mlp_b"], "mlp0")


def _layer1_loss(diff, h, w, tgt):
    B, S, T, Tp = _seq_dims(tgt)
    D = tgt.shape[-1]
    pos = jnp.arange(Tp, dtype=jnp.int32)

    cos, sin = _rope_tables(pos, RET_QK_DIM // 2)
    h = retention_block(h, w["od_w_in"], w["od_w_out"], diff["od_w_in"], diff["od_w_out"], diff["ln_mix_g"], diff["ln_mix_b"], cos, sin, (B, Tp))
    h = mlp_block(h, w["mlp_w1_1"], w["mlp_w2_1"], diff["mlp_w1_1"], diff["mlp_w2_1"], diff["ln_mlp_g"], diff["ln_mlp_b"], "mlp1")
    return loss_head(h.reshape(B, Tp, D), jnp.pad(tgt, ((0, 0), (N_META, Tp - T), (0, 0))), S)


_HBM = pl.BlockSpec(memory_space=pltpu.HBM)


def _place():
    return lax.axis_index("x"), lax.axis_index("y"), lax.axis_index("c")


def _other_chips(x, y):
    return [(1 - x, y), (x, 1 - y), (1 - x, 1 - y)]


def _chunks(rows, sublanes, most):
    for q in range(most, 0, -1):
        if rows % (q * sublanes) == 0:
            return q
    return 1


def _sublanes(dtype):
    return 8 * 4 // jnp.dtype(dtype).itemsize


def _gather_pieces(bufs):
    plan, first = [], []
    for b in bufs:
        Rh = b.shape[0] // 2
        Q = _chunks(Rh, _sublanes(b.dtype), 4) if Rh * b.shape[1] * b.dtype.itemsize > (1 << 20) else 1
        first.append(3 * sum(q for _, q, _ in plan))
        plan.append((Rh, Q, Rh // Q))
    return plan, first, 3 * sum(q for _, q, _ in plan)


def _allgather_chips(bufs, name):
    n = len(bufs)
    plan, first, n_sems = _gather_pieces(bufs)

    def body(*refs):
        x_refs, out_refs, (send_sems, recv_sems) = refs[:n], refs[n:2 * n], refs[2 * n:]
        x, y, c = _place()
        sibling = (x, y, 1 - c)
        chips = _other_chips(x, y)

        def copy(k, src, dst, to):
            return pltpu.make_async_remote_copy(src_ref=src, dst_ref=dst, send_sem=send_sems.at[k], recv_sem=recv_sems.at[k],
                                                device_id=to, device_id_type=MESH)

        def piece(i, cx, cy, hc, q):
            Rh, _, ch = plan[i]
            return out_refs[i].at[2 * cx + cy, pl.ds(hc * Rh + q * ch, ch), :]

        slots = [(i, q, j) for i in range(n) for q in range(plan[i][1]) for j in range(3)]
        sem = {(i, q, j): first[i] + 3 * q + j for i, q, j in slots}
        sent = [copy(sem[i, q, j], x_refs[i].at[pl.ds(c * plan[i][0] + q * plan[i][2], plan[i][2]), :], piece(i, x, y, c, q), (*chips[j], c))
                for i, q, j in slots]
        for cp in sent:
            cp.start()
        passed = []
        for i, q, j in slots:
            landed = piece(i, *chips[j], c, q)
            copy(sem[i, q, j], landed, landed, sibling).wait_recv()
            fwd = copy(n_sems + sem[i, q, j], landed, landed, sibling)
            fwd.start()
            passed.append(fwd)
        for i, q, j in slots:
            theirs = piece(i, *chips[j], 1 - c, q)
            copy(n_sems + sem[i, q, j], theirs, theirs, sibling).wait_recv()
        for cp in sent + passed:
            cp.wait_send()

    return pl.pallas_call(
        body, name=name, in_specs=[_HBM] * n, out_specs=[_HBM] * n,
        out_shape=[jax.ShapeDtypeStruct((N_CHIPS,) + b.shape, b.dtype) for b in bufs],
        scratch_shapes=[pltpu.SemaphoreType.DMA((2 * n_sems,)), pltpu.SemaphoreType.DMA((2 * n_sems,))],
    )(*bufs)


def _with_own(gathered, own):
    my = 2 * lax.axis_index("x") + lax.axis_index("y")
    return lax.dynamic_update_slice(gathered, own[None], (my, 0, 0))


def _sibling_gather(fs, name):
    n = len(fs)

    def body(*refs):
        out_refs, (send_sems, recv_sems) = refs[n:2 * n], refs[2 * n:]
        x, y, c = _place()
        copies = [pltpu.make_async_remote_copy(src_ref=out_ref.at[c], dst_ref=out_ref.at[c], send_sem=send_sems.at[i], recv_sem=recv_sems.at[i],
                                               device_id=(x, y, 1 - c), device_id_type=MESH) for i, out_ref in enumerate(out_refs)]
        for cp in copies:
            cp.start()
        for cp in copies:
            cp.wait()

    return pl.pallas_call(
        body, name=name, in_specs=[_HBM] * n, out_specs=[_HBM] * n,
        out_shape=[jax.ShapeDtypeStruct(f.shape, f.dtype) for f in fs], input_output_aliases={i: i for i in range(n)},
        scratch_shapes=[pltpu.SemaphoreType.DMA((n,)), pltpu.SemaphoreType.DMA((n,))],
    )(*fs)


def _axis_scalar(name):
    return lax.axis_index(name).astype(jnp.int32).reshape(1)


_SEM = pl.BlockSpec(memory_space=pltpu.SEMAPHORE)
_ANY = pl.BlockSpec(memory_space=pl.ANY)
_EFFECT = pltpu.SideEffectType.DATAFLOW_SIDE_EFFECTING


def _in_hbm(a):
    return pltpu.with_memory_space_constraint(a, pltpu.HBM)


def _half_copies(x_refs, land_refs, send_sems, recv_sems, arriving):
    x, y, c = _place()
    copies = []
    for i, (x_ref, land_ref) in enumerate(zip(x_refs, land_refs)):
        Rh = x_ref.shape[0] // 2
        rows = pl.ds(c * Rh, Rh)
        for j, (cx, cy) in enumerate(_other_chips(x, y)):
            copies.append(pltpu.make_async_remote_copy(
                src_ref=x_ref.at[rows, :], dst_ref=land_ref.at[2 * cx + cy if arriving else 2 * x + y, rows, :],
                send_sem=send_sems.at[3 * i + j], recv_sem=recv_sems.at[3 * i + j], device_id=(cx, cy, c), device_id_type=MESH))
    return copies


def _allgather_start(bufs, name):
    n = len(bufs)

    def body(*refs):
        x_refs, land_refs, (send_sems, recv_sems), token = refs[:n], refs[n:2 * n], refs[2 * n:2 * n + 2], refs[-1]
        for cp in _half_copies(x_refs, land_refs, send_sems, recv_sems, False):
            cp.start()
        token[...] = jnp.zeros_like(token)

    lands = [lax.empty((N_CHIPS,) + b.shape, b.dtype) for b in bufs]
    out = pl.pallas_call(
        body, name=name,
        out_shape=(pltpu.SemaphoreType.DMA((3 * n,)), pltpu.SemaphoreType.DMA((3 * n,)), *[pltpu.HBM(a.shape, a.dtype) for a in bufs + lands],
                   jax.ShapeDtypeStruct((8, 128), F32)),
        in_specs=[_HBM] * (2 * n), out_specs=(_SEM, _SEM, *[_HBM] * (2 * n), pl.BlockSpec(memory_space=pltpu.VMEM)),
        input_output_aliases={i: 2 + i for i in range(2 * n)}, compiler_params=pltpu.CompilerParams(has_side_effects=_EFFECT),
    )(*[_in_hbm(a) for a in bufs + lands])
    return (out[0], out[1], list(out[2:2 + n]), list(out[2 + n:2 + 2 * n])), out[-1][0, 0]


def _allgather_wait(pending, after, name):
    send_sems, recv_sems, bufs, lands = pending
    n = len(bufs)

    def body(*refs):
        x_refs, land_refs, send_sems, recv_sems = refs[:n], refs[n:2 * n], refs[2 * n], refs[2 * n + 1]
        for cp in _half_copies(x_refs, land_refs, send_sems, recv_sems, False):
            cp.wait_send()
        for cp in _half_copies(x_refs, land_refs, send_sems, recv_sems, True):
            cp.wait_recv()

    out = pl.pallas_call(
        body, name=name, out_shape=tuple(pltpu.HBM(a.shape, a.dtype) for a in bufs + lands),
        in_specs=[_HBM] * (2 * n) + [_SEM, _SEM, _ANY], out_specs=tuple([_HBM] * (2 * n)), input_output_aliases={i: i for i in range(2 * n)},
        compiler_params=pltpu.CompilerParams(has_side_effects=_EFFECT),
    )(*bufs, *lands, send_sems, recv_sems, after)
    return list(out[n:])


def _sibling_forward(lands, name):
    n = len(lands)
    plan, first, n_sems = _gather_pieces([jax.ShapeDtypeStruct(l.shape[1:], l.dtype) for l in lands])

    def body(*refs):
        out_refs, (send_sems, recv_sems) = refs[n:2 * n], refs[2 * n:]
        x, y, c = _place()

        def copies(hc):
            return [pltpu.make_async_remote_copy(
                        src_ref=out_refs[i].at[2 * cx + cy, pl.ds(hc * plan[i][0] + q * plan[i][2], plan[i][2]), :],
                        dst_ref=out_refs[i].at[2 * cx + cy, pl.ds(hc * plan[i][0] + q * plan[i][2], plan[i][2]), :],
                        send_sem=send_sems.at[first[i] + 3 * q + j], recv_sem=recv_sems.at[first[i] + 3 * q + j],
                        device_id=(x, y, 1 - c), device_id_type=MESH)
                    for i in range(n) for q in range(plan[i][1]) for j, (cx, cy) in enumerate(_other_chips(x, y))]

        mine = copies(c)
        for cp in mine:
            cp.start()
        for cp in mine:
            cp.wait_send()
        for cp in copies(1 - c):
            cp.wait_recv()

    return pl.pallas_call(
        body, name=name, in_specs=[_HBM] * n, out_specs=[_HBM] * n, out_shape=[jax.ShapeDtypeStruct(l.shape, l.dtype) for l in lands],
        input_output_aliases={i: i for i in range(n)},
        scratch_shapes=[pltpu.SemaphoreType.DMA((n_sems,)), pltpu.SemaphoreType.DMA((n_sems,))],
    )(*lands)


N_PEERS = 7


def _direct_copies(p_refs, t_refs, send_sems, recv_sems):
    x, y, c = _place()
    copies = []
    for i, (p_ref, t_ref) in enumerate(zip(p_refs, t_refs)):
        for f in range(1, N_PEERS + 1):
            px, py, pc = x ^ (f >> 2), y ^ ((f >> 1) & 1), c ^ (f & 1)
            copies.append(pltpu.make_async_remote_copy(
                src_ref=p_ref.at[2 * px + py, pc], dst_ref=t_ref.at[f - 1], send_sem=send_sems.at[N_PEERS * i + f - 1],
                recv_sem=recv_sems.at[N_PEERS * i + f - 1], device_id=(px, py, pc), device_id_type=MESH))
    return copies


def _direct_scatter_start(ps, name):
    n = len(ps)

    def body(*refs):
        p_refs, t_refs, (send_sems, recv_sems), token = refs[:n], refs[n:2 * n], refs[2 * n:2 * n + 2], refs[-1]
        for cp in _direct_copies(p_refs, t_refs, send_sems, recv_sems):
            cp.start()
        token[...] = jnp.zeros_like(token)

    lands = [lax.empty((N_PEERS,) + p.shape[2:], p.dtype) for p in ps]
    out = pl.pallas_call(
        body, name=name,
        out_shape=(pltpu.SemaphoreType.DMA((N_PEERS * n,)), pltpu.SemaphoreType.DMA((N_PEERS * n,)),
                   *[pltpu.HBM(a.shape, a.dtype) for a in ps + lands], jax.ShapeDtypeStruct((8, 128), F32)),
        in_specs=[_HBM] * (2 * n), out_specs=(_SEM, _SEM, *[_HBM] * (2 * n), pl.BlockSpec(memory_space=pltpu.VMEM)),
        input_output_aliases={i: 2 + i for i in range(2 * n)}, compiler_params=pltpu.CompilerParams(has_side_effects=_EFFECT),
    )(*[_in_hbm(a) for a in ps + lands])
    return (out[0], out[1], list(out[2:2 + n]), list(out[2 + n:2 + 2 * n])), out[-1][0, 0]


def _direct_scatter_wait(pending, after, name):
    send_sems, recv_sems, ps, lands = pending
    n = len(ps)

    def body(*refs):
        p_refs, t_refs, send_sems, recv_sems = refs[:n], refs[n:2 * n], refs[2 * n], refs[2 * n + 1]
        for cp in _direct_copies(p_refs, t_refs, send_sems, recv_sems):
            cp.wait_send()
            cp.wait_recv()

    out = pl.pallas_call(
        body, name=name, out_shape=tuple(pltpu.HBM(a.shape, a.dtype) for a in ps + lands),
        in_specs=[_HBM] * (2 * n) + [_SEM, _SEM] + [_ANY] * len(after), out_specs=tuple([_HBM] * (2 * n)),
        input_output_aliases={i: i for i in range(2 * n)}, compiler_params=pltpu.CompilerParams(has_side_effects=_EFFECT),
    )(*ps, *lands, send_sems, recv_sems, *after)
    return list(out[:n]), list(out[n:])


def _sum_direct(p, t, name):
    _, _, R, C = p.shape
    tr = _pick(R, 512, 16)

    def body(x_ref, y_ref, c_ref, p_ref, t_ref, o_ref):
        acc = p_ref[...].astype(F32)
        for f in range(N_PEERS):
            acc = acc + t_ref[f].astype(F32)
        o_ref[...] = acc

    grid_spec = pltpu.PrefetchScalarGridSpec(
        num_scalar_prefetch=3, grid=(R // tr,),
        in_specs=[pl.BlockSpec((None, None, tr, C), lambda i, x_ref, y_ref, c_ref: (2 * x_ref[0] + y_ref[0], c_ref[0], i, 0)),
                  pl.BlockSpec((N_PEERS, tr, C), lambda i, x_ref, y_ref, c_ref: (0, i, 0))],
        out_specs=pl.BlockSpec((None, tr, C), lambda i, x_ref, y_ref, c_ref: (c_ref[0], i, 0)))
    return pl.pallas_call(body, name=name, grid_spec=grid_spec, out_shape=jax.ShapeDtypeStruct((2, R, C), F32),
                          compiler_params=pltpu.CompilerParams(dimension_semantics=("parallel",)))(
        _axis_scalar("x"), _axis_scalar("y"), _axis_scalar("c"), p, t)


def _adamw(w, g, m, v, name):
    R, C = w.shape
    tr = _pick(R, 256, 8)

    def body(w_ref, g_ref, m_ref, v_ref, d_ref, nm_ref, nv_ref):
        g_ = g_ref[...]
        m_ = ADAM_B1 * m_ref[...] + (1.0 - ADAM_B1) * g_
        v_ = ADAM_B2 * v_ref[...] + (1.0 - ADAM_B2) * (g_ * g_)
        m_hat = m_ / (1.0 - ADAM_B1 ** ADAM_STEP)
        v_hat = v_ / (1.0 - ADAM_B2 ** ADAM_STEP)
        d_ref[...] = -ADAM_LR * (m_hat / (jnp.sqrt(v_hat) + ADAM_EPS) + ADAM_WD * w_ref[...])
        nm_ref[...] = m_
        nv_ref[...] = v_

    row = pl.BlockSpec((tr, C), lambda i: (i, 0))
    shp = jax.ShapeDtypeStruct((R, C), F32)
    return pl.pallas_call(body, name=name, grid=(R // tr,), in_specs=[row] * 4, out_specs=[row] * 3, out_shape=[shp] * 3,
                          compiler_params=pltpu.CompilerParams(dimension_semantics=("parallel",)))(w, g, m, v)


BIG_SPECS = (("ev_w_in", 1024, 1440, 1), ("ev_w_uq", 256, 768, 1), ("ev_w_ukv", 128, 1024, 1), ("ev_w_out", 1024, 1024, 0),
             ("od_w_in", 1024, 6144, 1), ("od_w_out", 2048, 1024, 0), ("mlp_w1_0", 1024, 4096, 1), ("mlp_w1_1", 1024, 4096, 1),
             ("mlp_w2_0", 4096, 1024, 0), ("mlp_w2_1", 4096, 1024, 0))
BIG_PARAMS = (("ev_w_in", ("ev_w_in",)), ("ev_w_uq", ("ev_w_uq",)), ("ev_w_ukv", ("ev_w_ukv",)), ("ev_w_out", ("ev_w_out",)),
              ("od_w_in", ("od_w_in",)), ("od_w_out", ("od_w_out",)), ("mlp_w1", ("mlp_w1_0", "mlp_w1_1")),
              ("mlp_w2", ("mlp_w2_0", "mlp_w2_1")))
REPLICATED = ("ev_conv_b", "ev_w_rg_a", "ev_b_rg_a", "ev_w_rg_x", "ev_b_rg_x", "ev_lru_lambda", "ev_q_norm_g", "ev_kv_norm_g",
              "ln_mix_g", "ln_mix_b", "ln_mlp_g", "ln_mlp_b")
SMALL_SHARDED = ("meta_tokens", "ev_conv_w")
COL_SHARD_GRADS = ("od_w_in", "mlp_w1_0", "mlp_w1_1")
MATRIX_GROUPS = (("ev_w_in", "ev_w_uq", "ev_w_ukv", "ev_w_out"), ("mlp_w1_0", "mlp_w2_0"), ("od_w_in", "od_w_out", "mlp_w1_1", "mlp_w2_1"))
LAYER_NORMS = ("ln_mix_g", "ln_mix_b", "ln_mlp_g", "ln_mlp_b")
WEIGHT_NAMES = ("meta_tokens", "ev_w_in", "ev_conv_w", "ev_conv_b", "ev_w_rg_a", "ev_b_rg_a", "ev_w_rg_x", "ev_b_rg_x",
                "ev_lru_lambda", "ev_q_norm_g", "ev_w_uq", "ev_kv_norm_g", "ev_w_ukv", "ev_w_out", "od_w_in", "od_w_out",
                "ln_mix_g", "ln_mix_b", "mlp_w1", "mlp_w2", "ln_mlp_g", "ln_mlp_b")


def _to_rows(flat, row_align):
    n = flat.shape[-1]
    rows = _round_up(-(-n // PACK_COLS), row_align)
    pad = rows * PACK_COLS - n
    if pad:
        flat = jnp.pad(flat, [(0, 0)] * (flat.ndim - 1) + [(0, pad)])
    return flat.reshape(flat.shape[:-1] + (rows, PACK_COLS))


def _shard_shape(K, N, axis):
    return (K // N_CHIPS, N) if axis == 0 else (K, N // N_CHIPS)


def _gather_shards(stacked, K, N, axis):
    if axis == 0:
        return stacked.reshape(K, N)
    return stacked.transpose(1, 0, 2).reshape(K, N)


def _split_shards(full, K, N, axis):
    if axis == 0:
        return full.reshape(N_CHIPS, -1)
    return full.reshape(K, N_CHIPS, N // N_CHIPS).transpose(1, 0, 2).reshape(N_CHIPS, -1)


def kernel(x, meta_tokens, ev_w_in, ev_conv_w, ev_conv_b, ev_w_rg_a, ev_b_rg_a, ev_w_rg_x, ev_b_rg_x, ev_lru_lambda, ev_q_norm_g, ev_w_uq, ev_kv_norm_g, ev_w_ukv, ev_w_out, od_w_in, od_w_out, ln_mix_g, ln_mix_b, mlp_w1, mlp_w2, ln_mlp_g, ln_mlp_b, loss_target, m_meta_tokens, m_ev_w_in, m_ev_conv_w, m_ev_conv_b, m_ev_w_rg_a, m_ev_b_rg_a, m_ev_w_rg_x, m_ev_b_rg_x, m_ev_lru_lambda, m_ev_q_norm_g, m_ev_w_uq, m_ev_kv_norm_g, m_ev_w_ukv, m_ev_w_out, m_od_w_in, m_od_w_out, m_ln_mix_g, m_ln_mix_b, m_mlp_w1, m_mlp_w2, m_ln_mlp_g, m_ln_mlp_b, v_meta_tokens, v_ev_w_in, v_ev_conv_w, v_ev_conv_b, v_ev_w_rg_a, v_ev_b_rg_a, v_ev_w_rg_x, v_ev_b_rg_x, v_ev_lru_lambda, v_ev_q_norm_g, v_ev_w_uq, v_ev_kv_norm_g, v_ev_w_ukv, v_ev_w_out, v_od_w_in, v_od_w_out, v_ln_mix_g, v_ln_mix_b, v_mlp_w1, v_mlp_w2, v_ln_mlp_g, v_ln_mlp_b):
    given = dict(locals())
    local_big = {"ev_w_in": ev_w_in[0], "ev_w_uq": ev_w_uq[0], "ev_w_ukv": ev_w_ukv[0], "ev_w_out": ev_w_out[0],
                 "od_w_in": od_w_in[0], "od_w_out": od_w_out[0], "mlp_w1_0": mlp_w1[0], "mlp_w1_1": mlp_w1[1],
                 "mlp_w2_0": mlp_w2[0], "mlp_w2_1": mlp_w2[1]}

    specs = {spec[0]: spec for spec in BIG_SPECS}
    mixer0_m, mlp0_m, layer1_m = MATRIX_GROUPS

    def shards(names):
        return [local_big[n].astype(MXU_DTYPE) for n in names]

    def whole(stacked, n):
        _, K, N, ax = specs[n]
        return stacked if n in COL_SHARD_GRADS else _gather_shards(stacked, K, N, ax)

    def filled(gathered, own, names):
        return {n: whole(_with_own(g_, o_), n) for n, g_, o_ in zip(names, gathered, own)}

    own_a, own_b, own_c = shards(mixer0_m), shards(mlp0_m), shards(layer1_m)
    small = [meta_tokens, jnp.pad(ev_conv_w[0], ((0, 16 - CONV_WIDTH), (0, 0)))]
    gathered_a = _allgather_chips(own_a + small, "weight_allgather_mixer0")
    pending_b, token1 = _allgather_start(own_b, "weight_allgather_mlp0_start")
    pending_c, token2 = _allgather_start(own_c, "weight_allgather_layer1_start")
    meta_full = _gather_shards(_with_own(gathered_a[-2], small[0]), N_META, D_MODEL, 1)
    conv_full = _gather_shards(_with_own(gathered_a[-1], small[1])[:, :CONV_WIDTH], CONV_WIDTH, LRU_WIDTH, 1)

    def slots(names, dtype):
        return {n: jnp.zeros((N_CHIPS, specs[n][1], specs[n][2] // N_CHIPS) if n in COL_SHARD_GRADS else specs[n][1:3], dtype) for n in names}

    def norms(names, layer):
        return {n: given[n][layer] for n in names}

    def finish_gather(pending, own, after, names, tag):
        landed = _allgather_wait(pending, lax.stop_gradient(after), "weight_allgather_%s_wait" % tag)
        return filled(_sibling_forward(landed, "weight_allgather_%s_forward" % tag), own, names)

    diff_a = {**slots(mixer0_m, MXU_DTYPE), **norms(("ln_mix_g", "ln_mix_b"), 0), **{n: given[n] for n in REPLICATED if n not in LAYER_NORMS},
              "x": x, "meta_tokens": meta_full, "ev_conv_w": conv_full}
    diff_b = {**slots(mlp0_m, MXU_DTYPE), **norms(("ln_mlp_g", "ln_mlp_b"), 0)}
    diff_c = {**slots(layer1_m, MXU_DTYPE), **norms(LAYER_NORMS, 1)}
    w_a = filled(gathered_a[:len(mixer0_m)], own_a, mixer0_m)
    h_a, back_a = jax.vjp(lambda d: _mixer0(d, w_a, token1 + token2), diff_a)
    w_b = finish_gather(pending_b, own_b, h_a, mlp0_m, "mlp0")
    h_b, back_b = jax.vjp(lambda d, hh: _mlp0(d, hh, w_b), diff_b, h_a)
    w_c = finish_gather(pending_c, own_c, h_b, layer1_m, "layer1")
    loss, back_c = jax.vjp(lambda d, hh: _layer1_loss(d, hh, w_c, loss_target), diff_c, h_b)
    loss = lax.psum(loss, ("x", "y", "c"))

    def blocks_of(grad, n):
        _, K, N, ax = specs[n]
        if n in COL_SHARD_GRADS:
            blocks = grad
        elif ax == 0:
            blocks = grad.reshape(N_CHIPS, K // N_CHIPS, N)
        else:
            blocks = grad.reshape(K, N_CHIPS, N // N_CHIPS).transpose(1, 0, 2)
        return blocks.reshape(N_CHIPS, 2, blocks.shape[1] // 2, blocks.shape[2])

    def start_reduce(grads_of, names, tag):
        return _direct_scatter_start([blocks_of(grads_of[n], n) for n in names], "grad_scatter_%s_start" % tag)

    g_c, dh = back_c(jnp.ones((), F32))
    flying_c, token = start_reduce(g_c, layer1_m, "layer1")
    g_b, dh = back_b(dh + token)
    flying_b, token = start_reduce(g_b, mlp0_m, "mlp0")
    (g_a,) = back_a(dh + token)

    g = {**g_a, **g_b, **g_c}
    g.update({n: jnp.stack([(g_b if n in g_b else g_a)[n], g_c[n]]) for n in LAYER_NORMS})
    repl = jnp.concatenate([g[n].reshape(-1) for n in REPLICATED]).reshape(N_CHIPS, -1)
    small = [_split_shards(g["meta_tokens"], N_META, D_MODEL, 1), _split_shards(g["ev_conv_w"], CONV_WIDTH, LRU_WIDTH, 1), repl]
    small = [pc.reshape(N_CHIPS, 2, -1) for pc in small]
    n_small = sum(pc.shape[2] for pc in small)
    small.append(jnp.zeros((N_CHIPS, 2, _round_up(n_small, 32 * PACK_COLS) - n_small), F32))
    p_small = jnp.concatenate(small, axis=2).reshape(N_CHIPS, 2, -1, PACK_COLS)
    flying_a, _ = _direct_scatter_start([blocks_of(g_a[n], n) for n in mixer0_m] + [p_small], "grad_scatter_mixer0_start")
    started = [g_a["x"], flying_a[2][0]]
    ps_c, ts_c = _direct_scatter_wait(flying_c, started, "grad_scatter_layer1_wait")
    ps_b, ts_b = _direct_scatter_wait(flying_b, started, "grad_scatter_mlp0_wait")
    fs_bc = [_sum_direct(p, t, "grad_sum_%d" % i) for i, (p, t) in enumerate(zip(ps_b + ps_c, ts_b + ts_c))]
    red_big = dict(zip(mlp0_m + layer1_m, _sibling_gather(fs_bc, "grad_sibling_gather")))

    grads, delta, new_m, new_v = {}, {}, {}, {}

    def update_big(names):
        done = []
        for name, parts in BIG_PARAMS:
            if parts[0] in names:
                shp = given[name].shape
                two_d = (-1, shp[-1])
                grads[name] = jnp.stack([red_big[part].reshape(shp[1:]) for part in parts])
                d, nm, nv = _adamw(given[name].reshape(two_d), grads[name].reshape(two_d), given["m_" + name].reshape(two_d),
                                   given["v_" + name].reshape(two_d), "adamw_" + name)
                delta[name], new_m[name], new_v[name] = d.reshape(shp), nm.reshape(shp), nv.reshape(shp)
                done.append(nv)
        return done

    updated = update_big(mlp0_m + layer1_m)
    ps_a, ts_a = _direct_scatter_wait(flying_a, updated, "grad_scatter_mixer0_wait")
    fs_a = [_sum_direct(p, t, "grad_sum_mixer0_%d" % i) for i, (p, t) in enumerate(zip(ps_a, ts_a))]
    reduced_a = _sibling_gather(fs_a, "grad_sibling_gather_mixer0")
    red_big.update(zip(mixer0_m, reduced_a))
    red_small = reduced_a[-1].reshape(2, -1)
    update_big(mixer0_m)

    def take(off, sz):
        return jnp.concatenate([red_small[0, off // 2:(off + sz) // 2], red_small[1, off // 2:(off + sz) // 2]])

    off = 0
    for name in SMALL_SHARDED:
        sz = given[name].size
        grads[name] = take(off, sz).reshape(given[name].shape)
        off += sz
    n_repl = repl.shape[1]
    own_repl = _to_rows(take(off, n_repl), 16)
    repl_all = _with_own(_allgather_chips([own_repl], "replicated_allgather")[0], own_repl).reshape(N_CHIPS, -1)[:, :n_repl].reshape(-1)
    off = 0
    for name in REPLICATED:
        sz = given[name].size
        grads[name] = repl_all[off:off + sz].reshape(given[name].shape)
        off += sz

    smalls = SMALL_SHARDED + REPLICATED

    def pack_small(get):
        return _to_rows(jnp.concatenate([get(n).reshape(-1) for n in smalls]), 8)

    outs = _adamw(pack_small(lambda n: given[n]), pack_small(lambda n: grads[n]), pack_small(lambda n: given["m_" + n]),
                  pack_small(lambda n: given["v_" + n]), "adamw_small")
    for res, flat in zip((delta, new_m, new_v), outs):
        flat, off = flat.reshape(-1), 0
        for n in smalls:
            sz = given[n].size
            res[n] = flat[off:off + sz].reshape(given[n].shape)
            off += sz

    return (loss, g_a["x"], *[grads[n] for n in WEIGHT_NAMES], *[delta[n] for n in WEIGHT_NAMES],
            *[new_m[n] for n in WEIGHT_NAMES], *[new_v[n] for n in WEIGHT_NAMES])
```

```python
import functools
import math

import jax
import jax.numpy as jnp
from jax import lax
from jax.experimental import pallas as pl
from jax.experimental.pallas import tpu as pltpu

F32 = jnp.float32
MXU_DTYPE = jnp.bfloat16

D_MODEL = 1024
N_META = 16
LRU_WIDTH = 512
LRU_HEADS = 4
LRU_HEAD_DIM = 128
CONV_WIDTH = 4
LRU_C = 8.0
MLA_HEADS = 8
MLA_NOPE = 64
MLA_ROPE = 32
MLA_V = 64
MLA_Q_RANK = 256
MLA_KV_RANK = 128
RET_HEADS = 4
RET_QK_DIM = 256
RET_V_DIM = 512
D_FF = 4096
ROPE_BASE = 10000.0
DN_ALPHA = 4.0 ** 0.25
EPS = 1e-5
NEG_INF = -1e30
SEQ_BLOCK = 128

ADAM_LR = 0.001
ADAM_B1 = 0.9
ADAM_B2 = 0.999
ADAM_EPS = 1e-08
ADAM_WD = 0.01
ADAM_STEP = 10

PACK_COLS = 1024
TN_INPUT_VMEM_BYTES = 28 << 20
N_CHIPS = 4

MESH = pl.DeviceIdType.MESH


def _pick(n, target, align):
    best = None
    for t in range(align, min(n, target) + 1, align):
        if n % t == 0:
            best = t
    return n if best is None else best


def _round_up(n, m):
    return (n + m - 1) // m * m


def _relu2(a):
    r = jnp.maximum(a, 0.0)
    return r * r


def _ln_stats(z):
    mu = jnp.mean(z, axis=-1, keepdims=True)
    zc = z - mu
    var = jnp.mean(zc * zc, axis=-1, keepdims=True)
    return zc, lax.rsqrt(var + EPS)


def _mm_nn(a, w, act, name, out_dtype=F32, norm=None):
    M, K = a.shape
    sharded = w.ndim == 3
    n = w.shape[-1]
    N = n * (w.shape[0] if sharded else 1)
    tm = _pick(M, 1088 if K * a.dtype.itemsize <= 4096 and norm is None else 544, 8)
    tn = _pick(n, 1024, 128)
    per = n // tn
    assert norm is None or tn == N

    def body(a_ref, w_ref, *rest):
        av = a_ref[...]
        if act:
            av = _relu2(av.astype(F32))
        r = jnp.dot(av.astype(MXU_DTYPE), w_ref[...].astype(MXU_DTYPE), preferred_element_type=F32)
        if norm is None:
            rest[0][...] = r.astype(out_dtype)
        else:
            r_ref, g_ref, b_ref, o_ref, z_ref = rest
            z = DN_ALPHA * r_ref[...] + r
            zc, rstd = _ln_stats(z)
            z_ref[...] = z
            o_ref[...] = zc * rstd * g_ref[...] + b_ref[...]

    w_spec = pl.BlockSpec((None, K, tn), lambda i, j: (j // per, 0, j % per)) if sharded else pl.BlockSpec((K, tn), lambda i, j: (0, j))
    tile = pl.BlockSpec((tm, tn), lambda i, j: (i, j))
    in_specs, args = [pl.BlockSpec((tm, K), lambda i, j: (i, 0)), w_spec], [a, w]
    if norm is None:
        out_specs, out_shape = tile, jax.ShapeDtypeStruct((M, N), out_dtype)
    else:
        vec = pl.BlockSpec((1, N), lambda i, j: (0, 0))
        in_specs += [tile, vec, vec]
        args += [norm[0], norm[1].reshape(1, N), norm[2].reshape(1, N)]
        out_specs, out_shape = [tile, tile], [jax.ShapeDtypeStruct((M, N), F32)] * 2
    return pl.pallas_call(
        body, name=name, grid=(M // tm, N // tn), in_specs=in_specs, out_specs=out_specs, out_shape=out_shape,
        compiler_params=pltpu.CompilerParams(dimension_semantics=("parallel", "arbitrary")),
    )(*args)


def _mm_nt(g, w, a_src, name, out_dtype=F32, plus=None):
    M, N = g.shape
    sharded = w.ndim == 3
    K, n = w.shape[-2], w.shape[-1]
    if sharded:
        tk, nk = N, 1
    else:
        tk = N if N * g.dtype.itemsize <= 8192 else _pick(N, 2048, 128)
        nk = N // tk
    tm = _pick(M, 1088 if tk * g.dtype.itemsize <= 4096 else 544, 8)
    tn = _pick(K, 1024, 128)
    has_src = a_src is not None
    assert nk == 1 or out_dtype == F32
    assert plus is None or not has_src

    def body(*refs):
        if has_src:
            g_ref, w_ref, s_ref, o_ref = refs
        elif plus is not None:
            g_ref, w_ref, p_ref, o_ref = refs
        else:
            g_ref, w_ref, o_ref = refs
        nt = (((1,), (1,)), ((), ()))
        if sharded:
            r = sum(lax.dot_general(g_ref[:, s * n:(s + 1) * n].astype(MXU_DTYPE), w_ref[s].astype(MXU_DTYPE), nt, preferred_element_type=F32)
                    for s in range(w_ref.shape[0]))
        else:
            r = lax.dot_general(g_ref[...].astype(MXU_DTYPE), w_ref[...].astype(MXU_DTYPE), nt, preferred_element_type=F32)
        if has_src:
            r = r * (2.0 * jnp.maximum(s_ref[...].astype(F32), 0.0))
        first = r if plus is None else r + DN_ALPHA * p_ref[...]
        if nk == 1:
            o_ref[...] = first.astype(out_dtype)
        else:
            k = pl.program_id(2)

            @pl.when(k == 0)
            def _():
                o_ref[...] = first

            @pl.when(k > 0)
            def _():
                o_ref[...] += r

    w_spec = (pl.BlockSpec((w.shape[0], tn, n), lambda i, j, k: (0, j, 0)) if sharded
              else pl.BlockSpec((tn, tk), lambda i, j, k: (j, k)))
    in_specs = [pl.BlockSpec((tm, tk), lambda i, j, k: (i, k)), w_spec]
    args = [g, w]
    if has_src:
        assert nk == 1
        in_specs.append(pl.BlockSpec((tm, tn), lambda i, j, k: (i, j)))
        args.append(a_src)
    if plus is not None:
        in_specs.append(pl.BlockSpec((tm, tn), lambda i, j, k: (i, j)))
        args.append(plus)
    return pl.pallas_call(
        body, name=name,
        grid=(M // tm, K // tn, nk),
        in_specs=in_specs,
        out_specs=pl.BlockSpec((tm, tn), lambda i, j, k: (i, j)),
        out_shape=jax.ShapeDtypeStruct((M, K), out_dtype),
        compiler_params=pltpu.CompilerParams(dimension_semantics=("parallel", "parallel", "arbitrary")),
    )(*args)


def _mm_tn(a, g, act, name, col_shards=1, out_dtype=F32):
    M, K = a.shape
    _, N = g.shape
    n = N // col_shards
    tm, tn = _pick(K, 1024, 128), _pick(n, 1024, 128)
    row_bytes = tm * a.dtype.itemsize + tn * g.dtype.itemsize
    tk = _pick(M, min(2176, TN_INPUT_VMEM_BYTES // (2 * row_bytes)), 8)
    nk = M // tk
    per = n // tn
    direct = out_dtype == F32

    def body(a_ref, g_ref, o_ref, *scratch):
        acc_ref = o_ref if direct else scratch[0]
        k = pl.program_id(2)
        av = a_ref[...]
        if act:
            av = _relu2(av.astype(F32))
        r = lax.dot_general(av.astype(MXU_DTYPE), g_ref[...].astype(MXU_DTYPE),
                            (((0,), (0,)), ((), ())), preferred_element_type=F32)

        @pl.when(k == 0)
        def _():
            acc_ref[...] = r

        @pl.when(k > 0)
        def _():
            acc_ref[...] += r

        if not direct:
            @pl.when(k == nk - 1)
            def _():
                o_ref[...] = acc_ref[...].astype(out_dtype)

    if col_shards == 1:
        out_spec, out_shape = pl.BlockSpec((tm, tn), lambda i, j, k: (i, j)), (K, N)
    else:
        out_spec, out_shape = pl.BlockSpec((None, tm, tn), lambda i, j, k: (j // per, i, j % per)), (col_shards, K, n)
    return pl.pallas_call(
        body, name=name,
        grid=(K // tm, N // tn, nk),
        in_specs=[pl.BlockSpec((tk, tm), lambda i, j, k: (k, i)), pl.BlockSpec((tk, tn), lambda i, j, k: (k, j))],
        out_specs=out_spec,
        out_shape=jax.ShapeDtypeStruct(out_shape, out_dtype),
        scratch_shapes=[] if direct else [pltpu.VMEM((tm, tn), F32)],
        compiler_params=pltpu.CompilerParams(dimension_semantics=("parallel", "parallel", "arbitrary")),
    )(a, g)


@functools.partial(jax.custom_vjp, nondiff_argnums=(3, 4, 5, 6))
def matmul(a, w, w_grad_slot, act, name, out_dtype, col_shards):
    return _mm_nn(a, w, act, name + "_fwd", out_dtype)


def _matmul_fwd(a, w, w_grad_slot, act, name, out_dtype, col_shards):
    return _mm_nn(a, w, act, name + "_fwd", out_dtype), (a, w, jnp.zeros((), w_grad_slot.dtype))


def _matmul_bwd(act, name, out_dtype, col_shards, res, g):
    a, w, slot_like = res
    w_grad_dtype = slot_like.dtype
    da = _mm_nt(g, w, a if act else None, name + "_dx")
    dw = _mm_tn(a, g, act, name + "_dw", col_shards, w_grad_dtype)
    return da, None, dw


matmul.defvjp(_matmul_fwd, _matmul_bwd)


def _ln_bwd_call(z, g, dy, name):
    M, D = z.shape
    tm = _pick(M, 544, 8)

    def body(z_ref, g_ref, dy_ref, dz_ref, dg_ref, db_ref):
        @pl.when(pl.program_id(0) == 0)
        def _():
            dg_ref[...] = jnp.zeros_like(dg_ref)
            db_ref[...] = jnp.zeros_like(db_ref)

        zc, rstd = _ln_stats(z_ref[...])
        xhat = zc * rstd
        dy = dy_ref[...]
        dxh = dy * g_ref[...]
        m1 = jnp.mean(dxh, axis=-1, keepdims=True)
        m2 = jnp.mean(dxh * xhat, axis=-1, keepdims=True)
        dz_ref[...] = rstd * (dxh - m1 - xhat * m2)
        dg_ref[...] += jnp.sum(dy * xhat, axis=0, keepdims=True)
        db_ref[...] += jnp.sum(dy, axis=0, keepdims=True)

    row = pl.BlockSpec((tm, D), lambda i: (i, 0))
    vec = pl.BlockSpec((1, D), lambda i: (0, 0))
    return pl.pallas_call(
        body, name=name, grid=(M // tm,), in_specs=[row, vec, row], out_specs=[row, vec, vec],
        out_shape=[jax.ShapeDtypeStruct((M, D), F32), jax.ShapeDtypeStruct((1, D), F32), jax.ShapeDtypeStruct((1, D), F32)],
        compiler_params=pltpu.CompilerParams(dimension_semantics=("arbitrary",)),
    )(z, g.reshape(1, D), dy)


@functools.partial(jax.custom_vjp, nondiff_argnums=(7,))
def mlp_block(h, w1, w2, w1_grad_slot, w2_grad_slot, g, b, name):
    return _mlp_block_fwd(h, w1, w2, w1_grad_slot, w2_grad_slot, g, b, name)[0]


def _mlp_block_fwd(h, w1, w2, w1_grad_slot, w2_grad_slot, g, b, name):
    u = _mm_nn(h, w1, False, name + "_w1_fwd", out_dtype=MXU_DTYPE)
    out, z = _mm_nn(u, w2, True, name + "_w2_norm_fwd", norm=(h, g, b))
    return out, (h, u, z, w1, w2, g, jnp.zeros((), w1_grad_slot.dtype))


def _mlp_block_bwd(name, res, dy):
    h, u, z, w1, w2, g, slot_like = res
    dz, dg, db = _ln_bwd_call(z, g, dy, name + "_norm_bwd")
    du = _mm_nt(dz, w2, u, name + "_w2_dx", out_dtype=MXU_DTYPE)
    dw2 = _mm_tn(u, dz, True, name + "_w2_dw", 1, slot_like.dtype)
    dh = _mm_nt(du, w1, None, name + "_w1_dx", plus=dz)
    dw1 = _mm_tn(h, du, False, name + "_w1_dw", N_CHIPS, slot_like.dtype)
    return dh, None, None, dw1, dw2, dg.reshape(g.shape), db.reshape(g.shape)


mlp_block.defvjp(_mlp_block_fwd, _mlp_block_bwd)


@functools.partial(jax.custom_vjp, nondiff_argnums=(6,))
def out_block(h, y, w, w_grad_slot, g, b, name):
    return _out_block_fwd(h, y, w, w_grad_slot, g, b, name)[0]


def _out_block_fwd(h, y, w, w_grad_slot, g, b, name):
    out, z = _mm_nn(y, w, False, name + "_norm_fwd", norm=(h, g, b))
    return out, (y, z, w, g, jnp.zeros((), w_grad_slot.dtype))


def _out_block_bwd(name, res, dy):
    y, z, w, g, slot_like = res
    dz, dg, db = _ln_bwd_call(z, g, dy, name + "_norm_bwd")
    d_y = _mm_nt(dz, w, None, name + "_dx")
    dw = _mm_tn(y, dz, False, name + "_dw", 1, slot_like.dtype)
    return DN_ALPHA * dz, d_y, None, dw, dg.reshape(g.shape), db.reshape(g.shape)


out_block.defvjp(_out_block_fwd, _out_block_bwd)


def _rms_fwd_call(x, g, name, col_block=0):
    R = x.shape[0]
    W = g.shape[-1]
    tr = _pick(R, 1088, 8)

    def body(x_ref, g_ref, o_ref):
        xv = x_ref[...]
        rstd = lax.rsqrt(jnp.mean(xv * xv, axis=-1, keepdims=True) + EPS)
        o_ref[...] = xv * rstd * g_ref[...]

    vec = pl.BlockSpec((1, W), lambda i: (0, 0))
    return pl.pallas_call(
        body, name=name, grid=(R // tr,), in_specs=[pl.BlockSpec((tr, W), lambda i: (i, col_block)), vec],
        out_specs=pl.BlockSpec((tr, W), lambda i: (i, 0)), out_shape=jax.ShapeDtypeStruct((R, W), F32),
        compiler_params=pltpu.CompilerParams(dimension_semantics=("parallel",)),
    )(x, g.reshape(1, W))


def _rms_bwd_call(x, g, dy, name, col_block=0):
    R = x.shape[0]
    W = g.shape[-1]
    tr = _pick(R, 1088, 8)

    def body(x_ref, g_ref, dy_ref, dx_ref, dg_ref):
        @pl.when(pl.program_id(0) == 0)
        def _():
            dg_ref[...] = jnp.zeros_like(dg_ref)

        xv = x_ref[...]
        rstd = lax.rsqrt(jnp.mean(xv * xv, axis=-1, keepdims=True) + EPS)
        xhat = xv * rstd
        dy = dy_ref[...]
        dxh = dy * g_ref[...]
        dx_ref[...] = rstd * (dxh - xhat * jnp.mean(dxh * xhat, axis=-1, keepdims=True))
        dg_ref[...] += jnp.sum(dy * xhat, axis=0, keepdims=True)

    row = pl.BlockSpec((tr, W), lambda i: (i, 0))
    vec = pl.BlockSpec((1, W), lambda i: (0, 0))
    return pl.pallas_call(
        body, name=name, grid=(R // tr,), in_specs=[pl.BlockSpec((tr, W), lambda i: (i, col_block)), vec, row], out_specs=[row, vec],
        out_shape=[jax.ShapeDtypeStruct((R, W), F32), jax.ShapeDtypeStruct((1, W), F32)],
        compiler_params=pltpu.CompilerParams(dimension_semantics=("arbitrary",)),
    )(x, g.reshape(1, W), dy)


def _loss_call(h, tgt, n_tokens, name):
    B, Tp, D = h.shape
    tr = _pick(Tp, 544, 8)

    def body(y_ref, t_ref, dy_ref, acc_ref):
        @pl.when(jnp.logical_and(pl.program_id(0) == 0, pl.program_id(1) == 0))
        def _():
            acc_ref[...] = jnp.zeros_like(acc_ref)

        t = lax.broadcasted_iota(jnp.int32, (tr, 1), 0) + pl.program_id(1) * tr
        counts = jnp.logical_and(t >= N_META, t < N_META + n_tokens)
        e = jnp.where(counts, y_ref[...] - t_ref[...], 0.0)
        dy_ref[...] = e * (1.0 / D)
        acc_ref[...] += jnp.sum(jnp.sum(e * e, axis=-1, keepdims=True), axis=0, keepdims=True) * (0.5 / D)

    row = pl.BlockSpec((None, tr, D), lambda b, i: (b, i, 0))
    one = pl.BlockSpec((1, 1), lambda b, i: (0, 0))
    return pl.pallas_call(
        body, name=name, grid=(B, Tp // tr), in_specs=[row, row], out_specs=[row, one],
        out_shape=[jax.ShapeDtypeStruct((B, Tp, D), F32), jax.ShapeDtypeStruct((1, 1), F32)],
        compiler_params=pltpu.CompilerParams(dimension_semantics=("arbitrary", "arbitrary")),
    )(h, tgt)


@functools.partial(jax.custom_vjp, nondiff_argnums=(2,))
def loss_head(h, tgt, n_tokens):
    return _loss_call(h, tgt, n_tokens, "loss_head")[1][0, 0]


def _loss_head_fwd(h, tgt, n_tokens):
    dy, acc = _loss_call(h, tgt, n_tokens, "loss_head")
    return acc[0, 0], dy


def _loss_head_bwd(n_tokens, dy, ct):
    return ct * dy, None


loss_head.defvjp(_loss_head_fwd, _loss_head_bwd)


_GELU_C = math.sqrt(2.0 / math.pi)


def _gelu_parts(x):
    x2 = x * x
    t = jnp.tanh(_GELU_C * (x + 0.044715 * x * x2))
    gelu = 0.5 * x * (1.0 + t)
    dgelu = 0.5 * (1.0 + t) + 0.5 * x * (1.0 - t * t) * (_GELU_C * (1.0 + 3.0 * 0.044715 * x2))
    return gelu, dgelu


def _sigmoid(x):
    return 1.0 / (1.0 + jnp.exp(-x))


def _scan8(a, b, carry, reverse):
    row = lax.broadcasted_iota(jnp.int32, a.shape, 0)
    for s in (1, 2, 4):
        shift = 8 - s if reverse else s
        keep = (row < 8 - s) if reverse else (row >= s)
        b = jnp.where(keep, a * pltpu.roll(b, shift, 0) + b, b)
        a = jnp.where(keep, a * pltpu.roll(a, shift, 0), a)
    return a * carry + b


def _lru_pre(prec_ref, prev_ref, first, cw_ref, cb_ref, wa_ref, ba_ref, wx_ref, bx_ref, sp_ref):
    tc = prec_ref.shape[0]
    prev = jnp.where(first, 0.0, prev_ref[...])
    ext = jnp.concatenate([prev, prec_ref[...]], axis=0)
    cw = cw_ref[...]
    taps = [ext[8:] if k == CONV_WIDTH - 1 else pltpu.roll(ext, CONV_WIDTH - 1 - k, 0)[8:] for k in range(CONV_WIDTH)]
    xc = cb_ref[...] + sum(cw[k:k + 1, :] * taps[k] for k in range(CONV_WIDTH))
    ga, gx = [], []
    for h in range(LRU_HEADS):
        xh = xc[:, h * LRU_HEAD_DIM:(h + 1) * LRU_HEAD_DIM].astype(MXU_DTYPE)
        ga.append(jnp.dot(xh, wa_ref[h].astype(MXU_DTYPE), preferred_element_type=F32))
        gx.append(jnp.dot(xh, wx_ref[h].astype(MXU_DTYPE), preferred_element_type=F32))
    r = _sigmoid(jnp.concatenate(ga, axis=1) + ba_ref[...])
    i = _sigmoid(jnp.concatenate(gx, axis=1) + bx_ref[...])
    log_a = -LRU_C * r * sp_ref[...]
    a = jnp.exp(log_a)
    a2 = a * a
    mult = jnp.sqrt(-jnp.tanh(log_a) * (a2 + 1.0))
    return taps, xc, r, i, a, a2, mult


def _lru_fwd_call(p, cw, cb, wa, ba, wx, bx, sp):
    B, Tp, _ = p.shape
    W = LRU_WIDTH
    tc = SEQ_BLOCK
    nc = Tp // tc

    def body(pg_ref, prec_ref, prev_ref, cw_ref, cb_ref, wa_ref, ba_ref, wx_ref, bx_ref, sp_ref, y_ref, h_ref, carry_ref):
        first = pl.program_id(1) == 0

        @pl.when(first)
        def _():
            carry_ref[...] = jnp.zeros_like(carry_ref)

        _, xc, r, i, a, a2, mult = _lru_pre(prec_ref, prev_ref, first, cw_ref, cb_ref, wa_ref, ba_ref, wx_ref, bx_ref, sp_ref)
        b = mult * (i * xc)
        carry = carry_ref[0:1, :]
        for t in range(tc // 8):
            h = _scan8(a[8 * t:8 * t + 8], b[8 * t:8 * t + 8], carry, False)
            h_ref[8 * t:8 * t + 8, :] = h
            carry = h[7:8, :]
        carry_ref[...] = jnp.broadcast_to(carry, carry_ref.shape)
        y_ref[...] = h_ref[...] * _gelu_parts(pg_ref[...])[0]

    cur = pl.BlockSpec((None, tc, W), lambda b, j: (b, j, 0))
    rec = pl.BlockSpec((None, tc, W), lambda b, j: (b, j, 1))
    prev = pl.BlockSpec((None, 8, W), lambda b, j: (b, jnp.maximum(j * (tc // 8) - 1, 0), 1))
    vec = pl.BlockSpec((1, W), lambda b, j: (0, 0))
    cws = pl.BlockSpec((CONV_WIDTH, W), lambda b, j: (0, 0))
    wsp = pl.BlockSpec((LRU_HEADS, LRU_HEAD_DIM, LRU_HEAD_DIM), lambda b, j: (0, 0, 0))
    return pl.pallas_call(
        body, name="lru_fwd", grid=(B, nc),
        in_specs=[cur, rec, prev, cws, vec, wsp, vec, wsp, vec, vec],
        out_specs=[cur, cur],
        out_shape=[jax.ShapeDtypeStruct((B, Tp, W), F32), jax.ShapeDtypeStruct((B, Tp, W), F32)],
        scratch_shapes=[pltpu.VMEM((8, W), F32)],
        compiler_params=pltpu.CompilerParams(dimension_semantics=("arbitrary", "arbitrary")),
    )(p, p, p, cw, cb, wa, ba, wx, bx, sp)


def _lru_bwd_call(p, hseq, dy, cw, cb, wa, ba, wx, bx, sp, dpq, dpkv, dkpe):
    B, Tp, P = p.shape
    W = LRU_WIDTH
    tc = SEQ_BLOCK
    nc = Tp // tc
    HD = LRU_HEAD_DIM

    def body(pg_ref, prec_ref, prev_ref, h_ref, hprev_ref, dy_ref, cw_ref, cb_ref, wa_ref, ba_ref, wx_ref, bx_ref, sp_ref,
             dpq_ref, dpkv_ref, dkpe_ref, dp_ref, dcw_ref, dcb_ref, dwa_ref, dba_ref, dwx_ref, dbx_ref, dsp_ref,
             gcar_ref, anext_ref, halo_ref, g_ref):
        j = pl.program_id(1)
        first = j == nc - 1
        last = j == 0

        @pl.when(jnp.logical_and(pl.program_id(0) == 0, last))
        def _():
            for ref in (dcw_ref, dcb_ref, dwa_ref, dba_ref, dwx_ref, dbx_ref, dsp_ref):
                ref[...] = jnp.zeros_like(ref)

        @pl.when(last)
        def _():
            gcar_ref[...] = jnp.zeros_like(gcar_ref)
            anext_ref[...] = jnp.zeros_like(anext_ref)
            halo_ref[...] = jnp.zeros_like(halo_ref)

        taps, xc, r, i, a, a2, mult = _lru_pre(prec_ref, prev_ref, first, cw_ref, cb_ref, wa_ref, ba_ref, wx_ref, bx_ref, sp_ref)
        row = lax.broadcasted_iota(jnp.int32, (tc, W), 0)
        gelu, dgelu = _gelu_parts(pg_ref[...])
        dy = dy_ref[...]
        hcur = h_ref[...]
        dp_ref[:, 0:W] = dy * hcur * dgelu
        dp_ref[:, 2 * W:2 * W + MLA_Q_RANK] = dpq_ref[...]
        dp_ref[:, _KPE_START - MLA_KV_RANK:_KPE_START] = dpkv_ref[...]
        dp_ref[:, _KPE_START:P] = pltpu.roll(dkpe_ref[...], HEAD_LANES - MLA_NOPE, 1)[:, 0:P - _KPE_START]
        dh = dy * gelu
        a_next = jnp.where(row == tc - 1, anext_ref[0:1, :], pltpu.roll(a, tc - 1, 0))
        carry = gcar_ref[0:1, :]
        for t in reversed(range(tc // 8)):
            g = _scan8(a_next[8 * t:8 * t + 8], dh[8 * t:8 * t + 8], carry, True)
            g_ref[8 * t:8 * t + 8, :] = g
            carry = g[0:1, :]
        gcar_ref[...] = jnp.broadcast_to(carry, gcar_ref.shape)
        anext_ref[...] = jnp.broadcast_to(a[0:1, :], anext_ref.shape)
        G = g_ref[...]
        h_before = jnp.where(first, 0.0, hprev_ref[7:8, :])
        hprev = jnp.where(row == 0, h_before, pltpu.roll(hcur, 1, 0))
        d_a = G * hprev
        gx_ = G * xc
        d_mult = gx_ * i
        d_i = gx_ * mult
        dxc = G * (mult * i)
        d_la = d_a * a - d_mult * (a2 / mult)
        sp = sp_ref[...]
        d_r = d_la * (-LRU_C * sp)
        dsp_ref[...] += jnp.sum(d_la * (-LRU_C * r), axis=0, keepdims=True)
        dga = d_r * r * (1.0 - r)
        dgx = d_i * i * (1.0 - i)
        dba_ref[...] += jnp.sum(dga, axis=0, keepdims=True)
        dbx_ref[...] += jnp.sum(dgx, axis=0, keepdims=True)
        back = []
        for h in range(LRU_HEADS):
            sl = slice(h * HD, (h + 1) * HD)
            xh = xc[:, sl].astype(MXU_DTYPE)
            ah = dga[:, sl].astype(MXU_DTYPE)
            bh = dgx[:, sl].astype(MXU_DTYPE)
            tn = (((0,), (0,)), ((), ()))
            nt = (((1,), (1,)), ((), ()))
            dwa_ref[h] += lax.dot_general(xh, ah, tn, preferred_element_type=F32)
            dwx_ref[h] += lax.dot_general(xh, bh, tn, preferred_element_type=F32)
            back.append(lax.dot_general(ah, wa_ref[h].astype(MXU_DTYPE), nt, preferred_element_type=F32)
                        + lax.dot_general(bh, wx_ref[h].astype(MXU_DTYPE), nt, preferred_element_type=F32))
        dxc = dxc + jnp.concatenate(back, axis=1)
        dcb_ref[...] += jnp.sum(dxc, axis=0, keepdims=True)
        for k in range(CONV_WIDTH):
            dcw_ref[k:k + 1, :] += jnp.sum(dxc * taps[k], axis=0, keepdims=True)
        ext = jnp.concatenate([dxc, halo_ref[...]], axis=0)
        cw = cw_ref[...]
        acc = cw[CONV_WIDTH - 1:CONV_WIDTH, :] * dxc
        for k in range(CONV_WIDTH - 1):
            s = CONV_WIDTH - 1 - k
            acc = acc + cw[k:k + 1, :] * pltpu.roll(ext, tc + 8 - s, 0)[:tc]
        dp_ref[:, W:2 * W] = acc
        halo_ref[...] = dxc[0:8, :]

    rev = lambda j: nc - 1 - j
    cur = pl.BlockSpec((None, tc, W), lambda b, j: (b, rev(j), 0))
    rec = pl.BlockSpec((None, tc, W), lambda b, j: (b, rev(j), 1))
    prev = pl.BlockSpec((None, 8, W), lambda b, j: (b, jnp.maximum(rev(j) * (tc // 8) - 1, 0), 0))
    prev_rec = pl.BlockSpec((None, 8, W), lambda b, j: (b, jnp.maximum(rev(j) * (tc // 8) - 1, 0), 1))
    vec = pl.BlockSpec((1, W), lambda b, j: (0, 0))
    cws = pl.BlockSpec((CONV_WIDTH, W), lambda b, j: (0, 0))
    wsp = pl.BlockSpec((LRU_HEADS, HD, HD), lambda b, j: (0, 0, 0))
    vs = jax.ShapeDtypeStruct((1, W), F32)
    ws = jax.ShapeDtypeStruct((LRU_HEADS, HD, HD), F32)

    def rows(width):
        return pl.BlockSpec((None, tc, width), lambda b, j: (b, rev(j), 0))

    return pl.pallas_call(
        body, name="lru_bwd", grid=(B, nc),
        in_specs=[cur, rec, prev_rec, cur, prev, cur, cws, vec, wsp, vec, wsp, vec, vec, rows(MLA_Q_RANK), rows(MLA_KV_RANK), rows(HEAD_LANES)],
        out_specs=[rows(P), cws, vec, wsp, vec, wsp, vec, vec],
        out_shape=[jax.ShapeDtypeStruct((B, Tp, P), F32), jax.ShapeDtypeStruct((CONV_WIDTH, W), F32), vs, ws, vs, ws, vs, vs],
        scratch_shapes=[pltpu.VMEM((8, W), F32), pltpu.VMEM((8, W), F32), pltpu.VMEM((8, W), F32), pltpu.VMEM((tc, W), F32)],
        compiler_params=pltpu.CompilerParams(dimension_semantics=("arbitrary", "arbitrary")),
    )(p, p, p, hseq, hseq, dy, cw, cb, wa, ba, wx, bx, sp, dpq, dpkv, dkpe)


_Q_BLOCK = 2 * LRU_WIDTH // MLA_Q_RANK
_KV_BLOCK = (2 * LRU_WIDTH + MLA_Q_RANK) // MLA_KV_RANK
_KPE_START = 2 * LRU_WIDTH + MLA_Q_RANK + MLA_KV_RANK


@jax.custom_vjp
def even_front(p, cw, cb, wa, ba, wx, bx, sp, gq, gkv):
    return _even_front_fwd(p, cw, cb, wa, ba, wx, bx, sp, gq, gkv)[0]


def _even_front_fwd(p, cw, cb, wa, ba, wx, bx, sp, gq, gkv):
    B, Tp, W = p.shape
    p2d = p.reshape(B * Tp, W)
    y, hseq = _lru_fwd_call(p, cw, cb, wa, ba, wx, bx, sp)
    qn = _rms_fwd_call(p2d, gq, "q_norm_fwd", _Q_BLOCK)
    kvn = _rms_fwd_call(p2d, gkv, "kv_norm_fwd", _KV_BLOCK)
    kpe = jnp.pad(p[:, :, _KPE_START:], ((0, 0), (0, 0), (MLA_NOPE, HEAD_LANES - MLA_NOPE - MLA_ROPE)))
    return (y, qn, kvn, kpe), (p, hseq, cw, cb, wa, ba, wx, bx, sp, gq, gkv)


def _even_front_bwd(res, cts):
    p, hseq, cw, cb, wa, ba, wx, bx, sp, gq, gkv = res
    dy, dqn, dkvn, dkpe = cts
    B, Tp, W = p.shape
    p2d = p.reshape(B * Tp, W)
    dpq, dgq = _rms_bwd_call(p2d, gq, dqn, "q_norm_bwd", _Q_BLOCK)
    dpkv, dgkv = _rms_bwd_call(p2d, gkv, dkvn, "kv_norm_bwd", _KV_BLOCK)
    dp, dcw, dcb, dwa, dba, dwx, dbx, dsp = _lru_bwd_call(p, hseq, dy, cw, cb, wa, ba, wx, bx, sp, dpq.reshape(B, Tp, -1),
                                                          dpkv.reshape(B, Tp, -1), dkpe)
    return dp, dcw, dcb, dwa, dba, dwx, dbx, dsp, dgq.reshape(gq.shape), dgkv.reshape(gkv.shape)


even_front.defvjp(_even_front_fwd, _even_front_bwd)


def _rope_tables(pos, half):
    inv = ROPE_BASE ** (-jnp.arange(half, dtype=F32) / half)
    ang = pos.astype(F32)[:, None] * inv[None, :]
    return jnp.cos(ang), jnp.sin(ang)


_NT = (((1,), (1,)), ((), ()))
_TN = (((0,), (0,)), ((), ()))
HEAD_LANES = 128
_MLA_SCALE = (MLA_NOPE + MLA_ROPE) ** -0.5
_LOG2E = math.log2(math.e)


Q_BLOCK = 512


def _query_blocks(Tp):
    first = Tp % Q_BLOCK or Q_BLOCK
    return [(0, first)] + [(r, r + Q_BLOCK) for r in range(first, Tp, Q_BLOCK)]


def _mask_diagonal(s, fill):
    R, L = s.shape
    row = lax.broadcasted_iota(jnp.int32, (R, R), 0)
    col = lax.broadcasted_iota(jnp.int32, (R, R), 1)
    last = jnp.where(col <= row, s[:, L - R:], fill)
    return last if L == R else jnp.concatenate([s[:, :L - R], last], axis=1)


def _mla_rope_tables(pos):
    half = MLA_ROPE // 2
    cos, sin = _rope_tables(pos, half)
    T = pos.shape[0]
    ones, zeros = jnp.ones((T, MLA_NOPE), F32), jnp.zeros((T, MLA_NOPE), F32)
    tail1, tail0 = jnp.ones((T, HEAD_LANES - MLA_NOPE - MLA_ROPE), F32), jnp.zeros((T, HEAD_LANES - MLA_NOPE - MLA_ROPE), F32)
    zh = jnp.zeros((T, half), F32)
    c = jnp.concatenate([ones, cos, cos, tail1], axis=1)
    s_up = jnp.concatenate([zeros, -sin, zh, tail0], axis=1)
    s_down = jnp.concatenate([zeros, zh, sin, tail0], axis=1)
    return c, s_up, s_down


def _rope_lanes(x, c, s_up, s_down):
    half = MLA_ROPE // 2
    return x * c + pltpu.roll(x, HEAD_LANES - half, 1) * s_up + pltpu.roll(x, half, 1) * s_down


def _unrope_lanes(d, c, s_up, s_down):
    half = MLA_ROPE // 2
    return d * c + pltpu.roll(d * s_up, half, 1) + pltpu.roll(d * s_down, HEAD_LANES - half, 1)


def _mla_operands(q_ref, kv_ref, kpe_ref, c, s_up, s_down):
    lane = lax.broadcasted_iota(jnp.int32, kv_ref.shape, 1)
    qr = (_rope_lanes(q_ref[...].astype(F32), c, s_up, s_down) * (_MLA_SCALE * _LOG2E)).astype(MXU_DTYPE)
    kr = jnp.where(lane < MLA_NOPE, kv_ref[...].astype(F32), _rope_lanes(kpe_ref[...], c, s_up, s_down)).astype(MXU_DTYPE)
    return qr, kr, lane


def _mla_specs(Tp):
    head = pl.BlockSpec((None, Tp, HEAD_LANES), lambda b, h: (b, 0, h))
    shared = pl.BlockSpec((None, Tp, HEAD_LANES), lambda b, h: (b, 0, 0))
    tab = pl.BlockSpec((Tp, HEAD_LANES), lambda b, h: (0, 0))
    lse = pl.BlockSpec((None, None, Tp, 1), lambda b, h: (b, h, 0, 0))
    return head, shared, tab, lse


def _attn_fwd_call(q, kv, kpe, tabs):
    B, Tp, _ = q.shape

    def body(q_ref, kv_ref, kpe_ref, c_ref, su_ref, sd_ref, o_ref, lse_ref, qr_ref, kr_ref):
        qr, kr, lane = _mla_operands(q_ref, kv_ref, kpe_ref, c_ref[...], su_ref[...], sd_ref[...])
        qr_ref[...] = qr
        kr_ref[...] = kr
        for r0, L in _query_blocks(Tp):
            blk = slice(r0, L)
            s = _mask_diagonal(lax.dot_general(qr_ref[blk, :], kr_ref[0:L, :], _NT, preferred_element_type=F32), NEG_INF)
            m = jnp.max(s, axis=-1, keepdims=True)
            p = jnp.exp2(s - m)
            l = jnp.sum(p, axis=-1, keepdims=True)
            o = jnp.dot(p.astype(MXU_DTYPE), kv_ref[0:L, :].astype(MXU_DTYPE), preferred_element_type=F32)
            o_ref[blk, :] = jnp.where(lane[blk, :] >= MLA_NOPE, o / l, 0.0)
            lse_ref[blk, :] = m + jnp.log2(l)

    head, shared, tab, lse = _mla_specs(Tp)
    return pl.pallas_call(
        body, name="mla_attn_fwd", grid=(B, MLA_HEADS), in_specs=[head, head, shared, tab, tab, tab], out_specs=[head, lse],
        out_shape=[jax.ShapeDtypeStruct((B, Tp, MLA_HEADS * HEAD_LANES), F32), jax.ShapeDtypeStruct((B, MLA_HEADS, Tp, 1), F32)],
        scratch_shapes=[pltpu.VMEM((Tp, HEAD_LANES), MXU_DTYPE), pltpu.VMEM((Tp, HEAD_LANES), MXU_DTYPE)],
        compiler_params=pltpu.CompilerParams(dimension_semantics=("parallel", "parallel")),
    )(q, kv, kpe, *tabs)


def _attn_bwd_call(q, kv, kpe, tabs, o, lse, do):
    B, Tp, _ = q.shape

    def body(q_ref, kv_ref, kpe_ref, c_ref, su_ref, sd_ref, o_ref, lse_ref, do_ref, dq_ref, dkv_ref, dkpe_ref,
             qr_ref, kr_ref, dqa_ref, dka_ref, dva_ref):
        c, s_up, s_down = c_ref[...], su_ref[...], sd_ref[...]
        qr, kr, lane = _mla_operands(q_ref, kv_ref, kpe_ref, c, s_up, s_down)
        qr_ref[...] = qr
        kr_ref[...] = kr
        dka_ref[...] = jnp.zeros_like(dka_ref)
        dva_ref[...] = jnp.zeros_like(dva_ref)
        for r0, L in _query_blocks(Tp):
            blk = slice(r0, L)
            qb = qr_ref[blk, :]
            do = jnp.where(lane[blk, :] >= MLA_NOPE, do_ref[blk, :], 0.0)
            delta = jnp.sum(do * o_ref[blk, :], axis=-1, keepdims=True)
            s = _mask_diagonal(lax.dot_general(qb, kr_ref[0:L, :], _NT, preferred_element_type=F32), NEG_INF)
            p = jnp.exp2(s - lse_ref[blk, :])
            dob = do.astype(MXU_DTYPE)
            dva_ref[0:L, :] += lax.dot_general(p.astype(MXU_DTYPE), dob, _TN, preferred_element_type=F32)
            dp = lax.dot_general(dob, kv_ref[0:L, :].astype(MXU_DTYPE), _NT, preferred_element_type=F32)
            ds = (p * (dp - delta)).astype(MXU_DTYPE)
            dqa_ref[blk, :] = jnp.dot(ds, kr_ref[0:L, :], preferred_element_type=F32)
            dka_ref[0:L, :] += lax.dot_general(ds, qb, _TN, preferred_element_type=F32)
        dq_ref[...] = _unrope_lanes(dqa_ref[...] * _MLA_SCALE, c, s_up, s_down).astype(dq_ref.dtype)
        dk = dka_ref[...] * (1.0 / _LOG2E)
        dkv_ref[...] = jnp.where(lane < MLA_NOPE, dk, dva_ref[...]).astype(dkv_ref.dtype)
        dkpe = jnp.where(lane >= MLA_NOPE, _unrope_lanes(dk, c, s_up, s_down), 0.0)

        @pl.when(pl.program_id(1) == 0)
        def _():
            dkpe_ref[...] = dkpe

        @pl.when(pl.program_id(1) > 0)
        def _():
            dkpe_ref[...] += dkpe

    head, shared, tab, lse_spec = _mla_specs(Tp)
    wide = jax.ShapeDtypeStruct((B, Tp, MLA_HEADS * HEAD_LANES), q.dtype)
    acc = pltpu.VMEM((Tp, HEAD_LANES), F32)
    return pl.pallas_call(
        body, name="mla_attn_bwd", grid=(B, MLA_HEADS),
        in_specs=[head, head, shared, tab, tab, tab, head, lse_spec, head], out_specs=[head, head, shared],
        out_shape=[wide, wide, jax.ShapeDtypeStruct((B, Tp, HEAD_LANES), F32)],
        scratch_shapes=[pltpu.VMEM((Tp, HEAD_LANES), MXU_DTYPE), pltpu.VMEM((Tp, HEAD_LANES), MXU_DTYPE), acc, acc, acc],
        compiler_params=pltpu.CompilerParams(dimension_semantics=("parallel", "arbitrary")),
    )(q, kv, kpe, *tabs, o, lse, do)


@jax.custom_vjp
def mla_attention(q, kv, kpe, tabs):
    return _attn_fwd_call(q, kv, kpe, tabs)[0]


def _mla_attention_fwd(q, kv, kpe, tabs):
    o, lse = _attn_fwd_call(q, kv, kpe, tabs)
    return o, (q, kv, kpe, tabs, o, lse)


def _mla_attention_bwd(res, do):
    q, kv, kpe, tabs, o, lse = res
    dq, dkv, dkpe = _attn_bwd_call(q, kv, kpe, tabs, o, lse, do)
    return dq, dkv, dkpe, None


mla_attention.defvjp(_mla_attention_fwd, _mla_attention_bwd)


def _rope_halves(x, cos, sin):
    half = x.shape[1] // 2
    x1, x2 = x[:, :half], x[:, half:]
    return jnp.concatenate([x1 * cos - x2 * sin, x1 * sin + x2 * cos], axis=1)


def _unrope_halves(d, cos, sin):
    half = d.shape[1] // 2
    d1, d2 = d[:, :half], d[:, half:]
    return jnp.concatenate([d1 * cos + d2 * sin, d2 * cos - d1 * sin], axis=1)


_RET_K_SCALE = RET_QK_DIM ** -0.5
_RET_Q_BLOCKS = RET_HEADS
_RET_V_BLOCK0 = 2 * RET_HEADS * RET_QK_DIM // RET_V_DIM
_RET_G_BLOCK0 = _RET_V_BLOCK0 + RET_HEADS


def _ret_specs(Tp):
    q = pl.BlockSpec((None, Tp, RET_QK_DIM), lambda b, h: (b, 0, h))
    k = pl.BlockSpec((None, Tp, RET_QK_DIM), lambda b, h: (b, 0, _RET_Q_BLOCKS + h))
    v = pl.BlockSpec((None, Tp, RET_V_DIM), lambda b, h: (b, 0, _RET_V_BLOCK0 + h))
    tab = pl.BlockSpec((Tp, RET_QK_DIM // 2), lambda b, h: (0, 0))
    lg = pl.BlockSpec((None, 1, 1), lambda b, h: (h, 0, 0))
    return q, k, v, tab, lg


def _ret_operands(q_ref, k_ref, cos, sin, lg):
    t = lax.broadcasted_iota(jnp.int32, (q_ref.shape[0], 1), 0).astype(F32)
    grow, shrink = jnp.exp(-lg * t), jnp.exp(lg * t)
    qs = (_rope_halves(q_ref[...].astype(F32), cos, sin) * shrink).astype(MXU_DTYPE)
    ks = (_rope_halves(k_ref[...].astype(F32), cos, sin) * (grow * _RET_K_SCALE)).astype(MXU_DTYPE)
    return qs, ks, shrink, grow * _RET_K_SCALE


def _ret_core_fwd_call(p, cos, sin, lg):
    B, Tp, _ = p.shape

    def body(q_ref, k_ref, v_ref, cos_ref, sin_ref, lg_ref, o_ref, qs_ref, ks_ref):
        qs_ref[...], ks_ref[...], _, _ = _ret_operands(q_ref, k_ref, cos_ref[...], sin_ref[...], lg_ref[...])
        for r0, L in _query_blocks(Tp):
            blk = slice(r0, L)
            s = _mask_diagonal(lax.dot_general(qs_ref[blk, :], ks_ref[0:L, :], _NT, preferred_element_type=F32), 0.0)
            o_ref[blk, :] = jnp.dot(s.astype(MXU_DTYPE), v_ref[0:L, :].astype(MXU_DTYPE), preferred_element_type=F32)

    q, k, v, tab, lgs = _ret_specs(Tp)
    return pl.pallas_call(
        body, name="retention_fwd", grid=(B, RET_HEADS), in_specs=[q, k, v, tab, tab, lgs],
        out_specs=pl.BlockSpec((None, Tp, RET_V_DIM), lambda b, h: (b, 0, h)),
        out_shape=jax.ShapeDtypeStruct((B, Tp, RET_HEADS * RET_V_DIM), F32),
        scratch_shapes=[pltpu.VMEM((Tp, RET_QK_DIM), MXU_DTYPE), pltpu.VMEM((Tp, RET_QK_DIM), MXU_DTYPE)],
        compiler_params=pltpu.CompilerParams(dimension_semantics=("parallel", "parallel")),
    )(p, p, p, cos, sin, lg)


def _ret_core_bwd_call(p, do, cos, sin, lg):
    B, Tp, _ = p.shape

    def body(q_ref, k_ref, v_ref, do_ref, cos_ref, sin_ref, lg_ref, dq_ref, dk_ref, dv_ref, qs_ref, ks_ref, dqa_ref, dka_ref, dva_ref):
        cos_, sin_ = cos_ref[...], sin_ref[...]
        qs_ref[...], ks_ref[...], q_scale, k_scale = _ret_operands(q_ref, k_ref, cos_, sin_, lg_ref[...])
        dka_ref[...] = jnp.zeros_like(dka_ref)
        dva_ref[...] = jnp.zeros_like(dva_ref)
        for r0, L in _query_blocks(Tp):
            blk = slice(r0, L)
            qb = qs_ref[blk, :]
            dob = do_ref[blk, :].astype(MXU_DTYPE)
            s = _mask_diagonal(lax.dot_general(qb, ks_ref[0:L, :], _NT, preferred_element_type=F32), 0.0).astype(MXU_DTYPE)
            dva_ref[0:L, :] += lax.dot_general(s, dob, _TN, preferred_element_type=F32)
            ds = _mask_diagonal(lax.dot_general(dob, v_ref[0:L, :].astype(MXU_DTYPE), _NT, preferred_element_type=F32), 0.0).astype(MXU_DTYPE)
            dqa_ref[blk, :] = jnp.dot(ds, ks_ref[0:L, :], preferred_element_type=F32)
            dka_ref[0:L, :] += lax.dot_general(ds, qb, _TN, preferred_element_type=F32)
        dq_ref[...] = _unrope_halves(dqa_ref[...] * q_scale, cos_, sin_).astype(dq_ref.dtype)
        dk_ref[...] = _unrope_halves(dka_ref[...] * k_scale, cos_, sin_).astype(dk_ref.dtype)
        dv_ref[...] = dva_ref[...].astype(dv_ref.dtype)

    q, k, v, tab, lgs = _ret_specs(Tp)
    qk_out = pl.BlockSpec((None, Tp, RET_QK_DIM), lambda b, h: (b, 0, h))
    v_out = pl.BlockSpec((None, Tp, RET_V_DIM), lambda b, h: (b, 0, h))
    return pl.pallas_call(
        body, name="retention_bwd", grid=(B, RET_HEADS), in_specs=[q, k, v, v_out, tab, tab, lgs],
        out_specs=[qk_out, qk_out, v_out],
        out_shape=[jax.ShapeDtypeStruct((B, Tp, RET_HEADS * RET_QK_DIM), p.dtype), jax.ShapeDtypeStruct((B, Tp, RET_HEADS * RET_QK_DIM), p.dtype),
                   jax.ShapeDtypeStruct((B, Tp, RET_HEADS * RET_V_DIM), p.dtype)],
        scratch_shapes=[pltpu.VMEM((Tp, RET_QK_DIM), MXU_DTYPE), pltpu.VMEM((Tp, RET_QK_DIM), MXU_DTYPE),
                        pltpu.VMEM((Tp, RET_QK_DIM), F32), pltpu.VMEM((Tp, RET_QK_DIM), F32), pltpu.VMEM((Tp, RET_V_DIM), F32)],
        compiler_params=pltpu.CompilerParams(dimension_semantics=("parallel", "parallel")),
    )(p, p, p, do, cos, sin, lg)


def _ret_gate_specs(M):
    tm = _pick(M, 1088, 8)
    head = pl.BlockSpec((tm, RET_V_DIM), lambda i, h: (i, h))
    gate = pl.BlockSpec((tm, RET_V_DIM), lambda i, h: (i, _RET_G_BLOCK0 + h))
    return tm, head, gate


def _ret_gate_fwd_call(o, p2d):
    M = o.shape[0]
    tm, head, gate = _ret_gate_specs(M)

    def body(o_ref, g_ref, y_ref):
        ov = o_ref[...]
        gv = g_ref[...].astype(F32)
        rstd = lax.rsqrt(jnp.mean(ov * ov, axis=-1, keepdims=True) + EPS)
        y_ref[...] = (gv * _sigmoid(gv)) * (ov * rstd)

    return pl.pallas_call(
        body, name="retention_gate_fwd", grid=(M // tm, RET_HEADS), in_specs=[head, gate], out_specs=head,
        out_shape=jax.ShapeDtypeStruct(o.shape, F32),
        compiler_params=pltpu.CompilerParams(dimension_semantics=("parallel", "parallel")),
    )(o, p2d)


def _ret_gate_bwd_call(o, p2d, dy):
    M = o.shape[0]
    tm, head, gate = _ret_gate_specs(M)

    def body(o_ref, g_ref, dy_ref, do_ref, dg_ref):
        ov = o_ref[...]
        gv = g_ref[...].astype(F32)
        dy = dy_ref[...]
        rstd = lax.rsqrt(jnp.mean(ov * ov, axis=-1, keepdims=True) + EPS)
        on = ov * rstd
        sg = _sigmoid(gv)
        dg_ref[...] = (dy * on * (sg * (1.0 + gv * (1.0 - sg)))).astype(dg_ref.dtype)
        don = dy * (gv * sg)
        do_ref[...] = (rstd * (don - on * jnp.mean(don * on, axis=-1, keepdims=True))).astype(do_ref.dtype)

    shp = jax.ShapeDtypeStruct(o.shape, p2d.dtype)
    return pl.pallas_call(
        body, name="retention_gate_bwd", grid=(M // tm, RET_HEADS), in_specs=[head, gate, head], out_specs=[head, head],
        out_shape=[shp, shp],
        compiler_params=pltpu.CompilerParams(dimension_semantics=("parallel", "parallel")),
    )(o, p2d, dy)


def _log_gamma():
    return jnp.log(1.0 - 2.0 ** (-5.0 - jnp.arange(RET_HEADS, dtype=F32))).reshape(RET_HEADS, 1, 1)


@functools.partial(jax.custom_vjp, nondiff_argnums=(9,))
def retention_block(h, w_in, w_out, w_in_grad_slot, w_out_grad_slot, g, b, cos, sin, dims):
    return _retention_block_fwd(h, w_in, w_out, w_in_grad_slot, w_out_grad_slot, g, b, cos, sin, dims)[0]


def _retention_block_fwd(h, w_in, w_out, w_in_grad_slot, w_out_grad_slot, g, b, cos, sin, dims):
    B, Tp = dims
    p = _mm_nn(h, w_in, False, "od_w_in_fwd", out_dtype=MXU_DTYPE)
    o = _ret_core_fwd_call(p.reshape(B, Tp, -1), cos, sin, _log_gamma())
    y = _ret_gate_fwd_call(o.reshape(B * Tp, -1), p)
    out, z = _mm_nn(y, w_out, False, "od_w_out_norm_fwd", norm=(h, g, b))
    return out, (h, p, o, y, z, w_in, w_out, g, cos, sin, jnp.zeros((), w_in_grad_slot.dtype))


def _retention_block_bwd(dims, res, dout):
    B, Tp = dims
    h, p, o, y, z, w_in, w_out, g, cos, sin, slot_like = res
    dz, dg, db = _ln_bwd_call(z, g, dout, "od_w_out_norm_bwd")
    dy = _mm_nt(dz, w_out, None, "od_w_out_dx")
    dw_out = _mm_tn(y, dz, False, "od_w_out_dw", 1, slot_like.dtype)
    do, dgate = _ret_gate_bwd_call(o.reshape(B * Tp, -1), p, dy)
    dq, dk, dv = _ret_core_bwd_call(p.reshape(B, Tp, -1), do.reshape(B, Tp, -1), cos, sin, _log_gamma())
    dp = jnp.concatenate([dq.reshape(B * Tp, -1), dk.reshape(B * Tp, -1), dv.reshape(B * Tp, -1), dgate], axis=-1)
    dh = _mm_nt(dp, w_in, None, "od_w_in_dx", plus=dz)
    dw_in = _mm_tn(h, dp, False, "od_w_in_dw", N_CHIPS, slot_like.dtype)
    return dh, None, None, dw_in, dw_out, dg.reshape(g.shape), db.reshape(g.shape), None, None


retention_block.defvjp(_retention_block_fwd, _retention_block_bwd)


def _heads_to_lanes(w):
    K = w.shape[0]
    w = w.reshape(K, MLA_HEADS, MLA_NOPE + MLA_ROPE)
    return jnp.pad(w, ((0, 0), (0, 0), (0, HEAD_LANES - MLA_NOPE - MLA_ROPE))).reshape(K, MLA_HEADS * HEAD_LANES)


def _out_rows_to_lanes(w):
    N = w.shape[1]
    att = w[LRU_WIDTH:].reshape(MLA_HEADS, MLA_V, N)
    att = jnp.pad(att, ((0, 0), (HEAD_LANES - MLA_V, 0), (0, 0))).reshape(MLA_HEADS * HEAD_LANES, N)
    return jnp.concatenate([w[:LRU_WIDTH], att], axis=0)


def _seq_dims(x):
    B, S, D = x.shape
    T = S + N_META
    Tp = _round_up(T, SEQ_BLOCK)
    return B, S, T, Tp


def _mixer0(diff, w, token):
    x = diff["x"]
    B, S, T, Tp = _seq_dims(x)
    D = x.shape[-1]
    M = B * Tp
    pos = jnp.arange(Tp, dtype=jnp.int32)

    def mm(a, name, act=False, out_dtype=F32, layout=lambda m: m, col_shards=1):
        return matmul(a, layout(w[name]), layout(diff[name]), act, name, out_dtype, col_shards)

    meta = jnp.broadcast_to((diff["meta_tokens"] + token)[None], (B, N_META, D))
    h = jnp.concatenate([meta, x, jnp.zeros((B, Tp - T, D), F32)], axis=1).reshape(M, D)
    p = mm(h, "ev_w_in")
    sp = jax.nn.softplus(-diff["ev_lru_lambda"]).reshape(1, LRU_WIDTH)
    y_rec, qn, kvn, kpe = even_front(
        p.reshape(B, Tp, -1), diff["ev_conv_w"].reshape(CONV_WIDTH, LRU_WIDTH), diff["ev_conv_b"].reshape(1, LRU_WIDTH),
        diff["ev_w_rg_a"].reshape(LRU_HEADS, LRU_HEAD_DIM, LRU_HEAD_DIM), diff["ev_b_rg_a"].reshape(1, LRU_WIDTH),
        diff["ev_w_rg_x"].reshape(LRU_HEADS, LRU_HEAD_DIM, LRU_HEAD_DIM), diff["ev_b_rg_x"].reshape(1, LRU_WIDTH),
        sp, diff["ev_q_norm_g"].reshape(-1), diff["ev_kv_norm_g"].reshape(-1))
    y_rec = y_rec.reshape(M, LRU_WIDTH)
    q = mm(qn, "ev_w_uq", out_dtype=MXU_DTYPE, layout=_heads_to_lanes).reshape(B, Tp, -1)
    kv = mm(kvn, "ev_w_ukv", out_dtype=MXU_DTYPE).reshape(B, Tp, -1)
    y_att = mla_attention(q, kv, kpe, _mla_rope_tables(pos)).reshape(M, -1)
    return out_block(h, jnp.concatenate([y_rec, y_att], axis=-1), _out_rows_to_lanes(w["ev_w_out"]), _out_rows_to_lanes(diff["ev_w_out"]),
                     diff["ln_mix_g"], diff["ln_mix_b"], "ev_w_out")


def _mlp0(diff, h, w):
    return mlp_block(h, w["mlp_w1_0"], w["mlp_w2_0"], diff["mlp_w1_0"], diff["mlp_w2_0"], diff["ln_mlp_g"], diff["ln_mlp_b"], "mlp0")


def _layer1_loss(diff, h, w, tgt):
    B, S, T, Tp = _seq_dims(tgt)
    D = tgt.shape[-1]
    pos = jnp.arange(Tp, dtype=jnp.int32)

    cos, sin = _rope_tables(pos, RET_QK_DIM // 2)
    h = retention_block(h, w["od_w_in"], w["od_w_out"], diff["od_w_in"], diff["od_w_out"], diff["ln_mix_g"], diff["ln_mix_b"], cos, sin, (B, Tp))
    h = mlp_block(h, w["mlp_w1_1"], w["mlp_w2_1"], diff["mlp_w1_1"], diff["mlp_w2_1"], diff["ln_mlp_g"], diff["ln_mlp_b"], "mlp1")
    return loss_head(h.reshape(B, Tp, D), jnp.pad(tgt, ((0, 0), (N_META, Tp - T), (0, 0))), S)


_HBM = pl.BlockSpec(memory_space=pltpu.HBM)


def _place():
    return lax.axis_index("x"), lax.axis_index("y"), lax.axis_index("c")


def _other_chips(x, y):
    return [(1 - x, y), (x, 1 - y), (1 - x, 1 - y)]


def _chunks(rows, sublanes, most):
    for q in range(most, 0, -1):
        if rows % (q * sublanes) == 0:
            return q
    return 1


def _sublanes(dtype):
    return 8 * 4 // jnp.dtype(dtype).itemsize


def _gather_pieces(bufs):
    plan, first = [], []
    for b in bufs:
        Rh = b.shape[0] // 2
        Q = _chunks(Rh, _sublanes(b.dtype), 4) if Rh * b.shape[1] * b.dtype.itemsize > (1 << 20) else 1
        first.append(3 * sum(q for _, q, _ in plan))
        plan.append((Rh, Q, Rh // Q))
    return plan, first, 3 * sum(q for _, q, _ in plan)


def _allgather_chips(bufs, name):
    n = len(bufs)
    plan, first, n_sems = _gather_pieces(bufs)

    def body(*refs):
        x_refs, out_refs, (send_sems, recv_sems) = refs[:n], refs[n:2 * n], refs[2 * n:]
        x, y, c = _place()
        sibling = (x, y, 1 - c)
        chips = _other_chips(x, y)

        def copy(k, src, dst, to):
            return pltpu.make_async_remote_copy(src_ref=src, dst_ref=dst, send_sem=send_sems.at[k], recv_sem=recv_sems.at[k],
                                                device_id=to, device_id_type=MESH)

        def piece(i, cx, cy, hc, q):
            Rh, _, ch = plan[i]
            return out_refs[i].at[2 * cx + cy, pl.ds(hc * Rh + q * ch, ch), :]

        slots = [(i, q, j) for i in range(n) for q in range(plan[i][1]) for j in range(3)]
        sem = {(i, q, j): first[i] + 3 * q + j for i, q, j in slots}
        sent = [copy(sem[i, q, j], x_refs[i].at[pl.ds(c * plan[i][0] + q * plan[i][2], plan[i][2]), :], piece(i, x, y, c, q), (*chips[j], c))
                for i, q, j in slots]
        for cp in sent:
            cp.start()
        passed = []
        for i, q, j in slots:
            landed = piece(i, *chips[j], c, q)
            copy(sem[i, q, j], landed, landed, sibling).wait_recv()
            fwd = copy(n_sems + sem[i, q, j], landed, landed, sibling)
            fwd.start()
            passed.append(fwd)
        for i, q, j in slots:
            theirs = piece(i, *chips[j], 1 - c, q)
            copy(n_sems + sem[i, q, j], theirs, theirs, sibling).wait_recv()
        for cp in sent + passed:
            cp.wait_send()

    return pl.pallas_call(
        body, name=name, in_specs=[_HBM] * n, out_specs=[_HBM] * n,
        out_shape=[jax.ShapeDtypeStruct((N_CHIPS,) + b.shape, b.dtype) for b in bufs],
        scratch_shapes=[pltpu.SemaphoreType.DMA((2 * n_sems,)), pltpu.SemaphoreType.DMA((2 * n_sems,))],
    )(*bufs)


def _with_own(gathered, own):
    my = 2 * lax.axis_index("x") + lax.axis_index("y")
    return lax.dynamic_update_slice(gathered, own[None], (my, 0, 0))


def _sibling_gather(fs, name):
    n = len(fs)

    def body(*refs):
        out_refs, (send_sems, recv_sems) = refs[n:2 * n], refs[2 * n:]
        x, y, c = _place()
        copies = [pltpu.make_async_remote_copy(src_ref=out_ref.at[c], dst_ref=out_ref.at[c], send_sem=send_sems.at[i], recv_sem=recv_sems.at[i],
                                               device_id=(x, y, 1 - c), device_id_type=MESH) for i, out_ref in enumerate(out_refs)]
        for cp in copies:
            cp.start()
        for cp in copies:
            cp.wait()

    return pl.pallas_call(
        body, name=name, in_specs=[_HBM] * n, out_specs=[_HBM] * n,
        out_shape=[jax.ShapeDtypeStruct(f.shape, f.dtype) for f in fs], input_output_aliases={i: i for i in range(n)},
        scratch_shapes=[pltpu.SemaphoreType.DMA((n,)), pltpu.SemaphoreType.DMA((n,))],
    )(*fs)


def _axis_scalar(name):
    return lax.axis_index(name).astype(jnp.int32).reshape(1)


_SEM = pl.BlockSpec(memory_space=pltpu.SEMAPHORE)
_ANY = pl.BlockSpec(memory_space=pl.ANY)
_EFFECT = pltpu.SideEffectType.DATAFLOW_SIDE_EFFECTING


def _in_hbm(a):
    return pltpu.with_memory_space_constraint(a, pltpu.HBM)


def _half_copies(x_refs, land_refs, send_sems, recv_sems, arriving):
    x, y, c = _place()
    copies = []
    for i, (x_ref, land_ref) in enumerate(zip(x_refs, land_refs)):
        Rh = x_ref.shape[0] // 2
        rows = pl.ds(c * Rh, Rh)
        for j, (cx, cy) in enumerate(_other_chips(x, y)):
            copies.append(pltpu.make_async_remote_copy(
                src_ref=x_ref.at[rows, :], dst_ref=land_ref.at[2 * cx + cy if arriving else 2 * x + y, rows, :],
                send_sem=send_sems.at[3 * i + j], recv_sem=recv_sems.at[3 * i + j], device_id=(cx, cy, c), device_id_type=MESH))
    return copies


def _allgather_start(bufs, name):
    n = len(bufs)

    def body(*refs):
        x_refs, land_refs, (send_sems, recv_sems), token = refs[:n], refs[n:2 * n], refs[2 * n:2 * n + 2], refs[-1]
        for cp in _half_copies(x_refs, land_refs, send_sems, recv_sems, False):
            cp.start()
        token[...] = jnp.zeros_like(token)

    lands = [lax.empty((N_CHIPS,) + b.shape, b.dtype) for b in bufs]
    out = pl.pallas_call(
        body, name=name,
        out_shape=(pltpu.SemaphoreType.DMA((3 * n,)), pltpu.SemaphoreType.DMA((3 * n,)), *[pltpu.HBM(a.shape, a.dtype) for a in bufs + lands],
                   jax.ShapeDtypeStruct((8, 128), F32)),
        in_specs=[_HBM] * (2 * n), out_specs=(_SEM, _SEM, *[_HBM] * (2 * n), pl.BlockSpec(memory_space=pltpu.VMEM)),
        input_output_aliases={i: 2 + i for i in range(2 * n)}, compiler_params=pltpu.CompilerParams(has_side_effects=_EFFECT),
    )(*[_in_hbm(a) for a in bufs + lands])
    return (out[0], out[1], list(out[2:2 + n]), list(out[2 + n:2 + 2 * n])), out[-1][0, 0]


def _allgather_wait(pending, after, name):
    send_sems, recv_sems, bufs, lands = pending
    n = len(bufs)

    def body(*refs):
        x_refs, land_refs, send_sems, recv_sems = refs[:n], refs[n:2 * n], refs[2 * n], refs[2 * n + 1]
        for cp in _half_copies(x_refs, land_refs, send_sems, recv_sems, False):
            cp.wait_send()
        for cp in _half_copies(x_refs, land_refs, send_sems, recv_sems, True):
            cp.wait_recv()

    out = pl.pallas_call(
        body, name=name, out_shape=tuple(pltpu.HBM(a.shape, a.dtype) for a in bufs + lands),
        in_specs=[_HBM] * (2 * n) + [_SEM, _SEM, _ANY], out_specs=tuple([_HBM] * (2 * n)), input_output_aliases={i: i for i in range(2 * n)},
        compiler_params=pltpu.CompilerParams(has_side_effects=_EFFECT),
    )(*bufs, *lands, send_sems, recv_sems, after)
    return list(out[n:])


def _sibling_forward(lands, name):
    n = len(lands)
    plan, first, n_sems = _gather_pieces([jax.ShapeDtypeStruct(l.shape[1:], l.dtype) for l in lands])

    def body(*refs):
        out_refs, (send_sems, recv_sems) = refs[n:2 * n], refs[2 * n:]
        x, y, c = _place()

        def copies(hc):
            return [pltpu.make_async_remote_copy(
                        src_ref=out_refs[i].at[2 * cx + cy, pl.ds(hc * plan[i][0] + q * plan[i][2], plan[i][2]), :],
                        dst_ref=out_refs[i].at[2 * cx + cy, pl.ds(hc * plan[i][0] + q * plan[i][2], plan[i][2]), :],
                        send_sem=send_sems.at[first[i] + 3 * q + j], recv_sem=recv_sems.at[first[i] + 3 * q + j],
                        device_id=(x, y, 1 - c), device_id_type=MESH)
                    for i in range(n) for q in range(plan[i][1]) for j, (cx, cy) in enumerate(_other_chips(x, y))]

        mine = copies(c)
        for cp in mine:
            cp.start()
        for cp in mine:
            cp.wait_send()
        for cp in copies(1 - c):
            cp.wait_recv()

    return pl.pallas_call(
        body, name=name, in_specs=[_HBM] * n, out_specs=[_HBM] * n, out_shape=[jax.ShapeDtypeStruct(l.shape, l.dtype) for l in lands],
        input_output_aliases={i: i for i in range(n)},
        scratch_shapes=[pltpu.SemaphoreType.DMA((n_sems,)), pltpu.SemaphoreType.DMA((n_sems,))],
    )(*lands)


N_PEERS = 7


def _direct_copies(p_refs, t_refs, send_sems, recv_sems):
    x, y, c = _place()
    copies = []
    for i, (p_ref, t_ref) in enumerate(zip(p_refs, t_refs)):
        for f in range(1, N_PEERS + 1):
            px, py, pc = x ^ (f >> 2), y ^ ((f >> 1) & 1), c ^ (f & 1)
            copies.append(pltpu.make_async_remote_copy(
                src_ref=p_ref.at[2 * px + py, pc], dst_ref=t_ref.at[f - 1], send_sem=send_sems.at[N_PEERS * i + f - 1],
                recv_sem=recv_sems.at[N_PEERS * i + f - 1], device_id=(px, py, pc), device_id_type=MESH))
    return copies


def _direct_scatter_start(ps, name, carried=()):
    n, m = len(ps), 2 * len(ps) + len(carried)

    def body(*refs):
        p_refs, t_refs, (send_sems, recv_sems) = refs[:n], refs[n:2 * n], refs[m:m + 2]
        for cp in _direct_copies(p_refs, t_refs, send_sems, recv_sems):
            cp.start()

    lands = [lax.empty((N_PEERS,) + p.shape[2:], p.dtype) for p in ps]
    through = ps + lands + list(carried)
    out = pl.pallas_call(
        body, name=name,
        out_shape=(pltpu.SemaphoreType.DMA((N_PEERS * n,)), pltpu.SemaphoreType.DMA((N_PEERS * n,)),
                   *[pltpu.HBM(a.shape, a.dtype) for a in through]),
        in_specs=[_HBM] * m, out_specs=(_SEM, _SEM, *[_HBM] * m),
        input_output_aliases={i: 2 + i for i in range(m)}, compiler_params=pltpu.CompilerParams(has_side_effects=_EFFECT),
    )(*[_in_hbm(a) for a in through])
    return (out[0], out[1], list(out[2:2 + n]), list(out[2 + n:2 + 2 * n])), list(out[2 + 2 * n:])


def _direct_scatter_wait(pending, after, name):
    send_sems, recv_sems, ps, lands = pending
    n = len(ps)

    def body(*refs):
        p_refs, t_refs, send_sems, recv_sems = refs[:n], refs[n:2 * n], refs[2 * n], refs[2 * n + 1]
        for cp in _direct_copies(p_refs, t_refs, send_sems, recv_sems):
            cp.wait_send()
            cp.wait_recv()

    out = pl.pallas_call(
        body, name=name, out_shape=tuple(pltpu.HBM(a.shape, a.dtype) for a in ps + lands),
        in_specs=[_HBM] * (2 * n) + [_SEM, _SEM] + [_ANY] * len(after), out_specs=tuple([_HBM] * (2 * n)),
        input_output_aliases={i: i for i in range(2 * n)}, compiler_params=pltpu.CompilerParams(has_side_effects=_EFFECT),
    )(*ps, *lands, send_sems, recv_sems, *after)
    return list(out[:n]), list(out[n:])


def _sum_direct(p, t, name):
    _, _, R, C = p.shape
    tr = _pick(R, 512, 16)

    def body(x_ref, y_ref, c_ref, p_ref, t_ref, o_ref):
        acc = p_ref[...].astype(F32)
        for f in range(N_PEERS):
            acc = acc + t_ref[f].astype(F32)
        o_ref[...] = acc

    grid_spec = pltpu.PrefetchScalarGridSpec(
        num_scalar_prefetch=3, grid=(R // tr,),
        in_specs=[pl.BlockSpec((None, None, tr, C), lambda i, x_ref, y_ref, c_ref: (2 * x_ref[0] + y_ref[0], c_ref[0], i, 0)),
                  pl.BlockSpec((N_PEERS, tr, C), lambda i, x_ref, y_ref, c_ref: (0, i, 0))],
        out_specs=pl.BlockSpec((None, tr, C), lambda i, x_ref, y_ref, c_ref: (c_ref[0], i, 0)))
    return pl.pallas_call(body, name=name, grid_spec=grid_spec, out_shape=jax.ShapeDtypeStruct((2, R, C), F32),
                          compiler_params=pltpu.CompilerParams(dimension_semantics=("parallel",)))(
        _axis_scalar("x"), _axis_scalar("y"), _axis_scalar("c"), p, t)


def _adamw(w, g, m, v, name):
    R, C = w.shape
    tr = _pick(R, 256, 8)

    def body(w_ref, g_ref, m_ref, v_ref, d_ref, nm_ref, nv_ref):
        g_ = g_ref[...]
        m_ = ADAM_B1 * m_ref[...] + (1.0 - ADAM_B1) * g_
        v_ = ADAM_B2 * v_ref[...] + (1.0 - ADAM_B2) * (g_ * g_)
        m_hat = m_ / (1.0 - ADAM_B1 ** ADAM_STEP)
        v_hat = v_ / (1.0 - ADAM_B2 ** ADAM_STEP)
        d_ref[...] = -ADAM_LR * (m_hat / (jnp.sqrt(v_hat) + ADAM_EPS) + ADAM_WD * w_ref[...])
        nm_ref[...] = m_
        nv_ref[...] = v_

    row = pl.BlockSpec((tr, C), lambda i: (i, 0))
    shp = jax.ShapeDtypeStruct((R, C), F32)
    return pl.pallas_call(body, name=name, grid=(R // tr,), in_specs=[row] * 4, out_specs=[row] * 3, out_shape=[shp] * 3,
                          compiler_params=pltpu.CompilerParams(dimension_semantics=("parallel",)))(w, g, m, v)


BIG_SPECS = (("ev_w_in", 1024, 1440, 1), ("ev_w_uq", 256, 768, 1), ("ev_w_ukv", 128, 1024, 1), ("ev_w_out", 1024, 1024, 0),
             ("od_w_in", 1024, 6144, 1), ("od_w_out", 2048, 1024, 0), ("mlp_w1_0", 1024, 4096, 1), ("mlp_w1_1", 1024, 4096, 1),
             ("mlp_w2_0", 4096, 1024, 0), ("mlp_w2_1", 4096, 1024, 0))
BIG_PARAMS = (("ev_w_in", ("ev_w_in",)), ("ev_w_uq", ("ev_w_uq",)), ("ev_w_ukv", ("ev_w_ukv",)), ("ev_w_out", ("ev_w_out",)),
              ("od_w_in", ("od_w_in",)), ("od_w_out", ("od_w_out",)), ("mlp_w1", ("mlp_w1_0", "mlp_w1_1")),
              ("mlp_w2", ("mlp_w2_0", "mlp_w2_1")))
REPLICATED = ("ev_conv_b", "ev_w_rg_a", "ev_b_rg_a", "ev_w_rg_x", "ev_b_rg_x", "ev_lru_lambda", "ev_q_norm_g", "ev_kv_norm_g",
              "ln_mix_g", "ln_mix_b", "ln_mlp_g", "ln_mlp_b")
SMALL_SHARDED = ("meta_tokens", "ev_conv_w")
COL_SHARD_GRADS = ("od_w_in", "mlp_w1_0", "mlp_w1_1")
MATRIX_GROUPS = (("ev_w_in", "ev_w_uq", "ev_w_ukv", "ev_w_out"), ("mlp_w1_0", "mlp_w2_0"), ("od_w_in", "od_w_out", "mlp_w1_1", "mlp_w2_1"))
LAYER_NORMS = ("ln_mix_g", "ln_mix_b", "ln_mlp_g", "ln_mlp_b")
WEIGHT_NAMES = ("meta_tokens", "ev_w_in", "ev_conv_w", "ev_conv_b", "ev_w_rg_a", "ev_b_rg_a", "ev_w_rg_x", "ev_b_rg_x",
                "ev_lru_lambda", "ev_q_norm_g", "ev_w_uq", "ev_kv_norm_g", "ev_w_ukv", "ev_w_out", "od_w_in", "od_w_out",
                "ln_mix_g", "ln_mix_b", "mlp_w1", "mlp_w2", "ln_mlp_g", "ln_mlp_b")


def _to_rows(flat, row_align):
    n = flat.shape[-1]
    rows = _round_up(-(-n // PACK_COLS), row_align)
    pad = rows * PACK_COLS - n
    if pad:
        flat = jnp.pad(flat, [(0, 0)] * (flat.ndim - 1) + [(0, pad)])
    return flat.reshape(flat.shape[:-1] + (rows, PACK_COLS))


def _shard_shape(K, N, axis):
    return (K // N_CHIPS, N) if axis == 0 else (K, N // N_CHIPS)


def _gather_shards(stacked, K, N, axis):
    if axis == 0:
        return stacked.reshape(K, N)
    return stacked.transpose(1, 0, 2).reshape(K, N)


def _split_shards(full, K, N, axis):
    if axis == 0:
        return full.reshape(N_CHIPS, -1)
    return full.reshape(K, N_CHIPS, N // N_CHIPS).transpose(1, 0, 2).reshape(N_CHIPS, -1)


def kernel(x, meta_tokens, ev_w_in, ev_conv_w, ev_conv_b, ev_w_rg_a, ev_b_rg_a, ev_w_rg_x, ev_b_rg_x, ev_lru_lambda, ev_q_norm_g, ev_w_uq, ev_kv_norm_g, ev_w_ukv, ev_w_out, od_w_in, od_w_out, ln_mix_g, ln_mix_b, mlp_w1, mlp_w2, ln_mlp_g, ln_mlp_b, loss_target, m_meta_tokens, m_ev_w_in, m_ev_conv_w, m_ev_conv_b, m_ev_w_rg_a, m_ev_b_rg_a, m_ev_w_rg_x, m_ev_b_rg_x, m_ev_lru_lambda, m_ev_q_norm_g, m_ev_w_uq, m_ev_kv_norm_g, m_ev_w_ukv, m_ev_w_out, m_od_w_in, m_od_w_out, m_ln_mix_g, m_ln_mix_b, m_mlp_w1, m_mlp_w2, m_ln_mlp_g, m_ln_mlp_b, v_meta_tokens, v_ev_w_in, v_ev_conv_w, v_ev_conv_b, v_ev_w_rg_a, v_ev_b_rg_a, v_ev_w_rg_x, v_ev_b_rg_x, v_ev_lru_lambda, v_ev_q_norm_g, v_ev_w_uq, v_ev_kv_norm_g, v_ev_w_ukv, v_ev_w_out, v_od_w_in, v_od_w_out, v_ln_mix_g, v_ln_mix_b, v_mlp_w1, v_mlp_w2, v_ln_mlp_g, v_ln_mlp_b):
    given = dict(locals())
    local_big = {"ev_w_in": ev_w_in[0], "ev_w_uq": ev_w_uq[0], "ev_w_ukv": ev_w_ukv[0], "ev_w_out": ev_w_out[0],
                 "od_w_in": od_w_in[0], "od_w_out": od_w_out[0], "mlp_w1_0": mlp_w1[0], "mlp_w1_1": mlp_w1[1],
                 "mlp_w2_0": mlp_w2[0], "mlp_w2_1": mlp_w2[1]}

    specs = {spec[0]: spec for spec in BIG_SPECS}
    mixer0_m, mlp0_m, layer1_m = MATRIX_GROUPS

    def shards(names):
        return [local_big[n].astype(MXU_DTYPE) for n in names]

    def whole(stacked, n):
        _, K, N, ax = specs[n]
        return stacked if n in COL_SHARD_GRADS else _gather_shards(stacked, K, N, ax)

    def filled(gathered, own, names):
        return {n: whole(_with_own(g_, o_), n) for n, g_, o_ in zip(names, gathered, own)}

    own_a, own_b, own_c = shards(mixer0_m), shards(mlp0_m), shards(layer1_m)
    small = [meta_tokens, jnp.pad(ev_conv_w[0], ((0, 16 - CONV_WIDTH), (0, 0)))]
    gathered_a = _allgather_chips(own_a + small, "weight_allgather_mixer0")
    pending_b, token1 = _allgather_start(own_b, "weight_allgather_mlp0_start")
    pending_c, token2 = _allgather_start(own_c, "weight_allgather_layer1_start")
    meta_full = _gather_shards(_with_own(gathered_a[-2], small[0]), N_META, D_MODEL, 1)
    conv_full = _gather_shards(_with_own(gathered_a[-1], small[1])[:, :CONV_WIDTH], CONV_WIDTH, LRU_WIDTH, 1)

    def slots(names, dtype):
        return {n: jnp.zeros((N_CHIPS, specs[n][1], specs[n][2] // N_CHIPS) if n in COL_SHARD_GRADS else specs[n][1:3], dtype) for n in names}

    def norms(names, layer):
        return {n: given[n][layer] for n in names}

    def finish_gather(pending, own, after, names, tag):
        landed = _allgather_wait(pending, lax.stop_gradient(after), "weight_allgather_%s_wait" % tag)
        return filled(_sibling_forward(landed, "weight_allgather_%s_forward" % tag), own, names)

    diff_a = {**slots(mixer0_m, MXU_DTYPE), **norms(("ln_mix_g", "ln_mix_b"), 0), **{n: given[n] for n in REPLICATED if n not in LAYER_NORMS},
              "x": x, "meta_tokens": meta_full, "ev_conv_w": conv_full}
    diff_b = {**slots(mlp0_m, MXU_DTYPE), **norms(("ln_mlp_g", "ln_mlp_b"), 0)}
    diff_c = {**slots(layer1_m, MXU_DTYPE), **norms(LAYER_NORMS, 1)}
    w_a = filled(gathered_a[:len(mixer0_m)], own_a, mixer0_m)
    h_a, back_a = jax.vjp(lambda d: _mixer0(d, w_a, token1 + token2), diff_a)
    w_b = finish_gather(pending_b, own_b, h_a, mlp0_m, "mlp0")
    h_b, back_b = jax.vjp(lambda d, hh: _mlp0(d, hh, w_b), diff_b, h_a)
    w_c = finish_gather(pending_c, own_c, h_b, layer1_m, "layer1")
    loss, back_c = jax.vjp(lambda d, hh: _layer1_loss(d, hh, w_c, loss_target), diff_c, h_b)
    loss = lax.psum(loss, ("x", "y", "c"))

    def blocks_of(grad, n):
        _, K, N, ax = specs[n]
        if n in COL_SHARD_GRADS:
            blocks = grad
        elif ax == 0:
            blocks = grad.reshape(N_CHIPS, K // N_CHIPS, N)
        else:
            blocks = grad.reshape(K, N_CHIPS, N // N_CHIPS).transpose(1, 0, 2)
        return blocks.reshape(N_CHIPS, 2, blocks.shape[1] // 2, blocks.shape[2])

    def start_reduce(grads_of, names, tag, dh):
        flying, (dh,) = _direct_scatter_start([blocks_of(grads_of[n], n) for n in names], "grad_scatter_%s_start" % tag, [dh])
        return flying, dh

    g_c, dh = back_c(jnp.ones((), F32))
    flying_c, dh = start_reduce(g_c, layer1_m, "layer1", dh)
    g_b, dh = back_b(dh)
    flying_b, dh = start_reduce(g_b, mlp0_m, "mlp0", dh)
    (g_a,) = back_a(dh)

    g = {**g_a, **g_b, **g_c}
    g.update({n: jnp.stack([(g_b if n in g_b else g_a)[n], g_c[n]]) for n in LAYER_NORMS})
    repl = jnp.concatenate([g[n].reshape(-1) for n in REPLICATED]).reshape(N_CHIPS, -1)
    small = [_split_shards(g["meta_tokens"], N_META, D_MODEL, 1), _split_shards(g["ev_conv_w"], CONV_WIDTH, LRU_WIDTH, 1), repl]
    small = [pc.reshape(N_CHIPS, 2, -1) for pc in small]
    n_small = sum(pc.shape[2] for pc in small)
    small.append(jnp.zeros((N_CHIPS, 2, _round_up(n_small, 32 * PACK_COLS) - n_small), F32))
    p_small = jnp.concatenate(small, axis=2).reshape(N_CHIPS, 2, -1, PACK_COLS)
    flying_a, _ = _direct_scatter_start([blocks_of(g_a[n], n) for n in mixer0_m] + [p_small], "grad_scatter_mixer0_start")
    started = [g_a["x"], flying_a[2][0]]
    ps_c, ts_c = _direct_scatter_wait(flying_c, started, "grad_scatter_layer1_wait")
    ps_b, ts_b = _direct_scatter_wait(flying_b, started, "grad_scatter_mlp0_wait")
    fs_bc = [_sum_direct(p, t, "grad_sum_%d" % i) for i, (p, t) in enumerate(zip(ps_b + ps_c, ts_b + ts_c))]
    red_big = dict(zip(mlp0_m + layer1_m, _sibling_gather(fs_bc, "grad_sibling_gather")))

    grads, delta, new_m, new_v = {}, {}, {}, {}

    def update_big(names):
        done = []
        for name, parts in BIG_PARAMS:
            if parts[0] in names:
                shp = given[name].shape
                two_d = (-1, shp[-1])
                grads[name] = jnp.stack([red_big[part].reshape(shp[1:]) for part in parts])
                d, nm, nv = _adamw(given[name].reshape(two_d), grads[name].reshape(two_d), given["m_" + name].reshape(two_d),
                                   given["v_" + name].reshape(two_d), "adamw_" + name)
                delta[name], new_m[name], new_v[name] = d.reshape(shp), nm.reshape(shp), nv.reshape(shp)
                done.append(nv)
        return done

    updated = update_big(mlp0_m + layer1_m)
    ps_a, ts_a = _direct_scatter_wait(flying_a, updated, "grad_scatter_mixer0_wait")
    fs_a = [_sum_direct(p, t, "grad_sum_mixer0_%d" % i) for i, (p, t) in enumerate(zip(ps_a, ts_a))]
    reduced_a = _sibling_gather(fs_a, "grad_sibling_gather_mixer0")
    red_big.update(zip(mixer0_m, reduced_a))
    red_small = reduced_a[-1].reshape(2, -1)
    update_big(mixer0_m)

    def take(off, sz):
        return jnp.concatenate([red_small[0, off // 2:(off + sz) // 2], red_small[1, off // 2:(off + sz) // 2]])

    off = 0
    for name in SMALL_SHARDED:
        sz = given[name].size
        grads[name] = take(off, sz).reshape(given[name].shape)
        off += sz
    n_repl = repl.shape[1]
    own_repl = _to_rows(take(off, n_repl), 16)
    repl_all = _with_own(_allgather_chips([own_repl], "replicated_allgather")[0], own_repl).reshape(N_CHIPS, -1)[:, :n_repl].reshape(-1)
    off = 0
    for name in REPLICATED:
        sz = given[name].size
        grads[name] = repl_all[off:off + sz].reshape(given[name].shape)
        off += sz

    smalls = SMALL_SHARDED + REPLICATED

    def pack_small(get):
        return _to_rows(jnp.concatenate([get(n).reshape(-1) for n in smalls]), 8)

    outs = _adamw(pack_small(lambda n: given[n]), pack_small(lambda n: grads[n]), pack_small(lambda n: given["m_" + n]),
                  pack_small(lambda n: given["v_" + n]), "adamw_small")
    for res, flat in zip((delta, new_m, new_v), outs):
        flat, off = flat.reshape(-1), 0
        for n in smalls:
            sz = given[n].size
            res[n] = flat[off:off + sz].reshape(given[n].shape)
            off += sz

    return (loss, g_a["x"], *[grads[n] for n in WEIGHT_NAMES], *[delta[n] for n in WEIGHT_NAMES],
            *[new_m[n] for n in WEIGHT_NAMES], *[new_v[n] for n in WEIGHT_NAMES])
```

```python
import functools
import math

import jax
import jax.numpy as jnp
from jax import lax
from jax.experimental import pallas as pl
from jax.experimental.pallas import tpu as pltpu

F32 = jnp.float32
MXU_DTYPE = jnp.bfloat16

D_MODEL = 1024
N_META = 16
LRU_WIDTH = 512
LRU_HEADS = 4
LRU_HEAD_DIM = 128
CONV_WIDTH = 4
LRU_C = 8.0
MLA_HEADS = 8
MLA_NOPE = 64
MLA_ROPE = 32
MLA_V = 64
MLA_Q_RANK = 256
MLA_KV_RANK = 128
RET_HEADS = 4
RET_QK_DIM = 256
RET_V_DIM = 512
D_FF = 4096
ROPE_BASE = 10000.0
DN_ALPHA = 4.0 ** 0.25
EPS = 1e-5
NEG_INF = -1e30
SEQ_BLOCK = 128

ADAM_LR = 0.001
ADAM_B1 = 0.9
ADAM_B2 = 0.999
ADAM_EPS = 1e-08
ADAM_WD = 0.01
ADAM_STEP = 10

PACK_COLS = 1024
TN_INPUT_VMEM_BYTES = 28 << 20
N_CHIPS = 4

MESH = pl.DeviceIdType.MESH


def _pick(n, target, align):
    best = None
    for t in range(align, min(n, target) + 1, align):
        if n % t == 0:
            best = t
    return n if best is None else best


def _round_up(n, m):
    return (n + m - 1) // m * m


def _relu2(a):
    r = jnp.maximum(a, 0.0)
    return r * r


def _ln_stats(z):
    mu = jnp.mean(z, axis=-1, keepdims=True)
    zc = z - mu
    var = jnp.mean(zc * zc, axis=-1, keepdims=True)
    return zc, lax.rsqrt(var + EPS)


def _mm_nn(a, w, act, name, out_dtype=F32, norm=None):
    M, K = a.shape
    sharded = w.ndim == 3
    n = w.shape[-1]
    N = n * (w.shape[0] if sharded else 1)
    tm = _pick(M, 1088 if K * a.dtype.itemsize <= 4096 and norm is None else 544, 8)
    tn = _pick(n, 1024, 128)
    per = n // tn
    assert norm is None or tn == N

    def body(a_ref, w_ref, *rest):
        av = a_ref[...]
        if act:
            av = _relu2(av.astype(F32))
        r = jnp.dot(av.astype(MXU_DTYPE), w_ref[...].astype(MXU_DTYPE), preferred_element_type=F32)
        if norm is None:
            rest[0][...] = r.astype(out_dtype)
        else:
            r_ref, g_ref, b_ref, o_ref, z_ref = rest
            z = DN_ALPHA * r_ref[...] + r
            zc, rstd = _ln_stats(z)
            z_ref[...] = z
            o_ref[...] = zc * rstd * g_ref[...] + b_ref[...]

    w_spec = pl.BlockSpec((None, K, tn), lambda i, j: (j // per, 0, j % per)) if sharded else pl.BlockSpec((K, tn), lambda i, j: (0, j))
    tile = pl.BlockSpec((tm, tn), lambda i, j: (i, j))
    in_specs, args = [pl.BlockSpec((tm, K), lambda i, j: (i, 0)), w_spec], [a, w]
    if norm is None:
        out_specs, out_shape = tile, jax.ShapeDtypeStruct((M, N), out_dtype)
    else:
        vec = pl.BlockSpec((1, N), lambda i, j: (0, 0))
        in_specs += [tile, vec, vec]
        args += [norm[0], norm[1].reshape(1, N), norm[2].reshape(1, N)]
        out_specs, out_shape = [tile, tile], [jax.ShapeDtypeStruct((M, N), F32)] * 2
    return pl.pallas_call(
        body, name=name, grid=(M // tm, N // tn), in_specs=in_specs, out_specs=out_specs, out_shape=out_shape,
        compiler_params=pltpu.CompilerParams(dimension_semantics=("parallel", "arbitrary")),
    )(*args)


def _mm_nt(g, w, a_src, name, out_dtype=F32, plus=None):
    M, N = g.shape
    sharded = w.ndim == 3
    K, n = w.shape[-2], w.shape[-1]
    if sharded:
        tk, nk = N, 1
    else:
        tk = N if N * g.dtype.itemsize <= 8192 else _pick(N, 2048, 128)
        nk = N // tk
    tm = _pick(M, 1088 if tk * g.dtype.itemsize <= 4096 else 544, 8)
    tn = _pick(K, 1024, 128)
    has_src = a_src is not None
    assert nk == 1 or out_dtype == F32
    assert plus is None or not has_src

    def body(*refs):
        if has_src:
            g_ref, w_ref, s_ref, o_ref = refs
        elif plus is not None:
            g_ref, w_ref, p_ref, o_ref = refs
        else:
            g_ref, w_ref, o_ref = refs
        nt = (((1,), (1,)), ((), ()))
        if sharded:
            r = sum(lax.dot_general(g_ref[:, s * n:(s + 1) * n].astype(MXU_DTYPE), w_ref[s].astype(MXU_DTYPE), nt, preferred_element_type=F32)
                    for s in range(w_ref.shape[0]))
        else:
            r = lax.dot_general(g_ref[...].astype(MXU_DTYPE), w_ref[...].astype(MXU_DTYPE), nt, preferred_element_type=F32)
        if has_src:
            r = r * (2.0 * jnp.maximum(s_ref[...].astype(F32), 0.0))
        first = r if plus is None else r + DN_ALPHA * p_ref[...]
        if nk == 1:
            o_ref[...] = first.astype(out_dtype)
        else:
            k = pl.program_id(2)

            @pl.when(k == 0)
            def _():
                o_ref[...] = first

            @pl.when(k > 0)
            def _():
                o_ref[...] += r

    w_spec = (pl.BlockSpec((w.shape[0], tn, n), lambda i, j, k: (0, j, 0)) if sharded
              else pl.BlockSpec((tn, tk), lambda i, j, k: (j, k)))
    in_specs = [pl.BlockSpec((tm, tk), lambda i, j, k: (i, k)), w_spec]
    args = [g, w]
    if has_src:
        assert nk == 1
        in_specs.append(pl.BlockSpec((tm, tn), lambda i, j, k: (i, j)))
        args.append(a_src)
    if plus is not None:
        in_specs.append(pl.BlockSpec((tm, tn), lambda i, j, k: (i, j)))
        args.append(plus)
    return pl.pallas_call(
        body, name=name,
        grid=(M // tm, K // tn, nk),
        in_specs=in_specs,
        out_specs=pl.BlockSpec((tm, tn), lambda i, j, k: (i, j)),
        out_shape=jax.ShapeDtypeStruct((M, K), out_dtype),
        compiler_params=pltpu.CompilerParams(dimension_semantics=("parallel", "parallel", "arbitrary")),
    )(*args)


def _mm_tn(a, g, act, name, col_shards=1, out_dtype=F32):
    M, K = a.shape
    _, N = g.shape
    n = N // col_shards
    tm, tn = _pick(K, 1024, 128), _pick(n, 1024, 128)
    row_bytes = tm * a.dtype.itemsize + tn * g.dtype.itemsize
    tk = _pick(M, min(2176, TN_INPUT_VMEM_BYTES // (2 * row_bytes)), 8)
    nk = M // tk
    per = n // tn
    direct = out_dtype == F32

    def body(a_ref, g_ref, o_ref, *scratch):
        acc_ref = o_ref if direct else scratch[0]
        k = pl.program_id(2)
        av = a_ref[...]
        if act:
            av = _relu2(av.astype(F32))
        r = lax.dot_general(av.astype(MXU_DTYPE), g_ref[...].astype(MXU_DTYPE),
                            (((0,), (0,)), ((), ())), preferred_element_type=F32)

        @pl.when(k == 0)
        def _():
            acc_ref[...] = r

        @pl.when(k > 0)
        def _():
            acc_ref[...] += r

        if not direct:
            @pl.when(k == nk - 1)
            def _():
                o_ref[...] = acc_ref[...].astype(out_dtype)

    if col_shards == 1:
        out_spec, out_shape = pl.BlockSpec((tm, tn), lambda i, j, k: (i, j)), (K, N)
    else:
        out_spec, out_shape = pl.BlockSpec((None, tm, tn), lambda i, j, k: (j // per, i, j % per)), (col_shards, K, n)
    return pl.pallas_call(
        body, name=name,
        grid=(K // tm, N // tn, nk),
        in_specs=[pl.BlockSpec((tk, tm), lambda i, j, k: (k, i)), pl.BlockSpec((tk, tn), lambda i, j, k: (k, j))],
        out_specs=out_spec,
        out_shape=jax.ShapeDtypeStruct(out_shape, out_dtype),
        scratch_shapes=[] if direct else [pltpu.VMEM((tm, tn), F32)],
        compiler_params=pltpu.CompilerParams(dimension_semantics=("parallel", "parallel", "arbitrary")),
    )(a, g)


@functools.partial(jax.custom_vjp, nondiff_argnums=(3, 4, 5, 6))
def matmul(a, w, w_grad_slot, act, name, out_dtype, col_shards):
    return _mm_nn(a, w, act, name + "_fwd", out_dtype)


def _matmul_fwd(a, w, w_grad_slot, act, name, out_dtype, col_shards):
    return _mm_nn(a, w, act, name + "_fwd", out_dtype), (a, w, jnp.zeros((), w_grad_slot.dtype))


def _matmul_bwd(act, name, out_dtype, col_shards, res, g):
    a, w, slot_like = res
    w_grad_dtype = slot_like.dtype
    da = _mm_nt(g, w, a if act else None, name + "_dx")
    dw = _mm_tn(a, g, act, name + "_dw", col_shards, w_grad_dtype)
    return da, None, dw


matmul.defvjp(_matmul_fwd, _matmul_bwd)


def _ln_bwd_call(z, g, dy, name):
    M, D = z.shape
    tm = _pick(M, 544, 8)

    def body(z_ref, g_ref, dy_ref, dz_ref, dg_ref, db_ref):
        @pl.when(pl.program_id(0) == 0)
        def _():
            dg_ref[...] = jnp.zeros_like(dg_ref)
            db_ref[...] = jnp.zeros_like(db_ref)

        zc, rstd = _ln_stats(z_ref[...])
        xhat = zc * rstd
        dy = dy_ref[...]
        dxh = dy * g_ref[...]
        m1 = jnp.mean(dxh, axis=-1, keepdims=True)
        m2 = jnp.mean(dxh * xhat, axis=-1, keepdims=True)
        dz_ref[...] = rstd * (dxh - m1 - xhat * m2)
        dg_ref[...] += jnp.sum(dy * xhat, axis=0, keepdims=True)
        db_ref[...] += jnp.sum(dy, axis=0, keepdims=True)

    row = pl.BlockSpec((tm, D), lambda i: (i, 0))
    vec = pl.BlockSpec((1, D), lambda i: (0, 0))
    return pl.pallas_call(
        body, name=name, grid=(M // tm,), in_specs=[row, vec, row], out_specs=[row, vec, vec],
        out_shape=[jax.ShapeDtypeStruct((M, D), F32), jax.ShapeDtypeStruct((1, D), F32), jax.ShapeDtypeStruct((1, D), F32)],
        compiler_params=pltpu.CompilerParams(dimension_semantics=("arbitrary",)),
    )(z, g.reshape(1, D), dy)


@functools.partial(jax.custom_vjp, nondiff_argnums=(7,))
def mlp_block(h, w1, w2, w1_grad_slot, w2_grad_slot, g, b, name):
    return _mlp_block_fwd(h, w1, w2, w1_grad_slot, w2_grad_slot, g, b, name)[0]


def _mlp_block_fwd(h, w1, w2, w1_grad_slot, w2_grad_slot, g, b, name):
    u = _mm_nn(h, w1, False, name + "_w1_fwd", out_dtype=MXU_DTYPE)
    out, z = _mm_nn(u, w2, True, name + "_w2_norm_fwd", norm=(h, g, b))
    return out, (h, u, z, w1, w2, g, jnp.zeros((), w1_grad_slot.dtype))


def _mlp_block_bwd(name, res, dy):
    h, u, z, w1, w2, g, slot_like = res
    dz, dg, db = _ln_bwd_call(z, g, dy, name + "_norm_bwd")
    du = _mm_nt(dz, w2, u, name + "_w2_dx", out_dtype=MXU_DTYPE)
    dw2 = _mm_tn(u, dz, True, name + "_w2_dw", 1, slot_like.dtype)
    dh = _mm_nt(du, w1, None, name + "_w1_dx", plus=dz)
    dw1 = _mm_tn(h, du, False, name + "_w1_dw", N_CHIPS, slot_like.dtype)
    return dh, None, None, dw1, dw2, dg.reshape(g.shape), db.reshape(g.shape)


mlp_block.defvjp(_mlp_block_fwd, _mlp_block_bwd)


@functools.partial(jax.custom_vjp, nondiff_argnums=(6,))
def out_block(h, y, w, w_grad_slot, g, b, name):
    return _out_block_fwd(h, y, w, w_grad_slot, g, b, name)[0]


def _out_block_fwd(h, y, w, w_grad_slot, g, b, name):
    out, z = _mm_nn(y, w, False, name + "_norm_fwd", norm=(h, g, b))
    return out, (y, z, w, g, jnp.zeros((), w_grad_slot.dtype))


def _out_block_bwd(name, res, dy):
    y, z, w, g, slot_like = res
    dz, dg, db = _ln_bwd_call(z, g, dy, name + "_norm_bwd")
    d_y = _mm_nt(dz, w, None, name + "_dx")
    dw = _mm_tn(y, dz, False, name + "_dw", 1, slot_like.dtype)
    return DN_ALPHA * dz, d_y, None, dw, dg.reshape(g.shape), db.reshape(g.shape)


out_block.defvjp(_out_block_fwd, _out_block_bwd)


def _rms_fwd_call(x, g, name, col_block=0):
    R = x.shape[0]
    W = g.shape[-1]
    tr = _pick(R, 1088, 8)

    def body(x_ref, g_ref, o_ref):
        xv = x_ref[...]
        rstd = lax.rsqrt(jnp.mean(xv * xv, axis=-1, keepdims=True) + EPS)
        o_ref[...] = xv * rstd * g_ref[...]

    vec = pl.BlockSpec((1, W), lambda i: (0, 0))
    return pl.pallas_call(
        body, name=name, grid=(R // tr,), in_specs=[pl.BlockSpec((tr, W), lambda i: (i, col_block)), vec],
        out_specs=pl.BlockSpec((tr, W), lambda i: (i, 0)), out_shape=jax.ShapeDtypeStruct((R, W), F32),
        compiler_params=pltpu.CompilerParams(dimension_semantics=("parallel",)),
    )(x, g.reshape(1, W))


def _rms_bwd_call(x, g, dy, name, col_block=0):
    R = x.shape[0]
    W = g.shape[-1]
    tr = _pick(R, 1088, 8)

    def body(x_ref, g_ref, dy_ref, dx_ref, dg_ref):
        @pl.when(pl.program_id(0) == 0)
        def _():
            dg_ref[...] = jnp.zeros_like(dg_ref)

        xv = x_ref[...]
        rstd = lax.rsqrt(jnp.mean(xv * xv, axis=-1, keepdims=True) + EPS)
        xhat = xv * rstd
        dy = dy_ref[...]
        dxh = dy * g_ref[...]
        dx_ref[...] = rstd * (dxh - xhat * jnp.mean(dxh * xhat, axis=-1, keepdims=True))
        dg_ref[...] += jnp.sum(dy * xhat, axis=0, keepdims=True)

    row = pl.BlockSpec((tr, W), lambda i: (i, 0))
    vec = pl.BlockSpec((1, W), lambda i: (0, 0))
    return pl.pallas_call(
        body, name=name, grid=(R // tr,), in_specs=[pl.BlockSpec((tr, W), lambda i: (i, col_block)), vec, row], out_specs=[row, vec],
        out_shape=[jax.ShapeDtypeStruct((R, W), F32), jax.ShapeDtypeStruct((1, W), F32)],
        compiler_params=pltpu.CompilerParams(dimension_semantics=("arbitrary",)),
    )(x, g.reshape(1, W), dy)


def _loss_call(h, tgt, n_tokens, name):
    B, Tp, D = h.shape
    tr = _pick(Tp, 544, 8)

    def body(y_ref, t_ref, dy_ref, acc_ref):
        @pl.when(jnp.logical_and(pl.program_id(0) == 0, pl.program_id(1) == 0))
        def _():
            acc_ref[...] = jnp.zeros_like(acc_ref)

        t = lax.broadcasted_iota(jnp.int32, (tr, 1), 0) + pl.program_id(1) * tr
        counts = jnp.logical_and(t >= N_META, t < N_META + n_tokens)
        e = jnp.where(counts, y_ref[...] - t_ref[...], 0.0)
        dy_ref[...] = e * (1.0 / D)
        acc_ref[...] += jnp.sum(jnp.sum(e * e, axis=-1, keepdims=True), axis=0, keepdims=True) * (0.5 / D)

    row = pl.BlockSpec((None, tr, D), lambda b, i: (b, i, 0))
    one = pl.BlockSpec((1, 1), lambda b, i: (0, 0))
    return pl.pallas_call(
        body, name=name, grid=(B, Tp // tr), in_specs=[row, row], out_specs=[row, one],
        out_shape=[jax.ShapeDtypeStruct((B, Tp, D), F32), jax.ShapeDtypeStruct((1, 1), F32)],
        compiler_params=pltpu.CompilerParams(dimension_semantics=("arbitrary", "arbitrary")),
    )(h, tgt)


@functools.partial(jax.custom_vjp, nondiff_argnums=(2,))
def loss_head(h, tgt, n_tokens):
    return _loss_call(h, tgt, n_tokens, "loss_head")[1][0, 0]


def _loss_head_fwd(h, tgt, n_tokens):
    dy, acc = _loss_call(h, tgt, n_tokens, "loss_head")
    return acc[0, 0], dy


def _loss_head_bwd(n_tokens, dy, ct):
    return ct * dy, None


loss_head.defvjp(_loss_head_fwd, _loss_head_bwd)


_GELU_C = math.sqrt(2.0 / math.pi)


def _gelu_parts(x):
    x2 = x * x
    t = jnp.tanh(_GELU_C * (x + 0.044715 * x * x2))
    gelu = 0.5 * x * (1.0 + t)
    dgelu = 0.5 * (1.0 + t) + 0.5 * x * (1.0 - t * t) * (_GELU_C * (1.0 + 3.0 * 0.044715 * x2))
    return gelu, dgelu


def _sigmoid(x):
    return 1.0 / (1.0 + jnp.exp(-x))


def _scan8(a, b, carry, reverse):
    row = lax.broadcasted_iota(jnp.int32, a.shape, 0)
    for s in (1, 2, 4):
        shift = 8 - s if reverse else s
        keep = (row < 8 - s) if reverse else (row >= s)
        b = jnp.where(keep, a * pltpu.roll(b, shift, 0) + b, b)
        a = jnp.where(keep, a * pltpu.roll(a, shift, 0), a)
    return a * carry + b


def _lru_pre(prec_ref, prev_ref, first, cw_ref, cb_ref, wa_ref, ba_ref, wx_ref, bx_ref, sp_ref):
    tc = prec_ref.shape[0]
    prev = jnp.where(first, 0.0, prev_ref[...])
    ext = jnp.concatenate([prev, prec_ref[...]], axis=0)
    cw = cw_ref[...]
    taps = [ext[8:] if k == CONV_WIDTH - 1 else pltpu.roll(ext, CONV_WIDTH - 1 - k, 0)[8:] for k in range(CONV_WIDTH)]
    xc = cb_ref[...] + sum(cw[k:k + 1, :] * taps[k] for k in range(CONV_WIDTH))
    ga, gx = [], []
    for h in range(LRU_HEADS):
        xh = xc[:, h * LRU_HEAD_DIM:(h + 1) * LRU_HEAD_DIM].astype(MXU_DTYPE)
        ga.append(jnp.dot(xh, wa_ref[h].astype(MXU_DTYPE), preferred_element_type=F32))
        gx.append(jnp.dot(xh, wx_ref[h].astype(MXU_DTYPE), preferred_element_type=F32))
    r = _sigmoid(jnp.concatenate(ga, axis=1) + ba_ref[...])
    i = _sigmoid(jnp.concatenate(gx, axis=1) + bx_ref[...])
    log_a = -LRU_C * r * sp_ref[...]
    a = jnp.exp(log_a)
    a2 = a * a
    mult = jnp.sqrt(-jnp.tanh(log_a) * (a2 + 1.0))
    return taps, xc, r, i, a, a2, mult


def _lru_fwd_call(p, cw, cb, wa, ba, wx, bx, sp):
    B, Tp, _ = p.shape
    W = LRU_WIDTH
    tc = SEQ_BLOCK
    nc = Tp // tc

    def body(pg_ref, prec_ref, prev_ref, cw_ref, cb_ref, wa_ref, ba_ref, wx_ref, bx_ref, sp_ref, y_ref, h_ref, carry_ref):
        first = pl.program_id(1) == 0

        @pl.when(first)
        def _():
            carry_ref[...] = jnp.zeros_like(carry_ref)

        _, xc, r, i, a, a2, mult = _lru_pre(prec_ref, prev_ref, first, cw_ref, cb_ref, wa_ref, ba_ref, wx_ref, bx_ref, sp_ref)
        b = mult * (i * xc)
        carry = carry_ref[0:1, :]
        for t in range(tc // 8):
            h = _scan8(a[8 * t:8 * t + 8], b[8 * t:8 * t + 8], carry, False)
            h_ref[8 * t:8 * t + 8, :] = h
            carry = h[7:8, :]
        carry_ref[...] = jnp.broadcast_to(carry, carry_ref.shape)
        y_ref[...] = h_ref[...] * _gelu_parts(pg_ref[...])[0]

    cur = pl.BlockSpec((None, tc, W), lambda b, j: (b, j, 0))
    rec = pl.BlockSpec((None, tc, W), lambda b, j: (b, j, 1))
    prev = pl.BlockSpec((None, 8, W), lambda b, j: (b, jnp.maximum(j * (tc // 8) - 1, 0), 1))
    vec = pl.BlockSpec((1, W), lambda b, j: (0, 0))
    cws = pl.BlockSpec((CONV_WIDTH, W), lambda b, j: (0, 0))
    wsp = pl.BlockSpec((LRU_HEADS, LRU_HEAD_DIM, LRU_HEAD_DIM), lambda b, j: (0, 0, 0))
    return pl.pallas_call(
        body, name="lru_fwd", grid=(B, nc),
        in_specs=[cur, rec, prev, cws, vec, wsp, vec, wsp, vec, vec],
        out_specs=[cur, cur],
        out_shape=[jax.ShapeDtypeStruct((B, Tp, W + MLA_HEADS * HEAD_LANES), F32), jax.ShapeDtypeStruct((B, Tp, W), F32)],
        scratch_shapes=[pltpu.VMEM((8, W), F32)],
        compiler_params=pltpu.CompilerParams(dimension_semantics=("arbitrary", "arbitrary")),
    )(p, p, p, cw, cb, wa, ba, wx, bx, sp)


def _lru_bwd_call(p, hseq, dy, cw, cb, wa, ba, wx, bx, sp, dpq, dpkv, dkpe):
    B, Tp, P = p.shape
    W = LRU_WIDTH
    tc = SEQ_BLOCK
    nc = Tp // tc
    HD = LRU_HEAD_DIM

    def body(pg_ref, prec_ref, prev_ref, h_ref, hprev_ref, dy_ref, cw_ref, cb_ref, wa_ref, ba_ref, wx_ref, bx_ref, sp_ref,
             dpq_ref, dpkv_ref, dkpe_ref, dp_ref, dcw_ref, dcb_ref, dwa_ref, dba_ref, dwx_ref, dbx_ref, dsp_ref,
             gcar_ref, anext_ref, halo_ref, g_ref):
        j = pl.program_id(1)
        first = j == nc - 1
        last = j == 0

        @pl.when(jnp.logical_and(pl.program_id(0) == 0, last))
        def _():
            for ref in (dcw_ref, dcb_ref, dwa_ref, dba_ref, dwx_ref, dbx_ref, dsp_ref):
                ref[...] = jnp.zeros_like(ref)

        @pl.when(last)
        def _():
            gcar_ref[...] = jnp.zeros_like(gcar_ref)
            anext_ref[...] = jnp.zeros_like(anext_ref)
            halo_ref[...] = jnp.zeros_like(halo_ref)

        taps, xc, r, i, a, a2, mult = _lru_pre(prec_ref, prev_ref, first, cw_ref, cb_ref, wa_ref, ba_ref, wx_ref, bx_ref, sp_ref)
        row = lax.broadcasted_iota(jnp.int32, (tc, W), 0)
        gelu, dgelu = _gelu_parts(pg_ref[...])
        dy = dy_ref[...]
        hcur = h_ref[...]
        dp_ref[:, 0:W] = dy * hcur * dgelu
        dp_ref[:, 2 * W:2 * W + MLA_Q_RANK] = dpq_ref[...]
        dp_ref[:, _KPE_START - MLA_KV_RANK:_KPE_START] = dpkv_ref[...]
        dp_ref[:, _KPE_START:P] = pltpu.roll(dkpe_ref[...], HEAD_LANES - MLA_NOPE, 1)[:, 0:P - _KPE_START]
        dh = dy * gelu
        a_next = jnp.where(row == tc - 1, anext_ref[0:1, :], pltpu.roll(a, tc - 1, 0))
        carry = gcar_ref[0:1, :]
        for t in reversed(range(tc // 8)):
            g = _scan8(a_next[8 * t:8 * t + 8], dh[8 * t:8 * t + 8], carry, True)
            g_ref[8 * t:8 * t + 8, :] = g
            carry = g[0:1, :]
        gcar_ref[...] = jnp.broadcast_to(carry, gcar_ref.shape)
        anext_ref[...] = jnp.broadcast_to(a[0:1, :], anext_ref.shape)
        G = g_ref[...]
        h_before = jnp.where(first, 0.0, hprev_ref[7:8, :])
        hprev = jnp.where(row == 0, h_before, pltpu.roll(hcur, 1, 0))
        d_a = G * hprev
        gx_ = G * xc
        d_mult = gx_ * i
        d_i = gx_ * mult
        dxc = G * (mult * i)
        d_la = d_a * a - d_mult * (a2 / mult)
        sp = sp_ref[...]
        d_r = d_la * (-LRU_C * sp)
        dsp_ref[...] += jnp.sum(d_la * (-LRU_C * r), axis=0, keepdims=True)
        dga = d_r * r * (1.0 - r)
        dgx = d_i * i * (1.0 - i)
        dba_ref[...] += jnp.sum(dga, axis=0, keepdims=True)
        dbx_ref[...] += jnp.sum(dgx, axis=0, keepdims=True)
        back = []
        for h in range(LRU_HEADS):
            sl = slice(h * HD, (h + 1) * HD)
            xh = xc[:, sl].astype(MXU_DTYPE)
            ah = dga[:, sl].astype(MXU_DTYPE)
            bh = dgx[:, sl].astype(MXU_DTYPE)
            tn = (((0,), (0,)), ((), ()))
            nt = (((1,), (1,)), ((), ()))
            dwa_ref[h] += lax.dot_general(xh, ah, tn, preferred_element_type=F32)
            dwx_ref[h] += lax.dot_general(xh, bh, tn, preferred_element_type=F32)
            back.append(lax.dot_general(ah, wa_ref[h].astype(MXU_DTYPE), nt, preferred_element_type=F32)
                        + lax.dot_general(bh, wx_ref[h].astype(MXU_DTYPE), nt, preferred_element_type=F32))
        dxc = dxc + jnp.concatenate(back, axis=1)
        dcb_ref[...] += jnp.sum(dxc, axis=0, keepdims=True)
        for k in range(CONV_WIDTH):
            dcw_ref[k:k + 1, :] += jnp.sum(dxc * taps[k], axis=0, keepdims=True)
        ext = jnp.concatenate([dxc, halo_ref[...]], axis=0)
        cw = cw_ref[...]
        acc = cw[CONV_WIDTH - 1:CONV_WIDTH, :] * dxc
        for k in range(CONV_WIDTH - 1):
            s = CONV_WIDTH - 1 - k
            acc = acc + cw[k:k + 1, :] * pltpu.roll(ext, tc + 8 - s, 0)[:tc]
        dp_ref[:, W:2 * W] = acc
        halo_ref[...] = dxc[0:8, :]

    rev = lambda j: nc - 1 - j
    cur = pl.BlockSpec((None, tc, W), lambda b, j: (b, rev(j), 0))
    rec = pl.BlockSpec((None, tc, W), lambda b, j: (b, rev(j), 1))
    prev = pl.BlockSpec((None, 8, W), lambda b, j: (b, jnp.maximum(rev(j) * (tc // 8) - 1, 0), 0))
    prev_rec = pl.BlockSpec((None, 8, W), lambda b, j: (b, jnp.maximum(rev(j) * (tc // 8) - 1, 0), 1))
    vec = pl.BlockSpec((1, W), lambda b, j: (0, 0))
    cws = pl.BlockSpec((CONV_WIDTH, W), lambda b, j: (0, 0))
    wsp = pl.BlockSpec((LRU_HEADS, HD, HD), lambda b, j: (0, 0, 0))
    vs = jax.ShapeDtypeStruct((1, W), F32)
    ws = jax.ShapeDtypeStruct((LRU_HEADS, HD, HD), F32)

    def rows(width):
        return pl.BlockSpec((None, tc, width), lambda b, j: (b, rev(j), 0))

    return pl.pallas_call(
        body, name="lru_bwd", grid=(B, nc),
        in_specs=[cur, rec, prev_rec, cur, prev, cur, cws, vec, wsp, vec, wsp, vec, vec, rows(MLA_Q_RANK), rows(MLA_KV_RANK), rows(HEAD_LANES)],
        out_specs=[rows(P), cws, vec, wsp, vec, wsp, vec, vec],
        out_shape=[jax.ShapeDtypeStruct((B, Tp, P), F32), jax.ShapeDtypeStruct((CONV_WIDTH, W), F32), vs, ws, vs, ws, vs, vs],
        scratch_shapes=[pltpu.VMEM((8, W), F32), pltpu.VMEM((8, W), F32), pltpu.VMEM((8, W), F32), pltpu.VMEM((tc, W), F32)],
        compiler_params=pltpu.CompilerParams(dimension_semantics=("arbitrary", "arbitrary")),
    )(p, p, p, hseq, hseq, dy, cw, cb, wa, ba, wx, bx, sp, dpq, dpkv, dkpe)


_Q_BLOCK = 2 * LRU_WIDTH // MLA_Q_RANK
_KV_BLOCK = (2 * LRU_WIDTH + MLA_Q_RANK) // MLA_KV_RANK
_KPE_START = 2 * LRU_WIDTH + MLA_Q_RANK + MLA_KV_RANK


@jax.custom_vjp
def even_front(p, cw, cb, wa, ba, wx, bx, sp, gq, gkv):
    return _even_front_fwd(p, cw, cb, wa, ba, wx, bx, sp, gq, gkv)[0]


def _even_front_fwd(p, cw, cb, wa, ba, wx, bx, sp, gq, gkv):
    B, Tp, W = p.shape
    p2d = p.reshape(B * Tp, W)
    y, hseq = _lru_fwd_call(p, cw, cb, wa, ba, wx, bx, sp)
    qn = _rms_fwd_call(p2d, gq, "q_norm_fwd", _Q_BLOCK)
    kvn = _rms_fwd_call(p2d, gkv, "kv_norm_fwd", _KV_BLOCK)
    kpe = jnp.pad(p[:, :, _KPE_START:], ((0, 0), (0, 0), (MLA_NOPE, HEAD_LANES - MLA_NOPE - MLA_ROPE)))
    return (y, qn, kvn, kpe), (p, hseq, cw, cb, wa, ba, wx, bx, sp, gq, gkv)


def _even_front_bwd(res, cts):
    p, hseq, cw, cb, wa, ba, wx, bx, sp, gq, gkv = res
    dy, dqn, dkvn, dkpe = cts
    B, Tp, W = p.shape
    p2d = p.reshape(B * Tp, W)
    dpq, dgq = _rms_bwd_call(p2d, gq, dqn, "q_norm_bwd", _Q_BLOCK)
    dpkv, dgkv = _rms_bwd_call(p2d, gkv, dkvn, "kv_norm_bwd", _KV_BLOCK)
    dp, dcw, dcb, dwa, dba, dwx, dbx, dsp = _lru_bwd_call(p, hseq, dy, cw, cb, wa, ba, wx, bx, sp, dpq.reshape(B, Tp, -1),
                                                          dpkv.reshape(B, Tp, -1), dkpe)
    return dp, dcw, dcb, dwa, dba, dwx, dbx, dsp, dgq.reshape(gq.shape), dgkv.reshape(gkv.shape)


even_front.defvjp(_even_front_fwd, _even_front_bwd)


def _rope_tables(pos, half):
    inv = ROPE_BASE ** (-jnp.arange(half, dtype=F32) / half)
    ang = pos.astype(F32)[:, None] * inv[None, :]
    return jnp.cos(ang), jnp.sin(ang)


_NT = (((1,), (1,)), ((), ()))
_TN = (((0,), (0,)), ((), ()))
HEAD_LANES = 128
_MLA_SCALE = (MLA_NOPE + MLA_ROPE) ** -0.5
_LOG2E = math.log2(math.e)


Q_BLOCK = 512


def _query_blocks(Tp):
    first = Tp % Q_BLOCK or Q_BLOCK
    return [(0, first)] + [(r, r + Q_BLOCK) for r in range(first, Tp, Q_BLOCK)]


def _mask_diagonal(s, fill):
    R, L = s.shape
    row = lax.broadcasted_iota(jnp.int32, (R, R), 0)
    col = lax.broadcasted_iota(jnp.int32, (R, R), 1)
    last = jnp.where(col <= row, s[:, L - R:], fill)
    return last if L == R else jnp.concatenate([s[:, :L - R], last], axis=1)


def _mla_rope_tables(pos):
    half = MLA_ROPE // 2
    cos, sin = _rope_tables(pos, half)
    T = pos.shape[0]
    ones, zeros = jnp.ones((T, MLA_NOPE), F32), jnp.zeros((T, MLA_NOPE), F32)
    tail1, tail0 = jnp.ones((T, HEAD_LANES - MLA_NOPE - MLA_ROPE), F32), jnp.zeros((T, HEAD_LANES - MLA_NOPE - MLA_ROPE), F32)
    zh = jnp.zeros((T, half), F32)
    c = jnp.concatenate([ones, cos, cos, tail1], axis=1)
    s_up = jnp.concatenate([zeros, -sin, zh, tail0], axis=1)
    s_down = jnp.concatenate([zeros, zh, sin, tail0], axis=1)
    return c, s_up, s_down


def _rope_lanes(x, c, s_up, s_down):
    half = MLA_ROPE // 2
    return x * c + pltpu.roll(x, HEAD_LANES - half, 1) * s_up + pltpu.roll(x, half, 1) * s_down


def _unrope_lanes(d, c, s_up, s_down):
    half = MLA_ROPE // 2
    return d * c + pltpu.roll(d * s_up, half, 1) + pltpu.roll(d * s_down, HEAD_LANES - half, 1)


def _mla_operands(q_ref, kv_ref, kpe_ref, c, s_up, s_down):
    lane = lax.broadcasted_iota(jnp.int32, kv_ref.shape, 1)
    qr = (_rope_lanes(q_ref[...].astype(F32), c, s_up, s_down) * (_MLA_SCALE * _LOG2E)).astype(MXU_DTYPE)
    kr = jnp.where(lane < MLA_NOPE, kv_ref[...].astype(F32), _rope_lanes(kpe_ref[...], c, s_up, s_down)).astype(MXU_DTYPE)
    return qr, kr, lane


def _mla_specs(Tp):
    head = pl.BlockSpec((None, Tp, HEAD_LANES), lambda b, h: (b, 0, h))
    shared = pl.BlockSpec((None, Tp, HEAD_LANES), lambda b, h: (b, 0, 0))
    tab = pl.BlockSpec((Tp, HEAD_LANES), lambda b, h: (0, 0))
    lse = pl.BlockSpec((None, None, Tp, 1), lambda b, h: (b, h, 0, 0))
    out = pl.BlockSpec((None, Tp, HEAD_LANES), lambda b, h: (b, 0, LRU_WIDTH // HEAD_LANES + h))
    return head, shared, tab, lse, out


def _attn_fwd_call(q, kv, kpe, tabs, y):
    B, Tp, _ = q.shape

    def body(q_ref, kv_ref, kpe_ref, c_ref, su_ref, sd_ref, y_ref, o_ref, lse_ref, qr_ref, kr_ref):
        qr, kr, lane = _mla_operands(q_ref, kv_ref, kpe_ref, c_ref[...], su_ref[...], sd_ref[...])
        qr_ref[...] = qr
        kr_ref[...] = kr
        for r0, L in _query_blocks(Tp):
            blk = slice(r0, L)
            s = _mask_diagonal(lax.dot_general(qr_ref[blk, :], kr_ref[0:L, :], _NT, preferred_element_type=F32), NEG_INF)
            m = jnp.max(s, axis=-1, keepdims=True)
            p = jnp.exp2(s - m)
            l = jnp.sum(p, axis=-1, keepdims=True)
            o = jnp.dot(p.astype(MXU_DTYPE), kv_ref[0:L, :].astype(MXU_DTYPE), preferred_element_type=F32)
            o_ref[blk, :] = jnp.where(lane[blk, :] >= MLA_NOPE, o / l, 0.0)
            lse_ref[blk, :] = m + jnp.log2(l)

    head, shared, tab, lse, out = _mla_specs(Tp)
    return pl.pallas_call(
        body, name="mla_attn_fwd", grid=(B, MLA_HEADS),
        in_specs=[head, head, shared, tab, tab, tab, pl.BlockSpec(memory_space=pl.ANY)], out_specs=[out, lse],
        out_shape=[jax.ShapeDtypeStruct(y.shape, F32), jax.ShapeDtypeStruct((B, MLA_HEADS, Tp, 1), F32)],
        input_output_aliases={6: 0},
        scratch_shapes=[pltpu.VMEM((Tp, HEAD_LANES), MXU_DTYPE), pltpu.VMEM((Tp, HEAD_LANES), MXU_DTYPE)],
        compiler_params=pltpu.CompilerParams(dimension_semantics=("parallel", "parallel")),
    )(q, kv, kpe, *tabs, y)


def _attn_bwd_call(q, kv, kpe, tabs, o, lse, do):
    B, Tp, _ = q.shape

    def body(q_ref, kv_ref, kpe_ref, c_ref, su_ref, sd_ref, o_ref, lse_ref, do_ref, dq_ref, dkv_ref, dkpe_ref,
             qr_ref, kr_ref, dqa_ref, dka_ref, dva_ref):
        c, s_up, s_down = c_ref[...], su_ref[...], sd_ref[...]
        qr, kr, lane = _mla_operands(q_ref, kv_ref, kpe_ref, c, s_up, s_down)
        qr_ref[...] = qr
        kr_ref[...] = kr
        dka_ref[...] = jnp.zeros_like(dka_ref)
        dva_ref[...] = jnp.zeros_like(dva_ref)
        for r0, L in _query_blocks(Tp):
            blk = slice(r0, L)
            qb = qr_ref[blk, :]
            do = jnp.where(lane[blk, :] >= MLA_NOPE, do_ref[blk, :], 0.0)
            delta = jnp.sum(do * o_ref[blk, :], axis=-1, keepdims=True)
            s = _mask_diagonal(lax.dot_general(qb, kr_ref[0:L, :], _NT, preferred_element_type=F32), NEG_INF)
            p = jnp.exp2(s - lse_ref[blk, :])
            dob = do.astype(MXU_DTYPE)
            dva_ref[0:L, :] += lax.dot_general(p.astype(MXU_DTYPE), dob, _TN, preferred_element_type=F32)
            dp = lax.dot_general(dob, kv_ref[0:L, :].astype(MXU_DTYPE), _NT, preferred_element_type=F32)
            ds = (p * (dp - delta)).astype(MXU_DTYPE)
            dqa_ref[blk, :] = jnp.dot(ds, kr_ref[0:L, :], preferred_element_type=F32)
            dka_ref[0:L, :] += lax.dot_general(ds, qb, _TN, preferred_element_type=F32)
        dq_ref[...] = _unrope_lanes(dqa_ref[...] * _MLA_SCALE, c, s_up, s_down).astype(dq_ref.dtype)
        dk = dka_ref[...] * (1.0 / _LOG2E)
        dkv_ref[...] = jnp.where(lane < MLA_NOPE, dk, dva_ref[...]).astype(dkv_ref.dtype)
        dkpe = jnp.where(lane >= MLA_NOPE, _unrope_lanes(dk, c, s_up, s_down), 0.0)

        @pl.when(pl.program_id(1) == 0)
        def _():
            dkpe_ref[...] = dkpe

        @pl.when(pl.program_id(1) > 0)
        def _():
            dkpe_ref[...] += dkpe

    head, shared, tab, lse_spec, out = _mla_specs(Tp)
    wide = jax.ShapeDtypeStruct((B, Tp, MLA_HEADS * HEAD_LANES), q.dtype)
    acc = pltpu.VMEM((Tp, HEAD_LANES), F32)
    return pl.pallas_call(
        body, name="mla_attn_bwd", grid=(B, MLA_HEADS),
        in_specs=[head, head, shared, tab, tab, tab, out, lse_spec, out], out_specs=[head, head, shared],
        out_shape=[wide, wide, jax.ShapeDtypeStruct((B, Tp, HEAD_LANES), F32)],
        scratch_shapes=[pltpu.VMEM((Tp, HEAD_LANES), MXU_DTYPE), pltpu.VMEM((Tp, HEAD_LANES), MXU_DTYPE), acc, acc, acc],
        compiler_params=pltpu.CompilerParams(dimension_semantics=("parallel", "arbitrary")),
    )(q, kv, kpe, *tabs, o, lse, do)


@jax.custom_vjp
def mla_attention(q, kv, kpe, tabs, y):
    return _attn_fwd_call(q, kv, kpe, tabs, y)[0]


def _mla_attention_fwd(q, kv, kpe, tabs, y):
    o, lse = _attn_fwd_call(q, kv, kpe, tabs, y)
    return o, (q, kv, kpe, tabs, o, lse)


def _mla_attention_bwd(res, do):
    q, kv, kpe, tabs, o, lse = res
    dq, dkv, dkpe = _attn_bwd_call(q, kv, kpe, tabs, o, lse, do)
    return dq, dkv, dkpe, None, do


mla_attention.defvjp(_mla_attention_fwd, _mla_attention_bwd)


def _rope_halves(x, cos, sin):
    half = x.shape[1] // 2
    x1, x2 = x[:, :half], x[:, half:]
    return jnp.concatenate([x1 * cos - x2 * sin, x1 * sin + x2 * cos], axis=1)


def _unrope_halves(d, cos, sin):
    half = d.shape[1] // 2
    d1, d2 = d[:, :half], d[:, half:]
    return jnp.concatenate([d1 * cos + d2 * sin, d2 * cos - d1 * sin], axis=1)


_RET_K_SCALE = RET_QK_DIM ** -0.5
_RET_Q_BLOCKS = RET_HEADS
_RET_V_BLOCK0 = 2 * RET_HEADS * RET_QK_DIM // RET_V_DIM
_RET_G_BLOCK0 = _RET_V_BLOCK0 + RET_HEADS


def _ret_specs(Tp):
    q = pl.BlockSpec((None, Tp, RET_QK_DIM), lambda b, h: (b, 0, h))
    k = pl.BlockSpec((None, Tp, RET_QK_DIM), lambda b, h: (b, 0, _RET_Q_BLOCKS + h))
    v = pl.BlockSpec((None, Tp, RET_V_DIM), lambda b, h: (b, 0, _RET_V_BLOCK0 + h))
    tab = pl.BlockSpec((Tp, RET_QK_DIM // 2), lambda b, h: (0, 0))
    lg = pl.BlockSpec((None, 1, 1), lambda b, h: (h, 0, 0))
    return q, k, v, tab, lg


def _ret_operands(q_ref, k_ref, cos, sin, lg):
    t = lax.broadcasted_iota(jnp.int32, (q_ref.shape[0], 1), 0).astype(F32)
    grow, shrink = jnp.exp(-lg * t), jnp.exp(lg * t)
    qs = (_rope_halves(q_ref[...].astype(F32), cos, sin) * shrink).astype(MXU_DTYPE)
    ks = (_rope_halves(k_ref[...].astype(F32), cos, sin) * (grow * _RET_K_SCALE)).astype(MXU_DTYPE)
    return qs, ks, shrink, grow * _RET_K_SCALE


def _ret_core_fwd_call(p, cos, sin, lg):
    B, Tp, _ = p.shape

    def body(q_ref, k_ref, v_ref, cos_ref, sin_ref, lg_ref, o_ref, qs_ref, ks_ref):
        qs_ref[...], ks_ref[...], _, _ = _ret_operands(q_ref, k_ref, cos_ref[...], sin_ref[...], lg_ref[...])
        for r0, L in _query_blocks(Tp):
            blk = slice(r0, L)
            s = _mask_diagonal(lax.dot_general(qs_ref[blk, :], ks_ref[0:L, :], _NT, preferred_element_type=F32), 0.0)
            o_ref[blk, :] = jnp.dot(s.astype(MXU_DTYPE), v_ref[0:L, :].astype(MXU_DTYPE), preferred_element_type=F32)

    q, k, v, tab, lgs = _ret_specs(Tp)
    return pl.pallas_call(
        body, name="retention_fwd", grid=(B, RET_HEADS), in_specs=[q, k, v, tab, tab, lgs],
        out_specs=pl.BlockSpec((None, Tp, RET_V_DIM), lambda b, h: (b, 0, h)),
        out_shape=jax.ShapeDtypeStruct((B, Tp, RET_HEADS * RET_V_DIM), F32),
        scratch_shapes=[pltpu.VMEM((Tp, RET_QK_DIM), MXU_DTYPE), pltpu.VMEM((Tp, RET_QK_DIM), MXU_DTYPE)],
        compiler_params=pltpu.CompilerParams(dimension_semantics=("parallel", "parallel")),
    )(p, p, p, cos, sin, lg)


def _ret_core_bwd_call(p, do, cos, sin, lg):
    B, Tp, _ = p.shape

    def body(q_ref, k_ref, v_ref, do_ref, cos_ref, sin_ref, lg_ref, dq_ref, dk_ref, dv_ref, qs_ref, ks_ref, dqa_ref, dka_ref, dva_ref):
        cos_, sin_ = cos_ref[...], sin_ref[...]
        qs_ref[...], ks_ref[...], q_scale, k_scale = _ret_operands(q_ref, k_ref, cos_, sin_, lg_ref[...])
        dka_ref[...] = jnp.zeros_like(dka_ref)
        dva_ref[...] = jnp.zeros_like(dva_ref)
        for r0, L in _query_blocks(Tp):
            blk = slice(r0, L)
            qb = qs_ref[blk, :]
            dob = do_ref[blk, :].astype(MXU_DTYPE)
            s = _mask_diagonal(lax.dot_general(qb, ks_ref[0:L, :], _NT, preferred_element_type=F32), 0.0).astype(MXU_DTYPE)
            dva_ref[0:L, :] += lax.dot_general(s, dob, _TN, preferred_element_type=F32)
            ds = _mask_diagonal(lax.dot_general(dob, v_ref[0:L, :].astype(MXU_DTYPE), _NT, preferred_element_type=F32), 0.0).astype(MXU_DTYPE)
            dqa_ref[blk, :] = jnp.dot(ds, ks_ref[0:L, :], preferred_element_type=F32)
            dka_ref[0:L, :] += lax.dot_general(ds, qb, _TN, preferred_element_type=F32)
        dq_ref[...] = _unrope_halves(dqa_ref[...] * q_scale, cos_, sin_).astype(dq_ref.dtype)
        dk_ref[...] = _unrope_halves(dka_ref[...] * k_scale, cos_, sin_).astype(dk_ref.dtype)
        dv_ref[...] = dva_ref[...].astype(dv_ref.dtype)

    q, k, v, tab, lgs = _ret_specs(Tp)
    qk_out = pl.BlockSpec((None, Tp, RET_QK_DIM), lambda b, h: (b, 0, h))
    v_out = pl.BlockSpec((None, Tp, RET_V_DIM), lambda b, h: (b, 0, h))
    return pl.pallas_call(
        body, name="retention_bwd", grid=(B, RET_HEADS), in_specs=[q, k, v, v_out, tab, tab, lgs],
        out_specs=[qk_out, qk_out, v_out],
        out_shape=[jax.ShapeDtypeStruct((B, Tp, RET_HEADS * RET_QK_DIM), p.dtype), jax.ShapeDtypeStruct((B, Tp, RET_HEADS * RET_QK_DIM), p.dtype),
                   jax.ShapeDtypeStruct((B, Tp, RET_HEADS * RET_V_DIM), p.dtype)],
        scratch_shapes=[pltpu.VMEM((Tp, RET_QK_DIM), MXU_DTYPE), pltpu.VMEM((Tp, RET_QK_DIM), MXU_DTYPE),
                        pltpu.VMEM((Tp, RET_QK_DIM), F32), pltpu.VMEM((Tp, RET_QK_DIM), F32), pltpu.VMEM((Tp, RET_V_DIM), F32)],
        compiler_params=pltpu.CompilerParams(dimension_semantics=("parallel", "parallel")),
    )(p, p, p, do, cos, sin, lg)


def _ret_gate_specs(M):
    tm = _pick(M, 1088, 8)
    head = pl.BlockSpec((tm, RET_V_DIM), lambda i, h: (i, h))
    gate = pl.BlockSpec((tm, RET_V_DIM), lambda i, h: (i, _RET_G_BLOCK0 + h))
    return tm, head, gate


def _ret_gate_fwd_call(o, p2d):
    M = o.shape[0]
    tm, head, gate = _ret_gate_specs(M)

    def body(o_ref, g_ref, y_ref):
        ov = o_ref[...]
        gv = g_ref[...].astype(F32)
        rstd = lax.rsqrt(jnp.mean(ov * ov, axis=-1, keepdims=True) + EPS)
        y_ref[...] = (gv * _sigmoid(gv)) * (ov * rstd)

    return pl.pallas_call(
        body, name="retention_gate_fwd", grid=(M // tm, RET_HEADS), in_specs=[head, gate], out_specs=head,
        out_shape=jax.ShapeDtypeStruct(o.shape, F32),
        compiler_params=pltpu.CompilerParams(dimension_semantics=("parallel", "parallel")),
    )(o, p2d)


def _ret_gate_bwd_call(o, p2d, dy):
    M = o.shape[0]
    tm, head, gate = _ret_gate_specs(M)

    def body(o_ref, g_ref, dy_ref, do_ref, dg_ref):
        ov = o_ref[...]
        gv = g_ref[...].astype(F32)
        dy = dy_ref[...]
        rstd = lax.rsqrt(jnp.mean(ov * ov, axis=-1, keepdims=True) + EPS)
        on = ov * rstd
        sg = _sigmoid(gv)
        dg_ref[...] = (dy * on * (sg * (1.0 + gv * (1.0 - sg)))).astype(dg_ref.dtype)
        don = dy * (gv * sg)
        do_ref[...] = (rstd * (don - on * jnp.mean(don * on, axis=-1, keepdims=True))).astype(do_ref.dtype)

    shp = jax.ShapeDtypeStruct(o.shape, p2d.dtype)
    return pl.pallas_call(
        body, name="retention_gate_bwd", grid=(M // tm, RET_HEADS), in_specs=[head, gate, head], out_specs=[head, head],
        out_shape=[shp, shp],
        compiler_params=pltpu.CompilerParams(dimension_semantics=("parallel", "parallel")),
    )(o, p2d, dy)


def _log_gamma():
    return jnp.log(1.0 - 2.0 ** (-5.0 - jnp.arange(RET_HEADS, dtype=F32))).reshape(RET_HEADS, 1, 1)


@functools.partial(jax.custom_vjp, nondiff_argnums=(9,))
def retention_block(h, w_in, w_out, w_in_grad_slot, w_out_grad_slot, g, b, cos, sin, dims):
    return _retention_block_fwd(h, w_in, w_out, w_in_grad_slot, w_out_grad_slot, g, b, cos, sin, dims)[0]


def _retention_block_fwd(h, w_in, w_out, w_in_grad_slot, w_out_grad_slot, g, b, cos, sin, dims):
    B, Tp = dims
    p = _mm_nn(h, w_in, False, "od_w_in_fwd", out_dtype=MXU_DTYPE)
    o = _ret_core_fwd_call(p.reshape(B, Tp, -1), cos, sin, _log_gamma())
    y = _ret_gate_fwd_call(o.reshape(B * Tp, -1), p)
    out, z = _mm_nn(y, w_out, False, "od_w_out_norm_fwd", norm=(h, g, b))
    return out, (h, p, o, y, z, w_in, w_out, g, cos, sin, jnp.zeros((), w_in_grad_slot.dtype))


def _retention_block_bwd(dims, res, dout):
    B, Tp = dims
    h, p, o, y, z, w_in, w_out, g, cos, sin, slot_like = res
    dz, dg, db = _ln_bwd_call(z, g, dout, "od_w_out_norm_bwd")
    dy = _mm_nt(dz, w_out, None, "od_w_out_dx")
    dw_out = _mm_tn(y, dz, False, "od_w_out_dw", 1, slot_like.dtype)
    do, dgate = _ret_gate_bwd_call(o.reshape(B * Tp, -1), p, dy)
    dq, dk, dv = _ret_core_bwd_call(p.reshape(B, Tp, -1), do.reshape(B, Tp, -1), cos, sin, _log_gamma())
    dp = jnp.concatenate([dq.reshape(B * Tp, -1), dk.reshape(B * Tp, -1), dv.reshape(B * Tp, -1), dgate], axis=-1)
    dh = _mm_nt(dp, w_in, None, "od_w_in_dx", plus=dz)
    dw_in = _mm_tn(h, dp, False, "od_w_in_dw", N_CHIPS, slot_like.dtype)
    return dh, None, None, dw_in, dw_out, dg.reshape(g.shape), db.reshape(g.shape), None, None


retention_block.defvjp(_retention_block_fwd, _retention_block_bwd)


def _heads_to_lanes(w):
    K = w.shape[0]
    w = w.reshape(K, MLA_HEADS, MLA_NOPE + MLA_ROPE)
    return jnp.pad(w, ((0, 0), (0, 0), (0, HEAD_LANES - MLA_NOPE - MLA_ROPE))).reshape(K, MLA_HEADS * HEAD_LANES)


def _out_rows_to_lanes(w):
    N = w.shape[1]
    att = w[LRU_WIDTH:].reshape(MLA_HEADS, MLA_V, N)
    att = jnp.pad(att, ((0, 0), (HEAD_LANES - MLA_V, 0), (0, 0))).reshape(MLA_HEADS * HEAD_LANES, N)
    return jnp.concatenate([w[:LRU_WIDTH], att], axis=0)


def _seq_dims(x):
    B, S, D = x.shape
    T = S + N_META
    Tp = _round_up(T, SEQ_BLOCK)
    return B, S, T, Tp


def _mixer0(diff, w, token):
    x = diff["x"]
    B, S, T, Tp = _seq_dims(x)
    D = x.shape[-1]
    M = B * Tp
    pos = jnp.arange(Tp, dtype=jnp.int32)

    def mm(a, name, act=False, out_dtype=F32, layout=lambda m: m, col_shards=1):
        return matmul(a, layout(w[name]), layout(diff[name]), act, name, out_dtype, col_shards)

    meta = jnp.broadcast_to((diff["meta_tokens"] + token)[None], (B, N_META, D))
    h = jnp.concatenate([meta, x, jnp.zeros((B, Tp - T, D), F32)], axis=1).reshape(M, D)
    p = mm(h, "ev_w_in")
    sp = jax.nn.softplus(-diff["ev_lru_lambda"]).reshape(1, LRU_WIDTH)
    y, qn, kvn, kpe = even_front(
        p.reshape(B, Tp, -1), diff["ev_conv_w"].reshape(CONV_WIDTH, LRU_WIDTH), diff["ev_conv_b"].reshape(1, LRU_WIDTH),
        diff["ev_w_rg_a"].reshape(LRU_HEADS, LRU_HEAD_DIM, LRU_HEAD_DIM), diff["ev_b_rg_a"].reshape(1, LRU_WIDTH),
        diff["ev_w_rg_x"].reshape(LRU_HEADS, LRU_HEAD_DIM, LRU_HEAD_DIM), diff["ev_b_rg_x"].reshape(1, LRU_WIDTH),
        sp, diff["ev_q_norm_g"].reshape(-1), diff["ev_kv_norm_g"].reshape(-1))
    q = mm(qn, "ev_w_uq", out_dtype=MXU_DTYPE, layout=_heads_to_lanes).reshape(B, Tp, -1)
    kv = mm(kvn, "ev_w_ukv", out_dtype=MXU_DTYPE).reshape(B, Tp, -1)
    y = mla_attention(q, kv, kpe, _mla_rope_tables(pos), y).reshape(M, -1)
    return out_block(h, y, _out_rows_to_lanes(w["ev_w_out"]), _out_rows_to_lanes(diff["ev_w_out"]),
                     diff["ln_mix_g"], diff["ln_mix_b"], "ev_w_out")


def _mlp0(diff, h, w):
    return mlp_block(h, w["mlp_w1_0"], w["mlp_w2_0"], diff["mlp_w1_0"], diff["mlp_w2_0"], diff["ln_mlp_g"], diff["ln_mlp_b"], "mlp0")


def _layer1_loss(diff, h, w, tgt):
    B, S, T, Tp = _seq_dims(tgt)
    D = tgt.shape[-1]
    pos = jnp.arange(Tp, dtype=jnp.int32)

    cos, sin = _rope_tables(pos, RET_QK_DIM // 2)
    h = retention_block(h, w["od_w_in"], w["od_w_out"], diff["od_w_in"], diff["od_w_out"], diff["ln_mix_g"], diff["ln_mix_b"], cos, sin, (B, Tp))
    h = mlp_block(h, w["mlp_w1_1"], w["mlp_w2_1"], diff["mlp_w1_1"], diff["mlp_w2_1"], diff["ln_mlp_g"], diff["ln_mlp_b"], "mlp1")
    return loss_head(h.reshape(B, Tp, D), jnp.pad(tgt, ((0, 0), (N_META, Tp - T), (0, 0))), S)


_HBM = pl.BlockSpec(memory_space=pltpu.HBM)


def _place():
    return lax.axis_index("x"), lax.axis_index("y"), lax.axis_index("c")


def _other_chips(x, y):
    return [(1 - x, y), (x, 1 - y), (1 - x, 1 - y)]


def _chunks(rows, sublanes, most):
    for q in range(most, 0, -1):
        if rows % (q * sublanes) == 0:
            return q
    return 1


def _sublanes(dtype):
    return 8 * 4 // jnp.dtype(dtype).itemsize


def _gather_pieces(bufs):
    plan, first = [], []
    for b in bufs:
        Rh = b.shape[0] // 2
        Q = _chunks(Rh, _sublanes(b.dtype), 4) if Rh * b.shape[1] * b.dtype.itemsize > (1 << 20) else 1
        first.append(3 * sum(q for _, q, _ in plan))
        plan.append((Rh, Q, Rh // Q))
    return plan, first, 3 * sum(q for _, q, _ in plan)


def _allgather_chips(bufs, name):
    n = len(bufs)
    plan, first, n_sems = _gather_pieces(bufs)

    def body(*refs):
        x_refs, out_refs, (send_sems, recv_sems) = refs[:n], refs[n:2 * n], refs[2 * n:]
        x, y, c = _place()
        sibling = (x, y, 1 - c)
        chips = _other_chips(x, y)

        def copy(k, src, dst, to):
            return pltpu.make_async_remote_copy(src_ref=src, dst_ref=dst, send_sem=send_sems.at[k], recv_sem=recv_sems.at[k],
                                                device_id=to, device_id_type=MESH)

        def piece(i, cx, cy, hc, q):
            Rh, _, ch = plan[i]
            return out_refs[i].at[2 * cx + cy, pl.ds(hc * Rh + q * ch, ch), :]

        slots = [(i, q, j) for i in range(n) for q in range(plan[i][1]) for j in range(3)]
        sem = {(i, q, j): first[i] + 3 * q + j for i, q, j in slots}
        sent = [copy(sem[i, q, j], x_refs[i].at[pl.ds(c * plan[i][0] + q * plan[i][2], plan[i][2]), :], piece(i, x, y, c, q), (*chips[j], c))
                for i, q, j in slots]
        for cp in sent:
            cp.start()
        passed = []
        for i, q, j in slots:
            landed = piece(i, *chips[j], c, q)
            copy(sem[i, q, j], landed, landed, sibling).wait_recv()
            fwd = copy(n_sems + sem[i, q, j], landed, landed, sibling)
            fwd.start()
            passed.append(fwd)
        for i, q, j in slots:
            theirs = piece(i, *chips[j], 1 - c, q)
            copy(n_sems + sem[i, q, j], theirs, theirs, sibling).wait_recv()
        for cp in sent + passed:
            cp.wait_send()

    return pl.pallas_call(
        body, name=name, in_specs=[_HBM] * n, out_specs=[_HBM] * n,
        out_shape=[jax.ShapeDtypeStruct((N_CHIPS,) + b.shape, b.dtype) for b in bufs],
        scratch_shapes=[pltpu.SemaphoreType.DMA((2 * n_sems,)), pltpu.SemaphoreType.DMA((2 * n_sems,))],
    )(*bufs)


def _with_own(gathered, own):
    my = 2 * lax.axis_index("x") + lax.axis_index("y")
    return lax.dynamic_update_slice(gathered, own[None], (my, 0, 0))


def _sibling_gather(fs, name):
    n = len(fs)

    def body(*refs):
        out_refs, (send_sems, recv_sems) = refs[n:2 * n], refs[2 * n:]
        x, y, c = _place()
        copies = [pltpu.make_async_remote_copy(src_ref=out_ref.at[c], dst_ref=out_ref.at[c], send_sem=send_sems.at[i], recv_sem=recv_sems.at[i],
                                               device_id=(x, y, 1 - c), device_id_type=MESH) for i, out_ref in enumerate(out_refs)]
        for cp in copies:
            cp.start()
        for cp in copies:
            cp.wait()

    return pl.pallas_call(
        body, name=name, in_specs=[_HBM] * n, out_specs=[_HBM] * n,
        out_shape=[jax.ShapeDtypeStruct(f.shape, f.dtype) for f in fs], input_output_aliases={i: i for i in range(n)},
        scratch_shapes=[pltpu.SemaphoreType.DMA((n,)), pltpu.SemaphoreType.DMA((n,))],
    )(*fs)


def _axis_scalar(name):
    return lax.axis_index(name).astype(jnp.int32).reshape(1)


_SEM = pl.BlockSpec(memory_space=pltpu.SEMAPHORE)
_ANY = pl.BlockSpec(memory_space=pl.ANY)
_EFFECT = pltpu.SideEffectType.DATAFLOW_SIDE_EFFECTING


def _in_hbm(a):
    return pltpu.with_memory_space_constraint(a, pltpu.HBM)


def _half_copies(x_refs, land_refs, send_sems, recv_sems, arriving):
    x, y, c = _place()
    copies = []
    for i, (x_ref, land_ref) in enumerate(zip(x_refs, land_refs)):
        Rh = x_ref.shape[0] // 2
        rows = pl.ds(c * Rh, Rh)
        for j, (cx, cy) in enumerate(_other_chips(x, y)):
            copies.append(pltpu.make_async_remote_copy(
                src_ref=x_ref.at[rows, :], dst_ref=land_ref.at[2 * cx + cy if arriving else 2 * x + y, rows, :],
                send_sem=send_sems.at[3 * i + j], recv_sem=recv_sems.at[3 * i + j], device_id=(cx, cy, c), device_id_type=MESH))
    return copies


def _allgather_start(bufs, name):
    n = len(bufs)

    def body(*refs):
        x_refs, land_refs, (send_sems, recv_sems), token = refs[:n], refs[n:2 * n], refs[2 * n:2 * n + 2], refs[-1]
        for cp in _half_copies(x_refs, land_refs, send_sems, recv_sems, False):
            cp.start()
        token[...] = jnp.zeros_like(token)

    lands = [lax.empty((N_CHIPS,) + b.shape, b.dtype) for b in bufs]
    out = pl.pallas_call(
        body, name=name,
        out_shape=(pltpu.SemaphoreType.DMA((3 * n,)), pltpu.SemaphoreType.DMA((3 * n,)), *[pltpu.HBM(a.shape, a.dtype) for a in bufs + lands],
                   jax.ShapeDtypeStruct((8, 128), F32)),
        in_specs=[_HBM] * (2 * n), out_specs=(_SEM, _SEM, *[_HBM] * (2 * n), pl.BlockSpec(memory_space=pltpu.VMEM)),
        input_output_aliases={i: 2 + i for i in range(2 * n)}, compiler_params=pltpu.CompilerParams(has_side_effects=_EFFECT),
    )(*[_in_hbm(a) for a in bufs + lands])
    return (out[0], out[1], list(out[2:2 + n]), list(out[2 + n:2 + 2 * n])), out[-1][0, 0]


def _allgather_wait(pending, after, name):
    send_sems, recv_sems, bufs, lands = pending
    n = len(bufs)

    def body(*refs):
        x_refs, land_refs, send_sems, recv_sems = refs[:n], refs[n:2 * n], refs[2 * n], refs[2 * n + 1]
        for cp in _half_copies(x_refs, land_refs, send_sems, recv_sems, False):
            cp.wait_send()
        for cp in _half_copies(x_refs, land_refs, send_sems, recv_sems, True):
            cp.wait_recv()

    out = pl.pallas_call(
        body, name=name, out_shape=tuple(pltpu.HBM(a.shape, a.dtype) for a in bufs + lands),
        in_specs=[_HBM] * (2 * n) + [_SEM, _SEM, _ANY], out_specs=tuple([_HBM] * (2 * n)), input_output_aliases={i: i for i in range(2 * n)},
        compiler_params=pltpu.CompilerParams(has_side_effects=_EFFECT),
    )(*bufs, *lands, send_sems, recv_sems, after)
    return list(out[n:])


def _sibling_forward(lands, name):
    n = len(lands)
    plan, first, n_sems = _gather_pieces([jax.ShapeDtypeStruct(l.shape[1:], l.dtype) for l in lands])

    def body(*refs):
        out_refs, (send_sems, recv_sems) = refs[n:2 * n], refs[2 * n:]
        x, y, c = _place()

        def copies(hc):
            return [pltpu.make_async_remote_copy(
                        src_ref=out_refs[i].at[2 * cx + cy, pl.ds(hc * plan[i][0] + q * plan[i][2], plan[i][2]), :],
                        dst_ref=out_refs[i].at[2 * cx + cy, pl.ds(hc * plan[i][0] + q * plan[i][2], plan[i][2]), :],
                        send_sem=send_sems.at[first[i] + 3 * q + j], recv_sem=recv_sems.at[first[i] + 3 * q + j],
                        device_id=(x, y, 1 - c), device_id_type=MESH)
                    for i in range(n) for q in range(plan[i][1]) for j, (cx, cy) in enumerate(_other_chips(x, y))]

        mine = copies(c)
        for cp in mine:
            cp.start()
        for cp in mine:
            cp.wait_send()
        for cp in copies(1 - c):
            cp.wait_recv()

    return pl.pallas_call(
        body, name=name, in_specs=[_HBM] * n, out_specs=[_HBM] * n, out_shape=[jax.ShapeDtypeStruct(l.shape, l.dtype) for l in lands],
        input_output_aliases={i: i for i in range(n)},
        scratch_shapes=[pltpu.SemaphoreType.DMA((n_sems,)), pltpu.SemaphoreType.DMA((n_sems,))],
    )(*lands)


N_PEERS = 7


def _direct_copies(p_refs, t_refs, send_sems, recv_sems):
    x, y, c = _place()
    copies = []
    for i, (p_ref, t_ref) in enumerate(zip(p_refs, t_refs)):
        for f in range(1, N_PEERS + 1):
            px, py, pc = x ^ (f >> 2), y ^ ((f >> 1) & 1), c ^ (f & 1)
            copies.append(pltpu.make_async_remote_copy(
                src_ref=p_ref.at[2 * px + py, pc], dst_ref=t_ref.at[f - 1], send_sem=send_sems.at[N_PEERS * i + f - 1],
                recv_sem=recv_sems.at[N_PEERS * i + f - 1], device_id=(px, py, pc), device_id_type=MESH))
    return copies


def _direct_scatter_start(ps, name, carried=()):
    n, m = len(ps), 2 * len(ps) + len(carried)

    def body(*refs):
        p_refs, t_refs, (send_sems, recv_sems) = refs[:n], refs[n:2 * n], refs[m:m + 2]
        for cp in _direct_copies(p_refs, t_refs, send_sems, recv_sems):
            cp.start()

    lands = [lax.empty((N_PEERS,) + p.shape[2:], p.dtype) for p in ps]
    through = ps + lands + list(carried)
    out = pl.pallas_call(
        body, name=name,
        out_shape=(pltpu.SemaphoreType.DMA((N_PEERS * n,)), pltpu.SemaphoreType.DMA((N_PEERS * n,)),
                   *[pltpu.HBM(a.shape, a.dtype) for a in through]),
        in_specs=[_HBM] * m, out_specs=(_SEM, _SEM, *[_HBM] * m),
        input_output_aliases={i: 2 + i for i in range(m)}, compiler_params=pltpu.CompilerParams(has_side_effects=_EFFECT),
    )(*[_in_hbm(a) for a in through])
    return (out[0], out[1], list(out[2:2 + n]), list(out[2 + n:2 + 2 * n])), list(out[2 + 2 * n:])


def _direct_scatter_wait(pending, after, name):
    send_sems, recv_sems, ps, lands = pending
    n = len(ps)

    def body(*refs):
        p_refs, t_refs, send_sems, recv_sems = refs[:n], refs[n:2 * n], refs[2 * n], refs[2 * n + 1]
        for cp in _direct_copies(p_refs, t_refs, send_sems, recv_sems):
            cp.wait_send()
            cp.wait_recv()

    out = pl.pallas_call(
        body, name=name, out_shape=tuple(pltpu.HBM(a.shape, a.dtype) for a in ps + lands),
        in_specs=[_HBM] * (2 * n) + [_SEM, _SEM] + [_ANY] * len(after), out_specs=tuple([_HBM] * (2 * n)),
        input_output_aliases={i: i for i in range(2 * n)}, compiler_params=pltpu.CompilerParams(has_side_effects=_EFFECT),
    )(*ps, *lands, send_sems, recv_sems, *after)
    return list(out[:n]), list(out[n:])


def _sum_direct(p, t, name):
    _, _, R, C = p.shape
    tr = _pick(R, 512, 16)

    def body(x_ref, y_ref, c_ref, p_ref, t_ref, o_ref):
        acc = p_ref[...].astype(F32)
        for f in range(N_PEERS):
            acc = acc + t_ref[f].astype(F32)
        o_ref[...] = acc

    grid_spec = pltpu.PrefetchScalarGridSpec(
        num_scalar_prefetch=3, grid=(R // tr,),
        in_specs=[pl.BlockSpec((None, None, tr, C), lambda i, x_ref, y_ref, c_ref: (2 * x_ref[0] + y_ref[0], c_ref[0], i, 0)),
                  pl.BlockSpec((N_PEERS, tr, C), lambda i, x_ref, y_ref, c_ref: (0, i, 0))],
        out_specs=pl.BlockSpec((None, tr, C), lambda i, x_ref, y_ref, c_ref: (c_ref[0], i, 0)))
    return pl.pallas_call(body, name=name, grid_spec=grid_spec, out_shape=jax.ShapeDtypeStruct((2, R, C), F32),
                          compiler_params=pltpu.CompilerParams(dimension_semantics=("parallel",)))(
        _axis_scalar("x"), _axis_scalar("y"), _axis_scalar("c"), p, t)


def _adamw(w, g, m, v, name):
    R, C = w.shape
    tr = _pick(R, 256, 8)

    def body(w_ref, g_ref, m_ref, v_ref, d_ref, nm_ref, nv_ref):
        g_ = g_ref[...]
        m_ = ADAM_B1 * m_ref[...] + (1.0 - ADAM_B1) * g_
        v_ = ADAM_B2 * v_ref[...] + (1.0 - ADAM_B2) * (g_ * g_)
        m_hat = m_ / (1.0 - ADAM_B1 ** ADAM_STEP)
        v_hat = v_ / (1.0 - ADAM_B2 ** ADAM_STEP)
        d_ref[...] = -ADAM_LR * (m_hat / (jnp.sqrt(v_hat) + ADAM_EPS) + ADAM_WD * w_ref[...])
        nm_ref[...] = m_
        nv_ref[...] = v_

    row = pl.BlockSpec((tr, C), lambda i: (i, 0))
    shp = jax.ShapeDtypeStruct((R, C), F32)
    return pl.pallas_call(body, name=name, grid=(R // tr,), in_specs=[row] * 4, out_specs=[row] * 3, out_shape=[shp] * 3,
                          compiler_params=pltpu.CompilerParams(dimension_semantics=("parallel",)))(w, g, m, v)


BIG_SPECS = (("ev_w_in", 1024, 1440, 1), ("ev_w_uq", 256, 768, 1), ("ev_w_ukv", 128, 1024, 1), ("ev_w_out", 1024, 1024, 0),
             ("od_w_in", 1024, 6144, 1), ("od_w_out", 2048, 1024, 0), ("mlp_w1_0", 1024, 4096, 1), ("mlp_w1_1", 1024, 4096, 1),
             ("mlp_w2_0", 4096, 1024, 0), ("mlp_w2_1", 4096, 1024, 0))
BIG_PARAMS = (("ev_w_in", ("ev_w_in",)), ("ev_w_uq", ("ev_w_uq",)), ("ev_w_ukv", ("ev_w_ukv",)), ("ev_w_out", ("ev_w_out",)),
              ("od_w_in", ("od_w_in",)), ("od_w_out", ("od_w_out",)), ("mlp_w1", ("mlp_w1_0", "mlp_w1_1")),
              ("mlp_w2", ("mlp_w2_0", "mlp_w2_1")))
REPLICATED = ("ev_conv_b", "ev_w_rg_a", "ev_b_rg_a", "ev_w_rg_x", "ev_b_rg_x", "ev_lru_lambda", "ev_q_norm_g", "ev_kv_norm_g",
              "ln_mix_g", "ln_mix_b", "ln_mlp_g", "ln_mlp_b")
SMALL_SHARDED = ("meta_tokens", "ev_conv_w")
COL_SHARD_GRADS = ("od_w_in", "mlp_w1_0", "mlp_w1_1")
MATRIX_GROUPS = (("ev_w_in", "ev_w_uq", "ev_w_ukv", "ev_w_out"), ("mlp_w1_0", "mlp_w2_0"), ("od_w_in", "od_w_out", "mlp_w1_1", "mlp_w2_1"))
LAYER_NORMS = ("ln_mix_g", "ln_mix_b", "ln_mlp_g", "ln_mlp_b")
WEIGHT_NAMES = ("meta_tokens", "ev_w_in", "ev_conv_w", "ev_conv_b", "ev_w_rg_a", "ev_b_rg_a", "ev_w_rg_x", "ev_b_rg_x",
                "ev_lru_lambda", "ev_q_norm_g", "ev_w_uq", "ev_kv_norm_g", "ev_w_ukv", "ev_w_out", "od_w_in", "od_w_out",
                "ln_mix_g", "ln_mix_b", "mlp_w1", "mlp_w2", "ln_mlp_g", "ln_mlp_b")


def _to_rows(flat, row_align):
    n = flat.shape[-1]
    rows = _round_up(-(-n // PACK_COLS), row_align)
    pad = rows * PACK_COLS - n
    if pad:
        flat = jnp.pad(flat, [(0, 0)] * (flat.ndim - 1) + [(0, pad)])
    return flat.reshape(flat.shape[:-1] + (rows, PACK_COLS))


def _shard_shape(K, N, axis):
    return (K // N_CHIPS, N) if axis == 0 else (K, N // N_CHIPS)


def _gather_shards(stacked, K, N, axis):
    if axis == 0:
        return stacked.reshape(K, N)
    return stacked.transpose(1, 0, 2).reshape(K, N)


def _split_shards(full, K, N, axis):
    if axis == 0:
        return full.reshape(N_CHIPS, -1)
    return full.reshape(K, N_CHIPS, N // N_CHIPS).transpose(1, 0, 2).reshape(N_CHIPS, -1)


def kernel(x, meta_tokens, ev_w_in, ev_conv_w, ev_conv_b, ev_w_rg_a, ev_b_rg_a, ev_w_rg_x, ev_b_rg_x, ev_lru_lambda, ev_q_norm_g, ev_w_uq, ev_kv_norm_g, ev_w_ukv, ev_w_out, od_w_in, od_w_out, ln_mix_g, ln_mix_b, mlp_w1, mlp_w2, ln_mlp_g, ln_mlp_b, loss_target, m_meta_tokens, m_ev_w_in, m_ev_conv_w, m_ev_conv_b, m_ev_w_rg_a, m_ev_b_rg_a, m_ev_w_rg_x, m_ev_b_rg_x, m_ev_lru_lambda, m_ev_q_norm_g, m_ev_w_uq, m_ev_kv_norm_g, m_ev_w_ukv, m_ev_w_out, m_od_w_in, m_od_w_out, m_ln_mix_g, m_ln_mix_b, m_mlp_w1, m_mlp_w2, m_ln_mlp_g, m_ln_mlp_b, v_meta_tokens, v_ev_w_in, v_ev_conv_w, v_ev_conv_b, v_ev_w_rg_a, v_ev_b_rg_a, v_ev_w_rg_x, v_ev_b_rg_x, v_ev_lru_lambda, v_ev_q_norm_g, v_ev_w_uq, v_ev_kv_norm_g, v_ev_w_ukv, v_ev_w_out, v_od_w_in, v_od_w_out, v_ln_mix_g, v_ln_mix_b, v_mlp_w1, v_mlp_w2, v_ln_mlp_g, v_ln_mlp_b):
    given = dict(locals())
    local_big = {"ev_w_in": ev_w_in[0], "ev_w_uq": ev_w_uq[0], "ev_w_ukv": ev_w_ukv[0], "ev_w_out": ev_w_out[0],
                 "od_w_in": od_w_in[0], "od_w_out": od_w_out[0], "mlp_w1_0": mlp_w1[0], "mlp_w1_1": mlp_w1[1],
                 "mlp_w2_0": mlp_w2[0], "mlp_w2_1": mlp_w2[1]}

    specs = {spec[0]: spec for spec in BIG_SPECS}
    mixer0_m, mlp0_m, layer1_m = MATRIX_GROUPS

    def shards(names):
        return [local_big[n].astype(MXU_DTYPE) for n in names]

    def whole(stacked, n):
        _, K, N, ax = specs[n]
        return stacked if n in COL_SHARD_GRADS else _gather_shards(stacked, K, N, ax)

    def filled(gathered, own, names):
        return {n: whole(_with_own(g_, o_), n) for n, g_, o_ in zip(names, gathered, own)}

    own_a, own_b, own_c = shards(mixer0_m), shards(mlp0_m), shards(layer1_m)
    small = [meta_tokens, jnp.pad(ev_conv_w[0], ((0, 16 - CONV_WIDTH), (0, 0)))]
    gathered_a = _allgather_chips(own_a + small, "weight_allgather_mixer0")
    pending_b, token1 = _allgather_start(own_b, "weight_allgather_mlp0_start")
    pending_c, token2 = _allgather_start(own_c, "weight_allgather_layer1_start")
    meta_full = _gather_shards(_with_own(gathered_a[-2], small[0]), N_META, D_MODEL, 1)
    conv_full = _gather_shards(_with_own(gathered_a[-1], small[1])[:, :CONV_WIDTH], CONV_WIDTH, LRU_WIDTH, 1)

    def slots(names, dtype):
        return {n: jnp.zeros((N_CHIPS, specs[n][1], specs[n][2] // N_CHIPS) if n in COL_SHARD_GRADS else specs[n][1:3], dtype) for n in names}

    def norms(names, layer):
        return {n: given[n][layer] for n in names}

    def finish_gather(pending, own, after, names, tag):
        landed = _allgather_wait(pending, lax.stop_gradient(after), "weight_allgather_%s_wait" % tag)
        return filled(_sibling_forward(landed, "weight_allgather_%s_forward" % tag), own, names)

    diff_a = {**slots(mixer0_m, MXU_DTYPE), **norms(("ln_mix_g", "ln_mix_b"), 0), **{n: given[n] for n in REPLICATED if n not in LAYER_NORMS},
              "x": x, "meta_tokens": meta_full, "ev_conv_w": conv_full}
    diff_b = {**slots(mlp0_m, MXU_DTYPE), **norms(("ln_mlp_g", "ln_mlp_b"), 0)}
    diff_c = {**slots(layer1_m, MXU_DTYPE), **norms(LAYER_NORMS, 1)}
    w_a = filled(gathered_a[:len(mixer0_m)], own_a, mixer0_m)
    h_a, back_a = jax.vjp(lambda d: _mixer0(d, w_a, token1 + token2), diff_a)
    w_b = finish_gather(pending_b, own_b, h_a, mlp0_m, "mlp0")
    h_b, back_b = jax.vjp(lambda d, hh: _mlp0(d, hh, w_b), diff_b, h_a)
    w_c = finish_gather(pending_c, own_c, h_b, layer1_m, "layer1")
    loss, back_c = jax.vjp(lambda d, hh: _layer1_loss(d, hh, w_c, loss_target), diff_c, h_b)
    loss = lax.psum(loss, ("x", "y", "c"))

    def blocks_of(grad, n):
        _, K, N, ax = specs[n]
        if n in COL_SHARD_GRADS:
            blocks = grad
        elif ax == 0:
            blocks = grad.reshape(N_CHIPS, K // N_CHIPS, N)
        else:
            blocks = grad.reshape(K, N_CHIPS, N // N_CHIPS).transpose(1, 0, 2)
        return blocks.reshape(N_CHIPS, 2, blocks.shape[1] // 2, blocks.shape[2])

    def start_reduce(grads_of, names, tag, dh):
        flying, (dh,) = _direct_scatter_start([blocks_of(grads_of[n], n) for n in names], "grad_scatter_%s_start" % tag, [dh])
        return flying, dh

    g_c, dh = back_c(jnp.ones((), F32))
    flying_c, dh = start_reduce(g_c, layer1_m, "layer1", dh)
    g_b, dh = back_b(dh)
    flying_b, dh = start_reduce(g_b, mlp0_m, "mlp0", dh)
    (g_a,) = back_a(dh)

    g = {**g_a, **g_b, **g_c}
    g.update({n: jnp.stack([(g_b if n in g_b else g_a)[n], g_c[n]]) for n in LAYER_NORMS})
    repl = jnp.concatenate([g[n].reshape(-1) for n in REPLICATED]).reshape(N_CHIPS, -1)
    small = [_split_shards(g["meta_tokens"], N_META, D_MODEL, 1), _split_shards(g["ev_conv_w"], CONV_WIDTH, LRU_WIDTH, 1), repl]
    small = [pc.reshape(N_CHIPS, 2, -1) for pc in small]
    n_small = sum(pc.shape[2] for pc in small)
    small.append(jnp.zeros((N_CHIPS, 2, _round_up(n_small, 32 * PACK_COLS) - n_small), F32))
    p_small = jnp.concatenate(small, axis=2).reshape(N_CHIPS, 2, -1, PACK_COLS)
    flying_a, _ = _direct_scatter_start([blocks_of(g_a[n], n) for n in mixer0_m] + [p_small], "grad_scatter_mixer0_start")
    started = [g_a["x"], flying_a[2][0]]
    ps_c, ts_c = _direct_scatter_wait(flying_c, started, "grad_scatter_layer1_wait")
    ps_b, ts_b = _direct_scatter_wait(flying_b, started, "grad_scatter_mlp0_wait")
    fs_bc = [_sum_direct(p, t, "grad_sum_%d" % i) for i, (p, t) in enumerate(zip(ps_b + ps_c, ts_b + ts_c))]
    red_big = dict(zip(mlp0_m + layer1_m, _sibling_gather(fs_bc, "grad_sibling_gather")))

    grads, delta, new_m, new_v = {}, {}, {}, {}

    def update_big(names):
        done = []
        for name, parts in BIG_PARAMS:
            if parts[0] in names:
                shp = given[name].shape
                two_d = (-1, shp[-1])
                grads[name] = jnp.stack([red_big[part].reshape(shp[1:]) for part in parts])
                d, nm, nv = _adamw(given[name].reshape(two_d), grads[name].reshape(two_d), given["m_" + name].reshape(two_d),
                                   given["v_" + name].reshape(two_d), "adamw_" + name)
                delta[name], new_m[name], new_v[name] = d.reshape(shp), nm.reshape(shp), nv.reshape(shp)
                done.append(nv)
        return done

    updated = update_big(mlp0_m + layer1_m)
    ps_a, ts_a = _direct_scatter_wait(flying_a, updated, "grad_scatter_mixer0_wait")
    fs_a = [_sum_direct(p, t, "grad_sum_mixer0_%d" % i) for i, (p, t) in enumerate(zip(ps_a, ts_a))]
    reduced_a = _sibling_gather(fs_a, "grad_sibling_gather_mixer0")
    red_big.update(zip(mixer0_m, reduced_a))
    red_small = reduced_a[-1].reshape(2, -1)
    update_big(mixer0_m)

    def take(off, sz):
        return jnp.concatenate([red_small[0, off // 2:(off + sz) // 2], red_small[1, off // 2:(off + sz) // 2]])

    off = 0
    for name in SMALL_SHARDED:
        sz = given[name].size
        grads[name] = take(off, sz).reshape(given[name].shape)
        off += sz
    n_repl = repl.shape[1]
    own_repl = _to_rows(take(off, n_repl), 16)
    repl_all = _with_own(_allgather_chips([own_repl], "replicated_allgather")[0], own_repl).reshape(N_CHIPS, -1)[:, :n_repl].reshape(-1)
    off = 0
    for name in REPLICATED:
        sz = given[name].size
        grads[name] = repl_all[off:off + sz].reshape(given[name].shape)
        off += sz

    smalls = SMALL_SHARDED + REPLICATED

    def pack_small(get):
        return _to_rows(jnp.concatenate([get(n).reshape(-1) for n in smalls]), 8)

    outs = _adamw(pack_small(lambda n: given[n]), pack_small(lambda n: grads[n]), pack_small(lambda n: given["m_" + n]),
                  pack_small(lambda n: given["v_" + n]), "adamw_small")
    for res, flat in zip((delta, new_m, new_v), outs):
        flat, off = flat.reshape(-1), 0
        for n in smalls:
            sz = given[n].size
            res[n] = flat[off:off + sz].reshape(given[n].shape)
            off += sz

    return (loss, g_a["x"], *[grads[n] for n in WEIGHT_NAMES], *[delta[n] for n in WEIGHT_NAMES],
            *[new_m[n] for n in WEIGHT_NAMES], *[new_v[n] for n in WEIGHT_NAMES])
```

```python
import functools
import math

import jax
import jax.numpy as jnp
from jax import lax
from jax.experimental import pallas as pl
from jax.experimental.pallas import tpu as pltpu

F32 = jnp.float32
MXU_DTYPE = jnp.bfloat16

D_MODEL = 1024
N_META = 16
LRU_WIDTH = 512
LRU_HEADS = 4
LRU_HEAD_DIM = 128
CONV_WIDTH = 4
LRU_C = 8.0
MLA_HEADS = 8
MLA_NOPE = 64
MLA_ROPE = 32
MLA_V = 64
MLA_Q_RANK = 256
MLA_KV_RANK = 128
RET_HEADS = 4
RET_QK_DIM = 256
RET_V_DIM = 512
D_FF = 4096
ROPE_BASE = 10000.0
DN_ALPHA = 4.0 ** 0.25
EPS = 1e-5
NEG_INF = -1e30
SEQ_BLOCK = 128

ADAM_LR = 0.001
ADAM_B1 = 0.9
ADAM_B2 = 0.999
ADAM_EPS = 1e-08
ADAM_WD = 0.01
ADAM_STEP = 10

PACK_COLS = 1024
TN_INPUT_VMEM_BYTES = 28 << 20
N_CHIPS = 4

MESH = pl.DeviceIdType.MESH


def _pick(n, target, align):
    best = None
    for t in range(align, min(n, target) + 1, align):
        if n % t == 0:
            best = t
    return n if best is None else best


def _round_up(n, m):
    return (n + m - 1) // m * m


def _relu2(a):
    r = jnp.maximum(a, 0.0)
    return r * r


def _ln_stats(z):
    mu = jnp.mean(z, axis=-1, keepdims=True)
    zc = z - mu
    var = jnp.mean(zc * zc, axis=-1, keepdims=True)
    return zc, lax.rsqrt(var + EPS)


def _mm_nn(a, w, act, name, out_dtype=F32, norm=None):
    M, K = a.shape
    sharded = w.ndim == 3
    n = w.shape[-1]
    N = n * (w.shape[0] if sharded else 1)
    tm = _pick(M, 1088 if K * a.dtype.itemsize <= 4096 and norm is None else 544, 8)
    tn = _pick(n, 1024, 128)
    per = n // tn
    assert norm is None or tn == N

    def body(a_ref, w_ref, *rest):
        av = a_ref[...]
        if act:
            av = _relu2(av.astype(F32))
        r = jnp.dot(av.astype(MXU_DTYPE), w_ref[...].astype(MXU_DTYPE), preferred_element_type=F32)
        if norm is None:
            rest[0][...] = r.astype(out_dtype)
        else:
            r_ref, g_ref, b_ref, o_ref, z_ref = rest
            z = DN_ALPHA * r_ref[...] + r
            zc, rstd = _ln_stats(z)
            z_ref[...] = z
            o_ref[...] = zc * rstd * g_ref[...] + b_ref[...]

    w_spec = pl.BlockSpec((None, K, tn), lambda i, j: (j // per, 0, j % per)) if sharded else pl.BlockSpec((K, tn), lambda i, j: (0, j))
    tile = pl.BlockSpec((tm, tn), lambda i, j: (i, j))
    in_specs, args = [pl.BlockSpec((tm, K), lambda i, j: (i, 0)), w_spec], [a, w]
    if norm is None:
        out_specs, out_shape = tile, jax.ShapeDtypeStruct((M, N), out_dtype)
    else:
        vec = pl.BlockSpec((1, N), lambda i, j: (0, 0))
        in_specs += [tile, vec, vec]
        args += [norm[0], norm[1].reshape(1, N), norm[2].reshape(1, N)]
        out_specs, out_shape = [tile, tile], [jax.ShapeDtypeStruct((M, N), F32)] * 2
    return pl.pallas_call(
        body, name=name, grid=(M // tm, N // tn), in_specs=in_specs, out_specs=out_specs, out_shape=out_shape,
        compiler_params=pltpu.CompilerParams(dimension_semantics=("parallel", "arbitrary")),
    )(*args)


def _mm_nt(g, w, a_src, name, out_dtype=F32, plus=None):
    M, N = g.shape
    sharded = w.ndim == 3
    K, n = w.shape[-2], w.shape[-1]
    if sharded:
        tk, nk = N, 1
    else:
        tk = N if N * g.dtype.itemsize <= 8192 else _pick(N, 2048, 128)
        nk = N // tk
    tm = _pick(M, 1088 if tk * g.dtype.itemsize <= 4096 else 544, 8)
    tn = _pick(K, 1024, 128)
    has_src = a_src is not None
    assert nk == 1 or out_dtype == F32
    assert plus is None or not has_src

    def body(*refs):
        if has_src:
            g_ref, w_ref, s_ref, o_ref = refs
        elif plus is not None:
            g_ref, w_ref, p_ref, o_ref = refs
        else:
            g_ref, w_ref, o_ref = refs
        nt = (((1,), (1,)), ((), ()))
        if sharded:
            r = sum(lax.dot_general(g_ref[:, s * n:(s + 1) * n].astype(MXU_DTYPE), w_ref[s].astype(MXU_DTYPE), nt, preferred_element_type=F32)
                    for s in range(w_ref.shape[0]))
        else:
            r = lax.dot_general(g_ref[...].astype(MXU_DTYPE), w_ref[...].astype(MXU_DTYPE), nt, preferred_element_type=F32)
        if has_src:
            r = r * (2.0 * jnp.maximum(s_ref[...].astype(F32), 0.0))
        first = r if plus is None else r + DN_ALPHA * p_ref[...]
        if nk == 1:
            o_ref[...] = first.astype(out_dtype)
        else:
            k = pl.program_id(2)

            @pl.when(k == 0)
            def _():
                o_ref[...] = first

            @pl.when(k > 0)
            def _():
                o_ref[...] += r

    w_spec = (pl.BlockSpec((w.shape[0], tn, n), lambda i, j, k: (0, j, 0)) if sharded
              else pl.BlockSpec((tn, tk), lambda i, j, k: (j, k)))
    in_specs = [pl.BlockSpec((tm, tk), lambda i, j, k: (i, k)), w_spec]
    args = [g, w]
    if has_src:
        assert nk == 1
        in_specs.append(pl.BlockSpec((tm, tn), lambda i, j, k: (i, j)))
        args.append(a_src)
    if plus is not None:
        in_specs.append(pl.BlockSpec((tm, tn), lambda i, j, k: (i, j)))
        args.append(plus)
    return pl.pallas_call(
        body, name=name,
        grid=(M // tm, K // tn, nk),
        in_specs=in_specs,
        out_specs=pl.BlockSpec((tm, tn), lambda i, j, k: (i, j)),
        out_shape=jax.ShapeDtypeStruct((M, K), out_dtype),
        compiler_params=pltpu.CompilerParams(dimension_semantics=("parallel", "parallel", "arbitrary")),
    )(*args)


def _mm_tn(a, g, act, name, col_shards=1, out_dtype=F32):
    M, K = a.shape
    _, N = g.shape
    n = N // col_shards
    tm, tn = _pick(K, 1024, 128), _pick(n, 1024, 128)
    row_bytes = tm * a.dtype.itemsize + tn * g.dtype.itemsize
    tk = _pick(M, min(2176, TN_INPUT_VMEM_BYTES // (2 * row_bytes)), 8)
    nk = M // tk
    per = n // tn
    direct = out_dtype == F32

    def body(a_ref, g_ref, o_ref, *scratch):
        acc_ref = o_ref if direct else scratch[0]
        k = pl.program_id(2)
        av = a_ref[...]
        if act:
            av = _relu2(av.astype(F32))
        r = lax.dot_general(av.astype(MXU_DTYPE), g_ref[...].astype(MXU_DTYPE),
                            (((0,), (0,)), ((), ())), preferred_element_type=F32)

        @pl.when(k == 0)
        def _():
            acc_ref[...] = r

        @pl.when(k > 0)
        def _():
            acc_ref[...] += r

        if not direct:
            @pl.when(k == nk - 1)
            def _():
                o_ref[...] = acc_ref[...].astype(out_dtype)

    if col_shards == 1:
        out_spec, out_shape = pl.BlockSpec((tm, tn), lambda i, j, k: (i, j)), (K, N)
    else:
        out_spec, out_shape = pl.BlockSpec((None, tm, tn), lambda i, j, k: (j // per, i, j % per)), (col_shards, K, n)
    return pl.pallas_call(
        body, name=name,
        grid=(K // tm, N // tn, nk),
        in_specs=[pl.BlockSpec((tk, tm), lambda i, j, k: (k, i)), pl.BlockSpec((tk, tn), lambda i, j, k: (k, j))],
        out_specs=out_spec,
        out_shape=jax.ShapeDtypeStruct(out_shape, out_dtype),
        scratch_shapes=[] if direct else [pltpu.VMEM((tm, tn), F32)],
        compiler_params=pltpu.CompilerParams(dimension_semantics=("parallel", "parallel", "arbitrary")),
    )(a, g)


@functools.partial(jax.custom_vjp, nondiff_argnums=(3, 4, 5, 6))
def matmul(a, w, w_grad_slot, act, name, out_dtype, col_shards):
    return _mm_nn(a, w, act, name + "_fwd", out_dtype)


def _matmul_fwd(a, w, w_grad_slot, act, name, out_dtype, col_shards):
    return _mm_nn(a, w, act, name + "_fwd", out_dtype), (a, w, jnp.zeros((), w_grad_slot.dtype))


def _matmul_bwd(act, name, out_dtype, col_shards, res, g):
    a, w, slot_like = res
    w_grad_dtype = slot_like.dtype
    da = _mm_nt(g, w, a if act else None, name + "_dx")
    dw = _mm_tn(a, g, act, name + "_dw", col_shards, w_grad_dtype)
    return da, None, dw


matmul.defvjp(_matmul_fwd, _matmul_bwd)


def _ln_bwd_call(z, g, dy, name):
    M, D = z.shape
    tm = _pick(M, 544, 8)

    def body(z_ref, g_ref, dy_ref, dz_ref, dg_ref, db_ref):
        @pl.when(pl.program_id(0) == 0)
        def _():
            dg_ref[...] = jnp.zeros_like(dg_ref)
            db_ref[...] = jnp.zeros_like(db_ref)

        zc, rstd = _ln_stats(z_ref[...])
        xhat = zc * rstd
        dy = dy_ref[...]
        dxh = dy * g_ref[...]
        m1 = jnp.mean(dxh, axis=-1, keepdims=True)
        m2 = jnp.mean(dxh * xhat, axis=-1, keepdims=True)
        dz_ref[...] = rstd * (dxh - m1 - xhat * m2)
        dg_ref[...] += jnp.sum(dy * xhat, axis=0, keepdims=True)
        db_ref[...] += jnp.sum(dy, axis=0, keepdims=True)

    row = pl.BlockSpec((tm, D), lambda i: (i, 0))
    vec = pl.BlockSpec((1, D), lambda i: (0, 0))
    return pl.pallas_call(
        body, name=name, grid=(M // tm,), in_specs=[row, vec, row], out_specs=[row, vec, vec],
        out_shape=[jax.ShapeDtypeStruct((M, D), F32), jax.ShapeDtypeStruct((1, D), F32), jax.ShapeDtypeStruct((1, D), F32)],
        compiler_params=pltpu.CompilerParams(dimension_semantics=("arbitrary",)),
    )(z, g.reshape(1, D), dy)


@functools.partial(jax.custom_vjp, nondiff_argnums=(7,))
def mlp_block(h, w1, w2, w1_grad_slot, w2_grad_slot, g, b, name):
    return _mlp_block_fwd(h, w1, w2, w1_grad_slot, w2_grad_slot, g, b, name)[0]


def _mlp_block_fwd(h, w1, w2, w1_grad_slot, w2_grad_slot, g, b, name):
    u = _mm_nn(h, w1, False, name + "_w1_fwd", out_dtype=MXU_DTYPE)
    out, z = _mm_nn(u, w2, True, name + "_w2_norm_fwd", norm=(h, g, b))
    return out, (h, u, z, w1, w2, g, jnp.zeros((), w1_grad_slot.dtype))


def _mlp_block_bwd(name, res, dy):
    h, u, z, w1, w2, g, slot_like = res
    dz, dg, db = _ln_bwd_call(z, g, dy, name + "_norm_bwd")
    du = _mm_nt(dz, w2, u, name + "_w2_dx", out_dtype=MXU_DTYPE)
    dw2 = _mm_tn(u, dz, True, name + "_w2_dw", 1, slot_like.dtype)
    dh = _mm_nt(du, w1, None, name + "_w1_dx", plus=dz)
    dw1 = _mm_tn(h, du, False, name + "_w1_dw", N_CHIPS, slot_like.dtype)
    return dh, None, None, dw1, dw2, dg.reshape(g.shape), db.reshape(g.shape)


mlp_block.defvjp(_mlp_block_fwd, _mlp_block_bwd)


@functools.partial(jax.custom_vjp, nondiff_argnums=(6,))
def out_block(h, y, w, w_grad_slot, g, b, name):
    return _out_block_fwd(h, y, w, w_grad_slot, g, b, name)[0]


def _out_block_fwd(h, y, w, w_grad_slot, g, b, name):
    out, z = _mm_nn(y, w, False, name + "_norm_fwd", norm=(h, g, b))
    return out, (y, z, w, g, jnp.zeros((), w_grad_slot.dtype))


def _out_block_bwd(name, res, dy):
    y, z, w, g, slot_like = res
    dz, dg, db = _ln_bwd_call(z, g, dy, name + "_norm_bwd")
    d_y = _mm_nt(dz, w, None, name + "_dx")
    dw = _mm_tn(y, dz, False, name + "_dw", 1, slot_like.dtype)
    return DN_ALPHA * dz, d_y, None, dw, dg.reshape(g.shape), db.reshape(g.shape)


out_block.defvjp(_out_block_fwd, _out_block_bwd)


def _rms_fwd_call(x, g, name, col_block=0):
    R = x.shape[0]
    W = g.shape[-1]
    tr = _pick(R, 1088, 8)

    def body(x_ref, g_ref, o_ref):
        xv = x_ref[...]
        rstd = lax.rsqrt(jnp.mean(xv * xv, axis=-1, keepdims=True) + EPS)
        o_ref[...] = xv * rstd * g_ref[...]

    vec = pl.BlockSpec((1, W), lambda i: (0, 0))
    return pl.pallas_call(
        body, name=name, grid=(R // tr,), in_specs=[pl.BlockSpec((tr, W), lambda i: (i, col_block)), vec],
        out_specs=pl.BlockSpec((tr, W), lambda i: (i, 0)), out_shape=jax.ShapeDtypeStruct((R, W), F32),
        compiler_params=pltpu.CompilerParams(dimension_semantics=("parallel",)),
    )(x, g.reshape(1, W))


def _rms_bwd_call(x, g, dy, name, col_block=0):
    R = x.shape[0]
    W = g.shape[-1]
    tr = _pick(R, 1088, 8)

    def body(x_ref, g_ref, dy_ref, dx_ref, dg_ref):
        @pl.when(pl.program_id(0) == 0)
        def _():
            dg_ref[...] = jnp.zeros_like(dg_ref)

        xv = x_ref[...]
        rstd = lax.rsqrt(jnp.mean(xv * xv, axis=-1, keepdims=True) + EPS)
        xhat = xv * rstd
        dy = dy_ref[...]
        dxh = dy * g_ref[...]
        dx_ref[...] = rstd * (dxh - xhat * jnp.mean(dxh * xhat, axis=-1, keepdims=True))
        dg_ref[...] += jnp.sum(dy * xhat, axis=0, keepdims=True)

    row = pl.BlockSpec((tr, W), lambda i: (i, 0))
    vec = pl.BlockSpec((1, W), lambda i: (0, 0))
    return pl.pallas_call(
        body, name=name, grid=(R // tr,), in_specs=[pl.BlockSpec((tr, W), lambda i: (i, col_block)), vec, row], out_specs=[row, vec],
        out_shape=[jax.ShapeDtypeStruct((R, W), F32), jax.ShapeDtypeStruct((1, W), F32)],
        compiler_params=pltpu.CompilerParams(dimension_semantics=("arbitrary",)),
    )(x, g.reshape(1, W), dy)


def _loss_call(h, tgt, n_tokens, name):
    B, Tp, D = h.shape
    tr = _pick(Tp, 544, 8)

    def body(y_ref, t_ref, dy_ref, acc_ref):
        @pl.when(jnp.logical_and(pl.program_id(0) == 0, pl.program_id(1) == 0))
        def _():
            acc_ref[...] = jnp.zeros_like(acc_ref)

        t = lax.broadcasted_iota(jnp.int32, (tr, 1), 0) + pl.program_id(1) * tr
        counts = jnp.logical_and(t >= N_META, t < N_META + n_tokens)
        e = jnp.where(counts, y_ref[...] - t_ref[...], 0.0)
        dy_ref[...] = e * (1.0 / D)
        acc_ref[...] += jnp.sum(jnp.sum(e * e, axis=-1, keepdims=True), axis=0, keepdims=True) * (0.5 / D)

    row = pl.BlockSpec((None, tr, D), lambda b, i: (b, i, 0))
    one = pl.BlockSpec((1, 1), lambda b, i: (0, 0))
    return pl.pallas_call(
        body, name=name, grid=(B, Tp // tr), in_specs=[row, row], out_specs=[row, one],
        out_shape=[jax.ShapeDtypeStruct((B, Tp, D), F32), jax.ShapeDtypeStruct((1, 1), F32)],
        compiler_params=pltpu.CompilerParams(dimension_semantics=("arbitrary", "arbitrary")),
    )(h, tgt)


@functools.partial(jax.custom_vjp, nondiff_argnums=(2,))
def loss_head(h, tgt, n_tokens):
    return _loss_call(h, tgt, n_tokens, "loss_head")[1][0, 0]


def _loss_head_fwd(h, tgt, n_tokens):
    dy, acc = _loss_call(h, tgt, n_tokens, "loss_head")
    return acc[0, 0], dy


def _loss_head_bwd(n_tokens, dy, ct):
    return ct * dy, None


loss_head.defvjp(_loss_head_fwd, _loss_head_bwd)


_GELU_C = math.sqrt(2.0 / math.pi)


def _gelu_parts(x):
    x2 = x * x
    t = jnp.tanh(_GELU_C * (x + 0.044715 * x * x2))
    gelu = 0.5 * x * (1.0 + t)
    dgelu = 0.5 * (1.0 + t) + 0.5 * x * (1.0 - t * t) * (_GELU_C * (1.0 + 3.0 * 0.044715 * x2))
    return gelu, dgelu


def _sigmoid(x):
    return 1.0 / (1.0 + jnp.exp(-x))


def _scan8(a, b, carry, reverse):
    row = lax.broadcasted_iota(jnp.int32, a.shape, 0)
    for s in (1, 2, 4):
        shift = 8 - s if reverse else s
        keep = (row < 8 - s) if reverse else (row >= s)
        b = jnp.where(keep, a * pltpu.roll(b, shift, 0) + b, b)
        a = jnp.where(keep, a * pltpu.roll(a, shift, 0), a)
    return a * carry + b


def _lru_pre(prec_ref, prev_ref, first, cw_ref, cb_ref, wa_ref, ba_ref, wx_ref, bx_ref, sp_ref):
    tc = prec_ref.shape[0]
    prev = jnp.where(first, 0.0, prev_ref[...])
    ext = jnp.concatenate([prev, prec_ref[...]], axis=0)
    cw = cw_ref[...]
    taps = [ext[8:] if k == CONV_WIDTH - 1 else pltpu.roll(ext, CONV_WIDTH - 1 - k, 0)[8:] for k in range(CONV_WIDTH)]
    xc = cb_ref[...] + sum(cw[k:k + 1, :] * taps[k] for k in range(CONV_WIDTH))
    ga, gx = [], []
    for h in range(LRU_HEADS):
        xh = xc[:, h * LRU_HEAD_DIM:(h + 1) * LRU_HEAD_DIM].astype(MXU_DTYPE)
        ga.append(jnp.dot(xh, wa_ref[h].astype(MXU_DTYPE), preferred_element_type=F32))
        gx.append(jnp.dot(xh, wx_ref[h].astype(MXU_DTYPE), preferred_element_type=F32))
    r = _sigmoid(jnp.concatenate(ga, axis=1) + ba_ref[...])
    i = _sigmoid(jnp.concatenate(gx, axis=1) + bx_ref[...])
    log_a = -LRU_C * r * sp_ref[...]
    a = jnp.exp(log_a)
    a2 = a * a
    mult = jnp.sqrt(-jnp.tanh(log_a) * (a2 + 1.0))
    return taps, xc, r, i, a, a2, mult


def _lru_fwd_call(p, cw, cb, wa, ba, wx, bx, sp):
    B, Tp, _ = p.shape
    W = LRU_WIDTH
    tc = SEQ_BLOCK
    nc = Tp // tc

    def body(pg_ref, prec_ref, prev_ref, cw_ref, cb_ref, wa_ref, ba_ref, wx_ref, bx_ref, sp_ref, y_ref, h_ref, carry_ref):
        first = pl.program_id(1) == 0

        @pl.when(first)
        def _():
            carry_ref[...] = jnp.zeros_like(carry_ref)

        _, xc, r, i, a, a2, mult = _lru_pre(prec_ref, prev_ref, first, cw_ref, cb_ref, wa_ref, ba_ref, wx_ref, bx_ref, sp_ref)
        b = mult * (i * xc)
        carry = carry_ref[0:1, :]
        for t in range(tc // 8):
            h = _scan8(a[8 * t:8 * t + 8], b[8 * t:8 * t + 8], carry, False)
            h_ref[8 * t:8 * t + 8, :] = h
            carry = h[7:8, :]
        carry_ref[...] = jnp.broadcast_to(carry, carry_ref.shape)
        y_ref[...] = h_ref[...] * _gelu_parts(pg_ref[...])[0]

    cur = pl.BlockSpec((None, tc, W), lambda b, j: (b, j, 0))
    rec = pl.BlockSpec((None, tc, W), lambda b, j: (b, j, 1))
    prev = pl.BlockSpec((None, 8, W), lambda b, j: (b, jnp.maximum(j * (tc // 8) - 1, 0), 1))
    vec = pl.BlockSpec((1, W), lambda b, j: (0, 0))
    cws = pl.BlockSpec((CONV_WIDTH, W), lambda b, j: (0, 0))
    wsp = pl.BlockSpec((LRU_HEADS, LRU_HEAD_DIM, LRU_HEAD_DIM), lambda b, j: (0, 0, 0))
    return pl.pallas_call(
        body, name="lru_fwd", grid=(B, nc),
        in_specs=[cur, rec, prev, cws, vec, wsp, vec, wsp, vec, vec],
        out_specs=[cur, cur],
        out_shape=[jax.ShapeDtypeStruct((B, Tp, W + MLA_HEADS * HEAD_LANES), F32), jax.ShapeDtypeStruct((B, Tp, W), F32)],
        scratch_shapes=[pltpu.VMEM((8, W), F32)],
        compiler_params=pltpu.CompilerParams(dimension_semantics=("arbitrary", "arbitrary")),
    )(p, p, p, cw, cb, wa, ba, wx, bx, sp)


def _lru_bwd_call(p, hseq, dy, cw, cb, wa, ba, wx, bx, sp, dpq, dpkv, dkpe):
    B, Tp, P = p.shape
    W = LRU_WIDTH
    tc = SEQ_BLOCK
    nc = Tp // tc
    HD = LRU_HEAD_DIM

    def body(pg_ref, prec_ref, prev_ref, h_ref, hprev_ref, dy_ref, cw_ref, cb_ref, wa_ref, ba_ref, wx_ref, bx_ref, sp_ref,
             dpq_ref, dpkv_ref, dkpe_ref, dp_ref, dcw_ref, dcb_ref, dwa_ref, dba_ref, dwx_ref, dbx_ref, dsp_ref,
             gcar_ref, anext_ref, halo_ref, g_ref):
        j = pl.program_id(1)
        first = j == nc - 1
        last = j == 0

        @pl.when(jnp.logical_and(pl.program_id(0) == 0, last))
        def _():
            for ref in (dcw_ref, dcb_ref, dwa_ref, dba_ref, dwx_ref, dbx_ref, dsp_ref):
                ref[...] = jnp.zeros_like(ref)

        @pl.when(last)
        def _():
            gcar_ref[...] = jnp.zeros_like(gcar_ref)
            anext_ref[...] = jnp.zeros_like(anext_ref)
            halo_ref[...] = jnp.zeros_like(halo_ref)

        taps, xc, r, i, a, a2, mult = _lru_pre(prec_ref, prev_ref, first, cw_ref, cb_ref, wa_ref, ba_ref, wx_ref, bx_ref, sp_ref)
        row = lax.broadcasted_iota(jnp.int32, (tc, W), 0)
        gelu, dgelu = _gelu_parts(pg_ref[...])
        dy = dy_ref[...]
        hcur = h_ref[...]
        dp_ref[:, 0:W] = dy * hcur * dgelu
        dp_ref[:, 2 * W:2 * W + MLA_Q_RANK] = dpq_ref[...]
        dp_ref[:, _KPE_START - MLA_KV_RANK:_KPE_START] = dpkv_ref[...]
        dp_ref[:, _KPE_START:P] = pltpu.roll(dkpe_ref[...], HEAD_LANES - MLA_NOPE, 1)[:, 0:P - _KPE_START]
        dh = dy * gelu
        a_next = jnp.where(row == tc - 1, anext_ref[0:1, :], pltpu.roll(a, tc - 1, 0))
        carry = gcar_ref[0:1, :]
        for t in reversed(range(tc // 8)):
            g = _scan8(a_next[8 * t:8 * t + 8], dh[8 * t:8 * t + 8], carry, True)
            g_ref[8 * t:8 * t + 8, :] = g
            carry = g[0:1, :]
        gcar_ref[...] = jnp.broadcast_to(carry, gcar_ref.shape)
        anext_ref[...] = jnp.broadcast_to(a[0:1, :], anext_ref.shape)
        G = g_ref[...]
        h_before = jnp.where(first, 0.0, hprev_ref[7:8, :])
        hprev = jnp.where(row == 0, h_before, pltpu.roll(hcur, 1, 0))
        d_a = G * hprev
        gx_ = G * xc
        d_mult = gx_ * i
        d_i = gx_ * mult
        dxc = G * (mult * i)
        d_la = d_a * a - d_mult * (a2 / mult)
        sp = sp_ref[...]
        d_r = d_la * (-LRU_C * sp)
        dsp_ref[...] += jnp.sum(d_la * (-LRU_C * r), axis=0, keepdims=True)
        dga = d_r * r * (1.0 - r)
        dgx = d_i * i * (1.0 - i)
        dba_ref[...] += jnp.sum(dga, axis=0, keepdims=True)
        dbx_ref[...] += jnp.sum(dgx, axis=0, keepdims=True)
        back = []
        for h in range(LRU_HEADS):
            sl = slice(h * HD, (h + 1) * HD)
            xh = xc[:, sl].astype(MXU_DTYPE)
            ah = dga[:, sl].astype(MXU_DTYPE)
            bh = dgx[:, sl].astype(MXU_DTYPE)
            tn = (((0,), (0,)), ((), ()))
            nt = (((1,), (1,)), ((), ()))
            dwa_ref[h] += lax.dot_general(xh, ah, tn, preferred_element_type=F32)
            dwx_ref[h] += lax.dot_general(xh, bh, tn, preferred_element_type=F32)
            back.append(lax.dot_general(ah, wa_ref[h].astype(MXU_DTYPE), nt, preferred_element_type=F32)
                        + lax.dot_general(bh, wx_ref[h].astype(MXU_DTYPE), nt, preferred_element_type=F32))
        dxc = dxc + jnp.concatenate(back, axis=1)
        dcb_ref[...] += jnp.sum(dxc, axis=0, keepdims=True)
        for k in range(CONV_WIDTH):
            dcw_ref[k:k + 1, :] += jnp.sum(dxc * taps[k], axis=0, keepdims=True)
        ext = jnp.concatenate([dxc, halo_ref[...]], axis=0)
        cw = cw_ref[...]
        acc = cw[CONV_WIDTH - 1:CONV_WIDTH, :] * dxc
        for k in range(CONV_WIDTH - 1):
            s = CONV_WIDTH - 1 - k
            acc = acc + cw[k:k + 1, :] * pltpu.roll(ext, tc + 8 - s, 0)[:tc]
        dp_ref[:, W:2 * W] = acc
        halo_ref[...] = dxc[0:8, :]

    rev = lambda j: nc - 1 - j
    cur = pl.BlockSpec((None, tc, W), lambda b, j: (b, rev(j), 0))
    rec = pl.BlockSpec((None, tc, W), lambda b, j: (b, rev(j), 1))
    prev = pl.BlockSpec((None, 8, W), lambda b, j: (b, jnp.maximum(rev(j) * (tc // 8) - 1, 0), 0))
    prev_rec = pl.BlockSpec((None, 8, W), lambda b, j: (b, jnp.maximum(rev(j) * (tc // 8) - 1, 0), 1))
    vec = pl.BlockSpec((1, W), lambda b, j: (0, 0))
    cws = pl.BlockSpec((CONV_WIDTH, W), lambda b, j: (0, 0))
    wsp = pl.BlockSpec((LRU_HEADS, HD, HD), lambda b, j: (0, 0, 0))
    vs = jax.ShapeDtypeStruct((1, W), F32)
    ws = jax.ShapeDtypeStruct((LRU_HEADS, HD, HD), F32)

    def rows(width):
        return pl.BlockSpec((None, tc, width), lambda b, j: (b, rev(j), 0))

    return pl.pallas_call(
        body, name="lru_bwd", grid=(B, nc),
        in_specs=[cur, rec, prev_rec, cur, prev, cur, cws, vec, wsp, vec, wsp, vec, vec, rows(MLA_Q_RANK), rows(MLA_KV_RANK), rows(HEAD_LANES)],
        out_specs=[rows(P), cws, vec, wsp, vec, wsp, vec, vec],
        out_shape=[jax.ShapeDtypeStruct((B, Tp, P), F32), jax.ShapeDtypeStruct((CONV_WIDTH, W), F32), vs, ws, vs, ws, vs, vs],
        scratch_shapes=[pltpu.VMEM((8, W), F32), pltpu.VMEM((8, W), F32), pltpu.VMEM((8, W), F32), pltpu.VMEM((tc, W), F32)],
        compiler_params=pltpu.CompilerParams(dimension_semantics=("arbitrary", "arbitrary")),
    )(p, p, p, hseq, hseq, dy, cw, cb, wa, ba, wx, bx, sp, dpq, dpkv, dkpe)


_Q_BLOCK = 2 * LRU_WIDTH // MLA_Q_RANK
_KV_BLOCK = (2 * LRU_WIDTH + MLA_Q_RANK) // MLA_KV_RANK
_KPE_START = 2 * LRU_WIDTH + MLA_Q_RANK + MLA_KV_RANK


@jax.custom_vjp
def even_front(p, cw, cb, wa, ba, wx, bx, sp, gq, gkv):
    return _even_front_fwd(p, cw, cb, wa, ba, wx, bx, sp, gq, gkv)[0]


def _even_front_fwd(p, cw, cb, wa, ba, wx, bx, sp, gq, gkv):
    B, Tp, W = p.shape
    p2d = p.reshape(B * Tp, W)
    y, hseq = _lru_fwd_call(p, cw, cb, wa, ba, wx, bx, sp)
    qn = _rms_fwd_call(p2d, gq, "q_norm_fwd", _Q_BLOCK)
    kvn = _rms_fwd_call(p2d, gkv, "kv_norm_fwd", _KV_BLOCK)
    kpe = jnp.pad(p[:, :, _KPE_START:], ((0, 0), (0, 0), (MLA_NOPE, HEAD_LANES - MLA_NOPE - MLA_ROPE)))
    return (y, qn, kvn, kpe), (p, hseq, cw, cb, wa, ba, wx, bx, sp, gq, gkv)


def _even_front_bwd(res, cts):
    p, hseq, cw, cb, wa, ba, wx, bx, sp, gq, gkv = res
    dy, dqn, dkvn, dkpe = cts
    B, Tp, W = p.shape
    p2d = p.reshape(B * Tp, W)
    dpq, dgq = _rms_bwd_call(p2d, gq, dqn, "q_norm_bwd", _Q_BLOCK)
    dpkv, dgkv = _rms_bwd_call(p2d, gkv, dkvn, "kv_norm_bwd", _KV_BLOCK)
    dp, dcw, dcb, dwa, dba, dwx, dbx, dsp = _lru_bwd_call(p, hseq, dy, cw, cb, wa, ba, wx, bx, sp, dpq.reshape(B, Tp, -1),
                                                          dpkv.reshape(B, Tp, -1), dkpe)
    return dp, dcw, dcb, dwa, dba, dwx, dbx, dsp, dgq.reshape(gq.shape), dgkv.reshape(gkv.shape)


even_front.defvjp(_even_front_fwd, _even_front_bwd)


def _rope_tables(pos, half):
    inv = ROPE_BASE ** (-jnp.arange(half, dtype=F32) / half)
    ang = pos.astype(F32)[:, None] * inv[None, :]
    return jnp.cos(ang), jnp.sin(ang)


_NT = (((1,), (1,)), ((), ()))
_TN = (((0,), (0,)), ((), ()))
HEAD_LANES = 128
_MLA_SCALE = (MLA_NOPE + MLA_ROPE) ** -0.5
_LOG2E = math.log2(math.e)


Q_BLOCK = 512


def _query_blocks(Tp):
    first = Tp % Q_BLOCK or Q_BLOCK
    return [(0, first)] + [(r, r + Q_BLOCK) for r in range(first, Tp, Q_BLOCK)]


def _mask_diagonal(s, fill):
    R, L = s.shape
    row = lax.broadcasted_iota(jnp.int32, (R, R), 0)
    col = lax.broadcasted_iota(jnp.int32, (R, R), 1)
    last = jnp.where(col <= row, s[:, L - R:], fill)
    return last if L == R else jnp.concatenate([s[:, :L - R], last], axis=1)


def _mla_rope_tables(pos):
    half = MLA_ROPE // 2
    cos, sin = _rope_tables(pos, half)
    T = pos.shape[0]
    ones, zeros = jnp.ones((T, MLA_NOPE), F32), jnp.zeros((T, MLA_NOPE), F32)
    tail1, tail0 = jnp.ones((T, HEAD_LANES - MLA_NOPE - MLA_ROPE), F32), jnp.zeros((T, HEAD_LANES - MLA_NOPE - MLA_ROPE), F32)
    zh = jnp.zeros((T, half), F32)
    c = jnp.concatenate([ones, cos, cos, tail1], axis=1)
    s_up = jnp.concatenate([zeros, -sin, zh, tail0], axis=1)
    s_down = jnp.concatenate([zeros, zh, sin, tail0], axis=1)
    return c, s_up, s_down


def _rope_lanes(x, c, s_up, s_down):
    half = MLA_ROPE // 2
    return x * c + pltpu.roll(x, HEAD_LANES - half, 1) * s_up + pltpu.roll(x, half, 1) * s_down


def _unrope_lanes(d, c, s_up, s_down):
    half = MLA_ROPE // 2
    return d * c + pltpu.roll(d * s_up, half, 1) + pltpu.roll(d * s_down, HEAD_LANES - half, 1)


def _mla_operands(q_ref, kv_ref, kpe_ref, c, s_up, s_down):
    lane = lax.broadcasted_iota(jnp.int32, kv_ref.shape, 1)
    qr = (_rope_lanes(q_ref[...].astype(F32), c, s_up, s_down) * (_MLA_SCALE * _LOG2E)).astype(MXU_DTYPE)
    kr = jnp.where(lane < MLA_NOPE, kv_ref[...].astype(F32), _rope_lanes(kpe_ref[...], c, s_up, s_down)).astype(MXU_DTYPE)
    return qr, kr, lane


def _mla_specs(Tp):
    head = pl.BlockSpec((None, Tp, HEAD_LANES), lambda b, h: (b, 0, h))
    shared = pl.BlockSpec((None, Tp, HEAD_LANES), lambda b, h: (b, 0, 0))
    tab = pl.BlockSpec((Tp, HEAD_LANES), lambda b, h: (0, 0))
    lse = pl.BlockSpec((None, None, Tp, 1), lambda b, h: (b, h, 0, 0))
    out = pl.BlockSpec((None, Tp, HEAD_LANES), lambda b, h: (b, 0, LRU_WIDTH // HEAD_LANES + h))
    return head, shared, tab, lse, out


def _attn_fwd_call(q, kv, kpe, tabs, y):
    B, Tp, _ = q.shape

    def body(q_ref, kv_ref, kpe_ref, c_ref, su_ref, sd_ref, y_ref, o_ref, lse_ref, qr_ref, kr_ref):
        qr, kr, lane = _mla_operands(q_ref, kv_ref, kpe_ref, c_ref[...], su_ref[...], sd_ref[...])
        qr_ref[...] = qr
        kr_ref[...] = kr
        for r0, L in _query_blocks(Tp):
            blk = slice(r0, L)
            s = _mask_diagonal(lax.dot_general(qr_ref[blk, :], kr_ref[0:L, :], _NT, preferred_element_type=F32), NEG_INF)
            m = jnp.max(s, axis=-1, keepdims=True)
            p = jnp.exp2(s - m)
            l = jnp.sum(p, axis=-1, keepdims=True)
            o = jnp.dot(p.astype(MXU_DTYPE), kv_ref[0:L, :].astype(MXU_DTYPE), preferred_element_type=F32)
            o_ref[blk, :] = jnp.where(lane[blk, :] >= MLA_NOPE, o / l, 0.0)
            lse_ref[blk, :] = m + jnp.log2(l)

    head, shared, tab, lse, out = _mla_specs(Tp)
    return pl.pallas_call(
        body, name="mla_attn_fwd", grid=(B, MLA_HEADS),
        in_specs=[head, head, shared, tab, tab, tab, pl.BlockSpec(memory_space=pl.ANY)], out_specs=[out, lse],
        out_shape=[jax.ShapeDtypeStruct(y.shape, F32), jax.ShapeDtypeStruct((B, MLA_HEADS, Tp, 1), F32)],
        input_output_aliases={6: 0},
        scratch_shapes=[pltpu.VMEM((Tp, HEAD_LANES), MXU_DTYPE), pltpu.VMEM((Tp, HEAD_LANES), MXU_DTYPE)],
        compiler_params=pltpu.CompilerParams(dimension_semantics=("parallel", "parallel")),
    )(q, kv, kpe, *tabs, y)


def _attn_bwd_call(q, kv, kpe, tabs, o, lse, do):
    B, Tp, _ = q.shape

    def body(q_ref, kv_ref, kpe_ref, c_ref, su_ref, sd_ref, o_ref, lse_ref, do_ref, dq_ref, dkv_ref, dkpe_ref,
             qr_ref, kr_ref, dqa_ref, dka_ref, dva_ref):
        c, s_up, s_down = c_ref[...], su_ref[...], sd_ref[...]
        qr, kr, lane = _mla_operands(q_ref, kv_ref, kpe_ref, c, s_up, s_down)
        qr_ref[...] = qr
        kr_ref[...] = kr
        dka_ref[...] = jnp.zeros_like(dka_ref)
        dva_ref[...] = jnp.zeros_like(dva_ref)
        for r0, L in _query_blocks(Tp):
            blk = slice(r0, L)
            qb = qr_ref[blk, :]
            do = jnp.where(lane[blk, :] >= MLA_NOPE, do_ref[blk, :], 0.0)
            delta = jnp.sum(do * o_ref[blk, :], axis=-1, keepdims=True)
            s = _mask_diagonal(lax.dot_general(qb, kr_ref[0:L, :], _NT, preferred_element_type=F32), NEG_INF)
            p = jnp.exp2(s - lse_ref[blk, :])
            dob = do.astype(MXU_DTYPE)
            dva_ref[0:L, :] += lax.dot_general(p.astype(MXU_DTYPE), dob, _TN, preferred_element_type=F32)
            dp = lax.dot_general(dob, kv_ref[0:L, :].astype(MXU_DTYPE), _NT, preferred_element_type=F32)
            ds = (p * (dp - delta)).astype(MXU_DTYPE)
            dqa_ref[blk, :] = jnp.dot(ds, kr_ref[0:L, :], preferred_element_type=F32)
            dka_ref[0:L, :] += lax.dot_general(ds, qb, _TN, preferred_element_type=F32)
        dq_ref[...] = _unrope_lanes(dqa_ref[...] * _MLA_SCALE, c, s_up, s_down).astype(dq_ref.dtype)
        dk = dka_ref[...] * (1.0 / _LOG2E)
        dkv_ref[...] = jnp.where(lane < MLA_NOPE, dk, dva_ref[...]).astype(dkv_ref.dtype)
        dkpe = jnp.where(lane >= MLA_NOPE, _unrope_lanes(dk, c, s_up, s_down), 0.0)

        @pl.when(pl.program_id(1) == 0)
        def _():
            dkpe_ref[...] = dkpe

        @pl.when(pl.program_id(1) > 0)
        def _():
            dkpe_ref[...] += dkpe

    head, shared, tab, lse_spec, out = _mla_specs(Tp)
    wide = jax.ShapeDtypeStruct((B, Tp, MLA_HEADS * HEAD_LANES), q.dtype)
    acc = pltpu.VMEM((Tp, HEAD_LANES), F32)
    return pl.pallas_call(
        body, name="mla_attn_bwd", grid=(B, MLA_HEADS),
        in_specs=[head, head, shared, tab, tab, tab, out, lse_spec, out], out_specs=[head, head, shared],
        out_shape=[wide, wide, jax.ShapeDtypeStruct((B, Tp, HEAD_LANES), F32)],
        scratch_shapes=[pltpu.VMEM((Tp, HEAD_LANES), MXU_DTYPE), pltpu.VMEM((Tp, HEAD_LANES), MXU_DTYPE), acc, acc, acc],
        compiler_params=pltpu.CompilerParams(dimension_semantics=("parallel", "arbitrary")),
    )(q, kv, kpe, *tabs, o, lse, do)


@jax.custom_vjp
def mla_attention(q, kv, kpe, tabs, y):
    return _attn_fwd_call(q, kv, kpe, tabs, y)[0]


def _mla_attention_fwd(q, kv, kpe, tabs, y):
    o, lse = _attn_fwd_call(q, kv, kpe, tabs, y)
    return o, (q, kv, kpe, tabs, o, lse)


def _mla_attention_bwd(res, do):
    q, kv, kpe, tabs, o, lse = res
    dq, dkv, dkpe = _attn_bwd_call(q, kv, kpe, tabs, o, lse, do)
    return dq, dkv, dkpe, None, do


mla_attention.defvjp(_mla_attention_fwd, _mla_attention_bwd)


def _rope_halves(x, cos, sin):
    half = x.shape[1] // 2
    x1, x2 = x[:, :half], x[:, half:]
    return jnp.concatenate([x1 * cos - x2 * sin, x1 * sin + x2 * cos], axis=1)


def _unrope_halves(d, cos, sin):
    half = d.shape[1] // 2
    d1, d2 = d[:, :half], d[:, half:]
    return jnp.concatenate([d1 * cos + d2 * sin, d2 * cos - d1 * sin], axis=1)


_RET_K_SCALE = RET_QK_DIM ** -0.5
_RET_Q_BLOCKS = RET_HEADS
_RET_V_BLOCK0 = 2 * RET_HEADS * RET_QK_DIM // RET_V_DIM
_RET_G_BLOCK0 = _RET_V_BLOCK0 + RET_HEADS
_ANY_SPACE = pl.BlockSpec(memory_space=pl.ANY)


def _ret_specs(Tp):
    q = pl.BlockSpec((None, Tp, RET_QK_DIM), lambda b, h: (b, 0, h))
    k = pl.BlockSpec((None, Tp, RET_QK_DIM), lambda b, h: (b, 0, _RET_Q_BLOCKS + h))
    v = pl.BlockSpec((None, Tp, RET_V_DIM), lambda b, h: (b, 0, _RET_V_BLOCK0 + h))
    tab = pl.BlockSpec((Tp, RET_QK_DIM // 2), lambda b, h: (0, 0))
    lg = pl.BlockSpec((None, 1, 1), lambda b, h: (h, 0, 0))
    return q, k, v, tab, lg


def _ret_operands(q_ref, k_ref, cos, sin, lg):
    t = lax.broadcasted_iota(jnp.int32, (q_ref.shape[0], 1), 0).astype(F32)
    grow, shrink = jnp.exp(-lg * t), jnp.exp(lg * t)
    qs = (_rope_halves(q_ref[...].astype(F32), cos, sin) * shrink).astype(MXU_DTYPE)
    ks = (_rope_halves(k_ref[...].astype(F32), cos, sin) * (grow * _RET_K_SCALE)).astype(MXU_DTYPE)
    return qs, ks, shrink, grow * _RET_K_SCALE


def _ret_core_fwd_call(p, cos, sin, lg):
    B, Tp, _ = p.shape

    def body(q_ref, k_ref, v_ref, cos_ref, sin_ref, lg_ref, o_ref, qs_ref, ks_ref):
        qs_ref[...], ks_ref[...], _, _ = _ret_operands(q_ref, k_ref, cos_ref[...], sin_ref[...], lg_ref[...])
        for r0, L in _query_blocks(Tp):
            blk = slice(r0, L)
            s = _mask_diagonal(lax.dot_general(qs_ref[blk, :], ks_ref[0:L, :], _NT, preferred_element_type=F32), 0.0)
            o_ref[blk, :] = jnp.dot(s.astype(MXU_DTYPE), v_ref[0:L, :].astype(MXU_DTYPE), preferred_element_type=F32)

    q, k, v, tab, lgs = _ret_specs(Tp)
    return pl.pallas_call(
        body, name="retention_fwd", grid=(B, RET_HEADS), in_specs=[q, k, v, tab, tab, lgs],
        out_specs=pl.BlockSpec((None, Tp, RET_V_DIM), lambda b, h: (b, 0, h)),
        out_shape=jax.ShapeDtypeStruct((B, Tp, RET_HEADS * RET_V_DIM), F32),
        scratch_shapes=[pltpu.VMEM((Tp, RET_QK_DIM), MXU_DTYPE), pltpu.VMEM((Tp, RET_QK_DIM), MXU_DTYPE)],
        compiler_params=pltpu.CompilerParams(dimension_semantics=("parallel", "parallel")),
    )(p, p, p, cos, sin, lg)


def _ret_core_bwd_call(p, do, cos, sin, lg, dp):
    B, Tp, _ = p.shape
    n_steps = B * RET_HEADS

    def body(q_ref, k_ref, v_ref, do_ref, cos_ref, sin_ref, lg_ref, dp_in_ref, dp_ref, qs_ref, ks_ref, dqa_ref, dka_ref, dva_ref,
             dq_ref, dk_ref, dv_ref, sems):
        b, h = pl.program_id(0), pl.program_id(1)
        step = b * RET_HEADS + h
        slot = step % 2

        def copies(s):
            cols = lambda start, width: pl.ds(pl.multiple_of(start + h * width, width), width)
            return [pltpu.make_async_copy(dq_ref.at[s], dp_ref.at[b, :, cols(0, RET_QK_DIM)], sems.at[s, 0]),
                    pltpu.make_async_copy(dk_ref.at[s], dp_ref.at[b, :, cols(RET_HEADS * RET_QK_DIM, RET_QK_DIM)], sems.at[s, 1]),
                    pltpu.make_async_copy(dv_ref.at[s], dp_ref.at[b, :, cols(2 * RET_HEADS * RET_QK_DIM, RET_V_DIM)], sems.at[s, 2])]

        cos_, sin_ = cos_ref[...], sin_ref[...]
        qs_ref[...], ks_ref[...], q_scale, k_scale = _ret_operands(q_ref, k_ref, cos_, sin_, lg_ref[...])
        dka_ref[...] = jnp.zeros_like(dka_ref)
        dva_ref[...] = jnp.zeros_like(dva_ref)
        for r0, L in _query_blocks(Tp):
            blk = slice(r0, L)
            qb = qs_ref[blk, :]
            dob = do_ref[blk, :].astype(MXU_DTYPE)
            s = _mask_diagonal(lax.dot_general(qb, ks_ref[0:L, :], _NT, preferred_element_type=F32), 0.0).astype(MXU_DTYPE)
            dva_ref[0:L, :] += lax.dot_general(s, dob, _TN, preferred_element_type=F32)
            ds = _mask_diagonal(lax.dot_general(dob, v_ref[0:L, :].astype(MXU_DTYPE), _NT, preferred_element_type=F32), 0.0).astype(MXU_DTYPE)
            dqa_ref[blk, :] = jnp.dot(ds, ks_ref[0:L, :], preferred_element_type=F32)
            dka_ref[0:L, :] += lax.dot_general(ds, qb, _TN, preferred_element_type=F32)
        dq_ref[slot] = _unrope_halves(dqa_ref[...] * q_scale, cos_, sin_).astype(dq_ref.dtype)
        dk_ref[slot] = _unrope_halves(dka_ref[...] * k_scale, cos_, sin_).astype(dk_ref.dtype)
        dv_ref[slot] = dva_ref[...].astype(dv_ref.dtype)
        for cp in copies(slot):
            cp.start()

        @pl.when(step > 0)
        def _():
            for cp in copies(1 - slot):
                cp.wait()

        @pl.when(step == n_steps - 1)
        def _():
            for cp in copies(slot):
                cp.wait()

    q, k, v, tab, lgs = _ret_specs(Tp)
    v_out = pl.BlockSpec((None, Tp, RET_V_DIM), lambda b, h: (b, 0, h))
    return pl.pallas_call(
        body, name="retention_bwd", grid=(B, RET_HEADS), in_specs=[q, k, v, v_out, tab, tab, lgs, _ANY_SPACE],
        out_specs=_ANY_SPACE, out_shape=jax.ShapeDtypeStruct(dp.shape, dp.dtype), input_output_aliases={7: 0},
        scratch_shapes=[pltpu.VMEM((Tp, RET_QK_DIM), MXU_DTYPE), pltpu.VMEM((Tp, RET_QK_DIM), MXU_DTYPE),
                        pltpu.VMEM((Tp, RET_QK_DIM), F32), pltpu.VMEM((Tp, RET_QK_DIM), F32), pltpu.VMEM((Tp, RET_V_DIM), F32),
                        pltpu.VMEM((2, Tp, RET_QK_DIM), dp.dtype), pltpu.VMEM((2, Tp, RET_QK_DIM), dp.dtype),
                        pltpu.VMEM((2, Tp, RET_V_DIM), dp.dtype), pltpu.SemaphoreType.DMA((2, 3))],
        compiler_params=pltpu.CompilerParams(dimension_semantics=("arbitrary", "arbitrary")),
    )(p, p, p, do, cos, sin, lg, dp)


def _ret_gate_specs(M):
    tm = _pick(M, 1088, 8)
    head = pl.BlockSpec((tm, RET_V_DIM), lambda i, h: (i, h))
    gate = pl.BlockSpec((tm, RET_V_DIM), lambda i, h: (i, _RET_G_BLOCK0 + h))
    return tm, head, gate


def _ret_gate_fwd_call(o, p2d):
    M = o.shape[0]
    tm, head, gate = _ret_gate_specs(M)

    def body(o_ref, g_ref, y_ref):
        ov = o_ref[...]
        gv = g_ref[...].astype(F32)
        rstd = lax.rsqrt(jnp.mean(ov * ov, axis=-1, keepdims=True) + EPS)
        y_ref[...] = (gv * _sigmoid(gv)) * (ov * rstd)

    return pl.pallas_call(
        body, name="retention_gate_fwd", grid=(M // tm, RET_HEADS), in_specs=[head, gate], out_specs=head,
        out_shape=jax.ShapeDtypeStruct(o.shape, F32),
        compiler_params=pltpu.CompilerParams(dimension_semantics=("parallel", "parallel")),
    )(o, p2d)


def _ret_gate_bwd_call(o, p2d, dy):
    M = o.shape[0]
    tm, head, gate = _ret_gate_specs(M)

    def body(o_ref, g_ref, dy_ref, do_ref, dg_ref):
        ov = o_ref[...]
        gv = g_ref[...].astype(F32)
        dy = dy_ref[...]
        rstd = lax.rsqrt(jnp.mean(ov * ov, axis=-1, keepdims=True) + EPS)
        on = ov * rstd
        sg = _sigmoid(gv)
        dg_ref[...] = (dy * on * (sg * (1.0 + gv * (1.0 - sg)))).astype(dg_ref.dtype)
        don = dy * (gv * sg)
        do_ref[...] = (rstd * (don - on * jnp.mean(don * on, axis=-1, keepdims=True))).astype(do_ref.dtype)

    return pl.pallas_call(
        body, name="retention_gate_bwd", grid=(M // tm, RET_HEADS), in_specs=[head, gate, head], out_specs=[head, gate],
        out_shape=[jax.ShapeDtypeStruct(o.shape, p2d.dtype), jax.ShapeDtypeStruct(p2d.shape, p2d.dtype)],
        compiler_params=pltpu.CompilerParams(dimension_semantics=("parallel", "parallel")),
    )(o, p2d, dy)


def _log_gamma():
    return jnp.log(1.0 - 2.0 ** (-5.0 - jnp.arange(RET_HEADS, dtype=F32))).reshape(RET_HEADS, 1, 1)


@functools.partial(jax.custom_vjp, nondiff_argnums=(9,))
def retention_block(h, w_in, w_out, w_in_grad_slot, w_out_grad_slot, g, b, cos, sin, dims):
    return _retention_block_fwd(h, w_in, w_out, w_in_grad_slot, w_out_grad_slot, g, b, cos, sin, dims)[0]


def _retention_block_fwd(h, w_in, w_out, w_in_grad_slot, w_out_grad_slot, g, b, cos, sin, dims):
    B, Tp = dims
    p = _mm_nn(h, w_in, False, "od_w_in_fwd", out_dtype=MXU_DTYPE)
    o = _ret_core_fwd_call(p.reshape(B, Tp, -1), cos, sin, _log_gamma())
    y = _ret_gate_fwd_call(o.reshape(B * Tp, -1), p)
    out, z = _mm_nn(y, w_out, False, "od_w_out_norm_fwd", norm=(h, g, b))
    return out, (h, p, o, y, z, w_in, w_out, g, cos, sin, jnp.zeros((), w_in_grad_slot.dtype))


def _retention_block_bwd(dims, res, dout):
    B, Tp = dims
    h, p, o, y, z, w_in, w_out, g, cos, sin, slot_like = res
    dz, dg, db = _ln_bwd_call(z, g, dout, "od_w_out_norm_bwd")
    dy = _mm_nt(dz, w_out, None, "od_w_out_dx")
    dw_out = _mm_tn(y, dz, False, "od_w_out_dw", 1, slot_like.dtype)
    do, dp = _ret_gate_bwd_call(o.reshape(B * Tp, -1), p, dy)
    dp = _ret_core_bwd_call(p.reshape(B, Tp, -1), do.reshape(B, Tp, -1), cos, sin, _log_gamma(), dp.reshape(B, Tp, -1)).reshape(B * Tp, -1)
    dh = _mm_nt(dp, w_in, None, "od_w_in_dx", plus=dz)
    dw_in = _mm_tn(h, dp, False, "od_w_in_dw", N_CHIPS, slot_like.dtype)
    return dh, None, None, dw_in, dw_out, dg.reshape(g.shape), db.reshape(g.shape), None, None


retention_block.defvjp(_retention_block_fwd, _retention_block_bwd)


def _heads_to_lanes(w):
    K = w.shape[0]
    w = w.reshape(K, MLA_HEADS, MLA_NOPE + MLA_ROPE)
    return jnp.pad(w, ((0, 0), (0, 0), (0, HEAD_LANES - MLA_NOPE - MLA_ROPE))).reshape(K, MLA_HEADS * HEAD_LANES)


def _out_rows_to_lanes(w):
    N = w.shape[1]
    att = w[LRU_WIDTH:].reshape(MLA_HEADS, MLA_V, N)
    att = jnp.pad(att, ((0, 0), (HEAD_LANES - MLA_V, 0), (0, 0))).reshape(MLA_HEADS * HEAD_LANES, N)
    return jnp.concatenate([w[:LRU_WIDTH], att], axis=0)


def _seq_dims(x):
    B, S, D = x.shape
    T = S + N_META
    Tp = _round_up(T, SEQ_BLOCK)
    return B, S, T, Tp


def _mixer0(diff, w, token):
    x = diff["x"]
    B, S, T, Tp = _seq_dims(x)
    D = x.shape[-1]
    M = B * Tp
    pos = jnp.arange(Tp, dtype=jnp.int32)

    def mm(a, name, act=False, out_dtype=F32, layout=lambda m: m, col_shards=1):
        return matmul(a, layout(w[name]), layout(diff[name]), act, name, out_dtype, col_shards)

    meta = jnp.broadcast_to((diff["meta_tokens"] + token)[None], (B, N_META, D))
    h = jnp.concatenate([meta, x, jnp.zeros((B, Tp - T, D), F32)], axis=1).reshape(M, D)
    p = mm(h, "ev_w_in")
    sp = jax.nn.softplus(-diff["ev_lru_lambda"]).reshape(1, LRU_WIDTH)
    y, qn, kvn, kpe = even_front(
        p.reshape(B, Tp, -1), diff["ev_conv_w"].reshape(CONV_WIDTH, LRU_WIDTH), diff["ev_conv_b"].reshape(1, LRU_WIDTH),
        diff["ev_w_rg_a"].reshape(LRU_HEADS, LRU_HEAD_DIM, LRU_HEAD_DIM), diff["ev_b_rg_a"].reshape(1, LRU_WIDTH),
        diff["ev_w_rg_x"].reshape(LRU_HEADS, LRU_HEAD_DIM, LRU_HEAD_DIM), diff["ev_b_rg_x"].reshape(1, LRU_WIDTH),
        sp, diff["ev_q_norm_g"].reshape(-1), diff["ev_kv_norm_g"].reshape(-1))
    q = mm(qn, "ev_w_uq", out_dtype=MXU_DTYPE, layout=_heads_to_lanes).reshape(B, Tp, -1)
    kv = mm(kvn, "ev_w_ukv", out_dtype=MXU_DTYPE).reshape(B, Tp, -1)
    y = mla_attention(q, kv, kpe, _mla_rope_tables(pos), y).reshape(M, -1)
    return out_block(h, y, _out_rows_to_lanes(w["ev_w_out"]), _out_rows_to_lanes(diff["ev_w_out"]),
                     diff["ln_mix_g"], diff["ln_mix_b"], "ev_w_out")


def _mlp0(diff, h, w):
    return mlp_block(h, w["mlp_w1_0"], w["mlp_w2_0"], diff["mlp_w1_0"], diff["mlp_w2_0"], diff["ln_mlp_g"], diff["ln_mlp_b"], "mlp0")


def _layer1_loss(diff, h, w, tgt):
    B, S, T, Tp = _seq_dims(tgt)
    D = tgt.shape[-1]
    pos = jnp.arange(Tp, dtype=jnp.int32)

    cos, sin = _rope_tables(pos, RET_QK_DIM // 2)
    h = retention_block(h, w["od_w_in"], w["od_w_out"], diff["od_w_in"], diff["od_w_out"], diff["ln_mix_g"], diff["ln_mix_b"], cos, sin, (B, Tp))
    h = mlp_block(h, w["mlp_w1_1"], w["mlp_w2_1"], diff["mlp_w1_1"], diff["mlp_w2_1"], diff["ln_mlp_g"], diff["ln_mlp_b"], "mlp1")
    return loss_head(h.reshape(B, Tp, D), jnp.pad(tgt, ((0, 0), (N_META, Tp - T), (0, 0))), S)


_HBM = pl.BlockSpec(memory_space=pltpu.HBM)


def _place():
    return lax.axis_index("x"), lax.axis_index("y"), lax.axis_index("c")


def _other_chips(x, y):
    return [(1 - x, y), (x, 1 - y), (1 - x, 1 - y)]


def _chunks(rows, sublanes, most):
    for q in range(most, 0, -1):
        if rows % (q * sublanes) == 0:
            return q
    return 1


def _sublanes(dtype):
    return 8 * 4 // jnp.dtype(dtype).itemsize


def _gather_pieces(bufs):
    plan, first = [], []
    for b in bufs:
        Rh = b.shape[0] // 2
        Q = _chunks(Rh, _sublanes(b.dtype), 4) if Rh * b.shape[1] * b.dtype.itemsize > (1 << 20) else 1
        first.append(3 * sum(q for _, q, _ in plan))
        plan.append((Rh, Q, Rh // Q))
    return plan, first, 3 * sum(q for _, q, _ in plan)


def _allgather_chips(bufs, name):
    n = len(bufs)
    plan, first, n_sems = _gather_pieces(bufs)

    def body(*refs):
        x_refs, out_refs, (send_sems, recv_sems) = refs[:n], refs[n:2 * n], refs[2 * n:]
        x, y, c = _place()
        sibling = (x, y, 1 - c)
        chips = _other_chips(x, y)

        def copy(k, src, dst, to):
            return pltpu.make_async_remote_copy(src_ref=src, dst_ref=dst, send_sem=send_sems.at[k], recv_sem=recv_sems.at[k],
                                                device_id=to, device_id_type=MESH)

        def piece(i, cx, cy, hc, q):
            Rh, _, ch = plan[i]
            return out_refs[i].at[2 * cx + cy, pl.ds(hc * Rh + q * ch, ch), :]

        slots = [(i, q, j) for i in range(n) for q in range(plan[i][1]) for j in range(3)]
        sem = {(i, q, j): first[i] + 3 * q + j for i, q, j in slots}
        sent = [copy(sem[i, q, j], x_refs[i].at[pl.ds(c * plan[i][0] + q * plan[i][2], plan[i][2]), :], piece(i, x, y, c, q), (*chips[j], c))
                for i, q, j in slots]
        for cp in sent:
            cp.start()
        passed = []
        for i, q, j in slots:
            landed = piece(i, *chips[j], c, q)
            copy(sem[i, q, j], landed, landed, sibling).wait_recv()
            fwd = copy(n_sems + sem[i, q, j], landed, landed, sibling)
            fwd.start()
            passed.append(fwd)
        for i, q, j in slots:
            theirs = piece(i, *chips[j], 1 - c, q)
            copy(n_sems + sem[i, q, j], theirs, theirs, sibling).wait_recv()
        for cp in sent + passed:
            cp.wait_send()

    return pl.pallas_call(
        body, name=name, in_specs=[_HBM] * n, out_specs=[_HBM] * n,
        out_shape=[jax.ShapeDtypeStruct((N_CHIPS,) + b.shape, b.dtype) for b in bufs],
        scratch_shapes=[pltpu.SemaphoreType.DMA((2 * n_sems,)), pltpu.SemaphoreType.DMA((2 * n_sems,))],
    )(*bufs)


def _with_own(gathered, own):
    my = 2 * lax.axis_index("x") + lax.axis_index("y")
    return lax.dynamic_update_slice(gathered, own[None], (my, 0, 0))


def _sibling_gather(fs, name):
    n = len(fs)

    def body(*refs):
        out_refs, (send_sems, recv_sems) = refs[n:2 * n], refs[2 * n:]
        x, y, c = _place()
        copies = [pltpu.make_async_remote_copy(src_ref=out_ref.at[c], dst_ref=out_ref.at[c], send_sem=send_sems.at[i], recv_sem=recv_sems.at[i],
                                               device_id=(x, y, 1 - c), device_id_type=MESH) for i, out_ref in enumerate(out_refs)]
        for cp in copies:
            cp.start()
        for cp in copies:
            cp.wait()

    return pl.pallas_call(
        body, name=name, in_specs=[_HBM] * n, out_specs=[_HBM] * n,
        out_shape=[jax.ShapeDtypeStruct(f.shape, f.dtype) for f in fs], input_output_aliases={i: i for i in range(n)},
        scratch_shapes=[pltpu.SemaphoreType.DMA((n,)), pltpu.SemaphoreType.DMA((n,))],
    )(*fs)


def _axis_scalar(name):
    return lax.axis_index(name).astype(jnp.int32).reshape(1)


_SEM = pl.BlockSpec(memory_space=pltpu.SEMAPHORE)
_ANY = pl.BlockSpec(memory_space=pl.ANY)
_EFFECT = pltpu.SideEffectType.DATAFLOW_SIDE_EFFECTING


def _in_hbm(a):
    return pltpu.with_memory_space_constraint(a, pltpu.HBM)


def _half_copies(x_refs, land_refs, send_sems, recv_sems, arriving):
    x, y, c = _place()
    copies = []
    for i, (x_ref, land_ref) in enumerate(zip(x_refs, land_refs)):
        Rh = x_ref.shape[0] // 2
        rows = pl.ds(c * Rh, Rh)
        for j, (cx, cy) in enumerate(_other_chips(x, y)):
            copies.append(pltpu.make_async_remote_copy(
                src_ref=x_ref.at[rows, :], dst_ref=land_ref.at[2 * cx + cy if arriving else 2 * x + y, rows, :],
                send_sem=send_sems.at[3 * i + j], recv_sem=recv_sems.at[3 * i + j], device_id=(cx, cy, c), device_id_type=MESH))
    return copies


def _allgather_start(bufs, name):
    n = len(bufs)

    def body(*refs):
        x_refs, land_refs, (send_sems, recv_sems), token = refs[:n], refs[n:2 * n], refs[2 * n:2 * n + 2], refs[-1]
        for cp in _half_copies(x_refs, land_refs, send_sems, recv_sems, False):
            cp.start()
        token[...] = jnp.zeros_like(token)

    lands = [lax.empty((N_CHIPS,) + b.shape, b.dtype) for b in bufs]
    out = pl.pallas_call(
        body, name=name,
        out_shape=(pltpu.SemaphoreType.DMA((3 * n,)), pltpu.SemaphoreType.DMA((3 * n,)), *[pltpu.HBM(a.shape, a.dtype) for a in bufs + lands],
                   jax.ShapeDtypeStruct((8, 128), F32)),
        in_specs=[_HBM] * (2 * n), out_specs=(_SEM, _SEM, *[_HBM] * (2 * n), pl.BlockSpec(memory_space=pltpu.VMEM)),
        input_output_aliases={i: 2 + i for i in range(2 * n)}, compiler_params=pltpu.CompilerParams(has_side_effects=_EFFECT),
    )(*[_in_hbm(a) for a in bufs + lands])
    return (out[0], out[1], list(out[2:2 + n]), list(out[2 + n:2 + 2 * n])), out[-1][0, 0]


def _allgather_wait(pending, after, name):
    send_sems, recv_sems, bufs, lands = pending
    n = len(bufs)

    def body(*refs):
        x_refs, land_refs, send_sems, recv_sems = refs[:n], refs[n:2 * n], refs[2 * n], refs[2 * n + 1]
        for cp in _half_copies(x_refs, land_refs, send_sems, recv_sems, False):
            cp.wait_send()
        for cp in _half_copies(x_refs, land_refs, send_sems, recv_sems, True):
            cp.wait_recv()

    out = pl.pallas_call(
        body, name=name, out_shape=tuple(pltpu.HBM(a.shape, a.dtype) for a in bufs + lands),
        in_specs=[_HBM] * (2 * n) + [_SEM, _SEM, _ANY], out_specs=tuple([_HBM] * (2 * n)), input_output_aliases={i: i for i in range(2 * n)},
        compiler_params=pltpu.CompilerParams(has_side_effects=_EFFECT),
    )(*bufs, *lands, send_sems, recv_sems, after)
    return list(out[n:])


def _sibling_forward(lands, name):
    n = len(lands)
    plan, first, n_sems = _gather_pieces([jax.ShapeDtypeStruct(l.shape[1:], l.dtype) for l in lands])

    def body(*refs):
        out_refs, (send_sems, recv_sems) = refs[n:2 * n], refs[2 * n:]
        x, y, c = _place()

        def copies(hc):
            return [pltpu.make_async_remote_copy(
                        src_ref=out_refs[i].at[2 * cx + cy, pl.ds(hc * plan[i][0] + q * plan[i][2], plan[i][2]), :],
                        dst_ref=out_refs[i].at[2 * cx + cy, pl.ds(hc * plan[i][0] + q * plan[i][2], plan[i][2]), :],
                        send_sem=send_sems.at[first[i] + 3 * q + j], recv_sem=recv_sems.at[first[i] + 3 * q + j],
                        device_id=(x, y, 1 - c), device_id_type=MESH)
                    for i in range(n) for q in range(plan[i][1]) for j, (cx, cy) in enumerate(_other_chips(x, y))]

        mine = copies(c)
        for cp in mine:
            cp.start()
        for cp in mine:
            cp.wait_send()
        for cp in copies(1 - c):
            cp.wait_recv()

    return pl.pallas_call(
        body, name=name, in_specs=[_HBM] * n, out_specs=[_HBM] * n, out_shape=[jax.ShapeDtypeStruct(l.shape, l.dtype) for l in lands],
        input_output_aliases={i: i for i in range(n)},
        scratch_shapes=[pltpu.SemaphoreType.DMA((n_sems,)), pltpu.SemaphoreType.DMA((n_sems,))],
    )(*lands)


N_PEERS = 7


def _direct_copies(p_refs, t_refs, send_sems, recv_sems):
    x, y, c = _place()
    copies = []
    for i, (p_ref, t_ref) in enumerate(zip(p_refs, t_refs)):
        for f in range(1, N_PEERS + 1):
            px, py, pc = x ^ (f >> 2), y ^ ((f >> 1) & 1), c ^ (f & 1)
            copies.append(pltpu.make_async_remote_copy(
                src_ref=p_ref.at[2 * px + py, pc], dst_ref=t_ref.at[f - 1], send_sem=send_sems.at[N_PEERS * i + f - 1],
                recv_sem=recv_sems.at[N_PEERS * i + f - 1], device_id=(px, py, pc), device_id_type=MESH))
    return copies


def _direct_scatter_start(ps, name, carried=()):
    n, m = len(ps), 2 * len(ps) + len(carried)

    def body(*refs):
        p_refs, t_refs, (send_sems, recv_sems) = refs[:n], refs[n:2 * n], refs[m:m + 2]
        for cp in _direct_copies(p_refs, t_refs, send_sems, recv_sems):
            cp.start()

    lands = [lax.empty((N_PEERS,) + p.shape[2:], p.dtype) for p in ps]
    through = ps + lands + list(carried)
    out = pl.pallas_call(
        body, name=name,
        out_shape=(pltpu.SemaphoreType.DMA((N_PEERS * n,)), pltpu.SemaphoreType.DMA((N_PEERS * n,)),
                   *[pltpu.HBM(a.shape, a.dtype) for a in through]),
        in_specs=[_HBM] * m, out_specs=(_SEM, _SEM, *[_HBM] * m),
        input_output_aliases={i: 2 + i for i in range(m)}, compiler_params=pltpu.CompilerParams(has_side_effects=_EFFECT),
    )(*[_in_hbm(a) for a in through])
    return (out[0], out[1], list(out[2:2 + n]), list(out[2 + n:2 + 2 * n])), list(out[2 + 2 * n:])


def _direct_scatter_wait(pending, after, name):
    send_sems, recv_sems, ps, lands = pending
    n = len(ps)

    def body(*refs):
        p_refs, t_refs, send_sems, recv_sems = refs[:n], refs[n:2 * n], refs[2 * n], refs[2 * n + 1]
        for cp in _direct_copies(p_refs, t_refs, send_sems, recv_sems):
            cp.wait_send()
            cp.wait_recv()

    out = pl.pallas_call(
        body, name=name, out_shape=tuple(pltpu.HBM(a.shape, a.dtype) for a in ps + lands),
        in_specs=[_HBM] * (2 * n) + [_SEM, _SEM] + [_ANY] * len(after), out_specs=tuple([_HBM] * (2 * n)),
        input_output_aliases={i: i for i in range(2 * n)}, compiler_params=pltpu.CompilerParams(has_side_effects=_EFFECT),
    )(*ps, *lands, send_sems, recv_sems, *after)
    return list(out[:n]), list(out[n:])


def _sum_direct(p, t, name):
    _, _, R, C = p.shape
    tr = _pick(R, 512, 16)

    def body(x_ref, y_ref, c_ref, p_ref, t_ref, o_ref):
        acc = p_ref[...].astype(F32)
        for f in range(N_PEERS):
            acc = acc + t_ref[f].astype(F32)
        o_ref[...] = acc

    grid_spec = pltpu.PrefetchScalarGridSpec(
        num_scalar_prefetch=3, grid=(R // tr,),
        in_specs=[pl.BlockSpec((None, None, tr, C), lambda i, x_ref, y_ref, c_ref: (2 * x_ref[0] + y_ref[0], c_ref[0], i, 0)),
                  pl.BlockSpec((N_PEERS, tr, C), lambda i, x_ref, y_ref, c_ref: (0, i, 0))],
        out_specs=pl.BlockSpec((None, tr, C), lambda i, x_ref, y_ref, c_ref: (c_ref[0], i, 0)))
    return pl.pallas_call(body, name=name, grid_spec=grid_spec, out_shape=jax.ShapeDtypeStruct((2, R, C), F32),
                          compiler_params=pltpu.CompilerParams(dimension_semantics=("parallel",)))(
        _axis_scalar("x"), _axis_scalar("y"), _axis_scalar("c"), p, t)


def _adamw(w, g, m, v, name):
    R, C = w.shape
    tr = _pick(R, 256, 8)

    def body(w_ref, g_ref, m_ref, v_ref, d_ref, nm_ref, nv_ref):
        g_ = g_ref[...]
        m_ = ADAM_B1 * m_ref[...] + (1.0 - ADAM_B1) * g_
        v_ = ADAM_B2 * v_ref[...] + (1.0 - ADAM_B2) * (g_ * g_)
        m_hat = m_ / (1.0 - ADAM_B1 ** ADAM_STEP)
        v_hat = v_ / (1.0 - ADAM_B2 ** ADAM_STEP)
        d_ref[...] = -ADAM_LR * (m_hat / (jnp.sqrt(v_hat) + ADAM_EPS) + ADAM_WD * w_ref[...])
        nm_ref[...] = m_
        nv_ref[...] = v_

    row = pl.BlockSpec((tr, C), lambda i: (i, 0))
    shp = jax.ShapeDtypeStruct((R, C), F32)
    return pl.pallas_call(body, name=name, grid=(R // tr,), in_specs=[row] * 4, out_specs=[row] * 3, out_shape=[shp] * 3,
                          compiler_params=pltpu.CompilerParams(dimension_semantics=("parallel",)))(w, g, m, v)


BIG_SPECS = (("ev_w_in", 1024, 1440, 1), ("ev_w_uq", 256, 768, 1), ("ev_w_ukv", 128, 1024, 1), ("ev_w_out", 1024, 1024, 0),
             ("od_w_in", 1024, 6144, 1), ("od_w_out", 2048, 1024, 0), ("mlp_w1_0", 1024, 4096, 1), ("mlp_w1_1", 1024, 4096, 1),
             ("mlp_w2_0", 4096, 1024, 0), ("mlp_w2_1", 4096, 1024, 0))
BIG_PARAMS = (("ev_w_in", ("ev_w_in",)), ("ev_w_uq", ("ev_w_uq",)), ("ev_w_ukv", ("ev_w_ukv",)), ("ev_w_out", ("ev_w_out",)),
              ("od_w_in", ("od_w_in",)), ("od_w_out", ("od_w_out",)), ("mlp_w1", ("mlp_w1_0", "mlp_w1_1")),
              ("mlp_w2", ("mlp_w2_0", "mlp_w2_1")))
REPLICATED = ("ev_conv_b", "ev_w_rg_a", "ev_b_rg_a", "ev_w_rg_x", "ev_b_rg_x", "ev_lru_lambda", "ev_q_norm_g", "ev_kv_norm_g",
              "ln_mix_g", "ln_mix_b", "ln_mlp_g", "ln_mlp_b")
SMALL_SHARDED = ("meta_tokens", "ev_conv_w")
COL_SHARD_GRADS = ("od_w_in", "mlp_w1_0", "mlp_w1_1")
MATRIX_GROUPS = (("ev_w_in", "ev_w_uq", "ev_w_ukv", "ev_w_out"), ("mlp_w1_0", "mlp_w2_0"), ("od_w_in", "od_w_out", "mlp_w1_1", "mlp_w2_1"))
LAYER_NORMS = ("ln_mix_g", "ln_mix_b", "ln_mlp_g", "ln_mlp_b")
WEIGHT_NAMES = ("meta_tokens", "ev_w_in", "ev_conv_w", "ev_conv_b", "ev_w_rg_a", "ev_b_rg_a", "ev_w_rg_x", "ev_b_rg_x",
                "ev_lru_lambda", "ev_q_norm_g", "ev_w_uq", "ev_kv_norm_g", "ev_w_ukv", "ev_w_out", "od_w_in", "od_w_out",
                "ln_mix_g", "ln_mix_b", "mlp_w1", "mlp_w2", "ln_mlp_g", "ln_mlp_b")


def _to_rows(flat, row_align):
    n = flat.shape[-1]
    rows = _round_up(-(-n // PACK_COLS), row_align)
    pad = rows * PACK_COLS - n
    if pad:
        flat = jnp.pad(flat, [(0, 0)] * (flat.ndim - 1) + [(0, pad)])
    return flat.reshape(flat.shape[:-1] + (rows, PACK_COLS))


def _shard_shape(K, N, axis):
    return (K // N_CHIPS, N) if axis == 0 else (K, N // N_CHIPS)


def _gather_shards(stacked, K, N, axis):
    if axis == 0:
        return stacked.reshape(K, N)
    return stacked.transpose(1, 0, 2).reshape(K, N)


def _split_shards(full, K, N, axis):
    if axis == 0:
        return full.reshape(N_CHIPS, -1)
    return full.reshape(K, N_CHIPS, N // N_CHIPS).transpose(1, 0, 2).reshape(N_CHIPS, -1)


def kernel(x, meta_tokens, ev_w_in, ev_conv_w, ev_conv_b, ev_w_rg_a, ev_b_rg_a, ev_w_rg_x, ev_b_rg_x, ev_lru_lambda, ev_q_norm_g, ev_w_uq, ev_kv_norm_g, ev_w_ukv, ev_w_out, od_w_in, od_w_out, ln_mix_g, ln_mix_b, mlp_w1, mlp_w2, ln_mlp_g, ln_mlp_b, loss_target, m_meta_tokens, m_ev_w_in, m_ev_conv_w, m_ev_conv_b, m_ev_w_rg_a, m_ev_b_rg_a, m_ev_w_rg_x, m_ev_b_rg_x, m_ev_lru_lambda, m_ev_q_norm_g, m_ev_w_uq, m_ev_kv_norm_g, m_ev_w_ukv, m_ev_w_out, m_od_w_in, m_od_w_out, m_ln_mix_g, m_ln_mix_b, m_mlp_w1, m_mlp_w2, m_ln_mlp_g, m_ln_mlp_b, v_meta_tokens, v_ev_w_in, v_ev_conv_w, v_ev_conv_b, v_ev_w_rg_a, v_ev_b_rg_a, v_ev_w_rg_x, v_ev_b_rg_x, v_ev_lru_lambda, v_ev_q_norm_g, v_ev_w_uq, v_ev_kv_norm_g, v_ev_w_ukv, v_ev_w_out, v_od_w_in, v_od_w_out, v_ln_mix_g, v_ln_mix_b, v_mlp_w1, v_mlp_w2, v_ln_mlp_g, v_ln_mlp_b):
    given = dict(locals())
    local_big = {"ev_w_in": ev_w_in[0], "ev_w_uq": ev_w_uq[0], "ev_w_ukv": ev_w_ukv[0], "ev_w_out": ev_w_out[0],
                 "od_w_in": od_w_in[0], "od_w_out": od_w_out[0], "mlp_w1_0": mlp_w1[0], "mlp_w1_1": mlp_w1[1],
                 "mlp_w2_0": mlp_w2[0], "mlp_w2_1": mlp_w2[1]}

    specs = {spec[0]: spec for spec in BIG_SPECS}
    mixer0_m, mlp0_m, layer1_m = MATRIX_GROUPS

    def shards(names):
        return [local_big[n].astype(MXU_DTYPE) for n in names]

    def whole(stacked, n):
        _, K, N, ax = specs[n]
        return stacked if n in COL_SHARD_GRADS else _gather_shards(stacked, K, N, ax)

    def filled(gathered, own, names):
        return {n: whole(_with_own(g_, o_), n) for n, g_, o_ in zip(names, gathered, own)}

    own_a, own_b, own_c = shards(mixer0_m), shards(mlp0_m), shards(layer1_m)
    small = [meta_tokens, jnp.pad(ev_conv_w[0], ((0, 16 - CONV_WIDTH), (0, 0)))]
    gathered_a = _allgather_chips(own_a + small, "weight_allgather_mixer0")
    pending_b, token1 = _allgather_start(own_b, "weight_allgather_mlp0_start")
    pending_c, token2 = _allgather_start(own_c, "weight_allgather_layer1_start")
    meta_full = _gather_shards(_with_own(gathered_a[-2], small[0]), N_META, D_MODEL, 1)
    conv_full = _gather_shards(_with_own(gathered_a[-1], small[1])[:, :CONV_WIDTH], CONV_WIDTH, LRU_WIDTH, 1)

    def slots(names, dtype):
        return {n: jnp.zeros((N_CHIPS, specs[n][1], specs[n][2] // N_CHIPS) if n in COL_SHARD_GRADS else specs[n][1:3], dtype) for n in names}

    def norms(names, layer):
        return {n: given[n][layer] for n in names}

    def finish_gather(pending, own, after, names, tag):
        landed = _allgather_wait(pending, lax.stop_gradient(after), "weight_allgather_%s_wait" % tag)
        return filled(_sibling_forward(landed, "weight_allgather_%s_forward" % tag), own, names)

    diff_a = {**slots(mixer0_m, MXU_DTYPE), **norms(("ln_mix_g", "ln_mix_b"), 0), **{n: given[n] for n in REPLICATED if n not in LAYER_NORMS},
              "x": x, "meta_tokens": meta_full, "ev_conv_w": conv_full}
    diff_b = {**slots(mlp0_m, MXU_DTYPE), **norms(("ln_mlp_g", "ln_mlp_b"), 0)}
    diff_c = {**slots(layer1_m, MXU_DTYPE), **norms(LAYER_NORMS, 1)}
    w_a = filled(gathered_a[:len(mixer0_m)], own_a, mixer0_m)
    h_a, back_a = jax.vjp(lambda d: _mixer0(d, w_a, token1 + token2), diff_a)
    w_b = finish_gather(pending_b, own_b, h_a, mlp0_m, "mlp0")
    h_b, back_b = jax.vjp(lambda d, hh: _mlp0(d, hh, w_b), diff_b, h_a)
    w_c = finish_gather(pending_c, own_c, h_b, layer1_m, "layer1")
    loss, back_c = jax.vjp(lambda d, hh: _layer1_loss(d, hh, w_c, loss_target), diff_c, h_b)
    loss = lax.psum(loss, ("x", "y", "c"))

    def blocks_of(grad, n):
        _, K, N, ax = specs[n]
        if n in COL_SHARD_GRADS:
            blocks = grad
        elif ax == 0:
            blocks = grad.reshape(N_CHIPS, K // N_CHIPS, N)
        else:
            blocks = grad.reshape(K, N_CHIPS, N // N_CHIPS).transpose(1, 0, 2)
        return blocks.reshape(N_CHIPS, 2, blocks.shape[1] // 2, blocks.shape[2])

    def start_reduce(grads_of, names, tag, dh):
        flying, (dh,) = _direct_scatter_start([blocks_of(grads_of[n], n) for n in names], "grad_scatter_%s_start" % tag, [dh])
        return flying, dh

    g_c, dh = back_c(jnp.ones((), F32))
    flying_c, dh = start_reduce(g_c, layer1_m, "layer1", dh)
    g_b, dh = back_b(dh)
    flying_b, dh = start_reduce(g_b, mlp0_m, "mlp0", dh)
    (g_a,) = back_a(dh)

    g = {**g_a, **g_b, **g_c}
    g.update({n: jnp.stack([(g_b if n in g_b else g_a)[n], g_c[n]]) for n in LAYER_NORMS})
    repl = jnp.concatenate([g[n].reshape(-1) for n in REPLICATED]).reshape(N_CHIPS, -1)
    small = [_split_shards(g["meta_tokens"], N_META, D_MODEL, 1), _split_shards(g["ev_conv_w"], CONV_WIDTH, LRU_WIDTH, 1), repl]
    small = [pc.reshape(N_CHIPS, 2, -1) for pc in small]
    n_small = sum(pc.shape[2] for pc in small)
    small.append(jnp.zeros((N_CHIPS, 2, _round_up(n_small, 32 * PACK_COLS) - n_small), F32))
    p_small = jnp.concatenate(small, axis=2).reshape(N_CHIPS, 2, -1, PACK_COLS)
    flying_a, _ = _direct_scatter_start([blocks_of(g_a[n], n) for n in mixer0_m] + [p_small], "grad_scatter_mixer0_start")
    started = [g_a["x"], flying_a[2][0]]
    ps_c, ts_c = _direct_scatter_wait(flying_c, started, "grad_scatter_layer1_wait")
    ps_b, ts_b = _direct_scatter_wait(flying_b, started, "grad_scatter_mlp0_wait")
    fs_bc = [_sum_direct(p, t, "grad_sum_%d" % i) for i, (p, t) in enumerate(zip(ps_b + ps_c, ts_b + ts_c))]
    red_big = dict(zip(mlp0_m + layer1_m, _sibling_gather(fs_bc, "grad_sibling_gather")))

    grads, delta, new_m, new_v = {}, {}, {}, {}

    def update_big(names):
        done = []
        for name, parts in BIG_PARAMS:
            if parts[0] in names:
                shp = given[name].shape
                two_d = (-1, shp[-1])
                grads[name] = jnp.stack([red_big[part].reshape(shp[1:]) for part in parts])
                d, nm, nv = _adamw(given[name].reshape(two_d), grads[name].reshape(two_d), given["m_" + name].reshape(two_d),
                                   given["v_" + name].reshape(two_d), "adamw_" + name)
                delta[name], new_m[name], new_v[name] = d.reshape(shp), nm.reshape(shp), nv.reshape(shp)
                done.append(nv)
        return done

    updated = update_big(mlp0_m + layer1_m)
    ps_a, ts_a = _direct_scatter_wait(flying_a, updated, "grad_scatter_mixer0_wait")
    fs_a = [_sum_direct(p, t, "grad_sum_mixer0_%d" % i) for i, (p, t) in enumerate(zip(ps_a, ts_a))]
    reduced_a = _sibling_gather(fs_a, "grad_sibling_gather_mixer0")
    red_big.update(zip(mixer0_m, reduced_a))
    red_small = reduced_a[-1].reshape(2, -1)
    update_big(mixer0_m)

    def take(off, sz):
        return jnp.concatenate([red_small[0, off // 2:(off + sz) // 2], red_small[1, off // 2:(off + sz) // 2]])

    off = 0
    for name in SMALL_SHARDED:
        sz = given[name].size
        grads[name] = take(off, sz).reshape(given[name].shape)
        off += sz
    n_repl = repl.shape[1]
    own_repl = _to_rows(take(off, n_repl), 16)
    repl_all = _with_own(_allgather_chips([own_repl], "replicated_allgather")[0], own_repl).reshape(N_CHIPS, -1)[:, :n_repl].reshape(-1)
    off = 0
    for name in REPLICATED:
        sz = given[name].size
        grads[name] = repl_all[off:off + sz].reshape(given[name].shape)
        off += sz

    smalls = SMALL_SHARDED + REPLICATED

    def pack_small(get):
        return _to_rows(jnp.concatenate([get(n).reshape(-1) for n in smalls]), 8)

    outs = _adamw(pack_small(lambda n: given[n]), pack_small(lambda n: grads[n]), pack_small(lambda n: given["m_" + n]),
                  pack_small(lambda n: given["v_" + n]), "adamw_small")
    for res, flat in zip((delta, new_m, new_v), outs):
        flat, off = flat.reshape(-1), 0
        for n in smalls:
            sz = given[n].size
            res[n] = flat[off:off + sz].reshape(given[n].shape)
            off += sz

    return (loss, g_a["x"], *[grads[n] for n in WEIGHT_NAMES], *[delta[n] for n in WEIGHT_NAMES],
            *[new_m[n] for n in WEIGHT_NAMES], *[new_v[n] for n in WEIGHT_NAMES])
```

```python
import functools
import math

import jax
import jax.numpy as jnp
from jax import lax
from jax.experimental import pallas as pl
from jax.experimental.pallas import tpu as pltpu

F32 = jnp.float32
MXU_DTYPE = jnp.bfloat16

D_MODEL = 1024
N_META = 16
LRU_WIDTH = 512
LRU_HEADS = 4
LRU_HEAD_DIM = 128
CONV_WIDTH = 4
LRU_C = 8.0
MLA_HEADS = 8
MLA_NOPE = 64
MLA_ROPE = 32
MLA_V = 64
MLA_Q_RANK = 256
MLA_KV_RANK = 128
RET_HEADS = 4
RET_QK_DIM = 256
RET_V_DIM = 512
D_FF = 4096
ROPE_BASE = 10000.0
DN_ALPHA = 4.0 ** 0.25
EPS = 1e-5
NEG_INF = -1e30
SEQ_BLOCK = 128

ADAM_LR = 0.001
ADAM_B1 = 0.9
ADAM_B2 = 0.999
ADAM_EPS = 1e-08
ADAM_WD = 0.01
ADAM_STEP = 10

PACK_COLS = 1024
TN_INPUT_VMEM_BYTES = 28 << 20
N_CHIPS = 4

MESH = pl.DeviceIdType.MESH


def _pick(n, target, align):
    best = None
    for t in range(align, min(n, target) + 1, align):
        if n % t == 0:
            best = t
    return n if best is None else best


def _round_up(n, m):
    return (n + m - 1) // m * m


def _relu2(a):
    r = jnp.maximum(a, 0.0)
    return r * r


def _ln_stats(z):
    mu = jnp.mean(z, axis=-1, keepdims=True)
    zc = z - mu
    var = jnp.mean(zc * zc, axis=-1, keepdims=True)
    return zc, lax.rsqrt(var + EPS)


def _mm_nn(a, w, act, name, out_dtype=F32, norm=None):
    M, K = a.shape
    sharded = w.ndim == 3
    n = w.shape[-1]
    N = n * (w.shape[0] if sharded else 1)
    tm = _pick(M, 1088 if K * a.dtype.itemsize <= 4096 and norm is None else 544, 8)
    tn = _pick(n, 1024, 128)
    per = n // tn
    assert norm is None or tn == N

    def body(a_ref, w_ref, *rest):
        av = a_ref[...]
        if act:
            av = _relu2(av.astype(F32))
        r = jnp.dot(av.astype(MXU_DTYPE), w_ref[...].astype(MXU_DTYPE), preferred_element_type=F32)
        if norm is None:
            rest[0][...] = r.astype(out_dtype)
        else:
            r_ref, g_ref, b_ref, o_ref, z_ref = rest
            z = DN_ALPHA * r_ref[...] + r
            zc, rstd = _ln_stats(z)
            z_ref[...] = z
            o_ref[...] = zc * rstd * g_ref[...] + b_ref[...]

    w_spec = pl.BlockSpec((None, K, tn), lambda i, j: (j // per, 0, j % per)) if sharded else pl.BlockSpec((K, tn), lambda i, j: (0, j))
    tile = pl.BlockSpec((tm, tn), lambda i, j: (i, j))
    in_specs, args = [pl.BlockSpec((tm, K), lambda i, j: (i, 0)), w_spec], [a, w]
    if norm is None:
        out_specs, out_shape = tile, jax.ShapeDtypeStruct((M, N), out_dtype)
    else:
        vec = pl.BlockSpec((1, N), lambda i, j: (0, 0))
        in_specs += [tile, vec, vec]
        args += [norm[0], norm[1].reshape(1, N), norm[2].reshape(1, N)]
        out_specs, out_shape = [tile, tile], [jax.ShapeDtypeStruct((M, N), F32)] * 2
    return pl.pallas_call(
        body, name=name, grid=(M // tm, N // tn), in_specs=in_specs, out_specs=out_specs, out_shape=out_shape,
        compiler_params=pltpu.CompilerParams(dimension_semantics=("parallel", "arbitrary")),
    )(*args)


def _mm_nt(g, w, a_src, name, out_dtype=F32, plus=None, ln=None):
    if ln is not None:
        assert g is None and plus is None and w.ndim == 2
        g = ln[0]
    M, N = g.shape
    sharded = w.ndim == 3
    K, n = w.shape[-2], w.shape[-1]
    if sharded:
        tk, nk = N, 1
    else:
        tk = N if N * g.dtype.itemsize <= 8192 else _pick(N, 2048, 128)
        nk = N // tk
    tm = _pick(M, 1088 if tk * g.dtype.itemsize <= 4096 and ln is None else 544, 8)
    tn = _pick(K, 1024, 128)
    has_src = a_src is not None
    assert nk == 1 or (out_dtype == F32 and ln is None)
    assert plus is None or not has_src

    def body(*refs):
        if ln is not None:
            z_ref, gain_ref, dy_ref, w_ref = refs[:4]
            o_ref, g_ref, dgain_ref, dbias_ref = refs[-4:]
            s_ref = refs[4] if has_src else None

            @pl.when(jnp.logical_and(pl.program_id(0) == 0, pl.program_id(1) == 0))
            def _():
                dgain_ref[...] = jnp.zeros_like(dgain_ref)
                dbias_ref[...] = jnp.zeros_like(dbias_ref)

            @pl.when(pl.program_id(1) == 0)
            def _():
                zc, rstd = _ln_stats(z_ref[...])
                xhat = zc * rstd
                dy = dy_ref[...]
                dxh = dy * gain_ref[...]
                m1 = jnp.mean(dxh, axis=-1, keepdims=True)
                m2 = jnp.mean(dxh * xhat, axis=-1, keepdims=True)
                g_ref[...] = rstd * (dxh - m1 - xhat * m2)
                dgain_ref[...] += jnp.sum(dy * xhat, axis=0, keepdims=True)
                dbias_ref[...] += jnp.sum(dy, axis=0, keepdims=True)
        elif has_src:
            g_ref, w_ref, s_ref, o_ref = refs
        elif plus is not None:
            g_ref, w_ref, p_ref, o_ref = refs
        else:
            g_ref, w_ref, o_ref = refs
        nt = (((1,), (1,)), ((), ()))
        if sharded:
            r = sum(lax.dot_general(g_ref[:, s * n:(s + 1) * n].astype(MXU_DTYPE), w_ref[s].astype(MXU_DTYPE), nt, preferred_element_type=F32)
                    for s in range(w_ref.shape[0]))
        else:
            r = lax.dot_general(g_ref[...].astype(MXU_DTYPE), w_ref[...].astype(MXU_DTYPE), nt, preferred_element_type=F32)
        if has_src:
            r = r * (2.0 * jnp.maximum(s_ref[...].astype(F32), 0.0))
        first = r if plus is None else r + DN_ALPHA * p_ref[...]
        if nk == 1:
            o_ref[...] = first.astype(out_dtype)
        else:
            k = pl.program_id(2)

            @pl.when(k == 0)
            def _():
                o_ref[...] = first

            @pl.when(k > 0)
            def _():
                o_ref[...] += r

    w_spec = (pl.BlockSpec((w.shape[0], tn, n), lambda i, j, k: (0, j, 0)) if sharded
              else pl.BlockSpec((tn, tk), lambda i, j, k: (j, k)))
    rows = pl.BlockSpec((tm, tk), lambda i, j, k: (i, k))
    tile = pl.BlockSpec((tm, tn), lambda i, j, k: (i, j))
    vec = pl.BlockSpec((1, N), lambda i, j, k: (0, 0))
    in_specs, args = ([rows, w_spec], [g, w]) if ln is None else ([rows, vec, rows, w_spec], [ln[0], ln[1].reshape(1, N), ln[2], w])
    if has_src:
        assert nk == 1
        in_specs.append(tile)
        args.append(a_src)
    if plus is not None:
        in_specs.append(tile)
        args.append(plus)
    out_specs, out_shape = tile, jax.ShapeDtypeStruct((M, K), out_dtype)
    if ln is not None:
        out_specs = [tile, rows, vec, vec]
        out_shape = [out_shape, jax.ShapeDtypeStruct((M, N), F32), jax.ShapeDtypeStruct((1, N), F32), jax.ShapeDtypeStruct((1, N), F32)]
    return pl.pallas_call(
        body, name=name,
        grid=(M // tm, K // tn, nk),
        in_specs=in_specs,
        out_specs=out_specs,
        out_shape=out_shape,
        compiler_params=pltpu.CompilerParams(dimension_semantics=("parallel", "parallel", "arbitrary") if ln is None else ("arbitrary",) * 3),
    )(*args)


def _mm_tn(a, g, act, name, col_shards=1, out_dtype=F32):
    M, K = a.shape
    _, N = g.shape
    n = N // col_shards
    tm, tn = _pick(K, 1024, 128), _pick(n, 1024, 128)
    row_bytes = tm * a.dtype.itemsize + tn * g.dtype.itemsize
    tk = _pick(M, min(2176, TN_INPUT_VMEM_BYTES // (2 * row_bytes)), 8)
    nk = M // tk
    per = n // tn
    direct = out_dtype == F32

    def body(a_ref, g_ref, o_ref, *scratch):
        acc_ref = o_ref if direct else scratch[0]
        k = pl.program_id(2)
        av = a_ref[...]
        if act:
            av = _relu2(av.astype(F32))
        r = lax.dot_general(av.astype(MXU_DTYPE), g_ref[...].astype(MXU_DTYPE),
                            (((0,), (0,)), ((), ())), preferred_element_type=F32)

        @pl.when(k == 0)
        def _():
            acc_ref[...] = r

        @pl.when(k > 0)
        def _():
            acc_ref[...] += r

        if not direct:
            @pl.when(k == nk - 1)
            def _():
                o_ref[...] = acc_ref[...].astype(out_dtype)

    if col_shards == 1:
        out_spec, out_shape = pl.BlockSpec((tm, tn), lambda i, j, k: (i, j)), (K, N)
    else:
        out_spec, out_shape = pl.BlockSpec((None, tm, tn), lambda i, j, k: (j // per, i, j % per)), (col_shards, K, n)
    return pl.pallas_call(
        body, name=name,
        grid=(K // tm, N // tn, nk),
        in_specs=[pl.BlockSpec((tk, tm), lambda i, j, k: (k, i)), pl.BlockSpec((tk, tn), lambda i, j, k: (k, j))],
        out_specs=out_spec,
        out_shape=jax.ShapeDtypeStruct(out_shape, out_dtype),
        scratch_shapes=[] if direct else [pltpu.VMEM((tm, tn), F32)],
        compiler_params=pltpu.CompilerParams(dimension_semantics=("parallel", "parallel", "arbitrary")),
    )(a, g)


@functools.partial(jax.custom_vjp, nondiff_argnums=(3, 4, 5, 6))
def matmul(a, w, w_grad_slot, act, name, out_dtype, col_shards):
    return _mm_nn(a, w, act, name + "_fwd", out_dtype)


def _matmul_fwd(a, w, w_grad_slot, act, name, out_dtype, col_shards):
    return _mm_nn(a, w, act, name + "_fwd", out_dtype), (a, w, jnp.zeros((), w_grad_slot.dtype))


def _matmul_bwd(act, name, out_dtype, col_shards, res, g):
    a, w, slot_like = res
    w_grad_dtype = slot_like.dtype
    da = _mm_nt(g, w, a if act else None, name + "_dx")
    dw = _mm_tn(a, g, act, name + "_dw", col_shards, w_grad_dtype)
    return da, None, dw


matmul.defvjp(_matmul_fwd, _matmul_bwd)


@functools.partial(jax.custom_vjp, nondiff_argnums=(7,))
def mlp_block(h, w1, w2, w1_grad_slot, w2_grad_slot, g, b, name):
    return _mlp_block_fwd(h, w1, w2, w1_grad_slot, w2_grad_slot, g, b, name)[0]


def _mlp_block_fwd(h, w1, w2, w1_grad_slot, w2_grad_slot, g, b, name):
    u = _mm_nn(h, w1, False, name + "_w1_fwd", out_dtype=MXU_DTYPE)
    out, z = _mm_nn(u, w2, True, name + "_w2_norm_fwd", norm=(h, g, b))
    return out, (h, u, z, w1, w2, g, jnp.zeros((), w1_grad_slot.dtype))


def _mlp_block_bwd(name, res, dy):
    h, u, z, w1, w2, g, slot_like = res
    du, dz, dg, db = _mm_nt(None, w2, u, name + "_norm_w2_dx", out_dtype=MXU_DTYPE, ln=(z, g, dy))
    dw2 = _mm_tn(u, dz, True, name + "_w2_dw", 1, slot_like.dtype)
    dh = _mm_nt(du, w1, None, name + "_w1_dx", plus=dz)
    dw1 = _mm_tn(h, du, False, name + "_w1_dw", N_CHIPS, slot_like.dtype)
    return dh, None, None, dw1, dw2, dg.reshape(g.shape), db.reshape(g.shape)


mlp_block.defvjp(_mlp_block_fwd, _mlp_block_bwd)


@functools.partial(jax.custom_vjp, nondiff_argnums=(6,))
def out_block(h, y, w, w_grad_slot, g, b, name):
    return _out_block_fwd(h, y, w, w_grad_slot, g, b, name)[0]


def _out_block_fwd(h, y, w, w_grad_slot, g, b, name):
    out, z = _mm_nn(y, w, False, name + "_norm_fwd", norm=(h, g, b))
    return out, (y, z, w, g, jnp.zeros((), w_grad_slot.dtype))


def _out_block_bwd(name, res, dy):
    y, z, w, g, slot_like = res
    d_y, dz, dg, db = _mm_nt(None, w, None, name + "_norm_dx", ln=(z, g, dy))
    dw = _mm_tn(y, dz, False, name + "_dw", 1, slot_like.dtype)
    return DN_ALPHA * dz, d_y, None, dw, dg.reshape(g.shape), db.reshape(g.shape)


out_block.defvjp(_out_block_fwd, _out_block_bwd)


def _rms_fwd_call(x, g, name, col_block=0):
    R = x.shape[0]
    W = g.shape[-1]
    tr = _pick(R, 1088, 8)

    def body(x_ref, g_ref, o_ref):
        xv = x_ref[...]
        rstd = lax.rsqrt(jnp.mean(xv * xv, axis=-1, keepdims=True) + EPS)
        o_ref[...] = xv * rstd * g_ref[...]

    vec = pl.BlockSpec((1, W), lambda i: (0, 0))
    return pl.pallas_call(
        body, name=name, grid=(R // tr,), in_specs=[pl.BlockSpec((tr, W), lambda i: (i, col_block)), vec],
        out_specs=pl.BlockSpec((tr, W), lambda i: (i, 0)), out_shape=jax.ShapeDtypeStruct((R, W), F32),
        compiler_params=pltpu.CompilerParams(dimension_semantics=("parallel",)),
    )(x, g.reshape(1, W))


def _rms_bwd_call(x, g, dy, name, col_block=0):
    R = x.shape[0]
    W = g.shape[-1]
    tr = _pick(R, 1088, 8)

    def body(x_ref, g_ref, dy_ref, dx_ref, dg_ref):
        @pl.when(pl.program_id(0) == 0)
        def _():
            dg_ref[...] = jnp.zeros_like(dg_ref)

        xv = x_ref[...]
        rstd = lax.rsqrt(jnp.mean(xv * xv, axis=-1, keepdims=True) + EPS)
        xhat = xv * rstd
        dy = dy_ref[...]
        dxh = dy * g_ref[...]
        dx_ref[...] = rstd * (dxh - xhat * jnp.mean(dxh * xhat, axis=-1, keepdims=True))
        dg_ref[...] += jnp.sum(dy * xhat, axis=0, keepdims=True)

    row = pl.BlockSpec((tr, W), lambda i: (i, 0))
    vec = pl.BlockSpec((1, W), lambda i: (0, 0))
    return pl.pallas_call(
        body, name=name, grid=(R // tr,), in_specs=[pl.BlockSpec((tr, W), lambda i: (i, col_block)), vec, row], out_specs=[row, vec],
        out_shape=[jax.ShapeDtypeStruct((R, W), F32), jax.ShapeDtypeStruct((1, W), F32)],
        compiler_params=pltpu.CompilerParams(dimension_semantics=("arbitrary",)),
    )(x, g.reshape(1, W), dy)


def _loss_call(h, tgt, n_tokens, name):
    B, Tp, D = h.shape
    tr = _pick(Tp, 544, 8)

    def body(y_ref, t_ref, dy_ref, acc_ref):
        @pl.when(jnp.logical_and(pl.program_id(0) == 0, pl.program_id(1) == 0))
        def _():
            acc_ref[...] = jnp.zeros_like(acc_ref)

        t = lax.broadcasted_iota(jnp.int32, (tr, 1), 0) + pl.program_id(1) * tr
        counts = jnp.logical_and(t >= N_META, t < N_META + n_tokens)
        e = jnp.where(counts, y_ref[...] - t_ref[...], 0.0)
        dy_ref[...] = e * (1.0 / D)
        acc_ref[...] += jnp.sum(jnp.sum(e * e, axis=-1, keepdims=True), axis=0, keepdims=True) * (0.5 / D)

    row = pl.BlockSpec((None, tr, D), lambda b, i: (b, i, 0))
    one = pl.BlockSpec((1, 1), lambda b, i: (0, 0))
    return pl.pallas_call(
        body, name=name, grid=(B, Tp // tr), in_specs=[row, row], out_specs=[row, one],
        out_shape=[jax.ShapeDtypeStruct((B, Tp, D), F32), jax.ShapeDtypeStruct((1, 1), F32)],
        compiler_params=pltpu.CompilerParams(dimension_semantics=("arbitrary", "arbitrary")),
    )(h, tgt)


@functools.partial(jax.custom_vjp, nondiff_argnums=(2,))
def loss_head(h, tgt, n_tokens):
    return _loss_call(h, tgt, n_tokens, "loss_head")[1][0, 0]


def _loss_head_fwd(h, tgt, n_tokens):
    dy, acc = _loss_call(h, tgt, n_tokens, "loss_head")
    return acc[0, 0], dy


def _loss_head_bwd(n_tokens, dy, ct):
    return ct * dy, None


loss_head.defvjp(_loss_head_fwd, _loss_head_bwd)


_GELU_C = math.sqrt(2.0 / math.pi)


def _gelu_parts(x):
    x2 = x * x
    t = jnp.tanh(_GELU_C * (x + 0.044715 * x * x2))
    gelu = 0.5 * x * (1.0 + t)
    dgelu = 0.5 * (1.0 + t) + 0.5 * x * (1.0 - t * t) * (_GELU_C * (1.0 + 3.0 * 0.044715 * x2))
    return gelu, dgelu


def _sigmoid(x):
    return 1.0 / (1.0 + jnp.exp(-x))


def _scan8(a, b, carry, reverse):
    row = lax.broadcasted_iota(jnp.int32, a.shape, 0)
    for s in (1, 2, 4):
        shift = 8 - s if reverse else s
        keep = (row < 8 - s) if reverse else (row >= s)
        b = jnp.where(keep, a * pltpu.roll(b, shift, 0) + b, b)
        a = jnp.where(keep, a * pltpu.roll(a, shift, 0), a)
    return a * carry + b


def _lru_pre(prec_ref, prev_ref, first, cw_ref, cb_ref, wa_ref, ba_ref, wx_ref, bx_ref, sp_ref):
    tc = prec_ref.shape[0]
    prev = jnp.where(first, 0.0, prev_ref[...])
    ext = jnp.concatenate([prev, prec_ref[...]], axis=0)
    cw = cw_ref[...]
    taps = [ext[8:] if k == CONV_WIDTH - 1 else pltpu.roll(ext, CONV_WIDTH - 1 - k, 0)[8:] for k in range(CONV_WIDTH)]
    xc = cb_ref[...] + sum(cw[k:k + 1, :] * taps[k] for k in range(CONV_WIDTH))
    ga, gx = [], []
    for h in range(LRU_HEADS):
        xh = xc[:, h * LRU_HEAD_DIM:(h + 1) * LRU_HEAD_DIM].astype(MXU_DTYPE)
        ga.append(jnp.dot(xh, wa_ref[h].astype(MXU_DTYPE), preferred_element_type=F32))
        gx.append(jnp.dot(xh, wx_ref[h].astype(MXU_DTYPE), preferred_element_type=F32))
    r = _sigmoid(jnp.concatenate(ga, axis=1) + ba_ref[...])
    i = _sigmoid(jnp.concatenate(gx, axis=1) + bx_ref[...])
    log_a = -LRU_C * r * sp_ref[...]
    a = jnp.exp(log_a)
    a2 = a * a
    mult = jnp.sqrt(-jnp.tanh(log_a) * (a2 + 1.0))
    return taps, xc, r, i, a, a2, mult


def _lru_fwd_call(p, cw, cb, wa, ba, wx, bx, sp):
    B, Tp, _ = p.shape
    W = LRU_WIDTH
    tc = SEQ_BLOCK
    nc = Tp // tc

    def body(pg_ref, prec_ref, prev_ref, cw_ref, cb_ref, wa_ref, ba_ref, wx_ref, bx_ref, sp_ref, y_ref, h_ref, carry_ref):
        first = pl.program_id(1) == 0

        @pl.when(first)
        def _():
            carry_ref[...] = jnp.zeros_like(carry_ref)

        _, xc, r, i, a, a2, mult = _lru_pre(prec_ref, prev_ref, first, cw_ref, cb_ref, wa_ref, ba_ref, wx_ref, bx_ref, sp_ref)
        b = mult * (i * xc)
        carry = carry_ref[0:1, :]
        for t in range(tc // 8):
            h = _scan8(a[8 * t:8 * t + 8], b[8 * t:8 * t + 8], carry, False)
            h_ref[8 * t:8 * t + 8, :] = h
            carry = h[7:8, :]
        carry_ref[...] = jnp.broadcast_to(carry, carry_ref.shape)
        y_ref[...] = h_ref[...] * _gelu_parts(pg_ref[...])[0]

    cur = pl.BlockSpec((None, tc, W), lambda b, j: (b, j, 0))
    rec = pl.BlockSpec((None, tc, W), lambda b, j: (b, j, 1))
    prev = pl.BlockSpec((None, 8, W), lambda b, j: (b, jnp.maximum(j * (tc // 8) - 1, 0), 1))
    vec = pl.BlockSpec((1, W), lambda b, j: (0, 0))
    cws = pl.BlockSpec((CONV_WIDTH, W), lambda b, j: (0, 0))
    wsp = pl.BlockSpec((LRU_HEADS, LRU_HEAD_DIM, LRU_HEAD_DIM), lambda b, j: (0, 0, 0))
    return pl.pallas_call(
        body, name="lru_fwd", grid=(B, nc),
        in_specs=[cur, rec, prev, cws, vec, wsp, vec, wsp, vec, vec],
        out_specs=[cur, cur],
        out_shape=[jax.ShapeDtypeStruct((B, Tp, W + MLA_HEADS * HEAD_LANES), F32), jax.ShapeDtypeStruct((B, Tp, W), F32)],
        scratch_shapes=[pltpu.VMEM((8, W), F32)],
        compiler_params=pltpu.CompilerParams(dimension_semantics=("arbitrary", "arbitrary")),
    )(p, p, p, cw, cb, wa, ba, wx, bx, sp)


def _lru_bwd_call(p, hseq, dy, cw, cb, wa, ba, wx, bx, sp, dpq, dpkv, dkpe):
    B, Tp, P = p.shape
    W = LRU_WIDTH
    tc = SEQ_BLOCK
    nc = Tp // tc
    HD = LRU_HEAD_DIM

    def body(pg_ref, prec_ref, prev_ref, h_ref, hprev_ref, dy_ref, cw_ref, cb_ref, wa_ref, ba_ref, wx_ref, bx_ref, sp_ref,
             dpq_ref, dpkv_ref, dkpe_ref, dp_ref, dcw_ref, dcb_ref, dwa_ref, dba_ref, dwx_ref, dbx_ref, dsp_ref,
             gcar_ref, anext_ref, halo_ref, g_ref):
        j = pl.program_id(1)
        first = j == nc - 1
        last = j == 0

        @pl.when(jnp.logical_and(pl.program_id(0) == 0, last))
        def _():
            for ref in (dcw_ref, dcb_ref, dwa_ref, dba_ref, dwx_ref, dbx_ref, dsp_ref):
                ref[...] = jnp.zeros_like(ref)

        @pl.when(last)
        def _():
            gcar_ref[...] = jnp.zeros_like(gcar_ref)
            anext_ref[...] = jnp.zeros_like(anext_ref)
            halo_ref[...] = jnp.zeros_like(halo_ref)

        taps, xc, r, i, a, a2, mult = _lru_pre(prec_ref, prev_ref, first, cw_ref, cb_ref, wa_ref, ba_ref, wx_ref, bx_ref, sp_ref)
        row = lax.broadcasted_iota(jnp.int32, (tc, W), 0)
        gelu, dgelu = _gelu_parts(pg_ref[...])
        dy = dy_ref[...]
        hcur = h_ref[...]
        dp_ref[:, 0:W] = dy * hcur * dgelu
        dp_ref[:, 2 * W:2 * W + MLA_Q_RANK] = dpq_ref[...]
        dp_ref[:, _KPE_START - MLA_KV_RANK:_KPE_START] = dpkv_ref[...]
        dp_ref[:, _KPE_START:P] = pltpu.roll(dkpe_ref[...], HEAD_LANES - MLA_NOPE, 1)[:, 0:P - _KPE_START]
        dh = dy * gelu
        a_next = jnp.where(row == tc - 1, anext_ref[0:1, :], pltpu.roll(a, tc - 1, 0))
        carry = gcar_ref[0:1, :]
        for t in reversed(range(tc // 8)):
            g = _scan8(a_next[8 * t:8 * t + 8], dh[8 * t:8 * t + 8], carry, True)
            g_ref[8 * t:8 * t + 8, :] = g
            carry = g[0:1, :]
        gcar_ref[...] = jnp.broadcast_to(carry, gcar_ref.shape)
        anext_ref[...] = jnp.broadcast_to(a[0:1, :], anext_ref.shape)
        G = g_ref[...]
        h_before = jnp.where(first, 0.0, hprev_ref[7:8, :])
        hprev = jnp.where(row == 0, h_before, pltpu.roll(hcur, 1, 0))
        d_a = G * hprev
        gx_ = G * xc
        d_mult = gx_ * i
        d_i = gx_ * mult
        dxc = G * (mult * i)
        d_la = d_a * a - d_mult * (a2 / mult)
        sp = sp_ref[...]
        d_r = d_la * (-LRU_C * sp)
        dsp_ref[...] += jnp.sum(d_la * (-LRU_C * r), axis=0, keepdims=True)
        dga = d_r * r * (1.0 - r)
        dgx = d_i * i * (1.0 - i)
        dba_ref[...] += jnp.sum(dga, axis=0, keepdims=True)
        dbx_ref[...] += jnp.sum(dgx, axis=0, keepdims=True)
        back = []
        for h in range(LRU_HEADS):
            sl = slice(h * HD, (h + 1) * HD)
            xh = xc[:, sl].astype(MXU_DTYPE)
            ah = dga[:, sl].astype(MXU_DTYPE)
            bh = dgx[:, sl].astype(MXU_DTYPE)
            tn = (((0,), (0,)), ((), ()))
            nt = (((1,), (1,)), ((), ()))
            dwa_ref[h] += lax.dot_general(xh, ah, tn, preferred_element_type=F32)
            dwx_ref[h] += lax.dot_general(xh, bh, tn, preferred_element_type=F32)
            back.append(lax.dot_general(ah, wa_ref[h].astype(MXU_DTYPE), nt, preferred_element_type=F32)
                        + lax.dot_general(bh, wx_ref[h].astype(MXU_DTYPE), nt, preferred_element_type=F32))
        dxc = dxc + jnp.concatenate(back, axis=1)
        dcb_ref[...] += jnp.sum(dxc, axis=0, keepdims=True)
        for k in range(CONV_WIDTH):
            dcw_ref[k:k + 1, :] += jnp.sum(dxc * taps[k], axis=0, keepdims=True)
        ext = jnp.concatenate([dxc, halo_ref[...]], axis=0)
        cw = cw_ref[...]
        acc = cw[CONV_WIDTH - 1:CONV_WIDTH, :] * dxc
        for k in range(CONV_WIDTH - 1):
            s = CONV_WIDTH - 1 - k
            acc = acc + cw[k:k + 1, :] * pltpu.roll(ext, tc + 8 - s, 0)[:tc]
        dp_ref[:, W:2 * W] = acc
        halo_ref[...] = dxc[0:8, :]

    rev = lambda j: nc - 1 - j
    cur = pl.BlockSpec((None, tc, W), lambda b, j: (b, rev(j), 0))
    rec = pl.BlockSpec((None, tc, W), lambda b, j: (b, rev(j), 1))
    prev = pl.BlockSpec((None, 8, W), lambda b, j: (b, jnp.maximum(rev(j) * (tc // 8) - 1, 0), 0))
    prev_rec = pl.BlockSpec((None, 8, W), lambda b, j: (b, jnp.maximum(rev(j) * (tc // 8) - 1, 0), 1))
    vec = pl.BlockSpec((1, W), lambda b, j: (0, 0))
    cws = pl.BlockSpec((CONV_WIDTH, W), lambda b, j: (0, 0))
    wsp = pl.BlockSpec((LRU_HEADS, HD, HD), lambda b, j: (0, 0, 0))
    vs = jax.ShapeDtypeStruct((1, W), F32)
    ws = jax.ShapeDtypeStruct((LRU_HEADS, HD, HD), F32)

    def rows(width):
        return pl.BlockSpec((None, tc, width), lambda b, j: (b, rev(j), 0))

    return pl.pallas_call(
        body, name="lru_bwd", grid=(B, nc),
        in_specs=[cur, rec, prev_rec, cur, prev, cur, cws, vec, wsp, vec, wsp, vec, vec, rows(MLA_Q_RANK), rows(MLA_KV_RANK), rows(HEAD_LANES)],
        out_specs=[rows(P), cws, vec, wsp, vec, wsp, vec, vec],
        out_shape=[jax.ShapeDtypeStruct((B, Tp, P), F32), jax.ShapeDtypeStruct((CONV_WIDTH, W), F32), vs, ws, vs, ws, vs, vs],
        scratch_shapes=[pltpu.VMEM((8, W), F32), pltpu.VMEM((8, W), F32), pltpu.VMEM((8, W), F32), pltpu.VMEM((tc, W), F32)],
        compiler_params=pltpu.CompilerParams(dimension_semantics=("arbitrary", "arbitrary")),
    )(p, p, p, hseq, hseq, dy, cw, cb, wa, ba, wx, bx, sp, dpq, dpkv, dkpe)


_Q_BLOCK = 2 * LRU_WIDTH // MLA_Q_RANK
_KV_BLOCK = (2 * LRU_WIDTH + MLA_Q_RANK) // MLA_KV_RANK
_KPE_START = 2 * LRU_WIDTH + MLA_Q_RANK + MLA_KV_RANK


@jax.custom_vjp
def even_front(p, cw, cb, wa, ba, wx, bx, sp, gq, gkv):
    return _even_front_fwd(p, cw, cb, wa, ba, wx, bx, sp, gq, gkv)[0]


def _even_front_fwd(p, cw, cb, wa, ba, wx, bx, sp, gq, gkv):
    B, Tp, W = p.shape
    p2d = p.reshape(B * Tp, W)
    y, hseq = _lru_fwd_call(p, cw, cb, wa, ba, wx, bx, sp)
    qn = _rms_fwd_call(p2d, gq, "q_norm_fwd", _Q_BLOCK)
    kvn = _rms_fwd_call(p2d, gkv, "kv_norm_fwd", _KV_BLOCK)
    kpe = jnp.pad(p[:, :, _KPE_START:], ((0, 0), (0, 0), (MLA_NOPE, HEAD_LANES - MLA_NOPE - MLA_ROPE)))
    return (y, qn, kvn, kpe), (p, hseq, cw, cb, wa, ba, wx, bx, sp, gq, gkv)


def _even_front_bwd(res, cts):
    p, hseq, cw, cb, wa, ba, wx, bx, sp, gq, gkv = res
    dy, dqn, dkvn, dkpe = cts
    B, Tp, W = p.shape
    p2d = p.reshape(B * Tp, W)
    dpq, dgq = _rms_bwd_call(p2d, gq, dqn, "q_norm_bwd", _Q_BLOCK)
    dpkv, dgkv = _rms_bwd_call(p2d, gkv, dkvn, "kv_norm_bwd", _KV_BLOCK)
    dp, dcw, dcb, dwa, dba, dwx, dbx, dsp = _lru_bwd_call(p, hseq, dy, cw, cb, wa, ba, wx, bx, sp, dpq.reshape(B, Tp, -1),
                                                          dpkv.reshape(B, Tp, -1), dkpe)
    return dp, dcw, dcb, dwa, dba, dwx, dbx, dsp, dgq.reshape(gq.shape), dgkv.reshape(gkv.shape)


even_front.defvjp(_even_front_fwd, _even_front_bwd)


def _rope_tables(pos, half):
    inv = ROPE_BASE ** (-jnp.arange(half, dtype=F32) / half)
    ang = pos.astype(F32)[:, None] * inv[None, :]
    return jnp.cos(ang), jnp.sin(ang)


_NT = (((1,), (1,)), ((), ()))
_TN = (((0,), (0,)), ((), ()))
HEAD_LANES = 128
_MLA_SCALE = (MLA_NOPE + MLA_ROPE) ** -0.5
_LOG2E = math.log2(math.e)


Q_BLOCK = 512


def _query_blocks(Tp):
    first = Tp % Q_BLOCK or Q_BLOCK
    return [(0, first)] + [(r, r + Q_BLOCK) for r in range(first, Tp, Q_BLOCK)]


def _mask_diagonal(s, fill):
    R, L = s.shape
    row = lax.broadcasted_iota(jnp.int32, (R, R), 0)
    col = lax.broadcasted_iota(jnp.int32, (R, R), 1)
    last = jnp.where(col <= row, s[:, L - R:], fill)
    return last if L == R else jnp.concatenate([s[:, :L - R], last], axis=1)


def _mla_rope_tables(pos):
    half = MLA_ROPE // 2
    cos, sin = _rope_tables(pos, half)
    T = pos.shape[0]
    ones, zeros = jnp.ones((T, MLA_NOPE), F32), jnp.zeros((T, MLA_NOPE), F32)
    tail1, tail0 = jnp.ones((T, HEAD_LANES - MLA_NOPE - MLA_ROPE), F32), jnp.zeros((T, HEAD_LANES - MLA_NOPE - MLA_ROPE), F32)
    zh = jnp.zeros((T, half), F32)
    c = jnp.concatenate([ones, cos, cos, tail1], axis=1)
    s_up = jnp.concatenate([zeros, -sin, zh, tail0], axis=1)
    s_down = jnp.concatenate([zeros, zh, sin, tail0], axis=1)
    return c, s_up, s_down


def _rope_lanes(x, c, s_up, s_down):
    half = MLA_ROPE // 2
    return x * c + pltpu.roll(x, HEAD_LANES - half, 1) * s_up + pltpu.roll(x, half, 1) * s_down


def _unrope_lanes(d, c, s_up, s_down):
    half = MLA_ROPE // 2
    return d * c + pltpu.roll(d * s_up, half, 1) + pltpu.roll(d * s_down, HEAD_LANES - half, 1)


def _mla_operands(q_ref, kv_ref, kpe_ref, c, s_up, s_down):
    lane = lax.broadcasted_iota(jnp.int32, kv_ref.shape, 1)
    qr = (_rope_lanes(q_ref[...].astype(F32), c, s_up, s_down) * (_MLA_SCALE * _LOG2E)).astype(MXU_DTYPE)
    kr = jnp.where(lane < MLA_NOPE, kv_ref[...].astype(F32), _rope_lanes(kpe_ref[...], c, s_up, s_down)).astype(MXU_DTYPE)
    return qr, kr, lane


def _mla_specs(Tp):
    head = pl.BlockSpec((None, Tp, HEAD_LANES), lambda b, h: (b, 0, h))
    shared = pl.BlockSpec((None, Tp, HEAD_LANES), lambda b, h: (b, 0, 0))
    tab = pl.BlockSpec((Tp, HEAD_LANES), lambda b, h: (0, 0))
    lse = pl.BlockSpec((None, None, Tp, 1), lambda b, h: (b, h, 0, 0))
    out = pl.BlockSpec((None, Tp, HEAD_LANES), lambda b, h: (b, 0, LRU_WIDTH // HEAD_LANES + h))
    return head, shared, tab, lse, out


def _attn_fwd_call(q, kv, kpe, tabs, y):
    B, Tp, _ = q.shape

    def body(q_ref, kv_ref, kpe_ref, c_ref, su_ref, sd_ref, y_ref, o_ref, lse_ref, qr_ref, kr_ref):
        qr, kr, lane = _mla_operands(q_ref, kv_ref, kpe_ref, c_ref[...], su_ref[...], sd_ref[...])
        qr_ref[...] = qr
        kr_ref[...] = kr
        for r0, L in _query_blocks(Tp):
            blk = slice(r0, L)
            s = _mask_diagonal(lax.dot_general(qr_ref[blk, :], kr_ref[0:L, :], _NT, preferred_element_type=F32), NEG_INF)
            m = jnp.max(s, axis=-1, keepdims=True)
            p = jnp.exp2(s - m)
            l = jnp.sum(p, axis=-1, keepdims=True)
            o = jnp.dot(p.astype(MXU_DTYPE), kv_ref[0:L, :].astype(MXU_DTYPE), preferred_element_type=F32)
            o_ref[blk, :] = jnp.where(lane[blk, :] >= MLA_NOPE, o / l, 0.0)
            lse_ref[blk, :] = m + jnp.log2(l)

    head, shared, tab, lse, out = _mla_specs(Tp)
    return pl.pallas_call(
        body, name="mla_attn_fwd", grid=(B, MLA_HEADS),
        in_specs=[head, head, shared, tab, tab, tab, pl.BlockSpec(memory_space=pl.ANY)], out_specs=[out, lse],
        out_shape=[jax.ShapeDtypeStruct(y.shape, F32), jax.ShapeDtypeStruct((B, MLA_HEADS, Tp, 1), F32)],
        input_output_aliases={6: 0},
        scratch_shapes=[pltpu.VMEM((Tp, HEAD_LANES), MXU_DTYPE), pltpu.VMEM((Tp, HEAD_LANES), MXU_DTYPE)],
        compiler_params=pltpu.CompilerParams(dimension_semantics=("parallel", "parallel")),
    )(q, kv, kpe, *tabs, y)


def _attn_bwd_call(q, kv, kpe, tabs, o, lse, do):
    B, Tp, _ = q.shape

    def body(q_ref, kv_ref, kpe_ref, c_ref, su_ref, sd_ref, o_ref, lse_ref, do_ref, dq_ref, dkv_ref, dkpe_ref,
             qr_ref, kr_ref, dqa_ref, dka_ref, dva_ref):
        c, s_up, s_down = c_ref[...], su_ref[...], sd_ref[...]
        qr, kr, lane = _mla_operands(q_ref, kv_ref, kpe_ref, c, s_up, s_down)
        qr_ref[...] = qr
        kr_ref[...] = kr
        dka_ref[...] = jnp.zeros_like(dka_ref)
        dva_ref[...] = jnp.zeros_like(dva_ref)
        for r0, L in _query_blocks(Tp):
            blk = slice(r0, L)
            qb = qr_ref[blk, :]
            do = jnp.where(lane[blk, :] >= MLA_NOPE, do_ref[blk, :], 0.0)
            delta = jnp.sum(do * o_ref[blk, :], axis=-1, keepdims=True)
            s = _mask_diagonal(lax.dot_general(qb, kr_ref[0:L, :], _NT, preferred_element_type=F32), NEG_INF)
            p = jnp.exp2(s - lse_ref[blk, :])
            dob = do.astype(MXU_DTYPE)
            dva_ref[0:L, :] += lax.dot_general(p.astype(MXU_DTYPE), dob, _TN, preferred_element_type=F32)
            dp = lax.dot_general(dob, kv_ref[0:L, :].astype(MXU_DTYPE), _NT, preferred_element_type=F32)
            ds = (p * (dp - delta)).astype(MXU_DTYPE)
            dqa_ref[blk, :] = jnp.dot(ds, kr_ref[0:L, :], preferred_element_type=F32)
            dka_ref[0:L, :] += lax.dot_general(ds, qb, _TN, preferred_element_type=F32)
        dq_ref[...] = _unrope_lanes(dqa_ref[...] * _MLA_SCALE, c, s_up, s_down).astype(dq_ref.dtype)
        dk = dka_ref[...] * (1.0 / _LOG2E)
        dkv_ref[...] = jnp.where(lane < MLA_NOPE, dk, dva_ref[...]).astype(dkv_ref.dtype)
        dkpe = jnp.where(lane >= MLA_NOPE, _unrope_lanes(dk, c, s_up, s_down), 0.0)

        @pl.when(pl.program_id(1) == 0)
        def _():
            dkpe_ref[...] = dkpe

        @pl.when(pl.program_id(1) > 0)
        def _():
            dkpe_ref[...] += dkpe

    head, shared, tab, lse_spec, out = _mla_specs(Tp)
    wide = jax.ShapeDtypeStruct((B, Tp, MLA_HEADS * HEAD_LANES), q.dtype)
    acc = pltpu.VMEM((Tp, HEAD_LANES), F32)
    return pl.pallas_call(
        body, name="mla_attn_bwd", grid=(B, MLA_HEADS),
        in_specs=[head, head, shared, tab, tab, tab, out, lse_spec, out], out_specs=[head, head, shared],
        out_shape=[wide, wide, jax.ShapeDtypeStruct((B, Tp, HEAD_LANES), F32)],
        scratch_shapes=[pltpu.VMEM((Tp, HEAD_LANES), MXU_DTYPE), pltpu.VMEM((Tp, HEAD_LANES), MXU_DTYPE), acc, acc, acc],
        compiler_params=pltpu.CompilerParams(dimension_semantics=("parallel", "arbitrary")),
    )(q, kv, kpe, *tabs, o, lse, do)


@jax.custom_vjp
def mla_attention(q, kv, kpe, tabs, y):
    return _attn_fwd_call(q, kv, kpe, tabs, y)[0]


def _mla_attention_fwd(q, kv, kpe, tabs, y):
    o, lse = _attn_fwd_call(q, kv, kpe, tabs, y)
    return o, (q, kv, kpe, tabs, o, lse)


def _mla_attention_bwd(res, do):
    q, kv, kpe, tabs, o, lse = res
    dq, dkv, dkpe = _attn_bwd_call(q, kv, kpe, tabs, o, lse, do)
    return dq, dkv, dkpe, None, do


mla_attention.defvjp(_mla_attention_fwd, _mla_attention_bwd)


def _rope_halves(x, cos, sin):
    half = x.shape[1] // 2
    x1, x2 = x[:, :half], x[:, half:]
    return jnp.concatenate([x1 * cos - x2 * sin, x1 * sin + x2 * cos], axis=1)


def _unrope_halves(d, cos, sin):
    half = d.shape[1] // 2
    d1, d2 = d[:, :half], d[:, half:]
    return jnp.concatenate([d1 * cos + d2 * sin, d2 * cos - d1 * sin], axis=1)


_RET_K_SCALE = RET_QK_DIM ** -0.5
_RET_Q_BLOCKS = RET_HEADS
_RET_V_BLOCK0 = 2 * RET_HEADS * RET_QK_DIM // RET_V_DIM
_RET_G_BLOCK0 = _RET_V_BLOCK0 + RET_HEADS


def _ret_specs(Tp):
    q = pl.BlockSpec((None, Tp, RET_QK_DIM), lambda b, h: (b, 0, h))
    k = pl.BlockSpec((None, Tp, RET_QK_DIM), lambda b, h: (b, 0, _RET_Q_BLOCKS + h))
    v = pl.BlockSpec((None, Tp, RET_V_DIM), lambda b, h: (b, 0, _RET_V_BLOCK0 + h))
    tab = pl.BlockSpec((Tp, RET_QK_DIM // 2), lambda b, h: (0, 0))
    lg = pl.BlockSpec((None, 1, 1), lambda b, h: (h, 0, 0))
    return q, k, v, tab, lg


def _ret_operands(q_ref, k_ref, cos, sin, lg):
    t = lax.broadcasted_iota(jnp.int32, (q_ref.shape[0], 1), 0).astype(F32)
    grow, shrink = jnp.exp(-lg * t), jnp.exp(lg * t)
    qs = (_rope_halves(q_ref[...].astype(F32), cos, sin) * shrink).astype(MXU_DTYPE)
    ks = (_rope_halves(k_ref[...].astype(F32), cos, sin) * (grow * _RET_K_SCALE)).astype(MXU_DTYPE)
    return qs, ks, shrink, grow * _RET_K_SCALE


def _ret_core_fwd_call(p, cos, sin, lg):
    B, Tp, _ = p.shape

    def body(q_ref, k_ref, v_ref, cos_ref, sin_ref, lg_ref, o_ref, qs_ref, ks_ref):
        qs_ref[...], ks_ref[...], _, _ = _ret_operands(q_ref, k_ref, cos_ref[...], sin_ref[...], lg_ref[...])
        for r0, L in _query_blocks(Tp):
            blk = slice(r0, L)
            s = _mask_diagonal(lax.dot_general(qs_ref[blk, :], ks_ref[0:L, :], _NT, preferred_element_type=F32), 0.0)
            o_ref[blk, :] = jnp.dot(s.astype(MXU_DTYPE), v_ref[0:L, :].astype(MXU_DTYPE), preferred_element_type=F32)

    q, k, v, tab, lgs = _ret_specs(Tp)
    return pl.pallas_call(
        body, name="retention_fwd", grid=(B, RET_HEADS), in_specs=[q, k, v, tab, tab, lgs],
        out_specs=pl.BlockSpec((None, Tp, RET_V_DIM), lambda b, h: (b, 0, h)),
        out_shape=jax.ShapeDtypeStruct((B, Tp, RET_HEADS * RET_V_DIM), F32),
        scratch_shapes=[pltpu.VMEM((Tp, RET_QK_DIM), MXU_DTYPE), pltpu.VMEM((Tp, RET_QK_DIM), MXU_DTYPE)],
        compiler_params=pltpu.CompilerParams(dimension_semantics=("parallel", "parallel")),
    )(p, p, p, cos, sin, lg)


def _ret_core_bwd_call(p, do, cos, sin, lg):
    B, Tp, _ = p.shape

    def body(q_ref, k_ref, v_ref, do_ref, cos_ref, sin_ref, lg_ref, dq_ref, dk_ref, dv_ref, qs_ref, ks_ref, dqa_ref, dka_ref, dva_ref):
        cos_, sin_ = cos_ref[...], sin_ref[...]
        qs_ref[...], ks_ref[...], q_scale, k_scale = _ret_operands(q_ref, k_ref, cos_, sin_, lg_ref[...])
        dka_ref[...] = jnp.zeros_like(dka_ref)
        dva_ref[...] = jnp.zeros_like(dva_ref)
        for r0, L in _query_blocks(Tp):
            blk = slice(r0, L)
            qb = qs_ref[blk, :]
            dob = do_ref[blk, :].astype(MXU_DTYPE)
            s = _mask_diagonal(lax.dot_general(qb, ks_ref[0:L, :], _NT, preferred_element_type=F32), 0.0).astype(MXU_DTYPE)
            dva_ref[0:L, :] += lax.dot_general(s, dob, _TN, preferred_element_type=F32)
            ds = _mask_diagonal(lax.dot_general(dob, v_ref[0:L, :].astype(MXU_DTYPE), _NT, preferred_element_type=F32), 0.0).astype(MXU_DTYPE)
            dqa_ref[blk, :] = jnp.dot(ds, ks_ref[0:L, :], preferred_element_type=F32)
            dka_ref[0:L, :] += lax.dot_general(ds, qb, _TN, preferred_element_type=F32)
        dq_ref[...] = _unrope_halves(dqa_ref[...] * q_scale, cos_, sin_).astype(dq_ref.dtype)
        dk_ref[...] = _unrope_halves(dka_ref[...] * k_scale, cos_, sin_).astype(dk_ref.dtype)
        dv_ref[...] = dva_ref[...].astype(dv_ref.dtype)

    q, k, v, tab, lgs = _ret_specs(Tp)
    qk_out = pl.BlockSpec((None, Tp, RET_QK_DIM), lambda b, h: (b, 0, h))
    v_out = pl.BlockSpec((None, Tp, RET_V_DIM), lambda b, h: (b, 0, h))
    return pl.pallas_call(
        body, name="retention_bwd", grid=(B, RET_HEADS), in_specs=[q, k, v, v_out, tab, tab, lgs],
        out_specs=[qk_out, qk_out, v_out],
        out_shape=[jax.ShapeDtypeStruct((B, Tp, RET_HEADS * RET_QK_DIM), p.dtype), jax.ShapeDtypeStruct((B, Tp, RET_HEADS * RET_QK_DIM), p.dtype),
                   jax.ShapeDtypeStruct((B, Tp, RET_HEADS * RET_V_DIM), p.dtype)],
        scratch_shapes=[pltpu.VMEM((Tp, RET_QK_DIM), MXU_DTYPE), pltpu.VMEM((Tp, RET_QK_DIM), MXU_DTYPE),
                        pltpu.VMEM((Tp, RET_QK_DIM), F32), pltpu.VMEM((Tp, RET_QK_DIM), F32), pltpu.VMEM((Tp, RET_V_DIM), F32)],
        compiler_params=pltpu.CompilerParams(dimension_semantics=("parallel", "parallel")),
    )(p, p, p, do, cos, sin, lg)


def _ret_gate_specs(M):
    tm = _pick(M, 1088, 8)
    head = pl.BlockSpec((tm, RET_V_DIM), lambda i, h: (i, h))
    gate = pl.BlockSpec((tm, RET_V_DIM), lambda i, h: (i, _RET_G_BLOCK0 + h))
    return tm, head, gate


def _ret_gate_fwd_call(o, p2d):
    M = o.shape[0]
    tm, head, gate = _ret_gate_specs(M)

    def body(o_ref, g_ref, y_ref):
        ov = o_ref[...]
        gv = g_ref[...].astype(F32)
        rstd = lax.rsqrt(jnp.mean(ov * ov, axis=-1, keepdims=True) + EPS)
        y_ref[...] = (gv * _sigmoid(gv)) * (ov * rstd)

    return pl.pallas_call(
        body, name="retention_gate_fwd", grid=(M // tm, RET_HEADS), in_specs=[head, gate], out_specs=head,
        out_shape=jax.ShapeDtypeStruct(o.shape, F32),
        compiler_params=pltpu.CompilerParams(dimension_semantics=("parallel", "parallel")),
    )(o, p2d)


def _ret_gate_bwd_call(o, p2d, dy):
    M = o.shape[0]
    tm, head, gate = _ret_gate_specs(M)

    def body(o_ref, g_ref, dy_ref, do_ref, dg_ref):
        ov = o_ref[...]
        gv = g_ref[...].astype(F32)
        dy = dy_ref[...]
        rstd = lax.rsqrt(jnp.mean(ov * ov, axis=-1, keepdims=True) + EPS)
        on = ov * rstd
        sg = _sigmoid(gv)
        dg_ref[...] = (dy * on * (sg * (1.0 + gv * (1.0 - sg)))).astype(dg_ref.dtype)
        don = dy * (gv * sg)
        do_ref[...] = (rstd * (don - on * jnp.mean(don * on, axis=-1, keepdims=True))).astype(do_ref.dtype)

    shp = jax.ShapeDtypeStruct(o.shape, p2d.dtype)
    return pl.pallas_call(
        body, name="retention_gate_bwd", grid=(M // tm, RET_HEADS), in_specs=[head, gate, head], out_specs=[head, head],
        out_shape=[shp, shp],
        compiler_params=pltpu.CompilerParams(dimension_semantics=("parallel", "parallel")),
    )(o, p2d, dy)


def _log_gamma():
    return jnp.log(1.0 - 2.0 ** (-5.0 - jnp.arange(RET_HEADS, dtype=F32))).reshape(RET_HEADS, 1, 1)


@functools.partial(jax.custom_vjp, nondiff_argnums=(9,))
def retention_block(h, w_in, w_out, w_in_grad_slot, w_out_grad_slot, g, b, cos, sin, dims):
    return _retention_block_fwd(h, w_in, w_out, w_in_grad_slot, w_out_grad_slot, g, b, cos, sin, dims)[0]


def _retention_block_fwd(h, w_in, w_out, w_in_grad_slot, w_out_grad_slot, g, b, cos, sin, dims):
    B, Tp = dims
    p = _mm_nn(h, w_in, False, "od_w_in_fwd", out_dtype=MXU_DTYPE)
    o = _ret_core_fwd_call(p.reshape(B, Tp, -1), cos, sin, _log_gamma())
    y = _ret_gate_fwd_call(o.reshape(B * Tp, -1), p)
    out, z = _mm_nn(y, w_out, False, "od_w_out_norm_fwd", norm=(h, g, b))
    return out, (h, p, o, y, z, w_in, w_out, g, cos, sin, jnp.zeros((), w_in_grad_slot.dtype))


def _retention_block_bwd(dims, res, dout):
    B, Tp = dims
    h, p, o, y, z, w_in, w_out, g, cos, sin, slot_like = res
    dy, dz, dg, db = _mm_nt(None, w_out, None, "od_w_out_norm_dx", ln=(z, g, dout))
    dw_out = _mm_tn(y, dz, False, "od_w_out_dw", 1, slot_like.dtype)
    do, dgate = _ret_gate_bwd_call(o.reshape(B * Tp, -1), p, dy)
    dq, dk, dv = _ret_core_bwd_call(p.reshape(B, Tp, -1), do.reshape(B, Tp, -1), cos, sin, _log_gamma())
    dp = jnp.concatenate([dq.reshape(B * Tp, -1), dk.reshape(B * Tp, -1), dv.reshape(B * Tp, -1), dgate], axis=-1)
    dh = _mm_nt(dp, w_in, None, "od_w_in_dx", plus=dz)
    dw_in = _mm_tn(h, dp, False, "od_w_in_dw", N_CHIPS, slot_like.dtype)
    return dh, None, None, dw_in, dw_out, dg.reshape(g.shape), db.reshape(g.shape), None, None


retention_block.defvjp(_retention_block_fwd, _retention_block_bwd)


def _heads_to_lanes(w):
    K = w.shape[0]
    w = w.reshape(K, MLA_HEADS, MLA_NOPE + MLA_ROPE)
    return jnp.pad(w, ((0, 0), (0, 0), (0, HEAD_LANES - MLA_NOPE - MLA_ROPE))).reshape(K, MLA_HEADS * HEAD_LANES)


def _out_rows_to_lanes(w):
    N = w.shape[1]
    att = w[LRU_WIDTH:].reshape(MLA_HEADS, MLA_V, N)
    att = jnp.pad(att, ((0, 0), (HEAD_LANES - MLA_V, 0), (0, 0))).reshape(MLA_HEADS * HEAD_LANES, N)
    return jnp.concatenate([w[:LRU_WIDTH], att], axis=0)


def _seq_dims(x):
    B, S, D = x.shape
    T = S + N_META
    Tp = _round_up(T, SEQ_BLOCK)
    return B, S, T, Tp


def _mixer0(diff, w, token):
    x = diff["x"]
    B, S, T, Tp = _seq_dims(x)
    D = x.shape[-1]
    M = B * Tp
    pos = jnp.arange(Tp, dtype=jnp.int32)

    def mm(a, name, act=False, out_dtype=F32, layout=lambda m: m, col_shards=1):
        return matmul(a, layout(w[name]), layout(diff[name]), act, name, out_dtype, col_shards)

    meta = jnp.broadcast_to((diff["meta_tokens"] + token)[None], (B, N_META, D))
    h = jnp.concatenate([meta, x, jnp.zeros((B, Tp - T, D), F32)], axis=1).reshape(M, D)
    p = mm(h, "ev_w_in")
    sp = jax.nn.softplus(-diff["ev_lru_lambda"]).reshape(1, LRU_WIDTH)
    y, qn, kvn, kpe = even_front(
        p.reshape(B, Tp, -1), diff["ev_conv_w"].reshape(CONV_WIDTH, LRU_WIDTH), diff["ev_conv_b"].reshape(1, LRU_WIDTH),
        diff["ev_w_rg_a"].reshape(LRU_HEADS, LRU_HEAD_DIM, LRU_HEAD_DIM), diff["ev_b_rg_a"].reshape(1, LRU_WIDTH),
        diff["ev_w_rg_x"].reshape(LRU_HEADS, LRU_HEAD_DIM, LRU_HEAD_DIM), diff["ev_b_rg_x"].reshape(1, LRU_WIDTH),
        sp, diff["ev_q_norm_g"].reshape(-1), diff["ev_kv_norm_g"].reshape(-1))
    q = mm(qn, "ev_w_uq", out_dtype=MXU_DTYPE, layout=_heads_to_lanes).reshape(B, Tp, -1)
    kv = mm(kvn, "ev_w_ukv", out_dtype=MXU_DTYPE).reshape(B, Tp, -1)
    y = mla_attention(q, kv, kpe, _mla_rope_tables(pos), y).reshape(M, -1)
    return out_block(h, y, _out_rows_to_lanes(w["ev_w_out"]), _out_rows_to_lanes(diff["ev_w_out"]),
                     diff["ln_mix_g"], diff["ln_mix_b"], "ev_w_out")


def _mlp0(diff, h, w):
    return mlp_block(h, w["mlp_w1_0"], w["mlp_w2_0"], diff["mlp_w1_0"], diff["mlp_w2_0"], diff["ln_mlp_g"], diff["ln_mlp_b"], "mlp0")


def _layer1_loss(diff, h, w, tgt):
    B, S, T, Tp = _seq_dims(tgt)
    D = tgt.shape[-1]
    pos = jnp.arange(Tp, dtype=jnp.int32)

    cos, sin = _rope_tables(pos, RET_QK_DIM // 2)
    h = retention_block(h, w["od_w_in"], w["od_w_out"], diff["od_w_in"], diff["od_w_out"], diff["ln_mix_g"], diff["ln_mix_b"], cos, sin, (B, Tp))
    h = mlp_block(h, w["mlp_w1_1"], w["mlp_w2_1"], diff["mlp_w1_1"], diff["mlp_w2_1"], diff["ln_mlp_g"], diff["ln_mlp_b"], "mlp1")
    return loss_head(h.reshape(B, Tp, D), jnp.pad(tgt, ((0, 0), (N_META, Tp - T), (0, 0))), S)


_HBM = pl.BlockSpec(memory_space=pltpu.HBM)


def _place():
    return lax.axis_index("x"), lax.axis_index("y"), lax.axis_index("c")


def _other_chips(x, y):
    return [(1 - x, y), (x, 1 - y), (1 - x, 1 - y)]


def _chunks(rows, sublanes, most):
    for q in range(most, 0, -1):
        if rows % (q * sublanes) == 0:
            return q
    return 1


def _sublanes(dtype):
    return 8 * 4 // jnp.dtype(dtype).itemsize


def _gather_pieces(bufs):
    plan, first = [], []
    for b in bufs:
        Rh = b.shape[0] // 2
        Q = _chunks(Rh, _sublanes(b.dtype), 4) if Rh * b.shape[1] * b.dtype.itemsize > (1 << 20) else 1
        first.append(3 * sum(q for _, q, _ in plan))
        plan.append((Rh, Q, Rh // Q))
    return plan, first, 3 * sum(q for _, q, _ in plan)


def _allgather_chips(bufs, name):
    n = len(bufs)
    plan, first, n_sems = _gather_pieces(bufs)

    def body(*refs):
        x_refs, out_refs, (send_sems, recv_sems) = refs[:n], refs[n:2 * n], refs[2 * n:]
        x, y, c = _place()
        sibling = (x, y, 1 - c)
        chips = _other_chips(x, y)

        def copy(k, src, dst, to):
            return pltpu.make_async_remote_copy(src_ref=src, dst_ref=dst, send_sem=send_sems.at[k], recv_sem=recv_sems.at[k],
                                                device_id=to, device_id_type=MESH)

        def piece(i, cx, cy, hc, q):
            Rh, _, ch = plan[i]
            return out_refs[i].at[2 * cx + cy, pl.ds(hc * Rh + q * ch, ch), :]

        slots = [(i, q, j) for i in range(n) for q in range(plan[i][1]) for j in range(3)]
        sem = {(i, q, j): first[i] + 3 * q + j for i, q, j in slots}
        sent = [copy(sem[i, q, j], x_refs[i].at[pl.ds(c * plan[i][0] + q * plan[i][2], plan[i][2]), :], piece(i, x, y, c, q), (*chips[j], c))
                for i, q, j in slots]
        for cp in sent:
            cp.start()
        passed = []
        for i, q, j in slots:
            landed = piece(i, *chips[j], c, q)
            copy(sem[i, q, j], landed, landed, sibling).wait_recv()
            fwd = copy(n_sems + sem[i, q, j], landed, landed, sibling)
            fwd.start()
            passed.append(fwd)
        for i, q, j in slots:
            theirs = piece(i, *chips[j], 1 - c, q)
            copy(n_sems + sem[i, q, j], theirs, theirs, sibling).wait_recv()
        for cp in sent + passed:
            cp.wait_send()

    return pl.pallas_call(
        body, name=name, in_specs=[_HBM] * n, out_specs=[_HBM] * n,
        out_shape=[jax.ShapeDtypeStruct((N_CHIPS,) + b.shape, b.dtype) for b in bufs],
        scratch_shapes=[pltpu.SemaphoreType.DMA((2 * n_sems,)), pltpu.SemaphoreType.DMA((2 * n_sems,))],
    )(*bufs)


def _with_own(gathered, own):
    my = 2 * lax.axis_index("x") + lax.axis_index("y")
    return lax.dynamic_update_slice(gathered, own[None], (my, 0, 0))


def _sibling_gather(fs, name):
    n = len(fs)

    def body(*refs):
        out_refs, (send_sems, recv_sems) = refs[n:2 * n], refs[2 * n:]
        x, y, c = _place()
        copies = [pltpu.make_async_remote_copy(src_ref=out_ref.at[c], dst_ref=out_ref.at[c], send_sem=send_sems.at[i], recv_sem=recv_sems.at[i],
                                               device_id=(x, y, 1 - c), device_id_type=MESH) for i, out_ref in enumerate(out_refs)]
        for cp in copies:
            cp.start()
        for cp in copies:
            cp.wait()

    return pl.pallas_call(
        body, name=name, in_specs=[_HBM] * n, out_specs=[_HBM] * n,
        out_shape=[jax.ShapeDtypeStruct(f.shape, f.dtype) for f in fs], input_output_aliases={i: i for i in range(n)},
        scratch_shapes=[pltpu.SemaphoreType.DMA((n,)), pltpu.SemaphoreType.DMA((n,))],
    )(*fs)


def _axis_scalar(name):
    return lax.axis_index(name).astype(jnp.int32).reshape(1)


_SEM = pl.BlockSpec(memory_space=pltpu.SEMAPHORE)
_ANY = pl.BlockSpec(memory_space=pl.ANY)
_EFFECT = pltpu.SideEffectType.DATAFLOW_SIDE_EFFECTING


def _in_hbm(a):
    return pltpu.with_memory_space_constraint(a, pltpu.HBM)


def _half_copies(x_refs, land_refs, send_sems, recv_sems, arriving):
    x, y, c = _place()
    copies = []
    for i, (x_ref, land_ref) in enumerate(zip(x_refs, land_refs)):
        Rh = x_ref.shape[0] // 2
        rows = pl.ds(c * Rh, Rh)
        for j, (cx, cy) in enumerate(_other_chips(x, y)):
            copies.append(pltpu.make_async_remote_copy(
                src_ref=x_ref.at[rows, :], dst_ref=land_ref.at[2 * cx + cy if arriving else 2 * x + y, rows, :],
                send_sem=send_sems.at[3 * i + j], recv_sem=recv_sems.at[3 * i + j], device_id=(cx, cy, c), device_id_type=MESH))
    return copies


def _allgather_start(bufs, name):
    n = len(bufs)

    def body(*refs):
        x_refs, land_refs, (send_sems, recv_sems), token = refs[:n], refs[n:2 * n], refs[2 * n:2 * n + 2], refs[-1]
        for cp in _half_copies(x_refs, land_refs, send_sems, recv_sems, False):
            cp.start()
        token[...] = jnp.zeros_like(token)

    lands = [lax.empty((N_CHIPS,) + b.shape, b.dtype) for b in bufs]
    out = pl.pallas_call(
        body, name=name,
        out_shape=(pltpu.SemaphoreType.DMA((3 * n,)), pltpu.SemaphoreType.DMA((3 * n,)), *[pltpu.HBM(a.shape, a.dtype) for a in bufs + lands],
                   jax.ShapeDtypeStruct((8, 128), F32)),
        in_specs=[_HBM] * (2 * n), out_specs=(_SEM, _SEM, *[_HBM] * (2 * n), pl.BlockSpec(memory_space=pltpu.VMEM)),
        input_output_aliases={i: 2 + i for i in range(2 * n)}, compiler_params=pltpu.CompilerParams(has_side_effects=_EFFECT),
    )(*[_in_hbm(a) for a in bufs + lands])
    return (out[0], out[1], list(out[2:2 + n]), list(out[2 + n:2 + 2 * n])), out[-1][0, 0]


def _allgather_wait(pending, after, name):
    send_sems, recv_sems, bufs, lands = pending
    n = len(bufs)

    def body(*refs):
        x_refs, land_refs, send_sems, recv_sems = refs[:n], refs[n:2 * n], refs[2 * n], refs[2 * n + 1]
        for cp in _half_copies(x_refs, land_refs, send_sems, recv_sems, False):
            cp.wait_send()
        for cp in _half_copies(x_refs, land_refs, send_sems, recv_sems, True):
            cp.wait_recv()

    out = pl.pallas_call(
        body, name=name, out_shape=tuple(pltpu.HBM(a.shape, a.dtype) for a in bufs + lands),
        in_specs=[_HBM] * (2 * n) + [_SEM, _SEM, _ANY], out_specs=tuple([_HBM] * (2 * n)), input_output_aliases={i: i for i in range(2 * n)},
        compiler_params=pltpu.CompilerParams(has_side_effects=_EFFECT),
    )(*bufs, *lands, send_sems, recv_sems, after)
    return list(out[n:])


def _sibling_forward(lands, name):
    n = len(lands)
    plan, first, n_sems = _gather_pieces([jax.ShapeDtypeStruct(l.shape[1:], l.dtype) for l in lands])

    def body(*refs):
        out_refs, (send_sems, recv_sems) = refs[n:2 * n], refs[2 * n:]
        x, y, c = _place()

        def copies(hc):
            return [pltpu.make_async_remote_copy(
                        src_ref=out_refs[i].at[2 * cx + cy, pl.ds(hc * plan[i][0] + q * plan[i][2], plan[i][2]), :],
                        dst_ref=out_refs[i].at[2 * cx + cy, pl.ds(hc * plan[i][0] + q * plan[i][2], plan[i][2]), :],
                        send_sem=send_sems.at[first[i] + 3 * q + j], recv_sem=recv_sems.at[first[i] + 3 * q + j],
                        device_id=(x, y, 1 - c), device_id_type=MESH)
                    for i in range(n) for q in range(plan[i][1]) for j, (cx, cy) in enumerate(_other_chips(x, y))]

        mine = copies(c)
        for cp in mine:
            cp.start()
        for cp in mine:
            cp.wait_send()
        for cp in copies(1 - c):
            cp.wait_recv()

    return pl.pallas_call(
        body, name=name, in_specs=[_HBM] * n, out_specs=[_HBM] * n, out_shape=[jax.ShapeDtypeStruct(l.shape, l.dtype) for l in lands],
        input_output_aliases={i: i for i in range(n)},
        scratch_shapes=[pltpu.SemaphoreType.DMA((n_sems,)), pltpu.SemaphoreType.DMA((n_sems,))],
    )(*lands)


N_PEERS = 7


def _direct_copies(p_refs, t_refs, send_sems, recv_sems):
    x, y, c = _place()
    copies = []
    for i, (p_ref, t_ref) in enumerate(zip(p_refs, t_refs)):
        for f in range(1, N_PEERS + 1):
            px, py, pc = x ^ (f >> 2), y ^ ((f >> 1) & 1), c ^ (f & 1)
            copies.append(pltpu.make_async_remote_copy(
                src_ref=p_ref.at[2 * px + py, pc], dst_ref=t_ref.at[f - 1], send_sem=send_sems.at[N_PEERS * i + f - 1],
                recv_sem=recv_sems.at[N_PEERS * i + f - 1], device_id=(px, py, pc), device_id_type=MESH))
    return copies


def _direct_scatter_start(ps, name, carried=()):
    n, m = len(ps), 2 * len(ps) + len(carried)

    def body(*refs):
        p_refs, t_refs, (send_sems, recv_sems) = refs[:n], refs[n:2 * n], refs[m:m + 2]
        for cp in _direct_copies(p_refs, t_refs, send_sems, recv_sems):
            cp.start()

    lands = [lax.empty((N_PEERS,) + p.shape[2:], p.dtype) for p in ps]
    through = ps + lands + list(carried)
    out = pl.pallas_call(
        body, name=name,
        out_shape=(pltpu.SemaphoreType.DMA((N_PEERS * n,)), pltpu.SemaphoreType.DMA((N_PEERS * n,)),
                   *[pltpu.HBM(a.shape, a.dtype) for a in through]),
        in_specs=[_HBM] * m, out_specs=(_SEM, _SEM, *[_HBM] * m),
        input_output_aliases={i: 2 + i for i in range(m)}, compiler_params=pltpu.CompilerParams(has_side_effects=_EFFECT),
    )(*[_in_hbm(a) for a in through])
    return (out[0], out[1], list(out[2:2 + n]), list(out[2 + n:2 + 2 * n])), list(out[2 + 2 * n:])


def _direct_scatter_wait(pending, after, name):
    send_sems, recv_sems, ps, lands = pending
    n = len(ps)

    def body(*refs):
        p_refs, t_refs, send_sems, recv_sems = refs[:n], refs[n:2 * n], refs[2 * n], refs[2 * n + 1]
        for cp in _direct_copies(p_refs, t_refs, send_sems, recv_sems):
            cp.wait_send()
            cp.wait_recv()

    out = pl.pallas_call(
        body, name=name, out_shape=tuple(pltpu.HBM(a.shape, a.dtype) for a in ps + lands),
        in_specs=[_HBM] * (2 * n) + [_SEM, _SEM] + [_ANY] * len(after), out_specs=tuple([_HBM] * (2 * n)),
        input_output_aliases={i: i for i in range(2 * n)}, compiler_params=pltpu.CompilerParams(has_side_effects=_EFFECT),
    )(*ps, *lands, send_sems, recv_sems, *after)
    return list(out[:n]), list(out[n:])


def _sum_direct(p, t, name):
    _, _, R, C = p.shape
    tr = _pick(R, 512, 16)

    def body(x_ref, y_ref, c_ref, p_ref, t_ref, o_ref):
        acc = p_ref[...].astype(F32)
        for f in range(N_PEERS):
            acc = acc + t_ref[f].astype(F32)
        o_ref[...] = acc

    grid_spec = pltpu.PrefetchScalarGridSpec(
        num_scalar_prefetch=3, grid=(R // tr,),
        in_specs=[pl.BlockSpec((None, None, tr, C), lambda i, x_ref, y_ref, c_ref: (2 * x_ref[0] + y_ref[0], c_ref[0], i, 0)),
                  pl.BlockSpec((N_PEERS, tr, C), lambda i, x_ref, y_ref, c_ref: (0, i, 0))],
        out_specs=pl.BlockSpec((None, tr, C), lambda i, x_ref, y_ref, c_ref: (c_ref[0], i, 0)))
    return pl.pallas_call(body, name=name, grid_spec=grid_spec, out_shape=jax.ShapeDtypeStruct((2, R, C), F32),
                          compiler_params=pltpu.CompilerParams(dimension_semantics=("parallel",)))(
        _axis_scalar("x"), _axis_scalar("y"), _axis_scalar("c"), p, t)


def _adamw(w, g, m, v, name):
    R, C = w.shape
    tr = _pick(R, 256, 8)

    def body(w_ref, g_ref, m_ref, v_ref, d_ref, nm_ref, nv_ref):
        g_ = g_ref[...]
        m_ = ADAM_B1 * m_ref[...] + (1.0 - ADAM_B1) * g_
        v_ = ADAM_B2 * v_ref[...] + (1.0 - ADAM_B2) * (g_ * g_)
        m_hat = m_ / (1.0 - ADAM_B1 ** ADAM_STEP)
        v_hat = v_ / (1.0 - ADAM_B2 ** ADAM_STEP)
        d_ref[...] = -ADAM_LR * (m_hat / (jnp.sqrt(v_hat) + ADAM_EPS) + ADAM_WD * w_ref[...])
        nm_ref[...] = m_
        nv_ref[...] = v_

    row = pl.BlockSpec((tr, C), lambda i: (i, 0))
    shp = jax.ShapeDtypeStruct((R, C), F32)
    return pl.pallas_call(body, name=name, grid=(R // tr,), in_specs=[row] * 4, out_specs=[row] * 3, out_shape=[shp] * 3,
                          compiler_params=pltpu.CompilerParams(dimension_semantics=("parallel",)))(w, g, m, v)


BIG_SPECS = (("ev_w_in", 1024, 1440, 1), ("ev_w_uq", 256, 768, 1), ("ev_w_ukv", 128, 1024, 1), ("ev_w_out", 1024, 1024, 0),
             ("od_w_in", 1024, 6144, 1), ("od_w_out", 2048, 1024, 0), ("mlp_w1_0", 1024, 4096, 1), ("mlp_w1_1", 1024, 4096, 1),
             ("mlp_w2_0", 4096, 1024, 0), ("mlp_w2_1", 4096, 1024, 0))
BIG_PARAMS = (("ev_w_in", ("ev_w_in",)), ("ev_w_uq", ("ev_w_uq",)), ("ev_w_ukv", ("ev_w_ukv",)), ("ev_w_out", ("ev_w_out",)),
              ("od_w_in", ("od_w_in",)), ("od_w_out", ("od_w_out",)), ("mlp_w1", ("mlp_w1_0", "mlp_w1_1")),
              ("mlp_w2", ("mlp_w2_0", "mlp_w2_1")))
REPLICATED = ("ev_conv_b", "ev_w_rg_a", "ev_b_rg_a", "ev_w_rg_x", "ev_b_rg_x", "ev_lru_lambda", "ev_q_norm_g", "ev_kv_norm_g",
              "ln_mix_g", "ln_mix_b", "ln_mlp_g", "ln_mlp_b")
SMALL_SHARDED = ("meta_tokens", "ev_conv_w")
COL_SHARD_GRADS = ("od_w_in", "mlp_w1_0", "mlp_w1_1")
MATRIX_GROUPS = (("ev_w_in", "ev_w_uq", "ev_w_ukv", "ev_w_out"), ("mlp_w1_0", "mlp_w2_0"), ("od_w_in", "od_w_out", "mlp_w1_1", "mlp_w2_1"))
LAYER_NORMS = ("ln_mix_g", "ln_mix_b", "ln_mlp_g", "ln_mlp_b")
WEIGHT_NAMES = ("meta_tokens", "ev_w_in", "ev_conv_w", "ev_conv_b", "ev_w_rg_a", "ev_b_rg_a", "ev_w_rg_x", "ev_b_rg_x",
                "ev_lru_lambda", "ev_q_norm_g", "ev_w_uq", "ev_kv_norm_g", "ev_w_ukv", "ev_w_out", "od_w_in", "od_w_out",
                "ln_mix_g", "ln_mix_b", "mlp_w1", "mlp_w2", "ln_mlp_g", "ln_mlp_b")


def _to_rows(flat, row_align):
    n = flat.shape[-1]
    rows = _round_up(-(-n // PACK_COLS), row_align)
    pad = rows * PACK_COLS - n
    if pad:
        flat = jnp.pad(flat, [(0, 0)] * (flat.ndim - 1) + [(0, pad)])
    return flat.reshape(flat.shape[:-1] + (rows, PACK_COLS))


def _shard_shape(K, N, axis):
    return (K // N_CHIPS, N) if axis == 0 else (K, N // N_CHIPS)


def _gather_shards(stacked, K, N, axis):
    if axis == 0:
        return stacked.reshape(K, N)
    return stacked.transpose(1, 0, 2).reshape(K, N)


def _split_shards(full, K, N, axis):
    if axis == 0:
        return full.reshape(N_CHIPS, -1)
    return full.reshape(K, N_CHIPS, N // N_CHIPS).transpose(1, 0, 2).reshape(N_CHIPS, -1)


def kernel(x, meta_tokens, ev_w_in, ev_conv_w, ev_conv_b, ev_w_rg_a, ev_b_rg_a, ev_w_rg_x, ev_b_rg_x, ev_lru_lambda, ev_q_norm_g, ev_w_uq, ev_kv_norm_g, ev_w_ukv, ev_w_out, od_w_in, od_w_out, ln_mix_g, ln_mix_b, mlp_w1, mlp_w2, ln_mlp_g, ln_mlp_b, loss_target, m_meta_tokens, m_ev_w_in, m_ev_conv_w, m_ev_conv_b, m_ev_w_rg_a, m_ev_b_rg_a, m_ev_w_rg_x, m_ev_b_rg_x, m_ev_lru_lambda, m_ev_q_norm_g, m_ev_w_uq, m_ev_kv_norm_g, m_ev_w_ukv, m_ev_w_out, m_od_w_in, m_od_w_out, m_ln_mix_g, m_ln_mix_b, m_mlp_w1, m_mlp_w2, m_ln_mlp_g, m_ln_mlp_b, v_meta_tokens, v_ev_w_in, v_ev_conv_w, v_ev_conv_b, v_ev_w_rg_a, v_ev_b_rg_a, v_ev_w_rg_x, v_ev_b_rg_x, v_ev_lru_lambda, v_ev_q_norm_g, v_ev_w_uq, v_ev_kv_norm_g, v_ev_w_ukv, v_ev_w_out, v_od_w_in, v_od_w_out, v_ln_mix_g, v_ln_mix_b, v_mlp_w1, v_mlp_w2, v_ln_mlp_g, v_ln_mlp_b):
    given = dict(locals())
    local_big = {"ev_w_in": ev_w_in[0], "ev_w_uq": ev_w_uq[0], "ev_w_ukv": ev_w_ukv[0], "ev_w_out": ev_w_out[0],
                 "od_w_in": od_w_in[0], "od_w_out": od_w_out[0], "mlp_w1_0": mlp_w1[0], "mlp_w1_1": mlp_w1[1],
                 "mlp_w2_0": mlp_w2[0], "mlp_w2_1": mlp_w2[1]}

    specs = {spec[0]: spec for spec in BIG_SPECS}
    mixer0_m, mlp0_m, layer1_m = MATRIX_GROUPS

    def shards(names):
        return [local_big[n].astype(MXU_DTYPE) for n in names]

    def whole(stacked, n):
        _, K, N, ax = specs[n]
        return stacked if n in COL_SHARD_GRADS else _gather_shards(stacked, K, N, ax)

    def filled(gathered, own, names):
        return {n: whole(_with_own(g_, o_), n) for n, g_, o_ in zip(names, gathered, own)}

    own_a, own_b, own_c = shards(mixer0_m), shards(mlp0_m), shards(layer1_m)
    small = [meta_tokens, jnp.pad(ev_conv_w[0], ((0, 16 - CONV_WIDTH), (0, 0)))]
    gathered_a = _allgather_chips(own_a + small, "weight_allgather_mixer0")
    pending_b, token1 = _allgather_start(own_b, "weight_allgather_mlp0_start")
    pending_c, token2 = _allgather_start(own_c, "weight_allgather_layer1_start")
    meta_full = _gather_shards(_with_own(gathered_a[-2], small[0]), N_META, D_MODEL, 1)
    conv_full = _gather_shards(_with_own(gathered_a[-1], small[1])[:, :CONV_WIDTH], CONV_WIDTH, LRU_WIDTH, 1)

    def slots(names, dtype):
        return {n: jnp.zeros((N_CHIPS, specs[n][1], specs[n][2] // N_CHIPS) if n in COL_SHARD_GRADS else specs[n][1:3], dtype) for n in names}

    def norms(names, layer):
        return {n: given[n][layer] for n in names}

    def finish_gather(pending, own, after, names, tag):
        landed = _allgather_wait(pending, lax.stop_gradient(after), "weight_allgather_%s_wait" % tag)
        return filled(_sibling_forward(landed, "weight_allgather_%s_forward" % tag), own, names)

    diff_a = {**slots(mixer0_m, MXU_DTYPE), **norms(("ln_mix_g", "ln_mix_b"), 0), **{n: given[n] for n in REPLICATED if n not in LAYER_NORMS},
              "x": x, "meta_tokens": meta_full, "ev_conv_w": conv_full}
    diff_b = {**slots(mlp0_m, MXU_DTYPE), **norms(("ln_mlp_g", "ln_mlp_b"), 0)}
    diff_c = {**slots(layer1_m, MXU_DTYPE), **norms(LAYER_NORMS, 1)}
    w_a = filled(gathered_a[:len(mixer0_m)], own_a, mixer0_m)
    h_a, back_a = jax.vjp(lambda d: _mixer0(d, w_a, token1 + token2), diff_a)
    w_b = finish_gather(pending_b, own_b, h_a, mlp0_m, "mlp0")
    h_b, back_b = jax.vjp(lambda d, hh: _mlp0(d, hh, w_b), diff_b, h_a)
    w_c = finish_gather(pending_c, own_c, h_b, layer1_m, "layer1")
    loss, back_c = jax.vjp(lambda d, hh: _layer1_loss(d, hh, w_c, loss_target), diff_c, h_b)
    loss = lax.psum(loss, ("x", "y", "c"))

    def blocks_of(grad, n):
        _, K, N, ax = specs[n]
        if n in COL_SHARD_GRADS:
            blocks = grad
        elif ax == 0:
            blocks = grad.reshape(N_CHIPS, K // N_CHIPS, N)
        else:
            blocks = grad.reshape(K, N_CHIPS, N // N_CHIPS).transpose(1, 0, 2)
        return blocks.reshape(N_CHIPS, 2, blocks.shape[1] // 2, blocks.shape[2])

    def start_reduce(grads_of, names, tag, dh):
        flying, (dh,) = _direct_scatter_start([blocks_of(grads_of[n], n) for n in names], "grad_scatter_%s_start" % tag, [dh])
        return flying, dh

    g_c, dh = back_c(jnp.ones((), F32))
    flying_c, dh = start_reduce(g_c, layer1_m, "layer1", dh)
    g_b, dh = back_b(dh)
    flying_b, dh = start_reduce(g_b, mlp0_m, "mlp0", dh)
    (g_a,) = back_a(dh)

    g = {**g_a, **g_b, **g_c}
    g.update({n: jnp.stack([(g_b if n in g_b else g_a)[n], g_c[n]]) for n in LAYER_NORMS})
    repl = jnp.concatenate([g[n].reshape(-1) for n in REPLICATED]).reshape(N_CHIPS, -1)
    small = [_split_shards(g["meta_tokens"], N_META, D_MODEL, 1), _split_shards(g["ev_conv_w"], CONV_WIDTH, LRU_WIDTH, 1), repl]
    small = [pc.reshape(N_CHIPS, 2, -1) for pc in small]
    n_small = sum(pc.shape[2] for pc in small)
    small.append(jnp.zeros((N_CHIPS, 2, _round_up(n_small, 32 * PACK_COLS) - n_small), F32))
    p_small = jnp.concatenate(small, axis=2).reshape(N_CHIPS, 2, -1, PACK_COLS)
    flying_a, _ = _direct_scatter_start([blocks_of(g_a[n], n) for n in mixer0_m] + [p_small], "grad_scatter_mixer0_start")
    started = [g_a["x"], flying_a[2][0]]
    ps_c, ts_c = _direct_scatter_wait(flying_c, started, "grad_scatter_layer1_wait")
    ps_b, ts_b = _direct_scatter_wait(flying_b, started, "grad_scatter_mlp0_wait")
    fs_bc = [_sum_direct(p, t, "grad_sum_%d" % i) for i, (p, t) in enumerate(zip(ps_b + ps_c, ts_b + ts_c))]
    red_big = dict(zip(mlp0_m + layer1_m, _sibling_gather(fs_bc, "grad_sibling_gather")))

    grads, delta, new_m, new_v = {}, {}, {}, {}

    def update_big(names):
        done = []
        for name, parts in BIG_PARAMS:
            if parts[0] in names:
                shp = given[name].shape
                two_d = (-1, shp[-1])
                grads[name] = jnp.stack([red_big[part].reshape(shp[1:]) for part in parts])
                d, nm, nv = _adamw(given[name].reshape(two_d), grads[name].reshape(two_d), given["m_" + name].reshape(two_d),
                                   given["v_" + name].reshape(two_d), "adamw_" + name)
                delta[name], new_m[name], new_v[name] = d.reshape(shp), nm.reshape(shp), nv.reshape(shp)
                done.append(nv)
        return done

    updated = update_big(mlp0_m + layer1_m)
    ps_a, ts_a = _direct_scatter_wait(flying_a, updated, "grad_scatter_mixer0_wait")
    fs_a = [_sum_direct(p, t, "grad_sum_mixer0_%d" % i) for i, (p, t) in enumerate(zip(ps_a, ts_a))]
    reduced_a = _sibling_gather(fs_a, "grad_sibling_gather_mixer0")
    red_big.update(zip(mixer0_m, reduced_a))
    red_small = reduced_a[-1].reshape(2, -1)
    update_big(mixer0_m)

    def take(off, sz):
        return jnp.concatenate([red_small[0, off // 2:(off + sz) // 2], red_small[1, off // 2:(off + sz) // 2]])

    off = 0
    for name in SMALL_SHARDED:
        sz = given[name].size
        grads[name] = take(off, sz).reshape(given[name].shape)
        off += sz
    n_repl = repl.shape[1]
    own_repl = _to_rows(take(off, n_repl), 16)
    repl_all = _with_own(_allgather_chips([own_repl], "replicated_allgather")[0], own_repl).reshape(N_CHIPS, -1)[:, :n_repl].reshape(-1)
    off = 0
    for name in REPLICATED:
        sz = given[name].size
        grads[name] = repl_all[off:off + sz].reshape(given[name].shape)
        off += sz

    smalls = SMALL_SHARDED + REPLICATED

    def pack_small(get):
        return _to_rows(jnp.concatenate([get(n).reshape(-1) for n in smalls]), 8)

    outs = _adamw(pack_small(lambda n: given[n]), pack_small(lambda n: grads[n]), pack_small(lambda n: given["m_" + n]),
                  pack_small(lambda n: given["v_" + n]), "adamw_small")
    for res, flat in zip((delta, new_m, new_v), outs):
        flat, off = flat.reshape(-1), 0
        for n in smalls:
            sz = given[n].size
            res[n] = flat[off:off + sz].reshape(given[n].shape)
            off += sz

    return (loss, g_a["x"], *[grads[n] for n in WEIGHT_NAMES], *[delta[n] for n in WEIGHT_NAMES],
            *[new_m[n] for n in WEIGHT_NAMES], *[new_v[n] for n in WEIGHT_NAMES])
```

```python
import functools
import math

import jax
import jax.numpy as jnp
from jax import lax
from jax.experimental import pallas as pl
from jax.experimental.pallas import tpu as pltpu

F32 = jnp.float32
MXU_DTYPE = jnp.bfloat16

D_MODEL = 1024
N_META = 16
LRU_WIDTH = 512
LRU_HEADS = 4
LRU_HEAD_DIM = 128
CONV_WIDTH = 4
LRU_C = 8.0
MLA_HEADS = 8
MLA_NOPE = 64
MLA_ROPE = 32
MLA_V = 64
MLA_Q_RANK = 256
MLA_KV_RANK = 128
RET_HEADS = 4
RET_QK_DIM = 256
RET_V_DIM = 512
D_FF = 4096
ROPE_BASE = 10000.0
DN_ALPHA = 4.0 ** 0.25
EPS = 1e-5
NEG_INF = -1e30
SEQ_BLOCK = 128

ADAM_LR = 0.001
ADAM_B1 = 0.9
ADAM_B2 = 0.999
ADAM_EPS = 1e-08
ADAM_WD = 0.01
ADAM_STEP = 10

PACK_COLS = 1024
TN_INPUT_VMEM_BYTES = 28 << 20
N_CHIPS = 4

MESH = pl.DeviceIdType.MESH


def _pick(n, target, align):
    best = None
    for t in range(align, min(n, target) + 1, align):
        if n % t == 0:
            best = t
    return n if best is None else best


def _round_up(n, m):
    return (n + m - 1) // m * m


def _relu2(a):
    r = jnp.maximum(a, 0.0)
    return r * r


def _ln_stats(z):
    mu = jnp.mean(z, axis=-1, keepdims=True)
    zc = z - mu
    var = jnp.mean(zc * zc, axis=-1, keepdims=True)
    return zc, lax.rsqrt(var + EPS)


def _mm_nn(a, w, act, name, out_dtype=F32, norm=None):
    M, K = a.shape
    sharded = w.ndim == 3
    n = w.shape[-1]
    N = n * (w.shape[0] if sharded else 1)
    tm = _pick(M, 1088 if K * a.dtype.itemsize <= 4096 and norm is None else 544, 8)
    tn = _pick(n, 1024, 128)
    per = n // tn
    assert norm is None or tn == N

    def body(a_ref, w_ref, *rest):
        av = a_ref[...]
        if act:
            av = _relu2(av.astype(F32))
        r = jnp.dot(av.astype(MXU_DTYPE), w_ref[...].astype(MXU_DTYPE), preferred_element_type=F32)
        if norm is None:
            rest[0][...] = r.astype(out_dtype)
        else:
            r_ref, g_ref, b_ref, o_ref, z_ref = rest
            z = DN_ALPHA * r_ref[...] + r
            zc, rstd = _ln_stats(z)
            z_ref[...] = z
            o_ref[...] = zc * rstd * g_ref[...] + b_ref[...]

    w_spec = pl.BlockSpec((None, K, tn), lambda i, j: (j // per, 0, j % per)) if sharded else pl.BlockSpec((K, tn), lambda i, j: (0, j))
    tile = pl.BlockSpec((tm, tn), lambda i, j: (i, j))
    in_specs, args = [pl.BlockSpec((tm, K), lambda i, j: (i, 0)), w_spec], [a, w]
    if norm is None:
        out_specs, out_shape = tile, jax.ShapeDtypeStruct((M, N), out_dtype)
    else:
        vec = pl.BlockSpec((1, N), lambda i, j: (0, 0))
        in_specs += [tile, vec, vec]
        args += [norm[0], norm[1].reshape(1, N), norm[2].reshape(1, N)]
        out_specs, out_shape = [tile, tile], [jax.ShapeDtypeStruct((M, N), F32)] * 2
    return pl.pallas_call(
        body, name=name, grid=(M // tm, N // tn), in_specs=in_specs, out_specs=out_specs, out_shape=out_shape,
        compiler_params=pltpu.CompilerParams(dimension_semantics=("parallel", "arbitrary")),
    )(*args)


def _mm_nt(g, w, a_src, name, out_dtype=F32, plus=None):
    M, N = g.shape
    sharded = w.ndim == 3
    K, n = w.shape[-2], w.shape[-1]
    if sharded:
        tk, nk = N, 1
    else:
        tk = N if N * g.dtype.itemsize <= 8192 else _pick(N, 2048, 128)
        nk = N // tk
    tm = _pick(M, 1088 if tk * g.dtype.itemsize <= 4096 else 544, 8)
    tn = _pick(K, 1024, 128)
    has_src = a_src is not None
    assert nk == 1 or out_dtype == F32
    assert plus is None or not has_src

    def body(*refs):
        if has_src:
            g_ref, w_ref, s_ref, o_ref = refs
        elif plus is not None:
            g_ref, w_ref, p_ref, o_ref = refs
        else:
            g_ref, w_ref, o_ref = refs
        nt = (((1,), (1,)), ((), ()))
        if sharded:
            r = sum(lax.dot_general(g_ref[:, s * n:(s + 1) * n].astype(MXU_DTYPE), w_ref[s].astype(MXU_DTYPE), nt, preferred_element_type=F32)
                    for s in range(w_ref.shape[0]))
        else:
            r = lax.dot_general(g_ref[...].astype(MXU_DTYPE), w_ref[...].astype(MXU_DTYPE), nt, preferred_element_type=F32)
        if has_src:
            r = r * (2.0 * jnp.maximum(s_ref[...].astype(F32), 0.0))
        first = r if plus is None else r + DN_ALPHA * p_ref[...]
        if nk == 1:
            o_ref[...] = first.astype(out_dtype)
        else:
            k = pl.program_id(2)

            @pl.when(k == 0)
            def _():
                o_ref[...] = first

            @pl.when(k > 0)
            def _():
                o_ref[...] += r

    w_spec = (pl.BlockSpec((w.shape[0], tn, n), lambda i, j, k: (0, j, 0)) if sharded
              else pl.BlockSpec((tn, tk), lambda i, j, k: (j, k)))
    in_specs = [pl.BlockSpec((tm, tk), lambda i, j, k: (i, k)), w_spec]
    args = [g, w]
    if has_src:
        assert nk == 1
        in_specs.append(pl.BlockSpec((tm, tn), lambda i, j, k: (i, j)))
        args.append(a_src)
    if plus is not None:
        in_specs.append(pl.BlockSpec((tm, tn), lambda i, j, k: (i, j)))
        args.append(plus)
    return pl.pallas_call(
        body, name=name,
        grid=(M // tm, K // tn, nk),
        in_specs=in_specs,
        out_specs=pl.BlockSpec((tm, tn), lambda i, j, k: (i, j)),
        out_shape=jax.ShapeDtypeStruct((M, K), out_dtype),
        compiler_params=pltpu.CompilerParams(dimension_semantics=("parallel", "parallel", "arbitrary")),
    )(*args)


def _mm_tn(a, g, act, name, col_shards=1, out_dtype=F32):
    M, K = a.shape
    _, N = g.shape
    n = N // col_shards
    tm, tn = _pick(K, 1024, 128), _pick(n, 1024, 128)
    row_bytes = tm * a.dtype.itemsize + tn * g.dtype.itemsize
    tk = _pick(M, min(2176, TN_INPUT_VMEM_BYTES // (2 * row_bytes)), 8)
    nk = M // tk
    per = n // tn
    direct = out_dtype == F32

    def body(a_ref, g_ref, o_ref, *scratch):
        acc_ref = o_ref if direct else scratch[0]
        k = pl.program_id(2)
        av = a_ref[...]
        if act:
            av = _relu2(av.astype(F32))
        r = lax.dot_general(av.astype(MXU_DTYPE), g_ref[...].astype(MXU_DTYPE),
                            (((0,), (0,)), ((), ())), preferred_element_type=F32)

        @pl.when(k == 0)
        def _():
            acc_ref[...] = r

        @pl.when(k > 0)
        def _():
            acc_ref[...] += r

        if not direct:
            @pl.when(k == nk - 1)
            def _():
                o_ref[...] = acc_ref[...].astype(out_dtype)

    if col_shards == 1:
        out_spec, out_shape = pl.BlockSpec((tm, tn), lambda i, j, k: (i, j)), (K, N)
    else:
        out_spec, out_shape = pl.BlockSpec((None, tm, tn), lambda i, j, k: (j // per, i, j % per)), (col_shards, K, n)
    return pl.pallas_call(
        body, name=name,
        grid=(K // tm, N // tn, nk),
        in_specs=[pl.BlockSpec((tk, tm), lambda i, j, k: (k, i)), pl.BlockSpec((tk, tn), lambda i, j, k: (k, j))],
        out_specs=out_spec,
        out_shape=jax.ShapeDtypeStruct(out_shape, out_dtype),
        scratch_shapes=[] if direct else [pltpu.VMEM((tm, tn), F32)],
        compiler_params=pltpu.CompilerParams(dimension_semantics=("parallel", "parallel", "arbitrary")),
    )(a, g)


@functools.partial(jax.custom_vjp, nondiff_argnums=(3, 4, 5, 6))
def matmul(a, w, w_grad_slot, act, name, out_dtype, col_shards):
    return _mm_nn(a, w, act, name + "_fwd", out_dtype)


def _matmul_fwd(a, w, w_grad_slot, act, name, out_dtype, col_shards):
    return _mm_nn(a, w, act, name + "_fwd", out_dtype), (a, w, jnp.zeros((), w_grad_slot.dtype))


def _matmul_bwd(act, name, out_dtype, col_shards, res, g):
    a, w, slot_like = res
    w_grad_dtype = slot_like.dtype
    da = _mm_nt(g, w, a if act else None, name + "_dx")
    dw = _mm_tn(a, g, act, name + "_dw", col_shards, w_grad_dtype)
    return da, None, dw


matmul.defvjp(_matmul_fwd, _matmul_bwd)


def _ln_bwd_call(z, g, dy, name):
    M, D = z.shape
    tm = _pick(M, 544, 8)

    def body(z_ref, g_ref, dy_ref, dz_ref, dg_ref, db_ref):
        @pl.when(pl.program_id(0) == 0)
        def _():
            dg_ref[...] = jnp.zeros_like(dg_ref)
            db_ref[...] = jnp.zeros_like(db_ref)

        zc, rstd = _ln_stats(z_ref[...])
        xhat = zc * rstd
        dy = dy_ref[...]
        dxh = dy * g_ref[...]
        m1 = jnp.mean(dxh, axis=-1, keepdims=True)
        m2 = jnp.mean(dxh * xhat, axis=-1, keepdims=True)
        dz_ref[...] = rstd * (dxh - m1 - xhat * m2)
        dg_ref[...] += jnp.sum(dy * xhat, axis=0, keepdims=True)
        db_ref[...] += jnp.sum(dy, axis=0, keepdims=True)

    row = pl.BlockSpec((tm, D), lambda i: (i, 0))
    vec = pl.BlockSpec((1, D), lambda i: (0, 0))
    return pl.pallas_call(
        body, name=name, grid=(M // tm,), in_specs=[row, vec, row], out_specs=[row, vec, vec],
        out_shape=[jax.ShapeDtypeStruct((M, D), F32), jax.ShapeDtypeStruct((1, D), F32), jax.ShapeDtypeStruct((1, D), F32)],
        compiler_params=pltpu.CompilerParams(dimension_semantics=("arbitrary",)),
    )(z, g.reshape(1, D), dy)


@functools.partial(jax.custom_vjp, nondiff_argnums=(7,))
def mlp_block(h, w1, w2, w1_grad_slot, w2_grad_slot, g, b, name):
    return _mlp_block_fwd(h, w1, w2, w1_grad_slot, w2_grad_slot, g, b, name)[0]


def _mlp_block_fwd(h, w1, w2, w1_grad_slot, w2_grad_slot, g, b, name):
    u = _mm_nn(h, w1, False, name + "_w1_fwd", out_dtype=MXU_DTYPE)
    out, z = _mm_nn(u, w2, True, name + "_w2_norm_fwd", norm=(h, g, b))
    return out, (h, u, z, w1, w2, g, jnp.zeros((), w1_grad_slot.dtype))


def _mlp_block_bwd(name, res, dy):
    h, u, z, w1, w2, g, slot_like = res
    dz, dg, db = _ln_bwd_call(z, g, dy, name + "_norm_bwd")
    du = _mm_nt(dz, w2, u, name + "_w2_dx", out_dtype=MXU_DTYPE)
    dw2 = _mm_tn(u, dz, True, name + "_w2_dw", 1, slot_like.dtype)
    dh = _mm_nt(du, w1, None, name + "_w1_dx", plus=dz)
    dw1 = _mm_tn(h, du, False, name + "_w1_dw", N_CHIPS, slot_like.dtype)
    return dh, None, None, dw1, dw2, dg.reshape(g.shape), db.reshape(g.shape)


mlp_block.defvjp(_mlp_block_fwd, _mlp_block_bwd)


@functools.partial(jax.custom_vjp, nondiff_argnums=(6,))
def out_block(h, y, w, w_grad_slot, g, b, name):
    return _out_block_fwd(h, y, w, w_grad_slot, g, b, name)[0]


def _out_block_fwd(h, y, w, w_grad_slot, g, b, name):
    out, z = _mm_nn(y, w, False, name + "_norm_fwd", norm=(h, g, b))
    return out, (y, z, w, g, jnp.zeros((), w_grad_slot.dtype))


def _out_block_bwd(name, res, dy):
    y, z, w, g, slot_like = res
    dz, dg, db = _ln_bwd_call(z, g, dy, name + "_norm_bwd")
    d_y = _mm_nt(dz, w, None, name + "_dx")
    dw = _mm_tn(y, dz, False, name + "_dw", 1, slot_like.dtype)
    return DN_ALPHA * dz, d_y, None, dw, dg.reshape(g.shape), db.reshape(g.shape)


out_block.defvjp(_out_block_fwd, _out_block_bwd)


def _rms_fwd_call(x, g, name, col_block=0):
    R = x.shape[0]
    W = g.shape[-1]
    tr = _pick(R, 1088, 8)

    def body(x_ref, g_ref, o_ref):
        xv = x_ref[...]
        rstd = lax.rsqrt(jnp.mean(xv * xv, axis=-1, keepdims=True) + EPS)
        o_ref[...] = xv * rstd * g_ref[...]

    vec = pl.BlockSpec((1, W), lambda i: (0, 0))
    return pl.pallas_call(
        body, name=name, grid=(R // tr,), in_specs=[pl.BlockSpec((tr, W), lambda i: (i, col_block)), vec],
        out_specs=pl.BlockSpec((tr, W), lambda i: (i, 0)), out_shape=jax.ShapeDtypeStruct((R, W), F32),
        compiler_params=pltpu.CompilerParams(dimension_semantics=("parallel",)),
    )(x, g.reshape(1, W))


def _rms_bwd_call(x, g, dy, name, col_block=0):
    R = x.shape[0]
    W = g.shape[-1]
    tr = _pick(R, 1088, 8)

    def body(x_ref, g_ref, dy_ref, dx_ref, dg_ref):
        @pl.when(pl.program_id(0) == 0)
        def _():
            dg_ref[...] = jnp.zeros_like(dg_ref)

        xv = x_ref[...]
        rstd = lax.rsqrt(jnp.mean(xv * xv, axis=-1, keepdims=True) + EPS)
        xhat = xv * rstd
        dy = dy_ref[...]
        dxh = dy * g_ref[...]
        dx_ref[...] = rstd * (dxh - xhat * jnp.mean(dxh * xhat, axis=-1, keepdims=True))
        dg_ref[...] += jnp.sum(dy * xhat, axis=0, keepdims=True)

    row = pl.BlockSpec((tr, W), lambda i: (i, 0))
    vec = pl.BlockSpec((1, W), lambda i: (0, 0))
    return pl.pallas_call(
        body, name=name, grid=(R // tr,), in_specs=[pl.BlockSpec((tr, W), lambda i: (i, col_block)), vec, row], out_specs=[row, vec],
        out_shape=[jax.ShapeDtypeStruct((R, W), F32), jax.ShapeDtypeStruct((1, W), F32)],
        compiler_params=pltpu.CompilerParams(dimension_semantics=("arbitrary",)),
    )(x, g.reshape(1, W), dy)


def _loss_call(h, tgt, n_tokens, name):
    B, Tp, D = h.shape
    tr = _pick(Tp, 544, 8)

    def body(y_ref, t_ref, dy_ref, acc_ref):
        @pl.when(jnp.logical_and(pl.program_id(0) == 0, pl.program_id(1) == 0))
        def _():
            acc_ref[...] = jnp.zeros_like(acc_ref)

        t = lax.broadcasted_iota(jnp.int32, (tr, 1), 0) + pl.program_id(1) * tr
        counts = jnp.logical_and(t >= N_META, t < N_META + n_tokens)
        e = jnp.where(counts, y_ref[...] - t_ref[...], 0.0)
        dy_ref[...] = e * (1.0 / D)
        acc_ref[...] += jnp.sum(jnp.sum(e * e, axis=-1, keepdims=True), axis=0, keepdims=True) * (0.5 / D)

    row = pl.BlockSpec((None, tr, D), lambda b, i: (b, i, 0))
    one = pl.BlockSpec((1, 1), lambda b, i: (0, 0))
    return pl.pallas_call(
        body, name=name, grid=(B, Tp // tr), in_specs=[row, row], out_specs=[row, one],
        out_shape=[jax.ShapeDtypeStruct((B, Tp, D), F32), jax.ShapeDtypeStruct((1, 1), F32)],
        compiler_params=pltpu.CompilerParams(dimension_semantics=("arbitrary", "arbitrary")),
    )(h, tgt)


@functools.partial(jax.custom_vjp, nondiff_argnums=(2,))
def loss_head(h, tgt, n_tokens):
    return _loss_call(h, tgt, n_tokens, "loss_head")[1][0, 0]


def _loss_head_fwd(h, tgt, n_tokens):
    dy, acc = _loss_call(h, tgt, n_tokens, "loss_head")
    return acc[0, 0], dy


def _loss_head_bwd(n_tokens, dy, ct):
    return ct * dy, None


loss_head.defvjp(_loss_head_fwd, _loss_head_bwd)


_GELU_C = math.sqrt(2.0 / math.pi)


def _gelu_parts(x):
    x2 = x * x
    t = jnp.tanh(_GELU_C * (x + 0.044715 * x * x2))
    gelu = 0.5 * x * (1.0 + t)
    dgelu = 0.5 * (1.0 + t) + 0.5 * x * (1.0 - t * t) * (_GELU_C * (1.0 + 3.0 * 0.044715 * x2))
    return gelu, dgelu


def _sigmoid(x):
    return 1.0 / (1.0 + jnp.exp(-x))


def _scan8(a, b, carry, reverse):
    row = lax.broadcasted_iota(jnp.int32, a.shape, 0)
    for s in (1, 2, 4):
        shift = 8 - s if reverse else s
        keep = (row < 8 - s) if reverse else (row >= s)
        b = jnp.where(keep, a * pltpu.roll(b, shift, 0) + b, b)
        a = jnp.where(keep, a * pltpu.roll(a, shift, 0), a)
    return a * carry + b


def _lru_pre(prec_ref, prev_ref, first, cw_ref, cb_ref, wa_ref, ba_ref, wx_ref, bx_ref, sp_ref):
    tc = prec_ref.shape[0]
    prev = jnp.where(first, 0.0, prev_ref[...])
    ext = jnp.concatenate([prev, prec_ref[...]], axis=0)
    cw = cw_ref[...]
    taps = [ext[8:] if k == CONV_WIDTH - 1 else pltpu.roll(ext, CONV_WIDTH - 1 - k, 0)[8:] for k in range(CONV_WIDTH)]
    xc = cb_ref[...] + sum(cw[k:k + 1, :] * taps[k] for k in range(CONV_WIDTH))
    ga, gx = [], []
    for h in range(LRU_HEADS):
        xh = xc[:, h * LRU_HEAD_DIM:(h + 1) * LRU_HEAD_DIM].astype(MXU_DTYPE)
        ga.append(jnp.dot(xh, wa_ref[h].astype(MXU_DTYPE), preferred_element_type=F32))
        gx.append(jnp.dot(xh, wx_ref[h].astype(MXU_DTYPE), preferred_element_type=F32))
    r = _sigmoid(jnp.concatenate(ga, axis=1) + ba_ref[...])
    i = _sigmoid(jnp.concatenate(gx, axis=1) + bx_ref[...])
    log_a = -LRU_C * r * sp_ref[...]
    a = jnp.exp(log_a)
    a2 = a * a
    mult = jnp.sqrt(-jnp.tanh(log_a) * (a2 + 1.0))
    return taps, xc, r, i, a, a2, mult


def _lru_fwd_call(p, cw, cb, wa, ba, wx, bx, sp):
    B, Tp, _ = p.shape
    W = LRU_WIDTH
    tc = SEQ_BLOCK
    nc = Tp // tc

    def body(pg_ref, prec_ref, prev_ref, cw_ref, cb_ref, wa_ref, ba_ref, wx_ref, bx_ref, sp_ref, y_ref, h_ref, carry_ref):
        first = pl.program_id(1) == 0

        @pl.when(first)
        def _():
            carry_ref[...] = jnp.zeros_like(carry_ref)

        _, xc, r, i, a, a2, mult = _lru_pre(prec_ref, prev_ref, first, cw_ref, cb_ref, wa_ref, ba_ref, wx_ref, bx_ref, sp_ref)
        b = mult * (i * xc)
        carry = carry_ref[0:1, :]
        for t in range(tc // 8):
            h = _scan8(a[8 * t:8 * t + 8], b[8 * t:8 * t + 8], carry, False)
            h_ref[8 * t:8 * t + 8, :] = h
            carry = h[7:8, :]
        carry_ref[...] = jnp.broadcast_to(carry, carry_ref.shape)
        y_ref[...] = h_ref[...] * _gelu_parts(pg_ref[...])[0]

    cur = pl.BlockSpec((None, tc, W), lambda b, j: (b, j, 0))
    rec = pl.BlockSpec((None, tc, W), lambda b, j: (b, j, 1))
    prev = pl.BlockSpec((None, 8, W), lambda b, j: (b, jnp.maximum(j * (tc // 8) - 1, 0), 1))
    vec = pl.BlockSpec((1, W), lambda b, j: (0, 0))
    cws = pl.BlockSpec((CONV_WIDTH, W), lambda b, j: (0, 0))
    wsp = pl.BlockSpec((LRU_HEADS, LRU_HEAD_DIM, LRU_HEAD_DIM), lambda b, j: (0, 0, 0))
    return pl.pallas_call(
        body, name="lru_fwd", grid=(B, nc),
        in_specs=[cur, rec, prev, cws, vec, wsp, vec, wsp, vec, vec],
        out_specs=[cur, cur],
        out_shape=[jax.ShapeDtypeStruct((B, Tp, W + MLA_HEADS * HEAD_LANES), F32), jax.ShapeDtypeStruct((B, Tp, W), F32)],
        scratch_shapes=[pltpu.VMEM((8, W), F32)],
        compiler_params=pltpu.CompilerParams(dimension_semantics=("arbitrary", "arbitrary")),
    )(p, p, p, cw, cb, wa, ba, wx, bx, sp)


def _lru_bwd_call(p, hseq, dy, cw, cb, wa, ba, wx, bx, sp, dpq, dpkv, dkpe):
    B, Tp, P = p.shape
    W = LRU_WIDTH
    tc = SEQ_BLOCK
    nc = Tp // tc
    HD = LRU_HEAD_DIM

    def body(pg_ref, prec_ref, prev_ref, h_ref, hprev_ref, dy_ref, cw_ref, cb_ref, wa_ref, ba_ref, wx_ref, bx_ref, sp_ref,
             dpq_ref, dpkv_ref, dkpe_ref, dp_ref, dcw_ref, dcb_ref, dwa_ref, dba_ref, dwx_ref, dbx_ref, dsp_ref,
             gcar_ref, anext_ref, halo_ref, g_ref):
        j = pl.program_id(1)
        first = j == nc - 1
        last = j == 0

        @pl.when(jnp.logical_and(pl.program_id(0) == 0, last))
        def _():
            for ref in (dcw_ref, dcb_ref, dwa_ref, dba_ref, dwx_ref, dbx_ref, dsp_ref):
                ref[...] = jnp.zeros_like(ref)

        @pl.when(last)
        def _():
            gcar_ref[...] = jnp.zeros_like(gcar_ref)
            anext_ref[...] = jnp.zeros_like(anext_ref)
            halo_ref[...] = jnp.zeros_like(halo_ref)

        taps, xc, r, i, a, a2, mult = _lru_pre(prec_ref, prev_ref, first, cw_ref, cb_ref, wa_ref, ba_ref, wx_ref, bx_ref, sp_ref)
        row = lax.broadcasted_iota(jnp.int32, (tc, W), 0)
        gelu, dgelu = _gelu_parts(pg_ref[...])
        dy = dy_ref[...]
        hcur = h_ref[...]
        dp_ref[:, 0:W] = dy * hcur * dgelu
        dp_ref[:, 2 * W:2 * W + MLA_Q_RANK] = dpq_ref[...]
        dp_ref[:, _KPE_START - MLA_KV_RANK:_KPE_START] = dpkv_ref[...]
        dp_ref[:, _KPE_START:P] = pltpu.roll(dkpe_ref[...], HEAD_LANES - MLA_NOPE, 1)[:, 0:P - _KPE_START]
        dh = dy * gelu
        a_next = jnp.where(row == tc - 1, anext_ref[0:1, :], pltpu.roll(a, tc - 1, 0))
        carry = gcar_ref[0:1, :]
        for t in reversed(range(tc // 8)):
            g = _scan8(a_next[8 * t:8 * t + 8], dh[8 * t:8 * t + 8], carry, True)
            g_ref[8 * t:8 * t + 8, :] = g
            carry = g[0:1, :]
        gcar_ref[...] = jnp.broadcast_to(carry, gcar_ref.shape)
        anext_ref[...] = jnp.broadcast_to(a[0:1, :], anext_ref.shape)
        G = g_ref[...]
        h_before = jnp.where(first, 0.0, hprev_ref[7:8, :])
        hprev = jnp.where(row == 0, h_before, pltpu.roll(hcur, 1, 0))
        d_a = G * hprev
        gx_ = G * xc
        d_mult = gx_ * i
        d_i = gx_ * mult
        dxc = G * (mult * i)
        d_la = d_a * a - d_mult * (a2 / mult)
        sp = sp_ref[...]
        d_r = d_la * (-LRU_C * sp)
        dsp_ref[...] += jnp.sum(d_la * (-LRU_C * r), axis=0, keepdims=True)
        dga = d_r * r * (1.0 - r)
        dgx = d_i * i * (1.0 - i)
        dba_ref[...] += jnp.sum(dga, axis=0, keepdims=True)
        dbx_ref[...] += jnp.sum(dgx, axis=0, keepdims=True)
        back = []
        for h in range(LRU_HEADS):
            sl = slice(h * HD, (h + 1) * HD)
            xh = xc[:, sl].astype(MXU_DTYPE)
            ah = dga[:, sl].astype(MXU_DTYPE)
            bh = dgx[:, sl].astype(MXU_DTYPE)
            tn = (((0,), (0,)), ((), ()))
            nt = (((1,), (1,)), ((), ()))
            dwa_ref[h] += lax.dot_general(xh, ah, tn, preferred_element_type=F32)
            dwx_ref[h] += lax.dot_general(xh, bh, tn, preferred_element_type=F32)
            back.append(lax.dot_general(ah, wa_ref[h].astype(MXU_DTYPE), nt, preferred_element_type=F32)
                        + lax.dot_general(bh, wx_ref[h].astype(MXU_DTYPE), nt, preferred_element_type=F32))
        dxc = dxc + jnp.concatenate(back, axis=1)
        dcb_ref[...] += jnp.sum(dxc, axis=0, keepdims=True)
        for k in range(CONV_WIDTH):
            dcw_ref[k:k + 1, :] += jnp.sum(dxc * taps[k], axis=0, keepdims=True)
        ext = jnp.concatenate([dxc, halo_ref[...]], axis=0)
        cw = cw_ref[...]
        acc = cw[CONV_WIDTH - 1:CONV_WIDTH, :] * dxc
        for k in range(CONV_WIDTH - 1):
            s = CONV_WIDTH - 1 - k
            acc = acc + cw[k:k + 1, :] * pltpu.roll(ext, tc + 8 - s, 0)[:tc]
        dp_ref[:, W:2 * W] = acc
        halo_ref[...] = dxc[0:8, :]

    rev = lambda j: nc - 1 - j
    cur = pl.BlockSpec((None, tc, W), lambda b, j: (b, rev(j), 0))
    rec = pl.BlockSpec((None, tc, W), lambda b, j: (b, rev(j), 1))
    prev = pl.BlockSpec((None, 8, W), lambda b, j: (b, jnp.maximum(rev(j) * (tc // 8) - 1, 0), 0))
    prev_rec = pl.BlockSpec((None, 8, W), lambda b, j: (b, jnp.maximum(rev(j) * (tc // 8) - 1, 0), 1))
    vec = pl.BlockSpec((1, W), lambda b, j: (0, 0))
    cws = pl.BlockSpec((CONV_WIDTH, W), lambda b, j: (0, 0))
    wsp = pl.BlockSpec((LRU_HEADS, HD, HD), lambda b, j: (0, 0, 0))
    vs = jax.ShapeDtypeStruct((1, W), F32)
    ws = jax.ShapeDtypeStruct((LRU_HEADS, HD, HD), F32)

    def rows(width):
        return pl.BlockSpec((None, tc, width), lambda b, j: (b, rev(j), 0))

    return pl.pallas_call(
        body, name="lru_bwd", grid=(B, nc),
        in_specs=[cur, rec, prev_rec, cur, prev, cur, cws, vec, wsp, vec, wsp, vec, vec, rows(MLA_Q_RANK), rows(MLA_KV_RANK), rows(HEAD_LANES)],
        out_specs=[rows(P), cws, vec, wsp, vec, wsp, vec, vec],
        out_shape=[jax.ShapeDtypeStruct((B, Tp, P), F32), jax.ShapeDtypeStruct((CONV_WIDTH, W), F32), vs, ws, vs, ws, vs, vs],
        scratch_shapes=[pltpu.VMEM((8, W), F32), pltpu.VMEM((8, W), F32), pltpu.VMEM((8, W), F32), pltpu.VMEM((tc, W), F32)],
        compiler_params=pltpu.CompilerParams(dimension_semantics=("arbitrary", "arbitrary")),
    )(p, p, p, hseq, hseq, dy, cw, cb, wa, ba, wx, bx, sp, dpq, dpkv, dkpe)


_Q_BLOCK = 2 * LRU_WIDTH // MLA_Q_RANK
_KV_BLOCK = (2 * LRU_WIDTH + MLA_Q_RANK) // MLA_KV_RANK
_KPE_START = 2 * LRU_WIDTH + MLA_Q_RANK + MLA_KV_RANK


@jax.custom_vjp
def even_front(p, cw, cb, wa, ba, wx, bx, sp, gq, gkv):
    return _even_front_fwd(p, cw, cb, wa, ba, wx, bx, sp, gq, gkv)[0]


def _even_front_fwd(p, cw, cb, wa, ba, wx, bx, sp, gq, gkv):
    B, Tp, W = p.shape
    p2d = p.reshape(B * Tp, W)
    y, hseq = _lru_fwd_call(p, cw, cb, wa, ba, wx, bx, sp)
    qn = _rms_fwd_call(p2d, gq, "q_norm_fwd", _Q_BLOCK)
    kvn = _rms_fwd_call(p2d, gkv, "kv_norm_fwd", _KV_BLOCK)
    kpe = jnp.pad(p[:, :, _KPE_START:], ((0, 0), (0, 0), (MLA_NOPE, HEAD_LANES - MLA_NOPE - MLA_ROPE)))
    return (y, qn, kvn, kpe), (p, hseq, cw, cb, wa, ba, wx, bx, sp, gq, gkv)


def _even_front_bwd(res, cts):
    p, hseq, cw, cb, wa, ba, wx, bx, sp, gq, gkv = res
    dy, dqn, dkvn, dkpe = cts
    B, Tp, W = p.shape
    p2d = p.reshape(B * Tp, W)
    dpq, dgq = _rms_bwd_call(p2d, gq, dqn, "q_norm_bwd", _Q_BLOCK)
    dpkv, dgkv = _rms_bwd_call(p2d, gkv, dkvn, "kv_norm_bwd", _KV_BLOCK)
    dp, dcw, dcb, dwa, dba, dwx, dbx, dsp = _lru_bwd_call(p, hseq, dy, cw, cb, wa, ba, wx, bx, sp, dpq.reshape(B, Tp, -1),
                                                          dpkv.reshape(B, Tp, -1), dkpe)
    return dp, dcw, dcb, dwa, dba, dwx, dbx, dsp, dgq.reshape(gq.shape), dgkv.reshape(gkv.shape)


even_front.defvjp(_even_front_fwd, _even_front_bwd)


def _rope_tables(pos, half):
    inv = ROPE_BASE ** (-jnp.arange(half, dtype=F32) / half)
    ang = pos.astype(F32)[:, None] * inv[None, :]
    return jnp.cos(ang), jnp.sin(ang)


_NT = (((1,), (1,)), ((), ()))
_TN = (((0,), (0,)), ((), ()))
HEAD_LANES = 128
_MLA_SCALE = (MLA_NOPE + MLA_ROPE) ** -0.5
_LOG2E = math.log2(math.e)


Q_BLOCK = 512


def _query_blocks(Tp):
    first = Tp % Q_BLOCK or Q_BLOCK
    return [(0, first)] + [(r, r + Q_BLOCK) for r in range(first, Tp, Q_BLOCK)]


def _mask_diagonal(s, fill):
    R, L = s.shape
    row = lax.broadcasted_iota(jnp.int32, (R, R), 0)
    col = lax.broadcasted_iota(jnp.int32, (R, R), 1)
    last = jnp.where(col <= row, s[:, L - R:], fill)
    return last if L == R else jnp.concatenate([s[:, :L - R], last], axis=1)


def _mla_rope_tables(pos):
    half = MLA_ROPE // 2
    cos, sin = _rope_tables(pos, half)
    T = pos.shape[0]
    ones, zeros = jnp.ones((T, MLA_NOPE), F32), jnp.zeros((T, MLA_NOPE), F32)
    tail1, tail0 = jnp.ones((T, HEAD_LANES - MLA_NOPE - MLA_ROPE), F32), jnp.zeros((T, HEAD_LANES - MLA_NOPE - MLA_ROPE), F32)
    zh = jnp.zeros((T, half), F32)
    c = jnp.concatenate([ones, cos, cos, tail1], axis=1)
    s_up = jnp.concatenate([zeros, -sin, zh, tail0], axis=1)
    s_down = jnp.concatenate([zeros, zh, sin, tail0], axis=1)
    return c, s_up, s_down


def _rope_lanes(x, c, s_up, s_down):
    half = MLA_ROPE // 2
    return x * c + pltpu.roll(x, HEAD_LANES - half, 1) * s_up + pltpu.roll(x, half, 1) * s_down


def _unrope_lanes(d, c, s_up, s_down):
    half = MLA_ROPE // 2
    return d * c + pltpu.roll(d * s_up, half, 1) + pltpu.roll(d * s_down, HEAD_LANES - half, 1)


def _mla_operands(q_ref, kv_ref, kpe_ref, c, s_up, s_down):
    lane = lax.broadcasted_iota(jnp.int32, kv_ref.shape, 1)
    qr = (_rope_lanes(q_ref[...].astype(F32), c, s_up, s_down) * (_MLA_SCALE * _LOG2E)).astype(MXU_DTYPE)
    kr = jnp.where(lane < MLA_NOPE, kv_ref[...].astype(F32), _rope_lanes(kpe_ref[...], c, s_up, s_down)).astype(MXU_DTYPE)
    return qr, kr, lane


def _mla_specs(Tp):
    head = pl.BlockSpec((None, Tp, HEAD_LANES), lambda b, h: (b, 0, h))
    shared = pl.BlockSpec((None, Tp, HEAD_LANES), lambda b, h: (b, 0, 0))
    tab = pl.BlockSpec((Tp, HEAD_LANES), lambda b, h: (0, 0))
    lse = pl.BlockSpec((None, None, Tp, 1), lambda b, h: (b, h, 0, 0))
    out = pl.BlockSpec((None, Tp, HEAD_LANES), lambda b, h: (b, 0, LRU_WIDTH // HEAD_LANES + h))
    return head, shared, tab, lse, out


def _attn_fwd_call(q, kv, kpe, tabs, y):
    B, Tp, _ = q.shape

    def body(q_ref, kv_ref, kpe_ref, c_ref, su_ref, sd_ref, y_ref, o_ref, lse_ref, qr_ref, kr_ref):
        qr, kr, lane = _mla_operands(q_ref, kv_ref, kpe_ref, c_ref[...], su_ref[...], sd_ref[...])
        qr_ref[...] = qr
        kr_ref[...] = kr
        for r0, L in _query_blocks(Tp):
            blk = slice(r0, L)
            s = _mask_diagonal(lax.dot_general(qr_ref[blk, :], kr_ref[0:L, :], _NT, preferred_element_type=F32), NEG_INF)
            m = jnp.max(s, axis=-1, keepdims=True)
            p = jnp.exp2(s - m)
            l = jnp.sum(p, axis=-1, keepdims=True)
            o = jnp.dot(p.astype(MXU_DTYPE), kv_ref[0:L, :].astype(MXU_DTYPE), preferred_element_type=F32)
            o_ref[blk, :] = jnp.where(lane[blk, :] >= MLA_NOPE, o / l, 0.0)
            lse_ref[blk, :] = m + jnp.log2(l)

    head, shared, tab, lse, out = _mla_specs(Tp)
    return pl.pallas_call(
        body, name="mla_attn_fwd", grid=(B, MLA_HEADS),
        in_specs=[head, head, shared, tab, tab, tab, pl.BlockSpec(memory_space=pl.ANY)], out_specs=[out, lse],
        out_shape=[jax.ShapeDtypeStruct(y.shape, F32), jax.ShapeDtypeStruct((B, MLA_HEADS, Tp, 1), F32)],
        input_output_aliases={6: 0},
        scratch_shapes=[pltpu.VMEM((Tp, HEAD_LANES), MXU_DTYPE), pltpu.VMEM((Tp, HEAD_LANES), MXU_DTYPE)],
        compiler_params=pltpu.CompilerParams(dimension_semantics=("parallel", "parallel")),
    )(q, kv, kpe, *tabs, y)


def _attn_bwd_call(q, kv, kpe, tabs, o, lse, do):
    B, Tp, _ = q.shape

    def body(q_ref, kv_ref, kpe_ref, c_ref, su_ref, sd_ref, o_ref, lse_ref, do_ref, dq_ref, dkv_ref, dkpe_ref,
             qr_ref, kr_ref, dqa_ref, dka_ref, dva_ref):
        c, s_up, s_down = c_ref[...], su_ref[...], sd_ref[...]
        qr, kr, lane = _mla_operands(q_ref, kv_ref, kpe_ref, c, s_up, s_down)
        qr_ref[...] = qr
        kr_ref[...] = kr
        dka_ref[...] = jnp.zeros_like(dka_ref)
        dva_ref[...] = jnp.zeros_like(dva_ref)
        for r0, L in _query_blocks(Tp):
            blk = slice(r0, L)
            qb = qr_ref[blk, :]
            do = jnp.where(lane[blk, :] >= MLA_NOPE, do_ref[blk, :], 0.0)
            delta = jnp.sum(do * o_ref[blk, :], axis=-1, keepdims=True)
            s = _mask_diagonal(lax.dot_general(qb, kr_ref[0:L, :], _NT, preferred_element_type=F32), NEG_INF)
            p = jnp.exp2(s - lse_ref[blk, :])
            dob = do.astype(MXU_DTYPE)
            dva_ref[0:L, :] += lax.dot_general(p.astype(MXU_DTYPE), dob, _TN, preferred_element_type=F32)
            dp = lax.dot_general(dob, kv_ref[0:L, :].astype(MXU_DTYPE), _NT, preferred_element_type=F32)
            ds = (p * (dp - delta)).astype(MXU_DTYPE)
            dqa_ref[blk, :] = jnp.dot(ds, kr_ref[0:L, :], preferred_element_type=F32)
            dka_ref[0:L, :] += lax.dot_general(ds, qb, _TN, preferred_element_type=F32)
        dq_ref[...] = _unrope_lanes(dqa_ref[...] * _MLA_SCALE, c, s_up, s_down).astype(dq_ref.dtype)
        dk = dka_ref[...] * (1.0 / _LOG2E)
        dkv_ref[...] = jnp.where(lane < MLA_NOPE, dk, dva_ref[...]).astype(dkv_ref.dtype)
        dkpe = jnp.where(lane >= MLA_NOPE, _unrope_lanes(dk, c, s_up, s_down), 0.0)

        @pl.when(pl.program_id(1) == 0)
        def _():
            dkpe_ref[...] = dkpe

        @pl.when(pl.program_id(1) > 0)
        def _():
            dkpe_ref[...] += dkpe

    head, shared, tab, lse_spec, out = _mla_specs(Tp)
    wide = jax.ShapeDtypeStruct((B, Tp, MLA_HEADS * HEAD_LANES), q.dtype)
    acc = pltpu.VMEM((Tp, HEAD_LANES), F32)
    return pl.pallas_call(
        body, name="mla_attn_bwd", grid=(B, MLA_HEADS),
        in_specs=[head, head, shared, tab, tab, tab, out, lse_spec, out], out_specs=[head, head, shared],
        out_shape=[wide, wide, jax.ShapeDtypeStruct((B, Tp, HEAD_LANES), F32)],
        scratch_shapes=[pltpu.VMEM((Tp, HEAD_LANES), MXU_DTYPE), pltpu.VMEM((Tp, HEAD_LANES), MXU_DTYPE), acc, acc, acc],
        compiler_params=pltpu.CompilerParams(dimension_semantics=("parallel", "arbitrary")),
    )(q, kv, kpe, *tabs, o, lse, do)


@jax.custom_vjp
def mla_attention(q, kv, kpe, tabs, y):
    return _attn_fwd_call(q, kv, kpe, tabs, y)[0]


def _mla_attention_fwd(q, kv, kpe, tabs, y):
    o, lse = _attn_fwd_call(q, kv, kpe, tabs, y)
    return o, (q, kv, kpe, tabs, o, lse)


def _mla_attention_bwd(res, do):
    q, kv, kpe, tabs, o, lse = res
    dq, dkv, dkpe = _attn_bwd_call(q, kv, kpe, tabs, o, lse, do)
    return dq, dkv, dkpe, None, do


mla_attention.defvjp(_mla_attention_fwd, _mla_attention_bwd)


def _rope_halves(x, cos, sin):
    half = x.shape[1] // 2
    x1, x2 = x[:, :half], x[:, half:]
    return jnp.concatenate([x1 * cos - x2 * sin, x1 * sin + x2 * cos], axis=1)


def _unrope_halves(d, cos, sin):
    half = d.shape[1] // 2
    d1, d2 = d[:, :half], d[:, half:]
    return jnp.concatenate([d1 * cos + d2 * sin, d2 * cos - d1 * sin], axis=1)


_RET_K_SCALE = RET_QK_DIM ** -0.5
_RET_Q_BLOCKS = RET_HEADS
_RET_V_BLOCK0 = 2 * RET_HEADS * RET_QK_DIM // RET_V_DIM
_RET_G_BLOCK0 = _RET_V_BLOCK0 + RET_HEADS


def _ret_specs(Tp):
    q = pl.BlockSpec((None, Tp, RET_QK_DIM), lambda b, h: (b, 0, h))
    k = pl.BlockSpec((None, Tp, RET_QK_DIM), lambda b, h: (b, 0, _RET_Q_BLOCKS + h))
    v = pl.BlockSpec((None, Tp, RET_V_DIM), lambda b, h: (b, 0, _RET_V_BLOCK0 + h))
    tab = pl.BlockSpec((Tp, RET_QK_DIM // 2), lambda b, h: (0, 0))
    lg = pl.BlockSpec((None, 1, 1), lambda b, h: (h, 0, 0))
    return q, k, v, tab, lg


def _ret_operands(q_ref, k_ref, cos, sin, lg):
    t = lax.broadcasted_iota(jnp.int32, (q_ref.shape[0], 1), 0).astype(F32)
    grow, shrink = jnp.exp(-lg * t), jnp.exp(lg * t)
    qs = (_rope_halves(q_ref[...].astype(F32), cos, sin) * shrink).astype(MXU_DTYPE)
    ks = (_rope_halves(k_ref[...].astype(F32), cos, sin) * (grow * _RET_K_SCALE)).astype(MXU_DTYPE)
    return qs, ks, shrink, grow * _RET_K_SCALE


def _ret_core_fwd_call(p, cos, sin, lg):
    B, Tp, _ = p.shape

    def body(q_ref, k_ref, v_ref, cos_ref, sin_ref, lg_ref, o_ref, qs_ref, ks_ref):
        qs_ref[...], ks_ref[...], _, _ = _ret_operands(q_ref, k_ref, cos_ref[...], sin_ref[...], lg_ref[...])
        for r0, L in _query_blocks(Tp):
            blk = slice(r0, L)
            s = _mask_diagonal(lax.dot_general(qs_ref[blk, :], ks_ref[0:L, :], _NT, preferred_element_type=F32), 0.0)
            o_ref[blk, :] = jnp.dot(s.astype(MXU_DTYPE), v_ref[0:L, :].astype(MXU_DTYPE), preferred_element_type=F32)

    q, k, v, tab, lgs = _ret_specs(Tp)
    return pl.pallas_call(
        body, name="retention_fwd", grid=(B, RET_HEADS), in_specs=[q, k, v, tab, tab, lgs],
        out_specs=pl.BlockSpec((None, Tp, RET_V_DIM), lambda b, h: (b, 0, h)),
        out_shape=jax.ShapeDtypeStruct((B, Tp, RET_HEADS * RET_V_DIM), F32),
        scratch_shapes=[pltpu.VMEM((Tp, RET_QK_DIM), MXU_DTYPE), pltpu.VMEM((Tp, RET_QK_DIM), MXU_DTYPE)],
        compiler_params=pltpu.CompilerParams(dimension_semantics=("parallel", "parallel")),
    )(p, p, p, cos, sin, lg)


def _ret_core_bwd_call(p, do, cos, sin, lg):
    B, Tp, _ = p.shape

    def body(q_ref, k_ref, v_ref, do_ref, cos_ref, sin_ref, lg_ref, dq_ref, dk_ref, dv_ref, qs_ref, ks_ref, dqa_ref, dka_ref, dva_ref):
        cos_, sin_ = cos_ref[...], sin_ref[...]
        qs_ref[...], ks_ref[...], q_scale, k_scale = _ret_operands(q_ref, k_ref, cos_, sin_, lg_ref[...])
        dka_ref[...] = jnp.zeros_like(dka_ref)
        dva_ref[...] = jnp.zeros_like(dva_ref)
        for r0, L in _query_blocks(Tp):
            blk = slice(r0, L)
            qb = qs_ref[blk, :]
            dob = do_ref[blk, :].astype(MXU_DTYPE)
            s = _mask_diagonal(lax.dot_general(qb, ks_ref[0:L, :], _NT, preferred_element_type=F32), 0.0).astype(MXU_DTYPE)
            dva_ref[0:L, :] += lax.dot_general(s, dob, _TN, preferred_element_type=F32)
            ds = _mask_diagonal(lax.dot_general(dob, v_ref[0:L, :].astype(MXU_DTYPE), _NT, preferred_element_type=F32), 0.0).astype(MXU_DTYPE)
            dqa_ref[blk, :] = jnp.dot(ds, ks_ref[0:L, :], preferred_element_type=F32)
            dka_ref[0:L, :] += lax.dot_general(ds, qb, _TN, preferred_element_type=F32)
        dq_ref[...] = _unrope_halves(dqa_ref[...] * q_scale, cos_, sin_).astype(dq_ref.dtype)
        dk_ref[...] = _unrope_halves(dka_ref[...] * k_scale, cos_, sin_).astype(dk_ref.dtype)
        dv_ref[...] = dva_ref[...].astype(dv_ref.dtype)

    q, k, v, tab, lgs = _ret_specs(Tp)
    qk_out = pl.BlockSpec((None, Tp, RET_QK_DIM), lambda b, h: (b, 0, h))
    v_out = pl.BlockSpec((None, Tp, RET_V_DIM), lambda b, h: (b, 0, h))
    return pl.pallas_call(
        body, name="retention_bwd", grid=(B, RET_HEADS), in_specs=[q, k, v, v_out, tab, tab, lgs],
        out_specs=[qk_out, qk_out, v_out],
        out_shape=[jax.ShapeDtypeStruct((B, Tp, RET_HEADS * RET_QK_DIM), p.dtype), jax.ShapeDtypeStruct((B, Tp, RET_HEADS * RET_QK_DIM), p.dtype),
                   jax.ShapeDtypeStruct((B, Tp, RET_HEADS * RET_V_DIM), p.dtype)],
        scratch_shapes=[pltpu.VMEM((Tp, RET_QK_DIM), MXU_DTYPE), pltpu.VMEM((Tp, RET_QK_DIM), MXU_DTYPE),
                        pltpu.VMEM((Tp, RET_QK_DIM), F32), pltpu.VMEM((Tp, RET_QK_DIM), F32), pltpu.VMEM((Tp, RET_V_DIM), F32)],
        compiler_params=pltpu.CompilerParams(dimension_semantics=("parallel", "parallel")),
    )(p, p, p, do, cos, sin, lg)


def _ret_gate_specs(M):
    tm = _pick(M, 1088, 8)
    head = pl.BlockSpec((tm, RET_V_DIM), lambda i, h: (i, h))
    gate = pl.BlockSpec((tm, RET_V_DIM), lambda i, h: (i, _RET_G_BLOCK0 + h))
    return tm, head, gate


def _ret_gate_fwd_call(o, p2d):
    M = o.shape[0]
    tm, head, gate = _ret_gate_specs(M)

    def body(o_ref, g_ref, y_ref):
        ov = o_ref[...]
        gv = g_ref[...].astype(F32)
        rstd = lax.rsqrt(jnp.mean(ov * ov, axis=-1, keepdims=True) + EPS)
        y_ref[...] = (gv * _sigmoid(gv)) * (ov * rstd)

    return pl.pallas_call(
        body, name="retention_gate_fwd", grid=(M // tm, RET_HEADS), in_specs=[head, gate], out_specs=head,
        out_shape=jax.ShapeDtypeStruct(o.shape, F32),
        compiler_params=pltpu.CompilerParams(dimension_semantics=("parallel", "parallel")),
    )(o, p2d)


def _ret_gate_bwd_call(o, p2d, dy):
    M = o.shape[0]
    tm, head, gate = _ret_gate_specs(M)

    def body(o_ref, g_ref, dy_ref, do_ref, dg_ref):
        ov = o_ref[...]
        gv = g_ref[...].astype(F32)
        dy = dy_ref[...]
        rstd = lax.rsqrt(jnp.mean(ov * ov, axis=-1, keepdims=True) + EPS)
        on = ov * rstd
        sg = _sigmoid(gv)
        dg_ref[...] = (dy * on * (sg * (1.0 + gv * (1.0 - sg)))).astype(dg_ref.dtype)
        don = dy * (gv * sg)
        do_ref[...] = (rstd * (don - on * jnp.mean(don * on, axis=-1, keepdims=True))).astype(do_ref.dtype)

    shp = jax.ShapeDtypeStruct(o.shape, p2d.dtype)
    return pl.pallas_call(
        body, name="retention_gate_bwd", grid=(M // tm, RET_HEADS), in_specs=[head, gate, head], out_specs=[head, head],
        out_shape=[shp, shp],
        compiler_params=pltpu.CompilerParams(dimension_semantics=("parallel", "parallel")),
    )(o, p2d, dy)


def _log_gamma():
    return jnp.log(1.0 - 2.0 ** (-5.0 - jnp.arange(RET_HEADS, dtype=F32))).reshape(RET_HEADS, 1, 1)


@functools.partial(jax.custom_vjp, nondiff_argnums=(9,))
def retention_block(h, w_in, w_out, w_in_grad_slot, w_out_grad_slot, g, b, cos, sin, dims):
    return _retention_block_fwd(h, w_in, w_out, w_in_grad_slot, w_out_grad_slot, g, b, cos, sin, dims)[0]


def _retention_block_fwd(h, w_in, w_out, w_in_grad_slot, w_out_grad_slot, g, b, cos, sin, dims):
    B, Tp = dims
    p = _mm_nn(h, w_in, False, "od_w_in_fwd", out_dtype=MXU_DTYPE)
    o = _ret_core_fwd_call(p.reshape(B, Tp, -1), cos, sin, _log_gamma())
    y = _ret_gate_fwd_call(o.reshape(B * Tp, -1), p)
    out, z = _mm_nn(y, w_out, False, "od_w_out_norm_fwd", norm=(h, g, b))
    return out, (h, p, o, y, z, w_in, w_out, g, cos, sin, jnp.zeros((), w_in_grad_slot.dtype))


def _retention_block_bwd(dims, res, dout):
    B, Tp = dims
    h, p, o, y, z, w_in, w_out, g, cos, sin, slot_like = res
    dz, dg, db = _ln_bwd_call(z, g, dout, "od_w_out_norm_bwd")
    dy = _mm_nt(dz, w_out, None, "od_w_out_dx")
    dw_out = _mm_tn(y, dz, False, "od_w_out_dw", 1, slot_like.dtype)
    do, dgate = _ret_gate_bwd_call(o.reshape(B * Tp, -1), p, dy)
    dq, dk, dv = _ret_core_bwd_call(p.reshape(B, Tp, -1), do.reshape(B, Tp, -1), cos, sin, _log_gamma())
    dp = jnp.concatenate([dq.reshape(B * Tp, -1), dk.reshape(B * Tp, -1), dv.reshape(B * Tp, -1), dgate], axis=-1)
    dh = _mm_nt(dp, w_in, None, "od_w_in_dx", plus=dz)
    dw_in = _mm_tn(h, dp, False, "od_w_in_dw", N_CHIPS, slot_like.dtype)
    return dh, None, None, dw_in, dw_out, dg.reshape(g.shape), db.reshape(g.shape), None, None


retention_block.defvjp(_retention_block_fwd, _retention_block_bwd)


def _heads_to_lanes(w):
    K = w.shape[0]
    w = w.reshape(K, MLA_HEADS, MLA_NOPE + MLA_ROPE)
    return jnp.pad(w, ((0, 0), (0, 0), (0, HEAD_LANES - MLA_NOPE - MLA_ROPE))).reshape(K, MLA_HEADS * HEAD_LANES)


def _out_rows_to_lanes(w):
    N = w.shape[1]
    att = w[LRU_WIDTH:].reshape(MLA_HEADS, MLA_V, N)
    att = jnp.pad(att, ((0, 0), (HEAD_LANES - MLA_V, 0), (0, 0))).reshape(MLA_HEADS * HEAD_LANES, N)
    return jnp.concatenate([w[:LRU_WIDTH], att], axis=0)


def _seq_dims(x):
    B, S, D = x.shape
    T = S + N_META
    Tp = _round_up(T, SEQ_BLOCK)
    return B, S, T, Tp


def _mixer0(diff, w, token):
    x = diff["x"]
    B, S, T, Tp = _seq_dims(x)
    D = x.shape[-1]
    M = B * Tp
    pos = jnp.arange(Tp, dtype=jnp.int32)

    def mm(a, name, act=False, out_dtype=F32, layout=lambda m: m, col_shards=1):
        return matmul(a, layout(w[name]), layout(diff[name]), act, name, out_dtype, col_shards)

    meta = jnp.broadcast_to((diff["meta_tokens"] + token)[None], (B, N_META, D))
    h = jnp.concatenate([meta, x, jnp.zeros((B, Tp - T, D), F32)], axis=1).reshape(M, D)
    p = mm(h, "ev_w_in")
    sp = jax.nn.softplus(-diff["ev_lru_lambda"]).reshape(1, LRU_WIDTH)
    y, qn, kvn, kpe = even_front(
        p.reshape(B, Tp, -1), diff["ev_conv_w"].reshape(CONV_WIDTH, LRU_WIDTH), diff["ev_conv_b"].reshape(1, LRU_WIDTH),
        diff["ev_w_rg_a"].reshape(LRU_HEADS, LRU_HEAD_DIM, LRU_HEAD_DIM), diff["ev_b_rg_a"].reshape(1, LRU_WIDTH),
        diff["ev_w_rg_x"].reshape(LRU_HEADS, LRU_HEAD_DIM, LRU_HEAD_DIM), diff["ev_b_rg_x"].reshape(1, LRU_WIDTH),
        sp, diff["ev_q_norm_g"].reshape(-1), diff["ev_kv_norm_g"].reshape(-1))
    q = mm(qn, "ev_w_uq", out_dtype=MXU_DTYPE, layout=_heads_to_lanes).reshape(B, Tp, -1)
    kv = mm(kvn, "ev_w_ukv", out_dtype=MXU_DTYPE).reshape(B, Tp, -1)
    y = mla_attention(q, kv, kpe, _mla_rope_tables(pos), y).reshape(M, -1)
    return out_block(h, y, _out_rows_to_lanes(w["ev_w_out"]), _out_rows_to_lanes(diff["ev_w_out"]),
                     diff["ln_mix_g"], diff["ln_mix_b"], "ev_w_out")


def _mlp0(diff, h, w):
    return mlp_block(h, w["mlp_w1_0"], w["mlp_w2_0"], diff["mlp_w1_0"], diff["mlp_w2_0"], diff["ln_mlp_g"], diff["ln_mlp_b"], "mlp0")


def _layer1_loss(diff, h, w, tgt):
    B, S, T, Tp = _seq_dims(tgt)
    D = tgt.shape[-1]
    pos = jnp.arange(Tp, dtype=jnp.int32)

    cos, sin = _rope_tables(pos, RET_QK_DIM // 2)
    h = retention_block(h, w["od_w_in"], w["od_w_out"], diff["od_w_in"], diff["od_w_out"], diff["ln_mix_g"], diff["ln_mix_b"], cos, sin, (B, Tp))
    h = mlp_block(h, w["mlp_w1_1"], w["mlp_w2_1"], diff["mlp_w1_1"], diff["mlp_w2_1"], diff["ln_mlp_g"], diff["ln_mlp_b"], "mlp1")
    return loss_head(h.reshape(B, Tp, D), jnp.pad(tgt, ((0, 0), (N_META, Tp - T), (0, 0))), S)


_HBM = pl.BlockSpec(memory_space=pltpu.HBM)


def _place():
    return lax.axis_index("x"), lax.axis_index("y"), lax.axis_index("c")


def _other_chips(x, y):
    return [(1 - x, y), (x, 1 - y), (1 - x, 1 - y)]


def _chunks(rows, sublanes, most):
    for q in range(most, 0, -1):
        if rows % (q * sublanes) == 0:
            return q
    return 1


def _sublanes(dtype):
    return 8 * 4 // jnp.dtype(dtype).itemsize


def _gather_pieces(bufs):
    plan, first = [], []
    for b in bufs:
        Rh = b.shape[0] // 2
        Q = _chunks(Rh, _sublanes(b.dtype), 4) if Rh * b.shape[1] * b.dtype.itemsize > (1 << 20) else 1
        first.append(3 * sum(q for _, q, _ in plan))
        plan.append((Rh, Q, Rh // Q))
    return plan, first, 3 * sum(q for _, q, _ in plan)


def _allgather_chips(bufs, name):
    n = len(bufs)
    plan, first, n_sems = _gather_pieces(bufs)

    def body(*refs):
        x_refs, out_refs, (send_sems, recv_sems) = refs[:n], refs[n:2 * n], refs[2 * n:]
        x, y, c = _place()
        sibling = (x, y, 1 - c)
        chips = _other_chips(x, y)

        def copy(k, src, dst, to):
            return pltpu.make_async_remote_copy(src_ref=src, dst_ref=dst, send_sem=send_sems.at[k], recv_sem=recv_sems.at[k],
                                                device_id=to, device_id_type=MESH)

        def piece(i, cx, cy, hc, q):
            Rh, _, ch = plan[i]
            return out_refs[i].at[2 * cx + cy, pl.ds(hc * Rh + q * ch, ch), :]

        slots = [(i, q, j) for i in range(n) for q in range(plan[i][1]) for j in range(3)]
        sem = {(i, q, j): first[i] + 3 * q + j for i, q, j in slots}
        sent = [copy(sem[i, q, j], x_refs[i].at[pl.ds(c * plan[i][0] + q * plan[i][2], plan[i][2]), :], piece(i, x, y, c, q), (*chips[j], c))
                for i, q, j in slots]
        for cp in sent:
            cp.start()
        passed = []
        for i, q, j in slots:
            landed = piece(i, *chips[j], c, q)
            copy(sem[i, q, j], landed, landed, sibling).wait_recv()
            fwd = copy(n_sems + sem[i, q, j], landed, landed, sibling)
            fwd.start()
            passed.append(fwd)
        for i, q, j in slots:
            theirs = piece(i, *chips[j], 1 - c, q)
            copy(n_sems + sem[i, q, j], theirs, theirs, sibling).wait_recv()
        for cp in sent + passed:
            cp.wait_send()

    return pl.pallas_call(
        body, name=name, in_specs=[_HBM] * n, out_specs=[_HBM] * n,
        out_shape=[jax.ShapeDtypeStruct((N_CHIPS,) + b.shape, b.dtype) for b in bufs],
        scratch_shapes=[pltpu.SemaphoreType.DMA((2 * n_sems,)), pltpu.SemaphoreType.DMA((2 * n_sems,))],
    )(*bufs)


def _with_own(gathered, own):
    my = 2 * lax.axis_index("x") + lax.axis_index("y")
    return lax.dynamic_update_slice(gathered, own[None], (my, 0, 0))


def _sibling_gather(fs, name):
    n = len(fs)

    def body(*refs):
        out_refs, (send_sems, recv_sems) = refs[n:2 * n], refs[2 * n:]
        x, y, c = _place()
        copies = [pltpu.make_async_remote_copy(src_ref=out_ref.at[c], dst_ref=out_ref.at[c], send_sem=send_sems.at[i], recv_sem=recv_sems.at[i],
                                               device_id=(x, y, 1 - c), device_id_type=MESH) for i, out_ref in enumerate(out_refs)]
        for cp in copies:
            cp.start()
        for cp in copies:
            cp.wait()

    return pl.pallas_call(
        body, name=name, in_specs=[_HBM] * n, out_specs=[_HBM] * n,
        out_shape=[jax.ShapeDtypeStruct(f.shape, f.dtype) for f in fs], input_output_aliases={i: i for i in range(n)},
        scratch_shapes=[pltpu.SemaphoreType.DMA((n,)), pltpu.SemaphoreType.DMA((n,))],
    )(*fs)


def _axis_scalar(name):
    return lax.axis_index(name).astype(jnp.int32).reshape(1)


_SEM = pl.BlockSpec(memory_space=pltpu.SEMAPHORE)
_ANY = pl.BlockSpec(memory_space=pl.ANY)
_EFFECT = pltpu.SideEffectType.DATAFLOW_SIDE_EFFECTING


def _in_hbm(a):
    return pltpu.with_memory_space_constraint(a, pltpu.HBM)


def _half_copies(x_refs, land_refs, send_sems, recv_sems, arriving, whole):
    x, y, c = _place()
    copies = []
    for i, (x_ref, land_ref) in enumerate(zip(x_refs, land_refs)):
        Rh = x_ref.shape[0] // 2
        rows = pl.ds(0, 2 * Rh) if whole else pl.ds(c * Rh, Rh)
        for j, (cx, cy) in enumerate(_other_chips(x, y)):
            copies.append(pltpu.make_async_remote_copy(
                src_ref=x_ref.at[rows, :], dst_ref=land_ref.at[2 * cx + cy if arriving else 2 * x + y, rows, :],
                send_sem=send_sems.at[3 * i + j], recv_sem=recv_sems.at[3 * i + j], device_id=(cx, cy, c), device_id_type=MESH))
    return copies


def _allgather_start(bufs, name, whole=False):
    n = len(bufs)

    def body(*refs):
        x_refs, land_refs, (send_sems, recv_sems), token = refs[:n], refs[n:2 * n], refs[2 * n:2 * n + 2], refs[-1]
        for cp in _half_copies(x_refs, land_refs, send_sems, recv_sems, False, whole):
            cp.start()
        token[...] = jnp.zeros_like(token)

    lands = [lax.empty((N_CHIPS,) + b.shape, b.dtype) for b in bufs]
    out = pl.pallas_call(
        body, name=name,
        out_shape=(pltpu.SemaphoreType.DMA((3 * n,)), pltpu.SemaphoreType.DMA((3 * n,)), *[pltpu.HBM(a.shape, a.dtype) for a in bufs + lands],
                   jax.ShapeDtypeStruct((8, 128), F32)),
        in_specs=[_HBM] * (2 * n), out_specs=(_SEM, _SEM, *[_HBM] * (2 * n), pl.BlockSpec(memory_space=pltpu.VMEM)),
        input_output_aliases={i: 2 + i for i in range(2 * n)}, compiler_params=pltpu.CompilerParams(has_side_effects=_EFFECT),
    )(*[_in_hbm(a) for a in bufs + lands])
    return (out[0], out[1], list(out[2:2 + n]), list(out[2 + n:2 + 2 * n])), out[-1][0, 0]


def _allgather_wait(pending, after, name, whole=False):
    send_sems, recv_sems, bufs, lands = pending
    n = len(bufs)

    def body(*refs):
        x_refs, land_refs, send_sems, recv_sems = refs[:n], refs[n:2 * n], refs[2 * n], refs[2 * n + 1]
        for cp in _half_copies(x_refs, land_refs, send_sems, recv_sems, False, whole):
            cp.wait_send()
        for cp in _half_copies(x_refs, land_refs, send_sems, recv_sems, True, whole):
            cp.wait_recv()

    out = pl.pallas_call(
        body, name=name, out_shape=tuple(pltpu.HBM(a.shape, a.dtype) for a in bufs + lands),
        in_specs=[_HBM] * (2 * n) + [_SEM, _SEM, _ANY], out_specs=tuple([_HBM] * (2 * n)), input_output_aliases={i: i for i in range(2 * n)},
        compiler_params=pltpu.CompilerParams(has_side_effects=_EFFECT),
    )(*bufs, *lands, send_sems, recv_sems, after)
    return list(out[n:])


def _sibling_forward(lands, name):
    n = len(lands)
    plan, first, n_sems = _gather_pieces([jax.ShapeDtypeStruct(l.shape[1:], l.dtype) for l in lands])

    def body(*refs):
        out_refs, (send_sems, recv_sems) = refs[n:2 * n], refs[2 * n:]
        x, y, c = _place()

        def copies(hc):
            return [pltpu.make_async_remote_copy(
                        src_ref=out_refs[i].at[2 * cx + cy, pl.ds(hc * plan[i][0] + q * plan[i][2], plan[i][2]), :],
                        dst_ref=out_refs[i].at[2 * cx + cy, pl.ds(hc * plan[i][0] + q * plan[i][2], plan[i][2]), :],
                        send_sem=send_sems.at[first[i] + 3 * q + j], recv_sem=recv_sems.at[first[i] + 3 * q + j],
                        device_id=(x, y, 1 - c), device_id_type=MESH)
                    for i in range(n) for q in range(plan[i][1]) for j, (cx, cy) in enumerate(_other_chips(x, y))]

        mine = copies(c)
        for cp in mine:
            cp.start()
        for cp in mine:
            cp.wait_send()
        for cp in copies(1 - c):
            cp.wait_recv()

    return pl.pallas_call(
        body, name=name, in_specs=[_HBM] * n, out_specs=[_HBM] * n, out_shape=[jax.ShapeDtypeStruct(l.shape, l.dtype) for l in lands],
        input_output_aliases={i: i for i in range(n)},
        scratch_shapes=[pltpu.SemaphoreType.DMA((n_sems,)), pltpu.SemaphoreType.DMA((n_sems,))],
    )(*lands)


N_PEERS = 7


def _direct_copies(p_refs, t_refs, send_sems, recv_sems):
    x, y, c = _place()
    copies = []
    for i, (p_ref, t_ref) in enumerate(zip(p_refs, t_refs)):
        for f in range(1, N_PEERS + 1):
            px, py, pc = x ^ (f >> 2), y ^ ((f >> 1) & 1), c ^ (f & 1)
            copies.append(pltpu.make_async_remote_copy(
                src_ref=p_ref.at[2 * px + py, pc], dst_ref=t_ref.at[f - 1], send_sem=send_sems.at[N_PEERS * i + f - 1],
                recv_sem=recv_sems.at[N_PEERS * i + f - 1], device_id=(px, py, pc), device_id_type=MESH))
    return copies


def _direct_scatter_start(ps, name, carried=()):
    n, m = len(ps), 2 * len(ps) + len(carried)

    def body(*refs):
        p_refs, t_refs, (send_sems, recv_sems) = refs[:n], refs[n:2 * n], refs[m:m + 2]
        for cp in _direct_copies(p_refs, t_refs, send_sems, recv_sems):
            cp.start()

    lands = [lax.empty((N_PEERS,) + p.shape[2:], p.dtype) for p in ps]
    through = ps + lands + list(carried)
    out = pl.pallas_call(
        body, name=name,
        out_shape=(pltpu.SemaphoreType.DMA((N_PEERS * n,)), pltpu.SemaphoreType.DMA((N_PEERS * n,)),
                   *[pltpu.HBM(a.shape, a.dtype) for a in through]),
        in_specs=[_HBM] * m, out_specs=(_SEM, _SEM, *[_HBM] * m),
        input_output_aliases={i: 2 + i for i in range(m)}, compiler_params=pltpu.CompilerParams(has_side_effects=_EFFECT),
    )(*[_in_hbm(a) for a in through])
    return (out[0], out[1], list(out[2:2 + n]), list(out[2 + n:2 + 2 * n])), list(out[2 + 2 * n:])


def _direct_scatter_wait(pending, after, name):
    send_sems, recv_sems, ps, lands = pending
    n = len(ps)

    def body(*refs):
        p_refs, t_refs, send_sems, recv_sems = refs[:n], refs[n:2 * n], refs[2 * n], refs[2 * n + 1]
        for cp in _direct_copies(p_refs, t_refs, send_sems, recv_sems):
            cp.wait_send()
            cp.wait_recv()

    out = pl.pallas_call(
        body, name=name, out_shape=tuple(pltpu.HBM(a.shape, a.dtype) for a in ps + lands),
        in_specs=[_HBM] * (2 * n) + [_SEM, _SEM] + [_ANY] * len(after), out_specs=tuple([_HBM] * (2 * n)),
        input_output_aliases={i: i for i in range(2 * n)}, compiler_params=pltpu.CompilerParams(has_side_effects=_EFFECT),
    )(*ps, *lands, send_sems, recv_sems, *after)
    return list(out[:n]), list(out[n:])


def _sum_direct(p, t, name):
    _, _, R, C = p.shape
    tr = _pick(R, 512, 16)

    def body(x_ref, y_ref, c_ref, p_ref, t_ref, o_ref):
        acc = p_ref[...].astype(F32)
        for f in range(N_PEERS):
            acc = acc + t_ref[f].astype(F32)
        o_ref[...] = acc

    grid_spec = pltpu.PrefetchScalarGridSpec(
        num_scalar_prefetch=3, grid=(R // tr,),
        in_specs=[pl.BlockSpec((None, None, tr, C), lambda i, x_ref, y_ref, c_ref: (2 * x_ref[0] + y_ref[0], c_ref[0], i, 0)),
                  pl.BlockSpec((N_PEERS, tr, C), lambda i, x_ref, y_ref, c_ref: (0, i, 0))],
        out_specs=pl.BlockSpec((None, tr, C), lambda i, x_ref, y_ref, c_ref: (c_ref[0], i, 0)))
    return pl.pallas_call(body, name=name, grid_spec=grid_spec, out_shape=jax.ShapeDtypeStruct((2, R, C), F32),
                          compiler_params=pltpu.CompilerParams(dimension_semantics=("parallel",)))(
        _axis_scalar("x"), _axis_scalar("y"), _axis_scalar("c"), p, t)


def _adamw(w, g, m, v, name):
    R, C = w.shape
    tr = _pick(R, 256, 8)

    def body(w_ref, g_ref, m_ref, v_ref, d_ref, nm_ref, nv_ref):
        g_ = g_ref[...]
        m_ = ADAM_B1 * m_ref[...] + (1.0 - ADAM_B1) * g_
        v_ = ADAM_B2 * v_ref[...] + (1.0 - ADAM_B2) * (g_ * g_)
        m_hat = m_ / (1.0 - ADAM_B1 ** ADAM_STEP)
        v_hat = v_ / (1.0 - ADAM_B2 ** ADAM_STEP)
        d_ref[...] = -ADAM_LR * (m_hat / (jnp.sqrt(v_hat) + ADAM_EPS) + ADAM_WD * w_ref[...])
        nm_ref[...] = m_
        nv_ref[...] = v_

    row = pl.BlockSpec((tr, C), lambda i: (i, 0))
    shp = jax.ShapeDtypeStruct((R, C), F32)
    return pl.pallas_call(body, name=name, grid=(R // tr,), in_specs=[row] * 4, out_specs=[row] * 3, out_shape=[shp] * 3,
                          compiler_params=pltpu.CompilerParams(dimension_semantics=("parallel",)))(w, g, m, v)


BIG_SPECS = (("ev_w_in", 1024, 1440, 1), ("ev_w_uq", 256, 768, 1), ("ev_w_ukv", 128, 1024, 1), ("ev_w_out", 1024, 1024, 0),
             ("od_w_in", 1024, 6144, 1), ("od_w_out", 2048, 1024, 0), ("mlp_w1_0", 1024, 4096, 1), ("mlp_w1_1", 1024, 4096, 1),
             ("mlp_w2_0", 4096, 1024, 0), ("mlp_w2_1", 4096, 1024, 0))
BIG_PARAMS = (("ev_w_in", ("ev_w_in",)), ("ev_w_uq", ("ev_w_uq",)), ("ev_w_ukv", ("ev_w_ukv",)), ("ev_w_out", ("ev_w_out",)),
              ("od_w_in", ("od_w_in",)), ("od_w_out", ("od_w_out",)), ("mlp_w1", ("mlp_w1_0", "mlp_w1_1")),
              ("mlp_w2", ("mlp_w2_0", "mlp_w2_1")))
REPLICATED = ("ev_conv_b", "ev_w_rg_a", "ev_b_rg_a", "ev_w_rg_x", "ev_b_rg_x", "ev_lru_lambda", "ev_q_norm_g", "ev_kv_norm_g",
              "ln_mix_g", "ln_mix_b", "ln_mlp_g", "ln_mlp_b")
SMALL_SHARDED = ("meta_tokens", "ev_conv_w")
COL_SHARD_GRADS = ("od_w_in", "mlp_w1_0", "mlp_w1_1")
MATRIX_GROUPS = (("ev_w_in", "ev_w_uq", "ev_w_ukv", "ev_w_out"), ("mlp_w1_0", "mlp_w2_0"), ("od_w_in", "od_w_out", "mlp_w1_1", "mlp_w2_1"))
LAYER_NORMS = ("ln_mix_g", "ln_mix_b", "ln_mlp_g", "ln_mlp_b")
WEIGHT_NAMES = ("meta_tokens", "ev_w_in", "ev_conv_w", "ev_conv_b", "ev_w_rg_a", "ev_b_rg_a", "ev_w_rg_x", "ev_b_rg_x",
                "ev_lru_lambda", "ev_q_norm_g", "ev_w_uq", "ev_kv_norm_g", "ev_w_ukv", "ev_w_out", "od_w_in", "od_w_out",
                "ln_mix_g", "ln_mix_b", "mlp_w1", "mlp_w2", "ln_mlp_g", "ln_mlp_b")


def _to_rows(flat, row_align):
    n = flat.shape[-1]
    rows = _round_up(-(-n // PACK_COLS), row_align)
    pad = rows * PACK_COLS - n
    if pad:
        flat = jnp.pad(flat, [(0, 0)] * (flat.ndim - 1) + [(0, pad)])
    return flat.reshape(flat.shape[:-1] + (rows, PACK_COLS))


def _shard_shape(K, N, axis):
    return (K // N_CHIPS, N) if axis == 0 else (K, N // N_CHIPS)


def _gather_shards(stacked, K, N, axis):
    if axis == 0:
        return stacked.reshape(K, N)
    return stacked.transpose(1, 0, 2).reshape(K, N)


def _split_shards(full, K, N, axis):
    if axis == 0:
        return full.reshape(N_CHIPS, -1)
    return full.reshape(K, N_CHIPS, N // N_CHIPS).transpose(1, 0, 2).reshape(N_CHIPS, -1)


def kernel(x, meta_tokens, ev_w_in, ev_conv_w, ev_conv_b, ev_w_rg_a, ev_b_rg_a, ev_w_rg_x, ev_b_rg_x, ev_lru_lambda, ev_q_norm_g, ev_w_uq, ev_kv_norm_g, ev_w_ukv, ev_w_out, od_w_in, od_w_out, ln_mix_g, ln_mix_b, mlp_w1, mlp_w2, ln_mlp_g, ln_mlp_b, loss_target, m_meta_tokens, m_ev_w_in, m_ev_conv_w, m_ev_conv_b, m_ev_w_rg_a, m_ev_b_rg_a, m_ev_w_rg_x, m_ev_b_rg_x, m_ev_lru_lambda, m_ev_q_norm_g, m_ev_w_uq, m_ev_kv_norm_g, m_ev_w_ukv, m_ev_w_out, m_od_w_in, m_od_w_out, m_ln_mix_g, m_ln_mix_b, m_mlp_w1, m_mlp_w2, m_ln_mlp_g, m_ln_mlp_b, v_meta_tokens, v_ev_w_in, v_ev_conv_w, v_ev_conv_b, v_ev_w_rg_a, v_ev_b_rg_a, v_ev_w_rg_x, v_ev_b_rg_x, v_ev_lru_lambda, v_ev_q_norm_g, v_ev_w_uq, v_ev_kv_norm_g, v_ev_w_ukv, v_ev_w_out, v_od_w_in, v_od_w_out, v_ln_mix_g, v_ln_mix_b, v_mlp_w1, v_mlp_w2, v_ln_mlp_g, v_ln_mlp_b):
    given = dict(locals())
    local_big = {"ev_w_in": ev_w_in[0], "ev_w_uq": ev_w_uq[0], "ev_w_ukv": ev_w_ukv[0], "ev_w_out": ev_w_out[0],
                 "od_w_in": od_w_in[0], "od_w_out": od_w_out[0], "mlp_w1_0": mlp_w1[0], "mlp_w1_1": mlp_w1[1],
                 "mlp_w2_0": mlp_w2[0], "mlp_w2_1": mlp_w2[1]}

    specs = {spec[0]: spec for spec in BIG_SPECS}
    mixer0_m, mlp0_m, layer1_m = MATRIX_GROUPS

    def shards(names):
        return [local_big[n].astype(MXU_DTYPE) for n in names]

    def whole(stacked, n):
        _, K, N, ax = specs[n]
        return stacked if n in COL_SHARD_GRADS else _gather_shards(stacked, K, N, ax)

    def filled(gathered, own, names):
        return {n: whole(_with_own(g_, o_), n) for n, g_, o_ in zip(names, gathered, own)}

    own_a, own_b, own_c = shards(mixer0_m), shards(mlp0_m), shards(layer1_m)
    small = [meta_tokens, jnp.pad(ev_conv_w[0], ((0, 16 - CONV_WIDTH), (0, 0)))]
    gathered_a = _allgather_chips(own_a + small, "weight_allgather_mixer0")
    pending_b, token1 = _allgather_start(own_b, "weight_allgather_mlp0_start")
    pending_c, token2 = _allgather_start(own_c, "weight_allgather_layer1_start", whole=True)
    meta_full = _gather_shards(_with_own(gathered_a[-2], small[0]), N_META, D_MODEL, 1)
    conv_full = _gather_shards(_with_own(gathered_a[-1], small[1])[:, :CONV_WIDTH], CONV_WIDTH, LRU_WIDTH, 1)

    def slots(names, dtype):
        return {n: jnp.zeros((N_CHIPS, specs[n][1], specs[n][2] // N_CHIPS) if n in COL_SHARD_GRADS else specs[n][1:3], dtype) for n in names}

    def norms(names, layer):
        return {n: given[n][layer] for n in names}

    def finish_gather(pending, own, after, names, tag, whole=False):
        landed = _allgather_wait(pending, lax.stop_gradient(after), "weight_allgather_%s_wait" % tag, whole)
        return filled(landed if whole else _sibling_forward(landed, "weight_allgather_%s_forward" % tag), own, names)

    diff_a = {**slots(mixer0_m, MXU_DTYPE), **norms(("ln_mix_g", "ln_mix_b"), 0), **{n: given[n] for n in REPLICATED if n not in LAYER_NORMS},
              "x": x, "meta_tokens": meta_full, "ev_conv_w": conv_full}
    diff_b = {**slots(mlp0_m, MXU_DTYPE), **norms(("ln_mlp_g", "ln_mlp_b"), 0)}
    diff_c = {**slots(layer1_m, MXU_DTYPE), **norms(LAYER_NORMS, 1)}
    w_a = filled(gathered_a[:len(mixer0_m)], own_a, mixer0_m)
    h_a, back_a = jax.vjp(lambda d: _mixer0(d, w_a, token1 + token2), diff_a)
    w_b = finish_gather(pending_b, own_b, h_a, mlp0_m, "mlp0")
    h_b, back_b = jax.vjp(lambda d, hh: _mlp0(d, hh, w_b), diff_b, h_a)
    w_c = finish_gather(pending_c, own_c, h_b, layer1_m, "layer1", whole=True)
    loss, back_c = jax.vjp(lambda d, hh: _layer1_loss(d, hh, w_c, loss_target), diff_c, h_b)
    loss = lax.psum(loss, ("x", "y", "c"))

    def blocks_of(grad, n):
        _, K, N, ax = specs[n]
        if n in COL_SHARD_GRADS:
            blocks = grad
        elif ax == 0:
            blocks = grad.reshape(N_CHIPS, K // N_CHIPS, N)
        else:
            blocks = grad.reshape(K, N_CHIPS, N // N_CHIPS).transpose(1, 0, 2)
        return blocks.reshape(N_CHIPS, 2, blocks.shape[1] // 2, blocks.shape[2])

    def start_reduce(grads_of, names, tag, dh):
        flying, (dh,) = _direct_scatter_start([blocks_of(grads_of[n], n) for n in names], "grad_scatter_%s_start" % tag, [dh])
        return flying, dh

    g_c, dh = back_c(jnp.ones((), F32))
    flying_c, dh = start_reduce(g_c, layer1_m, "layer1", dh)
    g_b, dh = back_b(dh)
    flying_b, dh = start_reduce(g_b, mlp0_m, "mlp0", dh)
    (g_a,) = back_a(dh)

    g = {**g_a, **g_b, **g_c}
    g.update({n: jnp.stack([(g_b if n in g_b else g_a)[n], g_c[n]]) for n in LAYER_NORMS})
    repl = jnp.concatenate([g[n].reshape(-1) for n in REPLICATED]).reshape(N_CHIPS, -1)
    small = [_split_shards(g["meta_tokens"], N_META, D_MODEL, 1), _split_shards(g["ev_conv_w"], CONV_WIDTH, LRU_WIDTH, 1), repl]
    small = [pc.reshape(N_CHIPS, 2, -1) for pc in small]
    n_small = sum(pc.shape[2] for pc in small)
    small.append(jnp.zeros((N_CHIPS, 2, _round_up(n_small, 32 * PACK_COLS) - n_small), F32))
    p_small = jnp.concatenate(small, axis=2).reshape(N_CHIPS, 2, -1, PACK_COLS)
    flying_a, _ = _direct_scatter_start([blocks_of(g_a[n], n) for n in mixer0_m] + [p_small], "grad_scatter_mixer0_start")
    started = [g_a["x"], flying_a[2][0]]
    ps_c, ts_c = _direct_scatter_wait(flying_c, started, "grad_scatter_layer1_wait")
    ps_b, ts_b = _direct_scatter_wait(flying_b, started, "grad_scatter_mlp0_wait")
    fs_bc = [_sum_direct(p, t, "grad_sum_%d" % i) for i, (p, t) in enumerate(zip(ps_b + ps_c, ts_b + ts_c))]
    red_big = dict(zip(mlp0_m + layer1_m, _sibling_gather(fs_bc, "grad_sibling_gather")))

    grads, delta, new_m, new_v = {}, {}, {}, {}

    def update_big(names):
        done = []
        for name, parts in BIG_PARAMS:
            if parts[0] in names:
                shp = given[name].shape
                two_d = (-1, shp[-1])
                grads[name] = jnp.stack([red_big[part].reshape(shp[1:]) for part in parts])
                d, nm, nv = _adamw(given[name].reshape(two_d), grads[name].reshape(two_d), given["m_" + name].reshape(two_d),
                                   given["v_" + name].reshape(two_d), "adamw_" + name)
                delta[name], new_m[name], new_v[name] = d.reshape(shp), nm.reshape(shp), nv.reshape(shp)
                done.append(nv)
        return done

    updated = update_big(mlp0_m + layer1_m)
    ps_a, ts_a = _direct_scatter_wait(flying_a, updated, "grad_scatter_mixer0_wait")
    fs_a = [_sum_direct(p, t, "grad_sum_mixer0_%d" % i) for i, (p, t) in enumerate(zip(ps_a, ts_a))]
    reduced_a = _sibling_gather(fs_a, "grad_sibling_gather_mixer0")
    red_big.update(zip(mixer0_m, reduced_a))
    red_small = reduced_a[-1].reshape(2, -1)
    update_big(mixer0_m)

    def take(off, sz):
        return jnp.concatenate([red_small[0, off // 2:(off + sz) // 2], red_small[1, off // 2:(off + sz) // 2]])

    off = 0
    for name in SMALL_SHARDED:
        sz = given[name].size
        grads[name] = take(off, sz).reshape(given[name].shape)
        off += sz
    n_repl = repl.shape[1]
    own_repl = _to_rows(take(off, n_repl), 16)
    repl_all = _with_own(_allgather_chips([own_repl], "replicated_allgather")[0], own_repl).reshape(N_CHIPS, -1)[:, :n_repl].reshape(-1)
    off = 0
    for name in REPLICATED:
        sz = given[name].size
        grads[name] = repl_all[off:off + sz].reshape(given[name].shape)
        off += sz

    smalls = SMALL_SHARDED + REPLICATED

    def pack_small(get):
        return _to_rows(jnp.concatenate([get(n).reshape(-1) for n in smalls]), 8)

    outs = _adamw(pack_small(lambda n: given[n]), pack_small(lambda n: grads[n]), pack_small(lambda n: given["m_" + n]),
                  pack_small(lambda n: given["v_" + n]), "adamw_small")
    for res, flat in zip((delta, new_m, new_v), outs):
        flat, off = flat.reshape(-1), 0
        for n in smalls:
            sz = given[n].size
            res[n] = flat[off:off + sz].reshape(given[n].shape)
            off += sz

    return (loss, g_a["x"], *[grads[n] for n in WEIGHT_NAMES], *[delta[n] for n in WEIGHT_NAMES],
            *[new_m[n] for n in WEIGHT_NAMES], *[new_v[n] for n in WEIGHT_NAMES])
```

```python
import functools
import math

import jax
import jax.numpy as jnp
import numpy as np
from jax import lax
from jax.experimental import pallas as pl
from jax.experimental.pallas import tpu as pltpu

F32 = jnp.float32
MXU_DTYPE = jnp.bfloat16

D_MODEL = 1024
N_META = 16
LRU_WIDTH = 512
LRU_HEADS = 4
LRU_HEAD_DIM = 128
CONV_WIDTH = 4
LRU_C = 8.0
MLA_HEADS = 8
MLA_NOPE = 64
MLA_ROPE = 32
MLA_V = 64
MLA_Q_RANK = 256
MLA_KV_RANK = 128
RET_HEADS = 4
RET_QK_DIM = 256
RET_V_DIM = 512
D_FF = 4096
ROPE_BASE = 10000.0
DN_ALPHA = 4.0 ** 0.25
EPS = 1e-5
NEG_INF = -1e30
SEQ_BLOCK = 128

ADAM_LR = 0.001
ADAM_B1 = 0.9
ADAM_B2 = 0.999
ADAM_EPS = 1e-08
ADAM_WD = 0.01
ADAM_STEP = 10

PACK_COLS = 1024
TN_INPUT_VMEM_BYTES = 28 << 20
N_CHIPS = 4

MESH = pl.DeviceIdType.MESH


def _pick(n, target, align):
    best = None
    for t in range(align, min(n, target) + 1, align):
        if n % t == 0:
            best = t
    return n if best is None else best


def _round_up(n, m):
    return (n + m - 1) // m * m


def _relu2(a):
    r = jnp.maximum(a, 0.0)
    return r * r


def _ln_stats(z):
    mu = jnp.mean(z, axis=-1, keepdims=True)
    zc = z - mu
    var = jnp.mean(zc * zc, axis=-1, keepdims=True)
    return zc, lax.rsqrt(var + EPS)


def _mm_nn(a, w, act, name, out_dtype=F32, norm=None):
    M, K = a.shape
    sharded = w.ndim == 3
    n = w.shape[-1]
    N = n * (w.shape[0] if sharded else 1)
    tm = _pick(M, 1088 if K * a.dtype.itemsize <= 4096 and norm is None else 544, 8)
    tn = _pick(n, 1024, 128)
    per = n // tn
    assert norm is None or tn == N

    def body(a_ref, w_ref, *rest):
        av = a_ref[...]
        if act:
            av = _relu2(av.astype(F32))
        r = jnp.dot(av.astype(MXU_DTYPE), w_ref[...].astype(MXU_DTYPE), preferred_element_type=F32)
        if norm is None:
            rest[0][...] = r.astype(out_dtype)
        else:
            r_ref, g_ref, b_ref, o_ref, z_ref = rest
            z = DN_ALPHA * r_ref[...] + r
            zc, rstd = _ln_stats(z)
            z_ref[...] = z
            o_ref[...] = zc * rstd * g_ref[...] + b_ref[...]

    w_spec = pl.BlockSpec((None, K, tn), lambda i, j: (j // per, 0, j % per)) if sharded else pl.BlockSpec((K, tn), lambda i, j: (0, j))
    tile = pl.BlockSpec((tm, tn), lambda i, j: (i, j))
    in_specs, args = [pl.BlockSpec((tm, K), lambda i, j: (i, 0)), w_spec], [a, w]
    if norm is None:
        out_specs, out_shape = tile, jax.ShapeDtypeStruct((M, N), out_dtype)
    else:
        vec = pl.BlockSpec((1, N), lambda i, j: (0, 0))
        in_specs += [tile, vec, vec]
        args += [norm[0], norm[1].reshape(1, N), norm[2].reshape(1, N)]
        out_specs, out_shape = [tile, tile], [jax.ShapeDtypeStruct((M, N), F32)] * 2
    return pl.pallas_call(
        body, name=name, grid=(M // tm, N // tn), in_specs=in_specs, out_specs=out_specs, out_shape=out_shape,
        compiler_params=pltpu.CompilerParams(dimension_semantics=("parallel", "arbitrary")),
    )(*args)


def _mm_nt(g, w, a_src, name, out_dtype=F32, plus=None):
    M, N = g.shape
    sharded = w.ndim == 3
    K, n = w.shape[-2], w.shape[-1]
    if sharded:
        tk, nk = N, 1
    else:
        tk = N if N * g.dtype.itemsize <= 8192 else _pick(N, 2048, 128)
        nk = N // tk
    tm = _pick(M, 1088 if tk * g.dtype.itemsize <= 4096 else 544, 8)
    tn = _pick(K, 1024, 128)
    has_src = a_src is not None
    assert nk == 1 or out_dtype == F32
    assert plus is None or not has_src

    def body(*refs):
        if has_src:
            g_ref, w_ref, s_ref, o_ref = refs
        elif plus is not None:
            g_ref, w_ref, p_ref, o_ref = refs
        else:
            g_ref, w_ref, o_ref = refs
        nt = (((1,), (1,)), ((), ()))
        if sharded:
            r = sum(lax.dot_general(g_ref[:, s * n:(s + 1) * n].astype(MXU_DTYPE), w_ref[s].astype(MXU_DTYPE), nt, preferred_element_type=F32)
                    for s in range(w_ref.shape[0]))
        else:
            r = lax.dot_general(g_ref[...].astype(MXU_DTYPE), w_ref[...].astype(MXU_DTYPE), nt, preferred_element_type=F32)
        if has_src:
            r = r * (2.0 * jnp.maximum(s_ref[...].astype(F32), 0.0))
        first = r if plus is None else r + DN_ALPHA * p_ref[...]
        if nk == 1:
            o_ref[...] = first.astype(out_dtype)
        else:
            k = pl.program_id(2)

            @pl.when(k == 0)
            def _():
                o_ref[...] = first

            @pl.when(k > 0)
            def _():
                o_ref[...] += r

    w_spec = (pl.BlockSpec((w.shape[0], tn, n), lambda i, j, k: (0, j, 0)) if sharded
              else pl.BlockSpec((tn, tk), lambda i, j, k: (j, k)))
    in_specs = [pl.BlockSpec((tm, tk), lambda i, j, k: (i, k)), w_spec]
    args = [g, w]
    if has_src:
        assert nk == 1
        in_specs.append(pl.BlockSpec((tm, tn), lambda i, j, k: (i, j)))
        args.append(a_src)
    if plus is not None:
        in_specs.append(pl.BlockSpec((tm, tn), lambda i, j, k: (i, j)))
        args.append(plus)
    return pl.pallas_call(
        body, name=name,
        grid=(M // tm, K // tn, nk),
        in_specs=in_specs,
        out_specs=pl.BlockSpec((tm, tn), lambda i, j, k: (i, j)),
        out_shape=jax.ShapeDtypeStruct((M, K), out_dtype),
        compiler_params=pltpu.CompilerParams(dimension_semantics=("parallel", "parallel", "arbitrary")),
    )(*args)


def _mm_tn(a, g, act, name, col_shards=1, out_dtype=F32):
    M, K = a.shape
    _, N = g.shape
    n = N // col_shards
    tm, tn = _pick(K, 1024, 128), _pick(n, 1024, 128)
    row_bytes = tm * a.dtype.itemsize + tn * g.dtype.itemsize
    tk = _pick(M, min(2176, TN_INPUT_VMEM_BYTES // (2 * row_bytes)), 8)
    nk = M // tk
    per = n // tn
    direct = out_dtype == F32

    def body(a_ref, g_ref, o_ref, *scratch):
        acc_ref = o_ref if direct else scratch[0]
        k = pl.program_id(2)
        av = a_ref[...]
        if act:
            av = _relu2(av.astype(F32))
        r = lax.dot_general(av.astype(MXU_DTYPE), g_ref[...].astype(MXU_DTYPE),
                            (((0,), (0,)), ((), ())), preferred_element_type=F32)

        @pl.when(k == 0)
        def _():
            acc_ref[...] = r

        @pl.when(k > 0)
        def _():
            acc_ref[...] += r

        if not direct:
            @pl.when(k == nk - 1)
            def _():
                o_ref[...] = acc_ref[...].astype(out_dtype)

    if col_shards == 1:
        out_spec, out_shape = pl.BlockSpec((tm, tn), lambda i, j, k: (i, j)), (K, N)
    else:
        out_spec, out_shape = pl.BlockSpec((None, tm, tn), lambda i, j, k: (j // per, i, j % per)), (col_shards, K, n)
    return pl.pallas_call(
        body, name=name,
        grid=(K // tm, N // tn, nk),
        in_specs=[pl.BlockSpec((tk, tm), lambda i, j, k: (k, i)), pl.BlockSpec((tk, tn), lambda i, j, k: (k, j))],
        out_specs=out_spec,
        out_shape=jax.ShapeDtypeStruct(out_shape, out_dtype),
        scratch_shapes=[] if direct else [pltpu.VMEM((tm, tn), F32)],
        compiler_params=pltpu.CompilerParams(dimension_semantics=("parallel", "parallel", "arbitrary")),
    )(a, g)


@functools.partial(jax.custom_vjp, nondiff_argnums=(3, 4, 5, 6))
def matmul(a, w, w_grad_slot, act, name, out_dtype, col_shards):
    return _mm_nn(a, w, act, name + "_fwd", out_dtype)


def _matmul_fwd(a, w, w_grad_slot, act, name, out_dtype, col_shards):
    return _mm_nn(a, w, act, name + "_fwd", out_dtype), (a, w, jnp.zeros((), w_grad_slot.dtype))


def _matmul_bwd(act, name, out_dtype, col_shards, res, g):
    a, w, slot_like = res
    w_grad_dtype = slot_like.dtype
    da = _mm_nt(g, w, a if act else None, name + "_dx")
    dw = _mm_tn(a, g, act, name + "_dw", col_shards, w_grad_dtype)
    return da, None, dw


matmul.defvjp(_matmul_fwd, _matmul_bwd)


def _ln_bwd_call(z, g, dy, name):
    M, D = z.shape
    tm = _pick(M, 544, 8)

    def body(z_ref, g_ref, dy_ref, dz_ref, dg_ref, db_ref):
        @pl.when(pl.program_id(0) == 0)
        def _():
            dg_ref[...] = jnp.zeros_like(dg_ref)
            db_ref[...] = jnp.zeros_like(db_ref)

        zc, rstd = _ln_stats(z_ref[...])
        xhat = zc * rstd
        dy = dy_ref[...]
        dxh = dy * g_ref[...]
        m1 = jnp.mean(dxh, axis=-1, keepdims=True)
        m2 = jnp.mean(dxh * xhat, axis=-1, keepdims=True)
        dz_ref[...] = rstd * (dxh - m1 - xhat * m2)
        dg_ref[...] += jnp.sum(dy * xhat, axis=0, keepdims=True)
        db_ref[...] += jnp.sum(dy, axis=0, keepdims=True)

    row = pl.BlockSpec((tm, D), lambda i: (i, 0))
    vec = pl.BlockSpec((1, D), lambda i: (0, 0))
    return pl.pallas_call(
        body, name=name, grid=(M // tm,), in_specs=[row, vec, row], out_specs=[row, vec, vec],
        out_shape=[jax.ShapeDtypeStruct((M, D), F32), jax.ShapeDtypeStruct((1, D), F32), jax.ShapeDtypeStruct((1, D), F32)],
        compiler_params=pltpu.CompilerParams(dimension_semantics=("arbitrary",)),
    )(z, g.reshape(1, D), dy)


@functools.partial(jax.custom_vjp, nondiff_argnums=(7,))
def mlp_block(h, w1, w2, w1_grad_slot, w2_grad_slot, g, b, name):
    return _mlp_block_fwd(h, w1, w2, w1_grad_slot, w2_grad_slot, g, b, name)[0]


def _mlp_block_fwd(h, w1, w2, w1_grad_slot, w2_grad_slot, g, b, name):
    u = _mm_nn(h, w1, False, name + "_w1_fwd", out_dtype=MXU_DTYPE)
    out, z = _mm_nn(u, w2, True, name + "_w2_norm_fwd", norm=(h, g, b))
    return out, (h, u, z, w1, w2, g, jnp.zeros((), w1_grad_slot.dtype))


def _mlp_block_bwd(name, res, dy):
    h, u, z, w1, w2, g, slot_like = res
    dz, dg, db = _ln_bwd_call(z, g, dy, name + "_norm_bwd")
    du = _mm_nt(dz, w2, u, name + "_w2_dx", out_dtype=MXU_DTYPE)
    dw2 = _mm_tn(u, dz, True, name + "_w2_dw", 1, slot_like.dtype)
    dh = _mm_nt(du, w1, None, name + "_w1_dx", plus=dz)
    dw1 = _mm_tn(h, du, False, name + "_w1_dw", N_CHIPS, slot_like.dtype)
    return dh, None, None, dw1, dw2, dg.reshape(g.shape), db.reshape(g.shape)


mlp_block.defvjp(_mlp_block_fwd, _mlp_block_bwd)


@functools.partial(jax.custom_vjp, nondiff_argnums=(6,))
def out_block(h, y, w, w_grad_slot, g, b, name):
    return _out_block_fwd(h, y, w, w_grad_slot, g, b, name)[0]


def _out_block_fwd(h, y, w, w_grad_slot, g, b, name):
    out, z = _mm_nn(y, w, False, name + "_norm_fwd", norm=(h, g, b))
    return out, (y, z, w, g, jnp.zeros((), w_grad_slot.dtype))


def _out_block_bwd(name, res, dy):
    y, z, w, g, slot_like = res
    dz, dg, db = _ln_bwd_call(z, g, dy, name + "_norm_bwd")
    d_y = _mm_nt(dz, w, None, name + "_dx")
    dw = _mm_tn(y, dz, False, name + "_dw", 1, slot_like.dtype)
    return DN_ALPHA * dz, d_y, None, dw, dg.reshape(g.shape), db.reshape(g.shape)


out_block.defvjp(_out_block_fwd, _out_block_bwd)


def _rms_fwd_call(x, g, name, col_block=0):
    R = x.shape[0]
    W = g.shape[-1]
    tr = _pick(R, 1088, 8)

    def body(x_ref, g_ref, o_ref):
        xv = x_ref[...]
        rstd = lax.rsqrt(jnp.mean(xv * xv, axis=-1, keepdims=True) + EPS)
        o_ref[...] = xv * rstd * g_ref[...]

    vec = pl.BlockSpec((1, W), lambda i: (0, 0))
    return pl.pallas_call(
        body, name=name, grid=(R // tr,), in_specs=[pl.BlockSpec((tr, W), lambda i: (i, col_block)), vec],
        out_specs=pl.BlockSpec((tr, W), lambda i: (i, 0)), out_shape=jax.ShapeDtypeStruct((R, W), F32),
        compiler_params=pltpu.CompilerParams(dimension_semantics=("parallel",)),
    )(x, g.reshape(1, W))


def _rms_bwd_call(x, g, dy, name, col_block=0):
    R = x.shape[0]
    W = g.shape[-1]
    tr = _pick(R, 1088, 8)

    def body(x_ref, g_ref, dy_ref, dx_ref, dg_ref):
        @pl.when(pl.program_id(0) == 0)
        def _():
            dg_ref[...] = jnp.zeros_like(dg_ref)

        xv = x_ref[...]
        rstd = lax.rsqrt(jnp.mean(xv * xv, axis=-1, keepdims=True) + EPS)
        xhat = xv * rstd
        dy = dy_ref[...]
        dxh = dy * g_ref[...]
        dx_ref[...] = rstd * (dxh - xhat * jnp.mean(dxh * xhat, axis=-1, keepdims=True))
        dg_ref[...] += jnp.sum(dy * xhat, axis=0, keepdims=True)

    row = pl.BlockSpec((tr, W), lambda i: (i, 0))
    vec = pl.BlockSpec((1, W), lambda i: (0, 0))
    return pl.pallas_call(
        body, name=name, grid=(R // tr,), in_specs=[pl.BlockSpec((tr, W), lambda i: (i, col_block)), vec, row], out_specs=[row, vec],
        out_shape=[jax.ShapeDtypeStruct((R, W), F32), jax.ShapeDtypeStruct((1, W), F32)],
        compiler_params=pltpu.CompilerParams(dimension_semantics=("arbitrary",)),
    )(x, g.reshape(1, W), dy)


def _loss_call(h, tgt, n_tokens, name):
    B, Tp, D = h.shape
    tr = _pick(Tp, 544, 8)

    def body(y_ref, t_ref, dy_ref, acc_ref):
        @pl.when(jnp.logical_and(pl.program_id(0) == 0, pl.program_id(1) == 0))
        def _():
            acc_ref[...] = jnp.zeros_like(acc_ref)

        t = lax.broadcasted_iota(jnp.int32, (tr, 1), 0) + pl.program_id(1) * tr
        counts = jnp.logical_and(t >= N_META, t < N_META + n_tokens)
        e = jnp.where(counts, y_ref[...] - t_ref[...], 0.0)
        dy_ref[...] = e * (1.0 / D)
        acc_ref[...] += jnp.sum(jnp.sum(e * e, axis=-1, keepdims=True), axis=0, keepdims=True) * (0.5 / D)

    row = pl.BlockSpec((None, tr, D), lambda b, i: (b, i, 0))
    one = pl.BlockSpec((1, 1), lambda b, i: (0, 0))
    return pl.pallas_call(
        body, name=name, grid=(B, Tp // tr), in_specs=[row, row], out_specs=[row, one],
        out_shape=[jax.ShapeDtypeStruct((B, Tp, D), F32), jax.ShapeDtypeStruct((1, 1), F32)],
        compiler_params=pltpu.CompilerParams(dimension_semantics=("arbitrary", "arbitrary")),
    )(h, tgt)


@functools.partial(jax.custom_vjp, nondiff_argnums=(2,))
def loss_head(h, tgt, n_tokens):
    return _loss_call(h, tgt, n_tokens, "loss_head")[1][0, 0]


def _loss_head_fwd(h, tgt, n_tokens):
    dy, acc = _loss_call(h, tgt, n_tokens, "loss_head")
    return acc[0, 0], dy


def _loss_head_bwd(n_tokens, dy, ct):
    return ct * dy, None


loss_head.defvjp(_loss_head_fwd, _loss_head_bwd)


_GELU_C = math.sqrt(2.0 / math.pi)


def _gelu_parts(x):
    x2 = x * x
    t = jnp.tanh(_GELU_C * (x + 0.044715 * x * x2))
    gelu = 0.5 * x * (1.0 + t)
    dgelu = 0.5 * (1.0 + t) + 0.5 * x * (1.0 - t * t) * (_GELU_C * (1.0 + 3.0 * 0.044715 * x2))
    return gelu, dgelu


def _sigmoid(x):
    return 1.0 / (1.0 + jnp.exp(-x))


def _scan8(a, b, carry, reverse):
    row = lax.broadcasted_iota(jnp.int32, a.shape, 0)
    for s in (1, 2, 4):
        shift = 8 - s if reverse else s
        keep = (row < 8 - s) if reverse else (row >= s)
        b = jnp.where(keep, a * pltpu.roll(b, shift, 0) + b, b)
        a = jnp.where(keep, a * pltpu.roll(a, shift, 0), a)
    return a * carry + b


def _lru_pre(prec_ref, prev_ref, first, cw_ref, cb_ref, wa_ref, ba_ref, wx_ref, bx_ref, sp_ref):
    tc = prec_ref.shape[0]
    prev = jnp.where(first, 0.0, prev_ref[...])
    ext = jnp.concatenate([prev, prec_ref[...]], axis=0)
    cw = cw_ref[...]
    taps = [ext[8:] if k == CONV_WIDTH - 1 else pltpu.roll(ext, CONV_WIDTH - 1 - k, 0)[8:] for k in range(CONV_WIDTH)]
    xc = cb_ref[...] + sum(cw[k:k + 1, :] * taps[k] for k in range(CONV_WIDTH))
    ga, gx = [], []
    for h in range(LRU_HEADS):
        xh = xc[:, h * LRU_HEAD_DIM:(h + 1) * LRU_HEAD_DIM].astype(MXU_DTYPE)
        ga.append(jnp.dot(xh, wa_ref[h].astype(MXU_DTYPE), preferred_element_type=F32))
        gx.append(jnp.dot(xh, wx_ref[h].astype(MXU_DTYPE), preferred_element_type=F32))
    r = _sigmoid(jnp.concatenate(ga, axis=1) + ba_ref[...])
    i = _sigmoid(jnp.concatenate(gx, axis=1) + bx_ref[...])
    log_a = -LRU_C * r * sp_ref[...]
    a = jnp.exp(log_a)
    a2 = a * a
    mult = jnp.sqrt(-jnp.tanh(log_a) * (a2 + 1.0))
    return taps, xc, r, i, a, a2, mult


def _lru_fwd_call(p, cw, cb, wa, ba, wx, bx, sp):
    B, Tp, _ = p.shape
    W = LRU_WIDTH
    tc = SEQ_BLOCK
    nc = Tp // tc

    def body(pg_ref, prec_ref, prev_ref, cw_ref, cb_ref, wa_ref, ba_ref, wx_ref, bx_ref, sp_ref, y_ref, h_ref, carry_ref):
        first = pl.program_id(1) == 0

        @pl.when(first)
        def _():
            carry_ref[...] = jnp.zeros_like(carry_ref)

        _, xc, r, i, a, a2, mult = _lru_pre(prec_ref, prev_ref, first, cw_ref, cb_ref, wa_ref, ba_ref, wx_ref, bx_ref, sp_ref)
        b = mult * (i * xc)
        carry = carry_ref[0:1, :]
        for t in range(tc // 8):
            h = _scan8(a[8 * t:8 * t + 8], b[8 * t:8 * t + 8], carry, False)
            h_ref[8 * t:8 * t + 8, :] = h
            carry = h[7:8, :]
        carry_ref[...] = jnp.broadcast_to(carry, carry_ref.shape)
        y_ref[...] = h_ref[...] * _gelu_parts(pg_ref[...])[0]

    cur = pl.BlockSpec((None, tc, W), lambda b, j: (b, j, 0))
    rec = pl.BlockSpec((None, tc, W), lambda b, j: (b, j, 1))
    prev = pl.BlockSpec((None, 8, W), lambda b, j: (b, jnp.maximum(j * (tc // 8) - 1, 0), 1))
    vec = pl.BlockSpec((1, W), lambda b, j: (0, 0))
    cws = pl.BlockSpec((CONV_WIDTH, W), lambda b, j: (0, 0))
    wsp = pl.BlockSpec((LRU_HEADS, LRU_HEAD_DIM, LRU_HEAD_DIM), lambda b, j: (0, 0, 0))
    return pl.pallas_call(
        body, name="lru_fwd", grid=(B, nc),
        in_specs=[cur, rec, prev, cws, vec, wsp, vec, wsp, vec, vec],
        out_specs=[cur, cur],
        out_shape=[jax.ShapeDtypeStruct((B, Tp, W + MLA_HEADS * HEAD_LANES), F32), jax.ShapeDtypeStruct((B, Tp, W), F32)],
        scratch_shapes=[pltpu.VMEM((8, W), F32)],
        compiler_params=pltpu.CompilerParams(dimension_semantics=("arbitrary", "arbitrary")),
    )(p, p, p, cw, cb, wa, ba, wx, bx, sp)


def _lru_bwd_call(p, hseq, dy, cw, cb, wa, ba, wx, bx, sp, dpq, dpkv, dkpe):
    B, Tp, P = p.shape
    W = LRU_WIDTH
    tc = SEQ_BLOCK
    nc = Tp // tc
    HD = LRU_HEAD_DIM

    def body(pg_ref, prec_ref, prev_ref, h_ref, hprev_ref, dy_ref, cw_ref, cb_ref, wa_ref, ba_ref, wx_ref, bx_ref, sp_ref,
             dpq_ref, dpkv_ref, dkpe_ref, dp_ref, dcw_ref, dcb_ref, dwa_ref, dba_ref, dwx_ref, dbx_ref, dsp_ref,
             gcar_ref, anext_ref, halo_ref, g_ref):
        j = pl.program_id(1)
        first = j == nc - 1
        last = j == 0

        @pl.when(jnp.logical_and(pl.program_id(0) == 0, last))
        def _():
            for ref in (dcw_ref, dcb_ref, dwa_ref, dba_ref, dwx_ref, dbx_ref, dsp_ref):
                ref[...] = jnp.zeros_like(ref)

        @pl.when(last)
        def _():
            gcar_ref[...] = jnp.zeros_like(gcar_ref)
            anext_ref[...] = jnp.zeros_like(anext_ref)
            halo_ref[...] = jnp.zeros_like(halo_ref)

        taps, xc, r, i, a, a2, mult = _lru_pre(prec_ref, prev_ref, first, cw_ref, cb_ref, wa_ref, ba_ref, wx_ref, bx_ref, sp_ref)
        row = lax.broadcasted_iota(jnp.int32, (tc, W), 0)
        gelu, dgelu = _gelu_parts(pg_ref[...])
        dy = dy_ref[...]
        hcur = h_ref[...]
        dp_ref[:, 0:W] = dy * hcur * dgelu
        dp_ref[:, 2 * W:2 * W + MLA_Q_RANK] = dpq_ref[...]
        dp_ref[:, _KPE_START - MLA_KV_RANK:_KPE_START] = dpkv_ref[...]
        dp_ref[:, _KPE_START:P] = pltpu.roll(dkpe_ref[...], HEAD_LANES - MLA_NOPE, 1)[:, 0:P - _KPE_START]
        dh = dy * gelu
        a_next = jnp.where(row == tc - 1, anext_ref[0:1, :], pltpu.roll(a, tc - 1, 0))
        carry = gcar_ref[0:1, :]
        for t in reversed(range(tc // 8)):
            g = _scan8(a_next[8 * t:8 * t + 8], dh[8 * t:8 * t + 8], carry, True)
            g_ref[8 * t:8 * t + 8, :] = g
            carry = g[0:1, :]
        gcar_ref[...] = jnp.broadcast_to(carry, gcar_ref.shape)
        anext_ref[...] = jnp.broadcast_to(a[0:1, :], anext_ref.shape)
        G = g_ref[...]
        h_before = jnp.where(first, 0.0, hprev_ref[7:8, :])
        hprev = jnp.where(row == 0, h_before, pltpu.roll(hcur, 1, 0))
        d_a = G * hprev
        gx_ = G * xc
        d_mult = gx_ * i
        d_i = gx_ * mult
        dxc = G * (mult * i)
        d_la = d_a * a - d_mult * (a2 / mult)
        sp = sp_ref[...]
        d_r = d_la * (-LRU_C * sp)
        dsp_ref[...] += jnp.sum(d_la * (-LRU_C * r), axis=0, keepdims=True)
        dga = d_r * r * (1.0 - r)
        dgx = d_i * i * (1.0 - i)
        dba_ref[...] += jnp.sum(dga, axis=0, keepdims=True)
        dbx_ref[...] += jnp.sum(dgx, axis=0, keepdims=True)
        back = []
        for h in range(LRU_HEADS):
            sl = slice(h * HD, (h + 1) * HD)
            xh = xc[:, sl].astype(MXU_DTYPE)
            ah = dga[:, sl].astype(MXU_DTYPE)
            bh = dgx[:, sl].astype(MXU_DTYPE)
            tn = (((0,), (0,)), ((), ()))
            nt = (((1,), (1,)), ((), ()))
            dwa_ref[h] += lax.dot_general(xh, ah, tn, preferred_element_type=F32)
            dwx_ref[h] += lax.dot_general(xh, bh, tn, preferred_element_type=F32)
            back.append(lax.dot_general(ah, wa_ref[h].astype(MXU_DTYPE), nt, preferred_element_type=F32)
                        + lax.dot_general(bh, wx_ref[h].astype(MXU_DTYPE), nt, preferred_element_type=F32))
        dxc = dxc + jnp.concatenate(back, axis=1)
        dcb_ref[...] += jnp.sum(dxc, axis=0, keepdims=True)
        for k in range(CONV_WIDTH):
            dcw_ref[k:k + 1, :] += jnp.sum(dxc * taps[k], axis=0, keepdims=True)
        ext = jnp.concatenate([dxc, halo_ref[...]], axis=0)
        cw = cw_ref[...]
        acc = cw[CONV_WIDTH - 1:CONV_WIDTH, :] * dxc
        for k in range(CONV_WIDTH - 1):
            s = CONV_WIDTH - 1 - k
            acc = acc + cw[k:k + 1, :] * pltpu.roll(ext, tc + 8 - s, 0)[:tc]
        dp_ref[:, W:2 * W] = acc
        halo_ref[...] = dxc[0:8, :]

    rev = lambda j: nc - 1 - j
    cur = pl.BlockSpec((None, tc, W), lambda b, j: (b, rev(j), 0))
    rec = pl.BlockSpec((None, tc, W), lambda b, j: (b, rev(j), 1))
    prev = pl.BlockSpec((None, 8, W), lambda b, j: (b, jnp.maximum(rev(j) * (tc // 8) - 1, 0), 0))
    prev_rec = pl.BlockSpec((None, 8, W), lambda b, j: (b, jnp.maximum(rev(j) * (tc // 8) - 1, 0), 1))
    vec = pl.BlockSpec((1, W), lambda b, j: (0, 0))
    cws = pl.BlockSpec((CONV_WIDTH, W), lambda b, j: (0, 0))
    wsp = pl.BlockSpec((LRU_HEADS, HD, HD), lambda b, j: (0, 0, 0))
    vs = jax.ShapeDtypeStruct((1, W), F32)
    ws = jax.ShapeDtypeStruct((LRU_HEADS, HD, HD), F32)

    def rows(width):
        return pl.BlockSpec((None, tc, width), lambda b, j: (b, rev(j), 0))

    return pl.pallas_call(
        body, name="lru_bwd", grid=(B, nc),
        in_specs=[cur, rec, prev_rec, cur, prev, cur, cws, vec, wsp, vec, wsp, vec, vec, rows(MLA_Q_RANK), rows(MLA_KV_RANK), rows(HEAD_LANES)],
        out_specs=[rows(P), cws, vec, wsp, vec, wsp, vec, vec],
        out_shape=[jax.ShapeDtypeStruct((B, Tp, P), F32), jax.ShapeDtypeStruct((CONV_WIDTH, W), F32), vs, ws, vs, ws, vs, vs],
        scratch_shapes=[pltpu.VMEM((8, W), F32), pltpu.VMEM((8, W), F32), pltpu.VMEM((8, W), F32), pltpu.VMEM((tc, W), F32)],
        compiler_params=pltpu.CompilerParams(dimension_semantics=("arbitrary", "arbitrary")),
    )(p, p, p, hseq, hseq, dy, cw, cb, wa, ba, wx, bx, sp, dpq, dpkv, dkpe)


_Q_BLOCK = 2 * LRU_WIDTH // MLA_Q_RANK
_KV_BLOCK = (2 * LRU_WIDTH + MLA_Q_RANK) // MLA_KV_RANK
_KPE_START = 2 * LRU_WIDTH + MLA_Q_RANK + MLA_KV_RANK


@jax.custom_vjp
def even_front(p, cw, cb, wa, ba, wx, bx, sp, gq, gkv):
    return _even_front_fwd(p, cw, cb, wa, ba, wx, bx, sp, gq, gkv)[0]


def _even_front_fwd(p, cw, cb, wa, ba, wx, bx, sp, gq, gkv):
    B, Tp, W = p.shape
    p2d = p.reshape(B * Tp, W)
    y, hseq = _lru_fwd_call(p, cw, cb, wa, ba, wx, bx, sp)
    qn = _rms_fwd_call(p2d, gq, "q_norm_fwd", _Q_BLOCK)
    kvn = _rms_fwd_call(p2d, gkv, "kv_norm_fwd", _KV_BLOCK)
    kpe = jnp.pad(p[:, :, _KPE_START:], ((0, 0), (0, 0), (MLA_NOPE, HEAD_LANES - MLA_NOPE - MLA_ROPE)))
    return (y, qn, kvn, kpe), (p, hseq, cw, cb, wa, ba, wx, bx, sp, gq, gkv)


def _even_front_bwd(res, cts):
    p, hseq, cw, cb, wa, ba, wx, bx, sp, gq, gkv = res
    dy, dqn, dkvn, dkpe = cts
    B, Tp, W = p.shape
    p2d = p.reshape(B * Tp, W)
    dpq, dgq = _rms_bwd_call(p2d, gq, dqn, "q_norm_bwd", _Q_BLOCK)
    dpkv, dgkv = _rms_bwd_call(p2d, gkv, dkvn, "kv_norm_bwd", _KV_BLOCK)
    dp, dcw, dcb, dwa, dba, dwx, dbx, dsp = _lru_bwd_call(p, hseq, dy, cw, cb, wa, ba, wx, bx, sp, dpq.reshape(B, Tp, -1),
                                                          dpkv.reshape(B, Tp, -1), dkpe)
    return dp, dcw, dcb, dwa, dba, dwx, dbx, dsp, dgq.reshape(gq.shape), dgkv.reshape(gkv.shape)


even_front.defvjp(_even_front_fwd, _even_front_bwd)


def _rope_tables(T, half):
    inv = np.float32(ROPE_BASE) ** (-np.arange(half, dtype=np.float32) / np.float32(half))
    ang = np.arange(T, dtype=np.float32)[:, None] * inv[None, :]
    return np.cos(ang), np.sin(ang)


_NT = (((1,), (1,)), ((), ()))
_TN = (((0,), (0,)), ((), ()))
HEAD_LANES = 128
_MLA_SCALE = (MLA_NOPE + MLA_ROPE) ** -0.5
_LOG2E = math.log2(math.e)


Q_BLOCK = 512


def _query_blocks(Tp):
    first = Tp % Q_BLOCK or Q_BLOCK
    return [(0, first)] + [(r, r + Q_BLOCK) for r in range(first, Tp, Q_BLOCK)]


def _mask_diagonal(s, fill):
    R, L = s.shape
    row = lax.broadcasted_iota(jnp.int32, (R, R), 0)
    col = lax.broadcasted_iota(jnp.int32, (R, R), 1)
    last = jnp.where(col <= row, s[:, L - R:], fill)
    return last if L == R else jnp.concatenate([s[:, :L - R], last], axis=1)


def _mla_rope_tables(T):
    half = MLA_ROPE // 2
    cos, sin = _rope_tables(T, half)
    ones, zeros = np.ones((T, MLA_NOPE), np.float32), np.zeros((T, MLA_NOPE), np.float32)
    tail1, tail0 = np.ones((T, HEAD_LANES - MLA_NOPE - MLA_ROPE), np.float32), np.zeros((T, HEAD_LANES - MLA_NOPE - MLA_ROPE), np.float32)
    zh = np.zeros((T, half), np.float32)
    c = np.concatenate([ones, cos, cos, tail1], axis=1)
    s_up = np.concatenate([zeros, -sin, zh, tail0], axis=1)
    s_down = np.concatenate([zeros, zh, sin, tail0], axis=1)
    return jnp.asarray(c), jnp.asarray(s_up), jnp.asarray(s_down)


def _rope_lanes(x, c, s_up, s_down):
    half = MLA_ROPE // 2
    return x * c + pltpu.roll(x, HEAD_LANES - half, 1) * s_up + pltpu.roll(x, half, 1) * s_down


def _unrope_lanes(d, c, s_up, s_down):
    half = MLA_ROPE // 2
    return d * c + pltpu.roll(d * s_up, half, 1) + pltpu.roll(d * s_down, HEAD_LANES - half, 1)


def _mla_operands(q_ref, kv_ref, kpe_ref, c, s_up, s_down):
    lane = lax.broadcasted_iota(jnp.int32, kv_ref.shape, 1)
    qr = (_rope_lanes(q_ref[...].astype(F32), c, s_up, s_down) * (_MLA_SCALE * _LOG2E)).astype(MXU_DTYPE)
    kr = jnp.where(lane < MLA_NOPE, kv_ref[...].astype(F32), _rope_lanes(kpe_ref[...], c, s_up, s_down)).astype(MXU_DTYPE)
    return qr, kr, lane


def _mla_specs(Tp):
    head = pl.BlockSpec((None, Tp, HEAD_LANES), lambda b, h: (b, 0, h))
    shared = pl.BlockSpec((None, Tp, HEAD_LANES), lambda b, h: (b, 0, 0))
    tab = pl.BlockSpec((Tp, HEAD_LANES), lambda b, h: (0, 0))
    lse = pl.BlockSpec((None, None, Tp, 1), lambda b, h: (b, h, 0, 0))
    out = pl.BlockSpec((None, Tp, HEAD_LANES), lambda b, h: (b, 0, LRU_WIDTH // HEAD_LANES + h))
    return head, shared, tab, lse, out


def _attn_fwd_call(q, kv, kpe, tabs, y):
    B, Tp, _ = q.shape

    def body(q_ref, kv_ref, kpe_ref, c_ref, su_ref, sd_ref, y_ref, o_ref, lse_ref, qr_ref, kr_ref):
        qr, kr, lane = _mla_operands(q_ref, kv_ref, kpe_ref, c_ref[...], su_ref[...], sd_ref[...])
        qr_ref[...] = qr
        kr_ref[...] = kr
        for r0, L in _query_blocks(Tp):
            blk = slice(r0, L)
            s = _mask_diagonal(lax.dot_general(qr_ref[blk, :], kr_ref[0:L, :], _NT, preferred_element_type=F32), NEG_INF)
            m = jnp.max(s, axis=-1, keepdims=True)
            p = jnp.exp2(s - m)
            l = jnp.sum(p, axis=-1, keepdims=True)
            o = jnp.dot(p.astype(MXU_DTYPE), kv_ref[0:L, :].astype(MXU_DTYPE), preferred_element_type=F32)
            o_ref[blk, :] = jnp.where(lane[blk, :] >= MLA_NOPE, o / l, 0.0)
            lse_ref[blk, :] = m + jnp.log2(l)

    head, shared, tab, lse, out = _mla_specs(Tp)
    return pl.pallas_call(
        body, name="mla_attn_fwd", grid=(B, MLA_HEADS),
        in_specs=[head, head, shared, tab, tab, tab, pl.BlockSpec(memory_space=pl.ANY)], out_specs=[out, lse],
        out_shape=[jax.ShapeDtypeStruct(y.shape, F32), jax.ShapeDtypeStruct((B, MLA_HEADS, Tp, 1), F32)],
        input_output_aliases={6: 0},
        scratch_shapes=[pltpu.VMEM((Tp, HEAD_LANES), MXU_DTYPE), pltpu.VMEM((Tp, HEAD_LANES), MXU_DTYPE)],
        compiler_params=pltpu.CompilerParams(dimension_semantics=("parallel", "parallel")),
    )(q, kv, kpe, *tabs, y)


def _attn_bwd_call(q, kv, kpe, tabs, o, lse, do):
    B, Tp, _ = q.shape

    def body(q_ref, kv_ref, kpe_ref, c_ref, su_ref, sd_ref, o_ref, lse_ref, do_ref, dq_ref, dkv_ref, dkpe_ref,
             qr_ref, kr_ref, dqa_ref, dka_ref, dva_ref):
        c, s_up, s_down = c_ref[...], su_ref[...], sd_ref[...]
        qr, kr, lane = _mla_operands(q_ref, kv_ref, kpe_ref, c, s_up, s_down)
        qr_ref[...] = qr
        kr_ref[...] = kr
        dka_ref[...] = jnp.zeros_like(dka_ref)
        dva_ref[...] = jnp.zeros_like(dva_ref)
        for r0, L in _query_blocks(Tp):
            blk = slice(r0, L)
            qb = qr_ref[blk, :]
            do = jnp.where(lane[blk, :] >= MLA_NOPE, do_ref[blk, :], 0.0)
            delta = jnp.sum(do * o_ref[blk, :], axis=-1, keepdims=True)
            s = _mask_diagonal(lax.dot_general(qb, kr_ref[0:L, :], _NT, preferred_element_type=F32), NEG_INF)
            p = jnp.exp2(s - lse_ref[blk, :])
            dob = do.astype(MXU_DTYPE)
            dva_ref[0:L, :] += lax.dot_general(p.astype(MXU_DTYPE), dob, _TN, preferred_element_type=F32)
            dp = lax.dot_general(dob, kv_ref[0:L, :].astype(MXU_DTYPE), _NT, preferred_element_type=F32)
            ds = (p * (dp - delta)).astype(MXU_DTYPE)
            dqa_ref[blk, :] = jnp.dot(ds, kr_ref[0:L, :], preferred_element_type=F32)
            dka_ref[0:L, :] += lax.dot_general(ds, qb, _TN, preferred_element_type=F32)
        dq_ref[...] = _unrope_lanes(dqa_ref[...] * _MLA_SCALE, c, s_up, s_down).astype(dq_ref.dtype)
        dk = dka_ref[...] * (1.0 / _LOG2E)
        dkv_ref[...] = jnp.where(lane < MLA_NOPE, dk, dva_ref[...]).astype(dkv_ref.dtype)
        dkpe = jnp.where(lane >= MLA_NOPE, _unrope_lanes(dk, c, s_up, s_down), 0.0)

        @pl.when(pl.program_id(1) == 0)
        def _():
            dkpe_ref[...] = dkpe

        @pl.when(pl.program_id(1) > 0)
        def _():
            dkpe_ref[...] += dkpe

    head, shared, tab, lse_spec, out = _mla_specs(Tp)
    wide = jax.ShapeDtypeStruct((B, Tp, MLA_HEADS * HEAD_LANES), q.dtype)
    acc = pltpu.VMEM((Tp, HEAD_LANES), F32)
    return pl.pallas_call(
        body, name="mla_attn_bwd", grid=(B, MLA_HEADS),
        in_specs=[head, head, shared, tab, tab, tab, out, lse_spec, out], out_specs=[head, head, shared],
        out_shape=[wide, wide, jax.ShapeDtypeStruct((B, Tp, HEAD_LANES), F32)],
        scratch_shapes=[pltpu.VMEM((Tp, HEAD_LANES), MXU_DTYPE), pltpu.VMEM((Tp, HEAD_LANES), MXU_DTYPE), acc, acc, acc],
        compiler_params=pltpu.CompilerParams(dimension_semantics=("parallel", "arbitrary")),
    )(q, kv, kpe, *tabs, o, lse, do)


@jax.custom_vjp
def mla_attention(q, kv, kpe, tabs, y):
    return _attn_fwd_call(q, kv, kpe, tabs, y)[0]


def _mla_attention_fwd(q, kv, kpe, tabs, y):
    o, lse = _attn_fwd_call(q, kv, kpe, tabs, y)
    return o, (q, kv, kpe, tabs, o, lse)


def _mla_attention_bwd(res, do):
    q, kv, kpe, tabs, o, lse = res
    dq, dkv, dkpe = _attn_bwd_call(q, kv, kpe, tabs, o, lse, do)
    return dq, dkv, dkpe, None, do


mla_attention.defvjp(_mla_attention_fwd, _mla_attention_bwd)


def _rope_halves(x, cos, sin):
    half = x.shape[1] // 2
    x1, x2 = x[:, :half], x[:, half:]
    return jnp.concatenate([x1 * cos - x2 * sin, x1 * sin + x2 * cos], axis=1)


def _unrope_halves(d, cos, sin):
    half = d.shape[1] // 2
    d1, d2 = d[:, :half], d[:, half:]
    return jnp.concatenate([d1 * cos + d2 * sin, d2 * cos - d1 * sin], axis=1)


_RET_K_SCALE = RET_QK_DIM ** -0.5
_RET_Q_BLOCKS = RET_HEADS
_RET_V_BLOCK0 = 2 * RET_HEADS * RET_QK_DIM // RET_V_DIM
_RET_G_BLOCK0 = _RET_V_BLOCK0 + RET_HEADS


def _ret_specs(Tp):
    q = pl.BlockSpec((None, Tp, RET_QK_DIM), lambda b, h: (b, 0, h))
    k = pl.BlockSpec((None, Tp, RET_QK_DIM), lambda b, h: (b, 0, _RET_Q_BLOCKS + h))
    v = pl.BlockSpec((None, Tp, RET_V_DIM), lambda b, h: (b, 0, _RET_V_BLOCK0 + h))
    tab = pl.BlockSpec((Tp, RET_QK_DIM // 2), lambda b, h: (0, 0))
    lg = pl.BlockSpec((None, 1, 1), lambda b, h: (h, 0, 0))
    return q, k, v, tab, lg


def _ret_operands(q_ref, k_ref, cos, sin, lg):
    t = lax.broadcasted_iota(jnp.int32, (q_ref.shape[0], 1), 0).astype(F32)
    grow, shrink = jnp.exp(-lg * t), jnp.exp(lg * t)
    qs = (_rope_halves(q_ref[...].astype(F32), cos, sin) * shrink).astype(MXU_DTYPE)
    ks = (_rope_halves(k_ref[...].astype(F32), cos, sin) * (grow * _RET_K_SCALE)).astype(MXU_DTYPE)
    return qs, ks, shrink, grow * _RET_K_SCALE


def _ret_core_fwd_call(p, cos, sin, lg):
    B, Tp, _ = p.shape

    def body(q_ref, k_ref, v_ref, cos_ref, sin_ref, lg_ref, o_ref, qs_ref, ks_ref):
        qs_ref[...], ks_ref[...], _, _ = _ret_operands(q_ref, k_ref, cos_ref[...], sin_ref[...], lg_ref[...])
        for r0, L in _query_blocks(Tp):
            blk = slice(r0, L)
            s = _mask_diagonal(lax.dot_general(qs_ref[blk, :], ks_ref[0:L, :], _NT, preferred_element_type=F32), 0.0)
            o_ref[blk, :] = jnp.dot(s.astype(MXU_DTYPE), v_ref[0:L, :].astype(MXU_DTYPE), preferred_element_type=F32)

    q, k, v, tab, lgs = _ret_specs(Tp)
    return pl.pallas_call(
        body, name="retention_fwd", grid=(B, RET_HEADS), in_specs=[q, k, v, tab, tab, lgs],
        out_specs=pl.BlockSpec((None, Tp, RET_V_DIM), lambda b, h: (b, 0, h)),
        out_shape=jax.ShapeDtypeStruct((B, Tp, RET_HEADS * RET_V_DIM), F32),
        scratch_shapes=[pltpu.VMEM((Tp, RET_QK_DIM), MXU_DTYPE), pltpu.VMEM((Tp, RET_QK_DIM), MXU_DTYPE)],
        compiler_params=pltpu.CompilerParams(dimension_semantics=("parallel", "parallel")),
    )(p, p, p, cos, sin, lg)


def _ret_core_bwd_call(p, do, cos, sin, lg):
    B, Tp, _ = p.shape

    def body(q_ref, k_ref, v_ref, do_ref, cos_ref, sin_ref, lg_ref, dq_ref, dk_ref, dv_ref, qs_ref, ks_ref, dqa_ref, dka_ref, dva_ref):
        cos_, sin_ = cos_ref[...], sin_ref[...]
        qs_ref[...], ks_ref[...], q_scale, k_scale = _ret_operands(q_ref, k_ref, cos_, sin_, lg_ref[...])
        dka_ref[...] = jnp.zeros_like(dka_ref)
        dva_ref[...] = jnp.zeros_like(dva_ref)
        for r0, L in _query_blocks(Tp):
            blk = slice(r0, L)
            qb = qs_ref[blk, :]
            dob = do_ref[blk, :].astype(MXU_DTYPE)
            s = _mask_diagonal(lax.dot_general(qb, ks_ref[0:L, :], _NT, preferred_element_type=F32), 0.0).astype(MXU_DTYPE)
            dva_ref[0:L, :] += lax.dot_general(s, dob, _TN, preferred_element_type=F32)
            ds = _mask_diagonal(lax.dot_general(dob, v_ref[0:L, :].astype(MXU_DTYPE), _NT, preferred_element_type=F32), 0.0).astype(MXU_DTYPE)
            dqa_ref[blk, :] = jnp.dot(ds, ks_ref[0:L, :], preferred_element_type=F32)
            dka_ref[0:L, :] += lax.dot_general(ds, qb, _TN, preferred_element_type=F32)
        dq_ref[...] = _unrope_halves(dqa_ref[...] * q_scale, cos_, sin_).astype(dq_ref.dtype)
        dk_ref[...] = _unrope_halves(dka_ref[...] * k_scale, cos_, sin_).astype(dk_ref.dtype)
        dv_ref[...] = dva_ref[...].astype(dv_ref.dtype)

    q, k, v, tab, lgs = _ret_specs(Tp)
    qk_out = pl.BlockSpec((None, Tp, RET_QK_DIM), lambda b, h: (b, 0, h))
    v_out = pl.BlockSpec((None, Tp, RET_V_DIM), lambda b, h: (b, 0, h))
    return pl.pallas_call(
        body, name="retention_bwd", grid=(B, RET_HEADS), in_specs=[q, k, v, v_out, tab, tab, lgs],
        out_specs=[qk_out, qk_out, v_out],
        out_shape=[jax.ShapeDtypeStruct((B, Tp, RET_HEADS * RET_QK_DIM), p.dtype), jax.ShapeDtypeStruct((B, Tp, RET_HEADS * RET_QK_DIM), p.dtype),
                   jax.ShapeDtypeStruct((B, Tp, RET_HEADS * RET_V_DIM), p.dtype)],
        scratch_shapes=[pltpu.VMEM((Tp, RET_QK_DIM), MXU_DTYPE), pltpu.VMEM((Tp, RET_QK_DIM), MXU_DTYPE),
                        pltpu.VMEM((Tp, RET_QK_DIM), F32), pltpu.VMEM((Tp, RET_QK_DIM), F32), pltpu.VMEM((Tp, RET_V_DIM), F32)],
        compiler_params=pltpu.CompilerParams(dimension_semantics=("parallel", "parallel")),
    )(p, p, p, do, cos, sin, lg)


def _ret_gate_specs(M):
    tm = _pick(M, 1088, 8)
    head = pl.BlockSpec((tm, RET_V_DIM), lambda i, h: (i, h))
    gate = pl.BlockSpec((tm, RET_V_DIM), lambda i, h: (i, _RET_G_BLOCK0 + h))
    return tm, head, gate


def _ret_gate_fwd_call(o, p2d):
    M = o.shape[0]
    tm, head, gate = _ret_gate_specs(M)

    def body(o_ref, g_ref, y_ref):
        ov = o_ref[...]
        gv = g_ref[...].astype(F32)
        rstd = lax.rsqrt(jnp.mean(ov * ov, axis=-1, keepdims=True) + EPS)
        y_ref[...] = (gv * _sigmoid(gv)) * (ov * rstd)

    return pl.pallas_call(
        body, name="retention_gate_fwd", grid=(M // tm, RET_HEADS), in_specs=[head, gate], out_specs=head,
        out_shape=jax.ShapeDtypeStruct(o.shape, F32),
        compiler_params=pltpu.CompilerParams(dimension_semantics=("parallel", "parallel")),
    )(o, p2d)


def _ret_gate_bwd_call(o, p2d, dy):
    M = o.shape[0]
    tm, head, gate = _ret_gate_specs(M)

    def body(o_ref, g_ref, dy_ref, do_ref, dg_ref):
        ov = o_ref[...]
        gv = g_ref[...].astype(F32)
        dy = dy_ref[...]
        rstd = lax.rsqrt(jnp.mean(ov * ov, axis=-1, keepdims=True) + EPS)
        on = ov * rstd
        sg = _sigmoid(gv)
        dg_ref[...] = (dy * on * (sg * (1.0 + gv * (1.0 - sg)))).astype(dg_ref.dtype)
        don = dy * (gv * sg)
        do_ref[...] = (rstd * (don - on * jnp.mean(don * on, axis=-1, keepdims=True))).astype(do_ref.dtype)

    shp = jax.ShapeDtypeStruct(o.shape, p2d.dtype)
    return pl.pallas_call(
        body, name="retention_gate_bwd", grid=(M // tm, RET_HEADS), in_specs=[head, gate, head], out_specs=[head, head],
        out_shape=[shp, shp],
        compiler_params=pltpu.CompilerParams(dimension_semantics=("parallel", "parallel")),
    )(o, p2d, dy)


def _log_gamma():
    return jnp.log(1.0 - 2.0 ** (-5.0 - jnp.arange(RET_HEADS, dtype=F32))).reshape(RET_HEADS, 1, 1)


@functools.partial(jax.custom_vjp, nondiff_argnums=(9,))
def retention_block(h, w_in, w_out, w_in_grad_slot, w_out_grad_slot, g, b, cos, sin, dims):
    return _retention_block_fwd(h, w_in, w_out, w_in_grad_slot, w_out_grad_slot, g, b, cos, sin, dims)[0]


def _retention_block_fwd(h, w_in, w_out, w_in_grad_slot, w_out_grad_slot, g, b, cos, sin, dims):
    B, Tp = dims
    p = _mm_nn(h, w_in, False, "od_w_in_fwd", out_dtype=MXU_DTYPE)
    o = _ret_core_fwd_call(p.reshape(B, Tp, -1), cos, sin, _log_gamma())
    y = _ret_gate_fwd_call(o.reshape(B * Tp, -1), p)
    out, z = _mm_nn(y, w_out, False, "od_w_out_norm_fwd", norm=(h, g, b))
    return out, (h, p, o, y, z, w_in, w_out, g, cos, sin, jnp.zeros((), w_in_grad_slot.dtype))


def _retention_block_bwd(dims, res, dout):
    B, Tp = dims
    h, p, o, y, z, w_in, w_out, g, cos, sin, slot_like = res
    dz, dg, db = _ln_bwd_call(z, g, dout, "od_w_out_norm_bwd")
    dy = _mm_nt(dz, w_out, None, "od_w_out_dx")
    dw_out = _mm_tn(y, dz, False, "od_w_out_dw", 1, slot_like.dtype)
    do, dgate = _ret_gate_bwd_call(o.reshape(B * Tp, -1), p, dy)
    dq, dk, dv = _ret_core_bwd_call(p.reshape(B, Tp, -1), do.reshape(B, Tp, -1), cos, sin, _log_gamma())
    dp = jnp.concatenate([dq.reshape(B * Tp, -1), dk.reshape(B * Tp, -1), dv.reshape(B * Tp, -1), dgate], axis=-1)
    dh = _mm_nt(dp, w_in, None, "od_w_in_dx", plus=dz)
    dw_in = _mm_tn(h, dp, False, "od_w_in_dw", N_CHIPS, slot_like.dtype)
    return dh, None, None, dw_in, dw_out, dg.reshape(g.shape), db.reshape(g.shape), None, None


retention_block.defvjp(_retention_block_fwd, _retention_block_bwd)


def _heads_to_lanes(w):
    K = w.shape[0]
    w = w.reshape(K, MLA_HEADS, MLA_NOPE + MLA_ROPE)
    return jnp.pad(w, ((0, 0), (0, 0), (0, HEAD_LANES - MLA_NOPE - MLA_ROPE))).reshape(K, MLA_HEADS * HEAD_LANES)


def _out_rows_to_lanes(w):
    N = w.shape[1]
    att = w[LRU_WIDTH:].reshape(MLA_HEADS, MLA_V, N)
    att = jnp.pad(att, ((0, 0), (HEAD_LANES - MLA_V, 0), (0, 0))).reshape(MLA_HEADS * HEAD_LANES, N)
    return jnp.concatenate([w[:LRU_WIDTH], att], axis=0)


def _seq_dims(x):
    B, S, D = x.shape
    T = S + N_META
    Tp = _round_up(T, SEQ_BLOCK)
    return B, S, T, Tp


def _mixer0(diff, w, token):
    x = diff["x"]
    B, S, T, Tp = _seq_dims(x)
    D = x.shape[-1]
    M = B * Tp

    def mm(a, name, act=False, out_dtype=F32, layout=lambda m: m, col_shards=1):
        return matmul(a, layout(w[name]), layout(diff[name]), act, name, out_dtype, col_shards)

    meta = jnp.broadcast_to((diff["meta_tokens"] + token)[None], (B, N_META, D))
    h = jnp.concatenate([meta, x, jnp.zeros((B, Tp - T, D), F32)], axis=1).reshape(M, D)
    p = mm(h, "ev_w_in")
    sp = jax.nn.softplus(-diff["ev_lru_lambda"]).reshape(1, LRU_WIDTH)
    y, qn, kvn, kpe = even_front(
        p.reshape(B, Tp, -1), diff["ev_conv_w"].reshape(CONV_WIDTH, LRU_WIDTH), diff["ev_conv_b"].reshape(1, LRU_WIDTH),
        diff["ev_w_rg_a"].reshape(LRU_HEADS, LRU_HEAD_DIM, LRU_HEAD_DIM), diff["ev_b_rg_a"].reshape(1, LRU_WIDTH),
        diff["ev_w_rg_x"].reshape(LRU_HEADS, LRU_HEAD_DIM, LRU_HEAD_DIM), diff["ev_b_rg_x"].reshape(1, LRU_WIDTH),
        sp, diff["ev_q_norm_g"].reshape(-1), diff["ev_kv_norm_g"].reshape(-1))
    q = mm(qn, "ev_w_uq", out_dtype=MXU_DTYPE, layout=_heads_to_lanes).reshape(B, Tp, -1)
    kv = mm(kvn, "ev_w_ukv", out_dtype=MXU_DTYPE).reshape(B, Tp, -1)
    y = mla_attention(q, kv, kpe, _mla_rope_tables(Tp), y).reshape(M, -1)
    return out_block(h, y, _out_rows_to_lanes(w["ev_w_out"]), _out_rows_to_lanes(diff["ev_w_out"]),
                     diff["ln_mix_g"], diff["ln_mix_b"], "ev_w_out")


def _mlp0(diff, h, w):
    return mlp_block(h, w["mlp_w1_0"], w["mlp_w2_0"], diff["mlp_w1_0"], diff["mlp_w2_0"], diff["ln_mlp_g"], diff["ln_mlp_b"], "mlp0")


def _layer1_loss(diff, h, w, tgt):
    B, S, T, Tp = _seq_dims(tgt)
    D = tgt.shape[-1]

    cos, sin = (jnp.asarray(t) for t in _rope_tables(Tp, RET_QK_DIM // 2))
    h = retention_block(h, w["od_w_in"], w["od_w_out"], diff["od_w_in"], diff["od_w_out"], diff["ln_mix_g"], diff["ln_mix_b"], cos, sin, (B, Tp))
    h = mlp_block(h, w["mlp_w1_1"], w["mlp_w2_1"], diff["mlp_w1_1"], diff["mlp_w2_1"], diff["ln_mlp_g"], diff["ln_mlp_b"], "mlp1")
    return loss_head(h.reshape(B, Tp, D), jnp.pad(tgt, ((0, 0), (N_META, Tp - T), (0, 0))), S)


_HBM = pl.BlockSpec(memory_space=pltpu.HBM)


def _place():
    return lax.axis_index("x"), lax.axis_index("y"), lax.axis_index("c")


def _other_chips(x, y):
    return [(1 - x, y), (x, 1 - y), (1 - x, 1 - y)]


def _chunks(rows, sublanes, most):
    for q in range(most, 0, -1):
        if rows % (q * sublanes) == 0:
            return q
    return 1


def _sublanes(dtype):
    return 8 * 4 // jnp.dtype(dtype).itemsize


def _gather_pieces(bufs):
    plan, first = [], []
    for b in bufs:
        Rh = b.shape[0] // 2
        Q = _chunks(Rh, _sublanes(b.dtype), 4) if Rh * b.shape[1] * b.dtype.itemsize > (1 << 20) else 1
        first.append(3 * sum(q for _, q, _ in plan))
        plan.append((Rh, Q, Rh // Q))
    return plan, first, 3 * sum(q for _, q, _ in plan)


def _allgather_chips(bufs, name):
    n = len(bufs)
    plan, first, n_sems = _gather_pieces(bufs)

    def body(*refs):
        x_refs, out_refs, (send_sems, recv_sems) = refs[:n], refs[n:2 * n], refs[2 * n:]
        x, y, c = _place()
        sibling = (x, y, 1 - c)
        chips = _other_chips(x, y)

        def copy(k, src, dst, to):
            return pltpu.make_async_remote_copy(src_ref=src, dst_ref=dst, send_sem=send_sems.at[k], recv_sem=recv_sems.at[k],
                                                device_id=to, device_id_type=MESH)

        def piece(i, cx, cy, hc, q):
            Rh, _, ch = plan[i]
            return out_refs[i].at[2 * cx + cy, pl.ds(hc * Rh + q * ch, ch), :]

        slots = [(i, q, j) for i in range(n) for q in range(plan[i][1]) for j in range(3)]
        sem = {(i, q, j): first[i] + 3 * q + j for i, q, j in slots}
        sent = [copy(sem[i, q, j], x_refs[i].at[pl.ds(c * plan[i][0] + q * plan[i][2], plan[i][2]), :], piece(i, x, y, c, q), (*chips[j], c))
                for i, q, j in slots]
        for cp in sent:
            cp.start()
        passed = []
        for i, q, j in slots:
            landed = piece(i, *chips[j], c, q)
            copy(sem[i, q, j], landed, landed, sibling).wait_recv()
            fwd = copy(n_sems + sem[i, q, j], landed, landed, sibling)
            fwd.start()
            passed.append(fwd)
        for i, q, j in slots:
            theirs = piece(i, *chips[j], 1 - c, q)
            copy(n_sems + sem[i, q, j], theirs, theirs, sibling).wait_recv()
        for cp in sent + passed:
            cp.wait_send()

    return pl.pallas_call(
        body, name=name, in_specs=[_HBM] * n, out_specs=[_HBM] * n,
        out_shape=[jax.ShapeDtypeStruct((N_CHIPS,) + b.shape, b.dtype) for b in bufs],
        scratch_shapes=[pltpu.SemaphoreType.DMA((2 * n_sems,)), pltpu.SemaphoreType.DMA((2 * n_sems,))],
    )(*bufs)


def _with_own(gathered, own):
    my = 2 * lax.axis_index("x") + lax.axis_index("y")
    return lax.dynamic_update_slice(gathered, own[None], (my, 0, 0))


def _sibling_gather(fs, name):
    n = len(fs)

    def body(*refs):
        out_refs, (send_sems, recv_sems) = refs[n:2 * n], refs[2 * n:]
        x, y, c = _place()
        copies = [pltpu.make_async_remote_copy(src_ref=out_ref.at[c], dst_ref=out_ref.at[c], send_sem=send_sems.at[i], recv_sem=recv_sems.at[i],
                                               device_id=(x, y, 1 - c), device_id_type=MESH) for i, out_ref in enumerate(out_refs)]
        for cp in copies:
            cp.start()
        for cp in copies:
            cp.wait()

    return pl.pallas_call(
        body, name=name, in_specs=[_HBM] * n, out_specs=[_HBM] * n,
        out_shape=[jax.ShapeDtypeStruct(f.shape, f.dtype) for f in fs], input_output_aliases={i: i for i in range(n)},
        scratch_shapes=[pltpu.SemaphoreType.DMA((n,)), pltpu.SemaphoreType.DMA((n,))],
    )(*fs)


def _axis_scalar(name):
    return lax.axis_index(name).astype(jnp.int32).reshape(1)


_SEM = pl.BlockSpec(memory_space=pltpu.SEMAPHORE)
_ANY = pl.BlockSpec(memory_space=pl.ANY)
_EFFECT = pltpu.SideEffectType.DATAFLOW_SIDE_EFFECTING


def _in_hbm(a):
    return pltpu.with_memory_space_constraint(a, pltpu.HBM)


def _half_copies(x_refs, land_refs, send_sems, recv_sems, arriving):
    x, y, c = _place()
    copies = []
    for i, (x_ref, land_ref) in enumerate(zip(x_refs, land_refs)):
        Rh = x_ref.shape[0] // 2
        rows = pl.ds(c * Rh, Rh)
        for j, (cx, cy) in enumerate(_other_chips(x, y)):
            copies.append(pltpu.make_async_remote_copy(
                src_ref=x_ref.at[rows, :], dst_ref=land_ref.at[2 * cx + cy if arriving else 2 * x + y, rows, :],
                send_sem=send_sems.at[3 * i + j], recv_sem=recv_sems.at[3 * i + j], device_id=(cx, cy, c), device_id_type=MESH))
    return copies


def _allgather_start(bufs, name):
    n = len(bufs)

    def body(*refs):
        x_refs, land_refs, (send_sems, recv_sems), token = refs[:n], refs[n:2 * n], refs[2 * n:2 * n + 2], refs[-1]
        for cp in _half_copies(x_refs, land_refs, send_sems, recv_sems, False):
            cp.start()
        token[...] = jnp.zeros_like(token)

    lands = [lax.empty((N_CHIPS,) + b.shape, b.dtype) for b in bufs]
    out = pl.pallas_call(
        body, name=name,
        out_shape=(pltpu.SemaphoreType.DMA((3 * n,)), pltpu.SemaphoreType.DMA((3 * n,)), *[pltpu.HBM(a.shape, a.dtype) for a in bufs + lands],
                   jax.ShapeDtypeStruct((8, 128), F32)),
        in_specs=[_HBM] * (2 * n), out_specs=(_SEM, _SEM, *[_HBM] * (2 * n), pl.BlockSpec(memory_space=pltpu.VMEM)),
        input_output_aliases={i: 2 + i for i in range(2 * n)}, compiler_params=pltpu.CompilerParams(has_side_effects=_EFFECT),
    )(*[_in_hbm(a) for a in bufs + lands])
    return (out[0], out[1], list(out[2:2 + n]), list(out[2 + n:2 + 2 * n])), out[-1][0, 0]


def _allgather_wait(pending, after, name):
    send_sems, recv_sems, bufs, lands = pending
    n = len(bufs)

    def body(*refs):
        x_refs, land_refs, send_sems, recv_sems = refs[:n], refs[n:2 * n], refs[2 * n], refs[2 * n + 1]
        for cp in _half_copies(x_refs, land_refs, send_sems, recv_sems, False):
            cp.wait_send()
        for cp in _half_copies(x_refs, land_refs, send_sems, recv_sems, True):
            cp.wait_recv()

    out = pl.pallas_call(
        body, name=name, out_shape=tuple(pltpu.HBM(a.shape, a.dtype) for a in bufs + lands),
        in_specs=[_HBM] * (2 * n) + [_SEM, _SEM, _ANY], out_specs=tuple([_HBM] * (2 * n)), input_output_aliases={i: i for i in range(2 * n)},
        compiler_params=pltpu.CompilerParams(has_side_effects=_EFFECT),
    )(*bufs, *lands, send_sems, recv_sems, after)
    return list(out[n:])


def _sibling_forward(lands, name):
    n = len(lands)
    plan, first, n_sems = _gather_pieces([jax.ShapeDtypeStruct(l.shape[1:], l.dtype) for l in lands])

    def body(*refs):
        out_refs, (send_sems, recv_sems) = refs[n:2 * n], refs[2 * n:]
        x, y, c = _place()

        def copies(hc):
            return [pltpu.make_async_remote_copy(
                        src_ref=out_refs[i].at[2 * cx + cy, pl.ds(hc * plan[i][0] + q * plan[i][2], plan[i][2]), :],
                        dst_ref=out_refs[i].at[2 * cx + cy, pl.ds(hc * plan[i][0] + q * plan[i][2], plan[i][2]), :],
                        send_sem=send_sems.at[first[i] + 3 * q + j], recv_sem=recv_sems.at[first[i] + 3 * q + j],
                        device_id=(x, y, 1 - c), device_id_type=MESH)
                    for i in range(n) for q in range(plan[i][1]) for j, (cx, cy) in enumerate(_other_chips(x, y))]

        mine = copies(c)
        for cp in mine:
            cp.start()
        for cp in mine:
            cp.wait_send()
        for cp in copies(1 - c):
            cp.wait_recv()

    return pl.pallas_call(
        body, name=name, in_specs=[_HBM] * n, out_specs=[_HBM] * n, out_shape=[jax.ShapeDtypeStruct(l.shape, l.dtype) for l in lands],
        input_output_aliases={i: i for i in range(n)},
        scratch_shapes=[pltpu.SemaphoreType.DMA((n_sems,)), pltpu.SemaphoreType.DMA((n_sems,))],
    )(*lands)


N_PEERS = 7


def _direct_copies(p_refs, t_refs, send_sems, recv_sems):
    x, y, c = _place()
    copies = []
    for i, (p_ref, t_ref) in enumerate(zip(p_refs, t_refs)):
        for f in range(1, N_PEERS + 1):
            px, py, pc = x ^ (f >> 2), y ^ ((f >> 1) & 1), c ^ (f & 1)
            copies.append(pltpu.make_async_remote_copy(
                src_ref=p_ref.at[2 * px + py, pc], dst_ref=t_ref.at[f - 1], send_sem=send_sems.at[N_PEERS * i + f - 1],
                recv_sem=recv_sems.at[N_PEERS * i + f - 1], device_id=(px, py, pc), device_id_type=MESH))
    return copies


def _direct_scatter_start(ps, name, carried=()):
    n, m = len(ps), 2 * len(ps) + len(carried)

    def body(*refs):
        p_refs, t_refs, (send_sems, recv_sems) = refs[:n], refs[n:2 * n], refs[m:m + 2]
        for cp in _direct_copies(p_refs, t_refs, send_sems, recv_sems):
            cp.start()

    lands = [lax.empty((N_PEERS,) + p.shape[2:], p.dtype) for p in ps]
    through = ps + lands + list(carried)
    out = pl.pallas_call(
        body, name=name,
        out_shape=(pltpu.SemaphoreType.DMA((N_PEERS * n,)), pltpu.SemaphoreType.DMA((N_PEERS * n,)),
                   *[pltpu.HBM(a.shape, a.dtype) for a in through]),
        in_specs=[_HBM] * m, out_specs=(_SEM, _SEM, *[_HBM] * m),
        input_output_aliases={i: 2 + i for i in range(m)}, compiler_params=pltpu.CompilerParams(has_side_effects=_EFFECT),
    )(*[_in_hbm(a) for a in through])
    return (out[0], out[1], list(out[2:2 + n]), list(out[2 + n:2 + 2 * n])), list(out[2 + 2 * n:])


def _direct_scatter_wait(pending, after, name):
    send_sems, recv_sems, ps, lands = pending
    n = len(ps)

    def body(*refs):
        p_refs, t_refs, send_sems, recv_sems = refs[:n], refs[n:2 * n], refs[2 * n], refs[2 * n + 1]
        for cp in _direct_copies(p_refs, t_refs, send_sems, recv_sems):
            cp.wait_send()
            cp.wait_recv()

    out = pl.pallas_call(
        body, name=name, out_shape=tuple(pltpu.HBM(a.shape, a.dtype) for a in ps + lands),
        in_specs=[_HBM] * (2 * n) + [_SEM, _SEM] + [_ANY] * len(after), out_specs=tuple([_HBM] * (2 * n)),
        input_output_aliases={i: i for i in range(2 * n)}, compiler_params=pltpu.CompilerParams(has_side_effects=_EFFECT),
    )(*ps, *lands, send_sems, recv_sems, *after)
    return list(out[:n]), list(out[n:])


def _sum_direct(p, t, name):
    _, _, R, C = p.shape
    tr = _pick(R, 512, 16)

    def body(x_ref, y_ref, c_ref, p_ref, t_ref, o_ref):
        acc = p_ref[...].astype(F32)
        for f in range(N_PEERS):
            acc = acc + t_ref[f].astype(F32)
        o_ref[...] = acc

    grid_spec = pltpu.PrefetchScalarGridSpec(
        num_scalar_prefetch=3, grid=(R // tr,),
        in_specs=[pl.BlockSpec((None, None, tr, C), lambda i, x_ref, y_ref, c_ref: (2 * x_ref[0] + y_ref[0], c_ref[0], i, 0)),
                  pl.BlockSpec((N_PEERS, tr, C), lambda i, x_ref, y_ref, c_ref: (0, i, 0))],
        out_specs=pl.BlockSpec((None, tr, C), lambda i, x_ref, y_ref, c_ref: (c_ref[0], i, 0)))
    return pl.pallas_call(body, name=name, grid_spec=grid_spec, out_shape=jax.ShapeDtypeStruct((2, R, C), F32),
                          compiler_params=pltpu.CompilerParams(dimension_semantics=("parallel",)))(
        _axis_scalar("x"), _axis_scalar("y"), _axis_scalar("c"), p, t)


def _adamw(w, g, m, v, name):
    R, C = w.shape
    tr = _pick(R, 256, 8)

    def body(w_ref, g_ref, m_ref, v_ref, d_ref, nm_ref, nv_ref):
        g_ = g_ref[...]
        m_ = ADAM_B1 * m_ref[...] + (1.0 - ADAM_B1) * g_
        v_ = ADAM_B2 * v_ref[...] + (1.0 - ADAM_B2) * (g_ * g_)
        m_hat = m_ / (1.0 - ADAM_B1 ** ADAM_STEP)
        v_hat = v_ / (1.0 - ADAM_B2 ** ADAM_STEP)
        d_ref[...] = -ADAM_LR * (m_hat / (jnp.sqrt(v_hat) + ADAM_EPS) + ADAM_WD * w_ref[...])
        nm_ref[...] = m_
        nv_ref[...] = v_

    row = pl.BlockSpec((tr, C), lambda i: (i, 0))
    shp = jax.ShapeDtypeStruct((R, C), F32)
    return pl.pallas_call(body, name=name, grid=(R // tr,), in_specs=[row] * 4, out_specs=[row] * 3, out_shape=[shp] * 3,
                          compiler_params=pltpu.CompilerParams(dimension_semantics=("parallel",)))(w, g, m, v)


BIG_SPECS = (("ev_w_in", 1024, 1440, 1), ("ev_w_uq", 256, 768, 1), ("ev_w_ukv", 128, 1024, 1), ("ev_w_out", 1024, 1024, 0),
             ("od_w_in", 1024, 6144, 1), ("od_w_out", 2048, 1024, 0), ("mlp_w1_0", 1024, 4096, 1), ("mlp_w1_1", 1024, 4096, 1),
             ("mlp_w2_0", 4096, 1024, 0), ("mlp_w2_1", 4096, 1024, 0))
BIG_PARAMS = (("ev_w_in", ("ev_w_in",)), ("ev_w_uq", ("ev_w_uq",)), ("ev_w_ukv", ("ev_w_ukv",)), ("ev_w_out", ("ev_w_out",)),
              ("od_w_in", ("od_w_in",)), ("od_w_out", ("od_w_out",)), ("mlp_w1", ("mlp_w1_0", "mlp_w1_1")),
              ("mlp_w2", ("mlp_w2_0", "mlp_w2_1")))
REPLICATED = ("ev_conv_b", "ev_w_rg_a", "ev_b_rg_a", "ev_w_rg_x", "ev_b_rg_x", "ev_lru_lambda", "ev_q_norm_g", "ev_kv_norm_g",
              "ln_mix_g", "ln_mix_b", "ln_mlp_g", "ln_mlp_b")
SMALL_SHARDED = ("meta_tokens", "ev_conv_w")
COL_SHARD_GRADS = ("od_w_in", "mlp_w1_0", "mlp_w1_1")
MATRIX_GROUPS = (("ev_w_in", "ev_w_uq", "ev_w_ukv", "ev_w_out"), ("mlp_w1_0", "mlp_w2_0"), ("od_w_in", "od_w_out", "mlp_w1_1", "mlp_w2_1"))
LAYER_NORMS = ("ln_mix_g", "ln_mix_b", "ln_mlp_g", "ln_mlp_b")
WEIGHT_NAMES = ("meta_tokens", "ev_w_in", "ev_conv_w", "ev_conv_b", "ev_w_rg_a", "ev_b_rg_a", "ev_w_rg_x", "ev_b_rg_x",
                "ev_lru_lambda", "ev_q_norm_g", "ev_w_uq", "ev_kv_norm_g", "ev_w_ukv", "ev_w_out", "od_w_in", "od_w_out",
                "ln_mix_g", "ln_mix_b", "mlp_w1", "mlp_w2", "ln_mlp_g", "ln_mlp_b")


def _to_rows(flat, row_align):
    n = flat.shape[-1]
    rows = _round_up(-(-n // PACK_COLS), row_align)
    pad = rows * PACK_COLS - n
    if pad:
        flat = jnp.pad(flat, [(0, 0)] * (flat.ndim - 1) + [(0, pad)])
    return flat.reshape(flat.shape[:-1] + (rows, PACK_COLS))


def _shard_shape(K, N, axis):
    return (K // N_CHIPS, N) if axis == 0 else (K, N // N_CHIPS)


def _gather_shards(stacked, K, N, axis):
    if axis == 0:
        return stacked.reshape(K, N)
    return stacked.transpose(1, 0, 2).reshape(K, N)


def _split_shards(full, K, N, axis):
    if axis == 0:
        return full.reshape(N_CHIPS, -1)
    return full.reshape(K, N_CHIPS, N // N_CHIPS).transpose(1, 0, 2).reshape(N_CHIPS, -1)


def kernel(x, meta_tokens, ev_w_in, ev_conv_w, ev_conv_b, ev_w_rg_a, ev_b_rg_a, ev_w_rg_x, ev_b_rg_x, ev_lru_lambda, ev_q_norm_g, ev_w_uq, ev_kv_norm_g, ev_w_ukv, ev_w_out, od_w_in, od_w_out, ln_mix_g, ln_mix_b, mlp_w1, mlp_w2, ln_mlp_g, ln_mlp_b, loss_target, m_meta_tokens, m_ev_w_in, m_ev_conv_w, m_ev_conv_b, m_ev_w_rg_a, m_ev_b_rg_a, m_ev_w_rg_x, m_ev_b_rg_x, m_ev_lru_lambda, m_ev_q_norm_g, m_ev_w_uq, m_ev_kv_norm_g, m_ev_w_ukv, m_ev_w_out, m_od_w_in, m_od_w_out, m_ln_mix_g, m_ln_mix_b, m_mlp_w1, m_mlp_w2, m_ln_mlp_g, m_ln_mlp_b, v_meta_tokens, v_ev_w_in, v_ev_conv_w, v_ev_conv_b, v_ev_w_rg_a, v_ev_b_rg_a, v_ev_w_rg_x, v_ev_b_rg_x, v_ev_lru_lambda, v_ev_q_norm_g, v_ev_w_uq, v_ev_kv_norm_g, v_ev_w_ukv, v_ev_w_out, v_od_w_in, v_od_w_out, v_ln_mix_g, v_ln_mix_b, v_mlp_w1, v_mlp_w2, v_ln_mlp_g, v_ln_mlp_b):
    given = dict(locals())
    local_big = {"ev_w_in": ev_w_in[0], "ev_w_uq": ev_w_uq[0], "ev_w_ukv": ev_w_ukv[0], "ev_w_out": ev_w_out[0],
                 "od_w_in": od_w_in[0], "od_w_out": od_w_out[0], "mlp_w1_0": mlp_w1[0], "mlp_w1_1": mlp_w1[1],
                 "mlp_w2_0": mlp_w2[0], "mlp_w2_1": mlp_w2[1]}

    specs = {spec[0]: spec for spec in BIG_SPECS}
    mixer0_m, mlp0_m, layer1_m = MATRIX_GROUPS

    def shards(names):
        return [local_big[n].astype(MXU_DTYPE) for n in names]

    def whole(stacked, n):
        _, K, N, ax = specs[n]
        return stacked if n in COL_SHARD_GRADS else _gather_shards(stacked, K, N, ax)

    def filled(gathered, own, names):
        return {n: whole(_with_own(g_, o_), n) for n, g_, o_ in zip(names, gathered, own)}

    own_a, own_b, own_c = shards(mixer0_m), shards(mlp0_m), shards(layer1_m)
    small = [meta_tokens, jnp.pad(ev_conv_w[0], ((0, 16 - CONV_WIDTH), (0, 0)))]
    gathered_a = _allgather_chips(own_a + small, "weight_allgather_mixer0")
    pending_b, token1 = _allgather_start(own_b, "weight_allgather_mlp0_start")
    pending_c, token2 = _allgather_start(own_c, "weight_allgather_layer1_start")
    meta_full = _gather_shards(_with_own(gathered_a[-2], small[0]), N_META, D_MODEL, 1)
    conv_full = _gather_shards(_with_own(gathered_a[-1], small[1])[:, :CONV_WIDTH], CONV_WIDTH, LRU_WIDTH, 1)

    def slots(names, dtype):
        return {n: jnp.zeros((N_CHIPS, specs[n][1], specs[n][2] // N_CHIPS) if n in COL_SHARD_GRADS else specs[n][1:3], dtype) for n in names}

    def norms(names, layer):
        return {n: given[n][layer] for n in names}

    def finish_gather(pending, own, after, names, tag):
        landed = _allgather_wait(pending, lax.stop_gradient(after), "weight_allgather_%s_wait" % tag)
        return filled(_sibling_forward(landed, "weight_allgather_%s_forward" % tag), own, names)

    diff_a = {**slots(mixer0_m, MXU_DTYPE), **norms(("ln_mix_g", "ln_mix_b"), 0), **{n: given[n] for n in REPLICATED if n not in LAYER_NORMS},
              "x": x, "meta_tokens": meta_full, "ev_conv_w": conv_full}
    diff_b = {**slots(mlp0_m, MXU_DTYPE), **norms(("ln_mlp_g", "ln_mlp_b"), 0)}
    diff_c = {**slots(layer1_m, MXU_DTYPE), **norms(LAYER_NORMS, 1)}
    w_a = filled(gathered_a[:len(mixer0_m)], own_a, mixer0_m)
    h_a, back_a = jax.vjp(lambda d: _mixer0(d, w_a, token1 + token2), diff_a)
    w_b = finish_gather(pending_b, own_b, h_a, mlp0_m, "mlp0")
    h_b, back_b = jax.vjp(lambda d, hh: _mlp0(d, hh, w_b), diff_b, h_a)
    w_c = finish_gather(pending_c, own_c, h_b, layer1_m, "layer1")
    loss, back_c = jax.vjp(lambda d, hh: _layer1_loss(d, hh, w_c, loss_target), diff_c, h_b)
    loss = lax.psum(loss, ("x", "y", "c"))

    def blocks_of(grad, n):
        _, K, N, ax = specs[n]
        if n in COL_SHARD_GRADS:
            blocks = grad
        elif ax == 0:
            blocks = grad.reshape(N_CHIPS, K // N_CHIPS, N)
        else:
            blocks = grad.reshape(K, N_CHIPS, N // N_CHIPS).transpose(1, 0, 2)
        return blocks.reshape(N_CHIPS, 2, blocks.shape[1] // 2, blocks.shape[2])

    def start_reduce(grads_of, names, tag, dh):
        flying, (dh,) = _direct_scatter_start([blocks_of(grads_of[n], n) for n in names], "grad_scatter_%s_start" % tag, [dh])
        return flying, dh

    g_c, dh = back_c(jnp.ones((), F32))
    flying_c, dh = start_reduce(g_c, layer1_m, "layer1", dh)
    g_b, dh = back_b(dh)
    flying_b, dh = start_reduce(g_b, mlp0_m, "mlp0", dh)
    (g_a,) = back_a(dh)

    g = {**g_a, **g_b, **g_c}
    g.update({n: jnp.stack([(g_b if n in g_b else g_a)[n], g_c[n]]) for n in LAYER_NORMS})
    repl = jnp.concatenate([g[n].reshape(-1) for n in REPLICATED]).reshape(N_CHIPS, -1)
    small = [_split_shards(g["meta_tokens"], N_META, D_MODEL, 1), _split_shards(g["ev_conv_w"], CONV_WIDTH, LRU_WIDTH, 1), repl]
    small = [pc.reshape(N_CHIPS, 2, -1) for pc in small]
    n_small = sum(pc.shape[2] for pc in small)
    small.append(jnp.zeros((N_CHIPS, 2, _round_up(n_small, 32 * PACK_COLS) - n_small), F32))
    p_small = jnp.concatenate(small, axis=2).reshape(N_CHIPS, 2, -1, PACK_COLS)
    flying_a, _ = _direct_scatter_start([blocks_of(g_a[n], n) for n in mixer0_m] + [p_small], "grad_scatter_mixer0_start")
    started = [g_a["x"], flying_a[2][0]]
    ps_c, ts_c = _direct_scatter_wait(flying_c, started, "grad_scatter_layer1_wait")
    ps_b, ts_b = _direct_scatter_wait(flying_b, started, "grad_scatter_mlp0_wait")
    fs_bc = [_sum_direct(p, t, "grad_sum_%d" % i) for i, (p, t) in enumerate(zip(ps_b + ps_c, ts_b + ts_c))]
    red_big = dict(zip(mlp0_m + layer1_m, _sibling_gather(fs_bc, "grad_sibling_gather")))

    grads, delta, new_m, new_v = {}, {}, {}, {}

    def update_big(names):
        done = []
        for name, parts in BIG_PARAMS:
            if parts[0] in names:
                shp = given[name].shape
                two_d = (-1, shp[-1])
                grads[name] = jnp.stack([red_big[part].reshape(shp[1:]) for part in parts])
                d, nm, nv = _adamw(given[name].reshape(two_d), grads[name].reshape(two_d), given["m_" + name].reshape(two_d),
                                   given["v_" + name].reshape(two_d), "adamw_" + name)
                delta[name], new_m[name], new_v[name] = d.reshape(shp), nm.reshape(shp), nv.reshape(shp)
                done.append(nv)
        return done

    updated = update_big(mlp0_m + layer1_m)
    ps_a, ts_a = _direct_scatter_wait(flying_a, updated, "grad_scatter_mixer0_wait")
    fs_a = [_sum_direct(p, t, "grad_sum_mixer0_%d" % i) for i, (p, t) in enumerate(zip(ps_a, ts_a))]
    reduced_a = _sibling_gather(fs_a, "grad_sibling_gather_mixer0")
    red_big.update(zip(mixer0_m, reduced_a))
    red_small = reduced_a[-1].reshape(2, -1)
    update_big(mixer0_m)

    def take(off, sz):
        return jnp.concatenate([red_small[0, off // 2:(off + sz) // 2], red_small[1, off // 2:(off + sz) // 2]])

    off = 0
    for name in SMALL_SHARDED:
        sz = given[name].size
        grads[name] = take(off, sz).reshape(given[name].shape)
        off += sz
    n_repl = repl.shape[1]
    own_repl = _to_rows(take(off, n_repl), 16)
    repl_all = _with_own(_allgather_chips([own_repl], "replicated_allgather")[0], own_repl).reshape(N_CHIPS, -1)[:, :n_repl].reshape(-1)
    off = 0
    for name in REPLICATED:
        sz = given[name].size
        grads[name] = repl_all[off:off + sz].reshape(given[name].shape)
        off += sz

    smalls = SMALL_SHARDED + REPLICATED

    def pack_small(get):
        return _to_rows(jnp.concatenate([get(n).reshape(-1) for n in smalls]), 8)

    outs = _adamw(pack_small(lambda n: given[n]), pack_small(lambda n: grads[n]), pack_small(lambda n: given["m_" + n]),
                  pack_small(lambda n: given["v_" + n]), "adamw_small")
    for res, flat in zip((delta, new_m, new_v), outs):
        flat, off = flat.reshape(-1), 0
        for n in smalls:
            sz = given[n].size
            res[n] = flat[off:off + sz].reshape(given[n].shape)
            off += sz

    return (loss, g_a["x"], *[grads[n] for n in WEIGHT_NAMES], *[delta[n] for n in WEIGHT_NAMES],
            *[new_m[n] for n in WEIGHT_NAMES], *[new_v[n] for n in WEIGHT_NAMES])
```

```python
import functools
import math

import jax
import jax.numpy as jnp
import numpy as np
from jax import lax
from jax.experimental import pallas as pl
from jax.experimental.pallas import tpu as pltpu

F32 = jnp.float32
MXU_DTYPE = jnp.bfloat16

D_MODEL = 1024
N_META = 16
LRU_WIDTH = 512
LRU_HEADS = 4
LRU_HEAD_DIM = 128
CONV_WIDTH = 4
LRU_C = 8.0
MLA_HEADS = 8
MLA_NOPE = 64
MLA_ROPE = 32
MLA_V = 64
MLA_Q_RANK = 256
MLA_KV_RANK = 128
RET_HEADS = 4
RET_QK_DIM = 256
RET_V_DIM = 512
D_FF = 4096
ROPE_BASE = 10000.0
DN_ALPHA = 4.0 ** 0.25
EPS = 1e-5
NEG_INF = -1e30
SEQ_BLOCK = 128

ADAM_LR = 0.001
ADAM_B1 = 0.9
ADAM_B2 = 0.999
ADAM_EPS = 1e-08
ADAM_WD = 0.01
ADAM_STEP = 10

PACK_COLS = 1024
TN_INPUT_VMEM_BYTES = 28 << 20
N_CHIPS = 4

MESH = pl.DeviceIdType.MESH


def _pick(n, target, align):
    best = None
    for t in range(align, min(n, target) + 1, align):
        if n % t == 0:
            best = t
    return n if best is None else best


def _round_up(n, m):
    return (n + m - 1) // m * m


def _relu2(a):
    r = jnp.maximum(a, 0.0)
    return r * r


def _ln_stats(z):
    mu = jnp.mean(z, axis=-1, keepdims=True)
    zc = z - mu
    var = jnp.mean(zc * zc, axis=-1, keepdims=True)
    return zc, lax.rsqrt(var + EPS)


def _mm_nn(a, w, act, name, out_dtype=F32, norm=None):
    M, K = a.shape
    sharded = w.ndim == 3
    n = w.shape[-1]
    N = n * (w.shape[0] if sharded else 1)
    tm = _pick(M, 1088 if K * a.dtype.itemsize <= 4096 and norm is None else 544, 8)
    tn = _pick(n, 1024, 128)
    per = n // tn
    assert norm is None or tn == N

    def body(a_ref, w_ref, *rest):
        av = a_ref[...]
        if act:
            av = _relu2(av.astype(F32))
        r = jnp.dot(av.astype(MXU_DTYPE), w_ref[...].astype(MXU_DTYPE), preferred_element_type=F32)
        if norm is None:
            rest[0][...] = r.astype(out_dtype)
        else:
            r_ref, g_ref, b_ref, o_ref, z_ref = rest
            z = DN_ALPHA * r_ref[...] + r
            zc, rstd = _ln_stats(z)
            z_ref[...] = z
            o_ref[...] = zc * rstd * g_ref[...] + b_ref[...]

    w_spec = pl.BlockSpec((None, K, tn), lambda i, j: (j // per, 0, j % per)) if sharded else pl.BlockSpec((K, tn), lambda i, j: (0, j))
    tile = pl.BlockSpec((tm, tn), lambda i, j: (i, j))
    in_specs, args = [pl.BlockSpec((tm, K), lambda i, j: (i, 0)), w_spec], [a, w]
    if norm is None:
        out_specs, out_shape = tile, jax.ShapeDtypeStruct((M, N), out_dtype)
    else:
        vec = pl.BlockSpec((1, N), lambda i, j: (0, 0))
        in_specs += [tile, vec, vec]
        args += [norm[0], norm[1].reshape(1, N), norm[2].reshape(1, N)]
        out_specs, out_shape = [tile, tile], [jax.ShapeDtypeStruct((M, N), F32)] * 2
    return pl.pallas_call(
        body, name=name, grid=(M // tm, N // tn), in_specs=in_specs, out_specs=out_specs, out_shape=out_shape,
        compiler_params=pltpu.CompilerParams(dimension_semantics=("parallel", "arbitrary")),
    )(*args)


def _mm_nt(g, w, a_src, name, out_dtype=F32, plus=None):
    M, N = g.shape
    sharded = w.ndim == 3
    K, n = w.shape[-2], w.shape[-1]
    if sharded:
        tk, nk = N, 1
    else:
        tk = N if N * g.dtype.itemsize <= 8192 else _pick(N, 2048, 128)
        nk = N // tk
    tm = _pick(M, 1088 if tk * g.dtype.itemsize <= 4096 else 544, 8)
    tn = _pick(K, 1024, 128)
    has_src = a_src is not None
    assert nk == 1 or out_dtype == F32
    assert plus is None or not has_src

    def body(*refs):
        if has_src:
            g_ref, w_ref, s_ref, o_ref = refs
        elif plus is not None:
            g_ref, w_ref, p_ref, o_ref = refs
        else:
            g_ref, w_ref, o_ref = refs
        nt = (((1,), (1,)), ((), ()))
        if sharded:
            r = sum(lax.dot_general(g_ref[:, s * n:(s + 1) * n].astype(MXU_DTYPE), w_ref[s].astype(MXU_DTYPE), nt, preferred_element_type=F32)
                    for s in range(w_ref.shape[0]))
        else:
            r = lax.dot_general(g_ref[...].astype(MXU_DTYPE), w_ref[...].astype(MXU_DTYPE), nt, preferred_element_type=F32)
        if has_src:
            r = r * (2.0 * jnp.maximum(s_ref[...].astype(F32), 0.0))
        first = r if plus is None else r + DN_ALPHA * p_ref[...]
        if nk == 1:
            o_ref[...] = first.astype(out_dtype)
        else:
            k = pl.program_id(2)

            @pl.when(k == 0)
            def _():
                o_ref[...] = first

            @pl.when(k > 0)
            def _():
                o_ref[...] += r

    w_spec = (pl.BlockSpec((w.shape[0], tn, n), lambda i, j, k: (0, j, 0)) if sharded
              else pl.BlockSpec((tn, tk), lambda i, j, k: (j, k)))
    in_specs = [pl.BlockSpec((tm, tk), lambda i, j, k: (i, k)), w_spec]
    args = [g, w]
    if has_src:
        assert nk == 1
        in_specs.append(pl.BlockSpec((tm, tn), lambda i, j, k: (i, j)))
        args.append(a_src)
    if plus is not None:
        in_specs.append(pl.BlockSpec((tm, tn), lambda i, j, k: (i, j)))
        args.append(plus)
    return pl.pallas_call(
        body, name=name,
        grid=(M // tm, K // tn, nk),
        in_specs=in_specs,
        out_specs=pl.BlockSpec((tm, tn), lambda i, j, k: (i, j)),
        out_shape=jax.ShapeDtypeStruct((M, K), out_dtype),
        compiler_params=pltpu.CompilerParams(dimension_semantics=("parallel", "parallel", "arbitrary")),
    )(*args)


def _mm_tn(a, g, act, name, col_shards=1, out_dtype=F32):
    M, K = a.shape
    _, N = g.shape
    n = N // col_shards
    tm, tn = _pick(K, 1024, 128), _pick(n, 1024, 128)
    row_bytes = tm * a.dtype.itemsize + tn * g.dtype.itemsize
    tk = _pick(M, min(2176, TN_INPUT_VMEM_BYTES // (2 * row_bytes)), 8)
    nk = M // tk
    per = n // tn
    direct = out_dtype == F32

    def body(a_ref, g_ref, o_ref, *scratch):
        acc_ref = o_ref if direct else scratch[0]
        k = pl.program_id(2)
        av = a_ref[...]
        if act:
            av = _relu2(av.astype(F32))
        r = lax.dot_general(av.astype(MXU_DTYPE), g_ref[...].astype(MXU_DTYPE),
                            (((0,), (0,)), ((), ())), preferred_element_type=F32)

        @pl.when(k == 0)
        def _():
            acc_ref[...] = r

        @pl.when(k > 0)
        def _():
            acc_ref[...] += r

        if not direct:
            @pl.when(k == nk - 1)
            def _():
                o_ref[...] = acc_ref[...].astype(out_dtype)

    if col_shards == 1:
        out_spec, out_shape = pl.BlockSpec((tm, tn), lambda i, j, k: (i, j)), (K, N)
    else:
        out_spec, out_shape = pl.BlockSpec((None, tm, tn), lambda i, j, k: (j // per, i, j % per)), (col_shards, K, n)
    return pl.pallas_call(
        body, name=name,
        grid=(K // tm, N // tn, nk),
        in_specs=[pl.BlockSpec((tk, tm), lambda i, j, k: (k, i)), pl.BlockSpec((tk, tn), lambda i, j, k: (k, j))],
        out_specs=out_spec,
        out_shape=jax.ShapeDtypeStruct(out_shape, out_dtype),
        scratch_shapes=[] if direct else [pltpu.VMEM((tm, tn), F32)],
        compiler_params=pltpu.CompilerParams(dimension_semantics=("parallel", "parallel", "arbitrary")),
    )(a, g)


@functools.partial(jax.custom_vjp, nondiff_argnums=(3, 4, 5, 6))
def matmul(a, w, w_grad_slot, act, name, out_dtype, col_shards):
    return _mm_nn(a, w, act, name + "_fwd", out_dtype)


def _matmul_fwd(a, w, w_grad_slot, act, name, out_dtype, col_shards):
    return _mm_nn(a, w, act, name + "_fwd", out_dtype), (a, w, jnp.zeros((), w_grad_slot.dtype))


def _matmul_bwd(act, name, out_dtype, col_shards, res, g):
    a, w, slot_like = res
    w_grad_dtype = slot_like.dtype
    da = _mm_nt(g, w, a if act else None, name + "_dx")
    dw = _mm_tn(a, g, act, name + "_dw", col_shards, w_grad_dtype)
    return da, None, dw


matmul.defvjp(_matmul_fwd, _matmul_bwd)


def _ln_bwd_call(z, g, dy, name):
    M, D = z.shape
    tm = _pick(M, 544, 8)

    def body(z_ref, g_ref, dy_ref, dz_ref, dg_ref, db_ref):
        @pl.when(pl.program_id(0) == 0)
        def _():
            dg_ref[...] = jnp.zeros_like(dg_ref)
            db_ref[...] = jnp.zeros_like(db_ref)

        zc, rstd = _ln_stats(z_ref[...])
        xhat = zc * rstd
        dy = dy_ref[...]
        dxh = dy * g_ref[...]
        m1 = jnp.mean(dxh, axis=-1, keepdims=True)
        m2 = jnp.mean(dxh * xhat, axis=-1, keepdims=True)
        dz_ref[...] = rstd * (dxh - m1 - xhat * m2)
        dg_ref[...] += jnp.sum(dy * xhat, axis=0, keepdims=True)
        db_ref[...] += jnp.sum(dy, axis=0, keepdims=True)

    row = pl.BlockSpec((tm, D), lambda i: (i, 0))
    vec = pl.BlockSpec((1, D), lambda i: (0, 0))
    return pl.pallas_call(
        body, name=name, grid=(M // tm,), in_specs=[row, vec, row], out_specs=[row, vec, vec],
        out_shape=[jax.ShapeDtypeStruct((M, D), F32), jax.ShapeDtypeStruct((1, D), F32), jax.ShapeDtypeStruct((1, D), F32)],
        compiler_params=pltpu.CompilerParams(dimension_semantics=("arbitrary",)),
    )(z, g.reshape(1, D), dy)


@functools.partial(jax.custom_vjp, nondiff_argnums=(7,))
def mlp_block(h, w1, w2, w1_grad_slot, w2_grad_slot, g, b, name):
    return _mlp_block_fwd(h, w1, w2, w1_grad_slot, w2_grad_slot, g, b, name)[0]


def _mlp_block_fwd(h, w1, w2, w1_grad_slot, w2_grad_slot, g, b, name):
    u = _mm_nn(h, w1, False, name + "_w1_fwd", out_dtype=MXU_DTYPE)
    out, z = _mm_nn(u, w2, True, name + "_w2_norm_fwd", norm=(h, g, b))
    return out, (h, u, z, w1, w2, g, jnp.zeros((), w1_grad_slot.dtype))


def _mlp_block_bwd(name, res, dy):
    h, u, z, w1, w2, g, slot_like = res
    dz, dg, db = _ln_bwd_call(z, g, dy, name + "_norm_bwd")
    du = _mm_nt(dz, w2, u, name + "_w2_dx", out_dtype=MXU_DTYPE)
    dw2 = _mm_tn(u, dz, True, name + "_w2_dw", 1, slot_like.dtype)
    dh = _mm_nt(du, w1, None, name + "_w1_dx", plus=dz)
    dw1 = _mm_tn(h, du, False, name + "_w1_dw", N_CHIPS, slot_like.dtype)
    return dh, None, None, dw1, dw2, dg.reshape(g.shape), db.reshape(g.shape)


mlp_block.defvjp(_mlp_block_fwd, _mlp_block_bwd)


@functools.partial(jax.custom_vjp, nondiff_argnums=(6,))
def out_block(h, y, w, w_grad_slot, g, b, name):
    return _out_block_fwd(h, y, w, w_grad_slot, g, b, name)[0]


def _out_block_fwd(h, y, w, w_grad_slot, g, b, name):
    out, z = _mm_nn(y, w, False, name + "_norm_fwd", norm=(h, g, b))
    return out, (y, z, w, g, jnp.zeros((), w_grad_slot.dtype))


def _out_block_bwd(name, res, dy):
    y, z, w, g, slot_like = res
    dz, dg, db = _ln_bwd_call(z, g, dy, name + "_norm_bwd")
    d_y = _mm_nt(dz, w, None, name + "_dx")
    dw = _mm_tn(y, dz, False, name + "_dw", 1, slot_like.dtype)
    return DN_ALPHA * dz, d_y, None, dw, dg.reshape(g.shape), db.reshape(g.shape)


out_block.defvjp(_out_block_fwd, _out_block_bwd)


def _rms_fwd_call(x, g, name, col_block=0):
    R = x.shape[0]
    W = g.shape[-1]
    tr = _pick(R, 1088, 8)

    def body(x_ref, g_ref, o_ref):
        xv = x_ref[...]
        rstd = lax.rsqrt(jnp.mean(xv * xv, axis=-1, keepdims=True) + EPS)
        o_ref[...] = xv * rstd * g_ref[...]

    vec = pl.BlockSpec((1, W), lambda i: (0, 0))
    return pl.pallas_call(
        body, name=name, grid=(R // tr,), in_specs=[pl.BlockSpec((tr, W), lambda i: (i, col_block)), vec],
        out_specs=pl.BlockSpec((tr, W), lambda i: (i, 0)), out_shape=jax.ShapeDtypeStruct((R, W), F32),
        compiler_params=pltpu.CompilerParams(dimension_semantics=("parallel",)),
    )(x, g.reshape(1, W))


def _rms_bwd_call(x, g, dy, name, col_block=0):
    R = x.shape[0]
    W = g.shape[-1]
    tr = _pick(R, 1088, 8)

    def body(x_ref, g_ref, dy_ref, dx_ref, dg_ref):
        @pl.when(pl.program_id(0) == 0)
        def _():
            dg_ref[...] = jnp.zeros_like(dg_ref)

        xv = x_ref[...]
        rstd = lax.rsqrt(jnp.mean(xv * xv, axis=-1, keepdims=True) + EPS)
        xhat = xv * rstd
        dy = dy_ref[...]
        dxh = dy * g_ref[...]
        dx_ref[...] = rstd * (dxh - xhat * jnp.mean(dxh * xhat, axis=-1, keepdims=True))
        dg_ref[...] += jnp.sum(dy * xhat, axis=0, keepdims=True)

    row = pl.BlockSpec((tr, W), lambda i: (i, 0))
    vec = pl.BlockSpec((1, W), lambda i: (0, 0))
    return pl.pallas_call(
        body, name=name, grid=(R // tr,), in_specs=[pl.BlockSpec((tr, W), lambda i: (i, col_block)), vec, row], out_specs=[row, vec],
        out_shape=[jax.ShapeDtypeStruct((R, W), F32), jax.ShapeDtypeStruct((1, W), F32)],
        compiler_params=pltpu.CompilerParams(dimension_semantics=("arbitrary",)),
    )(x, g.reshape(1, W), dy)


def _loss_call(h, tgt, n_tokens, name):
    B, Tp, D = h.shape
    tr = _pick(Tp, 544, 8)

    def body(y_ref, t_ref, dy_ref, acc_ref):
        @pl.when(jnp.logical_and(pl.program_id(0) == 0, pl.program_id(1) == 0))
        def _():
            acc_ref[...] = jnp.zeros_like(acc_ref)

        t = lax.broadcasted_iota(jnp.int32, (tr, 1), 0) + pl.program_id(1) * tr
        counts = jnp.logical_and(t >= N_META, t < N_META + n_tokens)
        e = jnp.where(counts, y_ref[...] - t_ref[...], 0.0)
        dy_ref[...] = e * (1.0 / D)
        acc_ref[...] += jnp.sum(jnp.sum(e * e, axis=-1, keepdims=True), axis=0, keepdims=True) * (0.5 / D)

    row = pl.BlockSpec((None, tr, D), lambda b, i: (b, i, 0))
    one = pl.BlockSpec((1, 1), lambda b, i: (0, 0))
    return pl.pallas_call(
        body, name=name, grid=(B, Tp // tr), in_specs=[row, row], out_specs=[row, one],
        out_shape=[jax.ShapeDtypeStruct((B, Tp, D), F32), jax.ShapeDtypeStruct((1, 1), F32)],
        compiler_params=pltpu.CompilerParams(dimension_semantics=("arbitrary", "arbitrary")),
    )(h, tgt)


@functools.partial(jax.custom_vjp, nondiff_argnums=(2,))
def loss_head(h, tgt, n_tokens):
    return _loss_call(h, tgt, n_tokens, "loss_head")[1][0, 0]


def _loss_head_fwd(h, tgt, n_tokens):
    dy, acc = _loss_call(h, tgt, n_tokens, "loss_head")
    return acc[0, 0], dy


def _loss_head_bwd(n_tokens, dy, ct):
    return ct * dy, None


loss_head.defvjp(_loss_head_fwd, _loss_head_bwd)


_GELU_C = math.sqrt(2.0 / math.pi)


def _gelu_parts(x):
    x2 = x * x
    t = jnp.tanh(_GELU_C * (x + 0.044715 * x * x2))
    gelu = 0.5 * x * (1.0 + t)
    dgelu = 0.5 * (1.0 + t) + 0.5 * x * (1.0 - t * t) * (_GELU_C * (1.0 + 3.0 * 0.044715 * x2))
    return gelu, dgelu


def _sigmoid(x):
    return 1.0 / (1.0 + jnp.exp(-x))


def _scan8(a, b, carry, reverse):
    row = lax.broadcasted_iota(jnp.int32, a.shape, 0)
    for s in (1, 2, 4):
        shift = 8 - s if reverse else s
        keep = (row < 8 - s) if reverse else (row >= s)
        b = jnp.where(keep, a * pltpu.roll(b, shift, 0) + b, b)
        a = jnp.where(keep, a * pltpu.roll(a, shift, 0), a)
    return a * carry + b


def _lru_pre(prec_ref, prev_ref, first, cw_ref, cb_ref, wa_ref, ba_ref, wx_ref, bx_ref, sp_ref):
    tc = prec_ref.shape[0]
    prev = jnp.where(first, 0.0, prev_ref[...])
    ext = jnp.concatenate([prev, prec_ref[...]], axis=0)
    cw = cw_ref[...]
    taps = [ext[8:] if k == CONV_WIDTH - 1 else pltpu.roll(ext, CONV_WIDTH - 1 - k, 0)[8:] for k in range(CONV_WIDTH)]
    xc = cb_ref[...] + sum(cw[k:k + 1, :] * taps[k] for k in range(CONV_WIDTH))
    ga, gx = [], []
    for h in range(LRU_HEADS):
        xh = xc[:, h * LRU_HEAD_DIM:(h + 1) * LRU_HEAD_DIM].astype(MXU_DTYPE)
        ga.append(jnp.dot(xh, wa_ref[h].astype(MXU_DTYPE), preferred_element_type=F32))
        gx.append(jnp.dot(xh, wx_ref[h].astype(MXU_DTYPE), preferred_element_type=F32))
    r = _sigmoid(jnp.concatenate(ga, axis=1) + ba_ref[...])
    i = _sigmoid(jnp.concatenate(gx, axis=1) + bx_ref[...])
    log_a = -LRU_C * r * sp_ref[...]
    a = jnp.exp(log_a)
    a2 = a * a
    mult = jnp.sqrt(-jnp.tanh(log_a) * (a2 + 1.0))
    return taps, xc, r, i, a, a2, mult


def _lru_fwd_call(p, cw, cb, wa, ba, wx, bx, sp):
    B, Tp, _ = p.shape
    W = LRU_WIDTH
    tc = SEQ_BLOCK
    nc = Tp // tc

    def body(pg_ref, prec_ref, prev_ref, cw_ref, cb_ref, wa_ref, ba_ref, wx_ref, bx_ref, sp_ref, y_ref, h_ref, carry_ref):
        first = pl.program_id(1) == 0

        @pl.when(first)
        def _():
            carry_ref[...] = jnp.zeros_like(carry_ref)

        _, xc, r, i, a, a2, mult = _lru_pre(prec_ref, prev_ref, first, cw_ref, cb_ref, wa_ref, ba_ref, wx_ref, bx_ref, sp_ref)
        b = mult * (i * xc)
        carry = carry_ref[0:1, :]
        for t in range(tc // 8):
            h = _scan8(a[8 * t:8 * t + 8], b[8 * t:8 * t + 8], carry, False)
            h_ref[8 * t:8 * t + 8, :] = h
            carry = h[7:8, :]
        carry_ref[...] = jnp.broadcast_to(carry, carry_ref.shape)
        y_ref[...] = h_ref[...] * _gelu_parts(pg_ref[...])[0]

    cur = pl.BlockSpec((None, tc, W), lambda b, j: (b, j, 0))
    rec = pl.BlockSpec((None, tc, W), lambda b, j: (b, j, 1))
    prev = pl.BlockSpec((None, 8, W), lambda b, j: (b, jnp.maximum(j * (tc // 8) - 1, 0), 1))
    vec = pl.BlockSpec((1, W), lambda b, j: (0, 0))
    cws = pl.BlockSpec((CONV_WIDTH, W), lambda b, j: (0, 0))
    wsp = pl.BlockSpec((LRU_HEADS, LRU_HEAD_DIM, LRU_HEAD_DIM), lambda b, j: (0, 0, 0))
    return pl.pallas_call(
        body, name="lru_fwd", grid=(B, nc),
        in_specs=[cur, rec, prev, cws, vec, wsp, vec, wsp, vec, vec],
        out_specs=[cur, cur],
        out_shape=[jax.ShapeDtypeStruct((B, Tp, W + MLA_HEADS * HEAD_LANES), F32), jax.ShapeDtypeStruct((B, Tp, W), F32)],
        scratch_shapes=[pltpu.VMEM((8, W), F32)],
        compiler_params=pltpu.CompilerParams(dimension_semantics=("arbitrary", "arbitrary")),
    )(p, p, p, cw, cb, wa, ba, wx, bx, sp)


def _lru_bwd_call(p, hseq, dy, cw, cb, wa, ba, wx, bx, sp, dpq, dpkv, dkpe):
    B, Tp, P = p.shape
    W = LRU_WIDTH
    tc = SEQ_BLOCK
    nc = Tp // tc
    HD = LRU_HEAD_DIM

    def body(pg_ref, prec_ref, prev_ref, h_ref, hprev_ref, dy_ref, cw_ref, cb_ref, wa_ref, ba_ref, wx_ref, bx_ref, sp_ref,
             dpq_ref, dpkv_ref, dkpe_ref, dp_ref, dcw_ref, dcb_ref, dwa_ref, dba_ref, dwx_ref, dbx_ref, dsp_ref,
             gcar_ref, anext_ref, halo_ref, g_ref):
        j = pl.program_id(1)
        first = j == nc - 1
        last = j == 0

        @pl.when(jnp.logical_and(pl.program_id(0) == 0, last))
        def _():
            for ref in (dcw_ref, dcb_ref, dwa_ref, dba_ref, dwx_ref, dbx_ref, dsp_ref):
                ref[...] = jnp.zeros_like(ref)

        @pl.when(last)
        def _():
            gcar_ref[...] = jnp.zeros_like(gcar_ref)
            anext_ref[...] = jnp.zeros_like(anext_ref)
            halo_ref[...] = jnp.zeros_like(halo_ref)

        taps, xc, r, i, a, a2, mult = _lru_pre(prec_ref, prev_ref, first, cw_ref, cb_ref, wa_ref, ba_ref, wx_ref, bx_ref, sp_ref)
        row = lax.broadcasted_iota(jnp.int32, (tc, W), 0)
        gelu, dgelu = _gelu_parts(pg_ref[...])
        dy = dy_ref[...]
        hcur = h_ref[...]
        dp_ref[:, 0:W] = dy * hcur * dgelu
        dp_ref[:, 2 * W:2 * W + MLA_Q_RANK] = dpq_ref[...]
        dp_ref[:, _KPE_START - MLA_KV_RANK:_KPE_START] = dpkv_ref[...]
        dp_ref[:, _KPE_START:P] = pltpu.roll(dkpe_ref[...], HEAD_LANES - MLA_NOPE, 1)[:, 0:P - _KPE_START]
        dh = dy * gelu
        a_next = jnp.where(row == tc - 1, anext_ref[0:1, :], pltpu.roll(a, tc - 1, 0))
        carry = gcar_ref[0:1, :]
        for t in reversed(range(tc // 8)):
            g = _scan8(a_next[8 * t:8 * t + 8], dh[8 * t:8 * t + 8], carry, True)
            g_ref[8 * t:8 * t + 8, :] = g
            carry = g[0:1, :]
        gcar_ref[...] = jnp.broadcast_to(carry, gcar_ref.shape)
        anext_ref[...] = jnp.broadcast_to(a[0:1, :], anext_ref.shape)
        G = g_ref[...]
        h_before = jnp.where(first, 0.0, hprev_ref[7:8, :])
        hprev = jnp.where(row == 0, h_before, pltpu.roll(hcur, 1, 0))
        d_a = G * hprev
        gx_ = G * xc
        d_mult = gx_ * i
        d_i = gx_ * mult
        dxc = G * (mult * i)
        d_la = d_a * a - d_mult * (a2 / mult)
        sp = sp_ref[...]
        d_r = d_la * (-LRU_C * sp)
        dsp_ref[...] += jnp.sum(d_la * (-LRU_C * r), axis=0, keepdims=True)
        dga = d_r * r * (1.0 - r)
        dgx = d_i * i * (1.0 - i)
        dba_ref[...] += jnp.sum(dga, axis=0, keepdims=True)
        dbx_ref[...] += jnp.sum(dgx, axis=0, keepdims=True)
        back = []
        for h in range(LRU_HEADS):
            sl = slice(h * HD, (h + 1) * HD)
            xh = xc[:, sl].astype(MXU_DTYPE)
            ah = dga[:, sl].astype(MXU_DTYPE)
            bh = dgx[:, sl].astype(MXU_DTYPE)
            tn = (((0,), (0,)), ((), ()))
            nt = (((1,), (1,)), ((), ()))
            dwa_ref[h] += lax.dot_general(xh, ah, tn, preferred_element_type=F32)
            dwx_ref[h] += lax.dot_general(xh, bh, tn, preferred_element_type=F32)
            back.append(lax.dot_general(ah, wa_ref[h].astype(MXU_DTYPE), nt, preferred_element_type=F32)
                        + lax.dot_general(bh, wx_ref[h].astype(MXU_DTYPE), nt, preferred_element_type=F32))
        dxc = dxc + jnp.concatenate(back, axis=1)
        dcb_ref[...] += jnp.sum(dxc, axis=0, keepdims=True)
        for k in range(CONV_WIDTH):
            dcw_ref[k:k + 1, :] += jnp.sum(dxc * taps[k], axis=0, keepdims=True)
        ext = jnp.concatenate([dxc, halo_ref[...]], axis=0)
        cw = cw_ref[...]
        acc = cw[CONV_WIDTH - 1:CONV_WIDTH, :] * dxc
        for k in range(CONV_WIDTH - 1):
            s = CONV_WIDTH - 1 - k
            acc = acc + cw[k:k + 1, :] * pltpu.roll(ext, tc + 8 - s, 0)[:tc]
        dp_ref[:, W:2 * W] = acc
        halo_ref[...] = dxc[0:8, :]

    rev = lambda j: nc - 1 - j
    cur = pl.BlockSpec((None, tc, W), lambda b, j: (b, rev(j), 0))
    rec = pl.BlockSpec((None, tc, W), lambda b, j: (b, rev(j), 1))
    prev = pl.BlockSpec((None, 8, W), lambda b, j: (b, jnp.maximum(rev(j) * (tc // 8) - 1, 0), 0))
    prev_rec = pl.BlockSpec((None, 8, W), lambda b, j: (b, jnp.maximum(rev(j) * (tc // 8) - 1, 0), 1))
    vec = pl.BlockSpec((1, W), lambda b, j: (0, 0))
    cws = pl.BlockSpec((CONV_WIDTH, W), lambda b, j: (0, 0))
    wsp = pl.BlockSpec((LRU_HEADS, HD, HD), lambda b, j: (0, 0, 0))
    vs = jax.ShapeDtypeStruct((1, W), F32)
    ws = jax.ShapeDtypeStruct((LRU_HEADS, HD, HD), F32)

    def rows(width):
        return pl.BlockSpec((None, tc, width), lambda b, j: (b, rev(j), 0))

    return pl.pallas_call(
        body, name="lru_bwd", grid=(B, nc),
        in_specs=[cur, rec, prev_rec, cur, prev, cur, cws, vec, wsp, vec, wsp, vec, vec, rows(MLA_Q_RANK), rows(MLA_KV_RANK), rows(HEAD_LANES)],
        out_specs=[rows(P), cws, vec, wsp, vec, wsp, vec, vec],
        out_shape=[jax.ShapeDtypeStruct((B, Tp, P), F32), jax.ShapeDtypeStruct((CONV_WIDTH, W), F32), vs, ws, vs, ws, vs, vs],
        scratch_shapes=[pltpu.VMEM((8, W), F32), pltpu.VMEM((8, W), F32), pltpu.VMEM((8, W), F32), pltpu.VMEM((tc, W), F32)],
        compiler_params=pltpu.CompilerParams(dimension_semantics=("arbitrary", "arbitrary")),
    )(p, p, p, hseq, hseq, dy, cw, cb, wa, ba, wx, bx, sp, dpq, dpkv, dkpe)


_Q_BLOCK = 2 * LRU_WIDTH // MLA_Q_RANK
_KV_BLOCK = (2 * LRU_WIDTH + MLA_Q_RANK) // MLA_KV_RANK
_KPE_START = 2 * LRU_WIDTH + MLA_Q_RANK + MLA_KV_RANK


@jax.custom_vjp
def even_front(p, cw, cb, wa, ba, wx, bx, sp, gq, gkv):
    return _even_front_fwd(p, cw, cb, wa, ba, wx, bx, sp, gq, gkv)[0]


def _even_front_fwd(p, cw, cb, wa, ba, wx, bx, sp, gq, gkv):
    B, Tp, W = p.shape
    p2d = p.reshape(B * Tp, W)
    y, hseq = _lru_fwd_call(p, cw, cb, wa, ba, wx, bx, sp)
    qn = _rms_fwd_call(p2d, gq, "q_norm_fwd", _Q_BLOCK)
    kvn = _rms_fwd_call(p2d, gkv, "kv_norm_fwd", _KV_BLOCK)
    kpe = jnp.pad(p[:, :, _KPE_START:], ((0, 0), (0, 0), (MLA_NOPE, HEAD_LANES - MLA_NOPE - MLA_ROPE)))
    return (y, qn, kvn, kpe), (p, hseq, cw, cb, wa, ba, wx, bx, sp, gq, gkv)


def _even_front_bwd(res, cts):
    p, hseq, cw, cb, wa, ba, wx, bx, sp, gq, gkv = res
    dy, dqn, dkvn, dkpe = cts
    B, Tp, W = p.shape
    p2d = p.reshape(B * Tp, W)
    dpq, dgq = _rms_bwd_call(p2d, gq, dqn, "q_norm_bwd", _Q_BLOCK)
    dpkv, dgkv = _rms_bwd_call(p2d, gkv, dkvn, "kv_norm_bwd", _KV_BLOCK)
    dp, dcw, dcb, dwa, dba, dwx, dbx, dsp = _lru_bwd_call(p, hseq, dy, cw, cb, wa, ba, wx, bx, sp, dpq.reshape(B, Tp, -1),
                                                          dpkv.reshape(B, Tp, -1), dkpe)
    return dp, dcw, dcb, dwa, dba, dwx, dbx, dsp, dgq.reshape(gq.shape), dgkv.reshape(gkv.shape)


even_front.defvjp(_even_front_fwd, _even_front_bwd)


def _rope_tables(T, half):
    inv = np.float32(ROPE_BASE) ** (-np.arange(half, dtype=np.float32) / np.float32(half))
    ang = np.arange(T, dtype=np.float32)[:, None] * inv[None, :]
    return np.cos(ang), np.sin(ang)


_NT = (((1,), (1,)), ((), ()))
_TN = (((0,), (0,)), ((), ()))
HEAD_LANES = 128
_MLA_SCALE = (MLA_NOPE + MLA_ROPE) ** -0.5
_LOG2E = math.log2(math.e)


Q_BLOCK = 512


def _query_blocks(Tp):
    first = Tp % Q_BLOCK or Q_BLOCK
    return [(0, first)] + [(r, r + Q_BLOCK) for r in range(first, Tp, Q_BLOCK)]


def _mask_diagonal(s, fill):
    R, L = s.shape
    row = lax.broadcasted_iota(jnp.int32, (R, R), 0)
    col = lax.broadcasted_iota(jnp.int32, (R, R), 1)
    last = jnp.where(col <= row, s[:, L - R:], fill)
    return last if L == R else jnp.concatenate([s[:, :L - R], last], axis=1)


def _mla_rope_tables(T):
    half = MLA_ROPE // 2
    cos, sin = _rope_tables(T, half)
    ones, zeros = np.ones((T, MLA_NOPE), np.float32), np.zeros((T, MLA_NOPE), np.float32)
    tail1, tail0 = np.ones((T, HEAD_LANES - MLA_NOPE - MLA_ROPE), np.float32), np.zeros((T, HEAD_LANES - MLA_NOPE - MLA_ROPE), np.float32)
    zh = np.zeros((T, half), np.float32)
    c = np.concatenate([ones, cos, cos, tail1], axis=1)
    s_up = np.concatenate([zeros, -sin, zh, tail0], axis=1)
    s_down = np.concatenate([zeros, zh, sin, tail0], axis=1)
    return jnp.asarray(c), jnp.asarray(s_up), jnp.asarray(s_down)


def _rope_lanes(x, c, s_up, s_down):
    half = MLA_ROPE // 2
    return x * c + pltpu.roll(x, HEAD_LANES - half, 1) * s_up + pltpu.roll(x, half, 1) * s_down


def _unrope_lanes(d, c, s_up, s_down):
    half = MLA_ROPE // 2
    return d * c + pltpu.roll(d * s_up, half, 1) + pltpu.roll(d * s_down, HEAD_LANES - half, 1)


def _mla_operands(q_ref, kv_ref, kpe_ref, c, s_up, s_down):
    lane = lax.broadcasted_iota(jnp.int32, kv_ref.shape, 1)
    qr = (_rope_lanes(q_ref[...].astype(F32), c, s_up, s_down) * (_MLA_SCALE * _LOG2E)).astype(MXU_DTYPE)
    kr = jnp.where(lane < MLA_NOPE, kv_ref[...].astype(F32), _rope_lanes(kpe_ref[...], c, s_up, s_down)).astype(MXU_DTYPE)
    return qr, kr, lane


def _mla_specs(Tp):
    head = pl.BlockSpec((None, Tp, HEAD_LANES), lambda b, h: (b, 0, h))
    shared = pl.BlockSpec((None, Tp, HEAD_LANES), lambda b, h: (b, 0, 0))
    tab = pl.BlockSpec((Tp, HEAD_LANES), lambda b, h: (0, 0))
    lse = pl.BlockSpec((None, None, Tp, 1), lambda b, h: (b, h, 0, 0))
    out = pl.BlockSpec((None, Tp, HEAD_LANES), lambda b, h: (b, 0, LRU_WIDTH // HEAD_LANES + h))
    return head, shared, tab, lse, out


def _attn_fwd_call(q, kv, kpe, tabs, y):
    B, Tp, _ = q.shape

    def body(q_ref, kv_ref, kpe_ref, c_ref, su_ref, sd_ref, y_ref, o_ref, lse_ref, qr_ref, kr_ref):
        qr, kr, lane = _mla_operands(q_ref, kv_ref, kpe_ref, c_ref[...], su_ref[...], sd_ref[...])
        qr_ref[...] = qr
        kr_ref[...] = kr
        for r0, L in _query_blocks(Tp):
            blk = slice(r0, L)
            s = _mask_diagonal(lax.dot_general(qr_ref[blk, :], kr_ref[0:L, :], _NT, preferred_element_type=F32), NEG_INF)
            m = jnp.max(s, axis=-1, keepdims=True)
            p = jnp.exp2(s - m)
            l = jnp.sum(p, axis=-1, keepdims=True)
            o = jnp.dot(p.astype(MXU_DTYPE), kv_ref[0:L, :].astype(MXU_DTYPE), preferred_element_type=F32)
            o_ref[blk, :] = jnp.where(lane[blk, :] >= MLA_NOPE, o / l, 0.0)
            lse_ref[blk, :] = m + jnp.log2(l)

    head, shared, tab, lse, out = _mla_specs(Tp)
    return pl.pallas_call(
        body, name="mla_attn_fwd", grid=(B, MLA_HEADS),
        in_specs=[head, head, shared, tab, tab, tab, pl.BlockSpec(memory_space=pl.ANY)], out_specs=[out, lse],
        out_shape=[jax.ShapeDtypeStruct(y.shape, F32), jax.ShapeDtypeStruct((B, MLA_HEADS, Tp, 1), F32)],
        input_output_aliases={6: 0},
        scratch_shapes=[pltpu.VMEM((Tp, HEAD_LANES), MXU_DTYPE), pltpu.VMEM((Tp, HEAD_LANES), MXU_DTYPE)],
        compiler_params=pltpu.CompilerParams(dimension_semantics=("parallel", "parallel")),
    )(q, kv, kpe, *tabs, y)


def _attn_bwd_call(q, kv, kpe, tabs, o, lse, do):
    B, Tp, _ = q.shape

    def body(q_ref, kv_ref, kpe_ref, c_ref, su_ref, sd_ref, o_ref, lse_ref, do_ref, dq_ref, dkv_ref, dkpe_ref,
             qr_ref, kr_ref, dqa_ref, dka_ref, dva_ref):
        c, s_up, s_down = c_ref[...], su_ref[...], sd_ref[...]
        qr, kr, lane = _mla_operands(q_ref, kv_ref, kpe_ref, c, s_up, s_down)
        qr_ref[...] = qr
        kr_ref[...] = kr
        dka_ref[...] = jnp.zeros_like(dka_ref)
        dva_ref[...] = jnp.zeros_like(dva_ref)
        for r0, L in _query_blocks(Tp):
            blk = slice(r0, L)
            qb = qr_ref[blk, :]
            do = jnp.where(lane[blk, :] >= MLA_NOPE, do_ref[blk, :], 0.0)
            delta = jnp.sum(do * o_ref[blk, :], axis=-1, keepdims=True)
            s = _mask_diagonal(lax.dot_general(qb, kr_ref[0:L, :], _NT, preferred_element_type=F32), NEG_INF)
            p = jnp.exp2(s - lse_ref[blk, :])
            dob = do.astype(MXU_DTYPE)
            dva_ref[0:L, :] += lax.dot_general(p.astype(MXU_DTYPE), dob, _TN, preferred_element_type=F32)
            dp = lax.dot_general(dob, kv_ref[0:L, :].astype(MXU_DTYPE), _NT, preferred_element_type=F32)
            ds = (p * (dp - delta)).astype(MXU_DTYPE)
            dqa_ref[blk, :] = jnp.dot(ds, kr_ref[0:L, :], preferred_element_type=F32)
            dka_ref[0:L, :] += lax.dot_general(ds, qb, _TN, preferred_element_type=F32)
        dq_ref[...] = _unrope_lanes(dqa_ref[...] * _MLA_SCALE, c, s_up, s_down).astype(dq_ref.dtype)
        dk = dka_ref[...] * (1.0 / _LOG2E)
        dkv_ref[...] = jnp.where(lane < MLA_NOPE, dk, dva_ref[...]).astype(dkv_ref.dtype)
        dkpe = jnp.where(lane >= MLA_NOPE, _unrope_lanes(dk, c, s_up, s_down), 0.0)

        @pl.when(pl.program_id(1) == 0)
        def _():
            dkpe_ref[...] = dkpe

        @pl.when(pl.program_id(1) > 0)
        def _():
            dkpe_ref[...] += dkpe

    head, shared, tab, lse_spec, out = _mla_specs(Tp)
    wide = jax.ShapeDtypeStruct((B, Tp, MLA_HEADS * HEAD_LANES), q.dtype)
    acc = pltpu.VMEM((Tp, HEAD_LANES), F32)
    return pl.pallas_call(
        body, name="mla_attn_bwd", grid=(B, MLA_HEADS),
        in_specs=[head, head, shared, tab, tab, tab, out, lse_spec, out], out_specs=[head, head, shared],
        out_shape=[wide, wide, jax.ShapeDtypeStruct((B, Tp, HEAD_LANES), F32)],
        scratch_shapes=[pltpu.VMEM((Tp, HEAD_LANES), MXU_DTYPE), pltpu.VMEM((Tp, HEAD_LANES), MXU_DTYPE), acc, acc, acc],
        compiler_params=pltpu.CompilerParams(dimension_semantics=("parallel", "arbitrary")),
    )(q, kv, kpe, *tabs, o, lse, do)


@jax.custom_vjp
def mla_attention(q, kv, kpe, tabs, y):
    return _attn_fwd_call(q, kv, kpe, tabs, y)[0]


def _mla_attention_fwd(q, kv, kpe, tabs, y):
    o, lse = _attn_fwd_call(q, kv, kpe, tabs, y)
    return o, (q, kv, kpe, tabs, o, lse)


def _mla_attention_bwd(res, do):
    q, kv, kpe, tabs, o, lse = res
    dq, dkv, dkpe = _attn_bwd_call(q, kv, kpe, tabs, o, lse, do)
    return dq, dkv, dkpe, None, do


mla_attention.defvjp(_mla_attention_fwd, _mla_attention_bwd)


def _rope_halves(x, cos, sin):
    half = x.shape[1] // 2
    x1, x2 = x[:, :half], x[:, half:]
    return jnp.concatenate([x1 * cos - x2 * sin, x1 * sin + x2 * cos], axis=1)


def _unrope_halves(d, cos, sin):
    half = d.shape[1] // 2
    d1, d2 = d[:, :half], d[:, half:]
    return jnp.concatenate([d1 * cos + d2 * sin, d2 * cos - d1 * sin], axis=1)


_RET_K_SCALE = RET_QK_DIM ** -0.5
_RET_Q_BLOCKS = RET_HEADS
_RET_V_BLOCK0 = 2 * RET_HEADS * RET_QK_DIM // RET_V_DIM
_RET_G_BLOCK0 = _RET_V_BLOCK0 + RET_HEADS
_ANY_SPACE = pl.BlockSpec(memory_space=pl.ANY)


def _ret_specs(Tp):
    q = pl.BlockSpec((None, Tp, RET_QK_DIM), lambda b, h: (b, 0, h))
    k = pl.BlockSpec((None, Tp, RET_QK_DIM), lambda b, h: (b, 0, _RET_Q_BLOCKS + h))
    v = pl.BlockSpec((None, Tp, RET_V_DIM), lambda b, h: (b, 0, _RET_V_BLOCK0 + h))
    tab = pl.BlockSpec((Tp, RET_QK_DIM // 2), lambda b, h: (0, 0))
    lg = pl.BlockSpec((None, 1, 1), lambda b, h: (h, 0, 0))
    return q, k, v, tab, lg


def _ret_operands(q_ref, k_ref, cos, sin, lg):
    t = lax.broadcasted_iota(jnp.int32, (q_ref.shape[0], 1), 0).astype(F32)
    grow, shrink = jnp.exp(-lg * t), jnp.exp(lg * t)
    qs = (_rope_halves(q_ref[...].astype(F32), cos, sin) * shrink).astype(MXU_DTYPE)
    ks = (_rope_halves(k_ref[...].astype(F32), cos, sin) * (grow * _RET_K_SCALE)).astype(MXU_DTYPE)
    return qs, ks, shrink, grow * _RET_K_SCALE


def _ret_core_fwd_call(p, cos, sin, lg):
    B, Tp, _ = p.shape

    def body(q_ref, k_ref, v_ref, cos_ref, sin_ref, lg_ref, o_ref, qs_ref, ks_ref):
        qs_ref[...], ks_ref[...], _, _ = _ret_operands(q_ref, k_ref, cos_ref[...], sin_ref[...], lg_ref[...])
        for r0, L in _query_blocks(Tp):
            blk = slice(r0, L)
            s = _mask_diagonal(lax.dot_general(qs_ref[blk, :], ks_ref[0:L, :], _NT, preferred_element_type=F32), 0.0)
            o_ref[blk, :] = jnp.dot(s.astype(MXU_DTYPE), v_ref[0:L, :].astype(MXU_DTYPE), preferred_element_type=F32)

    q, k, v, tab, lgs = _ret_specs(Tp)
    return pl.pallas_call(
        body, name="retention_fwd", grid=(B, RET_HEADS), in_specs=[q, k, v, tab, tab, lgs],
        out_specs=pl.BlockSpec((None, Tp, RET_V_DIM), lambda b, h: (b, 0, h)),
        out_shape=jax.ShapeDtypeStruct((B, Tp, RET_HEADS * RET_V_DIM), F32),
        scratch_shapes=[pltpu.VMEM((Tp, RET_QK_DIM), MXU_DTYPE), pltpu.VMEM((Tp, RET_QK_DIM), MXU_DTYPE)],
        compiler_params=pltpu.CompilerParams(dimension_semantics=("parallel", "parallel")),
    )(p, p, p, cos, sin, lg)


def _ret_core_bwd_call(p, do, cos, sin, lg, dp):
    B, Tp, _ = p.shape
    n_steps = B * RET_HEADS

    def body(q_ref, k_ref, v_ref, do_ref, cos_ref, sin_ref, lg_ref, dp_in_ref, dp_ref, qs_ref, ks_ref, dqa_ref, dka_ref, dva_ref,
             dq_ref, dk_ref, dv_ref, sems):
        b, h = pl.program_id(0), pl.program_id(1)
        step = b * RET_HEADS + h
        slot = step % 2

        def copies(s):
            cols = lambda start, width: pl.ds(pl.multiple_of(start + h * width, width), width)
            return [pltpu.make_async_copy(dq_ref.at[s], dp_ref.at[b, :, cols(0, RET_QK_DIM)], sems.at[s, 0]),
                    pltpu.make_async_copy(dk_ref.at[s], dp_ref.at[b, :, cols(RET_HEADS * RET_QK_DIM, RET_QK_DIM)], sems.at[s, 1]),
                    pltpu.make_async_copy(dv_ref.at[s], dp_ref.at[b, :, cols(2 * RET_HEADS * RET_QK_DIM, RET_V_DIM)], sems.at[s, 2])]

        cos_, sin_ = cos_ref[...], sin_ref[...]
        qs_ref[...], ks_ref[...], q_scale, k_scale = _ret_operands(q_ref, k_ref, cos_, sin_, lg_ref[...])
        dka_ref[...] = jnp.zeros_like(dka_ref)
        dva_ref[...] = jnp.zeros_like(dva_ref)
        for r0, L in _query_blocks(Tp):
            blk = slice(r0, L)
            qb = qs_ref[blk, :]
            dob = do_ref[blk, :].astype(MXU_DTYPE)
            s = _mask_diagonal(lax.dot_general(qb, ks_ref[0:L, :], _NT, preferred_element_type=F32), 0.0).astype(MXU_DTYPE)
            dva_ref[0:L, :] += lax.dot_general(s, dob, _TN, preferred_element_type=F32)
            ds = _mask_diagonal(lax.dot_general(dob, v_ref[0:L, :].astype(MXU_DTYPE), _NT, preferred_element_type=F32), 0.0).astype(MXU_DTYPE)
            dqa_ref[blk, :] = jnp.dot(ds, ks_ref[0:L, :], preferred_element_type=F32)
            dka_ref[0:L, :] += lax.dot_general(ds, qb, _TN, preferred_element_type=F32)
        dq_ref[slot] = _unrope_halves(dqa_ref[...] * q_scale, cos_, sin_).astype(dq_ref.dtype)
        dk_ref[slot] = _unrope_halves(dka_ref[...] * k_scale, cos_, sin_).astype(dk_ref.dtype)
        dv_ref[slot] = dva_ref[...].astype(dv_ref.dtype)
        for cp in copies(slot):
            cp.start()

        @pl.when(step > 0)
        def _():
            for cp in copies(1 - slot):
                cp.wait()

        @pl.when(step == n_steps - 1)
        def _():
            for cp in copies(slot):
                cp.wait()

    q, k, v, tab, lgs = _ret_specs(Tp)
    v_out = pl.BlockSpec((None, Tp, RET_V_DIM), lambda b, h: (b, 0, h))
    return pl.pallas_call(
        body, name="retention_bwd", grid=(B, RET_HEADS), in_specs=[q, k, v, v_out, tab, tab, lgs, _ANY_SPACE],
        out_specs=_ANY_SPACE, out_shape=jax.ShapeDtypeStruct(dp.shape, dp.dtype), input_output_aliases={7: 0},
        scratch_shapes=[pltpu.VMEM((Tp, RET_QK_DIM), MXU_DTYPE), pltpu.VMEM((Tp, RET_QK_DIM), MXU_DTYPE),
                        pltpu.VMEM((Tp, RET_QK_DIM), F32), pltpu.VMEM((Tp, RET_QK_DIM), F32), pltpu.VMEM((Tp, RET_V_DIM), F32),
                        pltpu.VMEM((2, Tp, RET_QK_DIM), dp.dtype), pltpu.VMEM((2, Tp, RET_QK_DIM), dp.dtype),
                        pltpu.VMEM((2, Tp, RET_V_DIM), dp.dtype), pltpu.SemaphoreType.DMA((2, 3))],
        compiler_params=pltpu.CompilerParams(dimension_semantics=("arbitrary", "arbitrary")),
    )(p, p, p, do, cos, sin, lg, dp)


def _ret_gate_specs(M):
    tm = _pick(M, 1088, 8)
    head = pl.BlockSpec((tm, RET_V_DIM), lambda i, h: (i, h))
    gate = pl.BlockSpec((tm, RET_V_DIM), lambda i, h: (i, _RET_G_BLOCK0 + h))
    return tm, head, gate


def _ret_gate_fwd_call(o, p2d):
    M = o.shape[0]
    tm, head, gate = _ret_gate_specs(M)

    def body(o_ref, g_ref, y_ref):
        ov = o_ref[...]
        gv = g_ref[...].astype(F32)
        rstd = lax.rsqrt(jnp.mean(ov * ov, axis=-1, keepdims=True) + EPS)
        y_ref[...] = (gv * _sigmoid(gv)) * (ov * rstd)

    return pl.pallas_call(
        body, name="retention_gate_fwd", grid=(M // tm, RET_HEADS), in_specs=[head, gate], out_specs=head,
        out_shape=jax.ShapeDtypeStruct(o.shape, F32),
        compiler_params=pltpu.CompilerParams(dimension_semantics=("parallel", "parallel")),
    )(o, p2d)


def _ret_gate_bwd_call(o, p2d, dy):
    M = o.shape[0]
    tm, head, gate = _ret_gate_specs(M)

    def body(o_ref, g_ref, dy_ref, do_ref, dg_ref):
        ov = o_ref[...]
        gv = g_ref[...].astype(F32)
        dy = dy_ref[...]
        rstd = lax.rsqrt(jnp.mean(ov * ov, axis=-1, keepdims=True) + EPS)
        on = ov * rstd
        sg = _sigmoid(gv)
        dg_ref[...] = (dy * on * (sg * (1.0 + gv * (1.0 - sg)))).astype(dg_ref.dtype)
        don = dy * (gv * sg)
        do_ref[...] = (rstd * (don - on * jnp.mean(don * on, axis=-1, keepdims=True))).astype(do_ref.dtype)

    return pl.pallas_call(
        body, name="retention_gate_bwd", grid=(M // tm, RET_HEADS), in_specs=[head, gate, head], out_specs=[head, gate],
        out_shape=[jax.ShapeDtypeStruct(o.shape, p2d.dtype), jax.ShapeDtypeStruct(p2d.shape, p2d.dtype)],
        compiler_params=pltpu.CompilerParams(dimension_semantics=("parallel", "parallel")),
    )(o, p2d, dy)


def _log_gamma():
    return jnp.log(1.0 - 2.0 ** (-5.0 - jnp.arange(RET_HEADS, dtype=F32))).reshape(RET_HEADS, 1, 1)


@functools.partial(jax.custom_vjp, nondiff_argnums=(9,))
def retention_block(h, w_in, w_out, w_in_grad_slot, w_out_grad_slot, g, b, cos, sin, dims):
    return _retention_block_fwd(h, w_in, w_out, w_in_grad_slot, w_out_grad_slot, g, b, cos, sin, dims)[0]


def _retention_block_fwd(h, w_in, w_out, w_in_grad_slot, w_out_grad_slot, g, b, cos, sin, dims):
    B, Tp = dims
    p = _mm_nn(h, w_in, False, "od_w_in_fwd", out_dtype=MXU_DTYPE)
    o = _ret_core_fwd_call(p.reshape(B, Tp, -1), cos, sin, _log_gamma())
    y = _ret_gate_fwd_call(o.reshape(B * Tp, -1), p)
    out, z = _mm_nn(y, w_out, False, "od_w_out_norm_fwd", norm=(h, g, b))
    return out, (h, p, o, y, z, w_in, w_out, g, cos, sin, jnp.zeros((), w_in_grad_slot.dtype))


def _retention_block_bwd(dims, res, dout):
    B, Tp = dims
    h, p, o, y, z, w_in, w_out, g, cos, sin, slot_like = res
    dz, dg, db = _ln_bwd_call(z, g, dout, "od_w_out_norm_bwd")
    dy = _mm_nt(dz, w_out, None, "od_w_out_dx")
    dw_out = _mm_tn(y, dz, False, "od_w_out_dw", 1, slot_like.dtype)
    do, dp = _ret_gate_bwd_call(o.reshape(B * Tp, -1), p, dy)
    dp = _ret_core_bwd_call(p.reshape(B, Tp, -1), do.reshape(B, Tp, -1), cos, sin, _log_gamma(), dp.reshape(B, Tp, -1)).reshape(B * Tp, -1)
    dh = _mm_nt(dp, w_in, None, "od_w_in_dx", plus=dz)
    dw_in = _mm_tn(h, dp, False, "od_w_in_dw", N_CHIPS, slot_like.dtype)
    return dh, None, None, dw_in, dw_out, dg.reshape(g.shape), db.reshape(g.shape), None, None


retention_block.defvjp(_retention_block_fwd, _retention_block_bwd)


def _heads_to_lanes(w):
    K = w.shape[0]
    w = w.reshape(K, MLA_HEADS, MLA_NOPE + MLA_ROPE)
    return jnp.pad(w, ((0, 0), (0, 0), (0, HEAD_LANES - MLA_NOPE - MLA_ROPE))).reshape(K, MLA_HEADS * HEAD_LANES)


def _out_rows_to_lanes(w):
    N = w.shape[1]
    att = w[LRU_WIDTH:].reshape(MLA_HEADS, MLA_V, N)
    att = jnp.pad(att, ((0, 0), (HEAD_LANES - MLA_V, 0), (0, 0))).reshape(MLA_HEADS * HEAD_LANES, N)
    return jnp.concatenate([w[:LRU_WIDTH], att], axis=0)


def _seq_dims(x):
    B, S, D = x.shape
    T = S + N_META
    Tp = _round_up(T, SEQ_BLOCK)
    return B, S, T, Tp


def _mixer0(diff, w, token):
    x = diff["x"]
    B, S, T, Tp = _seq_dims(x)
    D = x.shape[-1]
    M = B * Tp

    def mm(a, name, act=False, out_dtype=F32, layout=lambda m: m, col_shards=1):
        return matmul(a, layout(w[name]), layout(diff[name]), act, name, out_dtype, col_shards)

    meta = jnp.broadcast_to((diff["meta_tokens"] + token)[None], (B, N_META, D))
    h = jnp.concatenate([meta, x, jnp.zeros((B, Tp - T, D), F32)], axis=1).reshape(M, D)
    p = mm(h, "ev_w_in")
    sp = jax.nn.softplus(-diff["ev_lru_lambda"]).reshape(1, LRU_WIDTH)
    y, qn, kvn, kpe = even_front(
        p.reshape(B, Tp, -1), diff["ev_conv_w"].reshape(CONV_WIDTH, LRU_WIDTH), diff["ev_conv_b"].reshape(1, LRU_WIDTH),
        diff["ev_w_rg_a"].reshape(LRU_HEADS, LRU_HEAD_DIM, LRU_HEAD_DIM), diff["ev_b_rg_a"].reshape(1, LRU_WIDTH),
        diff["ev_w_rg_x"].reshape(LRU_HEADS, LRU_HEAD_DIM, LRU_HEAD_DIM), diff["ev_b_rg_x"].reshape(1, LRU_WIDTH),
        sp, diff["ev_q_norm_g"].reshape(-1), diff["ev_kv_norm_g"].reshape(-1))
    q = mm(qn, "ev_w_uq", out_dtype=MXU_DTYPE, layout=_heads_to_lanes).reshape(B, Tp, -1)
    kv = mm(kvn, "ev_w_ukv", out_dtype=MXU_DTYPE).reshape(B, Tp, -1)
    y = mla_attention(q, kv, kpe, _mla_rope_tables(Tp), y).reshape(M, -1)
    return out_block(h, y, _out_rows_to_lanes(w["ev_w_out"]), _out_rows_to_lanes(diff["ev_w_out"]),
                     diff["ln_mix_g"], diff["ln_mix_b"], "ev_w_out")


def _mlp0(diff, h, w):
    return mlp_block(h, w["mlp_w1_0"], w["mlp_w2_0"], diff["mlp_w1_0"], diff["mlp_w2_0"], diff["ln_mlp_g"], diff["ln_mlp_b"], "mlp0")


def _layer1_loss(diff, h, w, tgt):
    B, S, T, Tp = _seq_dims(tgt)
    D = tgt.shape[-1]

    cos, sin = (jnp.asarray(t) for t in _rope_tables(Tp, RET_QK_DIM // 2))
    h = retention_block(h, w["od_w_in"], w["od_w_out"], diff["od_w_in"], diff["od_w_out"], diff["ln_mix_g"], diff["ln_mix_b"], cos, sin, (B, Tp))
    h = mlp_block(h, w["mlp_w1_1"], w["mlp_w2_1"], diff["mlp_w1_1"], diff["mlp_w2_1"], diff["ln_mlp_g"], diff["ln_mlp_b"], "mlp1")
    return loss_head(h.reshape(B, Tp, D), jnp.pad(tgt, ((0, 0), (N_META, Tp - T), (0, 0))), S)


_HBM = pl.BlockSpec(memory_space=pltpu.HBM)


def _place():
    return lax.axis_index("x"), lax.axis_index("y"), lax.axis_index("c")


def _other_chips(x, y):
    return [(1 - x, y), (x, 1 - y), (1 - x, 1 - y)]


def _chunks(rows, sublanes, most):
    for q in range(most, 0, -1):
        if rows % (q * sublanes) == 0:
            return q
    return 1


def _sublanes(dtype):
    return 8 * 4 // jnp.dtype(dtype).itemsize


def _gather_pieces(bufs):
    plan, first = [], []
    for b in bufs:
        Rh = b.shape[0] // 2
        Q = _chunks(Rh, _sublanes(b.dtype), 4) if Rh * b.shape[1] * b.dtype.itemsize > (1 << 20) else 1
        first.append(3 * sum(q for _, q, _ in plan))
        plan.append((Rh, Q, Rh // Q))
    return plan, first, 3 * sum(q for _, q, _ in plan)


def _allgather_chips(bufs, name):
    n = len(bufs)
    plan, first, n_sems = _gather_pieces(bufs)

    def body(*refs):
        x_refs, out_refs, (send_sems, recv_sems) = refs[:n], refs[n:2 * n], refs[2 * n:]
        x, y, c = _place()
        sibling = (x, y, 1 - c)
        chips = _other_chips(x, y)

        def copy(k, src, dst, to):
            return pltpu.make_async_remote_copy(src_ref=src, dst_ref=dst, send_sem=send_sems.at[k], recv_sem=recv_sems.at[k],
                                                device_id=to, device_id_type=MESH)

        def piece(i, cx, cy, hc, q):
            Rh, _, ch = plan[i]
            return out_refs[i].at[2 * cx + cy, pl.ds(hc * Rh + q * ch, ch), :]

        slots = [(i, q, j) for i in range(n) for q in range(plan[i][1]) for j in range(3)]
        sem = {(i, q, j): first[i] + 3 * q + j for i, q, j in slots}
        sent = [copy(sem[i, q, j], x_refs[i].at[pl.ds(c * plan[i][0] + q * plan[i][2], plan[i][2]), :], piece(i, x, y, c, q), (*chips[j], c))
                for i, q, j in slots]
        for cp in sent:
            cp.start()
        passed = []
        for i, q, j in slots:
            landed = piece(i, *chips[j], c, q)
            copy(sem[i, q, j], landed, landed, sibling).wait_recv()
            fwd = copy(n_sems + sem[i, q, j], landed, landed, sibling)
            fwd.start()
            passed.append(fwd)
        for i, q, j in slots:
            theirs = piece(i, *chips[j], 1 - c, q)
            copy(n_sems + sem[i, q, j], theirs, theirs, sibling).wait_recv()
        for cp in sent + passed:
            cp.wait_send()

    return pl.pallas_call(
        body, name=name, in_specs=[_HBM] * n, out_specs=[_HBM] * n,
        out_shape=[jax.ShapeDtypeStruct((N_CHIPS,) + b.shape, b.dtype) for b in bufs],
        scratch_shapes=[pltpu.SemaphoreType.DMA((2 * n_sems,)), pltpu.SemaphoreType.DMA((2 * n_sems,))],
    )(*bufs)


def _with_own(gathered, own):
    my = 2 * lax.axis_index("x") + lax.axis_index("y")
    return lax.dynamic_update_slice(gathered, own[None], (my, 0, 0))


def _sibling_gather(fs, name):
    n = len(fs)

    def body(*refs):
        out_refs, (send_sems, recv_sems) = refs[n:2 * n], refs[2 * n:]
        x, y, c = _place()
        copies = [pltpu.make_async_remote_copy(src_ref=out_ref.at[c], dst_ref=out_ref.at[c], send_sem=send_sems.at[i], recv_sem=recv_sems.at[i],
                                               device_id=(x, y, 1 - c), device_id_type=MESH) for i, out_ref in enumerate(out_refs)]
        for cp in copies:
            cp.start()
        for cp in copies:
            cp.wait()

    return pl.pallas_call(
        body, name=name, in_specs=[_HBM] * n, out_specs=[_HBM] * n,
        out_shape=[jax.ShapeDtypeStruct(f.shape, f.dtype) for f in fs], input_output_aliases={i: i for i in range(n)},
        scratch_shapes=[pltpu.SemaphoreType.DMA((n,)), pltpu.SemaphoreType.DMA((n,))],
    )(*fs)


def _axis_scalar(name):
    return lax.axis_index(name).astype(jnp.int32).reshape(1)


_SEM = pl.BlockSpec(memory_space=pltpu.SEMAPHORE)
_ANY = pl.BlockSpec(memory_space=pl.ANY)
_EFFECT = pltpu.SideEffectType.DATAFLOW_SIDE_EFFECTING


def _in_hbm(a):
    return pltpu.with_memory_space_constraint(a, pltpu.HBM)


def _half_copies(x_refs, land_refs, send_sems, recv_sems, arriving):
    x, y, c = _place()
    copies = []
    for i, (x_ref, land_ref) in enumerate(zip(x_refs, land_refs)):
        Rh = x_ref.shape[0] // 2
        rows = pl.ds(c * Rh, Rh)
        for j, (cx, cy) in enumerate(_other_chips(x, y)):
            copies.append(pltpu.make_async_remote_copy(
                src_ref=x_ref.at[rows, :], dst_ref=land_ref.at[2 * cx + cy if arriving else 2 * x + y, rows, :],
                send_sem=send_sems.at[3 * i + j], recv_sem=recv_sems.at[3 * i + j], device_id=(cx, cy, c), device_id_type=MESH))
    return copies


def _allgather_start(bufs, name):
    n = len(bufs)

    def body(*refs):
        x_refs, land_refs, (send_sems, recv_sems), token = refs[:n], refs[n:2 * n], refs[2 * n:2 * n + 2], refs[-1]
        for cp in _half_copies(x_refs, land_refs, send_sems, recv_sems, False):
            cp.start()
        token[...] = jnp.zeros_like(token)

    lands = [lax.empty((N_CHIPS,) + b.shape, b.dtype) for b in bufs]
    out = pl.pallas_call(
        body, name=name,
        out_shape=(pltpu.SemaphoreType.DMA((3 * n,)), pltpu.SemaphoreType.DMA((3 * n,)), *[pltpu.HBM(a.shape, a.dtype) for a in bufs + lands],
                   jax.ShapeDtypeStruct((8, 128), F32)),
        in_specs=[_HBM] * (2 * n), out_specs=(_SEM, _SEM, *[_HBM] * (2 * n), pl.BlockSpec(memory_space=pltpu.VMEM)),
        input_output_aliases={i: 2 + i for i in range(2 * n)}, compiler_params=pltpu.CompilerParams(has_side_effects=_EFFECT),
    )(*[_in_hbm(a) for a in bufs + lands])
    return (out[0], out[1], list(out[2:2 + n]), list(out[2 + n:2 + 2 * n])), out[-1][0, 0]


def _allgather_wait(pending, after, name):
    send_sems, recv_sems, bufs, lands = pending
    n = len(bufs)

    def body(*refs):
        x_refs, land_refs, send_sems, recv_sems = refs[:n], refs[n:2 * n], refs[2 * n], refs[2 * n + 1]
        for cp in _half_copies(x_refs, land_refs, send_sems, recv_sems, False):
            cp.wait_send()
        for cp in _half_copies(x_refs, land_refs, send_sems, recv_sems, True):
            cp.wait_recv()

    out = pl.pallas_call(
        body, name=name, out_shape=tuple(pltpu.HBM(a.shape, a.dtype) for a in bufs + lands),
        in_specs=[_HBM] * (2 * n) + [_SEM, _SEM, _ANY], out_specs=tuple([_HBM] * (2 * n)), input_output_aliases={i: i for i in range(2 * n)},
        compiler_params=pltpu.CompilerParams(has_side_effects=_EFFECT),
    )(*bufs, *lands, send_sems, recv_sems, after)
    return list(out[n:])


def _sibling_forward(lands, name):
    n = len(lands)
    plan, first, n_sems = _gather_pieces([jax.ShapeDtypeStruct(l.shape[1:], l.dtype) for l in lands])

    def body(*refs):
        out_refs, (send_sems, recv_sems) = refs[n:2 * n], refs[2 * n:]
        x, y, c = _place()

        def copies(hc):
            return [pltpu.make_async_remote_copy(
                        src_ref=out_refs[i].at[2 * cx + cy, pl.ds(hc * plan[i][0] + q * plan[i][2], plan[i][2]), :],
                        dst_ref=out_refs[i].at[2 * cx + cy, pl.ds(hc * plan[i][0] + q * plan[i][2], plan[i][2]), :],
                        send_sem=send_sems.at[first[i] + 3 * q + j], recv_sem=recv_sems.at[first[i] + 3 * q + j],
                        device_id=(x, y, 1 - c), device_id_type=MESH)
                    for i in range(n) for q in range(plan[i][1]) for j, (cx, cy) in enumerate(_other_chips(x, y))]

        mine = copies(c)
        for cp in mine:
            cp.start()
        for cp in mine:
            cp.wait_send()
        for cp in copies(1 - c):
            cp.wait_recv()

    return pl.pallas_call(
        body, name=name, in_specs=[_HBM] * n, out_specs=[_HBM] * n, out_shape=[jax.ShapeDtypeStruct(l.shape, l.dtype) for l in lands],
        input_output_aliases={i: i for i in range(n)},
        scratch_shapes=[pltpu.SemaphoreType.DMA((n_sems,)), pltpu.SemaphoreType.DMA((n_sems,))],
    )(*lands)


N_PEERS = 7


def _direct_copies(p_refs, t_refs, send_sems, recv_sems):
    x, y, c = _place()
    copies = []
    for i, (p_ref, t_ref) in enumerate(zip(p_refs, t_refs)):
        for f in range(1, N_PEERS + 1):
            px, py, pc = x ^ (f >> 2), y ^ ((f >> 1) & 1), c ^ (f & 1)
            copies.append(pltpu.make_async_remote_copy(
                src_ref=p_ref.at[2 * px + py, pc], dst_ref=t_ref.at[f - 1], send_sem=send_sems.at[N_PEERS * i + f - 1],
                recv_sem=recv_sems.at[N_PEERS * i + f - 1], device_id=(px, py, pc), device_id_type=MESH))
    return copies


def _direct_scatter_start(ps, name, carried=()):
    n, m = len(ps), 2 * len(ps) + len(carried)

    def body(*refs):
        p_refs, t_refs, (send_sems, recv_sems) = refs[:n], refs[n:2 * n], refs[m:m + 2]
        for cp in _direct_copies(p_refs, t_refs, send_sems, recv_sems):
            cp.start()

    lands = [lax.empty((N_PEERS,) + p.shape[2:], p.dtype) for p in ps]
    through = ps + lands + list(carried)
    out = pl.pallas_call(
        body, name=name,
        out_shape=(pltpu.SemaphoreType.DMA((N_PEERS * n,)), pltpu.SemaphoreType.DMA((N_PEERS * n,)),
                   *[pltpu.HBM(a.shape, a.dtype) for a in through]),
        in_specs=[_HBM] * m, out_specs=(_SEM, _SEM, *[_HBM] * m),
        input_output_aliases={i: 2 + i for i in range(m)}, compiler_params=pltpu.CompilerParams(has_side_effects=_EFFECT),
    )(*[_in_hbm(a) for a in through])
    return (out[0], out[1], list(out[2:2 + n]), list(out[2 + n:2 + 2 * n])), list(out[2 + 2 * n:])


def _direct_scatter_wait(pending, after, name):
    send_sems, recv_sems, ps, lands = pending
    n = len(ps)

    def body(*refs):
        p_refs, t_refs, send_sems, recv_sems = refs[:n], refs[n:2 * n], refs[2 * n], refs[2 * n + 1]
        for cp in _direct_copies(p_refs, t_refs, send_sems, recv_sems):
            cp.wait_send()
            cp.wait_recv()

    out = pl.pallas_call(
        body, name=name, out_shape=tuple(pltpu.HBM(a.shape, a.dtype) for a in ps + lands),
        in_specs=[_HBM] * (2 * n) + [_SEM, _SEM] + [_ANY] * len(after), out_specs=tuple([_HBM] * (2 * n)),
        input_output_aliases={i: i for i in range(2 * n)}, compiler_params=pltpu.CompilerParams(has_side_effects=_EFFECT),
    )(*ps, *lands, send_sems, recv_sems, *after)
    return list(out[:n]), list(out[n:])


def _sum_direct(p, t, name):
    _, _, R, C = p.shape
    tr = _pick(R, 512, 16)

    def body(x_ref, y_ref, c_ref, p_ref, t_ref, o_ref):
        acc = p_ref[...].astype(F32)
        for f in range(N_PEERS):
            acc = acc + t_ref[f].astype(F32)
        o_ref[...] = acc

    grid_spec = pltpu.PrefetchScalarGridSpec(
        num_scalar_prefetch=3, grid=(R // tr,),
        in_specs=[pl.BlockSpec((None, None, tr, C), lambda i, x_ref, y_ref, c_ref: (2 * x_ref[0] + y_ref[0], c_ref[0], i, 0)),
                  pl.BlockSpec((N_PEERS, tr, C), lambda i, x_ref, y_ref, c_ref: (0, i, 0))],
        out_specs=pl.BlockSpec((None, tr, C), lambda i, x_ref, y_ref, c_ref: (c_ref[0], i, 0)))
    return pl.pallas_call(body, name=name, grid_spec=grid_spec, out_shape=jax.ShapeDtypeStruct((2, R, C), F32),
                          compiler_params=pltpu.CompilerParams(dimension_semantics=("parallel",)))(
        _axis_scalar("x"), _axis_scalar("y"), _axis_scalar("c"), p, t)


def _adamw(w, g, m, v, name):
    R, C = w.shape
    tr = _pick(R, 256, 8)

    def body(w_ref, g_ref, m_ref, v_ref, d_ref, nm_ref, nv_ref):
        g_ = g_ref[...]
        m_ = ADAM_B1 * m_ref[...] + (1.0 - ADAM_B1) * g_
        v_ = ADAM_B2 * v_ref[...] + (1.0 - ADAM_B2) * (g_ * g_)
        m_hat = m_ / (1.0 - ADAM_B1 ** ADAM_STEP)
        v_hat = v_ / (1.0 - ADAM_B2 ** ADAM_STEP)
        d_ref[...] = -ADAM_LR * (m_hat / (jnp.sqrt(v_hat) + ADAM_EPS) + ADAM_WD * w_ref[...])
        nm_ref[...] = m_
        nv_ref[...] = v_

    row = pl.BlockSpec((tr, C), lambda i: (i, 0))
    shp = jax.ShapeDtypeStruct((R, C), F32)
    return pl.pallas_call(body, name=name, grid=(R // tr,), in_specs=[row] * 4, out_specs=[row] * 3, out_shape=[shp] * 3,
                          compiler_params=pltpu.CompilerParams(dimension_semantics=("parallel",)))(w, g, m, v)


BIG_SPECS = (("ev_w_in", 1024, 1440, 1), ("ev_w_uq", 256, 768, 1), ("ev_w_ukv", 128, 1024, 1), ("ev_w_out", 1024, 1024, 0),
             ("od_w_in", 1024, 6144, 1), ("od_w_out", 2048, 1024, 0), ("mlp_w1_0", 1024, 4096, 1), ("mlp_w1_1", 1024, 4096, 1),
             ("mlp_w2_0", 4096, 1024, 0), ("mlp_w2_1", 4096, 1024, 0))
BIG_PARAMS = (("ev_w_in", ("ev_w_in",)), ("ev_w_uq", ("ev_w_uq",)), ("ev_w_ukv", ("ev_w_ukv",)), ("ev_w_out", ("ev_w_out",)),
              ("od_w_in", ("od_w_in",)), ("od_w_out", ("od_w_out",)), ("mlp_w1", ("mlp_w1_0", "mlp_w1_1")),
              ("mlp_w2", ("mlp_w2_0", "mlp_w2_1")))
REPLICATED = ("ev_conv_b", "ev_w_rg_a", "ev_b_rg_a", "ev_w_rg_x", "ev_b_rg_x", "ev_lru_lambda", "ev_q_norm_g", "ev_kv_norm_g",
              "ln_mix_g", "ln_mix_b", "ln_mlp_g", "ln_mlp_b")
SMALL_SHARDED = ("meta_tokens", "ev_conv_w")
COL_SHARD_GRADS = ("od_w_in", "mlp_w1_0", "mlp_w1_1")
MATRIX_GROUPS = (("ev_w_in", "ev_w_uq", "ev_w_ukv", "ev_w_out"), ("mlp_w1_0", "mlp_w2_0"), ("od_w_in", "od_w_out", "mlp_w1_1", "mlp_w2_1"))
LAYER_NORMS = ("ln_mix_g", "ln_mix_b", "ln_mlp_g", "ln_mlp_b")
WEIGHT_NAMES = ("meta_tokens", "ev_w_in", "ev_conv_w", "ev_conv_b", "ev_w_rg_a", "ev_b_rg_a", "ev_w_rg_x", "ev_b_rg_x",
                "ev_lru_lambda", "ev_q_norm_g", "ev_w_uq", "ev_kv_norm_g", "ev_w_ukv", "ev_w_out", "od_w_in", "od_w_out",
                "ln_mix_g", "ln_mix_b", "mlp_w1", "mlp_w2", "ln_mlp_g", "ln_mlp_b")


def _to_rows(flat, row_align):
    n = flat.shape[-1]
    rows = _round_up(-(-n // PACK_COLS), row_align)
    pad = rows * PACK_COLS - n
    if pad:
        flat = jnp.pad(flat, [(0, 0)] * (flat.ndim - 1) + [(0, pad)])
    return flat.reshape(flat.shape[:-1] + (rows, PACK_COLS))


def _shard_shape(K, N, axis):
    return (K // N_CHIPS, N) if axis == 0 else (K, N // N_CHIPS)


def _gather_shards(stacked, K, N, axis):
    if axis == 0:
        return stacked.reshape(K, N)
    return stacked.transpose(1, 0, 2).reshape(K, N)


def _split_shards(full, K, N, axis):
    if axis == 0:
        return full.reshape(N_CHIPS, -1)
    return full.reshape(K, N_CHIPS, N // N_CHIPS).transpose(1, 0, 2).reshape(N_CHIPS, -1)


def kernel(x, meta_tokens, ev_w_in, ev_conv_w, ev_conv_b, ev_w_rg_a, ev_b_rg_a, ev_w_rg_x, ev_b_rg_x, ev_lru_lambda, ev_q_norm_g, ev_w_uq, ev_kv_norm_g, ev_w_ukv, ev_w_out, od_w_in, od_w_out, ln_mix_g, ln_mix_b, mlp_w1, mlp_w2, ln_mlp_g, ln_mlp_b, loss_target, m_meta_tokens, m_ev_w_in, m_ev_conv_w, m_ev_conv_b, m_ev_w_rg_a, m_ev_b_rg_a, m_ev_w_rg_x, m_ev_b_rg_x, m_ev_lru_lambda, m_ev_q_norm_g, m_ev_w_uq, m_ev_kv_norm_g, m_ev_w_ukv, m_ev_w_out, m_od_w_in, m_od_w_out, m_ln_mix_g, m_ln_mix_b, m_mlp_w1, m_mlp_w2, m_ln_mlp_g, m_ln_mlp_b, v_meta_tokens, v_ev_w_in, v_ev_conv_w, v_ev_conv_b, v_ev_w_rg_a, v_ev_b_rg_a, v_ev_w_rg_x, v_ev_b_rg_x, v_ev_lru_lambda, v_ev_q_norm_g, v_ev_w_uq, v_ev_kv_norm_g, v_ev_w_ukv, v_ev_w_out, v_od_w_in, v_od_w_out, v_ln_mix_g, v_ln_mix_b, v_mlp_w1, v_mlp_w2, v_ln_mlp_g, v_ln_mlp_b):
    given = dict(locals())
    local_big = {"ev_w_in": ev_w_in[0], "ev_w_uq": ev_w_uq[0], "ev_w_ukv": ev_w_ukv[0], "ev_w_out": ev_w_out[0],
                 "od_w_in": od_w_in[0], "od_w_out": od_w_out[0], "mlp_w1_0": mlp_w1[0], "mlp_w1_1": mlp_w1[1],
                 "mlp_w2_0": mlp_w2[0], "mlp_w2_1": mlp_w2[1]}

    specs = {spec[0]: spec for spec in BIG_SPECS}
    mixer0_m, mlp0_m, layer1_m = MATRIX_GROUPS

    def shards(names):
        return [local_big[n].astype(MXU_DTYPE) for n in names]

    def whole(stacked, n):
        _, K, N, ax = specs[n]
        return stacked if n in COL_SHARD_GRADS else _gather_shards(stacked, K, N, ax)

    def filled(gathered, own, names):
        return {n: whole(_with_own(g_, o_), n) for n, g_, o_ in zip(names, gathered, own)}

    own_a, own_b, own_c = shards(mixer0_m), shards(mlp0_m), shards(layer1_m)
    small = [meta_tokens, jnp.pad(ev_conv_w[0], ((0, 16 - CONV_WIDTH), (0, 0)))]
    gathered_a = _allgather_chips(own_a + small, "weight_allgather_mixer0")
    pending_b, token1 = _allgather_start(own_b, "weight_allgather_mlp0_start")
    pending_c, token2 = _allgather_start(own_c, "weight_allgather_layer1_start")
    meta_full = _gather_shards(_with_own(gathered_a[-2], small[0]), N_META, D_MODEL, 1)
    conv_full = _gather_shards(_with_own(gathered_a[-1], small[1])[:, :CONV_WIDTH], CONV_WIDTH, LRU_WIDTH, 1)

    def slots(names, dtype):
        return {n: jnp.zeros((N_CHIPS, specs[n][1], specs[n][2] // N_CHIPS) if n in COL_SHARD_GRADS else specs[n][1:3], dtype) for n in names}

    def norms(names, layer):
        return {n: given[n][layer] for n in names}

    def finish_gather(pending, own, after, names, tag):
        landed = _allgather_wait(pending, lax.stop_gradient(after), "weight_allgather_%s_wait" % tag)
        return filled(_sibling_forward(landed, "weight_allgather_%s_forward" % tag), own, names)

    diff_a = {**slots(mixer0_m, MXU_DTYPE), **norms(("ln_mix_g", "ln_mix_b"), 0), **{n: given[n] for n in REPLICATED if n not in LAYER_NORMS},
              "x": x, "meta_tokens": meta_full, "ev_conv_w": conv_full}
    diff_b = {**slots(mlp0_m, MXU_DTYPE), **norms(("ln_mlp_g", "ln_mlp_b"), 0)}
    diff_c = {**slots(layer1_m, MXU_DTYPE), **norms(LAYER_NORMS, 1)}
    w_a = filled(gathered_a[:len(mixer0_m)], own_a, mixer0_m)
    h_a, back_a = jax.vjp(lambda d: _mixer0(d, w_a, token1 + token2), diff_a)
    w_b = finish_gather(pending_b, own_b, h_a, mlp0_m, "mlp0")
    h_b, back_b = jax.vjp(lambda d, hh: _mlp0(d, hh, w_b), diff_b, h_a)
    w_c = finish_gather(pending_c, own_c, h_b, layer1_m, "layer1")
    loss, back_c = jax.vjp(lambda d, hh: _layer1_loss(d, hh, w_c, loss_target), diff_c, h_b)
    loss = lax.psum(loss, ("x", "y", "c"))

    def blocks_of(grad, n):
        _, K, N, ax = specs[n]
        if n in COL_SHARD_GRADS:
            blocks = grad
        elif ax == 0:
            blocks = grad.reshape(N_CHIPS, K // N_CHIPS, N)
        else:
            blocks = grad.reshape(K, N_CHIPS, N // N_CHIPS).transpose(1, 0, 2)
        return blocks.reshape(N_CHIPS, 2, blocks.shape[1] // 2, blocks.shape[2])

    def start_reduce(grads_of, names, tag, dh):
        flying, (dh,) = _direct_scatter_start([blocks_of(grads_of[n], n) for n in names], "grad_scatter_%s_start" % tag, [dh])
        return flying, dh

    g_c, dh = back_c(jnp.ones((), F32))
    flying_c, dh = start_reduce(g_c, layer1_m, "layer1", dh)
    g_b, dh = back_b(dh)
    flying_b, dh = start_reduce(g_b, mlp0_m, "mlp0", dh)
    (g_a,) = back_a(dh)

    g = {**g_a, **g_b, **g_c}
    g.update({n: jnp.stack([(g_b if n in g_b else g_a)[n], g_c[n]]) for n in LAYER_NORMS})
    repl = jnp.concatenate([g[n].reshape(-1) for n in REPLICATED]).reshape(N_CHIPS, -1)
    small = [_split_shards(g["meta_tokens"], N_META, D_MODEL, 1), _split_shards(g["ev_conv_w"], CONV_WIDTH, LRU_WIDTH, 1), repl]
    small = [pc.reshape(N_CHIPS, 2, -1) for pc in small]
    n_small = sum(pc.shape[2] for pc in small)
    small.append(jnp.zeros((N_CHIPS, 2, _round_up(n_small, 32 * PACK_COLS) - n_small), F32))
    p_small = jnp.concatenate(small, axis=2).reshape(N_CHIPS, 2, -1, PACK_COLS)
    flying_a, _ = _direct_scatter_start([blocks_of(g_a[n], n) for n in mixer0_m] + [p_small], "grad_scatter_mixer0_start")
    started = [g_a["x"], flying_a[2][0]]
    ps_c, ts_c = _direct_scatter_wait(flying_c, started, "grad_scatter_layer1_wait")
    ps_b, ts_b = _direct_scatter_wait(flying_b, started, "grad_scatter_mlp0_wait")
    fs_bc = [_sum_direct(p, t, "grad_sum_%d" % i) for i, (p, t) in enumerate(zip(ps_b + ps_c, ts_b + ts_c))]
    red_big = dict(zip(mlp0_m + layer1_m, _sibling_gather(fs_bc, "grad_sibling_gather")))

    grads, delta, new_m, new_v = {}, {}, {}, {}

    def update_big(names):
        done = []
        for name, parts in BIG_PARAMS:
            if parts[0] in names:
                shp = given[name].shape
                two_d = (-1, shp[-1])
                grads[name] = jnp.stack([red_big[part].reshape(shp[1:]) for part in parts])
                d, nm, nv = _adamw(given[name].reshape(two_d), grads[name].reshape(two_d), given["m_" + name].reshape(two_d),
                                   given["v_" + name].reshape(two_d), "adamw_" + name)
                delta[name], new_m[name], new_v[name] = d.reshape(shp), nm.reshape(shp), nv.reshape(shp)
                done.append(nv)
        return done

    updated = update_big(mlp0_m + layer1_m)
    ps_a, ts_a = _direct_scatter_wait(flying_a, updated, "grad_scatter_mixer0_wait")
    fs_a = [_sum_direct(p, t, "grad_sum_mixer0_%d" % i) for i, (p, t) in enumerate(zip(ps_a, ts_a))]
    reduced_a = _sibling_gather(fs_a, "grad_sibling_gather_mixer0")
    red_big.update(zip(mixer0_m, reduced_a))
    red_small = reduced_a[-1].reshape(2, -1)
    update_big(mixer0_m)

    def take(off, sz):
        return jnp.concatenate([red_small[0, off // 2:(off + sz) // 2], red_small[1, off // 2:(off + sz) // 2]])

    off = 0
    for name in SMALL_SHARDED:
        sz = given[name].size
        grads[name] = take(off, sz).reshape(given[name].shape)
        off += sz
    n_repl = repl.shape[1]
    own_repl = _to_rows(take(off, n_repl), 16)
    repl_all = _with_own(_allgather_chips([own_repl], "replicated_allgather")[0], own_repl).reshape(N_CHIPS, -1)[:, :n_repl].reshape(-1)
    off = 0
    for name in REPLICATED:
        sz = given[name].size
        grads[name] = repl_all[off:off + sz].reshape(given[name].shape)
        off += sz

    smalls = SMALL_SHARDED + REPLICATED

    def pack_small(get):
        return _to_rows(jnp.concatenate([get(n).reshape(-1) for n in smalls]), 8)

    outs = _adamw(pack_small(lambda n: given[n]), pack_small(lambda n: grads[n]), pack_small(lambda n: given["m_" + n]),
                  pack_small(lambda n: given["v_" + n]), "adamw_small")
    for res, flat in zip((delta, new_m, new_v), outs):
        flat, off = flat.reshape(-1), 0
        for n in smalls:
            sz = given[n].size
            res[n] = flat[off:off + sz].reshape(given[n].shape)
            off += sz

    return (loss, g_a["x"], *[grads[n] for n in WEIGHT_NAMES], *[delta[n] for n in WEIGHT_NAMES],
            *[new_m[n] for n in WEIGHT_NAMES], *[new_v[n] for n in WEIGHT_NAMES])
```

```python
import functools
import math

import jax
import jax.numpy as jnp
import numpy as np
from jax import lax
from jax.experimental import pallas as pl
from jax.experimental.pallas import tpu as pltpu

F32 = jnp.float32
MXU_DTYPE = jnp.bfloat16

D_MODEL = 1024
N_META = 16
LRU_WIDTH = 512
LRU_HEADS = 4
LRU_HEAD_DIM = 128
CONV_WIDTH = 4
LRU_C = 8.0
MLA_HEADS = 8
MLA_NOPE = 64
MLA_ROPE = 32
MLA_V = 64
MLA_Q_RANK = 256
MLA_KV_RANK = 128
RET_HEADS = 4
RET_QK_DIM = 256
RET_V_DIM = 512
D_FF = 4096
ROPE_BASE = 10000.0
DN_ALPHA = 4.0 ** 0.25
EPS = 1e-5
NEG_INF = -1e30
SEQ_BLOCK = 128

ADAM_LR = 0.001
ADAM_B1 = 0.9
ADAM_B2 = 0.999
ADAM_EPS = 1e-08
ADAM_WD = 0.01
ADAM_STEP = 10

PACK_COLS = 1024
TN_INPUT_VMEM_BYTES = 28 << 20
N_CHIPS = 4

MESH = pl.DeviceIdType.MESH


def _pick(n, target, align):
    best = None
    for t in range(align, min(n, target) + 1, align):
        if n % t == 0:
            best = t
    return n if best is None else best


def _round_up(n, m):
    return (n + m - 1) // m * m


def _relu2(a):
    r = jnp.maximum(a, 0.0)
    return r * r


def _ln_stats(z):
    mu = jnp.mean(z, axis=-1, keepdims=True)
    zc = z - mu
    var = jnp.mean(zc * zc, axis=-1, keepdims=True)
    return zc, lax.rsqrt(var + EPS)


def _mm_nn(a, w, act, name, out_dtype=F32, norm=None):
    M, K = a.shape
    sharded = w.ndim == 3
    n = w.shape[-1]
    N = n * (w.shape[0] if sharded else 1)
    narrow_out = jnp.dtype(out_dtype).itemsize < 4
    tm = _pick(M, (2176 if narrow_out else 1088) if K * a.dtype.itemsize <= 4096 and norm is None else 544, 8)
    tn = _pick(n, 1024, 128)
    per = n // tn
    assert norm is None or tn == N

    def body(a_ref, w_ref, *rest):
        av = a_ref[...]
        if act:
            av = _relu2(av.astype(F32))
        r = jnp.dot(av.astype(MXU_DTYPE), w_ref[...].astype(MXU_DTYPE), preferred_element_type=F32)
        if norm is None:
            rest[0][...] = r.astype(out_dtype)
        else:
            r_ref, g_ref, b_ref, o_ref, z_ref = rest
            z = DN_ALPHA * r_ref[...] + r
            zc, rstd = _ln_stats(z)
            z_ref[...] = z
            o_ref[...] = zc * rstd * g_ref[...] + b_ref[...]

    w_spec = pl.BlockSpec((None, K, tn), lambda i, j: (j // per, 0, j % per)) if sharded else pl.BlockSpec((K, tn), lambda i, j: (0, j))
    tile = pl.BlockSpec((tm, tn), lambda i, j: (i, j))
    in_specs, args = [pl.BlockSpec((tm, K), lambda i, j: (i, 0)), w_spec], [a, w]
    if norm is None:
        out_specs, out_shape = tile, jax.ShapeDtypeStruct((M, N), out_dtype)
    else:
        vec = pl.BlockSpec((1, N), lambda i, j: (0, 0))
        in_specs += [tile, vec, vec]
        args += [norm[0], norm[1].reshape(1, N), norm[2].reshape(1, N)]
        out_specs, out_shape = [tile, tile], [jax.ShapeDtypeStruct((M, N), F32)] * 2
    return pl.pallas_call(
        body, name=name, grid=(M // tm, N // tn), in_specs=in_specs, out_specs=out_specs, out_shape=out_shape,
        compiler_params=pltpu.CompilerParams(dimension_semantics=("parallel", "arbitrary")),
    )(*args)


def _mm_nt(g, w, a_src, name, out_dtype=F32, plus=None):
    M, N = g.shape
    sharded = w.ndim == 3
    K, n = w.shape[-2], w.shape[-1]
    if sharded:
        tk, nk = N, 1
    else:
        tk = N if N * g.dtype.itemsize <= 8192 else _pick(N, 2048, 128)
        nk = N // tk
    tm = _pick(M, 1088 if tk * g.dtype.itemsize <= 4096 else 544, 8)
    tn = _pick(K, 1024, 128)
    has_src = a_src is not None
    assert nk == 1 or out_dtype == F32
    assert plus is None or not has_src

    def body(*refs):
        if has_src:
            g_ref, w_ref, s_ref, o_ref = refs
        elif plus is not None:
            g_ref, w_ref, p_ref, o_ref = refs
        else:
            g_ref, w_ref, o_ref = refs
        nt = (((1,), (1,)), ((), ()))
        if sharded:
            r = sum(lax.dot_general(g_ref[:, s * n:(s + 1) * n].astype(MXU_DTYPE), w_ref[s].astype(MXU_DTYPE), nt, preferred_element_type=F32)
                    for s in range(w_ref.shape[0]))
        else:
            r = lax.dot_general(g_ref[...].astype(MXU_DTYPE), w_ref[...].astype(MXU_DTYPE), nt, preferred_element_type=F32)
        if has_src:
            r = r * (2.0 * jnp.maximum(s_ref[...].astype(F32), 0.0))
        first = r if plus is None else r + DN_ALPHA * p_ref[...]
        if nk == 1:
            o_ref[...] = first.astype(out_dtype)
        else:
            k = pl.program_id(2)

            @pl.when(k == 0)
            def _():
                o_ref[...] = first

            @pl.when(k > 0)
            def _():
                o_ref[...] += r

    w_spec = (pl.BlockSpec((w.shape[0], tn, n), lambda i, j, k: (0, j, 0)) if sharded
              else pl.BlockSpec((tn, tk), lambda i, j, k: (j, k)))
    in_specs = [pl.BlockSpec((tm, tk), lambda i, j, k: (i, k)), w_spec]
    args = [g, w]
    if has_src:
        assert nk == 1
        in_specs.append(pl.BlockSpec((tm, tn), lambda i, j, k: (i, j)))
        args.append(a_src)
    if plus is not None:
        in_specs.append(pl.BlockSpec((tm, tn), lambda i, j, k: (i, j)))
        args.append(plus)
    return pl.pallas_call(
        body, name=name,
        grid=(M // tm, K // tn, nk),
        in_specs=in_specs,
        out_specs=pl.BlockSpec((tm, tn), lambda i, j, k: (i, j)),
        out_shape=jax.ShapeDtypeStruct((M, K), out_dtype),
        compiler_params=pltpu.CompilerParams(dimension_semantics=("parallel", "parallel", "arbitrary")),
    )(*args)


def _mm_tn(a, g, act, name, col_shards=1, out_dtype=F32):
    M, K = a.shape
    _, N = g.shape
    n = N // col_shards
    tm, tn = _pick(K, 1024, 128), _pick(n, 1024, 128)
    row_bytes = tm * a.dtype.itemsize + tn * g.dtype.itemsize
    tk = _pick(M, min(2176, TN_INPUT_VMEM_BYTES // (2 * row_bytes)), 8)
    nk = M // tk
    per = n // tn
    direct = out_dtype == F32

    def body(a_ref, g_ref, o_ref, *scratch):
        acc_ref = o_ref if direct else scratch[0]
        k = pl.program_id(2)
        av = a_ref[...]
        if act:
            av = _relu2(av.astype(F32))
        r = lax.dot_general(av.astype(MXU_DTYPE), g_ref[...].astype(MXU_DTYPE),
                            (((0,), (0,)), ((), ())), preferred_element_type=F32)

        @pl.when(k == 0)
        def _():
            acc_ref[...] = r

        @pl.when(k > 0)
        def _():
            acc_ref[...] += r

        if not direct:
            @pl.when(k == nk - 1)
            def _():
                o_ref[...] = acc_ref[...].astype(out_dtype)

    if col_shards == 1:
        out_spec, out_shape = pl.BlockSpec((tm, tn), lambda i, j, k: (i, j)), (K, N)
    else:
        out_spec, out_shape = pl.BlockSpec((None, tm, tn), lambda i, j, k: (j // per, i, j % per)), (col_shards, K, n)
    return pl.pallas_call(
        body, name=name,
        grid=(K // tm, N // tn, nk),
        in_specs=[pl.BlockSpec((tk, tm), lambda i, j, k: (k, i)), pl.BlockSpec((tk, tn), lambda i, j, k: (k, j))],
        out_specs=out_spec,
        out_shape=jax.ShapeDtypeStruct(out_shape, out_dtype),
        scratch_shapes=[] if direct else [pltpu.VMEM((tm, tn), F32)],
        compiler_params=pltpu.CompilerParams(dimension_semantics=("parallel", "parallel", "arbitrary")),
    )(a, g)


@functools.partial(jax.custom_vjp, nondiff_argnums=(3, 4, 5, 6))
def matmul(a, w, w_grad_slot, act, name, out_dtype, col_shards):
    return _mm_nn(a, w, act, name + "_fwd", out_dtype)


def _matmul_fwd(a, w, w_grad_slot, act, name, out_dtype, col_shards):
    return _mm_nn(a, w, act, name + "_fwd", out_dtype), (a, w, jnp.zeros((), w_grad_slot.dtype))


def _matmul_bwd(act, name, out_dtype, col_shards, res, g):
    a, w, slot_like = res
    w_grad_dtype = slot_like.dtype
    da = _mm_nt(g, w, a if act else None, name + "_dx")
    dw = _mm_tn(a, g, act, name + "_dw", col_shards, w_grad_dtype)
    return da, None, dw


matmul.defvjp(_matmul_fwd, _matmul_bwd)


def _ln_bwd_call(z, g, dy, name):
    M, D = z.shape
    tm = _pick(M, 544, 8)

    def body(z_ref, g_ref, dy_ref, dz_ref, dg_ref, db_ref):
        @pl.when(pl.program_id(0) == 0)
        def _():
            dg_ref[...] = jnp.zeros_like(dg_ref)
            db_ref[...] = jnp.zeros_like(db_ref)

        zc, rstd = _ln_stats(z_ref[...])
        xhat = zc * rstd
        dy = dy_ref[...]
        dxh = dy * g_ref[...]
        m1 = jnp.mean(dxh, axis=-1, keepdims=True)
        m2 = jnp.mean(dxh * xhat, axis=-1, keepdims=True)
        dz_ref[...] = rstd * (dxh - m1 - xhat * m2)
        dg_ref[...] += jnp.sum(dy * xhat, axis=0, keepdims=True)
        db_ref[...] += jnp.sum(dy, axis=0, keepdims=True)

    row = pl.BlockSpec((tm, D), lambda i: (i, 0))
    vec = pl.BlockSpec((1, D), lambda i: (0, 0))
    return pl.pallas_call(
        body, name=name, grid=(M // tm,), in_specs=[row, vec, row], out_specs=[row, vec, vec],
        out_shape=[jax.ShapeDtypeStruct((M, D), F32), jax.ShapeDtypeStruct((1, D), F32), jax.ShapeDtypeStruct((1, D), F32)],
        compiler_params=pltpu.CompilerParams(dimension_semantics=("arbitrary",)),
    )(z, g.reshape(1, D), dy)


@functools.partial(jax.custom_vjp, nondiff_argnums=(7,))
def mlp_block(h, w1, w2, w1_grad_slot, w2_grad_slot, g, b, name):
    return _mlp_block_fwd(h, w1, w2, w1_grad_slot, w2_grad_slot, g, b, name)[0]


def _mlp_block_fwd(h, w1, w2, w1_grad_slot, w2_grad_slot, g, b, name):
    u = _mm_nn(h, w1, False, name + "_w1_fwd", out_dtype=MXU_DTYPE)
    out, z = _mm_nn(u, w2, True, name + "_w2_norm_fwd", norm=(h, g, b))
    return out, (h, u, z, w1, w2, g, jnp.zeros((), w1_grad_slot.dtype))


def _mlp_block_bwd(name, res, dy):
    h, u, z, w1, w2, g, slot_like = res
    dz, dg, db = _ln_bwd_call(z, g, dy, name + "_norm_bwd")
    du = _mm_nt(dz, w2, u, name + "_w2_dx", out_dtype=MXU_DTYPE)
    dw2 = _mm_tn(u, dz, True, name + "_w2_dw", 1, slot_like.dtype)
    dh = _mm_nt(du, w1, None, name + "_w1_dx", plus=dz)
    dw1 = _mm_tn(h, du, False, name + "_w1_dw", N_CHIPS, slot_like.dtype)
    return dh, None, None, dw1, dw2, dg.reshape(g.shape), db.reshape(g.shape)


mlp_block.defvjp(_mlp_block_fwd, _mlp_block_bwd)


@functools.partial(jax.custom_vjp, nondiff_argnums=(6,))
def out_block(h, y, w, w_grad_slot, g, b, name):
    return _out_block_fwd(h, y, w, w_grad_slot, g, b, name)[0]


def _out_block_fwd(h, y, w, w_grad_slot, g, b, name):
    out, z = _mm_nn(y, w, False, name + "_norm_fwd", norm=(h, g, b))
    return out, (y, z, w, g, jnp.zeros((), w_grad_slot.dtype))


def _out_block_bwd(name, res, dy):
    y, z, w, g, slot_like = res
    dz, dg, db = _ln_bwd_call(z, g, dy, name + "_norm_bwd")
    d_y = _mm_nt(dz, w, None, name + "_dx")
    dw = _mm_tn(y, dz, False, name + "_dw", 1, slot_like.dtype)
    return DN_ALPHA * dz, d_y, None, dw, dg.reshape(g.shape), db.reshape(g.shape)


out_block.defvjp(_out_block_fwd, _out_block_bwd)


def _rms_fwd_call(x, g, name, col_block=0):
    R = x.shape[0]
    W = g.shape[-1]
    tr = _pick(R, 1088, 8)

    def body(x_ref, g_ref, o_ref):
        xv = x_ref[...]
        rstd = lax.rsqrt(jnp.mean(xv * xv, axis=-1, keepdims=True) + EPS)
        o_ref[...] = xv * rstd * g_ref[...]

    vec = pl.BlockSpec((1, W), lambda i: (0, 0))
    return pl.pallas_call(
        body, name=name, grid=(R // tr,), in_specs=[pl.BlockSpec((tr, W), lambda i: (i, col_block)), vec],
        out_specs=pl.BlockSpec((tr, W), lambda i: (i, 0)), out_shape=jax.ShapeDtypeStruct((R, W), F32),
        compiler_params=pltpu.CompilerParams(dimension_semantics=("parallel",)),
    )(x, g.reshape(1, W))


def _rms_bwd_call(x, g, dy, name, col_block=0):
    R = x.shape[0]
    W = g.shape[-1]
    tr = _pick(R, 1088, 8)

    def body(x_ref, g_ref, dy_ref, dx_ref, dg_ref):
        @pl.when(pl.program_id(0) == 0)
        def _():
            dg_ref[...] = jnp.zeros_like(dg_ref)

        xv = x_ref[...]
        rstd = lax.rsqrt(jnp.mean(xv * xv, axis=-1, keepdims=True) + EPS)
        xhat = xv * rstd
        dy = dy_ref[...]
        dxh = dy * g_ref[...]
        dx_ref[...] = rstd * (dxh - xhat * jnp.mean(dxh * xhat, axis=-1, keepdims=True))
        dg_ref[...] += jnp.sum(dy * xhat, axis=0, keepdims=True)

    row = pl.BlockSpec((tr, W), lambda i: (i, 0))
    vec = pl.BlockSpec((1, W), lambda i: (0, 0))
    return pl.pallas_call(
        body, name=name, grid=(R // tr,), in_specs=[pl.BlockSpec((tr, W), lambda i: (i, col_block)), vec, row], out_specs=[row, vec],
        out_shape=[jax.ShapeDtypeStruct((R, W), F32), jax.ShapeDtypeStruct((1, W), F32)],
        compiler_params=pltpu.CompilerParams(dimension_semantics=("arbitrary",)),
    )(x, g.reshape(1, W), dy)


def _loss_call(h, tgt, n_tokens, name):
    B, Tp, D = h.shape
    tr = _pick(Tp, 544, 8)

    def body(y_ref, t_ref, dy_ref, acc_ref):
        @pl.when(jnp.logical_and(pl.program_id(0) == 0, pl.program_id(1) == 0))
        def _():
            acc_ref[...] = jnp.zeros_like(acc_ref)

        t = lax.broadcasted_iota(jnp.int32, (tr, 1), 0) + pl.program_id(1) * tr
        counts = jnp.logical_and(t >= N_META, t < N_META + n_tokens)
        e = jnp.where(counts, y_ref[...] - t_ref[...], 0.0)
        dy_ref[...] = e * (1.0 / D)
        acc_ref[...] += jnp.sum(jnp.sum(e * e, axis=-1, keepdims=True), axis=0, keepdims=True) * (0.5 / D)

    row = pl.BlockSpec((None, tr, D), lambda b, i: (b, i, 0))
    one = pl.BlockSpec((1, 1), lambda b, i: (0, 0))
    return pl.pallas_call(
        body, name=name, grid=(B, Tp // tr), in_specs=[row, row], out_specs=[row, one],
        out_shape=[jax.ShapeDtypeStruct((B, Tp, D), F32), jax.ShapeDtypeStruct((1, 1), F32)],
        compiler_params=pltpu.CompilerParams(dimension_semantics=("arbitrary", "arbitrary")),
    )(h, tgt)


@functools.partial(jax.custom_vjp, nondiff_argnums=(2,))
def loss_head(h, tgt, n_tokens):
    return _loss_call(h, tgt, n_tokens, "loss_head")[1][0, 0]


def _loss_head_fwd(h, tgt, n_tokens):
    dy, acc = _loss_call(h, tgt, n_tokens, "loss_head")
    return acc[0, 0], dy


def _loss_head_bwd(n_tokens, dy, ct):
    return ct * dy, None


loss_head.defvjp(_loss_head_fwd, _loss_head_bwd)


_GELU_C = math.sqrt(2.0 / math.pi)


def _gelu_parts(x):
    x2 = x * x
    t = jnp.tanh(_GELU_C * (x + 0.044715 * x * x2))
    gelu = 0.5 * x * (1.0 + t)
    dgelu = 0.5 * (1.0 + t) + 0.5 * x * (1.0 - t * t) * (_GELU_C * (1.0 + 3.0 * 0.044715 * x2))
    return gelu, dgelu


def _sigmoid(x):
    return 1.0 / (1.0 + jnp.exp(-x))


def _scan8(a, b, carry, reverse):
    row = lax.broadcasted_iota(jnp.int32, a.shape, 0)
    for s in (1, 2, 4):
        shift = 8 - s if reverse else s
        keep = (row < 8 - s) if reverse else (row >= s)
        b = jnp.where(keep, a * pltpu.roll(b, shift, 0) + b, b)
        a = jnp.where(keep, a * pltpu.roll(a, shift, 0), a)
    return a * carry + b


def _lru_pre(prec_ref, prev_ref, first, cw_ref, cb_ref, wa_ref, ba_ref, wx_ref, bx_ref, sp_ref):
    tc = prec_ref.shape[0]
    prev = jnp.where(first, 0.0, prev_ref[...])
    ext = jnp.concatenate([prev, prec_ref[...]], axis=0)
    cw = cw_ref[...]
    taps = [ext[8:] if k == CONV_WIDTH - 1 else pltpu.roll(ext, CONV_WIDTH - 1 - k, 0)[8:] for k in range(CONV_WIDTH)]
    xc = cb_ref[...] + sum(cw[k:k + 1, :] * taps[k] for k in range(CONV_WIDTH))
    ga, gx = [], []
    for h in range(LRU_HEADS):
        xh = xc[:, h * LRU_HEAD_DIM:(h + 1) * LRU_HEAD_DIM].astype(MXU_DTYPE)
        ga.append(jnp.dot(xh, wa_ref[h].astype(MXU_DTYPE), preferred_element_type=F32))
        gx.append(jnp.dot(xh, wx_ref[h].astype(MXU_DTYPE), preferred_element_type=F32))
    r = _sigmoid(jnp.concatenate(ga, axis=1) + ba_ref[...])
    i = _sigmoid(jnp.concatenate(gx, axis=1) + bx_ref[...])
    log_a = -LRU_C * r * sp_ref[...]
    a = jnp.exp(log_a)
    a2 = a * a
    mult = jnp.sqrt(-jnp.tanh(log_a) * (a2 + 1.0))
    return taps, xc, r, i, a, a2, mult


def _lru_fwd_call(p, cw, cb, wa, ba, wx, bx, sp):
    B, Tp, _ = p.shape
    W = LRU_WIDTH
    tc = SEQ_BLOCK
    nc = Tp // tc

    def body(pg_ref, prec_ref, prev_ref, cw_ref, cb_ref, wa_ref, ba_ref, wx_ref, bx_ref, sp_ref, y_ref, h_ref, carry_ref):
        first = pl.program_id(1) == 0

        @pl.when(first)
        def _():
            carry_ref[...] = jnp.zeros_like(carry_ref)

        _, xc, r, i, a, a2, mult = _lru_pre(prec_ref, prev_ref, first, cw_ref, cb_ref, wa_ref, ba_ref, wx_ref, bx_ref, sp_ref)
        b = mult * (i * xc)
        carry = carry_ref[0:1, :]
        for t in range(tc // 8):
            h = _scan8(a[8 * t:8 * t + 8], b[8 * t:8 * t + 8], carry, False)
            h_ref[8 * t:8 * t + 8, :] = h
            carry = h[7:8, :]
        carry_ref[...] = jnp.broadcast_to(carry, carry_ref.shape)
        y_ref[...] = h_ref[...] * _gelu_parts(pg_ref[...])[0]

    cur = pl.BlockSpec((None, tc, W), lambda b, j: (b, j, 0))
    rec = pl.BlockSpec((None, tc, W), lambda b, j: (b, j, 1))
    prev = pl.BlockSpec((None, 8, W), lambda b, j: (b, jnp.maximum(j * (tc // 8) - 1, 0), 1))
    vec = pl.BlockSpec((1, W), lambda b, j: (0, 0))
    cws = pl.BlockSpec((CONV_WIDTH, W), lambda b, j: (0, 0))
    wsp = pl.BlockSpec((LRU_HEADS, LRU_HEAD_DIM, LRU_HEAD_DIM), lambda b, j: (0, 0, 0))
    return pl.pallas_call(
        body, name="lru_fwd", grid=(B, nc),
        in_specs=[cur, rec, prev, cws, vec, wsp, vec, wsp, vec, vec],
        out_specs=[cur, cur],
        out_shape=[jax.ShapeDtypeStruct((B, Tp, W + MLA_HEADS * HEAD_LANES), F32), jax.ShapeDtypeStruct((B, Tp, W), F32)],
        scratch_shapes=[pltpu.VMEM((8, W), F32)],
        compiler_params=pltpu.CompilerParams(dimension_semantics=("arbitrary", "arbitrary")),
    )(p, p, p, cw, cb, wa, ba, wx, bx, sp)


def _lru_bwd_call(p, hseq, dy, cw, cb, wa, ba, wx, bx, sp, dpq, dpkv, dkpe):
    B, Tp, P = p.shape
    W = LRU_WIDTH
    tc = SEQ_BLOCK
    nc = Tp // tc
    HD = LRU_HEAD_DIM

    def body(pg_ref, prec_ref, prev_ref, h_ref, hprev_ref, dy_ref, cw_ref, cb_ref, wa_ref, ba_ref, wx_ref, bx_ref, sp_ref,
             dpq_ref, dpkv_ref, dkpe_ref, dp_ref, dcw_ref, dcb_ref, dwa_ref, dba_ref, dwx_ref, dbx_ref, dsp_ref,
             gcar_ref, anext_ref, halo_ref, g_ref):
        j = pl.program_id(1)
        first = j == nc - 1
        last = j == 0

        @pl.when(jnp.logical_and(pl.program_id(0) == 0, last))
        def _():
            for ref in (dcw_ref, dcb_ref, dwa_ref, dba_ref, dwx_ref, dbx_ref, dsp_ref):
                ref[...] = jnp.zeros_like(ref)

        @pl.when(last)
        def _():
            gcar_ref[...] = jnp.zeros_like(gcar_ref)
            anext_ref[...] = jnp.zeros_like(anext_ref)
            halo_ref[...] = jnp.zeros_like(halo_ref)

        taps, xc, r, i, a, a2, mult = _lru_pre(prec_ref, prev_ref, first, cw_ref, cb_ref, wa_ref, ba_ref, wx_ref, bx_ref, sp_ref)
        row = lax.broadcasted_iota(jnp.int32, (tc, W), 0)
        gelu, dgelu = _gelu_parts(pg_ref[...])
        dy = dy_ref[...]
        hcur = h_ref[...]
        dp_ref[:, 0:W] = dy * hcur * dgelu
        dp_ref[:, 2 * W:2 * W + MLA_Q_RANK] = dpq_ref[...]
        dp_ref[:, _KPE_START - MLA_KV_RANK:_KPE_START] = dpkv_ref[...]
        dp_ref[:, _KPE_START:P] = pltpu.roll(dkpe_ref[...], HEAD_LANES - MLA_NOPE, 1)[:, 0:P - _KPE_START]
        dh = dy * gelu
        a_next = jnp.where(row == tc - 1, anext_ref[0:1, :], pltpu.roll(a, tc - 1, 0))
        carry = gcar_ref[0:1, :]
        for t in reversed(range(tc // 8)):
            g = _scan8(a_next[8 * t:8 * t + 8], dh[8 * t:8 * t + 8], carry, True)
            g_ref[8 * t:8 * t + 8, :] = g
            carry = g[0:1, :]
        gcar_ref[...] = jnp.broadcast_to(carry, gcar_ref.shape)
        anext_ref[...] = jnp.broadcast_to(a[0:1, :], anext_ref.shape)
        G = g_ref[...]
        h_before = jnp.where(first, 0.0, hprev_ref[7:8, :])
        hprev = jnp.where(row == 0, h_before, pltpu.roll(hcur, 1, 0))
        d_a = G * hprev
        gx_ = G * xc
        d_mult = gx_ * i
        d_i = gx_ * mult
        dxc = G * (mult * i)
        d_la = d_a * a - d_mult * (a2 / mult)
        sp = sp_ref[...]
        d_r = d_la * (-LRU_C * sp)
        dsp_ref[...] += jnp.sum(d_la * (-LRU_C * r), axis=0, keepdims=True)
        dga = d_r * r * (1.0 - r)
        dgx = d_i * i * (1.0 - i)
        dba_ref[...] += jnp.sum(dga, axis=0, keepdims=True)
        dbx_ref[...] += jnp.sum(dgx, axis=0, keepdims=True)
        back = []
        for h in range(LRU_HEADS):
            sl = slice(h * HD, (h + 1) * HD)
            xh = xc[:, sl].astype(MXU_DTYPE)
            ah = dga[:, sl].astype(MXU_DTYPE)
            bh = dgx[:, sl].astype(MXU_DTYPE)
            tn = (((0,), (0,)), ((), ()))
            nt = (((1,), (1,)), ((), ()))
            dwa_ref[h] += lax.dot_general(xh, ah, tn, preferred_element_type=F32)
            dwx_ref[h] += lax.dot_general(xh, bh, tn, preferred_element_type=F32)
            back.append(lax.dot_general(ah, wa_ref[h].astype(MXU_DTYPE), nt, preferred_element_type=F32)
                        + lax.dot_general(bh, wx_ref[h].astype(MXU_DTYPE), nt, preferred_element_type=F32))
        dxc = dxc + jnp.concatenate(back, axis=1)
        dcb_ref[...] += jnp.sum(dxc, axis=0, keepdims=True)
        for k in range(CONV_WIDTH):
            dcw_ref[k:k + 1, :] += jnp.sum(dxc * taps[k], axis=0, keepdims=True)
        ext = jnp.concatenate([dxc, halo_ref[...]], axis=0)
        cw = cw_ref[...]
        acc = cw[CONV_WIDTH - 1:CONV_WIDTH, :] * dxc
        for k in range(CONV_WIDTH - 1):
            s = CONV_WIDTH - 1 - k
            acc = acc + cw[k:k + 1, :] * pltpu.roll(ext, tc + 8 - s, 0)[:tc]
        dp_ref[:, W:2 * W] = acc
        halo_ref[...] = dxc[0:8, :]

    rev = lambda j: nc - 1 - j
    cur = pl.BlockSpec((None, tc, W), lambda b, j: (b, rev(j), 0))
    rec = pl.BlockSpec((None, tc, W), lambda b, j: (b, rev(j), 1))
    prev = pl.BlockSpec((None, 8, W), lambda b, j: (b, jnp.maximum(rev(j) * (tc // 8) - 1, 0), 0))
    prev_rec = pl.BlockSpec((None, 8, W), lambda b, j: (b, jnp.maximum(rev(j) * (tc // 8) - 1, 0), 1))
    vec = pl.BlockSpec((1, W), lambda b, j: (0, 0))
    cws = pl.BlockSpec((CONV_WIDTH, W), lambda b, j: (0, 0))
    wsp = pl.BlockSpec((LRU_HEADS, HD, HD), lambda b, j: (0, 0, 0))
    vs = jax.ShapeDtypeStruct((1, W), F32)
    ws = jax.ShapeDtypeStruct((LRU_HEADS, HD, HD), F32)

    def rows(width):
        return pl.BlockSpec((None, tc, width), lambda b, j: (b, rev(j), 0))

    return pl.pallas_call(
        body, name="lru_bwd", grid=(B, nc),
        in_specs=[cur, rec, prev_rec, cur, prev, cur, cws, vec, wsp, vec, wsp, vec, vec, rows(MLA_Q_RANK), rows(MLA_KV_RANK), rows(HEAD_LANES)],
        out_specs=[rows(P), cws, vec, wsp, vec, wsp, vec, vec],
        out_shape=[jax.ShapeDtypeStruct((B, Tp, P), F32), jax.ShapeDtypeStruct((CONV_WIDTH, W), F32), vs, ws, vs, ws, vs, vs],
        scratch_shapes=[pltpu.VMEM((8, W), F32), pltpu.VMEM((8, W), F32), pltpu.VMEM((8, W), F32), pltpu.VMEM((tc, W), F32)],
        compiler_params=pltpu.CompilerParams(dimension_semantics=("arbitrary", "arbitrary")),
    )(p, p, p, hseq, hseq, dy, cw, cb, wa, ba, wx, bx, sp, dpq, dpkv, dkpe)


_Q_BLOCK = 2 * LRU_WIDTH // MLA_Q_RANK
_KV_BLOCK = (2 * LRU_WIDTH + MLA_Q_RANK) // MLA_KV_RANK
_KPE_START = 2 * LRU_WIDTH + MLA_Q_RANK + MLA_KV_RANK


@jax.custom_vjp
def even_front(p, cw, cb, wa, ba, wx, bx, sp, gq, gkv):
    return _even_front_fwd(p, cw, cb, wa, ba, wx, bx, sp, gq, gkv)[0]


def _even_front_fwd(p, cw, cb, wa, ba, wx, bx, sp, gq, gkv):
    B, Tp, W = p.shape
    p2d = p.reshape(B * Tp, W)
    y, hseq = _lru_fwd_call(p, cw, cb, wa, ba, wx, bx, sp)
    qn = _rms_fwd_call(p2d, gq, "q_norm_fwd", _Q_BLOCK)
    kvn = _rms_fwd_call(p2d, gkv, "kv_norm_fwd", _KV_BLOCK)
    kpe = jnp.pad(p[:, :, _KPE_START:], ((0, 0), (0, 0), (MLA_NOPE, HEAD_LANES - MLA_NOPE - MLA_ROPE)))
    return (y, qn, kvn, kpe), (p, hseq, cw, cb, wa, ba, wx, bx, sp, gq, gkv)


def _even_front_bwd(res, cts):
    p, hseq, cw, cb, wa, ba, wx, bx, sp, gq, gkv = res
    dy, dqn, dkvn, dkpe = cts
    B, Tp, W = p.shape
    p2d = p.reshape(B * Tp, W)
    dpq, dgq = _rms_bwd_call(p2d, gq, dqn, "q_norm_bwd", _Q_BLOCK)
    dpkv, dgkv = _rms_bwd_call(p2d, gkv, dkvn, "kv_norm_bwd", _KV_BLOCK)
    dp, dcw, dcb, dwa, dba, dwx, dbx, dsp = _lru_bwd_call(p, hseq, dy, cw, cb, wa, ba, wx, bx, sp, dpq.reshape(B, Tp, -1),
                                                          dpkv.reshape(B, Tp, -1), dkpe)
    return dp, dcw, dcb, dwa, dba, dwx, dbx, dsp, dgq.reshape(gq.shape), dgkv.reshape(gkv.shape)


even_front.defvjp(_even_front_fwd, _even_front_bwd)


def _rope_tables(T, half):
    inv = np.float32(ROPE_BASE) ** (-np.arange(half, dtype=np.float32) / np.float32(half))
    ang = np.arange(T, dtype=np.float32)[:, None] * inv[None, :]
    return np.cos(ang), np.sin(ang)


_NT = (((1,), (1,)), ((), ()))
_TN = (((0,), (0,)), ((), ()))
HEAD_LANES = 128
_MLA_SCALE = (MLA_NOPE + MLA_ROPE) ** -0.5
_LOG2E = math.log2(math.e)


Q_BLOCK = 512


def _query_blocks(Tp):
    first = Tp % Q_BLOCK or Q_BLOCK
    return [(0, first)] + [(r, r + Q_BLOCK) for r in range(first, Tp, Q_BLOCK)]


def _mask_diagonal(s, fill):
    R, L = s.shape
    row = lax.broadcasted_iota(jnp.int32, (R, R), 0)
    col = lax.broadcasted_iota(jnp.int32, (R, R), 1)
    last = jnp.where(col <= row, s[:, L - R:], fill)
    return last if L == R else jnp.concatenate([s[:, :L - R], last], axis=1)


def _mla_rope_tables(T):
    half = MLA_ROPE // 2
    cos, sin = _rope_tables(T, half)
    ones, zeros = np.ones((T, MLA_NOPE), np.float32), np.zeros((T, MLA_NOPE), np.float32)
    tail1, tail0 = np.ones((T, HEAD_LANES - MLA_NOPE - MLA_ROPE), np.float32), np.zeros((T, HEAD_LANES - MLA_NOPE - MLA_ROPE), np.float32)
    zh = np.zeros((T, half), np.float32)
    c = np.concatenate([ones, cos, cos, tail1], axis=1)
    s_up = np.concatenate([zeros, -sin, zh, tail0], axis=1)
    s_down = np.concatenate([zeros, zh, sin, tail0], axis=1)
    return jnp.asarray(c), jnp.asarray(s_up), jnp.asarray(s_down)


def _rope_lanes(x, c, s_up, s_down):
    half = MLA_ROPE // 2
    return x * c + pltpu.roll(x, HEAD_LANES - half, 1) * s_up + pltpu.roll(x, half, 1) * s_down


def _unrope_lanes(d, c, s_up, s_down):
    half = MLA_ROPE // 2
    return d * c + pltpu.roll(d * s_up, half, 1) + pltpu.roll(d * s_down, HEAD_LANES - half, 1)


def _mla_operands(q_ref, kv_ref, kpe_ref, c, s_up, s_down):
    lane = lax.broadcasted_iota(jnp.int32, kv_ref.shape, 1)
    qr = (_rope_lanes(q_ref[...].astype(F32), c, s_up, s_down) * (_MLA_SCALE * _LOG2E)).astype(MXU_DTYPE)
    kr = jnp.where(lane < MLA_NOPE, kv_ref[...].astype(F32), _rope_lanes(kpe_ref[...], c, s_up, s_down)).astype(MXU_DTYPE)
    return qr, kr, lane


def _mla_specs(Tp):
    head = pl.BlockSpec((None, Tp, HEAD_LANES), lambda b, h: (b, 0, h))
    shared = pl.BlockSpec((None, Tp, HEAD_LANES), lambda b, h: (b, 0, 0))
    tab = pl.BlockSpec((Tp, HEAD_LANES), lambda b, h: (0, 0))
    lse = pl.BlockSpec((None, None, Tp, 1), lambda b, h: (b, h, 0, 0))
    out = pl.BlockSpec((None, Tp, HEAD_LANES), lambda b, h: (b, 0, LRU_WIDTH // HEAD_LANES + h))
    return head, shared, tab, lse, out


def _attn_fwd_call(q, kv, kpe, tabs, y):
    B, Tp, _ = q.shape

    def body(q_ref, kv_ref, kpe_ref, c_ref, su_ref, sd_ref, y_ref, o_ref, lse_ref, qr_ref, kr_ref):
        qr, kr, lane = _mla_operands(q_ref, kv_ref, kpe_ref, c_ref[...], su_ref[...], sd_ref[...])
        qr_ref[...] = qr
        kr_ref[...] = kr
        for r0, L in _query_blocks(Tp):
            blk = slice(r0, L)
            s = _mask_diagonal(lax.dot_general(qr_ref[blk, :], kr_ref[0:L, :], _NT, preferred_element_type=F32), NEG_INF)
            m = jnp.max(s, axis=-1, keepdims=True)
            p = jnp.exp2(s - m)
            l = jnp.sum(p, axis=-1, keepdims=True)
            o = jnp.dot(p.astype(MXU_DTYPE), kv_ref[0:L, :].astype(MXU_DTYPE), preferred_element_type=F32)
            o_ref[blk, :] = jnp.where(lane[blk, :] >= MLA_NOPE, o / l, 0.0)
            lse_ref[blk, :] = m + jnp.log2(l)

    head, shared, tab, lse, out = _mla_specs(Tp)
    return pl.pallas_call(
        body, name="mla_attn_fwd", grid=(B, MLA_HEADS),
        in_specs=[head, head, shared, tab, tab, tab, pl.BlockSpec(memory_space=pl.ANY)], out_specs=[out, lse],
        out_shape=[jax.ShapeDtypeStruct(y.shape, F32), jax.ShapeDtypeStruct((B, MLA_HEADS, Tp, 1), F32)],
        input_output_aliases={6: 0},
        scratch_shapes=[pltpu.VMEM((Tp, HEAD_LANES), MXU_DTYPE), pltpu.VMEM((Tp, HEAD_LANES), MXU_DTYPE)],
        compiler_params=pltpu.CompilerParams(dimension_semantics=("parallel", "parallel")),
    )(q, kv, kpe, *tabs, y)


def _attn_bwd_call(q, kv, kpe, tabs, o, lse, do):
    B, Tp, _ = q.shape

    def body(q_ref, kv_ref, kpe_ref, c_ref, su_ref, sd_ref, o_ref, lse_ref, do_ref, dq_ref, dkv_ref, dkpe_ref,
             qr_ref, kr_ref, dqa_ref, dka_ref, dva_ref):
        c, s_up, s_down = c_ref[...], su_ref[...], sd_ref[...]
        qr, kr, lane = _mla_operands(q_ref, kv_ref, kpe_ref, c, s_up, s_down)
        qr_ref[...] = qr
        kr_ref[...] = kr
        dka_ref[...] = jnp.zeros_like(dka_ref)
        dva_ref[...] = jnp.zeros_like(dva_ref)
        for r0, L in _query_blocks(Tp):
            blk = slice(r0, L)
            qb = qr_ref[blk, :]
            do = jnp.where(lane[blk, :] >= MLA_NOPE, do_ref[blk, :], 0.0)
            delta = jnp.sum(do * o_ref[blk, :], axis=-1, keepdims=True)
            s = _mask_diagonal(lax.dot_general(qb, kr_ref[0:L, :], _NT, preferred_element_type=F32), NEG_INF)
            p = jnp.exp2(s - lse_ref[blk, :])
            dob = do.astype(MXU_DTYPE)
            dva_ref[0:L, :] += lax.dot_general(p.astype(MXU_DTYPE), dob, _TN, preferred_element_type=F32)
            dp = lax.dot_general(dob, kv_ref[0:L, :].astype(MXU_DTYPE), _NT, preferred_element_type=F32)
            ds = (p * (dp - delta)).astype(MXU_DTYPE)
            dqa_ref[blk, :] = jnp.dot(ds, kr_ref[0:L, :], preferred_element_type=F32)
            dka_ref[0:L, :] += lax.dot_general(ds, qb, _TN, preferred_element_type=F32)
        dq_ref[...] = _unrope_lanes(dqa_ref[...] * _MLA_SCALE, c, s_up, s_down).astype(dq_ref.dtype)
        dk = dka_ref[...] * (1.0 / _LOG2E)
        dkv_ref[...] = jnp.where(lane < MLA_NOPE, dk, dva_ref[...]).astype(dkv_ref.dtype)
        dkpe = jnp.where(lane >= MLA_NOPE, _unrope_lanes(dk, c, s_up, s_down), 0.0)

        @pl.when(pl.program_id(1) == 0)
        def _():
            dkpe_ref[...] = dkpe

        @pl.when(pl.program_id(1) > 0)
        def _():
            dkpe_ref[...] += dkpe

    head, shared, tab, lse_spec, out = _mla_specs(Tp)
    wide = jax.ShapeDtypeStruct((B, Tp, MLA_HEADS * HEAD_LANES), q.dtype)
    acc = pltpu.VMEM((Tp, HEAD_LANES), F32)
    return pl.pallas_call(
        body, name="mla_attn_bwd", grid=(B, MLA_HEADS),
        in_specs=[head, head, shared, tab, tab, tab, out, lse_spec, out], out_specs=[head, head, shared],
        out_shape=[wide, wide, jax.ShapeDtypeStruct((B, Tp, HEAD_LANES), F32)],
        scratch_shapes=[pltpu.VMEM((Tp, HEAD_LANES), MXU_DTYPE), pltpu.VMEM((Tp, HEAD_LANES), MXU_DTYPE), acc, acc, acc],
        compiler_params=pltpu.CompilerParams(dimension_semantics=("parallel", "arbitrary")),
    )(q, kv, kpe, *tabs, o, lse, do)


@jax.custom_vjp
def mla_attention(q, kv, kpe, tabs, y):
    return _attn_fwd_call(q, kv, kpe, tabs, y)[0]


def _mla_attention_fwd(q, kv, kpe, tabs, y):
    o, lse = _attn_fwd_call(q, kv, kpe, tabs, y)
    return o, (q, kv, kpe, tabs, o, lse)


def _mla_attention_bwd(res, do):
    q, kv, kpe, tabs, o, lse = res
    dq, dkv, dkpe = _attn_bwd_call(q, kv, kpe, tabs, o, lse, do)
    return dq, dkv, dkpe, None, do


mla_attention.defvjp(_mla_attention_fwd, _mla_attention_bwd)


def _rope_halves(x, cos, sin):
    half = x.shape[1] // 2
    x1, x2 = x[:, :half], x[:, half:]
    return jnp.concatenate([x1 * cos - x2 * sin, x1 * sin + x2 * cos], axis=1)


def _unrope_halves(d, cos, sin):
    half = d.shape[1] // 2
    d1, d2 = d[:, :half], d[:, half:]
    return jnp.concatenate([d1 * cos + d2 * sin, d2 * cos - d1 * sin], axis=1)


_RET_K_SCALE = RET_QK_DIM ** -0.5
_RET_Q_BLOCKS = RET_HEADS
_RET_V_BLOCK0 = 2 * RET_HEADS * RET_QK_DIM // RET_V_DIM
_RET_G_BLOCK0 = _RET_V_BLOCK0 + RET_HEADS


def _ret_specs(Tp):
    q = pl.BlockSpec((None, Tp, RET_QK_DIM), lambda b, h: (b, 0, h))
    k = pl.BlockSpec((None, Tp, RET_QK_DIM), lambda b, h: (b, 0, _RET_Q_BLOCKS + h))
    v = pl.BlockSpec((None, Tp, RET_V_DIM), lambda b, h: (b, 0, _RET_V_BLOCK0 + h))
    tab = pl.BlockSpec((Tp, RET_QK_DIM // 2), lambda b, h: (0, 0))
    lg = pl.BlockSpec((None, 1, 1), lambda b, h: (h, 0, 0))
    return q, k, v, tab, lg


def _ret_operands(q_ref, k_ref, cos, sin, lg):
    t = lax.broadcasted_iota(jnp.int32, (q_ref.shape[0], 1), 0).astype(F32)
    grow, shrink = jnp.exp(-lg * t), jnp.exp(lg * t)
    qs = (_rope_halves(q_ref[...].astype(F32), cos, sin) * shrink).astype(MXU_DTYPE)
    ks = (_rope_halves(k_ref[...].astype(F32), cos, sin) * (grow * _RET_K_SCALE)).astype(MXU_DTYPE)
    return qs, ks, shrink, grow * _RET_K_SCALE


def _ret_core_fwd_call(p, cos, sin, lg):
    B, Tp, _ = p.shape

    def body(q_ref, k_ref, v_ref, cos_ref, sin_ref, lg_ref, o_ref, qs_ref, ks_ref):
        qs_ref[...], ks_ref[...], _, _ = _ret_operands(q_ref, k_ref, cos_ref[...], sin_ref[...], lg_ref[...])
        for r0, L in _query_blocks(Tp):
            blk = slice(r0, L)
            s = _mask_diagonal(lax.dot_general(qs_ref[blk, :], ks_ref[0:L, :], _NT, preferred_element_type=F32), 0.0)
            o_ref[blk, :] = jnp.dot(s.astype(MXU_DTYPE), v_ref[0:L, :].astype(MXU_DTYPE), preferred_element_type=F32)

    q, k, v, tab, lgs = _ret_specs(Tp)
    return pl.pallas_call(
        body, name="retention_fwd", grid=(B, RET_HEADS), in_specs=[q, k, v, tab, tab, lgs],
        out_specs=pl.BlockSpec((None, Tp, RET_V_DIM), lambda b, h: (b, 0, h)),
        out_shape=jax.ShapeDtypeStruct((B, Tp, RET_HEADS * RET_V_DIM), F32),
        scratch_shapes=[pltpu.VMEM((Tp, RET_QK_DIM), MXU_DTYPE), pltpu.VMEM((Tp, RET_QK_DIM), MXU_DTYPE)],
        compiler_params=pltpu.CompilerParams(dimension_semantics=("parallel", "parallel")),
    )(p, p, p, cos, sin, lg)


def _ret_core_bwd_call(p, do, cos, sin, lg):
    B, Tp, _ = p.shape

    def body(q_ref, k_ref, v_ref, do_ref, cos_ref, sin_ref, lg_ref, dq_ref, dk_ref, dv_ref, qs_ref, ks_ref, dqa_ref, dka_ref, dva_ref):
        cos_, sin_ = cos_ref[...], sin_ref[...]
        qs_ref[...], ks_ref[...], q_scale, k_scale = _ret_operands(q_ref, k_ref, cos_, sin_, lg_ref[...])
        dka_ref[...] = jnp.zeros_like(dka_ref)
        dva_ref[...] = jnp.zeros_like(dva_ref)
        for r0, L in _query_blocks(Tp):
            blk = slice(r0, L)
            qb = qs_ref[blk, :]
            dob = do_ref[blk, :].astype(MXU_DTYPE)
            s = _mask_diagonal(lax.dot_general(qb, ks_ref[0:L, :], _NT, preferred_element_type=F32), 0.0).astype(MXU_DTYPE)
            dva_ref[0:L, :] += lax.dot_general(s, dob, _TN, preferred_element_type=F32)
            ds = _mask_diagonal(lax.dot_general(dob, v_ref[0:L, :].astype(MXU_DTYPE), _NT, preferred_element_type=F32), 0.0).astype(MXU_DTYPE)
            dqa_ref[blk, :] = jnp.dot(ds, ks_ref[0:L, :], preferred_element_type=F32)
            dka_ref[0:L, :] += lax.dot_general(ds, qb, _TN, preferred_element_type=F32)
        dq_ref[...] = _unrope_halves(dqa_ref[...] * q_scale, cos_, sin_).astype(dq_ref.dtype)
        dk_ref[...] = _unrope_halves(dka_ref[...] * k_scale, cos_, sin_).astype(dk_ref.dtype)
        dv_ref[...] = dva_ref[...].astype(dv_ref.dtype)

    q, k, v, tab, lgs = _ret_specs(Tp)
    qk_out = pl.BlockSpec((None, Tp, RET_QK_DIM), lambda b, h: (b, 0, h))
    v_out = pl.BlockSpec((None, Tp, RET_V_DIM), lambda b, h: (b, 0, h))
    return pl.pallas_call(
        body, name="retention_bwd", grid=(B, RET_HEADS), in_specs=[q, k, v, v_out, tab, tab, lgs],
        out_specs=[qk_out, qk_out, v_out],
        out_shape=[jax.ShapeDtypeStruct((B, Tp, RET_HEADS * RET_QK_DIM), p.dtype), jax.ShapeDtypeStruct((B, Tp, RET_HEADS * RET_QK_DIM), p.dtype),
                   jax.ShapeDtypeStruct((B, Tp, RET_HEADS * RET_V_DIM), p.dtype)],
        scratch_shapes=[pltpu.VMEM((Tp, RET_QK_DIM), MXU_DTYPE), pltpu.VMEM((Tp, RET_QK_DIM), MXU_DTYPE),
                        pltpu.VMEM((Tp, RET_QK_DIM), F32), pltpu.VMEM((Tp, RET_QK_DIM), F32), pltpu.VMEM((Tp, RET_V_DIM), F32)],
        compiler_params=pltpu.CompilerParams(dimension_semantics=("parallel", "parallel")),
    )(p, p, p, do, cos, sin, lg)


def _ret_gate_specs(M):
    tm = _pick(M, 1088, 8)
    head = pl.BlockSpec((tm, RET_V_DIM), lambda i, h: (i, h))
    gate = pl.BlockSpec((tm, RET_V_DIM), lambda i, h: (i, _RET_G_BLOCK0 + h))
    return tm, head, gate


def _ret_gate_fwd_call(o, p2d):
    M = o.shape[0]
    tm, head, gate = _ret_gate_specs(M)

    def body(o_ref, g_ref, y_ref):
        ov = o_ref[...]
        gv = g_ref[...].astype(F32)
        rstd = lax.rsqrt(jnp.mean(ov * ov, axis=-1, keepdims=True) + EPS)
        y_ref[...] = (gv * _sigmoid(gv)) * (ov * rstd)

    return pl.pallas_call(
        body, name="retention_gate_fwd", grid=(M // tm, RET_HEADS), in_specs=[head, gate], out_specs=head,
        out_shape=jax.ShapeDtypeStruct(o.shape, F32),
        compiler_params=pltpu.CompilerParams(dimension_semantics=("parallel", "parallel")),
    )(o, p2d)


def _ret_gate_bwd_call(o, p2d, dy):
    M = o.shape[0]
    tm, head, gate = _ret_gate_specs(M)

    def body(o_ref, g_ref, dy_ref, do_ref, dg_ref):
        ov = o_ref[...]
        gv = g_ref[...].astype(F32)
        dy = dy_ref[...]
        rstd = lax.rsqrt(jnp.mean(ov * ov, axis=-1, keepdims=True) + EPS)
        on = ov * rstd
        sg = _sigmoid(gv)
        dg_ref[...] = (dy * on * (sg * (1.0 + gv * (1.0 - sg)))).astype(dg_ref.dtype)
        don = dy * (gv * sg)
        do_ref[...] = (rstd * (don - on * jnp.mean(don * on, axis=-1, keepdims=True))).astype(do_ref.dtype)

    shp = jax.ShapeDtypeStruct(o.shape, p2d.dtype)
    return pl.pallas_call(
        body, name="retention_gate_bwd", grid=(M // tm, RET_HEADS), in_specs=[head, gate, head], out_specs=[head, head],
        out_shape=[shp, shp],
        compiler_params=pltpu.CompilerParams(dimension_semantics=("parallel", "parallel")),
    )(o, p2d, dy)


def _log_gamma():
    return jnp.log(1.0 - 2.0 ** (-5.0 - jnp.arange(RET_HEADS, dtype=F32))).reshape(RET_HEADS, 1, 1)


@functools.partial(jax.custom_vjp, nondiff_argnums=(9,))
def retention_block(h, w_in, w_out, w_in_grad_slot, w_out_grad_slot, g, b, cos, sin, dims):
    return _retention_block_fwd(h, w_in, w_out, w_in_grad_slot, w_out_grad_slot, g, b, cos, sin, dims)[0]


def _retention_block_fwd(h, w_in, w_out, w_in_grad_slot, w_out_grad_slot, g, b, cos, sin, dims):
    B, Tp = dims
    p = _mm_nn(h, w_in, False, "od_w_in_fwd", out_dtype=MXU_DTYPE)
    o = _ret_core_fwd_call(p.reshape(B, Tp, -1), cos, sin, _log_gamma())
    y = _ret_gate_fwd_call(o.reshape(B * Tp, -1), p)
    out, z = _mm_nn(y, w_out, False, "od_w_out_norm_fwd", norm=(h, g, b))
    return out, (h, p, o, y, z, w_in, w_out, g, cos, sin, jnp.zeros((), w_in_grad_slot.dtype))


def _retention_block_bwd(dims, res, dout):
    B, Tp = dims
    h, p, o, y, z, w_in, w_out, g, cos, sin, slot_like = res
    dz, dg, db = _ln_bwd_call(z, g, dout, "od_w_out_norm_bwd")
    dy = _mm_nt(dz, w_out, None, "od_w_out_dx")
    dw_out = _mm_tn(y, dz, False, "od_w_out_dw", 1, slot_like.dtype)
    do, dgate = _ret_gate_bwd_call(o.reshape(B * Tp, -1), p, dy)
    dq, dk, dv = _ret_core_bwd_call(p.reshape(B, Tp, -1), do.reshape(B, Tp, -1), cos, sin, _log_gamma())
    dp = jnp.concatenate([dq.reshape(B * Tp, -1), dk.reshape(B * Tp, -1), dv.reshape(B * Tp, -1), dgate], axis=-1)
    dh = _mm_nt(dp, w_in, None, "od_w_in_dx", plus=dz)
    dw_in = _mm_tn(h, dp, False, "od_w_in_dw", N_CHIPS, slot_like.dtype)
    return dh, None, None, dw_in, dw_out, dg.reshape(g.shape), db.reshape(g.shape), None, None


retention_block.defvjp(_retention_block_fwd, _retention_block_bwd)


def _heads_to_lanes(w):
    K = w.shape[0]
    w = w.reshape(K, MLA_HEADS, MLA_NOPE + MLA_ROPE)
    return jnp.pad(w, ((0, 0), (0, 0), (0, HEAD_LANES - MLA_NOPE - MLA_ROPE))).reshape(K, MLA_HEADS * HEAD_LANES)


def _out_rows_to_lanes(w):
    N = w.shape[1]
    att = w[LRU_WIDTH:].reshape(MLA_HEADS, MLA_V, N)
    att = jnp.pad(att, ((0, 0), (HEAD_LANES - MLA_V, 0), (0, 0))).reshape(MLA_HEADS * HEAD_LANES, N)
    return jnp.concatenate([w[:LRU_WIDTH], att], axis=0)


def _seq_dims(x):
    B, S, D = x.shape
    T = S + N_META
    Tp = _round_up(T, SEQ_BLOCK)
    return B, S, T, Tp


def _mixer0(diff, w, token):
    x = diff["x"]
    B, S, T, Tp = _seq_dims(x)
    D = x.shape[-1]
    M = B * Tp

    def mm(a, name, act=False, out_dtype=F32, layout=lambda m: m, col_shards=1):
        return matmul(a, layout(w[name]), layout(diff[name]), act, name, out_dtype, col_shards)

    meta = jnp.broadcast_to((diff["meta_tokens"] + token)[None], (B, N_META, D))
    h = jnp.concatenate([meta, x, jnp.zeros((B, Tp - T, D), F32)], axis=1).reshape(M, D)
    p = mm(h, "ev_w_in")
    sp = jax.nn.softplus(-diff["ev_lru_lambda"]).reshape(1, LRU_WIDTH)
    y, qn, kvn, kpe = even_front(
        p.reshape(B, Tp, -1), diff["ev_conv_w"].reshape(CONV_WIDTH, LRU_WIDTH), diff["ev_conv_b"].reshape(1, LRU_WIDTH),
        diff["ev_w_rg_a"].reshape(LRU_HEADS, LRU_HEAD_DIM, LRU_HEAD_DIM), diff["ev_b_rg_a"].reshape(1, LRU_WIDTH),
        diff["ev_w_rg_x"].reshape(LRU_HEADS, LRU_HEAD_DIM, LRU_HEAD_DIM), diff["ev_b_rg_x"].reshape(1, LRU_WIDTH),
        sp, diff["ev_q_norm_g"].reshape(-1), diff["ev_kv_norm_g"].reshape(-1))
    q = mm(qn, "ev_w_uq", out_dtype=MXU_DTYPE, layout=_heads_to_lanes).reshape(B, Tp, -1)
    kv = mm(kvn, "ev_w_ukv", out_dtype=MXU_DTYPE).reshape(B, Tp, -1)
    y = mla_attention(q, kv, kpe, _mla_rope_tables(Tp), y).reshape(M, -1)
    return out_block(h, y, _out_rows_to_lanes(w["ev_w_out"]), _out_rows_to_lanes(diff["ev_w_out"]),
                     diff["ln_mix_g"], diff["ln_mix_b"], "ev_w_out")


def _mlp0(diff, h, w):
    return mlp_block(h, w["mlp_w1_0"], w["mlp_w2_0"], diff["mlp_w1_0"], diff["mlp_w2_0"], diff["ln_mlp_g"], diff["ln_mlp_b"], "mlp0")


def _layer1_loss(diff, h, w, tgt):
    B, S, T, Tp = _seq_dims(tgt)
    D = tgt.shape[-1]

    cos, sin = (jnp.asarray(t) for t in _rope_tables(Tp, RET_QK_DIM // 2))
    h = retention_block(h, w["od_w_in"], w["od_w_out"], diff["od_w_in"], diff["od_w_out"], diff["ln_mix_g"], diff["ln_mix_b"], cos, sin, (B, Tp))
    h = mlp_block(h, w["mlp_w1_1"], w["mlp_w2_1"], diff["mlp_w1_1"], diff["mlp_w2_1"], diff["ln_mlp_g"], diff["ln_mlp_b"], "mlp1")
    return loss_head(h.reshape(B, Tp, D), jnp.pad(tgt, ((0, 0), (N_META, Tp - T), (0, 0))), S)


_HBM = pl.BlockSpec(memory_space=pltpu.HBM)


def _place():
    return lax.axis_index("x"), lax.axis_index("y"), lax.axis_index("c")


def _other_chips(x, y):
    return [(1 - x, y), (x, 1 - y), (1 - x, 1 - y)]


def _chunks(rows, sublanes, most):
    for q in range(most, 0, -1):
        if rows % (q * sublanes) == 0:
            return q
    return 1


def _sublanes(dtype):
    return 8 * 4 // jnp.dtype(dtype).itemsize


def _gather_pieces(bufs):
    plan, first = [], []
    for b in bufs:
        Rh = b.shape[0] // 2
        Q = _chunks(Rh, _sublanes(b.dtype), 4) if Rh * b.shape[1] * b.dtype.itemsize > (1 << 20) else 1
        first.append(3 * sum(q for _, q, _ in plan))
        plan.append((Rh, Q, Rh // Q))
    return plan, first, 3 * sum(q for _, q, _ in plan)


def _allgather_chips(bufs, name):
    n = len(bufs)
    plan, first, n_sems = _gather_pieces(bufs)

    def body(*refs):
        x_refs, out_refs, (send_sems, recv_sems) = refs[:n], refs[n:2 * n], refs[2 * n:]
        x, y, c = _place()
        sibling = (x, y, 1 - c)
        chips = _other_chips(x, y)

        def copy(k, src, dst, to):
            return pltpu.make_async_remote_copy(src_ref=src, dst_ref=dst, send_sem=send_sems.at[k], recv_sem=recv_sems.at[k],
                                                device_id=to, device_id_type=MESH)

        def piece(i, cx, cy, hc, q):
            Rh, _, ch = plan[i]
            return out_refs[i].at[2 * cx + cy, pl.ds(hc * Rh + q * ch, ch), :]

        slots = [(i, q, j) for i in range(n) for q in range(plan[i][1]) for j in range(3)]
        sem = {(i, q, j): first[i] + 3 * q + j for i, q, j in slots}
        sent = [copy(sem[i, q, j], x_refs[i].at[pl.ds(c * plan[i][0] + q * plan[i][2], plan[i][2]), :], piece(i, x, y, c, q), (*chips[j], c))
                for i, q, j in slots]
        for cp in sent:
            cp.start()
        passed = []
        for i, q, j in slots:
            landed = piece(i, *chips[j], c, q)
            copy(sem[i, q, j], landed, landed, sibling).wait_recv()
            fwd = copy(n_sems + sem[i, q, j], landed, landed, sibling)
            fwd.start()
            passed.append(fwd)
        for i, q, j in slots:
            theirs = piece(i, *chips[j], 1 - c, q)
            copy(n_sems + sem[i, q, j], theirs, theirs, sibling).wait_recv()
        for cp in sent + passed:
            cp.wait_send()

    return pl.pallas_call(
        body, name=name, in_specs=[_HBM] * n, out_specs=[_HBM] * n,
        out_shape=[jax.ShapeDtypeStruct((N_CHIPS,) + b.shape, b.dtype) for b in bufs],
        scratch_shapes=[pltpu.SemaphoreType.DMA((2 * n_sems,)), pltpu.SemaphoreType.DMA((2 * n_sems,))],
    )(*bufs)


def _with_own(gathered, own):
    my = 2 * lax.axis_index("x") + lax.axis_index("y")
    return lax.dynamic_update_slice(gathered, own[None], (my, 0, 0))


def _sibling_gather(fs, name):
    n = len(fs)

    def body(*refs):
        out_refs, (send_sems, recv_sems) = refs[n:2 * n], refs[2 * n:]
        x, y, c = _place()
        copies = [pltpu.make_async_remote_copy(src_ref=out_ref.at[c], dst_ref=out_ref.at[c], send_sem=send_sems.at[i], recv_sem=recv_sems.at[i],
                                               device_id=(x, y, 1 - c), device_id_type=MESH) for i, out_ref in enumerate(out_refs)]
        for cp in copies:
            cp.start()
        for cp in copies:
            cp.wait()

    return pl.pallas_call(
        body, name=name, in_specs=[_HBM] * n, out_specs=[_HBM] * n,
        out_shape=[jax.ShapeDtypeStruct(f.shape, f.dtype) for f in fs], input_output_aliases={i: i for i in range(n)},
        scratch_shapes=[pltpu.SemaphoreType.DMA((n,)), pltpu.SemaphoreType.DMA((n,))],
    )(*fs)


def _axis_scalar(name):
    return lax.axis_index(name).astype(jnp.int32).reshape(1)


_SEM = pl.BlockSpec(memory_space=pltpu.SEMAPHORE)
_ANY = pl.BlockSpec(memory_space=pl.ANY)
_EFFECT = pltpu.SideEffectType.DATAFLOW_SIDE_EFFECTING


def _in_hbm(a):
    return pltpu.with_memory_space_constraint(a, pltpu.HBM)


def _half_copies(x_refs, land_refs, send_sems, recv_sems, arriving):
    x, y, c = _place()
    copies = []
    for i, (x_ref, land_ref) in enumerate(zip(x_refs, land_refs)):
        Rh = x_ref.shape[0] // 2
        rows = pl.ds(c * Rh, Rh)
        for j, (cx, cy) in enumerate(_other_chips(x, y)):
            copies.append(pltpu.make_async_remote_copy(
                src_ref=x_ref.at[rows, :], dst_ref=land_ref.at[2 * cx + cy if arriving else 2 * x + y, rows, :],
                send_sem=send_sems.at[3 * i + j], recv_sem=recv_sems.at[3 * i + j], device_id=(cx, cy, c), device_id_type=MESH))
    return copies


def _allgather_start(bufs, name):
    n = len(bufs)

    def body(*refs):
        x_refs, land_refs, (send_sems, recv_sems), token = refs[:n], refs[n:2 * n], refs[2 * n:2 * n + 2], refs[-1]
        for cp in _half_copies(x_refs, land_refs, send_sems, recv_sems, False):
            cp.start()
        token[...] = jnp.zeros_like(token)

    lands = [lax.empty((N_CHIPS,) + b.shape, b.dtype) for b in bufs]
    out = pl.pallas_call(
        body, name=name,
        out_shape=(pltpu.SemaphoreType.DMA((3 * n,)), pltpu.SemaphoreType.DMA((3 * n,)), *[pltpu.HBM(a.shape, a.dtype) for a in bufs + lands],
                   jax.ShapeDtypeStruct((8, 128), F32)),
        in_specs=[_HBM] * (2 * n), out_specs=(_SEM, _SEM, *[_HBM] * (2 * n), pl.BlockSpec(memory_space=pltpu.VMEM)),
        input_output_aliases={i: 2 + i for i in range(2 * n)}, compiler_params=pltpu.CompilerParams(has_side_effects=_EFFECT),
    )(*[_in_hbm(a) for a in bufs + lands])
    return (out[0], out[1], list(out[2:2 + n]), list(out[2 + n:2 + 2 * n])), out[-1][0, 0]


def _allgather_wait(pending, after, name):
    send_sems, recv_sems, bufs, lands = pending
    n = len(bufs)

    def body(*refs):
        x_refs, land_refs, send_sems, recv_sems = refs[:n], refs[n:2 * n], refs[2 * n], refs[2 * n + 1]
        for cp in _half_copies(x_refs, land_refs, send_sems, recv_sems, False):
            cp.wait_send()
        for cp in _half_copies(x_refs, land_refs, send_sems, recv_sems, True):
            cp.wait_recv()

    out = pl.pallas_call(
        body, name=name, out_shape=tuple(pltpu.HBM(a.shape, a.dtype) for a in bufs + lands),
        in_specs=[_HBM] * (2 * n) + [_SEM, _SEM, _ANY], out_specs=tuple([_HBM] * (2 * n)), input_output_aliases={i: i for i in range(2 * n)},
        compiler_params=pltpu.CompilerParams(has_side_effects=_EFFECT),
    )(*bufs, *lands, send_sems, recv_sems, after)
    return list(out[n:])


def _sibling_forward(lands, name):
    n = len(lands)
    plan, first, n_sems = _gather_pieces([jax.ShapeDtypeStruct(l.shape[1:], l.dtype) for l in lands])

    def body(*refs):
        out_refs, (send_sems, recv_sems) = refs[n:2 * n], refs[2 * n:]
        x, y, c = _place()

        def copies(hc):
            return [pltpu.make_async_remote_copy(
                        src_ref=out_refs[i].at[2 * cx + cy, pl.ds(hc * plan[i][0] + q * plan[i][2], plan[i][2]), :],
                        dst_ref=out_refs[i].at[2 * cx + cy, pl.ds(hc * plan[i][0] + q * plan[i][2], plan[i][2]), :],
                        send_sem=send_sems.at[first[i] + 3 * q + j], recv_sem=recv_sems.at[first[i] + 3 * q + j],
                        device_id=(x, y, 1 - c), device_id_type=MESH)
                    for i in range(n) for q in range(plan[i][1]) for j, (cx, cy) in enumerate(_other_chips(x, y))]

        mine = copies(c)
        for cp in mine:
            cp.start()
        for cp in mine:
            cp.wait_send()
        for cp in copies(1 - c):
            cp.wait_recv()

    return pl.pallas_call(
        body, name=name, in_specs=[_HBM] * n, out_specs=[_HBM] * n, out_shape=[jax.ShapeDtypeStruct(l.shape, l.dtype) for l in lands],
        input_output_aliases={i: i for i in range(n)},
        scratch_shapes=[pltpu.SemaphoreType.DMA((n_sems,)), pltpu.SemaphoreType.DMA((n_sems,))],
    )(*lands)


N_PEERS = 7


def _direct_copies(p_refs, t_refs, send_sems, recv_sems):
    x, y, c = _place()
    copies = []
    for i, (p_ref, t_ref) in enumerate(zip(p_refs, t_refs)):
        for f in range(1, N_PEERS + 1):
            px, py, pc = x ^ (f >> 2), y ^ ((f >> 1) & 1), c ^ (f & 1)
            copies.append(pltpu.make_async_remote_copy(
                src_ref=p_ref.at[2 * px + py, pc], dst_ref=t_ref.at[f - 1], send_sem=send_sems.at[N_PEERS * i + f - 1],
                recv_sem=recv_sems.at[N_PEERS * i + f - 1], device_id=(px, py, pc), device_id_type=MESH))
    return copies


def _direct_scatter_start(ps, name, carried=()):
    n, m = len(ps), 2 * len(ps) + len(carried)

    def body(*refs):
        p_refs, t_refs, (send_sems, recv_sems) = refs[:n], refs[n:2 * n], refs[m:m + 2]
        for cp in _direct_copies(p_refs, t_refs, send_sems, recv_sems):
            cp.start()

    lands = [lax.empty((N_PEERS,) + p.shape[2:], p.dtype) for p in ps]
    through = ps + lands + list(carried)
    out = pl.pallas_call(
        body, name=name,
        out_shape=(pltpu.SemaphoreType.DMA((N_PEERS * n,)), pltpu.SemaphoreType.DMA((N_PEERS * n,)),
                   *[pltpu.HBM(a.shape, a.dtype) for a in through]),
        in_specs=[_HBM] * m, out_specs=(_SEM, _SEM, *[_HBM] * m),
        input_output_aliases={i: 2 + i for i in range(m)}, compiler_params=pltpu.CompilerParams(has_side_effects=_EFFECT),
    )(*[_in_hbm(a) for a in through])
    return (out[0], out[1], list(out[2:2 + n]), list(out[2 + n:2 + 2 * n])), list(out[2 + 2 * n:])


def _direct_scatter_wait(pending, after, name):
    send_sems, recv_sems, ps, lands = pending
    n = len(ps)

    def body(*refs):
        p_refs, t_refs, send_sems, recv_sems = refs[:n], refs[n:2 * n], refs[2 * n], refs[2 * n + 1]
        for cp in _direct_copies(p_refs, t_refs, send_sems, recv_sems):
            cp.wait_send()
            cp.wait_recv()

    out = pl.pallas_call(
        body, name=name, out_shape=tuple(pltpu.HBM(a.shape, a.dtype) for a in ps + lands),
        in_specs=[_HBM] * (2 * n) + [_SEM, _SEM] + [_ANY] * len(after), out_specs=tuple([_HBM] * (2 * n)),
        input_output_aliases={i: i for i in range(2 * n)}, compiler_params=pltpu.CompilerParams(has_side_effects=_EFFECT),
    )(*ps, *lands, send_sems, recv_sems, *after)
    return list(out[:n]), list(out[n:])


def _sum_direct(p, t, name):
    _, _, R, C = p.shape
    tr = _pick(R, 512, 16)

    def body(x_ref, y_ref, c_ref, p_ref, t_ref, o_ref):
        acc = p_ref[...].astype(F32)
        for f in range(N_PEERS):
            acc = acc + t_ref[f].astype(F32)
        o_ref[...] = acc

    grid_spec = pltpu.PrefetchScalarGridSpec(
        num_scalar_prefetch=3, grid=(R // tr,),
        in_specs=[pl.BlockSpec((None, None, tr, C), lambda i, x_ref, y_ref, c_ref: (2 * x_ref[0] + y_ref[0], c_ref[0], i, 0)),
                  pl.BlockSpec((N_PEERS, tr, C), lambda i, x_ref, y_ref, c_ref: (0, i, 0))],
        out_specs=pl.BlockSpec((None, tr, C), lambda i, x_ref, y_ref, c_ref: (c_ref[0], i, 0)))
    return pl.pallas_call(body, name=name, grid_spec=grid_spec, out_shape=jax.ShapeDtypeStruct((2, R, C), F32),
                          compiler_params=pltpu.CompilerParams(dimension_semantics=("parallel",)))(
        _axis_scalar("x"), _axis_scalar("y"), _axis_scalar("c"), p, t)


def _adamw(w, g, m, v, name):
    R, C = w.shape
    tr = _pick(R, 256, 8)

    def body(w_ref, g_ref, m_ref, v_ref, d_ref, nm_ref, nv_ref):
        g_ = g_ref[...]
        m_ = ADAM_B1 * m_ref[...] + (1.0 - ADAM_B1) * g_
        v_ = ADAM_B2 * v_ref[...] + (1.0 - ADAM_B2) * (g_ * g_)
        m_hat = m_ / (1.0 - ADAM_B1 ** ADAM_STEP)
        v_hat = v_ / (1.0 - ADAM_B2 ** ADAM_STEP)
        d_ref[...] = -ADAM_LR * (m_hat / (jnp.sqrt(v_hat) + ADAM_EPS) + ADAM_WD * w_ref[...])
        nm_ref[...] = m_
        nv_ref[...] = v_

    row = pl.BlockSpec((tr, C), lambda i: (i, 0))
    shp = jax.ShapeDtypeStruct((R, C), F32)
    return pl.pallas_call(body, name=name, grid=(R // tr,), in_specs=[row] * 4, out_specs=[row] * 3, out_shape=[shp] * 3,
                          compiler_params=pltpu.CompilerParams(dimension_semantics=("parallel",)))(w, g, m, v)


BIG_SPECS = (("ev_w_in", 1024, 1440, 1), ("ev_w_uq", 256, 768, 1), ("ev_w_ukv", 128, 1024, 1), ("ev_w_out", 1024, 1024, 0),
             ("od_w_in", 1024, 6144, 1), ("od_w_out", 2048, 1024, 0), ("mlp_w1_0", 1024, 4096, 1), ("mlp_w1_1", 1024, 4096, 1),
             ("mlp_w2_0", 4096, 1024, 0), ("mlp_w2_1", 4096, 1024, 0))
BIG_PARAMS = (("ev_w_in", ("ev_w_in",)), ("ev_w_uq", ("ev_w_uq",)), ("ev_w_ukv", ("ev_w_ukv",)), ("ev_w_out", ("ev_w_out",)),
              ("od_w_in", ("od_w_in",)), ("od_w_out", ("od_w_out",)), ("mlp_w1", ("mlp_w1_0", "mlp_w1_1")),
              ("mlp_w2", ("mlp_w2_0", "mlp_w2_1")))
REPLICATED = ("ev_conv_b", "ev_w_rg_a", "ev_b_rg_a", "ev_w_rg_x", "ev_b_rg_x", "ev_lru_lambda", "ev_q_norm_g", "ev_kv_norm_g",
              "ln_mix_g", "ln_mix_b", "ln_mlp_g", "ln_mlp_b")
SMALL_SHARDED = ("meta_tokens", "ev_conv_w")
COL_SHARD_GRADS = ("od_w_in", "mlp_w1_0", "mlp_w1_1")
MATRIX_GROUPS = (("ev_w_in", "ev_w_uq", "ev_w_ukv", "ev_w_out"), ("mlp_w1_0", "mlp_w2_0"), ("od_w_in", "od_w_out", "mlp_w1_1", "mlp_w2_1"))
LAYER_NORMS = ("ln_mix_g", "ln_mix_b", "ln_mlp_g", "ln_mlp_b")
WEIGHT_NAMES = ("meta_tokens", "ev_w_in", "ev_conv_w", "ev_conv_b", "ev_w_rg_a", "ev_b_rg_a", "ev_w_rg_x", "ev_b_rg_x",
                "ev_lru_lambda", "ev_q_norm_g", "ev_w_uq", "ev_kv_norm_g", "ev_w_ukv", "ev_w_out", "od_w_in", "od_w_out",
                "ln_mix_g", "ln_mix_b", "mlp_w1", "mlp_w2", "ln_mlp_g", "ln_mlp_b")


def _to_rows(flat, row_align):
    n = flat.shape[-1]
    rows = _round_up(-(-n // PACK_COLS), row_align)
    pad = rows * PACK_COLS - n
    if pad:
        flat = jnp.pad(flat, [(0, 0)] * (flat.ndim - 1) + [(0, pad)])
    return flat.reshape(flat.shape[:-1] + (rows, PACK_COLS))


def _shard_shape(K, N, axis):
    return (K // N_CHIPS, N) if axis == 0 else (K, N // N_CHIPS)


def _gather_shards(stacked, K, N, axis):
    if axis == 0:
        return stacked.reshape(K, N)
    return stacked.transpose(1, 0, 2).reshape(K, N)


def _split_shards(full, K, N, axis):
    if axis == 0:
        return full.reshape(N_CHIPS, -1)
    return full.reshape(K, N_CHIPS, N // N_CHIPS).transpose(1, 0, 2).reshape(N_CHIPS, -1)


def kernel(x, meta_tokens, ev_w_in, ev_conv_w, ev_conv_b, ev_w_rg_a, ev_b_rg_a, ev_w_rg_x, ev_b_rg_x, ev_lru_lambda, ev_q_norm_g, ev_w_uq, ev_kv_norm_g, ev_w_ukv, ev_w_out, od_w_in, od_w_out, ln_mix_g, ln_mix_b, mlp_w1, mlp_w2, ln_mlp_g, ln_mlp_b, loss_target, m_meta_tokens, m_ev_w_in, m_ev_conv_w, m_ev_conv_b, m_ev_w_rg_a, m_ev_b_rg_a, m_ev_w_rg_x, m_ev_b_rg_x, m_ev_lru_lambda, m_ev_q_norm_g, m_ev_w_uq, m_ev_kv_norm_g, m_ev_w_ukv, m_ev_w_out, m_od_w_in, m_od_w_out, m_ln_mix_g, m_ln_mix_b, m_mlp_w1, m_mlp_w2, m_ln_mlp_g, m_ln_mlp_b, v_meta_tokens, v_ev_w_in, v_ev_conv_w, v_ev_conv_b, v_ev_w_rg_a, v_ev_b_rg_a, v_ev_w_rg_x, v_ev_b_rg_x, v_ev_lru_lambda, v_ev_q_norm_g, v_ev_w_uq, v_ev_kv_norm_g, v_ev_w_ukv, v_ev_w_out, v_od_w_in, v_od_w_out, v_ln_mix_g, v_ln_mix_b, v_mlp_w1, v_mlp_w2, v_ln_mlp_g, v_ln_mlp_b):
    given = dict(locals())
    local_big = {"ev_w_in": ev_w_in[0], "ev_w_uq": ev_w_uq[0], "ev_w_ukv": ev_w_ukv[0], "ev_w_out": ev_w_out[0],
                 "od_w_in": od_w_in[0], "od_w_out": od_w_out[0], "mlp_w1_0": mlp_w1[0], "mlp_w1_1": mlp_w1[1],
                 "mlp_w2_0": mlp_w2[0], "mlp_w2_1": mlp_w2[1]}

    specs = {spec[0]: spec for spec in BIG_SPECS}
    mixer0_m, mlp0_m, layer1_m = MATRIX_GROUPS

    def shards(names):
        return [local_big[n].astype(MXU_DTYPE) for n in names]

    def whole(stacked, n):
        _, K, N, ax = specs[n]
        return stacked if n in COL_SHARD_GRADS else _gather_shards(stacked, K, N, ax)

    def filled(gathered, own, names):
        return {n: whole(_with_own(g_, o_), n) for n, g_, o_ in zip(names, gathered, own)}

    own_a, own_b, own_c = shards(mixer0_m), shards(mlp0_m), shards(layer1_m)
    small = [meta_tokens, jnp.pad(ev_conv_w[0], ((0, 16 - CONV_WIDTH), (0, 0)))]
    gathered_a = _allgather_chips(own_a + small, "weight_allgather_mixer0")
    pending_b, token1 = _allgather_start(own_b, "weight_allgather_mlp0_start")
    pending_c, token2 = _allgather_start(own_c, "weight_allgather_layer1_start")
    meta_full = _gather_shards(_with_own(gathered_a[-2], small[0]), N_META, D_MODEL, 1)
    conv_full = _gather_shards(_with_own(gathered_a[-1], small[1])[:, :CONV_WIDTH], CONV_WIDTH, LRU_WIDTH, 1)

    def slots(names, dtype):
        return {n: jnp.zeros((N_CHIPS, specs[n][1], specs[n][2] // N_CHIPS) if n in COL_SHARD_GRADS else specs[n][1:3], dtype) for n in names}

    def norms(names, layer):
        return {n: given[n][layer] for n in names}

    def finish_gather(pending, own, after, names, tag):
        landed = _allgather_wait(pending, lax.stop_gradient(after), "weight_allgather_%s_wait" % tag)
        return filled(_sibling_forward(landed, "weight_allgather_%s_forward" % tag), own, names)

    diff_a = {**slots(mixer0_m, MXU_DTYPE), **norms(("ln_mix_g", "ln_mix_b"), 0), **{n: given[n] for n in REPLICATED if n not in LAYER_NORMS},
              "x": x, "meta_tokens": meta_full, "ev_conv_w": conv_full}
    diff_b = {**slots(mlp0_m, MXU_DTYPE), **norms(("ln_mlp_g", "ln_mlp_b"), 0)}
    diff_c = {**slots(layer1_m, MXU_DTYPE), **norms(LAYER_NORMS, 1)}
    w_a = filled(gathered_a[:len(mixer0_m)], own_a, mixer0_m)
    h_a, back_a = jax.vjp(lambda d: _mixer0(d, w_a, token1 + token2), diff_a)
    w_b = finish_gather(pending_b, own_b, h_a, mlp0_m, "mlp0")
    h_b, back_b = jax.vjp(lambda d, hh: _mlp0(d, hh, w_b), diff_b, h_a)
    w_c = finish_gather(pending_c, own_c, h_b, layer1_m, "layer1")
    loss, back_c = jax.vjp(lambda d, hh: _layer1_loss(d, hh, w_c, loss_target), diff_c, h_b)
    loss = lax.psum(loss, ("x", "y", "c"))

    def blocks_of(grad, n):
        _, K, N, ax = specs[n]
        if n in COL_SHARD_GRADS:
            blocks = grad
        elif ax == 0:
            blocks = grad.reshape(N_CHIPS, K // N_CHIPS, N)
        else:
            blocks = grad.reshape(K, N_CHIPS, N // N_CHIPS).transpose(1, 0, 2)
        return blocks.reshape(N_CHIPS, 2, blocks.shape[1] // 2, blocks.shape[2])

    def start_reduce(grads_of, names, tag, dh):
        flying, (dh,) = _direct_scatter_start([blocks_of(grads_of[n], n) for n in names], "grad_scatter_%s_start" % tag, [dh])
        return flying, dh

    g_c, dh = back_c(jnp.ones((), F32))
    flying_c, dh = start_reduce(g_c, layer1_m, "layer1", dh)
    g_b, dh = back_b(dh)
    flying_b, dh = start_reduce(g_b, mlp0_m, "mlp0", dh)
    (g_a,) = back_a(dh)

    g = {**g_a, **g_b, **g_c}
    g.update({n: jnp.stack([(g_b if n in g_b else g_a)[n], g_c[n]]) for n in LAYER_NORMS})
    repl = jnp.concatenate([g[n].reshape(-1) for n in REPLICATED]).reshape(N_CHIPS, -1)
    small = [_split_shards(g["meta_tokens"], N_META, D_MODEL, 1), _split_shards(g["ev_conv_w"], CONV_WIDTH, LRU_WIDTH, 1), repl]
    small = [pc.reshape(N_CHIPS, 2, -1) for pc in small]
    n_small = sum(pc.shape[2] for pc in small)
    small.append(jnp.zeros((N_CHIPS, 2, _round_up(n_small, 32 * PACK_COLS) - n_small), F32))
    p_small = jnp.concatenate(small, axis=2).reshape(N_CHIPS, 2, -1, PACK_COLS)
    flying_a, _ = _direct_scatter_start([blocks_of(g_a[n], n) for n in mixer0_m] + [p_small], "grad_scatter_mixer0_start")
    started = [g_a["x"], flying_a[2][0]]
    ps_c, ts_c = _direct_scatter_wait(flying_c, started, "grad_scatter_layer1_wait")
    ps_b, ts_b = _direct_scatter_wait(flying_b, started, "grad_scatter_mlp0_wait")
    fs_bc = [_sum_direct(p, t, "grad_sum_%d" % i) for i, (p, t) in enumerate(zip(ps_b + ps_c, ts_b + ts_c))]
    red_big = dict(zip(mlp0_m + layer1_m, _sibling_gather(fs_bc, "grad_sibling_gather")))

    grads, delta, new_m, new_v = {}, {}, {}, {}

    def update_big(names):
        done = []
        for name, parts in BIG_PARAMS:
            if parts[0] in names:
                shp = given[name].shape
                two_d = (-1, shp[-1])
                grads[name] = jnp.stack([red_big[part].reshape(shp[1:]) for part in parts])
                d, nm, nv = _adamw(given[name].reshape(two_d), grads[name].reshape(two_d), given["m_" + name].reshape(two_d),
                                   given["v_" + name].reshape(two_d), "adamw_" + name)
                delta[name], new_m[name], new_v[name] = d.reshape(shp), nm.reshape(shp), nv.reshape(shp)
                done.append(nv)
        return done

    updated = update_big(mlp0_m + layer1_m)
    ps_a, ts_a = _direct_scatter_wait(flying_a, updated, "grad_scatter_mixer0_wait")
    fs_a = [_sum_direct(p, t, "grad_sum_mixer0_%d" % i) for i, (p, t) in enumerate(zip(ps_a, ts_a))]
    reduced_a = _sibling_gather(fs_a, "grad_sibling_gather_mixer0")
    red_big.update(zip(mixer0_m, reduced_a))
    red_small = reduced_a[-1].reshape(2, -1)
    update_big(mixer0_m)

    def take(off, sz):
        return jnp.concatenate([red_small[0, off // 2:(off + sz) // 2], red_small[1, off // 2:(off + sz) // 2]])

    off = 0
    for name in SMALL_SHARDED:
        sz = given[name].size
        grads[name] = take(off, sz).reshape(given[name].shape)
        off += sz
    n_repl = repl.shape[1]
    own_repl = _to_rows(take(off, n_repl), 16)
    repl_all = _with_own(_allgather_chips([own_repl], "replicated_allgather")[0], own_repl).reshape(N_CHIPS, -1)[:, :n_repl].reshape(-1)
    off = 0
    for name in REPLICATED:
        sz = given[name].size
        grads[name] = repl_all[off:off + sz].reshape(given[name].shape)
        off += sz

    smalls = SMALL_SHARDED + REPLICATED

    def pack_small(get):
        return _to_rows(jnp.concatenate([get(n).reshape(-1) for n in smalls]), 8)

    outs = _adamw(pack_small(lambda n: given[n]), pack_small(lambda n: grads[n]), pack_small(lambda n: given["m_" + n]),
                  pack_small(lambda n: given["v_" + n]), "adamw_small")
    for res, flat in zip((delta, new_m, new_v), outs):
        flat, off = flat.reshape(-1), 0
        for n in smalls:
            sz = given[n].size
            res[n] = flat[off:off + sz].reshape(given[n].shape)
            off += sz

    return (loss, g_a["x"], *[grads[n] for n in WEIGHT_NAMES], *[delta[n] for n in WEIGHT_NAMES],
            *[new_m[n] for n in WEIGHT_NAMES], *[new_v[n] for n in WEIGHT_NAMES])
```

```python
import functools
import math

import jax
import jax.numpy as jnp
import numpy as np
from jax import lax
from jax.experimental import pallas as pl
from jax.experimental.pallas import tpu as pltpu

F32 = jnp.float32
MXU_DTYPE = jnp.bfloat16

D_MODEL = 1024
N_META = 16
LRU_WIDTH = 512
LRU_HEADS = 4
LRU_HEAD_DIM = 128
CONV_WIDTH = 4
LRU_C = 8.0
MLA_HEADS = 8
MLA_NOPE = 64
MLA_ROPE = 32
MLA_V = 64
MLA_Q_RANK = 256
MLA_KV_RANK = 128
RET_HEADS = 4
RET_QK_DIM = 256
RET_V_DIM = 512
D_FF = 4096
ROPE_BASE = 10000.0
DN_ALPHA = 4.0 ** 0.25
EPS = 1e-5
NEG_INF = -1e30
SEQ_BLOCK = 128

ADAM_LR = 0.001
ADAM_B1 = 0.9
ADAM_B2 = 0.999
ADAM_EPS = 1e-08
ADAM_WD = 0.01
ADAM_STEP = 10

PACK_COLS = 1024
TN_INPUT_VMEM_BYTES = 28 << 20
N_CHIPS = 4

MESH = pl.DeviceIdType.MESH


def _pick(n, target, align):
    best = None
    for t in range(align, min(n, target) + 1, align):
        if n % t == 0:
            best = t
    return n if best is None else best


def _round_up(n, m):
    return (n + m - 1) // m * m


def _relu2(a):
    r = jnp.maximum(a, 0.0)
    return r * r


def _ln_stats(z):
    mu = jnp.mean(z, axis=-1, keepdims=True)
    zc = z - mu
    var = jnp.mean(zc * zc, axis=-1, keepdims=True)
    return zc, lax.rsqrt(var + EPS)


def _mm_nn(a, w, act, name, out_dtype=F32, norm=None):
    M, K = a.shape
    sharded = w.ndim == 3
    n = w.shape[-1]
    N = n * (w.shape[0] if sharded else 1)
    narrow_out = jnp.dtype(out_dtype).itemsize < 4
    tm = _pick(M, (2176 if narrow_out else 1088) if K * a.dtype.itemsize <= 4096 and norm is None else 544, 8)
    tn = _pick(n, 1024, 128)
    per = n // tn
    assert norm is None or tn == N

    def body(a_ref, w_ref, *rest):
        av = a_ref[...]
        if act:
            av = _relu2(av.astype(F32))
        r = jnp.dot(av.astype(MXU_DTYPE), w_ref[...].astype(MXU_DTYPE), preferred_element_type=F32)
        if norm is None:
            rest[0][...] = r.astype(out_dtype)
        else:
            r_ref, g_ref, b_ref, o_ref, z_ref = rest
            z = DN_ALPHA * r_ref[...] + r
            zc, rstd = _ln_stats(z)
            z_ref[...] = z
            o_ref[...] = zc * rstd * g_ref[...] + b_ref[...]

    w_spec = pl.BlockSpec((None, K, tn), lambda i, j: (j // per, 0, j % per)) if sharded else pl.BlockSpec((K, tn), lambda i, j: (0, j))
    tile = pl.BlockSpec((tm, tn), lambda i, j: (i, j))
    in_specs, args = [pl.BlockSpec((tm, K), lambda i, j: (i, 0)), w_spec], [a, w]
    if norm is None:
        out_specs, out_shape = tile, jax.ShapeDtypeStruct((M, N), out_dtype)
    else:
        vec = pl.BlockSpec((1, N), lambda i, j: (0, 0))
        in_specs += [tile, vec, vec]
        args += [norm[0], norm[1].reshape(1, N), norm[2].reshape(1, N)]
        out_specs, out_shape = [tile, tile], [jax.ShapeDtypeStruct((M, N), F32)] * 2
    return pl.pallas_call(
        body, name=name, grid=(M // tm, N // tn), in_specs=in_specs, out_specs=out_specs, out_shape=out_shape,
        compiler_params=pltpu.CompilerParams(dimension_semantics=("parallel", "arbitrary")),
    )(*args)


def _mm_nt(g, w, a_src, name, out_dtype=F32, plus=None):
    M, N = g.shape
    sharded = w.ndim == 3
    K, n = w.shape[-2], w.shape[-1]
    if sharded:
        tk, nk = N, 1
    else:
        tk = N if N * g.dtype.itemsize <= 8192 else _pick(N, 2048, 128)
        nk = N // tk
    tm = _pick(M, 1088 if tk * g.dtype.itemsize <= 4096 else 544, 8)
    tn = _pick(K, 1024, 128)
    has_src = a_src is not None
    assert nk == 1 or out_dtype == F32
    assert plus is None or not has_src

    def body(*refs):
        if has_src:
            g_ref, w_ref, s_ref, o_ref = refs
        elif plus is not None:
            g_ref, w_ref, p_ref, o_ref = refs
        else:
            g_ref, w_ref, o_ref = refs
        nt = (((1,), (1,)), ((), ()))
        if sharded:
            r = sum(lax.dot_general(g_ref[:, s * n:(s + 1) * n].astype(MXU_DTYPE), w_ref[s].astype(MXU_DTYPE), nt, preferred_element_type=F32)
                    for s in range(w_ref.shape[0]))
        else:
            r = lax.dot_general(g_ref[...].astype(MXU_DTYPE), w_ref[...].astype(MXU_DTYPE), nt, preferred_element_type=F32)
        if has_src:
            r = r * (2.0 * jnp.maximum(s_ref[...].astype(F32), 0.0))
        first = r if plus is None else r + DN_ALPHA * p_ref[...]
        if nk == 1:
            o_ref[...] = first.astype(out_dtype)
        else:
            k = pl.program_id(2)

            @pl.when(k == 0)
            def _():
                o_ref[...] = first

            @pl.when(k > 0)
            def _():
                o_ref[...] += r

    w_spec = (pl.BlockSpec((w.shape[0], tn, n), lambda i, j, k: (0, j, 0)) if sharded
              else pl.BlockSpec((tn, tk), lambda i, j, k: (j, k)))
    in_specs = [pl.BlockSpec((tm, tk), lambda i, j, k: (i, k)), w_spec]
    args = [g, w]
    if has_src:
        assert nk == 1
        in_specs.append(pl.BlockSpec((tm, tn), lambda i, j, k: (i, j)))
        args.append(a_src)
    if plus is not None:
        in_specs.append(pl.BlockSpec((tm, tn), lambda i, j, k: (i, j)))
        args.append(plus)
    return pl.pallas_call(
        body, name=name,
        grid=(M // tm, K // tn, nk),
        in_specs=in_specs,
        out_specs=pl.BlockSpec((tm, tn), lambda i, j, k: (i, j)),
        out_shape=jax.ShapeDtypeStruct((M, K), out_dtype),
        compiler_params=pltpu.CompilerParams(dimension_semantics=("parallel", "parallel", "arbitrary")),
    )(*args)


def _mm_tn(a, g, act, name, col_shards=1, out_dtype=F32):
    M, K = a.shape
    _, N = g.shape
    n = N // col_shards
    tm, tn = _pick(K, 1024, 128), _pick(n, 1024, 128)
    row_bytes = tm * a.dtype.itemsize + tn * g.dtype.itemsize
    tk = _pick(M, min(2176, TN_INPUT_VMEM_BYTES // (2 * row_bytes)), 8)
    nk = M // tk
    per = n // tn
    direct = out_dtype == F32

    def body(a_ref, g_ref, o_ref, *scratch):
        acc_ref = o_ref if direct else scratch[0]
        k = pl.program_id(2)
        av = a_ref[...]
        if act:
            av = _relu2(av.astype(F32))
        r = lax.dot_general(av.astype(MXU_DTYPE), g_ref[...].astype(MXU_DTYPE),
                            (((0,), (0,)), ((), ())), preferred_element_type=F32)

        @pl.when(k == 0)
        def _():
            acc_ref[...] = r

        @pl.when(k > 0)
        def _():
            acc_ref[...] += r

        if not direct:
            @pl.when(k == nk - 1)
            def _():
                o_ref[...] = acc_ref[...].astype(out_dtype)

    if col_shards == 1:
        out_spec, out_shape = pl.BlockSpec((tm, tn), lambda i, j, k: (i, j)), (K, N)
    else:
        out_spec, out_shape = pl.BlockSpec((None, tm, tn), lambda i, j, k: (j // per, i, j % per)), (col_shards, K, n)
    return pl.pallas_call(
        body, name=name,
        grid=(K // tm, N // tn, nk),
        in_specs=[pl.BlockSpec((tk, tm), lambda i, j, k: (k, i)), pl.BlockSpec((tk, tn), lambda i, j, k: (k, j))],
        out_specs=out_spec,
        out_shape=jax.ShapeDtypeStruct(out_shape, out_dtype),
        scratch_shapes=[] if direct else [pltpu.VMEM((tm, tn), F32)],
        compiler_params=pltpu.CompilerParams(dimension_semantics=("parallel", "parallel", "arbitrary")),
    )(a, g)


@functools.partial(jax.custom_vjp, nondiff_argnums=(3, 4, 5, 6))
def matmul(a, w, w_grad_slot, act, name, out_dtype, col_shards):
    return _mm_nn(a, w, act, name + "_fwd", out_dtype)


def _matmul_fwd(a, w, w_grad_slot, act, name, out_dtype, col_shards):
    return _mm_nn(a, w, act, name + "_fwd", out_dtype), (a, w, jnp.zeros((), w_grad_slot.dtype))


def _matmul_bwd(act, name, out_dtype, col_shards, res, g):
    a, w, slot_like = res
    w_grad_dtype = slot_like.dtype
    da = _mm_nt(g, w, a if act else None, name + "_dx")
    dw = _mm_tn(a, g, act, name + "_dw", col_shards, w_grad_dtype)
    return da, None, dw


matmul.defvjp(_matmul_fwd, _matmul_bwd)


def _ln_bwd_call(z, g, dy, name):
    M, D = z.shape
    tm = _pick(M, 544, 8)

    def body(z_ref, g_ref, dy_ref, dz_ref, dg_ref, db_ref):
        @pl.when(pl.program_id(0) == 0)
        def _():
            dg_ref[...] = jnp.zeros_like(dg_ref)
            db_ref[...] = jnp.zeros_like(db_ref)

        zc, rstd = _ln_stats(z_ref[...])
        xhat = zc * rstd
        dy = dy_ref[...]
        dxh = dy * g_ref[...]
        m1 = jnp.mean(dxh, axis=-1, keepdims=True)
        m2 = jnp.mean(dxh * xhat, axis=-1, keepdims=True)
        dz_ref[...] = rstd * (dxh - m1 - xhat * m2)
        dg_ref[...] += jnp.sum(dy * xhat, axis=0, keepdims=True)
        db_ref[...] += jnp.sum(dy, axis=0, keepdims=True)

    row = pl.BlockSpec((tm, D), lambda i: (i, 0))
    vec = pl.BlockSpec((1, D), lambda i: (0, 0))
    return pl.pallas_call(
        body, name=name, grid=(M // tm,), in_specs=[row, vec, row], out_specs=[row, vec, vec],
        out_shape=[jax.ShapeDtypeStruct((M, D), F32), jax.ShapeDtypeStruct((1, D), F32), jax.ShapeDtypeStruct((1, D), F32)],
        compiler_params=pltpu.CompilerParams(dimension_semantics=("arbitrary",)),
    )(z, g.reshape(1, D), dy)


@functools.partial(jax.custom_vjp, nondiff_argnums=(7,))
def mlp_block(h, w1, w2, w1_grad_slot, w2_grad_slot, g, b, name):
    return _mlp_block_fwd(h, w1, w2, w1_grad_slot, w2_grad_slot, g, b, name)[0]


def _mlp_block_fwd(h, w1, w2, w1_grad_slot, w2_grad_slot, g, b, name):
    u = _mm_nn(h, w1, False, name + "_w1_fwd", out_dtype=MXU_DTYPE)
    out, z = _mm_nn(u, w2, True, name + "_w2_norm_fwd", norm=(h, g, b))
    return out, (h, u, z, w1, w2, g, jnp.zeros((), w1_grad_slot.dtype))


def _mlp_block_bwd(name, res, dy):
    h, u, z, w1, w2, g, slot_like = res
    dz, dg, db = _ln_bwd_call(z, g, dy, name + "_norm_bwd")
    du = _mm_nt(dz, w2, u, name + "_w2_dx", out_dtype=MXU_DTYPE)
    dw2 = _mm_tn(u, dz, True, name + "_w2_dw", 1, slot_like.dtype)
    dh = _mm_nt(du, w1, None, name + "_w1_dx", plus=dz)
    dw1 = _mm_tn(h, du, False, name + "_w1_dw", N_CHIPS, slot_like.dtype)
    return dh, None, None, dw1, dw2, dg.reshape(g.shape), db.reshape(g.shape)


mlp_block.defvjp(_mlp_block_fwd, _mlp_block_bwd)


@functools.partial(jax.custom_vjp, nondiff_argnums=(6,))
def out_block(h, y, w, w_grad_slot, g, b, name):
    return _out_block_fwd(h, y, w, w_grad_slot, g, b, name)[0]


def _out_block_fwd(h, y, w, w_grad_slot, g, b, name):
    out, z = _mm_nn(y, w, False, name + "_norm_fwd", norm=(h, g, b))
    return out, (y, z, w, g, jnp.zeros((), w_grad_slot.dtype))


def _out_block_bwd(name, res, dy):
    y, z, w, g, slot_like = res
    dz, dg, db = _ln_bwd_call(z, g, dy, name + "_norm_bwd")
    d_y = _mm_nt(dz, w, None, name + "_dx")
    dw = _mm_tn(y, dz, False, name + "_dw", 1, slot_like.dtype)
    return DN_ALPHA * dz, d_y, None, dw, dg.reshape(g.shape), db.reshape(g.shape)


out_block.defvjp(_out_block_fwd, _out_block_bwd)


def _rms_fwd_call(x, g, name, col_block=0):
    R = x.shape[0]
    W = g.shape[-1]
    tr = _pick(R, 1088, 8)

    def body(x_ref, g_ref, o_ref):
        xv = x_ref[...]
        rstd = lax.rsqrt(jnp.mean(xv * xv, axis=-1, keepdims=True) + EPS)
        o_ref[...] = xv * rstd * g_ref[...]

    vec = pl.BlockSpec((1, W), lambda i: (0, 0))
    return pl.pallas_call(
        body, name=name, grid=(R // tr,), in_specs=[pl.BlockSpec((tr, W), lambda i: (i, col_block)), vec],
        out_specs=pl.BlockSpec((tr, W), lambda i: (i, 0)), out_shape=jax.ShapeDtypeStruct((R, W), F32),
        compiler_params=pltpu.CompilerParams(dimension_semantics=("parallel",)),
    )(x, g.reshape(1, W))


def _rms_bwd_call(x, g, dy, name, col_block=0):
    R = x.shape[0]
    W = g.shape[-1]
    tr = _pick(R, 1088, 8)

    def body(x_ref, g_ref, dy_ref, dx_ref, dg_ref):
        @pl.when(pl.program_id(0) == 0)
        def _():
            dg_ref[...] = jnp.zeros_like(dg_ref)

        xv = x_ref[...]
        rstd = lax.rsqrt(jnp.mean(xv * xv, axis=-1, keepdims=True) + EPS)
        xhat = xv * rstd
        dy = dy_ref[...]
        dxh = dy * g_ref[...]
        dx_ref[...] = rstd * (dxh - xhat * jnp.mean(dxh * xhat, axis=-1, keepdims=True))
        dg_ref[...] += jnp.sum(dy * xhat, axis=0, keepdims=True)

    row = pl.BlockSpec((tr, W), lambda i: (i, 0))
    vec = pl.BlockSpec((1, W), lambda i: (0, 0))
    return pl.pallas_call(
        body, name=name, grid=(R // tr,), in_specs=[pl.BlockSpec((tr, W), lambda i: (i, col_block)), vec, row], out_specs=[row, vec],
        out_shape=[jax.ShapeDtypeStruct((R, W), F32), jax.ShapeDtypeStruct((1, W), F32)],
        compiler_params=pltpu.CompilerParams(dimension_semantics=("arbitrary",)),
    )(x, g.reshape(1, W), dy)


def _loss_call(h, tgt, n_tokens, name):
    B, Tp, D = h.shape
    tr = _pick(Tp, 544, 8)

    def body(y_ref, t_ref, dy_ref, acc_ref):
        @pl.when(jnp.logical_and(pl.program_id(0) == 0, pl.program_id(1) == 0))
        def _():
            acc_ref[...] = jnp.zeros_like(acc_ref)

        t = lax.broadcasted_iota(jnp.int32, (tr, 1), 0) + pl.program_id(1) * tr
        counts = jnp.logical_and(t >= N_META, t < N_META + n_tokens)
        e = jnp.where(counts, y_ref[...] - t_ref[...], 0.0)
        dy_ref[...] = e * (1.0 / D)
        acc_ref[...] += jnp.sum(jnp.sum(e * e, axis=-1, keepdims=True), axis=0, keepdims=True) * (0.5 / D)

    row = pl.BlockSpec((None, tr, D), lambda b, i: (b, i, 0))
    one = pl.BlockSpec((1, 1), lambda b, i: (0, 0))
    return pl.pallas_call(
        body, name=name, grid=(B, Tp // tr), in_specs=[row, row], out_specs=[row, one],
        out_shape=[jax.ShapeDtypeStruct((B, Tp, D), F32), jax.ShapeDtypeStruct((1, 1), F32)],
        compiler_params=pltpu.CompilerParams(dimension_semantics=("arbitrary", "arbitrary")),
    )(h, tgt)


@functools.partial(jax.custom_vjp, nondiff_argnums=(2,))
def loss_head(h, tgt, n_tokens):
    return _loss_call(h, tgt, n_tokens, "loss_head")[1][0, 0]


def _loss_head_fwd(h, tgt, n_tokens):
    dy, acc = _loss_call(h, tgt, n_tokens, "loss_head")
    return acc[0, 0], dy


def _loss_head_bwd(n_tokens, dy, ct):
    return ct * dy, None


loss_head.defvjp(_loss_head_fwd, _loss_head_bwd)


_GELU_C = math.sqrt(2.0 / math.pi)


def _gelu_parts(x):
    x2 = x * x
    t = jnp.tanh(_GELU_C * (x + 0.044715 * x * x2))
    gelu = 0.5 * x * (1.0 + t)
    dgelu = 0.5 * (1.0 + t) + 0.5 * x * (1.0 - t * t) * (_GELU_C * (1.0 + 3.0 * 0.044715 * x2))
    return gelu, dgelu


def _sigmoid(x):
    return 1.0 / (1.0 + jnp.exp(-x))


def _scan8(a, b, carry, reverse):
    row = lax.broadcasted_iota(jnp.int32, a.shape, 0)
    for s in (1, 2, 4):
        shift = 8 - s if reverse else s
        keep = (row < 8 - s) if reverse else (row >= s)
        b = jnp.where(keep, a * pltpu.roll(b, shift, 0) + b, b)
        a = jnp.where(keep, a * pltpu.roll(a, shift, 0), a)
    return a * carry + b


def _lru_pre(prec_ref, prev_ref, first, cw_ref, cb_ref, wa_ref, ba_ref, wx_ref, bx_ref, sp_ref):
    tc = prec_ref.shape[0]
    prev = jnp.where(first, 0.0, prev_ref[...])
    ext = jnp.concatenate([prev, prec_ref[...]], axis=0)
    cw = cw_ref[...]
    taps = [ext[8:] if k == CONV_WIDTH - 1 else pltpu.roll(ext, CONV_WIDTH - 1 - k, 0)[8:] for k in range(CONV_WIDTH)]
    xc = cb_ref[...] + sum(cw[k:k + 1, :] * taps[k] for k in range(CONV_WIDTH))
    ga, gx = [], []
    for h in range(LRU_HEADS):
        xh = xc[:, h * LRU_HEAD_DIM:(h + 1) * LRU_HEAD_DIM].astype(MXU_DTYPE)
        ga.append(jnp.dot(xh, wa_ref[h].astype(MXU_DTYPE), preferred_element_type=F32))
        gx.append(jnp.dot(xh, wx_ref[h].astype(MXU_DTYPE), preferred_element_type=F32))
    r = _sigmoid(jnp.concatenate(ga, axis=1) + ba_ref[...])
    i = _sigmoid(jnp.concatenate(gx, axis=1) + bx_ref[...])
    log_a = -LRU_C * r * sp_ref[...]
    a = jnp.exp(log_a)
    a2 = a * a
    mult = jnp.sqrt(-jnp.tanh(log_a) * (a2 + 1.0))
    return taps, xc, r, i, a, a2, mult


def _lru_fwd_call(p, cw, cb, wa, ba, wx, bx, sp):
    B, Tp, _ = p.shape
    W = LRU_WIDTH
    tc = SEQ_BLOCK
    nc = Tp // tc

    def body(pg_ref, prec_ref, prev_ref, cw_ref, cb_ref, wa_ref, ba_ref, wx_ref, bx_ref, sp_ref, y_ref, h_ref, carry_ref):
        first = pl.program_id(1) == 0

        @pl.when(first)
        def _():
            carry_ref[...] = jnp.zeros_like(carry_ref)

        _, xc, r, i, a, a2, mult = _lru_pre(prec_ref, prev_ref, first, cw_ref, cb_ref, wa_ref, ba_ref, wx_ref, bx_ref, sp_ref)
        b = mult * (i * xc)
        carry = carry_ref[0:1, :]
        for t in range(tc // 8):
            h = _scan8(a[8 * t:8 * t + 8], b[8 * t:8 * t + 8], carry, False)
            h_ref[8 * t:8 * t + 8, :] = h
            carry = h[7:8, :]
        carry_ref[...] = jnp.broadcast_to(carry, carry_ref.shape)
        y_ref[...] = h_ref[...] * _gelu_parts(pg_ref[...])[0]

    cur = pl.BlockSpec((None, tc, W), lambda b, j: (b, j, 0))
    rec = pl.BlockSpec((None, tc, W), lambda b, j: (b, j, 1))
    prev = pl.BlockSpec((None, 8, W), lambda b, j: (b, jnp.maximum(j * (tc // 8) - 1, 0), 1))
    vec = pl.BlockSpec((1, W), lambda b, j: (0, 0))
    cws = pl.BlockSpec((CONV_WIDTH, W), lambda b, j: (0, 0))
    wsp = pl.BlockSpec((LRU_HEADS, LRU_HEAD_DIM, LRU_HEAD_DIM), lambda b, j: (0, 0, 0))
    return pl.pallas_call(
        body, name="lru_fwd", grid=(B, nc),
        in_specs=[cur, rec, prev, cws, vec, wsp, vec, wsp, vec, vec],
        out_specs=[cur, cur],
        out_shape=[jax.ShapeDtypeStruct((B, Tp, W + MLA_HEADS * HEAD_LANES), F32), jax.ShapeDtypeStruct((B, Tp, W), F32)],
        scratch_shapes=[pltpu.VMEM((8, W), F32)],
        compiler_params=pltpu.CompilerParams(dimension_semantics=("arbitrary", "arbitrary")),
    )(p, p, p, cw, cb, wa, ba, wx, bx, sp)


def _lru_bwd_call(p, hseq, dy, cw, cb, wa, ba, wx, bx, sp, dpq, dpkv, dkpe):
    B, Tp, P = p.shape
    W = LRU_WIDTH
    tc = SEQ_BLOCK
    nc = Tp // tc
    HD = LRU_HEAD_DIM

    def body(pg_ref, prec_ref, prev_ref, h_ref, hprev_ref, dy_ref, cw_ref, cb_ref, wa_ref, ba_ref, wx_ref, bx_ref, sp_ref,
             dpq_ref, dpkv_ref, dkpe_ref, dp_ref, dcw_ref, dcb_ref, dwa_ref, dba_ref, dwx_ref, dbx_ref, dsp_ref,
             gcar_ref, anext_ref, halo_ref, g_ref):
        j = pl.program_id(1)
        first = j == nc - 1
        last = j == 0

        @pl.when(jnp.logical_and(pl.program_id(0) == 0, last))
        def _():
            for ref in (dcw_ref, dcb_ref, dwa_ref, dba_ref, dwx_ref, dbx_ref, dsp_ref):
                ref[...] = jnp.zeros_like(ref)

        @pl.when(last)
        def _():
            gcar_ref[...] = jnp.zeros_like(gcar_ref)
            anext_ref[...] = jnp.zeros_like(anext_ref)
            halo_ref[...] = jnp.zeros_like(halo_ref)

        taps, xc, r, i, a, a2, mult = _lru_pre(prec_ref, prev_ref, first, cw_ref, cb_ref, wa_ref, ba_ref, wx_ref, bx_ref, sp_ref)
        row = lax.broadcasted_iota(jnp.int32, (tc, W), 0)
        gelu, dgelu = _gelu_parts(pg_ref[...])
        dy = dy_ref[...]
        hcur = h_ref[...]
        dp_ref[:, 0:W] = dy * hcur * dgelu
        dp_ref[:, 2 * W:2 * W + MLA_Q_RANK] = dpq_ref[...]
        dp_ref[:, _KPE_START - MLA_KV_RANK:_KPE_START] = dpkv_ref[...]
        dp_ref[:, _KPE_START:P] = pltpu.roll(dkpe_ref[...], HEAD_LANES - MLA_NOPE, 1)[:, 0:P - _KPE_START]
        dh = dy * gelu
        a_next = jnp.where(row == tc - 1, anext_ref[0:1, :], pltpu.roll(a, tc - 1, 0))
        carry = gcar_ref[0:1, :]
        for t in reversed(range(tc // 8)):
            g = _scan8(a_next[8 * t:8 * t + 8], dh[8 * t:8 * t + 8], carry, True)
            g_ref[8 * t:8 * t + 8, :] = g
            carry = g[0:1, :]
        gcar_ref[...] = jnp.broadcast_to(carry, gcar_ref.shape)
        anext_ref[...] = jnp.broadcast_to(a[0:1, :], anext_ref.shape)
        G = g_ref[...]
        h_before = jnp.where(first, 0.0, hprev_ref[7:8, :])
        hprev = jnp.where(row == 0, h_before, pltpu.roll(hcur, 1, 0))
        d_a = G * hprev
        gx_ = G * xc
        d_mult = gx_ * i
        d_i = gx_ * mult
        dxc = G * (mult * i)
        d_la = d_a * a - d_mult * (a2 / mult)
        sp = sp_ref[...]
        d_r = d_la * (-LRU_C * sp)
        dsp_ref[...] += jnp.sum(d_la * (-LRU_C * r), axis=0, keepdims=True)
        dga = d_r * r * (1.0 - r)
        dgx = d_i * i * (1.0 - i)
        dba_ref[...] += jnp.sum(dga, axis=0, keepdims=True)
        dbx_ref[...] += jnp.sum(dgx, axis=0, keepdims=True)
        back = []
        for h in range(LRU_HEADS):
            sl = slice(h * HD, (h + 1) * HD)
            xh = xc[:, sl].astype(MXU_DTYPE)
            ah = dga[:, sl].astype(MXU_DTYPE)
            bh = dgx[:, sl].astype(MXU_DTYPE)
            tn = (((0,), (0,)), ((), ()))
            nt = (((1,), (1,)), ((), ()))
            dwa_ref[h] += lax.dot_general(xh, ah, tn, preferred_element_type=F32)
            dwx_ref[h] += lax.dot_general(xh, bh, tn, preferred_element_type=F32)
            back.append(lax.dot_general(ah, wa_ref[h].astype(MXU_DTYPE), nt, preferred_element_type=F32)
                        + lax.dot_general(bh, wx_ref[h].astype(MXU_DTYPE), nt, preferred_element_type=F32))
        dxc = dxc + jnp.concatenate(back, axis=1)
        dcb_ref[...] += jnp.sum(dxc, axis=0, keepdims=True)
        for k in range(CONV_WIDTH):
            dcw_ref[k:k + 1, :] += jnp.sum(dxc * taps[k], axis=0, keepdims=True)
        ext = jnp.concatenate([dxc, halo_ref[...]], axis=0)
        cw = cw_ref[...]
        acc = cw[CONV_WIDTH - 1:CONV_WIDTH, :] * dxc
        for k in range(CONV_WIDTH - 1):
            s = CONV_WIDTH - 1 - k
            acc = acc + cw[k:k + 1, :] * pltpu.roll(ext, tc + 8 - s, 0)[:tc]
        dp_ref[:, W:2 * W] = acc
        halo_ref[...] = dxc[0:8, :]

    rev = lambda j: nc - 1 - j
    cur = pl.BlockSpec((None, tc, W), lambda b, j: (b, rev(j), 0))
    rec = pl.BlockSpec((None, tc, W), lambda b, j: (b, rev(j), 1))
    prev = pl.BlockSpec((None, 8, W), lambda b, j: (b, jnp.maximum(rev(j) * (tc // 8) - 1, 0), 0))
    prev_rec = pl.BlockSpec((None, 8, W), lambda b, j: (b, jnp.maximum(rev(j) * (tc // 8) - 1, 0), 1))
    vec = pl.BlockSpec((1, W), lambda b, j: (0, 0))
    cws = pl.BlockSpec((CONV_WIDTH, W), lambda b, j: (0, 0))
    wsp = pl.BlockSpec((LRU_HEADS, HD, HD), lambda b, j: (0, 0, 0))
    vs = jax.ShapeDtypeStruct((1, W), F32)
    ws = jax.ShapeDtypeStruct((LRU_HEADS, HD, HD), F32)

    def rows(width):
        return pl.BlockSpec((None, tc, width), lambda b, j: (b, rev(j), 0))

    return pl.pallas_call(
        body, name="lru_bwd", grid=(B, nc),
        in_specs=[cur, rec, prev_rec, cur, prev, cur, cws, vec, wsp, vec, wsp, vec, vec, rows(MLA_Q_RANK), rows(MLA_KV_RANK), rows(HEAD_LANES)],
        out_specs=[rows(P), cws, vec, wsp, vec, wsp, vec, vec],
        out_shape=[jax.ShapeDtypeStruct((B, Tp, P), F32), jax.ShapeDtypeStruct((CONV_WIDTH, W), F32), vs, ws, vs, ws, vs, vs],
        scratch_shapes=[pltpu.VMEM((8, W), F32), pltpu.VMEM((8, W), F32), pltpu.VMEM((8, W), F32), pltpu.VMEM((tc, W), F32)],
        compiler_params=pltpu.CompilerParams(dimension_semantics=("arbitrary", "arbitrary")),
    )(p, p, p, hseq, hseq, dy, cw, cb, wa, ba, wx, bx, sp, dpq, dpkv, dkpe)


_Q_BLOCK = 2 * LRU_WIDTH // MLA_Q_RANK
_KV_BLOCK = (2 * LRU_WIDTH + MLA_Q_RANK) // MLA_KV_RANK
_KPE_START = 2 * LRU_WIDTH + MLA_Q_RANK + MLA_KV_RANK


@jax.custom_vjp
def even_front(p, cw, cb, wa, ba, wx, bx, sp, gq, gkv):
    return _even_front_fwd(p, cw, cb, wa, ba, wx, bx, sp, gq, gkv)[0]


def _even_front_fwd(p, cw, cb, wa, ba, wx, bx, sp, gq, gkv):
    B, Tp, W = p.shape
    p2d = p.reshape(B * Tp, W)
    y, hseq = _lru_fwd_call(p, cw, cb, wa, ba, wx, bx, sp)
    qn = _rms_fwd_call(p2d, gq, "q_norm_fwd", _Q_BLOCK)
    kvn = _rms_fwd_call(p2d, gkv, "kv_norm_fwd", _KV_BLOCK)
    kpe = jnp.pad(p[:, :, _KPE_START:], ((0, 0), (0, 0), (MLA_NOPE, HEAD_LANES - MLA_NOPE - MLA_ROPE)))
    return (y, qn, kvn, kpe), (p, hseq, cw, cb, wa, ba, wx, bx, sp, gq, gkv)


def _even_front_bwd(res, cts):
    p, hseq, cw, cb, wa, ba, wx, bx, sp, gq, gkv = res
    dy, dqn, dkvn, dkpe = cts
    B, Tp, W = p.shape
    p2d = p.reshape(B * Tp, W)
    dpq, dgq = _rms_bwd_call(p2d, gq, dqn, "q_norm_bwd", _Q_BLOCK)
    dpkv, dgkv = _rms_bwd_call(p2d, gkv, dkvn, "kv_norm_bwd", _KV_BLOCK)
    dp, dcw, dcb, dwa, dba, dwx, dbx, dsp = _lru_bwd_call(p, hseq, dy, cw, cb, wa, ba, wx, bx, sp, dpq.reshape(B, Tp, -1),
                                                          dpkv.reshape(B, Tp, -1), dkpe)
    return dp, dcw, dcb, dwa, dba, dwx, dbx, dsp, dgq.reshape(gq.shape), dgkv.reshape(gkv.shape)


even_front.defvjp(_even_front_fwd, _even_front_bwd)


def _rope_tables(T, half):
    inv = np.float32(ROPE_BASE) ** (-np.arange(half, dtype=np.float32) / np.float32(half))
    ang = np.arange(T, dtype=np.float32)[:, None] * inv[None, :]
    return np.cos(ang), np.sin(ang)


_NT = (((1,), (1,)), ((), ()))
_TN = (((0,), (0,)), ((), ()))
HEAD_LANES = 128
_MLA_SCALE = (MLA_NOPE + MLA_ROPE) ** -0.5
_LOG2E = math.log2(math.e)


Q_BLOCK = 512


def _query_blocks(Tp):
    first = Tp % Q_BLOCK or Q_BLOCK
    return [(0, first)] + [(r, r + Q_BLOCK) for r in range(first, Tp, Q_BLOCK)]


def _mask_diagonal(s, fill):
    R, L = s.shape
    row = lax.broadcasted_iota(jnp.int32, (R, R), 0)
    col = lax.broadcasted_iota(jnp.int32, (R, R), 1)
    last = jnp.where(col <= row, s[:, L - R:], fill)
    return last if L == R else jnp.concatenate([s[:, :L - R], last], axis=1)


def _mla_rope_tables(T):
    half = MLA_ROPE // 2
    cos, sin = _rope_tables(T, half)
    ones, zeros = np.ones((T, MLA_NOPE), np.float32), np.zeros((T, MLA_NOPE), np.float32)
    tail1, tail0 = np.ones((T, HEAD_LANES - MLA_NOPE - MLA_ROPE), np.float32), np.zeros((T, HEAD_LANES - MLA_NOPE - MLA_ROPE), np.float32)
    zh = np.zeros((T, half), np.float32)
    c = np.concatenate([ones, cos, cos, tail1], axis=1)
    s_up = np.concatenate([zeros, -sin, zh, tail0], axis=1)
    s_down = np.concatenate([zeros, zh, sin, tail0], axis=1)
    return jnp.asarray(c), jnp.asarray(s_up), jnp.asarray(s_down)


def _rope_lanes(x, c, s_up, s_down):
    half = MLA_ROPE // 2
    return x * c + pltpu.roll(x, HEAD_LANES - half, 1) * s_up + pltpu.roll(x, half, 1) * s_down


def _unrope_lanes(d, c, s_up, s_down):
    half = MLA_ROPE // 2
    return d * c + pltpu.roll(d * s_up, half, 1) + pltpu.roll(d * s_down, HEAD_LANES - half, 1)


def _mla_operands(q_ref, kv_ref, kpe_ref, c, s_up, s_down):
    lane = lax.broadcasted_iota(jnp.int32, kv_ref.shape, 1)
    qr = (_rope_lanes(q_ref[...].astype(F32), c, s_up, s_down) * (_MLA_SCALE * _LOG2E)).astype(MXU_DTYPE)
    kr = jnp.where(lane < MLA_NOPE, kv_ref[...].astype(F32), _rope_lanes(kpe_ref[...], c, s_up, s_down)).astype(MXU_DTYPE)
    return qr, kr, lane


def _mla_specs(Tp):
    head = pl.BlockSpec((None, Tp, HEAD_LANES), lambda b, h: (b, 0, h))
    shared = pl.BlockSpec((None, Tp, HEAD_LANES), lambda b, h: (b, 0, 0))
    tab = pl.BlockSpec((Tp, HEAD_LANES), lambda b, h: (0, 0))
    lse = pl.BlockSpec((None, None, Tp, 1), lambda b, h: (b, h, 0, 0))
    out = pl.BlockSpec((None, Tp, HEAD_LANES), lambda b, h: (b, 0, LRU_WIDTH // HEAD_LANES + h))
    return head, shared, tab, lse, out


def _attn_fwd_call(q, kv, kpe, tabs, y):
    B, Tp, _ = q.shape

    def body(q_ref, kv_ref, kpe_ref, c_ref, su_ref, sd_ref, y_ref, o_ref, lse_ref, qr_ref, kr_ref):
        qr, kr, lane = _mla_operands(q_ref, kv_ref, kpe_ref, c_ref[...], su_ref[...], sd_ref[...])
        qr_ref[...] = qr
        kr_ref[...] = kr
        for r0, L in _query_blocks(Tp):
            blk = slice(r0, L)
            s = _mask_diagonal(lax.dot_general(qr_ref[blk, :], kr_ref[0:L, :], _NT, preferred_element_type=F32), NEG_INF)
            m = jnp.max(s, axis=-1, keepdims=True)
            p = jnp.exp2(s - m)
            l = jnp.sum(p, axis=-1, keepdims=True)
            o = jnp.dot(p.astype(MXU_DTYPE), kv_ref[0:L, :].astype(MXU_DTYPE), preferred_element_type=F32)
            o_ref[blk, :] = jnp.where(lane[blk, :] >= MLA_NOPE, o / l, 0.0)
            lse_ref[blk, :] = m + jnp.log2(l)

    head, shared, tab, lse, out = _mla_specs(Tp)
    return pl.pallas_call(
        body, name="mla_attn_fwd", grid=(B, MLA_HEADS),
        in_specs=[head, head, shared, tab, tab, tab, pl.BlockSpec(memory_space=pl.ANY)], out_specs=[out, lse],
        out_shape=[jax.ShapeDtypeStruct(y.shape, F32), jax.ShapeDtypeStruct((B, MLA_HEADS, Tp, 1), F32)],
        input_output_aliases={6: 0},
        scratch_shapes=[pltpu.VMEM((Tp, HEAD_LANES), MXU_DTYPE), pltpu.VMEM((Tp, HEAD_LANES), MXU_DTYPE)],
        compiler_params=pltpu.CompilerParams(dimension_semantics=("parallel", "parallel")),
    )(q, kv, kpe, *tabs, y)


def _attn_bwd_call(q, kv, kpe, tabs, o, lse, do):
    B, Tp, _ = q.shape

    def body(q_ref, kv_ref, kpe_ref, c_ref, su_ref, sd_ref, o_ref, lse_ref, do_ref, dq_ref, dkv_ref, dkpe_ref,
             qr_ref, kr_ref, dqa_ref, dka_ref, dva_ref):
        c, s_up, s_down = c_ref[...], su_ref[...], sd_ref[...]
        qr, kr, lane = _mla_operands(q_ref, kv_ref, kpe_ref, c, s_up, s_down)
        qr_ref[...] = qr
        kr_ref[...] = kr
        dka_ref[...] = jnp.zeros_like(dka_ref)
        dva_ref[...] = jnp.zeros_like(dva_ref)
        for r0, L in _query_blocks(Tp):
            blk = slice(r0, L)
            qb = qr_ref[blk, :]
            do = jnp.where(lane[blk, :] >= MLA_NOPE, do_ref[blk, :], 0.0)
            delta = jnp.sum(do * o_ref[blk, :], axis=-1, keepdims=True)
            s = _mask_diagonal(lax.dot_general(qb, kr_ref[0:L, :], _NT, preferred_element_type=F32), NEG_INF)
            p = jnp.exp2(s - lse_ref[blk, :])
            dob = do.astype(MXU_DTYPE)
            dva_ref[0:L, :] += lax.dot_general(p.astype(MXU_DTYPE), dob, _TN, preferred_element_type=F32)
            dp = lax.dot_general(dob, kv_ref[0:L, :].astype(MXU_DTYPE), _NT, preferred_element_type=F32)
            ds = (p * (dp - delta)).astype(MXU_DTYPE)
            dqa_ref[blk, :] = jnp.dot(ds, kr_ref[0:L, :], preferred_element_type=F32)
            dka_ref[0:L, :] += lax.dot_general(ds, qb, _TN, preferred_element_type=F32)
        dq_ref[...] = _unrope_lanes(dqa_ref[...] * _MLA_SCALE, c, s_up, s_down).astype(dq_ref.dtype)
        dk = dka_ref[...] * (1.0 / _LOG2E)
        dkv_ref[...] = jnp.where(lane < MLA_NOPE, dk, dva_ref[...]).astype(dkv_ref.dtype)
        dkpe = jnp.where(lane >= MLA_NOPE, _unrope_lanes(dk, c, s_up, s_down), 0.0)

        @pl.when(pl.program_id(1) == 0)
        def _():
            dkpe_ref[...] = dkpe

        @pl.when(pl.program_id(1) > 0)
        def _():
            dkpe_ref[...] += dkpe

    head, shared, tab, lse_spec, out = _mla_specs(Tp)
    wide = jax.ShapeDtypeStruct((B, Tp, MLA_HEADS * HEAD_LANES), q.dtype)
    acc = pltpu.VMEM((Tp, HEAD_LANES), F32)
    return pl.pallas_call(
        body, name="mla_attn_bwd", grid=(B, MLA_HEADS),
        in_specs=[head, head, shared, tab, tab, tab, out, lse_spec, out], out_specs=[head, head, shared],
        out_shape=[wide, wide, jax.ShapeDtypeStruct((B, Tp, HEAD_LANES), F32)],
        scratch_shapes=[pltpu.VMEM((Tp, HEAD_LANES), MXU_DTYPE), pltpu.VMEM((Tp, HEAD_LANES), MXU_DTYPE), acc, acc, acc],
        compiler_params=pltpu.CompilerParams(dimension_semantics=("parallel", "arbitrary")),
    )(q, kv, kpe, *tabs, o, lse, do)


@jax.custom_vjp
def mla_attention(q, kv, kpe, tabs, y):
    return _attn_fwd_call(q, kv, kpe, tabs, y)[0]


def _mla_attention_fwd(q, kv, kpe, tabs, y):
    o, lse = _attn_fwd_call(q, kv, kpe, tabs, y)
    return o, (q, kv, kpe, tabs, o, lse)


def _mla_attention_bwd(res, do):
    q, kv, kpe, tabs, o, lse = res
    dq, dkv, dkpe = _attn_bwd_call(q, kv, kpe, tabs, o, lse, do)
    return dq, dkv, dkpe, None, do


mla_attention.defvjp(_mla_attention_fwd, _mla_attention_bwd)


def _rope_halves(x, cos, sin):
    half = x.shape[1] // 2
    x1, x2 = x[:, :half], x[:, half:]
    return jnp.concatenate([x1 * cos - x2 * sin, x1 * sin + x2 * cos], axis=1)


def _unrope_halves(d, cos, sin):
    half = d.shape[1] // 2
    d1, d2 = d[:, :half], d[:, half:]
    return jnp.concatenate([d1 * cos + d2 * sin, d2 * cos - d1 * sin], axis=1)


_RET_K_SCALE = RET_QK_DIM ** -0.5
_RET_Q_BLOCKS = RET_HEADS
_RET_V_BLOCK0 = 2 * RET_HEADS * RET_QK_DIM // RET_V_DIM
_RET_G_BLOCK0 = _RET_V_BLOCK0 + RET_HEADS


def _ret_specs(Tp):
    q = pl.BlockSpec((None, Tp, RET_QK_DIM), lambda b, h: (b, 0, h))
    k = pl.BlockSpec((None, Tp, RET_QK_DIM), lambda b, h: (b, 0, _RET_Q_BLOCKS + h))
    v = pl.BlockSpec((None, Tp, RET_V_DIM), lambda b, h: (b, 0, _RET_V_BLOCK0 + h))
    tab = pl.BlockSpec((Tp, RET_QK_DIM // 2), lambda b, h: (0, 0))
    lg = pl.BlockSpec((None, 1, 1), lambda b, h: (h, 0, 0))
    return q, k, v, tab, lg


def _ret_operands(q_ref, k_ref, cos, sin, lg):
    t = lax.broadcasted_iota(jnp.int32, (q_ref.shape[0], 1), 0).astype(F32)
    grow, shrink = jnp.exp(-lg * t), jnp.exp(lg * t)
    qs = (_rope_halves(q_ref[...].astype(F32), cos, sin) * shrink).astype(MXU_DTYPE)
    ks = (_rope_halves(k_ref[...].astype(F32), cos, sin) * (grow * _RET_K_SCALE)).astype(MXU_DTYPE)
    return qs, ks, shrink, grow * _RET_K_SCALE


def _ret_core_fwd_call(p, cos, sin, lg):
    B, Tp, _ = p.shape

    def body(q_ref, k_ref, v_ref, cos_ref, sin_ref, lg_ref, o_ref, qs_ref, ks_ref):
        qs_ref[...], ks_ref[...], _, _ = _ret_operands(q_ref, k_ref, cos_ref[...], sin_ref[...], lg_ref[...])
        for r0, L in _query_blocks(Tp):
            blk = slice(r0, L)
            s = _mask_diagonal(lax.dot_general(qs_ref[blk, :], ks_ref[0:L, :], _NT, preferred_element_type=F32), 0.0)
            o_ref[blk, :] = jnp.dot(s.astype(MXU_DTYPE), v_ref[0:L, :].astype(MXU_DTYPE), preferred_element_type=F32)

    q, k, v, tab, lgs = _ret_specs(Tp)
    return pl.pallas_call(
        body, name="retention_fwd", grid=(B, RET_HEADS), in_specs=[q, k, v, tab, tab, lgs],
        out_specs=pl.BlockSpec((None, Tp, RET_V_DIM), lambda b, h: (b, 0, h)),
        out_shape=jax.ShapeDtypeStruct((B, Tp, RET_HEADS * RET_V_DIM), F32),
        scratch_shapes=[pltpu.VMEM((Tp, RET_QK_DIM), MXU_DTYPE), pltpu.VMEM((Tp, RET_QK_DIM), MXU_DTYPE)],
        compiler_params=pltpu.CompilerParams(dimension_semantics=("parallel", "parallel")),
    )(p, p, p, cos, sin, lg)


def _ret_core_bwd_call(p, do, cos, sin, lg):
    B, Tp, _ = p.shape

    def body(q_ref, k_ref, v_ref, do_ref, cos_ref, sin_ref, lg_ref, dq_ref, dk_ref, dv_ref, qs_ref, ks_ref, dqa_ref, dka_ref, dva_ref):
        cos_, sin_ = cos_ref[...], sin_ref[...]
        qs_ref[...], ks_ref[...], q_scale, k_scale = _ret_operands(q_ref, k_ref, cos_, sin_, lg_ref[...])
        dka_ref[...] = jnp.zeros_like(dka_ref)
        dva_ref[...] = jnp.zeros_like(dva_ref)
        for r0, L in _query_blocks(Tp):
            blk = slice(r0, L)
            qb = qs_ref[blk, :]
            dob = do_ref[blk, :].astype(MXU_DTYPE)
            s = _mask_diagonal(lax.dot_general(qb, ks_ref[0:L, :], _NT, preferred_element_type=F32), 0.0).astype(MXU_DTYPE)
            dva_ref[0:L, :] += lax.dot_general(s, dob, _TN, preferred_element_type=F32)
            ds = _mask_diagonal(lax.dot_general(dob, v_ref[0:L, :].astype(MXU_DTYPE), _NT, preferred_element_type=F32), 0.0).astype(MXU_DTYPE)
            dqa_ref[blk, :] = jnp.dot(ds, ks_ref[0:L, :], preferred_element_type=F32)
            dka_ref[0:L, :] += lax.dot_general(ds, qb, _TN, preferred_element_type=F32)
        dq_ref[...] = _unrope_halves(dqa_ref[...] * q_scale, cos_, sin_).astype(dq_ref.dtype)
        dk_ref[...] = _unrope_halves(dka_ref[...] * k_scale, cos_, sin_).astype(dk_ref.dtype)
        dv_ref[...] = dva_ref[...].astype(dv_ref.dtype)

    q, k, v, tab, lgs = _ret_specs(Tp)
    qk_out = pl.BlockSpec((None, Tp, RET_QK_DIM), lambda b, h: (b, 0, h))
    v_out = pl.BlockSpec((None, Tp, RET_V_DIM), lambda b, h: (b, 0, h))
    return pl.pallas_call(
        body, name="retention_bwd", grid=(B, RET_HEADS), in_specs=[q, k, v, v_out, tab, tab, lgs],
        out_specs=[qk_out, qk_out, v_out],
        out_shape=[jax.ShapeDtypeStruct((B, Tp, RET_HEADS * RET_QK_DIM), p.dtype), jax.ShapeDtypeStruct((B, Tp, RET_HEADS * RET_QK_DIM), p.dtype),
                   jax.ShapeDtypeStruct((B, Tp, RET_HEADS * RET_V_DIM), p.dtype)],
        scratch_shapes=[pltpu.VMEM((Tp, RET_QK_DIM), MXU_DTYPE), pltpu.VMEM((Tp, RET_QK_DIM), MXU_DTYPE),
                        pltpu.VMEM((Tp, RET_QK_DIM), F32), pltpu.VMEM((Tp, RET_QK_DIM), F32), pltpu.VMEM((Tp, RET_V_DIM), F32)],
        compiler_params=pltpu.CompilerParams(dimension_semantics=("parallel", "parallel")),
    )(p, p, p, do, cos, sin, lg)


def _ret_gate_specs(M):
    tm = _pick(M, 1088, 8)
    head = pl.BlockSpec((tm, RET_V_DIM), lambda i, h: (i, h))
    gate = pl.BlockSpec((tm, RET_V_DIM), lambda i, h: (i, _RET_G_BLOCK0 + h))
    return tm, head, gate


def _ret_gate_fwd_call(o, p2d):
    M = o.shape[0]
    tm, head, gate = _ret_gate_specs(M)

    def body(o_ref, g_ref, y_ref):
        ov = o_ref[...]
        gv = g_ref[...].astype(F32)
        rstd = lax.rsqrt(jnp.mean(ov * ov, axis=-1, keepdims=True) + EPS)
        y_ref[...] = (gv * _sigmoid(gv)) * (ov * rstd)

    return pl.pallas_call(
        body, name="retention_gate_fwd", grid=(M // tm, RET_HEADS), in_specs=[head, gate], out_specs=head,
        out_shape=jax.ShapeDtypeStruct(o.shape, F32),
        compiler_params=pltpu.CompilerParams(dimension_semantics=("parallel", "parallel")),
    )(o, p2d)


def _ret_gate_bwd_call(o, p2d, dy):
    M = o.shape[0]
    tm, head, gate = _ret_gate_specs(M)

    def body(o_ref, g_ref, dy_ref, do_ref, dg_ref):
        ov = o_ref[...]
        gv = g_ref[...].astype(F32)
        dy = dy_ref[...]
        rstd = lax.rsqrt(jnp.mean(ov * ov, axis=-1, keepdims=True) + EPS)
        on = ov * rstd
        sg = _sigmoid(gv)
        dg_ref[...] = (dy * on * (sg * (1.0 + gv * (1.0 - sg)))).astype(dg_ref.dtype)
        don = dy * (gv * sg)
        do_ref[...] = (rstd * (don - on * jnp.mean(don * on, axis=-1, keepdims=True))).astype(do_ref.dtype)

    shp = jax.ShapeDtypeStruct(o.shape, p2d.dtype)
    return pl.pallas_call(
        body, name="retention_gate_bwd", grid=(M // tm, RET_HEADS), in_specs=[head, gate, head], out_specs=[head, head],
        out_shape=[shp, shp],
        compiler_params=pltpu.CompilerParams(dimension_semantics=("parallel", "parallel")),
    )(o, p2d, dy)


def _log_gamma():
    return jnp.log(1.0 - 2.0 ** (-5.0 - jnp.arange(RET_HEADS, dtype=F32))).reshape(RET_HEADS, 1, 1)


@functools.partial(jax.custom_vjp, nondiff_argnums=(9,))
def retention_block(h, w_in, w_out, w_in_grad_slot, w_out_grad_slot, g, b, cos, sin, dims):
    return _retention_block_fwd(h, w_in, w_out, w_in_grad_slot, w_out_grad_slot, g, b, cos, sin, dims)[0]


def _retention_block_fwd(h, w_in, w_out, w_in_grad_slot, w_out_grad_slot, g, b, cos, sin, dims):
    B, Tp = dims
    p = _mm_nn(h, w_in, False, "od_w_in_fwd", out_dtype=MXU_DTYPE)
    o = _ret_core_fwd_call(p.reshape(B, Tp, -1), cos, sin, _log_gamma())
    y = _ret_gate_fwd_call(o.reshape(B * Tp, -1), p)
    out, z = _mm_nn(y, w_out, False, "od_w_out_norm_fwd", norm=(h, g, b))
    return out, (h, p, o, y, z, w_in, w_out, g, cos, sin, jnp.zeros((), w_in_grad_slot.dtype))


def _retention_block_bwd(dims, res, dout):
    B, Tp = dims
    h, p, o, y, z, w_in, w_out, g, cos, sin, slot_like = res
    dz, dg, db = _ln_bwd_call(z, g, dout, "od_w_out_norm_bwd")
    dy = _mm_nt(dz, w_out, None, "od_w_out_dx")
    dw_out = _mm_tn(y, dz, False, "od_w_out_dw", 1, slot_like.dtype)
    do, dgate = _ret_gate_bwd_call(o.reshape(B * Tp, -1), p, dy)
    dq, dk, dv = _ret_core_bwd_call(p.reshape(B, Tp, -1), do.reshape(B, Tp, -1), cos, sin, _log_gamma())
    dp = jnp.concatenate([dq.reshape(B * Tp, -1), dk.reshape(B * Tp, -1), dv.reshape(B * Tp, -1), dgate], axis=-1)
    dh = _mm_nt(dp, w_in, None, "od_w_in_dx", plus=dz)
    dw_in = _mm_tn(h, dp, False, "od_w_in_dw", N_CHIPS, slot_like.dtype)
    return dh, None, None, dw_in, dw_out, dg.reshape(g.shape), db.reshape(g.shape), None, None


retention_block.defvjp(_retention_block_fwd, _retention_block_bwd)


def _heads_to_lanes(w):
    K = w.shape[0]
    w = w.reshape(K, MLA_HEADS, MLA_NOPE + MLA_ROPE)
    return jnp.pad(w, ((0, 0), (0, 0), (0, HEAD_LANES - MLA_NOPE - MLA_ROPE))).reshape(K, MLA_HEADS * HEAD_LANES)


def _out_rows_to_lanes(w):
    N = w.shape[1]
    att = w[LRU_WIDTH:].reshape(MLA_HEADS, MLA_V, N)
    att = jnp.pad(att, ((0, 0), (HEAD_LANES - MLA_V, 0), (0, 0))).reshape(MLA_HEADS * HEAD_LANES, N)
    return jnp.concatenate([w[:LRU_WIDTH], att], axis=0)


def _seq_dims(x):
    B, S, D = x.shape
    T = S + N_META
    Tp = _round_up(T, SEQ_BLOCK)
    return B, S, T, Tp


def _mixer0(diff, w, token):
    x = diff["x"]
    B, S, T, Tp = _seq_dims(x)
    D = x.shape[-1]
    M = B * Tp

    def mm(a, name, act=False, out_dtype=F32, layout=lambda m: m, col_shards=1):
        return matmul(a, layout(w[name]), layout(diff[name]), act, name, out_dtype, col_shards)

    meta = jnp.broadcast_to((diff["meta_tokens"] + token)[None], (B, N_META, D))
    h = jnp.concatenate([meta, x, jnp.zeros((B, Tp - T, D), F32)], axis=1).reshape(M, D)
    p = mm(h, "ev_w_in")
    sp = jax.nn.softplus(-diff["ev_lru_lambda"]).reshape(1, LRU_WIDTH)
    y, qn, kvn, kpe = even_front(
        p.reshape(B, Tp, -1), diff["ev_conv_w"].reshape(CONV_WIDTH, LRU_WIDTH), diff["ev_conv_b"].reshape(1, LRU_WIDTH),
        diff["ev_w_rg_a"].reshape(LRU_HEADS, LRU_HEAD_DIM, LRU_HEAD_DIM), diff["ev_b_rg_a"].reshape(1, LRU_WIDTH),
        diff["ev_w_rg_x"].reshape(LRU_HEADS, LRU_HEAD_DIM, LRU_HEAD_DIM), diff["ev_b_rg_x"].reshape(1, LRU_WIDTH),
        sp, diff["ev_q_norm_g"].reshape(-1), diff["ev_kv_norm_g"].reshape(-1))
    q = mm(qn, "ev_w_uq", out_dtype=MXU_DTYPE, layout=_heads_to_lanes).reshape(B, Tp, -1)
    kv = mm(kvn, "ev_w_ukv", out_dtype=MXU_DTYPE).reshape(B, Tp, -1)
    y = mla_attention(q, kv, kpe, _mla_rope_tables(Tp), y).reshape(M, -1)
    return out_block(h, y, _out_rows_to_lanes(w["ev_w_out"]), _out_rows_to_lanes(diff["ev_w_out"]),
                     diff["ln_mix_g"], diff["ln_mix_b"], "ev_w_out")


def _mlp0(diff, h, w):
    return mlp_block(h, w["mlp_w1_0"], w["mlp_w2_0"], diff["mlp_w1_0"], diff["mlp_w2_0"], diff["ln_mlp_g"], diff["ln_mlp_b"], "mlp0")


def _layer1_loss(diff, h, w, tgt):
    B, S, T, Tp = _seq_dims(tgt)
    D = tgt.shape[-1]

    cos, sin = (jnp.asarray(t) for t in _rope_tables(Tp, RET_QK_DIM // 2))
    h = retention_block(h, w["od_w_in"], w["od_w_out"], diff["od_w_in"], diff["od_w_out"], diff["ln_mix_g"], diff["ln_mix_b"], cos, sin, (B, Tp))
    h = mlp_block(h, w["mlp_w1_1"], w["mlp_w2_1"], diff["mlp_w1_1"], diff["mlp_w2_1"], diff["ln_mlp_g"], diff["ln_mlp_b"], "mlp1")
    return loss_head(h.reshape(B, Tp, D), jnp.pad(tgt, ((0, 0), (N_META, Tp - T), (0, 0))), S)


_HBM = pl.BlockSpec(memory_space=pltpu.HBM)


def _place():
    return lax.axis_index("x"), lax.axis_index("y"), lax.axis_index("c")


def _other_chips(x, y):
    return [(1 - x, y), (x, 1 - y), (1 - x, 1 - y)]


def _chunks(rows, sublanes, most):
    for q in range(most, 0, -1):
        if rows % (q * sublanes) == 0:
            return q
    return 1


def _sublanes(dtype):
    return 8 * 4 // jnp.dtype(dtype).itemsize


def _gather_pieces(bufs):
    plan, first = [], []
    for b in bufs:
        Rh = b.shape[0] // 2
        Q = _chunks(Rh, _sublanes(b.dtype), 4) if Rh * b.shape[1] * b.dtype.itemsize > (1 << 20) else 1
        first.append(3 * sum(q for _, q, _ in plan))
        plan.append((Rh, Q, Rh // Q))
    return plan, first, 3 * sum(q for _, q, _ in plan)


def _allgather_chips(bufs, name):
    n = len(bufs)
    plan, first, n_sems = _gather_pieces(bufs)

    def body(*refs):
        x_refs, out_refs, (send_sems, recv_sems) = refs[:n], refs[n:2 * n], refs[2 * n:]
        x, y, c = _place()
        sibling = (x, y, 1 - c)
        chips = _other_chips(x, y)

        def copy(k, src, dst, to):
            return pltpu.make_async_remote_copy(src_ref=src, dst_ref=dst, send_sem=send_sems.at[k], recv_sem=recv_sems.at[k],
                                                device_id=to, device_id_type=MESH)

        def piece(i, cx, cy, hc, q):
            Rh, _, ch = plan[i]
            return out_refs[i].at[2 * cx + cy, pl.ds(hc * Rh + q * ch, ch), :]

        slots = [(i, q, j) for i in range(n) for q in range(plan[i][1]) for j in range(3)]
        sem = {(i, q, j): first[i] + 3 * q + j for i, q, j in slots}
        sent = [copy(sem[i, q, j], x_refs[i].at[pl.ds(c * plan[i][0] + q * plan[i][2], plan[i][2]), :], piece(i, x, y, c, q), (*chips[j], c))
                for i, q, j in slots]
        for cp in sent:
            cp.start()
        passed = []
        for i, q, j in slots:
            landed = piece(i, *chips[j], c, q)
            copy(sem[i, q, j], landed, landed, sibling).wait_recv()
            fwd = copy(n_sems + sem[i, q, j], landed, landed, sibling)
            fwd.start()
            passed.append(fwd)
        for i, q, j in slots:
            theirs = piece(i, *chips[j], 1 - c, q)
            copy(n_sems + sem[i, q, j], theirs, theirs, sibling).wait_recv()
        for cp in sent + passed:
            cp.wait_send()

    return pl.pallas_call(
        body, name=name, in_specs=[_HBM] * n, out_specs=[_HBM] * n,
        out_shape=[jax.ShapeDtypeStruct((N_CHIPS,) + b.shape, b.dtype) for b in bufs],
        scratch_shapes=[pltpu.SemaphoreType.DMA((2 * n_sems,)), pltpu.SemaphoreType.DMA((2 * n_sems,))],
    )(*bufs)


def _with_own(gathered, own):
    my = 2 * lax.axis_index("x") + lax.axis_index("y")
    return lax.dynamic_update_slice(gathered, own[None], (my, 0, 0))


def _sibling_gather(fs, name):
    n = len(fs)

    def body(*refs):
        out_refs, (send_sems, recv_sems) = refs[n:2 * n], refs[2 * n:]
        x, y, c = _place()
        copies = [pltpu.make_async_remote_copy(src_ref=out_ref.at[c], dst_ref=out_ref.at[c], send_sem=send_sems.at[i], recv_sem=recv_sems.at[i],
                                               device_id=(x, y, 1 - c), device_id_type=MESH) for i, out_ref in enumerate(out_refs)]
        for cp in copies:
            cp.start()
        for cp in copies:
            cp.wait()

    return pl.pallas_call(
        body, name=name, in_specs=[_HBM] * n, out_specs=[_HBM] * n,
        out_shape=[jax.ShapeDtypeStruct(f.shape, f.dtype) for f in fs], input_output_aliases={i: i for i in range(n)},
        scratch_shapes=[pltpu.SemaphoreType.DMA((n,)), pltpu.SemaphoreType.DMA((n,))],
    )(*fs)


def _axis_scalar(name):
    return lax.axis_index(name).astype(jnp.int32).reshape(1)


_SEM = pl.BlockSpec(memory_space=pltpu.SEMAPHORE)
_ANY = pl.BlockSpec(memory_space=pl.ANY)
_EFFECT = pltpu.SideEffectType.DATAFLOW_SIDE_EFFECTING


def _in_hbm(a):
    return pltpu.with_memory_space_constraint(a, pltpu.HBM)


def _half_copies(x_refs, land_refs, send_sems, recv_sems, arriving):
    x, y, c = _place()
    copies = []
    for i, (x_ref, land_ref) in enumerate(zip(x_refs, land_refs)):
        Rh = x_ref.shape[0] // 2
        rows = pl.ds(c * Rh, Rh)
        for j, (cx, cy) in enumerate(_other_chips(x, y)):
            copies.append(pltpu.make_async_remote_copy(
                src_ref=x_ref.at[rows, :], dst_ref=land_ref.at[2 * cx + cy if arriving else 2 * x + y, rows, :],
                send_sem=send_sems.at[3 * i + j], recv_sem=recv_sems.at[3 * i + j], device_id=(cx, cy, c), device_id_type=MESH))
    return copies


def _allgather_start(bufs, name):
    n = len(bufs)

    def body(*refs):
        x_refs, land_refs, (send_sems, recv_sems), token = refs[:n], refs[n:2 * n], refs[2 * n:2 * n + 2], refs[-1]
        for cp in _half_copies(x_refs, land_refs, send_sems, recv_sems, False):
            cp.start()
        token[...] = jnp.zeros_like(token)

    lands = [lax.empty((N_CHIPS,) + b.shape, b.dtype) for b in bufs]
    out = pl.pallas_call(
        body, name=name,
        out_shape=(pltpu.SemaphoreType.DMA((3 * n,)), pltpu.SemaphoreType.DMA((3 * n,)), *[pltpu.HBM(a.shape, a.dtype) for a in bufs + lands],
                   jax.ShapeDtypeStruct((8, 128), F32)),
        in_specs=[_HBM] * (2 * n), out_specs=(_SEM, _SEM, *[_HBM] * (2 * n), pl.BlockSpec(memory_space=pltpu.VMEM)),
        input_output_aliases={i: 2 + i for i in range(2 * n)}, compiler_params=pltpu.CompilerParams(has_side_effects=_EFFECT),
    )(*[_in_hbm(a) for a in bufs + lands])
    return (out[0], out[1], list(out[2:2 + n]), list(out[2 + n:2 + 2 * n])), out[-1][0, 0]


def _allgather_wait(pending, after, name):
    send_sems, recv_sems, bufs, lands = pending
    n = len(bufs)

    def body(*refs):
        x_refs, land_refs, send_sems, recv_sems = refs[:n], refs[n:2 * n], refs[2 * n], refs[2 * n + 1]
        for cp in _half_copies(x_refs, land_refs, send_sems, recv_sems, False):
            cp.wait_send()
        for cp in _half_copies(x_refs, land_refs, send_sems, recv_sems, True):
            cp.wait_recv()

    out = pl.pallas_call(
        body, name=name, out_shape=tuple(pltpu.HBM(a.shape, a.dtype) for a in bufs + lands),
        in_specs=[_HBM] * (2 * n) + [_SEM, _SEM, _ANY], out_specs=tuple([_HBM] * (2 * n)), input_output_aliases={i: i for i in range(2 * n)},
        compiler_params=pltpu.CompilerParams(has_side_effects=_EFFECT),
    )(*bufs, *lands, send_sems, recv_sems, after)
    return list(out[n:])


def _sibling_forward(lands, name):
    n = len(lands)
    plan, first, n_sems = _gather_pieces([jax.ShapeDtypeStruct(l.shape[1:], l.dtype) for l in lands])

    def body(*refs):
        out_refs, (send_sems, recv_sems) = refs[n:2 * n], refs[2 * n:]
        x, y, c = _place()

        def copies(hc):
            return [pltpu.make_async_remote_copy(
                        src_ref=out_refs[i].at[2 * cx + cy, pl.ds(hc * plan[i][0] + q * plan[i][2], plan[i][2]), :],
                        dst_ref=out_refs[i].at[2 * cx + cy, pl.ds(hc * plan[i][0] + q * plan[i][2], plan[i][2]), :],
                        send_sem=send_sems.at[first[i] + 3 * q + j], recv_sem=recv_sems.at[first[i] + 3 * q + j],
                        device_id=(x, y, 1 - c), device_id_type=MESH)
                    for i in range(n) for q in range(plan[i][1]) for j, (cx, cy) in enumerate(_other_chips(x, y))]

        mine = copies(c)
        for cp in mine:
            cp.start()
        for cp in mine:
            cp.wait_send()
        for cp in copies(1 - c):
            cp.wait_recv()

    return pl.pallas_call(
        body, name=name, in_specs=[_HBM] * n, out_specs=[_HBM] * n, out_shape=[jax.ShapeDtypeStruct(l.shape, l.dtype) for l in lands],
        input_output_aliases={i: i for i in range(n)},
        scratch_shapes=[pltpu.SemaphoreType.DMA((n_sems,)), pltpu.SemaphoreType.DMA((n_sems,))],
    )(*lands)


N_PEERS = 7


def _direct_copies(p_refs, t_refs, send_sems, recv_sems):
    x, y, c = _place()
    copies = []
    for i, (p_ref, t_ref) in enumerate(zip(p_refs, t_refs)):
        for f in range(1, N_PEERS + 1):
            px, py, pc = x ^ (f >> 2), y ^ ((f >> 1) & 1), c ^ (f & 1)
            copies.append(pltpu.make_async_remote_copy(
                src_ref=p_ref.at[2 * px + py, pc], dst_ref=t_ref.at[f - 1], send_sem=send_sems.at[N_PEERS * i + f - 1],
                recv_sem=recv_sems.at[N_PEERS * i + f - 1], device_id=(px, py, pc), device_id_type=MESH))
    return copies


def _direct_scatter_start(ps, name, carried=()):
    n, m = len(ps), 2 * len(ps) + len(carried)

    def body(*refs):
        p_refs, t_refs, (send_sems, recv_sems) = refs[:n], refs[n:2 * n], refs[m:m + 2]
        for cp in _direct_copies(p_refs, t_refs, send_sems, recv_sems):
            cp.start()

    lands = [lax.empty((N_PEERS,) + p.shape[2:], p.dtype) for p in ps]
    through = ps + lands + list(carried)
    out = pl.pallas_call(
        body, name=name,
        out_shape=(pltpu.SemaphoreType.DMA((N_PEERS * n,)), pltpu.SemaphoreType.DMA((N_PEERS * n,)),
                   *[pltpu.HBM(a.shape, a.dtype) for a in through]),
        in_specs=[_HBM] * m, out_specs=(_SEM, _SEM, *[_HBM] * m),
        input_output_aliases={i: 2 + i for i in range(m)}, compiler_params=pltpu.CompilerParams(has_side_effects=_EFFECT),
    )(*[_in_hbm(a) for a in through])
    return (out[0], out[1], list(out[2:2 + n]), list(out[2 + n:2 + 2 * n])), list(out[2 + 2 * n:])


def _direct_scatter_wait(pending, after, name):
    send_sems, recv_sems, ps, lands = pending
    n = len(ps)

    def body(*refs):
        p_refs, t_refs, send_sems, recv_sems = refs[:n], refs[n:2 * n], refs[2 * n], refs[2 * n + 1]
        for cp in _direct_copies(p_refs, t_refs, send_sems, recv_sems):
            cp.wait_send()
            cp.wait_recv()

    out = pl.pallas_call(
        body, name=name, out_shape=tuple(pltpu.HBM(a.shape, a.dtype) for a in ps + lands),
        in_specs=[_HBM] * (2 * n) + [_SEM, _SEM] + [_ANY] * len(after), out_specs=tuple([_HBM] * (2 * n)),
        input_output_aliases={i: i for i in range(2 * n)}, compiler_params=pltpu.CompilerParams(has_side_effects=_EFFECT),
    )(*ps, *lands, send_sems, recv_sems, *after)
    return list(out[:n]), list(out[n:])


def _sum_direct(p, t, name):
    _, _, R, C = p.shape
    tr = _pick(R, 512, 16)

    def body(x_ref, y_ref, c_ref, p_ref, t_ref, o_ref):
        acc = p_ref[...].astype(F32)
        for f in range(N_PEERS):
            acc = acc + t_ref[f].astype(F32)
        o_ref[...] = acc

    grid_spec = pltpu.PrefetchScalarGridSpec(
        num_scalar_prefetch=3, grid=(R // tr,),
        in_specs=[pl.BlockSpec((None, None, tr, C), lambda i, x_ref, y_ref, c_ref: (2 * x_ref[0] + y_ref[0], c_ref[0], i, 0)),
                  pl.BlockSpec((N_PEERS, tr, C), lambda i, x_ref, y_ref, c_ref: (0, i, 0))],
        out_specs=pl.BlockSpec((None, tr, C), lambda i, x_ref, y_ref, c_ref: (c_ref[0], i, 0)))
    return pl.pallas_call(body, name=name, grid_spec=grid_spec, out_shape=jax.ShapeDtypeStruct((2, R, C), F32),
                          compiler_params=pltpu.CompilerParams(dimension_semantics=("parallel",)))(
        _axis_scalar("x"), _axis_scalar("y"), _axis_scalar("c"), p, t)


def _adamw(w, g, m, v, name):
    R, C = w.shape
    tr = _pick(R, 512 if C <= PACK_COLS else 256, 8)

    def body(w_ref, g_ref, m_ref, v_ref, d_ref, nm_ref, nv_ref):
        g_ = g_ref[...]
        m_ = ADAM_B1 * m_ref[...] + (1.0 - ADAM_B1) * g_
        v_ = ADAM_B2 * v_ref[...] + (1.0 - ADAM_B2) * (g_ * g_)
        m_hat = m_ / (1.0 - ADAM_B1 ** ADAM_STEP)
        v_hat = v_ / (1.0 - ADAM_B2 ** ADAM_STEP)
        d_ref[...] = -ADAM_LR * (m_hat / (jnp.sqrt(v_hat) + ADAM_EPS) + ADAM_WD * w_ref[...])
        nm_ref[...] = m_
        nv_ref[...] = v_

    row = pl.BlockSpec((tr, C), lambda i: (i, 0))
    shp = jax.ShapeDtypeStruct((R, C), F32)
    return pl.pallas_call(body, name=name, grid=(R // tr,), in_specs=[row] * 4, out_specs=[row] * 3, out_shape=[shp] * 3,
                          compiler_params=pltpu.CompilerParams(dimension_semantics=("parallel",)))(w, g, m, v)


BIG_SPECS = (("ev_w_in", 1024, 1440, 1), ("ev_w_uq", 256, 768, 1), ("ev_w_ukv", 128, 1024, 1), ("ev_w_out", 1024, 1024, 0),
             ("od_w_in", 1024, 6144, 1), ("od_w_out", 2048, 1024, 0), ("mlp_w1_0", 1024, 4096, 1), ("mlp_w1_1", 1024, 4096, 1),
             ("mlp_w2_0", 4096, 1024, 0), ("mlp_w2_1", 4096, 1024, 0))
BIG_PARAMS = (("ev_w_in", ("ev_w_in",)), ("ev_w_uq", ("ev_w_uq",)), ("ev_w_ukv", ("ev_w_ukv",)), ("ev_w_out", ("ev_w_out",)),
              ("od_w_in", ("od_w_in",)), ("od_w_out", ("od_w_out",)), ("mlp_w1", ("mlp_w1_0", "mlp_w1_1")),
              ("mlp_w2", ("mlp_w2_0", "mlp_w2_1")))
REPLICATED = ("ev_conv_b", "ev_w_rg_a", "ev_b_rg_a", "ev_w_rg_x", "ev_b_rg_x", "ev_lru_lambda", "ev_q_norm_g", "ev_kv_norm_g",
              "ln_mix_g", "ln_mix_b", "ln_mlp_g", "ln_mlp_b")
SMALL_SHARDED = ("meta_tokens", "ev_conv_w")
COL_SHARD_GRADS = ("od_w_in", "mlp_w1_0", "mlp_w1_1")
MATRIX_GROUPS = (("ev_w_in", "ev_w_uq", "ev_w_ukv", "ev_w_out"), ("mlp_w1_0", "mlp_w2_0"), ("od_w_in", "od_w_out", "mlp_w1_1", "mlp_w2_1"))
LAYER_NORMS = ("ln_mix_g", "ln_mix_b", "ln_mlp_g", "ln_mlp_b")
WEIGHT_NAMES = ("meta_tokens", "ev_w_in", "ev_conv_w", "ev_conv_b", "ev_w_rg_a", "ev_b_rg_a", "ev_w_rg_x", "ev_b_rg_x",
                "ev_lru_lambda", "ev_q_norm_g", "ev_w_uq", "ev_kv_norm_g", "ev_w_ukv", "ev_w_out", "od_w_in", "od_w_out",
                "ln_mix_g", "ln_mix_b", "mlp_w1", "mlp_w2", "ln_mlp_g", "ln_mlp_b")


def _to_rows(flat, row_align):
    n = flat.shape[-1]
    rows = _round_up(-(-n // PACK_COLS), row_align)
    pad = rows * PACK_COLS - n
    if pad:
        flat = jnp.pad(flat, [(0, 0)] * (flat.ndim - 1) + [(0, pad)])
    return flat.reshape(flat.shape[:-1] + (rows, PACK_COLS))


def _shard_shape(K, N, axis):
    return (K // N_CHIPS, N) if axis == 0 else (K, N // N_CHIPS)


def _gather_shards(stacked, K, N, axis):
    if axis == 0:
        return stacked.reshape(K, N)
    return stacked.transpose(1, 0, 2).reshape(K, N)


def _split_shards(full, K, N, axis):
    if axis == 0:
        return full.reshape(N_CHIPS, -1)
    return full.reshape(K, N_CHIPS, N // N_CHIPS).transpose(1, 0, 2).reshape(N_CHIPS, -1)


def kernel(x, meta_tokens, ev_w_in, ev_conv_w, ev_conv_b, ev_w_rg_a, ev_b_rg_a, ev_w_rg_x, ev_b_rg_x, ev_lru_lambda, ev_q_norm_g, ev_w_uq, ev_kv_norm_g, ev_w_ukv, ev_w_out, od_w_in, od_w_out, ln_mix_g, ln_mix_b, mlp_w1, mlp_w2, ln_mlp_g, ln_mlp_b, loss_target, m_meta_tokens, m_ev_w_in, m_ev_conv_w, m_ev_conv_b, m_ev_w_rg_a, m_ev_b_rg_a, m_ev_w_rg_x, m_ev_b_rg_x, m_ev_lru_lambda, m_ev_q_norm_g, m_ev_w_uq, m_ev_kv_norm_g, m_ev_w_ukv, m_ev_w_out, m_od_w_in, m_od_w_out, m_ln_mix_g, m_ln_mix_b, m_mlp_w1, m_mlp_w2, m_ln_mlp_g, m_ln_mlp_b, v_meta_tokens, v_ev_w_in, v_ev_conv_w, v_ev_conv_b, v_ev_w_rg_a, v_ev_b_rg_a, v_ev_w_rg_x, v_ev_b_rg_x, v_ev_lru_lambda, v_ev_q_norm_g, v_ev_w_uq, v_ev_kv_norm_g, v_ev_w_ukv, v_ev_w_out, v_od_w_in, v_od_w_out, v_ln_mix_g, v_ln_mix_b, v_mlp_w1, v_mlp_w2, v_ln_mlp_g, v_ln_mlp_b):
    given = dict(locals())
    local_big = {"ev_w_in": ev_w_in[0], "ev_w_uq": ev_w_uq[0], "ev_w_ukv": ev_w_ukv[0], "ev_w_out": ev_w_out[0],
                 "od_w_in": od_w_in[0], "od_w_out": od_w_out[0], "mlp_w1_0": mlp_w1[0], "mlp_w1_1": mlp_w1[1],
                 "mlp_w2_0": mlp_w2[0], "mlp_w2_1": mlp_w2[1]}

    specs = {spec[0]: spec for spec in BIG_SPECS}
    mixer0_m, mlp0_m, layer1_m = MATRIX_GROUPS

    def shards(names):
        return [local_big[n].astype(MXU_DTYPE) for n in names]

    def whole(stacked, n):
        _, K, N, ax = specs[n]
        return stacked if n in COL_SHARD_GRADS else _gather_shards(stacked, K, N, ax)

    def filled(gathered, own, names):
        return {n: whole(_with_own(g_, o_), n) for n, g_, o_ in zip(names, gathered, own)}

    own_a, own_b, own_c = shards(mixer0_m), shards(mlp0_m), shards(layer1_m)
    small = [meta_tokens, jnp.pad(ev_conv_w[0], ((0, 16 - CONV_WIDTH), (0, 0)))]
    gathered_a = _allgather_chips(own_a + small, "weight_allgather_mixer0")
    pending_b, token1 = _allgather_start(own_b, "weight_allgather_mlp0_start")
    pending_c, token2 = _allgather_start(own_c, "weight_allgather_layer1_start")
    meta_full = _gather_shards(_with_own(gathered_a[-2], small[0]), N_META, D_MODEL, 1)
    conv_full = _gather_shards(_with_own(gathered_a[-1], small[1])[:, :CONV_WIDTH], CONV_WIDTH, LRU_WIDTH, 1)

    def slots(names, dtype):
        return {n: jnp.zeros((N_CHIPS, specs[n][1], specs[n][2] // N_CHIPS) if n in COL_SHARD_GRADS else specs[n][1:3], dtype) for n in names}

    def norms(names, layer):
        return {n: given[n][layer] for n in names}

    def finish_gather(pending, own, after, names, tag):
        landed = _allgather_wait(pending, lax.stop_gradient(after), "weight_allgather_%s_wait" % tag)
        return filled(_sibling_forward(landed, "weight_allgather_%s_forward" % tag), own, names)

    diff_a = {**slots(mixer0_m, MXU_DTYPE), **norms(("ln_mix_g", "ln_mix_b"), 0), **{n: given[n] for n in REPLICATED if n not in LAYER_NORMS},
              "x": x, "meta_tokens": meta_full, "ev_conv_w": conv_full}
    diff_b = {**slots(mlp0_m, MXU_DTYPE), **norms(("ln_mlp_g", "ln_mlp_b"), 0)}
    diff_c = {**slots(layer1_m, MXU_DTYPE), **norms(LAYER_NORMS, 1)}
    w_a = filled(gathered_a[:len(mixer0_m)], own_a, mixer0_m)
    h_a, back_a = jax.vjp(lambda d: _mixer0(d, w_a, token1 + token2), diff_a)
    w_b = finish_gather(pending_b, own_b, h_a, mlp0_m, "mlp0")
    h_b, back_b = jax.vjp(lambda d, hh: _mlp0(d, hh, w_b), diff_b, h_a)
    w_c = finish_gather(pending_c, own_c, h_b, layer1_m, "layer1")
    loss, back_c = jax.vjp(lambda d, hh: _layer1_loss(d, hh, w_c, loss_target), diff_c, h_b)
    loss = lax.psum(loss, ("x", "y", "c"))

    def blocks_of(grad, n):
        _, K, N, ax = specs[n]
        if n in COL_SHARD_GRADS:
            blocks = grad
        elif ax == 0:
            blocks = grad.reshape(N_CHIPS, K // N_CHIPS, N)
        else:
            blocks = grad.reshape(K, N_CHIPS, N // N_CHIPS).transpose(1, 0, 2)
        return blocks.reshape(N_CHIPS, 2, blocks.shape[1] // 2, blocks.shape[2])

    def start_reduce(grads_of, names, tag, dh):
        flying, (dh,) = _direct_scatter_start([blocks_of(grads_of[n], n) for n in names], "grad_scatter_%s_start" % tag, [dh])
        return flying, dh

    g_c, dh = back_c(jnp.ones((), F32))
    flying_c, dh = start_reduce(g_c, layer1_m, "layer1", dh)
    g_b, dh = back_b(dh)
    flying_b, dh = start_reduce(g_b, mlp0_m, "mlp0", dh)
    (g_a,) = back_a(dh)

    g = {**g_a, **g_b, **g_c}
    g.update({n: jnp.stack([(g_b if n in g_b else g_a)[n], g_c[n]]) for n in LAYER_NORMS})
    repl = jnp.concatenate([g[n].reshape(-1) for n in REPLICATED]).reshape(N_CHIPS, -1)
    small = [_split_shards(g["meta_tokens"], N_META, D_MODEL, 1), _split_shards(g["ev_conv_w"], CONV_WIDTH, LRU_WIDTH, 1), repl]
    small = [pc.reshape(N_CHIPS, 2, -1) for pc in small]
    n_small = sum(pc.shape[2] for pc in small)
    small.append(jnp.zeros((N_CHIPS, 2, _round_up(n_small, 32 * PACK_COLS) - n_small), F32))
    p_small = jnp.concatenate(small, axis=2).reshape(N_CHIPS, 2, -1, PACK_COLS)
    flying_a, _ = _direct_scatter_start([blocks_of(g_a[n], n) for n in mixer0_m] + [p_small], "grad_scatter_mixer0_start")
    started = [g_a["x"], flying_a[2][0]]
    ps_c, ts_c = _direct_scatter_wait(flying_c, started, "grad_scatter_layer1_wait")
    ps_b, ts_b = _direct_scatter_wait(flying_b, started, "grad_scatter_mlp0_wait")
    fs_bc = [_sum_direct(p, t, "grad_sum_%d" % i) for i, (p, t) in enumerate(zip(ps_b + ps_c, ts_b + ts_c))]
    red_big = dict(zip(mlp0_m + layer1_m, _sibling_gather(fs_bc, "grad_sibling_gather")))

    grads, delta, new_m, new_v = {}, {}, {}, {}

    def update_big(names):
        done = []
        for name, parts in BIG_PARAMS:
            if parts[0] in names:
                shp = given[name].shape
                two_d = (-1, shp[-1])
                grads[name] = jnp.stack([red_big[part].reshape(shp[1:]) for part in parts])
                d, nm, nv = _adamw(given[name].reshape(two_d), grads[name].reshape(two_d), given["m_" + name].reshape(two_d),
                                   given["v_" + name].reshape(two_d), "adamw_" + name)
                delta[name], new_m[name], new_v[name] = d.reshape(shp), nm.reshape(shp), nv.reshape(shp)
                done.append(nv)
        return done

    updated = update_big(mlp0_m + layer1_m)
    ps_a, ts_a = _direct_scatter_wait(flying_a, updated, "grad_scatter_mixer0_wait")
    fs_a = [_sum_direct(p, t, "grad_sum_mixer0_%d" % i) for i, (p, t) in enumerate(zip(ps_a, ts_a))]
    reduced_a = _sibling_gather(fs_a, "grad_sibling_gather_mixer0")
    red_big.update(zip(mixer0_m, reduced_a))
    red_small = reduced_a[-1].reshape(2, -1)
    update_big(mixer0_m)

    def take(off, sz):
        return jnp.concatenate([red_small[0, off // 2:(off + sz) // 2], red_small[1, off // 2:(off + sz) // 2]])

    off = 0
    for name in SMALL_SHARDED:
        sz = given[name].size
        grads[name] = take(off, sz).reshape(given[name].shape)
        off += sz
    n_repl = repl.shape[1]
    own_repl = _to_rows(take(off, n_repl), 16)
    repl_all = _with_own(_allgather_chips([own_repl], "replicated_allgather")[0], own_repl).reshape(N_CHIPS, -1)[:, :n_repl].reshape(-1)
    off = 0
    for name in REPLICATED:
        sz = given[name].size
        grads[name] = repl_all[off:off + sz].reshape(given[name].shape)
        off += sz

    smalls = SMALL_SHARDED + REPLICATED

    def pack_small(get):
        return _to_rows(jnp.concatenate([get(n).reshape(-1) for n in smalls]), 8)

    outs = _adamw(pack_small(lambda n: given[n]), pack_small(lambda n: grads[n]), pack_small(lambda n: given["m_" + n]),
                  pack_small(lambda n: given["v_" + n]), "adamw_small")
    for res, flat in zip((delta, new_m, new_v), outs):
        flat, off = flat.reshape(-1), 0
        for n in smalls:
            sz = given[n].size
            res[n] = flat[off:off + sz].reshape(given[n].shape)
            off += sz

    return (loss, g_a["x"], *[grads[n] for n in WEIGHT_NAMES], *[delta[n] for n in WEIGHT_NAMES],
            *[new_m[n] for n in WEIGHT_NAMES], *[new_v[n] for n in WEIGHT_NAMES])
```
